```python
import math
import jax, jax.numpy as jnp
from jax import lax
import numpy as np

D_MODEL = 1024
BATCH = 8
SEQ = 4096
DEPTH = 2

CHUNK = 64
MEM_LEN = 256
EPS = 1e-6
A_WIDTH = 512
A_GROUPS = 4
A_GROUP_DIM = A_WIDTH // A_GROUPS
GMLP_BLOCK = 128
B_WIDTH = 512
CONV_WIDTH = 31
MIX_WIDTH = A_WIDTH + B_WIDTH
IN_WIDTH = 2 * A_WIDTH + 2 * B_WIDTH
C_WIDTH = 512
C_GROUP_CH = 16
C_GROUPS = C_WIDTH // C_GROUP_CH
C_STATE = 64
DT_MIN = 1e-3
DT_MAX = 1e-1
CA_HEADS = 4
CA_HEAD_DIM = D_MODEL // CA_HEADS
FFN_HIDDEN = -(-8 * D_MODEL // (3 * 256)) * 256
N_EVEN = (DEPTH + 1) // 2
N_ODD = DEPTH // 2

kernel_name = "chunk_causal_hybrid_gmlp_conformer_s5_trunk"


def rmsnorm(x, g):
    xf = x.astype(jnp.float32)
    y = xf * lax.rsqrt(jnp.mean(xf * xf, axis=-1, keepdims=True) + EPS)
    return (y * g.astype(jnp.float32)).astype(x.dtype)


def layernorm(x, g=None, b=None):
    xf = x.astype(jnp.float32)
    mu = jnp.mean(xf, axis=-1, keepdims=True)
    xc = xf - mu
    y = xc * lax.rsqrt(jnp.mean(xc * xc, axis=-1, keepdims=True) + EPS)
    if g is not None:
        y = y * g.astype(jnp.float32) + b.astype(jnp.float32)
    return y.astype(x.dtype)


def gmlp_spatial_gate(u, v, w_s, b_s):
    bn, s, _ = v.shape
    v = layernorm(v)
    v = v.reshape(bn, s // GMLP_BLOCK, GMLP_BLOCK, A_GROUPS, A_GROUP_DIM)
    chunk_id = jnp.arange(GMLP_BLOCK) // CHUNK
    mask = chunk_id[None, :] <= chunk_id[:, None]
    w = jnp.where(mask[None], w_s, jnp.zeros_like(w_s))
    sg = jnp.einsum('gij,bnjgc->bnigc', w, v) + b_s.T[None, None, :, :, None]
    return u * sg.reshape(bn, s, A_WIDTH)


def conformer_conv(a, g, conv_w, conv_b, ln_g, ln_b):
    h = a * jax.nn.sigmoid(g)
    h = lax.conv_general_dilated(
        h, conv_w[:, None, :].astype(h.dtype), window_strides=(1,),
        padding=[(CONV_WIDTH - 1, 0)], dimension_numbers=('NWC', 'WIO', 'NWC'),
        feature_group_count=B_WIDTH) + conv_b
    h = layernorm(h, ln_g, ln_b)
    return jax.nn.silu(h)


def _complex_affine_combine(e1, e2):
    a1r, a1i, b1r, b1i = e1
    a2r, a2i, b2r, b2i = e2
    ar = a1r * a2r - a1i * a2i
    ai = a1r * a2i + a1i * a2r
    br = a2r * b1r - a2i * b1i + b2r
    bi = a2r * b1i + a2i * b1r + b2i
    return (ar, ai, br, bi)


def s5_layer(u, lam_re, lam_im, log_dt, b_re, b_im, c_re, c_im, d_skip):
    bn, s, _ = u.shape
    f32 = jnp.float32
    uf = u.astype(f32)
    dt = jnp.exp(log_dt.astype(f32))[:, None]
    lr = lam_re.astype(f32)
    li = lam_im.astype(f32)
    mag = jnp.exp(lr * dt)
    ar = mag * jnp.cos(li * dt)
    ai = mag * jnp.sin(li * dt)
    den = lr * lr + li * li
    qr = ((ar - 1.0) * lr + ai * li) / den
    qi = (ai * lr - (ar - 1.0) * li) / den
    br_ = b_re.astype(f32)
    bi_ = b_im.astype(f32)
    bbr = qr[..., None] * br_ - qi[..., None] * bi_
    bbi = qr[..., None] * bi_ + qi[..., None] * br_
    ug = uf.reshape(bn, s, C_GROUPS, C_GROUP_CH).transpose(1, 0, 2, 3)
    bu_r = jnp.einsum('sbgc,gpc->sbgp', ug, bbr)
    bu_i = jnp.einsum('sbgc,gpc->sbgp', ug, bbi)
    a_r = jnp.broadcast_to(ar[None, None], (s, 1, C_GROUPS, C_STATE))
    a_i = jnp.broadcast_to(ai[None, None], (s, 1, C_GROUPS, C_STATE))
    _, _, xr, xi = lax.associative_scan(_complex_affine_combine, (a_r, a_i, bu_r, bu_i), axis=0)
    y = (jnp.einsum('sbgp,gcp->sbgc', xr, c_re.astype(f32))
         - jnp.einsum('sbgp,gcp->sbgc', xi, c_im.astype(f32)))
    y = y.transpose(1, 0, 2, 3).reshape(bn, s, C_WIDTH) + d_skip.astype(f32) * uf
    return y.astype(u.dtype)


def cross_attention(xn, memn, wq, wk, wv, wo):
    bn, s, _ = xn.shape
    m = memn.shape[1]
    q = (xn @ wq).reshape(bn, s, CA_HEADS, CA_HEAD_DIM)
    k = (memn @ wk).reshape(bn, m, CA_HEADS, CA_HEAD_DIM)
    v = (memn @ wv).reshape(bn, m, CA_HEADS, CA_HEAD_DIM)
    sc = jnp.einsum('bshd,bmhd->bhsm', q, k).astype(jnp.float32) * (CA_HEAD_DIM ** -0.5)
    p = jax.nn.softmax(sc, axis=-1).astype(v.dtype)
    o = jnp.einsum('bhsm,bmhd->bshd', p, v).reshape(bn, s, D_MODEL)
    return o @ wo


def swiglu(xn, wg, wu, wd):
    return (jax.nn.silu(xn @ wg) * (xn @ wu)) @ wd


def _fwd_setup_inputs(seed: int = 0) -> dict:
    key = jax.random.key(seed)
    ks = iter(jax.random.split(key, 48))

    def nrm(shape, scale):
        return jax.random.normal(next(ks), shape, jnp.float32) * scale

    def gain(shape):
        return 1.0 + nrm(shape, 0.02)

    d, h = D_MODEL, FFN_HIDDEN
    inp = {}
    inp['x'] = nrm((BATCH, SEQ, d), 1.0)
    inp['mem'] = nrm((BATCH, MEM_LEN, d), 1.0)
    inp['e_norm'] = gain((N_EVEN, d))
    inp['e_w_in'] = nrm((N_EVEN, d, IN_WIDTH), d ** -0.5)
    inp['e_gmlp_w'] = nrm((N_EVEN, A_GROUPS, GMLP_BLOCK, GMLP_BLOCK), 0.5 * GMLP_BLOCK ** -0.5)
    inp['e_gmlp_b'] = gain((N_EVEN, A_GROUPS, GMLP_BLOCK))
    inp['e_conv_w'] = nrm((N_EVEN, CONV_WIDTH, B_WIDTH), CONV_WIDTH ** -0.5)
    inp['e_conv_b'] = nrm((N_EVEN, B_WIDTH), 0.02)
    inp['e_conv_ln_g'] = gain((N_EVEN, B_WIDTH))
    inp['e_conv_ln_b'] = nrm((N_EVEN, B_WIDTH), 0.02)
    inp['e_w_out'] = nrm((N_EVEN, MIX_WIDTH, d), MIX_WIDTH ** -0.5)
    inp['o_norm'] = gain((N_ODD, d))
    inp['o_w_in'] = nrm((N_ODD, d, C_WIDTH), d ** -0.5)
    inp['o_lam_re'] = -0.5 + nrm((N_ODD, C_GROUPS, C_STATE), 0.01)
    inp['o_lam_im'] = (math.pi * jnp.arange(C_STATE, dtype=jnp.float32))[None, None, :] + nrm((N_ODD, C_GROUPS, C_STATE), 0.01)
    inp['o_log_dt'] = jax.random.uniform(next(ks), (N_ODD, C_GROUPS), jnp.float32, math.log(DT_MIN), math.log(DT_MAX))
    inp['o_b_re'] = nrm((N_ODD, C_GROUPS, C_STATE, C_GROUP_CH), (2 * C_GROUP_CH) ** -0.5)
    inp['o_b_im'] = nrm((N_ODD, C_GROUPS, C_STATE, C_GROUP_CH), (2 * C_GROUP_CH) ** -0.5)
    inp['o_c_re'] = nrm((N_ODD, C_GROUPS, C_GROUP_CH, C_STATE), (2 * C_STATE) ** -0.5)
    inp['o_c_im'] = nrm((N_ODD, C_GROUPS, C_GROUP_CH, C_STATE), (2 * C_STATE) ** -0.5)
    inp['o_d'] = gain((N_ODD, C_WIDTH))
    inp['o_w_out'] = nrm((N_ODD, C_WIDTH, 2 * d), C_WIDTH ** -0.5)
    inp['ca_norm'] = gain((DEPTH, d))
    inp['ca_mem_norm'] = gain((DEPTH, d))
    inp['ca_wq'] = nrm((DEPTH, d, d), d ** -0.5)
    inp['ca_wk'] = nrm((DEPTH, d, d), d ** -0.5)
    inp['ca_wv'] = nrm((DEPTH, d, d), d ** -0.5)
    inp['ca_wo'] = nrm((DEPTH, d, d), d ** -0.5)
    inp['ffn_norm'] = gain((DEPTH, d))
    inp['ffn_w_gate'] = nrm((DEPTH, d, h), d ** -0.5)
    inp['ffn_w_up'] = nrm((DEPTH, d, h), d ** -0.5)
    inp['ffn_w_down'] = nrm((DEPTH, h, d), h ** -0.5)
    inp['final_norm'] = gain((d,))
    return inp


def _fwd_reference(x, mem, e_norm, e_w_in, e_gmlp_w, e_gmlp_b, e_conv_w, e_conv_b, e_conv_ln_g,
              e_conv_ln_b, e_w_out, o_norm, o_w_in, o_lam_re, o_lam_im, o_log_dt, o_b_re,
              o_b_im, o_c_re, o_c_im, o_d, o_w_out, ca_norm, ca_mem_norm, ca_wq, ca_wk, ca_wv,
              ca_wo, ffn_norm, ffn_w_gate, ffn_w_up, ffn_w_down, final_norm):
    for i in range(DEPTH):
        j = i // 2
        if i % 2 == 0:
            hn = rmsnorm(x, e_norm[j])
            proj = hn @ e_w_in[j]
            a_u, a_v, b_a, b_g = jnp.split(proj, [A_WIDTH, 2 * A_WIDTH, 2 * A_WIDTH + B_WIDTH], axis=-1)
            out_a = gmlp_spatial_gate(jax.nn.gelu(a_u), jax.nn.gelu(a_v), e_gmlp_w[j], e_gmlp_b[j])
            out_b = conformer_conv(b_a, b_g, e_conv_w[j], e_conv_b[j], e_conv_ln_g[j], e_conv_ln_b[j])
            mix = jnp.concatenate([out_a, out_b], axis=-1) @ e_w_out[j]
        else:
            hn = rmsnorm(x, o_norm[j])
            u = hn @ o_w_in[j]
            y = s5_layer(u, o_lam_re[j], o_lam_im[j], o_log_dt[j], o_b_re[j], o_b_im[j],
                         o_c_re[j], o_c_im[j], o_d[j])
            o = jax.nn.gelu(y) @ o_w_out[j]
            mix = o[..., :D_MODEL] * jax.nn.sigmoid(o[..., D_MODEL:])
        x = x + mix
        x = x + cross_attention(rmsnorm(x, ca_norm[i]), rmsnorm(mem, ca_mem_norm[i]),
                                ca_wq[i], ca_wk[i], ca_wv[i], ca_wo[i])
        x = x + swiglu(rmsnorm(x, ffn_norm[i]), ffn_w_gate[i], ffn_w_up[i], ffn_w_down[i])
    return rmsnorm(x, final_norm)


import jax as _jax
import jax.numpy as _jnp

TWIN_FORMAT = 'train_step'
FWD_PARAMS = ['x', 'mem', 'e_norm', 'e_w_in', 'e_gmlp_w', 'e_gmlp_b', 'e_conv_w', 'e_conv_b', 'e_conv_ln_g', 'e_conv_ln_b', 'e_w_out', 'o_norm', 'o_w_in', 'o_lam_re', 'o_lam_im', 'o_log_dt', 'o_b_re', 'o_b_im', 'o_c_re', 'o_c_im', 'o_d', 'o_w_out', 'ca_norm', 'ca_mem_norm', 'ca_wq', 'ca_wk', 'ca_wv', 'ca_wo', 'ffn_norm', 'ffn_w_gate', 'ffn_w_up', 'ffn_w_down', 'final_norm']
TWIN_WEIGHTS = ['e_norm', 'e_w_in', 'e_gmlp_w', 'e_gmlp_b', 'e_conv_w', 'e_conv_b', 'e_conv_ln_g', 'e_conv_ln_b', 'e_w_out', 'o_norm', 'o_w_in', 'o_lam_re', 'o_lam_im', 'o_log_dt', 'o_b_re', 'o_b_im', 'o_c_re', 'o_c_im', 'o_d', 'o_w_out', 'ca_norm', 'ca_mem_norm', 'ca_wq', 'ca_wk', 'ca_wv', 'ca_wo', 'ffn_norm', 'ffn_w_gate', 'ffn_w_up', 'ffn_w_down', 'final_norm']
TWIN_DIFF_INPUT = 'x'
TWIN_INPUTS = ['x', 'mem', 'e_norm', 'e_w_in', 'e_gmlp_w', 'e_gmlp_b', 'e_conv_w', 'e_conv_b', 'e_conv_ln_g', 'e_conv_ln_b', 'e_w_out', 'o_norm', 'o_w_in', 'o_lam_re', 'o_lam_im', 'o_log_dt', 'o_b_re', 'o_b_im', 'o_c_re', 'o_c_im', 'o_d', 'o_w_out', 'ca_norm', 'ca_mem_norm', 'ca_wq', 'ca_wk', 'ca_wv', 'ca_wo', 'ffn_norm', 'ffn_w_gate', 'ffn_w_up', 'ffn_w_down', 'final_norm', 'loss_target', 'm_e_norm', 'm_e_w_in', 'm_e_gmlp_w', 'm_e_gmlp_b', 'm_e_conv_w', 'm_e_conv_b', 'm_e_conv_ln_g', 'm_e_conv_ln_b', 'm_e_w_out', 'm_o_norm', 'm_o_w_in', 'm_o_lam_re', 'm_o_lam_im', 'm_o_log_dt', 'm_o_b_re', 'm_o_b_im', 'm_o_c_re', 'm_o_c_im', 'm_o_d', 'm_o_w_out', 'm_ca_norm', 'm_ca_mem_norm', 'm_ca_wq', 'm_ca_wk', 'm_ca_wv', 'm_ca_wo', 'm_ffn_norm', 'm_ffn_w_gate', 'm_ffn_w_up', 'm_ffn_w_down', 'm_final_norm', 'v_e_norm', 'v_e_w_in', 'v_e_gmlp_w', 'v_e_gmlp_b', 'v_e_conv_w', 'v_e_conv_b', 'v_e_conv_ln_g', 'v_e_conv_ln_b', 'v_e_w_out', 'v_o_norm', 'v_o_w_in', 'v_o_lam_re', 'v_o_lam_im', 'v_o_log_dt', 'v_o_b_re', 'v_o_b_im', 'v_o_c_re', 'v_o_c_im', 'v_o_d', 'v_o_w_out', 'v_ca_norm', 'v_ca_mem_norm', 'v_ca_wq', 'v_ca_wk', 'v_ca_wv', 'v_ca_wo', 'v_ffn_norm', 'v_ffn_w_gate', 'v_ffn_w_up', 'v_ffn_w_down', 'v_final_norm']
TWIN_OUTPUTS = ['loss', 'grad_x', 'grad_e_norm', 'grad_e_w_in', 'grad_e_gmlp_w', 'grad_e_gmlp_b', 'grad_e_conv_w', 'grad_e_conv_b', 'grad_e_conv_ln_g', 'grad_e_conv_ln_b', 'grad_e_w_out', 'grad_o_norm', 'grad_o_w_in', 'grad_o_lam_re', 'grad_o_lam_im', 'grad_o_log_dt', 'grad_o_b_re', 'grad_o_b_im', 'grad_o_c_re', 'grad_o_c_im', 'grad_o_d', 'grad_o_w_out', 'grad_ca_norm', 'grad_ca_mem_norm', 'grad_ca_wq', 'grad_ca_wk', 'grad_ca_wv', 'grad_ca_wo', 'grad_ffn_norm', 'grad_ffn_w_gate', 'grad_ffn_w_up', 'grad_ffn_w_down', 'grad_final_norm', 'delta_e_norm', 'delta_e_w_in', 'delta_e_gmlp_w', 'delta_e_gmlp_b', 'delta_e_conv_w', 'delta_e_conv_b', 'delta_e_conv_ln_g', 'delta_e_conv_ln_b', 'delta_e_w_out', 'delta_o_norm', 'delta_o_w_in', 'delta_o_lam_re', 'delta_o_lam_im', 'delta_o_log_dt', 'delta_o_b_re', 'delta_o_b_im', 'delta_o_c_re', 'delta_o_c_im', 'delta_o_d', 'delta_o_w_out', 'delta_ca_norm', 'delta_ca_mem_norm', 'delta_ca_wq', 'delta_ca_wk', 'delta_ca_wv', 'delta_ca_wo', 'delta_ffn_norm', 'delta_ffn_w_gate', 'delta_ffn_w_up', 'delta_ffn_w_down', 'delta_final_norm', 'new_m_e_norm', 'new_m_e_w_in', 'new_m_e_gmlp_w', 'new_m_e_gmlp_b', 'new_m_e_conv_w', 'new_m_e_conv_b', 'new_m_e_conv_ln_g', 'new_m_e_conv_ln_b', 'new_m_e_w_out', 'new_m_o_norm', 'new_m_o_w_in', 'new_m_o_lam_re', 'new_m_o_lam_im', 'new_m_o_log_dt', 'new_m_o_b_re', 'new_m_o_b_im', 'new_m_o_c_re', 'new_m_o_c_im', 'new_m_o_d', 'new_m_o_w_out', 'new_m_ca_norm', 'new_m_ca_mem_norm', 'new_m_ca_wq', 'new_m_ca_wk', 'new_m_ca_wv', 'new_m_ca_wo', 'new_m_ffn_norm', 'new_m_ffn_w_gate', 'new_m_ffn_w_up', 'new_m_ffn_w_down', 'new_m_final_norm', 'new_v_e_norm', 'new_v_e_w_in', 'new_v_e_gmlp_w', 'new_v_e_gmlp_b', 'new_v_e_conv_w', 'new_v_e_conv_b', 'new_v_e_conv_ln_g', 'new_v_e_conv_ln_b', 'new_v_e_w_out', 'new_v_o_norm', 'new_v_o_w_in', 'new_v_o_lam_re', 'new_v_o_lam_im', 'new_v_o_log_dt', 'new_v_o_b_re', 'new_v_o_b_im', 'new_v_o_c_re', 'new_v_o_c_im', 'new_v_o_d', 'new_v_o_w_out', 'new_v_ca_norm', 'new_v_ca_mem_norm', 'new_v_ca_wq', 'new_v_ca_wk', 'new_v_ca_wv', 'new_v_ca_wo', 'new_v_ffn_norm', 'new_v_ffn_w_gate', 'new_v_ffn_w_up', 'new_v_ffn_w_down', 'new_v_final_norm']
TWIN_LEAF_KINDS = {'loss': 'loss', 'grad_x': 'grad_x', 'grad_e_norm': 'grad_w', 'grad_e_w_in': 'grad_w', 'grad_e_gmlp_w': 'grad_w', 'grad_e_gmlp_b': 'grad_w', 'grad_e_conv_w': 'grad_w', 'grad_e_conv_b': 'grad_w', 'grad_e_conv_ln_g': 'grad_w', 'grad_e_conv_ln_b': 'grad_w', 'grad_e_w_out': 'grad_w', 'grad_o_norm': 'grad_w', 'grad_o_w_in': 'grad_w', 'grad_o_lam_re': 'grad_w', 'grad_o_lam_im': 'grad_w', 'grad_o_log_dt': 'grad_w', 'grad_o_b_re': 'grad_w', 'grad_o_b_im': 'grad_w', 'grad_o_c_re': 'grad_w', 'grad_o_c_im': 'grad_w', 'grad_o_d': 'grad_w', 'grad_o_w_out': 'grad_w', 'grad_ca_norm': 'grad_w', 'grad_ca_mem_norm': 'grad_w', 'grad_ca_wq': 'grad_w', 'grad_ca_wk': 'grad_w', 'grad_ca_wv': 'grad_w', 'grad_ca_wo': 'grad_w', 'grad_ffn_norm': 'grad_w', 'grad_ffn_w_gate': 'grad_w', 'grad_ffn_w_up': 'grad_w', 'grad_ffn_w_down': 'grad_w', 'grad_final_norm': 'grad_w', 'delta_e_norm': 'delta_w', 'delta_e_w_in': 'delta_w', 'delta_e_gmlp_w': 'delta_w', 'delta_e_gmlp_b': 'delta_w', 'delta_e_conv_w': 'delta_w', 'delta_e_conv_b': 'delta_w', 'delta_e_conv_ln_g': 'delta_w', 'delta_e_conv_ln_b': 'delta_w', 'delta_e_w_out': 'delta_w', 'delta_o_norm': 'delta_w', 'delta_o_w_in': 'delta_w', 'delta_o_lam_re': 'delta_w', 'delta_o_lam_im': 'delta_w', 'delta_o_log_dt': 'delta_w', 'delta_o_b_re': 'delta_w', 'delta_o_b_im': 'delta_w', 'delta_o_c_re': 'delta_w', 'delta_o_c_im': 'delta_w', 'delta_o_d': 'delta_w', 'delta_o_w_out': 'delta_w', 'delta_ca_norm': 'delta_w', 'delta_ca_mem_norm': 'delta_w', 'delta_ca_wq': 'delta_w', 'delta_ca_wk': 'delta_w', 'delta_ca_wv': 'delta_w', 'delta_ca_wo': 'delta_w', 'delta_ffn_norm': 'delta_w', 'delta_ffn_w_gate': 'delta_w', 'delta_ffn_w_up': 'delta_w', 'delta_ffn_w_down': 'delta_w', 'delta_final_norm': 'delta_w', 'new_m_e_norm': 'new_m', 'new_m_e_w_in': 'new_m', 'new_m_e_gmlp_w': 'new_m', 'new_m_e_gmlp_b': 'new_m', 'new_m_e_conv_w': 'new_m', 'new_m_e_conv_b': 'new_m', 'new_m_e_conv_ln_g': 'new_m', 'new_m_e_conv_ln_b': 'new_m', 'new_m_e_w_out': 'new_m', 'new_m_o_norm': 'new_m', 'new_m_o_w_in': 'new_m', 'new_m_o_lam_re': 'new_m', 'new_m_o_lam_im': 'new_m', 'new_m_o_log_dt': 'new_m', 'new_m_o_b_re': 'new_m', 'new_m_o_b_im': 'new_m', 'new_m_o_c_re': 'new_m', 'new_m_o_c_im': 'new_m', 'new_m_o_d': 'new_m', 'new_m_o_w_out': 'new_m', 'new_m_ca_norm': 'new_m', 'new_m_ca_mem_norm': 'new_m', 'new_m_ca_wq': 'new_m', 'new_m_ca_wk': 'new_m', 'new_m_ca_wv': 'new_m', 'new_m_ca_wo': 'new_m', 'new_m_ffn_norm': 'new_m', 'new_m_ffn_w_gate': 'new_m', 'new_m_ffn_w_up': 'new_m', 'new_m_ffn_w_down': 'new_m', 'new_m_final_norm': 'new_m', 'new_v_e_norm': 'new_v', 'new_v_e_w_in': 'new_v', 'new_v_e_gmlp_w': 'new_v', 'new_v_e_gmlp_b': 'new_v', 'new_v_e_conv_w': 'new_v', 'new_v_e_conv_b': 'new_v', 'new_v_e_conv_ln_g': 'new_v', 'new_v_e_conv_ln_b': 'new_v', 'new_v_e_w_out': 'new_v', 'new_v_o_norm': 'new_v', 'new_v_o_w_in': 'new_v', 'new_v_o_lam_re': 'new_v', 'new_v_o_lam_im': 'new_v', 'new_v_o_log_dt': 'new_v', 'new_v_o_b_re': 'new_v', 'new_v_o_b_im': 'new_v', 'new_v_o_c_re': 'new_v', 'new_v_o_c_im': 'new_v', 'new_v_o_d': 'new_v', 'new_v_o_w_out': 'new_v', 'new_v_ca_norm': 'new_v', 'new_v_ca_mem_norm': 'new_v', 'new_v_ca_wq': 'new_v', 'new_v_ca_wk': 'new_v', 'new_v_ca_wv': 'new_v', 'new_v_ca_wo': 'new_v', 'new_v_ffn_norm': 'new_v', 'new_v_ffn_w_gate': 'new_v', 'new_v_ffn_w_up': 'new_v', 'new_v_ffn_w_down': 'new_v', 'new_v_final_norm': 'new_v'}


def _forward(args):
    return _fwd_reference(*[args[k] for k in FWD_PARAMS])


def _output_shape():
    out = _jax.eval_shape(lambda: _forward(_fwd_setup_inputs(0)))
    return out.shape, out.dtype

N_MICROBATCH = 1
ADAM_LR = 0.001
ADAM_B1 = 0.9
ADAM_B2 = 0.999
ADAM_EPS = 1e-08
ADAM_WD = 0.01
ADAM_STEP = 10
PER_EXAMPLE_BATCH_AXIS = {'x': 0, 'mem': 0, 'loss_target': 0}
SHARED_INPUTS = []
_WEIGHT_DTYPES = {'e_norm': _jnp.float32, 'e_w_in': _jnp.float32, 'e_gmlp_w': _jnp.float32, 'e_gmlp_b': _jnp.float32, 'e_conv_w': _jnp.float32, 'e_conv_b': _jnp.float32, 'e_conv_ln_g': _jnp.float32, 'e_conv_ln_b': _jnp.float32, 'e_w_out': _jnp.float32, 'o_norm': _jnp.float32, 'o_w_in': _jnp.float32, 'o_lam_re': _jnp.float32, 'o_lam_im': _jnp.float32, 'o_log_dt': _jnp.float32, 'o_b_re': _jnp.float32, 'o_b_im': _jnp.float32, 'o_c_re': _jnp.float32, 'o_c_im': _jnp.float32, 'o_d': _jnp.float32, 'o_w_out': _jnp.float32, 'ca_norm': _jnp.float32, 'ca_mem_norm': _jnp.float32, 'ca_wq': _jnp.float32, 'ca_wk': _jnp.float32, 'ca_wv': _jnp.float32, 'ca_wo': _jnp.float32, 'ffn_norm': _jnp.float32, 'ffn_w_gate': _jnp.float32, 'ffn_w_up': _jnp.float32, 'ffn_w_down': _jnp.float32, 'final_norm': _jnp.float32}
MOMENT_SCALE = {'e_norm': 1.295298e-01, 'e_w_in': 9.243375e-02, 'e_gmlp_w': 9.918204e-02, 'e_gmlp_b': 1.261979e-01, 'e_conv_w': 1.105199e-01, 'e_conv_b': 2.298932e-01, 'e_conv_ln_g': 1.558036e-01, 'e_conv_ln_b': 1.167821e-01, 'e_w_out': 1.164300e-01, 'o_norm': 5.727645e-02, 'o_w_in': 7.201833e-02, 'o_lam_re': 5.372877e-03, 'o_lam_im': 6.269253e-03, 'o_log_dt': 3.592405e+00, 'o_b_re': 3.084327e-03, 'o_b_im': 3.065679e-03, 'o_c_re': 6.502791e-03, 'o_c_im': 6.004167e-03, 'o_d': 8.259783e-02, 'o_w_out': 3.631188e-02, 'ca_norm': 1.655557e-02, 'ca_mem_norm': 2.499939e-02, 'ca_wq': 1.647390e-02, 'ca_wk': 1.649343e-02, 'ca_wv': 1.680213e-02, 'ca_wo': 1.691794e-02, 'ffn_norm': 1.135507e-01, 'ffn_w_gate': 4.856844e-02, 'ffn_w_up': 4.704323e-02, 'ffn_w_down': 7.788958e-02, 'final_norm': 3.202046e+01}


def _to_microbatches(a, axis):
    t = _jnp.moveaxis(a, axis, 0)
    t = t.reshape((N_MICROBATCH, t.shape[0] // N_MICROBATCH) + t.shape[1:])
    return _jnp.moveaxis(t, 1, axis + 1)


def setup_inputs(seed: int = 0) -> dict:
    inp = _fwd_setup_inputs(seed)
    key = _jax.random.fold_in(_jax.random.key(seed), 7919)
    shape, _ = _output_shape()
    out = dict(inp)
    out["loss_target"] = _jax.random.normal(_jax.random.fold_in(key, 0), shape, _jnp.float32)
    for i, name in enumerate(TWIN_WEIGHTS):
        w = inp[name].astype(_jnp.float32)
        if MOMENT_SCALE is None:
            s = _jnp.sqrt(_jnp.mean(_jnp.square(w)) + 1e-30)
        else:
            s = MOMENT_SCALE[name]
        km, kv = _jax.random.split(_jax.random.fold_in(key, i + 1))
        out[name] = w
        out["m_" + name] = s * _jax.random.normal(km, w.shape, _jnp.float32)
        out["v_" + name] = (s * s) * _jax.random.uniform(kv, w.shape, _jnp.float32, 0.5, 1.5)
    if N_MICROBATCH > 1:
        for name, axis in PER_EXAMPLE_BATCH_AXIS.items():
            out[name] = _to_microbatches(out[name], axis)
    return {'x': out['x'], 'mem': out['mem'], 'e_norm': out['e_norm'], 'e_w_in': out['e_w_in'], 'e_gmlp_w': out['e_gmlp_w'], 'e_gmlp_b': out['e_gmlp_b'], 'e_conv_w': out['e_conv_w'], 'e_conv_b': out['e_conv_b'], 'e_conv_ln_g': out['e_conv_ln_g'], 'e_conv_ln_b': out['e_conv_ln_b'], 'e_w_out': out['e_w_out'], 'o_norm': out['o_norm'], 'o_w_in': out['o_w_in'], 'o_lam_re': out['o_lam_re'], 'o_lam_im': out['o_lam_im'], 'o_log_dt': out['o_log_dt'], 'o_b_re': out['o_b_re'], 'o_b_im': out['o_b_im'], 'o_c_re': out['o_c_re'], 'o_c_im': out['o_c_im'], 'o_d': out['o_d'], 'o_w_out': out['o_w_out'], 'ca_norm': out['ca_norm'], 'ca_mem_norm': out['ca_mem_norm'], 'ca_wq': out['ca_wq'], 'ca_wk': out['ca_wk'], 'ca_wv': out['ca_wv'], 'ca_wo': out['ca_wo'], 'ffn_norm': out['ffn_norm'], 'ffn_w_gate': out['ffn_w_gate'], 'ffn_w_up': out['ffn_w_up'], 'ffn_w_down': out['ffn_w_down'], 'final_norm': out['final_norm'], 'loss_target': out['loss_target'], 'm_e_norm': out['m_e_norm'], 'm_e_w_in': out['m_e_w_in'], 'm_e_gmlp_w': out['m_e_gmlp_w'], 'm_e_gmlp_b': out['m_e_gmlp_b'], 'm_e_conv_w': out['m_e_conv_w'], 'm_e_conv_b': out['m_e_conv_b'], 'm_e_conv_ln_g': out['m_e_conv_ln_g'], 'm_e_conv_ln_b': out['m_e_conv_ln_b'], 'm_e_w_out': out['m_e_w_out'], 'm_o_norm': out['m_o_norm'], 'm_o_w_in': out['m_o_w_in'], 'm_o_lam_re': out['m_o_lam_re'], 'm_o_lam_im': out['m_o_lam_im'], 'm_o_log_dt': out['m_o_log_dt'], 'm_o_b_re': out['m_o_b_re'], 'm_o_b_im': out['m_o_b_im'], 'm_o_c_re': out['m_o_c_re'], 'm_o_c_im': out['m_o_c_im'], 'm_o_d': out['m_o_d'], 'm_o_w_out': out['m_o_w_out'], 'm_ca_norm': out['m_ca_norm'], 'm_ca_mem_norm': out['m_ca_mem_norm'], 'm_ca_wq': out['m_ca_wq'], 'm_ca_wk': out['m_ca_wk'], 'm_ca_wv': out['m_ca_wv'], 'm_ca_wo': out['m_ca_wo'], 'm_ffn_norm': out['m_ffn_norm'], 'm_ffn_w_gate': out['m_ffn_w_gate'], 'm_ffn_w_up': out['m_ffn_w_up'], 'm_ffn_w_down': out['m_ffn_w_down'], 'm_final_norm': out['m_final_norm'], 'v_e_norm': out['v_e_norm'], 'v_e_w_in': out['v_e_w_in'], 'v_e_gmlp_w': out['v_e_gmlp_w'], 'v_e_gmlp_b': out['v_e_gmlp_b'], 'v_e_conv_w': out['v_e_conv_w'], 'v_e_conv_b': out['v_e_conv_b'], 'v_e_conv_ln_g': out['v_e_conv_ln_g'], 'v_e_conv_ln_b': out['v_e_conv_ln_b'], 'v_e_w_out': out['v_e_w_out'], 'v_o_norm': out['v_o_norm'], 'v_o_w_in': out['v_o_w_in'], 'v_o_lam_re': out['v_o_lam_re'], 'v_o_lam_im': out['v_o_lam_im'], 'v_o_log_dt': out['v_o_log_dt'], 'v_o_b_re': out['v_o_b_re'], 'v_o_b_im': out['v_o_b_im'], 'v_o_c_re': out['v_o_c_re'], 'v_o_c_im': out['v_o_c_im'], 'v_o_d': out['v_o_d'], 'v_o_w_out': out['v_o_w_out'], 'v_ca_norm': out['v_ca_norm'], 'v_ca_mem_norm': out['v_ca_mem_norm'], 'v_ca_wq': out['v_ca_wq'], 'v_ca_wk': out['v_ca_wk'], 'v_ca_wv': out['v_ca_wv'], 'v_ca_wo': out['v_ca_wo'], 'v_ffn_norm': out['v_ffn_norm'], 'v_ffn_w_gate': out['v_ffn_w_gate'], 'v_ffn_w_up': out['v_ffn_w_up'], 'v_ffn_w_down': out['v_ffn_w_down'], 'v_final_norm': out['v_final_norm']}


def _loss(weights, diff, rest, loss_target):
    with _jax.named_scope("forward"):
        args = {**rest, TWIN_DIFF_INPUT: diff, **{k: w.astype(_WEIGHT_DTYPES[k]) for k, w in weights.items()}}
        y = _forward(args)
    with _jax.named_scope("loss_head"):
        err = _jnp.square(y.astype(_jnp.float32) - loss_target)
        return 0.5 * _jnp.sum(_jnp.mean(err, axis=-1)) if err.ndim else 0.5 * err


def _adamw(w, g, m, v):
    m = ADAM_B1 * m + (1.0 - ADAM_B1) * g
    v = ADAM_B2 * v + (1.0 - ADAM_B2) * _jnp.square(g)
    m_hat = m / (1.0 - ADAM_B1 ** ADAM_STEP)
    v_hat = v / (1.0 - ADAM_B2 ** ADAM_STEP)
    delta = -ADAM_LR * (m_hat / (_jnp.sqrt(v_hat) + ADAM_EPS) + ADAM_WD * w)
    return delta, m, v


def reference(x, mem, e_norm, e_w_in, e_gmlp_w, e_gmlp_b, e_conv_w, e_conv_b, e_conv_ln_g, e_conv_ln_b, e_w_out, o_norm, o_w_in, o_lam_re, o_lam_im, o_log_dt, o_b_re, o_b_im, o_c_re, o_c_im, o_d, o_w_out, ca_norm, ca_mem_norm, ca_wq, ca_wk, ca_wv, ca_wo, ffn_norm, ffn_w_gate, ffn_w_up, ffn_w_down, final_norm, loss_target, m_e_norm, m_e_w_in, m_e_gmlp_w, m_e_gmlp_b, m_e_conv_w, m_e_conv_b, m_e_conv_ln_g, m_e_conv_ln_b, m_e_w_out, m_o_norm, m_o_w_in, m_o_lam_re, m_o_lam_im, m_o_log_dt, m_o_b_re, m_o_b_im, m_o_c_re, m_o_c_im, m_o_d, m_o_w_out, m_ca_norm, m_ca_mem_norm, m_ca_wq, m_ca_wk, m_ca_wv, m_ca_wo, m_ffn_norm, m_ffn_w_gate, m_ffn_w_up, m_ffn_w_down, m_final_norm, v_e_norm, v_e_w_in, v_e_gmlp_w, v_e_gmlp_b, v_e_conv_w, v_e_conv_b, v_e_conv_ln_g, v_e_conv_ln_b, v_e_w_out, v_o_norm, v_o_w_in, v_o_lam_re, v_o_lam_im, v_o_log_dt, v_o_b_re, v_o_b_im, v_o_c_re, v_o_c_im, v_o_d, v_o_w_out, v_ca_norm, v_ca_mem_norm, v_ca_wq, v_ca_wk, v_ca_wv, v_ca_wo, v_ffn_norm, v_ffn_w_gate, v_ffn_w_up, v_ffn_w_down, v_final_norm):
    given = dict(x=x, mem=mem, e_norm=e_norm, e_w_in=e_w_in, e_gmlp_w=e_gmlp_w, e_gmlp_b=e_gmlp_b, e_conv_w=e_conv_w, e_conv_b=e_conv_b, e_conv_ln_g=e_conv_ln_g, e_conv_ln_b=e_conv_ln_b, e_w_out=e_w_out, o_norm=o_norm, o_w_in=o_w_in, o_lam_re=o_lam_re, o_lam_im=o_lam_im, o_log_dt=o_log_dt, o_b_re=o_b_re, o_b_im=o_b_im, o_c_re=o_c_re, o_c_im=o_c_im, o_d=o_d, o_w_out=o_w_out, ca_norm=ca_norm, ca_mem_norm=ca_mem_norm, ca_wq=ca_wq, ca_wk=ca_wk, ca_wv=ca_wv, ca_wo=ca_wo, ffn_norm=ffn_norm, ffn_w_gate=ffn_w_gate, ffn_w_up=ffn_w_up, ffn_w_down=ffn_w_down, final_norm=final_norm, loss_target=loss_target, m_e_norm=m_e_norm, m_e_w_in=m_e_w_in, m_e_gmlp_w=m_e_gmlp_w, m_e_gmlp_b=m_e_gmlp_b, m_e_conv_w=m_e_conv_w, m_e_conv_b=m_e_conv_b, m_e_conv_ln_g=m_e_conv_ln_g, m_e_conv_ln_b=m_e_conv_ln_b, m_e_w_out=m_e_w_out, m_o_norm=m_o_norm, m_o_w_in=m_o_w_in, m_o_lam_re=m_o_lam_re, m_o_lam_im=m_o_lam_im, m_o_log_dt=m_o_log_dt, m_o_b_re=m_o_b_re, m_o_b_im=m_o_b_im, m_o_c_re=m_o_c_re, m_o_c_im=m_o_c_im, m_o_d=m_o_d, m_o_w_out=m_o_w_out, m_ca_norm=m_ca_norm, m_ca_mem_norm=m_ca_mem_norm, m_ca_wq=m_ca_wq, m_ca_wk=m_ca_wk, m_ca_wv=m_ca_wv, m_ca_wo=m_ca_wo, m_ffn_norm=m_ffn_norm, m_ffn_w_gate=m_ffn_w_gate, m_ffn_w_up=m_ffn_w_up, m_ffn_w_down=m_ffn_w_down, m_final_norm=m_final_norm, v_e_norm=v_e_norm, v_e_w_in=v_e_w_in, v_e_gmlp_w=v_e_gmlp_w, v_e_gmlp_b=v_e_gmlp_b, v_e_conv_w=v_e_conv_w, v_e_conv_b=v_e_conv_b, v_e_conv_ln_g=v_e_conv_ln_g, v_e_conv_ln_b=v_e_conv_ln_b, v_e_w_out=v_e_w_out, v_o_norm=v_o_norm, v_o_w_in=v_o_w_in, v_o_lam_re=v_o_lam_re, v_o_lam_im=v_o_lam_im, v_o_log_dt=v_o_log_dt, v_o_b_re=v_o_b_re, v_o_b_im=v_o_b_im, v_o_c_re=v_o_c_re, v_o_c_im=v_o_c_im, v_o_d=v_o_d, v_o_w_out=v_o_w_out, v_ca_norm=v_ca_norm, v_ca_mem_norm=v_ca_mem_norm, v_ca_wq=v_ca_wq, v_ca_wk=v_ca_wk, v_ca_wv=v_ca_wv, v_ca_wo=v_ca_wo, v_ffn_norm=v_ffn_norm, v_ffn_w_gate=v_ffn_w_gate, v_ffn_w_up=v_ffn_w_up, v_ffn_w_down=v_ffn_w_down, v_final_norm=v_final_norm)
    weights = {n: given[n] for n in TWIN_WEIGHTS}
    shared = {n: given[n] for n in SHARED_INPUTS}
    per_example = {n: given[n] for n in ['x', 'mem']}
    grad_fn = _jax.value_and_grad(_loss, argnums=(0, 1))

    def one_microbatch(ex, loss_target):
        ex = dict(ex)
        diff = ex.pop(TWIN_DIFF_INPUT)
        return grad_fn(weights, diff, {**shared, **ex}, loss_target)

    if N_MICROBATCH == 1:
        loss, (grad_w, grad_x) = one_microbatch(per_example, given["loss_target"])
    else:
        def body(carry, xs):
            loss_sum, grad_sum = carry
            l_k, (gw_k, gx_k) = one_microbatch(xs[0], xs[1])
            with _jax.named_scope("update"):
                return (loss_sum + l_k, _jax.tree.map(_jnp.add, grad_sum, gw_k)), gx_k

        init = (_jnp.zeros((), _jnp.float32), _jax.tree.map(_jnp.zeros_like, weights))
        (loss, grad_w), grad_x = _jax.lax.scan(body, init, (per_example, given["loss_target"]))
    with _jax.named_scope("update"):
        delta_w, new_m, new_v = {}, {}, {}
        for n in TWIN_WEIGHTS:
            delta_w[n], new_m[n], new_v[n] = _adamw(weights[n], grad_w[n], given["m_" + n], given["v_" + n])
    return (loss, grad_x, *[grad_w[n] for n in TWIN_WEIGHTS], *[delta_w[n] for n in TWIN_WEIGHTS],
            *[new_m[n] for n in TWIN_WEIGHTS], *[new_v[n] for n in TWIN_WEIGHTS])
```

```python
import functools
import math

import jax
import jax.numpy as jnp
from jax import lax
from jax.experimental import pallas as pl
from jax.experimental.pallas import tpu as pltpu

F32 = jnp.float32
BF16 = jnp.bfloat16
MESH = pl.DeviceIdType.MESH

EPS = 1e-6
D_MODEL = 1024
A_WIDTH = 512
A_GROUPS = 4
GMLP_BLOCK = 128
CHUNK = 64
B_WIDTH = 512
CONV_WIDTH = 31
CONV_HALO = 32
C_WIDTH = 512
C_GROUP_CH = 16
C_GROUPS = 32
C_STATE = 64
N_STATE = C_GROUPS * C_STATE
STATE_ROWS = 8
STATE_LANES = N_STATE // STATE_ROWS
CA_HEADS = 4
CA_HEAD_DIM = 256
FFN_HIDDEN = 2816

ADAM_LR = 0.001
ADAM_B1 = 0.9
ADAM_B2 = 0.999
ADAM_EPS = 1e-08
ADAM_WD = 0.01
ADAM_STEP = 10

VMEM_LIMIT = 48 * 1024 * 1024
SLAB_W = 1024
SMALL_W = 128
N_CHIPS = 4
N_DEV = 8


def _params(sem=None):
    return pltpu.CompilerParams(dimension_semantics=sem, vmem_limit_bytes=VMEM_LIMIT)


def _tile(n, pref, mult=128):
    if n <= pref:
        return n
    t = (pref // mult) * mult
    while t >= mult:
        if n % t == 0:
            return t
        t -= mult
    return n


_GELU_C = 0.7978845608028654
_GELU_A = 0.044715


def _gelu(x):
    t = jnp.tanh(_GELU_C * (x + _GELU_A * (x * x * x)))
    return 0.5 * x * (1.0 + t), t


def _gelu_grad(x, t):
    return 0.5 * (1.0 + t) + 0.5 * x * (1.0 - t * t) * (_GELU_C * (1.0 + 3.0 * _GELU_A * x * x))


def _sigmoid(x):
    return 1.0 / (1.0 + jnp.exp(-x))


def _mean(x):
    return jnp.mean(x, axis=-1, keepdims=True)


def _mm(a, b, *, ta=False, tb=False, add=None, out_dtype=F32, name):
    if ta:
        K, M = a.shape
    else:
        M, K = a.shape
    if tb:
        N, Kb = b.shape
    else:
        Kb, N = b.shape
    assert K == Kb, (name, a.shape, b.shape)
    tm, tn, tk = _tile(M, 512), _tile(N, 512), _tile(K, 512)
    nk = K // tk
    dims = (((0 if ta else 1,), (1 if tb else 0,)), ((), ()))
    has_add = add is not None

    def body(*refs):
        if has_add:
            a_ref, b_ref, add_ref, o_ref, acc_ref = refs
        else:
            a_ref, b_ref, o_ref, acc_ref = refs
        k = pl.program_id(2)

        @pl.when(k == 0)
        def _():
            acc_ref[...] = jnp.zeros_like(acc_ref)

        acc_ref[...] += lax.dot_general(a_ref[...].astype(BF16), b_ref[...].astype(BF16), dims,
                                        preferred_element_type=F32)

        @pl.when(k == nk - 1)
        def _():
            r = acc_ref[...]
            if has_add:
                r = r + add_ref[...]
            o_ref[...] = r.astype(o_ref.dtype)

    a_spec = pl.BlockSpec((tk, tm), lambda i, j, k: (k, i)) if ta else pl.BlockSpec((tm, tk), lambda i, j, k: (i, k))
    b_spec = pl.BlockSpec((tn, tk), lambda i, j, k: (j, k)) if tb else pl.BlockSpec((tk, tn), lambda i, j, k: (k, j))
    in_specs = [a_spec, b_spec]
    args = [a, b]
    if has_add:
        in_specs.append(pl.BlockSpec((tm, tn), lambda i, j, k: (i, j)))
        args.append(add)
    return pl.pallas_call(
        body, name=name, grid=(M // tm, N // tn, nk),
        in_specs=in_specs, out_specs=pl.BlockSpec((tm, tn), lambda i, j, k: (i, j)),
        out_shape=jax.ShapeDtypeStruct((M, N), out_dtype),
        scratch_shapes=[pltpu.VMEM((tm, tn), F32)],
        compiler_params=_params(("parallel", "parallel", "arbitrary")),
    )(*args)


def _rms_fwd(x, g, *, name):
    S, D = x.shape
    tr = _tile(S, 256, 8)

    def body(x_ref, g_ref, o_ref):
        xv = x_ref[...]
        r = lax.rsqrt(_mean(xv * xv) + EPS)
        o_ref[...] = ((xv * r) * g_ref[...]).astype(BF16)

    return pl.pallas_call(
        body, name=name, grid=(S // tr,),
        in_specs=[pl.BlockSpec((tr, D), lambda i: (i, 0)), pl.BlockSpec((1, D), lambda i: (0, 0))],
        out_specs=pl.BlockSpec((tr, D), lambda i: (i, 0)),
        out_shape=jax.ShapeDtypeStruct((S, D), BF16),
        compiler_params=_params(("parallel",)),
    )(x, g.reshape(1, D))


def _rms_bwd(x, g, dy, dres, *, name):
    S, D = x.shape
    tr = _tile(S, 256, 8)
    has_res = dres is not None

    def body(*refs):
        if has_res:
            x_ref, g_ref, dy_ref, dr_ref, dx_ref, dg_ref = refs
        else:
            x_ref, g_ref, dy_ref, dx_ref, dg_ref = refs
        xv = x_ref[...]
        r = lax.rsqrt(_mean(xv * xv) + EPS)
        xh = xv * r
        dyv = dy_ref[...]
        dyg = dyv * g_ref[...]
        dx = r * (dyg - xh * _mean(dyg * xh))
        if has_res:
            dx = dx + dr_ref[...]
        dx_ref[...] = dx

        @pl.when(pl.program_id(0) == 0)
        def _():
            dg_ref[...] = jnp.zeros_like(dg_ref)

        dg_ref[...] += jnp.sum(dyv * xh, axis=0, keepdims=True)

    row = pl.BlockSpec((tr, D), lambda i: (i, 0))
    vec = pl.BlockSpec((1, D), lambda i: (0, 0))
    in_specs = [row, vec, row] + ([row] if has_res else [])
    args = [x, g.reshape(1, D), dy] + ([dres] if has_res else [])
    dx, dg = pl.pallas_call(
        body, name=name, grid=(S // tr,), in_specs=in_specs, out_specs=[row, vec],
        out_shape=[jax.ShapeDtypeStruct((S, D), F32), jax.ShapeDtypeStruct((1, D), F32)],
        compiler_params=_params(("arbitrary",)),
    )(*args)
    return dx, dg


def _ln_stats(v):
    mu = _mean(v)
    xc = v - mu
    rstd = lax.rsqrt(_mean(xc * xc) + EPS)
    return xc * rstd, rstd


def _even_fwd(proj, wm, bcol, cw, cb, lg, lb, *, name):
    S = proj.shape[0]
    tm = _tile(S, 256)
    hb = tm // CONV_HALO
    nblk = tm // GMLP_BLOCK

    def body(p_ref, halo_ref, wm_ref, b_ref, cw_ref, cb_ref, lg_ref, lb_ref, mix_ref, hc_ref, hext_ref):
        i = pl.program_id(0)
        gu, _ = _gelu(p_ref[:, 0:A_WIDTH])
        gv, _ = _gelu(p_ref[:, A_WIDTH:2 * A_WIDTH])
        vn, _ = _ln_stats(gv)
        vnb = vn.astype(BF16)
        for n in range(nblk):
            rows = slice(n * GMLP_BLOCK, (n + 1) * GMLP_BLOCK)
            for g in range(A_GROUPS):
                cols = slice(g * GMLP_BLOCK, (g + 1) * GMLP_BLOCK)
                sg = jnp.dot(wm_ref[g], vnb[rows, cols], preferred_element_type=F32) + b_ref[g]
                mix_ref[rows, cols] = (gu[rows, cols] * sg).astype(BF16)
        h = p_ref[:, 1024:1536] * _sigmoid(p_ref[:, 1536:2048])
        hh = halo_ref[:, 0:B_WIDTH] * _sigmoid(halo_ref[:, B_WIDTH:2 * B_WIDTH])
        hext_ref[0:CONV_HALO, :] = jnp.where(i > 0, hh, 0.0)
        hext_ref[CONV_HALO:CONV_HALO + tm, :] = h
        acc = jnp.zeros((tm, B_WIDTH), F32)
        for k in range(CONV_WIDTH):
            acc = acc + cw_ref[k:k + 1, :] * hext_ref[pl.ds(k + CONV_HALO - CONV_WIDTH + 1, tm), :]
        hc = acc + cb_ref[...]
        hc_ref[...] = hc
        hhat, _ = _ln_stats(hc)
        hl = hhat * lg_ref[...] + lb_ref[...]
        mix_ref[:, A_WIDTH:A_WIDTH + B_WIDTH] = (hl * _sigmoid(hl)).astype(BF16)

    vec = pl.BlockSpec((1, B_WIDTH), lambda i: (0, 0))
    return pl.pallas_call(
        body, name=name, grid=(S // tm,),
        in_specs=[
            pl.BlockSpec((tm, 2048), lambda i: (i, 0)),
            pl.BlockSpec((CONV_HALO, 1024), lambda i: (jnp.maximum(i * hb - 1, 0), 1)),
            pl.BlockSpec((A_GROUPS, GMLP_BLOCK, GMLP_BLOCK), lambda i: (0, 0, 0)),
            pl.BlockSpec((A_GROUPS, GMLP_BLOCK, 1), lambda i: (0, 0, 0)),
            pl.BlockSpec((CONV_HALO, B_WIDTH), lambda i: (0, 0)),
            vec, vec, vec,
        ],
        out_specs=[pl.BlockSpec((tm, 1024), lambda i: (i, 0)), pl.BlockSpec((tm, B_WIDTH), lambda i: (i, 0))],
        out_shape=[jax.ShapeDtypeStruct((S, 1024), BF16), jax.ShapeDtypeStruct((S, B_WIDTH), F32)],
        scratch_shapes=[pltpu.VMEM((tm + CONV_HALO, B_WIDTH), F32)],
        compiler_params=_params(("parallel",)),
    )(proj, proj, wm, bcol, cw, cb, lg, lb)


def _even_bwd1(proj, dmix, hc, wm, wmt, bcol, lg, lb, *, name):
    S = proj.shape[0]
    tm = _tile(S, 256)
    nblk = tm // GMLP_BLOCK

    def body(p_ref, dm_ref, hc_ref, wm_ref, wmt_ref, b_ref, lg_ref, lb_ref,
             dpa_ref, dhc_ref, dwm_ref, db_ref, dlg_ref, dlb_ref, dcb_ref, dgu_ref, dvn_ref):
        @pl.when(pl.program_id(0) == 0)
        def _():
            dwm_ref[...] = jnp.zeros_like(dwm_ref)
            db_ref[...] = jnp.zeros_like(db_ref)
            dlg_ref[...] = jnp.zeros_like(dlg_ref)
            dlb_ref[...] = jnp.zeros_like(dlb_ref)
            dcb_ref[...] = jnp.zeros_like(dcb_ref)

        au = p_ref[:, 0:A_WIDTH]
        av = p_ref[:, A_WIDTH:2 * A_WIDTH]
        gu, tu = _gelu(au)
        gv, tv = _gelu(av)
        vn, rstd = _ln_stats(gv)
        vnb = vn.astype(BF16)
        for n in range(nblk):
            rows = slice(n * GMLP_BLOCK, (n + 1) * GMLP_BLOCK)
            for g in range(A_GROUPS):
                cols = slice(g * GMLP_BLOCK, (g + 1) * GMLP_BLOCK)
                vb = vnb[rows, cols]
                sg = jnp.dot(wm_ref[g], vb, preferred_element_type=F32) + b_ref[g]
                da = dm_ref[rows, cols]
                dsg = da * gu[rows, cols]
                dgu_ref[rows, cols] = da * sg
                dsgb = dsg.astype(BF16)
                dwm_ref[g] += lax.dot_general(dsgb, vb, (((1,), (1,)), ((), ())), preferred_element_type=F32)
                db_ref[g] += jnp.sum(dsg, axis=1, keepdims=True)
                dvn_ref[rows, cols] = jnp.dot(wmt_ref[g], dsgb, preferred_element_type=F32)
        dvn = dvn_ref[...]
        dgv = rstd * (dvn - _mean(dvn) - vn * _mean(dvn * vn))
        dpa_ref[:, 0:A_WIDTH] = (dgu_ref[...] * _gelu_grad(au, tu)).astype(BF16)
        dpa_ref[:, A_WIDTH:2 * A_WIDTH] = (dgv * _gelu_grad(av, tv)).astype(BF16)
        hhat, rstd2 = _ln_stats(hc_ref[...])
        lgv = lg_ref[...]
        hl = hhat * lgv + lb_ref[...]
        s = _sigmoid(hl)
        dhl = dm_ref[:, A_WIDTH:A_WIDTH + B_WIDTH] * (s * (1.0 + hl * (1.0 - s)))
        dlg_ref[...] += jnp.sum(dhl * hhat, axis=0, keepdims=True)
        dlb_ref[...] += jnp.sum(dhl, axis=0, keepdims=True)
        dhh = dhl * lgv
        dhc = rstd2 * (dhh - _mean(dhh) - hhat * _mean(dhh * hhat))
        dcb_ref[...] += jnp.sum(dhc, axis=0, keepdims=True)
        dhc_ref[...] = dhc

    vec = pl.BlockSpec((1, B_WIDTH), lambda i: (0, 0))
    w3 = pl.BlockSpec((A_GROUPS, GMLP_BLOCK, GMLP_BLOCK), lambda i: (0, 0, 0))
    b3 = pl.BlockSpec((A_GROUPS, GMLP_BLOCK, 1), lambda i: (0, 0, 0))
    return pl.pallas_call(
        body, name=name, grid=(S // tm,),
        in_specs=[
            pl.BlockSpec((tm, 1024), lambda i: (i, 0)),
            pl.BlockSpec((tm, 1024), lambda i: (i, 0)),
            pl.BlockSpec((tm, B_WIDTH), lambda i: (i, 0)),
            w3, w3, b3, vec, vec,
        ],
        out_specs=[pl.BlockSpec((tm, 1024), lambda i: (i, 0)), pl.BlockSpec((tm, B_WIDTH), lambda i: (i, 0)),
                   w3, b3, vec, vec, vec],
        out_shape=[
            jax.ShapeDtypeStruct((S, 1024), BF16), jax.ShapeDtypeStruct((S, B_WIDTH), F32),
            jax.ShapeDtypeStruct((A_GROUPS, GMLP_BLOCK, GMLP_BLOCK), F32),
            jax.ShapeDtypeStruct((A_GROUPS, GMLP_BLOCK, 1), F32),
            jax.ShapeDtypeStruct((1, B_WIDTH), F32), jax.ShapeDtypeStruct((1, B_WIDTH), F32),
            jax.ShapeDtypeStruct((1, B_WIDTH), F32),
        ],
        scratch_shapes=[pltpu.VMEM((tm, A_WIDTH), F32), pltpu.VMEM((tm, A_WIDTH), F32)],
        compiler_params=_params(("arbitrary",)),
    )(proj, dmix, hc, wm, wmt, bcol, lg, lb)


def _even_bwd2(proj, dhc, cw, *, name):
    S = proj.shape[0]
    tm = _tile(S, 256)
    hb = tm // CONV_HALO
    nt = S // tm
    last_halo = S // CONV_HALO - 1
    lo = CONV_HALO - CONV_WIDTH + 1

    def body(p_ref, halo_ref, d_ref, dnext_ref, cw_ref, dpb_ref, dcw_ref, hext_ref, dext_ref):
        i = pl.program_id(0)

        @pl.when(i == 0)
        def _():
            dcw_ref[...] = jnp.zeros_like(dcw_ref)

        ba = p_ref[:, 0:B_WIDTH]
        sg = _sigmoid(p_ref[:, B_WIDTH:2 * B_WIDTH])
        hh = halo_ref[:, 0:B_WIDTH] * _sigmoid(halo_ref[:, B_WIDTH:2 * B_WIDTH])
        hext_ref[0:CONV_HALO, :] = jnp.where(i > 0, hh, 0.0)
        hext_ref[CONV_HALO:CONV_HALO + tm, :] = ba * sg
        dhc_t = d_ref[...]
        dext_ref[0:tm, :] = dhc_t
        dext_ref[tm:tm + CONV_HALO, :] = jnp.where(i < nt - 1, dnext_ref[...], 0.0)
        dh = jnp.zeros((tm, B_WIDTH), F32)
        for k in range(CONV_WIDTH):
            dh = dh + cw_ref[k:k + 1, :] * dext_ref[pl.ds(CONV_WIDTH - 1 - k, tm), :]
            dcw_ref[k:k + 1, :] += jnp.sum(dhc_t * hext_ref[pl.ds(k + lo, tm), :], axis=0, keepdims=True)
        dpb_ref[:, 0:B_WIDTH] = (dh * sg).astype(BF16)
        dpb_ref[:, B_WIDTH:2 * B_WIDTH] = (dh * ba * sg * (1.0 - sg)).astype(BF16)

    return pl.pallas_call(
        body, name=name, grid=(nt,),
        in_specs=[
            pl.BlockSpec((tm, 1024), lambda i: (i, 1)),
            pl.BlockSpec((CONV_HALO, 1024), lambda i: (jnp.maximum(i * hb - 1, 0), 1)),
            pl.BlockSpec((tm, B_WIDTH), lambda i: (i, 0)),
            pl.BlockSpec((CONV_HALO, B_WIDTH), lambda i: (jnp.minimum((i + 1) * hb, last_halo), 0)),
            pl.BlockSpec((CONV_HALO, B_WIDTH), lambda i: (0, 0)),
        ],
        out_specs=[pl.BlockSpec((tm, 1024), lambda i: (i, 0)), pl.BlockSpec((CONV_HALO, B_WIDTH), lambda i: (0, 0))],
        out_shape=[jax.ShapeDtypeStruct((S, 1024), BF16), jax.ShapeDtypeStruct((CONV_HALO, B_WIDTH), F32)],
        scratch_shapes=[pltpu.VMEM((tm + CONV_HALO, B_WIDTH), F32), pltpu.VMEM((tm + CONV_HALO, B_WIDTH), F32)],
        compiler_params=_params(("arbitrary",)),
    )(proj, proj, dhc, dhc, cw)


_CA_SCALE = CA_HEAD_DIM ** -0.5


def _softmax_rows(s):
    e = jnp.exp(s - jnp.max(s, axis=-1, keepdims=True))
    return e / jnp.sum(e, axis=-1, keepdims=True)


def _attn_fwd(q, k, v, *, name):
    S = q.shape[0]
    M = k.shape[0]
    tm = _tile(S, 512)

    def body(q_ref, k_ref, v_ref, o_ref):
        for h in range(CA_HEADS):
            cols = slice(h * CA_HEAD_DIM, (h + 1) * CA_HEAD_DIM)
            s = lax.dot_general(q_ref[:, cols], k_ref[:, cols], (((1,), (1,)), ((), ())),
                                preferred_element_type=F32) * _CA_SCALE
            p = _softmax_rows(s)
            o_ref[:, cols] = jnp.dot(p.astype(BF16), v_ref[:, cols], preferred_element_type=F32).astype(BF16)

    row = pl.BlockSpec((tm, D_MODEL), lambda i: (i, 0))
    kv = pl.BlockSpec((M, D_MODEL), lambda i: (0, 0))
    return pl.pallas_call(
        body, name=name, grid=(S // tm,), in_specs=[row, kv, kv], out_specs=row,
        out_shape=jax.ShapeDtypeStruct((S, D_MODEL), BF16), compiler_params=_params(("parallel",)),
    )(q, k, v)


def _attn_bwd(q, k, v, do, *, name):
    S = q.shape[0]
    M = k.shape[0]
    tm = _tile(S, 512)

    def body(q_ref, k_ref, v_ref, do_ref, dq_ref, dk_ref, dv_ref):
        @pl.when(pl.program_id(0) == 0)
        def _():
            dk_ref[...] = jnp.zeros_like(dk_ref)
            dv_ref[...] = jnp.zeros_like(dv_ref)

        for h in range(CA_HEADS):
            cols = slice(h * CA_HEAD_DIM, (h + 1) * CA_HEAD_DIM)
            qh = q_ref[:, cols]
            kh = k_ref[:, cols]
            vh = v_ref[:, cols]
            doh = do_ref[:, cols]
            s = lax.dot_general(qh, kh, (((1,), (1,)), ((), ())), preferred_element_type=F32) * _CA_SCALE
            p = _softmax_rows(s)
            pb = p.astype(BF16)
            dv_ref[:, cols] += lax.dot_general(pb, doh, (((0,), (0,)), ((), ())), preferred_element_type=F32)
            dp = lax.dot_general(doh, vh, (((1,), (1,)), ((), ())), preferred_element_type=F32)
            ds = (p * (dp - jnp.sum(dp * p, axis=-1, keepdims=True)) * _CA_SCALE).astype(BF16)
            dq_ref[:, cols] = jnp.dot(ds, kh, preferred_element_type=F32).astype(BF16)
            dk_ref[:, cols] += lax.dot_general(ds, qh, (((0,), (0,)), ((), ())), preferred_element_type=F32)

    row = pl.BlockSpec((tm, D_MODEL), lambda i: (i, 0))
    kv = pl.BlockSpec((M, D_MODEL), lambda i: (0, 0))
    return pl.pallas_call(
        body, name=name, grid=(S // tm,), in_specs=[row, kv, kv, row], out_specs=[row, kv, kv],
        out_shape=[jax.ShapeDtypeStruct((S, D_MODEL), BF16), jax.ShapeDtypeStruct((M, D_MODEL), F32),
                   jax.ShapeDtypeStruct((M, D_MODEL), F32)],
        compiler_params=_params(("arbitrary",)),
    )(q, k, v, do)


def _rowwise(body, ins, outs, *, name, tr=256):
    S = max(a.shape[0] for a in ins)
    t = _tile(S, tr, 8)

    def spec(rows, w):
        return pl.BlockSpec((t, w), lambda i: (i, 0)) if rows == S else pl.BlockSpec((rows, w), lambda i: (0, 0))

    return pl.pallas_call(
        body, name=name, grid=(S // t,),
        in_specs=[spec(a.shape[0], a.shape[1]) for a in ins],
        out_specs=[pl.BlockSpec((t, w), lambda i: (i, 0)) for w, _ in outs],
        out_shape=[jax.ShapeDtypeStruct((S, w), dt) for w, dt in outs],
        compiler_params=_params(("parallel",)),
    )(*ins)


def _swiglu_fwd(g, u, *, name):
    def body(g_ref, u_ref, h_ref):
        gv = g_ref[...]
        h_ref[...] = (gv * _sigmoid(gv) * u_ref[...]).astype(BF16)

    return _rowwise(body, [g, u], [(g.shape[1], BF16)], name=name)[0]


def _swiglu_bwd(g, u, dh, *, name):
    def body(g_ref, u_ref, dh_ref, dg_ref, du_ref):
        gv = g_ref[...]
        s = _sigmoid(gv)
        dhv = dh_ref[...]
        dg_ref[...] = (dhv * u_ref[...] * (s * (1.0 + gv * (1.0 - s)))).astype(BF16)
        du_ref[...] = (dhv * gv * s).astype(BF16)

    return _rowwise(body, [g, u, dh], [(g.shape[1], BF16), (g.shape[1], BF16)], name=name)


def _glu_fwd(o, x, *, name):
    def body(o_ref, x_ref, y_ref):
        y_ref[...] = x_ref[...] + o_ref[:, 0:D_MODEL] * _sigmoid(o_ref[:, D_MODEL:2 * D_MODEL])

    return _rowwise(body, [o, x], [(D_MODEL, F32)], name=name)[0]


def _glu_bwd(o, dy, *, name):
    def body(o_ref, dy_ref, do_ref):
        s = _sigmoid(o_ref[:, D_MODEL:2 * D_MODEL])
        dyv = dy_ref[...]
        do_ref[:, 0:D_MODEL] = (dyv * s).astype(BF16)
        do_ref[:, D_MODEL:2 * D_MODEL] = (dyv * o_ref[:, 0:D_MODEL] * s * (1.0 - s)).astype(BF16)

    return _rowwise(body, [o, dy], [(2 * D_MODEL, BF16)], name=name)[0]


def _s5_out_fwd(y0, u, d, *, name):
    def body(y0_ref, u_ref, d_ref, y_ref, yg_ref):
        y = y0_ref[...] + d_ref[...] * u_ref[...]
        y_ref[...] = y
        yg_ref[...] = _gelu(y)[0].astype(BF16)

    return _rowwise(body, [y0, u, d], [(C_WIDTH, F32), (C_WIDTH, BF16)], name=name)


def _s5_out_bwd(y, u, d, dyg, *, name):
    S = y.shape[0]
    tr = _tile(S, 256, 8)

    def body(y_ref, u_ref, d_ref, dyg_ref, dy_ref, dus_ref, dd_ref):
        yv = y_ref[...]
        _, t = _gelu(yv)
        dy = dyg_ref[...] * _gelu_grad(yv, t)
        dy_ref[...] = dy.astype(BF16)
        dus_ref[...] = d_ref[...] * dy

        @pl.when(pl.program_id(0) == 0)
        def _():
            dd_ref[...] = jnp.zeros_like(dd_ref)

        dd_ref[...] += jnp.sum(dy * u_ref[...], axis=0, keepdims=True)

    row = pl.BlockSpec((tr, C_WIDTH), lambda i: (i, 0))
    vec = pl.BlockSpec((1, C_WIDTH), lambda i: (0, 0))
    return pl.pallas_call(
        body, name=name, grid=(S // tr,), in_specs=[row, row, vec, row], out_specs=[row, row, vec],
        out_shape=[jax.ShapeDtypeStruct((S, C_WIDTH), BF16), jax.ShapeDtypeStruct((S, C_WIDTH), F32),
                   jax.ShapeDtypeStruct((1, C_WIDTH), F32)],
        compiler_params=_params(("arbitrary",)),
    )(y, u, d, dyg)


_SCAN_CHUNK = 128
_SCAN_UNROLL = 8
_RE = slice(0, STATE_ROWS)
_IM = slice(STATE_ROWS, 2 * STATE_ROWS)


def _scan_fwd(bu, a, *, name):
    S = bu.shape[0]
    tc = _tile(S, _SCAN_CHUNK, 8)

    def body(bu_ref, a_ref, xs_ref, st_ref):
        @pl.when(pl.program_id(0) == 0)
        def _():
            st_ref[...] = jnp.zeros_like(st_ref)

        ar = a_ref[_RE, :]
        ai = a_ref[_IM, :]

        def step(t, carry):
            xr, xi = carry
            nr = ar * xr - ai * xi + bu_ref[t, _RE, :]
            ni = ar * xi + ai * xr + bu_ref[t, _IM, :]
            xs_ref[t, _RE, :] = nr
            xs_ref[t, _IM, :] = ni
            return nr, ni

        xr, xi = lax.fori_loop(0, tc, step, (st_ref[_RE, :], st_ref[_IM, :]), unroll=_SCAN_UNROLL)
        st_ref[_RE, :] = xr
        st_ref[_IM, :] = xi

    blk = pl.BlockSpec((tc, 2 * STATE_ROWS, STATE_LANES), lambda i: (i, 0, 0))
    return pl.pallas_call(
        body, name=name, grid=(S // tc,),
        in_specs=[blk, pl.BlockSpec((2 * STATE_ROWS, STATE_LANES), lambda i: (0, 0))], out_specs=blk,
        out_shape=jax.ShapeDtypeStruct(bu.shape, F32),
        scratch_shapes=[pltpu.VMEM((2 * STATE_ROWS, STATE_LANES), F32)],
        compiler_params=_params(("arbitrary",)),
    )(bu, a)


def _scan_bwd(dxs, xs, a, *, name):
    S = dxs.shape[0]
    tc = _tile(S, _SCAN_CHUNK, 8)
    nc = S // tc

    def body(dx_ref, xs_ref, a_ref, g_ref, da_ref, st_ref):
        @pl.when(pl.program_id(0) == 0)
        def _():
            st_ref[...] = jnp.zeros_like(st_ref)
            da_ref[...] = jnp.zeros_like(da_ref)

        ar = a_ref[_RE, :]
        ai = a_ref[_IM, :]

        def step(j, carry):
            gr, gi, dar, dai = carry
            t = tc - 1 - j
            xr = xs_ref[t, _RE, :]
            xi = xs_ref[t, _IM, :]
            dar = dar + gr * xr + gi * xi
            dai = dai + gi * xr - gr * xi
            nr = dx_ref[t, _RE, :] + ar * gr + ai * gi
            ni = dx_ref[t, _IM, :] + ar * gi - ai * gr
            g_ref[t, _RE, :] = nr
            g_ref[t, _IM, :] = ni
            return nr, ni, dar, dai

        init = (st_ref[_RE, :], st_ref[_IM, :], da_ref[_RE, :], da_ref[_IM, :])
        gr, gi, dar, dai = lax.fori_loop(0, tc, step, init, unroll=_SCAN_UNROLL)
        st_ref[_RE, :] = gr
        st_ref[_IM, :] = gi
        da_ref[_RE, :] = dar
        da_ref[_IM, :] = dai

    blk = pl.BlockSpec((tc, 2 * STATE_ROWS, STATE_LANES), lambda i: (nc - 1 - i, 0, 0))
    vec = pl.BlockSpec((2 * STATE_ROWS, STATE_LANES), lambda i: (0, 0))
    return pl.pallas_call(
        body, name=name, grid=(nc,), in_specs=[blk, blk, vec], out_specs=[blk, vec],
        out_shape=[jax.ShapeDtypeStruct(dxs.shape, F32), jax.ShapeDtypeStruct((2 * STATE_ROWS, STATE_LANES), F32)],
        scratch_shapes=[pltpu.VMEM((2 * STATE_ROWS, STATE_LANES), F32)],
        compiler_params=_params(("arbitrary",)),
    )(dxs, xs, a)


def _loss_head(x, g, target, *, name):
    S, D = x.shape
    tr = _tile(S, 256, 8)

    def body(x_ref, g_ref, t_ref, dx_ref, dg_ref, loss_ref):
        @pl.when(pl.program_id(0) == 0)
        def _():
            dg_ref[...] = jnp.zeros_like(dg_ref)
            loss_ref[...] = jnp.zeros_like(loss_ref)

        xv = x_ref[...]
        gv = g_ref[...]
        r = lax.rsqrt(_mean(xv * xv) + EPS)
        xh = xv * r
        err = xh * gv - t_ref[...]
        loss_ref[...] += 0.5 * jnp.sum(_mean(err * err), axis=0, keepdims=True)
        dy = err * (1.0 / D)
        dyg = dy * gv
        dx_ref[...] = r * (dyg - xh * _mean(dyg * xh))
        dg_ref[...] += jnp.sum(dy * xh, axis=0, keepdims=True)

    row = pl.BlockSpec((tr, D), lambda i: (i, 0))
    vec = pl.BlockSpec((1, D), lambda i: (0, 0))
    return pl.pallas_call(
        body, name=name, grid=(S // tr,), in_specs=[row, vec, row],
        out_specs=[row, vec, pl.BlockSpec((1, 128), lambda i: (0, 0))],
        out_shape=[jax.ShapeDtypeStruct((S, D), F32), jax.ShapeDtypeStruct((1, D), F32),
                   jax.ShapeDtypeStruct((1, 128), F32)],
        compiler_params=_params(("arbitrary",)),
    )(x, g.reshape(1, D), target)


_ADAM_C1 = 1.0 - ADAM_B1 ** ADAM_STEP
_ADAM_C2 = 1.0 - ADAM_B2 ** ADAM_STEP


def _adamw(w, g, m, v, *, name):
    R, C = w.shape
    tr = _tile(R, 256, 8)

    def body(w_ref, g_ref, m_ref, v_ref, d_ref, nm_ref, nv_ref):
        gv = g_ref[...]
        nm = ADAM_B1 * m_ref[...] + (1.0 - ADAM_B1) * gv
        nv = ADAM_B2 * v_ref[...] + (1.0 - ADAM_B2) * (gv * gv)
        m_hat = nm / _ADAM_C1
        v_hat = nv / _ADAM_C2
        d_ref[...] = -ADAM_LR * (m_hat / (jnp.sqrt(v_hat) + ADAM_EPS) + ADAM_WD * w_ref[...])
        nm_ref[...] = nm
        nv_ref[...] = nv

    blk = pl.BlockSpec((tr, C), lambda i: (i, 0))
    out = jax.ShapeDtypeStruct((R, C), F32)
    return pl.pallas_call(
        body, name=name, grid=(R // tr,), in_specs=[blk] * 4, out_specs=[blk] * 3, out_shape=[out] * 3,
        compiler_params=_params(("parallel",)),
    )(w, g, m, v)


def _sum_slots(x, *, name):
    n, R, C = x.shape
    tr = _tile(R, 256, 8)

    def body(x_ref, o_ref):
        acc = x_ref[0]
        for k in range(1, n):
            acc = acc + x_ref[k]
        o_ref[...] = acc

    return pl.pallas_call(
        body, name=name, grid=(R // tr,),
        in_specs=[pl.BlockSpec((n, tr, C), lambda i: (0, i, 0))], out_specs=pl.BlockSpec((tr, C), lambda i: (i, 0)),
        out_shape=jax.ShapeDtypeStruct((R, C), F32), compiler_params=_params(("parallel",)),
    )(x)


def _pair_sum(g, r, half, *, name):
    n, R, C = g.shape
    Rh = R // 2
    tr = _tile(Rh, 232, 8)
    nb = Rh // tr

    def body(half_ref, g_ref, r_ref, o_ref):
        o_ref[...] = g_ref[...] + r_ref[...]

    return pl.pallas_call(
        body, name=name,
        grid_spec=pltpu.PrefetchScalarGridSpec(
            num_scalar_prefetch=1, grid=(n, nb),
            in_specs=[pl.BlockSpec((1, tr, C), lambda p, i, h: (p, h[0] * nb + i, 0)),
                      pl.BlockSpec((1, tr, C), lambda p, i, h: (p, i, 0))],
            out_specs=pl.BlockSpec((1, tr, C), lambda p, i, h: (p, i, 0)),
        ),
        out_shape=jax.ShapeDtypeStruct((n, Rh, C), F32), compiler_params=_params(("parallel", "parallel")),
    )(half, g, r)


ANY = pl.BlockSpec(memory_space=pl.ANY)


def _place():
    return lax.axis_index("x"), lax.axis_index("y"), lax.axis_index("c")


def _other_chips(x, y):
    return [(1 - x, y), (x, 1 - y), (1 - x, 1 - y)]


def _allgather_small(v, *, name):
    R, C = v.shape

    def body(x_ref, out_ref, send_sems, recv_sems, local_sem):
        x, y, c = _place()
        me, sibling = (x, y, c), (x, y, 1 - c)
        chips = _other_chips(x, y)

        def rows(px, py, pc):
            return out_ref.at[pl.ds((4 * px + 2 * py + pc) * R, R), :]

        def copy(k, block, to, src=None):
            return pltpu.make_async_remote_copy(
                src_ref=rows(*block) if src is None else src, dst_ref=rows(*block),
                send_sem=send_sems.at[k], recv_sem=recv_sems.at[k], device_id=to, device_id_type=MESH)

        mine = pltpu.make_async_copy(x_ref, rows(*me), local_sem)
        mine.start()
        first = [copy(0, me, sibling, src=x_ref)]
        first += [copy(1 + j, me, (*chip, c), src=x_ref) for j, chip in enumerate(chips)]
        for cp in first:
            cp.start()
        passed = [copy(4 + j, (*chip, c), sibling) for j, chip in enumerate(chips)]
        for j, chip in enumerate(chips):
            copy(1 + j, (*chip, c), me).wait_recv()
            passed[j].start()
        copy(0, sibling, me).wait_recv()
        for j, chip in enumerate(chips):
            copy(4 + j, (*chip, 1 - c), me).wait_recv()
        for cp in first + passed:
            cp.wait_send()
        mine.wait()

    return pl.pallas_call(
        body, name=name, out_shape=jax.ShapeDtypeStruct((N_DEV * R, C), v.dtype),
        in_specs=[pl.BlockSpec(memory_space=pltpu.VMEM)], out_specs=pl.BlockSpec(memory_space=pltpu.VMEM),
        scratch_shapes=[pltpu.SemaphoreType.DMA((7,)), pltpu.SemaphoreType.DMA((7,)), pltpu.SemaphoreType.DMA],
        compiler_params=pltpu.CompilerParams(vmem_limit_bytes=VMEM_LIMIT),
    )(v)


def _allgather_chips(slab, *, name):
    R, C = slab.shape
    Rh = R // 2

    def body(x_ref, out_ref, send_sems, recv_sems, local_sem):
        x, y, c = _place()
        chips = _other_chips(x, y)

        def half(px, py, hc):
            return out_ref.at[2 * px + py, pl.ds(hc * Rh, Rh), :]

        def copy(k, chip, hc, to, src=None):
            return pltpu.make_async_remote_copy(
                src_ref=half(*chip, hc) if src is None else src, dst_ref=half(*chip, hc),
                send_sem=send_sems.at[k], recv_sem=recv_sems.at[k], device_id=to, device_id_type=MESH)

        mine = pltpu.make_async_copy(x_ref, out_ref.at[2 * x + y], local_sem)
        mine.start()
        first = [copy(j, (x, y), c, (*chip, c), src=x_ref.at[pl.ds(c * Rh, Rh), :]) for j, chip in enumerate(chips)]
        for cp in first:
            cp.start()
        passed = [copy(3 + j, chip, c, (x, y, 1 - c)) for j, chip in enumerate(chips)]
        for j, chip in enumerate(chips):
            copy(j, chip, c, (x, y, c)).wait_recv()
            passed[j].start()
        for j, chip in enumerate(chips):
            copy(3 + j, chip, 1 - c, (x, y, c)).wait_recv()
        for cp in first + passed:
            cp.wait_send()
        mine.wait()

    return pl.pallas_call(
        body, name=name, out_shape=jax.ShapeDtypeStruct((N_CHIPS, R, C), slab.dtype),
        in_specs=[ANY], out_specs=ANY,
        scratch_shapes=[pltpu.SemaphoreType.DMA((6,)), pltpu.SemaphoreType.DMA((6,)), pltpu.SemaphoreType.DMA],
    )(slab)


def _pair_exchange(g, *, name):
    n, R, C = g.shape
    Rh = R // 2

    def body(g_ref, out_ref, send_sem, recv_sem):
        x, y, c = _place()
        cp = pltpu.make_async_remote_copy(
            src_ref=g_ref.at[:, pl.ds((1 - c) * Rh, Rh), :], dst_ref=out_ref,
            send_sem=send_sem, recv_sem=recv_sem, device_id=(x, y, 1 - c), device_id_type=MESH)
        cp.start()
        cp.wait()

    return pl.pallas_call(
        body, name=name, out_shape=jax.ShapeDtypeStruct((n, Rh, C), g.dtype), in_specs=[ANY], out_specs=ANY,
        scratch_shapes=[pltpu.SemaphoreType.DMA, pltpu.SemaphoreType.DMA],
    )(g)


def _chip_exchange(h, *, name):
    n, Rh, C = h.shape

    def body(h_ref, out_ref, send_sems, recv_sems, local_sem):
        x, y, c = _place()
        me = 2 * x + y
        chips = _other_chips(x, y)
        mine = pltpu.make_async_copy(h_ref.at[me], out_ref.at[me], local_sem)
        mine.start()
        sends = [pltpu.make_async_remote_copy(
            src_ref=h_ref.at[2 * cx + cy], dst_ref=out_ref.at[me], send_sem=send_sems.at[j], recv_sem=recv_sems.at[j],
            device_id=(cx, cy, c), device_id_type=MESH) for j, (cx, cy) in enumerate(chips)]
        for cp in sends:
            cp.start()
        for j, (cx, cy) in enumerate(chips):
            pltpu.make_async_remote_copy(
                src_ref=h_ref.at[me], dst_ref=out_ref.at[2 * cx + cy], send_sem=send_sems.at[j],
                recv_sem=recv_sems.at[j], device_id=(cx, cy, c), device_id_type=MESH).wait_recv()
        for cp in sends:
            cp.wait_send()
        mine.wait()

    return pl.pallas_call(
        body, name=name, out_shape=jax.ShapeDtypeStruct((n, Rh, C), h.dtype), in_specs=[ANY], out_specs=ANY,
        scratch_shapes=[pltpu.SemaphoreType.DMA((3,)), pltpu.SemaphoreType.DMA((3,)), pltpu.SemaphoreType.DMA],
    )(h)


def _pair_share(s, *, name):
    Rh, C = s.shape

    def body(s_ref, out_ref, send_sem, recv_sem, local_sem):
        x, y, c = _place()
        mine = pltpu.make_async_copy(s_ref, out_ref.at[pl.ds(c * Rh, Rh), :], local_sem)
        mine.start()
        cp = pltpu.make_async_remote_copy(
            src_ref=s_ref, dst_ref=out_ref.at[pl.ds(c * Rh, Rh), :], send_sem=send_sem, recv_sem=recv_sem,
            device_id=(x, y, 1 - c), device_id_type=MESH)
        cp.start()
        pltpu.make_async_remote_copy(
            src_ref=s_ref, dst_ref=out_ref.at[pl.ds((1 - c) * Rh, Rh), :], send_sem=send_sem, recv_sem=recv_sem,
            device_id=(x, y, 1 - c), device_id_type=MESH).wait_recv()
        cp.wait_send()
        mine.wait()

    return pl.pallas_call(
        body, name=name, out_shape=jax.ShapeDtypeStruct((2 * Rh, C), s.dtype), in_specs=[ANY], out_specs=ANY,
        scratch_shapes=[pltpu.SemaphoreType.DMA, pltpu.SemaphoreType.DMA, pltpu.SemaphoreType.DMA],
    )(s)


_BIG = (("e_w_in", 2), ("e_w_out", 1), ("o_w_in", 1), ("o_w_out", 2), ("ca_wq", 1), ("ca_wk", 1), ("ca_wv", 1),
        ("ca_wo", 1), ("ffn_w_gate", 2), ("ffn_w_up", 2), ("ffn_w_down", 1))
_SMALL_SHARDED = (("e_conv_w", 2), ("o_norm", 1), ("o_d", 1))
_REPLICATED = ("e_norm", "e_gmlp_w", "e_gmlp_b", "e_conv_b", "e_conv_ln_g", "e_conv_ln_b", "o_lam_re", "o_lam_im",
               "o_log_dt", "o_b_re", "o_b_im", "o_c_re", "o_c_im", "ca_norm", "ca_mem_norm", "ffn_norm", "final_norm")
_SMALL = tuple(n for n, _ in _SMALL_SHARDED) + _REPLICATED
_WEIGHTS = ("e_norm", "e_w_in", "e_gmlp_w", "e_gmlp_b", "e_conv_w", "e_conv_b", "e_conv_ln_g", "e_conv_ln_b",
            "e_w_out", "o_norm", "o_w_in", "o_lam_re", "o_lam_im", "o_log_dt", "o_b_re", "o_b_im", "o_c_re", "o_c_im",
            "o_d", "o_w_out", "ca_norm", "ca_mem_norm", "ca_wq", "ca_wk", "ca_wv", "ca_wo", "ffn_norm", "ffn_w_gate",
            "ffn_w_up", "ffn_w_down", "final_norm")


def _pack_rows(arrs, width, dtype, row_mult=8):
    parts, spans, r0 = [], [], 0
    for a in arrs:
        flat = a.reshape(-1).astype(dtype)
        rows = -(-flat.shape[0] // (width * row_mult)) * row_mult
        flat = jnp.pad(flat, (0, rows * width - flat.shape[0]))
        parts.append(flat.reshape(rows, width))
        spans.append((r0, rows))
        r0 += rows
    return jnp.concatenate(parts, axis=0), spans


def _unpack_rows(slab, spans, shapes):
    out = []
    for (r0, rows), shp in zip(spans, shapes):
        n = math.prod(shp)
        out.append(slab[r0:r0 + rows].reshape(-1)[:n].reshape(shp))
    return out


def _block_diag(b, pattern):
    return jnp.einsum(pattern, b, jnp.eye(C_GROUPS, dtype=b.dtype))


def _s5_discretize(lam_re, lam_im, log_dt, b_re, b_im):
    dt = jnp.exp(log_dt)[:, None]
    mag = jnp.exp(lam_re * dt)
    ar = mag * jnp.cos(lam_im * dt)
    ai = mag * jnp.sin(lam_im * dt)
    den = lam_re * lam_re + lam_im * lam_im
    qr = ((ar - 1.0) * lam_re + ai * lam_im) / den
    qi = (ai * lam_re - (ar - 1.0) * lam_im) / den
    bbr = qr[..., None] * b_re - qi[..., None] * b_im
    bbi = qr[..., None] * b_im + qi[..., None] * b_re
    return ar, ai, bbr, bbi


def _attention_block(x, mem, w, i, tag):
    xn = _rms_fwd(x, w["ca_norm"][i], name=f"{tag}_ca_norm")
    memn = _rms_fwd(mem, w["ca_mem_norm"][i], name=f"{tag}_ca_memnorm")
    q = _mm(xn, w["ca_wq"][i], out_dtype=BF16, name=f"{tag}_q")
    k = _mm(memn, w["ca_wk"][i], out_dtype=BF16, name=f"{tag}_k")
    v = _mm(memn, w["ca_wv"][i], out_dtype=BF16, name=f"{tag}_v")
    o = _attn_fwd(q, k, v, name=f"{tag}_attn")
    y = _mm(o, w["ca_wo"][i], add=x, name=f"{tag}_wo")
    return y, (x, xn, memn, q, k, v, o)


def _attention_block_bwd(dy, saved, mem, w, i, tag, grads):
    x, xn, memn, q, k, v, o = saved
    do = _mm(dy, w["ca_wo"][i], tb=True, out_dtype=BF16, name=f"{tag}_do")
    grads["ca_wo"][i] = _mm(o, dy, ta=True, name=f"{tag}_dwo")
    dq, dk, dv = _attn_bwd(q, k, v, do, name=f"{tag}_attn_bwd")
    dxn = _mm(dq, w["ca_wq"][i], tb=True, name=f"{tag}_dxn")
    grads["ca_wq"][i] = _mm(xn, dq, ta=True, name=f"{tag}_dwq")
    grads["ca_wk"][i] = _mm(memn, dk, ta=True, name=f"{tag}_dwk")
    grads["ca_wv"][i] = _mm(memn, dv, ta=True, name=f"{tag}_dwv")
    dmemn = _mm(dk, w["ca_wk"][i], tb=True, name=f"{tag}_dmemn_k")
    dmemn = _mm(dv, w["ca_wv"][i], tb=True, add=dmemn, name=f"{tag}_dmemn_v")
    dx, dg = _rms_bwd(x, w["ca_norm"][i], dxn, dy, name=f"{tag}_ca_norm_bwd")
    grads["ca_norm"][i] = dg[0]
    _, dgm = _rms_bwd(mem, w["ca_mem_norm"][i], dmemn, None, name=f"{tag}_ca_memnorm_bwd")
    grads["ca_mem_norm"][i] = dgm[0]
    return dx


def _ffn_block(x, w, i, tag):
    fn = _rms_fwd(x, w["ffn_norm"][i], name=f"{tag}_ffn_norm")
    g = _mm(fn, w["ffn_w_gate"][i], name=f"{tag}_gate")
    u = _mm(fn, w["ffn_w_up"][i], name=f"{tag}_up")
    h = _swiglu_fwd(g, u, name=f"{tag}_swiglu")
    y = _mm(h, w["ffn_w_down"][i], add=x, name=f"{tag}_down")
    return y, (x, fn, g, u, h)


def _ffn_block_bwd(dy, saved, w, i, tag, grads):
    x, fn, g, u, h = saved
    dh = _mm(dy, w["ffn_w_down"][i], tb=True, name=f"{tag}_dh")
    grads["ffn_w_down"][i] = _mm(h, dy, ta=True, name=f"{tag}_dwd")
    dg, du = _swiglu_bwd(g, u, dh, name=f"{tag}_swiglu_bwd")
    dfn = _mm(dg, w["ffn_w_gate"][i], tb=True, name=f"{tag}_dfn_g")
    dfn = _mm(du, w["ffn_w_up"][i], tb=True, add=dfn, name=f"{tag}_dfn_u")
    grads["ffn_w_gate"][i] = _mm(fn, dg, ta=True, name=f"{tag}_dwg")
    grads["ffn_w_up"][i] = _mm(fn, du, ta=True, name=f"{tag}_dwu")
    dx, dgn = _rms_bwd(x, w["ffn_norm"][i], dfn, dy, name=f"{tag}_ffn_norm_bwd")
    grads["ffn_norm"][i] = dgn[0]
    return dx


def _gmlp_mask():
    chunk = jnp.arange(GMLP_BLOCK) // CHUNK
    return chunk[None, :] <= chunk[:, None]


def _even_block(x, w, tag):
    hn = _rms_fwd(x, w["e_norm"][0], name=f"{tag}_norm")
    proj = _mm(hn, w["e_w_in"][0], name=f"{tag}_w_in")
    wm = jnp.where(_gmlp_mask()[None], w["e_gmlp_w"][0], 0.0).astype(BF16)
    bcol = w["e_gmlp_b"][0][:, :, None]
    cw = jnp.pad(w["e_conv_w"][0], ((0, CONV_HALO - CONV_WIDTH), (0, 0)))
    cb, lg, lb = w["e_conv_b"], w["e_conv_ln_g"], w["e_conv_ln_b"]
    mix, hc = _even_fwd(proj, wm, bcol, cw, cb, lg, lb, name=f"{tag}_mixers")
    y = _mm(mix, w["e_w_out"][0], add=x, name=f"{tag}_w_out")
    return y, (x, hn, proj, mix, hc, wm, bcol, cw)


def _even_block_bwd(dy, saved, w, tag, grads):
    x, hn, proj, mix, hc, wm, bcol, cw = saved
    dmix = _mm(dy, w["e_w_out"][0], tb=True, name=f"{tag}_dmix")
    grads["e_w_out"] = _mm(mix, dy, ta=True, name=f"{tag}_dw_out")[None]
    wmt = jnp.swapaxes(wm, 1, 2)
    dpa, dhc, dwm, db, dlg, dlb, dcb = _even_bwd1(proj, dmix, hc, wm, wmt, bcol, w["e_conv_ln_g"], w["e_conv_ln_b"],
                                                  name=f"{tag}_mixers_bwd1")
    dpb, dcw = _even_bwd2(proj, dhc, cw, name=f"{tag}_mixers_bwd2")
    grads["e_gmlp_w"] = jnp.where(_gmlp_mask()[None], dwm, 0.0)[None]
    grads["e_gmlp_b"] = db[:, :, 0][None]
    grads["e_conv_ln_g"], grads["e_conv_ln_b"], grads["e_conv_b"] = dlg, dlb, dcb
    grads["e_conv_w"] = dcw[:CONV_WIDTH][None]
    dproj = jnp.concatenate([dpa, dpb], axis=1)
    dhn = _mm(dproj, w["e_w_in"][0], tb=True, name=f"{tag}_dhn")
    grads["e_w_in"] = _mm(hn, dproj, ta=True, name=f"{tag}_dw_in")[None]
    dx, dg = _rms_bwd(x, w["e_norm"][0], dhn, dy, name=f"{tag}_norm_bwd")
    grads["e_norm"] = dg
    return dx


def _odd_block(x, w, tag):
    S = x.shape[0]
    hn = _rms_fwd(x, w["o_norm"][0], name=f"{tag}_norm")
    u = _mm(hn, w["o_w_in"][0], name=f"{tag}_w_in")
    disc_in = (w["o_lam_re"][0], w["o_lam_im"][0], w["o_log_dt"][0], w["o_b_re"][0], w["o_b_im"][0])
    (ar, ai, bbr, bbi), disc_vjp = jax.vjp(_s5_discretize, *disc_in)
    bd = jnp.concatenate([_block_diag(bbr, "gpc,gh->gchp").reshape(C_WIDTH, N_STATE),
                          _block_diag(bbi, "gpc,gh->gchp").reshape(C_WIDTH, N_STATE)], axis=1).astype(BF16)
    cd = jnp.concatenate([_block_diag(w["o_c_re"][0], "gcp,gh->gphc").reshape(N_STATE, C_WIDTH),
                          -_block_diag(w["o_c_im"][0], "gcp,gh->gphc").reshape(N_STATE, C_WIDTH)], axis=0).astype(BF16)
    a = jnp.concatenate([ar.reshape(STATE_ROWS, STATE_LANES), ai.reshape(STATE_ROWS, STATE_LANES)], axis=0)
    bu = _mm(u, bd, name=f"{tag}_bu")
    xs = _scan_fwd(bu.reshape(S, 2 * STATE_ROWS, STATE_LANES), a, name=f"{tag}_scan")
    xs2 = xs.reshape(S, 2 * N_STATE)
    y0 = _mm(xs2, cd, name=f"{tag}_readout")
    d = w["o_d"]
    yv, yg = _s5_out_fwd(y0, u, d, name=f"{tag}_skip_gelu")
    o = _mm(yg, w["o_w_out"][0], name=f"{tag}_w_out")
    y = _glu_fwd(o, x, name=f"{tag}_glu")
    return y, (x, hn, u, bd, cd, a, xs, yv, yg, o, disc_vjp)


def _odd_block_bwd(dy, saved, w, tag, grads):
    x, hn, u, bd, cd, a, xs, yv, yg, o, disc_vjp = saved
    S = x.shape[0]
    do = _glu_bwd(o, dy, name=f"{tag}_glu_bwd")
    dyg = _mm(do, w["o_w_out"][0], tb=True, name=f"{tag}_dyg")
    grads["o_w_out"] = _mm(yg, do, ta=True, name=f"{tag}_dw_out")[None]
    dyv, dus, dd = _s5_out_bwd(yv, u, w["o_d"], dyg, name=f"{tag}_skip_gelu_bwd")
    grads["o_d"] = dd
    xs2 = xs.reshape(S, 2 * N_STATE)
    dxs = _mm(dyv, cd, tb=True, name=f"{tag}_dxs")
    dcd = _mm(xs2, dyv, ta=True, name=f"{tag}_dcd")
    gs, da = _scan_bwd(dxs.reshape(S, 2 * STATE_ROWS, STATE_LANES), xs, a, name=f"{tag}_scan_bwd")
    gs2 = gs.reshape(S, 2 * N_STATE)
    du = _mm(gs2, bd, tb=True, add=dus, name=f"{tag}_du")
    dbd = _mm(u, gs2, ta=True, name=f"{tag}_dbd")
    eye = jnp.eye(C_GROUPS, dtype=F32)
    dcr = jnp.einsum("gphc,gh->gcp", dcd[:N_STATE].reshape(C_GROUPS, C_STATE, C_GROUPS, C_GROUP_CH), eye)
    dci = -jnp.einsum("gphc,gh->gcp", dcd[N_STATE:].reshape(C_GROUPS, C_STATE, C_GROUPS, C_GROUP_CH), eye)
    dbbr = jnp.einsum("gchp,gh->gpc", dbd[:, :N_STATE].reshape(C_GROUPS, C_GROUP_CH, C_GROUPS, C_STATE), eye)
    dbbi = jnp.einsum("gchp,gh->gpc", dbd[:, N_STATE:].reshape(C_GROUPS, C_GROUP_CH, C_GROUPS, C_STATE), eye)
    dar = da[:STATE_ROWS].reshape(C_GROUPS, C_STATE)
    dai = da[STATE_ROWS:].reshape(C_GROUPS, C_STATE)
    dlr, dli, dldt, dbr, dbi = disc_vjp((dar, dai, dbbr, dbbi))
    grads["o_lam_re"], grads["o_lam_im"], grads["o_log_dt"] = dlr[None], dli[None], dldt[None]
    grads["o_b_re"], grads["o_b_im"], grads["o_c_re"], grads["o_c_im"] = dbr[None], dbi[None], dcr[None], dci[None]
    dhn = _mm(du, w["o_w_in"][0], tb=True, name=f"{tag}_dhn")
    grads["o_w_in"] = _mm(hn, du, ta=True, name=f"{tag}_dw_in")[None]
    dx, dg = _rms_bwd(x, w["o_norm"][0], dhn, dy, name=f"{tag}_norm_bwd")
    grads["o_norm"] = dg
    return dx


def _forward_backward(xs_, mems_, tgt, w):
    x1, s_mix0 = _even_block(xs_, w, "l0")
    x2, s_att0 = _attention_block(x1, mems_, w, 0, "l0")
    x3, s_ffn0 = _ffn_block(x2, w, 0, "l0")
    x4, s_mix1 = _odd_block(x3, w, "l1")
    x5, s_att1 = _attention_block(x4, mems_, w, 1, "l1")
    x6, s_ffn1 = _ffn_block(x5, w, 1, "l1")
    dx, dfinal, loss_lanes = _loss_head(x6, w["final_norm"], tgt, name="loss_head")

    grads = {n: [None, None] for n in ("ca_wq", "ca_wk", "ca_wv", "ca_wo", "ca_norm", "ca_mem_norm", "ffn_norm",
                                       "ffn_w_gate", "ffn_w_up", "ffn_w_down")}
    grads["final_norm"] = dfinal[0]
    dx = _ffn_block_bwd(dx, s_ffn1, w, 1, "l1", grads)
    dx = _attention_block_bwd(dx, s_att1, mems_, w, 1, "l1", grads)
    dx = _odd_block_bwd(dx, s_mix1, w, "l1", grads)
    dx = _ffn_block_bwd(dx, s_ffn0, w, 0, "l0", grads)
    dx = _attention_block_bwd(dx, s_att0, mems_, w, 0, "l0", grads)
    dx = _even_block_bwd(dx, s_mix0, w, "l0", grads)
    for n in list(grads):
        if isinstance(grads[n], list):
            grads[n] = jnp.stack(grads[n], axis=0)
        grads[n] = grads[n].reshape(w[n].shape)
    return loss_lanes, dx, grads


def kernel(x, mem, e_norm, e_w_in, e_gmlp_w, e_gmlp_b, e_conv_w, e_conv_b, e_conv_ln_g, e_conv_ln_b, e_w_out, o_norm, o_w_in, o_lam_re, o_lam_im, o_log_dt, o_b_re, o_b_im, o_c_re, o_c_im, o_d, o_w_out, ca_norm, ca_mem_norm, ca_wq, ca_wk, ca_wv, ca_wo, ffn_norm, ffn_w_gate, ffn_w_up, ffn_w_down, final_norm, loss_target, m_e_norm, m_e_w_in, m_e_gmlp_w, m_e_gmlp_b, m_e_conv_w, m_e_conv_b, m_e_conv_ln_g, m_e_conv_ln_b, m_e_w_out, m_o_norm, m_o_w_in, m_o_lam_re, m_o_lam_im, m_o_log_dt, m_o_b_re, m_o_b_im, m_o_c_re, m_o_c_im, m_o_d, m_o_w_out, m_ca_norm, m_ca_mem_norm, m_ca_wq, m_ca_wk, m_ca_wv, m_ca_wo, m_ffn_norm, m_ffn_w_gate, m_ffn_w_up, m_ffn_w_down, m_final_norm, v_e_norm, v_e_w_in, v_e_gmlp_w, v_e_gmlp_b, v_e_conv_w, v_e_conv_b, v_e_conv_ln_g, v_e_conv_ln_b, v_e_w_out, v_o_norm, v_o_w_in, v_o_lam_re, v_o_lam_im, v_o_log_dt, v_o_b_re, v_o_b_im, v_o_c_re, v_o_c_im, v_o_d, v_o_w_out, v_ca_norm, v_ca_mem_norm, v_ca_wq, v_ca_wk, v_ca_wv, v_ca_wo, v_ffn_norm, v_ffn_w_gate, v_ffn_w_up, v_ffn_w_down, v_final_norm):
    args = dict(locals())
    local = {n: args[n] for n in _WEIGHTS}
    mom = {n: args["m_" + n] for n in _WEIGHTS}
    vel = {n: args["v_" + n] for n in _WEIGHTS}
    chip = 2 * lax.axis_index("x") + lax.axis_index("y")
    core = lax.axis_index("c")
    xs_, mems_, tgt = x[0], mem[0], loss_target[0]

    big_slab, big_spans = _pack_rows([local[n] for n, _ in _BIG], SLAB_W, BF16, row_mult=32)
    gathered = _allgather_chips(big_slab, name="gather_weights")
    w = {n: local[n] for n in _REPLICATED}
    for (n, ax), (r0, rows) in zip(_BIG, big_spans):
        shp = local[n].shape
        w[n] = jnp.concatenate([gathered[p, r0:r0 + rows].reshape(shp) for p in range(N_CHIPS)], axis=ax)
    sm_slab, sm_spans = _pack_rows([local[n] for n, _ in _SMALL_SHARDED], SMALL_W, F32)
    sm_all = _allgather_small(sm_slab, name="gather_small_weights").reshape(N_DEV, -1, SMALL_W)
    for (n, ax), span in zip(_SMALL_SHARDED, sm_spans):
        shp = local[n].shape
        w[n] = jnp.concatenate([_unpack_rows(sm_all[2 * p], [span], [shp])[0] for p in range(N_CHIPS)], axis=ax)

    loss_lanes, dx, grads = _forward_backward(xs_, mems_, tgt, w)

    parts = []
    for p in range(N_CHIPS):
        rows_p = []
        for n, ax in _BIG:
            width = local[n].shape[ax]
            rows_p.append(lax.slice_in_dim(grads[n], p * width, (p + 1) * width, axis=ax))
        parts.append(_pack_rows(rows_p, SLAB_W, F32, row_mult=32)[0])
    gslab = jnp.stack(parts, axis=0)
    other_half = _pair_exchange(gslab, name="grad_pair_exchange")
    pair = _pair_sum(gslab, other_half, core.reshape(1).astype(jnp.int32), name="grad_pair_sum")
    slots = _chip_exchange(pair, name="grad_chip_exchange")
    half_sum = _sum_slots(slots, name="grad_chip_sum")
    gsum = _pair_share(half_sum, name="grad_pair_share")
    big_grads = dict(zip([n for n, _ in _BIG], _unpack_rows(gsum, big_spans, [local[n].shape for n, _ in _BIG])))

    gs_slab, gs_spans = _pack_rows([grads[n] for n in _SMALL], SMALL_W, F32)
    gs_all = _allgather_small(gs_slab, name="gather_small_grads").reshape(N_DEV, -1, SMALL_W)
    gs_sum = _sum_slots(gs_all, name="small_grad_sum")
    small_full = dict(zip(_SMALL, _unpack_rows(gs_sum, gs_spans, [grads[n].shape for n in _SMALL])))
    small_grads = {}
    for n in _SMALL:
        g = small_full[n]
        for sn, ax in _SMALL_SHARDED:
            if sn == n:
                width = local[n].shape[ax]
                g = lax.dynamic_slice_in_dim(g, chip * width, width, axis=ax)
        small_grads[n] = g

    delta, new_m, new_v = {}, {}, {}
    for n, _ in _BIG:
        shp = local[n].shape
        two_d = (-1, shp[-1])
        d_, m_, v_ = _adamw(local[n].reshape(two_d), big_grads[n].reshape(two_d), mom[n].reshape(two_d),
                            vel[n].reshape(two_d), name=f"adamw_{n}")
        delta[n], new_m[n], new_v[n] = d_.reshape(shp), m_.reshape(shp), v_.reshape(shp)
    ws, spans = _pack_rows([local[n] for n in _SMALL], SMALL_W, F32)
    gsl = _pack_rows([small_grads[n] for n in _SMALL], SMALL_W, F32)[0]
    msl = _pack_rows([mom[n] for n in _SMALL], SMALL_W, F32)[0]
    vsl = _pack_rows([vel[n] for n in _SMALL], SMALL_W, F32)[0]
    d_, m_, v_ = _adamw(ws, gsl, msl, vsl, name="adamw_small")
    shapes = [local[n].shape for n in _SMALL]
    for n, dd, mm_, vv in zip(_SMALL, _unpack_rows(d_, spans, shapes), _unpack_rows(m_, spans, shapes),
                              _unpack_rows(v_, spans, shapes)):
        delta[n], new_m[n], new_v[n] = dd, mm_, vv

    out_grads = {**big_grads, **small_grads}
    loss = lax.psum(loss_lanes[0, 0], ("x", "y", "c"))
    return (loss, dx[None], *[out_grads[n] for n in _WEIGHTS], *[delta[n] for n in _WEIGHTS],
            *[new_m[n] for n in _WEIGHTS], *[new_v[n] for n in _WEIGHTS])
```

```python
import functools
import math

import jax
import jax.numpy as jnp
from jax import lax
from jax.experimental import pallas as pl
from jax.experimental.pallas import tpu as pltpu

F32 = jnp.float32
BF16 = jnp.bfloat16
MESH = pl.DeviceIdType.MESH

EPS = 1e-6
D_MODEL = 1024
A_WIDTH = 512
A_GROUPS = 4
GMLP_BLOCK = 128
CHUNK = 64
B_WIDTH = 512
CONV_WIDTH = 31
CONV_HALO = 32
C_WIDTH = 512
C_GROUP_CH = 16
C_GROUPS = 32
C_STATE = 64
N_STATE = C_GROUPS * C_STATE
STATE_ROWS = 8
STATE_LANES = N_STATE // STATE_ROWS
CA_HEADS = 4
CA_HEAD_DIM = 256
FFN_HIDDEN = 2816

ADAM_LR = 0.001
ADAM_B1 = 0.9
ADAM_B2 = 0.999
ADAM_EPS = 1e-08
ADAM_WD = 0.01
ADAM_STEP = 10

VMEM_LIMIT = 56 * 1024 * 1024
ACC_BYTES = 6 * 1024 * 1024
SLAB_W = 1024
SMALL_W = 128
N_CHIPS = 4
N_DEV = 8


def _params(sem=None):
    return pltpu.CompilerParams(dimension_semantics=sem, vmem_limit_bytes=VMEM_LIMIT)


def _tile(n, pref, mult=128):
    if n <= pref:
        return n
    t = (pref // mult) * mult
    while t >= mult:
        if n % t == 0:
            return t
        t -= mult
    return n


_GELU_C = 0.7978845608028654
_GELU_A = 0.044715


def _gelu(x):
    t = jnp.tanh(_GELU_C * (x + _GELU_A * (x * x * x)))
    return 0.5 * x * (1.0 + t), t


def _gelu_grad(x, t):
    return 0.5 * (1.0 + t) + 0.5 * x * (1.0 - t * t) * (_GELU_C * (1.0 + 3.0 * _GELU_A * x * x))


def _sigmoid(x):
    return 1.0 / (1.0 + jnp.exp(-x))


def _mean(x):
    return jnp.mean(x, axis=-1, keepdims=True)


def _dot(a, b):
    return jnp.dot(a, b, preferred_element_type=F32)


def _dot_nt(a, b):
    return lax.dot_general(a, b, (((1,), (1,)), ((), ())), preferred_element_type=F32)


def _dot_tn(a, b):
    return lax.dot_general(a, b, (((0,), (0,)), ((), ())), preferred_element_type=F32)


def _rms_tile(xv, gv):
    return (xv * lax.rsqrt(_mean(xv * xv) + EPS)) * gv


def _rms_bwd_tile(xv, gv, dyv):
    r = lax.rsqrt(_mean(xv * xv) + EPS)
    xh = xv * r
    dyg = dyv * gv
    return r * (dyg - xh * _mean(dyg * xh)), jnp.sum(dyv * xh, axis=0, keepdims=True)


def _rows_call(name, tm, rows, fulls, outs, accs, body, scratch=()):
    S = rows[0].shape[0]
    nr, nf, no, na = len(rows), len(fulls), len(outs), len(accs)

    def kern(*refs):
        r, f = refs[:nr], refs[nr:nr + nf]
        o, a = refs[nr + nf:nr + nf + no], refs[nr + nf + no:nr + nf + no + na]
        if na:
            @pl.when(pl.program_id(0) == 0)
            def _():
                for ref in a:
                    ref[...] = jnp.zeros_like(ref)
        body(r, f, o, a, refs[nr + nf + no + na:])

    def whole(shape):
        nd = len(shape)
        return pl.BlockSpec(tuple(shape), lambda i: (0,) * nd)

    res = pl.pallas_call(
        kern, name=name, grid=(S // tm,),
        in_specs=[pl.BlockSpec((tm, x.shape[1]), lambda i: (i, 0)) for x in rows] + [whole(x.shape) for x in fulls],
        out_specs=[pl.BlockSpec((tm, w), lambda i: (i, 0)) for w, _ in outs] + [whole(shp) for shp, _ in accs],
        out_shape=[jax.ShapeDtypeStruct((S, w), dt) for w, dt in outs]
        + [jax.ShapeDtypeStruct(tuple(shp), dt) for shp, dt in accs],
        scratch_shapes=list(scratch),
        compiler_params=_params(("arbitrary",) if na else ("parallel",)),
    )(*rows, *fulls)
    return res[:no], res[no:]


def _mm(a, b, *, ta=False, tb=False, add=None, out_dtype=F32, name):
    if ta:
        K, M = a.shape
    else:
        M, K = a.shape
    if tb:
        N, Kb = b.shape
    else:
        Kb, N = b.shape
    assert K == Kb, (name, a.shape, b.shape)
    tm, tn, tk = _tile(M, 512), _tile(N, 512), _tile(K, 512)
    nk = K // tk
    dims = (((0 if ta else 1,), (1 if tb else 0,)), ((), ()))
    has_add = add is not None

    def body(*refs):
        if has_add:
            a_ref, b_ref, add_ref, o_ref, acc_ref = refs
        else:
            a_ref, b_ref, o_ref, acc_ref = refs
        k = pl.program_id(2)

        @pl.when(k == 0)
        def _():
            acc_ref[...] = jnp.zeros_like(acc_ref)

        acc_ref[...] += lax.dot_general(a_ref[...].astype(BF16), b_ref[...].astype(BF16), dims,
                                        preferred_element_type=F32)

        @pl.when(k == nk - 1)
        def _():
            r = acc_ref[...]
            if has_add:
                r = r + add_ref[...]
            o_ref[...] = r.astype(o_ref.dtype)

    a_spec = pl.BlockSpec((tk, tm), lambda i, j, k: (k, i)) if ta else pl.BlockSpec((tm, tk), lambda i, j, k: (i, k))
    b_spec = pl.BlockSpec((tn, tk), lambda i, j, k: (j, k)) if tb else pl.BlockSpec((tk, tn), lambda i, j, k: (k, j))
    in_specs = [a_spec, b_spec]
    args = [a, b]
    if has_add:
        in_specs.append(pl.BlockSpec((tm, tn), lambda i, j, k: (i, j)))
        args.append(add)
    return pl.pallas_call(
        body, name=name, grid=(M // tm, N // tn, nk),
        in_specs=in_specs, out_specs=pl.BlockSpec((tm, tn), lambda i, j, k: (i, j)),
        out_shape=jax.ShapeDtypeStruct((M, N), out_dtype),
        scratch_shapes=[pltpu.VMEM((tm, tn), F32)],
        compiler_params=_params(("parallel", "parallel", "arbitrary")),
    )(*args)


def _mm_rows(a, w, *, nt=False, add=None, out_dtype=F32, tm=512, name):
    S = a.shape[0]
    N = w.shape[0] if nt else w.shape[1]
    tm = _tile(S, tm)
    has_add = add is not None

    def body(r, f, o, acc, s):
        av = r[0][...].astype(BF16)
        y = _dot_nt(av, f[0][...]) if nt else _dot(av, f[0][...])
        if has_add:
            y = y + r[1][...]
        o[0][...] = y.astype(out_dtype)

    (y,), _ = _rows_call(name, tm, [a] + ([add] if has_add else []), [w], [(N, out_dtype)], [], body)
    return y


def _mm_tn(a, b, *, name):
    S, K1 = a.shape
    N = b.shape[1]
    tn = _tile(N, max(128, (ACC_BYTES // (4 * K1)) // 128 * 128))
    ts = _tile(S, 512 if K1 * a.dtype.itemsize * 512 <= 4 * 1024 * 1024 else 256)

    def body(a_ref, b_ref, o_ref):
        @pl.when(pl.program_id(1) == 0)
        def _():
            o_ref[...] = jnp.zeros_like(o_ref)

        o_ref[...] += _dot_tn(a_ref[...].astype(BF16), b_ref[...].astype(BF16))

    return pl.pallas_call(
        body, name=name, grid=(N // tn, S // ts),
        in_specs=[pl.BlockSpec((ts, K1), lambda j, s: (s, 0)), pl.BlockSpec((ts, tn), lambda j, s: (s, j))],
        out_specs=pl.BlockSpec((K1, tn), lambda j, s: (0, j)),
        out_shape=jax.ShapeDtypeStruct((K1, N), F32),
        compiler_params=_params(("parallel", "arbitrary")),
    )(a, b)


def _vec(g):
    return g.reshape(1, -1)


def _norm_mm(x, g, w, *, out_dtype, name, tm=512):
    S, D = x.shape
    N = w.shape[1]

    def body(r, f, o, acc, s):
        xn = _rms_tile(r[0][...], f[0][...]).astype(BF16)
        o[0][...] = xn
        o[1][...] = _dot(xn, f[1][...]).astype(out_dtype)

    (xn, y), _ = _rows_call(name, _tile(S, tm), [x], [_vec(g), w], [(D, BF16), (N, out_dtype)], [], body)
    return xn, y


def _norm_bwd_mm(terms, weights, x, g, dres, *, name, tm=256):
    S, D = x.shape
    nt = len(terms)

    def body(r, f, o, acc, s):
        tot = None
        for t, (_, k, cols) in enumerate(terms):
            wv = f[1 + k][...] if cols is None else f[1 + k][:, cols]
            y = _dot_nt(r[t][...].astype(BF16), wv)
            tot = y if tot is None else tot + y
        dx, dg = _rms_bwd_tile(r[nt][...], f[0][...], tot)
        o[0][...] = dx + r[nt + 1][...]
        acc[0][...] += dg

    (dx,), (dg,) = _rows_call(name, _tile(S, tm), [a for a, _, _ in terms] + [x, dres], [_vec(g)] + list(weights),
                              [(D, F32)], [((1, D), F32)], body)
    return dx, dg


def _rms_fwd(x, g, *, name):
    def body(r, f, o, acc, s):
        o[0][...] = _rms_tile(r[0][...], f[0][...]).astype(BF16)

    (y,), _ = _rows_call(name, _tile(x.shape[0], 256, 8), [x], [_vec(g)], [(x.shape[1], BF16)], [], body)
    return y


def _rms_dg(x, g, dy, *, name):
    def body(r, f, o, acc, s):
        acc[0][...] += _rms_bwd_tile(r[0][...], f[0][...], r[1][...])[1]

    _, (dg,) = _rows_call(name, _tile(x.shape[0], 256, 8), [x, dy], [_vec(g)], [], [((1, x.shape[1]), F32)], body)
    return dg


def _ffn_up(x, g, wg, wu, *, name, tm=256):
    S, D = x.shape
    H = wg.shape[1]

    def body(r, f, o, acc, s):
        xn = _rms_tile(r[0][...], f[0][...]).astype(BF16)
        o[0][...] = xn
        gate = _dot(xn, f[1][...])
        up = _dot(xn, f[2][...])
        o[1][...] = gate.astype(BF16)
        o[2][...] = up.astype(BF16)
        o[3][...] = (gate * _sigmoid(gate) * up).astype(BF16)

    (xn, gate, up, h), _ = _rows_call(name, _tile(S, tm), [x], [_vec(g), wg, wu],
                                      [(D, BF16), (H, BF16), (H, BF16), (H, BF16)], [], body)
    return xn, gate, up, h


def _ffn_bwd_hidden(dy, wd, gate, up, *, name, tm=256):
    S = dy.shape[0]
    H = wd.shape[0]

    def body(r, f, o, acc, s):
        dh = _dot_nt(r[0][...].astype(BF16), f[0][...])
        gv = r[1][...].astype(F32)
        sg = _sigmoid(gv)
        o[0][...] = (dh * r[2][...].astype(F32) * (sg * (1.0 + gv * (1.0 - sg)))).astype(BF16)
        o[1][...] = (dh * gv * sg).astype(BF16)

    (dg, du), _ = _rows_call(name, _tile(S, tm), [dy, gate, up], [wd], [(H, BF16), (H, BF16)], [], body)
    return dg, du


def _ln_stats(v):
    mu = _mean(v)
    xc = v - mu
    rstd = lax.rsqrt(_mean(xc * xc) + EPS)
    return xc * rstd, rstd


def _even_fwd(proj, wm, bcol, cw, cb, lg, lb, *, name):
    S = proj.shape[0]
    tm = _tile(S, 256)
    hb = tm // CONV_HALO
    nblk = tm // GMLP_BLOCK

    def body(p_ref, halo_ref, wm_ref, b_ref, cw_ref, cb_ref, lg_ref, lb_ref, mix_ref, hc_ref, hext_ref):
        i = pl.program_id(0)
        gu, _ = _gelu(p_ref[:, 0:A_WIDTH])
        gv, _ = _gelu(p_ref[:, A_WIDTH:2 * A_WIDTH])
        vn, _ = _ln_stats(gv)
        vnb = vn.astype(BF16)
        for n in range(nblk):
            rows = slice(n * GMLP_BLOCK, (n + 1) * GMLP_BLOCK)
            for g in range(A_GROUPS):
                cols = slice(g * GMLP_BLOCK, (g + 1) * GMLP_BLOCK)
                sg = jnp.dot(wm_ref[g], vnb[rows, cols], preferred_element_type=F32) + b_ref[g]
                mix_ref[rows, cols] = (gu[rows, cols] * sg).astype(BF16)
        h = p_ref[:, 1024:1536] * _sigmoid(p_ref[:, 1536:2048])
        hh = halo_ref[:, 0:B_WIDTH] * _sigmoid(halo_ref[:, B_WIDTH:2 * B_WIDTH])
        hext_ref[0:CONV_HALO, :] = jnp.where(i > 0, hh, 0.0)
        hext_ref[CONV_HALO:CONV_HALO + tm, :] = h
        acc = jnp.zeros((tm, B_WIDTH), F32)
        for k in range(CONV_WIDTH):
            acc = acc + cw_ref[k:k + 1, :] * hext_ref[pl.ds(k + CONV_HALO - CONV_WIDTH + 1, tm), :]
        hc = acc + cb_ref[...]
        hc_ref[...] = hc
        hhat, _ = _ln_stats(hc)
        hl = hhat * lg_ref[...] + lb_ref[...]
        mix_ref[:, A_WIDTH:A_WIDTH + B_WIDTH] = (hl * _sigmoid(hl)).astype(BF16)

    vec = pl.BlockSpec((1, B_WIDTH), lambda i: (0, 0))
    return pl.pallas_call(
        body, name=name, grid=(S // tm,),
        in_specs=[
            pl.BlockSpec((tm, 2048), lambda i: (i, 0)),
            pl.BlockSpec((CONV_HALO, 1024), lambda i: (jnp.maximum(i * hb - 1, 0), 1)),
            pl.BlockSpec((A_GROUPS, GMLP_BLOCK, GMLP_BLOCK), lambda i: (0, 0, 0)),
            pl.BlockSpec((A_GROUPS, GMLP_BLOCK, 1), lambda i: (0, 0, 0)),
            pl.BlockSpec((CONV_HALO, B_WIDTH), lambda i: (0, 0)),
            vec, vec, vec,
        ],
        out_specs=[pl.BlockSpec((tm, 1024), lambda i: (i, 0)), pl.BlockSpec((tm, B_WIDTH), lambda i: (i, 0))],
        out_shape=[jax.ShapeDtypeStruct((S, 1024), BF16), jax.ShapeDtypeStruct((S, B_WIDTH), F32)],
        scratch_shapes=[pltpu.VMEM((tm + CONV_HALO, B_WIDTH), F32)],
        compiler_params=_params(("parallel",)),
    )(proj, proj, wm, bcol, cw, cb, lg, lb)


def _even_bwd1(proj, dmix, hc, wm, wmt, bcol, lg, lb, *, name):
    S = proj.shape[0]
    tm = _tile(S, 256)
    nblk = tm // GMLP_BLOCK

    def body(p_ref, dm_ref, hc_ref, wm_ref, wmt_ref, b_ref, lg_ref, lb_ref,
             dpa_ref, dhc_ref, dwm_ref, db_ref, dlg_ref, dlb_ref, dcb_ref, dgu_ref, dvn_ref):
        @pl.when(pl.program_id(0) == 0)
        def _():
            dwm_ref[...] = jnp.zeros_like(dwm_ref)
            db_ref[...] = jnp.zeros_like(db_ref)
            dlg_ref[...] = jnp.zeros_like(dlg_ref)
            dlb_ref[...] = jnp.zeros_like(dlb_ref)
            dcb_ref[...] = jnp.zeros_like(dcb_ref)

        au = p_ref[:, 0:A_WIDTH]
        av = p_ref[:, A_WIDTH:2 * A_WIDTH]
        gu, tu = _gelu(au)
        gv, tv = _gelu(av)
        vn, rstd = _ln_stats(gv)
        vnb = vn.astype(BF16)
        for n in range(nblk):
            rows = slice(n * GMLP_BLOCK, (n + 1) * GMLP_BLOCK)
            for g in range(A_GROUPS):
                cols = slice(g * GMLP_BLOCK, (g + 1) * GMLP_BLOCK)
                vb = vnb[rows, cols]
                sg = jnp.dot(wm_ref[g], vb, preferred_element_type=F32) + b_ref[g]
                da = dm_ref[rows, cols]
                dsg = da * gu[rows, cols]
                dgu_ref[rows, cols] = da * sg
                dsgb = dsg.astype(BF16)
                dwm_ref[g] += _dot_nt(dsgb, vb)
                db_ref[g] += jnp.sum(dsg, axis=1, keepdims=True)
                dvn_ref[rows, cols] = jnp.dot(wmt_ref[g], dsgb, preferred_element_type=F32)
        dvn = dvn_ref[...]
        dgv = rstd * (dvn - _mean(dvn) - vn * _mean(dvn * vn))
        dpa_ref[:, 0:A_WIDTH] = (dgu_ref[...] * _gelu_grad(au, tu)).astype(BF16)
        dpa_ref[:, A_WIDTH:2 * A_WIDTH] = (dgv * _gelu_grad(av, tv)).astype(BF16)
        hhat, rstd2 = _ln_stats(hc_ref[...])
        lgv = lg_ref[...]
        hl = hhat * lgv + lb_ref[...]
        s = _sigmoid(hl)
        dhl = dm_ref[:, A_WIDTH:A_WIDTH + B_WIDTH] * (s * (1.0 + hl * (1.0 - s)))
        dlg_ref[...] += jnp.sum(dhl * hhat, axis=0, keepdims=True)
        dlb_ref[...] += jnp.sum(dhl, axis=0, keepdims=True)
        dhh = dhl * lgv
        dhc = rstd2 * (dhh - _mean(dhh) - hhat * _mean(dhh * hhat))
        dcb_ref[...] += jnp.sum(dhc, axis=0, keepdims=True)
        dhc_ref[...] = dhc

    vec = pl.BlockSpec((1, B_WIDTH), lambda i: (0, 0))
    w3 = pl.BlockSpec((A_GROUPS, GMLP_BLOCK, GMLP_BLOCK), lambda i: (0, 0, 0))
    b3 = pl.BlockSpec((A_GROUPS, GMLP_BLOCK, 1), lambda i: (0, 0, 0))
    return pl.pallas_call(
        body, name=name, grid=(S // tm,),
        in_specs=[
            pl.BlockSpec((tm, 1024), lambda i: (i, 0)),
            pl.BlockSpec((tm, 1024), lambda i: (i, 0)),
            pl.BlockSpec((tm, B_WIDTH), lambda i: (i, 0)),
            w3, w3, b3, vec, vec,
        ],
        out_specs=[pl.BlockSpec((tm, 1024), lambda i: (i, 0)), pl.BlockSpec((tm, B_WIDTH), lambda i: (i, 0)),
                   w3, b3, vec, vec, vec],
        out_shape=[
            jax.ShapeDtypeStruct((S, 1024), BF16), jax.ShapeDtypeStruct((S, B_WIDTH), F32),
            jax.ShapeDtypeStruct((A_GROUPS, GMLP_BLOCK, GMLP_BLOCK), F32),
            jax.ShapeDtypeStruct((A_GROUPS, GMLP_BLOCK, 1), F32),
            jax.ShapeDtypeStruct((1, B_WIDTH), F32), jax.ShapeDtypeStruct((1, B_WIDTH), F32),
            jax.ShapeDtypeStruct((1, B_WIDTH), F32),
        ],
        scratch_shapes=[pltpu.VMEM((tm, A_WIDTH), F32), pltpu.VMEM((tm, A_WIDTH), F32)],
        compiler_params=_params(("arbitrary",)),
    )(proj, dmix, hc, wm, wmt, bcol, lg, lb)


def _even_bwd2(proj, dhc, cw, *, name):
    S = proj.shape[0]
    tm = _tile(S, 256)
    hb = tm // CONV_HALO
    nt = S // tm
    last_halo = S // CONV_HALO - 1
    lo = CONV_HALO - CONV_WIDTH + 1

    def body(p_ref, halo_ref, d_ref, dnext_ref, cw_ref, dpb_ref, dcw_ref, hext_ref, dext_ref):
        i = pl.program_id(0)

        @pl.when(i == 0)
        def _():
            dcw_ref[...] = jnp.zeros_like(dcw_ref)

        ba = p_ref[:, 0:B_WIDTH]
        sg = _sigmoid(p_ref[:, B_WIDTH:2 * B_WIDTH])
        hh = halo_ref[:, 0:B_WIDTH] * _sigmoid(halo_ref[:, B_WIDTH:2 * B_WIDTH])
        hext_ref[0:CONV_HALO, :] = jnp.where(i > 0, hh, 0.0)
        hext_ref[CONV_HALO:CONV_HALO + tm, :] = ba * sg
        dhc_t = d_ref[...]
        dext_ref[0:tm, :] = dhc_t
        dext_ref[tm:tm + CONV_HALO, :] = jnp.where(i < nt - 1, dnext_ref[...], 0.0)
        dh = jnp.zeros((tm, B_WIDTH), F32)
        for k in range(CONV_WIDTH):
            dh = dh + cw_ref[k:k + 1, :] * dext_ref[pl.ds(CONV_WIDTH - 1 - k, tm), :]
            dcw_ref[k:k + 1, :] += jnp.sum(dhc_t * hext_ref[pl.ds(k + lo, tm), :], axis=0, keepdims=True)
        dpb_ref[:, 0:B_WIDTH] = (dh * sg).astype(BF16)
        dpb_ref[:, B_WIDTH:2 * B_WIDTH] = (dh * ba * sg * (1.0 - sg)).astype(BF16)

    return pl.pallas_call(
        body, name=name, grid=(nt,),
        in_specs=[
            pl.BlockSpec((tm, 1024), lambda i: (i, 1)),
            pl.BlockSpec((CONV_HALO, 1024), lambda i: (jnp.maximum(i * hb - 1, 0), 1)),
            pl.BlockSpec((tm, B_WIDTH), lambda i: (i, 0)),
            pl.BlockSpec((CONV_HALO, B_WIDTH), lambda i: (jnp.minimum((i + 1) * hb, last_halo), 0)),
            pl.BlockSpec((CONV_HALO, B_WIDTH), lambda i: (0, 0)),
        ],
        out_specs=[pl.BlockSpec((tm, 1024), lambda i: (i, 0)), pl.BlockSpec((CONV_HALO, B_WIDTH), lambda i: (0, 0))],
        out_shape=[jax.ShapeDtypeStruct((S, 1024), BF16), jax.ShapeDtypeStruct((CONV_HALO, B_WIDTH), F32)],
        scratch_shapes=[pltpu.VMEM((tm + CONV_HALO, B_WIDTH), F32), pltpu.VMEM((tm + CONV_HALO, B_WIDTH), F32)],
        compiler_params=_params(("arbitrary",)),
    )(proj, proj, dhc, dhc, cw)


_CA_SCALE = CA_HEAD_DIM ** -0.5


def _softmax_rows(s):
    e = jnp.exp(s - jnp.max(s, axis=-1, keepdims=True))
    return e / jnp.sum(e, axis=-1, keepdims=True)


def _attn_fwd(q, k, v, *, name):
    S = q.shape[0]

    def body(r, f, o, acc, s):
        for h in range(CA_HEADS):
            cols = slice(h * CA_HEAD_DIM, (h + 1) * CA_HEAD_DIM)
            p = _softmax_rows(_dot_nt(r[0][:, cols], f[0][:, cols]) * _CA_SCALE)
            o[0][:, cols] = _dot(p.astype(BF16), f[1][:, cols]).astype(BF16)

    (o_,), _ = _rows_call(name, _tile(S, 512), [q], [k, v], [(D_MODEL, BF16)], [], body)
    return o_


def _attn_bwd(dy, wo, q, k, v, *, name):
    S = q.shape[0]
    M = k.shape[0]

    def body(r, f, o, acc, s):
        do = _dot_nt(r[0][...].astype(BF16), f[0][...]).astype(BF16)
        for h in range(CA_HEADS):
            cols = slice(h * CA_HEAD_DIM, (h + 1) * CA_HEAD_DIM)
            qh = r[1][:, cols]
            kh = f[1][:, cols]
            vh = f[2][:, cols]
            doh = do[:, cols]
            p = _softmax_rows(_dot_nt(qh, kh) * _CA_SCALE)
            acc[1][:, cols] += _dot_tn(p.astype(BF16), doh)
            dp = _dot_nt(doh, vh)
            ds = (p * (dp - jnp.sum(dp * p, axis=-1, keepdims=True)) * _CA_SCALE).astype(BF16)
            o[0][:, cols] = _dot(ds, kh).astype(BF16)
            acc[0][:, cols] += _dot_tn(ds, qh)

    (dq,), (dk, dv) = _rows_call(name, _tile(S, 512), [dy, q], [wo, k, v], [(D_MODEL, BF16)],
                                 [((M, D_MODEL), F32), ((M, D_MODEL), F32)], body)
    return dq, dk, dv


def _s5_readout(xs2, cd, u, d, *, name, tm=256):
    def body(r, f, o, acc, s):
        y = _dot(r[0][...].astype(BF16), f[0][...]) + f[1][...] * r[1][...]
        o[0][...] = y
        o[1][...] = _gelu(y)[0].astype(BF16)

    (y, yg), _ = _rows_call(name, _tile(xs2.shape[0], tm), [xs2, u], [cd, d], [(C_WIDTH, F32), (C_WIDTH, BF16)], [],
                            body)
    return y, yg


def _glu_out(yg, w, x, *, name, tm=512):
    def body(r, f, o, acc, s):
        ov = _dot(r[0][...], f[0][...])
        o[0][...] = ov.astype(BF16)
        o[1][...] = r[1][...] + ov[:, 0:D_MODEL] * _sigmoid(ov[:, D_MODEL:2 * D_MODEL])

    (o_, y), _ = _rows_call(name, _tile(x.shape[0], tm), [yg, x], [w], [(2 * D_MODEL, BF16), (D_MODEL, F32)], [], body)
    return o_, y


def _glu_out_bwd(o_, dy, w, y, u, d, *, name, tm=256):
    def body(r, f, o, acc, s):
        o1 = r[0][:, 0:D_MODEL].astype(F32)
        sg = _sigmoid(r[0][:, D_MODEL:2 * D_MODEL].astype(F32))
        dyv = r[1][...]
        do1 = (dyv * sg).astype(BF16)
        do2 = (dyv * o1 * sg * (1.0 - sg)).astype(BF16)
        o[0][:, 0:D_MODEL] = do1
        o[0][:, D_MODEL:2 * D_MODEL] = do2
        dyg = _dot_nt(do1, f[0][:, 0:D_MODEL]) + _dot_nt(do2, f[0][:, D_MODEL:2 * D_MODEL])
        yv = r[2][...]
        dys = dyg * _gelu_grad(yv, _gelu(yv)[1])
        o[1][...] = dys.astype(BF16)
        o[2][...] = f[1][...] * dys
        acc[0][...] += jnp.sum(dys * r[3][...], axis=0, keepdims=True)

    (do, dys, dus), (dd,) = _rows_call(name, _tile(dy.shape[0], tm), [o_, dy, y, u], [w, d],
                                       [(2 * D_MODEL, BF16), (C_WIDTH, BF16), (C_WIDTH, F32)], [((1, C_WIDTH), F32)],
                                       body)
    return do, dys, dus, dd


def _s5_in_bwd(gs2, bd, dus, w_in, x, g, dres, *, name, tm=256):
    D = x.shape[1]

    def body(r, f, o, acc, s):
        du = (_dot_nt(r[0][...].astype(BF16), f[1][...]) + r[1][...]).astype(BF16)
        o[0][...] = du
        dx, dg = _rms_bwd_tile(r[2][...], f[0][...], _dot_nt(du, f[2][...]))
        o[1][...] = dx + r[3][...]
        acc[0][...] += dg

    (du, dx), (dg,) = _rows_call(name, _tile(x.shape[0], tm), [gs2, dus, x, dres], [_vec(g), bd, w_in],
                                 [(C_WIDTH, BF16), (D, F32)], [((1, D), F32)], body)
    return du, dx, dg


_SCAN_CHUNK = 128
_SCAN_UNROLL = 8
_RE = slice(0, STATE_ROWS)
_IM = slice(STATE_ROWS, 2 * STATE_ROWS)


def _scan_fwd(bu, a, *, name):
    S = bu.shape[0]
    tc = _tile(S, _SCAN_CHUNK, 8)

    def body(bu_ref, a_ref, xs_ref, st_ref):
        @pl.when(pl.program_id(0) == 0)
        def _():
            st_ref[...] = jnp.zeros_like(st_ref)

        ar = a_ref[_RE, :]
        ai = a_ref[_IM, :]

        def step(t, carry):
            xr, xi = carry
            nr = ar * xr - ai * xi + bu_ref[t, _RE, :]
            ni = ar * xi + ai * xr + bu_ref[t, _IM, :]
            xs_ref[t, _RE, :] = nr
            xs_ref[t, _IM, :] = ni
            return nr, ni

        xr, xi = lax.fori_loop(0, tc, step, (st_ref[_RE, :], st_ref[_IM, :]), unroll=_SCAN_UNROLL)
        st_ref[_RE, :] = xr
        st_ref[_IM, :] = xi

    blk = pl.BlockSpec((tc, 2 * STATE_ROWS, STATE_LANES), lambda i: (i, 0, 0))
    return pl.pallas_call(
        body, name=name, grid=(S // tc,),
        in_specs=[blk, pl.BlockSpec((2 * STATE_ROWS, STATE_LANES), lambda i: (0, 0))], out_specs=blk,
        out_shape=jax.ShapeDtypeStruct(bu.shape, F32),
        scratch_shapes=[pltpu.VMEM((2 * STATE_ROWS, STATE_LANES), F32)],
        compiler_params=_params(("arbitrary",)),
    )(bu, a)


def _scan_bwd(dxs, xs, a, *, name):
    S = dxs.shape[0]
    tc = _tile(S, _SCAN_CHUNK, 8)
    nc = S // tc

    def body(dx_ref, xs_ref, a_ref, g_ref, da_ref, st_ref):
        @pl.when(pl.program_id(0) == 0)
        def _():
            st_ref[...] = jnp.zeros_like(st_ref)
            da_ref[...] = jnp.zeros_like(da_ref)

        ar = a_ref[_RE, :]
        ai = a_ref[_IM, :]

        def step(j, carry):
            gr, gi, dar, dai = carry
            t = tc - 1 - j
            xr = xs_ref[t, _RE, :]
            xi = xs_ref[t, _IM, :]
            dar = dar + gr * xr + gi * xi
            dai = dai + gi * xr - gr * xi
            nr = dx_ref[t, _RE, :] + ar * gr + ai * gi
            ni = dx_ref[t, _IM, :] + ar * gi - ai * gr
            g_ref[t, _RE, :] = nr
            g_ref[t, _IM, :] = ni
            return nr, ni, dar, dai

        init = (st_ref[_RE, :], st_ref[_IM, :], da_ref[_RE, :], da_ref[_IM, :])
        gr, gi, dar, dai = lax.fori_loop(0, tc, step, init, unroll=_SCAN_UNROLL)
        st_ref[_RE, :] = gr
        st_ref[_IM, :] = gi
        da_ref[_RE, :] = dar
        da_ref[_IM, :] = dai

    blk = pl.BlockSpec((tc, 2 * STATE_ROWS, STATE_LANES), lambda i: (nc - 1 - i, 0, 0))
    vec = pl.BlockSpec((2 * STATE_ROWS, STATE_LANES), lambda i: (0, 0))
    return pl.pallas_call(
        body, name=name, grid=(nc,), in_specs=[blk, blk, vec], out_specs=[blk, vec],
        out_shape=[jax.ShapeDtypeStruct(dxs.shape, F32), jax.ShapeDtypeStruct((2 * STATE_ROWS, STATE_LANES), F32)],
        scratch_shapes=[pltpu.VMEM((2 * STATE_ROWS, STATE_LANES), F32)],
        compiler_params=_params(("arbitrary",)),
    )(dxs, xs, a)


def _loss_head(x, g, target, *, name):
    S, D = x.shape

    def body(r, f, o, acc, s):
        xv = r[0][...]
        gv = f[0][...]
        rs = lax.rsqrt(_mean(xv * xv) + EPS)
        xh = xv * rs
        err = xh * gv - r[1][...]
        acc[1][...] += 0.5 * jnp.sum(_mean(err * err), axis=0, keepdims=True)
        dy = err * (1.0 / D)
        dyg = dy * gv
        o[0][...] = rs * (dyg - xh * _mean(dyg * xh))
        acc[0][...] += jnp.sum(dy * xh, axis=0, keepdims=True)

    (dx,), (dg, loss) = _rows_call(name, _tile(S, 256, 8), [x, target], [_vec(g)], [(D, F32)],
                                   [((1, D), F32), ((1, 128), F32)], body)
    return dx, dg, loss


_ADAM_C1 = 1.0 - ADAM_B1 ** ADAM_STEP
_ADAM_C2 = 1.0 - ADAM_B2 ** ADAM_STEP
_ONE_BLOCK_BYTES = 8 * 1024 * 1024


def _row_tile(R, row_bytes):
    return R if R * row_bytes <= _ONE_BLOCK_BYTES else _tile(R, 256, 8)


def _adamw(w, g, m, v, *, name):
    R, C = w.shape
    tr = _row_tile(R, 7 * C * 4)

    def body(w_ref, g_ref, m_ref, v_ref, d_ref, nm_ref, nv_ref):
        gv = g_ref[...]
        nm = ADAM_B1 * m_ref[...] + (1.0 - ADAM_B1) * gv
        nv = ADAM_B2 * v_ref[...] + (1.0 - ADAM_B2) * (gv * gv)
        m_hat = nm / _ADAM_C1
        v_hat = nv / _ADAM_C2
        d_ref[...] = -ADAM_LR * (m_hat / (jnp.sqrt(v_hat) + ADAM_EPS) + ADAM_WD * w_ref[...])
        nm_ref[...] = nm
        nv_ref[...] = nv

    blk = pl.BlockSpec((tr, C), lambda i: (i, 0))
    out = jax.ShapeDtypeStruct((R, C), F32)
    return pl.pallas_call(
        body, name=name, grid=(R // tr,), in_specs=[blk] * 4, out_specs=[blk] * 3, out_shape=[out] * 3,
        compiler_params=_params(("parallel",)),
    )(w, g, m, v)


def _sum_slots(x, *, name):
    n, R, C = x.shape
    tr = _row_tile(R, (n + 1) * C * 4)

    def body(x_ref, o_ref):
        acc = x_ref[0]
        for k in range(1, n):
            acc = acc + x_ref[k]
        o_ref[...] = acc

    return pl.pallas_call(
        body, name=name, grid=(R // tr,),
        in_specs=[pl.BlockSpec((n, tr, C), lambda i: (0, i, 0))], out_specs=pl.BlockSpec((tr, C), lambda i: (i, 0)),
        out_shape=jax.ShapeDtypeStruct((R, C), F32), compiler_params=_params(("parallel",)),
    )(x)


def _pair_sum(g, r, half, *, name):
    n, R, C = g.shape
    Rh = R // 2
    tr = _tile(Rh, 256, 8)
    nb = Rh // tr

    def body(half_ref, g_ref, r_ref, o_ref):
        o_ref[...] = (g_ref[...] + r_ref[...]).astype(BF16)

    return pl.pallas_call(
        body, name=name,
        grid_spec=pltpu.PrefetchScalarGridSpec(
            num_scalar_prefetch=1, grid=(n, nb),
            in_specs=[pl.BlockSpec((1, tr, C), lambda p, i, h: (p, h[0] * nb + i, 0)),
                      pl.BlockSpec((1, tr, C), lambda p, i, h: (p, i, 0))],
            out_specs=pl.BlockSpec((1, tr, C), lambda p, i, h: (p, i, 0)),
        ),
        out_shape=jax.ShapeDtypeStruct((n, Rh, C), BF16), compiler_params=_params(("parallel", "parallel")),
    )(half, g, r)


def _chip_sum(g, r, slots, where, *, name):
    n, R, C = g.shape
    Rh = R // 2
    tr = _tile(Rh, 256, 8)
    nb = Rh // tr

    def body(w_ref, g_ref, r_ref, s_ref, o_ref):
        acc = g_ref[0] + r_ref[0]
        for k in range(slots.shape[0]):
            acc = acc + s_ref[k].astype(F32)
        o_ref[...] = acc

    return pl.pallas_call(
        body, name=name,
        grid_spec=pltpu.PrefetchScalarGridSpec(
            num_scalar_prefetch=1, grid=(nb,),
            in_specs=[pl.BlockSpec((1, tr, C), lambda i, w: (w[0], w[1] * nb + i, 0)),
                      pl.BlockSpec((1, tr, C), lambda i, w: (w[0], i, 0)),
                      pl.BlockSpec((slots.shape[0], tr, C), lambda i, w: (0, i, 0))],
            out_specs=pl.BlockSpec((tr, C), lambda i, w: (i, 0)),
        ),
        out_shape=jax.ShapeDtypeStruct((Rh, C), F32), compiler_params=_params(("parallel",)),
    )(where, g, r, slots)


ANY = pl.BlockSpec(memory_space=pl.ANY)


def _place():
    return lax.axis_index("x"), lax.axis_index("y"), lax.axis_index("c")


def _other_chips(x, y):
    return [(1 - x, y), (x, 1 - y), (1 - x, 1 - y)]


def _allgather_small(v, *, name):
    R, C = v.shape

    def body(x_ref, out_ref, send_sems, recv_sems, local_sem):
        x, y, c = _place()
        me, sibling = (x, y, c), (x, y, 1 - c)
        chips = _other_chips(x, y)

        def rows(px, py, pc):
            return out_ref.at[pl.ds((4 * px + 2 * py + pc) * R, R), :]

        def copy(k, block, to, src=None):
            return pltpu.make_async_remote_copy(
                src_ref=rows(*block) if src is None else src, dst_ref=rows(*block),
                send_sem=send_sems.at[k], recv_sem=recv_sems.at[k], device_id=to, device_id_type=MESH)

        mine = pltpu.make_async_copy(x_ref, rows(*me), local_sem)
        mine.start()
        first = [copy(0, me, sibling, src=x_ref)]
        first += [copy(1 + j, me, (*chip, c), src=x_ref) for j, chip in enumerate(chips)]
        for cp in first:
            cp.start()
        passed = [copy(4 + j, (*chip, c), sibling) for j, chip in enumerate(chips)]
        for j, chip in enumerate(chips):
            copy(1 + j, (*chip, c), me).wait_recv()
            passed[j].start()
        copy(0, sibling, me).wait_recv()
        for j, chip in enumerate(chips):
            copy(4 + j, (*chip, 1 - c), me).wait_recv()
        for cp in first + passed:
            cp.wait_send()
        mine.wait()

    return pl.pallas_call(
        body, name=name, out_shape=jax.ShapeDtypeStruct((N_DEV * R, C), v.dtype),
        in_specs=[pl.BlockSpec(memory_space=pltpu.VMEM)], out_specs=pl.BlockSpec(memory_space=pltpu.VMEM),
        scratch_shapes=[pltpu.SemaphoreType.DMA((7,)), pltpu.SemaphoreType.DMA((7,)), pltpu.SemaphoreType.DMA],
        compiler_params=pltpu.CompilerParams(vmem_limit_bytes=VMEM_LIMIT),
    )(v)


def _allgather_chips(slab, *, name):
    R, C = slab.shape
    Rh = R // 2

    def body(x_ref, out_ref, send_sems, recv_sems):
        x, y, c = _place()
        chips = _other_chips(x, y)

        def half(px, py, hc):
            return out_ref.at[2 * px + py, pl.ds(hc * Rh, Rh), :]

        def copy(k, chip, hc, to, src=None):
            return pltpu.make_async_remote_copy(
                src_ref=half(*chip, hc) if src is None else src, dst_ref=half(*chip, hc),
                send_sem=send_sems.at[k], recv_sem=recv_sems.at[k], device_id=to, device_id_type=MESH)

        first = [copy(j, (x, y), c, (*chip, c), src=x_ref.at[pl.ds(c * Rh, Rh), :]) for j, chip in enumerate(chips)]
        for cp in first:
            cp.start()
        passed = [copy(3 + j, chip, c, (x, y, 1 - c)) for j, chip in enumerate(chips)]
        for j, chip in enumerate(chips):
            copy(j, chip, c, (x, y, c)).wait_recv()
            passed[j].start()
        for j, chip in enumerate(chips):
            copy(3 + j, chip, 1 - c, (x, y, c)).wait_recv()
        for cp in first + passed:
            cp.wait_send()

    return pl.pallas_call(
        body, name=name, out_shape=jax.ShapeDtypeStruct((N_CHIPS, R, C), slab.dtype),
        in_specs=[ANY], out_specs=ANY,
        scratch_shapes=[pltpu.SemaphoreType.DMA((6,)), pltpu.SemaphoreType.DMA((6,))],
    )(slab)


def _pair_exchange(g, *, name):
    n, R, C = g.shape
    Rh = R // 2

    def body(g_ref, out_ref, send_sem, recv_sem):
        x, y, c = _place()
        cp = pltpu.make_async_remote_copy(
            src_ref=g_ref.at[:, pl.ds((1 - c) * Rh, Rh), :], dst_ref=out_ref,
            send_sem=send_sem, recv_sem=recv_sem, device_id=(x, y, 1 - c), device_id_type=MESH)
        cp.start()
        cp.wait()

    return pl.pallas_call(
        body, name=name, out_shape=jax.ShapeDtypeStruct((n, Rh, C), g.dtype), in_specs=[ANY], out_specs=ANY,
        scratch_shapes=[pltpu.SemaphoreType.DMA, pltpu.SemaphoreType.DMA],
    )(g)


def _chip_exchange(h, *, name):
    n, Rh, C = h.shape

    def body(h_ref, out_ref, send_sems, recv_sems):
        x, y, c = _place()
        chips = _other_chips(x, y)
        sends = [pltpu.make_async_remote_copy(
            src_ref=h_ref.at[2 * cx + cy], dst_ref=out_ref.at[j], send_sem=send_sems.at[j], recv_sem=recv_sems.at[j],
            device_id=(cx, cy, c), device_id_type=MESH) for j, (cx, cy) in enumerate(chips)]
        for cp in sends:
            cp.start()
        for cp in sends:
            cp.wait()

    return pl.pallas_call(
        body, name=name, out_shape=jax.ShapeDtypeStruct((n - 1, Rh, C), h.dtype), in_specs=[ANY], out_specs=ANY,
        scratch_shapes=[pltpu.SemaphoreType.DMA((3,)), pltpu.SemaphoreType.DMA((3,))],
    )(h)


def _pair_share(s, *, name):
    def body(s_ref, out_ref, send_sem, recv_sem):
        x, y, c = _place()
        cp = pltpu.make_async_remote_copy(src_ref=s_ref, dst_ref=out_ref, send_sem=send_sem, recv_sem=recv_sem,
                                          device_id=(x, y, 1 - c), device_id_type=MESH)
        cp.start()
        cp.wait()

    return pl.pallas_call(
        body, name=name, out_shape=jax.ShapeDtypeStruct(s.shape, s.dtype), in_specs=[ANY], out_specs=ANY,
        scratch_shapes=[pltpu.SemaphoreType.DMA, pltpu.SemaphoreType.DMA],
    )(s)


_BIG = (("e_w_in", 2), ("e_w_out", 1), ("o_w_in", 1), ("o_w_out", 2), ("ca_wq", 1), ("ca_wk", 1), ("ca_wv", 1),
        ("ca_wo", 1), ("ffn_w_gate", 2), ("ffn_w_up", 2), ("ffn_w_down", 1))
_SMALL_SHARDED = (("e_conv_w", 2), ("o_norm", 1), ("o_d", 1))
_REPLICATED = ("e_norm", "e_gmlp_w", "e_gmlp_b", "e_conv_b", "e_conv_ln_g", "e_conv_ln_b", "o_lam_re", "o_lam_im",
               "o_log_dt", "o_b_re", "o_b_im", "o_c_re", "o_c_im", "ca_norm", "ca_mem_norm", "ffn_norm", "final_norm")
_SMALL = tuple(n for n, _ in _SMALL_SHARDED) + _REPLICATED
_WEIGHTS = ("e_norm", "e_w_in", "e_gmlp_w", "e_gmlp_b", "e_conv_w", "e_conv_b", "e_conv_ln_g", "e_conv_ln_b",
            "e_w_out", "o_norm", "o_w_in", "o_lam_re", "o_lam_im", "o_log_dt", "o_b_re", "o_b_im", "o_c_re", "o_c_im",
            "o_d", "o_w_out", "ca_norm", "ca_mem_norm", "ca_wq", "ca_wk", "ca_wv", "ca_wo", "ffn_norm", "ffn_w_gate",
            "ffn_w_up", "ffn_w_down", "final_norm")


def _pack_rows(arrs, width, dtype, row_mult=8):
    parts, spans, r0 = [], [], 0
    for a in arrs:
        flat = a.reshape(-1).astype(dtype)
        rows = -(-flat.shape[0] // (width * row_mult)) * row_mult
        if rows * width != flat.shape[0]:
            flat = jnp.pad(flat, (0, rows * width - flat.shape[0]))
        parts.append(flat.reshape(rows, width))
        spans.append((r0, rows))
        r0 += rows
    return jnp.concatenate(parts, axis=0), spans


def _unpack_rows(slab, spans, shapes):
    out = []
    for (r0, rows), shp in zip(spans, shapes):
        n = math.prod(shp)
        out.append(slab[r0:r0 + rows].reshape(-1)[:n].reshape(shp))
    return out


def _shards_first(g, ax):
    shp = g.shape
    return jnp.moveaxis(g.reshape(shp[:ax] + (N_CHIPS, shp[ax] // N_CHIPS) + shp[ax + 1:]), ax, 0).reshape(
        N_CHIPS, -1, SLAB_W)


def _shards_last(s, shard_shape, ax):
    full = list(shard_shape)
    full[ax] *= N_CHIPS
    return jnp.moveaxis(s.reshape((N_CHIPS,) + tuple(shard_shape)), 0, ax).reshape(full)


def _block_diag(b, pattern):
    return jnp.einsum(pattern, b, jnp.eye(C_GROUPS, dtype=b.dtype))


def _s5_discretize(lam_re, lam_im, log_dt, b_re, b_im):
    dt = jnp.exp(log_dt)[:, None]
    mag = jnp.exp(lam_re * dt)
    ar = mag * jnp.cos(lam_im * dt)
    ai = mag * jnp.sin(lam_im * dt)
    den = lam_re * lam_re + lam_im * lam_im
    qr = ((ar - 1.0) * lam_re + ai * lam_im) / den
    qi = (ai * lam_re - (ar - 1.0) * lam_im) / den
    bbr = qr[..., None] * b_re - qi[..., None] * b_im
    bbi = qr[..., None] * b_im + qi[..., None] * b_re
    return ar, ai, bbr, bbi


def _attention_block(x, mem, w, i, tag):
    xn, q = _norm_mm(x, w["ca_norm"][i], w["ca_wq"][i], out_dtype=BF16, name=f"{tag}_q")
    memn = _rms_fwd(mem, w["ca_mem_norm"][i], name=f"{tag}_ca_memnorm")
    k = _mm(memn, w["ca_wk"][i], out_dtype=BF16, name=f"{tag}_k")
    v = _mm(memn, w["ca_wv"][i], out_dtype=BF16, name=f"{tag}_v")
    o = _attn_fwd(q, k, v, name=f"{tag}_attn")
    y = _mm_rows(o, w["ca_wo"][i], add=x, name=f"{tag}_wo")
    return y, (x, xn, memn, q, k, v, o)


def _attention_block_bwd(dy, saved, mem, w, i, tag, grads):
    x, xn, memn, q, k, v, o = saved
    grads["ca_wo"][i] = _mm_tn(o, dy, name=f"{tag}_dwo")
    dq, dk, dv = _attn_bwd(dy, w["ca_wo"][i], q, k, v, name=f"{tag}_attn_bwd")
    grads["ca_wq"][i] = _mm_tn(xn, dq, name=f"{tag}_dwq")
    grads["ca_wk"][i] = _mm(memn, dk, ta=True, name=f"{tag}_dwk")
    grads["ca_wv"][i] = _mm(memn, dv, ta=True, name=f"{tag}_dwv")
    dmemn = _mm(dk, w["ca_wk"][i], tb=True, name=f"{tag}_dmemn_k")
    dmemn = _mm(dv, w["ca_wv"][i], tb=True, add=dmemn, name=f"{tag}_dmemn_v")
    dx, dg = _norm_bwd_mm([(dq, 0, None)], [w["ca_wq"][i]], x, w["ca_norm"][i], dy, name=f"{tag}_dq_norm_bwd")
    grads["ca_norm"][i] = dg[0]
    grads["ca_mem_norm"][i] = _rms_dg(mem, w["ca_mem_norm"][i], dmemn, name=f"{tag}_ca_memnorm_bwd")[0]
    return dx


def _ffn_block(x, w, i, tag):
    fn, gate, up, h = _ffn_up(x, w["ffn_norm"][i], w["ffn_w_gate"][i], w["ffn_w_up"][i], name=f"{tag}_ffn_up")
    y = _mm_rows(h, w["ffn_w_down"][i], add=x, name=f"{tag}_down")
    return y, (x, fn, gate, up, h)


def _ffn_block_bwd(dy, saved, w, i, tag, grads):
    x, fn, gate, up, h = saved
    grads["ffn_w_down"][i] = _mm_tn(h, dy, name=f"{tag}_dwd")
    dg, du = _ffn_bwd_hidden(dy, w["ffn_w_down"][i], gate, up, name=f"{tag}_ffn_bwd_hidden")
    grads["ffn_w_gate"][i] = _mm_tn(fn, dg, name=f"{tag}_dwg")
    grads["ffn_w_up"][i] = _mm_tn(fn, du, name=f"{tag}_dwu")
    dx, dgn = _norm_bwd_mm([(dg, 0, None), (du, 1, None)], [w["ffn_w_gate"][i], w["ffn_w_up"][i]], x,
                           w["ffn_norm"][i], dy, name=f"{tag}_ffn_in_bwd")
    grads["ffn_norm"][i] = dgn[0]
    return dx


def _gmlp_mask():
    chunk = jnp.arange(GMLP_BLOCK) // CHUNK
    return chunk[None, :] <= chunk[:, None]


def _even_block(x, w, tag):
    hn, proj = _norm_mm(x, w["e_norm"][0], w["e_w_in"][0], out_dtype=F32, name=f"{tag}_w_in")
    wm = jnp.where(_gmlp_mask()[None], w["e_gmlp_w"][0].astype(F32), 0.0).astype(BF16)
    bcol = w["e_gmlp_b"][0][:, :, None]
    cw = jnp.pad(w["e_conv_w"][0], ((0, CONV_HALO - CONV_WIDTH), (0, 0)))
    cb, lg, lb = w["e_conv_b"], w["e_conv_ln_g"], w["e_conv_ln_b"]
    mix, hc = _even_fwd(proj, wm, bcol, cw, cb, lg, lb, name=f"{tag}_mixers")
    y = _mm_rows(mix, w["e_w_out"][0], add=x, name=f"{tag}_w_out")
    return y, (x, hn, proj, mix, hc, wm, bcol, cw)


def _even_block_bwd(dy, saved, w, tag, grads):
    x, hn, proj, mix, hc, wm, bcol, cw = saved
    dmix = _mm_rows(dy, w["e_w_out"][0], nt=True, name=f"{tag}_dmix")
    grads["e_w_out"] = _mm_tn(mix, dy, name=f"{tag}_dw_out")[None]
    wmt = jnp.swapaxes(wm, 1, 2)
    dpa, dhc, dwm, db, dlg, dlb, dcb = _even_bwd1(proj, dmix, hc, wm, wmt, bcol, w["e_conv_ln_g"], w["e_conv_ln_b"],
                                                  name=f"{tag}_mixers_bwd1")
    dpb, dcw = _even_bwd2(proj, dhc, cw, name=f"{tag}_mixers_bwd2")
    grads["e_gmlp_w"] = jnp.where(_gmlp_mask()[None], dwm, 0.0)[None]
    grads["e_gmlp_b"] = db[:, :, 0][None]
    grads["e_conv_ln_g"], grads["e_conv_ln_b"], grads["e_conv_b"] = dlg, dlb, dcb
    grads["e_conv_w"] = dcw[:CONV_WIDTH][None]
    grads["e_w_in"] = jnp.concatenate([_mm_tn(hn, dpa, name=f"{tag}_dw_in_a"),
                                       _mm_tn(hn, dpb, name=f"{tag}_dw_in_b")], axis=1)[None]
    dx, dg = _norm_bwd_mm([(dpa, 0, slice(0, 1024)), (dpb, 0, slice(1024, 2048))], [w["e_w_in"][0]], x,
                          w["e_norm"][0], dy, name=f"{tag}_in_bwd")
    grads["e_norm"] = dg
    return dx


def _odd_block(x, w, tag):
    S = x.shape[0]
    hn, u = _norm_mm(x, w["o_norm"][0], w["o_w_in"][0], out_dtype=F32, name=f"{tag}_w_in")
    disc_in = (w["o_lam_re"][0], w["o_lam_im"][0], w["o_log_dt"][0], w["o_b_re"][0], w["o_b_im"][0])
    (ar, ai, bbr, bbi), disc_vjp = jax.vjp(_s5_discretize, *disc_in)
    bd = jnp.concatenate([_block_diag(bbr, "gpc,gh->gchp").reshape(C_WIDTH, N_STATE),
                          _block_diag(bbi, "gpc,gh->gchp").reshape(C_WIDTH, N_STATE)], axis=1).astype(BF16)
    cd = jnp.concatenate([_block_diag(w["o_c_re"][0], "gcp,gh->gphc").reshape(N_STATE, C_WIDTH),
                          -_block_diag(w["o_c_im"][0], "gcp,gh->gphc").reshape(N_STATE, C_WIDTH)], axis=0).astype(BF16)
    a = jnp.concatenate([ar.reshape(STATE_ROWS, STATE_LANES), ai.reshape(STATE_ROWS, STATE_LANES)], axis=0)
    bu = _mm_rows(u, bd, name=f"{tag}_bu")
    xs = _scan_fwd(bu.reshape(S, 2 * STATE_ROWS, STATE_LANES), a, name=f"{tag}_scan")
    yv, yg = _s5_readout(xs.reshape(S, 2 * N_STATE), cd, u, w["o_d"], name=f"{tag}_readout")
    o, y = _glu_out(yg, w["o_w_out"][0], x, name=f"{tag}_glu_out")
    return y, (x, hn, u, bd, cd, a, xs, yv, yg, o, disc_vjp)


def _odd_block_bwd(dy, saved, w, tag, grads):
    x, hn, u, bd, cd, a, xs, yv, yg, o, disc_vjp = saved
    S = x.shape[0]
    do, dys, dus, dd = _glu_out_bwd(o, dy, w["o_w_out"][0], yv, u, w["o_d"], name=f"{tag}_glu_out_bwd")
    grads["o_w_out"] = _mm_tn(yg, do, name=f"{tag}_dw_out")[None]
    grads["o_d"] = dd
    xs2 = xs.reshape(S, 2 * N_STATE)
    dxs = _mm_rows(dys, cd, nt=True, name=f"{tag}_dxs")
    dcd = _mm_tn(xs2, dys, name=f"{tag}_dcd")
    gs, da = _scan_bwd(dxs.reshape(S, 2 * STATE_ROWS, STATE_LANES), xs, a, name=f"{tag}_scan_bwd")
    gs2 = gs.reshape(S, 2 * N_STATE)
    dbd = _mm_tn(u, gs2, name=f"{tag}_dbd")
    du, dx, dg = _s5_in_bwd(gs2, bd, dus, w["o_w_in"][0], x, w["o_norm"][0], dy, name=f"{tag}_in_bwd")
    grads["o_w_in"] = _mm_tn(hn, du, name=f"{tag}_dw_in")[None]
    grads["o_norm"] = dg
    eye = jnp.eye(C_GROUPS, dtype=F32)
    dcr = jnp.einsum("gphc,gh->gcp", dcd[:N_STATE].reshape(C_GROUPS, C_STATE, C_GROUPS, C_GROUP_CH), eye)
    dci = -jnp.einsum("gphc,gh->gcp", dcd[N_STATE:].reshape(C_GROUPS, C_STATE, C_GROUPS, C_GROUP_CH), eye)
    dbbr = jnp.einsum("gchp,gh->gpc", dbd[:, :N_STATE].reshape(C_GROUPS, C_GROUP_CH, C_GROUPS, C_STATE), eye)
    dbbi = jnp.einsum("gchp,gh->gpc", dbd[:, N_STATE:].reshape(C_GROUPS, C_GROUP_CH, C_GROUPS, C_STATE), eye)
    dar = da[:STATE_ROWS].reshape(C_GROUPS, C_STATE)
    dai = da[STATE_ROWS:].reshape(C_GROUPS, C_STATE)
    dlr, dli, dldt, dbr, dbi = disc_vjp((dar, dai, dbbr, dbbi))
    grads["o_lam_re"], grads["o_lam_im"], grads["o_log_dt"] = dlr[None], dli[None], dldt[None]
    grads["o_b_re"], grads["o_b_im"], grads["o_c_re"], grads["o_c_im"] = dbr[None], dbi[None], dcr[None], dci[None]
    return dx


def _forward_backward(xs_, mems_, tgt, w):
    x1, s_mix0 = _even_block(xs_, w, "l0")
    x2, s_att0 = _attention_block(x1, mems_, w, 0, "l0")
    x3, s_ffn0 = _ffn_block(x2, w, 0, "l0")
    x4, s_mix1 = _odd_block(x3, w, "l1")
    x5, s_att1 = _attention_block(x4, mems_, w, 1, "l1")
    x6, s_ffn1 = _ffn_block(x5, w, 1, "l1")
    dx, dfinal, loss_lanes = _loss_head(x6, w["final_norm"], tgt, name="loss_head")

    grads = {n: [None, None] for n in ("ca_wq", "ca_wk", "ca_wv", "ca_wo", "ca_norm", "ca_mem_norm", "ffn_norm",
                                       "ffn_w_gate", "ffn_w_up", "ffn_w_down")}
    grads["final_norm"] = dfinal[0]
    dx = _ffn_block_bwd(dx, s_ffn1, w, 1, "l1", grads)
    dx = _attention_block_bwd(dx, s_att1, mems_, w, 1, "l1", grads)
    dx = _odd_block_bwd(dx, s_mix1, w, "l1", grads)
    dx = _ffn_block_bwd(dx, s_ffn0, w, 0, "l0", grads)
    dx = _attention_block_bwd(dx, s_att0, mems_, w, 0, "l0", grads)
    dx = _even_block_bwd(dx, s_mix0, w, "l0", grads)
    for n in list(grads):
        if isinstance(grads[n], list):
            grads[n] = jnp.stack(grads[n], axis=0)
        grads[n] = grads[n].reshape(w[n].shape)
    return loss_lanes, dx, grads


def kernel(x, mem, e_norm, e_w_in, e_gmlp_w, e_gmlp_b, e_conv_w, e_conv_b, e_conv_ln_g, e_conv_ln_b, e_w_out, o_norm, o_w_in, o_lam_re, o_lam_im, o_log_dt, o_b_re, o_b_im, o_c_re, o_c_im, o_d, o_w_out, ca_norm, ca_mem_norm, ca_wq, ca_wk, ca_wv, ca_wo, ffn_norm, ffn_w_gate, ffn_w_up, ffn_w_down, final_norm, loss_target, m_e_norm, m_e_w_in, m_e_gmlp_w, m_e_gmlp_b, m_e_conv_w, m_e_conv_b, m_e_conv_ln_g, m_e_conv_ln_b, m_e_w_out, m_o_norm, m_o_w_in, m_o_lam_re, m_o_lam_im, m_o_log_dt, m_o_b_re, m_o_b_im, m_o_c_re, m_o_c_im, m_o_d, m_o_w_out, m_ca_norm, m_ca_mem_norm, m_ca_wq, m_ca_wk, m_ca_wv, m_ca_wo, m_ffn_norm, m_ffn_w_gate, m_ffn_w_up, m_ffn_w_down, m_final_norm, v_e_norm, v_e_w_in, v_e_gmlp_w, v_e_gmlp_b, v_e_conv_w, v_e_conv_b, v_e_conv_ln_g, v_e_conv_ln_b, v_e_w_out, v_o_norm, v_o_w_in, v_o_lam_re, v_o_lam_im, v_o_log_dt, v_o_b_re, v_o_b_im, v_o_c_re, v_o_c_im, v_o_d, v_o_w_out, v_ca_norm, v_ca_mem_norm, v_ca_wq, v_ca_wk, v_ca_wv, v_ca_wo, v_ffn_norm, v_ffn_w_gate, v_ffn_w_up, v_ffn_w_down, v_final_norm):
    args = dict(locals())
    local = {n: args[n] for n in _WEIGHTS}
    mom = {n: args["m_" + n] for n in _WEIGHTS}
    vel = {n: args["v_" + n] for n in _WEIGHTS}
    chip = 2 * lax.axis_index("x") + lax.axis_index("y")
    core = lax.axis_index("c")
    xs_, mems_, tgt = x[0], mem[0], loss_target[0]

    big_slab, big_spans = _pack_rows([local[n] for n, _ in _BIG], SLAB_W, BF16, row_mult=32)
    gathered = _allgather_chips(big_slab, name="gather_weights")
    own = (jnp.arange(N_CHIPS) == chip)[:, None, None]
    w = {n: local[n] for n in _REPLICATED}
    for (n, ax), (r0, rows) in zip(_BIG, big_spans):
        shards = jnp.where(own, big_slab[None, r0:r0 + rows], gathered[:, r0:r0 + rows])
        w[n] = _shards_last(shards, local[n].shape, ax)
    sm_slab, sm_spans = _pack_rows([local[n] for n, _ in _SMALL_SHARDED], SMALL_W, F32)
    sm_all = _allgather_small(sm_slab, name="gather_small_weights").reshape(N_DEV, -1, SMALL_W)
    for (n, ax), span in zip(_SMALL_SHARDED, sm_spans):
        shp = local[n].shape
        w[n] = jnp.concatenate([_unpack_rows(sm_all[2 * p], [span], [shp])[0] for p in range(N_CHIPS)], axis=ax)

    loss_lanes, dx, grads = _forward_backward(xs_, mems_, tgt, w)

    gslab = jnp.concatenate([_shards_first(grads[n], ax) for n, ax in _BIG], axis=1)
    other_half = _pair_exchange(gslab, name="grad_pair_exchange")
    half = core.reshape(1).astype(jnp.int32)
    pair = _pair_sum(gslab, other_half, half, name="grad_pair_sum")
    slots = _chip_exchange(pair, name="grad_chip_exchange")
    where = jnp.stack([chip, core]).astype(jnp.int32)
    mine = _chip_sum(gslab, other_half, slots, where, name="grad_chip_sum")
    theirs = _pair_share(mine, name="grad_pair_share")
    gsum = jnp.concatenate([jnp.where(core == 0, mine, theirs), jnp.where(core == 0, theirs, mine)], axis=0)
    big_grads = dict(zip([n for n, _ in _BIG], _unpack_rows(gsum, big_spans, [local[n].shape for n, _ in _BIG])))

    gs_slab, gs_spans = _pack_rows([grads[n] for n in _SMALL], SMALL_W, F32)
    gs_all = _allgather_small(gs_slab, name="gather_small_grads").reshape(N_DEV, -1, SMALL_W)
    gs_sum = _sum_slots(gs_all, name="small_grad_sum")
    small_full = dict(zip(_SMALL, _unpack_rows(gs_sum, gs_spans, [grads[n].shape for n in _SMALL])))
    small_grads = {}
    for n in _SMALL:
        g = small_full[n]
        for sn, ax in _SMALL_SHARDED:
            if sn == n:
                width = local[n].shape[ax]
                g = lax.dynamic_slice_in_dim(g, chip * width, width, axis=ax)
        small_grads[n] = g

    delta, new_m, new_v = {}, {}, {}
    for n, _ in _BIG:
        shp = local[n].shape
        two_d = (-1, shp[-1])
        d_, m_, v_ = _adamw(local[n].reshape(two_d), big_grads[n].reshape(two_d), mom[n].reshape(two_d),
                            vel[n].reshape(two_d), name=f"adamw_{n}")
        delta[n], new_m[n], new_v[n] = d_.reshape(shp), m_.reshape(shp), v_.reshape(shp)
    ws, spans = _pack_rows([local[n] for n in _SMALL], SMALL_W, F32)
    gsl = _pack_rows([small_grads[n] for n in _SMALL], SMALL_W, F32)[0]
    msl = _pack_rows([mom[n] for n in _SMALL], SMALL_W, F32)[0]
    vsl = _pack_rows([vel[n] for n in _SMALL], SMALL_W, F32)[0]
    d_, m_, v_ = _adamw(ws, gsl, msl, vsl, name="adamw_small")
    shapes = [local[n].shape for n in _SMALL]
    for n, dd, mm_, vv in zip(_SMALL, _unpack_rows(d_, spans, shapes), _unpack_rows(m_, spans, shapes),
                              _unpack_rows(v_, spans, shapes)):
        delta[n], new_m[n], new_v[n] = dd, mm_, vv

    out_grads = {**big_grads, **small_grads}
    loss = lax.psum(loss_lanes[0, 0], ("x", "y", "c"))
    return (loss, dx[None], *[out_grads[n] for n in _WEIGHTS], *[delta[n] for n in _WEIGHTS],
            *[new_m[n] for n in _WEIGHTS], *[new_v[n] for n in _WEIGHTS])
```

```python
import functools
import math

import jax
import jax.numpy as jnp
from jax import lax
from jax.experimental import pallas as pl
from jax.experimental.pallas import tpu as pltpu

F32 = jnp.float32
BF16 = jnp.bfloat16
MESH = pl.DeviceIdType.MESH

EPS = 1e-6
D_MODEL = 1024
A_WIDTH = 512
A_GROUPS = 4
GMLP_BLOCK = 128
CHUNK = 64
B_WIDTH = 512
CONV_WIDTH = 31
CONV_HALO = 32
C_WIDTH = 512
C_GROUP_CH = 16
C_GROUPS = 32
C_STATE = 64
N_STATE = C_GROUPS * C_STATE
STATE_ROWS = 8
STATE_LANES = N_STATE // STATE_ROWS
CA_HEADS = 4
CA_HEAD_DIM = 256
FFN_HIDDEN = 2816

ADAM_LR = 0.001
ADAM_B1 = 0.9
ADAM_B2 = 0.999
ADAM_EPS = 1e-08
ADAM_WD = 0.01
ADAM_STEP = 10

VMEM_LIMIT = 56 * 1024 * 1024
ACC_BYTES = 6 * 1024 * 1024
SMALL_W = 128
N_CHIPS = 4
N_DEV = 8

_SLABS = {"A": (1024, 2304), "B": (1024, 1408), "C": (704, 4096), "D": (512, 1792)}
_PLACE = {
    "ca_wq": ("A", 0, 256, 2), "ca_wk": ("A", 512, 256, 2), "ca_wv": ("A", 1024, 256, 2), "ca_wo": ("A", 1536, 256, 2),
    "e_w_out": ("A", 2048, 256, 1),
    "ffn_w_down": ("B", 0, 704, 2),
    "ffn_w_gate": ("C", 0, 1024, 2), "ffn_w_up": ("C", 2048, 1024, 2),
    "e_w_in": ("D", 0, 1024, 1), "o_w_out": ("D", 1024, 512, 1), "o_w_in": ("D", 1536, 256, 1),
}


def _params(sem=None):
    return pltpu.CompilerParams(dimension_semantics=sem, vmem_limit_bytes=VMEM_LIMIT)


def _tile(n, pref, mult=128):
    if n <= pref:
        return n
    t = (pref // mult) * mult
    while t >= mult:
        if n % t == 0:
            return t
        t -= mult
    return n


def _blk(name, layer=0):
    slab, r0, rows, layers = _PLACE[name]
    assert layer < layers and (r0 + layer * rows) % rows == 0
    return slab, rows, (r0 + layer * rows) // rows


def _shards(slabs, name, layer=0):
    slab, rows, b = _blk(name, layer)
    return [(slabs[slab], (None, rows, _SLABS[slab][0]), (p, b, 0)) for p in range(N_CHIPS)]


_GELU_C = 0.7978845608028654
_GELU_A = 0.044715


def _gelu(x):
    t = jnp.tanh(_GELU_C * (x + _GELU_A * (x * x * x)))
    return 0.5 * x * (1.0 + t), t


def _gelu_grad(x, t):
    return 0.5 * (1.0 + t) + 0.5 * x * (1.0 - t * t) * (_GELU_C * (1.0 + 3.0 * _GELU_A * x * x))


def _sigmoid(x):
    return 1.0 / (1.0 + jnp.exp(-x))


def _mean(x):
    return jnp.mean(x, axis=-1, keepdims=True)


def _dot(a, b):
    return jnp.dot(a, b, preferred_element_type=F32)


def _dot_nt(a, b):
    return lax.dot_general(a, b, (((1,), (1,)), ((), ())), preferred_element_type=F32)


def _dot_tn(a, b):
    return lax.dot_general(a, b, (((0,), (0,)), ((), ())), preferred_element_type=F32)


def _rms_tile(xv, gv):
    return (xv * lax.rsqrt(_mean(xv * xv) + EPS)) * gv


def _rms_bwd_tile(xv, gv, dyv):
    r = lax.rsqrt(_mean(xv * xv) + EPS)
    xh = xv * r
    dyg = dyv * gv
    return r * (dyg - xh * _mean(dyg * xh)), jnp.sum(dyv * xh, axis=0, keepdims=True)


def _cols(p, width):
    return slice(p * width, (p + 1) * width)


def _sum_k(a, ws, k):
    tot = None
    for p in range(N_CHIPS):
        y = _dot(a[:, _cols(p, k)], ws[p][...])
        tot = y if tot is None else tot + y
    return tot


def _cat_nt(a, ws):
    return jnp.concatenate([_dot_nt(a, ws[p][...]) for p in range(N_CHIPS)], axis=1)


def _rows_call(name, tm, rows, fulls, outs, accs, body, scratch=()):
    S = rows[0].shape[-2]
    nr, nf, no, na = len(rows), len(fulls), len(outs), len(accs)

    def kern(*refs):
        r, f = refs[:nr], refs[nr:nr + nf]
        o, a = refs[nr + nf:nr + nf + no], refs[nr + nf + no:nr + nf + no + na]
        if na:
            @pl.when(pl.program_id(0) == 0)
            def _():
                for ref in a:
                    ref[...] = jnp.zeros_like(ref)
        body(r, f, o, a, refs[nr + nf + no + na:])

    def whole(shape):
        nd = len(shape)
        return pl.BlockSpec(tuple(shape), lambda i: (0,) * nd)

    def row_spec(shape):
        if len(shape) == 3:
            return pl.BlockSpec((shape[0], tm, shape[2]), lambda i: (0, i, 0))
        return pl.BlockSpec((tm, shape[1]), lambda i: (i, 0))

    def full_spec(x):
        if isinstance(x, tuple):
            _, bshape, bidx = x
            return pl.BlockSpec(bshape, lambda i: bidx, pipeline_mode=pl.Buffered(1))
        return whole(x.shape)

    out_shapes = [(S, o[0]) if len(o) == 2 else (o[0], S, o[1]) for o in outs]
    res = pl.pallas_call(
        kern, name=name, grid=(S // tm,),
        in_specs=[row_spec(x.shape) for x in rows] + [full_spec(x) for x in fulls],
        out_specs=[row_spec(s) for s in out_shapes] + [whole(shp) for shp, _ in accs],
        out_shape=[jax.ShapeDtypeStruct(s, o[-1]) for s, o in zip(out_shapes, outs)]
        + [jax.ShapeDtypeStruct(tuple(shp), dt) for shp, dt in accs],
        scratch_shapes=list(scratch),
        compiler_params=_params(("arbitrary",) if na else ("parallel",)),
    )(*rows, *[x[0] if isinstance(x, tuple) else x for x in fulls])
    return res[:no], res[no:]


def _mm_rows(a, w, *, nt=False, out_dtype=F32, tm=512, name):
    S = a.shape[0]
    N = w.shape[0] if nt else w.shape[1]

    def body(r, f, o, acc, s):
        av = r[0][...].astype(BF16)
        o[0][...] = (_dot_nt(av, f[0][...]) if nt else _dot(av, f[0][...])).astype(out_dtype)

    (y,), _ = _rows_call(name, _tile(S, tm), [a], [w], [(N, out_dtype)], [], body)
    return y


def _mm_tn(a, b, *, name):
    S, K1 = a.shape
    N = b.shape[1]
    tn = _tile(N, max(128, (ACC_BYTES // (4 * K1)) // 128 * 128))
    ts = _tile(S, 512 if K1 * a.dtype.itemsize * 512 <= 4 * 1024 * 1024 else 256)

    def body(a_ref, b_ref, o_ref):
        @pl.when(pl.program_id(1) == 0)
        def _():
            o_ref[...] = jnp.zeros_like(o_ref)

        o_ref[...] += _dot_tn(a_ref[...].astype(BF16), b_ref[...].astype(BF16))

    return pl.pallas_call(
        body, name=name, grid=(N // tn, S // ts),
        in_specs=[pl.BlockSpec((ts, K1), lambda j, s: (s, 0)), pl.BlockSpec((ts, tn), lambda j, s: (s, j))],
        out_specs=pl.BlockSpec((K1, tn), lambda j, s: (0, j)),
        out_shape=jax.ShapeDtypeStruct((K1, N), F32),
        compiler_params=_params(("parallel", "arbitrary")),
    )(a, b)


def _grad_to_slab(gslabs, wname, layer, a, b, *, a_cols=None, b_cols=None, chips=(0, N_CHIPS), name):
    slab, rows, bidx = _blk(wname, layer)
    width = _SLABS[slab][0]
    p0, n_p = chips
    S = a.shape[-2]
    ts = _tile(S, 512)

    def operand(x, cols):
        if x.ndim == 3:
            return pl.BlockSpec((None, ts, x.shape[2]), lambda p, s: (p + p0, s, 0))
        if cols is not None:
            return pl.BlockSpec((ts, cols), lambda p, s: (s, p))
        return pl.BlockSpec((ts, x.shape[1]), lambda p, s: (s, 0))

    def body(a_ref, b_ref, slab_ref, o_ref):
        @pl.when(pl.program_id(1) == 0)
        def _():
            o_ref[...] = jnp.zeros_like(o_ref)

        o_ref[...] += _dot_tn(a_ref[...].astype(BF16), b_ref[...].astype(BF16))

    g = gslabs[slab]
    out = pl.pallas_call(
        body, name=name, grid=(n_p, S // ts),
        in_specs=[operand(a, a_cols), operand(b, b_cols), pl.BlockSpec(memory_space=pl.ANY)],
        out_specs=pl.BlockSpec((None, rows, width), lambda p, s: (p + p0, bidx, 0)),
        out_shape=jax.ShapeDtypeStruct(g.shape, F32), input_output_aliases={2: 0},
        compiler_params=_params(("parallel", "arbitrary")),
    )(a, b, g)
    return {**gslabs, slab: out}


def _vec(g):
    return g.reshape(1, -1)


def _norm_mm(x, g, ws, *, split, out_dtype, name, tm=512):
    S, D = x.shape
    k, n = ws[0][1][1], ws[0][1][2]
    N = n if split == "k" else N_CHIPS * n

    def body(r, f, o, acc, s):
        xn = _rms_tile(r[0][...], f[0][...]).astype(BF16)
        o[0][...] = xn
        if split == "k":
            o[1][...] = _sum_k(xn, f[1:], k).astype(out_dtype)
        else:
            for p in range(N_CHIPS):
                o[1][:, _cols(p, n)] = _dot(xn, f[1 + p][...]).astype(out_dtype)

    (xn, y), _ = _rows_call(name, _tile(S, tm), [x], [_vec(g)] + ws, [(D, BF16), (N, out_dtype)], [], body)
    return xn, y


def _mm_k(a, ws, *, add=None, out_dtype=F32, name, tm=512):
    S = a.shape[-2]
    k, n = ws[0][1][1], ws[0][1][2]
    has_add = add is not None

    def body(r, f, o, acc, s):
        if a.ndim == 3:
            y = None
            for p in range(N_CHIPS):
                t = _dot(r[0][p].astype(BF16), f[p][...])
                y = t if y is None else y + t
        else:
            y = _sum_k(r[0][...].astype(BF16), f, k)
        if has_add:
            y = y + r[1][...]
        o[0][...] = y.astype(out_dtype)

    (y,), _ = _rows_call(name, _tile(S, tm), [a] + ([add] if has_add else []), ws, [(n, out_dtype)], [], body)
    return y


def _mm_k_t(terms, *, out_dtype=F32, name, tm=512):
    S = terms[0][0].shape[0]
    k = terms[0][1][0][1][1]

    def body(r, f, o, acc, s):
        y = None
        for t in range(len(terms)):
            yt = _cat_nt(r[t][...].astype(BF16), f[N_CHIPS * t:N_CHIPS * (t + 1)])
            y = yt if y is None else y + yt
        o[0][...] = y.astype(out_dtype)

    (y,), _ = _rows_call(name, _tile(S, tm), [a for a, _ in terms], [w for _, ws in terms for w in ws],
                         [(N_CHIPS * k, out_dtype)], [], body)
    return y


def _rms_fwd(x, g, *, name):
    def body(r, f, o, acc, s):
        o[0][...] = _rms_tile(r[0][...], f[0][...]).astype(BF16)

    (y,), _ = _rows_call(name, _tile(x.shape[0], 256, 8), [x], [_vec(g)], [(x.shape[1], BF16)], [], body)
    return y


def _rms_dg(x, g, dy, *, name):
    def body(r, f, o, acc, s):
        acc[0][...] += _rms_bwd_tile(r[0][...], f[0][...], r[1][...])[1]

    _, (dg,) = _rows_call(name, _tile(x.shape[0], 256, 8), [x, dy], [_vec(g)], [], [((1, x.shape[1]), F32)], body)
    return dg


def _ffn_up(x, g, wg, wu, *, name, tm=256):
    S, D = x.shape
    h = wg[0][1][2]

    def body(r, f, o, acc, s):
        xn = _rms_tile(r[0][...], f[0][...]).astype(BF16)
        o[0][...] = xn
        for p in range(N_CHIPS):
            gate = _dot(xn, f[1 + p][...])
            up = _dot(xn, f[1 + N_CHIPS + p][...])
            o[1][p] = gate.astype(BF16)
            o[2][p] = up.astype(BF16)
            o[3][p] = (gate * _sigmoid(gate) * up).astype(BF16)

    (xn, gate, up, hid), _ = _rows_call(name, _tile(S, tm), [x], [_vec(g)] + wg + wu,
                                        [(D, BF16), (N_CHIPS, h, BF16), (N_CHIPS, h, BF16), (N_CHIPS, h, BF16)], [],
                                        body)
    return xn, gate, up, hid


def _ffn_bwd_hidden(dy, wd, gate, up, *, name, tm=256):
    S = dy.shape[0]
    h = wd[0][1][1]

    def body(r, f, o, acc, s):
        dyb = r[0][...].astype(BF16)
        for p in range(N_CHIPS):
            dh = _dot_nt(dyb, f[p][...])
            gv = r[1][p].astype(F32)
            sg = _sigmoid(gv)
            o[0][p] = (dh * r[2][p].astype(F32) * (sg * (1.0 + gv * (1.0 - sg)))).astype(BF16)
            o[1][p] = (dh * gv * sg).astype(BF16)

    (dg, du), _ = _rows_call(name, _tile(S, tm), [dy, gate, up], wd, [(N_CHIPS, h, BF16), (N_CHIPS, h, BF16)], [],
                             body)
    return dg, du


def _ffn_in_bwd(dg, du, wg, wu, x, g, dres, *, name, tm=256):
    S, D = x.shape

    def body(r, f, o, acc, s):
        tot = None
        for p in range(N_CHIPS):
            y = _dot_nt(r[0][p], f[1 + p][...]) + _dot_nt(r[1][p], f[1 + N_CHIPS + p][...])
            tot = y if tot is None else tot + y
        dx, dgn = _rms_bwd_tile(r[2][...], f[0][...], tot)
        o[0][...] = dx + r[3][...]
        acc[0][...] += dgn

    (dx,), (dgn,) = _rows_call(name, _tile(S, tm), [dg, du, x, dres], [_vec(g)] + wg + wu, [(D, F32)],
                               [((1, D), F32)], body)
    return dx, dgn


def _norm_bwd_k(da, ws, x, g, dres, *, name, tm=512):
    S, D = x.shape

    def body(r, f, o, acc, s):
        dx, dg = _rms_bwd_tile(r[1][...], f[0][...], _cat_nt(r[0][...].astype(BF16), f[1:]))
        o[0][...] = dx + r[2][...]
        acc[0][...] += dg

    (dx,), (dg,) = _rows_call(name, _tile(S, tm), [da, x, dres], [_vec(g)] + ws, [(D, F32)], [((1, D), F32)], body)
    return dx, dg


def _norm_bwd_n(das, ws, x, g, dres, *, name, tm=256):
    S, D = x.shape
    n = ws[0][1][2]

    def body(r, f, o, acc, s):
        tot = None
        for p in range(N_CHIPS):
            y = _dot_nt(r[p // 2][:, _cols(p % 2, n)], f[1 + p][...])
            tot = y if tot is None else tot + y
        dx, dg = _rms_bwd_tile(r[2][...], f[0][...], tot)
        o[0][...] = dx + r[3][...]
        acc[0][...] += dg

    (dx,), (dg,) = _rows_call(name, _tile(S, tm), list(das) + [x, dres], [_vec(g)] + ws, [(D, F32)], [((1, D), F32)],
                              body)
    return dx, dg


def _ln_stats(v):
    mu = _mean(v)
    xc = v - mu
    rstd = lax.rsqrt(_mean(xc * xc) + EPS)
    return xc * rstd, rstd


def _even_fwd(proj, wm, bcol, cw, cb, lg, lb, *, name):
    S = proj.shape[0]
    tm = _tile(S, 256)
    hb = tm // CONV_HALO
    nblk = tm // GMLP_BLOCK

    def body(p_ref, halo_ref, wm_ref, b_ref, cw_ref, cb_ref, lg_ref, lb_ref, mix_ref, hc_ref, hext_ref):
        i = pl.program_id(0)
        gu, _ = _gelu(p_ref[:, 0:A_WIDTH])
        gv, _ = _gelu(p_ref[:, A_WIDTH:2 * A_WIDTH])
        vn, _ = _ln_stats(gv)
        vnb = vn.astype(BF16)
        for n in range(nblk):
            rows = slice(n * GMLP_BLOCK, (n + 1) * GMLP_BLOCK)
            for g in range(A_GROUPS):
                cols = slice(g * GMLP_BLOCK, (g + 1) * GMLP_BLOCK)
                sg = jnp.dot(wm_ref[g], vnb[rows, cols], preferred_element_type=F32) + b_ref[g]
                mix_ref[rows, cols] = (gu[rows, cols] * sg).astype(BF16)
        h = p_ref[:, 1024:1536] * _sigmoid(p_ref[:, 1536:2048])
        hh = halo_ref[:, 0:B_WIDTH] * _sigmoid(halo_ref[:, B_WIDTH:2 * B_WIDTH])
        hext_ref[0:CONV_HALO, :] = jnp.where(i > 0, hh, 0.0)
        hext_ref[CONV_HALO:CONV_HALO + tm, :] = h
        acc = jnp.zeros((tm, B_WIDTH), F32)
        for k in range(CONV_WIDTH):
            acc = acc + cw_ref[k:k + 1, :] * hext_ref[pl.ds(k + CONV_HALO - CONV_WIDTH + 1, tm), :]
        hc = acc + cb_ref[...]
        hc_ref[...] = hc
        hhat, _ = _ln_stats(hc)
        hl = hhat * lg_ref[...] + lb_ref[...]
        mix_ref[:, A_WIDTH:A_WIDTH + B_WIDTH] = (hl * _sigmoid(hl)).astype(BF16)

    vec = pl.BlockSpec((1, B_WIDTH), lambda i: (0, 0))
    return pl.pallas_call(
        body, name=name, grid=(S // tm,),
        in_specs=[
            pl.BlockSpec((tm, 2048), lambda i: (i, 0)),
            pl.BlockSpec((CONV_HALO, 1024), lambda i: (jnp.maximum(i * hb - 1, 0), 1)),
            pl.BlockSpec((A_GROUPS, GMLP_BLOCK, GMLP_BLOCK), lambda i: (0, 0, 0)),
            pl.BlockSpec((A_GROUPS, GMLP_BLOCK, 1), lambda i: (0, 0, 0)),
            pl.BlockSpec((CONV_HALO, B_WIDTH), lambda i: (0, 0)),
            vec, vec, vec,
        ],
        out_specs=[pl.BlockSpec((tm, 1024), lambda i: (i, 0)), pl.BlockSpec((tm, B_WIDTH), lambda i: (i, 0))],
        out_shape=[jax.ShapeDtypeStruct((S, 1024), BF16), jax.ShapeDtypeStruct((S, B_WIDTH), F32)],
        scratch_shapes=[pltpu.VMEM((tm + CONV_HALO, B_WIDTH), F32)],
        compiler_params=_params(("parallel",)),
    )(proj, proj, wm, bcol, cw, cb, lg, lb)


def _even_bwd1(proj, dmix, hc, wm, wmt, bcol, lg, lb, *, name):
    S = proj.shape[0]
    tm = _tile(S, 256)
    nblk = tm // GMLP_BLOCK

    def body(p_ref, dm_ref, hc_ref, wm_ref, wmt_ref, b_ref, lg_ref, lb_ref,
             dpa_ref, dhc_ref, dwm_ref, db_ref, dlg_ref, dlb_ref, dcb_ref, dgu_ref, dvn_ref):
        @pl.when(pl.program_id(0) == 0)
        def _():
            dwm_ref[...] = jnp.zeros_like(dwm_ref)
            db_ref[...] = jnp.zeros_like(db_ref)
            dlg_ref[...] = jnp.zeros_like(dlg_ref)
            dlb_ref[...] = jnp.zeros_like(dlb_ref)
            dcb_ref[...] = jnp.zeros_like(dcb_ref)

        au = p_ref[:, 0:A_WIDTH]
        av = p_ref[:, A_WIDTH:2 * A_WIDTH]
        gu, tu = _gelu(au)
        gv, tv = _gelu(av)
        vn, rstd = _ln_stats(gv)
        vnb = vn.astype(BF16)
        for n in range(nblk):
            rows = slice(n * GMLP_BLOCK, (n + 1) * GMLP_BLOCK)
            for g in range(A_GROUPS):
                cols = slice(g * GMLP_BLOCK, (g + 1) * GMLP_BLOCK)
                vb = vnb[rows, cols]
                sg = jnp.dot(wm_ref[g], vb, preferred_element_type=F32) + b_ref[g]
                da = dm_ref[rows, cols]
                dsg = da * gu[rows, cols]
                dgu_ref[rows, cols] = da * sg
                dsgb = dsg.astype(BF16)
                dwm_ref[g] += _dot_nt(dsgb, vb)
                db_ref[g] += jnp.sum(dsg, axis=1, keepdims=True)
                dvn_ref[rows, cols] = jnp.dot(wmt_ref[g], dsgb, preferred_element_type=F32)
        dvn = dvn_ref[...]
        dgv = rstd * (dvn - _mean(dvn) - vn * _mean(dvn * vn))
        dpa_ref[:, 0:A_WIDTH] = (dgu_ref[...] * _gelu_grad(au, tu)).astype(BF16)
        dpa_ref[:, A_WIDTH:2 * A_WIDTH] = (dgv * _gelu_grad(av, tv)).astype(BF16)
        hhat, rstd2 = _ln_stats(hc_ref[...])
        lgv = lg_ref[...]
        hl = hhat * lgv + lb_ref[...]
        s = _sigmoid(hl)
        dhl = dm_ref[:, A_WIDTH:A_WIDTH + B_WIDTH] * (s * (1.0 + hl * (1.0 - s)))
        dlg_ref[...] += jnp.sum(dhl * hhat, axis=0, keepdims=True)
        dlb_ref[...] += jnp.sum(dhl, axis=0, keepdims=True)
        dhh = dhl * lgv
        dhc = rstd2 * (dhh - _mean(dhh) - hhat * _mean(dhh * hhat))
        dcb_ref[...] += jnp.sum(dhc, axis=0, keepdims=True)
        dhc_ref[...] = dhc

    vec = pl.BlockSpec((1, B_WIDTH), lambda i: (0, 0))
    w3 = pl.BlockSpec((A_GROUPS, GMLP_BLOCK, GMLP_BLOCK), lambda i: (0, 0, 0))
    b3 = pl.BlockSpec((A_GROUPS, GMLP_BLOCK, 1), lambda i: (0, 0, 0))
    return pl.pallas_call(
        body, name=name, grid=(S // tm,),
        in_specs=[
            pl.BlockSpec((tm, 1024), lambda i: (i, 0)),
            pl.BlockSpec((tm, 1024), lambda i: (i, 0)),
            pl.BlockSpec((tm, B_WIDTH), lambda i: (i, 0)),
            w3, w3, b3, vec, vec,
        ],
        out_specs=[pl.BlockSpec((tm, 1024), lambda i: (i, 0)), pl.BlockSpec((tm, B_WIDTH), lambda i: (i, 0)),
                   w3, b3, vec, vec, vec],
        out_shape=[
            jax.ShapeDtypeStruct((S, 1024), BF16), jax.ShapeDtypeStruct((S, B_WIDTH), F32),
            jax.ShapeDtypeStruct((A_GROUPS, GMLP_BLOCK, GMLP_BLOCK), F32),
            jax.ShapeDtypeStruct((A_GROUPS, GMLP_BLOCK, 1), F32),
            jax.ShapeDtypeStruct((1, B_WIDTH), F32), jax.ShapeDtypeStruct((1, B_WIDTH), F32),
            jax.ShapeDtypeStruct((1, B_WIDTH), F32),
        ],
        scratch_shapes=[pltpu.VMEM((tm, A_WIDTH), F32), pltpu.VMEM((tm, A_WIDTH), F32)],
        compiler_params=_params(("arbitrary",)),
    )(proj, dmix, hc, wm, wmt, bcol, lg, lb)


def _even_bwd2(proj, dhc, cw, *, name):
    S = proj.shape[0]
    tm = _tile(S, 256)
    hb = tm // CONV_HALO
    nt = S // tm
    last_halo = S // CONV_HALO - 1
    lo = CONV_HALO - CONV_WIDTH + 1

    def body(p_ref, halo_ref, d_ref, dnext_ref, cw_ref, dpb_ref, dcw_ref, hext_ref, dext_ref):
        i = pl.program_id(0)

        @pl.when(i == 0)
        def _():
            dcw_ref[...] = jnp.zeros_like(dcw_ref)

        ba = p_ref[:, 0:B_WIDTH]
        sg = _sigmoid(p_ref[:, B_WIDTH:2 * B_WIDTH])
        hh = halo_ref[:, 0:B_WIDTH] * _sigmoid(halo_ref[:, B_WIDTH:2 * B_WIDTH])
        hext_ref[0:CONV_HALO, :] = jnp.where(i > 0, hh, 0.0)
        hext_ref[CONV_HALO:CONV_HALO + tm, :] = ba * sg
        dhc_t = d_ref[...]
        dext_ref[0:tm, :] = dhc_t
        dext_ref[tm:tm + CONV_HALO, :] = jnp.where(i < nt - 1, dnext_ref[...], 0.0)
        dh = jnp.zeros((tm, B_WIDTH), F32)
        for k in range(CONV_WIDTH):
            dh = dh + cw_ref[k:k + 1, :] * dext_ref[pl.ds(CONV_WIDTH - 1 - k, tm), :]
            dcw_ref[k:k + 1, :] += jnp.sum(dhc_t * hext_ref[pl.ds(k + lo, tm), :], axis=0, keepdims=True)
        dpb_ref[:, 0:B_WIDTH] = (dh * sg).astype(BF16)
        dpb_ref[:, B_WIDTH:2 * B_WIDTH] = (dh * ba * sg * (1.0 - sg)).astype(BF16)

    return pl.pallas_call(
        body, name=name, grid=(nt,),
        in_specs=[
            pl.BlockSpec((tm, 1024), lambda i: (i, 1)),
            pl.BlockSpec((CONV_HALO, 1024), lambda i: (jnp.maximum(i * hb - 1, 0), 1)),
            pl.BlockSpec((tm, B_WIDTH), lambda i: (i, 0)),
            pl.BlockSpec((CONV_HALO, B_WIDTH), lambda i: (jnp.minimum((i + 1) * hb, last_halo), 0)),
            pl.BlockSpec((CONV_HALO, B_WIDTH), lambda i: (0, 0)),
        ],
        out_specs=[pl.BlockSpec((tm, 1024), lambda i: (i, 0)), pl.BlockSpec((CONV_HALO, B_WIDTH), lambda i: (0, 0))],
        out_shape=[jax.ShapeDtypeStruct((S, 1024), BF16), jax.ShapeDtypeStruct((CONV_HALO, B_WIDTH), F32)],
        scratch_shapes=[pltpu.VMEM((tm + CONV_HALO, B_WIDTH), F32), pltpu.VMEM((tm + CONV_HALO, B_WIDTH), F32)],
        compiler_params=_params(("arbitrary",)),
    )(proj, proj, dhc, dhc, cw)


_CA_SCALE = CA_HEAD_DIM ** -0.5


def _softmax_rows(s):
    e = jnp.exp(s - jnp.max(s, axis=-1, keepdims=True))
    return e / jnp.sum(e, axis=-1, keepdims=True)


def _attn_fwd(q, k, v, *, name):
    S = q.shape[0]

    def body(r, f, o, acc, s):
        for h in range(CA_HEADS):
            cols = _cols(h, CA_HEAD_DIM)
            p = _softmax_rows(_dot_nt(r[0][:, cols], f[0][:, cols]) * _CA_SCALE)
            o[0][:, cols] = _dot(p.astype(BF16), f[1][:, cols]).astype(BF16)

    (o_,), _ = _rows_call(name, _tile(S, 512), [q], [k, v], [(D_MODEL, BF16)], [], body)
    return o_


def _attn_bwd(dy, wo, q, k, v, *, name):
    S = q.shape[0]
    M = k.shape[0]

    def body(r, f, o, acc, s):
        dyb = r[0][...].astype(BF16)
        for h in range(CA_HEADS):
            cols = _cols(h, CA_HEAD_DIM)
            qh = r[1][:, cols]
            kh = f[0][:, cols]
            vh = f[1][:, cols]
            doh = _dot_nt(dyb, f[2 + h][...]).astype(BF16)
            p = _softmax_rows(_dot_nt(qh, kh) * _CA_SCALE)
            acc[1][:, cols] += _dot_tn(p.astype(BF16), doh)
            dp = _dot_nt(doh, vh)
            ds = (p * (dp - jnp.sum(dp * p, axis=-1, keepdims=True)) * _CA_SCALE).astype(BF16)
            o[0][:, cols] = _dot(ds, kh).astype(BF16)
            acc[0][:, cols] += _dot_tn(ds, qh)

    (dq,), (dk, dv) = _rows_call(name, _tile(S, 512), [dy, q], [k, v] + wo, [(D_MODEL, BF16)],
                                 [((M, D_MODEL), F32), ((M, D_MODEL), F32)], body)
    return dq, dk, dv


def _s5_readout(xs2, cd, u, d, *, name, tm=256):
    def body(r, f, o, acc, s):
        y = _dot(r[0][...].astype(BF16), f[0][...]) + f[1][...] * r[1][...]
        o[0][...] = y
        o[1][...] = _gelu(y)[0].astype(BF16)

    (y, yg), _ = _rows_call(name, _tile(xs2.shape[0], tm), [xs2, u], [cd, d], [(C_WIDTH, F32), (C_WIDTH, BF16)], [],
                            body)
    return y, yg


def _glu_out(yg, ws, x, *, name, tm=512):
    n = ws[0][1][2]

    def body(r, f, o, acc, s):
        ygv = r[0][...]
        ov = [_dot(ygv, f[p][...]) for p in range(N_CHIPS)]
        for p in range(N_CHIPS):
            o[0][:, _cols(p, n)] = ov[p].astype(BF16)
        for p in range(2):
            o[1][:, _cols(p, n)] = r[1][:, _cols(p, n)] + ov[p] * _sigmoid(ov[2 + p])

    (o_, y), _ = _rows_call(name, _tile(x.shape[0], tm), [yg, x], ws, [(2 * D_MODEL, BF16), (D_MODEL, F32)], [], body)
    return o_, y


def _glu_out_bwd(o_, dy, ws, y, u, d, *, name, tm=256):
    n = ws[0][1][2]

    def body(r, f, o, acc, s):
        o1 = r[0][:, 0:D_MODEL].astype(F32)
        sg = _sigmoid(r[0][:, D_MODEL:2 * D_MODEL].astype(F32))
        dyv = r[1][...]
        do1 = (dyv * sg).astype(BF16)
        do2 = (dyv * o1 * sg * (1.0 - sg)).astype(BF16)
        o[0][:, 0:D_MODEL] = do1
        o[0][:, D_MODEL:2 * D_MODEL] = do2
        dyg = None
        for p in range(N_CHIPS):
            t = _dot_nt((do1 if p < 2 else do2)[:, _cols(p % 2, n)], f[1 + p][...])
            dyg = t if dyg is None else dyg + t
        yv = r[2][...]
        dys = dyg * _gelu_grad(yv, _gelu(yv)[1])
        o[1][...] = dys.astype(BF16)
        o[2][...] = f[0][...] * dys
        acc[0][...] += jnp.sum(dys * r[3][...], axis=0, keepdims=True)

    (do, dys, dus), (dd,) = _rows_call(name, _tile(dy.shape[0], tm), [o_, dy, y, u], [d] + ws,
                                       [(2 * D_MODEL, BF16), (C_WIDTH, BF16), (C_WIDTH, F32)], [((1, C_WIDTH), F32)],
                                       body)
    return do, dys, dus, dd


def _s5_in_bwd(gs2, bd, dus, ws, x, g, dres, *, name, tm=256):
    D = x.shape[1]

    def body(r, f, o, acc, s):
        du = (_dot_nt(r[0][...].astype(BF16), f[1][...]) + r[1][...]).astype(BF16)
        o[0][...] = du
        dx, dg = _rms_bwd_tile(r[2][...], f[0][...], _cat_nt(du, f[2:]))
        o[1][...] = dx + r[3][...]
        acc[0][...] += dg

    (du, dx), (dg,) = _rows_call(name, _tile(x.shape[0], tm), [gs2, dus, x, dres], [_vec(g), bd] + ws,
                                 [(C_WIDTH, BF16), (D, F32)], [((1, D), F32)], body)
    return du, dx, dg


_SCAN_CHUNK = 128
_SCAN_UNROLL = 8
_RE = slice(0, STATE_ROWS)
_IM = slice(STATE_ROWS, 2 * STATE_ROWS)


def _scan_fwd(bu, a, *, name):
    S = bu.shape[0]
    tc = _tile(S, _SCAN_CHUNK, 8)

    def body(bu_ref, a_ref, xs_ref, st_ref):
        @pl.when(pl.program_id(0) == 0)
        def _():
            st_ref[...] = jnp.zeros_like(st_ref)

        ar = a_ref[_RE, :]
        ai = a_ref[_IM, :]

        def step(t, carry):
            xr, xi = carry
            nr = ar * xr - ai * xi + bu_ref[t, _RE, :]
            ni = ar * xi + ai * xr + bu_ref[t, _IM, :]
            xs_ref[t, _RE, :] = nr
            xs_ref[t, _IM, :] = ni
            return nr, ni

        xr, xi = lax.fori_loop(0, tc, step, (st_ref[_RE, :], st_ref[_IM, :]), unroll=_SCAN_UNROLL)
        st_ref[_RE, :] = xr
        st_ref[_IM, :] = xi

    blk = pl.BlockSpec((tc, 2 * STATE_ROWS, STATE_LANES), lambda i: (i, 0, 0))
    return pl.pallas_call(
        body, name=name, grid=(S // tc,),
        in_specs=[blk, pl.BlockSpec((2 * STATE_ROWS, STATE_LANES), lambda i: (0, 0))], out_specs=blk,
        out_shape=jax.ShapeDtypeStruct(bu.shape, F32),
        scratch_shapes=[pltpu.VMEM((2 * STATE_ROWS, STATE_LANES), F32)],
        compiler_params=_params(("arbitrary",)),
    )(bu, a)


def _scan_bwd(dxs, xs, a, *, name):
    S = dxs.shape[0]
    tc = _tile(S, _SCAN_CHUNK, 8)
    nc = S // tc

    def body(dx_ref, xs_ref, a_ref, g_ref, da_ref, st_ref):
        @pl.when(pl.program_id(0) == 0)
        def _():
            st_ref[...] = jnp.zeros_like(st_ref)
            da_ref[...] = jnp.zeros_like(da_ref)

        ar = a_ref[_RE, :]
        ai = a_ref[_IM, :]

        def step(j, carry):
            gr, gi, dar, dai = carry
            t = tc - 1 - j
            xr = xs_ref[t, _RE, :]
            xi = xs_ref[t, _IM, :]
            dar = dar + gr * xr + gi * xi
            dai = dai + gi * xr - gr * xi
            nr = dx_ref[t, _RE, :] + ar * gr + ai * gi
            ni = dx_ref[t, _IM, :] + ar * gi - ai * gr
            g_ref[t, _RE, :] = nr
            g_ref[t, _IM, :] = ni
            return nr, ni, dar, dai

        init = (st_ref[_RE, :], st_ref[_IM, :], da_ref[_RE, :], da_ref[_IM, :])
        gr, gi, dar, dai = lax.fori_loop(0, tc, step, init, unroll=_SCAN_UNROLL)
        st_ref[_RE, :] = gr
        st_ref[_IM, :] = gi
        da_ref[_RE, :] = dar
        da_ref[_IM, :] = dai

    blk = pl.BlockSpec((tc, 2 * STATE_ROWS, STATE_LANES), lambda i: (nc - 1 - i, 0, 0))
    vec = pl.BlockSpec((2 * STATE_ROWS, STATE_LANES), lambda i: (0, 0))
    return pl.pallas_call(
        body, name=name, grid=(nc,), in_specs=[blk, blk, vec], out_specs=[blk, vec],
        out_shape=[jax.ShapeDtypeStruct(dxs.shape, F32), jax.ShapeDtypeStruct((2 * STATE_ROWS, STATE_LANES), F32)],
        scratch_shapes=[pltpu.VMEM((2 * STATE_ROWS, STATE_LANES), F32)],
        compiler_params=_params(("arbitrary",)),
    )(dxs, xs, a)


def _loss_head(x, g, target, *, name):
    S, D = x.shape

    def body(r, f, o, acc, s):
        xv = r[0][...]
        gv = f[0][...]
        rs = lax.rsqrt(_mean(xv * xv) + EPS)
        xh = xv * rs
        err = xh * gv - r[1][...]
        acc[1][...] += 0.5 * jnp.sum(_mean(err * err), axis=0, keepdims=True)
        dy = err * (1.0 / D)
        dyg = dy * gv
        o[0][...] = rs * (dyg - xh * _mean(dyg * xh))
        acc[0][...] += jnp.sum(dy * xh, axis=0, keepdims=True)

    (dx,), (dg, loss) = _rows_call(name, _tile(S, 256, 8), [x, target], [_vec(g)], [(D, F32)],
                                   [((1, D), F32), ((1, 128), F32)], body)
    return dx, dg, loss


_ADAM_C1 = 1.0 - ADAM_B1 ** ADAM_STEP
_ADAM_C2 = 1.0 - ADAM_B2 ** ADAM_STEP
_ONE_BLOCK_BYTES = 8 * 1024 * 1024


def _adamw_math(w, g, m, v):
    nm = ADAM_B1 * m + (1.0 - ADAM_B1) * g
    nv = ADAM_B2 * v + (1.0 - ADAM_B2) * (g * g)
    m_hat = nm / _ADAM_C1
    v_hat = nv / _ADAM_C2
    return -ADAM_LR * (m_hat / (jnp.sqrt(v_hat) + ADAM_EPS) + ADAM_WD * w), nm, nv


def _adamw_shard(w, gsum, r0, m, v, *, name):
    R, C = w.shape
    tr = math.gcd(r0, R)
    tr = _tile(tr, 256, 8) if tr > 256 else tr
    assert r0 % tr == 0 and R % tr == 0

    def body(w_ref, g_ref, m_ref, v_ref, go_ref, d_ref, nm_ref, nv_ref):
        gv = g_ref[...]
        go_ref[...] = gv
        d_ref[...], nm_ref[...], nv_ref[...] = _adamw_math(w_ref[...], gv, m_ref[...], v_ref[...])

    blk = pl.BlockSpec((tr, C), lambda i: (i, 0))
    out = jax.ShapeDtypeStruct((R, C), F32)
    return pl.pallas_call(
        body, name=name, grid=(R // tr,),
        in_specs=[blk, pl.BlockSpec((tr, C), lambda i: (r0 // tr + i, 0)), blk, blk], out_specs=[blk] * 4,
        out_shape=[out] * 4, compiler_params=_params(("parallel",)),
    )(w, gsum, m, v)


def _adamw_small(ws, gs, ms, vs, *, name):
    n = len(ws)

    def body(*refs):
        w_r, g_r, m_r, v_r = refs[:n], refs[n:2 * n], refs[2 * n:3 * n], refs[3 * n:4 * n]
        d_r, nm_r, nv_r = refs[4 * n:5 * n], refs[5 * n:6 * n], refs[6 * n:7 * n]
        for k in range(n):
            d_r[k][...], nm_r[k][...], nv_r[k][...] = _adamw_math(w_r[k][...], g_r[k][...], m_r[k][...], v_r[k][...])

    vm = pl.BlockSpec(memory_space=pltpu.VMEM)
    out = [jax.ShapeDtypeStruct(w.shape, F32) for w in ws]
    res = pl.pallas_call(body, name=name, in_specs=[vm] * (4 * n), out_specs=[vm] * (3 * n), out_shape=out * 3,
                         compiler_params=pltpu.CompilerParams(vmem_limit_bytes=VMEM_LIMIT))(*ws, *gs, *ms, *vs)
    return res[:n], res[n:2 * n], res[2 * n:]


def _sum_slots(x, *, name):
    n, R, C = x.shape
    tr = R if (n + 1) * R * C * 4 <= _ONE_BLOCK_BYTES else _tile(R, 256, 8)

    def body(x_ref, o_ref):
        acc = x_ref[0]
        for k in range(1, n):
            acc = acc + x_ref[k]
        o_ref[...] = acc

    return pl.pallas_call(
        body, name=name, grid=(R // tr,),
        in_specs=[pl.BlockSpec((n, tr, C), lambda i: (0, i, 0))], out_specs=pl.BlockSpec((tr, C), lambda i: (i, 0)),
        out_shape=jax.ShapeDtypeStruct((R, C), F32), compiler_params=_params(("parallel",)),
    )(x)


def _pair_sum(g, r, half, *, name):
    n, R, C = g.shape
    Rh = R // 2
    tr = _tile(Rh, 256, 8)
    nb = Rh // tr

    def body(half_ref, g_ref, r_ref, o_ref):
        o_ref[...] = (g_ref[...] + r_ref[...]).astype(BF16)

    return pl.pallas_call(
        body, name=name,
        grid_spec=pltpu.PrefetchScalarGridSpec(
            num_scalar_prefetch=1, grid=(n, nb),
            in_specs=[pl.BlockSpec((1, tr, C), lambda p, i, h: (p, h[0] * nb + i, 0)),
                      pl.BlockSpec((1, tr, C), lambda p, i, h: (p, i, 0))],
            out_specs=pl.BlockSpec((1, tr, C), lambda p, i, h: (p, i, 0)),
        ),
        out_shape=jax.ShapeDtypeStruct((n, Rh, C), BF16), compiler_params=_params(("parallel", "parallel")),
    )(half, g, r)


def _chip_sum(g, r, slots, where, *, name):
    n, R, C = g.shape
    Rh = R // 2
    tr = _tile(Rh, 256, 8)
    nb = Rh // tr

    def body(w_ref, g_ref, r_ref, s_ref, o_ref):
        acc = g_ref[0] + r_ref[0]
        for k in range(slots.shape[0]):
            acc = acc + s_ref[k].astype(F32)
        o_ref[...] = acc

    return pl.pallas_call(
        body, name=name,
        grid_spec=pltpu.PrefetchScalarGridSpec(
            num_scalar_prefetch=1, grid=(nb,),
            in_specs=[pl.BlockSpec((1, tr, C), lambda i, w: (w[0], w[1] * nb + i, 0)),
                      pl.BlockSpec((1, tr, C), lambda i, w: (w[0], i, 0)),
                      pl.BlockSpec((slots.shape[0], tr, C), lambda i, w: (0, i, 0))],
            out_specs=pl.BlockSpec((tr, C), lambda i, w: (w[1] * nb + i, 0)),
        ),
        out_shape=jax.ShapeDtypeStruct((R, C), F32), compiler_params=_params(("parallel",)),
    )(where, g, r, slots)


ANY = pl.BlockSpec(memory_space=pl.ANY)


def _place():
    return lax.axis_index("x"), lax.axis_index("y"), lax.axis_index("c")


def _other_chips(x, y):
    return [(1 - x, y), (x, 1 - y), (1 - x, 1 - y)]


def _allgather_small(v, *, name):
    R, C = v.shape

    def body(x_ref, out_ref, send_sems, recv_sems, local_sem):
        x, y, c = _place()
        me, sibling = (x, y, c), (x, y, 1 - c)
        chips = _other_chips(x, y)

        def rows(px, py, pc):
            return out_ref.at[pl.ds((4 * px + 2 * py + pc) * R, R), :]

        def copy(k, block, to, src=None):
            return pltpu.make_async_remote_copy(
                src_ref=rows(*block) if src is None else src, dst_ref=rows(*block),
                send_sem=send_sems.at[k], recv_sem=recv_sems.at[k], device_id=to, device_id_type=MESH)

        mine = pltpu.make_async_copy(x_ref, rows(*me), local_sem)
        mine.start()
        first = [copy(0, me, sibling, src=x_ref)]
        first += [copy(1 + j, me, (*chip, c), src=x_ref) for j, chip in enumerate(chips)]
        for cp in first:
            cp.start()
        passed = [copy(4 + j, (*chip, c), sibling) for j, chip in enumerate(chips)]
        for j, chip in enumerate(chips):
            copy(1 + j, (*chip, c), me).wait_recv()
            passed[j].start()
        copy(0, sibling, me).wait_recv()
        for j, chip in enumerate(chips):
            copy(4 + j, (*chip, 1 - c), me).wait_recv()
        for cp in first + passed:
            cp.wait_send()
        mine.wait()

    return pl.pallas_call(
        body, name=name, out_shape=jax.ShapeDtypeStruct((N_DEV * R, C), v.dtype),
        in_specs=[pl.BlockSpec(memory_space=pltpu.VMEM)], out_specs=pl.BlockSpec(memory_space=pltpu.VMEM),
        scratch_shapes=[pltpu.SemaphoreType.DMA((7,)), pltpu.SemaphoreType.DMA((7,)), pltpu.SemaphoreType.DMA],
        compiler_params=pltpu.CompilerParams(vmem_limit_bytes=VMEM_LIMIT),
    )(v)


def _aliased_comm_call(body, bufs, n_sems, *, name):
    n = len(bufs)
    return pl.pallas_call(
        body, name=name, out_shape=[jax.ShapeDtypeStruct(b.shape, b.dtype) for b in bufs],
        in_specs=[ANY] * n, out_specs=[ANY] * n, input_output_aliases={k: k for k in range(n)},
        scratch_shapes=[pltpu.SemaphoreType.DMA((n_sems,)), pltpu.SemaphoreType.DMA((n_sems,))],
    )(*bufs)


def _allgather_chips(bufs, *, name):
    n = len(bufs)

    def body(*refs):
        outs, send_sems, recv_sems = refs[n:2 * n], refs[2 * n], refs[2 * n + 1]
        x, y, c = _place()
        chips = _other_chips(x, y)

        def copy(b, j, chip, hc, to):
            rh = bufs[b].shape[1] // 2
            part = outs[b].at[2 * chip[0] + chip[1], pl.ds(hc * rh, rh), :]
            return pltpu.make_async_remote_copy(src_ref=part, dst_ref=part, send_sem=send_sems.at[6 * b + j],
                                                recv_sem=recv_sems.at[6 * b + j], device_id=to, device_id_type=MESH)

        first = [copy(b, j, (x, y), c, (*chip, c)) for b in range(n) for j, chip in enumerate(chips)]
        for cp in first:
            cp.start()
        passed = []
        for b in range(n):
            for j, chip in enumerate(chips):
                copy(b, j, chip, c, (x, y, c)).wait_recv()
                passed.append(copy(b, 3 + j, chip, c, (x, y, 1 - c)))
                passed[-1].start()
        for b in range(n):
            for j, chip in enumerate(chips):
                copy(b, 3 + j, chip, 1 - c, (x, y, c)).wait_recv()
        for cp in first + passed:
            cp.wait_send()

    return _aliased_comm_call(body, bufs, 6 * n, name=name)


def _pair_exchange(gs, *, name):
    n = len(gs)

    def body(*refs):
        ins, outs, send_sems, recv_sems = refs[:n], refs[n:2 * n], refs[2 * n], refs[2 * n + 1]
        x, y, c = _place()
        cps = []
        for b in range(n):
            rh = gs[b].shape[1] // 2
            cps.append(pltpu.make_async_remote_copy(
                src_ref=ins[b].at[:, pl.ds((1 - c) * rh, rh), :], dst_ref=outs[b], send_sem=send_sems.at[b],
                recv_sem=recv_sems.at[b], device_id=(x, y, 1 - c), device_id_type=MESH))
        for cp in cps:
            cp.start()
        for cp in cps:
            cp.wait()

    return pl.pallas_call(
        body, name=name, out_shape=[jax.ShapeDtypeStruct((g.shape[0], g.shape[1] // 2, g.shape[2]), g.dtype) for g in gs],
        in_specs=[ANY] * n, out_specs=[ANY] * n,
        scratch_shapes=[pltpu.SemaphoreType.DMA((n,)), pltpu.SemaphoreType.DMA((n,))],
    )(*gs)


def _chip_exchange(hs, *, name):
    n = len(hs)

    def body(*refs):
        ins, outs, send_sems, recv_sems = refs[:n], refs[n:2 * n], refs[2 * n], refs[2 * n + 1]
        x, y, c = _place()
        cps = [pltpu.make_async_remote_copy(
            src_ref=ins[b].at[2 * cx + cy], dst_ref=outs[b].at[j], send_sem=send_sems.at[3 * b + j],
            recv_sem=recv_sems.at[3 * b + j], device_id=(cx, cy, c), device_id_type=MESH)
            for b in range(n) for j, (cx, cy) in enumerate(_other_chips(x, y))]
        for cp in cps:
            cp.start()
        for cp in cps:
            cp.wait()

    return pl.pallas_call(
        body, name=name, out_shape=[jax.ShapeDtypeStruct((3,) + h.shape[1:], h.dtype) for h in hs],
        in_specs=[ANY] * n, out_specs=[ANY] * n,
        scratch_shapes=[pltpu.SemaphoreType.DMA((3 * n,)), pltpu.SemaphoreType.DMA((3 * n,))],
    )(*hs)


def _pair_share(ss, *, name):
    n = len(ss)

    def body(*refs):
        outs, send_sems, recv_sems = refs[n:2 * n], refs[2 * n], refs[2 * n + 1]
        x, y, c = _place()
        cps = []
        for b in range(n):
            rh = ss[b].shape[0] // 2
            mine = outs[b].at[pl.ds(c * rh, rh), :]
            cps.append(pltpu.make_async_remote_copy(src_ref=mine, dst_ref=mine, send_sem=send_sems.at[b],
                                                    recv_sem=recv_sems.at[b], device_id=(x, y, 1 - c),
                                                    device_id_type=MESH))
        for cp in cps:
            cp.start()
        for b, cp in enumerate(cps):
            rh = ss[b].shape[0] // 2
            theirs = outs[b].at[pl.ds((1 - c) * rh, rh), :]
            pltpu.make_async_remote_copy(src_ref=theirs, dst_ref=theirs, send_sem=send_sems.at[b],
                                         recv_sem=recv_sems.at[b], device_id=(x, y, 1 - c),
                                         device_id_type=MESH).wait_recv()
            cp.wait_send()

    return _aliased_comm_call(body, ss, n, name=name)


_SMALL_SHARDED = (("e_conv_w", 2), ("o_norm", 1), ("o_d", 1))
_REPLICATED = ("e_norm", "e_gmlp_w", "e_gmlp_b", "e_conv_b", "e_conv_ln_g", "e_conv_ln_b", "o_lam_re", "o_lam_im",
               "o_log_dt", "o_b_re", "o_b_im", "o_c_re", "o_c_im", "ca_norm", "ca_mem_norm", "ffn_norm", "final_norm")
_SMALL = tuple(n for n, _ in _SMALL_SHARDED) + _REPLICATED
_WEIGHTS = ("e_norm", "e_w_in", "e_gmlp_w", "e_gmlp_b", "e_conv_w", "e_conv_b", "e_conv_ln_g", "e_conv_ln_b",
            "e_w_out", "o_norm", "o_w_in", "o_lam_re", "o_lam_im", "o_log_dt", "o_b_re", "o_b_im", "o_c_re", "o_c_im",
            "o_d", "o_w_out", "ca_norm", "ca_mem_norm", "ca_wq", "ca_wk", "ca_wv", "ca_wo", "ffn_norm", "ffn_w_gate",
            "ffn_w_up", "ffn_w_down", "final_norm")


def _pack_rows(arrs, width, dtype, row_mult=8):
    parts, spans, r0 = [], [], 0
    for a in arrs:
        flat = a.reshape(-1).astype(dtype)
        rows = -(-flat.shape[0] // (width * row_mult)) * row_mult
        if rows * width != flat.shape[0]:
            flat = jnp.pad(flat, (0, rows * width - flat.shape[0]))
        parts.append(flat.reshape(rows, width))
        spans.append((r0, rows))
        r0 += rows
    return jnp.concatenate(parts, axis=0), spans


def _unpack_rows(slab, spans, shapes):
    out = []
    for (r0, rows), shp in zip(spans, shapes):
        n = math.prod(shp)
        out.append(slab[r0:r0 + rows].reshape(-1)[:n].reshape(shp))
    return out


def _two_d(a):
    return a.reshape(-1, a.shape[-1])


def _local_slab(local, slab, dtype):
    names = sorted((n for n in _PLACE if _PLACE[n][0] == slab), key=lambda n: _PLACE[n][1])
    return jnp.concatenate([_two_d(local[n]).astype(dtype) for n in names], axis=0)


def _block_diag(b, pattern):
    return jnp.einsum(pattern, b, jnp.eye(C_GROUPS, dtype=b.dtype))


def _s5_discretize(lam_re, lam_im, log_dt, b_re, b_im):
    dt = jnp.exp(log_dt)[:, None]
    mag = jnp.exp(lam_re * dt)
    ar = mag * jnp.cos(lam_im * dt)
    ai = mag * jnp.sin(lam_im * dt)
    den = lam_re * lam_re + lam_im * lam_im
    qr = ((ar - 1.0) * lam_re + ai * lam_im) / den
    qi = (ai * lam_re - (ar - 1.0) * lam_im) / den
    bbr = qr[..., None] * b_re - qi[..., None] * b_im
    bbi = qr[..., None] * b_im + qi[..., None] * b_re
    return ar, ai, bbr, bbi


def _attention_block(x, mem, W, w, i, tag):
    xn, q = _norm_mm(x, w["ca_norm"][i], _shards(W, "ca_wq", i), split="k", out_dtype=BF16, name=f"{tag}_q")
    memn = _rms_fwd(mem, w["ca_mem_norm"][i], name=f"{tag}_ca_memnorm")
    k = _mm_k(memn, _shards(W, "ca_wk", i), out_dtype=BF16, name=f"{tag}_k")
    v = _mm_k(memn, _shards(W, "ca_wv", i), out_dtype=BF16, name=f"{tag}_v")
    o = _attn_fwd(q, k, v, name=f"{tag}_attn")
    y = _mm_k(o, _shards(W, "ca_wo", i), add=x, name=f"{tag}_wo")
    return y, (x, xn, memn, q, k, v, o)


def _attention_block_bwd(dy, saved, mem, W, w, i, tag, G, grads):
    x, xn, memn, q, k, v, o = saved
    G = _grad_to_slab(G, "ca_wo", i, o, dy, a_cols=256, name=f"{tag}_dwo")
    dq, dk, dv = _attn_bwd(dy, _shards(W, "ca_wo", i), q, k, v, name=f"{tag}_attn_bwd")
    G = _grad_to_slab(G, "ca_wq", i, xn, dq, a_cols=256, name=f"{tag}_dwq")
    G = _grad_to_slab(G, "ca_wk", i, memn, dk, a_cols=256, name=f"{tag}_dwk")
    G = _grad_to_slab(G, "ca_wv", i, memn, dv, a_cols=256, name=f"{tag}_dwv")
    dmemn = _mm_k_t([(dk, _shards(W, "ca_wk", i)), (dv, _shards(W, "ca_wv", i))], name=f"{tag}_dmemn")
    dx, dg = _norm_bwd_k(dq, _shards(W, "ca_wq", i), x, w["ca_norm"][i], dy, name=f"{tag}_dq_norm_bwd")
    grads["ca_norm"][i] = dg[0]
    grads["ca_mem_norm"][i] = _rms_dg(mem, w["ca_mem_norm"][i], dmemn, name=f"{tag}_ca_memnorm_bwd")[0]
    return dx, G


def _ffn_block(x, W, w, i, tag):
    fn, gate, up, h = _ffn_up(x, w["ffn_norm"][i], _shards(W, "ffn_w_gate", i), _shards(W, "ffn_w_up", i),
                              name=f"{tag}_ffn_up")
    y = _mm_k(h, _shards(W, "ffn_w_down", i), add=x, name=f"{tag}_down")
    return y, (x, fn, gate, up, h)


def _ffn_block_bwd(dy, saved, W, w, i, tag, G, grads):
    x, fn, gate, up, h = saved
    G = _grad_to_slab(G, "ffn_w_down", i, h, dy, name=f"{tag}_dwd")
    dg, du = _ffn_bwd_hidden(dy, _shards(W, "ffn_w_down", i), gate, up, name=f"{tag}_ffn_bwd_hidden")
    G = _grad_to_slab(G, "ffn_w_gate", i, fn, dg, name=f"{tag}_dwg")
    G = _grad_to_slab(G, "ffn_w_up", i, fn, du, name=f"{tag}_dwu")
    dx, dgn = _ffn_in_bwd(dg, du, _shards(W, "ffn_w_gate", i), _shards(W, "ffn_w_up", i), x, w["ffn_norm"][i], dy,
                          name=f"{tag}_ffn_in_bwd")
    grads["ffn_norm"][i] = dgn[0]
    return dx, G


def _gmlp_mask():
    chunk = jnp.arange(GMLP_BLOCK) // CHUNK
    return chunk[None, :] <= chunk[:, None]


def _even_block(x, W, w, tag):
    hn, proj = _norm_mm(x, w["e_norm"][0], _shards(W, "e_w_in"), split="n", out_dtype=F32, name=f"{tag}_w_in")
    wm = jnp.where(_gmlp_mask()[None], w["e_gmlp_w"][0], 0.0).astype(BF16)
    bcol = w["e_gmlp_b"][0][:, :, None]
    cw = jnp.pad(w["e_conv_w"][0], ((0, CONV_HALO - CONV_WIDTH), (0, 0)))
    cb, lg, lb = w["e_conv_b"], w["e_conv_ln_g"], w["e_conv_ln_b"]
    mix, hc = _even_fwd(proj, wm, bcol, cw, cb, lg, lb, name=f"{tag}_mixers")
    y = _mm_k(mix, _shards(W, "e_w_out"), add=x, name=f"{tag}_w_out")
    return y, (x, hn, proj, mix, hc, wm, bcol, cw)


def _even_block_bwd(dy, saved, W, w, tag, G, grads):
    x, hn, proj, mix, hc, wm, bcol, cw = saved
    dmix = _mm_k_t([(dy, _shards(W, "e_w_out"))], name=f"{tag}_dmix")
    G = _grad_to_slab(G, "e_w_out", 0, mix, dy, a_cols=256, name=f"{tag}_dw_out")
    wmt = jnp.swapaxes(wm, 1, 2)
    dpa, dhc, dwm, db, dlg, dlb, dcb = _even_bwd1(proj, dmix, hc, wm, wmt, bcol, w["e_conv_ln_g"], w["e_conv_ln_b"],
                                                  name=f"{tag}_mixers_bwd1")
    dpb, dcw = _even_bwd2(proj, dhc, cw, name=f"{tag}_mixers_bwd2")
    grads["e_gmlp_w"] = jnp.where(_gmlp_mask()[None], dwm, 0.0)[None]
    grads["e_gmlp_b"] = db[:, :, 0][None]
    grads["e_conv_ln_g"], grads["e_conv_ln_b"], grads["e_conv_b"] = dlg, dlb, dcb
    grads["e_conv_w"] = dcw[:CONV_WIDTH][None]
    G = _grad_to_slab(G, "e_w_in", 0, hn, dpa, b_cols=512, chips=(0, 2), name=f"{tag}_dw_in_a")
    G = _grad_to_slab(G, "e_w_in", 0, hn, dpb, b_cols=512, chips=(2, 2), name=f"{tag}_dw_in_b")
    dx, dg = _norm_bwd_n((dpa, dpb), _shards(W, "e_w_in"), x, w["e_norm"][0], dy, name=f"{tag}_in_bwd")
    grads["e_norm"] = dg
    return dx, G


def _odd_block(x, W, w, tag):
    S = x.shape[0]
    hn, u = _norm_mm(x, w["o_norm"][0], _shards(W, "o_w_in"), split="k", out_dtype=F32, name=f"{tag}_w_in")
    disc_in = (w["o_lam_re"][0], w["o_lam_im"][0], w["o_log_dt"][0], w["o_b_re"][0], w["o_b_im"][0])
    (ar, ai, bbr, bbi), disc_vjp = jax.vjp(_s5_discretize, *disc_in)
    bd = jnp.concatenate([_block_diag(bbr, "gpc,gh->gchp").reshape(C_WIDTH, N_STATE),
                          _block_diag(bbi, "gpc,gh->gchp").reshape(C_WIDTH, N_STATE)], axis=1).astype(BF16)
    cd = jnp.concatenate([_block_diag(w["o_c_re"][0], "gcp,gh->gphc").reshape(N_STATE, C_WIDTH),
                          -_block_diag(w["o_c_im"][0], "gcp,gh->gphc").reshape(N_STATE, C_WIDTH)], axis=0).astype(BF16)
    a = jnp.concatenate([ar.reshape(STATE_ROWS, STATE_LANES), ai.reshape(STATE_ROWS, STATE_LANES)], axis=0)
    bu = _mm_rows(u, bd, name=f"{tag}_bu")
    xs = _scan_fwd(bu.reshape(S, 2 * STATE_ROWS, STATE_LANES), a, name=f"{tag}_scan")
    yv, yg = _s5_readout(xs.reshape(S, 2 * N_STATE), cd, u, w["o_d"], name=f"{tag}_readout")
    o, y = _glu_out(yg, _shards(W, "o_w_out"), x, name=f"{tag}_glu_out")
    return y, (x, hn, u, bd, cd, a, xs, yv, yg, o, disc_vjp)


def _odd_block_bwd(dy, saved, W, w, tag, G, grads):
    x, hn, u, bd, cd, a, xs, yv, yg, o, disc_vjp = saved
    S = x.shape[0]
    do, dys, dus, dd = _glu_out_bwd(o, dy, _shards(W, "o_w_out"), yv, u, w["o_d"], name=f"{tag}_glu_out_bwd")
    G = _grad_to_slab(G, "o_w_out", 0, yg, do, b_cols=512, name=f"{tag}_dw_out")
    grads["o_d"] = dd
    xs2 = xs.reshape(S, 2 * N_STATE)
    dxs = _mm_rows(dys, cd, nt=True, name=f"{tag}_dxs")
    dcd = _mm_tn(xs2, dys, name=f"{tag}_dcd")
    gs, da = _scan_bwd(dxs.reshape(S, 2 * STATE_ROWS, STATE_LANES), xs, a, name=f"{tag}_scan_bwd")
    gs2 = gs.reshape(S, 2 * N_STATE)
    dbd = _mm_tn(u, gs2, name=f"{tag}_dbd")
    du, dx, dg = _s5_in_bwd(gs2, bd, dus, _shards(W, "o_w_in"), x, w["o_norm"][0], dy, name=f"{tag}_in_bwd")
    G = _grad_to_slab(G, "o_w_in", 0, hn, du, a_cols=256, name=f"{tag}_dw_in")
    grads["o_norm"] = dg
    eye = jnp.eye(C_GROUPS, dtype=F32)
    dcr = jnp.einsum("gphc,gh->gcp", dcd[:N_STATE].reshape(C_GROUPS, C_STATE, C_GROUPS, C_GROUP_CH), eye)
    dci = -jnp.einsum("gphc,gh->gcp", dcd[N_STATE:].reshape(C_GROUPS, C_STATE, C_GROUPS, C_GROUP_CH), eye)
    dbbr = jnp.einsum("gchp,gh->gpc", dbd[:, :N_STATE].reshape(C_GROUPS, C_GROUP_CH, C_GROUPS, C_STATE), eye)
    dbbi = jnp.einsum("gchp,gh->gpc", dbd[:, N_STATE:].reshape(C_GROUPS, C_GROUP_CH, C_GROUPS, C_STATE), eye)
    dar = da[:STATE_ROWS].reshape(C_GROUPS, C_STATE)
    dai = da[STATE_ROWS:].reshape(C_GROUPS, C_STATE)
    dlr, dli, dldt, dbr, dbi = disc_vjp((dar, dai, dbbr, dbbi))
    grads["o_lam_re"], grads["o_lam_im"], grads["o_log_dt"] = dlr[None], dli[None], dldt[None]
    grads["o_b_re"], grads["o_b_im"], grads["o_c_re"], grads["o_c_im"] = dbr[None], dbi[None], dcr[None], dci[None]
    return dx, G


def _forward_backward(xs_, mems_, tgt, W, w, G):
    x1, s_mix0 = _even_block(xs_, W, w, "l0")
    x2, s_att0 = _attention_block(x1, mems_, W, w, 0, "l0")
    x3, s_ffn0 = _ffn_block(x2, W, w, 0, "l0")
    x4, s_mix1 = _odd_block(x3, W, w, "l1")
    x5, s_att1 = _attention_block(x4, mems_, W, w, 1, "l1")
    x6, s_ffn1 = _ffn_block(x5, W, w, 1, "l1")
    dx, dfinal, loss_lanes = _loss_head(x6, w["final_norm"], tgt, name="loss_head")

    grads = {n: [None, None] for n in ("ca_norm", "ca_mem_norm", "ffn_norm")}
    grads["final_norm"] = dfinal[0]
    dx, G = _ffn_block_bwd(dx, s_ffn1, W, w, 1, "l1", G, grads)
    dx, G = _attention_block_bwd(dx, s_att1, mems_, W, w, 1, "l1", G, grads)
    dx, G = _odd_block_bwd(dx, s_mix1, W, w, "l1", G, grads)
    dx, G = _ffn_block_bwd(dx, s_ffn0, W, w, 0, "l0", G, grads)
    dx, G = _attention_block_bwd(dx, s_att0, mems_, W, w, 0, "l0", G, grads)
    dx, G = _even_block_bwd(dx, s_mix0, W, w, "l0", G, grads)
    for n in list(grads):
        if isinstance(grads[n], list):
            grads[n] = jnp.stack(grads[n], axis=0)
        grads[n] = grads[n].reshape(w[n].shape)
    return loss_lanes, dx, G, grads


def kernel(x, mem, e_norm, e_w_in, e_gmlp_w, e_gmlp_b, e_conv_w, e_conv_b, e_conv_ln_g, e_conv_ln_b, e_w_out, o_norm, o_w_in, o_lam_re, o_lam_im, o_log_dt, o_b_re, o_b_im, o_c_re, o_c_im, o_d, o_w_out, ca_norm, ca_mem_norm, ca_wq, ca_wk, ca_wv, ca_wo, ffn_norm, ffn_w_gate, ffn_w_up, ffn_w_down, final_norm, loss_target, m_e_norm, m_e_w_in, m_e_gmlp_w, m_e_gmlp_b, m_e_conv_w, m_e_conv_b, m_e_conv_ln_g, m_e_conv_ln_b, m_e_w_out, m_o_norm, m_o_w_in, m_o_lam_re, m_o_lam_im, m_o_log_dt, m_o_b_re, m_o_b_im, m_o_c_re, m_o_c_im, m_o_d, m_o_w_out, m_ca_norm, m_ca_mem_norm, m_ca_wq, m_ca_wk, m_ca_wv, m_ca_wo, m_ffn_norm, m_ffn_w_gate, m_ffn_w_up, m_ffn_w_down, m_final_norm, v_e_norm, v_e_w_in, v_e_gmlp_w, v_e_gmlp_b, v_e_conv_w, v_e_conv_b, v_e_conv_ln_g, v_e_conv_ln_b, v_e_w_out, v_o_norm, v_o_w_in, v_o_lam_re, v_o_lam_im, v_o_log_dt, v_o_b_re, v_o_b_im, v_o_c_re, v_o_c_im, v_o_d, v_o_w_out, v_ca_norm, v_ca_mem_norm, v_ca_wq, v_ca_wk, v_ca_wv, v_ca_wo, v_ffn_norm, v_ffn_w_gate, v_ffn_w_up, v_ffn_w_down, v_final_norm):
    args = dict(locals())
    local = {n: args[n] for n in _WEIGHTS}
    mom = {n: args["m_" + n] for n in _WEIGHTS}
    vel = {n: args["v_" + n] for n in _WEIGHTS}
    chip = 2 * lax.axis_index("x") + lax.axis_index("y")
    core = lax.axis_index("c")
    xs_, mems_, tgt = x[0], mem[0], loss_target[0]
    names = sorted(_SLABS)

    bufs = [lax.dynamic_update_slice(lax.empty((N_CHIPS, _SLABS[s][1], _SLABS[s][0]), BF16),
                                     _local_slab(local, s, BF16)[None], (chip, 0, 0)) for s in names]
    W = dict(zip(names, _allgather_chips(bufs, name="gather_weights")))
    w = {n: local[n] for n in _REPLICATED}
    sm_slab, sm_spans = _pack_rows([local[n] for n, _ in _SMALL_SHARDED], SMALL_W, F32)
    sm_all = _allgather_small(sm_slab, name="gather_small_weights").reshape(N_DEV, -1, SMALL_W)
    for (n, ax), span in zip(_SMALL_SHARDED, sm_spans):
        shp = local[n].shape
        w[n] = jnp.concatenate([_unpack_rows(sm_all[2 * p], [span], [shp])[0] for p in range(N_CHIPS)], axis=ax)

    G = {s: lax.empty((N_CHIPS, _SLABS[s][1], _SLABS[s][0]), F32) for s in names}
    loss_lanes, dx, G, grads = _forward_backward(xs_, mems_, tgt, W, w, G)

    gl = [G[s] for s in names]
    other = _pair_exchange(gl, name="grad_pair_exchange")
    half = core.reshape(1).astype(jnp.int32)
    pairs = [_pair_sum(g, r, half, name=f"grad_pair_sum_{s}") for s, g, r in zip(names, gl, other)]
    slots = _chip_exchange(pairs, name="grad_chip_exchange")
    where = jnp.stack([chip, core]).astype(jnp.int32)
    halves = [_chip_sum(g, r, sl, where, name=f"grad_chip_sum_{s}") for s, g, r, sl in zip(names, gl, other, slots)]
    gsum = dict(zip(names, _pair_share(halves, name="grad_pair_share")))

    gs_slab, gs_spans = _pack_rows([grads[n] for n in _SMALL], SMALL_W, F32)
    gs_all = _allgather_small(gs_slab, name="gather_small_grads").reshape(N_DEV, -1, SMALL_W)
    gs_sum = _sum_slots(gs_all, name="small_grad_sum")
    out_grads = dict(zip(_SMALL, _unpack_rows(gs_sum, gs_spans, [grads[n].shape for n in _SMALL])))
    for n, ax in _SMALL_SHARDED:
        width = local[n].shape[ax]
        out_grads[n] = lax.dynamic_slice_in_dim(out_grads[n], chip * width, width, axis=ax)

    delta, new_m, new_v = {}, {}, {}
    for n, (s, r0, rows, layers) in _PLACE.items():
        shp = local[n].shape
        g_, d_, m_, v_ = _adamw_shard(_two_d(local[n]), gsum[s], r0, _two_d(mom[n]), _two_d(vel[n]), name=f"adamw_{n}")
        out_grads[n], delta[n], new_m[n], new_v[n] = g_.reshape(shp), d_.reshape(shp), m_.reshape(shp), v_.reshape(shp)
    d_, m_, v_ = _adamw_small([_two_d(local[n]) for n in _SMALL], [_two_d(out_grads[n]) for n in _SMALL],
                              [_two_d(mom[n]) for n in _SMALL], [_two_d(vel[n]) for n in _SMALL], name="adamw_small")
    for n, dd, mm_, vv in zip(_SMALL, d_, m_, v_):
        shp = local[n].shape
        delta[n], new_m[n], new_v[n] = dd.reshape(shp), mm_.reshape(shp), vv.reshape(shp)

    loss = lax.psum(loss_lanes[0, 0], ("x", "y", "c"))
    return (loss, dx[None], *[out_grads[n] for n in _WEIGHTS], *[delta[n] for n in _WEIGHTS],
            *[new_m[n] for n in _WEIGHTS], *[new_v[n] for n in _WEIGHTS])
```

```python
import functools
import math

import jax
import jax.numpy as jnp
from jax import lax
from jax.experimental import pallas as pl
from jax.experimental.pallas import tpu as pltpu

F32 = jnp.float32
BF16 = jnp.bfloat16
MESH = pl.DeviceIdType.MESH

EPS = 1e-6
D_MODEL = 1024
A_WIDTH = 512
A_GROUPS = 4
GMLP_BLOCK = 128
CHUNK = 64
B_WIDTH = 512
CONV_WIDTH = 31
CONV_HALO = 32
C_WIDTH = 512
C_GROUP_CH = 16
C_GROUPS = 32
C_STATE = 64
N_STATE = C_GROUPS * C_STATE
STATE_ROWS = 8
STATE_LANES = N_STATE // STATE_ROWS
CA_HEADS = 4
CA_HEAD_DIM = 256
FFN_HIDDEN = 2816

ADAM_LR = 0.001
ADAM_B1 = 0.9
ADAM_B2 = 0.999
ADAM_EPS = 1e-08
ADAM_WD = 0.01
ADAM_STEP = 10

VMEM_LIMIT = 56 * 1024 * 1024
ACC_BYTES = 6 * 1024 * 1024
TN_VMEM_BYTES = 44 * 1024 * 1024
SMALL_W = 128
N_CHIPS = 4
N_DEV = 8

_SLABS = {"A": (1024, 2304), "B": (1024, 1408), "C": (704, 4096), "D": (512, 1792)}
_PLACE = {
    "ca_wq": ("A", 0, 256, 2), "ca_wk": ("A", 512, 256, 2), "ca_wv": ("A", 1024, 256, 2), "ca_wo": ("A", 1536, 256, 2),
    "e_w_out": ("A", 2048, 256, 1),
    "ffn_w_down": ("B", 0, 704, 2),
    "ffn_w_gate": ("C", 0, 1024, 2), "ffn_w_up": ("C", 2048, 1024, 2),
    "e_w_in": ("D", 0, 1024, 1), "o_w_out": ("D", 1024, 512, 1), "o_w_in": ("D", 1536, 256, 1),
}


def _params(sem=None):
    return pltpu.CompilerParams(dimension_semantics=sem, vmem_limit_bytes=VMEM_LIMIT)


def _tile(n, pref, mult=128):
    if n <= pref:
        return n
    t = (pref // mult) * mult
    while t >= mult:
        if n % t == 0:
            return t
        t -= mult
    return n


def _blk(name, layer=0):
    slab, r0, rows, layers = _PLACE[name]
    assert layer < layers and (r0 + layer * rows) % rows == 0
    return slab, rows, (r0 + layer * rows) // rows


def _shards(slabs, name, layer=0):
    slab, rows, b = _blk(name, layer)
    return [(slabs[slab], (None, rows, _SLABS[slab][0]), (p, b, 0)) for p in range(N_CHIPS)]


_GELU_C = 0.7978845608028654
_GELU_A = 0.044715


def _gelu(x):
    t = jnp.tanh(_GELU_C * (x + _GELU_A * (x * x * x)))
    return 0.5 * x * (1.0 + t), t


def _gelu_grad(x, t):
    return 0.5 * (1.0 + t) + 0.5 * x * (1.0 - t * t) * (_GELU_C * (1.0 + 3.0 * _GELU_A * x * x))


def _sigmoid(x):
    return 1.0 / (1.0 + jnp.exp(-x))


def _mean(x):
    return jnp.mean(x, axis=-1, keepdims=True)


def _dot(a, b):
    return jnp.dot(a, b, preferred_element_type=F32)


def _dot_nt(a, b):
    return lax.dot_general(a, b, (((1,), (1,)), ((), ())), preferred_element_type=F32)


def _dot_tn(a, b):
    return lax.dot_general(a, b, (((0,), (0,)), ((), ())), preferred_element_type=F32)


def _rms_tile(xv, gv):
    return (xv * lax.rsqrt(_mean(xv * xv) + EPS)) * gv


def _rms_bwd_tile(xv, gv, dyv):
    r = lax.rsqrt(_mean(xv * xv) + EPS)
    xh = xv * r
    dyg = dyv * gv
    return r * (dyg - xh * _mean(dyg * xh)), jnp.sum(dyv * xh, axis=0, keepdims=True)


def _cols(p, width):
    return slice(p * width, (p + 1) * width)


def _sum_k(a, ws, k):
    tot = None
    for p in range(N_CHIPS):
        y = _dot(a[:, _cols(p, k)], ws[p][...])
        tot = y if tot is None else tot + y
    return tot


def _cat_nt(a, ws):
    return jnp.concatenate([_dot_nt(a, ws[p][...]) for p in range(N_CHIPS)], axis=1)


def _rows_call(name, tm, rows, fulls, outs, accs, body, scratch=()):
    S = rows[0].shape[-2]
    nr, nf, no, na = len(rows), len(fulls), len(outs), len(accs)

    def kern(*refs):
        r, f = refs[:nr], refs[nr:nr + nf]
        o, a = refs[nr + nf:nr + nf + no], refs[nr + nf + no:nr + nf + no + na]
        if na:
            @pl.when(pl.program_id(0) == 0)
            def _():
                for ref in a:
                    ref[...] = jnp.zeros_like(ref)
        body(r, f, o, a, refs[nr + nf + no + na:])

    def whole(shape):
        nd = len(shape)
        return pl.BlockSpec(tuple(shape), lambda i: (0,) * nd)

    def row_spec(shape):
        if len(shape) == 3:
            return pl.BlockSpec((shape[0], tm, shape[2]), lambda i: (0, i, 0))
        return pl.BlockSpec((tm, shape[1]), lambda i: (i, 0))

    def full_spec(x):
        if isinstance(x, tuple):
            _, bshape, bidx = x
            return pl.BlockSpec(bshape, lambda i: bidx, pipeline_mode=pl.Buffered(1))
        return whole(x.shape)

    out_shapes = [(S, o[0]) if len(o) == 2 else (o[0], S, o[1]) for o in outs]
    res = pl.pallas_call(
        kern, name=name, grid=(S // tm,),
        in_specs=[row_spec(x.shape) for x in rows] + [full_spec(x) for x in fulls],
        out_specs=[row_spec(s) for s in out_shapes] + [whole(shp) for shp, _ in accs],
        out_shape=[jax.ShapeDtypeStruct(s, o[-1]) for s, o in zip(out_shapes, outs)]
        + [jax.ShapeDtypeStruct(tuple(shp), dt) for shp, dt in accs],
        scratch_shapes=list(scratch),
        compiler_params=_params(("arbitrary",) if na else ("parallel",)),
    )(*rows, *[x[0] if isinstance(x, tuple) else x for x in fulls])
    return res[:no], res[no:]


def _mm_rows(a, w, *, nt=False, out_dtype=F32, tm=512, name):
    S = a.shape[0]
    N = w.shape[0] if nt else w.shape[1]

    def body(r, f, o, acc, s):
        av = r[0][...].astype(BF16)
        o[0][...] = (_dot_nt(av, f[0][...]) if nt else _dot(av, f[0][...])).astype(out_dtype)

    (y,), _ = _rows_call(name, _tile(S, tm), [a], [w], [(N, out_dtype)], [], body)
    return y


def _mm_tn(a, b, *, name):
    S, K1 = a.shape
    N = b.shape[1]
    tn = _tile(N, max(128, (ACC_BYTES // (4 * K1)) // 128 * 128))
    ts = _tile(S, 512 if K1 * a.dtype.itemsize * 512 <= 4 * 1024 * 1024 else 256)

    def body(a_ref, b_ref, o_ref):
        @pl.when(pl.program_id(1) == 0)
        def _():
            o_ref[...] = jnp.zeros_like(o_ref)

        o_ref[...] += _dot_tn(a_ref[...].astype(BF16), b_ref[...].astype(BF16))

    return pl.pallas_call(
        body, name=name, grid=(N // tn, S // ts),
        in_specs=[pl.BlockSpec((ts, K1), lambda j, s: (s, 0)), pl.BlockSpec((ts, tn), lambda j, s: (s, j))],
        out_specs=pl.BlockSpec((K1, tn), lambda j, s: (0, j)),
        out_shape=jax.ShapeDtypeStruct((K1, N), F32),
        compiler_params=_params(("parallel", "arbitrary")),
    )(a, b)


def _grad_to_slab(gslabs, wname, layer, a, b, *, a_cols=None, b_cols=None, chips=(0, N_CHIPS), name):
    slab, rows, bidx = _blk(wname, layer)
    width = _SLABS[slab][0]
    p0, n_p = chips
    assert p0 % n_p == 0
    S = a.shape[-2]

    def tile_bytes(x, ts):
        return ts * x.dtype.itemsize * (x.shape[2] * n_p if x.ndim == 3 else x.shape[1])

    acc_bytes = n_p * rows * (-(-width // 128) * 128) * 4
    ts = next(t for t in (2048, 1024, 512, 256, S) if S % t == 0
              and 2 * (tile_bytes(a, t) + tile_bytes(b, t) + acc_bytes) <= TN_VMEM_BYTES or t == S)

    def operand(x):
        if x.ndim == 3:
            return pl.BlockSpec((n_p, ts, x.shape[2]), lambda s: (p0 // n_p, s, 0))
        return pl.BlockSpec((ts, x.shape[1]), lambda s: (s, 0))

    def part(ref, cols, p):
        if len(ref.shape) == 3:
            return ref[p]
        return ref[...] if cols is None else ref[:, _cols(p, cols)]

    def body(a_ref, b_ref, slab_ref, o_ref):
        @pl.when(pl.program_id(0) == 0)
        def _():
            o_ref[...] = jnp.zeros_like(o_ref)

        for p in range(n_p):
            o_ref[p] += _dot_tn(part(a_ref, a_cols, p).astype(BF16), part(b_ref, b_cols, p).astype(BF16))

    g = gslabs[slab]
    out = pl.pallas_call(
        body, name=name, grid=(S // ts,),
        in_specs=[operand(a), operand(b), pl.BlockSpec(memory_space=pl.ANY)],
        out_specs=pl.BlockSpec((n_p, rows, width), lambda s: (p0 // n_p, bidx, 0)),
        out_shape=jax.ShapeDtypeStruct(g.shape, F32), input_output_aliases={2: 0},
        compiler_params=_params(("arbitrary",)),
    )(a, b, g)
    return {**gslabs, slab: out}


def _vec(g):
    return g.reshape(1, -1)


def _norm_mm(x, g, ws, *, split, out_dtype, name, tm=512):
    S, D = x.shape
    k, n = ws[0][1][1], ws[0][1][2]
    N = n if split == "k" else N_CHIPS * n

    def body(r, f, o, acc, s):
        xn = _rms_tile(r[0][...], f[0][...]).astype(BF16)
        o[0][...] = xn
        if split == "k":
            o[1][...] = _sum_k(xn, f[1:], k).astype(out_dtype)
        else:
            for p in range(N_CHIPS):
                o[1][:, _cols(p, n)] = _dot(xn, f[1 + p][...]).astype(out_dtype)

    (xn, y), _ = _rows_call(name, _tile(S, tm), [x], [_vec(g)] + ws, [(D, BF16), (N, out_dtype)], [], body)
    return xn, y


def _mm_k(a, ws, *, add=None, out_dtype=F32, name, tm=512):
    S = a.shape[-2]
    k, n = ws[0][1][1], ws[0][1][2]
    has_add = add is not None

    def body(r, f, o, acc, s):
        if a.ndim == 3:
            y = None
            for p in range(N_CHIPS):
                t = _dot(r[0][p].astype(BF16), f[p][...])
                y = t if y is None else y + t
        else:
            y = _sum_k(r[0][...].astype(BF16), f, k)
        if has_add:
            y = y + r[1][...]
        o[0][...] = y.astype(out_dtype)

    (y,), _ = _rows_call(name, _tile(S, tm), [a] + ([add] if has_add else []), ws, [(n, out_dtype)], [], body)
    return y


def _mm_k_t(terms, *, out_dtype=F32, name, tm=512):
    S = terms[0][0].shape[0]
    k = terms[0][1][0][1][1]

    def body(r, f, o, acc, s):
        y = None
        for t in range(len(terms)):
            yt = _cat_nt(r[t][...].astype(BF16), f[N_CHIPS * t:N_CHIPS * (t + 1)])
            y = yt if y is None else y + yt
        o[0][...] = y.astype(out_dtype)

    (y,), _ = _rows_call(name, _tile(S, tm), [a for a, _ in terms], [w for _, ws in terms for w in ws],
                         [(N_CHIPS * k, out_dtype)], [], body)
    return y


def _rms_fwd(x, g, *, name):
    def body(r, f, o, acc, s):
        o[0][...] = _rms_tile(r[0][...], f[0][...]).astype(BF16)

    (y,), _ = _rows_call(name, _tile(x.shape[0], 256, 8), [x], [_vec(g)], [(x.shape[1], BF16)], [], body)
    return y


def _rms_dg(x, g, dy, *, name):
    def body(r, f, o, acc, s):
        acc[0][...] += _rms_bwd_tile(r[0][...], f[0][...], r[1][...])[1]

    _, (dg,) = _rows_call(name, _tile(x.shape[0], 256, 8), [x, dy], [_vec(g)], [], [((1, x.shape[1]), F32)], body)
    return dg


def _ffn_up(x, g, wg, wu, *, name, tm=256):
    S, D = x.shape
    h = wg[0][1][2]

    def body(r, f, o, acc, s):
        xn = _rms_tile(r[0][...], f[0][...]).astype(BF16)
        o[0][...] = xn
        for p in range(N_CHIPS):
            gate = _dot(xn, f[1 + p][...])
            up = _dot(xn, f[1 + N_CHIPS + p][...])
            o[1][p] = gate.astype(BF16)
            o[2][p] = up.astype(BF16)
            o[3][p] = (gate * _sigmoid(gate) * up).astype(BF16)

    (xn, gate, up, hid), _ = _rows_call(name, _tile(S, tm), [x], [_vec(g)] + wg + wu,
                                        [(D, BF16), (N_CHIPS, h, BF16), (N_CHIPS, h, BF16), (N_CHIPS, h, BF16)], [],
                                        body)
    return xn, gate, up, hid


def _ffn_bwd_hidden(dy, wd, gate, up, *, name, tm=256):
    S = dy.shape[0]
    h = wd[0][1][1]

    def body(r, f, o, acc, s):
        dyb = r[0][...].astype(BF16)
        for p in range(N_CHIPS):
            dh = _dot_nt(dyb, f[p][...])
            gv = r[1][p].astype(F32)
            sg = _sigmoid(gv)
            o[0][p] = (dh * r[2][p].astype(F32) * (sg * (1.0 + gv * (1.0 - sg)))).astype(BF16)
            o[1][p] = (dh * gv * sg).astype(BF16)

    (dg, du), _ = _rows_call(name, _tile(S, tm), [dy, gate, up], wd, [(N_CHIPS, h, BF16), (N_CHIPS, h, BF16)], [],
                             body)
    return dg, du


def _ffn_in_bwd(dg, du, wg, wu, x, g, dres, *, name, tm=256):
    S, D = x.shape

    def body(r, f, o, acc, s):
        tot = None
        for p in range(N_CHIPS):
            y = _dot_nt(r[0][p], f[1 + p][...]) + _dot_nt(r[1][p], f[1 + N_CHIPS + p][...])
            tot = y if tot is None else tot + y
        dx, dgn = _rms_bwd_tile(r[2][...], f[0][...], tot)
        o[0][...] = dx + r[3][...]
        acc[0][...] += dgn

    (dx,), (dgn,) = _rows_call(name, _tile(S, tm), [dg, du, x, dres], [_vec(g)] + wg + wu, [(D, F32)],
                               [((1, D), F32)], body)
    return dx, dgn


def _norm_bwd_k(da, ws, x, g, dres, *, name, tm=512):
    S, D = x.shape

    def body(r, f, o, acc, s):
        dx, dg = _rms_bwd_tile(r[1][...], f[0][...], _cat_nt(r[0][...].astype(BF16), f[1:]))
        o[0][...] = dx + r[2][...]
        acc[0][...] += dg

    (dx,), (dg,) = _rows_call(name, _tile(S, tm), [da, x, dres], [_vec(g)] + ws, [(D, F32)], [((1, D), F32)], body)
    return dx, dg


def _norm_bwd_n(das, ws, x, g, dres, *, name, tm=256):
    S, D = x.shape
    n = ws[0][1][2]

    def body(r, f, o, acc, s):
        tot = None
        for p in range(N_CHIPS):
            y = _dot_nt(r[p // 2][:, _cols(p % 2, n)], f[1 + p][...])
            tot = y if tot is None else tot + y
        dx, dg = _rms_bwd_tile(r[2][...], f[0][...], tot)
        o[0][...] = dx + r[3][...]
        acc[0][...] += dg

    (dx,), (dg,) = _rows_call(name, _tile(S, tm), list(das) + [x, dres], [_vec(g)] + ws, [(D, F32)], [((1, D), F32)],
                              body)
    return dx, dg


def _ln_stats(v):
    mu = _mean(v)
    xc = v - mu
    rstd = lax.rsqrt(_mean(xc * xc) + EPS)
    return xc * rstd, rstd


def _even_fwd(proj, wm, bcol, cw, cb, lg, lb, *, name):
    S = proj.shape[0]
    tm = _tile(S, 256)
    hb = tm // CONV_HALO
    nblk = tm // GMLP_BLOCK

    def body(p_ref, halo_ref, wm_ref, b_ref, cw_ref, cb_ref, lg_ref, lb_ref, mix_ref, hc_ref, hext_ref):
        i = pl.program_id(0)
        gu, _ = _gelu(p_ref[:, 0:A_WIDTH])
        gv, _ = _gelu(p_ref[:, A_WIDTH:2 * A_WIDTH])
        vn, _ = _ln_stats(gv)
        vnb = vn.astype(BF16)
        for n in range(nblk):
            rows = slice(n * GMLP_BLOCK, (n + 1) * GMLP_BLOCK)
            for g in range(A_GROUPS):
                cols = slice(g * GMLP_BLOCK, (g + 1) * GMLP_BLOCK)
                sg = jnp.dot(wm_ref[g], vnb[rows, cols], preferred_element_type=F32) + b_ref[g]
                mix_ref[rows, cols] = (gu[rows, cols] * sg).astype(BF16)
        h = p_ref[:, 1024:1536] * _sigmoid(p_ref[:, 1536:2048])
        hh = halo_ref[:, 0:B_WIDTH] * _sigmoid(halo_ref[:, B_WIDTH:2 * B_WIDTH])
        hext_ref[0:CONV_HALO, :] = jnp.where(i > 0, hh, 0.0)
        hext_ref[CONV_HALO:CONV_HALO + tm, :] = h
        acc = jnp.zeros((tm, B_WIDTH), F32)
        for k in range(CONV_WIDTH):
            acc = acc + cw_ref[k:k + 1, :] * hext_ref[pl.ds(k + CONV_HALO - CONV_WIDTH + 1, tm), :]
        hc = acc + cb_ref[...]
        hc_ref[...] = hc
        hhat, _ = _ln_stats(hc)
        hl = hhat * lg_ref[...] + lb_ref[...]
        mix_ref[:, A_WIDTH:A_WIDTH + B_WIDTH] = (hl * _sigmoid(hl)).astype(BF16)

    vec = pl.BlockSpec((1, B_WIDTH), lambda i: (0, 0))
    return pl.pallas_call(
        body, name=name, grid=(S // tm,),
        in_specs=[
            pl.BlockSpec((tm, 2048), lambda i: (i, 0)),
            pl.BlockSpec((CONV_HALO, 1024), lambda i: (jnp.maximum(i * hb - 1, 0), 1)),
            pl.BlockSpec((A_GROUPS, GMLP_BLOCK, GMLP_BLOCK), lambda i: (0, 0, 0)),
            pl.BlockSpec((A_GROUPS, GMLP_BLOCK, 1), lambda i: (0, 0, 0)),
            pl.BlockSpec((CONV_HALO, B_WIDTH), lambda i: (0, 0)),
            vec, vec, vec,
        ],
        out_specs=[pl.BlockSpec((tm, 1024), lambda i: (i, 0)), pl.BlockSpec((tm, B_WIDTH), lambda i: (i, 0))],
        out_shape=[jax.ShapeDtypeStruct((S, 1024), BF16), jax.ShapeDtypeStruct((S, B_WIDTH), F32)],
        scratch_shapes=[pltpu.VMEM((tm + CONV_HALO, B_WIDTH), F32)],
        compiler_params=_params(("parallel",)),
    )(proj, proj, wm, bcol, cw, cb, lg, lb)


def _even_bwd1(proj, dmix, hc, wm, wmt, bcol, lg, lb, *, name):
    S = proj.shape[0]
    tm = _tile(S, 256)
    nblk = tm // GMLP_BLOCK

    def body(p_ref, dm_ref, hc_ref, wm_ref, wmt_ref, b_ref, lg_ref, lb_ref,
             dpa_ref, dhc_ref, dwm_ref, db_ref, dlg_ref, dlb_ref, dcb_ref, dgu_ref, dvn_ref):
        @pl.when(pl.program_id(0) == 0)
        def _():
            dwm_ref[...] = jnp.zeros_like(dwm_ref)
            db_ref[...] = jnp.zeros_like(db_ref)
            dlg_ref[...] = jnp.zeros_like(dlg_ref)
            dlb_ref[...] = jnp.zeros_like(dlb_ref)
            dcb_ref[...] = jnp.zeros_like(dcb_ref)

        au = p_ref[:, 0:A_WIDTH]
        av = p_ref[:, A_WIDTH:2 * A_WIDTH]
        gu, tu = _gelu(au)
        gv, tv = _gelu(av)
        vn, rstd = _ln_stats(gv)
        vnb = vn.astype(BF16)
        for n in range(nblk):
            rows = slice(n * GMLP_BLOCK, (n + 1) * GMLP_BLOCK)
            for g in range(A_GROUPS):
                cols = slice(g * GMLP_BLOCK, (g + 1) * GMLP_BLOCK)
                vb = vnb[rows, cols]
                sg = jnp.dot(wm_ref[g], vb, preferred_element_type=F32) + b_ref[g]
                da = dm_ref[rows, cols]
                dsg = da * gu[rows, cols]
                dgu_ref[rows, cols] = da * sg
                dsgb = dsg.astype(BF16)
                dwm_ref[g] += _dot_nt(dsgb, vb)
                db_ref[g] += jnp.sum(dsg, axis=1, keepdims=True)
                dvn_ref[rows, cols] = jnp.dot(wmt_ref[g], dsgb, preferred_element_type=F32)
        dvn = dvn_ref[...]
        dgv = rstd * (dvn - _mean(dvn) - vn * _mean(dvn * vn))
        dpa_ref[:, 0:A_WIDTH] = (dgu_ref[...] * _gelu_grad(au, tu)).astype(BF16)
        dpa_ref[:, A_WIDTH:2 * A_WIDTH] = (dgv * _gelu_grad(av, tv)).astype(BF16)
        hhat, rstd2 = _ln_stats(hc_ref[...])
        lgv = lg_ref[...]
        hl = hhat * lgv + lb_ref[...]
        s = _sigmoid(hl)
        dhl = dm_ref[:, A_WIDTH:A_WIDTH + B_WIDTH] * (s * (1.0 + hl * (1.0 - s)))
        dlg_ref[...] += jnp.sum(dhl * hhat, axis=0, keepdims=True)
        dlb_ref[...] += jnp.sum(dhl, axis=0, keepdims=True)
        dhh = dhl * lgv
        dhc = rstd2 * (dhh - _mean(dhh) - hhat * _mean(dhh * hhat))
        dcb_ref[...] += jnp.sum(dhc, axis=0, keepdims=True)
        dhc_ref[...] = dhc

    vec = pl.BlockSpec((1, B_WIDTH), lambda i: (0, 0))
    w3 = pl.BlockSpec((A_GROUPS, GMLP_BLOCK, GMLP_BLOCK), lambda i: (0, 0, 0))
    b3 = pl.BlockSpec((A_GROUPS, GMLP_BLOCK, 1), lambda i: (0, 0, 0))
    return pl.pallas_call(
        body, name=name, grid=(S // tm,),
        in_specs=[
            pl.BlockSpec((tm, 1024), lambda i: (i, 0)),
            pl.BlockSpec((tm, 1024), lambda i: (i, 0)),
            pl.BlockSpec((tm, B_WIDTH), lambda i: (i, 0)),
            w3, w3, b3, vec, vec,
        ],
        out_specs=[pl.BlockSpec((tm, 1024), lambda i: (i, 0)), pl.BlockSpec((tm, B_WIDTH), lambda i: (i, 0)),
                   w3, b3, vec, vec, vec],
        out_shape=[
            jax.ShapeDtypeStruct((S, 1024), BF16), jax.ShapeDtypeStruct((S, B_WIDTH), F32),
            jax.ShapeDtypeStruct((A_GROUPS, GMLP_BLOCK, GMLP_BLOCK), F32),
            jax.ShapeDtypeStruct((A_GROUPS, GMLP_BLOCK, 1), F32),
            jax.ShapeDtypeStruct((1, B_WIDTH), F32), jax.ShapeDtypeStruct((1, B_WIDTH), F32),
            jax.ShapeDtypeStruct((1, B_WIDTH), F32),
        ],
        scratch_shapes=[pltpu.VMEM((tm, A_WIDTH), F32), pltpu.VMEM((tm, A_WIDTH), F32)],
        compiler_params=_params(("arbitrary",)),
    )(proj, dmix, hc, wm, wmt, bcol, lg, lb)


def _even_bwd2(proj, dhc, cw, *, name):
    S = proj.shape[0]
    tm = _tile(S, 256)
    hb = tm // CONV_HALO
    nt = S // tm
    last_halo = S // CONV_HALO - 1
    lo = CONV_HALO - CONV_WIDTH + 1

    def body(p_ref, halo_ref, d_ref, dnext_ref, cw_ref, dpb_ref, dcw_ref, hext_ref, dext_ref):
        i = pl.program_id(0)

        @pl.when(i == 0)
        def _():
            dcw_ref[...] = jnp.zeros_like(dcw_ref)

        ba = p_ref[:, 0:B_WIDTH]
        sg = _sigmoid(p_ref[:, B_WIDTH:2 * B_WIDTH])
        hh = halo_ref[:, 0:B_WIDTH] * _sigmoid(halo_ref[:, B_WIDTH:2 * B_WIDTH])
        hext_ref[0:CONV_HALO, :] = jnp.where(i > 0, hh, 0.0)
        hext_ref[CONV_HALO:CONV_HALO + tm, :] = ba * sg
        dhc_t = d_ref[...]
        dext_ref[0:tm, :] = dhc_t
        dext_ref[tm:tm + CONV_HALO, :] = jnp.where(i < nt - 1, dnext_ref[...], 0.0)
        dh = jnp.zeros((tm, B_WIDTH), F32)
        for k in range(CONV_WIDTH):
            dh = dh + cw_ref[k:k + 1, :] * dext_ref[pl.ds(CONV_WIDTH - 1 - k, tm), :]
            dcw_ref[k:k + 1, :] += jnp.sum(dhc_t * hext_ref[pl.ds(k + lo, tm), :], axis=0, keepdims=True)
        dpb_ref[:, 0:B_WIDTH] = (dh * sg).astype(BF16)
        dpb_ref[:, B_WIDTH:2 * B_WIDTH] = (dh * ba * sg * (1.0 - sg)).astype(BF16)

    return pl.pallas_call(
        body, name=name, grid=(nt,),
        in_specs=[
            pl.BlockSpec((tm, 1024), lambda i: (i, 1)),
            pl.BlockSpec((CONV_HALO, 1024), lambda i: (jnp.maximum(i * hb - 1, 0), 1)),
            pl.BlockSpec((tm, B_WIDTH), lambda i: (i, 0)),
            pl.BlockSpec((CONV_HALO, B_WIDTH), lambda i: (jnp.minimum((i + 1) * hb, last_halo), 0)),
            pl.BlockSpec((CONV_HALO, B_WIDTH), lambda i: (0, 0)),
        ],
        out_specs=[pl.BlockSpec((tm, 1024), lambda i: (i, 0)), pl.BlockSpec((CONV_HALO, B_WIDTH), lambda i: (0, 0))],
        out_shape=[jax.ShapeDtypeStruct((S, 1024), BF16), jax.ShapeDtypeStruct((CONV_HALO, B_WIDTH), F32)],
        scratch_shapes=[pltpu.VMEM((tm + CONV_HALO, B_WIDTH), F32), pltpu.VMEM((tm + CONV_HALO, B_WIDTH), F32)],
        compiler_params=_params(("arbitrary",)),
    )(proj, proj, dhc, dhc, cw)


_CA_SCALE = CA_HEAD_DIM ** -0.5


def _softmax_rows(s):
    e = jnp.exp(s - jnp.max(s, axis=-1, keepdims=True))
    return e / jnp.sum(e, axis=-1, keepdims=True)


def _attn_fwd(q, k, v, *, name):
    S = q.shape[0]

    def body(r, f, o, acc, s):
        for h in range(CA_HEADS):
            cols = _cols(h, CA_HEAD_DIM)
            p = _softmax_rows(_dot_nt(r[0][:, cols], f[0][:, cols]) * _CA_SCALE)
            o[0][:, cols] = _dot(p.astype(BF16), f[1][:, cols]).astype(BF16)

    (o_,), _ = _rows_call(name, _tile(S, 512), [q], [k, v], [(D_MODEL, BF16)], [], body)
    return o_


def _attn_bwd(dy, wo, q, k, v, *, name):
    S = q.shape[0]
    M = k.shape[0]

    def body(r, f, o, acc, s):
        dyb = r[0][...].astype(BF16)
        for h in range(CA_HEADS):
            cols = _cols(h, CA_HEAD_DIM)
            qh = r[1][:, cols]
            kh = f[0][:, cols]
            vh = f[1][:, cols]
            doh = _dot_nt(dyb, f[2 + h][...]).astype(BF16)
            p = _softmax_rows(_dot_nt(qh, kh) * _CA_SCALE)
            acc[1][:, cols] += _dot_tn(p.astype(BF16), doh)
            dp = _dot_nt(doh, vh)
            ds = (p * (dp - jnp.sum(dp * p, axis=-1, keepdims=True)) * _CA_SCALE).astype(BF16)
            o[0][:, cols] = _dot(ds, kh).astype(BF16)
            acc[0][:, cols] += _dot_tn(ds, qh)

    (dq,), (dk, dv) = _rows_call(name, _tile(S, 512), [dy, q], [k, v] + wo, [(D_MODEL, BF16)],
                                 [((M, D_MODEL), F32), ((M, D_MODEL), F32)], body)
    return dq, dk, dv


def _s5_readout(xs2, cd, u, d, *, name, tm=256):
    def body(r, f, o, acc, s):
        y = _dot(r[0][...].astype(BF16), f[0][...]) + f[1][...] * r[1][...]
        o[0][...] = y
        o[1][...] = _gelu(y)[0].astype(BF16)

    (y, yg), _ = _rows_call(name, _tile(xs2.shape[0], tm), [xs2, u], [cd, d], [(C_WIDTH, F32), (C_WIDTH, BF16)], [],
                            body)
    return y, yg


def _glu_out(yg, ws, x, *, name, tm=512):
    n = ws[0][1][2]

    def body(r, f, o, acc, s):
        ygv = r[0][...]
        ov = [_dot(ygv, f[p][...]) for p in range(N_CHIPS)]
        for p in range(N_CHIPS):
            o[0][:, _cols(p, n)] = ov[p].astype(BF16)
        for p in range(2):
            o[1][:, _cols(p, n)] = r[1][:, _cols(p, n)] + ov[p] * _sigmoid(ov[2 + p])

    (o_, y), _ = _rows_call(name, _tile(x.shape[0], tm), [yg, x], ws, [(2 * D_MODEL, BF16), (D_MODEL, F32)], [], body)
    return o_, y


def _glu_out_bwd(o_, dy, ws, y, u, d, *, name, tm=256):
    n = ws[0][1][2]

    def body(r, f, o, acc, s):
        o1 = r[0][:, 0:D_MODEL].astype(F32)
        sg = _sigmoid(r[0][:, D_MODEL:2 * D_MODEL].astype(F32))
        dyv = r[1][...]
        do1 = (dyv * sg).astype(BF16)
        do2 = (dyv * o1 * sg * (1.0 - sg)).astype(BF16)
        o[0][:, 0:D_MODEL] = do1
        o[0][:, D_MODEL:2 * D_MODEL] = do2
        dyg = None
        for p in range(N_CHIPS):
            t = _dot_nt((do1 if p < 2 else do2)[:, _cols(p % 2, n)], f[1 + p][...])
            dyg = t if dyg is None else dyg + t
        yv = r[2][...]
        dys = dyg * _gelu_grad(yv, _gelu(yv)[1])
        o[1][...] = dys.astype(BF16)
        o[2][...] = f[0][...] * dys
        acc[0][...] += jnp.sum(dys * r[3][...], axis=0, keepdims=True)

    (do, dys, dus), (dd,) = _rows_call(name, _tile(dy.shape[0], tm), [o_, dy, y, u], [d] + ws,
                                       [(2 * D_MODEL, BF16), (C_WIDTH, BF16), (C_WIDTH, F32)], [((1, C_WIDTH), F32)],
                                       body)
    return do, dys, dus, dd


def _s5_in_bwd(gs2, bd, dus, ws, x, g, dres, *, name, tm=256):
    D = x.shape[1]

    def body(r, f, o, acc, s):
        du = (_dot_nt(r[0][...].astype(BF16), f[1][...]) + r[1][...]).astype(BF16)
        o[0][...] = du
        dx, dg = _rms_bwd_tile(r[2][...], f[0][...], _cat_nt(du, f[2:]))
        o[1][...] = dx + r[3][...]
        acc[0][...] += dg

    (du, dx), (dg,) = _rows_call(name, _tile(x.shape[0], tm), [gs2, dus, x, dres], [_vec(g), bd] + ws,
                                 [(C_WIDTH, BF16), (D, F32)], [((1, D), F32)], body)
    return du, dx, dg


_SCAN_CHUNK = 128
_SCAN_UNROLL = 8
_RE = slice(0, STATE_ROWS)
_IM = slice(STATE_ROWS, 2 * STATE_ROWS)


def _scan_fwd(bu, a, *, name):
    S = bu.shape[0]
    tc = _tile(S, _SCAN_CHUNK, 8)

    def body(bu_ref, a_ref, xs_ref, st_ref):
        @pl.when(pl.program_id(0) == 0)
        def _():
            st_ref[...] = jnp.zeros_like(st_ref)

        ar = a_ref[_RE, :]
        ai = a_ref[_IM, :]

        def step(t, carry):
            xr, xi = carry
            nr = ar * xr - ai * xi + bu_ref[t, _RE, :]
            ni = ar * xi + ai * xr + bu_ref[t, _IM, :]
            xs_ref[t, _RE, :] = nr
            xs_ref[t, _IM, :] = ni
            return nr, ni

        xr, xi = lax.fori_loop(0, tc, step, (st_ref[_RE, :], st_ref[_IM, :]), unroll=_SCAN_UNROLL)
        st_ref[_RE, :] = xr
        st_ref[_IM, :] = xi

    blk = pl.BlockSpec((tc, 2 * STATE_ROWS, STATE_LANES), lambda i: (i, 0, 0))
    return pl.pallas_call(
        body, name=name, grid=(S // tc,),
        in_specs=[blk, pl.BlockSpec((2 * STATE_ROWS, STATE_LANES), lambda i: (0, 0))], out_specs=blk,
        out_shape=jax.ShapeDtypeStruct(bu.shape, F32),
        scratch_shapes=[pltpu.VMEM((2 * STATE_ROWS, STATE_LANES), F32)],
        compiler_params=_params(("arbitrary",)),
    )(bu, a)


def _scan_bwd(dxs, xs, a, *, name):
    S = dxs.shape[0]
    tc = _tile(S, _SCAN_CHUNK, 8)
    nc = S // tc

    def body(dx_ref, xs_ref, a_ref, g_ref, da_ref, st_ref):
        @pl.when(pl.program_id(0) == 0)
        def _():
            st_ref[...] = jnp.zeros_like(st_ref)
            da_ref[...] = jnp.zeros_like(da_ref)

        ar = a_ref[_RE, :]
        ai = a_ref[_IM, :]

        def step(j, carry):
            gr, gi, dar, dai = carry
            t = tc - 1 - j
            xr = xs_ref[t, _RE, :]
            xi = xs_ref[t, _IM, :]
            dar = dar + gr * xr + gi * xi
            dai = dai + gi * xr - gr * xi
            nr = dx_ref[t, _RE, :] + ar * gr + ai * gi
            ni = dx_ref[t, _IM, :] + ar * gi - ai * gr
            g_ref[t, _RE, :] = nr
            g_ref[t, _IM, :] = ni
            return nr, ni, dar, dai

        init = (st_ref[_RE, :], st_ref[_IM, :], da_ref[_RE, :], da_ref[_IM, :])
        gr, gi, dar, dai = lax.fori_loop(0, tc, step, init, unroll=_SCAN_UNROLL)
        st_ref[_RE, :] = gr
        st_ref[_IM, :] = gi
        da_ref[_RE, :] = dar
        da_ref[_IM, :] = dai

    blk = pl.BlockSpec((tc, 2 * STATE_ROWS, STATE_LANES), lambda i: (nc - 1 - i, 0, 0))
    vec = pl.BlockSpec((2 * STATE_ROWS, STATE_LANES), lambda i: (0, 0))
    return pl.pallas_call(
        body, name=name, grid=(nc,), in_specs=[blk, blk, vec], out_specs=[blk, vec],
        out_shape=[jax.ShapeDtypeStruct(dxs.shape, F32), jax.ShapeDtypeStruct((2 * STATE_ROWS, STATE_LANES), F32)],
        scratch_shapes=[pltpu.VMEM((2 * STATE_ROWS, STATE_LANES), F32)],
        compiler_params=_params(("arbitrary",)),
    )(dxs, xs, a)


def _loss_head(x, g, target, *, name):
    S, D = x.shape

    def body(r, f, o, acc, s):
        xv = r[0][...]
        gv = f[0][...]
        rs = lax.rsqrt(_mean(xv * xv) + EPS)
        xh = xv * rs
        err = xh * gv - r[1][...]
        acc[1][...] += 0.5 * jnp.sum(_mean(err * err), axis=0, keepdims=True)
        dy = err * (1.0 / D)
        dyg = dy * gv
        o[0][...] = rs * (dyg - xh * _mean(dyg * xh))
        acc[0][...] += jnp.sum(dy * xh, axis=0, keepdims=True)

    (dx,), (dg, loss) = _rows_call(name, _tile(S, 256, 8), [x, target], [_vec(g)], [(D, F32)],
                                   [((1, D), F32), ((1, 128), F32)], body)
    return dx, dg, loss


_ADAM_C1 = 1.0 - ADAM_B1 ** ADAM_STEP
_ADAM_C2 = 1.0 - ADAM_B2 ** ADAM_STEP
_ONE_BLOCK_BYTES = 8 * 1024 * 1024


def _adamw_math(w, g, m, v):
    nm = ADAM_B1 * m + (1.0 - ADAM_B1) * g
    nv = ADAM_B2 * v + (1.0 - ADAM_B2) * (g * g)
    m_hat = nm / _ADAM_C1
    v_hat = nv / _ADAM_C2
    return -ADAM_LR * (m_hat / (jnp.sqrt(v_hat) + ADAM_EPS) + ADAM_WD * w), nm, nv


def _adamw_shard(w, gsum, r0, m, v, *, name):
    R, C = w.shape
    tr = math.gcd(r0, R)
    tr = _tile(tr, 256, 8) if tr > 256 else tr
    assert r0 % tr == 0 and R % tr == 0

    def body(w_ref, g_ref, m_ref, v_ref, go_ref, d_ref, nm_ref, nv_ref):
        gv = g_ref[...]
        go_ref[...] = gv
        d_ref[...], nm_ref[...], nv_ref[...] = _adamw_math(w_ref[...], gv, m_ref[...], v_ref[...])

    blk = pl.BlockSpec((tr, C), lambda i: (i, 0))
    out = jax.ShapeDtypeStruct((R, C), F32)
    return pl.pallas_call(
        body, name=name, grid=(R // tr,),
        in_specs=[blk, pl.BlockSpec((tr, C), lambda i: (r0 // tr + i, 0)), blk, blk], out_specs=[blk] * 4,
        out_shape=[out] * 4, compiler_params=_params(("parallel",)),
    )(w, gsum, m, v)


def _adamw_small(ws, gs, ms, vs, *, name):
    n = len(ws)

    def body(*refs):
        w_r, g_r, m_r, v_r = refs[:n], refs[n:2 * n], refs[2 * n:3 * n], refs[3 * n:4 * n]
        d_r, nm_r, nv_r = refs[4 * n:5 * n], refs[5 * n:6 * n], refs[6 * n:7 * n]
        for k in range(n):
            d_r[k][...], nm_r[k][...], nv_r[k][...] = _adamw_math(w_r[k][...], g_r[k][...], m_r[k][...], v_r[k][...])

    vm = pl.BlockSpec(memory_space=pltpu.VMEM)
    out = [jax.ShapeDtypeStruct(w.shape, F32) for w in ws]
    res = pl.pallas_call(body, name=name, in_specs=[vm] * (4 * n), out_specs=[vm] * (3 * n), out_shape=out * 3,
                         compiler_params=pltpu.CompilerParams(vmem_limit_bytes=VMEM_LIMIT))(*ws, *gs, *ms, *vs)
    return res[:n], res[n:2 * n], res[2 * n:]


def _sum_slots(x, *, name):
    n, R, C = x.shape
    tr = R if (n + 1) * R * C * 4 <= _ONE_BLOCK_BYTES else _tile(R, 256, 8)

    def body(x_ref, o_ref):
        acc = x_ref[0]
        for k in range(1, n):
            acc = acc + x_ref[k]
        o_ref[...] = acc

    return pl.pallas_call(
        body, name=name, grid=(R // tr,),
        in_specs=[pl.BlockSpec((n, tr, C), lambda i: (0, i, 0))], out_specs=pl.BlockSpec((tr, C), lambda i: (i, 0)),
        out_shape=jax.ShapeDtypeStruct((R, C), F32), compiler_params=_params(("parallel",)),
    )(x)


def _pair_sum(g, r, half, *, name):
    n, R, C = g.shape
    Rh = R // 2
    tr = _tile(Rh, 256, 8)
    nb = Rh // tr

    def body(half_ref, g_ref, r_ref, o_ref):
        o_ref[...] = (g_ref[...] + r_ref[...]).astype(BF16)

    return pl.pallas_call(
        body, name=name,
        grid_spec=pltpu.PrefetchScalarGridSpec(
            num_scalar_prefetch=1, grid=(n, nb),
            in_specs=[pl.BlockSpec((1, tr, C), lambda p, i, h: (p, h[0] * nb + i, 0)),
                      pl.BlockSpec((1, tr, C), lambda p, i, h: (p, i, 0))],
            out_specs=pl.BlockSpec((1, tr, C), lambda p, i, h: (p, i, 0)),
        ),
        out_shape=jax.ShapeDtypeStruct((n, Rh, C), BF16), compiler_params=_params(("parallel", "parallel")),
    )(half, g, r)


def _chip_sum(g, r, slots, where, *, name):
    n, R, C = g.shape
    Rh = R // 2
    tr = _tile(Rh, 256, 8)
    nb = Rh // tr

    def body(w_ref, g_ref, r_ref, s_ref, o_ref):
        acc = g_ref[0] + r_ref[0]
        for k in range(slots.shape[0]):
            acc = acc + s_ref[k].astype(F32)
        o_ref[...] = acc

    return pl.pallas_call(
        body, name=name,
        grid_spec=pltpu.PrefetchScalarGridSpec(
            num_scalar_prefetch=1, grid=(nb,),
            in_specs=[pl.BlockSpec((1, tr, C), lambda i, w: (w[0], w[1] * nb + i, 0)),
                      pl.BlockSpec((1, tr, C), lambda i, w: (w[0], i, 0)),
                      pl.BlockSpec((slots.shape[0], tr, C), lambda i, w: (0, i, 0))],
            out_specs=pl.BlockSpec((tr, C), lambda i, w: (w[1] * nb + i, 0)),
        ),
        out_shape=jax.ShapeDtypeStruct((R, C), F32), compiler_params=_params(("parallel",)),
    )(where, g, r, slots)


ANY = pl.BlockSpec(memory_space=pl.ANY)


def _place():
    return lax.axis_index("x"), lax.axis_index("y"), lax.axis_index("c")


def _other_chips(x, y):
    return [(1 - x, y), (x, 1 - y), (1 - x, 1 - y)]


def _allgather_small(v, *, name):
    R, C = v.shape

    def body(x_ref, out_ref, send_sems, recv_sems, local_sem):
        x, y, c = _place()
        me, sibling = (x, y, c), (x, y, 1 - c)
        chips = _other_chips(x, y)

        def rows(px, py, pc):
            return out_ref.at[pl.ds((4 * px + 2 * py + pc) * R, R), :]

        def copy(k, block, to, src=None):
            return pltpu.make_async_remote_copy(
                src_ref=rows(*block) if src is None else src, dst_ref=rows(*block),
                send_sem=send_sems.at[k], recv_sem=recv_sems.at[k], device_id=to, device_id_type=MESH)

        mine = pltpu.make_async_copy(x_ref, rows(*me), local_sem)
        mine.start()
        first = [copy(0, me, sibling, src=x_ref)]
        first += [copy(1 + j, me, (*chip, c), src=x_ref) for j, chip in enumerate(chips)]
        for cp in first:
            cp.start()
        passed = [copy(4 + j, (*chip, c), sibling) for j, chip in enumerate(chips)]
        for j, chip in enumerate(chips):
            copy(1 + j, (*chip, c), me).wait_recv()
            passed[j].start()
        copy(0, sibling, me).wait_recv()
        for j, chip in enumerate(chips):
            copy(4 + j, (*chip, 1 - c), me).wait_recv()
        for cp in first + passed:
            cp.wait_send()
        mine.wait()

    return pl.pallas_call(
        body, name=name, out_shape=jax.ShapeDtypeStruct((N_DEV * R, C), v.dtype),
        in_specs=[pl.BlockSpec(memory_space=pltpu.VMEM)], out_specs=pl.BlockSpec(memory_space=pltpu.VMEM),
        scratch_shapes=[pltpu.SemaphoreType.DMA((7,)), pltpu.SemaphoreType.DMA((7,)), pltpu.SemaphoreType.DMA],
        compiler_params=pltpu.CompilerParams(vmem_limit_bytes=VMEM_LIMIT),
    )(v)


def _aliased_comm_call(body, bufs, n_sems, *, name):
    n = len(bufs)
    return pl.pallas_call(
        body, name=name, out_shape=[jax.ShapeDtypeStruct(b.shape, b.dtype) for b in bufs],
        in_specs=[ANY] * n, out_specs=[ANY] * n, input_output_aliases={k: k for k in range(n)},
        scratch_shapes=[pltpu.SemaphoreType.DMA((n_sems,)), pltpu.SemaphoreType.DMA((n_sems,))],
    )(*bufs)


def _allgather_chips(bufs, *, name):
    n = len(bufs)

    def body(*refs):
        outs, send_sems, recv_sems = refs[n:2 * n], refs[2 * n], refs[2 * n + 1]
        x, y, c = _place()
        chips = _other_chips(x, y)

        def copy(b, j, chip, hc, to):
            rh = bufs[b].shape[1] // 2
            part = outs[b].at[2 * chip[0] + chip[1], pl.ds(hc * rh, rh), :]
            return pltpu.make_async_remote_copy(src_ref=part, dst_ref=part, send_sem=send_sems.at[6 * b + j],
                                                recv_sem=recv_sems.at[6 * b + j], device_id=to, device_id_type=MESH)

        first = [copy(b, j, (x, y), c, (*chip, c)) for b in range(n) for j, chip in enumerate(chips)]
        for cp in first:
            cp.start()
        passed = []
        for b in range(n):
            for j, chip in enumerate(chips):
                copy(b, j, chip, c, (x, y, c)).wait_recv()
                passed.append(copy(b, 3 + j, chip, c, (x, y, 1 - c)))
                passed[-1].start()
        for b in range(n):
            for j, chip in enumerate(chips):
                copy(b, 3 + j, chip, 1 - c, (x, y, c)).wait_recv()
        for cp in first + passed:
            cp.wait_send()

    return _aliased_comm_call(body, bufs, 6 * n, name=name)


def _pair_exchange(gs, *, name):
    n = len(gs)

    def body(*refs):
        ins, outs, send_sems, recv_sems = refs[:n], refs[n:2 * n], refs[2 * n], refs[2 * n + 1]
        x, y, c = _place()
        cps = []
        for b in range(n):
            rh = gs[b].shape[1] // 2
            cps.append(pltpu.make_async_remote_copy(
                src_ref=ins[b].at[:, pl.ds((1 - c) * rh, rh), :], dst_ref=outs[b], send_sem=send_sems.at[b],
                recv_sem=recv_sems.at[b], device_id=(x, y, 1 - c), device_id_type=MESH))
        for cp in cps:
            cp.start()
        for cp in cps:
            cp.wait()

    return pl.pallas_call(
        body, name=name, out_shape=[jax.ShapeDtypeStruct((g.shape[0], g.shape[1] // 2, g.shape[2]), g.dtype) for g in gs],
        in_specs=[ANY] * n, out_specs=[ANY] * n,
        scratch_shapes=[pltpu.SemaphoreType.DMA((n,)), pltpu.SemaphoreType.DMA((n,))],
    )(*gs)


def _chip_exchange(hs, *, name):
    n = len(hs)

    def body(*refs):
        ins, outs, send_sems, recv_sems = refs[:n], refs[n:2 * n], refs[2 * n], refs[2 * n + 1]
        x, y, c = _place()
        cps = [pltpu.make_async_remote_copy(
            src_ref=ins[b].at[2 * cx + cy], dst_ref=outs[b].at[j], send_sem=send_sems.at[3 * b + j],
            recv_sem=recv_sems.at[3 * b + j], device_id=(cx, cy, c), device_id_type=MESH)
            for b in range(n) for j, (cx, cy) in enumerate(_other_chips(x, y))]
        for cp in cps:
            cp.start()
        for cp in cps:
            cp.wait()

    return pl.pallas_call(
        body, name=name, out_shape=[jax.ShapeDtypeStruct((3,) + h.shape[1:], h.dtype) for h in hs],
        in_specs=[ANY] * n, out_specs=[ANY] * n,
        scratch_shapes=[pltpu.SemaphoreType.DMA((3 * n,)), pltpu.SemaphoreType.DMA((3 * n,))],
    )(*hs)


def _pair_share(ss, *, name):
    n = len(ss)

    def body(*refs):
        outs, send_sems, recv_sems = refs[n:2 * n], refs[2 * n], refs[2 * n + 1]
        x, y, c = _place()
        cps = []
        for b in range(n):
            rh = ss[b].shape[0] // 2
            mine = outs[b].at[pl.ds(c * rh, rh), :]
            cps.append(pltpu.make_async_remote_copy(src_ref=mine, dst_ref=mine, send_sem=send_sems.at[b],
                                                    recv_sem=recv_sems.at[b], device_id=(x, y, 1 - c),
                                                    device_id_type=MESH))
        for cp in cps:
            cp.start()
        for b, cp in enumerate(cps):
            rh = ss[b].shape[0] // 2
            theirs = outs[b].at[pl.ds((1 - c) * rh, rh), :]
            pltpu.make_async_remote_copy(src_ref=theirs, dst_ref=theirs, send_sem=send_sems.at[b],
                                         recv_sem=recv_sems.at[b], device_id=(x, y, 1 - c),
                                         device_id_type=MESH).wait_recv()
            cp.wait_send()

    return _aliased_comm_call(body, ss, n, name=name)


_SMALL_SHARDED = (("e_conv_w", 2), ("o_norm", 1), ("o_d", 1))
_REPLICATED = ("e_norm", "e_gmlp_w", "e_gmlp_b", "e_conv_b", "e_conv_ln_g", "e_conv_ln_b", "o_lam_re", "o_lam_im",
               "o_log_dt", "o_b_re", "o_b_im", "o_c_re", "o_c_im", "ca_norm", "ca_mem_norm", "ffn_norm", "final_norm")
_SMALL = tuple(n for n, _ in _SMALL_SHARDED) + _REPLICATED
_WEIGHTS = ("e_norm", "e_w_in", "e_gmlp_w", "e_gmlp_b", "e_conv_w", "e_conv_b", "e_conv_ln_g", "e_conv_ln_b",
            "e_w_out", "o_norm", "o_w_in", "o_lam_re", "o_lam_im", "o_log_dt", "o_b_re", "o_b_im", "o_c_re", "o_c_im",
            "o_d", "o_w_out", "ca_norm", "ca_mem_norm", "ca_wq", "ca_wk", "ca_wv", "ca_wo", "ffn_norm", "ffn_w_gate",
            "ffn_w_up", "ffn_w_down", "final_norm")


def _pack_rows(arrs, width, dtype, row_mult=8):
    parts, spans, r0 = [], [], 0
    for a in arrs:
        flat = a.reshape(-1).astype(dtype)
        rows = -(-flat.shape[0] // (width * row_mult)) * row_mult
        if rows * width != flat.shape[0]:
            flat = jnp.pad(flat, (0, rows * width - flat.shape[0]))
        parts.append(flat.reshape(rows, width))
        spans.append((r0, rows))
        r0 += rows
    return jnp.concatenate(parts, axis=0), spans


def _unpack_rows(slab, spans, shapes):
    out = []
    for (r0, rows), shp in zip(spans, shapes):
        n = math.prod(shp)
        out.append(slab[r0:r0 + rows].reshape(-1)[:n].reshape(shp))
    return out


def _two_d(a):
    return a.reshape(-1, a.shape[-1])


def _local_slab(local, slab, dtype):
    names = sorted((n for n in _PLACE if _PLACE[n][0] == slab), key=lambda n: _PLACE[n][1])
    return jnp.concatenate([_two_d(local[n]).astype(dtype) for n in names], axis=0)


def _block_diag(b, pattern):
    return jnp.einsum(pattern, b, jnp.eye(C_GROUPS, dtype=b.dtype))


def _s5_discretize(lam_re, lam_im, log_dt, b_re, b_im):
    dt = jnp.exp(log_dt)[:, None]
    mag = jnp.exp(lam_re * dt)
    ar = mag * jnp.cos(lam_im * dt)
    ai = mag * jnp.sin(lam_im * dt)
    den = lam_re * lam_re + lam_im * lam_im
    qr = ((ar - 1.0) * lam_re + ai * lam_im) / den
    qi = (ai * lam_re - (ar - 1.0) * lam_im) / den
    bbr = qr[..., None] * b_re - qi[..., None] * b_im
    bbi = qr[..., None] * b_im + qi[..., None] * b_re
    return ar, ai, bbr, bbi


def _attention_block(x, mem, W, w, i, tag):
    xn, q = _norm_mm(x, w["ca_norm"][i], _shards(W, "ca_wq", i), split="k", out_dtype=BF16, name=f"{tag}_q")
    memn = _rms_fwd(mem, w["ca_mem_norm"][i], name=f"{tag}_ca_memnorm")
    k = _mm_k(memn, _shards(W, "ca_wk", i), out_dtype=BF16, name=f"{tag}_k")
    v = _mm_k(memn, _shards(W, "ca_wv", i), out_dtype=BF16, name=f"{tag}_v")
    o = _attn_fwd(q, k, v, name=f"{tag}_attn")
    y = _mm_k(o, _shards(W, "ca_wo", i), add=x, name=f"{tag}_wo")
    return y, (x, xn, memn, q, k, v, o)


def _attention_block_bwd(dy, saved, mem, W, w, i, tag, G, grads):
    x, xn, memn, q, k, v, o = saved
    G = _grad_to_slab(G, "ca_wo", i, o, dy, a_cols=256, name=f"{tag}_dwo")
    dq, dk, dv = _attn_bwd(dy, _shards(W, "ca_wo", i), q, k, v, name=f"{tag}_attn_bwd")
    G = _grad_to_slab(G, "ca_wq", i, xn, dq, a_cols=256, name=f"{tag}_dwq")
    G = _grad_to_slab(G, "ca_wk", i, memn, dk, a_cols=256, name=f"{tag}_dwk")
    G = _grad_to_slab(G, "ca_wv", i, memn, dv, a_cols=256, name=f"{tag}_dwv")
    dmemn = _mm_k_t([(dk, _shards(W, "ca_wk", i)), (dv, _shards(W, "ca_wv", i))], name=f"{tag}_dmemn")
    dx, dg = _norm_bwd_k(dq, _shards(W, "ca_wq", i), x, w["ca_norm"][i], dy, name=f"{tag}_dq_norm_bwd")
    grads["ca_norm"][i] = dg[0]
    grads["ca_mem_norm"][i] = _rms_dg(mem, w["ca_mem_norm"][i], dmemn, name=f"{tag}_ca_memnorm_bwd")[0]
    return dx, G


def _ffn_block(x, W, w, i, tag):
    fn, gate, up, h = _ffn_up(x, w["ffn_norm"][i], _shards(W, "ffn_w_gate", i), _shards(W, "ffn_w_up", i),
                              name=f"{tag}_ffn_up")
    y = _mm_k(h, _shards(W, "ffn_w_down", i), add=x, name=f"{tag}_down")
    return y, (x, fn, gate, up, h)


def _ffn_block_bwd(dy, saved, W, w, i, tag, G, grads):
    x, fn, gate, up, h = saved
    G = _grad_to_slab(G, "ffn_w_down", i, h, dy, name=f"{tag}_dwd")
    dg, du = _ffn_bwd_hidden(dy, _shards(W, "ffn_w_down", i), gate, up, name=f"{tag}_ffn_bwd_hidden")
    G = _grad_to_slab(G, "ffn_w_gate", i, fn, dg, name=f"{tag}_dwg")
    G = _grad_to_slab(G, "ffn_w_up", i, fn, du, name=f"{tag}_dwu")
    dx, dgn = _ffn_in_bwd(dg, du, _shards(W, "ffn_w_gate", i), _shards(W, "ffn_w_up", i), x, w["ffn_norm"][i], dy,
                          name=f"{tag}_ffn_in_bwd")
    grads["ffn_norm"][i] = dgn[0]
    return dx, G


def _gmlp_mask():
    chunk = jnp.arange(GMLP_BLOCK) // CHUNK
    return chunk[None, :] <= chunk[:, None]


def _even_block(x, W, w, tag):
    hn, proj = _norm_mm(x, w["e_norm"][0], _shards(W, "e_w_in"), split="n", out_dtype=F32, name=f"{tag}_w_in")
    wm = jnp.where(_gmlp_mask()[None], w["e_gmlp_w"][0], 0.0).astype(BF16)
    bcol = w["e_gmlp_b"][0][:, :, None]
    cw = jnp.pad(w["e_conv_w"][0], ((0, CONV_HALO - CONV_WIDTH), (0, 0)))
    cb, lg, lb = w["e_conv_b"], w["e_conv_ln_g"], w["e_conv_ln_b"]
    mix, hc = _even_fwd(proj, wm, bcol, cw, cb, lg, lb, name=f"{tag}_mixers")
    y = _mm_k(mix, _shards(W, "e_w_out"), add=x, name=f"{tag}_w_out")
    return y, (x, hn, proj, mix, hc, wm, bcol, cw)


def _even_block_bwd(dy, saved, W, w, tag, G, grads):
    x, hn, proj, mix, hc, wm, bcol, cw = saved
    dmix = _mm_k_t([(dy, _shards(W, "e_w_out"))], name=f"{tag}_dmix")
    G = _grad_to_slab(G, "e_w_out", 0, mix, dy, a_cols=256, name=f"{tag}_dw_out")
    wmt = jnp.swapaxes(wm, 1, 2)
    dpa, dhc, dwm, db, dlg, dlb, dcb = _even_bwd1(proj, dmix, hc, wm, wmt, bcol, w["e_conv_ln_g"], w["e_conv_ln_b"],
                                                  name=f"{tag}_mixers_bwd1")
    dpb, dcw = _even_bwd2(proj, dhc, cw, name=f"{tag}_mixers_bwd2")
    grads["e_gmlp_w"] = jnp.where(_gmlp_mask()[None], dwm, 0.0)[None]
    grads["e_gmlp_b"] = db[:, :, 0][None]
    grads["e_conv_ln_g"], grads["e_conv_ln_b"], grads["e_conv_b"] = dlg, dlb, dcb
    grads["e_conv_w"] = dcw[:CONV_WIDTH][None]
    G = _grad_to_slab(G, "e_w_in", 0, hn, dpa, b_cols=512, chips=(0, 2), name=f"{tag}_dw_in_a")
    G = _grad_to_slab(G, "e_w_in", 0, hn, dpb, b_cols=512, chips=(2, 2), name=f"{tag}_dw_in_b")
    dx, dg = _norm_bwd_n((dpa, dpb), _shards(W, "e_w_in"), x, w["e_norm"][0], dy, name=f"{tag}_in_bwd")
    grads["e_norm"] = dg
    return dx, G


def _odd_block(x, W, w, tag):
    S = x.shape[0]
    hn, u = _norm_mm(x, w["o_norm"][0], _shards(W, "o_w_in"), split="k", out_dtype=F32, name=f"{tag}_w_in")
    disc_in = (w["o_lam_re"][0], w["o_lam_im"][0], w["o_log_dt"][0], w["o_b_re"][0], w["o_b_im"][0])
    (ar, ai, bbr, bbi), disc_vjp = jax.vjp(_s5_discretize, *disc_in)
    bd = jnp.concatenate([_block_diag(bbr, "gpc,gh->gchp").reshape(C_WIDTH, N_STATE),
                          _block_diag(bbi, "gpc,gh->gchp").reshape(C_WIDTH, N_STATE)], axis=1).astype(BF16)
    cd = jnp.concatenate([_block_diag(w["o_c_re"][0], "gcp,gh->gphc").reshape(N_STATE, C_WIDTH),
                          -_block_diag(w["o_c_im"][0], "gcp,gh->gphc").reshape(N_STATE, C_WIDTH)], axis=0).astype(BF16)
    a = jnp.concatenate([ar.reshape(STATE_ROWS, STATE_LANES), ai.reshape(STATE_ROWS, STATE_LANES)], axis=0)
    bu = _mm_rows(u, bd, name=f"{tag}_bu")
    xs = _scan_fwd(bu.reshape(S, 2 * STATE_ROWS, STATE_LANES), a, name=f"{tag}_scan")
    yv, yg = _s5_readout(xs.reshape(S, 2 * N_STATE), cd, u, w["o_d"], name=f"{tag}_readout")
    o, y = _glu_out(yg, _shards(W, "o_w_out"), x, name=f"{tag}_glu_out")
    return y, (x, hn, u, bd, cd, a, xs, yv, yg, o, disc_vjp)


def _odd_block_bwd(dy, saved, W, w, tag, G, grads):
    x, hn, u, bd, cd, a, xs, yv, yg, o, disc_vjp = saved
    S = x.shape[0]
    do, dys, dus, dd = _glu_out_bwd(o, dy, _shards(W, "o_w_out"), yv, u, w["o_d"], name=f"{tag}_glu_out_bwd")
    G = _grad_to_slab(G, "o_w_out", 0, yg, do, b_cols=512, name=f"{tag}_dw_out")
    grads["o_d"] = dd
    xs2 = xs.reshape(S, 2 * N_STATE)
    dxs = _mm_rows(dys, cd, nt=True, name=f"{tag}_dxs")
    dcd = _mm_tn(xs2, dys, name=f"{tag}_dcd")
    gs, da = _scan_bwd(dxs.reshape(S, 2 * STATE_ROWS, STATE_LANES), xs, a, name=f"{tag}_scan_bwd")
    gs2 = gs.reshape(S, 2 * N_STATE)
    dbd = _mm_tn(u, gs2, name=f"{tag}_dbd")
    du, dx, dg = _s5_in_bwd(gs2, bd, dus, _shards(W, "o_w_in"), x, w["o_norm"][0], dy, name=f"{tag}_in_bwd")
    G = _grad_to_slab(G, "o_w_in", 0, hn, du, a_cols=256, name=f"{tag}_dw_in")
    grads["o_norm"] = dg
    eye = jnp.eye(C_GROUPS, dtype=F32)
    dcr = jnp.einsum("gphc,gh->gcp", dcd[:N_STATE].reshape(C_GROUPS, C_STATE, C_GROUPS, C_GROUP_CH), eye)
    dci = -jnp.einsum("gphc,gh->gcp", dcd[N_STATE:].reshape(C_GROUPS, C_STATE, C_GROUPS, C_GROUP_CH), eye)
    dbbr = jnp.einsum("gchp,gh->gpc", dbd[:, :N_STATE].reshape(C_GROUPS, C_GROUP_CH, C_GROUPS, C_STATE), eye)
    dbbi = jnp.einsum("gchp,gh->gpc", dbd[:, N_STATE:].reshape(C_GROUPS, C_GROUP_CH, C_GROUPS, C_STATE), eye)
    dar = da[:STATE_ROWS].reshape(C_GROUPS, C_STATE)
    dai = da[STATE_ROWS:].reshape(C_GROUPS, C_STATE)
    dlr, dli, dldt, dbr, dbi = disc_vjp((dar, dai, dbbr, dbbi))
    grads["o_lam_re"], grads["o_lam_im"], grads["o_log_dt"] = dlr[None], dli[None], dldt[None]
    grads["o_b_re"], grads["o_b_im"], grads["o_c_re"], grads["o_c_im"] = dbr[None], dbi[None], dcr[None], dci[None]
    return dx, G


def _forward_backward(xs_, mems_, tgt, W, w, G):
    x1, s_mix0 = _even_block(xs_, W, w, "l0")
    x2, s_att0 = _attention_block(x1, mems_, W, w, 0, "l0")
    x3, s_ffn0 = _ffn_block(x2, W, w, 0, "l0")
    x4, s_mix1 = _odd_block(x3, W, w, "l1")
    x5, s_att1 = _attention_block(x4, mems_, W, w, 1, "l1")
    x6, s_ffn1 = _ffn_block(x5, W, w, 1, "l1")
    dx, dfinal, loss_lanes = _loss_head(x6, w["final_norm"], tgt, name="loss_head")

    grads = {n: [None, None] for n in ("ca_norm", "ca_mem_norm", "ffn_norm")}
    grads["final_norm"] = dfinal[0]
    dx, G = _ffn_block_bwd(dx, s_ffn1, W, w, 1, "l1", G, grads)
    dx, G = _attention_block_bwd(dx, s_att1, mems_, W, w, 1, "l1", G, grads)
    dx, G = _odd_block_bwd(dx, s_mix1, W, w, "l1", G, grads)
    dx, G = _ffn_block_bwd(dx, s_ffn0, W, w, 0, "l0", G, grads)
    dx, G = _attention_block_bwd(dx, s_att0, mems_, W, w, 0, "l0", G, grads)
    dx, G = _even_block_bwd(dx, s_mix0, W, w, "l0", G, grads)
    for n in list(grads):
        if isinstance(grads[n], list):
            grads[n] = jnp.stack(grads[n], axis=0)
        grads[n] = grads[n].reshape(w[n].shape)
    return loss_lanes, dx, G, grads


def kernel(x, mem, e_norm, e_w_in, e_gmlp_w, e_gmlp_b, e_conv_w, e_conv_b, e_conv_ln_g, e_conv_ln_b, e_w_out, o_norm, o_w_in, o_lam_re, o_lam_im, o_log_dt, o_b_re, o_b_im, o_c_re, o_c_im, o_d, o_w_out, ca_norm, ca_mem_norm, ca_wq, ca_wk, ca_wv, ca_wo, ffn_norm, ffn_w_gate, ffn_w_up, ffn_w_down, final_norm, loss_target, m_e_norm, m_e_w_in, m_e_gmlp_w, m_e_gmlp_b, m_e_conv_w, m_e_conv_b, m_e_conv_ln_g, m_e_conv_ln_b, m_e_w_out, m_o_norm, m_o_w_in, m_o_lam_re, m_o_lam_im, m_o_log_dt, m_o_b_re, m_o_b_im, m_o_c_re, m_o_c_im, m_o_d, m_o_w_out, m_ca_norm, m_ca_mem_norm, m_ca_wq, m_ca_wk, m_ca_wv, m_ca_wo, m_ffn_norm, m_ffn_w_gate, m_ffn_w_up, m_ffn_w_down, m_final_norm, v_e_norm, v_e_w_in, v_e_gmlp_w, v_e_gmlp_b, v_e_conv_w, v_e_conv_b, v_e_conv_ln_g, v_e_conv_ln_b, v_e_w_out, v_o_norm, v_o_w_in, v_o_lam_re, v_o_lam_im, v_o_log_dt, v_o_b_re, v_o_b_im, v_o_c_re, v_o_c_im, v_o_d, v_o_w_out, v_ca_norm, v_ca_mem_norm, v_ca_wq, v_ca_wk, v_ca_wv, v_ca_wo, v_ffn_norm, v_ffn_w_gate, v_ffn_w_up, v_ffn_w_down, v_final_norm):
    args = dict(locals())
    local = {n: args[n] for n in _WEIGHTS}
    mom = {n: args["m_" + n] for n in _WEIGHTS}
    vel = {n: args["v_" + n] for n in _WEIGHTS}
    chip = 2 * lax.axis_index("x") + lax.axis_index("y")
    core = lax.axis_index("c")
    xs_, mems_, tgt = x[0], mem[0], loss_target[0]
    names = sorted(_SLABS)

    bufs = [lax.dynamic_update_slice(lax.empty((N_CHIPS, _SLABS[s][1], _SLABS[s][0]), BF16),
                                     _local_slab(local, s, BF16)[None], (chip, 0, 0)) for s in names]
    W = dict(zip(names, _allgather_chips(bufs, name="gather_weights")))
    w = {n: local[n] for n in _REPLICATED}
    sm_slab, sm_spans = _pack_rows([local[n] for n, _ in _SMALL_SHARDED], SMALL_W, F32)
    sm_all = _allgather_small(sm_slab, name="gather_small_weights").reshape(N_DEV, -1, SMALL_W)
    for (n, ax), span in zip(_SMALL_SHARDED, sm_spans):
        shp = local[n].shape
        w[n] = jnp.concatenate([_unpack_rows(sm_all[2 * p], [span], [shp])[0] for p in range(N_CHIPS)], axis=ax)

    G = {s: lax.empty((N_CHIPS, _SLABS[s][1], _SLABS[s][0]), F32) for s in names}
    loss_lanes, dx, G, grads = _forward_backward(xs_, mems_, tgt, W, w, G)

    gl = [G[s] for s in names]
    other = _pair_exchange(gl, name="grad_pair_exchange")
    half = core.reshape(1).astype(jnp.int32)
    pairs = [_pair_sum(g, r, half, name=f"grad_pair_sum_{s}") for s, g, r in zip(names, gl, other)]
    slots = _chip_exchange(pairs, name="grad_chip_exchange")
    where = jnp.stack([chip, core]).astype(jnp.int32)
    halves = [_chip_sum(g, r, sl, where, name=f"grad_chip_sum_{s}") for s, g, r, sl in zip(names, gl, other, slots)]
    gsum = dict(zip(names, _pair_share(halves, name="grad_pair_share")))

    gs_slab, gs_spans = _pack_rows([grads[n] for n in _SMALL], SMALL_W, F32)
    gs_all = _allgather_small(gs_slab, name="gather_small_grads").reshape(N_DEV, -1, SMALL_W)
    gs_sum = _sum_slots(gs_all, name="small_grad_sum")
    out_grads = dict(zip(_SMALL, _unpack_rows(gs_sum, gs_spans, [grads[n].shape for n in _SMALL])))
    for n, ax in _SMALL_SHARDED:
        width = local[n].shape[ax]
        out_grads[n] = lax.dynamic_slice_in_dim(out_grads[n], chip * width, width, axis=ax)

    delta, new_m, new_v = {}, {}, {}
    for n, (s, r0, rows, layers) in _PLACE.items():
        shp = local[n].shape
        g_, d_, m_, v_ = _adamw_shard(_two_d(local[n]), gsum[s], r0, _two_d(mom[n]), _two_d(vel[n]), name=f"adamw_{n}")
        out_grads[n], delta[n], new_m[n], new_v[n] = g_.reshape(shp), d_.reshape(shp), m_.reshape(shp), v_.reshape(shp)
    d_, m_, v_ = _adamw_small([_two_d(local[n]) for n in _SMALL], [_two_d(out_grads[n]) for n in _SMALL],
                              [_two_d(mom[n]) for n in _SMALL], [_two_d(vel[n]) for n in _SMALL], name="adamw_small")
    for n, dd, mm_, vv in zip(_SMALL, d_, m_, v_):
        shp = local[n].shape
        delta[n], new_m[n], new_v[n] = dd.reshape(shp), mm_.reshape(shp), vv.reshape(shp)

    loss = lax.psum(loss_lanes[0, 0], ("x", "y", "c"))
    return (loss, dx[None], *[out_grads[n] for n in _WEIGHTS], *[delta[n] for n in _WEIGHTS],
            *[new_m[n] for n in _WEIGHTS], *[new_v[n] for n in _WEIGHTS])
```

```python
import functools
import math

import jax
import jax.numpy as jnp
from jax import lax
from jax.experimental import pallas as pl
from jax.experimental.pallas import tpu as pltpu

F32 = jnp.float32
BF16 = jnp.bfloat16
MESH = pl.DeviceIdType.MESH

EPS = 1e-6
D_MODEL = 1024
A_WIDTH = 512
A_GROUPS = 4
GMLP_BLOCK = 128
CHUNK = 64
B_WIDTH = 512
CONV_WIDTH = 31
CONV_HALO = 32
C_WIDTH = 512
C_GROUP_CH = 16
C_GROUPS = 32
C_STATE = 64
N_STATE = C_GROUPS * C_STATE
STATE_ROWS = 8
STATE_LANES = N_STATE // STATE_ROWS
CA_HEADS = 4
CA_HEAD_DIM = 256
FFN_HIDDEN = 2816

ADAM_LR = 0.001
ADAM_B1 = 0.9
ADAM_B2 = 0.999
ADAM_EPS = 1e-08
ADAM_WD = 0.01
ADAM_STEP = 10

VMEM_LIMIT = 56 * 1024 * 1024
ACC_BYTES = 6 * 1024 * 1024
TN_VMEM_BYTES = 44 * 1024 * 1024
SMALL_W = 128
N_CHIPS = 4
N_DEV = 8

_SLABS = {"D0": (512, 1024), "E0": (1024, 256), "A0": (1024, 1024), "B0": (1024, 704), "C0": (704, 2048),
          "D1": (512, 768), "A1": (1024, 1024), "B1": (1024, 704), "C1": (704, 2048)}
_STAGES = (("D0", "E0"), ("A0", "B0", "C0"), ("D1", "A1", "B1", "C1"))
_PLACE = {
    "e_w_in": (1024, (("D0", 0),)), "e_w_out": (256, (("E0", 0),)),
    "o_w_out": (512, (("D1", 0),)), "o_w_in": (256, (("D1", 512),)),
    "ca_wq": (256, (("A0", 0), ("A1", 0))), "ca_wk": (256, (("A0", 256), ("A1", 256))),
    "ca_wv": (256, (("A0", 512), ("A1", 512))), "ca_wo": (256, (("A0", 768), ("A1", 768))),
    "ffn_w_down": (704, (("B0", 0), ("B1", 0))),
    "ffn_w_gate": (1024, (("C0", 0), ("C1", 0))), "ffn_w_up": (1024, (("C0", 1024), ("C1", 1024))),
}


def _params(sem=None):
    return pltpu.CompilerParams(dimension_semantics=sem, vmem_limit_bytes=VMEM_LIMIT)


def _tile(n, pref, mult=128):
    if n <= pref:
        return n
    t = (pref // mult) * mult
    while t >= mult:
        if n % t == 0:
            return t
        t -= mult
    return n


def _blk(name, layer=0):
    rows, where = _PLACE[name]
    slab, r0 = where[layer]
    assert r0 % rows == 0
    return slab, rows, r0 // rows


def _shards(slabs, name, layer=0):
    slab, rows, b = _blk(name, layer)
    return [(slabs[slab], (None, rows, _SLABS[slab][0]), (p, b, 0)) for p in range(N_CHIPS)]


_GELU_C = 0.7978845608028654
_GELU_A = 0.044715


def _gelu(x):
    t = jnp.tanh(_GELU_C * (x + _GELU_A * (x * x * x)))
    return 0.5 * x * (1.0 + t), t


def _gelu_grad(x, t):
    return 0.5 * (1.0 + t) + 0.5 * x * (1.0 - t * t) * (_GELU_C * (1.0 + 3.0 * _GELU_A * x * x))


def _sigmoid(x):
    return 1.0 / (1.0 + jnp.exp(-x))


def _mean(x):
    return jnp.mean(x, axis=-1, keepdims=True)


def _dot(a, b):
    return jnp.dot(a, b, preferred_element_type=F32)


def _dot_nt(a, b):
    return lax.dot_general(a, b, (((1,), (1,)), ((), ())), preferred_element_type=F32)


def _dot_tn(a, b):
    return lax.dot_general(a, b, (((0,), (0,)), ((), ())), preferred_element_type=F32)


def _rms_tile(xv, gv):
    return (xv * lax.rsqrt(_mean(xv * xv) + EPS)) * gv


def _rms_bwd_tile(xv, gv, dyv):
    r = lax.rsqrt(_mean(xv * xv) + EPS)
    xh = xv * r
    dyg = dyv * gv
    return r * (dyg - xh * _mean(dyg * xh)), jnp.sum(dyv * xh, axis=0, keepdims=True)


def _cols(p, width):
    return slice(p * width, (p + 1) * width)


def _sum_k(a, ws, k):
    tot = None
    for p in range(N_CHIPS):
        y = _dot(a[:, _cols(p, k)], ws[p][...])
        tot = y if tot is None else tot + y
    return tot


def _cat_nt(a, ws):
    return jnp.concatenate([_dot_nt(a, ws[p][...]) for p in range(N_CHIPS)], axis=1)


def _rows_call(name, tm, rows, fulls, outs, accs, body, scratch=()):
    S = rows[0].shape[-2]
    nr, nf, no, na = len(rows), len(fulls), len(outs), len(accs)

    def kern(*refs):
        r, f = refs[:nr], refs[nr:nr + nf]
        o, a = refs[nr + nf:nr + nf + no], refs[nr + nf + no:nr + nf + no + na]
        if na:
            @pl.when(pl.program_id(0) == 0)
            def _():
                for ref in a:
                    ref[...] = jnp.zeros_like(ref)
        body(r, f, o, a, refs[nr + nf + no + na:])

    def whole(shape):
        nd = len(shape)
        return pl.BlockSpec(tuple(shape), lambda i: (0,) * nd)

    def row_spec(shape):
        if len(shape) == 3:
            return pl.BlockSpec((shape[0], tm, shape[2]), lambda i: (0, i, 0))
        return pl.BlockSpec((tm, shape[1]), lambda i: (i, 0))

    def full_spec(x):
        if isinstance(x, tuple):
            _, bshape, bidx = x
            return pl.BlockSpec(bshape, lambda i: bidx, pipeline_mode=pl.Buffered(1))
        return whole(x.shape)

    out_shapes = [(S, o[0]) if len(o) == 2 else (o[0], S, o[1]) for o in outs]
    res = pl.pallas_call(
        kern, name=name, grid=(S // tm,),
        in_specs=[row_spec(x.shape) for x in rows] + [full_spec(x) for x in fulls],
        out_specs=[row_spec(s) for s in out_shapes] + [whole(shp) for shp, _ in accs],
        out_shape=[jax.ShapeDtypeStruct(s, o[-1]) for s, o in zip(out_shapes, outs)]
        + [jax.ShapeDtypeStruct(tuple(shp), dt) for shp, dt in accs],
        scratch_shapes=list(scratch),
        compiler_params=_params(("arbitrary",) if na else ("parallel",)),
    )(*rows, *[x[0] if isinstance(x, tuple) else x for x in fulls])
    return res[:no], res[no:]


def _mm_rows(a, w, *, nt=False, out_dtype=F32, tm=512, name):
    S = a.shape[0]
    N = w.shape[0] if nt else w.shape[1]

    def body(r, f, o, acc, s):
        av = r[0][...].astype(BF16)
        o[0][...] = (_dot_nt(av, f[0][...]) if nt else _dot(av, f[0][...])).astype(out_dtype)

    (y,), _ = _rows_call(name, _tile(S, tm), [a], [w], [(N, out_dtype)], [], body)
    return y


def _mm_tn(a, b, *, name):
    S, K1 = a.shape
    N = b.shape[1]
    tn = _tile(N, max(128, (ACC_BYTES // (4 * K1)) // 128 * 128))
    ts = _tile(S, 512 if K1 * a.dtype.itemsize * 512 <= 4 * 1024 * 1024 else 256)

    def body(a_ref, b_ref, o_ref):
        @pl.when(pl.program_id(1) == 0)
        def _():
            o_ref[...] = jnp.zeros_like(o_ref)

        o_ref[...] += _dot_tn(a_ref[...].astype(BF16), b_ref[...].astype(BF16))

    return pl.pallas_call(
        body, name=name, grid=(N // tn, S // ts),
        in_specs=[pl.BlockSpec((ts, K1), lambda j, s: (s, 0)), pl.BlockSpec((ts, tn), lambda j, s: (s, j))],
        out_specs=pl.BlockSpec((K1, tn), lambda j, s: (0, j)),
        out_shape=jax.ShapeDtypeStruct((K1, N), F32),
        compiler_params=_params(("parallel", "arbitrary")),
    )(a, b)


def _grad_to_slab(gslabs, wname, layer, a, b, *, a_cols=None, b_cols=None, chips=(0, N_CHIPS), name):
    slab, rows, bidx = _blk(wname, layer)
    width = _SLABS[slab][0]
    p0, n_p = chips
    assert p0 % n_p == 0
    S = a.shape[-2]

    def tile_bytes(x, ts):
        return ts * x.dtype.itemsize * (x.shape[2] * n_p if x.ndim == 3 else x.shape[1])

    acc_bytes = n_p * rows * (-(-width // 128) * 128) * 4
    ts = next(t for t in (2048, 1024, 512, 256, S) if S % t == 0
              and 2 * (tile_bytes(a, t) + tile_bytes(b, t) + acc_bytes) <= TN_VMEM_BYTES or t == S)

    def operand(x):
        if x.ndim == 3:
            return pl.BlockSpec((n_p, ts, x.shape[2]), lambda s: (p0 // n_p, s, 0))
        return pl.BlockSpec((ts, x.shape[1]), lambda s: (s, 0))

    def part(ref, cols, p):
        if len(ref.shape) == 3:
            return ref[p]
        return ref[...] if cols is None else ref[:, _cols(p, cols)]

    def body(a_ref, b_ref, slab_ref, o_ref):
        @pl.when(pl.program_id(0) == 0)
        def _():
            o_ref[...] = jnp.zeros_like(o_ref)

        for p in range(n_p):
            o_ref[p] += _dot_tn(part(a_ref, a_cols, p).astype(BF16), part(b_ref, b_cols, p).astype(BF16))

    g = gslabs[slab]
    out = pl.pallas_call(
        body, name=name, grid=(S // ts,),
        in_specs=[operand(a), operand(b), pl.BlockSpec(memory_space=pl.ANY)],
        out_specs=pl.BlockSpec((n_p, rows, width), lambda s: (p0 // n_p, bidx, 0)),
        out_shape=jax.ShapeDtypeStruct(g.shape, F32), input_output_aliases={2: 0},
        compiler_params=_params(("arbitrary",)),
    )(a, b, g)
    return {**gslabs, slab: out}


def _vec(g):
    return g.reshape(1, -1)


def _norm_mm(x, g, ws, *, split, out_dtype, name, tm=512):
    S, D = x.shape
    k, n = ws[0][1][1], ws[0][1][2]
    N = n if split == "k" else N_CHIPS * n

    def body(r, f, o, acc, s):
        xn = _rms_tile(r[0][...], f[0][...]).astype(BF16)
        o[0][...] = xn
        if split == "k":
            o[1][...] = _sum_k(xn, f[1:], k).astype(out_dtype)
        else:
            for p in range(N_CHIPS):
                o[1][:, _cols(p, n)] = _dot(xn, f[1 + p][...]).astype(out_dtype)

    (xn, y), _ = _rows_call(name, _tile(S, tm), [x], [_vec(g)] + ws, [(D, BF16), (N, out_dtype)], [], body)
    return xn, y


def _mm_k(a, ws, *, add=None, out_dtype=F32, name, tm=512):
    S = a.shape[-2]
    k, n = ws[0][1][1], ws[0][1][2]
    has_add = add is not None

    def body(r, f, o, acc, s):
        if a.ndim == 3:
            y = None
            for p in range(N_CHIPS):
                t = _dot(r[0][p].astype(BF16), f[p][...])
                y = t if y is None else y + t
        else:
            y = _sum_k(r[0][...].astype(BF16), f, k)
        if has_add:
            y = y + r[1][...]
        o[0][...] = y.astype(out_dtype)

    (y,), _ = _rows_call(name, _tile(S, tm), [a] + ([add] if has_add else []), ws, [(n, out_dtype)], [], body)
    return y


def _mm_k_t(terms, *, out_dtype=F32, name, tm=512):
    S = terms[0][0].shape[0]
    k = terms[0][1][0][1][1]

    def body(r, f, o, acc, s):
        y = None
        for t in range(len(terms)):
            yt = _cat_nt(r[t][...].astype(BF16), f[N_CHIPS * t:N_CHIPS * (t + 1)])
            y = yt if y is None else y + yt
        o[0][...] = y.astype(out_dtype)

    (y,), _ = _rows_call(name, _tile(S, tm), [a for a, _ in terms], [w for _, ws in terms for w in ws],
                         [(N_CHIPS * k, out_dtype)], [], body)
    return y


def _rms_fwd(x, g, *, name):
    def body(r, f, o, acc, s):
        o[0][...] = _rms_tile(r[0][...], f[0][...]).astype(BF16)

    (y,), _ = _rows_call(name, _tile(x.shape[0], 256, 8), [x], [_vec(g)], [(x.shape[1], BF16)], [], body)
    return y


def _rms_dg(x, g, dy, *, name):
    def body(r, f, o, acc, s):
        acc[0][...] += _rms_bwd_tile(r[0][...], f[0][...], r[1][...])[1]

    _, (dg,) = _rows_call(name, _tile(x.shape[0], 256, 8), [x, dy], [_vec(g)], [], [((1, x.shape[1]), F32)], body)
    return dg


def _ffn_up(x, g, wg, wu, *, name, tm=256):
    S, D = x.shape
    h = wg[0][1][2]

    def body(r, f, o, acc, s):
        xn = _rms_tile(r[0][...], f[0][...]).astype(BF16)
        o[0][...] = xn
        for p in range(N_CHIPS):
            gate = _dot(xn, f[1 + p][...])
            up = _dot(xn, f[1 + N_CHIPS + p][...])
            o[1][p] = gate.astype(BF16)
            o[2][p] = up.astype(BF16)
            o[3][p] = (gate * _sigmoid(gate) * up).astype(BF16)

    (xn, gate, up, hid), _ = _rows_call(name, _tile(S, tm), [x], [_vec(g)] + wg + wu,
                                        [(D, BF16), (N_CHIPS, h, BF16), (N_CHIPS, h, BF16), (N_CHIPS, h, BF16)], [],
                                        body)
    return xn, gate, up, hid


def _ffn_bwd_hidden(dy, wd, gate, up, *, name, tm=256):
    S = dy.shape[0]
    h = wd[0][1][1]

    def body(r, f, o, acc, s):
        dyb = r[0][...].astype(BF16)
        for p in range(N_CHIPS):
            dh = _dot_nt(dyb, f[p][...])
            gv = r[1][p].astype(F32)
            sg = _sigmoid(gv)
            o[0][p] = (dh * r[2][p].astype(F32) * (sg * (1.0 + gv * (1.0 - sg)))).astype(BF16)
            o[1][p] = (dh * gv * sg).astype(BF16)

    (dg, du), _ = _rows_call(name, _tile(S, tm), [dy, gate, up], wd, [(N_CHIPS, h, BF16), (N_CHIPS, h, BF16)], [],
                             body)
    return dg, du


def _ffn_in_bwd(dg, du, wg, wu, x, g, dres, *, name, tm=256):
    S, D = x.shape

    def body(r, f, o, acc, s):
        tot = None
        for p in range(N_CHIPS):
            y = _dot_nt(r[0][p], f[1 + p][...]) + _dot_nt(r[1][p], f[1 + N_CHIPS + p][...])
            tot = y if tot is None else tot + y
        dx, dgn = _rms_bwd_tile(r[2][...], f[0][...], tot)
        o[0][...] = dx + r[3][...]
        acc[0][...] += dgn

    (dx,), (dgn,) = _rows_call(name, _tile(S, tm), [dg, du, x, dres], [_vec(g)] + wg + wu, [(D, F32)],
                               [((1, D), F32)], body)
    return dx, dgn


def _norm_bwd_k(da, ws, x, g, dres, *, name, tm=512):
    S, D = x.shape

    def body(r, f, o, acc, s):
        dx, dg = _rms_bwd_tile(r[1][...], f[0][...], _cat_nt(r[0][...].astype(BF16), f[1:]))
        o[0][...] = dx + r[2][...]
        acc[0][...] += dg

    (dx,), (dg,) = _rows_call(name, _tile(S, tm), [da, x, dres], [_vec(g)] + ws, [(D, F32)], [((1, D), F32)], body)
    return dx, dg


def _norm_bwd_n(das, ws, x, g, dres, *, name, tm=256):
    S, D = x.shape
    n = ws[0][1][2]

    def body(r, f, o, acc, s):
        tot = None
        for p in range(N_CHIPS):
            y = _dot_nt(r[p // 2][:, _cols(p % 2, n)], f[1 + p][...])
            tot = y if tot is None else tot + y
        dx, dg = _rms_bwd_tile(r[2][...], f[0][...], tot)
        o[0][...] = dx + r[3][...]
        acc[0][...] += dg

    (dx,), (dg,) = _rows_call(name, _tile(S, tm), list(das) + [x, dres], [_vec(g)] + ws, [(D, F32)], [((1, D), F32)],
                              body)
    return dx, dg


def _ln_stats(v):
    mu = _mean(v)
    xc = v - mu
    rstd = lax.rsqrt(_mean(xc * xc) + EPS)
    return xc * rstd, rstd


def _even_fwd(proj, wm, bcol, cw, cb, lg, lb, *, name):
    S = proj.shape[0]
    tm = _tile(S, 256)
    hb = tm // CONV_HALO
    nblk = tm // GMLP_BLOCK

    def body(p_ref, halo_ref, wm_ref, b_ref, cw_ref, cb_ref, lg_ref, lb_ref, mix_ref, hc_ref, hext_ref):
        i = pl.program_id(0)
        gu, _ = _gelu(p_ref[:, 0:A_WIDTH])
        gv, _ = _gelu(p_ref[:, A_WIDTH:2 * A_WIDTH])
        vn, _ = _ln_stats(gv)
        vnb = vn.astype(BF16)
        for n in range(nblk):
            rows = slice(n * GMLP_BLOCK, (n + 1) * GMLP_BLOCK)
            for g in range(A_GROUPS):
                cols = slice(g * GMLP_BLOCK, (g + 1) * GMLP_BLOCK)
                sg = jnp.dot(wm_ref[g], vnb[rows, cols], preferred_element_type=F32) + b_ref[g]
                mix_ref[rows, cols] = (gu[rows, cols] * sg).astype(BF16)
        h = p_ref[:, 1024:1536] * _sigmoid(p_ref[:, 1536:2048])
        hh = halo_ref[:, 0:B_WIDTH] * _sigmoid(halo_ref[:, B_WIDTH:2 * B_WIDTH])
        hext_ref[0:CONV_HALO, :] = jnp.where(i > 0, hh, 0.0)
        hext_ref[CONV_HALO:CONV_HALO + tm, :] = h
        acc = jnp.zeros((tm, B_WIDTH), F32)
        for k in range(CONV_WIDTH):
            acc = acc + cw_ref[k:k + 1, :] * hext_ref[pl.ds(k + CONV_HALO - CONV_WIDTH + 1, tm), :]
        hc = acc + cb_ref[...]
        hc_ref[...] = hc
        hhat, _ = _ln_stats(hc)
        hl = hhat * lg_ref[...] + lb_ref[...]
        mix_ref[:, A_WIDTH:A_WIDTH + B_WIDTH] = (hl * _sigmoid(hl)).astype(BF16)

    vec = pl.BlockSpec((1, B_WIDTH), lambda i: (0, 0))
    return pl.pallas_call(
        body, name=name, grid=(S // tm,),
        in_specs=[
            pl.BlockSpec((tm, 2048), lambda i: (i, 0)),
            pl.BlockSpec((CONV_HALO, 1024), lambda i: (jnp.maximum(i * hb - 1, 0), 1)),
            pl.BlockSpec((A_GROUPS, GMLP_BLOCK, GMLP_BLOCK), lambda i: (0, 0, 0)),
            pl.BlockSpec((A_GROUPS, GMLP_BLOCK, 1), lambda i: (0, 0, 0)),
            pl.BlockSpec((CONV_HALO, B_WIDTH), lambda i: (0, 0)),
            vec, vec, vec,
        ],
        out_specs=[pl.BlockSpec((tm, 1024), lambda i: (i, 0)), pl.BlockSpec((tm, B_WIDTH), lambda i: (i, 0))],
        out_shape=[jax.ShapeDtypeStruct((S, 1024), BF16), jax.ShapeDtypeStruct((S, B_WIDTH), F32)],
        scratch_shapes=[pltpu.VMEM((tm + CONV_HALO, B_WIDTH), F32)],
        compiler_params=_params(("parallel",)),
    )(proj, proj, wm, bcol, cw, cb, lg, lb)


def _even_bwd1(proj, dmix, hc, wm, wmt, bcol, lg, lb, *, name):
    S = proj.shape[0]
    tm = _tile(S, 256)
    nblk = tm // GMLP_BLOCK

    def body(p_ref, dm_ref, hc_ref, wm_ref, wmt_ref, b_ref, lg_ref, lb_ref,
             dpa_ref, dhc_ref, dwm_ref, db_ref, dlg_ref, dlb_ref, dcb_ref, dgu_ref, dvn_ref):
        @pl.when(pl.program_id(0) == 0)
        def _():
            dwm_ref[...] = jnp.zeros_like(dwm_ref)
            db_ref[...] = jnp.zeros_like(db_ref)
            dlg_ref[...] = jnp.zeros_like(dlg_ref)
            dlb_ref[...] = jnp.zeros_like(dlb_ref)
            dcb_ref[...] = jnp.zeros_like(dcb_ref)

        au = p_ref[:, 0:A_WIDTH]
        av = p_ref[:, A_WIDTH:2 * A_WIDTH]
        gu, tu = _gelu(au)
        gv, tv = _gelu(av)
        vn, rstd = _ln_stats(gv)
        vnb = vn.astype(BF16)
        for n in range(nblk):
            rows = slice(n * GMLP_BLOCK, (n + 1) * GMLP_BLOCK)
            for g in range(A_GROUPS):
                cols = slice(g * GMLP_BLOCK, (g + 1) * GMLP_BLOCK)
                vb = vnb[rows, cols]
                sg = jnp.dot(wm_ref[g], vb, preferred_element_type=F32) + b_ref[g]
                da = dm_ref[rows, cols]
                dsg = da * gu[rows, cols]
                dgu_ref[rows, cols] = da * sg
                dsgb = dsg.astype(BF16)
                dwm_ref[g] += _dot_nt(dsgb, vb)
                db_ref[g] += jnp.sum(dsg, axis=1, keepdims=True)
                dvn_ref[rows, cols] = jnp.dot(wmt_ref[g], dsgb, preferred_element_type=F32)
        dvn = dvn_ref[...]
        dgv = rstd * (dvn - _mean(dvn) - vn * _mean(dvn * vn))
        dpa_ref[:, 0:A_WIDTH] = (dgu_ref[...] * _gelu_grad(au, tu)).astype(BF16)
        dpa_ref[:, A_WIDTH:2 * A_WIDTH] = (dgv * _gelu_grad(av, tv)).astype(BF16)
        hhat, rstd2 = _ln_stats(hc_ref[...])
        lgv = lg_ref[...]
        hl = hhat * lgv + lb_ref[...]
        s = _sigmoid(hl)
        dhl = dm_ref[:, A_WIDTH:A_WIDTH + B_WIDTH] * (s * (1.0 + hl * (1.0 - s)))
        dlg_ref[...] += jnp.sum(dhl * hhat, axis=0, keepdims=True)
        dlb_ref[...] += jnp.sum(dhl, axis=0, keepdims=True)
        dhh = dhl * lgv
        dhc = rstd2 * (dhh - _mean(dhh) - hhat * _mean(dhh * hhat))
        dcb_ref[...] += jnp.sum(dhc, axis=0, keepdims=True)
        dhc_ref[...] = dhc

    vec = pl.BlockSpec((1, B_WIDTH), lambda i: (0, 0))
    w3 = pl.BlockSpec((A_GROUPS, GMLP_BLOCK, GMLP_BLOCK), lambda i: (0, 0, 0))
    b3 = pl.BlockSpec((A_GROUPS, GMLP_BLOCK, 1), lambda i: (0, 0, 0))
    return pl.pallas_call(
        body, name=name, grid=(S // tm,),
        in_specs=[
            pl.BlockSpec((tm, 1024), lambda i: (i, 0)),
            pl.BlockSpec((tm, 1024), lambda i: (i, 0)),
            pl.BlockSpec((tm, B_WIDTH), lambda i: (i, 0)),
            w3, w3, b3, vec, vec,
        ],
        out_specs=[pl.BlockSpec((tm, 1024), lambda i: (i, 0)), pl.BlockSpec((tm, B_WIDTH), lambda i: (i, 0)),
                   w3, b3, vec, vec, vec],
        out_shape=[
            jax.ShapeDtypeStruct((S, 1024), BF16), jax.ShapeDtypeStruct((S, B_WIDTH), F32),
            jax.ShapeDtypeStruct((A_GROUPS, GMLP_BLOCK, GMLP_BLOCK), F32),
            jax.ShapeDtypeStruct((A_GROUPS, GMLP_BLOCK, 1), F32),
            jax.ShapeDtypeStruct((1, B_WIDTH), F32), jax.ShapeDtypeStruct((1, B_WIDTH), F32),
            jax.ShapeDtypeStruct((1, B_WIDTH), F32),
        ],
        scratch_shapes=[pltpu.VMEM((tm, A_WIDTH), F32), pltpu.VMEM((tm, A_WIDTH), F32)],
        compiler_params=_params(("arbitrary",)),
    )(proj, dmix, hc, wm, wmt, bcol, lg, lb)


def _even_bwd2(proj, dhc, cw, *, name):
    S = proj.shape[0]
    tm = _tile(S, 256)
    hb = tm // CONV_HALO
    nt = S // tm
    last_halo = S // CONV_HALO - 1
    lo = CONV_HALO - CONV_WIDTH + 1

    def body(p_ref, halo_ref, d_ref, dnext_ref, cw_ref, dpb_ref, dcw_ref, hext_ref, dext_ref):
        i = pl.program_id(0)

        @pl.when(i == 0)
        def _():
            dcw_ref[...] = jnp.zeros_like(dcw_ref)

        ba = p_ref[:, 0:B_WIDTH]
        sg = _sigmoid(p_ref[:, B_WIDTH:2 * B_WIDTH])
        hh = halo_ref[:, 0:B_WIDTH] * _sigmoid(halo_ref[:, B_WIDTH:2 * B_WIDTH])
        hext_ref[0:CONV_HALO, :] = jnp.where(i > 0, hh, 0.0)
        hext_ref[CONV_HALO:CONV_HALO + tm, :] = ba * sg
        dhc_t = d_ref[...]
        dext_ref[0:tm, :] = dhc_t
        dext_ref[tm:tm + CONV_HALO, :] = jnp.where(i < nt - 1, dnext_ref[...], 0.0)
        dh = jnp.zeros((tm, B_WIDTH), F32)
        for k in range(CONV_WIDTH):
            dh = dh + cw_ref[k:k + 1, :] * dext_ref[pl.ds(CONV_WIDTH - 1 - k, tm), :]
            dcw_ref[k:k + 1, :] += jnp.sum(dhc_t * hext_ref[pl.ds(k + lo, tm), :], axis=0, keepdims=True)
        dpb_ref[:, 0:B_WIDTH] = (dh * sg).astype(BF16)
        dpb_ref[:, B_WIDTH:2 * B_WIDTH] = (dh * ba * sg * (1.0 - sg)).astype(BF16)

    return pl.pallas_call(
        body, name=name, grid=(nt,),
        in_specs=[
            pl.BlockSpec((tm, 1024), lambda i: (i, 1)),
            pl.BlockSpec((CONV_HALO, 1024), lambda i: (jnp.maximum(i * hb - 1, 0), 1)),
            pl.BlockSpec((tm, B_WIDTH), lambda i: (i, 0)),
            pl.BlockSpec((CONV_HALO, B_WIDTH), lambda i: (jnp.minimum((i + 1) * hb, last_halo), 0)),
            pl.BlockSpec((CONV_HALO, B_WIDTH), lambda i: (0, 0)),
        ],
        out_specs=[pl.BlockSpec((tm, 1024), lambda i: (i, 0)), pl.BlockSpec((CONV_HALO, B_WIDTH), lambda i: (0, 0))],
        out_shape=[jax.ShapeDtypeStruct((S, 1024), BF16), jax.ShapeDtypeStruct((CONV_HALO, B_WIDTH), F32)],
        scratch_shapes=[pltpu.VMEM((tm + CONV_HALO, B_WIDTH), F32), pltpu.VMEM((tm + CONV_HALO, B_WIDTH), F32)],
        compiler_params=_params(("arbitrary",)),
    )(proj, proj, dhc, dhc, cw)


_CA_SCALE = CA_HEAD_DIM ** -0.5


def _softmax_rows(s):
    e = jnp.exp(s - jnp.max(s, axis=-1, keepdims=True))
    return e / jnp.sum(e, axis=-1, keepdims=True)


def _attn_fwd(q, k, v, *, name):
    S = q.shape[0]

    def body(r, f, o, acc, s):
        for h in range(CA_HEADS):
            cols = _cols(h, CA_HEAD_DIM)
            p = _softmax_rows(_dot_nt(r[0][:, cols], f[0][:, cols]) * _CA_SCALE)
            o[0][:, cols] = _dot(p.astype(BF16), f[1][:, cols]).astype(BF16)

    (o_,), _ = _rows_call(name, _tile(S, 512), [q], [k, v], [(D_MODEL, BF16)], [], body)
    return o_


def _attn_bwd(dy, wo, q, k, v, *, name):
    S = q.shape[0]
    M = k.shape[0]

    def body(r, f, o, acc, s):
        dyb = r[0][...].astype(BF16)
        for h in range(CA_HEADS):
            cols = _cols(h, CA_HEAD_DIM)
            qh = r[1][:, cols]
            kh = f[0][:, cols]
            vh = f[1][:, cols]
            doh = _dot_nt(dyb, f[2 + h][...]).astype(BF16)
            p = _softmax_rows(_dot_nt(qh, kh) * _CA_SCALE)
            acc[1][:, cols] += _dot_tn(p.astype(BF16), doh)
            dp = _dot_nt(doh, vh)
            ds = (p * (dp - jnp.sum(dp * p, axis=-1, keepdims=True)) * _CA_SCALE).astype(BF16)
            o[0][:, cols] = _dot(ds, kh).astype(BF16)
            acc[0][:, cols] += _dot_tn(ds, qh)

    (dq,), (dk, dv) = _rows_call(name, _tile(S, 512), [dy, q], [k, v] + wo, [(D_MODEL, BF16)],
                                 [((M, D_MODEL), F32), ((M, D_MODEL), F32)], body)
    return dq, dk, dv


def _s5_readout(xs2, cd, u, d, *, name, tm=256):
    def body(r, f, o, acc, s):
        y = _dot(r[0][...].astype(BF16), f[0][...]) + f[1][...] * r[1][...]
        o[0][...] = y
        o[1][...] = _gelu(y)[0].astype(BF16)

    (y, yg), _ = _rows_call(name, _tile(xs2.shape[0], tm), [xs2, u], [cd, d], [(C_WIDTH, F32), (C_WIDTH, BF16)], [],
                            body)
    return y, yg


def _glu_out(yg, ws, x, *, name, tm=512):
    n = ws[0][1][2]

    def body(r, f, o, acc, s):
        ygv = r[0][...]
        ov = [_dot(ygv, f[p][...]) for p in range(N_CHIPS)]
        for p in range(N_CHIPS):
            o[0][:, _cols(p, n)] = ov[p].astype(BF16)
        for p in range(2):
            o[1][:, _cols(p, n)] = r[1][:, _cols(p, n)] + ov[p] * _sigmoid(ov[2 + p])

    (o_, y), _ = _rows_call(name, _tile(x.shape[0], tm), [yg, x], ws, [(2 * D_MODEL, BF16), (D_MODEL, F32)], [], body)
    return o_, y


def _glu_out_bwd(o_, dy, ws, y, u, d, *, name, tm=256):
    n = ws[0][1][2]

    def body(r, f, o, acc, s):
        o1 = r[0][:, 0:D_MODEL].astype(F32)
        sg = _sigmoid(r[0][:, D_MODEL:2 * D_MODEL].astype(F32))
        dyv = r[1][...]
        do1 = (dyv * sg).astype(BF16)
        do2 = (dyv * o1 * sg * (1.0 - sg)).astype(BF16)
        o[0][:, 0:D_MODEL] = do1
        o[0][:, D_MODEL:2 * D_MODEL] = do2
        dyg = None
        for p in range(N_CHIPS):
            t = _dot_nt((do1 if p < 2 else do2)[:, _cols(p % 2, n)], f[1 + p][...])
            dyg = t if dyg is None else dyg + t
        yv = r[2][...]
        dys = dyg * _gelu_grad(yv, _gelu(yv)[1])
        o[1][...] = dys.astype(BF16)
        o[2][...] = f[0][...] * dys
        acc[0][...] += jnp.sum(dys * r[3][...], axis=0, keepdims=True)

    (do, dys, dus), (dd,) = _rows_call(name, _tile(dy.shape[0], tm), [o_, dy, y, u], [d] + ws,
                                       [(2 * D_MODEL, BF16), (C_WIDTH, BF16), (C_WIDTH, F32)], [((1, C_WIDTH), F32)],
                                       body)
    return do, dys, dus, dd


def _s5_in_bwd(gs2, bd, dus, ws, x, g, dres, *, name, tm=256):
    D = x.shape[1]

    def body(r, f, o, acc, s):
        du = (_dot_nt(r[0][...].astype(BF16), f[1][...]) + r[1][...]).astype(BF16)
        o[0][...] = du
        dx, dg = _rms_bwd_tile(r[2][...], f[0][...], _cat_nt(du, f[2:]))
        o[1][...] = dx + r[3][...]
        acc[0][...] += dg

    (du, dx), (dg,) = _rows_call(name, _tile(x.shape[0], tm), [gs2, dus, x, dres], [_vec(g), bd] + ws,
                                 [(C_WIDTH, BF16), (D, F32)], [((1, D), F32)], body)
    return du, dx, dg


_SCAN_CHUNK = 128
_SCAN_UNROLL = 8
_RE = slice(0, STATE_ROWS)
_IM = slice(STATE_ROWS, 2 * STATE_ROWS)


def _scan_fwd(bu, a, *, name):
    S = bu.shape[0]
    tc = _tile(S, _SCAN_CHUNK, 8)

    def body(bu_ref, a_ref, xs_ref, st_ref):
        @pl.when(pl.program_id(0) == 0)
        def _():
            st_ref[...] = jnp.zeros_like(st_ref)

        ar = a_ref[_RE, :]
        ai = a_ref[_IM, :]

        def step(t, carry):
            xr, xi = carry
            nr = ar * xr - ai * xi + bu_ref[t, _RE, :]
            ni = ar * xi + ai * xr + bu_ref[t, _IM, :]
            xs_ref[t, _RE, :] = nr
            xs_ref[t, _IM, :] = ni
            return nr, ni

        xr, xi = lax.fori_loop(0, tc, step, (st_ref[_RE, :], st_ref[_IM, :]), unroll=_SCAN_UNROLL)
        st_ref[_RE, :] = xr
        st_ref[_IM, :] = xi

    blk = pl.BlockSpec((tc, 2 * STATE_ROWS, STATE_LANES), lambda i: (i, 0, 0))
    return pl.pallas_call(
        body, name=name, grid=(S // tc,),
        in_specs=[blk, pl.BlockSpec((2 * STATE_ROWS, STATE_LANES), lambda i: (0, 0))], out_specs=blk,
        out_shape=jax.ShapeDtypeStruct(bu.shape, F32),
        scratch_shapes=[pltpu.VMEM((2 * STATE_ROWS, STATE_LANES), F32)],
        compiler_params=_params(("arbitrary",)),
    )(bu, a)


def _scan_bwd(dxs, xs, a, *, name):
    S = dxs.shape[0]
    tc = _tile(S, _SCAN_CHUNK, 8)
    nc = S // tc

    def body(dx_ref, xs_ref, a_ref, g_ref, da_ref, st_ref):
        @pl.when(pl.program_id(0) == 0)
        def _():
            st_ref[...] = jnp.zeros_like(st_ref)
            da_ref[...] = jnp.zeros_like(da_ref)

        ar = a_ref[_RE, :]
        ai = a_ref[_IM, :]

        def step(j, carry):
            gr, gi, dar, dai = carry
            t = tc - 1 - j
            xr = xs_ref[t, _RE, :]
            xi = xs_ref[t, _IM, :]
            dar = dar + gr * xr + gi * xi
            dai = dai + gi * xr - gr * xi
            nr = dx_ref[t, _RE, :] + ar * gr + ai * gi
            ni = dx_ref[t, _IM, :] + ar * gi - ai * gr
            g_ref[t, _RE, :] = nr
            g_ref[t, _IM, :] = ni
            return nr, ni, dar, dai

        init = (st_ref[_RE, :], st_ref[_IM, :], da_ref[_RE, :], da_ref[_IM, :])
        gr, gi, dar, dai = lax.fori_loop(0, tc, step, init, unroll=_SCAN_UNROLL)
        st_ref[_RE, :] = gr
        st_ref[_IM, :] = gi
        da_ref[_RE, :] = dar
        da_ref[_IM, :] = dai

    blk = pl.BlockSpec((tc, 2 * STATE_ROWS, STATE_LANES), lambda i: (nc - 1 - i, 0, 0))
    vec = pl.BlockSpec((2 * STATE_ROWS, STATE_LANES), lambda i: (0, 0))
    return pl.pallas_call(
        body, name=name, grid=(nc,), in_specs=[blk, blk, vec], out_specs=[blk, vec],
        out_shape=[jax.ShapeDtypeStruct(dxs.shape, F32), jax.ShapeDtypeStruct((2 * STATE_ROWS, STATE_LANES), F32)],
        scratch_shapes=[pltpu.VMEM((2 * STATE_ROWS, STATE_LANES), F32)],
        compiler_params=_params(("arbitrary",)),
    )(dxs, xs, a)


def _loss_head(x, g, target, *, name):
    S, D = x.shape

    def body(r, f, o, acc, s):
        xv = r[0][...]
        gv = f[0][...]
        rs = lax.rsqrt(_mean(xv * xv) + EPS)
        xh = xv * rs
        err = xh * gv - r[1][...]
        acc[1][...] += 0.5 * jnp.sum(_mean(err * err), axis=0, keepdims=True)
        dy = err * (1.0 / D)
        dyg = dy * gv
        o[0][...] = rs * (dyg - xh * _mean(dyg * xh))
        acc[0][...] += jnp.sum(dy * xh, axis=0, keepdims=True)

    (dx,), (dg, loss) = _rows_call(name, _tile(S, 256, 8), [x, target], [_vec(g)], [(D, F32)],
                                   [((1, D), F32), ((1, 128), F32)], body)
    return dx, dg, loss


_ADAM_C1 = 1.0 - ADAM_B1 ** ADAM_STEP
_ADAM_C2 = 1.0 - ADAM_B2 ** ADAM_STEP
_ONE_BLOCK_BYTES = 8 * 1024 * 1024


def _adamw_math(w, g, m, v):
    nm = ADAM_B1 * m + (1.0 - ADAM_B1) * g
    nv = ADAM_B2 * v + (1.0 - ADAM_B2) * (g * g)
    m_hat = nm / _ADAM_C1
    v_hat = nv / _ADAM_C2
    return -ADAM_LR * (m_hat / (jnp.sqrt(v_hat) + ADAM_EPS) + ADAM_WD * w), nm, nv


def _adamw_shard(w, gsrc, m, v, *, name):
    R, C = w.shape
    n_l = len(gsrc)
    rows = R // n_l
    tr = rows
    for _, r0 in gsrc:
        tr = math.gcd(tr, r0) if r0 else tr
    tr = _tile(tr, 256, 8) if tr > 256 else tr
    nb = rows // tr
    assert rows % tr == 0 and all(r0 % tr == 0 for _, r0 in gsrc)

    def body(*refs):
        w_ref, g_refs, (m_ref, v_ref, go_ref, d_ref, nm_ref, nv_ref) = refs[0], refs[1:1 + n_l], refs[1 + n_l:]
        layer = pl.program_id(0) // nb
        gv = g_refs[0][...]
        for l in range(1, n_l):
            gv = jnp.where(layer == l, g_refs[l][...], gv)
        go_ref[...] = gv
        d_ref[...], nm_ref[...], nv_ref[...] = _adamw_math(w_ref[...], gv, m_ref[...], v_ref[...])

    def g_spec(l, r0):
        return pl.BlockSpec((tr, C), lambda i: (r0 // tr + jnp.clip(i - l * nb, 0, nb - 1), 0))

    blk = pl.BlockSpec((tr, C), lambda i: (i, 0))
    out = jax.ShapeDtypeStruct((R, C), F32)
    return pl.pallas_call(
        body, name=name, grid=(R // tr,),
        in_specs=[blk] + [g_spec(l, r0) for l, (_, r0) in enumerate(gsrc)] + [blk, blk], out_specs=[blk] * 4,
        out_shape=[out] * 4, compiler_params=_params(("parallel",)),
    )(w, *[g for g, _ in gsrc], m, v)


def _adamw_small(ws, gs, ms, vs, *, name):
    n = len(ws)

    def body(*refs):
        w_r, g_r, m_r, v_r = refs[:n], refs[n:2 * n], refs[2 * n:3 * n], refs[3 * n:4 * n]
        d_r, nm_r, nv_r = refs[4 * n:5 * n], refs[5 * n:6 * n], refs[6 * n:7 * n]
        for k in range(n):
            d_r[k][...], nm_r[k][...], nv_r[k][...] = _adamw_math(w_r[k][...], g_r[k][...], m_r[k][...], v_r[k][...])

    vm = pl.BlockSpec(memory_space=pltpu.VMEM)
    out = [jax.ShapeDtypeStruct(w.shape, F32) for w in ws]
    res = pl.pallas_call(body, name=name, in_specs=[vm] * (4 * n), out_specs=[vm] * (3 * n), out_shape=out * 3,
                         compiler_params=pltpu.CompilerParams(vmem_limit_bytes=VMEM_LIMIT))(*ws, *gs, *ms, *vs)
    return res[:n], res[n:2 * n], res[2 * n:]


def _sum_slots(x, *, name):
    n, R, C = x.shape
    tr = R if (n + 1) * R * C * 4 <= _ONE_BLOCK_BYTES else _tile(R, 256, 8)

    def body(x_ref, o_ref):
        acc = x_ref[0]
        for k in range(1, n):
            acc = acc + x_ref[k]
        o_ref[...] = acc

    return pl.pallas_call(
        body, name=name, grid=(R // tr,),
        in_specs=[pl.BlockSpec((n, tr, C), lambda i: (0, i, 0))], out_specs=pl.BlockSpec((tr, C), lambda i: (i, 0)),
        out_shape=jax.ShapeDtypeStruct((R, C), F32), compiler_params=_params(("parallel",)),
    )(x)


def _pair_sum(g, r, half, *, name):
    n, R, C = g.shape
    Rh = R // 2
    tr = _tile(Rh, 256, 8)
    nb = Rh // tr

    def body(half_ref, g_ref, r_ref, o_ref):
        o_ref[...] = (g_ref[...] + r_ref[...]).astype(BF16)

    return pl.pallas_call(
        body, name=name,
        grid_spec=pltpu.PrefetchScalarGridSpec(
            num_scalar_prefetch=1, grid=(n, nb),
            in_specs=[pl.BlockSpec((1, tr, C), lambda p, i, h: (p, h[0] * nb + i, 0)),
                      pl.BlockSpec((1, tr, C), lambda p, i, h: (p, i, 0))],
            out_specs=pl.BlockSpec((1, tr, C), lambda p, i, h: (p, i, 0)),
        ),
        out_shape=jax.ShapeDtypeStruct((n, Rh, C), BF16), compiler_params=_params(("parallel", "parallel")),
    )(half, g, r)


def _chip_sum(g, r, slots, where, *, name):
    n, R, C = g.shape
    Rh = R // 2
    tr = _tile(Rh, 256, 8)
    nb = Rh // tr

    def body(w_ref, g_ref, r_ref, s_ref, o_ref):
        acc = g_ref[0] + r_ref[0]
        for k in range(slots.shape[0]):
            acc = acc + s_ref[k].astype(F32)
        o_ref[...] = acc

    return pl.pallas_call(
        body, name=name,
        grid_spec=pltpu.PrefetchScalarGridSpec(
            num_scalar_prefetch=1, grid=(nb,),
            in_specs=[pl.BlockSpec((1, tr, C), lambda i, w: (w[0], w[1] * nb + i, 0)),
                      pl.BlockSpec((1, tr, C), lambda i, w: (w[0], i, 0)),
                      pl.BlockSpec((slots.shape[0], tr, C), lambda i, w: (0, i, 0))],
            out_specs=pl.BlockSpec((tr, C), lambda i, w: (w[1] * nb + i, 0)),
        ),
        out_shape=jax.ShapeDtypeStruct((R, C), F32), compiler_params=_params(("parallel",)),
    )(where, g, r, slots)


ANY = pl.BlockSpec(memory_space=pl.ANY)


def _place():
    return lax.axis_index("x"), lax.axis_index("y"), lax.axis_index("c")


def _other_chips(x, y):
    return [(1 - x, y), (x, 1 - y), (1 - x, 1 - y)]


def _allgather_small(v, *, name):
    R, C = v.shape

    def body(x_ref, out_ref, send_sems, recv_sems, local_sem):
        x, y, c = _place()
        me, sibling = (x, y, c), (x, y, 1 - c)
        chips = _other_chips(x, y)

        def rows(px, py, pc):
            return out_ref.at[pl.ds((4 * px + 2 * py + pc) * R, R), :]

        def copy(k, block, to, src=None):
            return pltpu.make_async_remote_copy(
                src_ref=rows(*block) if src is None else src, dst_ref=rows(*block),
                send_sem=send_sems.at[k], recv_sem=recv_sems.at[k], device_id=to, device_id_type=MESH)

        mine = pltpu.make_async_copy(x_ref, rows(*me), local_sem)
        mine.start()
        first = [copy(0, me, sibling, src=x_ref)]
        first += [copy(1 + j, me, (*chip, c), src=x_ref) for j, chip in enumerate(chips)]
        for cp in first:
            cp.start()
        passed = [copy(4 + j, (*chip, c), sibling) for j, chip in enumerate(chips)]
        for j, chip in enumerate(chips):
            copy(1 + j, (*chip, c), me).wait_recv()
            passed[j].start()
        copy(0, sibling, me).wait_recv()
        for j, chip in enumerate(chips):
            copy(4 + j, (*chip, 1 - c), me).wait_recv()
        for cp in first + passed:
            cp.wait_send()
        mine.wait()

    return pl.pallas_call(
        body, name=name, out_shape=jax.ShapeDtypeStruct((N_DEV * R, C), v.dtype),
        in_specs=[pl.BlockSpec(memory_space=pltpu.VMEM)], out_specs=pl.BlockSpec(memory_space=pltpu.VMEM),
        scratch_shapes=[pltpu.SemaphoreType.DMA((7,)), pltpu.SemaphoreType.DMA((7,)), pltpu.SemaphoreType.DMA],
        compiler_params=pltpu.CompilerParams(vmem_limit_bytes=VMEM_LIMIT),
    )(v)


def _aliased_comm_call(body, bufs, n_sems, *, name):
    n = len(bufs)
    return pl.pallas_call(
        body, name=name, out_shape=[jax.ShapeDtypeStruct(b.shape, b.dtype) for b in bufs],
        in_specs=[ANY] * n, out_specs=[ANY] * n, input_output_aliases={k: k for k in range(n)},
        scratch_shapes=[pltpu.SemaphoreType.DMA((n_sems,)), pltpu.SemaphoreType.DMA((n_sems,))],
    )(*bufs)


def _allgather_chips(bufs, *, name):
    n = len(bufs)

    def body(*refs):
        outs, send_sems, recv_sems = refs[n:2 * n], refs[2 * n], refs[2 * n + 1]
        x, y, c = _place()
        chips = _other_chips(x, y)

        def copy(b, j, chip, hc, to):
            rh = bufs[b].shape[1] // 2
            part = outs[b].at[2 * chip[0] + chip[1], pl.ds(hc * rh, rh), :]
            return pltpu.make_async_remote_copy(src_ref=part, dst_ref=part, send_sem=send_sems.at[6 * b + j],
                                                recv_sem=recv_sems.at[6 * b + j], device_id=to, device_id_type=MESH)

        first = [copy(b, j, (x, y), c, (*chip, c)) for b in range(n) for j, chip in enumerate(chips)]
        for cp in first:
            cp.start()
        passed = []
        for b in range(n):
            for j, chip in enumerate(chips):
                copy(b, j, chip, c, (x, y, c)).wait_recv()
                passed.append(copy(b, 3 + j, chip, c, (x, y, 1 - c)))
                passed[-1].start()
        for b in range(n):
            for j, chip in enumerate(chips):
                copy(b, 3 + j, chip, 1 - c, (x, y, c)).wait_recv()
        for cp in first + passed:
            cp.wait_send()

    return _aliased_comm_call(body, bufs, 6 * n, name=name)


HBM = pl.BlockSpec(memory_space=pltpu.HBM)
SEM = pl.BlockSpec(memory_space=pltpu.SEMAPHORE)
_SPLIT = pltpu.CompilerParams(has_side_effects=pltpu.SideEffectType.DATAFLOW_SIDE_EFFECTING)


def _in_hbm(arrs):
    return [pltpu.with_memory_space_constraint(a, pltpu.HBM) for a in arrs]


def _gather_ici_start(bufs, after, *, name):
    n = len(bufs)

    def body(*refs):
        send_sems, recv_sems, outs, token = refs[n + 1], refs[n + 2], refs[n + 3:2 * n + 3], refs[2 * n + 3]
        x, y, c = _place()
        for b in range(n):
            rh = bufs[b].shape[1] // 2
            part = outs[b].at[2 * x + y, pl.ds(c * rh, rh), :]
            for j, chip in enumerate(_other_chips(x, y)):
                pltpu.make_async_remote_copy(src_ref=part, dst_ref=part, send_sem=send_sems.at[3 * b + j],
                                             recv_sem=recv_sems.at[3 * b + j], device_id=(*chip, c),
                                             device_id_type=MESH).start()
        token[...] = jnp.zeros_like(token)

    res = pl.pallas_call(
        body, name=name,
        out_shape=(pltpu.SemaphoreType.DMA((3 * n,)), pltpu.SemaphoreType.DMA((3 * n,)),
                   *[pltpu.HBM(b.shape, b.dtype) for b in bufs], jax.ShapeDtypeStruct((8, 128), F32)),
        in_specs=[HBM] * n + [ANY], out_specs=(SEM, SEM, *[HBM] * n, pl.BlockSpec(memory_space=pltpu.VMEM)),
        input_output_aliases={k: k + 2 for k in range(n)}, compiler_params=_SPLIT,
    )(*_in_hbm(bufs), after)
    return res[0], res[1], list(res[2:2 + n]), res[2 + n]


def _gather_ici_wait(send_sems, recv_sems, bufs, after, *, name):
    n = len(bufs)

    def body(*refs):
        ins, ss, rs = refs[:n], refs[n], refs[n + 1]
        x, y, c = _place()
        for b in range(n):
            rh = bufs[b].shape[1] // 2
            mine = ins[b].at[2 * x + y, pl.ds(c * rh, rh), :]
            for j, (cx, cy) in enumerate(_other_chips(x, y)):
                theirs = ins[b].at[2 * cx + cy, pl.ds(c * rh, rh), :]
                cp = pltpu.make_async_remote_copy(src_ref=mine, dst_ref=theirs, send_sem=ss.at[3 * b + j],
                                                  recv_sem=rs.at[3 * b + j], device_id=(cx, cy, c),
                                                  device_id_type=MESH)
                cp.wait_send()
                cp.wait_recv()

    return list(pl.pallas_call(
        body, name=name, out_shape=[pltpu.HBM(b.shape, b.dtype) for b in bufs],
        in_specs=[HBM] * n + [SEM, SEM, ANY], out_specs=[HBM] * n,
        input_output_aliases={k: k for k in range(n)}, compiler_params=_SPLIT,
    )(*bufs, send_sems, recv_sems, after))


def _gather_forward(bufs, *, name):
    n = len(bufs)

    def body(*refs):
        outs, send_sems, recv_sems = refs[n:2 * n], refs[2 * n], refs[2 * n + 1]
        x, y, c = _place()

        def copy(b, j, chip, hc):
            rh = bufs[b].shape[1] // 2
            part = outs[b].at[2 * chip[0] + chip[1], pl.ds(hc * rh, rh), :]
            return pltpu.make_async_remote_copy(src_ref=part, dst_ref=part, send_sem=send_sems.at[3 * b + j],
                                                recv_sem=recv_sems.at[3 * b + j], device_id=(x, y, 1 - c),
                                                device_id_type=MESH)

        sends = [copy(b, j, chip, c) for b in range(n) for j, chip in enumerate(_other_chips(x, y))]
        for cp in sends:
            cp.start()
        for b in range(n):
            for j, chip in enumerate(_other_chips(x, y)):
                copy(b, j, chip, 1 - c).wait_recv()
        for cp in sends:
            cp.wait_send()

    return _aliased_comm_call(body, bufs, 3 * n, name=name)


def _chip_exchange_start(hs, *, name):
    n = len(hs)
    lands = [lax.empty((3,) + h.shape[1:], h.dtype) for h in hs]

    def body(*refs):
        send_sems, recv_sems = refs[2 * n], refs[2 * n + 1]
        h_out, l_out, token = refs[2 * n + 2:3 * n + 2], refs[3 * n + 2:4 * n + 2], refs[4 * n + 2]
        x, y, c = _place()
        for b in range(n):
            for j, (cx, cy) in enumerate(_other_chips(x, y)):
                pltpu.make_async_remote_copy(src_ref=h_out[b].at[2 * cx + cy], dst_ref=l_out[b].at[j],
                                             send_sem=send_sems.at[3 * b + j], recv_sem=recv_sems.at[3 * b + j],
                                             device_id=(cx, cy, c), device_id_type=MESH).start()
        token[...] = jnp.zeros_like(token)

    res = pl.pallas_call(
        body, name=name,
        out_shape=(pltpu.SemaphoreType.DMA((3 * n,)), pltpu.SemaphoreType.DMA((3 * n,)),
                   *[pltpu.HBM(a.shape, a.dtype) for a in hs + lands], jax.ShapeDtypeStruct((8, 128), F32)),
        in_specs=[HBM] * (2 * n), out_specs=(SEM, SEM, *[HBM] * (2 * n), pl.BlockSpec(memory_space=pltpu.VMEM)),
        input_output_aliases={k: k + 2 for k in range(2 * n)}, compiler_params=_SPLIT,
    )(*_in_hbm(hs + lands))
    return res[0], res[1], list(res[2:2 + n]), list(res[2 + n:2 + 2 * n]), res[2 + 2 * n]


def _chip_exchange_wait(send_sems, recv_sems, hs, lands, after, *, name):
    n = len(hs)

    def body(*refs):
        h_in, l_in, ss, rs = refs[:n], refs[n:2 * n], refs[2 * n], refs[2 * n + 1]
        x, y, c = _place()
        for b in range(n):
            for j, (cx, cy) in enumerate(_other_chips(x, y)):
                cp = pltpu.make_async_remote_copy(src_ref=h_in[b].at[2 * cx + cy], dst_ref=l_in[b].at[j],
                                                  send_sem=ss.at[3 * b + j], recv_sem=rs.at[3 * b + j],
                                                  device_id=(cx, cy, c), device_id_type=MESH)
                cp.wait_send()
                cp.wait_recv()

    res = pl.pallas_call(
        body, name=name, out_shape=[pltpu.HBM(a.shape, a.dtype) for a in hs + lands],
        in_specs=[HBM] * (2 * n) + [SEM, SEM, ANY], out_specs=[HBM] * (2 * n),
        input_output_aliases={k: k for k in range(2 * n)}, compiler_params=_SPLIT,
    )(*hs, *lands, send_sems, recv_sems, after)
    return list(res[n:])


def _pair_exchange(gs, *, name):
    n = len(gs)

    def body(*refs):
        ins, outs, send_sems, recv_sems = refs[:n], refs[n:2 * n], refs[2 * n], refs[2 * n + 1]
        x, y, c = _place()
        cps = []
        for b in range(n):
            rh = gs[b].shape[1] // 2
            cps.append(pltpu.make_async_remote_copy(
                src_ref=ins[b].at[:, pl.ds((1 - c) * rh, rh), :], dst_ref=outs[b], send_sem=send_sems.at[b],
                recv_sem=recv_sems.at[b], device_id=(x, y, 1 - c), device_id_type=MESH))
        for cp in cps:
            cp.start()
        for cp in cps:
            cp.wait()

    return pl.pallas_call(
        body, name=name, out_shape=[jax.ShapeDtypeStruct((g.shape[0], g.shape[1] // 2, g.shape[2]), g.dtype) for g in gs],
        in_specs=[ANY] * n, out_specs=[ANY] * n,
        scratch_shapes=[pltpu.SemaphoreType.DMA((n,)), pltpu.SemaphoreType.DMA((n,))],
    )(*gs)


def _chip_exchange(hs, *, name):
    n = len(hs)

    def body(*refs):
        ins, outs, send_sems, recv_sems = refs[:n], refs[n:2 * n], refs[2 * n], refs[2 * n + 1]
        x, y, c = _place()
        cps = [pltpu.make_async_remote_copy(
            src_ref=ins[b].at[2 * cx + cy], dst_ref=outs[b].at[j], send_sem=send_sems.at[3 * b + j],
            recv_sem=recv_sems.at[3 * b + j], device_id=(cx, cy, c), device_id_type=MESH)
            for b in range(n) for j, (cx, cy) in enumerate(_other_chips(x, y))]
        for cp in cps:
            cp.start()
        for cp in cps:
            cp.wait()

    return pl.pallas_call(
        body, name=name, out_shape=[jax.ShapeDtypeStruct((3,) + h.shape[1:], h.dtype) for h in hs],
        in_specs=[ANY] * n, out_specs=[ANY] * n,
        scratch_shapes=[pltpu.SemaphoreType.DMA((3 * n,)), pltpu.SemaphoreType.DMA((3 * n,))],
    )(*hs)


def _pair_share(ss, *, name):
    n = len(ss)

    def body(*refs):
        outs, send_sems, recv_sems = refs[n:2 * n], refs[2 * n], refs[2 * n + 1]
        x, y, c = _place()
        cps = []
        for b in range(n):
            rh = ss[b].shape[0] // 2
            mine = outs[b].at[pl.ds(c * rh, rh), :]
            cps.append(pltpu.make_async_remote_copy(src_ref=mine, dst_ref=mine, send_sem=send_sems.at[b],
                                                    recv_sem=recv_sems.at[b], device_id=(x, y, 1 - c),
                                                    device_id_type=MESH))
        for cp in cps:
            cp.start()
        for b, cp in enumerate(cps):
            rh = ss[b].shape[0] // 2
            theirs = outs[b].at[pl.ds((1 - c) * rh, rh), :]
            pltpu.make_async_remote_copy(src_ref=theirs, dst_ref=theirs, send_sem=send_sems.at[b],
                                         recv_sem=recv_sems.at[b], device_id=(x, y, 1 - c),
                                         device_id_type=MESH).wait_recv()
            cp.wait_send()

    return _aliased_comm_call(body, ss, n, name=name)


_SMALL_SHARDED = (("e_conv_w", 2), ("o_norm", 1), ("o_d", 1))
_REPLICATED = ("e_norm", "e_gmlp_w", "e_gmlp_b", "e_conv_b", "e_conv_ln_g", "e_conv_ln_b", "o_lam_re", "o_lam_im",
               "o_log_dt", "o_b_re", "o_b_im", "o_c_re", "o_c_im", "ca_norm", "ca_mem_norm", "ffn_norm", "final_norm")
_SMALL = tuple(n for n, _ in _SMALL_SHARDED) + _REPLICATED
_WEIGHTS = ("e_norm", "e_w_in", "e_gmlp_w", "e_gmlp_b", "e_conv_w", "e_conv_b", "e_conv_ln_g", "e_conv_ln_b",
            "e_w_out", "o_norm", "o_w_in", "o_lam_re", "o_lam_im", "o_log_dt", "o_b_re", "o_b_im", "o_c_re", "o_c_im",
            "o_d", "o_w_out", "ca_norm", "ca_mem_norm", "ca_wq", "ca_wk", "ca_wv", "ca_wo", "ffn_norm", "ffn_w_gate",
            "ffn_w_up", "ffn_w_down", "final_norm")


def _pack_rows(arrs, width, dtype, row_mult=8):
    parts, spans, r0 = [], [], 0
    for a in arrs:
        flat = a.reshape(-1).astype(dtype)
        rows = -(-flat.shape[0] // (width * row_mult)) * row_mult
        if rows * width != flat.shape[0]:
            flat = jnp.pad(flat, (0, rows * width - flat.shape[0]))
        parts.append(flat.reshape(rows, width))
        spans.append((r0, rows))
        r0 += rows
    return jnp.concatenate(parts, axis=0), spans


def _unpack_rows(slab, spans, shapes):
    out = []
    for (r0, rows), shp in zip(spans, shapes):
        n = math.prod(shp)
        out.append(slab[r0:r0 + rows].reshape(-1)[:n].reshape(shp))
    return out


def _two_d(a):
    return a.reshape(-1, a.shape[-1])


def _local_slab(local, slab, dtype):
    parts = sorted((r0, n, l) for n, (_, where) in _PLACE.items() for l, (s, r0) in enumerate(where) if s == slab)
    shards = [local[n] if len(_PLACE[n][1]) == 1 else local[n][l] for _, n, l in parts]
    return jnp.concatenate([_two_d(a).astype(dtype) for a in shards], axis=0)


def _block_diag(b, pattern):
    return jnp.einsum(pattern, b, jnp.eye(C_GROUPS, dtype=b.dtype))


def _s5_discretize(lam_re, lam_im, log_dt, b_re, b_im):
    dt = jnp.exp(log_dt)[:, None]
    mag = jnp.exp(lam_re * dt)
    ar = mag * jnp.cos(lam_im * dt)
    ai = mag * jnp.sin(lam_im * dt)
    den = lam_re * lam_re + lam_im * lam_im
    qr = ((ar - 1.0) * lam_re + ai * lam_im) / den
    qi = (ai * lam_re - (ar - 1.0) * lam_im) / den
    bbr = qr[..., None] * b_re - qi[..., None] * b_im
    bbi = qr[..., None] * b_im + qi[..., None] * b_re
    return ar, ai, bbr, bbi


def _attention_block(x, mem, W, w, i, tag):
    xn, q = _norm_mm(x, w["ca_norm"][i], _shards(W, "ca_wq", i), split="k", out_dtype=BF16, name=f"{tag}_q")
    memn = _rms_fwd(mem, w["ca_mem_norm"][i], name=f"{tag}_ca_memnorm")
    k = _mm_k(memn, _shards(W, "ca_wk", i), out_dtype=BF16, name=f"{tag}_k")
    v = _mm_k(memn, _shards(W, "ca_wv", i), out_dtype=BF16, name=f"{tag}_v")
    o = _attn_fwd(q, k, v, name=f"{tag}_attn")
    y = _mm_k(o, _shards(W, "ca_wo", i), add=x, name=f"{tag}_wo")
    return y, (x, xn, memn, q, k, v, o)


def _attention_block_bwd(dy, saved, mem, W, w, i, tag, G, grads):
    x, xn, memn, q, k, v, o = saved
    G = _grad_to_slab(G, "ca_wo", i, o, dy, a_cols=256, name=f"{tag}_dwo")
    dq, dk, dv = _attn_bwd(dy, _shards(W, "ca_wo", i), q, k, v, name=f"{tag}_attn_bwd")
    G = _grad_to_slab(G, "ca_wq", i, xn, dq, a_cols=256, name=f"{tag}_dwq")
    G = _grad_to_slab(G, "ca_wk", i, memn, dk, a_cols=256, name=f"{tag}_dwk")
    G = _grad_to_slab(G, "ca_wv", i, memn, dv, a_cols=256, name=f"{tag}_dwv")
    dmemn = _mm_k_t([(dk, _shards(W, "ca_wk", i)), (dv, _shards(W, "ca_wv", i))], name=f"{tag}_dmemn")
    dx, dg = _norm_bwd_k(dq, _shards(W, "ca_wq", i), x, w["ca_norm"][i], dy, name=f"{tag}_dq_norm_bwd")
    grads["ca_norm"][i] = dg[0]
    grads["ca_mem_norm"][i] = _rms_dg(mem, w["ca_mem_norm"][i], dmemn, name=f"{tag}_ca_memnorm_bwd")[0]
    return dx, G


def _ffn_block(x, W, w, i, tag):
    fn, gate, up, h = _ffn_up(x, w["ffn_norm"][i], _shards(W, "ffn_w_gate", i), _shards(W, "ffn_w_up", i),
                              name=f"{tag}_ffn_up")
    y = _mm_k(h, _shards(W, "ffn_w_down", i), add=x, name=f"{tag}_down")
    return y, (x, fn, gate, up, h)


def _ffn_block_bwd(dy, saved, W, w, i, tag, G, grads):
    x, fn, gate, up, h = saved
    G = _grad_to_slab(G, "ffn_w_down", i, h, dy, name=f"{tag}_dwd")
    dg, du = _ffn_bwd_hidden(dy, _shards(W, "ffn_w_down", i), gate, up, name=f"{tag}_ffn_bwd_hidden")
    G = _grad_to_slab(G, "ffn_w_gate", i, fn, dg, name=f"{tag}_dwg")
    G = _grad_to_slab(G, "ffn_w_up", i, fn, du, name=f"{tag}_dwu")
    dx, dgn = _ffn_in_bwd(dg, du, _shards(W, "ffn_w_gate", i), _shards(W, "ffn_w_up", i), x, w["ffn_norm"][i], dy,
                          name=f"{tag}_ffn_in_bwd")
    grads["ffn_norm"][i] = dgn[0]
    return dx, G


def _gmlp_mask():
    chunk = jnp.arange(GMLP_BLOCK) // CHUNK
    return chunk[None, :] <= chunk[:, None]


def _even_block(x, W, w, tag):
    hn, proj = _norm_mm(x, w["e_norm"][0], _shards(W, "e_w_in"), split="n", out_dtype=F32, name=f"{tag}_w_in")
    wm = jnp.where(_gmlp_mask()[None], w["e_gmlp_w"][0], 0.0).astype(BF16)
    bcol = w["e_gmlp_b"][0][:, :, None]
    cw = jnp.pad(w["e_conv_w"][0], ((0, CONV_HALO - CONV_WIDTH), (0, 0)))
    cb, lg, lb = w["e_conv_b"], w["e_conv_ln_g"], w["e_conv_ln_b"]
    mix, hc = _even_fwd(proj, wm, bcol, cw, cb, lg, lb, name=f"{tag}_mixers")
    y = _mm_k(mix, _shards(W, "e_w_out"), add=x, name=f"{tag}_w_out")
    return y, (x, hn, proj, mix, hc, wm, bcol, cw)


def _even_block_bwd(dy, saved, W, w, tag, G, grads):
    x, hn, proj, mix, hc, wm, bcol, cw = saved
    dmix = _mm_k_t([(dy, _shards(W, "e_w_out"))], name=f"{tag}_dmix")
    G = _grad_to_slab(G, "e_w_out", 0, mix, dy, a_cols=256, name=f"{tag}_dw_out")
    wmt = jnp.swapaxes(wm, 1, 2)
    dpa, dhc, dwm, db, dlg, dlb, dcb = _even_bwd1(proj, dmix, hc, wm, wmt, bcol, w["e_conv_ln_g"], w["e_conv_ln_b"],
                                                  name=f"{tag}_mixers_bwd1")
    dpb, dcw = _even_bwd2(proj, dhc, cw, name=f"{tag}_mixers_bwd2")
    grads["e_gmlp_w"] = jnp.where(_gmlp_mask()[None], dwm, 0.0)[None]
    grads["e_gmlp_b"] = db[:, :, 0][None]
    grads["e_conv_ln_g"], grads["e_conv_ln_b"], grads["e_conv_b"] = dlg, dlb, dcb
    grads["e_conv_w"] = dcw[:CONV_WIDTH][None]
    G = _grad_to_slab(G, "e_w_in", 0, hn, dpa, b_cols=512, chips=(0, 2), name=f"{tag}_dw_in_a")
    G = _grad_to_slab(G, "e_w_in", 0, hn, dpb, b_cols=512, chips=(2, 2), name=f"{tag}_dw_in_b")
    dx, dg = _norm_bwd_n((dpa, dpb), _shards(W, "e_w_in"), x, w["e_norm"][0], dy, name=f"{tag}_in_bwd")
    grads["e_norm"] = dg
    return dx, G


def _odd_block(x, W, w, tag):
    S = x.shape[0]
    hn, u = _norm_mm(x, w["o_norm"][0], _shards(W, "o_w_in"), split="k", out_dtype=F32, name=f"{tag}_w_in")
    disc_in = (w["o_lam_re"][0], w["o_lam_im"][0], w["o_log_dt"][0], w["o_b_re"][0], w["o_b_im"][0])
    (ar, ai, bbr, bbi), disc_vjp = jax.vjp(_s5_discretize, *disc_in)
    bd = jnp.concatenate([_block_diag(bbr, "gpc,gh->gchp").reshape(C_WIDTH, N_STATE),
                          _block_diag(bbi, "gpc,gh->gchp").reshape(C_WIDTH, N_STATE)], axis=1).astype(BF16)
    cd = jnp.concatenate([_block_diag(w["o_c_re"][0], "gcp,gh->gphc").reshape(N_STATE, C_WIDTH),
                          -_block_diag(w["o_c_im"][0], "gcp,gh->gphc").reshape(N_STATE, C_WIDTH)], axis=0).astype(BF16)
    a = jnp.concatenate([ar.reshape(STATE_ROWS, STATE_LANES), ai.reshape(STATE_ROWS, STATE_LANES)], axis=0)
    bu = _mm_rows(u, bd, name=f"{tag}_bu")
    xs = _scan_fwd(bu.reshape(S, 2 * STATE_ROWS, STATE_LANES), a, name=f"{tag}_scan")
    yv, yg = _s5_readout(xs.reshape(S, 2 * N_STATE), cd, u, w["o_d"], name=f"{tag}_readout")
    o, y = _glu_out(yg, _shards(W, "o_w_out"), x, name=f"{tag}_glu_out")
    return y, (x, hn, u, bd, cd, a, xs, yv, yg, o, disc_vjp)


def _odd_block_bwd(dy, saved, W, w, tag, G, grads):
    x, hn, u, bd, cd, a, xs, yv, yg, o, disc_vjp = saved
    S = x.shape[0]
    do, dys, dus, dd = _glu_out_bwd(o, dy, _shards(W, "o_w_out"), yv, u, w["o_d"], name=f"{tag}_glu_out_bwd")
    G = _grad_to_slab(G, "o_w_out", 0, yg, do, b_cols=512, name=f"{tag}_dw_out")
    grads["o_d"] = dd
    xs2 = xs.reshape(S, 2 * N_STATE)
    dxs = _mm_rows(dys, cd, nt=True, name=f"{tag}_dxs")
    dcd = _mm_tn(xs2, dys, name=f"{tag}_dcd")
    gs, da = _scan_bwd(dxs.reshape(S, 2 * STATE_ROWS, STATE_LANES), xs, a, name=f"{tag}_scan_bwd")
    gs2 = gs.reshape(S, 2 * N_STATE)
    dbd = _mm_tn(u, gs2, name=f"{tag}_dbd")
    du, dx, dg = _s5_in_bwd(gs2, bd, dus, _shards(W, "o_w_in"), x, w["o_norm"][0], dy, name=f"{tag}_in_bwd")
    G = _grad_to_slab(G, "o_w_in", 0, hn, du, a_cols=256, name=f"{tag}_dw_in")
    grads["o_norm"] = dg
    eye = jnp.eye(C_GROUPS, dtype=F32)
    dcr = jnp.einsum("gphc,gh->gcp", dcd[:N_STATE].reshape(C_GROUPS, C_STATE, C_GROUPS, C_GROUP_CH), eye)
    dci = -jnp.einsum("gphc,gh->gcp", dcd[N_STATE:].reshape(C_GROUPS, C_STATE, C_GROUPS, C_GROUP_CH), eye)
    dbbr = jnp.einsum("gchp,gh->gpc", dbd[:, :N_STATE].reshape(C_GROUPS, C_GROUP_CH, C_GROUPS, C_STATE), eye)
    dbbi = jnp.einsum("gchp,gh->gpc", dbd[:, N_STATE:].reshape(C_GROUPS, C_GROUP_CH, C_GROUPS, C_STATE), eye)
    dar = da[:STATE_ROWS].reshape(C_GROUPS, C_STATE)
    dai = da[STATE_ROWS:].reshape(C_GROUPS, C_STATE)
    dlr, dli, dldt, dbr, dbi = disc_vjp((dar, dai, dbbr, dbbi))
    grads["o_lam_re"], grads["o_lam_im"], grads["o_log_dt"] = dlr[None], dli[None], dldt[None]
    grads["o_b_re"], grads["o_b_im"], grads["o_c_re"], grads["o_c_im"] = dbr[None], dbi[None], dcr[None], dci[None]
    return dx, G


def _behind(value, token):
    return lax.optimization_barrier((value, token))[0]


class _NoExchange:
    def __init__(self, W):
        self.W = W

    def first_weights(self, w):
        return self.W, w

    def after_even_mixer(self, x, W, w):
        return W, w

    def after_layer0(self, x, W):
        return W

    def after_layer1_backward(self, dx, G):
        return dx


def _forward_backward(xs_, mems_, tgt, w, G, exchange):
    W, w = exchange.first_weights(w)
    x1, s_mix0 = _even_block(xs_, W, w, "l0")
    W, w = exchange.after_even_mixer(x1, W, w)
    x2, s_att0 = _attention_block(x1, mems_, W, w, 0, "l0")
    x3, s_ffn0 = _ffn_block(x2, W, w, 0, "l0")
    W = exchange.after_layer0(x3, W)
    x4, s_mix1 = _odd_block(x3, W, w, "l1")
    x5, s_att1 = _attention_block(x4, mems_, W, w, 1, "l1")
    x6, s_ffn1 = _ffn_block(x5, W, w, 1, "l1")
    dx, dfinal, loss_lanes = _loss_head(x6, w["final_norm"], tgt, name="loss_head")

    grads = {n: [None, None] for n in ("ca_norm", "ca_mem_norm", "ffn_norm")}
    grads["final_norm"] = dfinal[0]
    dx, G = _ffn_block_bwd(dx, s_ffn1, W, w, 1, "l1", G, grads)
    dx, G = _attention_block_bwd(dx, s_att1, mems_, W, w, 1, "l1", G, grads)
    dx, G = _odd_block_bwd(dx, s_mix1, W, w, "l1", G, grads)
    dx = exchange.after_layer1_backward(dx, G)
    dx, G = _ffn_block_bwd(dx, s_ffn0, W, w, 0, "l0", G, grads)
    dx, G = _attention_block_bwd(dx, s_att0, mems_, W, w, 0, "l0", G, grads)
    dx, G = _even_block_bwd(dx, s_mix0, W, w, "l0", G, grads)
    for n in list(grads):
        if isinstance(grads[n], list):
            grads[n] = jnp.stack(grads[n], axis=0)
        grads[n] = grads[n].reshape(w[n].shape)
    return loss_lanes, dx, G, grads


class _Exchange:
    def __init__(self, local, chip, core):
        self.bufs = {s: lax.dynamic_update_slice(lax.empty((N_CHIPS, rows, width), BF16),
                                                 _local_slab(local, s, BF16)[None], (chip, 0, 0))
                     for s, (width, rows) in _SLABS.items()}
        self.half = core.reshape(1).astype(jnp.int32)
        self.where = jnp.stack([chip, core]).astype(jnp.int32)
        self.flight = None
        self.layer1_reduce = None

    def _start(self, stage, after, tag):
        self.flight = _gather_ici_start([self.bufs[s] for s in stage], after, name=f"gather_{tag}_start")
        return self.flight[3]

    def _land(self, stage, after, tag):
        send_sems, recv_sems, bufs, _ = self.flight
        bufs = _gather_ici_wait(send_sems, recv_sems, bufs, after, name=f"gather_{tag}_wait")
        return dict(zip(stage, _gather_forward(bufs, name=f"gather_{tag}_forward")))

    def first_weights(self, w):
        W = dict(zip(_STAGES[0], _allgather_chips([self.bufs[s] for s in _STAGES[0]], name="gather_stage0")))
        token = self._start(_STAGES[1], W[_STAGES[0][0]], "stage1")
        return W, {**w, "e_norm": _behind(w["e_norm"], token)}

    def after_even_mixer(self, x, W, w):
        W = {**W, **self._land(_STAGES[1], x, "stage1")}
        token = self._start(_STAGES[2], W[_STAGES[1][0]], "stage2")
        return W, {**w, "ca_norm": _behind(w["ca_norm"], token)}

    def after_layer0(self, x, W):
        return {**W, **self._land(_STAGES[2], x, "stage2")}

    def reduce_start(self, G, slabs, tag):
        gl = [G[s] for s in slabs]
        other = _pair_exchange(gl, name=f"grad_{tag}_pair_exchange")
        pairs = [_pair_sum(g, r, self.half, name=f"grad_pair_sum_{s}") for s, g, r in zip(slabs, gl, other)]
        send_sems, recv_sems, pairs, lands, token = _chip_exchange_start(pairs, name=f"grad_{tag}_chip_start")
        return (slabs, gl, other, send_sems, recv_sems, pairs, lands), token

    def reduce_finish(self, state, after, tag):
        slabs, gl, other, send_sems, recv_sems, pairs, lands = state
        slots = _chip_exchange_wait(send_sems, recv_sems, pairs, lands, after, name=f"grad_{tag}_chip_wait")
        halves = [_chip_sum(g, r, sl, self.where, name=f"grad_chip_sum_{s}")
                  for s, g, r, sl in zip(slabs, gl, other, slots)]
        return dict(zip(slabs, _pair_share(halves, name=f"grad_{tag}_pair_share")))

    def after_layer1_backward(self, dx, G):
        self.layer1_reduce, token = self.reduce_start(G, _STAGES[2], "l1")
        return _behind(dx, token)


def kernel(x, mem, e_norm, e_w_in, e_gmlp_w, e_gmlp_b, e_conv_w, e_conv_b, e_conv_ln_g, e_conv_ln_b, e_w_out, o_norm, o_w_in, o_lam_re, o_lam_im, o_log_dt, o_b_re, o_b_im, o_c_re, o_c_im, o_d, o_w_out, ca_norm, ca_mem_norm, ca_wq, ca_wk, ca_wv, ca_wo, ffn_norm, ffn_w_gate, ffn_w_up, ffn_w_down, final_norm, loss_target, m_e_norm, m_e_w_in, m_e_gmlp_w, m_e_gmlp_b, m_e_conv_w, m_e_conv_b, m_e_conv_ln_g, m_e_conv_ln_b, m_e_w_out, m_o_norm, m_o_w_in, m_o_lam_re, m_o_lam_im, m_o_log_dt, m_o_b_re, m_o_b_im, m_o_c_re, m_o_c_im, m_o_d, m_o_w_out, m_ca_norm, m_ca_mem_norm, m_ca_wq, m_ca_wk, m_ca_wv, m_ca_wo, m_ffn_norm, m_ffn_w_gate, m_ffn_w_up, m_ffn_w_down, m_final_norm, v_e_norm, v_e_w_in, v_e_gmlp_w, v_e_gmlp_b, v_e_conv_w, v_e_conv_b, v_e_conv_ln_g, v_e_conv_ln_b, v_e_w_out, v_o_norm, v_o_w_in, v_o_lam_re, v_o_lam_im, v_o_log_dt, v_o_b_re, v_o_b_im, v_o_c_re, v_o_c_im, v_o_d, v_o_w_out, v_ca_norm, v_ca_mem_norm, v_ca_wq, v_ca_wk, v_ca_wv, v_ca_wo, v_ffn_norm, v_ffn_w_gate, v_ffn_w_up, v_ffn_w_down, v_final_norm):
    args = dict(locals())
    local = {n: args[n] for n in _WEIGHTS}
    mom = {n: args["m_" + n] for n in _WEIGHTS}
    vel = {n: args["v_" + n] for n in _WEIGHTS}
    chip = 2 * lax.axis_index("x") + lax.axis_index("y")
    core = lax.axis_index("c")
    xs_, mems_, tgt = x[0], mem[0], loss_target[0]

    w = {n: local[n] for n in _REPLICATED}
    sm_slab, sm_spans = _pack_rows([local[n] for n, _ in _SMALL_SHARDED], SMALL_W, F32)
    sm_all = _allgather_small(sm_slab, name="gather_small_weights").reshape(N_DEV, -1, SMALL_W)
    for (n, ax), span in zip(_SMALL_SHARDED, sm_spans):
        shp = local[n].shape
        w[n] = jnp.concatenate([_unpack_rows(sm_all[2 * p], [span], [shp])[0] for p in range(N_CHIPS)], axis=ax)

    exchange = _Exchange(local, chip, core)
    G = {s: lax.empty((N_CHIPS, rows, width), F32) for s, (width, rows) in _SLABS.items()}
    loss_lanes, dx, G, grads = _forward_backward(xs_, mems_, tgt, w, G, exchange)

    gsum = exchange.reduce_finish(exchange.layer1_reduce, dx, "l1")
    layer0_reduce, token = exchange.reduce_start(G, _STAGES[0] + _STAGES[1], "l0")

    grads = {**grads, _SMALL[0]: _behind(grads[_SMALL[0]], token)}
    gs_slab, gs_spans = _pack_rows([grads[n] for n in _SMALL], SMALL_W, F32)
    gs_all = _allgather_small(gs_slab, name="gather_small_grads").reshape(N_DEV, -1, SMALL_W)
    gs_sum = _sum_slots(gs_all, name="small_grad_sum")
    out_grads = dict(zip(_SMALL, _unpack_rows(gs_sum, gs_spans, [grads[n].shape for n in _SMALL])))
    for n, ax in _SMALL_SHARDED:
        width = local[n].shape[ax]
        out_grads[n] = lax.dynamic_slice_in_dim(out_grads[n], chip * width, width, axis=ax)

    delta, new_m, new_v = {}, {}, {}
    d_, m_, v_ = _adamw_small([_two_d(local[n]) for n in _SMALL], [_two_d(out_grads[n]) for n in _SMALL],
                              [_two_d(mom[n]) for n in _SMALL], [_two_d(vel[n]) for n in _SMALL], name="adamw_small")
    for n, dd, mm_, vv in zip(_SMALL, d_, m_, v_):
        shp = local[n].shape
        delta[n], new_m[n], new_v[n] = dd.reshape(shp), mm_.reshape(shp), vv.reshape(shp)
    gsum = {**gsum, **exchange.reduce_finish(layer0_reduce, d_[0], "l0")}
    for n, (rows, where) in _PLACE.items():
        shp = local[n].shape
        g_, d_, m_, v_ = _adamw_shard(_two_d(local[n]), [(gsum[s], r0) for s, r0 in where], _two_d(mom[n]),
                                      _two_d(vel[n]), name=f"adamw_{n}")
        out_grads[n], delta[n], new_m[n], new_v[n] = g_.reshape(shp), d_.reshape(shp), m_.reshape(shp), v_.reshape(shp)

    loss = lax.psum(loss_lanes[0, 0], ("x", "y", "c"))
    return (loss, dx[None], *[out_grads[n] for n in _WEIGHTS], *[delta[n] for n in _WEIGHTS],
            *[new_m[n] for n in _WEIGHTS], *[new_v[n] for n in _WEIGHTS])
```

```python
import functools
import math

import jax
import jax.numpy as jnp
from jax import lax
from jax.experimental import pallas as pl
from jax.experimental.pallas import tpu as pltpu

F32 = jnp.float32
BF16 = jnp.bfloat16
MESH = pl.DeviceIdType.MESH

EPS = 1e-6
D_MODEL = 1024
A_WIDTH = 512
A_GROUPS = 4
GMLP_BLOCK = 128
CHUNK = 64
B_WIDTH = 512
CONV_WIDTH = 31
CONV_HALO = 32
C_WIDTH = 512
C_GROUP_CH = 16
C_GROUPS = 32
C_STATE = 64
N_STATE = C_GROUPS * C_STATE
STATE_ROWS = 8
STATE_LANES = N_STATE // STATE_ROWS
CA_HEADS = 4
CA_HEAD_DIM = 256
FFN_HIDDEN = 2816

ADAM_LR = 0.001
ADAM_B1 = 0.9
ADAM_B2 = 0.999
ADAM_EPS = 1e-08
ADAM_WD = 0.01
ADAM_STEP = 10

VMEM_LIMIT = 56 * 1024 * 1024
ACC_BYTES = 6 * 1024 * 1024
TN_VMEM_BYTES = 44 * 1024 * 1024
SMALL_W = 128
N_CHIPS = 4
N_DEV = 8

_SLABS = {"D0": (512, 1024), "E0": (1024, 256), "A0": (1024, 1024), "B0": (1024, 704), "C0": (704, 2048),
          "D1": (512, 768), "A1": (1024, 1024), "B1": (1024, 704), "C1": (704, 2048)}
_STAGES = (("D0", "E0"), ("A0", "B0", "C0"), ("D1", "A1", "B1", "C1"))
_FFN0_SLABS = ("B0", "C0")
_REST0_SLABS = ("D0", "E0", "A0")
_PLACE = {
    "e_w_in": (1024, (("D0", 0),)), "e_w_out": (256, (("E0", 0),)),
    "o_w_out": (512, (("D1", 0),)), "o_w_in": (256, (("D1", 512),)),
    "ca_wq": (256, (("A0", 0), ("A1", 0))), "ca_wk": (256, (("A0", 256), ("A1", 256))),
    "ca_wv": (256, (("A0", 512), ("A1", 512))), "ca_wo": (256, (("A0", 768), ("A1", 768))),
    "ffn_w_down": (704, (("B0", 0), ("B1", 0))),
    "ffn_w_gate": (1024, (("C0", 0), ("C1", 0))), "ffn_w_up": (1024, (("C0", 1024), ("C1", 1024))),
}


def _params(sem=None):
    return pltpu.CompilerParams(dimension_semantics=sem, vmem_limit_bytes=VMEM_LIMIT)


def _tile(n, pref, mult=128):
    if n <= pref:
        return n
    t = (pref // mult) * mult
    while t >= mult:
        if n % t == 0:
            return t
        t -= mult
    return n


def _blk(name, layer=0):
    rows, where = _PLACE[name]
    slab, r0 = where[layer]
    assert r0 % rows == 0
    return slab, rows, r0 // rows


def _shards(slabs, name, layer=0):
    slab, rows, b = _blk(name, layer)
    return [(slabs[slab], (None, rows, _SLABS[slab][0]), (p, b, 0)) for p in range(N_CHIPS)]


_GELU_C = 0.7978845608028654
_GELU_A = 0.044715


def _gelu(x):
    t = jnp.tanh(_GELU_C * (x + _GELU_A * (x * x * x)))
    return 0.5 * x * (1.0 + t), t


def _gelu_grad(x, t):
    return 0.5 * (1.0 + t) + 0.5 * x * (1.0 - t * t) * (_GELU_C * (1.0 + 3.0 * _GELU_A * x * x))


def _sigmoid(x):
    return 1.0 / (1.0 + jnp.exp(-x))


def _mean(x):
    return jnp.mean(x, axis=-1, keepdims=True)


def _dot(a, b):
    return jnp.dot(a, b, preferred_element_type=F32)


def _dot_nt(a, b):
    return lax.dot_general(a, b, (((1,), (1,)), ((), ())), preferred_element_type=F32)


def _dot_tn(a, b):
    return lax.dot_general(a, b, (((0,), (0,)), ((), ())), preferred_element_type=F32)


def _rms_tile(xv, gv):
    return (xv * lax.rsqrt(_mean(xv * xv) + EPS)) * gv


def _rms_bwd_tile(xv, gv, dyv):
    r = lax.rsqrt(_mean(xv * xv) + EPS)
    xh = xv * r
    dyg = dyv * gv
    return r * (dyg - xh * _mean(dyg * xh)), jnp.sum(dyv * xh, axis=0, keepdims=True)


def _cols(p, width):
    return slice(p * width, (p + 1) * width)


def _sum_k(a, ws, k):
    tot = None
    for p in range(N_CHIPS):
        y = _dot(a[:, _cols(p, k)], ws[p][...])
        tot = y if tot is None else tot + y
    return tot


def _cat_nt(a, ws):
    return jnp.concatenate([_dot_nt(a, ws[p][...]) for p in range(N_CHIPS)], axis=1)


def _rows_call(name, tm, rows, fulls, outs, accs, body, scratch=()):
    S = rows[0].shape[-2]
    nr, nf, no, na = len(rows), len(fulls), len(outs), len(accs)

    def kern(*refs):
        r, f = refs[:nr], refs[nr:nr + nf]
        o, a = refs[nr + nf:nr + nf + no], refs[nr + nf + no:nr + nf + no + na]
        if na:
            @pl.when(pl.program_id(0) == 0)
            def _():
                for ref in a:
                    ref[...] = jnp.zeros_like(ref)
        body(r, f, o, a, refs[nr + nf + no + na:])

    def whole(shape):
        nd = len(shape)
        return pl.BlockSpec(tuple(shape), lambda i: (0,) * nd)

    def row_spec(shape):
        if len(shape) == 3:
            return pl.BlockSpec((shape[0], tm, shape[2]), lambda i: (0, i, 0))
        return pl.BlockSpec((tm, shape[1]), lambda i: (i, 0))

    def full_spec(x):
        if isinstance(x, tuple):
            _, bshape, bidx = x
            return pl.BlockSpec(bshape, lambda i: bidx, pipeline_mode=pl.Buffered(1))
        return whole(x.shape)

    out_shapes = [(S, o[0]) if len(o) == 2 else (o[0], S, o[1]) for o in outs]
    res = pl.pallas_call(
        kern, name=name, grid=(S // tm,),
        in_specs=[row_spec(x.shape) for x in rows] + [full_spec(x) for x in fulls],
        out_specs=[row_spec(s) for s in out_shapes] + [whole(shp) for shp, _ in accs],
        out_shape=[jax.ShapeDtypeStruct(s, o[-1]) for s, o in zip(out_shapes, outs)]
        + [jax.ShapeDtypeStruct(tuple(shp), dt) for shp, dt in accs],
        scratch_shapes=list(scratch),
        compiler_params=_params(("arbitrary",) if na else ("parallel",)),
    )(*rows, *[x[0] if isinstance(x, tuple) else x for x in fulls])
    return res[:no], res[no:]


def _mm_rows(a, w, *, nt=False, out_dtype=F32, tm=512, name):
    S = a.shape[0]
    N = w.shape[0] if nt else w.shape[1]

    def body(r, f, o, acc, s):
        av = r[0][...].astype(BF16)
        o[0][...] = (_dot_nt(av, f[0][...]) if nt else _dot(av, f[0][...])).astype(out_dtype)

    (y,), _ = _rows_call(name, _tile(S, tm), [a], [w], [(N, out_dtype)], [], body)
    return y


def _mm_tn(a, b, *, name):
    S, K1 = a.shape
    N = b.shape[1]
    tn = _tile(N, max(128, (ACC_BYTES // (4 * K1)) // 128 * 128))
    ts = _tile(S, 512 if K1 * a.dtype.itemsize * 512 <= 4 * 1024 * 1024 else 256)

    def body(a_ref, b_ref, o_ref):
        @pl.when(pl.program_id(1) == 0)
        def _():
            o_ref[...] = jnp.zeros_like(o_ref)

        o_ref[...] += _dot_tn(a_ref[...].astype(BF16), b_ref[...].astype(BF16))

    return pl.pallas_call(
        body, name=name, grid=(N // tn, S // ts),
        in_specs=[pl.BlockSpec((ts, K1), lambda j, s: (s, 0)), pl.BlockSpec((ts, tn), lambda j, s: (s, j))],
        out_specs=pl.BlockSpec((K1, tn), lambda j, s: (0, j)),
        out_shape=jax.ShapeDtypeStruct((K1, N), F32),
        compiler_params=_params(("parallel", "arbitrary")),
    )(a, b)


def _grad_to_slab(gslabs, wname, layer, a, b, *, a_cols=None, b_cols=None, chips=(0, N_CHIPS), name):
    slab, rows, bidx = _blk(wname, layer)
    width = _SLABS[slab][0]
    p0, n_p = chips
    assert p0 % n_p == 0
    S = a.shape[-2]

    def tile_bytes(x, ts):
        return ts * x.dtype.itemsize * (x.shape[2] * n_p if x.ndim == 3 else x.shape[1])

    acc_bytes = n_p * rows * (-(-width // 128) * 128) * 4
    ts = next(t for t in (2048, 1024, 512, 256, S) if S % t == 0
              and 2 * (tile_bytes(a, t) + tile_bytes(b, t) + acc_bytes) <= TN_VMEM_BYTES or t == S)

    def operand(x):
        if x.ndim == 3:
            return pl.BlockSpec((n_p, ts, x.shape[2]), lambda s: (p0 // n_p, s, 0))
        return pl.BlockSpec((ts, x.shape[1]), lambda s: (s, 0))

    def part(ref, cols, p):
        if len(ref.shape) == 3:
            return ref[p]
        return ref[...] if cols is None else ref[:, _cols(p, cols)]

    def body(a_ref, b_ref, slab_ref, o_ref):
        @pl.when(pl.program_id(0) == 0)
        def _():
            o_ref[...] = jnp.zeros_like(o_ref)

        for p in range(n_p):
            o_ref[p] += _dot_tn(part(a_ref, a_cols, p).astype(BF16), part(b_ref, b_cols, p).astype(BF16))

    g = gslabs[slab]
    out = pl.pallas_call(
        body, name=name, grid=(S // ts,),
        in_specs=[operand(a), operand(b), pl.BlockSpec(memory_space=pl.ANY)],
        out_specs=pl.BlockSpec((n_p, rows, width), lambda s: (p0 // n_p, bidx, 0)),
        out_shape=jax.ShapeDtypeStruct(g.shape, F32), input_output_aliases={2: 0},
        compiler_params=_params(("arbitrary",)),
    )(a, b, g)
    return {**gslabs, slab: out}


def _vec(g):
    return g.reshape(1, -1)


def _norm_mm(x, g, ws, *, split, out_dtype, name, tm=512):
    S, D = x.shape
    k, n = ws[0][1][1], ws[0][1][2]
    N = n if split == "k" else N_CHIPS * n

    def body(r, f, o, acc, s):
        xn = _rms_tile(r[0][...], f[0][...]).astype(BF16)
        o[0][...] = xn
        if split == "k":
            o[1][...] = _sum_k(xn, f[1:], k).astype(out_dtype)
        else:
            for p in range(N_CHIPS):
                o[1][:, _cols(p, n)] = _dot(xn, f[1 + p][...]).astype(out_dtype)

    (xn, y), _ = _rows_call(name, _tile(S, tm), [x], [_vec(g)] + ws, [(D, BF16), (N, out_dtype)], [], body)
    return xn, y


def _mm_k(a, ws, *, add=None, out_dtype=F32, name, tm=512):
    S = a.shape[-2]
    k, n = ws[0][1][1], ws[0][1][2]
    has_add = add is not None

    def body(r, f, o, acc, s):
        if a.ndim == 3:
            y = None
            for p in range(N_CHIPS):
                t = _dot(r[0][p].astype(BF16), f[p][...])
                y = t if y is None else y + t
        else:
            y = _sum_k(r[0][...].astype(BF16), f, k)
        if has_add:
            y = y + r[1][...]
        o[0][...] = y.astype(out_dtype)

    (y,), _ = _rows_call(name, _tile(S, tm), [a] + ([add] if has_add else []), ws, [(n, out_dtype)], [], body)
    return y


def _mm_k_t(terms, *, out_dtype=F32, name, tm=512):
    S = terms[0][0].shape[0]
    k = terms[0][1][0][1][1]

    def body(r, f, o, acc, s):
        y = None
        for t in range(len(terms)):
            yt = _cat_nt(r[t][...].astype(BF16), f[N_CHIPS * t:N_CHIPS * (t + 1)])
            y = yt if y is None else y + yt
        o[0][...] = y.astype(out_dtype)

    (y,), _ = _rows_call(name, _tile(S, tm), [a for a, _ in terms], [w for _, ws in terms for w in ws],
                         [(N_CHIPS * k, out_dtype)], [], body)
    return y


def _rms_fwd(x, g, *, name):
    def body(r, f, o, acc, s):
        o[0][...] = _rms_tile(r[0][...], f[0][...]).astype(BF16)

    (y,), _ = _rows_call(name, _tile(x.shape[0], 256, 8), [x], [_vec(g)], [(x.shape[1], BF16)], [], body)
    return y


def _rms_dg(x, g, dy, *, name):
    def body(r, f, o, acc, s):
        acc[0][...] += _rms_bwd_tile(r[0][...], f[0][...], r[1][...])[1]

    _, (dg,) = _rows_call(name, _tile(x.shape[0], 256, 8), [x, dy], [_vec(g)], [], [((1, x.shape[1]), F32)], body)
    return dg


def _ffn_up(x, g, wg, wu, *, name, tm=256):
    S, D = x.shape
    h = wg[0][1][2]

    def body(r, f, o, acc, s):
        xn = _rms_tile(r[0][...], f[0][...]).astype(BF16)
        o[0][...] = xn
        for p in range(N_CHIPS):
            gate = _dot(xn, f[1 + p][...])
            up = _dot(xn, f[1 + N_CHIPS + p][...])
            o[1][p] = gate.astype(BF16)
            o[2][p] = up.astype(BF16)
            o[3][p] = (gate * _sigmoid(gate) * up).astype(BF16)

    (xn, gate, up, hid), _ = _rows_call(name, _tile(S, tm), [x], [_vec(g)] + wg + wu,
                                        [(D, BF16), (N_CHIPS, h, BF16), (N_CHIPS, h, BF16), (N_CHIPS, h, BF16)], [],
                                        body)
    return xn, gate, up, hid


def _ffn_bwd_hidden(dy, wd, gate, up, token=None, *, name, tm=256):
    S = dy.shape[0]
    h = wd[0][1][1]

    def body(r, f, o, acc, s):
        dyv = r[0][...]
        if token is not None:
            dyv = dyv + jnp.sum(f[N_CHIPS][...])
        dyb = dyv.astype(BF16)
        for p in range(N_CHIPS):
            dh = _dot_nt(dyb, f[p][...])
            gv = r[1][p].astype(F32)
            sg = _sigmoid(gv)
            o[0][p] = (dh * r[2][p].astype(F32) * (sg * (1.0 + gv * (1.0 - sg)))).astype(BF16)
            o[1][p] = (dh * gv * sg).astype(BF16)

    (dg, du), _ = _rows_call(name, _tile(S, tm), [dy, gate, up], wd + ([] if token is None else [token]),
                             [(N_CHIPS, h, BF16), (N_CHIPS, h, BF16)], [], body)
    return dg, du


def _ffn_in_bwd(dg, du, wg, wu, x, g, dres, *, name, tm=256):
    S, D = x.shape

    def body(r, f, o, acc, s):
        tot = None
        for p in range(N_CHIPS):
            y = _dot_nt(r[0][p], f[1 + p][...]) + _dot_nt(r[1][p], f[1 + N_CHIPS + p][...])
            tot = y if tot is None else tot + y
        dx, dgn = _rms_bwd_tile(r[2][...], f[0][...], tot)
        o[0][...] = dx + r[3][...]
        acc[0][...] += dgn

    (dx,), (dgn,) = _rows_call(name, _tile(S, tm), [dg, du, x, dres], [_vec(g)] + wg + wu, [(D, F32)],
                               [((1, D), F32)], body)
    return dx, dgn


def _norm_bwd_k(da, ws, x, g, dres, *, name, tm=512):
    S, D = x.shape

    def body(r, f, o, acc, s):
        dx, dg = _rms_bwd_tile(r[1][...], f[0][...], _cat_nt(r[0][...].astype(BF16), f[1:]))
        o[0][...] = dx + r[2][...]
        acc[0][...] += dg

    (dx,), (dg,) = _rows_call(name, _tile(S, tm), [da, x, dres], [_vec(g)] + ws, [(D, F32)], [((1, D), F32)], body)
    return dx, dg


def _norm_bwd_n(das, ws, x, g, dres, *, name, tm=256):
    S, D = x.shape
    n = ws[0][1][2]

    def body(r, f, o, acc, s):
        tot = None
        for p in range(N_CHIPS):
            y = _dot_nt(r[p // 2][:, _cols(p % 2, n)], f[1 + p][...])
            tot = y if tot is None else tot + y
        dx, dg = _rms_bwd_tile(r[2][...], f[0][...], tot)
        o[0][...] = dx + r[3][...]
        acc[0][...] += dg

    (dx,), (dg,) = _rows_call(name, _tile(S, tm), list(das) + [x, dres], [_vec(g)] + ws, [(D, F32)], [((1, D), F32)],
                              body)
    return dx, dg


def _ln_stats(v):
    mu = _mean(v)
    xc = v - mu
    rstd = lax.rsqrt(_mean(xc * xc) + EPS)
    return xc * rstd, rstd


def _even_fwd(proj, wm, bcol, cw, cb, lg, lb, *, name):
    S = proj.shape[0]
    tm = _tile(S, 256)
    hb = tm // CONV_HALO
    nblk = tm // GMLP_BLOCK

    def body(p_ref, halo_ref, wm_ref, b_ref, cw_ref, cb_ref, lg_ref, lb_ref, mix_ref, hc_ref, hext_ref):
        i = pl.program_id(0)
        gu, _ = _gelu(p_ref[:, 0:A_WIDTH])
        gv, _ = _gelu(p_ref[:, A_WIDTH:2 * A_WIDTH])
        vn, _ = _ln_stats(gv)
        vnb = vn.astype(BF16)
        for n in range(nblk):
            rows = slice(n * GMLP_BLOCK, (n + 1) * GMLP_BLOCK)
            for g in range(A_GROUPS):
                cols = slice(g * GMLP_BLOCK, (g + 1) * GMLP_BLOCK)
                sg = jnp.dot(wm_ref[g], vnb[rows, cols], preferred_element_type=F32) + b_ref[g]
                mix_ref[rows, cols] = (gu[rows, cols] * sg).astype(BF16)
        h = p_ref[:, 1024:1536] * _sigmoid(p_ref[:, 1536:2048])
        hh = halo_ref[:, 0:B_WIDTH] * _sigmoid(halo_ref[:, B_WIDTH:2 * B_WIDTH])
        hext_ref[0:CONV_HALO, :] = jnp.where(i > 0, hh, 0.0)
        hext_ref[CONV_HALO:CONV_HALO + tm, :] = h
        acc = jnp.zeros((tm, B_WIDTH), F32)
        for k in range(CONV_WIDTH):
            acc = acc + cw_ref[k:k + 1, :] * hext_ref[pl.ds(k + CONV_HALO - CONV_WIDTH + 1, tm), :]
        hc = acc + cb_ref[...]
        hc_ref[...] = hc
        hhat, _ = _ln_stats(hc)
        hl = hhat * lg_ref[...] + lb_ref[...]
        mix_ref[:, A_WIDTH:A_WIDTH + B_WIDTH] = (hl * _sigmoid(hl)).astype(BF16)

    vec = pl.BlockSpec((1, B_WIDTH), lambda i: (0, 0))
    return pl.pallas_call(
        body, name=name, grid=(S // tm,),
        in_specs=[
            pl.BlockSpec((tm, 2048), lambda i: (i, 0)),
            pl.BlockSpec((CONV_HALO, 1024), lambda i: (jnp.maximum(i * hb - 1, 0), 1)),
            pl.BlockSpec((A_GROUPS, GMLP_BLOCK, GMLP_BLOCK), lambda i: (0, 0, 0)),
            pl.BlockSpec((A_GROUPS, GMLP_BLOCK, 1), lambda i: (0, 0, 0)),
            pl.BlockSpec((CONV_HALO, B_WIDTH), lambda i: (0, 0)),
            vec, vec, vec,
        ],
        out_specs=[pl.BlockSpec((tm, 1024), lambda i: (i, 0)), pl.BlockSpec((tm, B_WIDTH), lambda i: (i, 0))],
        out_shape=[jax.ShapeDtypeStruct((S, 1024), BF16), jax.ShapeDtypeStruct((S, B_WIDTH), F32)],
        scratch_shapes=[pltpu.VMEM((tm + CONV_HALO, B_WIDTH), F32)],
        compiler_params=_params(("parallel",)),
    )(proj, proj, wm, bcol, cw, cb, lg, lb)


def _even_bwd1(proj, dmix, hc, wm, wmt, bcol, lg, lb, *, name):
    S = proj.shape[0]
    tm = _tile(S, 256)
    nblk = tm // GMLP_BLOCK

    def body(p_ref, dm_ref, hc_ref, wm_ref, wmt_ref, b_ref, lg_ref, lb_ref,
             dpa_ref, dhc_ref, dwm_ref, db_ref, dlg_ref, dlb_ref, dcb_ref, dgu_ref, dvn_ref):
        @pl.when(pl.program_id(0) == 0)
        def _():
            dwm_ref[...] = jnp.zeros_like(dwm_ref)
            db_ref[...] = jnp.zeros_like(db_ref)
            dlg_ref[...] = jnp.zeros_like(dlg_ref)
            dlb_ref[...] = jnp.zeros_like(dlb_ref)
            dcb_ref[...] = jnp.zeros_like(dcb_ref)

        au = p_ref[:, 0:A_WIDTH]
        av = p_ref[:, A_WIDTH:2 * A_WIDTH]
        gu, tu = _gelu(au)
        gv, tv = _gelu(av)
        vn, rstd = _ln_stats(gv)
        vnb = vn.astype(BF16)
        for n in range(nblk):
            rows = slice(n * GMLP_BLOCK, (n + 1) * GMLP_BLOCK)
            for g in range(A_GROUPS):
                cols = slice(g * GMLP_BLOCK, (g + 1) * GMLP_BLOCK)
                vb = vnb[rows, cols]
                sg = jnp.dot(wm_ref[g], vb, preferred_element_type=F32) + b_ref[g]
                da = dm_ref[rows, cols]
                dsg = da * gu[rows, cols]
                dgu_ref[rows, cols] = da * sg
                dsgb = dsg.astype(BF16)
                dwm_ref[g] += _dot_nt(dsgb, vb)
                db_ref[g] += jnp.sum(dsg, axis=1, keepdims=True)
                dvn_ref[rows, cols] = jnp.dot(wmt_ref[g], dsgb, preferred_element_type=F32)
        dvn = dvn_ref[...]
        dgv = rstd * (dvn - _mean(dvn) - vn * _mean(dvn * vn))
        dpa_ref[:, 0:A_WIDTH] = (dgu_ref[...] * _gelu_grad(au, tu)).astype(BF16)
        dpa_ref[:, A_WIDTH:2 * A_WIDTH] = (dgv * _gelu_grad(av, tv)).astype(BF16)
        hhat, rstd2 = _ln_stats(hc_ref[...])
        lgv = lg_ref[...]
        hl = hhat * lgv + lb_ref[...]
        s = _sigmoid(hl)
        dhl = dm_ref[:, A_WIDTH:A_WIDTH + B_WIDTH] * (s * (1.0 + hl * (1.0 - s)))
        dlg_ref[...] += jnp.sum(dhl * hhat, axis=0, keepdims=True)
        dlb_ref[...] += jnp.sum(dhl, axis=0, keepdims=True)
        dhh = dhl * lgv
        dhc = rstd2 * (dhh - _mean(dhh) - hhat * _mean(dhh * hhat))
        dcb_ref[...] += jnp.sum(dhc, axis=0, keepdims=True)
        dhc_ref[...] = dhc

    vec = pl.BlockSpec((1, B_WIDTH), lambda i: (0, 0))
    w3 = pl.BlockSpec((A_GROUPS, GMLP_BLOCK, GMLP_BLOCK), lambda i: (0, 0, 0))
    b3 = pl.BlockSpec((A_GROUPS, GMLP_BLOCK, 1), lambda i: (0, 0, 0))
    return pl.pallas_call(
        body, name=name, grid=(S // tm,),
        in_specs=[
            pl.BlockSpec((tm, 1024), lambda i: (i, 0)),
            pl.BlockSpec((tm, 1024), lambda i: (i, 0)),
            pl.BlockSpec((tm, B_WIDTH), lambda i: (i, 0)),
            w3, w3, b3, vec, vec,
        ],
        out_specs=[pl.BlockSpec((tm, 1024), lambda i: (i, 0)), pl.BlockSpec((tm, B_WIDTH), lambda i: (i, 0)),
                   w3, b3, vec, vec, vec],
        out_shape=[
            jax.ShapeDtypeStruct((S, 1024), BF16), jax.ShapeDtypeStruct((S, B_WIDTH), F32),
            jax.ShapeDtypeStruct((A_GROUPS, GMLP_BLOCK, GMLP_BLOCK), F32),
            jax.ShapeDtypeStruct((A_GROUPS, GMLP_BLOCK, 1), F32),
            jax.ShapeDtypeStruct((1, B_WIDTH), F32), jax.ShapeDtypeStruct((1, B_WIDTH), F32),
            jax.ShapeDtypeStruct((1, B_WIDTH), F32),
        ],
        scratch_shapes=[pltpu.VMEM((tm, A_WIDTH), F32), pltpu.VMEM((tm, A_WIDTH), F32)],
        compiler_params=_params(("arbitrary",)),
    )(proj, dmix, hc, wm, wmt, bcol, lg, lb)


def _even_bwd2(proj, dhc, cw, *, name):
    S = proj.shape[0]
    tm = _tile(S, 256)
    hb = tm // CONV_HALO
    nt = S // tm
    last_halo = S // CONV_HALO - 1
    lo = CONV_HALO - CONV_WIDTH + 1

    def body(p_ref, halo_ref, d_ref, dnext_ref, cw_ref, dpb_ref, dcw_ref, hext_ref, dext_ref):
        i = pl.program_id(0)

        @pl.when(i == 0)
        def _():
            dcw_ref[...] = jnp.zeros_like(dcw_ref)

        ba = p_ref[:, 0:B_WIDTH]
        sg = _sigmoid(p_ref[:, B_WIDTH:2 * B_WIDTH])
        hh = halo_ref[:, 0:B_WIDTH] * _sigmoid(halo_ref[:, B_WIDTH:2 * B_WIDTH])
        hext_ref[0:CONV_HALO, :] = jnp.where(i > 0, hh, 0.0)
        hext_ref[CONV_HALO:CONV_HALO + tm, :] = ba * sg
        dhc_t = d_ref[...]
        dext_ref[0:tm, :] = dhc_t
        dext_ref[tm:tm + CONV_HALO, :] = jnp.where(i < nt - 1, dnext_ref[...], 0.0)
        dh = jnp.zeros((tm, B_WIDTH), F32)
        for k in range(CONV_WIDTH):
            dh = dh + cw_ref[k:k + 1, :] * dext_ref[pl.ds(CONV_WIDTH - 1 - k, tm), :]
            dcw_ref[k:k + 1, :] += jnp.sum(dhc_t * hext_ref[pl.ds(k + lo, tm), :], axis=0, keepdims=True)
        dpb_ref[:, 0:B_WIDTH] = (dh * sg).astype(BF16)
        dpb_ref[:, B_WIDTH:2 * B_WIDTH] = (dh * ba * sg * (1.0 - sg)).astype(BF16)

    return pl.pallas_call(
        body, name=name, grid=(nt,),
        in_specs=[
            pl.BlockSpec((tm, 1024), lambda i: (i, 1)),
            pl.BlockSpec((CONV_HALO, 1024), lambda i: (jnp.maximum(i * hb - 1, 0), 1)),
            pl.BlockSpec((tm, B_WIDTH), lambda i: (i, 0)),
            pl.BlockSpec((CONV_HALO, B_WIDTH), lambda i: (jnp.minimum((i + 1) * hb, last_halo), 0)),
            pl.BlockSpec((CONV_HALO, B_WIDTH), lambda i: (0, 0)),
        ],
        out_specs=[pl.BlockSpec((tm, 1024), lambda i: (i, 0)), pl.BlockSpec((CONV_HALO, B_WIDTH), lambda i: (0, 0))],
        out_shape=[jax.ShapeDtypeStruct((S, 1024), BF16), jax.ShapeDtypeStruct((CONV_HALO, B_WIDTH), F32)],
        scratch_shapes=[pltpu.VMEM((tm + CONV_HALO, B_WIDTH), F32), pltpu.VMEM((tm + CONV_HALO, B_WIDTH), F32)],
        compiler_params=_params(("arbitrary",)),
    )(proj, proj, dhc, dhc, cw)


_CA_SCALE = CA_HEAD_DIM ** -0.5


def _softmax_rows(s):
    e = jnp.exp(s - jnp.max(s, axis=-1, keepdims=True))
    return e / jnp.sum(e, axis=-1, keepdims=True)


def _attn_fwd(q, k, v, *, name):
    S = q.shape[0]

    def body(r, f, o, acc, s):
        for h in range(CA_HEADS):
            cols = _cols(h, CA_HEAD_DIM)
            p = _softmax_rows(_dot_nt(r[0][:, cols], f[0][:, cols]) * _CA_SCALE)
            o[0][:, cols] = _dot(p.astype(BF16), f[1][:, cols]).astype(BF16)

    (o_,), _ = _rows_call(name, _tile(S, 512), [q], [k, v], [(D_MODEL, BF16)], [], body)
    return o_


def _attn_bwd(dy, wo, q, k, v, *, name):
    S = q.shape[0]
    M = k.shape[0]

    def body(r, f, o, acc, s):
        dyb = r[0][...].astype(BF16)
        for h in range(CA_HEADS):
            cols = _cols(h, CA_HEAD_DIM)
            qh = r[1][:, cols]
            kh = f[0][:, cols]
            vh = f[1][:, cols]
            doh = _dot_nt(dyb, f[2 + h][...]).astype(BF16)
            p = _softmax_rows(_dot_nt(qh, kh) * _CA_SCALE)
            acc[1][:, cols] += _dot_tn(p.astype(BF16), doh)
            dp = _dot_nt(doh, vh)
            ds = (p * (dp - jnp.sum(dp * p, axis=-1, keepdims=True)) * _CA_SCALE).astype(BF16)
            o[0][:, cols] = _dot(ds, kh).astype(BF16)
            acc[0][:, cols] += _dot_tn(ds, qh)

    (dq,), (dk, dv) = _rows_call(name, _tile(S, 512), [dy, q], [k, v] + wo, [(D_MODEL, BF16)],
                                 [((M, D_MODEL), F32), ((M, D_MODEL), F32)], body)
    return dq, dk, dv


def _s5_readout(xs2, cd, u, d, *, name, tm=256):
    def body(r, f, o, acc, s):
        y = _dot(r[0][...].astype(BF16), f[0][...]) + f[1][...] * r[1][...]
        o[0][...] = y
        o[1][...] = _gelu(y)[0].astype(BF16)

    (y, yg), _ = _rows_call(name, _tile(xs2.shape[0], tm), [xs2, u], [cd, d], [(C_WIDTH, F32), (C_WIDTH, BF16)], [],
                            body)
    return y, yg


def _glu_out(yg, ws, x, *, name, tm=512):
    n = ws[0][1][2]

    def body(r, f, o, acc, s):
        ygv = r[0][...]
        ov = [_dot(ygv, f[p][...]) for p in range(N_CHIPS)]
        for p in range(N_CHIPS):
            o[0][:, _cols(p, n)] = ov[p].astype(BF16)
        for p in range(2):
            o[1][:, _cols(p, n)] = r[1][:, _cols(p, n)] + ov[p] * _sigmoid(ov[2 + p])

    (o_, y), _ = _rows_call(name, _tile(x.shape[0], tm), [yg, x], ws, [(2 * D_MODEL, BF16), (D_MODEL, F32)], [], body)
    return o_, y


def _glu_out_bwd(o_, dy, ws, y, u, d, *, name, tm=256):
    n = ws[0][1][2]

    def body(r, f, o, acc, s):
        o1 = r[0][:, 0:D_MODEL].astype(F32)
        sg = _sigmoid(r[0][:, D_MODEL:2 * D_MODEL].astype(F32))
        dyv = r[1][...]
        do1 = (dyv * sg).astype(BF16)
        do2 = (dyv * o1 * sg * (1.0 - sg)).astype(BF16)
        o[0][:, 0:D_MODEL] = do1
        o[0][:, D_MODEL:2 * D_MODEL] = do2
        dyg = None
        for p in range(N_CHIPS):
            t = _dot_nt((do1 if p < 2 else do2)[:, _cols(p % 2, n)], f[1 + p][...])
            dyg = t if dyg is None else dyg + t
        yv = r[2][...]
        dys = dyg * _gelu_grad(yv, _gelu(yv)[1])
        o[1][...] = dys.astype(BF16)
        o[2][...] = f[0][...] * dys
        acc[0][...] += jnp.sum(dys * r[3][...], axis=0, keepdims=True)

    (do, dys, dus), (dd,) = _rows_call(name, _tile(dy.shape[0], tm), [o_, dy, y, u], [d] + ws,
                                       [(2 * D_MODEL, BF16), (C_WIDTH, BF16), (C_WIDTH, F32)], [((1, C_WIDTH), F32)],
                                       body)
    return do, dys, dus, dd


def _s5_in_bwd(gs2, bd, dus, ws, x, g, dres, *, name, tm=256):
    D = x.shape[1]

    def body(r, f, o, acc, s):
        du = (_dot_nt(r[0][...].astype(BF16), f[1][...]) + r[1][...]).astype(BF16)
        o[0][...] = du
        dx, dg = _rms_bwd_tile(r[2][...], f[0][...], _cat_nt(du, f[2:]))
        o[1][...] = dx + r[3][...]
        acc[0][...] += dg

    (du, dx), (dg,) = _rows_call(name, _tile(x.shape[0], tm), [gs2, dus, x, dres], [_vec(g), bd] + ws,
                                 [(C_WIDTH, BF16), (D, F32)], [((1, D), F32)], body)
    return du, dx, dg


_SCAN_CHUNK = 128
_SCAN_UNROLL = 8
_RE = slice(0, STATE_ROWS)
_IM = slice(STATE_ROWS, 2 * STATE_ROWS)


def _scan_fwd(bu, a, *, name):
    S = bu.shape[0]
    tc = _tile(S, _SCAN_CHUNK, 8)

    def body(bu_ref, a_ref, xs_ref, st_ref):
        @pl.when(pl.program_id(0) == 0)
        def _():
            st_ref[...] = jnp.zeros_like(st_ref)

        ar = a_ref[_RE, :]
        ai = a_ref[_IM, :]

        def step(t, carry):
            xr, xi = carry
            nr = ar * xr - ai * xi + bu_ref[t, _RE, :]
            ni = ar * xi + ai * xr + bu_ref[t, _IM, :]
            xs_ref[t, _RE, :] = nr
            xs_ref[t, _IM, :] = ni
            return nr, ni

        xr, xi = lax.fori_loop(0, tc, step, (st_ref[_RE, :], st_ref[_IM, :]), unroll=_SCAN_UNROLL)
        st_ref[_RE, :] = xr
        st_ref[_IM, :] = xi

    blk = pl.BlockSpec((tc, 2 * STATE_ROWS, STATE_LANES), lambda i: (i, 0, 0))
    return pl.pallas_call(
        body, name=name, grid=(S // tc,),
        in_specs=[blk, pl.BlockSpec((2 * STATE_ROWS, STATE_LANES), lambda i: (0, 0))], out_specs=blk,
        out_shape=jax.ShapeDtypeStruct(bu.shape, F32),
        scratch_shapes=[pltpu.VMEM((2 * STATE_ROWS, STATE_LANES), F32)],
        compiler_params=_params(("arbitrary",)),
    )(bu, a)


def _scan_bwd(dxs, xs, a, *, name):
    S = dxs.shape[0]
    tc = _tile(S, _SCAN_CHUNK, 8)
    nc = S // tc

    def body(dx_ref, xs_ref, a_ref, g_ref, da_ref, st_ref):
        @pl.when(pl.program_id(0) == 0)
        def _():
            st_ref[...] = jnp.zeros_like(st_ref)
            da_ref[...] = jnp.zeros_like(da_ref)

        ar = a_ref[_RE, :]
        ai = a_ref[_IM, :]

        def step(j, carry):
            gr, gi, dar, dai = carry
            t = tc - 1 - j
            xr = xs_ref[t, _RE, :]
            xi = xs_ref[t, _IM, :]
            dar = dar + gr * xr + gi * xi
            dai = dai + gi * xr - gr * xi
            nr = dx_ref[t, _RE, :] + ar * gr + ai * gi
            ni = dx_ref[t, _IM, :] + ar * gi - ai * gr
            g_ref[t, _RE, :] = nr
            g_ref[t, _IM, :] = ni
            return nr, ni, dar, dai

        init = (st_ref[_RE, :], st_ref[_IM, :], da_ref[_RE, :], da_ref[_IM, :])
        gr, gi, dar, dai = lax.fori_loop(0, tc, step, init, unroll=_SCAN_UNROLL)
        st_ref[_RE, :] = gr
        st_ref[_IM, :] = gi
        da_ref[_RE, :] = dar
        da_ref[_IM, :] = dai

    blk = pl.BlockSpec((tc, 2 * STATE_ROWS, STATE_LANES), lambda i: (nc - 1 - i, 0, 0))
    vec = pl.BlockSpec((2 * STATE_ROWS, STATE_LANES), lambda i: (0, 0))
    return pl.pallas_call(
        body, name=name, grid=(nc,), in_specs=[blk, blk, vec], out_specs=[blk, vec],
        out_shape=[jax.ShapeDtypeStruct(dxs.shape, F32), jax.ShapeDtypeStruct((2 * STATE_ROWS, STATE_LANES), F32)],
        scratch_shapes=[pltpu.VMEM((2 * STATE_ROWS, STATE_LANES), F32)],
        compiler_params=_params(("arbitrary",)),
    )(dxs, xs, a)


def _loss_head(x, g, target, *, name):
    S, D = x.shape

    def body(r, f, o, acc, s):
        xv = r[0][...]
        gv = f[0][...]
        rs = lax.rsqrt(_mean(xv * xv) + EPS)
        xh = xv * rs
        err = xh * gv - r[1][...]
        acc[1][...] += 0.5 * jnp.sum(_mean(err * err), axis=0, keepdims=True)
        dy = err * (1.0 / D)
        dyg = dy * gv
        o[0][...] = rs * (dyg - xh * _mean(dyg * xh))
        acc[0][...] += jnp.sum(dy * xh, axis=0, keepdims=True)

    (dx,), (dg, loss) = _rows_call(name, _tile(S, 256, 8), [x, target], [_vec(g)], [(D, F32)],
                                   [((1, D), F32), ((1, 128), F32)], body)
    return dx, dg, loss


_ADAM_C1 = 1.0 - ADAM_B1 ** ADAM_STEP
_ADAM_C2 = 1.0 - ADAM_B2 ** ADAM_STEP
_ONE_BLOCK_BYTES = 8 * 1024 * 1024


def _adamw_math(w, g, m, v):
    nm = ADAM_B1 * m + (1.0 - ADAM_B1) * g
    nv = ADAM_B2 * v + (1.0 - ADAM_B2) * (g * g)
    m_hat = nm / _ADAM_C1
    v_hat = nv / _ADAM_C2
    return -ADAM_LR * (m_hat / (jnp.sqrt(v_hat) + ADAM_EPS) + ADAM_WD * w), nm, nv


def _adamw_shard(w, gsrc, m, v, *, name):
    R, C = w.shape
    n_l = len(gsrc)
    rows = R // n_l
    tr = rows
    for _, r0 in gsrc:
        tr = math.gcd(tr, r0) if r0 else tr
    tr = _tile(tr, 256, 8) if tr > 256 else tr
    nb = rows // tr
    assert rows % tr == 0 and all(r0 % tr == 0 for _, r0 in gsrc)

    def body(*refs):
        w_ref, g_refs, (m_ref, v_ref, go_ref, d_ref, nm_ref, nv_ref) = refs[0], refs[1:1 + n_l], refs[1 + n_l:]
        layer = pl.program_id(0) // nb
        gv = g_refs[0][...]
        for l in range(1, n_l):
            gv = jnp.where(layer == l, g_refs[l][...], gv)
        go_ref[...] = gv
        d_ref[...], nm_ref[...], nv_ref[...] = _adamw_math(w_ref[...], gv, m_ref[...], v_ref[...])

    def g_spec(l, r0):
        return pl.BlockSpec((tr, C), lambda i: (r0 // tr + jnp.clip(i - l * nb, 0, nb - 1), 0))

    blk = pl.BlockSpec((tr, C), lambda i: (i, 0))
    out = jax.ShapeDtypeStruct((R, C), F32)
    return pl.pallas_call(
        body, name=name, grid=(R // tr,),
        in_specs=[blk] + [g_spec(l, r0) for l, (_, r0) in enumerate(gsrc)] + [blk, blk], out_specs=[blk] * 4,
        out_shape=[out] * 4, compiler_params=_params(("parallel",)),
    )(w, *[g for g, _ in gsrc], m, v)


def _adamw_small(ws, gs, ms, vs, *, name):
    n = len(ws)

    def body(*refs):
        w_r, g_r, m_r, v_r = refs[:n], refs[n:2 * n], refs[2 * n:3 * n], refs[3 * n:4 * n]
        d_r, nm_r, nv_r = refs[4 * n:5 * n], refs[5 * n:6 * n], refs[6 * n:7 * n]
        for k in range(n):
            d_r[k][...], nm_r[k][...], nv_r[k][...] = _adamw_math(w_r[k][...], g_r[k][...], m_r[k][...], v_r[k][...])

    vm = pl.BlockSpec(memory_space=pltpu.VMEM)
    out = [jax.ShapeDtypeStruct(w.shape, F32) for w in ws]
    res = pl.pallas_call(body, name=name, in_specs=[vm] * (4 * n), out_specs=[vm] * (3 * n), out_shape=out * 3,
                         compiler_params=pltpu.CompilerParams(vmem_limit_bytes=VMEM_LIMIT))(*ws, *gs, *ms, *vs)
    return res[:n], res[n:2 * n], res[2 * n:]


def _sum_slots(x, *, name):
    n, R, C = x.shape
    tr = R if (n + 1) * R * C * 4 <= _ONE_BLOCK_BYTES else _tile(R, 256, 8)

    def body(x_ref, o_ref):
        acc = x_ref[0]
        for k in range(1, n):
            acc = acc + x_ref[k]
        o_ref[...] = acc

    return pl.pallas_call(
        body, name=name, grid=(R // tr,),
        in_specs=[pl.BlockSpec((n, tr, C), lambda i: (0, i, 0))], out_specs=pl.BlockSpec((tr, C), lambda i: (i, 0)),
        out_shape=jax.ShapeDtypeStruct((R, C), F32), compiler_params=_params(("parallel",)),
    )(x)


def _pair_sum(g, r, half, *, name):
    n, R, C = g.shape
    Rh = R // 2
    tr = _tile(Rh, 256, 8)
    nb = Rh // tr

    def body(half_ref, g_ref, r_ref, o_ref):
        o_ref[...] = (g_ref[...] + r_ref[...]).astype(BF16)

    return pl.pallas_call(
        body, name=name,
        grid_spec=pltpu.PrefetchScalarGridSpec(
            num_scalar_prefetch=1, grid=(n, nb),
            in_specs=[pl.BlockSpec((1, tr, C), lambda p, i, h: (p, h[0] * nb + i, 0)),
                      pl.BlockSpec((1, tr, C), lambda p, i, h: (p, i, 0))],
            out_specs=pl.BlockSpec((1, tr, C), lambda p, i, h: (p, i, 0)),
        ),
        out_shape=jax.ShapeDtypeStruct((n, Rh, C), BF16), compiler_params=_params(("parallel", "parallel")),
    )(half, g, r)


def _chip_sum(g, r, slots, where, *, name):
    n, R, C = g.shape
    Rh = R // 2
    tr = _tile(Rh, 256, 8)
    nb = Rh // tr

    def body(w_ref, g_ref, r_ref, s_ref, o_ref):
        acc = g_ref[0] + r_ref[0]
        for k in range(slots.shape[0]):
            acc = acc + s_ref[k].astype(F32)
        o_ref[...] = acc

    return pl.pallas_call(
        body, name=name,
        grid_spec=pltpu.PrefetchScalarGridSpec(
            num_scalar_prefetch=1, grid=(nb,),
            in_specs=[pl.BlockSpec((1, tr, C), lambda i, w: (w[0], w[1] * nb + i, 0)),
                      pl.BlockSpec((1, tr, C), lambda i, w: (w[0], i, 0)),
                      pl.BlockSpec((slots.shape[0], tr, C), lambda i, w: (0, i, 0))],
            out_specs=pl.BlockSpec((tr, C), lambda i, w: (w[1] * nb + i, 0)),
        ),
        out_shape=jax.ShapeDtypeStruct((R, C), F32), compiler_params=_params(("parallel",)),
    )(where, g, r, slots)


ANY = pl.BlockSpec(memory_space=pl.ANY)


def _place():
    return lax.axis_index("x"), lax.axis_index("y"), lax.axis_index("c")


def _other_chips(x, y):
    return [(1 - x, y), (x, 1 - y), (1 - x, 1 - y)]


def _allgather_small(v, *, name):
    R, C = v.shape

    def body(x_ref, out_ref, send_sems, recv_sems, local_sem):
        x, y, c = _place()
        me, sibling = (x, y, c), (x, y, 1 - c)
        chips = _other_chips(x, y)

        def rows(px, py, pc):
            return out_ref.at[pl.ds((4 * px + 2 * py + pc) * R, R), :]

        def copy(k, block, to, src=None):
            return pltpu.make_async_remote_copy(
                src_ref=rows(*block) if src is None else src, dst_ref=rows(*block),
                send_sem=send_sems.at[k], recv_sem=recv_sems.at[k], device_id=to, device_id_type=MESH)

        mine = pltpu.make_async_copy(x_ref, rows(*me), local_sem)
        mine.start()
        first = [copy(0, me, sibling, src=x_ref)]
        first += [copy(1 + j, me, (*chip, c), src=x_ref) for j, chip in enumerate(chips)]
        for cp in first:
            cp.start()
        passed = [copy(4 + j, (*chip, c), sibling) for j, chip in enumerate(chips)]
        for j, chip in enumerate(chips):
            copy(1 + j, (*chip, c), me).wait_recv()
            passed[j].start()
        copy(0, sibling, me).wait_recv()
        for j, chip in enumerate(chips):
            copy(4 + j, (*chip, 1 - c), me).wait_recv()
        for cp in first + passed:
            cp.wait_send()
        mine.wait()

    return pl.pallas_call(
        body, name=name, out_shape=jax.ShapeDtypeStruct((N_DEV * R, C), v.dtype),
        in_specs=[pl.BlockSpec(memory_space=pltpu.VMEM)], out_specs=pl.BlockSpec(memory_space=pltpu.VMEM),
        scratch_shapes=[pltpu.SemaphoreType.DMA((7,)), pltpu.SemaphoreType.DMA((7,)), pltpu.SemaphoreType.DMA],
        compiler_params=pltpu.CompilerParams(vmem_limit_bytes=VMEM_LIMIT),
    )(v)


def _aliased_comm_call(body, bufs, n_sems, *, name):
    n = len(bufs)
    return pl.pallas_call(
        body, name=name, out_shape=[jax.ShapeDtypeStruct(b.shape, b.dtype) for b in bufs],
        in_specs=[ANY] * n, out_specs=[ANY] * n, input_output_aliases={k: k for k in range(n)},
        scratch_shapes=[pltpu.SemaphoreType.DMA((n_sems,)), pltpu.SemaphoreType.DMA((n_sems,))],
    )(*bufs)


def _allgather_chips(bufs, *, name):
    n = len(bufs)

    def body(*refs):
        outs, send_sems, recv_sems = refs[n:2 * n], refs[2 * n], refs[2 * n + 1]
        x, y, c = _place()
        chips = _other_chips(x, y)

        def copy(b, j, chip, hc, to):
            rh = bufs[b].shape[1] // 2
            part = outs[b].at[2 * chip[0] + chip[1], pl.ds(hc * rh, rh), :]
            return pltpu.make_async_remote_copy(src_ref=part, dst_ref=part, send_sem=send_sems.at[6 * b + j],
                                                recv_sem=recv_sems.at[6 * b + j], device_id=to, device_id_type=MESH)

        first = [copy(b, j, (x, y), c, (*chip, c)) for b in range(n) for j, chip in enumerate(chips)]
        for cp in first:
            cp.start()
        passed = []
        for b in range(n):
            for j, chip in enumerate(chips):
                copy(b, j, chip, c, (x, y, c)).wait_recv()
                passed.append(copy(b, 3 + j, chip, c, (x, y, 1 - c)))
                passed[-1].start()
        for b in range(n):
            for j, chip in enumerate(chips):
                copy(b, 3 + j, chip, 1 - c, (x, y, c)).wait_recv()
        for cp in first + passed:
            cp.wait_send()

    return _aliased_comm_call(body, bufs, 6 * n, name=name)


HBM = pl.BlockSpec(memory_space=pltpu.HBM)
SEM = pl.BlockSpec(memory_space=pltpu.SEMAPHORE)
_SPLIT = pltpu.CompilerParams(has_side_effects=pltpu.SideEffectType.DATAFLOW_SIDE_EFFECTING)


def _in_hbm(arrs):
    return [pltpu.with_memory_space_constraint(a, pltpu.HBM) for a in arrs]


def _gather_ici_start(bufs, after, *, name):
    n = len(bufs)

    def body(*refs):
        send_sems, recv_sems, outs, token = refs[n + 1], refs[n + 2], refs[n + 3:2 * n + 3], refs[2 * n + 3]
        x, y, c = _place()
        for b in range(n):
            rh = bufs[b].shape[1] // 2
            part = outs[b].at[2 * x + y, pl.ds(c * rh, rh), :]
            for j, chip in enumerate(_other_chips(x, y)):
                pltpu.make_async_remote_copy(src_ref=part, dst_ref=part, send_sem=send_sems.at[3 * b + j],
                                             recv_sem=recv_sems.at[3 * b + j], device_id=(*chip, c),
                                             device_id_type=MESH).start()
        token[...] = jnp.zeros_like(token)

    res = pl.pallas_call(
        body, name=name,
        out_shape=(pltpu.SemaphoreType.DMA((3 * n,)), pltpu.SemaphoreType.DMA((3 * n,)),
                   *[pltpu.HBM(b.shape, b.dtype) for b in bufs], jax.ShapeDtypeStruct((8, 128), F32)),
        in_specs=[HBM] * n + [ANY], out_specs=(SEM, SEM, *[HBM] * n, pl.BlockSpec(memory_space=pltpu.VMEM)),
        input_output_aliases={k: k + 2 for k in range(n)}, compiler_params=_SPLIT,
    )(*_in_hbm(bufs), after)
    return res[0], res[1], list(res[2:2 + n]), res[2 + n]


def _gather_ici_wait(send_sems, recv_sems, bufs, after, *, name):
    n = len(bufs)

    def body(*refs):
        ins, ss, rs = refs[:n], refs[n], refs[n + 1]
        x, y, c = _place()
        for b in range(n):
            rh = bufs[b].shape[1] // 2
            mine = ins[b].at[2 * x + y, pl.ds(c * rh, rh), :]
            for j, (cx, cy) in enumerate(_other_chips(x, y)):
                theirs = ins[b].at[2 * cx + cy, pl.ds(c * rh, rh), :]
                cp = pltpu.make_async_remote_copy(src_ref=mine, dst_ref=theirs, send_sem=ss.at[3 * b + j],
                                                  recv_sem=rs.at[3 * b + j], device_id=(cx, cy, c),
                                                  device_id_type=MESH)
                cp.wait_send()
                cp.wait_recv()

    return list(pl.pallas_call(
        body, name=name, out_shape=[pltpu.HBM(b.shape, b.dtype) for b in bufs],
        in_specs=[HBM] * n + [SEM, SEM, ANY], out_specs=[HBM] * n,
        input_output_aliases={k: k for k in range(n)}, compiler_params=_SPLIT,
    )(*bufs, send_sems, recv_sems, after))


def _gather_forward(bufs, *, name):
    n = len(bufs)

    def body(*refs):
        outs, send_sems, recv_sems = refs[n:2 * n], refs[2 * n], refs[2 * n + 1]
        x, y, c = _place()

        def copy(b, j, chip, hc):
            rh = bufs[b].shape[1] // 2
            part = outs[b].at[2 * chip[0] + chip[1], pl.ds(hc * rh, rh), :]
            return pltpu.make_async_remote_copy(src_ref=part, dst_ref=part, send_sem=send_sems.at[3 * b + j],
                                                recv_sem=recv_sems.at[3 * b + j], device_id=(x, y, 1 - c),
                                                device_id_type=MESH)

        sends = [copy(b, j, chip, c) for b in range(n) for j, chip in enumerate(_other_chips(x, y))]
        for cp in sends:
            cp.start()
        for b in range(n):
            for j, chip in enumerate(_other_chips(x, y)):
                copy(b, j, chip, 1 - c).wait_recv()
        for cp in sends:
            cp.wait_send()

    return _aliased_comm_call(body, bufs, 3 * n, name=name)


def _chip_exchange_start(hs, *, name):
    n = len(hs)
    lands = [lax.empty((3,) + h.shape[1:], h.dtype) for h in hs]

    def body(*refs):
        send_sems, recv_sems = refs[2 * n], refs[2 * n + 1]
        h_out, l_out, token = refs[2 * n + 2:3 * n + 2], refs[3 * n + 2:4 * n + 2], refs[4 * n + 2]
        x, y, c = _place()
        for b in range(n):
            for j, (cx, cy) in enumerate(_other_chips(x, y)):
                pltpu.make_async_remote_copy(src_ref=h_out[b].at[2 * cx + cy], dst_ref=l_out[b].at[j],
                                             send_sem=send_sems.at[3 * b + j], recv_sem=recv_sems.at[3 * b + j],
                                             device_id=(cx, cy, c), device_id_type=MESH).start()
        token[...] = jnp.zeros_like(token)

    res = pl.pallas_call(
        body, name=name,
        out_shape=(pltpu.SemaphoreType.DMA((3 * n,)), pltpu.SemaphoreType.DMA((3 * n,)),
                   *[pltpu.HBM(a.shape, a.dtype) for a in hs + lands], jax.ShapeDtypeStruct((8, 128), F32)),
        in_specs=[HBM] * (2 * n), out_specs=(SEM, SEM, *[HBM] * (2 * n), pl.BlockSpec(memory_space=pltpu.VMEM)),
        input_output_aliases={k: k + 2 for k in range(2 * n)}, compiler_params=_SPLIT,
    )(*_in_hbm(hs + lands))
    return res[0], res[1], list(res[2:2 + n]), list(res[2 + n:2 + 2 * n]), res[2 + 2 * n]


def _chip_exchange_wait(send_sems, recv_sems, hs, lands, after, *, name):
    n = len(hs)

    def body(*refs):
        h_in, l_in, ss, rs = refs[:n], refs[n:2 * n], refs[2 * n], refs[2 * n + 1]
        x, y, c = _place()
        for b in range(n):
            for j, (cx, cy) in enumerate(_other_chips(x, y)):
                cp = pltpu.make_async_remote_copy(src_ref=h_in[b].at[2 * cx + cy], dst_ref=l_in[b].at[j],
                                                  send_sem=ss.at[3 * b + j], recv_sem=rs.at[3 * b + j],
                                                  device_id=(cx, cy, c), device_id_type=MESH)
                cp.wait_send()
                cp.wait_recv()

    res = pl.pallas_call(
        body, name=name, out_shape=[pltpu.HBM(a.shape, a.dtype) for a in hs + lands],
        in_specs=[HBM] * (2 * n) + [SEM, SEM, ANY], out_specs=[HBM] * (2 * n),
        input_output_aliases={k: k for k in range(2 * n)}, compiler_params=_SPLIT,
    )(*hs, *lands, send_sems, recv_sems, after)
    return list(res[n:])


def _pair_exchange(gs, *, name):
    n = len(gs)

    def body(*refs):
        ins, outs, send_sems, recv_sems = refs[:n], refs[n:2 * n], refs[2 * n], refs[2 * n + 1]
        x, y, c = _place()
        cps = []
        for b in range(n):
            rh = gs[b].shape[1] // 2
            cps.append(pltpu.make_async_remote_copy(
                src_ref=ins[b].at[:, pl.ds((1 - c) * rh, rh), :], dst_ref=outs[b], send_sem=send_sems.at[b],
                recv_sem=recv_sems.at[b], device_id=(x, y, 1 - c), device_id_type=MESH))
        for cp in cps:
            cp.start()
        for cp in cps:
            cp.wait()

    return pl.pallas_call(
        body, name=name, out_shape=[jax.ShapeDtypeStruct((g.shape[0], g.shape[1] // 2, g.shape[2]), g.dtype) for g in gs],
        in_specs=[ANY] * n, out_specs=[ANY] * n,
        scratch_shapes=[pltpu.SemaphoreType.DMA((n,)), pltpu.SemaphoreType.DMA((n,))],
    )(*gs)


def _chip_exchange(hs, *, name):
    n = len(hs)

    def body(*refs):
        ins, outs, send_sems, recv_sems = refs[:n], refs[n:2 * n], refs[2 * n], refs[2 * n + 1]
        x, y, c = _place()
        cps = [pltpu.make_async_remote_copy(
            src_ref=ins[b].at[2 * cx + cy], dst_ref=outs[b].at[j], send_sem=send_sems.at[3 * b + j],
            recv_sem=recv_sems.at[3 * b + j], device_id=(cx, cy, c), device_id_type=MESH)
            for b in range(n) for j, (cx, cy) in enumerate(_other_chips(x, y))]
        for cp in cps:
            cp.start()
        for cp in cps:
            cp.wait()

    return pl.pallas_call(
        body, name=name, out_shape=[jax.ShapeDtypeStruct((3,) + h.shape[1:], h.dtype) for h in hs],
        in_specs=[ANY] * n, out_specs=[ANY] * n,
        scratch_shapes=[pltpu.SemaphoreType.DMA((3 * n,)), pltpu.SemaphoreType.DMA((3 * n,))],
    )(*hs)


def _pair_share(ss, *, name):
    n = len(ss)

    def body(*refs):
        outs, send_sems, recv_sems = refs[n:2 * n], refs[2 * n], refs[2 * n + 1]
        x, y, c = _place()
        cps = []
        for b in range(n):
            rh = ss[b].shape[0] // 2
            mine = outs[b].at[pl.ds(c * rh, rh), :]
            cps.append(pltpu.make_async_remote_copy(src_ref=mine, dst_ref=mine, send_sem=send_sems.at[b],
                                                    recv_sem=recv_sems.at[b], device_id=(x, y, 1 - c),
                                                    device_id_type=MESH))
        for cp in cps:
            cp.start()
        for b, cp in enumerate(cps):
            rh = ss[b].shape[0] // 2
            theirs = outs[b].at[pl.ds((1 - c) * rh, rh), :]
            pltpu.make_async_remote_copy(src_ref=theirs, dst_ref=theirs, send_sem=send_sems.at[b],
                                         recv_sem=recv_sems.at[b], device_id=(x, y, 1 - c),
                                         device_id_type=MESH).wait_recv()
            cp.wait_send()

    return _aliased_comm_call(body, ss, n, name=name)


_SMALL_SHARDED = (("e_conv_w", 2), ("o_norm", 1), ("o_d", 1))
_REPLICATED = ("e_norm", "e_gmlp_w", "e_gmlp_b", "e_conv_b", "e_conv_ln_g", "e_conv_ln_b", "o_lam_re", "o_lam_im",
               "o_log_dt", "o_b_re", "o_b_im", "o_c_re", "o_c_im", "ca_norm", "ca_mem_norm", "ffn_norm", "final_norm")
_SMALL = tuple(n for n, _ in _SMALL_SHARDED) + _REPLICATED
_WEIGHTS = ("e_norm", "e_w_in", "e_gmlp_w", "e_gmlp_b", "e_conv_w", "e_conv_b", "e_conv_ln_g", "e_conv_ln_b",
            "e_w_out", "o_norm", "o_w_in", "o_lam_re", "o_lam_im", "o_log_dt", "o_b_re", "o_b_im", "o_c_re", "o_c_im",
            "o_d", "o_w_out", "ca_norm", "ca_mem_norm", "ca_wq", "ca_wk", "ca_wv", "ca_wo", "ffn_norm", "ffn_w_gate",
            "ffn_w_up", "ffn_w_down", "final_norm")


def _pack_rows(arrs, width, dtype, row_mult=8):
    parts, spans, r0 = [], [], 0
    for a in arrs:
        flat = a.reshape(-1).astype(dtype)
        rows = -(-flat.shape[0] // (width * row_mult)) * row_mult
        if rows * width != flat.shape[0]:
            flat = jnp.pad(flat, (0, rows * width - flat.shape[0]))
        parts.append(flat.reshape(rows, width))
        spans.append((r0, rows))
        r0 += rows
    return jnp.concatenate(parts, axis=0), spans


def _unpack_rows(slab, spans, shapes):
    out = []
    for (r0, rows), shp in zip(spans, shapes):
        n = math.prod(shp)
        out.append(slab[r0:r0 + rows].reshape(-1)[:n].reshape(shp))
    return out


def _two_d(a):
    return a.reshape(-1, a.shape[-1])


def _local_slab(local, slab, dtype):
    parts = sorted((r0, n, l) for n, (_, where) in _PLACE.items() for l, (s, r0) in enumerate(where) if s == slab)
    shards = [local[n] if len(_PLACE[n][1]) == 1 else local[n][l] for _, n, l in parts]
    return jnp.concatenate([_two_d(a).astype(dtype) for a in shards], axis=0)


def _block_diag(b, pattern):
    return jnp.einsum(pattern, b, jnp.eye(C_GROUPS, dtype=b.dtype))


def _s5_discretize(lam_re, lam_im, log_dt, b_re, b_im):
    dt = jnp.exp(log_dt)[:, None]
    mag = jnp.exp(lam_re * dt)
    ar = mag * jnp.cos(lam_im * dt)
    ai = mag * jnp.sin(lam_im * dt)
    den = lam_re * lam_re + lam_im * lam_im
    qr = ((ar - 1.0) * lam_re + ai * lam_im) / den
    qi = (ai * lam_re - (ar - 1.0) * lam_im) / den
    bbr = qr[..., None] * b_re - qi[..., None] * b_im
    bbi = qr[..., None] * b_im + qi[..., None] * b_re
    return ar, ai, bbr, bbi


def _attention_block(x, mem, W, w, i, tag):
    xn, q = _norm_mm(x, w["ca_norm"][i], _shards(W, "ca_wq", i), split="k", out_dtype=BF16, name=f"{tag}_q")
    memn = _rms_fwd(mem, w["ca_mem_norm"][i], name=f"{tag}_ca_memnorm")
    k = _mm_k(memn, _shards(W, "ca_wk", i), out_dtype=BF16, name=f"{tag}_k")
    v = _mm_k(memn, _shards(W, "ca_wv", i), out_dtype=BF16, name=f"{tag}_v")
    o = _attn_fwd(q, k, v, name=f"{tag}_attn")
    y = _mm_k(o, _shards(W, "ca_wo", i), add=x, name=f"{tag}_wo")
    return y, (x, xn, memn, q, k, v, o)


def _attention_block_bwd(dy, saved, mem, W, w, i, tag, G, grads, token=None):
    x, xn, memn, q, k, v, o = saved
    if token is not None:
        k = _behind(k, token)
    G = _grad_to_slab(G, "ca_wo", i, o, dy, a_cols=256, name=f"{tag}_dwo")
    dq, dk, dv = _attn_bwd(dy, _shards(W, "ca_wo", i), q, k, v, name=f"{tag}_attn_bwd")
    G = _grad_to_slab(G, "ca_wq", i, xn, dq, a_cols=256, name=f"{tag}_dwq")
    G = _grad_to_slab(G, "ca_wk", i, memn, dk, a_cols=256, name=f"{tag}_dwk")
    G = _grad_to_slab(G, "ca_wv", i, memn, dv, a_cols=256, name=f"{tag}_dwv")
    dmemn = _mm_k_t([(dk, _shards(W, "ca_wk", i)), (dv, _shards(W, "ca_wv", i))], name=f"{tag}_dmemn")
    dx, dg = _norm_bwd_k(dq, _shards(W, "ca_wq", i), x, w["ca_norm"][i], dy, name=f"{tag}_dq_norm_bwd")
    grads["ca_norm"][i] = dg[0]
    grads["ca_mem_norm"][i] = _rms_dg(mem, w["ca_mem_norm"][i], dmemn, name=f"{tag}_ca_memnorm_bwd")[0]
    return dx, G


def _ffn_block(x, W, w, i, tag):
    fn, gate, up, h = _ffn_up(x, w["ffn_norm"][i], _shards(W, "ffn_w_gate", i), _shards(W, "ffn_w_up", i),
                              name=f"{tag}_ffn_up")
    y = _mm_k(h, _shards(W, "ffn_w_down", i), add=x, name=f"{tag}_down")
    return y, (x, fn, gate, up, h)


def _ffn_block_bwd(dy, saved, W, w, i, tag, G, grads, token=None):
    x, fn, gate, up, h = saved
    G = _grad_to_slab(G, "ffn_w_down", i, h, dy, name=f"{tag}_dwd")
    dg, du = _ffn_bwd_hidden(dy, _shards(W, "ffn_w_down", i), gate, up, token, name=f"{tag}_ffn_bwd_hidden")
    G = _grad_to_slab(G, "ffn_w_gate", i, fn, dg, name=f"{tag}_dwg")
    G = _grad_to_slab(G, "ffn_w_up", i, fn, du, name=f"{tag}_dwu")
    dx, dgn = _ffn_in_bwd(dg, du, _shards(W, "ffn_w_gate", i), _shards(W, "ffn_w_up", i), x, w["ffn_norm"][i], dy,
                          name=f"{tag}_ffn_in_bwd")
    grads["ffn_norm"][i] = dgn[0]
    return dx, G


def _gmlp_mask():
    chunk = jnp.arange(GMLP_BLOCK) // CHUNK
    return chunk[None, :] <= chunk[:, None]


def _even_block(x, W, w, tag):
    hn, proj = _norm_mm(x, w["e_norm"][0], _shards(W, "e_w_in"), split="n", out_dtype=F32, name=f"{tag}_w_in")
    wm = jnp.where(_gmlp_mask()[None], w["e_gmlp_w"][0], 0.0).astype(BF16)
    bcol = w["e_gmlp_b"][0][:, :, None]
    cw = jnp.pad(w["e_conv_w"][0], ((0, CONV_HALO - CONV_WIDTH), (0, 0)))
    cb, lg, lb = w["e_conv_b"], w["e_conv_ln_g"], w["e_conv_ln_b"]
    mix, hc = _even_fwd(proj, wm, bcol, cw, cb, lg, lb, name=f"{tag}_mixers")
    y = _mm_k(mix, _shards(W, "e_w_out"), add=x, name=f"{tag}_w_out")
    return y, (x, hn, proj, mix, hc, wm, bcol, cw)


def _even_block_bwd(dy, saved, W, w, tag, G, grads):
    x, hn, proj, mix, hc, wm, bcol, cw = saved
    dmix = _mm_k_t([(dy, _shards(W, "e_w_out"))], name=f"{tag}_dmix")
    G = _grad_to_slab(G, "e_w_out", 0, mix, dy, a_cols=256, name=f"{tag}_dw_out")
    wmt = jnp.swapaxes(wm, 1, 2)
    dpa, dhc, dwm, db, dlg, dlb, dcb = _even_bwd1(proj, dmix, hc, wm, wmt, bcol, w["e_conv_ln_g"], w["e_conv_ln_b"],
                                                  name=f"{tag}_mixers_bwd1")
    dpb, dcw = _even_bwd2(proj, dhc, cw, name=f"{tag}_mixers_bwd2")
    grads["e_gmlp_w"] = jnp.where(_gmlp_mask()[None], dwm, 0.0)[None]
    grads["e_gmlp_b"] = db[:, :, 0][None]
    grads["e_conv_ln_g"], grads["e_conv_ln_b"], grads["e_conv_b"] = dlg, dlb, dcb
    grads["e_conv_w"] = dcw[:CONV_WIDTH][None]
    G = _grad_to_slab(G, "e_w_in", 0, hn, dpa, b_cols=512, chips=(0, 2), name=f"{tag}_dw_in_a")
    G = _grad_to_slab(G, "e_w_in", 0, hn, dpb, b_cols=512, chips=(2, 2), name=f"{tag}_dw_in_b")
    dx, dg = _norm_bwd_n((dpa, dpb), _shards(W, "e_w_in"), x, w["e_norm"][0], dy, name=f"{tag}_in_bwd")
    grads["e_norm"] = dg
    return dx, G


def _odd_block(x, W, w, tag):
    S = x.shape[0]
    hn, u = _norm_mm(x, w["o_norm"][0], _shards(W, "o_w_in"), split="k", out_dtype=F32, name=f"{tag}_w_in")
    disc_in = (w["o_lam_re"][0], w["o_lam_im"][0], w["o_log_dt"][0], w["o_b_re"][0], w["o_b_im"][0])
    (ar, ai, bbr, bbi), disc_vjp = jax.vjp(_s5_discretize, *disc_in)
    bd = jnp.concatenate([_block_diag(bbr, "gpc,gh->gchp").reshape(C_WIDTH, N_STATE),
                          _block_diag(bbi, "gpc,gh->gchp").reshape(C_WIDTH, N_STATE)], axis=1).astype(BF16)
    cd = jnp.concatenate([_block_diag(w["o_c_re"][0], "gcp,gh->gphc").reshape(N_STATE, C_WIDTH),
                          -_block_diag(w["o_c_im"][0], "gcp,gh->gphc").reshape(N_STATE, C_WIDTH)], axis=0).astype(BF16)
    a = jnp.concatenate([ar.reshape(STATE_ROWS, STATE_LANES), ai.reshape(STATE_ROWS, STATE_LANES)], axis=0)
    bu = _mm_rows(u, bd, name=f"{tag}_bu")
    xs = _scan_fwd(bu.reshape(S, 2 * STATE_ROWS, STATE_LANES), a, name=f"{tag}_scan")
    yv, yg = _s5_readout(xs.reshape(S, 2 * N_STATE), cd, u, w["o_d"], name=f"{tag}_readout")
    o, y = _glu_out(yg, _shards(W, "o_w_out"), x, name=f"{tag}_glu_out")
    return y, (x, hn, u, bd, cd, a, xs, yv, yg, o, disc_vjp)


def _odd_block_bwd(dy, saved, W, w, tag, G, grads):
    x, hn, u, bd, cd, a, xs, yv, yg, o, disc_vjp = saved
    S = x.shape[0]
    do, dys, dus, dd = _glu_out_bwd(o, dy, _shards(W, "o_w_out"), yv, u, w["o_d"], name=f"{tag}_glu_out_bwd")
    G = _grad_to_slab(G, "o_w_out", 0, yg, do, b_cols=512, name=f"{tag}_dw_out")
    grads["o_d"] = dd
    xs2 = xs.reshape(S, 2 * N_STATE)
    dxs = _mm_rows(dys, cd, nt=True, name=f"{tag}_dxs")
    dcd = _mm_tn(xs2, dys, name=f"{tag}_dcd")
    gs, da = _scan_bwd(dxs.reshape(S, 2 * STATE_ROWS, STATE_LANES), xs, a, name=f"{tag}_scan_bwd")
    gs2 = gs.reshape(S, 2 * N_STATE)
    dbd = _mm_tn(u, gs2, name=f"{tag}_dbd")
    du, dx, dg = _s5_in_bwd(gs2, bd, dus, _shards(W, "o_w_in"), x, w["o_norm"][0], dy, name=f"{tag}_in_bwd")
    G = _grad_to_slab(G, "o_w_in", 0, hn, du, a_cols=256, name=f"{tag}_dw_in")
    grads["o_norm"] = dg
    eye = jnp.eye(C_GROUPS, dtype=F32)
    dcr = jnp.einsum("gphc,gh->gcp", dcd[:N_STATE].reshape(C_GROUPS, C_STATE, C_GROUPS, C_GROUP_CH), eye)
    dci = -jnp.einsum("gphc,gh->gcp", dcd[N_STATE:].reshape(C_GROUPS, C_STATE, C_GROUPS, C_GROUP_CH), eye)
    dbbr = jnp.einsum("gchp,gh->gpc", dbd[:, :N_STATE].reshape(C_GROUPS, C_GROUP_CH, C_GROUPS, C_STATE), eye)
    dbbi = jnp.einsum("gchp,gh->gpc", dbd[:, N_STATE:].reshape(C_GROUPS, C_GROUP_CH, C_GROUPS, C_STATE), eye)
    dar = da[:STATE_ROWS].reshape(C_GROUPS, C_STATE)
    dai = da[STATE_ROWS:].reshape(C_GROUPS, C_STATE)
    dlr, dli, dldt, dbr, dbi = disc_vjp((dar, dai, dbbr, dbbi))
    grads["o_lam_re"], grads["o_lam_im"], grads["o_log_dt"] = dlr[None], dli[None], dldt[None]
    grads["o_b_re"], grads["o_b_im"], grads["o_c_re"], grads["o_c_im"] = dbr[None], dbi[None], dcr[None], dci[None]
    return dx, G


def _behind(value, token):
    return value + token[0, 0].astype(value.dtype)


class _NoExchange:
    def __init__(self, W):
        self.W = W

    def first_weights(self, w):
        return self.W, w

    def after_even_mixer(self, x, W, w):
        return W, w

    def after_layer0(self, x, W):
        return W

    def after_layer1_backward(self, G):
        return None

    def after_ffn0_backward(self, G):
        return None


def _forward_backward(xs_, mems_, tgt, w, G, exchange):
    W, w = exchange.first_weights(w)
    x1, s_mix0 = _even_block(xs_, W, w, "l0")
    W, w = exchange.after_even_mixer(x1, W, w)
    x2, s_att0 = _attention_block(x1, mems_, W, w, 0, "l0")
    x3, s_ffn0 = _ffn_block(x2, W, w, 0, "l0")
    W = exchange.after_layer0(x3, W)
    x4, s_mix1 = _odd_block(x3, W, w, "l1")
    x5, s_att1 = _attention_block(x4, mems_, W, w, 1, "l1")
    x6, s_ffn1 = _ffn_block(x5, W, w, 1, "l1")
    dx, dfinal, loss_lanes = _loss_head(x6, w["final_norm"], tgt, name="loss_head")

    grads = {n: [None, None] for n in ("ca_norm", "ca_mem_norm", "ffn_norm")}
    grads["final_norm"] = dfinal[0]
    dx, G = _ffn_block_bwd(dx, s_ffn1, W, w, 1, "l1", G, grads)
    dx, G = _attention_block_bwd(dx, s_att1, mems_, W, w, 1, "l1", G, grads)
    dx, G = _odd_block_bwd(dx, s_mix1, W, w, "l1", G, grads)
    token = exchange.after_layer1_backward(G)
    dx, G = _ffn_block_bwd(dx, s_ffn0, W, w, 0, "l0", G, grads, token)
    token = exchange.after_ffn0_backward(G)
    dx, G = _attention_block_bwd(dx, s_att0, mems_, W, w, 0, "l0", G, grads, token)
    dx, G = _even_block_bwd(dx, s_mix0, W, w, "l0", G, grads)
    for n in list(grads):
        if isinstance(grads[n], list):
            grads[n] = jnp.stack(grads[n], axis=0)
        grads[n] = grads[n].reshape(w[n].shape)
    return loss_lanes, dx, G, grads


class _Exchange:
    def __init__(self, local, chip, core):
        self.bufs = {s: lax.dynamic_update_slice(lax.empty((N_CHIPS, rows, width), BF16),
                                                 _local_slab(local, s, BF16)[None], (chip, 0, 0))
                     for s, (width, rows) in _SLABS.items()}
        self.half = core.reshape(1).astype(jnp.int32)
        self.where = jnp.stack([chip, core]).astype(jnp.int32)
        self.flight = None
        self.layer1_reduce = None

    def _start(self, stage, after, tag):
        self.flight = _gather_ici_start([self.bufs[s] for s in stage], after, name=f"gather_{tag}_start")
        return self.flight[3]

    def _land(self, stage, after, tag):
        send_sems, recv_sems, bufs, _ = self.flight
        bufs = _gather_ici_wait(send_sems, recv_sems, bufs, after, name=f"gather_{tag}_wait")
        return dict(zip(stage, _gather_forward(bufs, name=f"gather_{tag}_forward")))

    def first_weights(self, w):
        W = dict(zip(_STAGES[0], _allgather_chips([self.bufs[s] for s in _STAGES[0]], name="gather_stage0")))
        token = self._start(_STAGES[1], W[_STAGES[0][0]], "stage1")
        return W, {**w, "e_norm": _behind(w["e_norm"], token)}

    def after_even_mixer(self, x, W, w):
        W = {**W, **self._land(_STAGES[1], x, "stage1")}
        token = self._start(_STAGES[2], W[_STAGES[1][0]], "stage2")
        return W, {**w, "ca_norm": _behind(w["ca_norm"], token)}

    def after_layer0(self, x, W):
        return {**W, **self._land(_STAGES[2], x, "stage2")}

    def reduce_start(self, G, slabs, tag):
        gl = [G[s] for s in slabs]
        other = _pair_exchange(gl, name=f"grad_{tag}_pair_exchange")
        pairs = [_pair_sum(g, r, self.half, name=f"grad_pair_sum_{s}") for s, g, r in zip(slabs, gl, other)]
        send_sems, recv_sems, pairs, lands, token = _chip_exchange_start(pairs, name=f"grad_{tag}_chip_start")
        return (slabs, gl, other, send_sems, recv_sems, pairs, lands), token

    def reduce_finish(self, state, after, tag):
        slabs, gl, other, send_sems, recv_sems, pairs, lands = state
        slots = _chip_exchange_wait(send_sems, recv_sems, pairs, lands, after, name=f"grad_{tag}_chip_wait")
        halves = [_chip_sum(g, r, sl, self.where, name=f"grad_chip_sum_{s}")
                  for s, g, r, sl in zip(slabs, gl, other, slots)]
        return dict(zip(slabs, _pair_share(halves, name=f"grad_{tag}_pair_share")))

    def after_layer1_backward(self, G):
        self.layer1_reduce, token = self.reduce_start(G, _STAGES[2], "l1")
        return token

    def after_ffn0_backward(self, G):
        self.ffn0_reduce, token = self.reduce_start(G, _FFN0_SLABS, "ffn0")
        return token


def kernel(x, mem, e_norm, e_w_in, e_gmlp_w, e_gmlp_b, e_conv_w, e_conv_b, e_conv_ln_g, e_conv_ln_b, e_w_out, o_norm, o_w_in, o_lam_re, o_lam_im, o_log_dt, o_b_re, o_b_im, o_c_re, o_c_im, o_d, o_w_out, ca_norm, ca_mem_norm, ca_wq, ca_wk, ca_wv, ca_wo, ffn_norm, ffn_w_gate, ffn_w_up, ffn_w_down, final_norm, loss_target, m_e_norm, m_e_w_in, m_e_gmlp_w, m_e_gmlp_b, m_e_conv_w, m_e_conv_b, m_e_conv_ln_g, m_e_conv_ln_b, m_e_w_out, m_o_norm, m_o_w_in, m_o_lam_re, m_o_lam_im, m_o_log_dt, m_o_b_re, m_o_b_im, m_o_c_re, m_o_c_im, m_o_d, m_o_w_out, m_ca_norm, m_ca_mem_norm, m_ca_wq, m_ca_wk, m_ca_wv, m_ca_wo, m_ffn_norm, m_ffn_w_gate, m_ffn_w_up, m_ffn_w_down, m_final_norm, v_e_norm, v_e_w_in, v_e_gmlp_w, v_e_gmlp_b, v_e_conv_w, v_e_conv_b, v_e_conv_ln_g, v_e_conv_ln_b, v_e_w_out, v_o_norm, v_o_w_in, v_o_lam_re, v_o_lam_im, v_o_log_dt, v_o_b_re, v_o_b_im, v_o_c_re, v_o_c_im, v_o_d, v_o_w_out, v_ca_norm, v_ca_mem_norm, v_ca_wq, v_ca_wk, v_ca_wv, v_ca_wo, v_ffn_norm, v_ffn_w_gate, v_ffn_w_up, v_ffn_w_down, v_final_norm):
    args = dict(locals())
    local = {n: args[n] for n in _WEIGHTS}
    mom = {n: args["m_" + n] for n in _WEIGHTS}
    vel = {n: args["v_" + n] for n in _WEIGHTS}
    chip = 2 * lax.axis_index("x") + lax.axis_index("y")
    core = lax.axis_index("c")
    xs_, mems_, tgt = x[0], mem[0], loss_target[0]

    w = {n: local[n] for n in _REPLICATED}
    sm_slab, sm_spans = _pack_rows([local[n] for n, _ in _SMALL_SHARDED], SMALL_W, F32)
    sm_all = _allgather_small(sm_slab, name="gather_small_weights").reshape(N_DEV, -1, SMALL_W)
    for (n, ax), span in zip(_SMALL_SHARDED, sm_spans):
        shp = local[n].shape
        w[n] = jnp.concatenate([_unpack_rows(sm_all[2 * p], [span], [shp])[0] for p in range(N_CHIPS)], axis=ax)

    exchange = _Exchange(local, chip, core)
    G = {s: lax.empty((N_CHIPS, rows, width), F32) for s, (width, rows) in _SLABS.items()}
    loss_lanes, dx, G, grads = _forward_backward(xs_, mems_, tgt, w, G, exchange)

    gsum = exchange.reduce_finish(exchange.layer1_reduce, dx, "l1")
    gsum = {**gsum, **exchange.reduce_finish(exchange.ffn0_reduce, dx, "ffn0")}
    rest0_reduce, token = exchange.reduce_start(G, _REST0_SLABS, "rest0")

    grads = {**grads, _SMALL[0]: _behind(grads[_SMALL[0]], token)}
    gs_slab, gs_spans = _pack_rows([grads[n] for n in _SMALL], SMALL_W, F32)
    gs_all = _allgather_small(gs_slab, name="gather_small_grads").reshape(N_DEV, -1, SMALL_W)
    gs_sum = _sum_slots(gs_all, name="small_grad_sum")
    out_grads = dict(zip(_SMALL, _unpack_rows(gs_sum, gs_spans, [grads[n].shape for n in _SMALL])))
    for n, ax in _SMALL_SHARDED:
        width = local[n].shape[ax]
        out_grads[n] = lax.dynamic_slice_in_dim(out_grads[n], chip * width, width, axis=ax)

    delta, new_m, new_v = {}, {}, {}
    d_, m_, v_ = _adamw_small([_two_d(local[n]) for n in _SMALL], [_two_d(out_grads[n]) for n in _SMALL],
                              [_two_d(mom[n]) for n in _SMALL], [_two_d(vel[n]) for n in _SMALL], name="adamw_small")
    for n, dd, mm_, vv in zip(_SMALL, d_, m_, v_):
        shp = local[n].shape
        delta[n], new_m[n], new_v[n] = dd.reshape(shp), mm_.reshape(shp), vv.reshape(shp)
    def adamw_large(names):
        for n in names:
            shp = local[n].shape
            g_, d_, m_, v_ = _adamw_shard(_two_d(local[n]), [(gsum[s], r0) for s, r0 in _PLACE[n][1]],
                                          _two_d(mom[n]), _two_d(vel[n]), name=f"adamw_{n}")
            out_grads[n], delta[n], new_m[n], new_v[n] = (g_.reshape(shp), d_.reshape(shp), m_.reshape(shp),
                                                          v_.reshape(shp))

    ready = [n for n, (_, where) in _PLACE.items() if all(s in gsum for s, _ in where)]
    adamw_large(ready)
    done = jnp.concatenate([delta[n].reshape(-1)[:1] for n in ready + list(_SMALL[:1])])
    gsum = {**gsum, **exchange.reduce_finish(rest0_reduce, done, "rest0")}
    adamw_large([n for n in _PLACE if n not in ready])

    loss = lax.psum(loss_lanes[0, 0], ("x", "y", "c"))
    return (loss, dx[None], *[out_grads[n] for n in _WEIGHTS], *[delta[n] for n in _WEIGHTS],
            *[new_m[n] for n in _WEIGHTS], *[new_v[n] for n in _WEIGHTS])
```

```python
import functools
import math

import jax
import jax.numpy as jnp
from jax import lax
from jax.experimental import pallas as pl
from jax.experimental.pallas import tpu as pltpu

F32 = jnp.float32
BF16 = jnp.bfloat16
MESH = pl.DeviceIdType.MESH

EPS = 1e-6
D_MODEL = 1024
A_WIDTH = 512
A_GROUPS = 4
GMLP_BLOCK = 128
CHUNK = 64
B_WIDTH = 512
CONV_WIDTH = 31
CONV_HALO = 32
C_WIDTH = 512
C_GROUP_CH = 16
C_GROUPS = 32
C_STATE = 64
N_STATE = C_GROUPS * C_STATE
STATE_LANES = 128
STATE_ROWS = N_STATE // STATE_LANES
SCAN_BLOCK = 8
CA_HEADS = 4
CA_HEAD_DIM = 256
FFN_HIDDEN = 2816

ADAM_LR = 0.001
ADAM_B1 = 0.9
ADAM_B2 = 0.999
ADAM_EPS = 1e-08
ADAM_WD = 0.01
ADAM_STEP = 10

VMEM_LIMIT = 56 * 1024 * 1024
ACC_BYTES = 6 * 1024 * 1024
TN_VMEM_BYTES = 44 * 1024 * 1024
SMALL_W = 128
N_CHIPS = 4
N_DEV = 8

_SLABS = {"D0": (512, 1024), "E0": (1024, 256), "A0": (1024, 1024), "B0": (1024, 704), "C0": (704, 2048),
          "D1": (512, 768), "A1": (1024, 1024), "B1": (1024, 704), "C1": (704, 2048)}
_STAGES = (("D0", "E0"), ("A0", "B0", "C0"), ("D1", "A1", "B1", "C1"))
_FFN0_SLABS = ("B0", "C0")
_REST0_SLABS = ("D0", "E0", "A0")
_PLACE = {
    "e_w_in": (1024, (("D0", 0),)), "e_w_out": (256, (("E0", 0),)),
    "o_w_out": (512, (("D1", 0),)), "o_w_in": (256, (("D1", 512),)),
    "ca_wq": (256, (("A0", 0), ("A1", 0))), "ca_wk": (256, (("A0", 256), ("A1", 256))),
    "ca_wv": (256, (("A0", 512), ("A1", 512))), "ca_wo": (256, (("A0", 768), ("A1", 768))),
    "ffn_w_down": (704, (("B0", 0), ("B1", 0))),
    "ffn_w_gate": (1024, (("C0", 0), ("C1", 0))), "ffn_w_up": (1024, (("C0", 1024), ("C1", 1024))),
}


def _params(sem=None):
    return pltpu.CompilerParams(dimension_semantics=sem, vmem_limit_bytes=VMEM_LIMIT)


def _tile(n, pref, mult=128):
    if n <= pref:
        return n
    t = (pref // mult) * mult
    while t >= mult:
        if n % t == 0:
            return t
        t -= mult
    return n


def _blk(name, layer=0):
    rows, where = _PLACE[name]
    slab, r0 = where[layer]
    assert r0 % rows == 0
    return slab, rows, r0 // rows


def _shards(slabs, name, layer=0):
    slab, rows, b = _blk(name, layer)
    return [(slabs[slab], (None, rows, _SLABS[slab][0]), (p, b, 0)) for p in range(N_CHIPS)]


_GELU_C = 0.7978845608028654
_GELU_A = 0.044715


def _gelu(x):
    t = jnp.tanh(_GELU_C * (x + _GELU_A * (x * x * x)))
    return 0.5 * x * (1.0 + t), t


def _gelu_grad(x, t):
    return 0.5 * (1.0 + t) + 0.5 * x * (1.0 - t * t) * (_GELU_C * (1.0 + 3.0 * _GELU_A * x * x))


def _sigmoid(x):
    return 1.0 / (1.0 + jnp.exp(-x))


def _mean(x):
    return jnp.mean(x, axis=-1, keepdims=True)


def _dot(a, b):
    return jnp.dot(a, b, preferred_element_type=F32)


def _dot_nt(a, b):
    return lax.dot_general(a, b, (((1,), (1,)), ((), ())), preferred_element_type=F32)


def _dot_tn(a, b):
    return lax.dot_general(a, b, (((0,), (0,)), ((), ())), preferred_element_type=F32)


def _rms_tile(xv, gv):
    return (xv * lax.rsqrt(_mean(xv * xv) + EPS)) * gv


def _rms_bwd_tile(xv, gv, dyv):
    r = lax.rsqrt(_mean(xv * xv) + EPS)
    xh = xv * r
    dyg = dyv * gv
    return r * (dyg - xh * _mean(dyg * xh)), jnp.sum(dyv * xh, axis=0, keepdims=True)


def _cols(p, width):
    return slice(p * width, (p + 1) * width)


def _sum_k(a, ws, k):
    tot = None
    for p in range(N_CHIPS):
        y = _dot(a[:, _cols(p, k)], ws[p][...])
        tot = y if tot is None else tot + y
    return tot


def _cat_nt(a, ws):
    return jnp.concatenate([_dot_nt(a, ws[p][...]) for p in range(N_CHIPS)], axis=1)


def _rows_call(name, tm, rows, fulls, outs, accs, body, scratch=()):
    S = min(x.shape[-2] for x in rows)
    nr, nf, no, na = len(rows), len(fulls), len(outs), len(accs)

    def kern(*refs):
        r, f = refs[:nr], refs[nr:nr + nf]
        o, a = refs[nr + nf:nr + nf + no], refs[nr + nf + no:nr + nf + no + na]
        if na:
            @pl.when(pl.program_id(0) == 0)
            def _():
                for ref in a:
                    ref[...] = jnp.zeros_like(ref)
        body(r, f, o, a, refs[nr + nf + no + na:])

    def whole(shape):
        nd = len(shape)
        return pl.BlockSpec(tuple(shape), lambda i: (0,) * nd)

    def row_spec(shape):
        if len(shape) == 3:
            return pl.BlockSpec((shape[0], tm, shape[2]), lambda i: (0, i, 0))
        return pl.BlockSpec((tm * (shape[0] // S), shape[1]), lambda i: (i, 0))

    def full_spec(x):
        if isinstance(x, tuple):
            _, bshape, bidx = x
            return pl.BlockSpec(bshape, lambda i: bidx, pipeline_mode=pl.Buffered(1))
        return whole(x.shape)

    def out_shape_of(o):
        if o[0] == "state":
            return (S * 2 * STATE_ROWS, STATE_LANES)
        return (S, o[0]) if len(o) == 2 else (o[0], S, o[1])

    out_shapes = [out_shape_of(o) for o in outs]
    res = pl.pallas_call(
        kern, name=name, grid=(S // tm,),
        in_specs=[row_spec(x.shape) for x in rows] + [full_spec(x) for x in fulls],
        out_specs=[row_spec(s) for s in out_shapes] + [whole(shp) for shp, _ in accs],
        out_shape=[jax.ShapeDtypeStruct(s, o[-1]) for s, o in zip(out_shapes, outs)]
        + [jax.ShapeDtypeStruct(tuple(shp), dt) for shp, dt in accs],
        scratch_shapes=list(scratch),
        compiler_params=_params(("arbitrary",) if na else ("parallel",)),
    )(*rows, *[x[0] if isinstance(x, tuple) else x for x in fulls])
    return res[:no], res[no:]


def _grad_to_slab(gslabs, wname, layer, a, b, *, a_cols=None, b_cols=None, chips=(0, N_CHIPS), name):
    slab, rows, bidx = _blk(wname, layer)
    width = _SLABS[slab][0]
    p0, n_p = chips
    assert p0 % n_p == 0
    S = a.shape[-2]

    def tile_bytes(x, ts):
        return ts * x.dtype.itemsize * (x.shape[2] * n_p if x.ndim == 3 else x.shape[1])

    acc_bytes = n_p * rows * (-(-width // 128) * 128) * 4
    ts = next(t for t in (2048, 1024, 512, 256, S) if S % t == 0
              and 2 * (tile_bytes(a, t) + tile_bytes(b, t) + acc_bytes) <= TN_VMEM_BYTES or t == S)

    def operand(x):
        if x.ndim == 3:
            return pl.BlockSpec((n_p, ts, x.shape[2]), lambda s: (p0 // n_p, s, 0))
        return pl.BlockSpec((ts, x.shape[1]), lambda s: (s, 0))

    def part(ref, cols, p):
        if len(ref.shape) == 3:
            return ref[p]
        return ref[...] if cols is None else ref[:, _cols(p, cols)]

    def body(a_ref, b_ref, slab_ref, o_ref):
        @pl.when(pl.program_id(0) == 0)
        def _():
            o_ref[...] = jnp.zeros_like(o_ref)

        for p in range(n_p):
            o_ref[p] += _dot_tn(part(a_ref, a_cols, p).astype(BF16), part(b_ref, b_cols, p).astype(BF16))

    g = gslabs[slab]
    out = pl.pallas_call(
        body, name=name, grid=(S // ts,),
        in_specs=[operand(a), operand(b), pl.BlockSpec(memory_space=pl.ANY)],
        out_specs=pl.BlockSpec((n_p, rows, width), lambda s: (p0 // n_p, bidx, 0)),
        out_shape=jax.ShapeDtypeStruct(g.shape, F32), input_output_aliases={2: 0},
        compiler_params=_params(("arbitrary",)),
    )(a, b, g)
    return {**gslabs, slab: out}


def _vec(g):
    return g.reshape(1, -1)


def _norm_mm(x, g, ws, *, split, out_dtype, name, tm=512):
    S, D = x.shape
    k, n = ws[0][1][1], ws[0][1][2]
    N = n if split == "k" else N_CHIPS * n

    def body(r, f, o, acc, s):
        xn = _rms_tile(r[0][...], f[0][...]).astype(BF16)
        o[0][...] = xn
        if split == "k":
            o[1][...] = _sum_k(xn, f[1:], k).astype(out_dtype)
        else:
            for p in range(N_CHIPS):
                o[1][:, _cols(p, n)] = _dot(xn, f[1 + p][...]).astype(out_dtype)

    (xn, y), _ = _rows_call(name, _tile(S, tm), [x], [_vec(g)] + ws, [(D, BF16), (N, out_dtype)], [], body)
    return xn, y


def _mm_k(a, ws, *, add=None, out_dtype=F32, name, tm=512):
    S = a.shape[-2]
    k, n = ws[0][1][1], ws[0][1][2]
    has_add = add is not None

    def body(r, f, o, acc, s):
        if a.ndim == 3:
            y = None
            for p in range(N_CHIPS):
                t = _dot(r[0][p].astype(BF16), f[p][...])
                y = t if y is None else y + t
        else:
            y = _sum_k(r[0][...].astype(BF16), f, k)
        if has_add:
            y = y + r[1][...]
        o[0][...] = y.astype(out_dtype)

    (y,), _ = _rows_call(name, _tile(S, tm), [a] + ([add] if has_add else []), ws, [(n, out_dtype)], [], body)
    return y


def _mm_k_t(terms, *, out_dtype=F32, name, tm=512):
    S = terms[0][0].shape[0]
    k = terms[0][1][0][1][1]

    def body(r, f, o, acc, s):
        y = None
        for t in range(len(terms)):
            yt = _cat_nt(r[t][...].astype(BF16), f[N_CHIPS * t:N_CHIPS * (t + 1)])
            y = yt if y is None else y + yt
        o[0][...] = y.astype(out_dtype)

    (y,), _ = _rows_call(name, _tile(S, tm), [a for a, _ in terms], [w for _, ws in terms for w in ws],
                         [(N_CHIPS * k, out_dtype)], [], body)
    return y


def _rms_fwd(x, g, *, name):
    def body(r, f, o, acc, s):
        o[0][...] = _rms_tile(r[0][...], f[0][...]).astype(BF16)

    (y,), _ = _rows_call(name, _tile(x.shape[0], 256, 8), [x], [_vec(g)], [(x.shape[1], BF16)], [], body)
    return y


def _rms_dg(x, g, dy, *, name):
    def body(r, f, o, acc, s):
        acc[0][...] += _rms_bwd_tile(r[0][...], f[0][...], r[1][...])[1]

    _, (dg,) = _rows_call(name, _tile(x.shape[0], 256, 8), [x, dy], [_vec(g)], [], [((1, x.shape[1]), F32)], body)
    return dg


def _ffn_up(x, g, wg, wu, *, name, tm=256):
    S, D = x.shape
    h = wg[0][1][2]

    def body(r, f, o, acc, s):
        xn = _rms_tile(r[0][...], f[0][...]).astype(BF16)
        o[0][...] = xn
        for p in range(N_CHIPS):
            gate = _dot(xn, f[1 + p][...])
            up = _dot(xn, f[1 + N_CHIPS + p][...])
            o[1][p] = gate.astype(BF16)
            o[2][p] = up.astype(BF16)
            o[3][p] = (gate * _sigmoid(gate) * up).astype(BF16)

    (xn, gate, up, hid), _ = _rows_call(name, _tile(S, tm), [x], [_vec(g)] + wg + wu,
                                        [(D, BF16), (N_CHIPS, h, BF16), (N_CHIPS, h, BF16), (N_CHIPS, h, BF16)], [],
                                        body)
    return xn, gate, up, hid


def _ffn_bwd_hidden(dy, wd, gate, up, token=None, *, name, tm=256):
    S = dy.shape[0]
    h = wd[0][1][1]

    def body(r, f, o, acc, s):
        dyv = r[0][...]
        if token is not None:
            dyv = dyv + jnp.sum(f[N_CHIPS][...])
        dyb = dyv.astype(BF16)
        for p in range(N_CHIPS):
            dh = _dot_nt(dyb, f[p][...])
            gv = r[1][p].astype(F32)
            sg = _sigmoid(gv)
            o[0][p] = (dh * r[2][p].astype(F32) * (sg * (1.0 + gv * (1.0 - sg)))).astype(BF16)
            o[1][p] = (dh * gv * sg).astype(BF16)

    (dg, du), _ = _rows_call(name, _tile(S, tm), [dy, gate, up], wd + ([] if token is None else [token]),
                             [(N_CHIPS, h, BF16), (N_CHIPS, h, BF16)], [], body)
    return dg, du


def _ffn_in_bwd(dg, du, wg, wu, x, g, dres, *, name, tm=256):
    S, D = x.shape

    def body(r, f, o, acc, s):
        tot = None
        for p in range(N_CHIPS):
            y = _dot_nt(r[0][p], f[1 + p][...]) + _dot_nt(r[1][p], f[1 + N_CHIPS + p][...])
            tot = y if tot is None else tot + y
        dx, dgn = _rms_bwd_tile(r[2][...], f[0][...], tot)
        o[0][...] = dx + r[3][...]
        acc[0][...] += dgn

    (dx,), (dgn,) = _rows_call(name, _tile(S, tm), [dg, du, x, dres], [_vec(g)] + wg + wu, [(D, F32)],
                               [((1, D), F32)], body)
    return dx, dgn


def _norm_bwd_k(da, ws, x, g, dres, *, name, tm=512):
    S, D = x.shape

    def body(r, f, o, acc, s):
        dx, dg = _rms_bwd_tile(r[1][...], f[0][...], _cat_nt(r[0][...].astype(BF16), f[1:]))
        o[0][...] = dx + r[2][...]
        acc[0][...] += dg

    (dx,), (dg,) = _rows_call(name, _tile(S, tm), [da, x, dres], [_vec(g)] + ws, [(D, F32)], [((1, D), F32)], body)
    return dx, dg


def _norm_bwd_n(das, ws, x, g, dres, *, name, tm=256):
    S, D = x.shape
    n = ws[0][1][2]

    def body(r, f, o, acc, s):
        tot = None
        for p in range(N_CHIPS):
            y = _dot_nt(r[p // 2][:, _cols(p % 2, n)], f[1 + p][...])
            tot = y if tot is None else tot + y
        dx, dg = _rms_bwd_tile(r[2][...], f[0][...], tot)
        o[0][...] = dx + r[3][...]
        acc[0][...] += dg

    (dx,), (dg,) = _rows_call(name, _tile(S, tm), list(das) + [x, dres], [_vec(g)] + ws, [(D, F32)], [((1, D), F32)],
                              body)
    return dx, dg


def _ln_stats(v):
    mu = _mean(v)
    xc = v - mu
    rstd = lax.rsqrt(_mean(xc * xc) + EPS)
    return xc * rstd, rstd


_SHIFTS = 8


def _fill_shifts(sh_ref, ext_ref, tm):
    sh_ref[0] = ext_ref[...]
    for s in range(1, _SHIFTS):
        sh_ref[s, 0:tm + CONV_HALO - _SHIFTS, :] = ext_ref[pl.ds(s, tm + CONV_HALO - _SHIFTS), :]


def _window(sh_ref, off, tm):
    return sh_ref[off % _SHIFTS, pl.ds(off - off % _SHIFTS, tm), :]


def _even_fwd(proj, wm, bcol, cw, cb, lg, lb, *, name):
    S = proj.shape[0]
    tm = _tile(S, 256)
    hb = tm // CONV_HALO
    nblk = tm // GMLP_BLOCK

    def body(p_ref, halo_ref, wm_ref, b_ref, cw_ref, cb_ref, lg_ref, lb_ref, mix_ref, hc_ref, hext_ref, hsh_ref):
        i = pl.program_id(0)
        gu, _ = _gelu(p_ref[:, 0:A_WIDTH])
        gv, _ = _gelu(p_ref[:, A_WIDTH:2 * A_WIDTH])
        vn, _ = _ln_stats(gv)
        vnb = vn.astype(BF16)
        for n in range(nblk):
            rows = slice(n * GMLP_BLOCK, (n + 1) * GMLP_BLOCK)
            for g in range(A_GROUPS):
                cols = slice(g * GMLP_BLOCK, (g + 1) * GMLP_BLOCK)
                sg = jnp.dot(wm_ref[g], vnb[rows, cols], preferred_element_type=F32) + b_ref[g]
                mix_ref[rows, cols] = (gu[rows, cols] * sg).astype(BF16)
        h = p_ref[:, 1024:1536] * _sigmoid(p_ref[:, 1536:2048])
        hh = halo_ref[:, 0:B_WIDTH] * _sigmoid(halo_ref[:, B_WIDTH:2 * B_WIDTH])
        hext_ref[0:CONV_HALO, :] = jnp.where(i > 0, hh, 0.0)
        hext_ref[CONV_HALO:CONV_HALO + tm, :] = h
        _fill_shifts(hsh_ref, hext_ref, tm)
        acc = jnp.zeros((tm, B_WIDTH), F32)
        for k in range(CONV_WIDTH):
            acc = acc + cw_ref[k:k + 1, :] * _window(hsh_ref, k + CONV_HALO - CONV_WIDTH + 1, tm)
        hc = acc + cb_ref[...]
        hc_ref[...] = hc
        hhat, _ = _ln_stats(hc)
        hl = hhat * lg_ref[...] + lb_ref[...]
        mix_ref[:, A_WIDTH:A_WIDTH + B_WIDTH] = (hl * _sigmoid(hl)).astype(BF16)

    vec = pl.BlockSpec((1, B_WIDTH), lambda i: (0, 0))
    return pl.pallas_call(
        body, name=name, grid=(S // tm,),
        in_specs=[
            pl.BlockSpec((tm, 2048), lambda i: (i, 0)),
            pl.BlockSpec((CONV_HALO, 1024), lambda i: (jnp.maximum(i * hb - 1, 0), 1)),
            pl.BlockSpec((A_GROUPS, GMLP_BLOCK, GMLP_BLOCK), lambda i: (0, 0, 0)),
            pl.BlockSpec((A_GROUPS, GMLP_BLOCK, 1), lambda i: (0, 0, 0)),
            pl.BlockSpec((CONV_HALO, B_WIDTH), lambda i: (0, 0)),
            vec, vec, vec,
        ],
        out_specs=[pl.BlockSpec((tm, 1024), lambda i: (i, 0)), pl.BlockSpec((tm, B_WIDTH), lambda i: (i, 0))],
        out_shape=[jax.ShapeDtypeStruct((S, 1024), BF16), jax.ShapeDtypeStruct((S, B_WIDTH), F32)],
        scratch_shapes=[pltpu.VMEM((tm + CONV_HALO, B_WIDTH), F32),
                        pltpu.VMEM((_SHIFTS, tm + CONV_HALO, B_WIDTH), F32)],
        compiler_params=_params(("parallel",)),
    )(proj, proj, wm, bcol, cw, cb, lg, lb)


def _even_bwd1(proj, dmix, hc, wm, wmt, bcol, lg, lb, *, name):
    S = proj.shape[0]
    tm = _tile(S, 256)
    nblk = tm // GMLP_BLOCK

    def body(p_ref, dm_ref, hc_ref, wm_ref, wmt_ref, b_ref, lg_ref, lb_ref,
             dpa_ref, dhc_ref, dwm_ref, db_ref, dlg_ref, dlb_ref, dcb_ref, dgu_ref, dvn_ref):
        @pl.when(pl.program_id(0) == 0)
        def _():
            dwm_ref[...] = jnp.zeros_like(dwm_ref)
            db_ref[...] = jnp.zeros_like(db_ref)
            dlg_ref[...] = jnp.zeros_like(dlg_ref)
            dlb_ref[...] = jnp.zeros_like(dlb_ref)
            dcb_ref[...] = jnp.zeros_like(dcb_ref)

        au = p_ref[:, 0:A_WIDTH]
        av = p_ref[:, A_WIDTH:2 * A_WIDTH]
        gu, tu = _gelu(au)
        gv, tv = _gelu(av)
        vn, rstd = _ln_stats(gv)
        vnb = vn.astype(BF16)
        for n in range(nblk):
            rows = slice(n * GMLP_BLOCK, (n + 1) * GMLP_BLOCK)
            for g in range(A_GROUPS):
                cols = slice(g * GMLP_BLOCK, (g + 1) * GMLP_BLOCK)
                vb = vnb[rows, cols]
                sg = jnp.dot(wm_ref[g], vb, preferred_element_type=F32) + b_ref[g]
                da = dm_ref[rows, cols]
                dsg = da * gu[rows, cols]
                dgu_ref[rows, cols] = da * sg
                dsgb = dsg.astype(BF16)
                dwm_ref[g] += _dot_nt(dsgb, vb)
                db_ref[g] += jnp.sum(dsg, axis=1, keepdims=True)
                dvn_ref[rows, cols] = jnp.dot(wmt_ref[g], dsgb, preferred_element_type=F32)
        dvn = dvn_ref[...]
        dgv = rstd * (dvn - _mean(dvn) - vn * _mean(dvn * vn))
        dpa_ref[:, 0:A_WIDTH] = (dgu_ref[...] * _gelu_grad(au, tu)).astype(BF16)
        dpa_ref[:, A_WIDTH:2 * A_WIDTH] = (dgv * _gelu_grad(av, tv)).astype(BF16)
        hhat, rstd2 = _ln_stats(hc_ref[...])
        lgv = lg_ref[...]
        hl = hhat * lgv + lb_ref[...]
        s = _sigmoid(hl)
        dhl = dm_ref[:, A_WIDTH:A_WIDTH + B_WIDTH] * (s * (1.0 + hl * (1.0 - s)))
        dlg_ref[...] += jnp.sum(dhl * hhat, axis=0, keepdims=True)
        dlb_ref[...] += jnp.sum(dhl, axis=0, keepdims=True)
        dhh = dhl * lgv
        dhc = rstd2 * (dhh - _mean(dhh) - hhat * _mean(dhh * hhat))
        dcb_ref[...] += jnp.sum(dhc, axis=0, keepdims=True)
        dhc_ref[...] = dhc

    vec = pl.BlockSpec((1, B_WIDTH), lambda i: (0, 0))
    w3 = pl.BlockSpec((A_GROUPS, GMLP_BLOCK, GMLP_BLOCK), lambda i: (0, 0, 0))
    b3 = pl.BlockSpec((A_GROUPS, GMLP_BLOCK, 1), lambda i: (0, 0, 0))
    return pl.pallas_call(
        body, name=name, grid=(S // tm,),
        in_specs=[
            pl.BlockSpec((tm, 1024), lambda i: (i, 0)),
            pl.BlockSpec((tm, 1024), lambda i: (i, 0)),
            pl.BlockSpec((tm, B_WIDTH), lambda i: (i, 0)),
            w3, w3, b3, vec, vec,
        ],
        out_specs=[pl.BlockSpec((tm, 1024), lambda i: (i, 0)), pl.BlockSpec((tm, B_WIDTH), lambda i: (i, 0)),
                   w3, b3, vec, vec, vec],
        out_shape=[
            jax.ShapeDtypeStruct((S, 1024), BF16), jax.ShapeDtypeStruct((S, B_WIDTH), F32),
            jax.ShapeDtypeStruct((A_GROUPS, GMLP_BLOCK, GMLP_BLOCK), F32),
            jax.ShapeDtypeStruct((A_GROUPS, GMLP_BLOCK, 1), F32),
            jax.ShapeDtypeStruct((1, B_WIDTH), F32), jax.ShapeDtypeStruct((1, B_WIDTH), F32),
            jax.ShapeDtypeStruct((1, B_WIDTH), F32),
        ],
        scratch_shapes=[pltpu.VMEM((tm, A_WIDTH), F32), pltpu.VMEM((tm, A_WIDTH), F32)],
        compiler_params=_params(("arbitrary",)),
    )(proj, dmix, hc, wm, wmt, bcol, lg, lb)


def _even_bwd2(proj, dhc, cw, *, name):
    S = proj.shape[0]
    tm = _tile(S, 256)
    hb = tm // CONV_HALO
    nt = S // tm
    last_halo = S // CONV_HALO - 1
    lo = CONV_HALO - CONV_WIDTH + 1

    def body(p_ref, halo_ref, d_ref, dnext_ref, cw_ref, dpb_ref, dcw_ref, hext_ref, dext_ref, hsh_ref, dsh_ref):
        i = pl.program_id(0)

        @pl.when(i == 0)
        def _():
            dcw_ref[...] = jnp.zeros_like(dcw_ref)

        ba = p_ref[:, 0:B_WIDTH]
        sg = _sigmoid(p_ref[:, B_WIDTH:2 * B_WIDTH])
        hh = halo_ref[:, 0:B_WIDTH] * _sigmoid(halo_ref[:, B_WIDTH:2 * B_WIDTH])
        hext_ref[0:CONV_HALO, :] = jnp.where(i > 0, hh, 0.0)
        hext_ref[CONV_HALO:CONV_HALO + tm, :] = ba * sg
        dhc_t = d_ref[...]
        dext_ref[0:tm, :] = dhc_t
        dext_ref[tm:tm + CONV_HALO, :] = jnp.where(i < nt - 1, dnext_ref[...], 0.0)
        _fill_shifts(hsh_ref, hext_ref, tm)
        _fill_shifts(dsh_ref, dext_ref, tm)
        dh = jnp.zeros((tm, B_WIDTH), F32)
        for k in range(CONV_WIDTH):
            dh = dh + cw_ref[k:k + 1, :] * _window(dsh_ref, CONV_WIDTH - 1 - k, tm)
            dcw_ref[k:k + 1, :] += jnp.sum(dhc_t * _window(hsh_ref, k + lo, tm), axis=0, keepdims=True)
        dpb_ref[:, 0:B_WIDTH] = (dh * sg).astype(BF16)
        dpb_ref[:, B_WIDTH:2 * B_WIDTH] = (dh * ba * sg * (1.0 - sg)).astype(BF16)

    return pl.pallas_call(
        body, name=name, grid=(nt,),
        in_specs=[
            pl.BlockSpec((tm, 1024), lambda i: (i, 1)),
            pl.BlockSpec((CONV_HALO, 1024), lambda i: (jnp.maximum(i * hb - 1, 0), 1)),
            pl.BlockSpec((tm, B_WIDTH), lambda i: (i, 0)),
            pl.BlockSpec((CONV_HALO, B_WIDTH), lambda i: (jnp.minimum((i + 1) * hb, last_halo), 0)),
            pl.BlockSpec((CONV_HALO, B_WIDTH), lambda i: (0, 0)),
        ],
        out_specs=[pl.BlockSpec((tm, 1024), lambda i: (i, 0)), pl.BlockSpec((CONV_HALO, B_WIDTH), lambda i: (0, 0))],
        out_shape=[jax.ShapeDtypeStruct((S, 1024), BF16), jax.ShapeDtypeStruct((CONV_HALO, B_WIDTH), F32)],
        scratch_shapes=[pltpu.VMEM((tm + CONV_HALO, B_WIDTH), F32), pltpu.VMEM((tm + CONV_HALO, B_WIDTH), F32),
                        pltpu.VMEM((_SHIFTS, tm + CONV_HALO, B_WIDTH), F32),
                        pltpu.VMEM((_SHIFTS, tm + CONV_HALO, B_WIDTH), F32)],
        compiler_params=_params(("arbitrary",)),
    )(proj, proj, dhc, dhc, cw)


_CA_SCALE = CA_HEAD_DIM ** -0.5


def _softmax_rows(s):
    e = jnp.exp(s - jnp.max(s, axis=-1, keepdims=True))
    return e / jnp.sum(e, axis=-1, keepdims=True)


def _attn_fwd(q, k, v, *, name):
    S = q.shape[0]

    def body(r, f, o, acc, s):
        for h in range(CA_HEADS):
            cols = _cols(h, CA_HEAD_DIM)
            p = _softmax_rows(_dot_nt(r[0][:, cols], f[0][:, cols]) * _CA_SCALE)
            o[0][:, cols] = _dot(p.astype(BF16), f[1][:, cols]).astype(BF16)

    (o_,), _ = _rows_call(name, _tile(S, 512), [q], [k, v], [(D_MODEL, BF16)], [], body)
    return o_


def _attn_bwd(dy, wo, q, k, v, *, name):
    S = q.shape[0]
    M = k.shape[0]

    def body(r, f, o, acc, s):
        dyb = r[0][...].astype(BF16)
        for h in range(CA_HEADS):
            cols = _cols(h, CA_HEAD_DIM)
            qh = r[1][:, cols]
            kh = f[0][:, cols]
            vh = f[1][:, cols]
            doh = _dot_nt(dyb, f[2 + h][...]).astype(BF16)
            p = _softmax_rows(_dot_nt(qh, kh) * _CA_SCALE)
            acc[1][:, cols] += _dot_tn(p.astype(BF16), doh)
            dp = _dot_nt(doh, vh)
            ds = (p * (dp - jnp.sum(dp * p, axis=-1, keepdims=True)) * _CA_SCALE).astype(BF16)
            o[0][:, cols] = _dot(ds, kh).astype(BF16)
            acc[0][:, cols] += _dot_tn(ds, qh)

    (dq,), (dk, dv) = _rows_call(name, _tile(S, 512), [dy, q], [k, v] + wo, [(D_MODEL, BF16)],
                                 [((M, D_MODEL), F32), ((M, D_MODEL), F32)], body)
    return dq, dk, dv


_STATE_TILE = 2 * STATE_ROWS


def _state_cols(ref, tm):
    return jnp.concatenate([ref[pl.ds(c, tm, stride=_STATE_TILE), :].astype(BF16) for c in range(_STATE_TILE)], axis=1)


def _put_state_cols(ref, y, tm):
    for c in range(_STATE_TILE):
        ref[pl.ds(c, tm, stride=_STATE_TILE), :] = y[:, _cols(c, STATE_LANES)]


def _mm_to_state(a, w, *, nt=False, name, tm=256):
    S = a.shape[0]
    tm = _tile(S, tm)

    def body(r, f, o, acc, s):
        av = r[0][...].astype(BF16)
        _put_state_cols(o[0], _dot_nt(av, f[0][...]) if nt else _dot(av, f[0][...]), tm)

    (y,), _ = _rows_call(name, tm, [a], [w], [("state", F32)], [], body)
    return y


def _s5_readout(xs, cd, u, d, *, name, tm=256):
    tm = _tile(u.shape[0], tm)

    def body(r, f, o, acc, s):
        y = _dot(_state_cols(r[0], tm), f[0][...]) + f[1][...] * r[1][...]
        o[0][...] = y
        o[1][...] = _gelu(y)[0].astype(BF16)

    (y, yg), _ = _rows_call(name, tm, [xs, u], [cd, d], [(C_WIDTH, F32), (C_WIDTH, BF16)], [], body)
    return y, yg


def _state_grad_tn(a, b, *, name, ts=256):
    S = min(a.shape[0], b.shape[0])
    ts = _tile(S, ts)
    a_state, b_state = a.shape[0] != S, b.shape[0] != S
    K1 = 2 * N_STATE if a_state else a.shape[1]
    N = 2 * N_STATE if b_state else b.shape[1]

    def body(r, f, o, acc, s):
        av = _state_cols(r[0], ts) if a_state else r[0][...].astype(BF16)
        bv = _state_cols(r[1], ts) if b_state else r[1][...].astype(BF16)
        acc[0][...] += _dot_tn(av, bv)

    _, (out,) = _rows_call(name, ts, [a, b], [], [], [((K1, N), F32)], body)
    return out


def _glu_out(yg, ws, x, *, name, tm=512):
    n = ws[0][1][2]

    def body(r, f, o, acc, s):
        ygv = r[0][...]
        ov = [_dot(ygv, f[p][...]) for p in range(N_CHIPS)]
        for p in range(N_CHIPS):
            o[0][:, _cols(p, n)] = ov[p].astype(BF16)
        for p in range(2):
            o[1][:, _cols(p, n)] = r[1][:, _cols(p, n)] + ov[p] * _sigmoid(ov[2 + p])

    (o_, y), _ = _rows_call(name, _tile(x.shape[0], tm), [yg, x], ws, [(2 * D_MODEL, BF16), (D_MODEL, F32)], [], body)
    return o_, y


def _glu_out_bwd(o_, dy, ws, y, u, d, *, name, tm=256):
    n = ws[0][1][2]

    def body(r, f, o, acc, s):
        o1 = r[0][:, 0:D_MODEL].astype(F32)
        sg = _sigmoid(r[0][:, D_MODEL:2 * D_MODEL].astype(F32))
        dyv = r[1][...]
        do1 = (dyv * sg).astype(BF16)
        do2 = (dyv * o1 * sg * (1.0 - sg)).astype(BF16)
        o[0][:, 0:D_MODEL] = do1
        o[0][:, D_MODEL:2 * D_MODEL] = do2
        dyg = None
        for p in range(N_CHIPS):
            t = _dot_nt((do1 if p < 2 else do2)[:, _cols(p % 2, n)], f[1 + p][...])
            dyg = t if dyg is None else dyg + t
        yv = r[2][...]
        dys = dyg * _gelu_grad(yv, _gelu(yv)[1])
        o[1][...] = dys.astype(BF16)
        o[2][...] = f[0][...] * dys
        acc[0][...] += jnp.sum(dys * r[3][...], axis=0, keepdims=True)

    (do, dys, dus), (dd,) = _rows_call(name, _tile(dy.shape[0], tm), [o_, dy, y, u], [d] + ws,
                                       [(2 * D_MODEL, BF16), (C_WIDTH, BF16), (C_WIDTH, F32)], [((1, C_WIDTH), F32)],
                                       body)
    return do, dys, dus, dd


def _s5_in_bwd(gs, bd, dus, ws, x, g, dres, *, name, tm=256):
    D = x.shape[1]
    tm = _tile(x.shape[0], tm)

    def body(r, f, o, acc, s):
        du = (_dot_nt(_state_cols(r[0], tm), f[1][...]) + r[1][...]).astype(BF16)
        o[0][...] = du
        dx, dg = _rms_bwd_tile(r[2][...], f[0][...], _cat_nt(du, f[2:]))
        o[1][...] = dx + r[3][...]
        acc[0][...] += dg

    (du, dx), (dg,) = _rows_call(name, tm, [gs, dus, x, dres], [_vec(g), bd] + ws,
                                 [(C_WIDTH, BF16), (D, F32)], [((1, D), F32)], body)
    return du, dx, dg


_SCAN_CHUNK = 128
_RE = slice(0, STATE_ROWS)
_IM = slice(STATE_ROWS, 2 * STATE_ROWS)


def _scan_fwd(bu, pw, *, name):
    S = bu.shape[0]
    tc = _tile(S, _SCAN_CHUNK, 8)

    def body(bu_ref, pw_ref, xs_ref, st_ref):
        @pl.when(pl.program_id(0) == 0)
        def _():
            st_ref[...] = jnp.zeros_like(st_ref)

        ar = pw_ref[0, _RE, :]
        ai = pw_ref[0, _IM, :]

        def block(i, carry):
            xr, xi = carry
            t0 = i * SCAN_BLOCK
            cr = ci = nr = ni = None
            for j in range(SCAN_BLOCK):
                br, bi = bu_ref[t0 + j, _RE, :], bu_ref[t0 + j, _IM, :]
                cr, ci = (br, bi) if j == 0 else (ar * cr - ai * ci + br, ar * ci + ai * cr + bi)
                pr, pi = pw_ref[j, _RE, :], pw_ref[j, _IM, :]
                nr = pr * xr - pi * xi + cr
                ni = pr * xi + pi * xr + ci
                xs_ref[t0 + j, _RE, :] = nr
                xs_ref[t0 + j, _IM, :] = ni
            return nr, ni

        xr, xi = lax.fori_loop(0, tc // SCAN_BLOCK, block, (st_ref[_RE, :], st_ref[_IM, :]), unroll=2)
        st_ref[_RE, :] = xr
        st_ref[_IM, :] = xi

    blk = pl.BlockSpec((tc, 2 * STATE_ROWS, STATE_LANES), lambda i: (i, 0, 0))
    return pl.pallas_call(
        body, name=name, grid=(S // tc,),
        in_specs=[blk, pl.BlockSpec(pw.shape, lambda i: (0, 0, 0))], out_specs=blk,
        out_shape=jax.ShapeDtypeStruct(bu.shape, F32),
        scratch_shapes=[pltpu.VMEM((2 * STATE_ROWS, STATE_LANES), F32)],
        compiler_params=_params(("arbitrary",)),
    )(bu, pw)


def _scan_bwd(dxs, xs, pw, *, name):
    S = dxs.shape[0]
    tc = _tile(S, _SCAN_CHUNK, 8)
    nc = S // tc

    def body(dx_ref, xs_ref, pw_ref, g_ref, da_ref, st_ref):
        @pl.when(pl.program_id(0) == 0)
        def _():
            st_ref[...] = jnp.zeros_like(st_ref)
            da_ref[...] = jnp.zeros_like(da_ref)

        ar = pw_ref[0, _RE, :]
        ai = pw_ref[0, _IM, :]

        def block(i, carry):
            gr, gi, dar, dai = carry
            top = tc - 1 - i * SCAN_BLOCK
            cr = ci = None
            pgr, pgi = gr, gi
            for j in range(SCAN_BLOCK):
                t = top - j
                xr, xi = xs_ref[t, _RE, :], xs_ref[t, _IM, :]
                dar = dar + pgr * xr + pgi * xi
                dai = dai + pgi * xr - pgr * xi
                dr, di = dx_ref[t, _RE, :], dx_ref[t, _IM, :]
                cr, ci = (dr, di) if j == 0 else (ar * cr + ai * ci + dr, ar * ci - ai * cr + di)
                pr, pi = pw_ref[j, _RE, :], pw_ref[j, _IM, :]
                pgr = pr * gr + pi * gi + cr
                pgi = pr * gi - pi * gr + ci
                g_ref[t, _RE, :] = pgr
                g_ref[t, _IM, :] = pgi
            return pgr, pgi, dar, dai

        init = (st_ref[_RE, :], st_ref[_IM, :], da_ref[_RE, :], da_ref[_IM, :])
        gr, gi, dar, dai = lax.fori_loop(0, tc // SCAN_BLOCK, block, init, unroll=2)
        st_ref[_RE, :] = gr
        st_ref[_IM, :] = gi
        da_ref[_RE, :] = dar
        da_ref[_IM, :] = dai

    blk = pl.BlockSpec((tc, 2 * STATE_ROWS, STATE_LANES), lambda i: (nc - 1 - i, 0, 0))
    vec = pl.BlockSpec((2 * STATE_ROWS, STATE_LANES), lambda i: (0, 0))
    return pl.pallas_call(
        body, name=name, grid=(nc,), in_specs=[blk, blk, pl.BlockSpec(pw.shape, lambda i: (0, 0, 0))],
        out_specs=[blk, vec],
        out_shape=[jax.ShapeDtypeStruct(dxs.shape, F32), jax.ShapeDtypeStruct((2 * STATE_ROWS, STATE_LANES), F32)],
        scratch_shapes=[pltpu.VMEM((2 * STATE_ROWS, STATE_LANES), F32)],
        compiler_params=_params(("arbitrary",)),
    )(dxs, xs, pw)


def _loss_head(x, g, target, *, name):
    S, D = x.shape

    def body(r, f, o, acc, s):
        xv = r[0][...]
        gv = f[0][...]
        rs = lax.rsqrt(_mean(xv * xv) + EPS)
        xh = xv * rs
        err = xh * gv - r[1][...]
        acc[1][...] += 0.5 * jnp.sum(_mean(err * err), axis=0, keepdims=True)
        dy = err * (1.0 / D)
        dyg = dy * gv
        o[0][...] = rs * (dyg - xh * _mean(dyg * xh))
        acc[0][...] += jnp.sum(dy * xh, axis=0, keepdims=True)

    (dx,), (dg, loss) = _rows_call(name, _tile(S, 256, 8), [x, target], [_vec(g)], [(D, F32)],
                                   [((1, D), F32), ((1, 128), F32)], body)
    return dx, dg, loss


_ADAM_C1 = 1.0 - ADAM_B1 ** ADAM_STEP
_ADAM_C2 = 1.0 - ADAM_B2 ** ADAM_STEP
_ONE_BLOCK_BYTES = 8 * 1024 * 1024


def _adamw_math(w, g, m, v):
    nm = ADAM_B1 * m + (1.0 - ADAM_B1) * g
    nv = ADAM_B2 * v + (1.0 - ADAM_B2) * (g * g)
    m_hat = nm / _ADAM_C1
    v_hat = nv / _ADAM_C2
    return -ADAM_LR * (m_hat / (jnp.sqrt(v_hat) + ADAM_EPS) + ADAM_WD * w), nm, nv


def _adamw_shard(w, gsrc, m, v, *, name):
    R, C = w.shape
    n_l = len(gsrc)
    rows = R // n_l
    tr = rows
    for _, r0 in gsrc:
        tr = math.gcd(tr, r0) if r0 else tr
    tr = _tile(tr, 256, 8) if tr > 256 else tr
    nb = rows // tr
    assert rows % tr == 0 and all(r0 % tr == 0 for _, r0 in gsrc)

    def body(*refs):
        w_ref, g_refs, (m_ref, v_ref, go_ref, d_ref, nm_ref, nv_ref) = refs[0], refs[1:1 + n_l], refs[1 + n_l:]
        layer = pl.program_id(0) // nb
        gv = g_refs[0][...]
        for l in range(1, n_l):
            gv = jnp.where(layer == l, g_refs[l][...], gv)
        go_ref[...] = gv
        d_ref[...], nm_ref[...], nv_ref[...] = _adamw_math(w_ref[...], gv, m_ref[...], v_ref[...])

    def g_spec(l, r0):
        return pl.BlockSpec((tr, C), lambda i: (r0 // tr + jnp.clip(i - l * nb, 0, nb - 1), 0))

    blk = pl.BlockSpec((tr, C), lambda i: (i, 0))
    out = jax.ShapeDtypeStruct((R, C), F32)
    return pl.pallas_call(
        body, name=name, grid=(R // tr,),
        in_specs=[blk] + [g_spec(l, r0) for l, (_, r0) in enumerate(gsrc)] + [blk, blk], out_specs=[blk] * 4,
        out_shape=[out] * 4, compiler_params=_params(("parallel",)),
    )(w, *[g for g, _ in gsrc], m, v)


def _adamw_small(ws, gs, ms, vs, *, name):
    n = len(ws)

    def body(*refs):
        w_r, g_r, m_r, v_r = refs[:n], refs[n:2 * n], refs[2 * n:3 * n], refs[3 * n:4 * n]
        d_r, nm_r, nv_r = refs[4 * n:5 * n], refs[5 * n:6 * n], refs[6 * n:7 * n]
        for k in range(n):
            d_r[k][...], nm_r[k][...], nv_r[k][...] = _adamw_math(w_r[k][...], g_r[k][...], m_r[k][...], v_r[k][...])

    vm = pl.BlockSpec(memory_space=pltpu.VMEM)
    out = [jax.ShapeDtypeStruct(w.shape, F32) for w in ws]
    res = pl.pallas_call(body, name=name, in_specs=[vm] * (4 * n), out_specs=[vm] * (3 * n), out_shape=out * 3,
                         compiler_params=pltpu.CompilerParams(vmem_limit_bytes=VMEM_LIMIT))(*ws, *gs, *ms, *vs)
    return res[:n], res[n:2 * n], res[2 * n:]


def _sum_slots(x, *, name):
    n, R, C = x.shape
    tr = R if (n + 1) * R * C * 4 <= _ONE_BLOCK_BYTES else _tile(R, 256, 8)

    def body(x_ref, o_ref):
        acc = x_ref[0]
        for k in range(1, n):
            acc = acc + x_ref[k]
        o_ref[...] = acc

    return pl.pallas_call(
        body, name=name, grid=(R // tr,),
        in_specs=[pl.BlockSpec((n, tr, C), lambda i: (0, i, 0))], out_specs=pl.BlockSpec((tr, C), lambda i: (i, 0)),
        out_shape=jax.ShapeDtypeStruct((R, C), F32), compiler_params=_params(("parallel",)),
    )(x)


def _pair_sum(g, r, half, *, name):
    n, R, C = g.shape
    Rh = R // 2
    tr = _tile(Rh, 256, 8)
    nb = Rh // tr

    def body(half_ref, g_ref, r_ref, o_ref):
        o_ref[...] = (g_ref[...] + r_ref[...]).astype(BF16)

    return pl.pallas_call(
        body, name=name,
        grid_spec=pltpu.PrefetchScalarGridSpec(
            num_scalar_prefetch=1, grid=(n, nb),
            in_specs=[pl.BlockSpec((1, tr, C), lambda p, i, h: (p, h[0] * nb + i, 0)),
                      pl.BlockSpec((1, tr, C), lambda p, i, h: (p, i, 0))],
            out_specs=pl.BlockSpec((1, tr, C), lambda p, i, h: (p, i, 0)),
        ),
        out_shape=jax.ShapeDtypeStruct((n, Rh, C), BF16), compiler_params=_params(("parallel", "parallel")),
    )(half, g, r)


def _chip_sum(g, r, slots, where, *, name):
    n, R, C = g.shape
    Rh = R // 2
    tr = _tile(Rh, 256, 8)
    nb = Rh // tr

    def body(w_ref, g_ref, r_ref, s_ref, o_ref):
        acc = g_ref[0] + r_ref[0]
        for k in range(slots.shape[0]):
            acc = acc + s_ref[k].astype(F32)
        o_ref[...] = acc

    return pl.pallas_call(
        body, name=name,
        grid_spec=pltpu.PrefetchScalarGridSpec(
            num_scalar_prefetch=1, grid=(nb,),
            in_specs=[pl.BlockSpec((1, tr, C), lambda i, w: (w[0], w[1] * nb + i, 0)),
                      pl.BlockSpec((1, tr, C), lambda i, w: (w[0], i, 0)),
                      pl.BlockSpec((slots.shape[0], tr, C), lambda i, w: (0, i, 0))],
            out_specs=pl.BlockSpec((tr, C), lambda i, w: (w[1] * nb + i, 0)),
        ),
        out_shape=jax.ShapeDtypeStruct((R, C), F32), compiler_params=_params(("parallel",)),
    )(where, g, r, slots)


ANY = pl.BlockSpec(memory_space=pl.ANY)


def _place():
    return lax.axis_index("x"), lax.axis_index("y"), lax.axis_index("c")


def _other_chips(x, y):
    return [(1 - x, y), (x, 1 - y), (1 - x, 1 - y)]


def _allgather_small(v, *, name):
    R, C = v.shape

    def body(x_ref, out_ref, send_sems, recv_sems, local_sem):
        x, y, c = _place()
        me, sibling = (x, y, c), (x, y, 1 - c)
        chips = _other_chips(x, y)

        def rows(px, py, pc):
            return out_ref.at[pl.ds((4 * px + 2 * py + pc) * R, R), :]

        def copy(k, block, to, src=None):
            return pltpu.make_async_remote_copy(
                src_ref=rows(*block) if src is None else src, dst_ref=rows(*block),
                send_sem=send_sems.at[k], recv_sem=recv_sems.at[k], device_id=to, device_id_type=MESH)

        mine = pltpu.make_async_copy(x_ref, rows(*me), local_sem)
        mine.start()
        first = [copy(0, me, sibling, src=x_ref)]
        first += [copy(1 + j, me, (*chip, c), src=x_ref) for j, chip in enumerate(chips)]
        for cp in first:
            cp.start()
        passed = [copy(4 + j, (*chip, c), sibling) for j, chip in enumerate(chips)]
        for j, chip in enumerate(chips):
            copy(1 + j, (*chip, c), me).wait_recv()
            passed[j].start()
        copy(0, sibling, me).wait_recv()
        for j, chip in enumerate(chips):
            copy(4 + j, (*chip, 1 - c), me).wait_recv()
        for cp in first + passed:
            cp.wait_send()
        mine.wait()

    return pl.pallas_call(
        body, name=name, out_shape=jax.ShapeDtypeStruct((N_DEV * R, C), v.dtype),
        in_specs=[pl.BlockSpec(memory_space=pltpu.VMEM)], out_specs=pl.BlockSpec(memory_space=pltpu.VMEM),
        scratch_shapes=[pltpu.SemaphoreType.DMA((7,)), pltpu.SemaphoreType.DMA((7,)), pltpu.SemaphoreType.DMA],
        compiler_params=pltpu.CompilerParams(vmem_limit_bytes=VMEM_LIMIT),
    )(v)


def _aliased_comm_call(body, bufs, n_sems, *, name):
    n = len(bufs)
    return pl.pallas_call(
        body, name=name, out_shape=[jax.ShapeDtypeStruct(b.shape, b.dtype) for b in bufs],
        in_specs=[ANY] * n, out_specs=[ANY] * n, input_output_aliases={k: k for k in range(n)},
        scratch_shapes=[pltpu.SemaphoreType.DMA((n_sems,)), pltpu.SemaphoreType.DMA((n_sems,))],
    )(*bufs)


def _allgather_chips(bufs, *, name):
    n = len(bufs)

    def body(*refs):
        outs, send_sems, recv_sems = refs[n:2 * n], refs[2 * n], refs[2 * n + 1]
        x, y, c = _place()
        chips = _other_chips(x, y)

        def copy(b, j, chip, hc, to):
            rh = bufs[b].shape[1] // 2
            part = outs[b].at[2 * chip[0] + chip[1], pl.ds(hc * rh, rh), :]
            return pltpu.make_async_remote_copy(src_ref=part, dst_ref=part, send_sem=send_sems.at[6 * b + j],
                                                recv_sem=recv_sems.at[6 * b + j], device_id=to, device_id_type=MESH)

        first = [copy(b, j, (x, y), c, (*chip, c)) for b in range(n) for j, chip in enumerate(chips)]
        for cp in first:
            cp.start()
        passed = []
        for b in range(n):
            for j, chip in enumerate(chips):
                copy(b, j, chip, c, (x, y, c)).wait_recv()
                passed.append(copy(b, 3 + j, chip, c, (x, y, 1 - c)))
                passed[-1].start()
        for b in range(n):
            for j, chip in enumerate(chips):
                copy(b, 3 + j, chip, 1 - c, (x, y, c)).wait_recv()
        for cp in first + passed:
            cp.wait_send()

    return _aliased_comm_call(body, bufs, 6 * n, name=name)


HBM = pl.BlockSpec(memory_space=pltpu.HBM)
SEM = pl.BlockSpec(memory_space=pltpu.SEMAPHORE)
_SPLIT = pltpu.CompilerParams(has_side_effects=pltpu.SideEffectType.DATAFLOW_SIDE_EFFECTING)


def _in_hbm(arrs):
    return [pltpu.with_memory_space_constraint(a, pltpu.HBM) for a in arrs]


def _gather_ici_start(bufs, after, *, name):
    n = len(bufs)

    def body(*refs):
        send_sems, recv_sems, outs, token = refs[n + 1], refs[n + 2], refs[n + 3:2 * n + 3], refs[2 * n + 3]
        x, y, c = _place()
        for b in range(n):
            rh = bufs[b].shape[1] // 2
            part = outs[b].at[2 * x + y, pl.ds(c * rh, rh), :]
            for j, chip in enumerate(_other_chips(x, y)):
                pltpu.make_async_remote_copy(src_ref=part, dst_ref=part, send_sem=send_sems.at[3 * b + j],
                                             recv_sem=recv_sems.at[3 * b + j], device_id=(*chip, c),
                                             device_id_type=MESH).start()
        token[...] = jnp.zeros_like(token)

    res = pl.pallas_call(
        body, name=name,
        out_shape=(pltpu.SemaphoreType.DMA((3 * n,)), pltpu.SemaphoreType.DMA((3 * n,)),
                   *[pltpu.HBM(b.shape, b.dtype) for b in bufs], jax.ShapeDtypeStruct((8, 128), F32)),
        in_specs=[HBM] * n + [ANY], out_specs=(SEM, SEM, *[HBM] * n, pl.BlockSpec(memory_space=pltpu.VMEM)),
        input_output_aliases={k: k + 2 for k in range(n)}, compiler_params=_SPLIT,
    )(*_in_hbm(bufs), after)
    return res[0], res[1], list(res[2:2 + n]), res[2 + n]


def _gather_ici_wait(send_sems, recv_sems, bufs, after, *, name):
    n = len(bufs)

    def body(*refs):
        ins, ss, rs = refs[:n], refs[n], refs[n + 1]
        x, y, c = _place()
        for b in range(n):
            rh = bufs[b].shape[1] // 2
            mine = ins[b].at[2 * x + y, pl.ds(c * rh, rh), :]
            for j, (cx, cy) in enumerate(_other_chips(x, y)):
                theirs = ins[b].at[2 * cx + cy, pl.ds(c * rh, rh), :]
                cp = pltpu.make_async_remote_copy(src_ref=mine, dst_ref=theirs, send_sem=ss.at[3 * b + j],
                                                  recv_sem=rs.at[3 * b + j], device_id=(cx, cy, c),
                                                  device_id_type=MESH)
                cp.wait_send()
                cp.wait_recv()

    return list(pl.pallas_call(
        body, name=name, out_shape=[pltpu.HBM(b.shape, b.dtype) for b in bufs],
        in_specs=[HBM] * n + [SEM, SEM, ANY], out_specs=[HBM] * n,
        input_output_aliases={k: k for k in range(n)}, compiler_params=_SPLIT,
    )(*bufs, send_sems, recv_sems, after))


def _gather_forward(bufs, *, name):
    n = len(bufs)

    def body(*refs):
        outs, send_sems, recv_sems = refs[n:2 * n], refs[2 * n], refs[2 * n + 1]
        x, y, c = _place()

        def copy(b, j, chip, hc):
            rh = bufs[b].shape[1] // 2
            part = outs[b].at[2 * chip[0] + chip[1], pl.ds(hc * rh, rh), :]
            return pltpu.make_async_remote_copy(src_ref=part, dst_ref=part, send_sem=send_sems.at[3 * b + j],
                                                recv_sem=recv_sems.at[3 * b + j], device_id=(x, y, 1 - c),
                                                device_id_type=MESH)

        sends = [copy(b, j, chip, c) for b in range(n) for j, chip in enumerate(_other_chips(x, y))]
        for cp in sends:
            cp.start()
        for b in range(n):
            for j, chip in enumerate(_other_chips(x, y)):
                copy(b, j, chip, 1 - c).wait_recv()
        for cp in sends:
            cp.wait_send()

    return _aliased_comm_call(body, bufs, 3 * n, name=name)


def _chip_exchange_start(hs, *, name):
    n = len(hs)
    lands = [lax.empty((3,) + h.shape[1:], h.dtype) for h in hs]

    def body(*refs):
        send_sems, recv_sems = refs[2 * n], refs[2 * n + 1]
        h_out, l_out, token = refs[2 * n + 2:3 * n + 2], refs[3 * n + 2:4 * n + 2], refs[4 * n + 2]
        x, y, c = _place()
        for b in range(n):
            for j, (cx, cy) in enumerate(_other_chips(x, y)):
                pltpu.make_async_remote_copy(src_ref=h_out[b].at[2 * cx + cy], dst_ref=l_out[b].at[j],
                                             send_sem=send_sems.at[3 * b + j], recv_sem=recv_sems.at[3 * b + j],
                                             device_id=(cx, cy, c), device_id_type=MESH).start()
        token[...] = jnp.zeros_like(token)

    res = pl.pallas_call(
        body, name=name,
        out_shape=(pltpu.SemaphoreType.DMA((3 * n,)), pltpu.SemaphoreType.DMA((3 * n,)),
                   *[pltpu.HBM(a.shape, a.dtype) for a in hs + lands], jax.ShapeDtypeStruct((8, 128), F32)),
        in_specs=[HBM] * (2 * n), out_specs=(SEM, SEM, *[HBM] * (2 * n), pl.BlockSpec(memory_space=pltpu.VMEM)),
        input_output_aliases={k: k + 2 for k in range(2 * n)}, compiler_params=_SPLIT,
    )(*_in_hbm(hs + lands))
    return res[0], res[1], list(res[2:2 + n]), list(res[2 + n:2 + 2 * n]), res[2 + 2 * n]


def _chip_exchange_wait(send_sems, recv_sems, hs, lands, after, *, name):
    n = len(hs)

    def body(*refs):
        h_in, l_in, ss, rs = refs[:n], refs[n:2 * n], refs[2 * n], refs[2 * n + 1]
        x, y, c = _place()
        for b in range(n):
            for j, (cx, cy) in enumerate(_other_chips(x, y)):
                cp = pltpu.make_async_remote_copy(src_ref=h_in[b].at[2 * cx + cy], dst_ref=l_in[b].at[j],
                                                  send_sem=ss.at[3 * b + j], recv_sem=rs.at[3 * b + j],
                                                  device_id=(cx, cy, c), device_id_type=MESH)
                cp.wait_send()
                cp.wait_recv()

    res = pl.pallas_call(
        body, name=name, out_shape=[pltpu.HBM(a.shape, a.dtype) for a in hs + lands],
        in_specs=[HBM] * (2 * n) + [SEM, SEM, ANY], out_specs=[HBM] * (2 * n),
        input_output_aliases={k: k for k in range(2 * n)}, compiler_params=_SPLIT,
    )(*hs, *lands, send_sems, recv_sems, after)
    return list(res[n:])


def _pair_exchange(gs, *, name):
    n = len(gs)

    def body(*refs):
        ins, outs, send_sems, recv_sems = refs[:n], refs[n:2 * n], refs[2 * n], refs[2 * n + 1]
        x, y, c = _place()
        cps = []
        for b in range(n):
            rh = gs[b].shape[1] // 2
            cps.append(pltpu.make_async_remote_copy(
                src_ref=ins[b].at[:, pl.ds((1 - c) * rh, rh), :], dst_ref=outs[b], send_sem=send_sems.at[b],
                recv_sem=recv_sems.at[b], device_id=(x, y, 1 - c), device_id_type=MESH))
        for cp in cps:
            cp.start()
        for cp in cps:
            cp.wait()

    return pl.pallas_call(
        body, name=name, out_shape=[jax.ShapeDtypeStruct((g.shape[0], g.shape[1] // 2, g.shape[2]), g.dtype) for g in gs],
        in_specs=[ANY] * n, out_specs=[ANY] * n,
        scratch_shapes=[pltpu.SemaphoreType.DMA((n,)), pltpu.SemaphoreType.DMA((n,))],
    )(*gs)


def _pair_share(ss, *, name):
    n = len(ss)

    def body(*refs):
        outs, send_sems, recv_sems = refs[n:2 * n], refs[2 * n], refs[2 * n + 1]
        x, y, c = _place()
        cps = []
        for b in range(n):
            rh = ss[b].shape[0] // 2
            mine = outs[b].at[pl.ds(c * rh, rh), :]
            cps.append(pltpu.make_async_remote_copy(src_ref=mine, dst_ref=mine, send_sem=send_sems.at[b],
                                                    recv_sem=recv_sems.at[b], device_id=(x, y, 1 - c),
                                                    device_id_type=MESH))
        for cp in cps:
            cp.start()
        for b, cp in enumerate(cps):
            rh = ss[b].shape[0] // 2
            theirs = outs[b].at[pl.ds((1 - c) * rh, rh), :]
            pltpu.make_async_remote_copy(src_ref=theirs, dst_ref=theirs, send_sem=send_sems.at[b],
                                         recv_sem=recv_sems.at[b], device_id=(x, y, 1 - c),
                                         device_id_type=MESH).wait_recv()
            cp.wait_send()

    return _aliased_comm_call(body, ss, n, name=name)


_SMALL_SHARDED = (("e_conv_w", 2), ("o_norm", 1), ("o_d", 1))
_REPLICATED = ("e_norm", "e_gmlp_w", "e_gmlp_b", "e_conv_b", "e_conv_ln_g", "e_conv_ln_b", "o_lam_re", "o_lam_im",
               "o_log_dt", "o_b_re", "o_b_im", "o_c_re", "o_c_im", "ca_norm", "ca_mem_norm", "ffn_norm", "final_norm")
_SMALL = tuple(n for n, _ in _SMALL_SHARDED) + _REPLICATED
_WEIGHTS = ("e_norm", "e_w_in", "e_gmlp_w", "e_gmlp_b", "e_conv_w", "e_conv_b", "e_conv_ln_g", "e_conv_ln_b",
            "e_w_out", "o_norm", "o_w_in", "o_lam_re", "o_lam_im", "o_log_dt", "o_b_re", "o_b_im", "o_c_re", "o_c_im",
            "o_d", "o_w_out", "ca_norm", "ca_mem_norm", "ca_wq", "ca_wk", "ca_wv", "ca_wo", "ffn_norm", "ffn_w_gate",
            "ffn_w_up", "ffn_w_down", "final_norm")


def _pack_rows(arrs, width, dtype, row_mult=8):
    parts, spans, r0 = [], [], 0
    for a in arrs:
        flat = a.reshape(-1).astype(dtype)
        rows = -(-flat.shape[0] // (width * row_mult)) * row_mult
        if rows * width != flat.shape[0]:
            flat = jnp.pad(flat, (0, rows * width - flat.shape[0]))
        parts.append(flat.reshape(rows, width))
        spans.append((r0, rows))
        r0 += rows
    return jnp.concatenate(parts, axis=0), spans


def _unpack_rows(slab, spans, shapes):
    out = []
    for (r0, rows), shp in zip(spans, shapes):
        n = math.prod(shp)
        out.append(slab[r0:r0 + rows].reshape(-1)[:n].reshape(shp))
    return out


def _two_d(a):
    return a.reshape(-1, a.shape[-1])


def _local_slab(local, slab, dtype):
    parts = sorted((r0, n, l) for n, (_, where) in _PLACE.items() for l, (s, r0) in enumerate(where) if s == slab)
    shards = [local[n] if len(_PLACE[n][1]) == 1 else local[n][l] for _, n, l in parts]
    return jnp.concatenate([_two_d(a).astype(dtype) for a in shards], axis=0)


def _block_diag(b, pattern):
    return jnp.einsum(pattern, b, jnp.eye(C_GROUPS, dtype=b.dtype))


def _s5_discretize(lam_re, lam_im, log_dt, b_re, b_im):
    dt = jnp.exp(log_dt)[:, None]
    mag = jnp.exp(lam_re * dt)
    ar = mag * jnp.cos(lam_im * dt)
    ai = mag * jnp.sin(lam_im * dt)
    den = lam_re * lam_re + lam_im * lam_im
    qr = ((ar - 1.0) * lam_re + ai * lam_im) / den
    qi = (ai * lam_re - (ar - 1.0) * lam_im) / den
    bbr = qr[..., None] * b_re - qi[..., None] * b_im
    bbi = qr[..., None] * b_im + qi[..., None] * b_re
    return ar, ai, bbr, bbi


def _attention_block(x, mem, W, w, i, tag):
    xn, q = _norm_mm(x, w["ca_norm"][i], _shards(W, "ca_wq", i), split="k", out_dtype=BF16, name=f"{tag}_q")
    memn = _rms_fwd(mem, w["ca_mem_norm"][i], name=f"{tag}_ca_memnorm")
    k = _mm_k(memn, _shards(W, "ca_wk", i), out_dtype=BF16, name=f"{tag}_k")
    v = _mm_k(memn, _shards(W, "ca_wv", i), out_dtype=BF16, name=f"{tag}_v")
    o = _attn_fwd(q, k, v, name=f"{tag}_attn")
    y = _mm_k(o, _shards(W, "ca_wo", i), add=x, name=f"{tag}_wo")
    return y, (x, xn, memn, q, k, v, o)


def _attention_block_bwd(dy, saved, mem, W, w, i, tag, G, grads, token=None):
    x, xn, memn, q, k, v, o = saved
    if token is not None:
        k = _behind(k, token)
    G = _grad_to_slab(G, "ca_wo", i, o, dy, a_cols=256, name=f"{tag}_dwo")
    dq, dk, dv = _attn_bwd(dy, _shards(W, "ca_wo", i), q, k, v, name=f"{tag}_attn_bwd")
    G = _grad_to_slab(G, "ca_wq", i, xn, dq, a_cols=256, name=f"{tag}_dwq")
    G = _grad_to_slab(G, "ca_wk", i, memn, dk, a_cols=256, name=f"{tag}_dwk")
    G = _grad_to_slab(G, "ca_wv", i, memn, dv, a_cols=256, name=f"{tag}_dwv")
    dmemn = _mm_k_t([(dk, _shards(W, "ca_wk", i)), (dv, _shards(W, "ca_wv", i))], name=f"{tag}_dmemn")
    dx, dg = _norm_bwd_k(dq, _shards(W, "ca_wq", i), x, w["ca_norm"][i], dy, name=f"{tag}_dq_norm_bwd")
    grads["ca_norm"][i] = dg[0]
    grads["ca_mem_norm"][i] = _rms_dg(mem, w["ca_mem_norm"][i], dmemn, name=f"{tag}_ca_memnorm_bwd")[0]
    return dx, G


def _ffn_block(x, W, w, i, tag):
    fn, gate, up, h = _ffn_up(x, w["ffn_norm"][i], _shards(W, "ffn_w_gate", i), _shards(W, "ffn_w_up", i),
                              name=f"{tag}_ffn_up")
    y = _mm_k(h, _shards(W, "ffn_w_down", i), add=x, name=f"{tag}_down")
    return y, (x, fn, gate, up, h)


def _ffn_block_bwd(dy, saved, W, w, i, tag, G, grads, token=None):
    x, fn, gate, up, h = saved
    G = _grad_to_slab(G, "ffn_w_down", i, h, dy, name=f"{tag}_dwd")
    dg, du = _ffn_bwd_hidden(dy, _shards(W, "ffn_w_down", i), gate, up, token, name=f"{tag}_ffn_bwd_hidden")
    G = _grad_to_slab(G, "ffn_w_gate", i, fn, dg, name=f"{tag}_dwg")
    G = _grad_to_slab(G, "ffn_w_up", i, fn, du, name=f"{tag}_dwu")
    dx, dgn = _ffn_in_bwd(dg, du, _shards(W, "ffn_w_gate", i), _shards(W, "ffn_w_up", i), x, w["ffn_norm"][i], dy,
                          name=f"{tag}_ffn_in_bwd")
    grads["ffn_norm"][i] = dgn[0]
    return dx, G


def _gmlp_mask():
    chunk = jnp.arange(GMLP_BLOCK) // CHUNK
    return chunk[None, :] <= chunk[:, None]


def _even_block(x, W, w, tag):
    hn, proj = _norm_mm(x, w["e_norm"][0], _shards(W, "e_w_in"), split="n", out_dtype=F32, name=f"{tag}_w_in")
    wm = jnp.where(_gmlp_mask()[None], w["e_gmlp_w"][0], 0.0).astype(BF16)
    bcol = w["e_gmlp_b"][0][:, :, None]
    cw = jnp.pad(w["e_conv_w"][0], ((0, CONV_HALO - CONV_WIDTH), (0, 0)))
    cb, lg, lb = w["e_conv_b"], w["e_conv_ln_g"], w["e_conv_ln_b"]
    mix, hc = _even_fwd(proj, wm, bcol, cw, cb, lg, lb, name=f"{tag}_mixers")
    y = _mm_k(mix, _shards(W, "e_w_out"), add=x, name=f"{tag}_w_out")
    return y, (x, hn, proj, mix, hc, wm, bcol, cw)


def _even_block_bwd(dy, saved, W, w, tag, G, grads):
    x, hn, proj, mix, hc, wm, bcol, cw = saved
    dmix = _mm_k_t([(dy, _shards(W, "e_w_out"))], name=f"{tag}_dmix")
    G = _grad_to_slab(G, "e_w_out", 0, mix, dy, a_cols=256, name=f"{tag}_dw_out")
    wmt = jnp.swapaxes(wm, 1, 2)
    dpa, dhc, dwm, db, dlg, dlb, dcb = _even_bwd1(proj, dmix, hc, wm, wmt, bcol, w["e_conv_ln_g"], w["e_conv_ln_b"],
                                                  name=f"{tag}_mixers_bwd1")
    dpb, dcw = _even_bwd2(proj, dhc, cw, name=f"{tag}_mixers_bwd2")
    grads["e_gmlp_w"] = jnp.where(_gmlp_mask()[None], dwm, 0.0)[None]
    grads["e_gmlp_b"] = db[:, :, 0][None]
    grads["e_conv_ln_g"], grads["e_conv_ln_b"], grads["e_conv_b"] = dlg, dlb, dcb
    grads["e_conv_w"] = dcw[:CONV_WIDTH][None]
    G = _grad_to_slab(G, "e_w_in", 0, hn, dpa, b_cols=512, chips=(0, 2), name=f"{tag}_dw_in_a")
    G = _grad_to_slab(G, "e_w_in", 0, hn, dpb, b_cols=512, chips=(2, 2), name=f"{tag}_dw_in_b")
    dx, dg = _norm_bwd_n((dpa, dpb), _shards(W, "e_w_in"), x, w["e_norm"][0], dy, name=f"{tag}_in_bwd")
    grads["e_norm"] = dg
    return dx, G


def _odd_block(x, W, w, tag):
    S = x.shape[0]
    hn, u = _norm_mm(x, w["o_norm"][0], _shards(W, "o_w_in"), split="k", out_dtype=F32, name=f"{tag}_w_in")
    disc_in = (w["o_lam_re"][0], w["o_lam_im"][0], w["o_log_dt"][0], w["o_b_re"][0], w["o_b_im"][0])
    (ar, ai, bbr, bbi), disc_vjp = jax.vjp(_s5_discretize, *disc_in)
    bd = jnp.concatenate([_block_diag(bbr, "gpc,gh->gchp").reshape(C_WIDTH, N_STATE),
                          _block_diag(bbi, "gpc,gh->gchp").reshape(C_WIDTH, N_STATE)], axis=1).astype(BF16)
    cd = jnp.concatenate([_block_diag(w["o_c_re"][0], "gcp,gh->gphc").reshape(N_STATE, C_WIDTH),
                          -_block_diag(w["o_c_im"][0], "gcp,gh->gphc").reshape(N_STATE, C_WIDTH)], axis=0).astype(BF16)
    powers, pr, pi = [], ar, ai
    for _ in range(SCAN_BLOCK):
        powers.append(jnp.concatenate([pr.reshape(STATE_ROWS, STATE_LANES), pi.reshape(STATE_ROWS, STATE_LANES)], 0))
        pr, pi = pr * ar - pi * ai, pr * ai + pi * ar
    pw = jnp.stack(powers, axis=0)
    state3 = (S, 2 * STATE_ROWS, STATE_LANES)
    bu = _mm_to_state(u, bd, name=f"{tag}_bu")
    xs = _scan_fwd(bu.reshape(state3), pw, name=f"{tag}_scan").reshape(bu.shape)
    yv, yg = _s5_readout(xs, cd, u, w["o_d"], name=f"{tag}_readout")
    o, y = _glu_out(yg, _shards(W, "o_w_out"), x, name=f"{tag}_glu_out")
    return y, (x, hn, u, bd, cd, pw, xs, yv, yg, o, disc_vjp)


def _odd_block_bwd(dy, saved, W, w, tag, G, grads):
    x, hn, u, bd, cd, pw, xs, yv, yg, o, disc_vjp = saved
    S = x.shape[0]
    state3 = (S, 2 * STATE_ROWS, STATE_LANES)
    do, dys, dus, dd = _glu_out_bwd(o, dy, _shards(W, "o_w_out"), yv, u, w["o_d"], name=f"{tag}_glu_out_bwd")
    G = _grad_to_slab(G, "o_w_out", 0, yg, do, b_cols=512, name=f"{tag}_dw_out")
    grads["o_d"] = dd
    dxs = _mm_to_state(dys, cd, nt=True, name=f"{tag}_dxs")
    dcd = _state_grad_tn(xs, dys, name=f"{tag}_dcd")
    gs, da = _scan_bwd(dxs.reshape(state3), xs.reshape(state3), pw, name=f"{tag}_scan_bwd")
    gs = gs.reshape(xs.shape)
    dbd = _state_grad_tn(u, gs, name=f"{tag}_dbd")
    du, dx, dg = _s5_in_bwd(gs, bd, dus, _shards(W, "o_w_in"), x, w["o_norm"][0], dy, name=f"{tag}_in_bwd")
    G = _grad_to_slab(G, "o_w_in", 0, hn, du, a_cols=256, name=f"{tag}_dw_in")
    grads["o_norm"] = dg
    eye = jnp.eye(C_GROUPS, dtype=F32)
    dcr = jnp.einsum("gphc,gh->gcp", dcd[:N_STATE].reshape(C_GROUPS, C_STATE, C_GROUPS, C_GROUP_CH), eye)
    dci = -jnp.einsum("gphc,gh->gcp", dcd[N_STATE:].reshape(C_GROUPS, C_STATE, C_GROUPS, C_GROUP_CH), eye)
    dbbr = jnp.einsum("gchp,gh->gpc", dbd[:, :N_STATE].reshape(C_GROUPS, C_GROUP_CH, C_GROUPS, C_STATE), eye)
    dbbi = jnp.einsum("gchp,gh->gpc", dbd[:, N_STATE:].reshape(C_GROUPS, C_GROUP_CH, C_GROUPS, C_STATE), eye)
    dar = da[:STATE_ROWS].reshape(C_GROUPS, C_STATE)
    dai = da[STATE_ROWS:].reshape(C_GROUPS, C_STATE)
    dlr, dli, dldt, dbr, dbi = disc_vjp((dar, dai, dbbr, dbbi))
    grads["o_lam_re"], grads["o_lam_im"], grads["o_log_dt"] = dlr[None], dli[None], dldt[None]
    grads["o_b_re"], grads["o_b_im"], grads["o_c_re"], grads["o_c_im"] = dbr[None], dbi[None], dcr[None], dci[None]
    return dx, G


def _behind(value, token):
    return value + token[0, 0].astype(value.dtype)


class _NoExchange:
    def __init__(self, W):
        self.W = W

    def first_weights(self, w):
        return self.W, w

    def after_even_mixer(self, x, W, w):
        return W, w

    def after_layer0(self, x, W):
        return W

    def after_layer1_backward(self, G):
        return None

    def after_ffn0_backward(self, G):
        return None


def _forward_backward(xs_, mems_, tgt, w, G, exchange):
    W, w = exchange.first_weights(w)
    x1, s_mix0 = _even_block(xs_, W, w, "l0")
    W, w = exchange.after_even_mixer(x1, W, w)
    x2, s_att0 = _attention_block(x1, mems_, W, w, 0, "l0")
    x3, s_ffn0 = _ffn_block(x2, W, w, 0, "l0")
    W = exchange.after_layer0(x3, W)
    x4, s_mix1 = _odd_block(x3, W, w, "l1")
    x5, s_att1 = _attention_block(x4, mems_, W, w, 1, "l1")
    x6, s_ffn1 = _ffn_block(x5, W, w, 1, "l1")
    dx, dfinal, loss_lanes = _loss_head(x6, w["final_norm"], tgt, name="loss_head")

    grads = {n: [None, None] for n in ("ca_norm", "ca_mem_norm", "ffn_norm")}
    grads["final_norm"] = dfinal[0]
    dx, G = _ffn_block_bwd(dx, s_ffn1, W, w, 1, "l1", G, grads)
    dx, G = _attention_block_bwd(dx, s_att1, mems_, W, w, 1, "l1", G, grads)
    dx, G = _odd_block_bwd(dx, s_mix1, W, w, "l1", G, grads)
    token = exchange.after_layer1_backward(G)
    dx, G = _ffn_block_bwd(dx, s_ffn0, W, w, 0, "l0", G, grads, token)
    token = exchange.after_ffn0_backward(G)
    dx, G = _attention_block_bwd(dx, s_att0, mems_, W, w, 0, "l0", G, grads, token)
    dx, G = _even_block_bwd(dx, s_mix0, W, w, "l0", G, grads)
    for n in list(grads):
        if isinstance(grads[n], list):
            grads[n] = jnp.stack(grads[n], axis=0)
        grads[n] = grads[n].reshape(w[n].shape)
    return loss_lanes, dx, G, grads


class _Exchange:
    def __init__(self, local, chip, core):
        self.bufs = {s: lax.dynamic_update_slice(lax.empty((N_CHIPS, rows, width), BF16),
                                                 _local_slab(local, s, BF16)[None], (chip, 0, 0))
                     for s, (width, rows) in _SLABS.items()}
        self.half = core.reshape(1).astype(jnp.int32)
        self.where = jnp.stack([chip, core]).astype(jnp.int32)
        self.flight = None
        self.layer1_reduce = None

    def _start(self, stage, after, tag):
        self.flight = _gather_ici_start([self.bufs[s] for s in stage], after, name=f"gather_{tag}_start")
        return self.flight[3]

    def _land(self, stage, after, tag):
        send_sems, recv_sems, bufs, _ = self.flight
        bufs = _gather_ici_wait(send_sems, recv_sems, bufs, after, name=f"gather_{tag}_wait")
        return dict(zip(stage, _gather_forward(bufs, name=f"gather_{tag}_forward")))

    def first_weights(self, w):
        W = dict(zip(_STAGES[0], _allgather_chips([self.bufs[s] for s in _STAGES[0]], name="gather_stage0")))
        landed = W[_STAGES[0][0]][0, :8, :STATE_LANES].astype(F32) + w["e_conv_w"].reshape(-1)[0]
        token = self._start(_STAGES[1], landed, "stage1")
        return W, {**w, "e_norm": _behind(w["e_norm"], token)}

    def after_even_mixer(self, x, W, w):
        W = {**W, **self._land(_STAGES[1], x, "stage1")}
        token = self._start(_STAGES[2], W[_STAGES[1][0]], "stage2")
        return W, {**w, "ca_norm": _behind(w["ca_norm"], token)}

    def after_layer0(self, x, W):
        return {**W, **self._land(_STAGES[2], x, "stage2")}

    def reduce_start(self, G, slabs, tag):
        gl = [G[s] for s in slabs]
        other = _pair_exchange(gl, name=f"grad_{tag}_pair_exchange")
        pairs = [_pair_sum(g, r, self.half, name=f"grad_pair_sum_{s}") for s, g, r in zip(slabs, gl, other)]
        send_sems, recv_sems, pairs, lands, token = _chip_exchange_start(pairs, name=f"grad_{tag}_chip_start")
        return (slabs, gl, other, send_sems, recv_sems, pairs, lands), token

    def reduce_finish(self, state, after, tag):
        slabs, gl, other, send_sems, recv_sems, pairs, lands = state
        slots = _chip_exchange_wait(send_sems, recv_sems, pairs, lands, after, name=f"grad_{tag}_chip_wait")
        halves = [_chip_sum(g, r, sl, self.where, name=f"grad_chip_sum_{s}")
                  for s, g, r, sl in zip(slabs, gl, other, slots)]
        return dict(zip(slabs, _pair_share(halves, name=f"grad_{tag}_pair_share")))

    def after_layer1_backward(self, G):
        self.layer1_reduce, token = self.reduce_start(G, _STAGES[2], "l1")
        return token

    def after_ffn0_backward(self, G):
        self.ffn0_reduce, token = self.reduce_start(G, _FFN0_SLABS, "ffn0")
        return token


def kernel(x, mem, e_norm, e_w_in, e_gmlp_w, e_gmlp_b, e_conv_w, e_conv_b, e_conv_ln_g, e_conv_ln_b, e_w_out, o_norm, o_w_in, o_lam_re, o_lam_im, o_log_dt, o_b_re, o_b_im, o_c_re, o_c_im, o_d, o_w_out, ca_norm, ca_mem_norm, ca_wq, ca_wk, ca_wv, ca_wo, ffn_norm, ffn_w_gate, ffn_w_up, ffn_w_down, final_norm, loss_target, m_e_norm, m_e_w_in, m_e_gmlp_w, m_e_gmlp_b, m_e_conv_w, m_e_conv_b, m_e_conv_ln_g, m_e_conv_ln_b, m_e_w_out, m_o_norm, m_o_w_in, m_o_lam_re, m_o_lam_im, m_o_log_dt, m_o_b_re, m_o_b_im, m_o_c_re, m_o_c_im, m_o_d, m_o_w_out, m_ca_norm, m_ca_mem_norm, m_ca_wq, m_ca_wk, m_ca_wv, m_ca_wo, m_ffn_norm, m_ffn_w_gate, m_ffn_w_up, m_ffn_w_down, m_final_norm, v_e_norm, v_e_w_in, v_e_gmlp_w, v_e_gmlp_b, v_e_conv_w, v_e_conv_b, v_e_conv_ln_g, v_e_conv_ln_b, v_e_w_out, v_o_norm, v_o_w_in, v_o_lam_re, v_o_lam_im, v_o_log_dt, v_o_b_re, v_o_b_im, v_o_c_re, v_o_c_im, v_o_d, v_o_w_out, v_ca_norm, v_ca_mem_norm, v_ca_wq, v_ca_wk, v_ca_wv, v_ca_wo, v_ffn_norm, v_ffn_w_gate, v_ffn_w_up, v_ffn_w_down, v_final_norm):
    args = dict(locals())
    local = {n: args[n] for n in _WEIGHTS}
    mom = {n: args["m_" + n] for n in _WEIGHTS}
    vel = {n: args["v_" + n] for n in _WEIGHTS}
    chip = 2 * lax.axis_index("x") + lax.axis_index("y")
    core = lax.axis_index("c")
    xs_, mems_, tgt = x[0], mem[0], loss_target[0]

    w = {n: local[n] for n in _REPLICATED}
    sm_slab, sm_spans = _pack_rows([local[n] for n, _ in _SMALL_SHARDED], SMALL_W, F32)
    sm_all = _allgather_small(sm_slab, name="gather_small_weights").reshape(N_DEV, -1, SMALL_W)
    for (n, ax), span in zip(_SMALL_SHARDED, sm_spans):
        shp = local[n].shape
        w[n] = jnp.concatenate([_unpack_rows(sm_all[2 * p], [span], [shp])[0] for p in range(N_CHIPS)], axis=ax)

    exchange = _Exchange(local, chip, core)
    G = {s: lax.empty((N_CHIPS, rows, width), F32) for s, (width, rows) in _SLABS.items()}
    loss_lanes, dx, G, grads = _forward_backward(xs_, mems_, tgt, w, G, exchange)

    gsum = exchange.reduce_finish(exchange.layer1_reduce, dx, "l1")
    gsum = {**gsum, **exchange.reduce_finish(exchange.ffn0_reduce, dx, "ffn0")}
    rest0_reduce, token = exchange.reduce_start(G, _REST0_SLABS, "rest0")

    grads = {**grads, _SMALL[0]: _behind(grads[_SMALL[0]], token)}
    gs_slab, gs_spans = _pack_rows([grads[n] for n in _SMALL], SMALL_W, F32)
    gs_all = _allgather_small(gs_slab, name="gather_small_grads").reshape(N_DEV, -1, SMALL_W)
    gs_sum = _sum_slots(gs_all, name="small_grad_sum")
    out_grads = dict(zip(_SMALL, _unpack_rows(gs_sum, gs_spans, [grads[n].shape for n in _SMALL])))
    for n, ax in _SMALL_SHARDED:
        width = local[n].shape[ax]
        out_grads[n] = lax.dynamic_slice_in_dim(out_grads[n], chip * width, width, axis=ax)

    delta, new_m, new_v = {}, {}, {}
    d_, m_, v_ = _adamw_small([_two_d(local[n]) for n in _SMALL], [_two_d(out_grads[n]) for n in _SMALL],
                              [_two_d(mom[n]) for n in _SMALL], [_two_d(vel[n]) for n in _SMALL], name="adamw_small")
    for n, dd, mm_, vv in zip(_SMALL, d_, m_, v_):
        shp = local[n].shape
        delta[n], new_m[n], new_v[n] = dd.reshape(shp), mm_.reshape(shp), vv.reshape(shp)
    def adamw_large(names):
        for n in names:
            shp = local[n].shape
            g_, d_, m_, v_ = _adamw_shard(_two_d(local[n]), [(gsum[s], r0) for s, r0 in _PLACE[n][1]],
                                          _two_d(mom[n]), _two_d(vel[n]), name=f"adamw_{n}")
            out_grads[n], delta[n], new_m[n], new_v[n] = (g_.reshape(shp), d_.reshape(shp), m_.reshape(shp),
                                                          v_.reshape(shp))

    ready = [n for n, (_, where) in _PLACE.items() if all(s in gsum for s, _ in where)]
    adamw_large(ready)
    done = jnp.concatenate([delta[n].reshape(-1)[:1] for n in ready + list(_SMALL[:1])])
    gsum = {**gsum, **exchange.reduce_finish(rest0_reduce, done, "rest0")}
    adamw_large([n for n in _PLACE if n not in ready])

    loss = lax.psum(loss_lanes[0, 0], ("x", "y", "c"))
    return (loss, dx[None], *[out_grads[n] for n in _WEIGHTS], *[delta[n] for n in _WEIGHTS],
            *[new_m[n] for n in _WEIGHTS], *[new_v[n] for n in _WEIGHTS])
```

```python
import functools
import math

import jax
import jax.numpy as jnp
from jax import lax
from jax.experimental import pallas as pl
from jax.experimental.pallas import tpu as pltpu

F32 = jnp.float32
BF16 = jnp.bfloat16
MESH = pl.DeviceIdType.MESH

EPS = 1e-6
D_MODEL = 1024
A_WIDTH = 512
A_GROUPS = 4
GMLP_BLOCK = 128
CHUNK = 64
B_WIDTH = 512
CONV_WIDTH = 31
CONV_HALO = 32
C_WIDTH = 512
C_GROUP_CH = 16
C_GROUPS = 32
C_STATE = 64
N_STATE = C_GROUPS * C_STATE
STATE_LANES = 128
STATE_ROWS = N_STATE // STATE_LANES
SCAN_BLOCK = 8
CA_HEADS = 4
CA_HEAD_DIM = 256
FFN_HIDDEN = 2816

ADAM_LR = 0.001
ADAM_B1 = 0.9
ADAM_B2 = 0.999
ADAM_EPS = 1e-08
ADAM_WD = 0.01
ADAM_STEP = 10

VMEM_LIMIT = 56 * 1024 * 1024
ACC_BYTES = 6 * 1024 * 1024
TN_VMEM_BYTES = 44 * 1024 * 1024
SMALL_W = 128
N_CHIPS = 4
N_DEV = 8

_SLABS = {"D0": (512, 1024), "E0": (1024, 256), "A0": (1024, 1024), "B0": (1024, 704), "C0": (704, 2048),
          "D1": (512, 768), "A1": (1024, 1024), "B1": (1024, 704), "C1": (704, 2048)}
_STAGES = (("D0", "E0"), ("A0", "B0", "C0"), ("D1", "A1", "B1", "C1"))
_FFN0_SLABS = ("B0", "C0")
_REST0_SLABS = ("D0", "E0", "A0")
_PLACE = {
    "e_w_in": (1024, (("D0", 0),)), "e_w_out": (256, (("E0", 0),)),
    "o_w_out": (512, (("D1", 0),)), "o_w_in": (256, (("D1", 512),)),
    "ca_wq": (256, (("A0", 0), ("A1", 0))), "ca_wk": (256, (("A0", 256), ("A1", 256))),
    "ca_wv": (256, (("A0", 512), ("A1", 512))), "ca_wo": (256, (("A0", 768), ("A1", 768))),
    "ffn_w_down": (704, (("B0", 0), ("B1", 0))),
    "ffn_w_gate": (1024, (("C0", 0), ("C1", 0))), "ffn_w_up": (1024, (("C0", 1024), ("C1", 1024))),
}


def _params(sem=None):
    return pltpu.CompilerParams(dimension_semantics=sem, vmem_limit_bytes=VMEM_LIMIT)


def _tile(n, pref, mult=128):
    if n <= pref:
        return n
    t = (pref // mult) * mult
    while t >= mult:
        if n % t == 0:
            return t
        t -= mult
    return n


def _blk(name, layer=0):
    rows, where = _PLACE[name]
    slab, r0 = where[layer]
    assert r0 % rows == 0
    return slab, rows, r0 // rows


def _shards(slabs, name, layer=0):
    slab, rows, b = _blk(name, layer)
    return [(slabs[slab], (None, rows, _SLABS[slab][0]), (p, b, 0)) for p in range(N_CHIPS)]


_GELU_C = 0.7978845608028654
_GELU_A = 0.044715


def _gelu(x):
    t = jnp.tanh(_GELU_C * (x + _GELU_A * (x * x * x)))
    return 0.5 * x * (1.0 + t), t


def _gelu_grad(x, t):
    return 0.5 * (1.0 + t) + 0.5 * x * (1.0 - t * t) * (_GELU_C * (1.0 + 3.0 * _GELU_A * x * x))


def _sigmoid(x):
    return 1.0 / (1.0 + jnp.exp(-x))


def _mean(x):
    return jnp.mean(x, axis=-1, keepdims=True)


def _dot(a, b):
    return jnp.dot(a, b, preferred_element_type=F32)


def _dot_nt(a, b):
    return lax.dot_general(a, b, (((1,), (1,)), ((), ())), preferred_element_type=F32)


def _dot_tn(a, b):
    return lax.dot_general(a, b, (((0,), (0,)), ((), ())), preferred_element_type=F32)


def _rms_tile(xv, gv):
    return (xv * lax.rsqrt(_mean(xv * xv) + EPS)) * gv


def _rms_bwd_tile(xv, gv, dyv):
    r = lax.rsqrt(_mean(xv * xv) + EPS)
    xh = xv * r
    dyg = dyv * gv
    return r * (dyg - xh * _mean(dyg * xh)), jnp.sum(dyv * xh, axis=0, keepdims=True)


def _cols(p, width):
    return slice(p * width, (p + 1) * width)


def _sum_k(a, ws, k):
    tot = None
    for p in range(N_CHIPS):
        y = _dot(a[:, _cols(p, k)], ws[p][...])
        tot = y if tot is None else tot + y
    return tot


def _cat_nt(a, ws):
    return jnp.concatenate([_dot_nt(a, ws[p][...]) for p in range(N_CHIPS)], axis=1)


def _rows_call(name, tm, rows, fulls, outs, accs, body, scratch=()):
    S = min(x.shape[-2] for x in rows if x.ndim != 4)
    nr, nf, no, na = len(rows), len(fulls), len(outs), len(accs)

    def kern(*refs):
        r, f = refs[:nr], refs[nr:nr + nf]
        o, a = refs[nr + nf:nr + nf + no], refs[nr + nf + no:nr + nf + no + na]
        if na:
            @pl.when(pl.program_id(0) == 0)
            def _():
                for ref in a:
                    ref[...] = jnp.zeros_like(ref)
        body(r, f, o, a, refs[nr + nf + no + na:])

    def whole(shape):
        nd = len(shape)
        return pl.BlockSpec(tuple(shape), lambda i: (0,) * nd)

    def row_spec(shape):
        if len(shape) == 4:
            return pl.BlockSpec((tm // 8,) + tuple(shape[1:]), lambda i: (i, 0, 0, 0))
        if len(shape) == 3:
            return pl.BlockSpec((shape[0], tm, shape[2]), lambda i: (0, i, 0))
        return pl.BlockSpec((tm, shape[1]), lambda i: (i, 0))

    def full_spec(x):
        if isinstance(x, tuple):
            _, bshape, bidx = x
            return pl.BlockSpec(bshape, lambda i: bidx, pipeline_mode=pl.Buffered(1))
        return whole(x.shape)

    def out_shape_of(o):
        if o[0] == "state":
            return (S // 8, 2 * STATE_ROWS, 8, STATE_LANES)
        return (S, o[0]) if len(o) == 2 else (o[0], S, o[1])

    out_shapes = [out_shape_of(o) for o in outs]
    res = pl.pallas_call(
        kern, name=name, grid=(S // tm,),
        in_specs=[row_spec(x.shape) for x in rows] + [full_spec(x) for x in fulls],
        out_specs=[row_spec(s) for s in out_shapes] + [whole(shp) for shp, _ in accs],
        out_shape=[jax.ShapeDtypeStruct(s, o[-1]) for s, o in zip(out_shapes, outs)]
        + [jax.ShapeDtypeStruct(tuple(shp), dt) for shp, dt in accs],
        scratch_shapes=list(scratch),
        compiler_params=_params(("arbitrary",) if na else ("parallel",)),
    )(*rows, *[x[0] if isinstance(x, tuple) else x for x in fulls])
    return res[:no], res[no:]


def _grad_to_slab(gslabs, wname, layer, a, b, *, a_cols=None, b_cols=None, chips=(0, N_CHIPS), name):
    slab, rows, bidx = _blk(wname, layer)
    width = _SLABS[slab][0]
    p0, n_p = chips
    assert p0 % n_p == 0
    S = a.shape[-2]

    def tile_bytes(x, ts):
        return ts * x.dtype.itemsize * (x.shape[2] * n_p if x.ndim == 3 else x.shape[1])

    acc_bytes = n_p * rows * (-(-width // 128) * 128) * 4
    ts = next(t for t in (2048, 1024, 512, 256, S) if S % t == 0
              and 2 * (tile_bytes(a, t) + tile_bytes(b, t) + acc_bytes) <= TN_VMEM_BYTES or t == S)

    def operand(x):
        if x.ndim == 3:
            return pl.BlockSpec((n_p, ts, x.shape[2]), lambda s: (p0 // n_p, s, 0))
        return pl.BlockSpec((ts, x.shape[1]), lambda s: (s, 0))

    def part(ref, cols, p):
        if len(ref.shape) == 3:
            return ref[p]
        return ref[...] if cols is None else ref[:, _cols(p, cols)]

    def body(a_ref, b_ref, slab_ref, o_ref):
        @pl.when(pl.program_id(0) == 0)
        def _():
            o_ref[...] = jnp.zeros_like(o_ref)

        for p in range(n_p):
            o_ref[p] += _dot_tn(part(a_ref, a_cols, p).astype(BF16), part(b_ref, b_cols, p).astype(BF16))

    g = gslabs[slab]
    out = pl.pallas_call(
        body, name=name, grid=(S // ts,),
        in_specs=[operand(a), operand(b), pl.BlockSpec(memory_space=pl.ANY)],
        out_specs=pl.BlockSpec((n_p, rows, width), lambda s: (p0 // n_p, bidx, 0)),
        out_shape=jax.ShapeDtypeStruct(g.shape, F32), input_output_aliases={2: 0},
        compiler_params=_params(("arbitrary",)),
    )(a, b, g)
    return {**gslabs, slab: out}


def _vec(g):
    return g.reshape(1, -1)


def _norm_mm(x, g, ws, *, split, out_dtype, name, tm=512):
    S, D = x.shape
    k, n = ws[0][1][1], ws[0][1][2]
    N = n if split == "k" else N_CHIPS * n

    def body(r, f, o, acc, s):
        xn = _rms_tile(r[0][...], f[0][...]).astype(BF16)
        o[0][...] = xn
        if split == "k":
            o[1][...] = _sum_k(xn, f[1:], k).astype(out_dtype)
        else:
            for p in range(N_CHIPS):
                o[1][:, _cols(p, n)] = _dot(xn, f[1 + p][...]).astype(out_dtype)

    (xn, y), _ = _rows_call(name, _tile(S, tm), [x], [_vec(g)] + ws, [(D, BF16), (N, out_dtype)], [], body)
    return xn, y


def _mm_k(a, ws, *, add=None, out_dtype=F32, name, tm=512):
    S = a.shape[-2]
    k, n = ws[0][1][1], ws[0][1][2]
    has_add = add is not None

    def body(r, f, o, acc, s):
        if a.ndim == 3:
            y = None
            for p in range(N_CHIPS):
                t = _dot(r[0][p].astype(BF16), f[p][...])
                y = t if y is None else y + t
        else:
            y = _sum_k(r[0][...].astype(BF16), f, k)
        if has_add:
            y = y + r[1][...]
        o[0][...] = y.astype(out_dtype)

    (y,), _ = _rows_call(name, _tile(S, tm), [a] + ([add] if has_add else []), ws, [(n, out_dtype)], [], body)
    return y


def _mm_k_t(terms, *, out_dtype=F32, name, tm=512):
    S = terms[0][0].shape[0]
    k = terms[0][1][0][1][1]

    def body(r, f, o, acc, s):
        y = None
        for t in range(len(terms)):
            yt = _cat_nt(r[t][...].astype(BF16), f[N_CHIPS * t:N_CHIPS * (t + 1)])
            y = yt if y is None else y + yt
        o[0][...] = y.astype(out_dtype)

    (y,), _ = _rows_call(name, _tile(S, tm), [a for a, _ in terms], [w for _, ws in terms for w in ws],
                         [(N_CHIPS * k, out_dtype)], [], body)
    return y


def _rms_fwd(x, g, *, name):
    def body(r, f, o, acc, s):
        o[0][...] = _rms_tile(r[0][...], f[0][...]).astype(BF16)

    (y,), _ = _rows_call(name, _tile(x.shape[0], 256, 8), [x], [_vec(g)], [(x.shape[1], BF16)], [], body)
    return y


def _rms_dg(x, g, dy, *, name):
    def body(r, f, o, acc, s):
        acc[0][...] += _rms_bwd_tile(r[0][...], f[0][...], r[1][...])[1]

    _, (dg,) = _rows_call(name, _tile(x.shape[0], 256, 8), [x, dy], [_vec(g)], [], [((1, x.shape[1]), F32)], body)
    return dg


def _ffn_up(x, g, wg, wu, *, name, tm=256):
    S, D = x.shape
    h = wg[0][1][2]

    def body(r, f, o, acc, s):
        xn = _rms_tile(r[0][...], f[0][...]).astype(BF16)
        o[0][...] = xn
        for p in range(N_CHIPS):
            gate = _dot(xn, f[1 + p][...])
            up = _dot(xn, f[1 + N_CHIPS + p][...])
            o[1][p] = gate.astype(BF16)
            o[2][p] = up.astype(BF16)
            o[3][p] = (gate * _sigmoid(gate) * up).astype(BF16)

    (xn, gate, up, hid), _ = _rows_call(name, _tile(S, tm), [x], [_vec(g)] + wg + wu,
                                        [(D, BF16), (N_CHIPS, h, BF16), (N_CHIPS, h, BF16), (N_CHIPS, h, BF16)], [],
                                        body)
    return xn, gate, up, hid


def _ffn_bwd_hidden(dy, wd, gate, up, token=None, *, name, tm=256):
    S = dy.shape[0]
    h = wd[0][1][1]

    def body(r, f, o, acc, s):
        dyv = r[0][...]
        if token is not None:
            dyv = dyv + jnp.sum(f[N_CHIPS][...])
        dyb = dyv.astype(BF16)
        for p in range(N_CHIPS):
            dh = _dot_nt(dyb, f[p][...])
            gv = r[1][p].astype(F32)
            sg = _sigmoid(gv)
            o[0][p] = (dh * r[2][p].astype(F32) * (sg * (1.0 + gv * (1.0 - sg)))).astype(BF16)
            o[1][p] = (dh * gv * sg).astype(BF16)

    (dg, du), _ = _rows_call(name, _tile(S, tm), [dy, gate, up], wd + ([] if token is None else [token]),
                             [(N_CHIPS, h, BF16), (N_CHIPS, h, BF16)], [], body)
    return dg, du


def _ffn_in_bwd(dg, du, wg, wu, x, g, dres, *, name, tm=256):
    S, D = x.shape

    def body(r, f, o, acc, s):
        tot = None
        for p in range(N_CHIPS):
            y = _dot_nt(r[0][p], f[1 + p][...]) + _dot_nt(r[1][p], f[1 + N_CHIPS + p][...])
            tot = y if tot is None else tot + y
        dx, dgn = _rms_bwd_tile(r[2][...], f[0][...], tot)
        o[0][...] = dx + r[3][...]
        acc[0][...] += dgn

    (dx,), (dgn,) = _rows_call(name, _tile(S, tm), [dg, du, x, dres], [_vec(g)] + wg + wu, [(D, F32)],
                               [((1, D), F32)], body)
    return dx, dgn


def _norm_bwd_k(da, ws, x, g, dres, *, name, tm=512):
    S, D = x.shape

    def body(r, f, o, acc, s):
        dx, dg = _rms_bwd_tile(r[1][...], f[0][...], _cat_nt(r[0][...].astype(BF16), f[1:]))
        o[0][...] = dx + r[2][...]
        acc[0][...] += dg

    (dx,), (dg,) = _rows_call(name, _tile(S, tm), [da, x, dres], [_vec(g)] + ws, [(D, F32)], [((1, D), F32)], body)
    return dx, dg


def _norm_bwd_n(das, ws, x, g, dres, *, name, tm=256):
    S, D = x.shape
    n = ws[0][1][2]

    def body(r, f, o, acc, s):
        tot = None
        for p in range(N_CHIPS):
            y = _dot_nt(r[p // 2][:, _cols(p % 2, n)], f[1 + p][...])
            tot = y if tot is None else tot + y
        dx, dg = _rms_bwd_tile(r[2][...], f[0][...], tot)
        o[0][...] = dx + r[3][...]
        acc[0][...] += dg

    (dx,), (dg,) = _rows_call(name, _tile(S, tm), list(das) + [x, dres], [_vec(g)] + ws, [(D, F32)], [((1, D), F32)],
                              body)
    return dx, dg


def _ln_stats(v):
    mu = _mean(v)
    xc = v - mu
    rstd = lax.rsqrt(_mean(xc * xc) + EPS)
    return xc * rstd, rstd


_SHIFTS = 8


def _fill_shifts(sh_ref, ext_ref, tm):
    sh_ref[0] = ext_ref[...]
    for s in range(1, _SHIFTS):
        sh_ref[s, 0:tm + CONV_HALO - _SHIFTS, :] = ext_ref[pl.ds(s, tm + CONV_HALO - _SHIFTS), :]


def _window(sh_ref, off, tm):
    return sh_ref[off % _SHIFTS, pl.ds(off - off % _SHIFTS, tm), :]


def _even_fwd(proj, wm, bcol, cw, cb, lg, lb, *, name):
    S = proj.shape[0]
    tm = _tile(S, 256)
    hb = tm // CONV_HALO
    nblk = tm // GMLP_BLOCK

    def body(p_ref, halo_ref, wm_ref, b_ref, cw_ref, cb_ref, lg_ref, lb_ref, mix_ref, hc_ref, hext_ref, hsh_ref):
        i = pl.program_id(0)
        gu, _ = _gelu(p_ref[:, 0:A_WIDTH])
        gv, _ = _gelu(p_ref[:, A_WIDTH:2 * A_WIDTH])
        vn, _ = _ln_stats(gv)
        vnb = vn.astype(BF16)
        for n in range(nblk):
            rows = slice(n * GMLP_BLOCK, (n + 1) * GMLP_BLOCK)
            for g in range(A_GROUPS):
                cols = slice(g * GMLP_BLOCK, (g + 1) * GMLP_BLOCK)
                sg = jnp.dot(wm_ref[g], vnb[rows, cols], preferred_element_type=F32) + b_ref[g]
                mix_ref[rows, cols] = (gu[rows, cols] * sg).astype(BF16)
        h = p_ref[:, 1024:1536] * _sigmoid(p_ref[:, 1536:2048])
        hh = halo_ref[:, 0:B_WIDTH] * _sigmoid(halo_ref[:, B_WIDTH:2 * B_WIDTH])
        hext_ref[0:CONV_HALO, :] = jnp.where(i > 0, hh, 0.0)
        hext_ref[CONV_HALO:CONV_HALO + tm, :] = h
        _fill_shifts(hsh_ref, hext_ref, tm)
        acc = jnp.zeros((tm, B_WIDTH), F32)
        for k in range(CONV_WIDTH):
            acc = acc + cw_ref[k:k + 1, :] * _window(hsh_ref, k + CONV_HALO - CONV_WIDTH + 1, tm)
        hc = acc + cb_ref[...]
        hc_ref[...] = hc
        hhat, _ = _ln_stats(hc)
        hl = hhat * lg_ref[...] + lb_ref[...]
        mix_ref[:, A_WIDTH:A_WIDTH + B_WIDTH] = (hl * _sigmoid(hl)).astype(BF16)

    vec = pl.BlockSpec((1, B_WIDTH), lambda i: (0, 0))
    return pl.pallas_call(
        body, name=name, grid=(S // tm,),
        in_specs=[
            pl.BlockSpec((tm, 2048), lambda i: (i, 0)),
            pl.BlockSpec((CONV_HALO, 1024), lambda i: (jnp.maximum(i * hb - 1, 0), 1)),
            pl.BlockSpec((A_GROUPS, GMLP_BLOCK, GMLP_BLOCK), lambda i: (0, 0, 0)),
            pl.BlockSpec((A_GROUPS, GMLP_BLOCK, 1), lambda i: (0, 0, 0)),
            pl.BlockSpec((CONV_HALO, B_WIDTH), lambda i: (0, 0)),
            vec, vec, vec,
        ],
        out_specs=[pl.BlockSpec((tm, 1024), lambda i: (i, 0)), pl.BlockSpec((tm, B_WIDTH), lambda i: (i, 0))],
        out_shape=[jax.ShapeDtypeStruct((S, 1024), BF16), jax.ShapeDtypeStruct((S, B_WIDTH), F32)],
        scratch_shapes=[pltpu.VMEM((tm + CONV_HALO, B_WIDTH), F32),
                        pltpu.VMEM((_SHIFTS, tm + CONV_HALO, B_WIDTH), F32)],
        compiler_params=_params(("parallel",)),
    )(proj, proj, wm, bcol, cw, cb, lg, lb)


def _even_bwd1(proj, dmix, hc, wm, wmt, bcol, lg, lb, *, name):
    S = proj.shape[0]
    tm = _tile(S, 256)
    nblk = tm // GMLP_BLOCK

    def body(p_ref, dm_ref, hc_ref, wm_ref, wmt_ref, b_ref, lg_ref, lb_ref,
             dpa_ref, dhc_ref, dwm_ref, db_ref, dlg_ref, dlb_ref, dcb_ref, dgu_ref, dvn_ref):
        @pl.when(pl.program_id(0) == 0)
        def _():
            dwm_ref[...] = jnp.zeros_like(dwm_ref)
            db_ref[...] = jnp.zeros_like(db_ref)
            dlg_ref[...] = jnp.zeros_like(dlg_ref)
            dlb_ref[...] = jnp.zeros_like(dlb_ref)
            dcb_ref[...] = jnp.zeros_like(dcb_ref)

        au = p_ref[:, 0:A_WIDTH]
        av = p_ref[:, A_WIDTH:2 * A_WIDTH]
        gu, tu = _gelu(au)
        gv, tv = _gelu(av)
        vn, rstd = _ln_stats(gv)
        vnb = vn.astype(BF16)
        for n in range(nblk):
            rows = slice(n * GMLP_BLOCK, (n + 1) * GMLP_BLOCK)
            for g in range(A_GROUPS):
                cols = slice(g * GMLP_BLOCK, (g + 1) * GMLP_BLOCK)
                vb = vnb[rows, cols]
                sg = jnp.dot(wm_ref[g], vb, preferred_element_type=F32) + b_ref[g]
                da = dm_ref[rows, cols]
                dsg = da * gu[rows, cols]
                dgu_ref[rows, cols] = da * sg
                dsgb = dsg.astype(BF16)
                dwm_ref[g] += _dot_nt(dsgb, vb)
                db_ref[g] += jnp.sum(dsg, axis=1, keepdims=True)
                dvn_ref[rows, cols] = jnp.dot(wmt_ref[g], dsgb, preferred_element_type=F32)
        dvn = dvn_ref[...]
        dgv = rstd * (dvn - _mean(dvn) - vn * _mean(dvn * vn))
        dpa_ref[:, 0:A_WIDTH] = (dgu_ref[...] * _gelu_grad(au, tu)).astype(BF16)
        dpa_ref[:, A_WIDTH:2 * A_WIDTH] = (dgv * _gelu_grad(av, tv)).astype(BF16)
        hhat, rstd2 = _ln_stats(hc_ref[...])
        lgv = lg_ref[...]
        hl = hhat * lgv + lb_ref[...]
        s = _sigmoid(hl)
        dhl = dm_ref[:, A_WIDTH:A_WIDTH + B_WIDTH] * (s * (1.0 + hl * (1.0 - s)))
        dlg_ref[...] += jnp.sum(dhl * hhat, axis=0, keepdims=True)
        dlb_ref[...] += jnp.sum(dhl, axis=0, keepdims=True)
        dhh = dhl * lgv
        dhc = rstd2 * (dhh - _mean(dhh) - hhat * _mean(dhh * hhat))
        dcb_ref[...] += jnp.sum(dhc, axis=0, keepdims=True)
        dhc_ref[...] = dhc

    vec = pl.BlockSpec((1, B_WIDTH), lambda i: (0, 0))
    w3 = pl.BlockSpec((A_GROUPS, GMLP_BLOCK, GMLP_BLOCK), lambda i: (0, 0, 0))
    b3 = pl.BlockSpec((A_GROUPS, GMLP_BLOCK, 1), lambda i: (0, 0, 0))
    return pl.pallas_call(
        body, name=name, grid=(S // tm,),
        in_specs=[
            pl.BlockSpec((tm, 1024), lambda i: (i, 0)),
            pl.BlockSpec((tm, 1024), lambda i: (i, 0)),
            pl.BlockSpec((tm, B_WIDTH), lambda i: (i, 0)),
            w3, w3, b3, vec, vec,
        ],
        out_specs=[pl.BlockSpec((tm, 1024), lambda i: (i, 0)), pl.BlockSpec((tm, B_WIDTH), lambda i: (i, 0)),
                   w3, b3, vec, vec, vec],
        out_shape=[
            jax.ShapeDtypeStruct((S, 1024), BF16), jax.ShapeDtypeStruct((S, B_WIDTH), F32),
            jax.ShapeDtypeStruct((A_GROUPS, GMLP_BLOCK, GMLP_BLOCK), F32),
            jax.ShapeDtypeStruct((A_GROUPS, GMLP_BLOCK, 1), F32),
            jax.ShapeDtypeStruct((1, B_WIDTH), F32), jax.ShapeDtypeStruct((1, B_WIDTH), F32),
            jax.ShapeDtypeStruct((1, B_WIDTH), F32),
        ],
        scratch_shapes=[pltpu.VMEM((tm, A_WIDTH), F32), pltpu.VMEM((tm, A_WIDTH), F32)],
        compiler_params=_params(("arbitrary",)),
    )(proj, dmix, hc, wm, wmt, bcol, lg, lb)


def _even_bwd2(proj, dhc, cw, *, name):
    S = proj.shape[0]
    tm = _tile(S, 256)
    hb = tm // CONV_HALO
    nt = S // tm
    last_halo = S // CONV_HALO - 1
    lo = CONV_HALO - CONV_WIDTH + 1

    def body(p_ref, halo_ref, d_ref, dnext_ref, cw_ref, dpb_ref, dcw_ref, hext_ref, dext_ref, hsh_ref, dsh_ref):
        i = pl.program_id(0)

        @pl.when(i == 0)
        def _():
            dcw_ref[...] = jnp.zeros_like(dcw_ref)

        ba = p_ref[:, 0:B_WIDTH]
        sg = _sigmoid(p_ref[:, B_WIDTH:2 * B_WIDTH])
        hh = halo_ref[:, 0:B_WIDTH] * _sigmoid(halo_ref[:, B_WIDTH:2 * B_WIDTH])
        hext_ref[0:CONV_HALO, :] = jnp.where(i > 0, hh, 0.0)
        hext_ref[CONV_HALO:CONV_HALO + tm, :] = ba * sg
        dhc_t = d_ref[...]
        dext_ref[0:tm, :] = dhc_t
        dext_ref[tm:tm + CONV_HALO, :] = jnp.where(i < nt - 1, dnext_ref[...], 0.0)
        _fill_shifts(hsh_ref, hext_ref, tm)
        _fill_shifts(dsh_ref, dext_ref, tm)
        dh = jnp.zeros((tm, B_WIDTH), F32)
        for k in range(CONV_WIDTH):
            dh = dh + cw_ref[k:k + 1, :] * _window(dsh_ref, CONV_WIDTH - 1 - k, tm)
            dcw_ref[k:k + 1, :] += jnp.sum(dhc_t * _window(hsh_ref, k + lo, tm), axis=0, keepdims=True)
        dpb_ref[:, 0:B_WIDTH] = (dh * sg).astype(BF16)
        dpb_ref[:, B_WIDTH:2 * B_WIDTH] = (dh * ba * sg * (1.0 - sg)).astype(BF16)

    return pl.pallas_call(
        body, name=name, grid=(nt,),
        in_specs=[
            pl.BlockSpec((tm, 1024), lambda i: (i, 1)),
            pl.BlockSpec((CONV_HALO, 1024), lambda i: (jnp.maximum(i * hb - 1, 0), 1)),
            pl.BlockSpec((tm, B_WIDTH), lambda i: (i, 0)),
            pl.BlockSpec((CONV_HALO, B_WIDTH), lambda i: (jnp.minimum((i + 1) * hb, last_halo), 0)),
            pl.BlockSpec((CONV_HALO, B_WIDTH), lambda i: (0, 0)),
        ],
        out_specs=[pl.BlockSpec((tm, 1024), lambda i: (i, 0)), pl.BlockSpec((CONV_HALO, B_WIDTH), lambda i: (0, 0))],
        out_shape=[jax.ShapeDtypeStruct((S, 1024), BF16), jax.ShapeDtypeStruct((CONV_HALO, B_WIDTH), F32)],
        scratch_shapes=[pltpu.VMEM((tm + CONV_HALO, B_WIDTH), F32), pltpu.VMEM((tm + CONV_HALO, B_WIDTH), F32),
                        pltpu.VMEM((_SHIFTS, tm + CONV_HALO, B_WIDTH), F32),
                        pltpu.VMEM((_SHIFTS, tm + CONV_HALO, B_WIDTH), F32)],
        compiler_params=_params(("arbitrary",)),
    )(proj, proj, dhc, dhc, cw)


_CA_SCALE = CA_HEAD_DIM ** -0.5


def _softmax_rows(s):
    e = jnp.exp(s - jnp.max(s, axis=-1, keepdims=True))
    return e / jnp.sum(e, axis=-1, keepdims=True)


def _attn_fwd(q, k, v, *, name):
    S = q.shape[0]

    def body(r, f, o, acc, s):
        for h in range(CA_HEADS):
            cols = _cols(h, CA_HEAD_DIM)
            p = _softmax_rows(_dot_nt(r[0][:, cols], f[0][:, cols]) * _CA_SCALE)
            o[0][:, cols] = _dot(p.astype(BF16), f[1][:, cols]).astype(BF16)

    (o_,), _ = _rows_call(name, _tile(S, 512), [q], [k, v], [(D_MODEL, BF16)], [], body)
    return o_


def _attn_bwd(dy, wo, q, k, v, *, name):
    S = q.shape[0]
    M = k.shape[0]

    def body(r, f, o, acc, s):
        dyb = r[0][...].astype(BF16)
        for h in range(CA_HEADS):
            cols = _cols(h, CA_HEAD_DIM)
            qh = r[1][:, cols]
            kh = f[0][:, cols]
            vh = f[1][:, cols]
            doh = _dot_nt(dyb, f[2 + h][...]).astype(BF16)
            p = _softmax_rows(_dot_nt(qh, kh) * _CA_SCALE)
            acc[1][:, cols] += _dot_tn(p.astype(BF16), doh)
            dp = _dot_nt(doh, vh)
            ds = (p * (dp - jnp.sum(dp * p, axis=-1, keepdims=True)) * _CA_SCALE).astype(BF16)
            o[0][:, cols] = _dot(ds, kh).astype(BF16)
            acc[0][:, cols] += _dot_tn(ds, qh)

    (dq,), (dk, dv) = _rows_call(name, _tile(S, 512), [dy, q], [k, v] + wo, [(D_MODEL, BF16)],
                                 [((M, D_MODEL), F32), ((M, D_MODEL), F32)], body)
    return dq, dk, dv


_STATE_TILE = 2 * STATE_ROWS


def _state_cols(ref, tm):
    return jnp.concatenate([ref[:, c, :, :].reshape(tm, STATE_LANES).astype(BF16) for c in range(_STATE_TILE)],
                           axis=1)


def _put_state_cols(ref, y, tm):
    for c in range(_STATE_TILE):
        ref[:, c, :, :] = y[:, _cols(c, STATE_LANES)].reshape(tm // 8, 8, STATE_LANES)


def _mm_to_state(a, w, *, nt=False, name, tm=256):
    S = a.shape[0]
    tm = _tile(S, tm)

    def body(r, f, o, acc, s):
        av = r[0][...].astype(BF16)
        _put_state_cols(o[0], _dot_nt(av, f[0][...]) if nt else _dot(av, f[0][...]), tm)

    (y,), _ = _rows_call(name, tm, [a], [w], [("state", F32)], [], body)
    return y


def _s5_readout(xs, cd, u, d, *, name, tm=256):
    tm = _tile(u.shape[0], tm)

    def body(r, f, o, acc, s):
        y = _dot(_state_cols(r[0], tm), f[0][...]) + f[1][...] * r[1][...]
        o[0][...] = y
        o[1][...] = _gelu(y)[0].astype(BF16)

    (y, yg), _ = _rows_call(name, tm, [xs, u], [cd, d], [(C_WIDTH, F32), (C_WIDTH, BF16)], [], body)
    return y, yg


def _state_grad_tn(a, b, *, name, ts=256):
    a_state, b_state = a.ndim == 4, b.ndim == 4
    S = b.shape[0] if a_state else a.shape[0]
    ts = _tile(S, ts)
    K1 = 2 * N_STATE if a_state else a.shape[1]
    N = 2 * N_STATE if b_state else b.shape[1]

    def body(r, f, o, acc, s):
        av = _state_cols(r[0], ts) if a_state else r[0][...].astype(BF16)
        bv = _state_cols(r[1], ts) if b_state else r[1][...].astype(BF16)
        acc[0][...] += _dot_tn(av, bv)

    _, (out,) = _rows_call(name, ts, [a, b], [], [], [((K1, N), F32)], body)
    return out


def _glu_out(yg, ws, x, *, name, tm=512):
    n = ws[0][1][2]

    def body(r, f, o, acc, s):
        ygv = r[0][...]
        ov = [_dot(ygv, f[p][...]) for p in range(N_CHIPS)]
        for p in range(N_CHIPS):
            o[0][:, _cols(p, n)] = ov[p].astype(BF16)
        for p in range(2):
            o[1][:, _cols(p, n)] = r[1][:, _cols(p, n)] + ov[p] * _sigmoid(ov[2 + p])

    (o_, y), _ = _rows_call(name, _tile(x.shape[0], tm), [yg, x], ws, [(2 * D_MODEL, BF16), (D_MODEL, F32)], [], body)
    return o_, y


def _glu_out_bwd(o_, dy, ws, y, u, d, *, name, tm=256):
    n = ws[0][1][2]

    def body(r, f, o, acc, s):
        o1 = r[0][:, 0:D_MODEL].astype(F32)
        sg = _sigmoid(r[0][:, D_MODEL:2 * D_MODEL].astype(F32))
        dyv = r[1][...]
        do1 = (dyv * sg).astype(BF16)
        do2 = (dyv * o1 * sg * (1.0 - sg)).astype(BF16)
        o[0][:, 0:D_MODEL] = do1
        o[0][:, D_MODEL:2 * D_MODEL] = do2
        dyg = None
        for p in range(N_CHIPS):
            t = _dot_nt((do1 if p < 2 else do2)[:, _cols(p % 2, n)], f[1 + p][...])
            dyg = t if dyg is None else dyg + t
        yv = r[2][...]
        dys = dyg * _gelu_grad(yv, _gelu(yv)[1])
        o[1][...] = dys.astype(BF16)
        o[2][...] = f[0][...] * dys
        acc[0][...] += jnp.sum(dys * r[3][...], axis=0, keepdims=True)

    (do, dys, dus), (dd,) = _rows_call(name, _tile(dy.shape[0], tm), [o_, dy, y, u], [d] + ws,
                                       [(2 * D_MODEL, BF16), (C_WIDTH, BF16), (C_WIDTH, F32)], [((1, C_WIDTH), F32)],
                                       body)
    return do, dys, dus, dd


def _s5_in_bwd(gs, bd, dus, ws, x, g, dres, *, name, tm=256):
    D = x.shape[1]
    tm = _tile(x.shape[0], tm)

    def body(r, f, o, acc, s):
        du = (_dot_nt(_state_cols(r[0], tm), f[1][...]) + r[1][...]).astype(BF16)
        o[0][...] = du
        dx, dg = _rms_bwd_tile(r[2][...], f[0][...], _cat_nt(du, f[2:]))
        o[1][...] = dx + r[3][...]
        acc[0][...] += dg

    (du, dx), (dg,) = _rows_call(name, tm, [gs, dus, x, dres], [_vec(g), bd] + ws,
                                 [(C_WIDTH, BF16), (D, F32)], [((1, D), F32)], body)
    return du, dx, dg


_SCAN_CHUNK = 128
_RE = slice(0, STATE_ROWS)
_IM = slice(STATE_ROWS, 2 * STATE_ROWS)
assert SCAN_BLOCK == 8


def _token(g, i):
    return pl.ds(pl.multiple_of(g * (_STATE_TILE * SCAN_BLOCK), _STATE_TILE * SCAN_BLOCK) + i, _STATE_TILE,
                 stride=SCAN_BLOCK)


def _scan_fwd(bu, pw, *, name):
    S = bu.shape[0] // _STATE_TILE
    tc = _tile(S, _SCAN_CHUNK, 8)

    def body(bu_ref, pw_ref, xs_ref, st_ref):
        @pl.when(pl.program_id(0) == 0)
        def _():
            st_ref[...] = jnp.zeros_like(st_ref)

        ar = pw_ref[0, _RE, :]
        ai = pw_ref[0, _IM, :]

        def block(g, carry):
            xr, xi = carry
            cr = ci = nr = ni = None
            for j in range(SCAN_BLOCK):
                b = bu_ref[_token(g, j), :]
                br, bi = b[_RE], b[_IM]
                cr, ci = (br, bi) if j == 0 else (ar * cr - ai * ci + br, ar * ci + ai * cr + bi)
                pr, pi = pw_ref[j, _RE, :], pw_ref[j, _IM, :]
                nr = pr * xr - pi * xi + cr
                ni = pr * xi + pi * xr + ci
                xs_ref[_token(g, j), :] = jnp.concatenate([nr, ni], axis=0)
            return nr, ni

        xr, xi = lax.fori_loop(0, tc // SCAN_BLOCK, block, (st_ref[_RE, :], st_ref[_IM, :]), unroll=2)
        st_ref[_RE, :] = xr
        st_ref[_IM, :] = xi

    blk = pl.BlockSpec((tc * _STATE_TILE, STATE_LANES), lambda i: (i, 0))
    return pl.pallas_call(
        body, name=name, grid=(S // tc,),
        in_specs=[blk, pl.BlockSpec(pw.shape, lambda i: (0, 0, 0))], out_specs=blk,
        out_shape=jax.ShapeDtypeStruct(bu.shape, F32),
        scratch_shapes=[pltpu.VMEM((2 * STATE_ROWS, STATE_LANES), F32)],
        compiler_params=_params(("arbitrary",)),
    )(bu, pw)


def _scan_bwd(dxs, xs, pw, *, name):
    S = dxs.shape[0] // _STATE_TILE
    tc = _tile(S, _SCAN_CHUNK, 8)
    nc = S // tc

    def body(dx_ref, xs_ref, pw_ref, g_ref, da_ref, st_ref):
        @pl.when(pl.program_id(0) == 0)
        def _():
            st_ref[...] = jnp.zeros_like(st_ref)
            da_ref[...] = jnp.zeros_like(da_ref)

        ar = pw_ref[0, _RE, :]
        ai = pw_ref[0, _IM, :]

        def block(i, carry):
            gr, gi, dar, dai = carry
            g = tc // SCAN_BLOCK - 1 - i
            cr = ci = None
            pgr, pgi = gr, gi
            for j in range(SCAN_BLOCK):
                tok = _token(g, SCAN_BLOCK - 1 - j)
                x = xs_ref[tok, :]
                xr, xi = x[_RE], x[_IM]
                dar = dar + pgr * xr + pgi * xi
                dai = dai + pgi * xr - pgr * xi
                d = dx_ref[tok, :]
                dr, di = d[_RE], d[_IM]
                cr, ci = (dr, di) if j == 0 else (ar * cr + ai * ci + dr, ar * ci - ai * cr + di)
                pr, pi = pw_ref[j, _RE, :], pw_ref[j, _IM, :]
                pgr = pr * gr + pi * gi + cr
                pgi = pr * gi - pi * gr + ci
                g_ref[tok, :] = jnp.concatenate([pgr, pgi], axis=0)
            return pgr, pgi, dar, dai

        init = (st_ref[_RE, :], st_ref[_IM, :], da_ref[_RE, :], da_ref[_IM, :])
        gr, gi, dar, dai = lax.fori_loop(0, tc // SCAN_BLOCK, block, init, unroll=2)
        st_ref[_RE, :] = gr
        st_ref[_IM, :] = gi
        da_ref[_RE, :] = dar
        da_ref[_IM, :] = dai

    blk = pl.BlockSpec((tc * _STATE_TILE, STATE_LANES), lambda i: (nc - 1 - i, 0))
    vec = pl.BlockSpec((2 * STATE_ROWS, STATE_LANES), lambda i: (0, 0))
    return pl.pallas_call(
        body, name=name, grid=(nc,), in_specs=[blk, blk, pl.BlockSpec(pw.shape, lambda i: (0, 0, 0))],
        out_specs=[blk, vec],
        out_shape=[jax.ShapeDtypeStruct(dxs.shape, F32), jax.ShapeDtypeStruct((2 * STATE_ROWS, STATE_LANES), F32)],
        scratch_shapes=[pltpu.VMEM((2 * STATE_ROWS, STATE_LANES), F32)],
        compiler_params=_params(("arbitrary",)),
    )(dxs, xs, pw)


def _loss_head(x, g, target, *, name):
    S, D = x.shape

    def body(r, f, o, acc, s):
        xv = r[0][...]
        gv = f[0][...]
        rs = lax.rsqrt(_mean(xv * xv) + EPS)
        xh = xv * rs
        err = xh * gv - r[1][...]
        acc[1][...] += 0.5 * jnp.sum(_mean(err * err), axis=0, keepdims=True)
        dy = err * (1.0 / D)
        dyg = dy * gv
        o[0][...] = rs * (dyg - xh * _mean(dyg * xh))
        acc[0][...] += jnp.sum(dy * xh, axis=0, keepdims=True)

    (dx,), (dg, loss) = _rows_call(name, _tile(S, 256, 8), [x, target], [_vec(g)], [(D, F32)],
                                   [((1, D), F32), ((1, 128), F32)], body)
    return dx, dg, loss


_ADAM_C1 = 1.0 - ADAM_B1 ** ADAM_STEP
_ADAM_C2 = 1.0 - ADAM_B2 ** ADAM_STEP
_ONE_BLOCK_BYTES = 8 * 1024 * 1024


def _adamw_math(w, g, m, v):
    nm = ADAM_B1 * m + (1.0 - ADAM_B1) * g
    nv = ADAM_B2 * v + (1.0 - ADAM_B2) * (g * g)
    m_hat = nm / _ADAM_C1
    v_hat = nv / _ADAM_C2
    return -ADAM_LR * (m_hat / (jnp.sqrt(v_hat) + ADAM_EPS) + ADAM_WD * w), nm, nv


def _adamw_shard(w, gsrc, m, v, *, name):
    R, C = w.shape
    n_l = len(gsrc)
    rows = R // n_l
    tr = rows
    for _, r0 in gsrc:
        tr = math.gcd(tr, r0) if r0 else tr
    tr = _tile(tr, 256, 8) if tr > 256 else tr
    nb = rows // tr
    assert rows % tr == 0 and all(r0 % tr == 0 for _, r0 in gsrc)

    def body(*refs):
        w_ref, g_refs, (m_ref, v_ref, go_ref, d_ref, nm_ref, nv_ref) = refs[0], refs[1:1 + n_l], refs[1 + n_l:]
        layer = pl.program_id(0) // nb
        gv = g_refs[0][...]
        for l in range(1, n_l):
            gv = jnp.where(layer == l, g_refs[l][...], gv)
        go_ref[...] = gv
        d_ref[...], nm_ref[...], nv_ref[...] = _adamw_math(w_ref[...], gv, m_ref[...], v_ref[...])

    def g_spec(l, r0):
        return pl.BlockSpec((tr, C), lambda i: (r0 // tr + jnp.clip(i - l * nb, 0, nb - 1), 0))

    blk = pl.BlockSpec((tr, C), lambda i: (i, 0))
    out = jax.ShapeDtypeStruct((R, C), F32)
    return pl.pallas_call(
        body, name=name, grid=(R // tr,),
        in_specs=[blk] + [g_spec(l, r0) for l, (_, r0) in enumerate(gsrc)] + [blk, blk], out_specs=[blk] * 4,
        out_shape=[out] * 4, compiler_params=_params(("parallel",)),
    )(w, *[g for g, _ in gsrc], m, v)


def _adamw_small(ws, gs, ms, vs, *, name):
    n = len(ws)

    def body(*refs):
        w_r, g_r, m_r, v_r = refs[:n], refs[n:2 * n], refs[2 * n:3 * n], refs[3 * n:4 * n]
        d_r, nm_r, nv_r = refs[4 * n:5 * n], refs[5 * n:6 * n], refs[6 * n:7 * n]
        for k in range(n):
            d_r[k][...], nm_r[k][...], nv_r[k][...] = _adamw_math(w_r[k][...], g_r[k][...], m_r[k][...], v_r[k][...])

    vm = pl.BlockSpec(memory_space=pltpu.VMEM)
    out = [jax.ShapeDtypeStruct(w.shape, F32) for w in ws]
    res = pl.pallas_call(body, name=name, in_specs=[vm] * (4 * n), out_specs=[vm] * (3 * n), out_shape=out * 3,
                         compiler_params=pltpu.CompilerParams(vmem_limit_bytes=VMEM_LIMIT))(*ws, *gs, *ms, *vs)
    return res[:n], res[n:2 * n], res[2 * n:]


def _sum_slots(x, *, name):
    n, R, C = x.shape
    tr = R if (n + 1) * R * C * 4 <= _ONE_BLOCK_BYTES else _tile(R, 256, 8)

    def body(x_ref, o_ref):
        acc = x_ref[0]
        for k in range(1, n):
            acc = acc + x_ref[k]
        o_ref[...] = acc

    return pl.pallas_call(
        body, name=name, grid=(R // tr,),
        in_specs=[pl.BlockSpec((n, tr, C), lambda i: (0, i, 0))], out_specs=pl.BlockSpec((tr, C), lambda i: (i, 0)),
        out_shape=jax.ShapeDtypeStruct((R, C), F32), compiler_params=_params(("parallel",)),
    )(x)


def _pair_sum(g, r, half, *, name):
    n, R, C = g.shape
    Rh = R // 2
    tr = _tile(Rh, 256, 8)
    nb = Rh // tr

    def body(half_ref, g_ref, r_ref, o_ref):
        o_ref[...] = (g_ref[...] + r_ref[...]).astype(BF16)

    return pl.pallas_call(
        body, name=name,
        grid_spec=pltpu.PrefetchScalarGridSpec(
            num_scalar_prefetch=1, grid=(n, nb),
            in_specs=[pl.BlockSpec((1, tr, C), lambda p, i, h: (p, h[0] * nb + i, 0)),
                      pl.BlockSpec((1, tr, C), lambda p, i, h: (p, i, 0))],
            out_specs=pl.BlockSpec((1, tr, C), lambda p, i, h: (p, i, 0)),
        ),
        out_shape=jax.ShapeDtypeStruct((n, Rh, C), BF16), compiler_params=_params(("parallel", "parallel")),
    )(half, g, r)


def _chip_sum(g, r, slots, where, *, name):
    n, R, C = g.shape
    Rh = R // 2
    tr = _tile(Rh, 256, 8)
    nb = Rh // tr

    def body(w_ref, g_ref, r_ref, s_ref, o_ref):
        acc = g_ref[0] + r_ref[0]
        for k in range(slots.shape[0]):
            acc = acc + s_ref[k].astype(F32)
        o_ref[...] = acc

    return pl.pallas_call(
        body, name=name,
        grid_spec=pltpu.PrefetchScalarGridSpec(
            num_scalar_prefetch=1, grid=(nb,),
            in_specs=[pl.BlockSpec((1, tr, C), lambda i, w: (w[0], w[1] * nb + i, 0)),
                      pl.BlockSpec((1, tr, C), lambda i, w: (w[0], i, 0)),
                      pl.BlockSpec((slots.shape[0], tr, C), lambda i, w: (0, i, 0))],
            out_specs=pl.BlockSpec((tr, C), lambda i, w: (w[1] * nb + i, 0)),
        ),
        out_shape=jax.ShapeDtypeStruct((R, C), F32), compiler_params=_params(("parallel",)),
    )(where, g, r, slots)


ANY = pl.BlockSpec(memory_space=pl.ANY)


def _place():
    return lax.axis_index("x"), lax.axis_index("y"), lax.axis_index("c")


def _other_chips(x, y):
    return [(1 - x, y), (x, 1 - y), (1 - x, 1 - y)]


def _allgather_small(v, *, name):
    R, C = v.shape

    def body(x_ref, out_ref, send_sems, recv_sems, local_sem):
        x, y, c = _place()
        me, sibling = (x, y, c), (x, y, 1 - c)
        chips = _other_chips(x, y)

        def rows(px, py, pc):
            return out_ref.at[pl.ds((4 * px + 2 * py + pc) * R, R), :]

        def copy(k, block, to, src=None):
            return pltpu.make_async_remote_copy(
                src_ref=rows(*block) if src is None else src, dst_ref=rows(*block),
                send_sem=send_sems.at[k], recv_sem=recv_sems.at[k], device_id=to, device_id_type=MESH)

        mine = pltpu.make_async_copy(x_ref, rows(*me), local_sem)
        mine.start()
        first = [copy(0, me, sibling, src=x_ref)]
        first += [copy(1 + j, me, (*chip, c), src=x_ref) for j, chip in enumerate(chips)]
        for cp in first:
            cp.start()
        passed = [copy(4 + j, (*chip, c), sibling) for j, chip in enumerate(chips)]
        for j, chip in enumerate(chips):
            copy(1 + j, (*chip, c), me).wait_recv()
            passed[j].start()
        copy(0, sibling, me).wait_recv()
        for j, chip in enumerate(chips):
            copy(4 + j, (*chip, 1 - c), me).wait_recv()
        for cp in first + passed:
            cp.wait_send()
        mine.wait()

    return pl.pallas_call(
        body, name=name, out_shape=jax.ShapeDtypeStruct((N_DEV * R, C), v.dtype),
        in_specs=[pl.BlockSpec(memory_space=pltpu.VMEM)], out_specs=pl.BlockSpec(memory_space=pltpu.VMEM),
        scratch_shapes=[pltpu.SemaphoreType.DMA((7,)), pltpu.SemaphoreType.DMA((7,)), pltpu.SemaphoreType.DMA],
        compiler_params=pltpu.CompilerParams(vmem_limit_bytes=VMEM_LIMIT),
    )(v)


def _aliased_comm_call(body, bufs, n_sems, *, name):
    n = len(bufs)
    return pl.pallas_call(
        body, name=name, out_shape=[jax.ShapeDtypeStruct(b.shape, b.dtype) for b in bufs],
        in_specs=[ANY] * n, out_specs=[ANY] * n, input_output_aliases={k: k for k in range(n)},
        scratch_shapes=[pltpu.SemaphoreType.DMA((n_sems,)), pltpu.SemaphoreType.DMA((n_sems,))],
    )(*bufs)


def _allgather_chips(bufs, *, name):
    n = len(bufs)

    def body(*refs):
        outs, send_sems, recv_sems = refs[n:2 * n], refs[2 * n], refs[2 * n + 1]
        x, y, c = _place()
        chips = _other_chips(x, y)

        def copy(b, j, chip, hc, to):
            rh = bufs[b].shape[1] // 2
            part = outs[b].at[2 * chip[0] + chip[1], pl.ds(hc * rh, rh), :]
            return pltpu.make_async_remote_copy(src_ref=part, dst_ref=part, send_sem=send_sems.at[6 * b + j],
                                                recv_sem=recv_sems.at[6 * b + j], device_id=to, device_id_type=MESH)

        first = [copy(b, j, (x, y), c, (*chip, c)) for b in range(n) for j, chip in enumerate(chips)]
        for cp in first:
            cp.start()
        passed = []
        for b in range(n):
            for j, chip in enumerate(chips):
                copy(b, j, chip, c, (x, y, c)).wait_recv()
                passed.append(copy(b, 3 + j, chip, c, (x, y, 1 - c)))
                passed[-1].start()
        for b in range(n):
            for j, chip in enumerate(chips):
                copy(b, 3 + j, chip, 1 - c, (x, y, c)).wait_recv()
        for cp in first + passed:
            cp.wait_send()

    return _aliased_comm_call(body, bufs, 6 * n, name=name)


HBM = pl.BlockSpec(memory_space=pltpu.HBM)
SEM = pl.BlockSpec(memory_space=pltpu.SEMAPHORE)
_SPLIT = pltpu.CompilerParams(has_side_effects=pltpu.SideEffectType.DATAFLOW_SIDE_EFFECTING)


def _in_hbm(arrs):
    return [pltpu.with_memory_space_constraint(a, pltpu.HBM) for a in arrs]


def _gather_ici_start(bufs, after, *, name):
    n = len(bufs)

    def body(*refs):
        send_sems, recv_sems, outs, token = refs[n + 1], refs[n + 2], refs[n + 3:2 * n + 3], refs[2 * n + 3]
        x, y, c = _place()
        for b in range(n):
            rh = bufs[b].shape[1] // 2
            part = outs[b].at[2 * x + y, pl.ds(c * rh, rh), :]
            for j, chip in enumerate(_other_chips(x, y)):
                pltpu.make_async_remote_copy(src_ref=part, dst_ref=part, send_sem=send_sems.at[3 * b + j],
                                             recv_sem=recv_sems.at[3 * b + j], device_id=(*chip, c),
                                             device_id_type=MESH).start()
        token[...] = jnp.zeros_like(token)

    res = pl.pallas_call(
        body, name=name,
        out_shape=(pltpu.SemaphoreType.DMA((3 * n,)), pltpu.SemaphoreType.DMA((3 * n,)),
                   *[pltpu.HBM(b.shape, b.dtype) for b in bufs], jax.ShapeDtypeStruct((8, 128), F32)),
        in_specs=[HBM] * n + [ANY], out_specs=(SEM, SEM, *[HBM] * n, pl.BlockSpec(memory_space=pltpu.VMEM)),
        input_output_aliases={k: k + 2 for k in range(n)}, compiler_params=_SPLIT,
    )(*_in_hbm(bufs), after)
    return res[0], res[1], list(res[2:2 + n]), res[2 + n]


def _gather_ici_wait(send_sems, recv_sems, bufs, after, *, name):
    n = len(bufs)

    def body(*refs):
        ins, ss, rs = refs[:n], refs[n], refs[n + 1]
        x, y, c = _place()
        for b in range(n):
            rh = bufs[b].shape[1] // 2
            mine = ins[b].at[2 * x + y, pl.ds(c * rh, rh), :]
            for j, (cx, cy) in enumerate(_other_chips(x, y)):
                theirs = ins[b].at[2 * cx + cy, pl.ds(c * rh, rh), :]
                cp = pltpu.make_async_remote_copy(src_ref=mine, dst_ref=theirs, send_sem=ss.at[3 * b + j],
                                                  recv_sem=rs.at[3 * b + j], device_id=(cx, cy, c),
                                                  device_id_type=MESH)
                cp.wait_send()
                cp.wait_recv()

    return list(pl.pallas_call(
        body, name=name, out_shape=[pltpu.HBM(b.shape, b.dtype) for b in bufs],
        in_specs=[HBM] * n + [SEM, SEM, ANY], out_specs=[HBM] * n,
        input_output_aliases={k: k for k in range(n)}, compiler_params=_SPLIT,
    )(*bufs, send_sems, recv_sems, after))


def _gather_forward(bufs, *, name):
    n = len(bufs)

    def body(*refs):
        outs, send_sems, recv_sems = refs[n:2 * n], refs[2 * n], refs[2 * n + 1]
        x, y, c = _place()

        def copy(b, j, chip, hc):
            rh = bufs[b].shape[1] // 2
            part = outs[b].at[2 * chip[0] + chip[1], pl.ds(hc * rh, rh), :]
            return pltpu.make_async_remote_copy(src_ref=part, dst_ref=part, send_sem=send_sems.at[3 * b + j],
                                                recv_sem=recv_sems.at[3 * b + j], device_id=(x, y, 1 - c),
                                                device_id_type=MESH)

        sends = [copy(b, j, chip, c) for b in range(n) for j, chip in enumerate(_other_chips(x, y))]
        for cp in sends:
            cp.start()
        for b in range(n):
            for j, chip in enumerate(_other_chips(x, y)):
                copy(b, j, chip, 1 - c).wait_recv()
        for cp in sends:
            cp.wait_send()

    return _aliased_comm_call(body, bufs, 3 * n, name=name)


def _chip_exchange_start(hs, *, name):
    n = len(hs)
    lands = [lax.empty((3,) + h.shape[1:], h.dtype) for h in hs]

    def body(*refs):
        send_sems, recv_sems = refs[2 * n], refs[2 * n + 1]
        h_out, l_out, token = refs[2 * n + 2:3 * n + 2], refs[3 * n + 2:4 * n + 2], refs[4 * n + 2]
        x, y, c = _place()
        for b in range(n):
            for j, (cx, cy) in enumerate(_other_chips(x, y)):
                pltpu.make_async_remote_copy(src_ref=h_out[b].at[2 * cx + cy], dst_ref=l_out[b].at[j],
                                             send_sem=send_sems.at[3 * b + j], recv_sem=recv_sems.at[3 * b + j],
                                             device_id=(cx, cy, c), device_id_type=MESH).start()
        token[...] = jnp.zeros_like(token)

    res = pl.pallas_call(
        body, name=name,
        out_shape=(pltpu.SemaphoreType.DMA((3 * n,)), pltpu.SemaphoreType.DMA((3 * n,)),
                   *[pltpu.HBM(a.shape, a.dtype) for a in hs + lands], jax.ShapeDtypeStruct((8, 128), F32)),
        in_specs=[HBM] * (2 * n), out_specs=(SEM, SEM, *[HBM] * (2 * n), pl.BlockSpec(memory_space=pltpu.VMEM)),
        input_output_aliases={k: k + 2 for k in range(2 * n)}, compiler_params=_SPLIT,
    )(*_in_hbm(hs + lands))
    return res[0], res[1], list(res[2:2 + n]), list(res[2 + n:2 + 2 * n]), res[2 + 2 * n]


def _chip_exchange_wait(send_sems, recv_sems, hs, lands, after, *, name):
    n = len(hs)

    def body(*refs):
        h_in, l_in, ss, rs = refs[:n], refs[n:2 * n], refs[2 * n], refs[2 * n + 1]
        x, y, c = _place()
        for b in range(n):
            for j, (cx, cy) in enumerate(_other_chips(x, y)):
                cp = pltpu.make_async_remote_copy(src_ref=h_in[b].at[2 * cx + cy], dst_ref=l_in[b].at[j],
                                                  send_sem=ss.at[3 * b + j], recv_sem=rs.at[3 * b + j],
                                                  device_id=(cx, cy, c), device_id_type=MESH)
                cp.wait_send()
                cp.wait_recv()

    res = pl.pallas_call(
        body, name=name, out_shape=[pltpu.HBM(a.shape, a.dtype) for a in hs + lands],
        in_specs=[HBM] * (2 * n) + [SEM, SEM, ANY], out_specs=[HBM] * (2 * n),
        input_output_aliases={k: k for k in range(2 * n)}, compiler_params=_SPLIT,
    )(*hs, *lands, send_sems, recv_sems, after)
    return list(res[n:])


def _pair_exchange(gs, *, name):
    n = len(gs)

    def body(*refs):
        ins, outs, send_sems, recv_sems = refs[:n], refs[n:2 * n], refs[2 * n], refs[2 * n + 1]
        x, y, c = _place()
        cps = []
        for b in range(n):
            rh = gs[b].shape[1] // 2
            cps.append(pltpu.make_async_remote_copy(
                src_ref=ins[b].at[:, pl.ds((1 - c) * rh, rh), :], dst_ref=outs[b], send_sem=send_sems.at[b],
                recv_sem=recv_sems.at[b], device_id=(x, y, 1 - c), device_id_type=MESH))
        for cp in cps:
            cp.start()
        for cp in cps:
            cp.wait()

    return pl.pallas_call(
        body, name=name, out_shape=[jax.ShapeDtypeStruct((g.shape[0], g.shape[1] // 2, g.shape[2]), g.dtype) for g in gs],
        in_specs=[ANY] * n, out_specs=[ANY] * n,
        scratch_shapes=[pltpu.SemaphoreType.DMA((n,)), pltpu.SemaphoreType.DMA((n,))],
    )(*gs)


def _pair_share(ss, *, name):
    n = len(ss)

    def body(*refs):
        outs, send_sems, recv_sems = refs[n:2 * n], refs[2 * n], refs[2 * n + 1]
        x, y, c = _place()
        cps = []
        for b in range(n):
            rh = ss[b].shape[0] // 2
            mine = outs[b].at[pl.ds(c * rh, rh), :]
            cps.append(pltpu.make_async_remote_copy(src_ref=mine, dst_ref=mine, send_sem=send_sems.at[b],
                                                    recv_sem=recv_sems.at[b], device_id=(x, y, 1 - c),
                                                    device_id_type=MESH))
        for cp in cps:
            cp.start()
        for b, cp in enumerate(cps):
            rh = ss[b].shape[0] // 2
            theirs = outs[b].at[pl.ds((1 - c) * rh, rh), :]
            pltpu.make_async_remote_copy(src_ref=theirs, dst_ref=theirs, send_sem=send_sems.at[b],
                                         recv_sem=recv_sems.at[b], device_id=(x, y, 1 - c),
                                         device_id_type=MESH).wait_recv()
            cp.wait_send()

    return _aliased_comm_call(body, ss, n, name=name)


_SMALL_SHARDED = (("e_conv_w", 2), ("o_norm", 1), ("o_d", 1))
_REPLICATED = ("e_norm", "e_gmlp_w", "e_gmlp_b", "e_conv_b", "e_conv_ln_g", "e_conv_ln_b", "o_lam_re", "o_lam_im",
               "o_log_dt", "o_b_re", "o_b_im", "o_c_re", "o_c_im", "ca_norm", "ca_mem_norm", "ffn_norm", "final_norm")
_SMALL = tuple(n for n, _ in _SMALL_SHARDED) + _REPLICATED
_WEIGHTS = ("e_norm", "e_w_in", "e_gmlp_w", "e_gmlp_b", "e_conv_w", "e_conv_b", "e_conv_ln_g", "e_conv_ln_b",
            "e_w_out", "o_norm", "o_w_in", "o_lam_re", "o_lam_im", "o_log_dt", "o_b_re", "o_b_im", "o_c_re", "o_c_im",
            "o_d", "o_w_out", "ca_norm", "ca_mem_norm", "ca_wq", "ca_wk", "ca_wv", "ca_wo", "ffn_norm", "ffn_w_gate",
            "ffn_w_up", "ffn_w_down", "final_norm")


def _pack_rows(arrs, width, dtype, row_mult=8):
    parts, spans, r0 = [], [], 0
    for a in arrs:
        flat = a.reshape(-1).astype(dtype)
        rows = -(-flat.shape[0] // (width * row_mult)) * row_mult
        if rows * width != flat.shape[0]:
            flat = jnp.pad(flat, (0, rows * width - flat.shape[0]))
        parts.append(flat.reshape(rows, width))
        spans.append((r0, rows))
        r0 += rows
    return jnp.concatenate(parts, axis=0), spans


def _unpack_rows(slab, spans, shapes):
    out = []
    for (r0, rows), shp in zip(spans, shapes):
        n = math.prod(shp)
        out.append(slab[r0:r0 + rows].reshape(-1)[:n].reshape(shp))
    return out


def _two_d(a):
    return a.reshape(-1, a.shape[-1])


def _local_slab(local, slab, dtype):
    parts = sorted((r0, n, l) for n, (_, where) in _PLACE.items() for l, (s, r0) in enumerate(where) if s == slab)
    shards = [local[n] if len(_PLACE[n][1]) == 1 else local[n][l] for _, n, l in parts]
    return jnp.concatenate([_two_d(a).astype(dtype) for a in shards], axis=0)


def _block_diag(b, pattern):
    return jnp.einsum(pattern, b, jnp.eye(C_GROUPS, dtype=b.dtype))


def _s5_discretize(lam_re, lam_im, log_dt, b_re, b_im):
    dt = jnp.exp(log_dt)[:, None]
    mag = jnp.exp(lam_re * dt)
    ar = mag * jnp.cos(lam_im * dt)
    ai = mag * jnp.sin(lam_im * dt)
    den = lam_re * lam_re + lam_im * lam_im
    qr = ((ar - 1.0) * lam_re + ai * lam_im) / den
    qi = (ai * lam_re - (ar - 1.0) * lam_im) / den
    bbr = qr[..., None] * b_re - qi[..., None] * b_im
    bbi = qr[..., None] * b_im + qi[..., None] * b_re
    return ar, ai, bbr, bbi


def _attention_block(x, mem, W, w, i, tag):
    xn, q = _norm_mm(x, w["ca_norm"][i], _shards(W, "ca_wq", i), split="k", out_dtype=BF16, name=f"{tag}_q")
    memn = _rms_fwd(mem, w["ca_mem_norm"][i], name=f"{tag}_ca_memnorm")
    k = _mm_k(memn, _shards(W, "ca_wk", i), out_dtype=BF16, name=f"{tag}_k")
    v = _mm_k(memn, _shards(W, "ca_wv", i), out_dtype=BF16, name=f"{tag}_v")
    o = _attn_fwd(q, k, v, name=f"{tag}_attn")
    y = _mm_k(o, _shards(W, "ca_wo", i), add=x, name=f"{tag}_wo")
    return y, (x, xn, memn, q, k, v, o)


def _attention_block_bwd(dy, saved, mem, W, w, i, tag, G, grads, token=None):
    x, xn, memn, q, k, v, o = saved
    if token is not None:
        k = _behind(k, token)
    G = _grad_to_slab(G, "ca_wo", i, o, dy, a_cols=256, name=f"{tag}_dwo")
    dq, dk, dv = _attn_bwd(dy, _shards(W, "ca_wo", i), q, k, v, name=f"{tag}_attn_bwd")
    G = _grad_to_slab(G, "ca_wq", i, xn, dq, a_cols=256, name=f"{tag}_dwq")
    G = _grad_to_slab(G, "ca_wk", i, memn, dk, a_cols=256, name=f"{tag}_dwk")
    G = _grad_to_slab(G, "ca_wv", i, memn, dv, a_cols=256, name=f"{tag}_dwv")
    dmemn = _mm_k_t([(dk, _shards(W, "ca_wk", i)), (dv, _shards(W, "ca_wv", i))], name=f"{tag}_dmemn")
    dx, dg = _norm_bwd_k(dq, _shards(W, "ca_wq", i), x, w["ca_norm"][i], dy, name=f"{tag}_dq_norm_bwd")
    grads["ca_norm"][i] = dg[0]
    grads["ca_mem_norm"][i] = _rms_dg(mem, w["ca_mem_norm"][i], dmemn, name=f"{tag}_ca_memnorm_bwd")[0]
    return dx, G


def _ffn_block(x, W, w, i, tag):
    fn, gate, up, h = _ffn_up(x, w["ffn_norm"][i], _shards(W, "ffn_w_gate", i), _shards(W, "ffn_w_up", i),
                              name=f"{tag}_ffn_up")
    y = _mm_k(h, _shards(W, "ffn_w_down", i), add=x, name=f"{tag}_down")
    return y, (x, fn, gate, up, h)


def _ffn_block_bwd(dy, saved, W, w, i, tag, G, grads, token=None):
    x, fn, gate, up, h = saved
    G = _grad_to_slab(G, "ffn_w_down", i, h, dy, name=f"{tag}_dwd")
    dg, du = _ffn_bwd_hidden(dy, _shards(W, "ffn_w_down", i), gate, up, token, name=f"{tag}_ffn_bwd_hidden")
    G = _grad_to_slab(G, "ffn_w_gate", i, fn, dg, name=f"{tag}_dwg")
    G = _grad_to_slab(G, "ffn_w_up", i, fn, du, name=f"{tag}_dwu")
    dx, dgn = _ffn_in_bwd(dg, du, _shards(W, "ffn_w_gate", i), _shards(W, "ffn_w_up", i), x, w["ffn_norm"][i], dy,
                          name=f"{tag}_ffn_in_bwd")
    grads["ffn_norm"][i] = dgn[0]
    return dx, G


def _gmlp_mask():
    chunk = jnp.arange(GMLP_BLOCK) // CHUNK
    return chunk[None, :] <= chunk[:, None]


def _even_block(x, W, w, tag):
    hn, proj = _norm_mm(x, w["e_norm"][0], _shards(W, "e_w_in"), split="n", out_dtype=F32, name=f"{tag}_w_in")
    wm = jnp.where(_gmlp_mask()[None], w["e_gmlp_w"][0], 0.0).astype(BF16)
    bcol = w["e_gmlp_b"][0][:, :, None]
    cw = jnp.pad(w["e_conv_w"][0], ((0, CONV_HALO - CONV_WIDTH), (0, 0)))
    cb, lg, lb = w["e_conv_b"], w["e_conv_ln_g"], w["e_conv_ln_b"]
    mix, hc = _even_fwd(proj, wm, bcol, cw, cb, lg, lb, name=f"{tag}_mixers")
    y = _mm_k(mix, _shards(W, "e_w_out"), add=x, name=f"{tag}_w_out")
    return y, (x, hn, proj, mix, hc, wm, bcol, cw)


def _even_block_bwd(dy, saved, W, w, tag, G, grads):
    x, hn, proj, mix, hc, wm, bcol, cw = saved
    dmix = _mm_k_t([(dy, _shards(W, "e_w_out"))], name=f"{tag}_dmix")
    G = _grad_to_slab(G, "e_w_out", 0, mix, dy, a_cols=256, name=f"{tag}_dw_out")
    wmt = jnp.swapaxes(wm, 1, 2)
    dpa, dhc, dwm, db, dlg, dlb, dcb = _even_bwd1(proj, dmix, hc, wm, wmt, bcol, w["e_conv_ln_g"], w["e_conv_ln_b"],
                                                  name=f"{tag}_mixers_bwd1")
    dpb, dcw = _even_bwd2(proj, dhc, cw, name=f"{tag}_mixers_bwd2")
    grads["e_gmlp_w"] = jnp.where(_gmlp_mask()[None], dwm, 0.0)[None]
    grads["e_gmlp_b"] = db[:, :, 0][None]
    grads["e_conv_ln_g"], grads["e_conv_ln_b"], grads["e_conv_b"] = dlg, dlb, dcb
    grads["e_conv_w"] = dcw[:CONV_WIDTH][None]
    G = _grad_to_slab(G, "e_w_in", 0, hn, dpa, b_cols=512, chips=(0, 2), name=f"{tag}_dw_in_a")
    G = _grad_to_slab(G, "e_w_in", 0, hn, dpb, b_cols=512, chips=(2, 2), name=f"{tag}_dw_in_b")
    dx, dg = _norm_bwd_n((dpa, dpb), _shards(W, "e_w_in"), x, w["e_norm"][0], dy, name=f"{tag}_in_bwd")
    grads["e_norm"] = dg
    return dx, G


def _odd_block(x, W, w, tag):
    S = x.shape[0]
    hn, u = _norm_mm(x, w["o_norm"][0], _shards(W, "o_w_in"), split="k", out_dtype=F32, name=f"{tag}_w_in")
    disc_in = (w["o_lam_re"][0], w["o_lam_im"][0], w["o_log_dt"][0], w["o_b_re"][0], w["o_b_im"][0])
    (ar, ai, bbr, bbi), disc_vjp = jax.vjp(_s5_discretize, *disc_in)
    bd = jnp.concatenate([_block_diag(bbr, "gpc,gh->gchp").reshape(C_WIDTH, N_STATE),
                          _block_diag(bbi, "gpc,gh->gchp").reshape(C_WIDTH, N_STATE)], axis=1).astype(BF16)
    cd = jnp.concatenate([_block_diag(w["o_c_re"][0], "gcp,gh->gphc").reshape(N_STATE, C_WIDTH),
                          -_block_diag(w["o_c_im"][0], "gcp,gh->gphc").reshape(N_STATE, C_WIDTH)], axis=0).astype(BF16)
    powers, pr, pi = [], ar, ai
    for _ in range(SCAN_BLOCK):
        powers.append(jnp.concatenate([pr.reshape(STATE_ROWS, STATE_LANES), pi.reshape(STATE_ROWS, STATE_LANES)], 0))
        pr, pi = pr * ar - pi * ai, pr * ai + pi * ar
    pw = jnp.stack(powers, axis=0)
    state_rows = (S * _STATE_TILE, STATE_LANES)
    bu = _mm_to_state(u, bd, name=f"{tag}_bu")
    xs = _scan_fwd(bu.reshape(state_rows), pw, name=f"{tag}_scan").reshape(bu.shape)
    yv, yg = _s5_readout(xs, cd, u, w["o_d"], name=f"{tag}_readout")
    o, y = _glu_out(yg, _shards(W, "o_w_out"), x, name=f"{tag}_glu_out")
    return y, (x, hn, u, bd, cd, pw, xs, yv, yg, o, disc_vjp)


def _odd_block_bwd(dy, saved, W, w, tag, G, grads):
    x, hn, u, bd, cd, pw, xs, yv, yg, o, disc_vjp = saved
    S = x.shape[0]
    state_rows = (S * _STATE_TILE, STATE_LANES)
    do, dys, dus, dd = _glu_out_bwd(o, dy, _shards(W, "o_w_out"), yv, u, w["o_d"], name=f"{tag}_glu_out_bwd")
    G = _grad_to_slab(G, "o_w_out", 0, yg, do, b_cols=512, name=f"{tag}_dw_out")
    grads["o_d"] = dd
    dxs = _mm_to_state(dys, cd, nt=True, name=f"{tag}_dxs")
    dcd_t = _state_grad_tn(dys, xs, name=f"{tag}_dcd")
    gs, da = _scan_bwd(dxs.reshape(state_rows), xs.reshape(state_rows), pw, name=f"{tag}_scan_bwd")
    gs = gs.reshape(xs.shape)
    dbd = _state_grad_tn(u, gs, name=f"{tag}_dbd")
    du, dx, dg = _s5_in_bwd(gs, bd, dus, _shards(W, "o_w_in"), x, w["o_norm"][0], dy, name=f"{tag}_in_bwd")
    G = _grad_to_slab(G, "o_w_in", 0, hn, du, a_cols=256, name=f"{tag}_dw_in")
    grads["o_norm"] = dg
    eye = jnp.eye(C_GROUPS, dtype=F32)
    dcr = jnp.einsum("hcgp,gh->gcp", dcd_t[:, :N_STATE].reshape(C_GROUPS, C_GROUP_CH, C_GROUPS, C_STATE), eye)
    dci = -jnp.einsum("hcgp,gh->gcp", dcd_t[:, N_STATE:].reshape(C_GROUPS, C_GROUP_CH, C_GROUPS, C_STATE), eye)
    dbbr = jnp.einsum("gchp,gh->gpc", dbd[:, :N_STATE].reshape(C_GROUPS, C_GROUP_CH, C_GROUPS, C_STATE), eye)
    dbbi = jnp.einsum("gchp,gh->gpc", dbd[:, N_STATE:].reshape(C_GROUPS, C_GROUP_CH, C_GROUPS, C_STATE), eye)
    dar = da[:STATE_ROWS].reshape(C_GROUPS, C_STATE)
    dai = da[STATE_ROWS:].reshape(C_GROUPS, C_STATE)
    dlr, dli, dldt, dbr, dbi = disc_vjp((dar, dai, dbbr, dbbi))
    grads["o_lam_re"], grads["o_lam_im"], grads["o_log_dt"] = dlr[None], dli[None], dldt[None]
    grads["o_b_re"], grads["o_b_im"], grads["o_c_re"], grads["o_c_im"] = dbr[None], dbi[None], dcr[None], dci[None]
    return dx, G


def _behind(value, token):
    return value + token[0, 0].astype(value.dtype)


class _NoExchange:
    def __init__(self, W):
        self.W = W

    def first_weights(self, w):
        return self.W, w

    def after_even_mixer(self, x, W, w):
        return W, w

    def after_layer0(self, x, W):
        return W

    def after_layer1_backward(self, G):
        return None

    def after_ffn0_backward(self, G):
        return None


def _forward_backward(xs_, mems_, tgt, w, G, exchange):
    W, w = exchange.first_weights(w)
    x1, s_mix0 = _even_block(xs_, W, w, "l0")
    W, w = exchange.after_even_mixer(x1, W, w)
    x2, s_att0 = _attention_block(x1, mems_, W, w, 0, "l0")
    x3, s_ffn0 = _ffn_block(x2, W, w, 0, "l0")
    W = exchange.after_layer0(x3, W)
    x4, s_mix1 = _odd_block(x3, W, w, "l1")
    x5, s_att1 = _attention_block(x4, mems_, W, w, 1, "l1")
    x6, s_ffn1 = _ffn_block(x5, W, w, 1, "l1")
    dx, dfinal, loss_lanes = _loss_head(x6, w["final_norm"], tgt, name="loss_head")

    grads = {n: [None, None] for n in ("ca_norm", "ca_mem_norm", "ffn_norm")}
    grads["final_norm"] = dfinal[0]
    dx, G = _ffn_block_bwd(dx, s_ffn1, W, w, 1, "l1", G, grads)
    dx, G = _attention_block_bwd(dx, s_att1, mems_, W, w, 1, "l1", G, grads)
    dx, G = _odd_block_bwd(dx, s_mix1, W, w, "l1", G, grads)
    token = exchange.after_layer1_backward(G)
    dx, G = _ffn_block_bwd(dx, s_ffn0, W, w, 0, "l0", G, grads, token)
    token = exchange.after_ffn0_backward(G)
    dx, G = _attention_block_bwd(dx, s_att0, mems_, W, w, 0, "l0", G, grads, token)
    dx, G = _even_block_bwd(dx, s_mix0, W, w, "l0", G, grads)
    for n in list(grads):
        if isinstance(grads[n], list):
            grads[n] = jnp.stack(grads[n], axis=0)
        grads[n] = grads[n].reshape(w[n].shape)
    return loss_lanes, dx, G, grads


class _Exchange:
    def __init__(self, local, chip, core):
        self.bufs = {s: lax.dynamic_update_slice(lax.empty((N_CHIPS, rows, width), BF16),
                                                 _local_slab(local, s, BF16)[None], (chip, 0, 0))
                     for s, (width, rows) in _SLABS.items()}
        self.half = core.reshape(1).astype(jnp.int32)
        self.where = jnp.stack([chip, core]).astype(jnp.int32)
        self.flight = None
        self.layer1_reduce = None

    def _start(self, stage, after, tag):
        self.flight = _gather_ici_start([self.bufs[s] for s in stage], after, name=f"gather_{tag}_start")
        return self.flight[3]

    def _land(self, stage, after, tag):
        send_sems, recv_sems, bufs, _ = self.flight
        bufs = _gather_ici_wait(send_sems, recv_sems, bufs, after, name=f"gather_{tag}_wait")
        return dict(zip(stage, _gather_forward(bufs, name=f"gather_{tag}_forward")))

    def first_weights(self, w):
        W = dict(zip(_STAGES[0], _allgather_chips([self.bufs[s] for s in _STAGES[0]], name="gather_stage0")))
        landed = W[_STAGES[0][0]][0, :8, :STATE_LANES].astype(F32) + w["e_conv_w"].reshape(-1)[0]
        token = self._start(_STAGES[1], landed, "stage1")
        return W, {**w, "e_norm": _behind(w["e_norm"], token)}

    def after_even_mixer(self, x, W, w):
        W = {**W, **self._land(_STAGES[1], x, "stage1")}
        token = self._start(_STAGES[2], W[_STAGES[1][0]], "stage2")
        return W, {**w, "ca_norm": _behind(w["ca_norm"], token)}

    def after_layer0(self, x, W):
        return {**W, **self._land(_STAGES[2], x, "stage2")}

    def reduce_start(self, G, slabs, tag):
        gl = [G[s] for s in slabs]
        other = _pair_exchange(gl, name=f"grad_{tag}_pair_exchange")
        pairs = [_pair_sum(g, r, self.half, name=f"grad_pair_sum_{s}") for s, g, r in zip(slabs, gl, other)]
        send_sems, recv_sems, pairs, lands, token = _chip_exchange_start(pairs, name=f"grad_{tag}_chip_start")
        return (slabs, gl, other, send_sems, recv_sems, pairs, lands), token

    def reduce_finish(self, state, after, tag):
        slabs, gl, other, send_sems, recv_sems, pairs, lands = state
        slots = _chip_exchange_wait(send_sems, recv_sems, pairs, lands, after, name=f"grad_{tag}_chip_wait")
        halves = [_chip_sum(g, r, sl, self.where, name=f"grad_chip_sum_{s}")
                  for s, g, r, sl in zip(slabs, gl, other, slots)]
        return dict(zip(slabs, _pair_share(halves, name=f"grad_{tag}_pair_share")))

    def after_layer1_backward(self, G):
        self.layer1_reduce, token = self.reduce_start(G, _STAGES[2], "l1")
        return token

    def after_ffn0_backward(self, G):
        self.ffn0_reduce, token = self.reduce_start(G, _FFN0_SLABS, "ffn0")
        return token


def kernel(x, mem, e_norm, e_w_in, e_gmlp_w, e_gmlp_b, e_conv_w, e_conv_b, e_conv_ln_g, e_conv_ln_b, e_w_out, o_norm, o_w_in, o_lam_re, o_lam_im, o_log_dt, o_b_re, o_b_im, o_c_re, o_c_im, o_d, o_w_out, ca_norm, ca_mem_norm, ca_wq, ca_wk, ca_wv, ca_wo, ffn_norm, ffn_w_gate, ffn_w_up, ffn_w_down, final_norm, loss_target, m_e_norm, m_e_w_in, m_e_gmlp_w, m_e_gmlp_b, m_e_conv_w, m_e_conv_b, m_e_conv_ln_g, m_e_conv_ln_b, m_e_w_out, m_o_norm, m_o_w_in, m_o_lam_re, m_o_lam_im, m_o_log_dt, m_o_b_re, m_o_b_im, m_o_c_re, m_o_c_im, m_o_d, m_o_w_out, m_ca_norm, m_ca_mem_norm, m_ca_wq, m_ca_wk, m_ca_wv, m_ca_wo, m_ffn_norm, m_ffn_w_gate, m_ffn_w_up, m_ffn_w_down, m_final_norm, v_e_norm, v_e_w_in, v_e_gmlp_w, v_e_gmlp_b, v_e_conv_w, v_e_conv_b, v_e_conv_ln_g, v_e_conv_ln_b, v_e_w_out, v_o_norm, v_o_w_in, v_o_lam_re, v_o_lam_im, v_o_log_dt, v_o_b_re, v_o_b_im, v_o_c_re, v_o_c_im, v_o_d, v_o_w_out, v_ca_norm, v_ca_mem_norm, v_ca_wq, v_ca_wk, v_ca_wv, v_ca_wo, v_ffn_norm, v_ffn_w_gate, v_ffn_w_up, v_ffn_w_down, v_final_norm):
    args = dict(locals())
    local = {n: args[n] for n in _WEIGHTS}
    mom = {n: args["m_" + n] for n in _WEIGHTS}
    vel = {n: args["v_" + n] for n in _WEIGHTS}
    chip = 2 * lax.axis_index("x") + lax.axis_index("y")
    core = lax.axis_index("c")
    xs_, mems_, tgt = x[0], mem[0], loss_target[0]

    w = {n: local[n] for n in _REPLICATED}
    sm_slab, sm_spans = _pack_rows([local[n] for n, _ in _SMALL_SHARDED], SMALL_W, F32)
    sm_all = _allgather_small(sm_slab, name="gather_small_weights").reshape(N_DEV, -1, SMALL_W)
    for (n, ax), span in zip(_SMALL_SHARDED, sm_spans):
        shp = local[n].shape
        w[n] = jnp.concatenate([_unpack_rows(sm_all[2 * p], [span], [shp])[0] for p in range(N_CHIPS)], axis=ax)

    exchange = _Exchange(local, chip, core)
    G = {s: lax.empty((N_CHIPS, rows, width), F32) for s, (width, rows) in _SLABS.items()}
    loss_lanes, dx, G, grads = _forward_backward(xs_, mems_, tgt, w, G, exchange)

    gsum = exchange.reduce_finish(exchange.layer1_reduce, dx, "l1")
    gsum = {**gsum, **exchange.reduce_finish(exchange.ffn0_reduce, dx, "ffn0")}
    rest0_reduce, token = exchange.reduce_start(G, _REST0_SLABS, "rest0")

    grads = {**grads, _SMALL[0]: _behind(grads[_SMALL[0]], token)}
    gs_slab, gs_spans = _pack_rows([grads[n] for n in _SMALL], SMALL_W, F32)
    gs_all = _allgather_small(gs_slab, name="gather_small_grads").reshape(N_DEV, -1, SMALL_W)
    gs_sum = _sum_slots(gs_all, name="small_grad_sum")
    out_grads = dict(zip(_SMALL, _unpack_rows(gs_sum, gs_spans, [grads[n].shape for n in _SMALL])))
    for n, ax in _SMALL_SHARDED:
        width = local[n].shape[ax]
        out_grads[n] = lax.dynamic_slice_in_dim(out_grads[n], chip * width, width, axis=ax)

    delta, new_m, new_v = {}, {}, {}
    d_, m_, v_ = _adamw_small([_two_d(local[n]) for n in _SMALL], [_two_d(out_grads[n]) for n in _SMALL],
                              [_two_d(mom[n]) for n in _SMALL], [_two_d(vel[n]) for n in _SMALL], name="adamw_small")
    for n, dd, mm_, vv in zip(_SMALL, d_, m_, v_):
        shp = local[n].shape
        delta[n], new_m[n], new_v[n] = dd.reshape(shp), mm_.reshape(shp), vv.reshape(shp)
    def adamw_large(names):
        for n in names:
            shp = local[n].shape
            g_, d_, m_, v_ = _adamw_shard(_two_d(local[n]), [(gsum[s], r0) for s, r0 in _PLACE[n][1]],
                                          _two_d(mom[n]), _two_d(vel[n]), name=f"adamw_{n}")
            out_grads[n], delta[n], new_m[n], new_v[n] = (g_.reshape(shp), d_.reshape(shp), m_.reshape(shp),
                                                          v_.reshape(shp))

    ready = [n for n, (_, where) in _PLACE.items() if all(s in gsum for s, _ in where)]
    adamw_large(ready)
    done = jnp.concatenate([delta[n].reshape(-1)[:1] for n in ready + list(_SMALL[:1])])
    gsum = {**gsum, **exchange.reduce_finish(rest0_reduce, done, "rest0")}
    adamw_large([n for n in _PLACE if n not in ready])

    loss = lax.psum(loss_lanes[0, 0], ("x", "y", "c"))
    return (loss, dx[None], *[out_grads[n] for n in _WEIGHTS], *[delta[n] for n in _WEIGHTS],
            *[new_m[n] for n in _WEIGHTS], *[new_v[n] for n in _WEIGHTS])
```

```python
import functools
import math

import jax
import jax.numpy as jnp
from jax import lax
from jax.experimental import pallas as pl
from jax.experimental.pallas import tpu as pltpu

F32 = jnp.float32
BF16 = jnp.bfloat16
MESH = pl.DeviceIdType.MESH

EPS = 1e-6
D_MODEL = 1024
A_WIDTH = 512
A_GROUPS = 4
GMLP_BLOCK = 128
CHUNK = 64
B_WIDTH = 512
CONV_WIDTH = 31
CONV_HALO = 32
C_WIDTH = 512
C_GROUP_CH = 16
C_GROUPS = 32
C_STATE = 64
N_STATE = C_GROUPS * C_STATE
STATE_LANES = 128
STATE_ROWS = N_STATE // STATE_LANES
SCAN_BLOCK = 8
CA_HEADS = 4
CA_HEAD_DIM = 256
FFN_HIDDEN = 2816

ADAM_LR = 0.001
ADAM_B1 = 0.9
ADAM_B2 = 0.999
ADAM_EPS = 1e-08
ADAM_WD = 0.01
ADAM_STEP = 10

VMEM_LIMIT = 56 * 1024 * 1024
ACC_BYTES = 6 * 1024 * 1024
TN_VMEM_BYTES = 44 * 1024 * 1024
SMALL_W = 128
N_CHIPS = 4
N_DEV = 8

_SLABS = {"D0": (512, 1024), "E0": (1024, 256), "A0": (1024, 1024), "B0": (1024, 704), "C0": (704, 2048),
          "D1": (512, 768), "A1": (1024, 1024), "B1": (1024, 704), "C1": (704, 2048)}
_STAGES = (("D0", "E0"), ("A0",), ("B0", "C0"), ("D1", "A1", "B1", "C1"))
_LAYER1_SLABS = _STAGES[3]
_FFN0_SLABS = _STAGES[2]
_REST0_SLABS = _STAGES[0] + _STAGES[1]
_PLACE = {
    "e_w_in": (1024, (("D0", 0),)), "e_w_out": (256, (("E0", 0),)),
    "o_w_out": (512, (("D1", 0),)), "o_w_in": (256, (("D1", 512),)),
    "ca_wq": (256, (("A0", 0), ("A1", 0))), "ca_wk": (256, (("A0", 256), ("A1", 256))),
    "ca_wv": (256, (("A0", 512), ("A1", 512))), "ca_wo": (256, (("A0", 768), ("A1", 768))),
    "ffn_w_down": (704, (("B0", 0), ("B1", 0))),
    "ffn_w_gate": (1024, (("C0", 0), ("C1", 0))), "ffn_w_up": (1024, (("C0", 1024), ("C1", 1024))),
}


def _params(sem=None):
    return pltpu.CompilerParams(dimension_semantics=sem, vmem_limit_bytes=VMEM_LIMIT)


def _tile(n, pref, mult=128):
    if n <= pref:
        return n
    t = (pref // mult) * mult
    while t >= mult:
        if n % t == 0:
            return t
        t -= mult
    return n


def _blk(name, layer=0):
    rows, where = _PLACE[name]
    slab, r0 = where[layer]
    assert r0 % rows == 0
    return slab, rows, r0 // rows


def _shards(slabs, name, layer=0):
    slab, rows, b = _blk(name, layer)
    return [(slabs[slab], (None, rows, _SLABS[slab][0]), (p, b, 0)) for p in range(N_CHIPS)]


_GELU_C = 0.7978845608028654
_GELU_A = 0.044715


def _gelu(x):
    t = jnp.tanh(_GELU_C * (x + _GELU_A * (x * x * x)))
    return 0.5 * x * (1.0 + t), t


def _gelu_grad(x, t):
    return 0.5 * (1.0 + t) + 0.5 * x * (1.0 - t * t) * (_GELU_C * (1.0 + 3.0 * _GELU_A * x * x))


def _sigmoid(x):
    return 1.0 / (1.0 + jnp.exp(-x))


def _mean(x):
    return jnp.mean(x, axis=-1, keepdims=True)


def _dot(a, b):
    return jnp.dot(a, b, preferred_element_type=F32)


def _dot_nt(a, b):
    return lax.dot_general(a, b, (((1,), (1,)), ((), ())), preferred_element_type=F32)


def _dot_tn(a, b):
    return lax.dot_general(a, b, (((0,), (0,)), ((), ())), preferred_element_type=F32)


def _rms_tile(xv, gv):
    return (xv * lax.rsqrt(_mean(xv * xv) + EPS)) * gv


def _rms_bwd_tile(xv, gv, dyv):
    r = lax.rsqrt(_mean(xv * xv) + EPS)
    xh = xv * r
    dyg = dyv * gv
    return r * (dyg - xh * _mean(dyg * xh)), jnp.sum(dyv * xh, axis=0, keepdims=True)


def _cols(p, width):
    return slice(p * width, (p + 1) * width)


def _sum_k(a, ws, k):
    tot = None
    for p in range(N_CHIPS):
        y = _dot(a[:, _cols(p, k)], ws[p][...])
        tot = y if tot is None else tot + y
    return tot


def _cat_nt(a, ws):
    return jnp.concatenate([_dot_nt(a, ws[p][...]) for p in range(N_CHIPS)], axis=1)


def _rows_call(name, tm, rows, fulls, outs, accs, body, scratch=()):
    S = min(x.shape[-2] for x in rows if x.ndim != 4)
    nr, nf, no, na = len(rows), len(fulls), len(outs), len(accs)

    def kern(*refs):
        r, f = refs[:nr], refs[nr:nr + nf]
        o, a = refs[nr + nf:nr + nf + no], refs[nr + nf + no:nr + nf + no + na]
        if na:
            @pl.when(pl.program_id(0) == 0)
            def _():
                for ref in a:
                    ref[...] = jnp.zeros_like(ref)
        body(r, f, o, a, refs[nr + nf + no + na:])

    def whole(shape):
        nd = len(shape)
        return pl.BlockSpec(tuple(shape), lambda i: (0,) * nd)

    def row_spec(shape):
        if len(shape) == 4:
            return pl.BlockSpec((tm // 8,) + tuple(shape[1:]), lambda i: (i, 0, 0, 0))
        if len(shape) == 3:
            return pl.BlockSpec((shape[0], tm, shape[2]), lambda i: (0, i, 0))
        return pl.BlockSpec((tm, shape[1]), lambda i: (i, 0))

    def full_spec(x):
        if isinstance(x, tuple):
            _, bshape, bidx = x
            return pl.BlockSpec(bshape, lambda i: bidx, pipeline_mode=pl.Buffered(1))
        return whole(x.shape)

    def out_shape_of(o):
        if o[0] == "state":
            return (S // 8, 2 * STATE_ROWS, 8, STATE_LANES)
        return (S, o[0]) if len(o) == 2 else (o[0], S, o[1])

    out_shapes = [out_shape_of(o) for o in outs]
    res = pl.pallas_call(
        kern, name=name, grid=(S // tm,),
        in_specs=[row_spec(x.shape) for x in rows] + [full_spec(x) for x in fulls],
        out_specs=[row_spec(s) for s in out_shapes] + [whole(shp) for shp, _ in accs],
        out_shape=[jax.ShapeDtypeStruct(s, o[-1]) for s, o in zip(out_shapes, outs)]
        + [jax.ShapeDtypeStruct(tuple(shp), dt) for shp, dt in accs],
        scratch_shapes=list(scratch),
        compiler_params=_params(("arbitrary",) if na else ("parallel",)),
    )(*rows, *[x[0] if isinstance(x, tuple) else x for x in fulls])
    return res[:no], res[no:]


def _grad_to_slab(gslabs, wname, layer, a, b, *, a_cols=None, b_cols=None, chips=(0, N_CHIPS), name):
    slab, rows, bidx = _blk(wname, layer)
    width = _SLABS[slab][0]
    p0, n_p = chips
    assert p0 % n_p == 0
    S = a.shape[-2]

    def tile_bytes(x, ts):
        return ts * x.dtype.itemsize * (x.shape[2] * n_p if x.ndim == 3 else x.shape[1])

    acc_bytes = n_p * rows * (-(-width // 128) * 128) * 4
    ts = next(t for t in (2048, 1024, 512, 256, S) if S % t == 0
              and 2 * (tile_bytes(a, t) + tile_bytes(b, t) + acc_bytes) <= TN_VMEM_BYTES or t == S)

    def operand(x):
        if x.ndim == 3:
            return pl.BlockSpec((n_p, ts, x.shape[2]), lambda s: (p0 // n_p, s, 0))
        return pl.BlockSpec((ts, x.shape[1]), lambda s: (s, 0))

    def part(ref, cols, p):
        if len(ref.shape) == 3:
            return ref[p]
        return ref[...] if cols is None else ref[:, _cols(p, cols)]

    def body(a_ref, b_ref, slab_ref, o_ref):
        @pl.when(pl.program_id(0) == 0)
        def _():
            o_ref[...] = jnp.zeros_like(o_ref)

        for p in range(n_p):
            o_ref[p] += _dot_tn(part(a_ref, a_cols, p).astype(BF16), part(b_ref, b_cols, p).astype(BF16))

    g = gslabs[slab]
    out = pl.pallas_call(
        body, name=name, grid=(S // ts,),
        in_specs=[operand(a), operand(b), pl.BlockSpec(memory_space=pl.ANY)],
        out_specs=pl.BlockSpec((n_p, rows, width), lambda s: (p0 // n_p, bidx, 0)),
        out_shape=jax.ShapeDtypeStruct(g.shape, F32), input_output_aliases={2: 0},
        compiler_params=_params(("arbitrary",)),
    )(a, b, g)
    return {**gslabs, slab: out}


def _vec(g):
    return g.reshape(1, -1)


def _norm_mm(x, g, ws, *, split, out_dtype, name, tm=512):
    S, D = x.shape
    k, n = ws[0][1][1], ws[0][1][2]
    N = n if split == "k" else N_CHIPS * n

    def body(r, f, o, acc, s):
        xn = _rms_tile(r[0][...], f[0][...]).astype(BF16)
        o[0][...] = xn
        if split == "k":
            o[1][...] = _sum_k(xn, f[1:], k).astype(out_dtype)
        else:
            for p in range(N_CHIPS):
                o[1][:, _cols(p, n)] = _dot(xn, f[1 + p][...]).astype(out_dtype)

    (xn, y), _ = _rows_call(name, _tile(S, tm), [x], [_vec(g)] + ws, [(D, BF16), (N, out_dtype)], [], body)
    return xn, y


def _mm_k(a, ws, *, add=None, out_dtype=F32, name, tm=512):
    S = a.shape[-2]
    k, n = ws[0][1][1], ws[0][1][2]
    has_add = add is not None

    def body(r, f, o, acc, s):
        if a.ndim == 3:
            y = None
            for p in range(N_CHIPS):
                t = _dot(r[0][p].astype(BF16), f[p][...])
                y = t if y is None else y + t
        else:
            y = _sum_k(r[0][...].astype(BF16), f, k)
        if has_add:
            y = y + r[1][...]
        o[0][...] = y.astype(out_dtype)

    (y,), _ = _rows_call(name, _tile(S, tm), [a] + ([add] if has_add else []), ws, [(n, out_dtype)], [], body)
    return y


def _mm_k_t(terms, *, out_dtype=F32, name, tm=512):
    S = terms[0][0].shape[0]
    k = terms[0][1][0][1][1]

    def body(r, f, o, acc, s):
        y = None
        for t in range(len(terms)):
            yt = _cat_nt(r[t][...].astype(BF16), f[N_CHIPS * t:N_CHIPS * (t + 1)])
            y = yt if y is None else y + yt
        o[0][...] = y.astype(out_dtype)

    (y,), _ = _rows_call(name, _tile(S, tm), [a for a, _ in terms], [w for _, ws in terms for w in ws],
                         [(N_CHIPS * k, out_dtype)], [], body)
    return y


def _rms_fwd(x, g, *, name):
    def body(r, f, o, acc, s):
        o[0][...] = _rms_tile(r[0][...], f[0][...]).astype(BF16)

    (y,), _ = _rows_call(name, _tile(x.shape[0], 256, 8), [x], [_vec(g)], [(x.shape[1], BF16)], [], body)
    return y


def _rms_dg(x, g, dy, *, name):
    def body(r, f, o, acc, s):
        acc[0][...] += _rms_bwd_tile(r[0][...], f[0][...], r[1][...])[1]

    _, (dg,) = _rows_call(name, _tile(x.shape[0], 256, 8), [x, dy], [_vec(g)], [], [((1, x.shape[1]), F32)], body)
    return dg


def _ffn_up(x, g, wg, wu, *, name, tm=256):
    S, D = x.shape
    h = wg[0][1][2]

    def body(r, f, o, acc, s):
        xn = _rms_tile(r[0][...], f[0][...]).astype(BF16)
        o[0][...] = xn
        for p in range(N_CHIPS):
            gate = _dot(xn, f[1 + p][...])
            up = _dot(xn, f[1 + N_CHIPS + p][...])
            o[1][p] = gate.astype(BF16)
            o[2][p] = up.astype(BF16)
            o[3][p] = (gate * _sigmoid(gate) * up).astype(BF16)

    (xn, gate, up, hid), _ = _rows_call(name, _tile(S, tm), [x], [_vec(g)] + wg + wu,
                                        [(D, BF16), (N_CHIPS, h, BF16), (N_CHIPS, h, BF16), (N_CHIPS, h, BF16)], [],
                                        body)
    return xn, gate, up, hid


def _ffn_bwd_hidden(dy, wd, gate, up, token=None, *, name, tm=256):
    S = dy.shape[0]
    h = wd[0][1][1]

    def body(r, f, o, acc, s):
        dyv = r[0][...]
        if token is not None:
            dyv = dyv + jnp.sum(f[N_CHIPS][...])
        dyb = dyv.astype(BF16)
        for p in range(N_CHIPS):
            dh = _dot_nt(dyb, f[p][...])
            gv = r[1][p].astype(F32)
            sg = _sigmoid(gv)
            o[0][p] = (dh * r[2][p].astype(F32) * (sg * (1.0 + gv * (1.0 - sg)))).astype(BF16)
            o[1][p] = (dh * gv * sg).astype(BF16)

    (dg, du), _ = _rows_call(name, _tile(S, tm), [dy, gate, up], wd + ([] if token is None else [token]),
                             [(N_CHIPS, h, BF16), (N_CHIPS, h, BF16)], [], body)
    return dg, du


def _ffn_in_bwd(dg, du, wg, wu, x, g, dres, *, name, tm=256):
    S, D = x.shape

    def body(r, f, o, acc, s):
        tot = None
        for p in range(N_CHIPS):
            y = _dot_nt(r[0][p], f[1 + p][...]) + _dot_nt(r[1][p], f[1 + N_CHIPS + p][...])
            tot = y if tot is None else tot + y
        dx, dgn = _rms_bwd_tile(r[2][...], f[0][...], tot)
        o[0][...] = dx + r[3][...]
        acc[0][...] += dgn

    (dx,), (dgn,) = _rows_call(name, _tile(S, tm), [dg, du, x, dres], [_vec(g)] + wg + wu, [(D, F32)],
                               [((1, D), F32)], body)
    return dx, dgn


def _norm_bwd_k(da, ws, x, g, dres, *, name, tm=512):
    S, D = x.shape

    def body(r, f, o, acc, s):
        dx, dg = _rms_bwd_tile(r[1][...], f[0][...], _cat_nt(r[0][...].astype(BF16), f[1:]))
        o[0][...] = dx + r[2][...]
        acc[0][...] += dg

    (dx,), (dg,) = _rows_call(name, _tile(S, tm), [da, x, dres], [_vec(g)] + ws, [(D, F32)], [((1, D), F32)], body)
    return dx, dg


def _norm_bwd_n(das, ws, x, g, dres, *, name, tm=256):
    S, D = x.shape
    n = ws[0][1][2]

    def body(r, f, o, acc, s):
        tot = None
        for p in range(N_CHIPS):
            y = _dot_nt(r[p // 2][:, _cols(p % 2, n)], f[1 + p][...])
            tot = y if tot is None else tot + y
        dx, dg = _rms_bwd_tile(r[2][...], f[0][...], tot)
        o[0][...] = dx + r[3][...]
        acc[0][...] += dg

    (dx,), (dg,) = _rows_call(name, _tile(S, tm), list(das) + [x, dres], [_vec(g)] + ws, [(D, F32)], [((1, D), F32)],
                              body)
    return dx, dg


def _ln_stats(v):
    mu = _mean(v)
    xc = v - mu
    rstd = lax.rsqrt(_mean(xc * xc) + EPS)
    return xc * rstd, rstd


_SHIFTS = 8
_CONV_ROWS = 64


def _fill_shifts(sh_ref, ext_ref, tm):
    sh_ref[0] = ext_ref[...]
    for s in range(1, _SHIFTS):
        sh_ref[s, 0:tm + CONV_HALO - _SHIFTS, :] = ext_ref[pl.ds(s, tm + CONV_HALO - _SHIFTS), :]


def _window(sh_ref, off, tm):
    return sh_ref[off % _SHIFTS, pl.ds(off - off % _SHIFTS, tm), :]


def _even_fwd(proj, wm, bcol, cw, cb, lg, lb, *, name):
    S = proj.shape[0]
    tm = _tile(S, 256)
    hb = tm // CONV_HALO
    nblk = tm // GMLP_BLOCK

    def body(p_ref, halo_ref, wm_ref, b_ref, cw_ref, cb_ref, lg_ref, lb_ref, mix_ref, hc_ref, hext_ref, hsh_ref):
        i = pl.program_id(0)
        gu, _ = _gelu(p_ref[:, 0:A_WIDTH])
        gv, _ = _gelu(p_ref[:, A_WIDTH:2 * A_WIDTH])
        vn, _ = _ln_stats(gv)
        vnb = vn.astype(BF16)
        for n in range(nblk):
            rows = slice(n * GMLP_BLOCK, (n + 1) * GMLP_BLOCK)
            for g in range(A_GROUPS):
                cols = slice(g * GMLP_BLOCK, (g + 1) * GMLP_BLOCK)
                sg = jnp.dot(wm_ref[g], vnb[rows, cols], preferred_element_type=F32) + b_ref[g]
                mix_ref[rows, cols] = (gu[rows, cols] * sg).astype(BF16)
        h = p_ref[:, 1024:1536] * _sigmoid(p_ref[:, 1536:2048])
        hh = halo_ref[:, 0:B_WIDTH] * _sigmoid(halo_ref[:, B_WIDTH:2 * B_WIDTH])
        hext_ref[0:CONV_HALO, :] = jnp.where(i > 0, hh, 0.0)
        hext_ref[CONV_HALO:CONV_HALO + tm, :] = h
        _fill_shifts(hsh_ref, hext_ref, tm)
        for r0 in range(0, tm, _CONV_ROWS):
            acc = jnp.zeros((_CONV_ROWS, B_WIDTH), F32)
            for k in range(CONV_WIDTH):
                acc = acc + cw_ref[k:k + 1, :] * _window(hsh_ref, r0 + k + CONV_HALO - CONV_WIDTH + 1, _CONV_ROWS)
            hc_ref[r0:r0 + _CONV_ROWS, :] = acc + cb_ref[...]
        hc = hc_ref[...]
        hhat, _ = _ln_stats(hc)
        hl = hhat * lg_ref[...] + lb_ref[...]
        mix_ref[:, A_WIDTH:A_WIDTH + B_WIDTH] = (hl * _sigmoid(hl)).astype(BF16)

    vec = pl.BlockSpec((1, B_WIDTH), lambda i: (0, 0))
    return pl.pallas_call(
        body, name=name, grid=(S // tm,),
        in_specs=[
            pl.BlockSpec((tm, 2048), lambda i: (i, 0)),
            pl.BlockSpec((CONV_HALO, 1024), lambda i: (jnp.maximum(i * hb - 1, 0), 1)),
            pl.BlockSpec((A_GROUPS, GMLP_BLOCK, GMLP_BLOCK), lambda i: (0, 0, 0)),
            pl.BlockSpec((A_GROUPS, GMLP_BLOCK, 1), lambda i: (0, 0, 0)),
            pl.BlockSpec((CONV_HALO, B_WIDTH), lambda i: (0, 0)),
            vec, vec, vec,
        ],
        out_specs=[pl.BlockSpec((tm, 1024), lambda i: (i, 0)), pl.BlockSpec((tm, B_WIDTH), lambda i: (i, 0))],
        out_shape=[jax.ShapeDtypeStruct((S, 1024), BF16), jax.ShapeDtypeStruct((S, B_WIDTH), F32)],
        scratch_shapes=[pltpu.VMEM((tm + CONV_HALO, B_WIDTH), F32),
                        pltpu.VMEM((_SHIFTS, tm + CONV_HALO, B_WIDTH), F32)],
        compiler_params=_params(("parallel",)),
    )(proj, proj, wm, bcol, cw, cb, lg, lb)


def _even_bwd1(proj, dmix, hc, wm, wmt, bcol, lg, lb, *, name):
    S = proj.shape[0]
    tm = _tile(S, 256)
    nblk = tm // GMLP_BLOCK

    def body(p_ref, dm_ref, hc_ref, wm_ref, wmt_ref, b_ref, lg_ref, lb_ref,
             dpa_ref, dhc_ref, dwm_ref, db_ref, dlg_ref, dlb_ref, dcb_ref, dgu_ref, dvn_ref):
        @pl.when(pl.program_id(0) == 0)
        def _():
            dwm_ref[...] = jnp.zeros_like(dwm_ref)
            db_ref[...] = jnp.zeros_like(db_ref)
            dlg_ref[...] = jnp.zeros_like(dlg_ref)
            dlb_ref[...] = jnp.zeros_like(dlb_ref)
            dcb_ref[...] = jnp.zeros_like(dcb_ref)

        au = p_ref[:, 0:A_WIDTH]
        av = p_ref[:, A_WIDTH:2 * A_WIDTH]
        gu, tu = _gelu(au)
        gv, tv = _gelu(av)
        vn, rstd = _ln_stats(gv)
        vnb = vn.astype(BF16)
        for n in range(nblk):
            rows = slice(n * GMLP_BLOCK, (n + 1) * GMLP_BLOCK)
            for g in range(A_GROUPS):
                cols = slice(g * GMLP_BLOCK, (g + 1) * GMLP_BLOCK)
                vb = vnb[rows, cols]
                sg = jnp.dot(wm_ref[g], vb, preferred_element_type=F32) + b_ref[g]
                da = dm_ref[rows, cols]
                dsg = da * gu[rows, cols]
                dgu_ref[rows, cols] = da * sg
                dsgb = dsg.astype(BF16)
                dwm_ref[g] += _dot_nt(dsgb, vb)
                db_ref[g] += jnp.sum(dsg, axis=1, keepdims=True)
                dvn_ref[rows, cols] = jnp.dot(wmt_ref[g], dsgb, preferred_element_type=F32)
        dvn = dvn_ref[...]
        dgv = rstd * (dvn - _mean(dvn) - vn * _mean(dvn * vn))
        dpa_ref[:, 0:A_WIDTH] = (dgu_ref[...] * _gelu_grad(au, tu)).astype(BF16)
        dpa_ref[:, A_WIDTH:2 * A_WIDTH] = (dgv * _gelu_grad(av, tv)).astype(BF16)
        hhat, rstd2 = _ln_stats(hc_ref[...])
        lgv = lg_ref[...]
        hl = hhat * lgv + lb_ref[...]
        s = _sigmoid(hl)
        dhl = dm_ref[:, A_WIDTH:A_WIDTH + B_WIDTH] * (s * (1.0 + hl * (1.0 - s)))
        dlg_ref[...] += jnp.sum(dhl * hhat, axis=0, keepdims=True)
        dlb_ref[...] += jnp.sum(dhl, axis=0, keepdims=True)
        dhh = dhl * lgv
        dhc = rstd2 * (dhh - _mean(dhh) - hhat * _mean(dhh * hhat))
        dcb_ref[...] += jnp.sum(dhc, axis=0, keepdims=True)
        dhc_ref[...] = dhc

    vec = pl.BlockSpec((1, B_WIDTH), lambda i: (0, 0))
    w3 = pl.BlockSpec((A_GROUPS, GMLP_BLOCK, GMLP_BLOCK), lambda i: (0, 0, 0))
    b3 = pl.BlockSpec((A_GROUPS, GMLP_BLOCK, 1), lambda i: (0, 0, 0))
    return pl.pallas_call(
        body, name=name, grid=(S // tm,),
        in_specs=[
            pl.BlockSpec((tm, 1024), lambda i: (i, 0)),
            pl.BlockSpec((tm, 1024), lambda i: (i, 0)),
            pl.BlockSpec((tm, B_WIDTH), lambda i: (i, 0)),
            w3, w3, b3, vec, vec,
        ],
        out_specs=[pl.BlockSpec((tm, 1024), lambda i: (i, 0)), pl.BlockSpec((tm, B_WIDTH), lambda i: (i, 0)),
                   w3, b3, vec, vec, vec],
        out_shape=[
            jax.ShapeDtypeStruct((S, 1024), BF16), jax.ShapeDtypeStruct((S, B_WIDTH), F32),
            jax.ShapeDtypeStruct((A_GROUPS, GMLP_BLOCK, GMLP_BLOCK), F32),
            jax.ShapeDtypeStruct((A_GROUPS, GMLP_BLOCK, 1), F32),
            jax.ShapeDtypeStruct((1, B_WIDTH), F32), jax.ShapeDtypeStruct((1, B_WIDTH), F32),
            jax.ShapeDtypeStruct((1, B_WIDTH), F32),
        ],
        scratch_shapes=[pltpu.VMEM((tm, A_WIDTH), F32), pltpu.VMEM((tm, A_WIDTH), F32)],
        compiler_params=_params(("arbitrary",)),
    )(proj, dmix, hc, wm, wmt, bcol, lg, lb)


def _even_bwd2(proj, dhc, cw, *, name):
    S = proj.shape[0]
    tm = _tile(S, 256)
    hb = tm // CONV_HALO
    nt = S // tm
    last_halo = S // CONV_HALO - 1
    lo = CONV_HALO - CONV_WIDTH + 1

    def body(p_ref, halo_ref, d_ref, dnext_ref, cw_ref, dpb_ref, dcw_ref, hext_ref, dext_ref, hsh_ref, dsh_ref):
        i = pl.program_id(0)

        @pl.when(i == 0)
        def _():
            dcw_ref[...] = jnp.zeros_like(dcw_ref)

        hh = halo_ref[:, 0:B_WIDTH] * _sigmoid(halo_ref[:, B_WIDTH:2 * B_WIDTH])
        hext_ref[0:CONV_HALO, :] = jnp.where(i > 0, hh, 0.0)
        hext_ref[CONV_HALO:CONV_HALO + tm, :] = p_ref[:, 0:B_WIDTH] * _sigmoid(p_ref[:, B_WIDTH:2 * B_WIDTH])
        dext_ref[0:tm, :] = d_ref[...]
        dext_ref[tm:tm + CONV_HALO, :] = jnp.where(i < nt - 1, dnext_ref[...], 0.0)
        _fill_shifts(hsh_ref, hext_ref, tm)
        _fill_shifts(dsh_ref, dext_ref, tm)
        for r0 in range(0, tm, _CONV_ROWS):
            rows = slice(r0, r0 + _CONV_ROWS)
            dhc_b = d_ref[rows, :]
            dh = jnp.zeros((_CONV_ROWS, B_WIDTH), F32)
            for k in range(CONV_WIDTH):
                dh = dh + cw_ref[k:k + 1, :] * _window(dsh_ref, r0 + CONV_WIDTH - 1 - k, _CONV_ROWS)
                dcw_ref[k:k + 1, :] += jnp.sum(dhc_b * _window(hsh_ref, r0 + k + lo, _CONV_ROWS), axis=0,
                                               keepdims=True)
            ba_b = p_ref[rows, 0:B_WIDTH]
            sg_b = _sigmoid(p_ref[rows, B_WIDTH:2 * B_WIDTH])
            dpb_ref[rows, 0:B_WIDTH] = (dh * sg_b).astype(BF16)
            dpb_ref[rows, B_WIDTH:2 * B_WIDTH] = (dh * ba_b * sg_b * (1.0 - sg_b)).astype(BF16)

    return pl.pallas_call(
        body, name=name, grid=(nt,),
        in_specs=[
            pl.BlockSpec((tm, 1024), lambda i: (i, 1)),
            pl.BlockSpec((CONV_HALO, 1024), lambda i: (jnp.maximum(i * hb - 1, 0), 1)),
            pl.BlockSpec((tm, B_WIDTH), lambda i: (i, 0)),
            pl.BlockSpec((CONV_HALO, B_WIDTH), lambda i: (jnp.minimum((i + 1) * hb, last_halo), 0)),
            pl.BlockSpec((CONV_HALO, B_WIDTH), lambda i: (0, 0)),
        ],
        out_specs=[pl.BlockSpec((tm, 1024), lambda i: (i, 0)), pl.BlockSpec((CONV_HALO, B_WIDTH), lambda i: (0, 0))],
        out_shape=[jax.ShapeDtypeStruct((S, 1024), BF16), jax.ShapeDtypeStruct((CONV_HALO, B_WIDTH), F32)],
        scratch_shapes=[pltpu.VMEM((tm + CONV_HALO, B_WIDTH), F32), pltpu.VMEM((tm + CONV_HALO, B_WIDTH), F32),
                        pltpu.VMEM((_SHIFTS, tm + CONV_HALO, B_WIDTH), F32),
                        pltpu.VMEM((_SHIFTS, tm + CONV_HALO, B_WIDTH), F32)],
        compiler_params=_params(("arbitrary",)),
    )(proj, proj, dhc, dhc, cw)


_CA_SCALE = CA_HEAD_DIM ** -0.5


def _softmax_rows(s):
    e = jnp.exp(s - jnp.max(s, axis=-1, keepdims=True))
    return e / jnp.sum(e, axis=-1, keepdims=True)


def _attn_fwd(q, k, v, *, name):
    S = q.shape[0]

    def body(r, f, o, acc, s):
        for h in range(CA_HEADS):
            cols = _cols(h, CA_HEAD_DIM)
            p = _softmax_rows(_dot_nt(r[0][:, cols], f[0][:, cols]) * _CA_SCALE)
            o[0][:, cols] = _dot(p.astype(BF16), f[1][:, cols]).astype(BF16)

    (o_,), _ = _rows_call(name, _tile(S, 512), [q], [k, v], [(D_MODEL, BF16)], [], body)
    return o_


def _attn_bwd(dy, wo, q, k, v, *, name):
    S = q.shape[0]
    M = k.shape[0]

    def body(r, f, o, acc, s):
        dyb = r[0][...].astype(BF16)
        for h in range(CA_HEADS):
            cols = _cols(h, CA_HEAD_DIM)
            qh = r[1][:, cols]
            kh = f[0][:, cols]
            vh = f[1][:, cols]
            doh = _dot_nt(dyb, f[2 + h][...]).astype(BF16)
            p = _softmax_rows(_dot_nt(qh, kh) * _CA_SCALE)
            acc[1][:, cols] += _dot_tn(p.astype(BF16), doh)
            dp = _dot_nt(doh, vh)
            ds = (p * (dp - jnp.sum(dp * p, axis=-1, keepdims=True)) * _CA_SCALE).astype(BF16)
            o[0][:, cols] = _dot(ds, kh).astype(BF16)
            acc[0][:, cols] += _dot_tn(ds, qh)

    (dq,), (dk, dv) = _rows_call(name, _tile(S, 512), [dy, q], [k, v] + wo, [(D_MODEL, BF16)],
                                 [((M, D_MODEL), F32), ((M, D_MODEL), F32)], body)
    return dq, dk, dv


_STATE_TILE = 2 * STATE_ROWS


def _state_cols(ref, tm):
    return jnp.concatenate([ref[:, c, :, :].reshape(tm, STATE_LANES).astype(BF16) for c in range(_STATE_TILE)],
                           axis=1)


def _put_state_cols(ref, y, tm):
    for c in range(_STATE_TILE):
        ref[:, c, :, :] = y[:, _cols(c, STATE_LANES)].reshape(tm // 8, 8, STATE_LANES)


def _mm_to_state(a, w, *, nt=False, name, tm=256):
    S = a.shape[0]
    tm = _tile(S, tm)

    def body(r, f, o, acc, s):
        av = r[0][...].astype(BF16)
        _put_state_cols(o[0], _dot_nt(av, f[0][...]) if nt else _dot(av, f[0][...]), tm)

    (y,), _ = _rows_call(name, tm, [a], [w], [("state", F32)], [], body)
    return y


def _s5_readout(xs, cd, u, d, *, name, tm=256):
    tm = _tile(u.shape[0], tm)

    def body(r, f, o, acc, s):
        y = _dot(_state_cols(r[0], tm), f[0][...]) + f[1][...] * r[1][...]
        o[0][...] = y
        o[1][...] = _gelu(y)[0].astype(BF16)

    (y, yg), _ = _rows_call(name, tm, [xs, u], [cd, d], [(C_WIDTH, F32), (C_WIDTH, BF16)], [], body)
    return y, yg


def _state_grad_tn(a, b, *, name, ts=256):
    a_state, b_state = a.ndim == 4, b.ndim == 4
    S = b.shape[0] if a_state else a.shape[0]
    ts = _tile(S, ts)
    K1 = 2 * N_STATE if a_state else a.shape[1]
    N = 2 * N_STATE if b_state else b.shape[1]

    def body(r, f, o, acc, s):
        av = _state_cols(r[0], ts) if a_state else r[0][...].astype(BF16)
        bv = _state_cols(r[1], ts) if b_state else r[1][...].astype(BF16)
        acc[0][...] += _dot_tn(av, bv)

    _, (out,) = _rows_call(name, ts, [a, b], [], [], [((K1, N), F32)], body)
    return out


def _glu_out(yg, ws, x, *, name, tm=512):
    n = ws[0][1][2]

    def body(r, f, o, acc, s):
        ygv = r[0][...]
        ov = [_dot(ygv, f[p][...]) for p in range(N_CHIPS)]
        for p in range(N_CHIPS):
            o[0][:, _cols(p, n)] = ov[p].astype(BF16)
        for p in range(2):
            o[1][:, _cols(p, n)] = r[1][:, _cols(p, n)] + ov[p] * _sigmoid(ov[2 + p])

    (o_, y), _ = _rows_call(name, _tile(x.shape[0], tm), [yg, x], ws, [(2 * D_MODEL, BF16), (D_MODEL, F32)], [], body)
    return o_, y


def _glu_out_bwd(o_, dy, ws, y, u, d, *, name, tm=256):
    n = ws[0][1][2]

    def body(r, f, o, acc, s):
        o1 = r[0][:, 0:D_MODEL].astype(F32)
        sg = _sigmoid(r[0][:, D_MODEL:2 * D_MODEL].astype(F32))
        dyv = r[1][...]
        do1 = (dyv * sg).astype(BF16)
        do2 = (dyv * o1 * sg * (1.0 - sg)).astype(BF16)
        o[0][:, 0:D_MODEL] = do1
        o[0][:, D_MODEL:2 * D_MODEL] = do2
        dyg = None
        for p in range(N_CHIPS):
            t = _dot_nt((do1 if p < 2 else do2)[:, _cols(p % 2, n)], f[1 + p][...])
            dyg = t if dyg is None else dyg + t
        yv = r[2][...]
        dys = dyg * _gelu_grad(yv, _gelu(yv)[1])
        o[1][...] = dys.astype(BF16)
        o[2][...] = f[0][...] * dys
        acc[0][...] += jnp.sum(dys * r[3][...], axis=0, keepdims=True)

    (do, dys, dus), (dd,) = _rows_call(name, _tile(dy.shape[0], tm), [o_, dy, y, u], [d] + ws,
                                       [(2 * D_MODEL, BF16), (C_WIDTH, BF16), (C_WIDTH, F32)], [((1, C_WIDTH), F32)],
                                       body)
    return do, dys, dus, dd


def _s5_in_bwd(gs, bd, dus, ws, x, g, dres, *, name, tm=256):
    D = x.shape[1]
    tm = _tile(x.shape[0], tm)

    def body(r, f, o, acc, s):
        du = (_dot_nt(_state_cols(r[0], tm), f[1][...]) + r[1][...]).astype(BF16)
        o[0][...] = du
        dx, dg = _rms_bwd_tile(r[2][...], f[0][...], _cat_nt(du, f[2:]))
        o[1][...] = dx + r[3][...]
        acc[0][...] += dg

    (du, dx), (dg,) = _rows_call(name, tm, [gs, dus, x, dres], [_vec(g), bd] + ws,
                                 [(C_WIDTH, BF16), (D, F32)], [((1, D), F32)], body)
    return du, dx, dg


_SCAN_CHUNK = 128
_RE = slice(0, STATE_ROWS)
_IM = slice(STATE_ROWS, 2 * STATE_ROWS)
assert SCAN_BLOCK == 8


def _token(g, i):
    return pl.ds(pl.multiple_of(g * (_STATE_TILE * SCAN_BLOCK), _STATE_TILE * SCAN_BLOCK) + i, _STATE_TILE,
                 stride=SCAN_BLOCK)


def _scan_fwd(bu, pw, *, name):
    S = bu.shape[0] // _STATE_TILE
    tc = _tile(S, _SCAN_CHUNK, 8)

    def body(bu_ref, pw_ref, xs_ref, st_ref):
        @pl.when(pl.program_id(0) == 0)
        def _():
            st_ref[...] = jnp.zeros_like(st_ref)

        ar = pw_ref[0, _RE, :]
        ai = pw_ref[0, _IM, :]

        def block(g, carry):
            xr, xi = carry
            cr = ci = nr = ni = None
            for j in range(SCAN_BLOCK):
                b = bu_ref[_token(g, j), :]
                br, bi = b[_RE], b[_IM]
                cr, ci = (br, bi) if j == 0 else (ar * cr - ai * ci + br, ar * ci + ai * cr + bi)
                pr, pi = pw_ref[j, _RE, :], pw_ref[j, _IM, :]
                nr = pr * xr - pi * xi + cr
                ni = pr * xi + pi * xr + ci
                xs_ref[_token(g, j), :] = jnp.concatenate([nr, ni], axis=0)
            return nr, ni

        xr, xi = lax.fori_loop(0, tc // SCAN_BLOCK, block, (st_ref[_RE, :], st_ref[_IM, :]), unroll=2)
        st_ref[_RE, :] = xr
        st_ref[_IM, :] = xi

    blk = pl.BlockSpec((tc * _STATE_TILE, STATE_LANES), lambda i: (i, 0))
    return pl.pallas_call(
        body, name=name, grid=(S // tc,),
        in_specs=[blk, pl.BlockSpec(pw.shape, lambda i: (0, 0, 0))], out_specs=blk,
        out_shape=jax.ShapeDtypeStruct(bu.shape, F32),
        scratch_shapes=[pltpu.VMEM((2 * STATE_ROWS, STATE_LANES), F32)],
        compiler_params=_params(("arbitrary",)),
    )(bu, pw)


def _scan_bwd(dxs, xs, pw, *, name):
    S = dxs.shape[0] // _STATE_TILE
    tc = _tile(S, _SCAN_CHUNK, 8)
    nc = S // tc

    def body(dx_ref, xs_ref, pw_ref, g_ref, da_ref, st_ref):
        @pl.when(pl.program_id(0) == 0)
        def _():
            st_ref[...] = jnp.zeros_like(st_ref)
            da_ref[...] = jnp.zeros_like(da_ref)

        ar = pw_ref[0, _RE, :]
        ai = pw_ref[0, _IM, :]

        def block(i, carry):
            gr, gi, dar, dai = carry
            g = tc // SCAN_BLOCK - 1 - i
            cr = ci = None
            pgr, pgi = gr, gi
            for j in range(SCAN_BLOCK):
                tok = _token(g, SCAN_BLOCK - 1 - j)
                x = xs_ref[tok, :]
                xr, xi = x[_RE], x[_IM]
                dar = dar + pgr * xr + pgi * xi
                dai = dai + pgi * xr - pgr * xi
                d = dx_ref[tok, :]
                dr, di = d[_RE], d[_IM]
                cr, ci = (dr, di) if j == 0 else (ar * cr + ai * ci + dr, ar * ci - ai * cr + di)
                pr, pi = pw_ref[j, _RE, :], pw_ref[j, _IM, :]
                pgr = pr * gr + pi * gi + cr
                pgi = pr * gi - pi * gr + ci
                g_ref[tok, :] = jnp.concatenate([pgr, pgi], axis=0)
            return pgr, pgi, dar, dai

        init = (st_ref[_RE, :], st_ref[_IM, :], da_ref[_RE, :], da_ref[_IM, :])
        gr, gi, dar, dai = lax.fori_loop(0, tc // SCAN_BLOCK, block, init, unroll=2)
        st_ref[_RE, :] = gr
        st_ref[_IM, :] = gi
        da_ref[_RE, :] = dar
        da_ref[_IM, :] = dai

    blk = pl.BlockSpec((tc * _STATE_TILE, STATE_LANES), lambda i: (nc - 1 - i, 0))
    vec = pl.BlockSpec((2 * STATE_ROWS, STATE_LANES), lambda i: (0, 0))
    return pl.pallas_call(
        body, name=name, grid=(nc,), in_specs=[blk, blk, pl.BlockSpec(pw.shape, lambda i: (0, 0, 0))],
        out_specs=[blk, vec],
        out_shape=[jax.ShapeDtypeStruct(dxs.shape, F32), jax.ShapeDtypeStruct((2 * STATE_ROWS, STATE_LANES), F32)],
        scratch_shapes=[pltpu.VMEM((2 * STATE_ROWS, STATE_LANES), F32)],
        compiler_params=_params(("arbitrary",)),
    )(dxs, xs, pw)


def _loss_head(x, g, target, *, name):
    S, D = x.shape

    def body(r, f, o, acc, s):
        xv = r[0][...]
        gv = f[0][...]
        rs = lax.rsqrt(_mean(xv * xv) + EPS)
        xh = xv * rs
        err = xh * gv - r[1][...]
        acc[1][...] += 0.5 * jnp.sum(_mean(err * err), axis=0, keepdims=True)
        dy = err * (1.0 / D)
        dyg = dy * gv
        o[0][...] = rs * (dyg - xh * _mean(dyg * xh))
        acc[0][...] += jnp.sum(dy * xh, axis=0, keepdims=True)

    (dx,), (dg, loss) = _rows_call(name, _tile(S, 256, 8), [x, target], [_vec(g)], [(D, F32)],
                                   [((1, D), F32), ((1, 128), F32)], body)
    return dx, dg, loss


_ADAM_C1 = 1.0 - ADAM_B1 ** ADAM_STEP
_ADAM_C2 = 1.0 - ADAM_B2 ** ADAM_STEP
_ONE_BLOCK_BYTES = 8 * 1024 * 1024


def _adamw_math(w, g, m, v):
    nm = ADAM_B1 * m + (1.0 - ADAM_B1) * g
    nv = ADAM_B2 * v + (1.0 - ADAM_B2) * (g * g)
    m_hat = nm / _ADAM_C1
    v_hat = nv / _ADAM_C2
    return -ADAM_LR * (m_hat / (jnp.sqrt(v_hat) + ADAM_EPS) + ADAM_WD * w), nm, nv


def _adamw_shard(w, gsrc, m, v, *, name):
    R, C = w.shape
    n_l = len(gsrc)
    rows = R // n_l
    tr = rows
    for _, r0 in gsrc:
        tr = math.gcd(tr, r0) if r0 else tr
    tr = _tile(tr, 256, 8) if tr > 256 else tr
    nb = rows // tr
    assert rows % tr == 0 and all(r0 % tr == 0 for _, r0 in gsrc)

    def body(*refs):
        w_ref, g_refs, (m_ref, v_ref, go_ref, d_ref, nm_ref, nv_ref) = refs[0], refs[1:1 + n_l], refs[1 + n_l:]
        layer = pl.program_id(0) // nb
        gv = g_refs[0][...]
        for l in range(1, n_l):
            gv = jnp.where(layer == l, g_refs[l][...], gv)
        go_ref[...] = gv
        d_ref[...], nm_ref[...], nv_ref[...] = _adamw_math(w_ref[...], gv, m_ref[...], v_ref[...])

    def g_spec(l, r0):
        return pl.BlockSpec((tr, C), lambda i: (r0 // tr + jnp.clip(i - l * nb, 0, nb - 1), 0))

    blk = pl.BlockSpec((tr, C), lambda i: (i, 0))
    out = jax.ShapeDtypeStruct((R, C), F32)
    return pl.pallas_call(
        body, name=name, grid=(R // tr,),
        in_specs=[blk] + [g_spec(l, r0) for l, (_, r0) in enumerate(gsrc)] + [blk, blk], out_specs=[blk] * 4,
        out_shape=[out] * 4, compiler_params=_params(("parallel",)),
    )(w, *[g for g, _ in gsrc], m, v)


def _adamw_small(ws, gs, ms, vs, *, name):
    n = len(ws)

    def body(*refs):
        w_r, g_r, m_r, v_r = refs[:n], refs[n:2 * n], refs[2 * n:3 * n], refs[3 * n:4 * n]
        d_r, nm_r, nv_r = refs[4 * n:5 * n], refs[5 * n:6 * n], refs[6 * n:7 * n]
        for k in range(n):
            d_r[k][...], nm_r[k][...], nv_r[k][...] = _adamw_math(w_r[k][...], g_r[k][...], m_r[k][...], v_r[k][...])

    vm = pl.BlockSpec(memory_space=pltpu.VMEM)
    out = [jax.ShapeDtypeStruct(w.shape, F32) for w in ws]
    res = pl.pallas_call(body, name=name, in_specs=[vm] * (4 * n), out_specs=[vm] * (3 * n), out_shape=out * 3,
                         compiler_params=pltpu.CompilerParams(vmem_limit_bytes=VMEM_LIMIT))(*ws, *gs, *ms, *vs)
    return res[:n], res[n:2 * n], res[2 * n:]


def _sum_slots(x, *, name):
    n, R, C = x.shape
    tr = R if (n + 1) * R * C * 4 <= _ONE_BLOCK_BYTES else _tile(R, 256, 8)

    def body(x_ref, o_ref):
        acc = x_ref[0]
        for k in range(1, n):
            acc = acc + x_ref[k]
        o_ref[...] = acc

    return pl.pallas_call(
        body, name=name, grid=(R // tr,),
        in_specs=[pl.BlockSpec((n, tr, C), lambda i: (0, i, 0))], out_specs=pl.BlockSpec((tr, C), lambda i: (i, 0)),
        out_shape=jax.ShapeDtypeStruct((R, C), F32), compiler_params=_params(("parallel",)),
    )(x)


def _pair_sum(g, r, half, *, name):
    n, R, C = g.shape
    Rh = R // 2
    tr = _tile(Rh, 256, 8)
    nb = Rh // tr

    def body(half_ref, g_ref, r_ref, o_ref):
        o_ref[...] = (g_ref[...] + r_ref[...]).astype(BF16)

    return pl.pallas_call(
        body, name=name,
        grid_spec=pltpu.PrefetchScalarGridSpec(
            num_scalar_prefetch=1, grid=(n, nb),
            in_specs=[pl.BlockSpec((1, tr, C), lambda p, i, h: (p, h[0] * nb + i, 0)),
                      pl.BlockSpec((1, tr, C), lambda p, i, h: (p, i, 0))],
            out_specs=pl.BlockSpec((1, tr, C), lambda p, i, h: (p, i, 0)),
        ),
        out_shape=jax.ShapeDtypeStruct((n, Rh, C), BF16), compiler_params=_params(("parallel", "parallel")),
    )(half, g, r)


def _chip_sum(g, r, slots, where, *, name):
    n, R, C = g.shape
    Rh = R // 2
    tr = _tile(Rh, 256, 8)
    nb = Rh // tr

    def body(w_ref, g_ref, r_ref, s_ref, o_ref):
        acc = g_ref[0] + r_ref[0]
        for k in range(slots.shape[0]):
            acc = acc + s_ref[k].astype(F32)
        o_ref[...] = acc

    return pl.pallas_call(
        body, name=name,
        grid_spec=pltpu.PrefetchScalarGridSpec(
            num_scalar_prefetch=1, grid=(nb,),
            in_specs=[pl.BlockSpec((1, tr, C), lambda i, w: (w[0], w[1] * nb + i, 0)),
                      pl.BlockSpec((1, tr, C), lambda i, w: (w[0], i, 0)),
                      pl.BlockSpec((slots.shape[0], tr, C), lambda i, w: (0, i, 0))],
            out_specs=pl.BlockSpec((tr, C), lambda i, w: (w[1] * nb + i, 0)),
        ),
        out_shape=jax.ShapeDtypeStruct((R, C), F32), compiler_params=_params(("parallel",)),
    )(where, g, r, slots)


ANY = pl.BlockSpec(memory_space=pl.ANY)


def _place():
    return lax.axis_index("x"), lax.axis_index("y"), lax.axis_index("c")


def _other_chips(x, y):
    return [(1 - x, y), (x, 1 - y), (1 - x, 1 - y)]


def _allgather_small(v, *, name):
    R, C = v.shape

    def body(x_ref, out_ref, send_sems, recv_sems, local_sem):
        x, y, c = _place()
        me, sibling = (x, y, c), (x, y, 1 - c)
        chips = _other_chips(x, y)

        def rows(px, py, pc):
            return out_ref.at[pl.ds((4 * px + 2 * py + pc) * R, R), :]

        def copy(k, block, to, src=None):
            return pltpu.make_async_remote_copy(
                src_ref=rows(*block) if src is None else src, dst_ref=rows(*block),
                send_sem=send_sems.at[k], recv_sem=recv_sems.at[k], device_id=to, device_id_type=MESH)

        mine = pltpu.make_async_copy(x_ref, rows(*me), local_sem)
        mine.start()
        first = [copy(0, me, sibling, src=x_ref)]
        first += [copy(1 + j, me, (*chip, c), src=x_ref) for j, chip in enumerate(chips)]
        for cp in first:
            cp.start()
        passed = [copy(4 + j, (*chip, c), sibling) for j, chip in enumerate(chips)]
        for j, chip in enumerate(chips):
            copy(1 + j, (*chip, c), me).wait_recv()
            passed[j].start()
        copy(0, sibling, me).wait_recv()
        for j, chip in enumerate(chips):
            copy(4 + j, (*chip, 1 - c), me).wait_recv()
        for cp in first + passed:
            cp.wait_send()
        mine.wait()

    return pl.pallas_call(
        body, name=name, out_shape=jax.ShapeDtypeStruct((N_DEV * R, C), v.dtype),
        in_specs=[pl.BlockSpec(memory_space=pltpu.VMEM)], out_specs=pl.BlockSpec(memory_space=pltpu.VMEM),
        scratch_shapes=[pltpu.SemaphoreType.DMA((7,)), pltpu.SemaphoreType.DMA((7,)), pltpu.SemaphoreType.DMA],
        compiler_params=pltpu.CompilerParams(vmem_limit_bytes=VMEM_LIMIT),
    )(v)


def _aliased_comm_call(body, bufs, n_sems, *, name):
    n = len(bufs)
    return pl.pallas_call(
        body, name=name, out_shape=[jax.ShapeDtypeStruct(b.shape, b.dtype) for b in bufs],
        in_specs=[ANY] * n, out_specs=[ANY] * n, input_output_aliases={k: k for k in range(n)},
        scratch_shapes=[pltpu.SemaphoreType.DMA((n_sems,)), pltpu.SemaphoreType.DMA((n_sems,))],
    )(*bufs)


HBM = pl.BlockSpec(memory_space=pltpu.HBM)
SEM = pl.BlockSpec(memory_space=pltpu.SEMAPHORE)
_SPLIT = pltpu.CompilerParams(has_side_effects=pltpu.SideEffectType.DATAFLOW_SIDE_EFFECTING)


def _in_hbm(arrs):
    return [pltpu.with_memory_space_constraint(a, pltpu.HBM) for a in arrs]


def _gather_ici_start(bufs, after, *, name):
    n = len(bufs)

    def body(*refs):
        send_sems, recv_sems, outs, token = refs[n + 1], refs[n + 2], refs[n + 3:2 * n + 3], refs[2 * n + 3]
        x, y, c = _place()
        for b in range(n):
            rh = bufs[b].shape[1] // 2
            part = outs[b].at[2 * x + y, pl.ds(c * rh, rh), :]
            for j, chip in enumerate(_other_chips(x, y)):
                pltpu.make_async_remote_copy(src_ref=part, dst_ref=part, send_sem=send_sems.at[3 * b + j],
                                             recv_sem=recv_sems.at[3 * b + j], device_id=(*chip, c),
                                             device_id_type=MESH).start()
        token[...] = jnp.zeros_like(token)

    res = pl.pallas_call(
        body, name=name,
        out_shape=(pltpu.SemaphoreType.DMA((3 * n,)), pltpu.SemaphoreType.DMA((3 * n,)),
                   *[pltpu.HBM(b.shape, b.dtype) for b in bufs], jax.ShapeDtypeStruct((8, 128), F32)),
        in_specs=[HBM] * n + [ANY], out_specs=(SEM, SEM, *[HBM] * n, pl.BlockSpec(memory_space=pltpu.VMEM)),
        input_output_aliases={k: k + 2 for k in range(n)}, compiler_params=_SPLIT,
    )(*_in_hbm(bufs), after)
    return res[0], res[1], list(res[2:2 + n]), res[2 + n]


def _gather_ici_wait(send_sems, recv_sems, bufs, after, *, name):
    n = len(bufs)

    def body(*refs):
        ins, ss, rs = refs[:n], refs[n], refs[n + 1]
        x, y, c = _place()
        for b in range(n):
            rh = bufs[b].shape[1] // 2
            mine = ins[b].at[2 * x + y, pl.ds(c * rh, rh), :]
            for j, (cx, cy) in enumerate(_other_chips(x, y)):
                theirs = ins[b].at[2 * cx + cy, pl.ds(c * rh, rh), :]
                cp = pltpu.make_async_remote_copy(src_ref=mine, dst_ref=theirs, send_sem=ss.at[3 * b + j],
                                                  recv_sem=rs.at[3 * b + j], device_id=(cx, cy, c),
                                                  device_id_type=MESH)
                cp.wait_send()
                cp.wait_recv()

    return list(pl.pallas_call(
        body, name=name, out_shape=[pltpu.HBM(b.shape, b.dtype) for b in bufs],
        in_specs=[HBM] * n + [SEM, SEM, ANY], out_specs=[HBM] * n,
        input_output_aliases={k: k for k in range(n)}, compiler_params=_SPLIT,
    )(*bufs, send_sems, recv_sems, after))


def _gather_forward(bufs, *, name):
    n = len(bufs)

    def body(*refs):
        outs, send_sems, recv_sems = refs[n:2 * n], refs[2 * n], refs[2 * n + 1]
        x, y, c = _place()

        def copy(b, j, chip, hc):
            rh = bufs[b].shape[1] // 2
            part = outs[b].at[2 * chip[0] + chip[1], pl.ds(hc * rh, rh), :]
            return pltpu.make_async_remote_copy(src_ref=part, dst_ref=part, send_sem=send_sems.at[3 * b + j],
                                                recv_sem=recv_sems.at[3 * b + j], device_id=(x, y, 1 - c),
                                                device_id_type=MESH)

        sends = [copy(b, j, chip, c) for b in range(n) for j, chip in enumerate(_other_chips(x, y))]
        for cp in sends:
            cp.start()
        for b in range(n):
            for j, chip in enumerate(_other_chips(x, y)):
                copy(b, j, chip, 1 - c).wait_recv()
        for cp in sends:
            cp.wait_send()

    return _aliased_comm_call(body, bufs, 3 * n, name=name)


def _chip_exchange_start(hs, *, name):
    n = len(hs)
    lands = [lax.empty((3,) + h.shape[1:], h.dtype) for h in hs]

    def body(*refs):
        send_sems, recv_sems = refs[2 * n], refs[2 * n + 1]
        h_out, l_out, token = refs[2 * n + 2:3 * n + 2], refs[3 * n + 2:4 * n + 2], refs[4 * n + 2]
        x, y, c = _place()
        for b in range(n):
            for j, (cx, cy) in enumerate(_other_chips(x, y)):
                pltpu.make_async_remote_copy(src_ref=h_out[b].at[2 * cx + cy], dst_ref=l_out[b].at[j],
                                             send_sem=send_sems.at[3 * b + j], recv_sem=recv_sems.at[3 * b + j],
                                             device_id=(cx, cy, c), device_id_type=MESH).start()
        token[...] = jnp.zeros_like(token)

    res = pl.pallas_call(
        body, name=name,
        out_shape=(pltpu.SemaphoreType.DMA((3 * n,)), pltpu.SemaphoreType.DMA((3 * n,)),
                   *[pltpu.HBM(a.shape, a.dtype) for a in hs + lands], jax.ShapeDtypeStruct((8, 128), F32)),
        in_specs=[HBM] * (2 * n), out_specs=(SEM, SEM, *[HBM] * (2 * n), pl.BlockSpec(memory_space=pltpu.VMEM)),
        input_output_aliases={k: k + 2 for k in range(2 * n)}, compiler_params=_SPLIT,
    )(*_in_hbm(hs + lands))
    return res[0], res[1], list(res[2:2 + n]), list(res[2 + n:2 + 2 * n]), res[2 + 2 * n]


def _chip_exchange_wait(send_sems, recv_sems, hs, lands, after, *, name):
    n = len(hs)

    def body(*refs):
        h_in, l_in, ss, rs = refs[:n], refs[n:2 * n], refs[2 * n], refs[2 * n + 1]
        x, y, c = _place()
        for b in range(n):
            for j, (cx, cy) in enumerate(_other_chips(x, y)):
                cp = pltpu.make_async_remote_copy(src_ref=h_in[b].at[2 * cx + cy], dst_ref=l_in[b].at[j],
                                                  send_sem=ss.at[3 * b + j], recv_sem=rs.at[3 * b + j],
                                                  device_id=(cx, cy, c), device_id_type=MESH)
                cp.wait_send()
                cp.wait_recv()

    res = pl.pallas_call(
        body, name=name, out_shape=[pltpu.HBM(a.shape, a.dtype) for a in hs + lands],
        in_specs=[HBM] * (2 * n) + [SEM, SEM, ANY], out_specs=[HBM] * (2 * n),
        input_output_aliases={k: k for k in range(2 * n)}, compiler_params=_SPLIT,
    )(*hs, *lands, send_sems, recv_sems, after)
    return list(res[n:])


def _peers(x, y, c):
    return [((1 - x) if fx else x, (1 - y) if fy else y, (1 - c) if fc else c)
            for fx in (0, 1) for fy in (0, 1) for fc in (0, 1) if fx or fy or fc]


def _all_to_all_start(slab, after, *, name):
    land = lax.empty((N_DEV,) + slab.shape, slab.dtype)

    def body(slab_in, land_in, after_ref, send_sems, recv_sems, slab_out, land_out, token):
        x, y, c = _place()
        for k, peer in enumerate(_peers(x, y, c)):
            pltpu.make_async_remote_copy(src_ref=slab_out, dst_ref=land_out.at[4 * x + 2 * y + c],
                                         send_sem=send_sems.at[k], recv_sem=recv_sems.at[k], device_id=peer,
                                         device_id_type=MESH).start()
        token[...] = jnp.zeros_like(token)

    return pl.pallas_call(
        body, name=name,
        out_shape=(pltpu.SemaphoreType.DMA((N_DEV - 1,)), pltpu.SemaphoreType.DMA((N_DEV - 1,)),
                   pltpu.HBM(slab.shape, slab.dtype), pltpu.HBM(land.shape, land.dtype),
                   jax.ShapeDtypeStruct((8, 128), F32)),
        in_specs=[HBM, HBM, ANY], out_specs=(SEM, SEM, HBM, HBM, pl.BlockSpec(memory_space=pltpu.VMEM)),
        input_output_aliases={0: 2, 1: 3}, compiler_params=_SPLIT,
    )(*_in_hbm([slab, land]), after)


def _all_to_all_wait(send_sems, recv_sems, slab, land, after, *, name):
    def body(slab_in, land_in, ss, rs, after_ref, slab_out, land_out):
        x, y, c = _place()
        for k, (px, py, pc) in enumerate(_peers(x, y, c)):
            cp = pltpu.make_async_remote_copy(src_ref=slab_in, dst_ref=land_in.at[4 * px + 2 * py + pc],
                                              send_sem=ss.at[k], recv_sem=rs.at[k], device_id=(px, py, pc),
                                              device_id_type=MESH)
            cp.wait_send()
            cp.wait_recv()

    return pl.pallas_call(
        body, name=name, out_shape=[pltpu.HBM(slab.shape, slab.dtype), pltpu.HBM(land.shape, land.dtype)],
        in_specs=[HBM, HBM, SEM, SEM, ANY], out_specs=[HBM, HBM], input_output_aliases={0: 0, 1: 1},
        compiler_params=_SPLIT,
    )(slab, land, send_sems, recv_sems, after)


def _pair_exchange(gs, *, name):
    n = len(gs)

    def body(*refs):
        ins, outs, send_sems, recv_sems = refs[:n], refs[n:2 * n], refs[2 * n], refs[2 * n + 1]
        x, y, c = _place()
        cps = []
        for b in range(n):
            rh = gs[b].shape[1] // 2
            cps.append(pltpu.make_async_remote_copy(
                src_ref=ins[b].at[:, pl.ds((1 - c) * rh, rh), :], dst_ref=outs[b], send_sem=send_sems.at[b],
                recv_sem=recv_sems.at[b], device_id=(x, y, 1 - c), device_id_type=MESH))
        for cp in cps:
            cp.start()
        for cp in cps:
            cp.wait()

    return pl.pallas_call(
        body, name=name, out_shape=[jax.ShapeDtypeStruct((g.shape[0], g.shape[1] // 2, g.shape[2]), g.dtype) for g in gs],
        in_specs=[ANY] * n, out_specs=[ANY] * n,
        scratch_shapes=[pltpu.SemaphoreType.DMA((n,)), pltpu.SemaphoreType.DMA((n,))],
    )(*gs)


def _pair_share(ss, *, name):
    n = len(ss)

    def body(*refs):
        outs, send_sems, recv_sems = refs[n:2 * n], refs[2 * n], refs[2 * n + 1]
        x, y, c = _place()
        cps = []
        for b in range(n):
            rh = ss[b].shape[0] // 2
            mine = outs[b].at[pl.ds(c * rh, rh), :]
            cps.append(pltpu.make_async_remote_copy(src_ref=mine, dst_ref=mine, send_sem=send_sems.at[b],
                                                    recv_sem=recv_sems.at[b], device_id=(x, y, 1 - c),
                                                    device_id_type=MESH))
        for cp in cps:
            cp.start()
        for b, cp in enumerate(cps):
            rh = ss[b].shape[0] // 2
            theirs = outs[b].at[pl.ds((1 - c) * rh, rh), :]
            pltpu.make_async_remote_copy(src_ref=theirs, dst_ref=theirs, send_sem=send_sems.at[b],
                                         recv_sem=recv_sems.at[b], device_id=(x, y, 1 - c),
                                         device_id_type=MESH).wait_recv()
            cp.wait_send()

    return _aliased_comm_call(body, ss, n, name=name)


_SMALL_SHARDED = (("e_conv_w", 2), ("o_norm", 1), ("o_d", 1))
_REPLICATED = ("e_norm", "e_gmlp_w", "e_gmlp_b", "e_conv_b", "e_conv_ln_g", "e_conv_ln_b", "o_lam_re", "o_lam_im",
               "o_log_dt", "o_b_re", "o_b_im", "o_c_re", "o_c_im", "ca_norm", "ca_mem_norm", "ffn_norm", "final_norm")
_SMALL = tuple(n for n, _ in _SMALL_SHARDED) + _REPLICATED
_WEIGHTS = ("e_norm", "e_w_in", "e_gmlp_w", "e_gmlp_b", "e_conv_w", "e_conv_b", "e_conv_ln_g", "e_conv_ln_b",
            "e_w_out", "o_norm", "o_w_in", "o_lam_re", "o_lam_im", "o_log_dt", "o_b_re", "o_b_im", "o_c_re", "o_c_im",
            "o_d", "o_w_out", "ca_norm", "ca_mem_norm", "ca_wq", "ca_wk", "ca_wv", "ca_wo", "ffn_norm", "ffn_w_gate",
            "ffn_w_up", "ffn_w_down", "final_norm")


def _pack_rows(arrs, width, dtype, row_mult=8):
    parts, spans, r0 = [], [], 0
    for a in arrs:
        flat = a.reshape(-1).astype(dtype)
        rows = -(-flat.shape[0] // (width * row_mult)) * row_mult
        if rows * width != flat.shape[0]:
            flat = jnp.pad(flat, (0, rows * width - flat.shape[0]))
        parts.append(flat.reshape(rows, width))
        spans.append((r0, rows))
        r0 += rows
    return jnp.concatenate(parts, axis=0), spans


def _unpack_rows(slab, spans, shapes):
    out = []
    for (r0, rows), shp in zip(spans, shapes):
        n = math.prod(shp)
        out.append(slab[r0:r0 + rows].reshape(-1)[:n].reshape(shp))
    return out


def _two_d(a):
    return a.reshape(-1, a.shape[-1])


def _local_slab(local, slab, dtype):
    parts = sorted((r0, n, l) for n, (_, where) in _PLACE.items() for l, (s, r0) in enumerate(where) if s == slab)
    shards = [local[n] if len(_PLACE[n][1]) == 1 else local[n][l] for _, n, l in parts]
    return jnp.concatenate([_two_d(a).astype(dtype) for a in shards], axis=0)


def _block_diag(b, pattern):
    return jnp.einsum(pattern, b, jnp.eye(C_GROUPS, dtype=b.dtype))


def _s5_discretize(lam_re, lam_im, log_dt, b_re, b_im):
    dt = jnp.exp(log_dt)[:, None]
    mag = jnp.exp(lam_re * dt)
    ar = mag * jnp.cos(lam_im * dt)
    ai = mag * jnp.sin(lam_im * dt)
    den = lam_re * lam_re + lam_im * lam_im
    qr = ((ar - 1.0) * lam_re + ai * lam_im) / den
    qi = (ai * lam_re - (ar - 1.0) * lam_im) / den
    bbr = qr[..., None] * b_re - qi[..., None] * b_im
    bbi = qr[..., None] * b_im + qi[..., None] * b_re
    return ar, ai, bbr, bbi


def _attention_block(x, mem, W, w, i, tag):
    xn, q = _norm_mm(x, w["ca_norm"][i], _shards(W, "ca_wq", i), split="k", out_dtype=BF16, name=f"{tag}_q")
    memn = _rms_fwd(mem, w["ca_mem_norm"][i], name=f"{tag}_ca_memnorm")
    k = _mm_k(memn, _shards(W, "ca_wk", i), out_dtype=BF16, name=f"{tag}_k")
    v = _mm_k(memn, _shards(W, "ca_wv", i), out_dtype=BF16, name=f"{tag}_v")
    o = _attn_fwd(q, k, v, name=f"{tag}_attn")
    y = _mm_k(o, _shards(W, "ca_wo", i), add=x, name=f"{tag}_wo")
    return y, (x, xn, memn, q, k, v, o)


def _attention_block_bwd(dy, saved, mem, W, w, i, tag, G, grads, token=None):
    x, xn, memn, q, k, v, o = saved
    if token is not None:
        k = _behind(k, token)
    G = _grad_to_slab(G, "ca_wo", i, o, dy, a_cols=256, name=f"{tag}_dwo")
    dq, dk, dv = _attn_bwd(dy, _shards(W, "ca_wo", i), q, k, v, name=f"{tag}_attn_bwd")
    G = _grad_to_slab(G, "ca_wq", i, xn, dq, a_cols=256, name=f"{tag}_dwq")
    G = _grad_to_slab(G, "ca_wk", i, memn, dk, a_cols=256, name=f"{tag}_dwk")
    G = _grad_to_slab(G, "ca_wv", i, memn, dv, a_cols=256, name=f"{tag}_dwv")
    dmemn = _mm_k_t([(dk, _shards(W, "ca_wk", i)), (dv, _shards(W, "ca_wv", i))], name=f"{tag}_dmemn")
    dx, dg = _norm_bwd_k(dq, _shards(W, "ca_wq", i), x, w["ca_norm"][i], dy, name=f"{tag}_dq_norm_bwd")
    grads["ca_norm"][i] = dg[0]
    grads["ca_mem_norm"][i] = _rms_dg(mem, w["ca_mem_norm"][i], dmemn, name=f"{tag}_ca_memnorm_bwd")[0]
    return dx, G


def _ffn_block(x, W, w, i, tag):
    fn, gate, up, h = _ffn_up(x, w["ffn_norm"][i], _shards(W, "ffn_w_gate", i), _shards(W, "ffn_w_up", i),
                              name=f"{tag}_ffn_up")
    y = _mm_k(h, _shards(W, "ffn_w_down", i), add=x, name=f"{tag}_down")
    return y, (x, fn, gate, up, h)


def _ffn_block_bwd(dy, saved, W, w, i, tag, G, grads, token=None):
    x, fn, gate, up, h = saved
    G = _grad_to_slab(G, "ffn_w_down", i, h, dy, name=f"{tag}_dwd")
    dg, du = _ffn_bwd_hidden(dy, _shards(W, "ffn_w_down", i), gate, up, token, name=f"{tag}_ffn_bwd_hidden")
    G = _grad_to_slab(G, "ffn_w_gate", i, fn, dg, name=f"{tag}_dwg")
    G = _grad_to_slab(G, "ffn_w_up", i, fn, du, name=f"{tag}_dwu")
    dx, dgn = _ffn_in_bwd(dg, du, _shards(W, "ffn_w_gate", i), _shards(W, "ffn_w_up", i), x, w["ffn_norm"][i], dy,
                          name=f"{tag}_ffn_in_bwd")
    grads["ffn_norm"][i] = dgn[0]
    return dx, G


def _gmlp_mask():
    chunk = jnp.arange(GMLP_BLOCK) // CHUNK
    return chunk[None, :] <= chunk[:, None]


def _even_block(x, W, w, tag):
    hn, proj = _norm_mm(x, w["e_norm"][0], _shards(W, "e_w_in"), split="n", out_dtype=F32, name=f"{tag}_w_in")
    wm = jnp.where(_gmlp_mask()[None], w["e_gmlp_w"][0], 0.0).astype(BF16)
    bcol = w["e_gmlp_b"][0][:, :, None]
    cw = jnp.pad(w["e_conv_w"][0], ((0, CONV_HALO - CONV_WIDTH), (0, 0)))
    cb, lg, lb = w["e_conv_b"], w["e_conv_ln_g"], w["e_conv_ln_b"]
    mix, hc = _even_fwd(proj, wm, bcol, cw, cb, lg, lb, name=f"{tag}_mixers")
    y = _mm_k(mix, _shards(W, "e_w_out"), add=x, name=f"{tag}_w_out")
    return y, (x, hn, proj, mix, hc, wm, bcol, cw)


def _even_block_bwd(dy, saved, W, w, tag, G, grads):
    x, hn, proj, mix, hc, wm, bcol, cw = saved
    dmix = _mm_k_t([(dy, _shards(W, "e_w_out"))], name=f"{tag}_dmix")
    G = _grad_to_slab(G, "e_w_out", 0, mix, dy, a_cols=256, name=f"{tag}_dw_out")
    wmt = jnp.swapaxes(wm, 1, 2)
    dpa, dhc, dwm, db, dlg, dlb, dcb = _even_bwd1(proj, dmix, hc, wm, wmt, bcol, w["e_conv_ln_g"], w["e_conv_ln_b"],
                                                  name=f"{tag}_mixers_bwd1")
    dpb, dcw = _even_bwd2(proj, dhc, cw, name=f"{tag}_mixers_bwd2")
    grads["e_gmlp_w"] = jnp.where(_gmlp_mask()[None], dwm, 0.0)[None]
    grads["e_gmlp_b"] = db[:, :, 0][None]
    grads["e_conv_ln_g"], grads["e_conv_ln_b"], grads["e_conv_b"] = dlg, dlb, dcb
    grads["e_conv_w"] = dcw[:CONV_WIDTH][None]
    G = _grad_to_slab(G, "e_w_in", 0, hn, dpa, b_cols=512, chips=(0, 2), name=f"{tag}_dw_in_a")
    G = _grad_to_slab(G, "e_w_in", 0, hn, dpb, b_cols=512, chips=(2, 2), name=f"{tag}_dw_in_b")
    dx, dg = _norm_bwd_n((dpa, dpb), _shards(W, "e_w_in"), x, w["e_norm"][0], dy, name=f"{tag}_in_bwd")
    grads["e_norm"] = dg
    return dx, G


def _odd_block(x, W, w, tag):
    S = x.shape[0]
    hn, u = _norm_mm(x, w["o_norm"][0], _shards(W, "o_w_in"), split="k", out_dtype=F32, name=f"{tag}_w_in")
    disc_in = (w["o_lam_re"][0], w["o_lam_im"][0], w["o_log_dt"][0], w["o_b_re"][0], w["o_b_im"][0])
    (ar, ai, bbr, bbi), disc_vjp = jax.vjp(_s5_discretize, *disc_in)
    bd = jnp.concatenate([_block_diag(bbr, "gpc,gh->gchp").reshape(C_WIDTH, N_STATE),
                          _block_diag(bbi, "gpc,gh->gchp").reshape(C_WIDTH, N_STATE)], axis=1).astype(BF16)
    cd = jnp.concatenate([_block_diag(w["o_c_re"][0], "gcp,gh->gphc").reshape(N_STATE, C_WIDTH),
                          -_block_diag(w["o_c_im"][0], "gcp,gh->gphc").reshape(N_STATE, C_WIDTH)], axis=0).astype(BF16)
    powers, pr, pi = [], ar, ai
    for _ in range(SCAN_BLOCK):
        powers.append(jnp.concatenate([pr.reshape(STATE_ROWS, STATE_LANES), pi.reshape(STATE_ROWS, STATE_LANES)], 0))
        pr, pi = pr * ar - pi * ai, pr * ai + pi * ar
    pw = jnp.stack(powers, axis=0)
    state_rows = (S * _STATE_TILE, STATE_LANES)
    bu = _mm_to_state(u, bd, name=f"{tag}_bu")
    xs = _scan_fwd(bu.reshape(state_rows), pw, name=f"{tag}_scan").reshape(bu.shape)
    yv, yg = _s5_readout(xs, cd, u, w["o_d"], name=f"{tag}_readout")
    o, y = _glu_out(yg, _shards(W, "o_w_out"), x, name=f"{tag}_glu_out")
    return y, (x, hn, u, bd, cd, pw, xs, yv, yg, o, disc_vjp)


def _odd_block_bwd(dy, saved, W, w, tag, G, grads):
    x, hn, u, bd, cd, pw, xs, yv, yg, o, disc_vjp = saved
    S = x.shape[0]
    state_rows = (S * _STATE_TILE, STATE_LANES)
    do, dys, dus, dd = _glu_out_bwd(o, dy, _shards(W, "o_w_out"), yv, u, w["o_d"], name=f"{tag}_glu_out_bwd")
    G = _grad_to_slab(G, "o_w_out", 0, yg, do, b_cols=512, name=f"{tag}_dw_out")
    grads["o_d"] = dd
    dxs = _mm_to_state(dys, cd, nt=True, name=f"{tag}_dxs")
    dcd_t = _state_grad_tn(dys, xs, name=f"{tag}_dcd")
    gs, da = _scan_bwd(dxs.reshape(state_rows), xs.reshape(state_rows), pw, name=f"{tag}_scan_bwd")
    gs = gs.reshape(xs.shape)
    dbd = _state_grad_tn(u, gs, name=f"{tag}_dbd")
    du, dx, dg = _s5_in_bwd(gs, bd, dus, _shards(W, "o_w_in"), x, w["o_norm"][0], dy, name=f"{tag}_in_bwd")
    G = _grad_to_slab(G, "o_w_in", 0, hn, du, a_cols=256, name=f"{tag}_dw_in")
    grads["o_norm"] = dg
    eye = jnp.eye(C_GROUPS, dtype=F32)
    dcr = jnp.einsum("hcgp,gh->gcp", dcd_t[:, :N_STATE].reshape(C_GROUPS, C_GROUP_CH, C_GROUPS, C_STATE), eye)
    dci = -jnp.einsum("hcgp,gh->gcp", dcd_t[:, N_STATE:].reshape(C_GROUPS, C_GROUP_CH, C_GROUPS, C_STATE), eye)
    dbbr = jnp.einsum("gchp,gh->gpc", dbd[:, :N_STATE].reshape(C_GROUPS, C_GROUP_CH, C_GROUPS, C_STATE), eye)
    dbbi = jnp.einsum("gchp,gh->gpc", dbd[:, N_STATE:].reshape(C_GROUPS, C_GROUP_CH, C_GROUPS, C_STATE), eye)
    dar = da[:STATE_ROWS].reshape(C_GROUPS, C_STATE)
    dai = da[STATE_ROWS:].reshape(C_GROUPS, C_STATE)
    dlr, dli, dldt, dbr, dbi = disc_vjp((dar, dai, dbbr, dbbi))
    grads["o_lam_re"], grads["o_lam_im"], grads["o_log_dt"] = dlr[None], dli[None], dldt[None]
    grads["o_b_re"], grads["o_b_im"], grads["o_c_re"], grads["o_c_im"] = dbr[None], dbi[None], dcr[None], dci[None]
    return dx, G


def _behind(value, token):
    return value + token[0, 0].astype(value.dtype)


class _NoExchange:
    def __init__(self, W):
        self.W = W

    def first_weights(self, w):
        return self.W, w

    def weights(self, stage, after):
        return {}

    def after_layer1_backward(self, G):
        return None

    def after_ffn0_backward(self, G):
        return None


def _forward_backward(xs_, mems_, tgt, w, G, exchange):
    W, w = exchange.first_weights(w)
    x1, s_mix0 = _even_block(xs_, W, w, "l0")
    W = {**W, **exchange.weights(1, x1)}
    x2, s_att0 = _attention_block(x1, mems_, W, w, 0, "l0")
    W = {**W, **exchange.weights(2, x2)}
    x3, s_ffn0 = _ffn_block(x2, W, w, 0, "l0")
    W = {**W, **exchange.weights(3, x3)}
    x4, s_mix1 = _odd_block(x3, W, w, "l1")
    x5, s_att1 = _attention_block(x4, mems_, W, w, 1, "l1")
    x6, s_ffn1 = _ffn_block(x5, W, w, 1, "l1")
    dx, dfinal, loss_lanes = _loss_head(x6, w["final_norm"], tgt, name="loss_head")

    grads = {n: [None, None] for n in ("ca_norm", "ca_mem_norm", "ffn_norm")}
    grads["final_norm"] = dfinal[0]
    dx, G = _ffn_block_bwd(dx, s_ffn1, W, w, 1, "l1", G, grads)
    dx, G = _attention_block_bwd(dx, s_att1, mems_, W, w, 1, "l1", G, grads)
    dx, G = _odd_block_bwd(dx, s_mix1, W, w, "l1", G, grads)
    token = exchange.after_layer1_backward(G)
    dx, G = _ffn_block_bwd(dx, s_ffn0, W, w, 0, "l0", G, grads, token)
    token = exchange.after_ffn0_backward(G)
    dx, G = _attention_block_bwd(dx, s_att0, mems_, W, w, 0, "l0", G, grads, token)
    dx, G = _even_block_bwd(dx, s_mix0, W, w, "l0", G, grads)
    for n in list(grads):
        if isinstance(grads[n], list):
            grads[n] = jnp.stack(grads[n], axis=0)
        grads[n] = grads[n].reshape(w[n].shape)
    return loss_lanes, dx, G, grads


class _Exchange:
    def __init__(self, local, chip, core):
        self.bufs = {s: lax.dynamic_update_slice(lax.empty((N_CHIPS, rows, width), BF16),
                                                 _local_slab(local, s, BF16)[None], (chip, 0, 0))
                     for s, (width, rows) in _SLABS.items()}
        self.half = core.reshape(1).astype(jnp.int32)
        self.where = jnp.stack([chip, core]).astype(jnp.int32)
        self.flights = []

    def weights(self, stage, after):
        send_sems, recv_sems, bufs, _ = self.flights[stage]
        bufs = _gather_ici_wait(send_sems, recv_sems, bufs, after, name=f"gather_stage{stage}_wait")
        return dict(zip(_STAGES[stage], _gather_forward(bufs, name=f"gather_stage{stage}_forward")))

    def first_weights(self, w):
        after = w["e_conv_w"].reshape(-1)[:STATE_LANES]
        for k, stage in enumerate(_STAGES):
            self.flights.append(_gather_ici_start([self.bufs[s] for s in stage], after, name=f"gather_stage{k}_start"))
            after = self.flights[-1][3]
        return self.weights(0, after), {**w, "e_norm": _behind(w["e_norm"], after)}

    def reduce_start(self, G, slabs, tag):
        gl = [G[s] for s in slabs]
        other = _pair_exchange(gl, name=f"grad_{tag}_pair_exchange")
        pairs = [_pair_sum(g, r, self.half, name=f"grad_pair_sum_{s}") for s, g, r in zip(slabs, gl, other)]
        send_sems, recv_sems, pairs, lands, token = _chip_exchange_start(pairs, name=f"grad_{tag}_chip_start")
        return (slabs, gl, other, send_sems, recv_sems, pairs, lands), token

    def reduce_finish(self, state, after, tag):
        slabs, gl, other, send_sems, recv_sems, pairs, lands = state
        slots = _chip_exchange_wait(send_sems, recv_sems, pairs, lands, after, name=f"grad_{tag}_chip_wait")
        halves = [_chip_sum(g, r, sl, self.where, name=f"grad_chip_sum_{s}")
                  for s, g, r, sl in zip(slabs, gl, other, slots)]
        return dict(zip(slabs, _pair_share(halves, name=f"grad_{tag}_pair_share")))

    def after_layer1_backward(self, G):
        self.layer1_reduce, token = self.reduce_start(G, _LAYER1_SLABS, "l1")
        return token

    def after_ffn0_backward(self, G):
        self.ffn0_reduce, token = self.reduce_start(G, _FFN0_SLABS, "ffn0")
        return token


def kernel(x, mem, e_norm, e_w_in, e_gmlp_w, e_gmlp_b, e_conv_w, e_conv_b, e_conv_ln_g, e_conv_ln_b, e_w_out, o_norm, o_w_in, o_lam_re, o_lam_im, o_log_dt, o_b_re, o_b_im, o_c_re, o_c_im, o_d, o_w_out, ca_norm, ca_mem_norm, ca_wq, ca_wk, ca_wv, ca_wo, ffn_norm, ffn_w_gate, ffn_w_up, ffn_w_down, final_norm, loss_target, m_e_norm, m_e_w_in, m_e_gmlp_w, m_e_gmlp_b, m_e_conv_w, m_e_conv_b, m_e_conv_ln_g, m_e_conv_ln_b, m_e_w_out, m_o_norm, m_o_w_in, m_o_lam_re, m_o_lam_im, m_o_log_dt, m_o_b_re, m_o_b_im, m_o_c_re, m_o_c_im, m_o_d, m_o_w_out, m_ca_norm, m_ca_mem_norm, m_ca_wq, m_ca_wk, m_ca_wv, m_ca_wo, m_ffn_norm, m_ffn_w_gate, m_ffn_w_up, m_ffn_w_down, m_final_norm, v_e_norm, v_e_w_in, v_e_gmlp_w, v_e_gmlp_b, v_e_conv_w, v_e_conv_b, v_e_conv_ln_g, v_e_conv_ln_b, v_e_w_out, v_o_norm, v_o_w_in, v_o_lam_re, v_o_lam_im, v_o_log_dt, v_o_b_re, v_o_b_im, v_o_c_re, v_o_c_im, v_o_d, v_o_w_out, v_ca_norm, v_ca_mem_norm, v_ca_wq, v_ca_wk, v_ca_wv, v_ca_wo, v_ffn_norm, v_ffn_w_gate, v_ffn_w_up, v_ffn_w_down, v_final_norm):
    args = dict(locals())
    local = {n: args[n] for n in _WEIGHTS}
    mom = {n: args["m_" + n] for n in _WEIGHTS}
    vel = {n: args["v_" + n] for n in _WEIGHTS}
    chip = 2 * lax.axis_index("x") + lax.axis_index("y")
    core = lax.axis_index("c")
    xs_, mems_, tgt = x[0], mem[0], loss_target[0]

    w = {n: local[n] for n in _REPLICATED}
    sm_slab, sm_spans = _pack_rows([local[n] for n, _ in _SMALL_SHARDED], SMALL_W, F32)
    sm_all = _allgather_small(sm_slab, name="gather_small_weights").reshape(N_DEV, -1, SMALL_W)
    for (n, ax), span in zip(_SMALL_SHARDED, sm_spans):
        shp = local[n].shape
        w[n] = jnp.concatenate([_unpack_rows(sm_all[2 * p], [span], [shp])[0] for p in range(N_CHIPS)], axis=ax)

    exchange = _Exchange(local, chip, core)
    G = {s: lax.empty((N_CHIPS, rows, width), F32) for s, (width, rows) in _SLABS.items()}
    loss_lanes, dx, G, grads = _forward_backward(xs_, mems_, tgt, w, G, exchange)

    gs_slab, gs_spans = _pack_rows([grads[n] for n in _SMALL], SMALL_W, F32)
    small_flight = _all_to_all_start(gs_slab, dx, name="small_grads_start")
    gsum = exchange.reduce_finish(exchange.layer1_reduce, small_flight[4], "l1")
    gsum = {**gsum, **exchange.reduce_finish(exchange.ffn0_reduce, small_flight[4], "ffn0")}
    rest0_reduce, token = exchange.reduce_start(G, _REST0_SLABS, "rest0")

    gs_slab, gs_all = _all_to_all_wait(*small_flight[:4], token, name="small_grads_wait")
    gs_all = lax.dynamic_update_slice(gs_all, gs_slab[None], (2 * chip + core, 0, 0))
    gs_sum = _sum_slots(gs_all, name="small_grad_sum")
    out_grads = dict(zip(_SMALL, _unpack_rows(gs_sum, gs_spans, [grads[n].shape for n in _SMALL])))
    for n, ax in _SMALL_SHARDED:
        width = local[n].shape[ax]
        out_grads[n] = lax.dynamic_slice_in_dim(out_grads[n], chip * width, width, axis=ax)

    delta, new_m, new_v = {}, {}, {}
    d_, m_, v_ = _adamw_small([_two_d(local[n]) for n in _SMALL], [_two_d(out_grads[n]) for n in _SMALL],
                              [_two_d(mom[n]) for n in _SMALL], [_two_d(vel[n]) for n in _SMALL], name="adamw_small")
    for n, dd, mm_, vv in zip(_SMALL, d_, m_, v_):
        shp = local[n].shape
        delta[n], new_m[n], new_v[n] = dd.reshape(shp), mm_.reshape(shp), vv.reshape(shp)
    def adamw_large(names):
        for n in names:
            shp = local[n].shape
            g_, d_, m_, v_ = _adamw_shard(_two_d(local[n]), [(gsum[s], r0) for s, r0 in _PLACE[n][1]],
                                          _two_d(mom[n]), _two_d(vel[n]), name=f"adamw_{n}")
            out_grads[n], delta[n], new_m[n], new_v[n] = (g_.reshape(shp), d_.reshape(shp), m_.reshape(shp),
                                                          v_.reshape(shp))

    ready = [n for n, (_, where) in _PLACE.items() if all(s in gsum for s, _ in where)]
    adamw_large(ready)
    done = jnp.concatenate([delta[n].reshape(-1)[:1] for n in ready + list(_SMALL[:1])])
    gsum = {**gsum, **exchange.reduce_finish(rest0_reduce, done, "rest0")}
    adamw_large([n for n in _PLACE if n not in ready])

    loss = lax.psum(loss_lanes[0, 0], ("x", "y", "c"))
    return (loss, dx[None], *[out_grads[n] for n in _WEIGHTS], *[delta[n] for n in _WEIGHTS],
            *[new_m[n] for n in _WEIGHTS], *[new_v[n] for n in _WEIGHTS])
```

```python
import functools
import math

import jax
import jax.numpy as jnp
from jax import lax
from jax.experimental import pallas as pl
from jax.experimental.pallas import tpu as pltpu

F32 = jnp.float32
BF16 = jnp.bfloat16
MESH = pl.DeviceIdType.MESH

EPS = 1e-6
D_MODEL = 1024
A_WIDTH = 512
A_GROUPS = 4
GMLP_BLOCK = 128
CHUNK = 64
B_WIDTH = 512
CONV_WIDTH = 31
CONV_HALO = 32
C_WIDTH = 512
C_GROUP_CH = 16
C_GROUPS = 32
C_STATE = 64
N_STATE = C_GROUPS * C_STATE
STATE_LANES = 128
STATE_ROWS = N_STATE // STATE_LANES
SCAN_BLOCK = 8
CA_HEADS = 4
CA_HEAD_DIM = 256
FFN_HIDDEN = 2816

ADAM_LR = 0.001
ADAM_B1 = 0.9
ADAM_B2 = 0.999
ADAM_EPS = 1e-08
ADAM_WD = 0.01
ADAM_STEP = 10

VMEM_LIMIT = 56 * 1024 * 1024
ACC_BYTES = 6 * 1024 * 1024
TN_VMEM_BYTES = 44 * 1024 * 1024
SMALL_W = 128
N_CHIPS = 4
N_DEV = 8

_SLABS = {"D0": (512, 1024), "E0": (1024, 256), "A0": (1024, 1024), "B0": (1024, 704), "C0": (1024, 1408),
          "D1": (512, 768), "A1": (1024, 1024), "B1": (1024, 704), "C1": (1024, 1408)}
_STAGES = (("D0", "E0"), ("A0",), ("B0", "C0"), ("D1", "A1", "B1", "C1"))
_LAYER1_SLABS = _STAGES[3]
_FFN0_SLABS = _STAGES[2]
_REST0_SLABS = _STAGES[0] + _STAGES[1]
_PLACE = {
    "e_w_in": (1024, (("D0", 0),)), "e_w_out": (256, (("E0", 0),)),
    "o_w_out": (512, (("D1", 0),)), "o_w_in": (256, (("D1", 512),)),
    "ca_wq": (256, (("A0", 0), ("A1", 0))), "ca_wk": (256, (("A0", 256), ("A1", 256))),
    "ca_wv": (256, (("A0", 512), ("A1", 512))), "ca_wo": (256, (("A0", 768), ("A1", 768))),
    "ffn_w_down": (704, (("B0", 0), ("B1", 0))),
    "ffn_w_gate": (704, (("C0", 0), ("C1", 0))), "ffn_w_up": (704, (("C0", 704), ("C1", 704))),
}
_TRANSPOSED = ("ffn_w_gate", "ffn_w_up")


def _params(sem=None):
    return pltpu.CompilerParams(dimension_semantics=sem, vmem_limit_bytes=VMEM_LIMIT)


def _tile(n, pref, mult=128):
    if n <= pref:
        return n
    t = (pref // mult) * mult
    while t >= mult:
        if n % t == 0:
            return t
        t -= mult
    return n


def _blk(name, layer=0):
    rows, where = _PLACE[name]
    slab, r0 = where[layer]
    assert r0 % rows == 0
    return slab, rows, r0 // rows


def _shards(slabs, name, layer=0):
    slab, rows, b = _blk(name, layer)
    return [(slabs[slab], (None, rows, _SLABS[slab][0]), (p, b, 0)) for p in range(N_CHIPS)]


_GELU_C = 0.7978845608028654
_GELU_A = 0.044715


def _gelu(x):
    t = jnp.tanh(_GELU_C * (x + _GELU_A * (x * x * x)))
    return 0.5 * x * (1.0 + t), t


def _gelu_grad(x, t):
    return 0.5 * (1.0 + t) + 0.5 * x * (1.0 - t * t) * (_GELU_C * (1.0 + 3.0 * _GELU_A * x * x))


def _sigmoid(x):
    return 1.0 / (1.0 + jnp.exp(-x))


def _mean(x):
    return jnp.mean(x, axis=-1, keepdims=True)


def _dot(a, b):
    return jnp.dot(a, b, preferred_element_type=F32)


def _dot_nt(a, b):
    return lax.dot_general(a, b, (((1,), (1,)), ((), ())), preferred_element_type=F32)


def _dot_tn(a, b):
    return lax.dot_general(a, b, (((0,), (0,)), ((), ())), preferred_element_type=F32)


def _rms_tile(xv, gv):
    return (xv * lax.rsqrt(_mean(xv * xv) + EPS)) * gv


def _rms_bwd_tile(xv, gv, dyv):
    r = lax.rsqrt(_mean(xv * xv) + EPS)
    xh = xv * r
    dyg = dyv * gv
    return r * (dyg - xh * _mean(dyg * xh)), jnp.sum(dyv * xh, axis=0, keepdims=True)


def _cols(p, width):
    return slice(p * width, (p + 1) * width)


def _sum_k(a, ws, k):
    tot = None
    for p in range(N_CHIPS):
        y = _dot(a[:, _cols(p, k)], ws[p][...])
        tot = y if tot is None else tot + y
    return tot


def _cat_nt(a, ws):
    return jnp.concatenate([_dot_nt(a, ws[p][...]) for p in range(N_CHIPS)], axis=1)


def _rows_call(name, tm, rows, fulls, outs, accs, body, scratch=()):
    S = min(x.shape[-2] for x in rows if x.ndim != 4)
    nr, nf, no, na = len(rows), len(fulls), len(outs), len(accs)

    def kern(*refs):
        r, f = refs[:nr], refs[nr:nr + nf]
        o, a = refs[nr + nf:nr + nf + no], refs[nr + nf + no:nr + nf + no + na]
        if na:
            @pl.when(pl.program_id(0) == 0)
            def _():
                for ref in a:
                    ref[...] = jnp.zeros_like(ref)
        body(r, f, o, a, refs[nr + nf + no + na:])

    def whole(shape):
        nd = len(shape)
        return pl.BlockSpec(tuple(shape), lambda i: (0,) * nd)

    def row_spec(shape):
        if len(shape) == 4:
            return pl.BlockSpec((tm // 8,) + tuple(shape[1:]), lambda i: (i, 0, 0, 0))
        if len(shape) == 3:
            return pl.BlockSpec((shape[0], tm, shape[2]), lambda i: (0, i, 0))
        return pl.BlockSpec((tm, shape[1]), lambda i: (i, 0))

    def full_spec(x):
        if isinstance(x, tuple):
            _, bshape, bidx = x
            return pl.BlockSpec(bshape, lambda i: bidx, pipeline_mode=pl.Buffered(1))
        return whole(x.shape)

    def out_shape_of(o):
        if o[0] == "state":
            return (S // 8, 2 * STATE_ROWS, 8, STATE_LANES)
        return (S, o[0]) if len(o) == 2 else (o[0], S, o[1])

    out_shapes = [out_shape_of(o) for o in outs]
    res = pl.pallas_call(
        kern, name=name, grid=(S // tm,),
        in_specs=[row_spec(x.shape) for x in rows] + [full_spec(x) for x in fulls],
        out_specs=[row_spec(s) for s in out_shapes] + [whole(shp) for shp, _ in accs],
        out_shape=[jax.ShapeDtypeStruct(s, o[-1]) for s, o in zip(out_shapes, outs)]
        + [jax.ShapeDtypeStruct(tuple(shp), dt) for shp, dt in accs],
        scratch_shapes=list(scratch),
        compiler_params=_params(("arbitrary",) if na else ("parallel",)),
    )(*rows, *[x[0] if isinstance(x, tuple) else x for x in fulls])
    return res[:no], res[no:]


def _grad_to_slab(gslabs, wname, layer, a, b, *, a_cols=None, b_cols=None, chips=(0, N_CHIPS), name):
    slab, rows, bidx = _blk(wname, layer)
    width = _SLABS[slab][0]
    p0, n_p = chips
    assert p0 % n_p == 0
    S = a.shape[-2]

    def tile_bytes(x, ts):
        return ts * x.dtype.itemsize * (x.shape[2] * n_p if x.ndim == 3 else x.shape[1])

    acc_bytes = n_p * rows * (-(-width // 128) * 128) * 4
    ts = next(t for t in (2048, 1024, 512, 256, S) if S % t == 0
              and 2 * (tile_bytes(a, t) + tile_bytes(b, t) + acc_bytes) <= TN_VMEM_BYTES or t == S)

    def operand(x):
        if x.ndim == 3:
            return pl.BlockSpec((n_p, ts, x.shape[2]), lambda s: (p0 // n_p, s, 0))
        return pl.BlockSpec((ts, x.shape[1]), lambda s: (s, 0))

    def part(ref, cols, p):
        if len(ref.shape) == 3:
            return ref[p]
        return ref[...] if cols is None else ref[:, _cols(p, cols)]

    def body(a_ref, b_ref, slab_ref, o_ref):
        @pl.when(pl.program_id(0) == 0)
        def _():
            o_ref[...] = jnp.zeros_like(o_ref)

        for p in range(n_p):
            o_ref[p] += _dot_tn(part(a_ref, a_cols, p).astype(BF16), part(b_ref, b_cols, p).astype(BF16))

    g = gslabs[slab]
    out = pl.pallas_call(
        body, name=name, grid=(S // ts,),
        in_specs=[operand(a), operand(b), pl.BlockSpec(memory_space=pl.ANY)],
        out_specs=pl.BlockSpec((n_p, rows, width), lambda s: (p0 // n_p, bidx, 0)),
        out_shape=jax.ShapeDtypeStruct(g.shape, F32), input_output_aliases={2: 0},
        compiler_params=_params(("arbitrary",)),
    )(a, b, g)
    return {**gslabs, slab: out}


def _vec(g):
    return g.reshape(1, -1)


def _norm_mm(x, g, ws, *, split, out_dtype, name, tm=512):
    S, D = x.shape
    k, n = ws[0][1][1], ws[0][1][2]
    N = n if split == "k" else N_CHIPS * n

    def body(r, f, o, acc, s):
        xn = _rms_tile(r[0][...], f[0][...]).astype(BF16)
        o[0][...] = xn
        if split == "k":
            o[1][...] = _sum_k(xn, f[1:], k).astype(out_dtype)
        else:
            for p in range(N_CHIPS):
                o[1][:, _cols(p, n)] = _dot(xn, f[1 + p][...]).astype(out_dtype)

    (xn, y), _ = _rows_call(name, _tile(S, tm), [x], [_vec(g)] + ws, [(D, BF16), (N, out_dtype)], [], body)
    return xn, y


def _mm_k(a, ws, *, add=None, out_dtype=F32, name, tm=512):
    S = a.shape[-2]
    k, n = ws[0][1][1], ws[0][1][2]
    has_add = add is not None

    def body(r, f, o, acc, s):
        if a.ndim == 3:
            y = None
            for p in range(N_CHIPS):
                t = _dot(r[0][p].astype(BF16), f[p][...])
                y = t if y is None else y + t
        else:
            y = _sum_k(r[0][...].astype(BF16), f, k)
        if has_add:
            y = y + r[1][...]
        o[0][...] = y.astype(out_dtype)

    (y,), _ = _rows_call(name, _tile(S, tm), [a] + ([add] if has_add else []), ws, [(n, out_dtype)], [], body)
    return y


def _mm_k_t(terms, *, out_dtype=F32, name, tm=512):
    S = terms[0][0].shape[0]
    k = terms[0][1][0][1][1]

    def body(r, f, o, acc, s):
        y = None
        for t in range(len(terms)):
            yt = _cat_nt(r[t][...].astype(BF16), f[N_CHIPS * t:N_CHIPS * (t + 1)])
            y = yt if y is None else y + yt
        o[0][...] = y.astype(out_dtype)

    (y,), _ = _rows_call(name, _tile(S, tm), [a for a, _ in terms], [w for _, ws in terms for w in ws],
                         [(N_CHIPS * k, out_dtype)], [], body)
    return y


def _rms_fwd(x, g, *, name):
    def body(r, f, o, acc, s):
        o[0][...] = _rms_tile(r[0][...], f[0][...]).astype(BF16)

    (y,), _ = _rows_call(name, _tile(x.shape[0], 256, 8), [x], [_vec(g)], [(x.shape[1], BF16)], [], body)
    return y


def _rms_dg(x, g, dy, *, name):
    def body(r, f, o, acc, s):
        acc[0][...] += _rms_bwd_tile(r[0][...], f[0][...], r[1][...])[1]

    _, (dg,) = _rows_call(name, _tile(x.shape[0], 256, 8), [x, dy], [_vec(g)], [], [((1, x.shape[1]), F32)], body)
    return dg


def _ffn_up(x, g, wg, wu, *, name, tm=256):
    S, D = x.shape
    h = wg[0][1][1]

    def body(r, f, o, acc, s):
        xn = _rms_tile(r[0][...], f[0][...]).astype(BF16)
        o[0][...] = xn
        for p in range(N_CHIPS):
            gate = _dot_nt(xn, f[1 + p][...])
            up = _dot_nt(xn, f[1 + N_CHIPS + p][...])
            o[1][p] = gate.astype(BF16)
            o[2][p] = up.astype(BF16)
            o[3][p] = (gate * _sigmoid(gate) * up).astype(BF16)

    (xn, gate, up, hid), _ = _rows_call(name, _tile(S, tm), [x], [_vec(g)] + wg + wu,
                                        [(D, BF16), (N_CHIPS, h, BF16), (N_CHIPS, h, BF16), (N_CHIPS, h, BF16)], [],
                                        body)
    return xn, gate, up, hid


def _ffn_bwd_hidden(dy, wd, gate, up, token=None, *, name, tm=256):
    S = dy.shape[0]
    h = wd[0][1][1]

    def body(r, f, o, acc, s):
        dyv = r[0][...]
        if token is not None:
            dyv = dyv + jnp.sum(f[N_CHIPS][...])
        dyb = dyv.astype(BF16)
        for p in range(N_CHIPS):
            dh = _dot_nt(dyb, f[p][...])
            gv = r[1][p].astype(F32)
            sg = _sigmoid(gv)
            o[0][p] = (dh * r[2][p].astype(F32) * (sg * (1.0 + gv * (1.0 - sg)))).astype(BF16)
            o[1][p] = (dh * gv * sg).astype(BF16)

    (dg, du), _ = _rows_call(name, _tile(S, tm), [dy, gate, up], wd + ([] if token is None else [token]),
                             [(N_CHIPS, h, BF16), (N_CHIPS, h, BF16)], [], body)
    return dg, du


def _ffn_in_bwd(dg, du, wg, wu, x, g, dres, *, name, tm=256):
    S, D = x.shape

    def body(r, f, o, acc, s):
        tot = None
        for p in range(N_CHIPS):
            y = _dot(r[0][p], f[1 + p][...]) + _dot(r[1][p], f[1 + N_CHIPS + p][...])
            tot = y if tot is None else tot + y
        dx, dgn = _rms_bwd_tile(r[2][...], f[0][...], tot)
        o[0][...] = dx + r[3][...]
        acc[0][...] += dgn

    (dx,), (dgn,) = _rows_call(name, _tile(S, tm), [dg, du, x, dres], [_vec(g)] + wg + wu, [(D, F32)],
                               [((1, D), F32)], body)
    return dx, dgn


def _norm_bwd_k(da, ws, x, g, dres, *, name, tm=512):
    S, D = x.shape

    def body(r, f, o, acc, s):
        dx, dg = _rms_bwd_tile(r[1][...], f[0][...], _cat_nt(r[0][...].astype(BF16), f[1:]))
        o[0][...] = dx + r[2][...]
        acc[0][...] += dg

    (dx,), (dg,) = _rows_call(name, _tile(S, tm), [da, x, dres], [_vec(g)] + ws, [(D, F32)], [((1, D), F32)], body)
    return dx, dg


def _norm_bwd_n(das, ws, x, g, dres, *, name, tm=256):
    S, D = x.shape
    n = ws[0][1][2]

    def body(r, f, o, acc, s):
        tot = None
        for p in range(N_CHIPS):
            y = _dot_nt(r[p // 2][:, _cols(p % 2, n)], f[1 + p][...])
            tot = y if tot is None else tot + y
        dx, dg = _rms_bwd_tile(r[2][...], f[0][...], tot)
        o[0][...] = dx + r[3][...]
        acc[0][...] += dg

    (dx,), (dg,) = _rows_call(name, _tile(S, tm), list(das) + [x, dres], [_vec(g)] + ws, [(D, F32)], [((1, D), F32)],
                              body)
    return dx, dg


def _ln_stats(v):
    mu = _mean(v)
    xc = v - mu
    rstd = lax.rsqrt(_mean(xc * xc) + EPS)
    return xc * rstd, rstd


_SHIFTS = 8
_CONV_ROWS = 64


def _fill_shifts(sh_ref, ext_ref, tm):
    sh_ref[0] = ext_ref[...]
    for s in range(1, _SHIFTS):
        sh_ref[s, 0:tm + CONV_HALO - _SHIFTS, :] = ext_ref[pl.ds(s, tm + CONV_HALO - _SHIFTS), :]


def _window(sh_ref, off, tm):
    return sh_ref[off % _SHIFTS, pl.ds(off - off % _SHIFTS, tm), :]


def _even_fwd(proj, wm, bcol, cw, cb, lg, lb, *, name):
    S = proj.shape[0]
    tm = _tile(S, 256)
    hb = tm // CONV_HALO
    nblk = tm // GMLP_BLOCK

    def body(p_ref, halo_ref, wm_ref, b_ref, cw_ref, cb_ref, lg_ref, lb_ref, mix_ref, hc_ref, hext_ref, hsh_ref):
        i = pl.program_id(0)
        gu, _ = _gelu(p_ref[:, 0:A_WIDTH])
        gv, _ = _gelu(p_ref[:, A_WIDTH:2 * A_WIDTH])
        vn, _ = _ln_stats(gv)
        vnb = vn.astype(BF16)
        for n in range(nblk):
            rows = slice(n * GMLP_BLOCK, (n + 1) * GMLP_BLOCK)
            for g in range(A_GROUPS):
                cols = slice(g * GMLP_BLOCK, (g + 1) * GMLP_BLOCK)
                sg = jnp.dot(wm_ref[g], vnb[rows, cols], preferred_element_type=F32) + b_ref[g]
                mix_ref[rows, cols] = (gu[rows, cols] * sg).astype(BF16)
        h = p_ref[:, 1024:1536] * _sigmoid(p_ref[:, 1536:2048])
        hh = halo_ref[:, 0:B_WIDTH] * _sigmoid(halo_ref[:, B_WIDTH:2 * B_WIDTH])
        hext_ref[0:CONV_HALO, :] = jnp.where(i > 0, hh, 0.0)
        hext_ref[CONV_HALO:CONV_HALO + tm, :] = h
        _fill_shifts(hsh_ref, hext_ref, tm)
        for r0 in range(0, tm, _CONV_ROWS):
            acc = jnp.zeros((_CONV_ROWS, B_WIDTH), F32)
            for k in range(CONV_WIDTH):
                acc = acc + cw_ref[k:k + 1, :] * _window(hsh_ref, r0 + k + CONV_HALO - CONV_WIDTH + 1, _CONV_ROWS)
            hc_ref[r0:r0 + _CONV_ROWS, :] = acc + cb_ref[...]
        hc = hc_ref[...]
        hhat, _ = _ln_stats(hc)
        hl = hhat * lg_ref[...] + lb_ref[...]
        mix_ref[:, A_WIDTH:A_WIDTH + B_WIDTH] = (hl * _sigmoid(hl)).astype(BF16)

    vec = pl.BlockSpec((1, B_WIDTH), lambda i: (0, 0))
    return pl.pallas_call(
        body, name=name, grid=(S // tm,),
        in_specs=[
            pl.BlockSpec((tm, 2048), lambda i: (i, 0)),
            pl.BlockSpec((CONV_HALO, 1024), lambda i: (jnp.maximum(i * hb - 1, 0), 1)),
            pl.BlockSpec((A_GROUPS, GMLP_BLOCK, GMLP_BLOCK), lambda i: (0, 0, 0)),
            pl.BlockSpec((A_GROUPS, GMLP_BLOCK, 1), lambda i: (0, 0, 0)),
            pl.BlockSpec((CONV_HALO, B_WIDTH), lambda i: (0, 0)),
            vec, vec, vec,
        ],
        out_specs=[pl.BlockSpec((tm, 1024), lambda i: (i, 0)), pl.BlockSpec((tm, B_WIDTH), lambda i: (i, 0))],
        out_shape=[jax.ShapeDtypeStruct((S, 1024), BF16), jax.ShapeDtypeStruct((S, B_WIDTH), F32)],
        scratch_shapes=[pltpu.VMEM((tm + CONV_HALO, B_WIDTH), F32),
                        pltpu.VMEM((_SHIFTS, tm + CONV_HALO, B_WIDTH), F32)],
        compiler_params=_params(("parallel",)),
    )(proj, proj, wm, bcol, cw, cb, lg, lb)


def _even_bwd1(proj, dmix, hc, wm, wmt, bcol, lg, lb, *, name):
    S = proj.shape[0]
    tm = _tile(S, 256)
    nblk = tm // GMLP_BLOCK

    def body(p_ref, dm_ref, hc_ref, wm_ref, wmt_ref, b_ref, lg_ref, lb_ref,
             dpa_ref, dhc_ref, dwm_ref, db_ref, dlg_ref, dlb_ref, dcb_ref, dgu_ref, dvn_ref):
        @pl.when(pl.program_id(0) == 0)
        def _():
            dwm_ref[...] = jnp.zeros_like(dwm_ref)
            db_ref[...] = jnp.zeros_like(db_ref)
            dlg_ref[...] = jnp.zeros_like(dlg_ref)
            dlb_ref[...] = jnp.zeros_like(dlb_ref)
            dcb_ref[...] = jnp.zeros_like(dcb_ref)

        au = p_ref[:, 0:A_WIDTH]
        av = p_ref[:, A_WIDTH:2 * A_WIDTH]
        gu, tu = _gelu(au)
        gv, tv = _gelu(av)
        vn, rstd = _ln_stats(gv)
        vnb = vn.astype(BF16)
        for n in range(nblk):
            rows = slice(n * GMLP_BLOCK, (n + 1) * GMLP_BLOCK)
            for g in range(A_GROUPS):
                cols = slice(g * GMLP_BLOCK, (g + 1) * GMLP_BLOCK)
                vb = vnb[rows, cols]
                sg = jnp.dot(wm_ref[g], vb, preferred_element_type=F32) + b_ref[g]
                da = dm_ref[rows, cols]
                dsg = da * gu[rows, cols]
                dgu_ref[rows, cols] = da * sg
                dsgb = dsg.astype(BF16)
                dwm_ref[g] += _dot_nt(dsgb, vb)
                db_ref[g] += jnp.sum(dsg, axis=1, keepdims=True)
                dvn_ref[rows, cols] = jnp.dot(wmt_ref[g], dsgb, preferred_element_type=F32)
        dvn = dvn_ref[...]
        dgv = rstd * (dvn - _mean(dvn) - vn * _mean(dvn * vn))
        dpa_ref[:, 0:A_WIDTH] = (dgu_ref[...] * _gelu_grad(au, tu)).astype(BF16)
        dpa_ref[:, A_WIDTH:2 * A_WIDTH] = (dgv * _gelu_grad(av, tv)).astype(BF16)
        hhat, rstd2 = _ln_stats(hc_ref[...])
        lgv = lg_ref[...]
        hl = hhat * lgv + lb_ref[...]
        s = _sigmoid(hl)
        dhl = dm_ref[:, A_WIDTH:A_WIDTH + B_WIDTH] * (s * (1.0 + hl * (1.0 - s)))
        dlg_ref[...] += jnp.sum(dhl * hhat, axis=0, keepdims=True)
        dlb_ref[...] += jnp.sum(dhl, axis=0, keepdims=True)
        dhh = dhl * lgv
        dhc = rstd2 * (dhh - _mean(dhh) - hhat * _mean(dhh * hhat))
        dcb_ref[...] += jnp.sum(dhc, axis=0, keepdims=True)
        dhc_ref[...] = dhc

    vec = pl.BlockSpec((1, B_WIDTH), lambda i: (0, 0))
    w3 = pl.BlockSpec((A_GROUPS, GMLP_BLOCK, GMLP_BLOCK), lambda i: (0, 0, 0))
    b3 = pl.BlockSpec((A_GROUPS, GMLP_BLOCK, 1), lambda i: (0, 0, 0))
    return pl.pallas_call(
        body, name=name, grid=(S // tm,),
        in_specs=[
            pl.BlockSpec((tm, 1024), lambda i: (i, 0)),
            pl.BlockSpec((tm, 1024), lambda i: (i, 0)),
            pl.BlockSpec((tm, B_WIDTH), lambda i: (i, 0)),
            w3, w3, b3, vec, vec,
        ],
        out_specs=[pl.BlockSpec((tm, 1024), lambda i: (i, 0)), pl.BlockSpec((tm, B_WIDTH), lambda i: (i, 0)),
                   w3, b3, vec, vec, vec],
        out_shape=[
            jax.ShapeDtypeStruct((S, 1024), BF16), jax.ShapeDtypeStruct((S, B_WIDTH), F32),
            jax.ShapeDtypeStruct((A_GROUPS, GMLP_BLOCK, GMLP_BLOCK), F32),
            jax.ShapeDtypeStruct((A_GROUPS, GMLP_BLOCK, 1), F32),
            jax.ShapeDtypeStruct((1, B_WIDTH), F32), jax.ShapeDtypeStruct((1, B_WIDTH), F32),
            jax.ShapeDtypeStruct((1, B_WIDTH), F32),
        ],
        scratch_shapes=[pltpu.VMEM((tm, A_WIDTH), F32), pltpu.VMEM((tm, A_WIDTH), F32)],
        compiler_params=_params(("arbitrary",)),
    )(proj, dmix, hc, wm, wmt, bcol, lg, lb)


def _even_bwd2(proj, dhc, cw, *, name):
    S = proj.shape[0]
    tm = _tile(S, 256)
    hb = tm // CONV_HALO
    nt = S // tm
    last_halo = S // CONV_HALO - 1
    lo = CONV_HALO - CONV_WIDTH + 1

    def body(p_ref, halo_ref, d_ref, dnext_ref, cw_ref, dpb_ref, dcw_ref, hext_ref, dext_ref, hsh_ref, dsh_ref):
        i = pl.program_id(0)

        @pl.when(i == 0)
        def _():
            dcw_ref[...] = jnp.zeros_like(dcw_ref)

        hh = halo_ref[:, 0:B_WIDTH] * _sigmoid(halo_ref[:, B_WIDTH:2 * B_WIDTH])
        hext_ref[0:CONV_HALO, :] = jnp.where(i > 0, hh, 0.0)
        hext_ref[CONV_HALO:CONV_HALO + tm, :] = p_ref[:, 0:B_WIDTH] * _sigmoid(p_ref[:, B_WIDTH:2 * B_WIDTH])
        dext_ref[0:tm, :] = d_ref[...]
        dext_ref[tm:tm + CONV_HALO, :] = jnp.where(i < nt - 1, dnext_ref[...], 0.0)
        _fill_shifts(hsh_ref, hext_ref, tm)
        _fill_shifts(dsh_ref, dext_ref, tm)
        for r0 in range(0, tm, _CONV_ROWS):
            rows = slice(r0, r0 + _CONV_ROWS)
            dhc_b = d_ref[rows, :]
            dh = jnp.zeros((_CONV_ROWS, B_WIDTH), F32)
            for k in range(CONV_WIDTH):
                dh = dh + cw_ref[k:k + 1, :] * _window(dsh_ref, r0 + CONV_WIDTH - 1 - k, _CONV_ROWS)
                dcw_ref[k:k + 1, :] += jnp.sum(dhc_b * _window(hsh_ref, r0 + k + lo, _CONV_ROWS), axis=0,
                                               keepdims=True)
            ba_b = p_ref[rows, 0:B_WIDTH]
            sg_b = _sigmoid(p_ref[rows, B_WIDTH:2 * B_WIDTH])
            dpb_ref[rows, 0:B_WIDTH] = (dh * sg_b).astype(BF16)
            dpb_ref[rows, B_WIDTH:2 * B_WIDTH] = (dh * ba_b * sg_b * (1.0 - sg_b)).astype(BF16)

    return pl.pallas_call(
        body, name=name, grid=(nt,),
        in_specs=[
            pl.BlockSpec((tm, 1024), lambda i: (i, 1)),
            pl.BlockSpec((CONV_HALO, 1024), lambda i: (jnp.maximum(i * hb - 1, 0), 1)),
            pl.BlockSpec((tm, B_WIDTH), lambda i: (i, 0)),
            pl.BlockSpec((CONV_HALO, B_WIDTH), lambda i: (jnp.minimum((i + 1) * hb, last_halo), 0)),
            pl.BlockSpec((CONV_HALO, B_WIDTH), lambda i: (0, 0)),
        ],
        out_specs=[pl.BlockSpec((tm, 1024), lambda i: (i, 0)), pl.BlockSpec((CONV_HALO, B_WIDTH), lambda i: (0, 0))],
        out_shape=[jax.ShapeDtypeStruct((S, 1024), BF16), jax.ShapeDtypeStruct((CONV_HALO, B_WIDTH), F32)],
        scratch_shapes=[pltpu.VMEM((tm + CONV_HALO, B_WIDTH), F32), pltpu.VMEM((tm + CONV_HALO, B_WIDTH), F32),
                        pltpu.VMEM((_SHIFTS, tm + CONV_HALO, B_WIDTH), F32),
                        pltpu.VMEM((_SHIFTS, tm + CONV_HALO, B_WIDTH), F32)],
        compiler_params=_params(("arbitrary",)),
    )(proj, proj, dhc, dhc, cw)


_CA_SCALE = CA_HEAD_DIM ** -0.5


def _softmax_rows(s):
    e = jnp.exp(s - jnp.max(s, axis=-1, keepdims=True))
    return e / jnp.sum(e, axis=-1, keepdims=True)


def _attn_fwd(q, k, v, *, name):
    S = q.shape[0]

    def body(r, f, o, acc, s):
        for h in range(CA_HEADS):
            cols = _cols(h, CA_HEAD_DIM)
            p = _softmax_rows(_dot_nt(r[0][:, cols], f[0][:, cols]) * _CA_SCALE)
            o[0][:, cols] = _dot(p.astype(BF16), f[1][:, cols]).astype(BF16)

    (o_,), _ = _rows_call(name, _tile(S, 512), [q], [k, v], [(D_MODEL, BF16)], [], body)
    return o_


def _attn_bwd(dy, wo, q, k, v, *, name):
    S = q.shape[0]
    M = k.shape[0]

    def body(r, f, o, acc, s):
        dyb = r[0][...].astype(BF16)
        for h in range(CA_HEADS):
            cols = _cols(h, CA_HEAD_DIM)
            qh = r[1][:, cols]
            kh = f[0][:, cols]
            vh = f[1][:, cols]
            doh = _dot_nt(dyb, f[2 + h][...]).astype(BF16)
            p = _softmax_rows(_dot_nt(qh, kh) * _CA_SCALE)
            acc[1][:, cols] += _dot_tn(p.astype(BF16), doh)
            dp = _dot_nt(doh, vh)
            ds = (p * (dp - jnp.sum(dp * p, axis=-1, keepdims=True)) * _CA_SCALE).astype(BF16)
            o[0][:, cols] = _dot(ds, kh).astype(BF16)
            acc[0][:, cols] += _dot_tn(ds, qh)

    (dq,), (dk, dv) = _rows_call(name, _tile(S, 512), [dy, q], [k, v] + wo, [(D_MODEL, BF16)],
                                 [((M, D_MODEL), F32), ((M, D_MODEL), F32)], body)
    return dq, dk, dv


_STATE_TILE = 2 * STATE_ROWS


def _state_cols(ref, tm):
    return jnp.concatenate([ref[:, c, :, :].reshape(tm, STATE_LANES).astype(BF16) for c in range(_STATE_TILE)],
                           axis=1)


def _put_state_cols(ref, y, tm):
    for c in range(_STATE_TILE):
        ref[:, c, :, :] = y[:, _cols(c, STATE_LANES)].reshape(tm // 8, 8, STATE_LANES)


def _mm_to_state(a, w, *, nt=False, name, tm=256):
    S = a.shape[0]
    tm = _tile(S, tm)

    def body(r, f, o, acc, s):
        av = r[0][...].astype(BF16)
        _put_state_cols(o[0], _dot_nt(av, f[0][...]) if nt else _dot(av, f[0][...]), tm)

    (y,), _ = _rows_call(name, tm, [a], [w], [("state", F32)], [], body)
    return y


def _s5_readout(xs, cd, u, d, *, name, tm=256):
    tm = _tile(u.shape[0], tm)

    def body(r, f, o, acc, s):
        y = _dot(_state_cols(r[0], tm), f[0][...]) + f[1][...] * r[1][...]
        o[0][...] = y
        o[1][...] = _gelu(y)[0].astype(BF16)

    (y, yg), _ = _rows_call(name, tm, [xs, u], [cd, d], [(C_WIDTH, F32), (C_WIDTH, BF16)], [], body)
    return y, yg


def _state_grad_tn(a, b, *, name, ts=256):
    a_state, b_state = a.ndim == 4, b.ndim == 4
    S = b.shape[0] if a_state else a.shape[0]
    ts = _tile(S, ts)
    K1 = 2 * N_STATE if a_state else a.shape[1]
    N = 2 * N_STATE if b_state else b.shape[1]

    def body(r, f, o, acc, s):
        av = _state_cols(r[0], ts) if a_state else r[0][...].astype(BF16)
        bv = _state_cols(r[1], ts) if b_state else r[1][...].astype(BF16)
        acc[0][...] += _dot_tn(av, bv)

    _, (out,) = _rows_call(name, ts, [a, b], [], [], [((K1, N), F32)], body)
    return out


def _glu_out(yg, ws, x, *, name, tm=512):
    n = ws[0][1][2]

    def body(r, f, o, acc, s):
        ygv = r[0][...]
        ov = [_dot(ygv, f[p][...]) for p in range(N_CHIPS)]
        for p in range(N_CHIPS):
            o[0][:, _cols(p, n)] = ov[p].astype(BF16)
        for p in range(2):
            o[1][:, _cols(p, n)] = r[1][:, _cols(p, n)] + ov[p] * _sigmoid(ov[2 + p])

    (o_, y), _ = _rows_call(name, _tile(x.shape[0], tm), [yg, x], ws, [(2 * D_MODEL, BF16), (D_MODEL, F32)], [], body)
    return o_, y


def _glu_out_bwd(o_, dy, ws, y, u, d, *, name, tm=256):
    n = ws[0][1][2]

    def body(r, f, o, acc, s):
        o1 = r[0][:, 0:D_MODEL].astype(F32)
        sg = _sigmoid(r[0][:, D_MODEL:2 * D_MODEL].astype(F32))
        dyv = r[1][...]
        do1 = (dyv * sg).astype(BF16)
        do2 = (dyv * o1 * sg * (1.0 - sg)).astype(BF16)
        o[0][:, 0:D_MODEL] = do1
        o[0][:, D_MODEL:2 * D_MODEL] = do2
        dyg = None
        for p in range(N_CHIPS):
            t = _dot_nt((do1 if p < 2 else do2)[:, _cols(p % 2, n)], f[1 + p][...])
            dyg = t if dyg is None else dyg + t
        yv = r[2][...]
        dys = dyg * _gelu_grad(yv, _gelu(yv)[1])
        o[1][...] = dys.astype(BF16)
        o[2][...] = f[0][...] * dys
        acc[0][...] += jnp.sum(dys * r[3][...], axis=0, keepdims=True)

    (do, dys, dus), (dd,) = _rows_call(name, _tile(dy.shape[0], tm), [o_, dy, y, u], [d] + ws,
                                       [(2 * D_MODEL, BF16), (C_WIDTH, BF16), (C_WIDTH, F32)], [((1, C_WIDTH), F32)],
                                       body)
    return do, dys, dus, dd


def _s5_in_bwd(gs, bd, dus, ws, x, g, dres, *, name, tm=256):
    D = x.shape[1]
    tm = _tile(x.shape[0], tm)

    def body(r, f, o, acc, s):
        du = (_dot_nt(_state_cols(r[0], tm), f[1][...]) + r[1][...]).astype(BF16)
        o[0][...] = du
        dx, dg = _rms_bwd_tile(r[2][...], f[0][...], _cat_nt(du, f[2:]))
        o[1][...] = dx + r[3][...]
        acc[0][...] += dg

    (du, dx), (dg,) = _rows_call(name, tm, [gs, dus, x, dres], [_vec(g), bd] + ws,
                                 [(C_WIDTH, BF16), (D, F32)], [((1, D), F32)], body)
    return du, dx, dg


_SCAN_CHUNK = 128
_RE = slice(0, STATE_ROWS)
_IM = slice(STATE_ROWS, 2 * STATE_ROWS)
assert SCAN_BLOCK == 8


def _token(g, i):
    return pl.ds(pl.multiple_of(g * (_STATE_TILE * SCAN_BLOCK), _STATE_TILE * SCAN_BLOCK) + i, _STATE_TILE,
                 stride=SCAN_BLOCK)


def _scan_fwd(bu, pw, *, name):
    S = bu.shape[0] // _STATE_TILE
    tc = _tile(S, _SCAN_CHUNK, 8)

    def body(bu_ref, pw_ref, xs_ref, st_ref):
        @pl.when(pl.program_id(0) == 0)
        def _():
            st_ref[...] = jnp.zeros_like(st_ref)

        ar = pw_ref[0, _RE, :]
        ai = pw_ref[0, _IM, :]

        def block(g, carry):
            xr, xi = carry
            cr = ci = nr = ni = None
            for j in range(SCAN_BLOCK):
                b = bu_ref[_token(g, j), :]
                br, bi = b[_RE], b[_IM]
                cr, ci = (br, bi) if j == 0 else (ar * cr - ai * ci + br, ar * ci + ai * cr + bi)
                pr, pi = pw_ref[j, _RE, :], pw_ref[j, _IM, :]
                nr = pr * xr - pi * xi + cr
                ni = pr * xi + pi * xr + ci
                xs_ref[_token(g, j), :] = jnp.concatenate([nr, ni], axis=0)
            return nr, ni

        xr, xi = lax.fori_loop(0, tc // SCAN_BLOCK, block, (st_ref[_RE, :], st_ref[_IM, :]), unroll=2)
        st_ref[_RE, :] = xr
        st_ref[_IM, :] = xi

    blk = pl.BlockSpec((tc * _STATE_TILE, STATE_LANES), lambda i: (i, 0))
    return pl.pallas_call(
        body, name=name, grid=(S // tc,),
        in_specs=[blk, pl.BlockSpec(pw.shape, lambda i: (0, 0, 0))], out_specs=blk,
        out_shape=jax.ShapeDtypeStruct(bu.shape, F32),
        scratch_shapes=[pltpu.VMEM((2 * STATE_ROWS, STATE_LANES), F32)],
        compiler_params=_params(("arbitrary",)),
    )(bu, pw)


def _scan_bwd(dxs, xs, pw, *, name):
    S = dxs.shape[0] // _STATE_TILE
    tc = _tile(S, _SCAN_CHUNK, 8)
    nc = S // tc

    def body(dx_ref, xs_ref, pw_ref, g_ref, da_ref, st_ref):
        @pl.when(pl.program_id(0) == 0)
        def _():
            st_ref[...] = jnp.zeros_like(st_ref)
            da_ref[...] = jnp.zeros_like(da_ref)

        ar = pw_ref[0, _RE, :]
        ai = pw_ref[0, _IM, :]

        def block(i, carry):
            gr, gi, dar, dai = carry
            g = tc // SCAN_BLOCK - 1 - i
            cr = ci = None
            pgr, pgi = gr, gi
            for j in range(SCAN_BLOCK):
                tok = _token(g, SCAN_BLOCK - 1 - j)
                x = xs_ref[tok, :]
                xr, xi = x[_RE], x[_IM]
                dar = dar + pgr * xr + pgi * xi
                dai = dai + pgi * xr - pgr * xi
                d = dx_ref[tok, :]
                dr, di = d[_RE], d[_IM]
                cr, ci = (dr, di) if j == 0 else (ar * cr + ai * ci + dr, ar * ci - ai * cr + di)
                pr, pi = pw_ref[j, _RE, :], pw_ref[j, _IM, :]
                pgr = pr * gr + pi * gi + cr
                pgi = pr * gi - pi * gr + ci
                g_ref[tok, :] = jnp.concatenate([pgr, pgi], axis=0)
            return pgr, pgi, dar, dai

        init = (st_ref[_RE, :], st_ref[_IM, :], da_ref[_RE, :], da_ref[_IM, :])
        gr, gi, dar, dai = lax.fori_loop(0, tc // SCAN_BLOCK, block, init, unroll=2)
        st_ref[_RE, :] = gr
        st_ref[_IM, :] = gi
        da_ref[_RE, :] = dar
        da_ref[_IM, :] = dai

    blk = pl.BlockSpec((tc * _STATE_TILE, STATE_LANES), lambda i: (nc - 1 - i, 0))
    vec = pl.BlockSpec((2 * STATE_ROWS, STATE_LANES), lambda i: (0, 0))
    return pl.pallas_call(
        body, name=name, grid=(nc,), in_specs=[blk, blk, pl.BlockSpec(pw.shape, lambda i: (0, 0, 0))],
        out_specs=[blk, vec],
        out_shape=[jax.ShapeDtypeStruct(dxs.shape, F32), jax.ShapeDtypeStruct((2 * STATE_ROWS, STATE_LANES), F32)],
        scratch_shapes=[pltpu.VMEM((2 * STATE_ROWS, STATE_LANES), F32)],
        compiler_params=_params(("arbitrary",)),
    )(dxs, xs, pw)


def _loss_head(x, g, target, *, name):
    S, D = x.shape

    def body(r, f, o, acc, s):
        xv = r[0][...]
        gv = f[0][...]
        rs = lax.rsqrt(_mean(xv * xv) + EPS)
        xh = xv * rs
        err = xh * gv - r[1][...]
        acc[1][...] += 0.5 * jnp.sum(_mean(err * err), axis=0, keepdims=True)
        dy = err * (1.0 / D)
        dyg = dy * gv
        o[0][...] = rs * (dyg - xh * _mean(dyg * xh))
        acc[0][...] += jnp.sum(dy * xh, axis=0, keepdims=True)

    (dx,), (dg, loss) = _rows_call(name, _tile(S, 256, 8), [x, target], [_vec(g)], [(D, F32)],
                                   [((1, D), F32), ((1, 128), F32)], body)
    return dx, dg, loss


_ADAM_C1 = 1.0 - ADAM_B1 ** ADAM_STEP
_ADAM_C2 = 1.0 - ADAM_B2 ** ADAM_STEP
_ONE_BLOCK_BYTES = 8 * 1024 * 1024


def _adamw_math(w, g, m, v):
    nm = ADAM_B1 * m + (1.0 - ADAM_B1) * g
    nv = ADAM_B2 * v + (1.0 - ADAM_B2) * (g * g)
    m_hat = nm / _ADAM_C1
    v_hat = nv / _ADAM_C2
    return -ADAM_LR * (m_hat / (jnp.sqrt(v_hat) + ADAM_EPS) + ADAM_WD * w), nm, nv


def _adamw_shard(w, gsrc, m, v, *, name):
    R, C = w.shape
    n_l = len(gsrc)
    rows = R // n_l
    tr = rows
    for _, r0 in gsrc:
        tr = math.gcd(tr, r0) if r0 else tr
    tr = _tile(tr, 256, 8) if tr > 256 else tr
    nb = rows // tr
    assert rows % tr == 0 and all(r0 % tr == 0 for _, r0 in gsrc)

    def body(*refs):
        w_ref, g_refs, (m_ref, v_ref, go_ref, d_ref, nm_ref, nv_ref) = refs[0], refs[1:1 + n_l], refs[1 + n_l:]
        layer = pl.program_id(0) // nb
        gv = g_refs[0][...]
        for l in range(1, n_l):
            gv = jnp.where(layer == l, g_refs[l][...], gv)
        go_ref[...] = gv
        d_ref[...], nm_ref[...], nv_ref[...] = _adamw_math(w_ref[...], gv, m_ref[...], v_ref[...])

    def g_spec(l, r0):
        return pl.BlockSpec((tr, C), lambda i: (r0 // tr + jnp.clip(i - l * nb, 0, nb - 1), 0))

    blk = pl.BlockSpec((tr, C), lambda i: (i, 0))
    out = jax.ShapeDtypeStruct((R, C), F32)
    return pl.pallas_call(
        body, name=name, grid=(R // tr,),
        in_specs=[blk] + [g_spec(l, r0) for l, (_, r0) in enumerate(gsrc)] + [blk, blk], out_specs=[blk] * 4,
        out_shape=[out] * 4, compiler_params=_params(("parallel",)),
    )(w, *[g for g, _ in gsrc], m, v)


def _adamw_small(ws, gs, ms, vs, *, name):
    n = len(ws)

    def body(*refs):
        w_r, g_r, m_r, v_r = refs[:n], refs[n:2 * n], refs[2 * n:3 * n], refs[3 * n:4 * n]
        d_r, nm_r, nv_r = refs[4 * n:5 * n], refs[5 * n:6 * n], refs[6 * n:7 * n]
        for k in range(n):
            d_r[k][...], nm_r[k][...], nv_r[k][...] = _adamw_math(w_r[k][...], g_r[k][...], m_r[k][...], v_r[k][...])

    vm = pl.BlockSpec(memory_space=pltpu.VMEM)
    out = [jax.ShapeDtypeStruct(w.shape, F32) for w in ws]
    res = pl.pallas_call(body, name=name, in_specs=[vm] * (4 * n), out_specs=[vm] * (3 * n), out_shape=out * 3,
                         compiler_params=pltpu.CompilerParams(vmem_limit_bytes=VMEM_LIMIT))(*ws, *gs, *ms, *vs)
    return res[:n], res[n:2 * n], res[2 * n:]


def _sum_slots(x, *, name):
    n, R, C = x.shape
    tr = R if (n + 1) * R * C * 4 <= _ONE_BLOCK_BYTES else _tile(R, 256, 8)

    def body(x_ref, o_ref):
        acc = x_ref[0]
        for k in range(1, n):
            acc = acc + x_ref[k]
        o_ref[...] = acc

    return pl.pallas_call(
        body, name=name, grid=(R // tr,),
        in_specs=[pl.BlockSpec((n, tr, C), lambda i: (0, i, 0))], out_specs=pl.BlockSpec((tr, C), lambda i: (i, 0)),
        out_shape=jax.ShapeDtypeStruct((R, C), F32), compiler_params=_params(("parallel",)),
    )(x)


def _pair_sum(g, r, half, *, name):
    n, R, C = g.shape
    Rh = R // 2
    tr = _tile(Rh, 256, 8)
    nb = Rh // tr

    def body(half_ref, g_ref, r_ref, o_ref):
        o_ref[...] = (g_ref[...] + r_ref[...]).astype(BF16)

    return pl.pallas_call(
        body, name=name,
        grid_spec=pltpu.PrefetchScalarGridSpec(
            num_scalar_prefetch=1, grid=(n, nb),
            in_specs=[pl.BlockSpec((1, tr, C), lambda p, i, h: (p, h[0] * nb + i, 0)),
                      pl.BlockSpec((1, tr, C), lambda p, i, h: (p, i, 0))],
            out_specs=pl.BlockSpec((1, tr, C), lambda p, i, h: (p, i, 0)),
        ),
        out_shape=jax.ShapeDtypeStruct((n, Rh, C), BF16), compiler_params=_params(("parallel", "parallel")),
    )(half, g, r)


def _chip_sum(g, r, slots, where, *, name):
    n, R, C = g.shape
    Rh = R // 2
    tr = _tile(Rh, 256, 8)
    nb = Rh // tr

    def body(w_ref, g_ref, r_ref, s_ref, o_ref):
        acc = g_ref[0] + r_ref[0]
        for k in range(slots.shape[0]):
            acc = acc + s_ref[k].astype(F32)
        o_ref[...] = acc

    return pl.pallas_call(
        body, name=name,
        grid_spec=pltpu.PrefetchScalarGridSpec(
            num_scalar_prefetch=1, grid=(nb,),
            in_specs=[pl.BlockSpec((1, tr, C), lambda i, w: (w[0], w[1] * nb + i, 0)),
                      pl.BlockSpec((1, tr, C), lambda i, w: (w[0], i, 0)),
                      pl.BlockSpec((slots.shape[0], tr, C), lambda i, w: (0, i, 0))],
            out_specs=pl.BlockSpec((tr, C), lambda i, w: (w[1] * nb + i, 0)),
        ),
        out_shape=jax.ShapeDtypeStruct((R, C), F32), compiler_params=_params(("parallel",)),
    )(where, g, r, slots)


ANY = pl.BlockSpec(memory_space=pl.ANY)


def _place():
    return lax.axis_index("x"), lax.axis_index("y"), lax.axis_index("c")


def _other_chips(x, y):
    return [(1 - x, y), (x, 1 - y), (1 - x, 1 - y)]


def _allgather_small(v, *, name):
    R, C = v.shape

    def body(x_ref, out_ref, send_sems, recv_sems, local_sem):
        x, y, c = _place()
        me, sibling = (x, y, c), (x, y, 1 - c)
        chips = _other_chips(x, y)

        def rows(px, py, pc):
            return out_ref.at[pl.ds((4 * px + 2 * py + pc) * R, R), :]

        def copy(k, block, to, src=None):
            return pltpu.make_async_remote_copy(
                src_ref=rows(*block) if src is None else src, dst_ref=rows(*block),
                send_sem=send_sems.at[k], recv_sem=recv_sems.at[k], device_id=to, device_id_type=MESH)

        mine = pltpu.make_async_copy(x_ref, rows(*me), local_sem)
        mine.start()
        first = [copy(0, me, sibling, src=x_ref)]
        first += [copy(1 + j, me, (*chip, c), src=x_ref) for j, chip in enumerate(chips)]
        for cp in first:
            cp.start()
        passed = [copy(4 + j, (*chip, c), sibling) for j, chip in enumerate(chips)]
        for j, chip in enumerate(chips):
            copy(1 + j, (*chip, c), me).wait_recv()
            passed[j].start()
        copy(0, sibling, me).wait_recv()
        for j, chip in enumerate(chips):
            copy(4 + j, (*chip, 1 - c), me).wait_recv()
        for cp in first + passed:
            cp.wait_send()
        mine.wait()

    return pl.pallas_call(
        body, name=name, out_shape=jax.ShapeDtypeStruct((N_DEV * R, C), v.dtype),
        in_specs=[pl.BlockSpec(memory_space=pltpu.VMEM)], out_specs=pl.BlockSpec(memory_space=pltpu.VMEM),
        scratch_shapes=[pltpu.SemaphoreType.DMA((7,)), pltpu.SemaphoreType.DMA((7,)), pltpu.SemaphoreType.DMA],
        compiler_params=pltpu.CompilerParams(vmem_limit_bytes=VMEM_LIMIT),
    )(v)


def _aliased_comm_call(body, bufs, n_sems, *, name):
    n = len(bufs)
    return pl.pallas_call(
        body, name=name, out_shape=[jax.ShapeDtypeStruct(b.shape, b.dtype) for b in bufs],
        in_specs=[ANY] * n, out_specs=[ANY] * n, input_output_aliases={k: k for k in range(n)},
        scratch_shapes=[pltpu.SemaphoreType.DMA((n_sems,)), pltpu.SemaphoreType.DMA((n_sems,))],
    )(*bufs)


HBM = pl.BlockSpec(memory_space=pltpu.HBM)
SEM = pl.BlockSpec(memory_space=pltpu.SEMAPHORE)
_SPLIT = pltpu.CompilerParams(has_side_effects=pltpu.SideEffectType.DATAFLOW_SIDE_EFFECTING)


def _in_hbm(arrs):
    return [pltpu.with_memory_space_constraint(a, pltpu.HBM) for a in arrs]


def _gather_ici_start(bufs, after, *, name):
    n = len(bufs)

    def body(*refs):
        send_sems, recv_sems, outs, token = refs[n + 1], refs[n + 2], refs[n + 3:2 * n + 3], refs[2 * n + 3]
        x, y, c = _place()
        for b in range(n):
            rh = bufs[b].shape[1] // 2
            part = outs[b].at[2 * x + y, pl.ds(c * rh, rh), :]
            for j, chip in enumerate(_other_chips(x, y)):
                pltpu.make_async_remote_copy(src_ref=part, dst_ref=part, send_sem=send_sems.at[3 * b + j],
                                             recv_sem=recv_sems.at[3 * b + j], device_id=(*chip, c),
                                             device_id_type=MESH).start()
        token[...] = jnp.zeros_like(token)

    res = pl.pallas_call(
        body, name=name,
        out_shape=(pltpu.SemaphoreType.DMA((3 * n,)), pltpu.SemaphoreType.DMA((3 * n,)),
                   *[pltpu.HBM(b.shape, b.dtype) for b in bufs], jax.ShapeDtypeStruct((8, 128), F32)),
        in_specs=[HBM] * n + [ANY], out_specs=(SEM, SEM, *[HBM] * n, pl.BlockSpec(memory_space=pltpu.VMEM)),
        input_output_aliases={k: k + 2 for k in range(n)}, compiler_params=_SPLIT,
    )(*_in_hbm(bufs), after)
    return res[0], res[1], list(res[2:2 + n]), res[2 + n]


def _gather_ici_wait(send_sems, recv_sems, bufs, after, *, name):
    n = len(bufs)

    def body(*refs):
        ins, ss, rs = refs[:n], refs[n], refs[n + 1]
        x, y, c = _place()
        for b in range(n):
            rh = bufs[b].shape[1] // 2
            mine = ins[b].at[2 * x + y, pl.ds(c * rh, rh), :]
            for j, (cx, cy) in enumerate(_other_chips(x, y)):
                theirs = ins[b].at[2 * cx + cy, pl.ds(c * rh, rh), :]
                cp = pltpu.make_async_remote_copy(src_ref=mine, dst_ref=theirs, send_sem=ss.at[3 * b + j],
                                                  recv_sem=rs.at[3 * b + j], device_id=(cx, cy, c),
                                                  device_id_type=MESH)
                cp.wait_send()
                cp.wait_recv()

    return list(pl.pallas_call(
        body, name=name, out_shape=[pltpu.HBM(b.shape, b.dtype) for b in bufs],
        in_specs=[HBM] * n + [SEM, SEM, ANY], out_specs=[HBM] * n,
        input_output_aliases={k: k for k in range(n)}, compiler_params=_SPLIT,
    )(*bufs, send_sems, recv_sems, after))


def _gather_forward(bufs, *, name):
    n = len(bufs)

    def body(*refs):
        outs, send_sems, recv_sems = refs[n:2 * n], refs[2 * n], refs[2 * n + 1]
        x, y, c = _place()

        def copy(b, j, chip, hc):
            rh = bufs[b].shape[1] // 2
            part = outs[b].at[2 * chip[0] + chip[1], pl.ds(hc * rh, rh), :]
            return pltpu.make_async_remote_copy(src_ref=part, dst_ref=part, send_sem=send_sems.at[3 * b + j],
                                                recv_sem=recv_sems.at[3 * b + j], device_id=(x, y, 1 - c),
                                                device_id_type=MESH)

        sends = [copy(b, j, chip, c) for b in range(n) for j, chip in enumerate(_other_chips(x, y))]
        for cp in sends:
            cp.start()
        for b in range(n):
            for j, chip in enumerate(_other_chips(x, y)):
                copy(b, j, chip, 1 - c).wait_recv()
        for cp in sends:
            cp.wait_send()

    return _aliased_comm_call(body, bufs, 3 * n, name=name)


def _chip_exchange_start(hs, *, name):
    n = len(hs)
    lands = [lax.empty((3,) + h.shape[1:], h.dtype) for h in hs]

    def body(*refs):
        send_sems, recv_sems = refs[2 * n], refs[2 * n + 1]
        h_out, l_out, token = refs[2 * n + 2:3 * n + 2], refs[3 * n + 2:4 * n + 2], refs[4 * n + 2]
        x, y, c = _place()
        for b in range(n):
            for j, (cx, cy) in enumerate(_other_chips(x, y)):
                pltpu.make_async_remote_copy(src_ref=h_out[b].at[2 * cx + cy], dst_ref=l_out[b].at[j],
                                             send_sem=send_sems.at[3 * b + j], recv_sem=recv_sems.at[3 * b + j],
                                             device_id=(cx, cy, c), device_id_type=MESH).start()
        token[...] = jnp.zeros_like(token)

    res = pl.pallas_call(
        body, name=name,
        out_shape=(pltpu.SemaphoreType.DMA((3 * n,)), pltpu.SemaphoreType.DMA((3 * n,)),
                   *[pltpu.HBM(a.shape, a.dtype) for a in hs + lands], jax.ShapeDtypeStruct((8, 128), F32)),
        in_specs=[HBM] * (2 * n), out_specs=(SEM, SEM, *[HBM] * (2 * n), pl.BlockSpec(memory_space=pltpu.VMEM)),
        input_output_aliases={k: k + 2 for k in range(2 * n)}, compiler_params=_SPLIT,
    )(*_in_hbm(hs + lands))
    return res[0], res[1], list(res[2:2 + n]), list(res[2 + n:2 + 2 * n]), res[2 + 2 * n]


def _chip_exchange_wait(send_sems, recv_sems, hs, lands, after, *, name):
    n = len(hs)

    def body(*refs):
        h_in, l_in, ss, rs = refs[:n], refs[n:2 * n], refs[2 * n], refs[2 * n + 1]
        x, y, c = _place()
        for b in range(n):
            for j, (cx, cy) in enumerate(_other_chips(x, y)):
                cp = pltpu.make_async_remote_copy(src_ref=h_in[b].at[2 * cx + cy], dst_ref=l_in[b].at[j],
                                                  send_sem=ss.at[3 * b + j], recv_sem=rs.at[3 * b + j],
                                                  device_id=(cx, cy, c), device_id_type=MESH)
                cp.wait_send()
                cp.wait_recv()

    res = pl.pallas_call(
        body, name=name, out_shape=[pltpu.HBM(a.shape, a.dtype) for a in hs + lands],
        in_specs=[HBM] * (2 * n) + [SEM, SEM, ANY], out_specs=[HBM] * (2 * n),
        input_output_aliases={k: k for k in range(2 * n)}, compiler_params=_SPLIT,
    )(*hs, *lands, send_sems, recv_sems, after)
    return list(res[n:])


def _peers(x, y, c):
    return [((1 - x) if fx else x, (1 - y) if fy else y, (1 - c) if fc else c)
            for fx in (0, 1) for fy in (0, 1) for fc in (0, 1) if fx or fy or fc]


def _all_to_all_start(slab, after, *, name):
    land = lax.empty((N_DEV,) + slab.shape, slab.dtype)

    def body(slab_in, land_in, after_ref, send_sems, recv_sems, slab_out, land_out, token):
        x, y, c = _place()
        for k, peer in enumerate(_peers(x, y, c)):
            pltpu.make_async_remote_copy(src_ref=slab_out, dst_ref=land_out.at[4 * x + 2 * y + c],
                                         send_sem=send_sems.at[k], recv_sem=recv_sems.at[k], device_id=peer,
                                         device_id_type=MESH).start()
        token[...] = jnp.zeros_like(token)

    return pl.pallas_call(
        body, name=name,
        out_shape=(pltpu.SemaphoreType.DMA((N_DEV - 1,)), pltpu.SemaphoreType.DMA((N_DEV - 1,)),
                   pltpu.HBM(slab.shape, slab.dtype), pltpu.HBM(land.shape, land.dtype),
                   jax.ShapeDtypeStruct((8, 128), F32)),
        in_specs=[HBM, HBM, ANY], out_specs=(SEM, SEM, HBM, HBM, pl.BlockSpec(memory_space=pltpu.VMEM)),
        input_output_aliases={0: 2, 1: 3}, compiler_params=_SPLIT,
    )(*_in_hbm([slab, land]), after)


def _all_to_all_wait(send_sems, recv_sems, slab, land, after, *, name):
    def body(slab_in, land_in, ss, rs, after_ref, slab_out, land_out):
        x, y, c = _place()
        for k, (px, py, pc) in enumerate(_peers(x, y, c)):
            cp = pltpu.make_async_remote_copy(src_ref=slab_in, dst_ref=land_in.at[4 * px + 2 * py + pc],
                                              send_sem=ss.at[k], recv_sem=rs.at[k], device_id=(px, py, pc),
                                              device_id_type=MESH)
            cp.wait_send()
            cp.wait_recv()

    return pl.pallas_call(
        body, name=name, out_shape=[pltpu.HBM(slab.shape, slab.dtype), pltpu.HBM(land.shape, land.dtype)],
        in_specs=[HBM, HBM, SEM, SEM, ANY], out_specs=[HBM, HBM], input_output_aliases={0: 0, 1: 1},
        compiler_params=_SPLIT,
    )(slab, land, send_sems, recv_sems, after)


def _pair_exchange(gs, *, name):
    n = len(gs)

    def body(*refs):
        ins, outs, send_sems, recv_sems = refs[:n], refs[n:2 * n], refs[2 * n], refs[2 * n + 1]
        x, y, c = _place()
        cps = []
        for b in range(n):
            rh = gs[b].shape[1] // 2
            cps.append(pltpu.make_async_remote_copy(
                src_ref=ins[b].at[:, pl.ds((1 - c) * rh, rh), :], dst_ref=outs[b], send_sem=send_sems.at[b],
                recv_sem=recv_sems.at[b], device_id=(x, y, 1 - c), device_id_type=MESH))
        for cp in cps:
            cp.start()
        for cp in cps:
            cp.wait()

    return pl.pallas_call(
        body, name=name, out_shape=[jax.ShapeDtypeStruct((g.shape[0], g.shape[1] // 2, g.shape[2]), g.dtype) for g in gs],
        in_specs=[ANY] * n, out_specs=[ANY] * n,
        scratch_shapes=[pltpu.SemaphoreType.DMA((n,)), pltpu.SemaphoreType.DMA((n,))],
    )(*gs)


def _pair_share(ss, *, name):
    n = len(ss)

    def body(*refs):
        outs, send_sems, recv_sems = refs[n:2 * n], refs[2 * n], refs[2 * n + 1]
        x, y, c = _place()
        cps = []
        for b in range(n):
            rh = ss[b].shape[0] // 2
            mine = outs[b].at[pl.ds(c * rh, rh), :]
            cps.append(pltpu.make_async_remote_copy(src_ref=mine, dst_ref=mine, send_sem=send_sems.at[b],
                                                    recv_sem=recv_sems.at[b], device_id=(x, y, 1 - c),
                                                    device_id_type=MESH))
        for cp in cps:
            cp.start()
        for b, cp in enumerate(cps):
            rh = ss[b].shape[0] // 2
            theirs = outs[b].at[pl.ds((1 - c) * rh, rh), :]
            pltpu.make_async_remote_copy(src_ref=theirs, dst_ref=theirs, send_sem=send_sems.at[b],
                                         recv_sem=recv_sems.at[b], device_id=(x, y, 1 - c),
                                         device_id_type=MESH).wait_recv()
            cp.wait_send()

    return _aliased_comm_call(body, ss, n, name=name)


_SMALL_SHARDED = (("e_conv_w", 2), ("o_norm", 1), ("o_d", 1))
_REPLICATED = ("e_norm", "e_gmlp_w", "e_gmlp_b", "e_conv_b", "e_conv_ln_g", "e_conv_ln_b", "o_lam_re", "o_lam_im",
               "o_log_dt", "o_b_re", "o_b_im", "o_c_re", "o_c_im", "ca_norm", "ca_mem_norm", "ffn_norm", "final_norm")
_SMALL = tuple(n for n, _ in _SMALL_SHARDED) + _REPLICATED
_WEIGHTS = ("e_norm", "e_w_in", "e_gmlp_w", "e_gmlp_b", "e_conv_w", "e_conv_b", "e_conv_ln_g", "e_conv_ln_b",
            "e_w_out", "o_norm", "o_w_in", "o_lam_re", "o_lam_im", "o_log_dt", "o_b_re", "o_b_im", "o_c_re", "o_c_im",
            "o_d", "o_w_out", "ca_norm", "ca_mem_norm", "ca_wq", "ca_wk", "ca_wv", "ca_wo", "ffn_norm", "ffn_w_gate",
            "ffn_w_up", "ffn_w_down", "final_norm")


def _pack_rows(arrs, width, dtype, row_mult=8):
    parts, spans, r0 = [], [], 0
    for a in arrs:
        flat = a.reshape(-1).astype(dtype)
        rows = -(-flat.shape[0] // (width * row_mult)) * row_mult
        if rows * width != flat.shape[0]:
            flat = jnp.pad(flat, (0, rows * width - flat.shape[0]))
        parts.append(flat.reshape(rows, width))
        spans.append((r0, rows))
        r0 += rows
    return jnp.concatenate(parts, axis=0), spans


def _unpack_rows(slab, spans, shapes):
    out = []
    for (r0, rows), shp in zip(spans, shapes):
        n = math.prod(shp)
        out.append(slab[r0:r0 + rows].reshape(-1)[:n].reshape(shp))
    return out


def _two_d(a):
    return a.reshape(-1, a.shape[-1])


def _shard_rows(n, a):
    return _two_d(jnp.swapaxes(a, -1, -2) if n in _TRANSPOSED else a)


def _from_shard_rows(n, rows, shape):
    if n in _TRANSPOSED:
        return jnp.swapaxes(rows.reshape(shape[:-2] + (shape[-1], shape[-2])), -1, -2)
    return rows.reshape(shape)


def _local_slab(local, slab, dtype):
    parts = sorted((r0, n, l) for n, (_, where) in _PLACE.items() for l, (s, r0) in enumerate(where) if s == slab)
    shards = [_shard_rows(n, local[n] if len(_PLACE[n][1]) == 1 else local[n][l]) for _, n, l in parts]
    return jnp.concatenate([a.astype(dtype) for a in shards], axis=0)


def _block_diag(b, pattern):
    return jnp.einsum(pattern, b, jnp.eye(C_GROUPS, dtype=b.dtype))


def _s5_discretize(lam_re, lam_im, log_dt, b_re, b_im):
    dt = jnp.exp(log_dt)[:, None]
    mag = jnp.exp(lam_re * dt)
    ar = mag * jnp.cos(lam_im * dt)
    ai = mag * jnp.sin(lam_im * dt)
    den = lam_re * lam_re + lam_im * lam_im
    qr = ((ar - 1.0) * lam_re + ai * lam_im) / den
    qi = (ai * lam_re - (ar - 1.0) * lam_im) / den
    bbr = qr[..., None] * b_re - qi[..., None] * b_im
    bbi = qr[..., None] * b_im + qi[..., None] * b_re
    return ar, ai, bbr, bbi


def _attention_block(x, mem, W, w, i, tag):
    xn, q = _norm_mm(x, w["ca_norm"][i], _shards(W, "ca_wq", i), split="k", out_dtype=BF16, name=f"{tag}_q")
    memn = _rms_fwd(mem, w["ca_mem_norm"][i], name=f"{tag}_ca_memnorm")
    k = _mm_k(memn, _shards(W, "ca_wk", i), out_dtype=BF16, name=f"{tag}_k")
    v = _mm_k(memn, _shards(W, "ca_wv", i), out_dtype=BF16, name=f"{tag}_v")
    o = _attn_fwd(q, k, v, name=f"{tag}_attn")
    y = _mm_k(o, _shards(W, "ca_wo", i), add=x, name=f"{tag}_wo")
    return y, (x, xn, memn, q, k, v, o)


def _attention_block_bwd(dy, saved, mem, W, w, i, tag, G, grads, token=None):
    x, xn, memn, q, k, v, o = saved
    if token is not None:
        k = _behind(k, token)
    G = _grad_to_slab(G, "ca_wo", i, o, dy, a_cols=256, name=f"{tag}_dwo")
    dq, dk, dv = _attn_bwd(dy, _shards(W, "ca_wo", i), q, k, v, name=f"{tag}_attn_bwd")
    G = _grad_to_slab(G, "ca_wq", i, xn, dq, a_cols=256, name=f"{tag}_dwq")
    G = _grad_to_slab(G, "ca_wk", i, memn, dk, a_cols=256, name=f"{tag}_dwk")
    G = _grad_to_slab(G, "ca_wv", i, memn, dv, a_cols=256, name=f"{tag}_dwv")
    dmemn = _mm_k_t([(dk, _shards(W, "ca_wk", i)), (dv, _shards(W, "ca_wv", i))], name=f"{tag}_dmemn")
    dx, dg = _norm_bwd_k(dq, _shards(W, "ca_wq", i), x, w["ca_norm"][i], dy, name=f"{tag}_dq_norm_bwd")
    grads["ca_norm"][i] = dg[0]
    grads["ca_mem_norm"][i] = _rms_dg(mem, w["ca_mem_norm"][i], dmemn, name=f"{tag}_ca_memnorm_bwd")[0]
    return dx, G


def _ffn_block(x, W, w, i, tag):
    fn, gate, up, h = _ffn_up(x, w["ffn_norm"][i], _shards(W, "ffn_w_gate", i), _shards(W, "ffn_w_up", i),
                              name=f"{tag}_ffn_up")
    y = _mm_k(h, _shards(W, "ffn_w_down", i), add=x, name=f"{tag}_down")
    return y, (x, fn, gate, up, h)


def _ffn_block_bwd(dy, saved, W, w, i, tag, G, grads, token=None):
    x, fn, gate, up, h = saved
    G = _grad_to_slab(G, "ffn_w_down", i, h, dy, name=f"{tag}_dwd")
    dg, du = _ffn_bwd_hidden(dy, _shards(W, "ffn_w_down", i), gate, up, token, name=f"{tag}_ffn_bwd_hidden")
    G = _grad_to_slab(G, "ffn_w_gate", i, dg, fn, name=f"{tag}_dwg")
    G = _grad_to_slab(G, "ffn_w_up", i, du, fn, name=f"{tag}_dwu")
    dx, dgn = _ffn_in_bwd(dg, du, _shards(W, "ffn_w_gate", i), _shards(W, "ffn_w_up", i), x, w["ffn_norm"][i], dy,
                          name=f"{tag}_ffn_in_bwd")
    grads["ffn_norm"][i] = dgn[0]
    return dx, G


def _gmlp_mask():
    chunk = jnp.arange(GMLP_BLOCK) // CHUNK
    return chunk[None, :] <= chunk[:, None]


def _even_block(x, W, w, tag):
    hn, proj = _norm_mm(x, w["e_norm"][0], _shards(W, "e_w_in"), split="n", out_dtype=F32, name=f"{tag}_w_in")
    wm = jnp.where(_gmlp_mask()[None], w["e_gmlp_w"][0], 0.0).astype(BF16)
    bcol = w["e_gmlp_b"][0][:, :, None]
    cw = jnp.pad(w["e_conv_w"][0], ((0, CONV_HALO - CONV_WIDTH), (0, 0)))
    cb, lg, lb = w["e_conv_b"], w["e_conv_ln_g"], w["e_conv_ln_b"]
    mix, hc = _even_fwd(proj, wm, bcol, cw, cb, lg, lb, name=f"{tag}_mixers")
    y = _mm_k(mix, _shards(W, "e_w_out"), add=x, name=f"{tag}_w_out")
    return y, (x, hn, proj, mix, hc, wm, bcol, cw)


def _even_block_bwd(dy, saved, W, w, tag, G, grads):
    x, hn, proj, mix, hc, wm, bcol, cw = saved
    dmix = _mm_k_t([(dy, _shards(W, "e_w_out"))], name=f"{tag}_dmix")
    G = _grad_to_slab(G, "e_w_out", 0, mix, dy, a_cols=256, name=f"{tag}_dw_out")
    wmt = jnp.swapaxes(wm, 1, 2)
    dpa, dhc, dwm, db, dlg, dlb, dcb = _even_bwd1(proj, dmix, hc, wm, wmt, bcol, w["e_conv_ln_g"], w["e_conv_ln_b"],
                                                  name=f"{tag}_mixers_bwd1")
    dpb, dcw = _even_bwd2(proj, dhc, cw, name=f"{tag}_mixers_bwd2")
    grads["e_gmlp_w"] = jnp.where(_gmlp_mask()[None], dwm, 0.0)[None]
    grads["e_gmlp_b"] = db[:, :, 0][None]
    grads["e_conv_ln_g"], grads["e_conv_ln_b"], grads["e_conv_b"] = dlg, dlb, dcb
    grads["e_conv_w"] = dcw[:CONV_WIDTH][None]
    G = _grad_to_slab(G, "e_w_in", 0, hn, dpa, b_cols=512, chips=(0, 2), name=f"{tag}_dw_in_a")
    G = _grad_to_slab(G, "e_w_in", 0, hn, dpb, b_cols=512, chips=(2, 2), name=f"{tag}_dw_in_b")
    dx, dg = _norm_bwd_n((dpa, dpb), _shards(W, "e_w_in"), x, w["e_norm"][0], dy, name=f"{tag}_in_bwd")
    grads["e_norm"] = dg
    return dx, G


def _odd_block(x, W, w, tag):
    S = x.shape[0]
    hn, u = _norm_mm(x, w["o_norm"][0], _shards(W, "o_w_in"), split="k", out_dtype=F32, name=f"{tag}_w_in")
    disc_in = (w["o_lam_re"][0], w["o_lam_im"][0], w["o_log_dt"][0], w["o_b_re"][0], w["o_b_im"][0])
    (ar, ai, bbr, bbi), disc_vjp = jax.vjp(_s5_discretize, *disc_in)
    bd = jnp.concatenate([_block_diag(bbr, "gpc,gh->gchp").reshape(C_WIDTH, N_STATE),
                          _block_diag(bbi, "gpc,gh->gchp").reshape(C_WIDTH, N_STATE)], axis=1).astype(BF16)
    cd = jnp.concatenate([_block_diag(w["o_c_re"][0], "gcp,gh->gphc").reshape(N_STATE, C_WIDTH),
                          -_block_diag(w["o_c_im"][0], "gcp,gh->gphc").reshape(N_STATE, C_WIDTH)], axis=0).astype(BF16)
    powers, pr, pi = [], ar, ai
    for _ in range(SCAN_BLOCK):
        powers.append(jnp.concatenate([pr.reshape(STATE_ROWS, STATE_LANES), pi.reshape(STATE_ROWS, STATE_LANES)], 0))
        pr, pi = pr * ar - pi * ai, pr * ai + pi * ar
    pw = jnp.stack(powers, axis=0)
    state_rows = (S * _STATE_TILE, STATE_LANES)
    bu = _mm_to_state(u, bd, name=f"{tag}_bu")
    xs = _scan_fwd(bu.reshape(state_rows), pw, name=f"{tag}_scan").reshape(bu.shape)
    yv, yg = _s5_readout(xs, cd, u, w["o_d"], name=f"{tag}_readout")
    o, y = _glu_out(yg, _shards(W, "o_w_out"), x, name=f"{tag}_glu_out")
    return y, (x, hn, u, bd, cd, pw, xs, yv, yg, o, disc_vjp)


def _odd_block_bwd(dy, saved, W, w, tag, G, grads):
    x, hn, u, bd, cd, pw, xs, yv, yg, o, disc_vjp = saved
    S = x.shape[0]
    state_rows = (S * _STATE_TILE, STATE_LANES)
    do, dys, dus, dd = _glu_out_bwd(o, dy, _shards(W, "o_w_out"), yv, u, w["o_d"], name=f"{tag}_glu_out_bwd")
    G = _grad_to_slab(G, "o_w_out", 0, yg, do, b_cols=512, name=f"{tag}_dw_out")
    grads["o_d"] = dd
    dxs = _mm_to_state(dys, cd, nt=True, name=f"{tag}_dxs")
    dcd_t = _state_grad_tn(dys, xs, name=f"{tag}_dcd")
    gs, da = _scan_bwd(dxs.reshape(state_rows), xs.reshape(state_rows), pw, name=f"{tag}_scan_bwd")
    gs = gs.reshape(xs.shape)
    dbd = _state_grad_tn(u, gs, name=f"{tag}_dbd")
    du, dx, dg = _s5_in_bwd(gs, bd, dus, _shards(W, "o_w_in"), x, w["o_norm"][0], dy, name=f"{tag}_in_bwd")
    G = _grad_to_slab(G, "o_w_in", 0, hn, du, a_cols=256, name=f"{tag}_dw_in")
    grads["o_norm"] = dg
    eye = jnp.eye(C_GROUPS, dtype=F32)
    dcr = jnp.einsum("hcgp,gh->gcp", dcd_t[:, :N_STATE].reshape(C_GROUPS, C_GROUP_CH, C_GROUPS, C_STATE), eye)
    dci = -jnp.einsum("hcgp,gh->gcp", dcd_t[:, N_STATE:].reshape(C_GROUPS, C_GROUP_CH, C_GROUPS, C_STATE), eye)
    dbbr = jnp.einsum("gchp,gh->gpc", dbd[:, :N_STATE].reshape(C_GROUPS, C_GROUP_CH, C_GROUPS, C_STATE), eye)
    dbbi = jnp.einsum("gchp,gh->gpc", dbd[:, N_STATE:].reshape(C_GROUPS, C_GROUP_CH, C_GROUPS, C_STATE), eye)
    dar = da[:STATE_ROWS].reshape(C_GROUPS, C_STATE)
    dai = da[STATE_ROWS:].reshape(C_GROUPS, C_STATE)
    dlr, dli, dldt, dbr, dbi = disc_vjp((dar, dai, dbbr, dbbi))
    grads["o_lam_re"], grads["o_lam_im"], grads["o_log_dt"] = dlr[None], dli[None], dldt[None]
    grads["o_b_re"], grads["o_b_im"], grads["o_c_re"], grads["o_c_im"] = dbr[None], dbi[None], dcr[None], dci[None]
    return dx, G


def _behind(value, token):
    return value + token[0, 0].astype(value.dtype)


class _NoExchange:
    def __init__(self, W):
        self.W = W

    def first_weights(self, w):
        return self.W, w

    def weights(self, stage, after):
        return {}

    def after_layer1_backward(self, G):
        return None

    def after_ffn0_backward(self, G):
        return None


def _forward_backward(xs_, mems_, tgt, w, G, exchange):
    W, w = exchange.first_weights(w)
    x1, s_mix0 = _even_block(xs_, W, w, "l0")
    W = {**W, **exchange.weights(1, x1)}
    x2, s_att0 = _attention_block(x1, mems_, W, w, 0, "l0")
    W = {**W, **exchange.weights(2, x2)}
    x3, s_ffn0 = _ffn_block(x2, W, w, 0, "l0")
    W = {**W, **exchange.weights(3, x3)}
    x4, s_mix1 = _odd_block(x3, W, w, "l1")
    x5, s_att1 = _attention_block(x4, mems_, W, w, 1, "l1")
    x6, s_ffn1 = _ffn_block(x5, W, w, 1, "l1")
    dx, dfinal, loss_lanes = _loss_head(x6, w["final_norm"], tgt, name="loss_head")

    grads = {n: [None, None] for n in ("ca_norm", "ca_mem_norm", "ffn_norm")}
    grads["final_norm"] = dfinal[0]
    dx, G = _ffn_block_bwd(dx, s_ffn1, W, w, 1, "l1", G, grads)
    dx, G = _attention_block_bwd(dx, s_att1, mems_, W, w, 1, "l1", G, grads)
    dx, G = _odd_block_bwd(dx, s_mix1, W, w, "l1", G, grads)
    token = exchange.after_layer1_backward(G)
    dx, G = _ffn_block_bwd(dx, s_ffn0, W, w, 0, "l0", G, grads, token)
    token = exchange.after_ffn0_backward(G)
    dx, G = _attention_block_bwd(dx, s_att0, mems_, W, w, 0, "l0", G, grads, token)
    dx, G = _even_block_bwd(dx, s_mix0, W, w, "l0", G, grads)
    for n in list(grads):
        if isinstance(grads[n], list):
            grads[n] = jnp.stack(grads[n], axis=0)
        grads[n] = grads[n].reshape(w[n].shape)
    return loss_lanes, dx, G, grads


class _Exchange:
    def __init__(self, local, chip, core):
        self.bufs = {s: lax.dynamic_update_slice(lax.empty((N_CHIPS, rows, width), BF16),
                                                 _local_slab(local, s, BF16)[None], (chip, 0, 0))
                     for s, (width, rows) in _SLABS.items()}
        self.half = core.reshape(1).astype(jnp.int32)
        self.where = jnp.stack([chip, core]).astype(jnp.int32)
        self.flights = []

    def weights(self, stage, after):
        send_sems, recv_sems, bufs, _ = self.flights[stage]
        bufs = _gather_ici_wait(send_sems, recv_sems, bufs, after, name=f"gather_stage{stage}_wait")
        return dict(zip(_STAGES[stage], _gather_forward(bufs, name=f"gather_stage{stage}_forward")))

    def first_weights(self, w):
        after = w["e_conv_w"].reshape(-1)[:STATE_LANES]
        for k, stage in enumerate(_STAGES):
            self.flights.append(_gather_ici_start([self.bufs[s] for s in stage], after, name=f"gather_stage{k}_start"))
            after = self.flights[-1][3]
        return self.weights(0, after), {**w, "e_norm": _behind(w["e_norm"], after)}

    def reduce_start(self, G, slabs, tag):
        gl = [G[s] for s in slabs]
        other = _pair_exchange(gl, name=f"grad_{tag}_pair_exchange")
        pairs = [_pair_sum(g, r, self.half, name=f"grad_pair_sum_{s}") for s, g, r in zip(slabs, gl, other)]
        send_sems, recv_sems, pairs, lands, token = _chip_exchange_start(pairs, name=f"grad_{tag}_chip_start")
        return (slabs, gl, other, send_sems, recv_sems, pairs, lands), token

    def reduce_finish(self, state, after, tag):
        slabs, gl, other, send_sems, recv_sems, pairs, lands = state
        slots = _chip_exchange_wait(send_sems, recv_sems, pairs, lands, after, name=f"grad_{tag}_chip_wait")
        halves = [_chip_sum(g, r, sl, self.where, name=f"grad_chip_sum_{s}")
                  for s, g, r, sl in zip(slabs, gl, other, slots)]
        return dict(zip(slabs, _pair_share(halves, name=f"grad_{tag}_pair_share")))

    def after_layer1_backward(self, G):
        self.layer1_reduce, token = self.reduce_start(G, _LAYER1_SLABS, "l1")
        return token

    def after_ffn0_backward(self, G):
        self.ffn0_reduce, token = self.reduce_start(G, _FFN0_SLABS, "ffn0")
        return token


def kernel(x, mem, e_norm, e_w_in, e_gmlp_w, e_gmlp_b, e_conv_w, e_conv_b, e_conv_ln_g, e_conv_ln_b, e_w_out, o_norm, o_w_in, o_lam_re, o_lam_im, o_log_dt, o_b_re, o_b_im, o_c_re, o_c_im, o_d, o_w_out, ca_norm, ca_mem_norm, ca_wq, ca_wk, ca_wv, ca_wo, ffn_norm, ffn_w_gate, ffn_w_up, ffn_w_down, final_norm, loss_target, m_e_norm, m_e_w_in, m_e_gmlp_w, m_e_gmlp_b, m_e_conv_w, m_e_conv_b, m_e_conv_ln_g, m_e_conv_ln_b, m_e_w_out, m_o_norm, m_o_w_in, m_o_lam_re, m_o_lam_im, m_o_log_dt, m_o_b_re, m_o_b_im, m_o_c_re, m_o_c_im, m_o_d, m_o_w_out, m_ca_norm, m_ca_mem_norm, m_ca_wq, m_ca_wk, m_ca_wv, m_ca_wo, m_ffn_norm, m_ffn_w_gate, m_ffn_w_up, m_ffn_w_down, m_final_norm, v_e_norm, v_e_w_in, v_e_gmlp_w, v_e_gmlp_b, v_e_conv_w, v_e_conv_b, v_e_conv_ln_g, v_e_conv_ln_b, v_e_w_out, v_o_norm, v_o_w_in, v_o_lam_re, v_o_lam_im, v_o_log_dt, v_o_b_re, v_o_b_im, v_o_c_re, v_o_c_im, v_o_d, v_o_w_out, v_ca_norm, v_ca_mem_norm, v_ca_wq, v_ca_wk, v_ca_wv, v_ca_wo, v_ffn_norm, v_ffn_w_gate, v_ffn_w_up, v_ffn_w_down, v_final_norm):
    args = dict(locals())
    local = {n: args[n] for n in _WEIGHTS}
    mom = {n: args["m_" + n] for n in _WEIGHTS}
    vel = {n: args["v_" + n] for n in _WEIGHTS}
    chip = 2 * lax.axis_index("x") + lax.axis_index("y")
    core = lax.axis_index("c")
    xs_, mems_, tgt = x[0], mem[0], loss_target[0]

    w = {n: local[n] for n in _REPLICATED}
    sm_slab, sm_spans = _pack_rows([local[n] for n, _ in _SMALL_SHARDED], SMALL_W, F32)
    sm_all = _allgather_small(sm_slab, name="gather_small_weights").reshape(N_DEV, -1, SMALL_W)
    for (n, ax), span in zip(_SMALL_SHARDED, sm_spans):
        shp = local[n].shape
        w[n] = jnp.concatenate([_unpack_rows(sm_all[2 * p], [span], [shp])[0] for p in range(N_CHIPS)], axis=ax)

    exchange = _Exchange(local, chip, core)
    G = {s: lax.empty((N_CHIPS, rows, width), F32) for s, (width, rows) in _SLABS.items()}
    loss_lanes, dx, G, grads = _forward_backward(xs_, mems_, tgt, w, G, exchange)

    gs_slab, gs_spans = _pack_rows([grads[n] for n in _SMALL] + [loss_lanes], SMALL_W, F32)
    small_flight = _all_to_all_start(gs_slab, dx, name="small_grads_start")
    gsum = exchange.reduce_finish(exchange.layer1_reduce, small_flight[4], "l1")
    gsum = {**gsum, **exchange.reduce_finish(exchange.ffn0_reduce, small_flight[4], "ffn0")}
    rest0_reduce, token = exchange.reduce_start(G, _REST0_SLABS, "rest0")

    gs_slab, gs_all = _all_to_all_wait(*small_flight[:4], token, name="small_grads_wait")
    gs_all = lax.dynamic_update_slice(gs_all, gs_slab[None], (2 * chip + core, 0, 0))
    gs_sum = _sum_slots(gs_all, name="small_grad_sum")
    *small_sums, loss_sum = _unpack_rows(gs_sum, gs_spans, [grads[n].shape for n in _SMALL] + [loss_lanes.shape])
    out_grads = dict(zip(_SMALL, small_sums))
    for n, ax in _SMALL_SHARDED:
        width = local[n].shape[ax]
        out_grads[n] = lax.dynamic_slice_in_dim(out_grads[n], chip * width, width, axis=ax)

    delta, new_m, new_v = {}, {}, {}
    d_, m_, v_ = _adamw_small([_two_d(local[n]) for n in _SMALL], [_two_d(out_grads[n]) for n in _SMALL],
                              [_two_d(mom[n]) for n in _SMALL], [_two_d(vel[n]) for n in _SMALL], name="adamw_small")
    for n, dd, mm_, vv in zip(_SMALL, d_, m_, v_):
        shp = local[n].shape
        delta[n], new_m[n], new_v[n] = dd.reshape(shp), mm_.reshape(shp), vv.reshape(shp)
    def adamw_large(names):
        for n in names:
            shp = local[n].shape
            g_, d_, m_, v_ = _adamw_shard(_shard_rows(n, local[n]), [(gsum[s], r0) for s, r0 in _PLACE[n][1]],
                                          _shard_rows(n, mom[n]), _shard_rows(n, vel[n]), name=f"adamw_{n}")
            out_grads[n], delta[n], new_m[n], new_v[n] = (_from_shard_rows(n, t, shp) for t in (g_, d_, m_, v_))

    ready = [n for n, (_, where) in _PLACE.items() if all(s in gsum for s, _ in where)]
    adamw_large(ready)
    done = jnp.concatenate([delta[n].reshape(-1)[:1] for n in ready + list(_SMALL[:1])])
    gsum = {**gsum, **exchange.reduce_finish(rest0_reduce, done, "rest0")}
    adamw_large([n for n in _PLACE if n not in ready])

    return (loss_sum[0, 0], dx[None], *[out_grads[n] for n in _WEIGHTS], *[delta[n] for n in _WEIGHTS],
            *[new_m[n] for n in _WEIGHTS], *[new_v[n] for n in _WEIGHTS])
```

```python
import functools
import math

import jax
import jax.numpy as jnp
from jax import lax
from jax.experimental import pallas as pl
from jax.experimental.pallas import tpu as pltpu

F32 = jnp.float32
BF16 = jnp.bfloat16
MESH = pl.DeviceIdType.MESH

EPS = 1e-6
D_MODEL = 1024
A_WIDTH = 512
A_GROUPS = 4
GMLP_BLOCK = 128
CHUNK = 64
B_WIDTH = 512
CONV_WIDTH = 31
CONV_HALO = 32
C_WIDTH = 512
C_GROUP_CH = 16
C_GROUPS = 32
C_STATE = 64
N_STATE = C_GROUPS * C_STATE
STATE_LANES = 128
STATE_ROWS = N_STATE // STATE_LANES
SCAN_BLOCK = 8
CA_HEADS = 4
CA_HEAD_DIM = 256
FFN_HIDDEN = 2816

ADAM_LR = 0.001
ADAM_B1 = 0.9
ADAM_B2 = 0.999
ADAM_EPS = 1e-08
ADAM_WD = 0.01
ADAM_STEP = 10

VMEM_LIMIT = 56 * 1024 * 1024
ACC_BYTES = 6 * 1024 * 1024
TN_VMEM_BYTES = 44 * 1024 * 1024
SMALL_W = 128
N_CHIPS = 4
N_DEV = 8

_SLABS = {"D0": (512, 1024), "E0": (1024, 256), "A0": (1024, 1024), "B0": (1024, 704), "C0": (1024, 1408),
          "D1": (512, 768), "A1": (1024, 1024), "B1": (1024, 704), "C1": (1024, 1408)}
_STAGES = (("D0", "E0"), ("A0",), ("B0", "C0"), ("D1", "A1", "B1", "C1"))
_GRAD_PIECES = {"l1": _STAGES[3], "ffn0": _STAGES[2], "rest0": _STAGES[0] + _STAGES[1]}
_PLACE = {
    "e_w_in": (1024, (("D0", 0),)), "e_w_out": (256, (("E0", 0),)),
    "o_w_out": (512, (("D1", 0),)), "o_w_in": (256, (("D1", 512),)),
    "ca_wq": (256, (("A0", 0), ("A1", 0))), "ca_wk": (256, (("A0", 256), ("A1", 256))),
    "ca_wv": (256, (("A0", 512), ("A1", 512))), "ca_wo": (256, (("A0", 768), ("A1", 768))),
    "ffn_w_down": (704, (("B0", 0), ("B1", 0))),
    "ffn_w_gate": (704, (("C0", 0), ("C1", 0))), "ffn_w_up": (704, (("C0", 704), ("C1", 704))),
}
_TRANSPOSED = ("ffn_w_gate", "ffn_w_up")


def _params(sem=None):
    return pltpu.CompilerParams(dimension_semantics=sem, vmem_limit_bytes=VMEM_LIMIT)


def _tile(n, pref, mult=128):
    if n <= pref:
        return n
    t = (pref // mult) * mult
    while t >= mult:
        if n % t == 0:
            return t
        t -= mult
    return n


def _blk(name, layer=0):
    rows, where = _PLACE[name]
    slab, r0 = where[layer]
    assert r0 % rows == 0
    return slab, rows, r0 // rows


def _shards(slabs, name, layer=0):
    slab, rows, b = _blk(name, layer)
    return [(slabs[slab], (None, rows, _SLABS[slab][0]), (p, b, 0)) for p in range(N_CHIPS)]


_GELU_C = 0.7978845608028654
_GELU_A = 0.044715


def _gelu(x):
    t = jnp.tanh(_GELU_C * (x + _GELU_A * (x * x * x)))
    return 0.5 * x * (1.0 + t), t


def _gelu_grad(x, t):
    return 0.5 * (1.0 + t) + 0.5 * x * (1.0 - t * t) * (_GELU_C * (1.0 + 3.0 * _GELU_A * x * x))


def _sigmoid(x):
    return 1.0 / (1.0 + jnp.exp(-x))


def _mean(x):
    return jnp.mean(x, axis=-1, keepdims=True)


def _dot(a, b):
    return jnp.dot(a, b, preferred_element_type=F32)


def _dot_nt(a, b):
    return lax.dot_general(a, b, (((1,), (1,)), ((), ())), preferred_element_type=F32)


def _dot_tn(a, b):
    return lax.dot_general(a, b, (((0,), (0,)), ((), ())), preferred_element_type=F32)


def _rms_tile(xv, gv):
    return (xv * lax.rsqrt(_mean(xv * xv) + EPS)) * gv


def _rms_bwd_tile(xv, gv, dyv):
    r = lax.rsqrt(_mean(xv * xv) + EPS)
    xh = xv * r
    dyg = dyv * gv
    return r * (dyg - xh * _mean(dyg * xh)), jnp.sum(dyv * xh, axis=0, keepdims=True)


def _cols(p, width):
    return slice(p * width, (p + 1) * width)


def _sum_k(a, ws, k):
    tot = None
    for p in range(N_CHIPS):
        y = _dot(a[:, _cols(p, k)], ws[p][...])
        tot = y if tot is None else tot + y
    return tot


def _cat_nt(a, ws):
    return jnp.concatenate([_dot_nt(a, ws[p][...]) for p in range(N_CHIPS)], axis=1)


def _rows_call(name, tm, rows, fulls, outs, accs, body, scratch=()):
    S = min(x.shape[-2] for x in rows if x.ndim != 4)
    nr, nf, no, na = len(rows), len(fulls), len(outs), len(accs)

    def kern(*refs):
        r, f = refs[:nr], refs[nr:nr + nf]
        o, a = refs[nr + nf:nr + nf + no], refs[nr + nf + no:nr + nf + no + na]
        if na:
            @pl.when(pl.program_id(0) == 0)
            def _():
                for ref in a:
                    ref[...] = jnp.zeros_like(ref)
        body(r, f, o, a, refs[nr + nf + no + na:])

    def whole(shape):
        nd = len(shape)
        return pl.BlockSpec(tuple(shape), lambda i: (0,) * nd)

    def row_spec(shape):
        if len(shape) == 4:
            return pl.BlockSpec((tm // 8,) + tuple(shape[1:]), lambda i: (i, 0, 0, 0))
        if len(shape) == 3:
            return pl.BlockSpec((shape[0], tm, shape[2]), lambda i: (0, i, 0))
        return pl.BlockSpec((tm, shape[1]), lambda i: (i, 0))

    def full_spec(x):
        if isinstance(x, tuple):
            _, bshape, bidx = x
            return pl.BlockSpec(bshape, lambda i: bidx, pipeline_mode=pl.Buffered(1))
        return whole(x.shape)

    def out_shape_of(o):
        if o[0] == "state":
            return (S // 8, 2 * STATE_ROWS, 8, STATE_LANES)
        return (S, o[0]) if len(o) == 2 else (o[0], S, o[1])

    out_shapes = [out_shape_of(o) for o in outs]
    res = pl.pallas_call(
        kern, name=name, grid=(S // tm,),
        in_specs=[row_spec(x.shape) for x in rows] + [full_spec(x) for x in fulls],
        out_specs=[row_spec(s) for s in out_shapes] + [whole(shp) for shp, _ in accs],
        out_shape=[jax.ShapeDtypeStruct(s, o[-1]) for s, o in zip(out_shapes, outs)]
        + [jax.ShapeDtypeStruct(tuple(shp), dt) for shp, dt in accs],
        scratch_shapes=list(scratch),
        compiler_params=_params(("arbitrary",) if na else ("parallel",)),
    )(*rows, *[x[0] if isinstance(x, tuple) else x for x in fulls])
    return res[:no], res[no:]


def _grad_to_slab(gslabs, wname, layer, a, b, *, a_cols=None, b_cols=None, chips=(0, N_CHIPS), name):
    slab, rows, bidx = _blk(wname, layer)
    width = _SLABS[slab][0]
    p0, n_p = chips
    assert p0 % n_p == 0
    S = a.shape[-2]

    def tile_bytes(x, ts):
        return ts * x.dtype.itemsize * (x.shape[2] * n_p if x.ndim == 3 else x.shape[1])

    acc_bytes = n_p * rows * (-(-width // 128) * 128) * 4
    ts = next(t for t in (2048, 1024, 512, 256, S) if S % t == 0
              and 2 * (tile_bytes(a, t) + tile_bytes(b, t) + acc_bytes) <= TN_VMEM_BYTES or t == S)

    def operand(x):
        if x.ndim == 3:
            return pl.BlockSpec((n_p, ts, x.shape[2]), lambda s: (p0 // n_p, s, 0))
        return pl.BlockSpec((ts, x.shape[1]), lambda s: (s, 0))

    def part(ref, cols, p):
        if len(ref.shape) == 3:
            return ref[p]
        return ref[...] if cols is None else ref[:, _cols(p, cols)]

    def body(a_ref, b_ref, slab_ref, o_ref):
        @pl.when(pl.program_id(0) == 0)
        def _():
            o_ref[...] = jnp.zeros_like(o_ref)

        for p in range(n_p):
            o_ref[p] += _dot_tn(part(a_ref, a_cols, p).astype(BF16), part(b_ref, b_cols, p).astype(BF16))

    g = gslabs[slab]
    out = pl.pallas_call(
        body, name=name, grid=(S // ts,),
        in_specs=[operand(a), operand(b), pl.BlockSpec(memory_space=pl.ANY)],
        out_specs=pl.BlockSpec((n_p, rows, width), lambda s: (p0 // n_p, bidx, 0)),
        out_shape=jax.ShapeDtypeStruct(g.shape, F32), input_output_aliases={2: 0},
        compiler_params=_params(("arbitrary",)),
    )(a, b, g)
    return {**gslabs, slab: out}


def _vec(g):
    return g.reshape(1, -1)


def _norm_mm(x, g, ws, *, split, out_dtype, name, tm=512):
    S, D = x.shape
    k, n = ws[0][1][1], ws[0][1][2]
    N = n if split == "k" else N_CHIPS * n

    def body(r, f, o, acc, s):
        xn = _rms_tile(r[0][...], f[0][...]).astype(BF16)
        o[0][...] = xn
        if split == "k":
            o[1][...] = _sum_k(xn, f[1:], k).astype(out_dtype)
        else:
            for p in range(N_CHIPS):
                o[1][:, _cols(p, n)] = _dot(xn, f[1 + p][...]).astype(out_dtype)

    (xn, y), _ = _rows_call(name, _tile(S, tm), [x], [_vec(g)] + ws, [(D, BF16), (N, out_dtype)], [], body)
    return xn, y


def _mm_k(a, ws, *, add=None, out_dtype=F32, name, tm=512):
    S = a.shape[-2]
    k, n = ws[0][1][1], ws[0][1][2]
    has_add = add is not None

    def body(r, f, o, acc, s):
        if a.ndim == 3:
            y = None
            for p in range(N_CHIPS):
                t = _dot(r[0][p].astype(BF16), f[p][...])
                y = t if y is None else y + t
        else:
            y = _sum_k(r[0][...].astype(BF16), f, k)
        if has_add:
            y = y + r[1][...]
        o[0][...] = y.astype(out_dtype)

    (y,), _ = _rows_call(name, _tile(S, tm), [a] + ([add] if has_add else []), ws, [(n, out_dtype)], [], body)
    return y


def _mm_k_t(terms, *, out_dtype=F32, name, tm=512):
    S = terms[0][0].shape[0]
    k = terms[0][1][0][1][1]

    def body(r, f, o, acc, s):
        y = None
        for t in range(len(terms)):
            yt = _cat_nt(r[t][...].astype(BF16), f[N_CHIPS * t:N_CHIPS * (t + 1)])
            y = yt if y is None else y + yt
        o[0][...] = y.astype(out_dtype)

    (y,), _ = _rows_call(name, _tile(S, tm), [a for a, _ in terms], [w for _, ws in terms for w in ws],
                         [(N_CHIPS * k, out_dtype)], [], body)
    return y


def _rms_fwd(x, g, *, name):
    def body(r, f, o, acc, s):
        o[0][...] = _rms_tile(r[0][...], f[0][...]).astype(BF16)

    (y,), _ = _rows_call(name, _tile(x.shape[0], 256, 8), [x], [_vec(g)], [(x.shape[1], BF16)], [], body)
    return y


def _rms_dg(x, g, dy, *, name):
    def body(r, f, o, acc, s):
        acc[0][...] += _rms_bwd_tile(r[0][...], f[0][...], r[1][...])[1]

    _, (dg,) = _rows_call(name, _tile(x.shape[0], 256, 8), [x, dy], [_vec(g)], [], [((1, x.shape[1]), F32)], body)
    return dg


def _ffn_up(x, g, wg, wu, *, name, tm=256):
    S, D = x.shape
    h = wg[0][1][1]

    def body(r, f, o, acc, s):
        xn = _rms_tile(r[0][...], f[0][...]).astype(BF16)
        o[0][...] = xn
        for p in range(N_CHIPS):
            gate = _dot_nt(xn, f[1 + p][...])
            up = _dot_nt(xn, f[1 + N_CHIPS + p][...])
            o[1][p] = gate.astype(BF16)
            o[2][p] = up.astype(BF16)
            o[3][p] = (gate * _sigmoid(gate) * up).astype(BF16)

    (xn, gate, up, hid), _ = _rows_call(name, _tile(S, tm), [x], [_vec(g)] + wg + wu,
                                        [(D, BF16), (N_CHIPS, h, BF16), (N_CHIPS, h, BF16), (N_CHIPS, h, BF16)], [],
                                        body)
    return xn, gate, up, hid


def _ffn_bwd_hidden(dy, wd, gate, up, token=None, *, name, tm=256):
    S = dy.shape[0]
    h = wd[0][1][1]

    def body(r, f, o, acc, s):
        dyv = r[0][...]
        if token is not None:
            dyv = dyv + jnp.sum(f[N_CHIPS][...])
        dyb = dyv.astype(BF16)
        for p in range(N_CHIPS):
            dh = _dot_nt(dyb, f[p][...])
            gv = r[1][p].astype(F32)
            sg = _sigmoid(gv)
            o[0][p] = (dh * r[2][p].astype(F32) * (sg * (1.0 + gv * (1.0 - sg)))).astype(BF16)
            o[1][p] = (dh * gv * sg).astype(BF16)

    (dg, du), _ = _rows_call(name, _tile(S, tm), [dy, gate, up], wd + ([] if token is None else [token]),
                             [(N_CHIPS, h, BF16), (N_CHIPS, h, BF16)], [], body)
    return dg, du


def _ffn_in_bwd(dg, du, wg, wu, x, g, dres, *, name, tm=256):
    S, D = x.shape

    def body(r, f, o, acc, s):
        tot = None
        for p in range(N_CHIPS):
            y = _dot(r[0][p], f[1 + p][...]) + _dot(r[1][p], f[1 + N_CHIPS + p][...])
            tot = y if tot is None else tot + y
        dx, dgn = _rms_bwd_tile(r[2][...], f[0][...], tot)
        o[0][...] = dx + r[3][...]
        acc[0][...] += dgn

    (dx,), (dgn,) = _rows_call(name, _tile(S, tm), [dg, du, x, dres], [_vec(g)] + wg + wu, [(D, F32)],
                               [((1, D), F32)], body)
    return dx, dgn


def _norm_bwd_k(da, ws, x, g, dres, *, name, tm=512):
    S, D = x.shape

    def body(r, f, o, acc, s):
        dx, dg = _rms_bwd_tile(r[1][...], f[0][...], _cat_nt(r[0][...].astype(BF16), f[1:]))
        o[0][...] = dx + r[2][...]
        acc[0][...] += dg

    (dx,), (dg,) = _rows_call(name, _tile(S, tm), [da, x, dres], [_vec(g)] + ws, [(D, F32)], [((1, D), F32)], body)
    return dx, dg


def _norm_bwd_n(das, ws, x, g, dres, *, name, tm=256):
    S, D = x.shape
    n = ws[0][1][2]

    def body(r, f, o, acc, s):
        tot = None
        for p in range(N_CHIPS):
            y = _dot_nt(r[p // 2][:, _cols(p % 2, n)], f[1 + p][...])
            tot = y if tot is None else tot + y
        dx, dg = _rms_bwd_tile(r[2][...], f[0][...], tot)
        o[0][...] = dx + r[3][...]
        acc[0][...] += dg

    (dx,), (dg,) = _rows_call(name, _tile(S, tm), list(das) + [x, dres], [_vec(g)] + ws, [(D, F32)], [((1, D), F32)],
                              body)
    return dx, dg


def _ln_stats(v):
    mu = _mean(v)
    xc = v - mu
    rstd = lax.rsqrt(_mean(xc * xc) + EPS)
    return xc * rstd, rstd


_SHIFTS = 8
_CONV_ROWS = 64


def _fill_shifts(sh_ref, ext_ref, tm):
    sh_ref[0] = ext_ref[...]
    for s in range(1, _SHIFTS):
        sh_ref[s, 0:tm + CONV_HALO - _SHIFTS, :] = ext_ref[pl.ds(s, tm + CONV_HALO - _SHIFTS), :]


def _window(sh_ref, off, tm):
    return sh_ref[off % _SHIFTS, pl.ds(off - off % _SHIFTS, tm), :]


def _even_fwd(proj, wm, bcol, cw, cb, lg, lb, *, name):
    S = proj.shape[0]
    tm = _tile(S, 256)
    hb = tm // CONV_HALO
    nblk = tm // GMLP_BLOCK

    def body(p_ref, halo_ref, wm_ref, b_ref, cw_ref, cb_ref, lg_ref, lb_ref, mix_ref, hc_ref, hext_ref, hsh_ref):
        i = pl.program_id(0)
        gu, _ = _gelu(p_ref[:, 0:A_WIDTH])
        gv, _ = _gelu(p_ref[:, A_WIDTH:2 * A_WIDTH])
        vn, _ = _ln_stats(gv)
        vnb = vn.astype(BF16)
        for n in range(nblk):
            rows = slice(n * GMLP_BLOCK, (n + 1) * GMLP_BLOCK)
            for g in range(A_GROUPS):
                cols = slice(g * GMLP_BLOCK, (g + 1) * GMLP_BLOCK)
                sg = jnp.dot(wm_ref[g], vnb[rows, cols], preferred_element_type=F32) + b_ref[g]
                mix_ref[rows, cols] = (gu[rows, cols] * sg).astype(BF16)
        h = p_ref[:, 1024:1536] * _sigmoid(p_ref[:, 1536:2048])
        hh = halo_ref[:, 0:B_WIDTH] * _sigmoid(halo_ref[:, B_WIDTH:2 * B_WIDTH])
        hext_ref[0:CONV_HALO, :] = jnp.where(i > 0, hh, 0.0)
        hext_ref[CONV_HALO:CONV_HALO + tm, :] = h
        _fill_shifts(hsh_ref, hext_ref, tm)
        for r0 in range(0, tm, _CONV_ROWS):
            acc = jnp.zeros((_CONV_ROWS, B_WIDTH), F32)
            for k in range(CONV_WIDTH):
                acc = acc + cw_ref[k:k + 1, :] * _window(hsh_ref, r0 + k + CONV_HALO - CONV_WIDTH + 1, _CONV_ROWS)
            hc_ref[r0:r0 + _CONV_ROWS, :] = acc + cb_ref[...]
        hc = hc_ref[...]
        hhat, _ = _ln_stats(hc)
        hl = hhat * lg_ref[...] + lb_ref[...]
        mix_ref[:, A_WIDTH:A_WIDTH + B_WIDTH] = (hl * _sigmoid(hl)).astype(BF16)

    vec = pl.BlockSpec((1, B_WIDTH), lambda i: (0, 0))
    return pl.pallas_call(
        body, name=name, grid=(S // tm,),
        in_specs=[
            pl.BlockSpec((tm, 2048), lambda i: (i, 0)),
            pl.BlockSpec((CONV_HALO, 1024), lambda i: (jnp.maximum(i * hb - 1, 0), 1)),
            pl.BlockSpec((A_GROUPS, GMLP_BLOCK, GMLP_BLOCK), lambda i: (0, 0, 0)),
            pl.BlockSpec((A_GROUPS, GMLP_BLOCK, 1), lambda i: (0, 0, 0)),
            pl.BlockSpec((CONV_HALO, B_WIDTH), lambda i: (0, 0)),
            vec, vec, vec,
        ],
        out_specs=[pl.BlockSpec((tm, 1024), lambda i: (i, 0)), pl.BlockSpec((tm, B_WIDTH), lambda i: (i, 0))],
        out_shape=[jax.ShapeDtypeStruct((S, 1024), BF16), jax.ShapeDtypeStruct((S, B_WIDTH), F32)],
        scratch_shapes=[pltpu.VMEM((tm + CONV_HALO, B_WIDTH), F32),
                        pltpu.VMEM((_SHIFTS, tm + CONV_HALO, B_WIDTH), F32)],
        compiler_params=_params(("parallel",)),
    )(proj, proj, wm, bcol, cw, cb, lg, lb)


def _even_bwd1(proj, dmix, hc, wm, wmt, bcol, lg, lb, *, name):
    S = proj.shape[0]
    tm = _tile(S, 256)
    nblk = tm // GMLP_BLOCK

    def body(p_ref, dm_ref, hc_ref, wm_ref, wmt_ref, b_ref, lg_ref, lb_ref,
             dpa_ref, dhc_ref, dwm_ref, db_ref, dlg_ref, dlb_ref, dcb_ref, dgu_ref, dvn_ref):
        @pl.when(pl.program_id(0) == 0)
        def _():
            dwm_ref[...] = jnp.zeros_like(dwm_ref)
            db_ref[...] = jnp.zeros_like(db_ref)
            dlg_ref[...] = jnp.zeros_like(dlg_ref)
            dlb_ref[...] = jnp.zeros_like(dlb_ref)
            dcb_ref[...] = jnp.zeros_like(dcb_ref)

        au = p_ref[:, 0:A_WIDTH]
        av = p_ref[:, A_WIDTH:2 * A_WIDTH]
        gu, tu = _gelu(au)
        gv, tv = _gelu(av)
        vn, rstd = _ln_stats(gv)
        vnb = vn.astype(BF16)
        for n in range(nblk):
            rows = slice(n * GMLP_BLOCK, (n + 1) * GMLP_BLOCK)
            for g in range(A_GROUPS):
                cols = slice(g * GMLP_BLOCK, (g + 1) * GMLP_BLOCK)
                vb = vnb[rows, cols]
                sg = jnp.dot(wm_ref[g], vb, preferred_element_type=F32) + b_ref[g]
                da = dm_ref[rows, cols]
                dsg = da * gu[rows, cols]
                dgu_ref[rows, cols] = da * sg
                dsgb = dsg.astype(BF16)
                dwm_ref[g] += _dot_nt(dsgb, vb)
                db_ref[g] += jnp.sum(dsg, axis=1, keepdims=True)
                dvn_ref[rows, cols] = jnp.dot(wmt_ref[g], dsgb, preferred_element_type=F32)
        dvn = dvn_ref[...]
        dgv = rstd * (dvn - _mean(dvn) - vn * _mean(dvn * vn))
        dpa_ref[:, 0:A_WIDTH] = (dgu_ref[...] * _gelu_grad(au, tu)).astype(BF16)
        dpa_ref[:, A_WIDTH:2 * A_WIDTH] = (dgv * _gelu_grad(av, tv)).astype(BF16)
        hhat, rstd2 = _ln_stats(hc_ref[...])
        lgv = lg_ref[...]
        hl = hhat * lgv + lb_ref[...]
        s = _sigmoid(hl)
        dhl = dm_ref[:, A_WIDTH:A_WIDTH + B_WIDTH] * (s * (1.0 + hl * (1.0 - s)))
        dlg_ref[...] += jnp.sum(dhl * hhat, axis=0, keepdims=True)
        dlb_ref[...] += jnp.sum(dhl, axis=0, keepdims=True)
        dhh = dhl * lgv
        dhc = rstd2 * (dhh - _mean(dhh) - hhat * _mean(dhh * hhat))
        dcb_ref[...] += jnp.sum(dhc, axis=0, keepdims=True)
        dhc_ref[...] = dhc

    vec = pl.BlockSpec((1, B_WIDTH), lambda i: (0, 0))
    w3 = pl.BlockSpec((A_GROUPS, GMLP_BLOCK, GMLP_BLOCK), lambda i: (0, 0, 0))
    b3 = pl.BlockSpec((A_GROUPS, GMLP_BLOCK, 1), lambda i: (0, 0, 0))
    return pl.pallas_call(
        body, name=name, grid=(S // tm,),
        in_specs=[
            pl.BlockSpec((tm, 1024), lambda i: (i, 0)),
            pl.BlockSpec((tm, 1024), lambda i: (i, 0)),
            pl.BlockSpec((tm, B_WIDTH), lambda i: (i, 0)),
            w3, w3, b3, vec, vec,
        ],
        out_specs=[pl.BlockSpec((tm, 1024), lambda i: (i, 0)), pl.BlockSpec((tm, B_WIDTH), lambda i: (i, 0)),
                   w3, b3, vec, vec, vec],
        out_shape=[
            jax.ShapeDtypeStruct((S, 1024), BF16), jax.ShapeDtypeStruct((S, B_WIDTH), F32),
            jax.ShapeDtypeStruct((A_GROUPS, GMLP_BLOCK, GMLP_BLOCK), F32),
            jax.ShapeDtypeStruct((A_GROUPS, GMLP_BLOCK, 1), F32),
            jax.ShapeDtypeStruct((1, B_WIDTH), F32), jax.ShapeDtypeStruct((1, B_WIDTH), F32),
            jax.ShapeDtypeStruct((1, B_WIDTH), F32),
        ],
        scratch_shapes=[pltpu.VMEM((tm, A_WIDTH), F32), pltpu.VMEM((tm, A_WIDTH), F32)],
        compiler_params=_params(("arbitrary",)),
    )(proj, dmix, hc, wm, wmt, bcol, lg, lb)


def _even_bwd2(proj, dhc, cw, *, name):
    S = proj.shape[0]
    tm = _tile(S, 256)
    hb = tm // CONV_HALO
    nt = S // tm
    last_halo = S // CONV_HALO - 1
    lo = CONV_HALO - CONV_WIDTH + 1

    def body(p_ref, halo_ref, d_ref, dnext_ref, cw_ref, dpb_ref, dcw_ref, hext_ref, dext_ref, hsh_ref, dsh_ref):
        i = pl.program_id(0)

        @pl.when(i == 0)
        def _():
            dcw_ref[...] = jnp.zeros_like(dcw_ref)

        hh = halo_ref[:, 0:B_WIDTH] * _sigmoid(halo_ref[:, B_WIDTH:2 * B_WIDTH])
        hext_ref[0:CONV_HALO, :] = jnp.where(i > 0, hh, 0.0)
        hext_ref[CONV_HALO:CONV_HALO + tm, :] = p_ref[:, 0:B_WIDTH] * _sigmoid(p_ref[:, B_WIDTH:2 * B_WIDTH])
        dext_ref[0:tm, :] = d_ref[...]
        dext_ref[tm:tm + CONV_HALO, :] = jnp.where(i < nt - 1, dnext_ref[...], 0.0)
        _fill_shifts(hsh_ref, hext_ref, tm)
        _fill_shifts(dsh_ref, dext_ref, tm)
        for r0 in range(0, tm, _CONV_ROWS):
            rows = slice(r0, r0 + _CONV_ROWS)
            dhc_b = d_ref[rows, :]
            dh = jnp.zeros((_CONV_ROWS, B_WIDTH), F32)
            for k in range(CONV_WIDTH):
                dh = dh + cw_ref[k:k + 1, :] * _window(dsh_ref, r0 + CONV_WIDTH - 1 - k, _CONV_ROWS)
                dcw_ref[k:k + 1, :] += jnp.sum(dhc_b * _window(hsh_ref, r0 + k + lo, _CONV_ROWS), axis=0,
                                               keepdims=True)
            ba_b = p_ref[rows, 0:B_WIDTH]
            sg_b = _sigmoid(p_ref[rows, B_WIDTH:2 * B_WIDTH])
            dpb_ref[rows, 0:B_WIDTH] = (dh * sg_b).astype(BF16)
            dpb_ref[rows, B_WIDTH:2 * B_WIDTH] = (dh * ba_b * sg_b * (1.0 - sg_b)).astype(BF16)

    return pl.pallas_call(
        body, name=name, grid=(nt,),
        in_specs=[
            pl.BlockSpec((tm, 1024), lambda i: (i, 1)),
            pl.BlockSpec((CONV_HALO, 1024), lambda i: (jnp.maximum(i * hb - 1, 0), 1)),
            pl.BlockSpec((tm, B_WIDTH), lambda i: (i, 0)),
            pl.BlockSpec((CONV_HALO, B_WIDTH), lambda i: (jnp.minimum((i + 1) * hb, last_halo), 0)),
            pl.BlockSpec((CONV_HALO, B_WIDTH), lambda i: (0, 0)),
        ],
        out_specs=[pl.BlockSpec((tm, 1024), lambda i: (i, 0)), pl.BlockSpec((CONV_HALO, B_WIDTH), lambda i: (0, 0))],
        out_shape=[jax.ShapeDtypeStruct((S, 1024), BF16), jax.ShapeDtypeStruct((CONV_HALO, B_WIDTH), F32)],
        scratch_shapes=[pltpu.VMEM((tm + CONV_HALO, B_WIDTH), F32), pltpu.VMEM((tm + CONV_HALO, B_WIDTH), F32),
                        pltpu.VMEM((_SHIFTS, tm + CONV_HALO, B_WIDTH), F32),
                        pltpu.VMEM((_SHIFTS, tm + CONV_HALO, B_WIDTH), F32)],
        compiler_params=_params(("arbitrary",)),
    )(proj, proj, dhc, dhc, cw)


_CA_SCALE = CA_HEAD_DIM ** -0.5


def _softmax_rows(s):
    e = jnp.exp(s - jnp.max(s, axis=-1, keepdims=True))
    return e / jnp.sum(e, axis=-1, keepdims=True)


def _attn_fwd(q, k, v, *, name):
    S = q.shape[0]

    def body(r, f, o, acc, s):
        for h in range(CA_HEADS):
            cols = _cols(h, CA_HEAD_DIM)
            p = _softmax_rows(_dot_nt(r[0][:, cols], f[0][:, cols]) * _CA_SCALE)
            o[0][:, cols] = _dot(p.astype(BF16), f[1][:, cols]).astype(BF16)

    (o_,), _ = _rows_call(name, _tile(S, 512), [q], [k, v], [(D_MODEL, BF16)], [], body)
    return o_


def _attn_bwd(dy, wo, q, k, v, *, name):
    S = q.shape[0]
    M = k.shape[0]

    def body(r, f, o, acc, s):
        dyb = r[0][...].astype(BF16)
        for h in range(CA_HEADS):
            cols = _cols(h, CA_HEAD_DIM)
            qh = r[1][:, cols]
            kh = f[0][:, cols]
            vh = f[1][:, cols]
            doh = _dot_nt(dyb, f[2 + h][...]).astype(BF16)
            p = _softmax_rows(_dot_nt(qh, kh) * _CA_SCALE)
            acc[1][:, cols] += _dot_tn(p.astype(BF16), doh)
            dp = _dot_nt(doh, vh)
            ds = (p * (dp - jnp.sum(dp * p, axis=-1, keepdims=True)) * _CA_SCALE).astype(BF16)
            o[0][:, cols] = _dot(ds, kh).astype(BF16)
            acc[0][:, cols] += _dot_tn(ds, qh)

    (dq,), (dk, dv) = _rows_call(name, _tile(S, 512), [dy, q], [k, v] + wo, [(D_MODEL, BF16)],
                                 [((M, D_MODEL), F32), ((M, D_MODEL), F32)], body)
    return dq, dk, dv


_STATE_TILE = 2 * STATE_ROWS


def _state_cols(ref, tm):
    return jnp.concatenate([ref[:, c, :, :].reshape(tm, STATE_LANES).astype(BF16) for c in range(_STATE_TILE)],
                           axis=1)


def _put_state_cols(ref, y, tm):
    for c in range(_STATE_TILE):
        ref[:, c, :, :] = y[:, _cols(c, STATE_LANES)].reshape(tm // 8, 8, STATE_LANES)


def _mm_to_state(a, w, *, nt=False, name, tm=256):
    S = a.shape[0]
    tm = _tile(S, tm)

    def body(r, f, o, acc, s):
        av = r[0][...].astype(BF16)
        _put_state_cols(o[0], _dot_nt(av, f[0][...]) if nt else _dot(av, f[0][...]), tm)

    (y,), _ = _rows_call(name, tm, [a], [w], [("state", F32)], [], body)
    return y


def _s5_readout(xs, cd, u, d, *, name, tm=256):
    tm = _tile(u.shape[0], tm)

    def body(r, f, o, acc, s):
        y = _dot(_state_cols(r[0], tm), f[0][...]) + f[1][...] * r[1][...]
        o[0][...] = y
        o[1][...] = _gelu(y)[0].astype(BF16)

    (y, yg), _ = _rows_call(name, tm, [xs, u], [cd, d], [(C_WIDTH, F32), (C_WIDTH, BF16)], [], body)
    return y, yg


def _state_grad_tn(a, b, *, name, ts=256):
    a_state, b_state = a.ndim == 4, b.ndim == 4
    S = b.shape[0] if a_state else a.shape[0]
    ts = _tile(S, ts)
    K1 = 2 * N_STATE if a_state else a.shape[1]
    N = 2 * N_STATE if b_state else b.shape[1]

    def body(r, f, o, acc, s):
        av = _state_cols(r[0], ts) if a_state else r[0][...].astype(BF16)
        bv = _state_cols(r[1], ts) if b_state else r[1][...].astype(BF16)
        acc[0][...] += _dot_tn(av, bv)

    _, (out,) = _rows_call(name, ts, [a, b], [], [], [((K1, N), F32)], body)
    return out


def _glu_out(yg, ws, x, *, name, tm=512):
    n = ws[0][1][2]

    def body(r, f, o, acc, s):
        ygv = r[0][...]
        ov = [_dot(ygv, f[p][...]) for p in range(N_CHIPS)]
        for p in range(N_CHIPS):
            o[0][:, _cols(p, n)] = ov[p].astype(BF16)
        for p in range(2):
            o[1][:, _cols(p, n)] = r[1][:, _cols(p, n)] + ov[p] * _sigmoid(ov[2 + p])

    (o_, y), _ = _rows_call(name, _tile(x.shape[0], tm), [yg, x], ws, [(2 * D_MODEL, BF16), (D_MODEL, F32)], [], body)
    return o_, y


def _glu_out_bwd(o_, dy, ws, y, u, d, *, name, tm=256):
    n = ws[0][1][2]

    def body(r, f, o, acc, s):
        o1 = r[0][:, 0:D_MODEL].astype(F32)
        sg = _sigmoid(r[0][:, D_MODEL:2 * D_MODEL].astype(F32))
        dyv = r[1][...]
        do1 = (dyv * sg).astype(BF16)
        do2 = (dyv * o1 * sg * (1.0 - sg)).astype(BF16)
        o[0][:, 0:D_MODEL] = do1
        o[0][:, D_MODEL:2 * D_MODEL] = do2
        dyg = None
        for p in range(N_CHIPS):
            t = _dot_nt((do1 if p < 2 else do2)[:, _cols(p % 2, n)], f[1 + p][...])
            dyg = t if dyg is None else dyg + t
        yv = r[2][...]
        dys = dyg * _gelu_grad(yv, _gelu(yv)[1])
        o[1][...] = dys.astype(BF16)
        o[2][...] = f[0][...] * dys
        acc[0][...] += jnp.sum(dys * r[3][...], axis=0, keepdims=True)

    (do, dys, dus), (dd,) = _rows_call(name, _tile(dy.shape[0], tm), [o_, dy, y, u], [d] + ws,
                                       [(2 * D_MODEL, BF16), (C_WIDTH, BF16), (C_WIDTH, F32)], [((1, C_WIDTH), F32)],
                                       body)
    return do, dys, dus, dd


def _s5_in_bwd(gs, bd, dus, ws, x, g, dres, *, name, tm=256):
    D = x.shape[1]
    tm = _tile(x.shape[0], tm)

    def body(r, f, o, acc, s):
        du = (_dot_nt(_state_cols(r[0], tm), f[1][...]) + r[1][...]).astype(BF16)
        o[0][...] = du
        dx, dg = _rms_bwd_tile(r[2][...], f[0][...], _cat_nt(du, f[2:]))
        o[1][...] = dx + r[3][...]
        acc[0][...] += dg

    (du, dx), (dg,) = _rows_call(name, tm, [gs, dus, x, dres], [_vec(g), bd] + ws,
                                 [(C_WIDTH, BF16), (D, F32)], [((1, D), F32)], body)
    return du, dx, dg


_SCAN_CHUNK = 128
_RE = slice(0, STATE_ROWS)
_IM = slice(STATE_ROWS, 2 * STATE_ROWS)
assert SCAN_BLOCK == 8


def _token(g, i):
    return pl.ds(pl.multiple_of(g * (_STATE_TILE * SCAN_BLOCK), _STATE_TILE * SCAN_BLOCK) + i, _STATE_TILE,
                 stride=SCAN_BLOCK)


def _scan_fwd(bu, pw, *, name):
    S = bu.shape[0] // _STATE_TILE
    tc = _tile(S, _SCAN_CHUNK, 8)

    def body(bu_ref, pw_ref, xs_ref, st_ref):
        @pl.when(pl.program_id(0) == 0)
        def _():
            st_ref[...] = jnp.zeros_like(st_ref)

        ar = pw_ref[0, _RE, :]
        ai = pw_ref[0, _IM, :]

        def block(g, carry):
            xr, xi = carry
            cr = ci = nr = ni = None
            for j in range(SCAN_BLOCK):
                b = bu_ref[_token(g, j), :]
                br, bi = b[_RE], b[_IM]
                cr, ci = (br, bi) if j == 0 else (ar * cr - ai * ci + br, ar * ci + ai * cr + bi)
                pr, pi = pw_ref[j, _RE, :], pw_ref[j, _IM, :]
                nr = pr * xr - pi * xi + cr
                ni = pr * xi + pi * xr + ci
                xs_ref[_token(g, j), :] = jnp.concatenate([nr, ni], axis=0)
            return nr, ni

        xr, xi = lax.fori_loop(0, tc // SCAN_BLOCK, block, (st_ref[_RE, :], st_ref[_IM, :]), unroll=2)
        st_ref[_RE, :] = xr
        st_ref[_IM, :] = xi

    blk = pl.BlockSpec((tc * _STATE_TILE, STATE_LANES), lambda i: (i, 0))
    return pl.pallas_call(
        body, name=name, grid=(S // tc,),
        in_specs=[blk, pl.BlockSpec(pw.shape, lambda i: (0, 0, 0))], out_specs=blk,
        out_shape=jax.ShapeDtypeStruct(bu.shape, F32),
        scratch_shapes=[pltpu.VMEM((2 * STATE_ROWS, STATE_LANES), F32)],
        compiler_params=_params(("arbitrary",)),
    )(bu, pw)


def _scan_bwd(dxs, xs, pw, *, name):
    S = dxs.shape[0] // _STATE_TILE
    tc = _tile(S, _SCAN_CHUNK, 8)
    nc = S // tc

    def body(dx_ref, xs_ref, pw_ref, g_ref, da_ref, st_ref):
        @pl.when(pl.program_id(0) == 0)
        def _():
            st_ref[...] = jnp.zeros_like(st_ref)
            da_ref[...] = jnp.zeros_like(da_ref)

        ar = pw_ref[0, _RE, :]
        ai = pw_ref[0, _IM, :]

        def block(i, carry):
            gr, gi, dar, dai = carry
            g = tc // SCAN_BLOCK - 1 - i
            cr = ci = None
            pgr, pgi = gr, gi
            for j in range(SCAN_BLOCK):
                tok = _token(g, SCAN_BLOCK - 1 - j)
                x = xs_ref[tok, :]
                xr, xi = x[_RE], x[_IM]
                dar = dar + pgr * xr + pgi * xi
                dai = dai + pgi * xr - pgr * xi
                d = dx_ref[tok, :]
                dr, di = d[_RE], d[_IM]
                cr, ci = (dr, di) if j == 0 else (ar * cr + ai * ci + dr, ar * ci - ai * cr + di)
                pr, pi = pw_ref[j, _RE, :], pw_ref[j, _IM, :]
                pgr = pr * gr + pi * gi + cr
                pgi = pr * gi - pi * gr + ci
                g_ref[tok, :] = jnp.concatenate([pgr, pgi], axis=0)
            return pgr, pgi, dar, dai

        init = (st_ref[_RE, :], st_ref[_IM, :], da_ref[_RE, :], da_ref[_IM, :])
        gr, gi, dar, dai = lax.fori_loop(0, tc // SCAN_BLOCK, block, init, unroll=2)
        st_ref[_RE, :] = gr
        st_ref[_IM, :] = gi
        da_ref[_RE, :] = dar
        da_ref[_IM, :] = dai

    blk = pl.BlockSpec((tc * _STATE_TILE, STATE_LANES), lambda i: (nc - 1 - i, 0))
    vec = pl.BlockSpec((2 * STATE_ROWS, STATE_LANES), lambda i: (0, 0))
    return pl.pallas_call(
        body, name=name, grid=(nc,), in_specs=[blk, blk, pl.BlockSpec(pw.shape, lambda i: (0, 0, 0))],
        out_specs=[blk, vec],
        out_shape=[jax.ShapeDtypeStruct(dxs.shape, F32), jax.ShapeDtypeStruct((2 * STATE_ROWS, STATE_LANES), F32)],
        scratch_shapes=[pltpu.VMEM((2 * STATE_ROWS, STATE_LANES), F32)],
        compiler_params=_params(("arbitrary",)),
    )(dxs, xs, pw)


def _loss_head(x, g, target, *, name):
    S, D = x.shape

    def body(r, f, o, acc, s):
        xv = r[0][...]
        gv = f[0][...]
        rs = lax.rsqrt(_mean(xv * xv) + EPS)
        xh = xv * rs
        err = xh * gv - r[1][...]
        acc[1][...] += 0.5 * jnp.sum(_mean(err * err), axis=0, keepdims=True)
        dy = err * (1.0 / D)
        dyg = dy * gv
        o[0][...] = rs * (dyg - xh * _mean(dyg * xh))
        acc[0][...] += jnp.sum(dy * xh, axis=0, keepdims=True)

    (dx,), (dg, loss) = _rows_call(name, _tile(S, 256, 8), [x, target], [_vec(g)], [(D, F32)],
                                   [((1, D), F32), ((1, 128), F32)], body)
    return dx, dg, loss


_ADAM_C1 = 1.0 - ADAM_B1 ** ADAM_STEP
_ADAM_C2 = 1.0 - ADAM_B2 ** ADAM_STEP
_ONE_BLOCK_BYTES = 8 * 1024 * 1024


def _adamw_math(w, g, m, v):
    nm = ADAM_B1 * m + (1.0 - ADAM_B1) * g
    nv = ADAM_B2 * v + (1.0 - ADAM_B2) * (g * g)
    m_hat = nm / _ADAM_C1
    v_hat = nv / _ADAM_C2
    return -ADAM_LR * (m_hat / (jnp.sqrt(v_hat) + ADAM_EPS) + ADAM_WD * w), nm, nv


def _adamw_shard(w, gsrc, m, v, *, name):
    R, C = w.shape
    n_l = len(gsrc)
    rows = R // n_l
    tr = rows
    for _, r0 in gsrc:
        tr = math.gcd(tr, r0) if r0 else tr
    tr = _tile(tr, 256, 8) if tr > 256 else tr
    nb = rows // tr
    assert rows % tr == 0 and all(r0 % tr == 0 for _, r0 in gsrc)

    def body(*refs):
        w_ref, g_refs, (m_ref, v_ref, go_ref, d_ref, nm_ref, nv_ref) = refs[0], refs[1:1 + n_l], refs[1 + n_l:]
        layer = pl.program_id(0) // nb
        gv = g_refs[0][...]
        for l in range(1, n_l):
            gv = jnp.where(layer == l, g_refs[l][...], gv)
        go_ref[...] = gv
        d_ref[...], nm_ref[...], nv_ref[...] = _adamw_math(w_ref[...], gv, m_ref[...], v_ref[...])

    def g_spec(l, r0):
        return pl.BlockSpec((tr, C), lambda i: (r0 // tr + jnp.clip(i - l * nb, 0, nb - 1), 0))

    blk = pl.BlockSpec((tr, C), lambda i: (i, 0))
    out = jax.ShapeDtypeStruct((R, C), F32)
    return pl.pallas_call(
        body, name=name, grid=(R // tr,),
        in_specs=[blk] + [g_spec(l, r0) for l, (_, r0) in enumerate(gsrc)] + [blk, blk], out_specs=[blk] * 4,
        out_shape=[out] * 4, compiler_params=_params(("parallel",)),
    )(w, *[g for g, _ in gsrc], m, v)


def _adamw_small(ws, gs, ms, vs, *, name):
    n = len(ws)

    def body(*refs):
        w_r, g_r, m_r, v_r = refs[:n], refs[n:2 * n], refs[2 * n:3 * n], refs[3 * n:4 * n]
        d_r, nm_r, nv_r = refs[4 * n:5 * n], refs[5 * n:6 * n], refs[6 * n:7 * n]
        for k in range(n):
            d_r[k][...], nm_r[k][...], nv_r[k][...] = _adamw_math(w_r[k][...], g_r[k][...], m_r[k][...], v_r[k][...])

    vm = pl.BlockSpec(memory_space=pltpu.VMEM)
    out = [jax.ShapeDtypeStruct(w.shape, F32) for w in ws]
    res = pl.pallas_call(body, name=name, in_specs=[vm] * (4 * n), out_specs=[vm] * (3 * n), out_shape=out * 3,
                         compiler_params=pltpu.CompilerParams(vmem_limit_bytes=VMEM_LIMIT))(*ws, *gs, *ms, *vs)
    return res[:n], res[n:2 * n], res[2 * n:]


def _sum_slots(x, *, name):
    n, R, C = x.shape
    tr = R if (n + 1) * R * C * 4 <= _ONE_BLOCK_BYTES else _tile(R, 256, 8)

    def body(x_ref, o_ref):
        acc = x_ref[0]
        for k in range(1, n):
            acc = acc + x_ref[k]
        o_ref[...] = acc

    return pl.pallas_call(
        body, name=name, grid=(R // tr,),
        in_specs=[pl.BlockSpec((n, tr, C), lambda i: (0, i, 0))], out_specs=pl.BlockSpec((tr, C), lambda i: (i, 0)),
        out_shape=jax.ShapeDtypeStruct((R, C), F32), compiler_params=_params(("parallel",)),
    )(x)


def _pair_sum(g, r, half, *, name):
    n, R, C = g.shape
    Rh = R // 2
    tr = _tile(Rh, 256, 8)
    nb = Rh // tr

    def body(half_ref, g_ref, r_ref, o_ref):
        o_ref[...] = (g_ref[...] + r_ref[...]).astype(BF16)

    return pl.pallas_call(
        body, name=name,
        grid_spec=pltpu.PrefetchScalarGridSpec(
            num_scalar_prefetch=1, grid=(n, nb),
            in_specs=[pl.BlockSpec((1, tr, C), lambda p, i, h: (p, h[0] * nb + i, 0)),
                      pl.BlockSpec((1, tr, C), lambda p, i, h: (p, i, 0))],
            out_specs=pl.BlockSpec((1, tr, C), lambda p, i, h: (p, i, 0)),
        ),
        out_shape=jax.ShapeDtypeStruct((n, Rh, C), BF16), compiler_params=_params(("parallel", "parallel")),
    )(half, g, r)


def _chip_sum(g, r, slots, where, *, name):
    n, R, C = g.shape
    Rh = R // 2
    tr = _tile(Rh, 256, 8)
    nb = Rh // tr

    def body(w_ref, g_ref, r_ref, s_ref, o_ref):
        acc = g_ref[0] + r_ref[0]
        for k in range(slots.shape[0]):
            acc = acc + s_ref[k].astype(F32)
        o_ref[...] = acc

    return pl.pallas_call(
        body, name=name,
        grid_spec=pltpu.PrefetchScalarGridSpec(
            num_scalar_prefetch=1, grid=(nb,),
            in_specs=[pl.BlockSpec((1, tr, C), lambda i, w: (w[0], w[1] * nb + i, 0)),
                      pl.BlockSpec((1, tr, C), lambda i, w: (w[0], i, 0)),
                      pl.BlockSpec((slots.shape[0], tr, C), lambda i, w: (0, i, 0))],
            out_specs=pl.BlockSpec((tr, C), lambda i, w: (w[1] * nb + i, 0)),
        ),
        out_shape=jax.ShapeDtypeStruct((R, C), F32), compiler_params=_params(("parallel",)),
    )(where, g, r, slots)


ANY = pl.BlockSpec(memory_space=pl.ANY)


def _place():
    return lax.axis_index("x"), lax.axis_index("y"), lax.axis_index("c")


def _other_chips(x, y):
    return [(1 - x, y), (x, 1 - y), (1 - x, 1 - y)]


def _allgather_small(v, *, name):
    R, C = v.shape

    def body(x_ref, out_ref, send_sems, recv_sems, local_sem):
        x, y, c = _place()
        me, sibling = (x, y, c), (x, y, 1 - c)
        chips = _other_chips(x, y)

        def rows(px, py, pc):
            return out_ref.at[pl.ds((4 * px + 2 * py + pc) * R, R), :]

        def copy(k, block, to, src=None):
            return pltpu.make_async_remote_copy(
                src_ref=rows(*block) if src is None else src, dst_ref=rows(*block),
                send_sem=send_sems.at[k], recv_sem=recv_sems.at[k], device_id=to, device_id_type=MESH)

        mine = pltpu.make_async_copy(x_ref, rows(*me), local_sem)
        mine.start()
        first = [copy(0, me, sibling, src=x_ref)]
        first += [copy(1 + j, me, (*chip, c), src=x_ref) for j, chip in enumerate(chips)]
        for cp in first:
            cp.start()
        passed = [copy(4 + j, (*chip, c), sibling) for j, chip in enumerate(chips)]
        for j, chip in enumerate(chips):
            copy(1 + j, (*chip, c), me).wait_recv()
            passed[j].start()
        copy(0, sibling, me).wait_recv()
        for j, chip in enumerate(chips):
            copy(4 + j, (*chip, 1 - c), me).wait_recv()
        for cp in first + passed:
            cp.wait_send()
        mine.wait()

    return pl.pallas_call(
        body, name=name, out_shape=jax.ShapeDtypeStruct((N_DEV * R, C), v.dtype),
        in_specs=[pl.BlockSpec(memory_space=pltpu.VMEM)], out_specs=pl.BlockSpec(memory_space=pltpu.VMEM),
        scratch_shapes=[pltpu.SemaphoreType.DMA((7,)), pltpu.SemaphoreType.DMA((7,)), pltpu.SemaphoreType.DMA],
        compiler_params=pltpu.CompilerParams(vmem_limit_bytes=VMEM_LIMIT),
    )(v)


def _aliased_comm_call(body, bufs, n_sems, *, name):
    n = len(bufs)
    return pl.pallas_call(
        body, name=name, out_shape=[jax.ShapeDtypeStruct(b.shape, b.dtype) for b in bufs],
        in_specs=[ANY] * n, out_specs=[ANY] * n, input_output_aliases={k: k for k in range(n)},
        scratch_shapes=[pltpu.SemaphoreType.DMA((n_sems,)), pltpu.SemaphoreType.DMA((n_sems,))],
    )(*bufs)


HBM = pl.BlockSpec(memory_space=pltpu.HBM)
SEM = pl.BlockSpec(memory_space=pltpu.SEMAPHORE)
_SPLIT = pltpu.CompilerParams(has_side_effects=pltpu.SideEffectType.DATAFLOW_SIDE_EFFECTING)


def _in_hbm(arrs):
    return [pltpu.with_memory_space_constraint(a, pltpu.HBM) for a in arrs]


def _gather_ici_start(bufs, after, *, name):
    n = len(bufs)

    def body(*refs):
        send_sems, recv_sems, outs, token = refs[n + 1], refs[n + 2], refs[n + 3:2 * n + 3], refs[2 * n + 3]
        x, y, c = _place()
        for b in range(n):
            rh = bufs[b].shape[1] // 2
            part = outs[b].at[2 * x + y, pl.ds(c * rh, rh), :]
            for j, chip in enumerate(_other_chips(x, y)):
                pltpu.make_async_remote_copy(src_ref=part, dst_ref=part, send_sem=send_sems.at[3 * b + j],
                                             recv_sem=recv_sems.at[3 * b + j], device_id=(*chip, c),
                                             device_id_type=MESH).start()
        token[...] = jnp.zeros_like(token)

    res = pl.pallas_call(
        body, name=name,
        out_shape=(pltpu.SemaphoreType.DMA((3 * n,)), pltpu.SemaphoreType.DMA((3 * n,)),
                   *[pltpu.HBM(b.shape, b.dtype) for b in bufs], jax.ShapeDtypeStruct((8, 128), F32)),
        in_specs=[HBM] * n + [ANY], out_specs=(SEM, SEM, *[HBM] * n, pl.BlockSpec(memory_space=pltpu.VMEM)),
        input_output_aliases={k: k + 2 for k in range(n)}, compiler_params=_SPLIT,
    )(*_in_hbm(bufs), after)
    return res[0], res[1], list(res[2:2 + n]), res[2 + n]


def _gather_ici_wait(send_sems, recv_sems, bufs, after, *, name):
    n = len(bufs)

    def body(*refs):
        ins, ss, rs = refs[:n], refs[n], refs[n + 1]
        x, y, c = _place()
        for b in range(n):
            rh = bufs[b].shape[1] // 2
            mine = ins[b].at[2 * x + y, pl.ds(c * rh, rh), :]
            for j, (cx, cy) in enumerate(_other_chips(x, y)):
                theirs = ins[b].at[2 * cx + cy, pl.ds(c * rh, rh), :]
                cp = pltpu.make_async_remote_copy(src_ref=mine, dst_ref=theirs, send_sem=ss.at[3 * b + j],
                                                  recv_sem=rs.at[3 * b + j], device_id=(cx, cy, c),
                                                  device_id_type=MESH)
                cp.wait_send()
                cp.wait_recv()

    return list(pl.pallas_call(
        body, name=name, out_shape=[pltpu.HBM(b.shape, b.dtype) for b in bufs],
        in_specs=[HBM] * n + [SEM, SEM, ANY], out_specs=[HBM] * n,
        input_output_aliases={k: k for k in range(n)}, compiler_params=_SPLIT,
    )(*bufs, send_sems, recv_sems, after))


def _gather_forward(bufs, *, name):
    n = len(bufs)

    def body(*refs):
        outs, send_sems, recv_sems = refs[n:2 * n], refs[2 * n], refs[2 * n + 1]
        x, y, c = _place()

        def copy(b, j, chip, hc):
            rh = bufs[b].shape[1] // 2
            part = outs[b].at[2 * chip[0] + chip[1], pl.ds(hc * rh, rh), :]
            return pltpu.make_async_remote_copy(src_ref=part, dst_ref=part, send_sem=send_sems.at[3 * b + j],
                                                recv_sem=recv_sems.at[3 * b + j], device_id=(x, y, 1 - c),
                                                device_id_type=MESH)

        sends = [copy(b, j, chip, c) for b in range(n) for j, chip in enumerate(_other_chips(x, y))]
        for cp in sends:
            cp.start()
        for b in range(n):
            for j, chip in enumerate(_other_chips(x, y)):
                copy(b, j, chip, 1 - c).wait_recv()
        for cp in sends:
            cp.wait_send()

    return _aliased_comm_call(body, bufs, 3 * n, name=name)


def _chip_exchange_start(hs, *, name):
    n = len(hs)
    lands = [lax.empty((3,) + h.shape[1:], h.dtype) for h in hs]

    def body(*refs):
        send_sems, recv_sems = refs[2 * n], refs[2 * n + 1]
        h_out, l_out, token = refs[2 * n + 2:3 * n + 2], refs[3 * n + 2:4 * n + 2], refs[4 * n + 2]
        x, y, c = _place()
        for b in range(n):
            for j, (cx, cy) in enumerate(_other_chips(x, y)):
                pltpu.make_async_remote_copy(src_ref=h_out[b].at[2 * cx + cy], dst_ref=l_out[b].at[j],
                                             send_sem=send_sems.at[3 * b + j], recv_sem=recv_sems.at[3 * b + j],
                                             device_id=(cx, cy, c), device_id_type=MESH).start()
        token[...] = jnp.zeros_like(token)

    res = pl.pallas_call(
        body, name=name,
        out_shape=(pltpu.SemaphoreType.DMA((3 * n,)), pltpu.SemaphoreType.DMA((3 * n,)),
                   *[pltpu.HBM(a.shape, a.dtype) for a in hs + lands], jax.ShapeDtypeStruct((8, 128), F32)),
        in_specs=[HBM] * (2 * n), out_specs=(SEM, SEM, *[HBM] * (2 * n), pl.BlockSpec(memory_space=pltpu.VMEM)),
        input_output_aliases={k: k + 2 for k in range(2 * n)}, compiler_params=_SPLIT,
    )(*_in_hbm(hs + lands))
    return res[0], res[1], list(res[2:2 + n]), list(res[2 + n:2 + 2 * n]), res[2 + 2 * n]


def _chip_exchange_wait(send_sems, recv_sems, hs, lands, after, *, name):
    n = len(hs)

    def body(*refs):
        h_in, l_in, ss, rs = refs[:n], refs[n:2 * n], refs[2 * n], refs[2 * n + 1]
        x, y, c = _place()
        for b in range(n):
            for j, (cx, cy) in enumerate(_other_chips(x, y)):
                cp = pltpu.make_async_remote_copy(src_ref=h_in[b].at[2 * cx + cy], dst_ref=l_in[b].at[j],
                                                  send_sem=ss.at[3 * b + j], recv_sem=rs.at[3 * b + j],
                                                  device_id=(cx, cy, c), device_id_type=MESH)
                cp.wait_send()
                cp.wait_recv()

    res = pl.pallas_call(
        body, name=name, out_shape=[pltpu.HBM(a.shape, a.dtype) for a in hs + lands],
        in_specs=[HBM] * (2 * n) + [SEM, SEM, ANY], out_specs=[HBM] * (2 * n),
        input_output_aliases={k: k for k in range(2 * n)}, compiler_params=_SPLIT,
    )(*hs, *lands, send_sems, recv_sems, after)
    return list(res[n:])


def _peers(x, y, c):
    return [((1 - x) if fx else x, (1 - y) if fy else y, (1 - c) if fc else c)
            for fx in (0, 1) for fy in (0, 1) for fc in (0, 1) if fx or fy or fc]


def _all_to_all_start(slab, after, *, name):
    land = lax.empty((N_DEV,) + slab.shape, slab.dtype)

    def body(slab_in, land_in, after_ref, send_sems, recv_sems, slab_out, land_out, token):
        x, y, c = _place()
        for k, peer in enumerate(_peers(x, y, c)):
            pltpu.make_async_remote_copy(src_ref=slab_out, dst_ref=land_out.at[4 * x + 2 * y + c],
                                         send_sem=send_sems.at[k], recv_sem=recv_sems.at[k], device_id=peer,
                                         device_id_type=MESH).start()
        token[...] = jnp.zeros_like(token)

    return pl.pallas_call(
        body, name=name,
        out_shape=(pltpu.SemaphoreType.DMA((N_DEV - 1,)), pltpu.SemaphoreType.DMA((N_DEV - 1,)),
                   pltpu.HBM(slab.shape, slab.dtype), pltpu.HBM(land.shape, land.dtype),
                   jax.ShapeDtypeStruct((8, 128), F32)),
        in_specs=[HBM, HBM, ANY], out_specs=(SEM, SEM, HBM, HBM, pl.BlockSpec(memory_space=pltpu.VMEM)),
        input_output_aliases={0: 2, 1: 3}, compiler_params=_SPLIT,
    )(*_in_hbm([slab, land]), after)


def _all_to_all_wait(send_sems, recv_sems, slab, land, after, *, name):
    def body(slab_in, land_in, ss, rs, after_ref, slab_out, land_out):
        x, y, c = _place()
        for k, (px, py, pc) in enumerate(_peers(x, y, c)):
            cp = pltpu.make_async_remote_copy(src_ref=slab_in, dst_ref=land_in.at[4 * px + 2 * py + pc],
                                              send_sem=ss.at[k], recv_sem=rs.at[k], device_id=(px, py, pc),
                                              device_id_type=MESH)
            cp.wait_send()
            cp.wait_recv()

    return pl.pallas_call(
        body, name=name, out_shape=[pltpu.HBM(slab.shape, slab.dtype), pltpu.HBM(land.shape, land.dtype)],
        in_specs=[HBM, HBM, SEM, SEM, ANY], out_specs=[HBM, HBM], input_output_aliases={0: 0, 1: 1},
        compiler_params=_SPLIT,
    )(slab, land, send_sems, recv_sems, after)


def _pair_exchange_start(gs, *, name):
    n = len(gs)
    lands = [lax.empty((g.shape[0], g.shape[1] // 2, g.shape[2]), g.dtype) for g in gs]

    def body(*refs):
        send_sems, recv_sems = refs[2 * n], refs[2 * n + 1]
        g_out, l_out, token = refs[2 * n + 2:3 * n + 2], refs[3 * n + 2:4 * n + 2], refs[4 * n + 2]
        x, y, c = _place()
        for b in range(n):
            rh = gs[b].shape[1] // 2
            pltpu.make_async_remote_copy(src_ref=g_out[b].at[:, pl.ds((1 - c) * rh, rh), :], dst_ref=l_out[b],
                                         send_sem=send_sems.at[b], recv_sem=recv_sems.at[b],
                                         device_id=(x, y, 1 - c), device_id_type=MESH).start()
        token[...] = jnp.zeros_like(token)

    res = pl.pallas_call(
        body, name=name,
        out_shape=(pltpu.SemaphoreType.DMA((n,)), pltpu.SemaphoreType.DMA((n,)),
                   *[pltpu.HBM(a.shape, a.dtype) for a in gs + lands], jax.ShapeDtypeStruct((8, 128), F32)),
        in_specs=[HBM] * (2 * n), out_specs=(SEM, SEM, *[HBM] * (2 * n), pl.BlockSpec(memory_space=pltpu.VMEM)),
        input_output_aliases={k: k + 2 for k in range(2 * n)}, compiler_params=_SPLIT,
    )(*_in_hbm(gs + lands))
    return res[0], res[1], list(res[2:2 + n]), list(res[2 + n:2 + 2 * n]), res[2 + 2 * n]


def _pair_exchange_wait(send_sems, recv_sems, gs, lands, after, *, name):
    n = len(gs)

    def body(*refs):
        g_in, l_in, ss, rs = refs[:n], refs[n:2 * n], refs[2 * n], refs[2 * n + 1]
        x, y, c = _place()
        for b in range(n):
            rh = gs[b].shape[1] // 2
            cp = pltpu.make_async_remote_copy(src_ref=g_in[b].at[:, pl.ds((1 - c) * rh, rh), :], dst_ref=l_in[b],
                                              send_sem=ss.at[b], recv_sem=rs.at[b], device_id=(x, y, 1 - c),
                                              device_id_type=MESH)
            cp.wait_send()
            cp.wait_recv()

    res = pl.pallas_call(
        body, name=name, out_shape=[pltpu.HBM(a.shape, a.dtype) for a in gs + lands],
        in_specs=[HBM] * (2 * n) + [SEM, SEM, ANY], out_specs=[HBM] * (2 * n),
        input_output_aliases={k: k for k in range(2 * n)}, compiler_params=_SPLIT,
    )(*gs, *lands, send_sems, recv_sems, after)
    return list(res[:n]), list(res[n:])


def _pair_share(ss, *, name):
    n = len(ss)

    def body(*refs):
        outs, send_sems, recv_sems = refs[n:2 * n], refs[2 * n], refs[2 * n + 1]
        x, y, c = _place()
        cps = []
        for b in range(n):
            rh = ss[b].shape[0] // 2
            mine = outs[b].at[pl.ds(c * rh, rh), :]
            cps.append(pltpu.make_async_remote_copy(src_ref=mine, dst_ref=mine, send_sem=send_sems.at[b],
                                                    recv_sem=recv_sems.at[b], device_id=(x, y, 1 - c),
                                                    device_id_type=MESH))
        for cp in cps:
            cp.start()
        for b, cp in enumerate(cps):
            rh = ss[b].shape[0] // 2
            theirs = outs[b].at[pl.ds((1 - c) * rh, rh), :]
            pltpu.make_async_remote_copy(src_ref=theirs, dst_ref=theirs, send_sem=send_sems.at[b],
                                         recv_sem=recv_sems.at[b], device_id=(x, y, 1 - c),
                                         device_id_type=MESH).wait_recv()
            cp.wait_send()

    return _aliased_comm_call(body, ss, n, name=name)


_SMALL_SHARDED = (("e_conv_w", 2), ("o_norm", 1), ("o_d", 1))
_REPLICATED = ("e_norm", "e_gmlp_w", "e_gmlp_b", "e_conv_b", "e_conv_ln_g", "e_conv_ln_b", "o_lam_re", "o_lam_im",
               "o_log_dt", "o_b_re", "o_b_im", "o_c_re", "o_c_im", "ca_norm", "ca_mem_norm", "ffn_norm", "final_norm")
_SMALL = tuple(n for n, _ in _SMALL_SHARDED) + _REPLICATED
_WEIGHTS = ("e_norm", "e_w_in", "e_gmlp_w", "e_gmlp_b", "e_conv_w", "e_conv_b", "e_conv_ln_g", "e_conv_ln_b",
            "e_w_out", "o_norm", "o_w_in", "o_lam_re", "o_lam_im", "o_log_dt", "o_b_re", "o_b_im", "o_c_re", "o_c_im",
            "o_d", "o_w_out", "ca_norm", "ca_mem_norm", "ca_wq", "ca_wk", "ca_wv", "ca_wo", "ffn_norm", "ffn_w_gate",
            "ffn_w_up", "ffn_w_down", "final_norm")


def _pack_rows(arrs, width, dtype, row_mult=8):
    parts, spans, r0 = [], [], 0
    for a in arrs:
        flat = a.reshape(-1).astype(dtype)
        rows = -(-flat.shape[0] // (width * row_mult)) * row_mult
        if rows * width != flat.shape[0]:
            flat = jnp.pad(flat, (0, rows * width - flat.shape[0]))
        parts.append(flat.reshape(rows, width))
        spans.append((r0, rows))
        r0 += rows
    return jnp.concatenate(parts, axis=0), spans


def _unpack_rows(slab, spans, shapes):
    out = []
    for (r0, rows), shp in zip(spans, shapes):
        n = math.prod(shp)
        out.append(slab[r0:r0 + rows].reshape(-1)[:n].reshape(shp))
    return out


def _two_d(a):
    return a.reshape(-1, a.shape[-1])


def _shard_rows(n, a):
    return _two_d(jnp.swapaxes(a, -1, -2) if n in _TRANSPOSED else a)


def _from_shard_rows(n, rows, shape):
    if n in _TRANSPOSED:
        return jnp.swapaxes(rows.reshape(shape[:-2] + (shape[-1], shape[-2])), -1, -2)
    return rows.reshape(shape)


def _local_slab(local, slab, dtype):
    parts = sorted((r0, n, l) for n, (_, where) in _PLACE.items() for l, (s, r0) in enumerate(where) if s == slab)
    shards = [_shard_rows(n, local[n] if len(_PLACE[n][1]) == 1 else local[n][l]) for _, n, l in parts]
    return jnp.concatenate([a.astype(dtype) for a in shards], axis=0)


def _block_diag(b, pattern):
    return jnp.einsum(pattern, b, jnp.eye(C_GROUPS, dtype=b.dtype))


def _s5_discretize(lam_re, lam_im, log_dt, b_re, b_im):
    dt = jnp.exp(log_dt)[:, None]
    mag = jnp.exp(lam_re * dt)
    ar = mag * jnp.cos(lam_im * dt)
    ai = mag * jnp.sin(lam_im * dt)
    den = lam_re * lam_re + lam_im * lam_im
    qr = ((ar - 1.0) * lam_re + ai * lam_im) / den
    qi = (ai * lam_re - (ar - 1.0) * lam_im) / den
    bbr = qr[..., None] * b_re - qi[..., None] * b_im
    bbi = qr[..., None] * b_im + qi[..., None] * b_re
    return ar, ai, bbr, bbi


def _attention_block(x, mem, W, w, i, tag):
    xn, q = _norm_mm(x, w["ca_norm"][i], _shards(W, "ca_wq", i), split="k", out_dtype=BF16, name=f"{tag}_q")
    memn = _rms_fwd(mem, w["ca_mem_norm"][i], name=f"{tag}_ca_memnorm")
    k = _mm_k(memn, _shards(W, "ca_wk", i), out_dtype=BF16, name=f"{tag}_k")
    v = _mm_k(memn, _shards(W, "ca_wv", i), out_dtype=BF16, name=f"{tag}_v")
    o = _attn_fwd(q, k, v, name=f"{tag}_attn")
    y = _mm_k(o, _shards(W, "ca_wo", i), add=x, name=f"{tag}_wo")
    return y, (x, xn, memn, q, k, v, o)


def _attention_block_bwd(dy, saved, mem, W, w, i, tag, G, grads, token=None, mid=None):
    x, xn, memn, q, k, v, o = saved
    gain = w["ca_norm"][i]
    if token is not None:
        k = _behind(k, token)
    G = _grad_to_slab(G, "ca_wo", i, o, dy, a_cols=256, name=f"{tag}_dwo")
    dq, dk, dv = _attn_bwd(dy, _shards(W, "ca_wo", i), q, k, v, name=f"{tag}_attn_bwd")
    token = mid(dq) if mid is not None else None
    if token is not None:
        gain = _behind(gain, token)
    G = _grad_to_slab(G, "ca_wq", i, xn, dq, a_cols=256, name=f"{tag}_dwq")
    G = _grad_to_slab(G, "ca_wk", i, memn, dk, a_cols=256, name=f"{tag}_dwk")
    G = _grad_to_slab(G, "ca_wv", i, memn, dv, a_cols=256, name=f"{tag}_dwv")
    dmemn = _mm_k_t([(dk, _shards(W, "ca_wk", i)), (dv, _shards(W, "ca_wv", i))], name=f"{tag}_dmemn")
    dx, dg = _norm_bwd_k(dq, _shards(W, "ca_wq", i), x, gain, dy, name=f"{tag}_dq_norm_bwd")
    grads["ca_norm"][i] = dg[0]
    grads["ca_mem_norm"][i] = _rms_dg(mem, w["ca_mem_norm"][i], dmemn, name=f"{tag}_ca_memnorm_bwd")[0]
    return dx, G


def _ffn_block(x, W, w, i, tag):
    fn, gate, up, h = _ffn_up(x, w["ffn_norm"][i], _shards(W, "ffn_w_gate", i), _shards(W, "ffn_w_up", i),
                              name=f"{tag}_ffn_up")
    y = _mm_k(h, _shards(W, "ffn_w_down", i), add=x, name=f"{tag}_down")
    return y, (x, fn, gate, up, h)


def _ffn_block_bwd(dy, saved, W, w, i, tag, G, grads, token=None, mid=None):
    x, fn, gate, up, h = saved
    gain = w["ffn_norm"][i]
    G = _grad_to_slab(G, "ffn_w_down", i, h, dy, name=f"{tag}_dwd")
    dg, du = _ffn_bwd_hidden(dy, _shards(W, "ffn_w_down", i), gate, up, token, name=f"{tag}_ffn_bwd_hidden")
    token = mid(dg) if mid is not None else None
    if token is not None:
        gain = _behind(gain, token)
    G = _grad_to_slab(G, "ffn_w_gate", i, dg, fn, name=f"{tag}_dwg")
    G = _grad_to_slab(G, "ffn_w_up", i, du, fn, name=f"{tag}_dwu")
    dx, dgn = _ffn_in_bwd(dg, du, _shards(W, "ffn_w_gate", i), _shards(W, "ffn_w_up", i), x, gain, dy,
                          name=f"{tag}_ffn_in_bwd")
    grads["ffn_norm"][i] = dgn[0]
    return dx, G


def _gmlp_mask():
    chunk = jnp.arange(GMLP_BLOCK) // CHUNK
    return chunk[None, :] <= chunk[:, None]


def _even_block(x, W, w, tag):
    hn, proj = _norm_mm(x, w["e_norm"][0], _shards(W, "e_w_in"), split="n", out_dtype=F32, name=f"{tag}_w_in")
    wm = jnp.where(_gmlp_mask()[None], w["e_gmlp_w"][0], 0.0).astype(BF16)
    bcol = w["e_gmlp_b"][0][:, :, None]
    cw = jnp.pad(w["e_conv_w"][0], ((0, CONV_HALO - CONV_WIDTH), (0, 0)))
    cb, lg, lb = w["e_conv_b"], w["e_conv_ln_g"], w["e_conv_ln_b"]
    mix, hc = _even_fwd(proj, wm, bcol, cw, cb, lg, lb, name=f"{tag}_mixers")
    y = _mm_k(mix, _shards(W, "e_w_out"), add=x, name=f"{tag}_w_out")
    return y, (x, hn, proj, mix, hc, wm, bcol, cw)


def _even_block_bwd(dy, saved, W, w, tag, G, grads):
    x, hn, proj, mix, hc, wm, bcol, cw = saved
    dmix = _mm_k_t([(dy, _shards(W, "e_w_out"))], name=f"{tag}_dmix")
    G = _grad_to_slab(G, "e_w_out", 0, mix, dy, a_cols=256, name=f"{tag}_dw_out")
    wmt = jnp.swapaxes(wm, 1, 2)
    dpa, dhc, dwm, db, dlg, dlb, dcb = _even_bwd1(proj, dmix, hc, wm, wmt, bcol, w["e_conv_ln_g"], w["e_conv_ln_b"],
                                                  name=f"{tag}_mixers_bwd1")
    dpb, dcw = _even_bwd2(proj, dhc, cw, name=f"{tag}_mixers_bwd2")
    grads["e_gmlp_w"] = jnp.where(_gmlp_mask()[None], dwm, 0.0)[None]
    grads["e_gmlp_b"] = db[:, :, 0][None]
    grads["e_conv_ln_g"], grads["e_conv_ln_b"], grads["e_conv_b"] = dlg, dlb, dcb
    grads["e_conv_w"] = dcw[:CONV_WIDTH][None]
    G = _grad_to_slab(G, "e_w_in", 0, hn, dpa, b_cols=512, chips=(0, 2), name=f"{tag}_dw_in_a")
    G = _grad_to_slab(G, "e_w_in", 0, hn, dpb, b_cols=512, chips=(2, 2), name=f"{tag}_dw_in_b")
    dx, dg = _norm_bwd_n((dpa, dpb), _shards(W, "e_w_in"), x, w["e_norm"][0], dy, name=f"{tag}_in_bwd")
    grads["e_norm"] = dg
    return dx, G


def _odd_block(x, W, w, tag):
    S = x.shape[0]
    hn, u = _norm_mm(x, w["o_norm"][0], _shards(W, "o_w_in"), split="k", out_dtype=F32, name=f"{tag}_w_in")
    disc_in = (w["o_lam_re"][0], w["o_lam_im"][0], w["o_log_dt"][0], w["o_b_re"][0], w["o_b_im"][0])
    (ar, ai, bbr, bbi), disc_vjp = jax.vjp(_s5_discretize, *disc_in)
    bd = jnp.concatenate([_block_diag(bbr, "gpc,gh->gchp").reshape(C_WIDTH, N_STATE),
                          _block_diag(bbi, "gpc,gh->gchp").reshape(C_WIDTH, N_STATE)], axis=1).astype(BF16)
    cd = jnp.concatenate([_block_diag(w["o_c_re"][0], "gcp,gh->gphc").reshape(N_STATE, C_WIDTH),
                          -_block_diag(w["o_c_im"][0], "gcp,gh->gphc").reshape(N_STATE, C_WIDTH)], axis=0).astype(BF16)
    powers, pr, pi = [], ar, ai
    for _ in range(SCAN_BLOCK):
        powers.append(jnp.concatenate([pr.reshape(STATE_ROWS, STATE_LANES), pi.reshape(STATE_ROWS, STATE_LANES)], 0))
        pr, pi = pr * ar - pi * ai, pr * ai + pi * ar
    pw = jnp.stack(powers, axis=0)
    state_rows = (S * _STATE_TILE, STATE_LANES)
    bu = _mm_to_state(u, bd, name=f"{tag}_bu")
    xs = _scan_fwd(bu.reshape(state_rows), pw, name=f"{tag}_scan").reshape(bu.shape)
    yv, yg = _s5_readout(xs, cd, u, w["o_d"], name=f"{tag}_readout")
    o, y = _glu_out(yg, _shards(W, "o_w_out"), x, name=f"{tag}_glu_out")
    return y, (x, hn, u, bd, cd, pw, xs, yv, yg, o, disc_vjp)


def _odd_block_bwd(dy, saved, W, w, tag, G, grads):
    x, hn, u, bd, cd, pw, xs, yv, yg, o, disc_vjp = saved
    S = x.shape[0]
    state_rows = (S * _STATE_TILE, STATE_LANES)
    do, dys, dus, dd = _glu_out_bwd(o, dy, _shards(W, "o_w_out"), yv, u, w["o_d"], name=f"{tag}_glu_out_bwd")
    G = _grad_to_slab(G, "o_w_out", 0, yg, do, b_cols=512, name=f"{tag}_dw_out")
    grads["o_d"] = dd
    dxs = _mm_to_state(dys, cd, nt=True, name=f"{tag}_dxs")
    dcd_t = _state_grad_tn(dys, xs, name=f"{tag}_dcd")
    gs, da = _scan_bwd(dxs.reshape(state_rows), xs.reshape(state_rows), pw, name=f"{tag}_scan_bwd")
    gs = gs.reshape(xs.shape)
    dbd = _state_grad_tn(u, gs, name=f"{tag}_dbd")
    du, dx, dg = _s5_in_bwd(gs, bd, dus, _shards(W, "o_w_in"), x, w["o_norm"][0], dy, name=f"{tag}_in_bwd")
    G = _grad_to_slab(G, "o_w_in", 0, hn, du, a_cols=256, name=f"{tag}_dw_in")
    grads["o_norm"] = dg
    eye = jnp.eye(C_GROUPS, dtype=F32)
    dcr = jnp.einsum("hcgp,gh->gcp", dcd_t[:, :N_STATE].reshape(C_GROUPS, C_GROUP_CH, C_GROUPS, C_STATE), eye)
    dci = -jnp.einsum("hcgp,gh->gcp", dcd_t[:, N_STATE:].reshape(C_GROUPS, C_GROUP_CH, C_GROUPS, C_STATE), eye)
    dbbr = jnp.einsum("gchp,gh->gpc", dbd[:, :N_STATE].reshape(C_GROUPS, C_GROUP_CH, C_GROUPS, C_STATE), eye)
    dbbi = jnp.einsum("gchp,gh->gpc", dbd[:, N_STATE:].reshape(C_GROUPS, C_GROUP_CH, C_GROUPS, C_STATE), eye)
    dar = da[:STATE_ROWS].reshape(C_GROUPS, C_STATE)
    dai = da[STATE_ROWS:].reshape(C_GROUPS, C_STATE)
    dlr, dli, dldt, dbr, dbi = disc_vjp((dar, dai, dbbr, dbbi))
    grads["o_lam_re"], grads["o_lam_im"], grads["o_log_dt"] = dlr[None], dli[None], dldt[None]
    grads["o_b_re"], grads["o_b_im"], grads["o_c_re"], grads["o_c_im"] = dbr[None], dbi[None], dcr[None], dci[None]
    return dx, G


def _behind(value, token):
    return value + token[0, 0].astype(value.dtype)


class _NoExchange:
    def __init__(self, W):
        self.W = W

    def first_weights(self, w):
        return self.W, w

    def weights(self, stage, after):
        return {}

    def grads_ready(self, piece, G):
        return None

    def grads_crossed(self, piece, after):
        return None


def _forward_backward(xs_, mems_, tgt, w, G, exchange):
    W, w = exchange.first_weights(w)
    x1, s_mix0 = _even_block(xs_, W, w, "l0")
    W = {**W, **exchange.weights(1, x1)}
    x2, s_att0 = _attention_block(x1, mems_, W, w, 0, "l0")
    W = {**W, **exchange.weights(2, x2)}
    x3, s_ffn0 = _ffn_block(x2, W, w, 0, "l0")
    W = {**W, **exchange.weights(3, x3)}
    x4, s_mix1 = _odd_block(x3, W, w, "l1")
    x5, s_att1 = _attention_block(x4, mems_, W, w, 1, "l1")
    x6, s_ffn1 = _ffn_block(x5, W, w, 1, "l1")
    dx, dfinal, loss_lanes = _loss_head(x6, w["final_norm"], tgt, name="loss_head")

    grads = {n: [None, None] for n in ("ca_norm", "ca_mem_norm", "ffn_norm")}
    grads["final_norm"] = dfinal[0]
    dx, G = _ffn_block_bwd(dx, s_ffn1, W, w, 1, "l1", G, grads)
    dx, G = _attention_block_bwd(dx, s_att1, mems_, W, w, 1, "l1", G, grads)
    dx, G = _odd_block_bwd(dx, s_mix1, W, w, "l1", G, grads)
    token = exchange.grads_ready("l1", G)
    dx, G = _ffn_block_bwd(dx, s_ffn0, W, w, 0, "l0", G, grads, token,
                           lambda after: exchange.grads_crossed("l1", after))
    token = exchange.grads_ready("ffn0", G)
    dx, G = _attention_block_bwd(dx, s_att0, mems_, W, w, 0, "l0", G, grads, token,
                                 lambda after: exchange.grads_crossed("ffn0", after))
    dx, G = _even_block_bwd(dx, s_mix0, W, w, "l0", G, grads)
    for n in list(grads):
        if isinstance(grads[n], list):
            grads[n] = jnp.stack(grads[n], axis=0)
        grads[n] = grads[n].reshape(w[n].shape)
    return loss_lanes, dx, G, grads


class _Exchange:
    def __init__(self, local, chip, core):
        self.bufs = {s: lax.dynamic_update_slice(lax.empty((N_CHIPS, rows, width), BF16),
                                                 _local_slab(local, s, BF16)[None], (chip, 0, 0))
                     for s, (width, rows) in _SLABS.items()}
        self.half = core.reshape(1).astype(jnp.int32)
        self.where = jnp.stack([chip, core]).astype(jnp.int32)
        self.flights = []
        self.reduces = {}

    def weights(self, stage, after):
        send_sems, recv_sems, bufs, _ = self.flights[stage]
        bufs = _gather_ici_wait(send_sems, recv_sems, bufs, after, name=f"gather_stage{stage}_wait")
        return dict(zip(_STAGES[stage], _gather_forward(bufs, name=f"gather_stage{stage}_forward")))

    def first_weights(self, w):
        after = w["e_conv_w"].reshape(-1)[:STATE_LANES]
        for k, stage in enumerate(_STAGES):
            self.flights.append(_gather_ici_start([self.bufs[s] for s in stage], after, name=f"gather_stage{k}_start"))
            after = self.flights[-1][3]
        return self.weights(0, after), {**w, "e_norm": _behind(w["e_norm"], after)}

    def pair_start(self, G, slabs, tag):
        send_sems, recv_sems, gl, lands, token = _pair_exchange_start([G[s] for s in slabs],
                                                                      name=f"grad_{tag}_pair_start")
        return (slabs, send_sems, recv_sems, gl, lands), token

    def pair_land(self, state, after, tag):
        slabs, send_sems, recv_sems, gl, lands = state
        gl, other = _pair_exchange_wait(send_sems, recv_sems, gl, lands, after, name=f"grad_{tag}_pair_wait")
        pairs = [_pair_sum(g, r, self.half, name=f"grad_pair_sum_{s}") for s, g, r in zip(slabs, gl, other)]
        send_sems, recv_sems, pairs, lands, token = _chip_exchange_start(pairs, name=f"grad_{tag}_chip_start")
        return (slabs, gl, other, send_sems, recv_sems, pairs, lands), token

    def reduce_finish(self, state, after, tag):
        slabs, gl, other, send_sems, recv_sems, pairs, lands = state
        slots = _chip_exchange_wait(send_sems, recv_sems, pairs, lands, after, name=f"grad_{tag}_chip_wait")
        halves = [_chip_sum(g, r, sl, self.where, name=f"grad_chip_sum_{s}")
                  for s, g, r, sl in zip(slabs, gl, other, slots)]
        return dict(zip(slabs, _pair_share(halves, name=f"grad_{tag}_pair_share")))

    def grads_ready(self, piece, G):
        self.reduces[piece], token = self.pair_start(G, _GRAD_PIECES[piece], piece)
        return token

    def grads_crossed(self, piece, after):
        self.reduces[piece], token = self.pair_land(self.reduces[piece], after, piece)
        return token


def kernel(x, mem, e_norm, e_w_in, e_gmlp_w, e_gmlp_b, e_conv_w, e_conv_b, e_conv_ln_g, e_conv_ln_b, e_w_out, o_norm, o_w_in, o_lam_re, o_lam_im, o_log_dt, o_b_re, o_b_im, o_c_re, o_c_im, o_d, o_w_out, ca_norm, ca_mem_norm, ca_wq, ca_wk, ca_wv, ca_wo, ffn_norm, ffn_w_gate, ffn_w_up, ffn_w_down, final_norm, loss_target, m_e_norm, m_e_w_in, m_e_gmlp_w, m_e_gmlp_b, m_e_conv_w, m_e_conv_b, m_e_conv_ln_g, m_e_conv_ln_b, m_e_w_out, m_o_norm, m_o_w_in, m_o_lam_re, m_o_lam_im, m_o_log_dt, m_o_b_re, m_o_b_im, m_o_c_re, m_o_c_im, m_o_d, m_o_w_out, m_ca_norm, m_ca_mem_norm, m_ca_wq, m_ca_wk, m_ca_wv, m_ca_wo, m_ffn_norm, m_ffn_w_gate, m_ffn_w_up, m_ffn_w_down, m_final_norm, v_e_norm, v_e_w_in, v_e_gmlp_w, v_e_gmlp_b, v_e_conv_w, v_e_conv_b, v_e_conv_ln_g, v_e_conv_ln_b, v_e_w_out, v_o_norm, v_o_w_in, v_o_lam_re, v_o_lam_im, v_o_log_dt, v_o_b_re, v_o_b_im, v_o_c_re, v_o_c_im, v_o_d, v_o_w_out, v_ca_norm, v_ca_mem_norm, v_ca_wq, v_ca_wk, v_ca_wv, v_ca_wo, v_ffn_norm, v_ffn_w_gate, v_ffn_w_up, v_ffn_w_down, v_final_norm):
    args = dict(locals())
    local = {n: args[n] for n in _WEIGHTS}
    mom = {n: args["m_" + n] for n in _WEIGHTS}
    vel = {n: args["v_" + n] for n in _WEIGHTS}
    chip = 2 * lax.axis_index("x") + lax.axis_index("y")
    core = lax.axis_index("c")
    xs_, mems_, tgt = x[0], mem[0], loss_target[0]

    w = {n: local[n] for n in _REPLICATED}
    sm_slab, sm_spans = _pack_rows([local[n] for n, _ in _SMALL_SHARDED], SMALL_W, F32)
    sm_all = _allgather_small(sm_slab, name="gather_small_weights").reshape(N_DEV, -1, SMALL_W)
    for (n, ax), span in zip(_SMALL_SHARDED, sm_spans):
        shp = local[n].shape
        w[n] = jnp.concatenate([_unpack_rows(sm_all[2 * p], [span], [shp])[0] for p in range(N_CHIPS)], axis=ax)

    exchange = _Exchange(local, chip, core)
    G = {s: lax.empty((N_CHIPS, rows, width), F32) for s, (width, rows) in _SLABS.items()}
    loss_lanes, dx, G, grads = _forward_backward(xs_, mems_, tgt, w, G, exchange)

    gs_slab, gs_spans = _pack_rows([grads[n] for n in _SMALL] + [loss_lanes], SMALL_W, F32)
    small_flight = _all_to_all_start(gs_slab, dx, name="small_grads_start")
    exchange.grads_ready("rest0", G)
    gsum = exchange.reduce_finish(exchange.reduces["l1"], small_flight[4], "l1")
    gsum = {**gsum, **exchange.reduce_finish(exchange.reduces["ffn0"], small_flight[4], "ffn0")}
    token = exchange.grads_crossed("rest0", gsum["B0"])

    gs_slab, gs_all = _all_to_all_wait(*small_flight[:4], token, name="small_grads_wait")
    gs_all = lax.dynamic_update_slice(gs_all, gs_slab[None], (2 * chip + core, 0, 0))
    gs_sum = _sum_slots(gs_all, name="small_grad_sum")
    *small_sums, loss_sum = _unpack_rows(gs_sum, gs_spans, [grads[n].shape for n in _SMALL] + [loss_lanes.shape])
    out_grads = dict(zip(_SMALL, small_sums))
    for n, ax in _SMALL_SHARDED:
        width = local[n].shape[ax]
        out_grads[n] = lax.dynamic_slice_in_dim(out_grads[n], chip * width, width, axis=ax)

    delta, new_m, new_v = {}, {}, {}
    d_, m_, v_ = _adamw_small([_two_d(local[n]) for n in _SMALL], [_two_d(out_grads[n]) for n in _SMALL],
                              [_two_d(mom[n]) for n in _SMALL], [_two_d(vel[n]) for n in _SMALL], name="adamw_small")
    for n, dd, mm_, vv in zip(_SMALL, d_, m_, v_):
        shp = local[n].shape
        delta[n], new_m[n], new_v[n] = dd.reshape(shp), mm_.reshape(shp), vv.reshape(shp)
    def adamw_large(names):
        for n in names:
            shp = local[n].shape
            g_, d_, m_, v_ = _adamw_shard(_shard_rows(n, local[n]), [(gsum[s], r0) for s, r0 in _PLACE[n][1]],
                                          _shard_rows(n, mom[n]), _shard_rows(n, vel[n]), name=f"adamw_{n}")
            out_grads[n], delta[n], new_m[n], new_v[n] = (_from_shard_rows(n, t, shp) for t in (g_, d_, m_, v_))

    ready = [n for n, (_, where) in _PLACE.items() if all(s in gsum for s, _ in where)]
    adamw_large(ready)
    done = jnp.concatenate([delta[n].reshape(-1)[:1] for n in ready + list(_SMALL[:1])])
    gsum = {**gsum, **exchange.reduce_finish(exchange.reduces["rest0"], done, "rest0")}
    adamw_large([n for n in _PLACE if n not in ready])

    return (loss_sum[0, 0], dx[None], *[out_grads[n] for n in _WEIGHTS], *[delta[n] for n in _WEIGHTS],
            *[new_m[n] for n in _WEIGHTS], *[new_v[n] for n in _WEIGHTS])
```

```python
import functools
import math

import jax
import jax.numpy as jnp
from jax import lax
from jax.experimental import pallas as pl
from jax.experimental.pallas import tpu as pltpu

F32 = jnp.float32
BF16 = jnp.bfloat16
MESH = pl.DeviceIdType.MESH

EPS = 1e-6
D_MODEL = 1024
A_WIDTH = 512
A_GROUPS = 4
GMLP_BLOCK = 128
CHUNK = 64
B_WIDTH = 512
CONV_WIDTH = 31
CONV_HALO = 32
C_WIDTH = 512
C_GROUP_CH = 16
C_GROUPS = 32
C_STATE = 64
N_STATE = C_GROUPS * C_STATE
STATE_LANES = 128
STATE_ROWS = N_STATE // STATE_LANES
SCAN_BLOCK = 8
CA_HEADS = 4
CA_HEAD_DIM = 256
FFN_HIDDEN = 2816

ADAM_LR = 0.001
ADAM_B1 = 0.9
ADAM_B2 = 0.999
ADAM_EPS = 1e-08
ADAM_WD = 0.01
ADAM_STEP = 10

VMEM_LIMIT = 56 * 1024 * 1024
ACC_BYTES = 6 * 1024 * 1024
TN_VMEM_BYTES = 44 * 1024 * 1024
SMALL_W = 128
N_CHIPS = 4
N_DEV = 8

_SLABS = {"D0": (512, 1024), "E0": (1024, 256), "A0": (1024, 1024), "B0": (1024, 704), "C0": (1024, 1408),
          "D1": (512, 768), "A1": (1024, 1024), "B1": (1024, 704), "C1": (1024, 1408)}
_STAGES = (("D0", "E0"), ("A0",), ("B0", "C0"), ("D1", "A1", "B1", "C1"))
_GRAD_PIECES = {"l1": _STAGES[3], "ffn0": _STAGES[2], "rest0": _STAGES[0] + _STAGES[1]}
_PLACE = {
    "e_w_in": (1024, (("D0", 0),)), "e_w_out": (256, (("E0", 0),)),
    "o_w_out": (512, (("D1", 0),)), "o_w_in": (256, (("D1", 512),)),
    "ca_wq": (256, (("A0", 0), ("A1", 0))), "ca_wk": (256, (("A0", 256), ("A1", 256))),
    "ca_wv": (256, (("A0", 512), ("A1", 512))), "ca_wo": (256, (("A0", 768), ("A1", 768))),
    "ffn_w_down": (704, (("B0", 0), ("B1", 0))),
    "ffn_w_gate": (704, (("C0", 0), ("C1", 0))), "ffn_w_up": (704, (("C0", 704), ("C1", 704))),
}
_TRANSPOSED = ("ffn_w_gate", "ffn_w_up")


def _params(sem=None):
    return pltpu.CompilerParams(dimension_semantics=sem, vmem_limit_bytes=VMEM_LIMIT)


def _tile(n, pref, mult=128):
    if n <= pref:
        return n
    t = (pref // mult) * mult
    while t >= mult:
        if n % t == 0:
            return t
        t -= mult
    return n


def _blk(name, layer=0):
    rows, where = _PLACE[name]
    slab, r0 = where[layer]
    assert r0 % rows == 0
    return slab, rows, r0 // rows


def _shards(slabs, name, layer=0):
    slab, rows, b = _blk(name, layer)
    return [(slabs[slab], (None, rows, _SLABS[slab][0]), (p, b, 0)) for p in range(N_CHIPS)]


_GELU_C = 0.7978845608028654
_GELU_A = 0.044715


def _gelu(x):
    t = jnp.tanh(_GELU_C * (x + _GELU_A * (x * x * x)))
    return 0.5 * x * (1.0 + t), t


def _gelu_grad(x, t):
    return 0.5 * (1.0 + t) + 0.5 * x * (1.0 - t * t) * (_GELU_C * (1.0 + 3.0 * _GELU_A * x * x))


def _sigmoid(x):
    return 1.0 / (1.0 + jnp.exp(-x))


def _mean(x):
    return jnp.mean(x, axis=-1, keepdims=True)


def _dot(a, b):
    return jnp.dot(a, b, preferred_element_type=F32)


def _dot_nt(a, b):
    return lax.dot_general(a, b, (((1,), (1,)), ((), ())), preferred_element_type=F32)


def _dot_tn(a, b):
    return lax.dot_general(a, b, (((0,), (0,)), ((), ())), preferred_element_type=F32)


def _rms_tile(xv, gv):
    return (xv * lax.rsqrt(_mean(xv * xv) + EPS)) * gv


def _rms_bwd_tile(xv, gv, dyv):
    r = lax.rsqrt(_mean(xv * xv) + EPS)
    xh = xv * r
    dyg = dyv * gv
    return r * (dyg - xh * _mean(dyg * xh)), jnp.sum(dyv * xh, axis=0, keepdims=True)


def _cols(p, width):
    return slice(p * width, (p + 1) * width)


def _sum_k(a, ws, k):
    tot = None
    for p in range(N_CHIPS):
        y = _dot(a[:, _cols(p, k)], ws[p][...])
        tot = y if tot is None else tot + y
    return tot


def _cat_nt(a, ws):
    return jnp.concatenate([_dot_nt(a, ws[p][...]) for p in range(N_CHIPS)], axis=1)


def _rows_call(name, tm, rows, fulls, outs, accs, body, scratch=()):
    S = min(x.shape[-2] for x in rows if x.ndim != 4)
    nr, nf, no, na = len(rows), len(fulls), len(outs), len(accs)

    def kern(*refs):
        r, f = refs[:nr], refs[nr:nr + nf]
        o, a = refs[nr + nf:nr + nf + no], refs[nr + nf + no:nr + nf + no + na]
        if na:
            @pl.when(pl.program_id(0) == 0)
            def _():
                for ref in a:
                    ref[...] = jnp.zeros_like(ref)
        body(r, f, o, a, refs[nr + nf + no + na:])

    def whole(shape):
        nd = len(shape)
        return pl.BlockSpec(tuple(shape), lambda i: (0,) * nd)

    def row_spec(shape):
        if len(shape) == 4:
            return pl.BlockSpec((tm // 8,) + tuple(shape[1:]), lambda i: (i, 0, 0, 0))
        if len(shape) == 3:
            return pl.BlockSpec((shape[0], tm, shape[2]), lambda i: (0, i, 0))
        return pl.BlockSpec((tm, shape[1]), lambda i: (i, 0))

    def full_spec(x):
        if isinstance(x, tuple):
            _, bshape, bidx = x
            return pl.BlockSpec(bshape, lambda i: bidx, pipeline_mode=pl.Buffered(1))
        return whole(x.shape)

    def out_shape_of(o):
        if o[0] == "state":
            return (S // 8, 2 * STATE_ROWS, 8, STATE_LANES)
        return (S, o[0]) if len(o) == 2 else (o[0], S, o[1])

    out_shapes = [out_shape_of(o) for o in outs]
    res = pl.pallas_call(
        kern, name=name, grid=(S // tm,),
        in_specs=[row_spec(x.shape) for x in rows] + [full_spec(x) for x in fulls],
        out_specs=[row_spec(s) for s in out_shapes] + [whole(shp) for shp, _ in accs],
        out_shape=[jax.ShapeDtypeStruct(s, o[-1]) for s, o in zip(out_shapes, outs)]
        + [jax.ShapeDtypeStruct(tuple(shp), dt) for shp, dt in accs],
        scratch_shapes=list(scratch),
        compiler_params=_params(("arbitrary",) if na else ("parallel",)),
    )(*rows, *[x[0] if isinstance(x, tuple) else x for x in fulls])
    return res[:no], res[no:]


def _grad_to_slab(gslabs, wname, layer, a, b, *, a_cols=None, b_cols=None, chips=(0, N_CHIPS), name):
    slab, rows, bidx = _blk(wname, layer)
    width = _SLABS[slab][0]
    p0, n_p = chips
    assert p0 % n_p == 0
    S = a.shape[-2]

    def tile_bytes(x, ts):
        return ts * x.dtype.itemsize * (x.shape[2] * n_p if x.ndim == 3 else x.shape[1])

    acc_bytes = n_p * rows * (-(-width // 128) * 128) * 4
    ts = next(t for t in (2048, 1024, 512, 256, S) if S % t == 0
              and 2 * (tile_bytes(a, t) + tile_bytes(b, t) + acc_bytes) <= TN_VMEM_BYTES or t == S)

    def operand(x):
        if x.ndim == 3:
            return pl.BlockSpec((n_p, ts, x.shape[2]), lambda s: (p0 // n_p, s, 0))
        return pl.BlockSpec((ts, x.shape[1]), lambda s: (s, 0))

    def part(ref, cols, p):
        if len(ref.shape) == 3:
            return ref[p]
        return ref[...] if cols is None else ref[:, _cols(p, cols)]

    def body(a_ref, b_ref, slab_ref, o_ref):
        @pl.when(pl.program_id(0) == 0)
        def _():
            o_ref[...] = jnp.zeros_like(o_ref)

        for p in range(n_p):
            o_ref[p] += _dot_tn(part(a_ref, a_cols, p).astype(BF16), part(b_ref, b_cols, p).astype(BF16))

    g = gslabs[slab]
    out = pl.pallas_call(
        body, name=name, grid=(S // ts,),
        in_specs=[operand(a), operand(b), pl.BlockSpec(memory_space=pl.ANY)],
        out_specs=pl.BlockSpec((n_p, rows, width), lambda s: (p0 // n_p, bidx, 0)),
        out_shape=jax.ShapeDtypeStruct(g.shape, F32), input_output_aliases={2: 0},
        compiler_params=_params(("arbitrary",)),
    )(a, b, g)
    return {**gslabs, slab: out}


def _vec(g):
    return g.reshape(1, -1)


def _norm_mm(x, g, ws, *, split, out_dtype, name, tm=512):
    S, D = x.shape
    k, n = ws[0][1][1], ws[0][1][2]
    N = n if split == "k" else N_CHIPS * n

    def body(r, f, o, acc, s):
        xn = _rms_tile(r[0][...], f[0][...]).astype(BF16)
        o[0][...] = xn
        if split == "k":
            o[1][...] = _sum_k(xn, f[1:], k).astype(out_dtype)
        else:
            for p in range(N_CHIPS):
                o[1][:, _cols(p, n)] = _dot(xn, f[1 + p][...]).astype(out_dtype)

    (xn, y), _ = _rows_call(name, _tile(S, tm), [x], [_vec(g)] + ws, [(D, BF16), (N, out_dtype)], [], body)
    return xn, y


def _mm_k(a, ws, *, add=None, out_dtype=F32, name, tm=512):
    S = a.shape[-2]
    k, n = ws[0][1][1], ws[0][1][2]
    has_add = add is not None

    def body(r, f, o, acc, s):
        if a.ndim == 3:
            y = None
            for p in range(N_CHIPS):
                t = _dot(r[0][p].astype(BF16), f[p][...])
                y = t if y is None else y + t
        else:
            y = _sum_k(r[0][...].astype(BF16), f, k)
        if has_add:
            y = y + r[1][...]
        o[0][...] = y.astype(out_dtype)

    (y,), _ = _rows_call(name, _tile(S, tm), [a] + ([add] if has_add else []), ws, [(n, out_dtype)], [], body)
    return y


def _mm_k_t(terms, *, out_dtype=F32, name, tm=512):
    S = terms[0][0].shape[0]
    k = terms[0][1][0][1][1]

    def body(r, f, o, acc, s):
        y = None
        for t in range(len(terms)):
            yt = _cat_nt(r[t][...].astype(BF16), f[N_CHIPS * t:N_CHIPS * (t + 1)])
            y = yt if y is None else y + yt
        o[0][...] = y.astype(out_dtype)

    (y,), _ = _rows_call(name, _tile(S, tm), [a for a, _ in terms], [w for _, ws in terms for w in ws],
                         [(N_CHIPS * k, out_dtype)], [], body)
    return y


def _rms_fwd(x, g, *, name):
    def body(r, f, o, acc, s):
        o[0][...] = _rms_tile(r[0][...], f[0][...]).astype(BF16)

    (y,), _ = _rows_call(name, _tile(x.shape[0], 256, 8), [x], [_vec(g)], [(x.shape[1], BF16)], [], body)
    return y


def _rms_dg(x, g, dy, *, name):
    def body(r, f, o, acc, s):
        acc[0][...] += _rms_bwd_tile(r[0][...], f[0][...], r[1][...])[1]

    _, (dg,) = _rows_call(name, _tile(x.shape[0], 256, 8), [x, dy], [_vec(g)], [], [((1, x.shape[1]), F32)], body)
    return dg


def _ffn_up(x, g, wg, wu, *, name, tm=256):
    S, D = x.shape
    h = wg[0][1][1]

    def body(r, f, o, acc, s):
        xn = _rms_tile(r[0][...], f[0][...]).astype(BF16)
        o[0][...] = xn
        for p in range(N_CHIPS):
            gate = _dot_nt(xn, f[1 + p][...])
            up = _dot_nt(xn, f[1 + N_CHIPS + p][...])
            o[1][p] = gate.astype(BF16)
            o[2][p] = up.astype(BF16)
            o[3][p] = (gate * _sigmoid(gate) * up).astype(BF16)

    (xn, gate, up, hid), _ = _rows_call(name, _tile(S, tm), [x], [_vec(g)] + wg + wu,
                                        [(D, BF16), (N_CHIPS, h, BF16), (N_CHIPS, h, BF16), (N_CHIPS, h, BF16)], [],
                                        body)
    return xn, gate, up, hid


def _ffn_bwd_hidden(dy, wd, gate, up, token=None, *, name, tm=256):
    S = dy.shape[0]
    h = wd[0][1][1]

    def body(r, f, o, acc, s):
        dyv = r[0][...]
        if token is not None:
            dyv = dyv + jnp.sum(f[N_CHIPS][...])
        dyb = dyv.astype(BF16)
        for p in range(N_CHIPS):
            dh = _dot_nt(dyb, f[p][...])
            gv = r[1][p].astype(F32)
            sg = _sigmoid(gv)
            o[0][p] = (dh * r[2][p].astype(F32) * (sg * (1.0 + gv * (1.0 - sg)))).astype(BF16)
            o[1][p] = (dh * gv * sg).astype(BF16)

    (dg, du), _ = _rows_call(name, _tile(S, tm), [dy, gate, up], wd + ([] if token is None else [token]),
                             [(N_CHIPS, h, BF16), (N_CHIPS, h, BF16)], [], body)
    return dg, du


def _ffn_in_bwd(dg, du, wg, wu, x, g, dres, *, name, tm=256):
    S, D = x.shape

    def body(r, f, o, acc, s):
        tot = None
        for p in range(N_CHIPS):
            y = _dot(r[0][p], f[1 + p][...]) + _dot(r[1][p], f[1 + N_CHIPS + p][...])
            tot = y if tot is None else tot + y
        dx, dgn = _rms_bwd_tile(r[2][...], f[0][...], tot)
        o[0][...] = dx + r[3][...]
        acc[0][...] += dgn

    (dx,), (dgn,) = _rows_call(name, _tile(S, tm), [dg, du, x, dres], [_vec(g)] + wg + wu, [(D, F32)],
                               [((1, D), F32)], body)
    return dx, dgn


def _norm_bwd_k(da, ws, x, g, dres, *, name, tm=512):
    S, D = x.shape

    def body(r, f, o, acc, s):
        dx, dg = _rms_bwd_tile(r[1][...], f[0][...], _cat_nt(r[0][...].astype(BF16), f[1:]))
        o[0][...] = dx + r[2][...]
        acc[0][...] += dg

    (dx,), (dg,) = _rows_call(name, _tile(S, tm), [da, x, dres], [_vec(g)] + ws, [(D, F32)], [((1, D), F32)], body)
    return dx, dg


def _norm_bwd_n(das, ws, x, g, dres, *, name, tm=256):
    S, D = x.shape
    n = ws[0][1][2]

    def body(r, f, o, acc, s):
        tot = None
        for p in range(N_CHIPS):
            y = _dot_nt(r[p // 2][:, _cols(p % 2, n)], f[1 + p][...])
            tot = y if tot is None else tot + y
        dx, dg = _rms_bwd_tile(r[2][...], f[0][...], tot)
        o[0][...] = dx + r[3][...]
        acc[0][...] += dg

    (dx,), (dg,) = _rows_call(name, _tile(S, tm), list(das) + [x, dres], [_vec(g)] + ws, [(D, F32)], [((1, D), F32)],
                              body)
    return dx, dg


def _ln_stats(v):
    mu = _mean(v)
    xc = v - mu
    rstd = lax.rsqrt(_mean(xc * xc) + EPS)
    return xc * rstd, rstd


_SHIFTS = 8
_CONV_ROWS = 64


def _fill_shifts(sh_ref, ext_ref, tm):
    sh_ref[0] = ext_ref[...]
    for s in range(1, _SHIFTS):
        sh_ref[s, 0:tm + CONV_HALO - _SHIFTS, :] = ext_ref[pl.ds(s, tm + CONV_HALO - _SHIFTS), :]


def _window(sh_ref, off, tm):
    return sh_ref[off % _SHIFTS, pl.ds(off - off % _SHIFTS, tm), :]


def _even_fwd(proj, wm, bcol, cw, cb, lg, lb, *, name):
    S = proj.shape[0]
    tm = _tile(S, 256)
    hb = tm // CONV_HALO
    nblk = tm // GMLP_BLOCK

    def body(p_ref, halo_ref, wm_ref, b_ref, cw_ref, cb_ref, lg_ref, lb_ref, mix_ref, hc_ref, hext_ref, hsh_ref):
        i = pl.program_id(0)
        gu, _ = _gelu(p_ref[:, 0:A_WIDTH])
        gv, _ = _gelu(p_ref[:, A_WIDTH:2 * A_WIDTH])
        vn, _ = _ln_stats(gv)
        vnb = vn.astype(BF16)
        for n in range(nblk):
            rows = slice(n * GMLP_BLOCK, (n + 1) * GMLP_BLOCK)
            for g in range(A_GROUPS):
                cols = slice(g * GMLP_BLOCK, (g + 1) * GMLP_BLOCK)
                sg = jnp.dot(wm_ref[g], vnb[rows, cols], preferred_element_type=F32) + b_ref[g]
                mix_ref[rows, cols] = (gu[rows, cols] * sg).astype(BF16)
        h = p_ref[:, 1024:1536] * _sigmoid(p_ref[:, 1536:2048])
        hh = halo_ref[:, 0:B_WIDTH] * _sigmoid(halo_ref[:, B_WIDTH:2 * B_WIDTH])
        hext_ref[0:CONV_HALO, :] = jnp.where(i > 0, hh, 0.0)
        hext_ref[CONV_HALO:CONV_HALO + tm, :] = h
        _fill_shifts(hsh_ref, hext_ref, tm)
        for r0 in range(0, tm, _CONV_ROWS):
            acc = jnp.zeros((_CONV_ROWS, B_WIDTH), F32)
            for k in range(CONV_WIDTH):
                acc = acc + cw_ref[k:k + 1, :] * _window(hsh_ref, r0 + k + CONV_HALO - CONV_WIDTH + 1, _CONV_ROWS)
            hc_ref[r0:r0 + _CONV_ROWS, :] = acc + cb_ref[...]
        hc = hc_ref[...]
        hhat, _ = _ln_stats(hc)
        hl = hhat * lg_ref[...] + lb_ref[...]
        mix_ref[:, A_WIDTH:A_WIDTH + B_WIDTH] = (hl * _sigmoid(hl)).astype(BF16)

    vec = pl.BlockSpec((1, B_WIDTH), lambda i: (0, 0))
    return pl.pallas_call(
        body, name=name, grid=(S // tm,),
        in_specs=[
            pl.BlockSpec((tm, 2048), lambda i: (i, 0)),
            pl.BlockSpec((CONV_HALO, 1024), lambda i: (jnp.maximum(i * hb - 1, 0), 1)),
            pl.BlockSpec((A_GROUPS, GMLP_BLOCK, GMLP_BLOCK), lambda i: (0, 0, 0)),
            pl.BlockSpec((A_GROUPS, GMLP_BLOCK, 1), lambda i: (0, 0, 0)),
            pl.BlockSpec((CONV_HALO, B_WIDTH), lambda i: (0, 0)),
            vec, vec, vec,
        ],
        out_specs=[pl.BlockSpec((tm, 1024), lambda i: (i, 0)), pl.BlockSpec((tm, B_WIDTH), lambda i: (i, 0))],
        out_shape=[jax.ShapeDtypeStruct((S, 1024), BF16), jax.ShapeDtypeStruct((S, B_WIDTH), F32)],
        scratch_shapes=[pltpu.VMEM((tm + CONV_HALO, B_WIDTH), F32),
                        pltpu.VMEM((_SHIFTS, tm + CONV_HALO, B_WIDTH), F32)],
        compiler_params=_params(("parallel",)),
    )(proj, proj, wm, bcol, cw, cb, lg, lb)


def _even_bwd1(proj, dmix, hc, wm, wmt, bcol, lg, lb, *, name):
    S = proj.shape[0]
    tm = _tile(S, 256)
    nblk = tm // GMLP_BLOCK

    def body(p_ref, dm_ref, hc_ref, wm_ref, wmt_ref, b_ref, lg_ref, lb_ref,
             dpa_ref, dhc_ref, dwm_ref, db_ref, dlg_ref, dlb_ref, dcb_ref, dgu_ref, dvn_ref):
        @pl.when(pl.program_id(0) == 0)
        def _():
            dwm_ref[...] = jnp.zeros_like(dwm_ref)
            db_ref[...] = jnp.zeros_like(db_ref)
            dlg_ref[...] = jnp.zeros_like(dlg_ref)
            dlb_ref[...] = jnp.zeros_like(dlb_ref)
            dcb_ref[...] = jnp.zeros_like(dcb_ref)

        au = p_ref[:, 0:A_WIDTH]
        av = p_ref[:, A_WIDTH:2 * A_WIDTH]
        gu, tu = _gelu(au)
        gv, tv = _gelu(av)
        vn, rstd = _ln_stats(gv)
        vnb = vn.astype(BF16)
        for n in range(nblk):
            rows = slice(n * GMLP_BLOCK, (n + 1) * GMLP_BLOCK)
            for g in range(A_GROUPS):
                cols = slice(g * GMLP_BLOCK, (g + 1) * GMLP_BLOCK)
                vb = vnb[rows, cols]
                sg = jnp.dot(wm_ref[g], vb, preferred_element_type=F32) + b_ref[g]
                da = dm_ref[rows, cols]
                dsg = da * gu[rows, cols]
                dgu_ref[rows, cols] = da * sg
                dsgb = dsg.astype(BF16)
                dwm_ref[g] += _dot_nt(dsgb, vb)
                db_ref[g] += jnp.sum(dsg, axis=1, keepdims=True)
                dvn_ref[rows, cols] = jnp.dot(wmt_ref[g], dsgb, preferred_element_type=F32)
        dvn = dvn_ref[...]
        dgv = rstd * (dvn - _mean(dvn) - vn * _mean(dvn * vn))
        dpa_ref[:, 0:A_WIDTH] = (dgu_ref[...] * _gelu_grad(au, tu)).astype(BF16)
        dpa_ref[:, A_WIDTH:2 * A_WIDTH] = (dgv * _gelu_grad(av, tv)).astype(BF16)
        hhat, rstd2 = _ln_stats(hc_ref[...])
        lgv = lg_ref[...]
        hl = hhat * lgv + lb_ref[...]
        s = _sigmoid(hl)
        dhl = dm_ref[:, A_WIDTH:A_WIDTH + B_WIDTH] * (s * (1.0 + hl * (1.0 - s)))
        dlg_ref[...] += jnp.sum(dhl * hhat, axis=0, keepdims=True)
        dlb_ref[...] += jnp.sum(dhl, axis=0, keepdims=True)
        dhh = dhl * lgv
        dhc = rstd2 * (dhh - _mean(dhh) - hhat * _mean(dhh * hhat))
        dcb_ref[...] += jnp.sum(dhc, axis=0, keepdims=True)
        dhc_ref[...] = dhc

    vec = pl.BlockSpec((1, B_WIDTH), lambda i: (0, 0))
    w3 = pl.BlockSpec((A_GROUPS, GMLP_BLOCK, GMLP_BLOCK), lambda i: (0, 0, 0))
    b3 = pl.BlockSpec((A_GROUPS, GMLP_BLOCK, 1), lambda i: (0, 0, 0))
    return pl.pallas_call(
        body, name=name, grid=(S // tm,),
        in_specs=[
            pl.BlockSpec((tm, 1024), lambda i: (i, 0)),
            pl.BlockSpec((tm, 1024), lambda i: (i, 0)),
            pl.BlockSpec((tm, B_WIDTH), lambda i: (i, 0)),
            w3, w3, b3, vec, vec,
        ],
        out_specs=[pl.BlockSpec((tm, 1024), lambda i: (i, 0)), pl.BlockSpec((tm, B_WIDTH), lambda i: (i, 0)),
                   w3, b3, vec, vec, vec],
        out_shape=[
            jax.ShapeDtypeStruct((S, 1024), BF16), jax.ShapeDtypeStruct((S, B_WIDTH), F32),
            jax.ShapeDtypeStruct((A_GROUPS, GMLP_BLOCK, GMLP_BLOCK), F32),
            jax.ShapeDtypeStruct((A_GROUPS, GMLP_BLOCK, 1), F32),
            jax.ShapeDtypeStruct((1, B_WIDTH), F32), jax.ShapeDtypeStruct((1, B_WIDTH), F32),
            jax.ShapeDtypeStruct((1, B_WIDTH), F32),
        ],
        scratch_shapes=[pltpu.VMEM((tm, A_WIDTH), F32), pltpu.VMEM((tm, A_WIDTH), F32)],
        compiler_params=_params(("arbitrary",)),
    )(proj, dmix, hc, wm, wmt, bcol, lg, lb)


def _even_bwd2(proj, dhc, cw, *, name):
    S = proj.shape[0]
    tm = _tile(S, 256)
    hb = tm // CONV_HALO
    nt = S // tm
    last_halo = S // CONV_HALO - 1
    lo = CONV_HALO - CONV_WIDTH + 1

    def body(p_ref, halo_ref, d_ref, dnext_ref, cw_ref, dpb_ref, dcw_ref, hext_ref, dext_ref, hsh_ref, dsh_ref):
        i = pl.program_id(0)

        @pl.when(i == 0)
        def _():
            dcw_ref[...] = jnp.zeros_like(dcw_ref)

        hh = halo_ref[:, 0:B_WIDTH] * _sigmoid(halo_ref[:, B_WIDTH:2 * B_WIDTH])
        hext_ref[0:CONV_HALO, :] = jnp.where(i > 0, hh, 0.0)
        hext_ref[CONV_HALO:CONV_HALO + tm, :] = p_ref[:, 0:B_WIDTH] * _sigmoid(p_ref[:, B_WIDTH:2 * B_WIDTH])
        dext_ref[0:tm, :] = d_ref[...]
        dext_ref[tm:tm + CONV_HALO, :] = jnp.where(i < nt - 1, dnext_ref[...], 0.0)
        _fill_shifts(hsh_ref, hext_ref, tm)
        _fill_shifts(dsh_ref, dext_ref, tm)
        for r0 in range(0, tm, _CONV_ROWS):
            rows = slice(r0, r0 + _CONV_ROWS)
            dhc_b = d_ref[rows, :]
            dh = jnp.zeros((_CONV_ROWS, B_WIDTH), F32)
            for k in range(CONV_WIDTH):
                dh = dh + cw_ref[k:k + 1, :] * _window(dsh_ref, r0 + CONV_WIDTH - 1 - k, _CONV_ROWS)
                dcw_ref[k:k + 1, :] += jnp.sum(dhc_b * _window(hsh_ref, r0 + k + lo, _CONV_ROWS), axis=0,
                                               keepdims=True)
            ba_b = p_ref[rows, 0:B_WIDTH]
            sg_b = _sigmoid(p_ref[rows, B_WIDTH:2 * B_WIDTH])
            dpb_ref[rows, 0:B_WIDTH] = (dh * sg_b).astype(BF16)
            dpb_ref[rows, B_WIDTH:2 * B_WIDTH] = (dh * ba_b * sg_b * (1.0 - sg_b)).astype(BF16)

    return pl.pallas_call(
        body, name=name, grid=(nt,),
        in_specs=[
            pl.BlockSpec((tm, 1024), lambda i: (i, 1)),
            pl.BlockSpec((CONV_HALO, 1024), lambda i: (jnp.maximum(i * hb - 1, 0), 1)),
            pl.BlockSpec((tm, B_WIDTH), lambda i: (i, 0)),
            pl.BlockSpec((CONV_HALO, B_WIDTH), lambda i: (jnp.minimum((i + 1) * hb, last_halo), 0)),
            pl.BlockSpec((CONV_HALO, B_WIDTH), lambda i: (0, 0)),
        ],
        out_specs=[pl.BlockSpec((tm, 1024), lambda i: (i, 0)), pl.BlockSpec((CONV_HALO, B_WIDTH), lambda i: (0, 0))],
        out_shape=[jax.ShapeDtypeStruct((S, 1024), BF16), jax.ShapeDtypeStruct((CONV_HALO, B_WIDTH), F32)],
        scratch_shapes=[pltpu.VMEM((tm + CONV_HALO, B_WIDTH), F32), pltpu.VMEM((tm + CONV_HALO, B_WIDTH), F32),
                        pltpu.VMEM((_SHIFTS, tm + CONV_HALO, B_WIDTH), F32),
                        pltpu.VMEM((_SHIFTS, tm + CONV_HALO, B_WIDTH), F32)],
        compiler_params=_params(("arbitrary",)),
    )(proj, proj, dhc, dhc, cw)


_CA_SCALE = CA_HEAD_DIM ** -0.5


def _softmax_rows(s):
    e = jnp.exp(s - jnp.max(s, axis=-1, keepdims=True))
    return e / jnp.sum(e, axis=-1, keepdims=True)


def _attn_fwd(q, k, v, *, name):
    S = q.shape[0]

    def body(r, f, o, acc, s):
        for h in range(CA_HEADS):
            cols = _cols(h, CA_HEAD_DIM)
            p = _softmax_rows(_dot_nt(r[0][:, cols], f[0][:, cols]) * _CA_SCALE)
            o[0][:, cols] = _dot(p.astype(BF16), f[1][:, cols]).astype(BF16)

    (o_,), _ = _rows_call(name, _tile(S, 512), [q], [k, v], [(D_MODEL, BF16)], [], body)
    return o_


def _attn_bwd(dy, wo, q, k, v, *, name):
    S = q.shape[0]
    M = k.shape[0]

    def body(r, f, o, acc, s):
        dyb = r[0][...].astype(BF16)
        for h in range(CA_HEADS):
            cols = _cols(h, CA_HEAD_DIM)
            qh = r[1][:, cols]
            kh = f[0][:, cols]
            vh = f[1][:, cols]
            doh = _dot_nt(dyb, f[2 + h][...]).astype(BF16)
            p = _softmax_rows(_dot_nt(qh, kh) * _CA_SCALE)
            acc[1][:, cols] += _dot_tn(p.astype(BF16), doh)
            dp = _dot_nt(doh, vh)
            ds = (p * (dp - jnp.sum(dp * p, axis=-1, keepdims=True)) * _CA_SCALE).astype(BF16)
            o[0][:, cols] = _dot(ds, kh).astype(BF16)
            acc[0][:, cols] += _dot_tn(ds, qh)

    (dq,), (dk, dv) = _rows_call(name, _tile(S, 512), [dy, q], [k, v] + wo, [(D_MODEL, BF16)],
                                 [((M, D_MODEL), F32), ((M, D_MODEL), F32)], body)
    return dq, dk, dv


_STATE_TILE = 2 * STATE_ROWS
N_SETS = 4
SET_CH = C_WIDTH // N_SETS
SET_COLS = N_STATE // N_SETS // STATE_LANES


def _set_groups(j):
    return [SET_COLS * j + c for c in range(SET_COLS)] + [STATE_ROWS + SET_COLS * j + c for c in range(SET_COLS)]


def _state_set(ref, tm, j):
    return jnp.concatenate([ref[:, c, :, :].reshape(tm, STATE_LANES).astype(BF16) for c in _set_groups(j)], axis=1)


def _put_state_set(ref, y, tm, j):
    for k, c in enumerate(_set_groups(j)):
        ref[:, c, :, :] = y[:, _cols(k, STATE_LANES)].reshape(tm // 8, 8, STATE_LANES)


def _mm_to_state(a, wset, *, nt=False, name, tm=256):
    S = a.shape[0]
    tm = _tile(S, tm)

    def body(r, f, o, acc, s):
        for j in range(N_SETS):
            av = r[0][:, _cols(j, SET_CH)].astype(BF16)
            _put_state_set(o[0], _dot_nt(av, f[0][j]) if nt else _dot(av, f[0][j]), tm, j)

    (y,), _ = _rows_call(name, tm, [a], [wset], [("state", F32)], [], body)
    return y


def _s5_readout(xs, cset, u, d, *, name, tm=256):
    tm = _tile(u.shape[0], tm)

    def body(r, f, o, acc, s):
        y0 = jnp.concatenate([_dot(_state_set(r[0], tm, j), f[0][j]) for j in range(N_SETS)], axis=1)
        y = y0 + f[1][...] * r[1][...]
        o[0][...] = y
        o[1][...] = _gelu(y)[0].astype(BF16)

    (y, yg), _ = _rows_call(name, tm, [xs, u], [cset, d], [(C_WIDTH, F32), (C_WIDTH, BF16)], [], body)
    return y, yg


def _state_grad_sets(a, st, *, name, ts=256):
    ts = _tile(a.shape[0], ts)

    def body(r, f, o, acc, s):
        for j in range(N_SETS):
            acc[0][j] += _dot_tn(r[0][:, _cols(j, SET_CH)].astype(BF16), _state_set(r[1], ts, j))

    _, (out,) = _rows_call(name, ts, [a, st], [], [], [((N_SETS, SET_CH, 2 * N_STATE // N_SETS), F32)], body)
    return out


def _glu_out(yg, ws, x, *, name, tm=512):
    n = ws[0][1][2]

    def body(r, f, o, acc, s):
        ygv = r[0][...]
        ov = [_dot(ygv, f[p][...]) for p in range(N_CHIPS)]
        for p in range(N_CHIPS):
            o[0][:, _cols(p, n)] = ov[p].astype(BF16)
        for p in range(2):
            o[1][:, _cols(p, n)] = r[1][:, _cols(p, n)] + ov[p] * _sigmoid(ov[2 + p])

    (o_, y), _ = _rows_call(name, _tile(x.shape[0], tm), [yg, x], ws, [(2 * D_MODEL, BF16), (D_MODEL, F32)], [], body)
    return o_, y


def _glu_out_bwd(o_, dy, ws, y, u, d, *, name, tm=256):
    n = ws[0][1][2]

    def body(r, f, o, acc, s):
        o1 = r[0][:, 0:D_MODEL].astype(F32)
        sg = _sigmoid(r[0][:, D_MODEL:2 * D_MODEL].astype(F32))
        dyv = r[1][...]
        do1 = (dyv * sg).astype(BF16)
        do2 = (dyv * o1 * sg * (1.0 - sg)).astype(BF16)
        o[0][:, 0:D_MODEL] = do1
        o[0][:, D_MODEL:2 * D_MODEL] = do2
        dyg = None
        for p in range(N_CHIPS):
            t = _dot_nt((do1 if p < 2 else do2)[:, _cols(p % 2, n)], f[1 + p][...])
            dyg = t if dyg is None else dyg + t
        yv = r[2][...]
        dys = dyg * _gelu_grad(yv, _gelu(yv)[1])
        o[1][...] = dys.astype(BF16)
        o[2][...] = f[0][...] * dys
        acc[0][...] += jnp.sum(dys * r[3][...], axis=0, keepdims=True)

    (do, dys, dus), (dd,) = _rows_call(name, _tile(dy.shape[0], tm), [o_, dy, y, u], [d] + ws,
                                       [(2 * D_MODEL, BF16), (C_WIDTH, BF16), (C_WIDTH, F32)], [((1, C_WIDTH), F32)],
                                       body)
    return do, dys, dus, dd


def _s5_in_bwd(gs, bset, dus, ws, x, g, dres, *, name, tm=256):
    D = x.shape[1]
    tm = _tile(x.shape[0], tm)

    def body(r, f, o, acc, s):
        du0 = jnp.concatenate([_dot_nt(_state_set(r[0], tm, j), f[1][j]) for j in range(N_SETS)], axis=1)
        du = (du0 + r[1][...]).astype(BF16)
        o[0][...] = du
        dx, dg = _rms_bwd_tile(r[2][...], f[0][...], _cat_nt(du, f[2:]))
        o[1][...] = dx + r[3][...]
        acc[0][...] += dg

    (du, dx), (dg,) = _rows_call(name, tm, [gs, dus, x, dres], [_vec(g), bset] + ws,
                                 [(C_WIDTH, BF16), (D, F32)], [((1, D), F32)], body)
    return du, dx, dg


_SCAN_CHUNK = 128
_RE = slice(0, STATE_ROWS)
_IM = slice(STATE_ROWS, 2 * STATE_ROWS)
assert SCAN_BLOCK == 8


def _token(g, i):
    return pl.ds(pl.multiple_of(g * (_STATE_TILE * SCAN_BLOCK), _STATE_TILE * SCAN_BLOCK) + i, _STATE_TILE,
                 stride=SCAN_BLOCK)


def _scan_fwd(bu, pw, *, name):
    S = bu.shape[0] // _STATE_TILE
    tc = _tile(S, _SCAN_CHUNK, 8)

    def body(bu_ref, pw_ref, xs_ref, st_ref):
        @pl.when(pl.program_id(0) == 0)
        def _():
            st_ref[...] = jnp.zeros_like(st_ref)

        ar = pw_ref[0, _RE, :]
        ai = pw_ref[0, _IM, :]

        def block(g, carry):
            xr, xi = carry
            cr = ci = nr = ni = None
            for j in range(SCAN_BLOCK):
                b = bu_ref[_token(g, j), :]
                br, bi = b[_RE], b[_IM]
                cr, ci = (br, bi) if j == 0 else (ar * cr - ai * ci + br, ar * ci + ai * cr + bi)
                pr, pi = pw_ref[j, _RE, :], pw_ref[j, _IM, :]
                nr = pr * xr - pi * xi + cr
                ni = pr * xi + pi * xr + ci
                xs_ref[_token(g, j), :] = jnp.concatenate([nr, ni], axis=0)
            return nr, ni

        xr, xi = lax.fori_loop(0, tc // SCAN_BLOCK, block, (st_ref[_RE, :], st_ref[_IM, :]), unroll=2)
        st_ref[_RE, :] = xr
        st_ref[_IM, :] = xi

    blk = pl.BlockSpec((tc * _STATE_TILE, STATE_LANES), lambda i: (i, 0))
    return pl.pallas_call(
        body, name=name, grid=(S // tc,),
        in_specs=[blk, pl.BlockSpec(pw.shape, lambda i: (0, 0, 0))], out_specs=blk,
        out_shape=jax.ShapeDtypeStruct(bu.shape, F32),
        scratch_shapes=[pltpu.VMEM((2 * STATE_ROWS, STATE_LANES), F32)],
        compiler_params=_params(("arbitrary",)),
    )(bu, pw)


def _scan_bwd(dxs, xs, pw, *, name):
    S = dxs.shape[0] // _STATE_TILE
    tc = _tile(S, _SCAN_CHUNK, 8)
    nc = S // tc

    def body(dx_ref, xs_ref, pw_ref, g_ref, da_ref, st_ref):
        @pl.when(pl.program_id(0) == 0)
        def _():
            st_ref[...] = jnp.zeros_like(st_ref)
            da_ref[...] = jnp.zeros_like(da_ref)

        ar = pw_ref[0, _RE, :]
        ai = pw_ref[0, _IM, :]

        def block(i, carry):
            gr, gi, dar, dai = carry
            g = tc // SCAN_BLOCK - 1 - i
            cr = ci = None
            pgr, pgi = gr, gi
            for j in range(SCAN_BLOCK):
                tok = _token(g, SCAN_BLOCK - 1 - j)
                x = xs_ref[tok, :]
                xr, xi = x[_RE], x[_IM]
                dar = dar + pgr * xr + pgi * xi
                dai = dai + pgi * xr - pgr * xi
                d = dx_ref[tok, :]
                dr, di = d[_RE], d[_IM]
                cr, ci = (dr, di) if j == 0 else (ar * cr + ai * ci + dr, ar * ci - ai * cr + di)
                pr, pi = pw_ref[j, _RE, :], pw_ref[j, _IM, :]
                pgr = pr * gr + pi * gi + cr
                pgi = pr * gi - pi * gr + ci
                g_ref[tok, :] = jnp.concatenate([pgr, pgi], axis=0)
            return pgr, pgi, dar, dai

        init = (st_ref[_RE, :], st_ref[_IM, :], da_ref[_RE, :], da_ref[_IM, :])
        gr, gi, dar, dai = lax.fori_loop(0, tc // SCAN_BLOCK, block, init, unroll=2)
        st_ref[_RE, :] = gr
        st_ref[_IM, :] = gi
        da_ref[_RE, :] = dar
        da_ref[_IM, :] = dai

    blk = pl.BlockSpec((tc * _STATE_TILE, STATE_LANES), lambda i: (nc - 1 - i, 0))
    vec = pl.BlockSpec((2 * STATE_ROWS, STATE_LANES), lambda i: (0, 0))
    return pl.pallas_call(
        body, name=name, grid=(nc,), in_specs=[blk, blk, pl.BlockSpec(pw.shape, lambda i: (0, 0, 0))],
        out_specs=[blk, vec],
        out_shape=[jax.ShapeDtypeStruct(dxs.shape, F32), jax.ShapeDtypeStruct((2 * STATE_ROWS, STATE_LANES), F32)],
        scratch_shapes=[pltpu.VMEM((2 * STATE_ROWS, STATE_LANES), F32)],
        compiler_params=_params(("arbitrary",)),
    )(dxs, xs, pw)


def _loss_head(x, g, target, *, name):
    S, D = x.shape

    def body(r, f, o, acc, s):
        xv = r[0][...]
        gv = f[0][...]
        rs = lax.rsqrt(_mean(xv * xv) + EPS)
        xh = xv * rs
        err = xh * gv - r[1][...]
        acc[1][...] += 0.5 * jnp.sum(_mean(err * err), axis=0, keepdims=True)
        dy = err * (1.0 / D)
        dyg = dy * gv
        o[0][...] = rs * (dyg - xh * _mean(dyg * xh))
        acc[0][...] += jnp.sum(dy * xh, axis=0, keepdims=True)

    (dx,), (dg, loss) = _rows_call(name, _tile(S, 256, 8), [x, target], [_vec(g)], [(D, F32)],
                                   [((1, D), F32), ((1, 128), F32)], body)
    return dx, dg, loss


_ADAM_C1 = 1.0 - ADAM_B1 ** ADAM_STEP
_ADAM_C2 = 1.0 - ADAM_B2 ** ADAM_STEP
_ONE_BLOCK_BYTES = 8 * 1024 * 1024


def _adamw_math(w, g, m, v):
    nm = ADAM_B1 * m + (1.0 - ADAM_B1) * g
    nv = ADAM_B2 * v + (1.0 - ADAM_B2) * (g * g)
    m_hat = nm / _ADAM_C1
    v_hat = nv / _ADAM_C2
    return -ADAM_LR * (m_hat / (jnp.sqrt(v_hat) + ADAM_EPS) + ADAM_WD * w), nm, nv


def _adamw_shard(w, gsrc, m, v, *, name):
    R, C = w.shape
    n_l = len(gsrc)
    rows = R // n_l
    tr = rows
    for _, r0 in gsrc:
        tr = math.gcd(tr, r0) if r0 else tr
    tr = _tile(tr, 256, 8) if tr > 256 else tr
    nb = rows // tr
    assert rows % tr == 0 and all(r0 % tr == 0 for _, r0 in gsrc)

    def body(*refs):
        w_ref, g_refs, (m_ref, v_ref, go_ref, d_ref, nm_ref, nv_ref) = refs[0], refs[1:1 + n_l], refs[1 + n_l:]
        layer = pl.program_id(0) // nb
        gv = g_refs[0][...]
        for l in range(1, n_l):
            gv = jnp.where(layer == l, g_refs[l][...], gv)
        go_ref[...] = gv
        d_ref[...], nm_ref[...], nv_ref[...] = _adamw_math(w_ref[...], gv, m_ref[...], v_ref[...])

    def g_spec(l, r0):
        return pl.BlockSpec((tr, C), lambda i: (r0 // tr + jnp.clip(i - l * nb, 0, nb - 1), 0))

    blk = pl.BlockSpec((tr, C), lambda i: (i, 0))
    out = jax.ShapeDtypeStruct((R, C), F32)
    return pl.pallas_call(
        body, name=name, grid=(R // tr,),
        in_specs=[blk] + [g_spec(l, r0) for l, (_, r0) in enumerate(gsrc)] + [blk, blk], out_specs=[blk] * 4,
        out_shape=[out] * 4, compiler_params=_params(("parallel",)),
    )(w, *[g for g, _ in gsrc], m, v)


def _adamw_small(ws, gs, ms, vs, *, name):
    n = len(ws)

    def body(*refs):
        w_r, g_r, m_r, v_r = refs[:n], refs[n:2 * n], refs[2 * n:3 * n], refs[3 * n:4 * n]
        d_r, nm_r, nv_r = refs[4 * n:5 * n], refs[5 * n:6 * n], refs[6 * n:7 * n]
        for k in range(n):
            d_r[k][...], nm_r[k][...], nv_r[k][...] = _adamw_math(w_r[k][...], g_r[k][...], m_r[k][...], v_r[k][...])

    vm = pl.BlockSpec(memory_space=pltpu.VMEM)
    out = [jax.ShapeDtypeStruct(w.shape, F32) for w in ws]
    res = pl.pallas_call(body, name=name, in_specs=[vm] * (4 * n), out_specs=[vm] * (3 * n), out_shape=out * 3,
                         compiler_params=pltpu.CompilerParams(vmem_limit_bytes=VMEM_LIMIT))(*ws, *gs, *ms, *vs)
    return res[:n], res[n:2 * n], res[2 * n:]


def _sum_slots(x, *, name):
    n, R, C = x.shape
    tr = R if (n + 1) * R * C * 4 <= _ONE_BLOCK_BYTES else _tile(R, 256, 8)

    def body(x_ref, o_ref):
        acc = x_ref[0]
        for k in range(1, n):
            acc = acc + x_ref[k]
        o_ref[...] = acc

    return pl.pallas_call(
        body, name=name, grid=(R // tr,),
        in_specs=[pl.BlockSpec((n, tr, C), lambda i: (0, i, 0))], out_specs=pl.BlockSpec((tr, C), lambda i: (i, 0)),
        out_shape=jax.ShapeDtypeStruct((R, C), F32), compiler_params=_params(("parallel",)),
    )(x)


def _pair_sum(g, r, half, *, name):
    n, R, C = g.shape
    Rh = R // 2
    tr = _tile(Rh, 256, 8)
    nb = Rh // tr

    def body(half_ref, g_ref, r_ref, o_ref):
        o_ref[...] = (g_ref[...] + r_ref[...]).astype(BF16)

    return pl.pallas_call(
        body, name=name,
        grid_spec=pltpu.PrefetchScalarGridSpec(
            num_scalar_prefetch=1, grid=(n, nb),
            in_specs=[pl.BlockSpec((1, tr, C), lambda p, i, h: (p, h[0] * nb + i, 0)),
                      pl.BlockSpec((1, tr, C), lambda p, i, h: (p, i, 0))],
            out_specs=pl.BlockSpec((1, tr, C), lambda p, i, h: (p, i, 0)),
        ),
        out_shape=jax.ShapeDtypeStruct((n, Rh, C), BF16), compiler_params=_params(("parallel", "parallel")),
    )(half, g, r)


def _chip_sum(g, r, slots, where, *, name):
    n, R, C = g.shape
    Rh = R // 2
    tr = _tile(Rh, 256, 8)
    nb = Rh // tr

    def body(w_ref, g_ref, r_ref, s_ref, o_ref):
        acc = g_ref[0] + r_ref[0]
        for k in range(slots.shape[0]):
            acc = acc + s_ref[k].astype(F32)
        o_ref[...] = acc

    return pl.pallas_call(
        body, name=name,
        grid_spec=pltpu.PrefetchScalarGridSpec(
            num_scalar_prefetch=1, grid=(nb,),
            in_specs=[pl.BlockSpec((1, tr, C), lambda i, w: (w[0], w[1] * nb + i, 0)),
                      pl.BlockSpec((1, tr, C), lambda i, w: (w[0], i, 0)),
                      pl.BlockSpec((slots.shape[0], tr, C), lambda i, w: (0, i, 0))],
            out_specs=pl.BlockSpec((tr, C), lambda i, w: (w[1] * nb + i, 0)),
        ),
        out_shape=jax.ShapeDtypeStruct((R, C), F32), compiler_params=_params(("parallel",)),
    )(where, g, r, slots)


ANY = pl.BlockSpec(memory_space=pl.ANY)


def _place():
    return lax.axis_index("x"), lax.axis_index("y"), lax.axis_index("c")


def _other_chips(x, y):
    return [(1 - x, y), (x, 1 - y), (1 - x, 1 - y)]


def _allgather_small(v, *, name):
    R, C = v.shape

    def body(x_ref, out_ref, send_sems, recv_sems, local_sem):
        x, y, c = _place()
        me, sibling = (x, y, c), (x, y, 1 - c)
        chips = _other_chips(x, y)

        def rows(px, py, pc):
            return out_ref.at[pl.ds((4 * px + 2 * py + pc) * R, R), :]

        def copy(k, block, to, src=None):
            return pltpu.make_async_remote_copy(
                src_ref=rows(*block) if src is None else src, dst_ref=rows(*block),
                send_sem=send_sems.at[k], recv_sem=recv_sems.at[k], device_id=to, device_id_type=MESH)

        mine = pltpu.make_async_copy(x_ref, rows(*me), local_sem)
        mine.start()
        first = [copy(0, me, sibling, src=x_ref)]
        first += [copy(1 + j, me, (*chip, c), src=x_ref) for j, chip in enumerate(chips)]
        for cp in first:
            cp.start()
        passed = [copy(4 + j, (*chip, c), sibling) for j, chip in enumerate(chips)]
        for j, chip in enumerate(chips):
            copy(1 + j, (*chip, c), me).wait_recv()
            passed[j].start()
        copy(0, sibling, me).wait_recv()
        for j, chip in enumerate(chips):
            copy(4 + j, (*chip, 1 - c), me).wait_recv()
        for cp in first + passed:
            cp.wait_send()
        mine.wait()

    return pl.pallas_call(
        body, name=name, out_shape=jax.ShapeDtypeStruct((N_DEV * R, C), v.dtype),
        in_specs=[pl.BlockSpec(memory_space=pltpu.VMEM)], out_specs=pl.BlockSpec(memory_space=pltpu.VMEM),
        scratch_shapes=[pltpu.SemaphoreType.DMA((7,)), pltpu.SemaphoreType.DMA((7,)), pltpu.SemaphoreType.DMA],
        compiler_params=pltpu.CompilerParams(vmem_limit_bytes=VMEM_LIMIT),
    )(v)


def _aliased_comm_call(body, bufs, n_sems, *, name):
    n = len(bufs)
    return pl.pallas_call(
        body, name=name, out_shape=[jax.ShapeDtypeStruct(b.shape, b.dtype) for b in bufs],
        in_specs=[ANY] * n, out_specs=[ANY] * n, input_output_aliases={k: k for k in range(n)},
        scratch_shapes=[pltpu.SemaphoreType.DMA((n_sems,)), pltpu.SemaphoreType.DMA((n_sems,))],
    )(*bufs)


HBM = pl.BlockSpec(memory_space=pltpu.HBM)
SEM = pl.BlockSpec(memory_space=pltpu.SEMAPHORE)
_SPLIT = pltpu.CompilerParams(has_side_effects=pltpu.SideEffectType.DATAFLOW_SIDE_EFFECTING)


def _in_hbm(arrs):
    return [pltpu.with_memory_space_constraint(a, pltpu.HBM) for a in arrs]


def _gather_ici_start(bufs, after, *, name):
    n = len(bufs)

    def body(*refs):
        send_sems, recv_sems, outs, token = refs[n + 1], refs[n + 2], refs[n + 3:2 * n + 3], refs[2 * n + 3]
        x, y, c = _place()
        for b in range(n):
            rh = bufs[b].shape[1] // 2
            part = outs[b].at[2 * x + y, pl.ds(c * rh, rh), :]
            for j, chip in enumerate(_other_chips(x, y)):
                pltpu.make_async_remote_copy(src_ref=part, dst_ref=part, send_sem=send_sems.at[3 * b + j],
                                             recv_sem=recv_sems.at[3 * b + j], device_id=(*chip, c),
                                             device_id_type=MESH).start()
        token[...] = jnp.zeros_like(token)

    res = pl.pallas_call(
        body, name=name,
        out_shape=(pltpu.SemaphoreType.DMA((3 * n,)), pltpu.SemaphoreType.DMA((3 * n,)),
                   *[pltpu.HBM(b.shape, b.dtype) for b in bufs], jax.ShapeDtypeStruct((8, 128), F32)),
        in_specs=[HBM] * n + [ANY], out_specs=(SEM, SEM, *[HBM] * n, pl.BlockSpec(memory_space=pltpu.VMEM)),
        input_output_aliases={k: k + 2 for k in range(n)}, compiler_params=_SPLIT,
    )(*_in_hbm(bufs), after)
    return res[0], res[1], list(res[2:2 + n]), res[2 + n]


def _gather_ici_wait(send_sems, recv_sems, bufs, after, *, name):
    n = len(bufs)

    def body(*refs):
        ins, ss, rs = refs[:n], refs[n], refs[n + 1]
        x, y, c = _place()
        for b in range(n):
            rh = bufs[b].shape[1] // 2
            mine = ins[b].at[2 * x + y, pl.ds(c * rh, rh), :]
            for j, (cx, cy) in enumerate(_other_chips(x, y)):
                theirs = ins[b].at[2 * cx + cy, pl.ds(c * rh, rh), :]
                cp = pltpu.make_async_remote_copy(src_ref=mine, dst_ref=theirs, send_sem=ss.at[3 * b + j],
                                                  recv_sem=rs.at[3 * b + j], device_id=(cx, cy, c),
                                                  device_id_type=MESH)
                cp.wait_send()
                cp.wait_recv()

    return list(pl.pallas_call(
        body, name=name, out_shape=[pltpu.HBM(b.shape, b.dtype) for b in bufs],
        in_specs=[HBM] * n + [SEM, SEM, ANY], out_specs=[HBM] * n,
        input_output_aliases={k: k for k in range(n)}, compiler_params=_SPLIT,
    )(*bufs, send_sems, recv_sems, after))


def _gather_forward(bufs, *, name):
    n = len(bufs)

    def body(*refs):
        outs, send_sems, recv_sems = refs[n:2 * n], refs[2 * n], refs[2 * n + 1]
        x, y, c = _place()

        def copy(b, j, chip, hc):
            rh = bufs[b].shape[1] // 2
            part = outs[b].at[2 * chip[0] + chip[1], pl.ds(hc * rh, rh), :]
            return pltpu.make_async_remote_copy(src_ref=part, dst_ref=part, send_sem=send_sems.at[3 * b + j],
                                                recv_sem=recv_sems.at[3 * b + j], device_id=(x, y, 1 - c),
                                                device_id_type=MESH)

        sends = [copy(b, j, chip, c) for b in range(n) for j, chip in enumerate(_other_chips(x, y))]
        for cp in sends:
            cp.start()
        for b in range(n):
            for j, chip in enumerate(_other_chips(x, y)):
                copy(b, j, chip, 1 - c).wait_recv()
        for cp in sends:
            cp.wait_send()

    return _aliased_comm_call(body, bufs, 3 * n, name=name)


def _chip_exchange_start(hs, *, name):
    n = len(hs)
    lands = [lax.empty((3,) + h.shape[1:], h.dtype) for h in hs]

    def body(*refs):
        send_sems, recv_sems = refs[2 * n], refs[2 * n + 1]
        h_out, l_out, token = refs[2 * n + 2:3 * n + 2], refs[3 * n + 2:4 * n + 2], refs[4 * n + 2]
        x, y, c = _place()
        for b in range(n):
            for j, (cx, cy) in enumerate(_other_chips(x, y)):
                pltpu.make_async_remote_copy(src_ref=h_out[b].at[2 * cx + cy], dst_ref=l_out[b].at[j],
                                             send_sem=send_sems.at[3 * b + j], recv_sem=recv_sems.at[3 * b + j],
                                             device_id=(cx, cy, c), device_id_type=MESH).start()
        token[...] = jnp.zeros_like(token)

    res = pl.pallas_call(
        body, name=name,
        out_shape=(pltpu.SemaphoreType.DMA((3 * n,)), pltpu.SemaphoreType.DMA((3 * n,)),
                   *[pltpu.HBM(a.shape, a.dtype) for a in hs + lands], jax.ShapeDtypeStruct((8, 128), F32)),
        in_specs=[HBM] * (2 * n), out_specs=(SEM, SEM, *[HBM] * (2 * n), pl.BlockSpec(memory_space=pltpu.VMEM)),
        input_output_aliases={k: k + 2 for k in range(2 * n)}, compiler_params=_SPLIT,
    )(*_in_hbm(hs + lands))
    return res[0], res[1], list(res[2:2 + n]), list(res[2 + n:2 + 2 * n]), res[2 + 2 * n]


def _chip_exchange_wait(send_sems, recv_sems, hs, lands, after, *, name):
    n = len(hs)

    def body(*refs):
        h_in, l_in, ss, rs = refs[:n], refs[n:2 * n], refs[2 * n], refs[2 * n + 1]
        x, y, c = _place()
        for b in range(n):
            for j, (cx, cy) in enumerate(_other_chips(x, y)):
                cp = pltpu.make_async_remote_copy(src_ref=h_in[b].at[2 * cx + cy], dst_ref=l_in[b].at[j],
                                                  send_sem=ss.at[3 * b + j], recv_sem=rs.at[3 * b + j],
                                                  device_id=(cx, cy, c), device_id_type=MESH)
                cp.wait_send()
                cp.wait_recv()

    res = pl.pallas_call(
        body, name=name, out_shape=[pltpu.HBM(a.shape, a.dtype) for a in hs + lands],
        in_specs=[HBM] * (2 * n) + [SEM, SEM, ANY], out_specs=[HBM] * (2 * n),
        input_output_aliases={k: k for k in range(2 * n)}, compiler_params=_SPLIT,
    )(*hs, *lands, send_sems, recv_sems, after)
    return list(res[n:])


def _peers(x, y, c):
    return [((1 - x) if fx else x, (1 - y) if fy else y, (1 - c) if fc else c)
            for fx in (0, 1) for fy in (0, 1) for fc in (0, 1) if fx or fy or fc]


def _all_to_all_start(slab, after, *, name):
    land = lax.empty((N_DEV,) + slab.shape, slab.dtype)

    def body(slab_in, land_in, after_ref, send_sems, recv_sems, slab_out, land_out, token):
        x, y, c = _place()
        for k, peer in enumerate(_peers(x, y, c)):
            pltpu.make_async_remote_copy(src_ref=slab_out, dst_ref=land_out.at[4 * x + 2 * y + c],
                                         send_sem=send_sems.at[k], recv_sem=recv_sems.at[k], device_id=peer,
                                         device_id_type=MESH).start()
        token[...] = jnp.zeros_like(token)

    return pl.pallas_call(
        body, name=name,
        out_shape=(pltpu.SemaphoreType.DMA((N_DEV - 1,)), pltpu.SemaphoreType.DMA((N_DEV - 1,)),
                   pltpu.HBM(slab.shape, slab.dtype), pltpu.HBM(land.shape, land.dtype),
                   jax.ShapeDtypeStruct((8, 128), F32)),
        in_specs=[HBM, HBM, ANY], out_specs=(SEM, SEM, HBM, HBM, pl.BlockSpec(memory_space=pltpu.VMEM)),
        input_output_aliases={0: 2, 1: 3}, compiler_params=_SPLIT,
    )(*_in_hbm([slab, land]), after)


def _all_to_all_wait(send_sems, recv_sems, slab, land, after, *, name):
    def body(slab_in, land_in, ss, rs, after_ref, slab_out, land_out):
        x, y, c = _place()
        for k, (px, py, pc) in enumerate(_peers(x, y, c)):
            cp = pltpu.make_async_remote_copy(src_ref=slab_in, dst_ref=land_in.at[4 * px + 2 * py + pc],
                                              send_sem=ss.at[k], recv_sem=rs.at[k], device_id=(px, py, pc),
                                              device_id_type=MESH)
            cp.wait_send()
            cp.wait_recv()

    return pl.pallas_call(
        body, name=name, out_shape=[pltpu.HBM(slab.shape, slab.dtype), pltpu.HBM(land.shape, land.dtype)],
        in_specs=[HBM, HBM, SEM, SEM, ANY], out_specs=[HBM, HBM], input_output_aliases={0: 0, 1: 1},
        compiler_params=_SPLIT,
    )(slab, land, send_sems, recv_sems, after)


def _pair_exchange_start(gs, *, name):
    n = len(gs)
    lands = [lax.empty((g.shape[0], g.shape[1] // 2, g.shape[2]), g.dtype) for g in gs]

    def body(*refs):
        send_sems, recv_sems = refs[2 * n], refs[2 * n + 1]
        g_out, l_out, token = refs[2 * n + 2:3 * n + 2], refs[3 * n + 2:4 * n + 2], refs[4 * n + 2]
        x, y, c = _place()
        for b in range(n):
            rh = gs[b].shape[1] // 2
            pltpu.make_async_remote_copy(src_ref=g_out[b].at[:, pl.ds((1 - c) * rh, rh), :], dst_ref=l_out[b],
                                         send_sem=send_sems.at[b], recv_sem=recv_sems.at[b],
                                         device_id=(x, y, 1 - c), device_id_type=MESH).start()
        token[...] = jnp.zeros_like(token)

    res = pl.pallas_call(
        body, name=name,
        out_shape=(pltpu.SemaphoreType.DMA((n,)), pltpu.SemaphoreType.DMA((n,)),
                   *[pltpu.HBM(a.shape, a.dtype) for a in gs + lands], jax.ShapeDtypeStruct((8, 128), F32)),
        in_specs=[HBM] * (2 * n), out_specs=(SEM, SEM, *[HBM] * (2 * n), pl.BlockSpec(memory_space=pltpu.VMEM)),
        input_output_aliases={k: k + 2 for k in range(2 * n)}, compiler_params=_SPLIT,
    )(*_in_hbm(gs + lands))
    return res[0], res[1], list(res[2:2 + n]), list(res[2 + n:2 + 2 * n]), res[2 + 2 * n]


def _pair_exchange_wait(send_sems, recv_sems, gs, lands, after, *, name):
    n = len(gs)

    def body(*refs):
        g_in, l_in, ss, rs = refs[:n], refs[n:2 * n], refs[2 * n], refs[2 * n + 1]
        x, y, c = _place()
        for b in range(n):
            rh = gs[b].shape[1] // 2
            cp = pltpu.make_async_remote_copy(src_ref=g_in[b].at[:, pl.ds((1 - c) * rh, rh), :], dst_ref=l_in[b],
                                              send_sem=ss.at[b], recv_sem=rs.at[b], device_id=(x, y, 1 - c),
                                              device_id_type=MESH)
            cp.wait_send()
            cp.wait_recv()

    res = pl.pallas_call(
        body, name=name, out_shape=[pltpu.HBM(a.shape, a.dtype) for a in gs + lands],
        in_specs=[HBM] * (2 * n) + [SEM, SEM, ANY], out_specs=[HBM] * (2 * n),
        input_output_aliases={k: k for k in range(2 * n)}, compiler_params=_SPLIT,
    )(*gs, *lands, send_sems, recv_sems, after)
    return list(res[:n]), list(res[n:])


def _pair_share(ss, *, name):
    n = len(ss)

    def body(*refs):
        outs, send_sems, recv_sems = refs[n:2 * n], refs[2 * n], refs[2 * n + 1]
        x, y, c = _place()
        cps = []
        for b in range(n):
            rh = ss[b].shape[0] // 2
            mine = outs[b].at[pl.ds(c * rh, rh), :]
            cps.append(pltpu.make_async_remote_copy(src_ref=mine, dst_ref=mine, send_sem=send_sems.at[b],
                                                    recv_sem=recv_sems.at[b], device_id=(x, y, 1 - c),
                                                    device_id_type=MESH))
        for cp in cps:
            cp.start()
        for b, cp in enumerate(cps):
            rh = ss[b].shape[0] // 2
            theirs = outs[b].at[pl.ds((1 - c) * rh, rh), :]
            pltpu.make_async_remote_copy(src_ref=theirs, dst_ref=theirs, send_sem=send_sems.at[b],
                                         recv_sem=recv_sems.at[b], device_id=(x, y, 1 - c),
                                         device_id_type=MESH).wait_recv()
            cp.wait_send()

    return _aliased_comm_call(body, ss, n, name=name)


_SMALL_SHARDED = (("e_conv_w", 2), ("o_norm", 1), ("o_d", 1))
_REPLICATED = ("e_norm", "e_gmlp_w", "e_gmlp_b", "e_conv_b", "e_conv_ln_g", "e_conv_ln_b", "o_lam_re", "o_lam_im",
               "o_log_dt", "o_b_re", "o_b_im", "o_c_re", "o_c_im", "ca_norm", "ca_mem_norm", "ffn_norm", "final_norm")
_SMALL = tuple(n for n, _ in _SMALL_SHARDED) + _REPLICATED
_WEIGHTS = ("e_norm", "e_w_in", "e_gmlp_w", "e_gmlp_b", "e_conv_w", "e_conv_b", "e_conv_ln_g", "e_conv_ln_b",
            "e_w_out", "o_norm", "o_w_in", "o_lam_re", "o_lam_im", "o_log_dt", "o_b_re", "o_b_im", "o_c_re", "o_c_im",
            "o_d", "o_w_out", "ca_norm", "ca_mem_norm", "ca_wq", "ca_wk", "ca_wv", "ca_wo", "ffn_norm", "ffn_w_gate",
            "ffn_w_up", "ffn_w_down", "final_norm")


def _pack_rows(arrs, width, dtype, row_mult=8):
    parts, spans, r0 = [], [], 0
    for a in arrs:
        flat = a.reshape(-1).astype(dtype)
        rows = -(-flat.shape[0] // (width * row_mult)) * row_mult
        if rows * width != flat.shape[0]:
            flat = jnp.pad(flat, (0, rows * width - flat.shape[0]))
        parts.append(flat.reshape(rows, width))
        spans.append((r0, rows))
        r0 += rows
    return jnp.concatenate(parts, axis=0), spans


def _unpack_rows(slab, spans, shapes):
    out = []
    for (r0, rows), shp in zip(spans, shapes):
        n = math.prod(shp)
        out.append(slab[r0:r0 + rows].reshape(-1)[:n].reshape(shp))
    return out


def _two_d(a):
    return a.reshape(-1, a.shape[-1])


def _shard_rows(n, a):
    return _two_d(jnp.swapaxes(a, -1, -2) if n in _TRANSPOSED else a)


def _from_shard_rows(n, rows, shape):
    if n in _TRANSPOSED:
        return jnp.swapaxes(rows.reshape(shape[:-2] + (shape[-1], shape[-2])), -1, -2)
    return rows.reshape(shape)


def _local_slab(local, slab, dtype):
    parts = sorted((r0, n, l) for n, (_, where) in _PLACE.items() for l, (s, r0) in enumerate(where) if s == slab)
    shards = [_shard_rows(n, local[n] if len(_PLACE[n][1]) == 1 else local[n][l]) for _, n, l in parts]
    return jnp.concatenate([a.astype(dtype) for a in shards], axis=0)


def _set_diag(b, pattern):
    return jnp.einsum(pattern, b, jnp.eye(C_GROUPS // N_SETS, dtype=b.dtype))


def _s5_discretize(lam_re, lam_im, log_dt, b_re, b_im):
    dt = jnp.exp(log_dt)[:, None]
    mag = jnp.exp(lam_re * dt)
    ar = mag * jnp.cos(lam_im * dt)
    ai = mag * jnp.sin(lam_im * dt)
    den = lam_re * lam_re + lam_im * lam_im
    qr = ((ar - 1.0) * lam_re + ai * lam_im) / den
    qi = (ai * lam_re - (ar - 1.0) * lam_im) / den
    bbr = qr[..., None] * b_re - qi[..., None] * b_im
    bbi = qr[..., None] * b_im + qi[..., None] * b_re
    return ar, ai, bbr, bbi


def _attention_block(x, mem, W, w, i, tag):
    xn, q = _norm_mm(x, w["ca_norm"][i], _shards(W, "ca_wq", i), split="k", out_dtype=BF16, name=f"{tag}_q")
    memn = _rms_fwd(mem, w["ca_mem_norm"][i], name=f"{tag}_ca_memnorm")
    k = _mm_k(memn, _shards(W, "ca_wk", i), out_dtype=BF16, name=f"{tag}_k")
    v = _mm_k(memn, _shards(W, "ca_wv", i), out_dtype=BF16, name=f"{tag}_v")
    o = _attn_fwd(q, k, v, name=f"{tag}_attn")
    y = _mm_k(o, _shards(W, "ca_wo", i), add=x, name=f"{tag}_wo")
    return y, (x, xn, memn, q, k, v, o)


def _attention_block_bwd(dy, saved, mem, W, w, i, tag, G, grads, token=None, mid=None):
    x, xn, memn, q, k, v, o = saved
    gain = w["ca_norm"][i]
    if token is not None:
        k = _behind(k, token)
    G = _grad_to_slab(G, "ca_wo", i, o, dy, a_cols=256, name=f"{tag}_dwo")
    dq, dk, dv = _attn_bwd(dy, _shards(W, "ca_wo", i), q, k, v, name=f"{tag}_attn_bwd")
    token = mid(dq) if mid is not None else None
    if token is not None:
        gain = _behind(gain, token)
    G = _grad_to_slab(G, "ca_wq", i, xn, dq, a_cols=256, name=f"{tag}_dwq")
    G = _grad_to_slab(G, "ca_wk", i, memn, dk, a_cols=256, name=f"{tag}_dwk")
    G = _grad_to_slab(G, "ca_wv", i, memn, dv, a_cols=256, name=f"{tag}_dwv")
    dmemn = _mm_k_t([(dk, _shards(W, "ca_wk", i)), (dv, _shards(W, "ca_wv", i))], name=f"{tag}_dmemn")
    dx, dg = _norm_bwd_k(dq, _shards(W, "ca_wq", i), x, gain, dy, name=f"{tag}_dq_norm_bwd")
    grads["ca_norm"][i] = dg[0]
    grads["ca_mem_norm"][i] = _rms_dg(mem, w["ca_mem_norm"][i], dmemn, name=f"{tag}_ca_memnorm_bwd")[0]
    return dx, G


def _ffn_block(x, W, w, i, tag):
    fn, gate, up, h = _ffn_up(x, w["ffn_norm"][i], _shards(W, "ffn_w_gate", i), _shards(W, "ffn_w_up", i),
                              name=f"{tag}_ffn_up")
    y = _mm_k(h, _shards(W, "ffn_w_down", i), add=x, name=f"{tag}_down")
    return y, (x, fn, gate, up, h)


def _ffn_block_bwd(dy, saved, W, w, i, tag, G, grads, token=None, mid=None):
    x, fn, gate, up, h = saved
    gain = w["ffn_norm"][i]
    G = _grad_to_slab(G, "ffn_w_down", i, h, dy, name=f"{tag}_dwd")
    dg, du = _ffn_bwd_hidden(dy, _shards(W, "ffn_w_down", i), gate, up, token, name=f"{tag}_ffn_bwd_hidden")
    token = mid(dg) if mid is not None else None
    if token is not None:
        gain = _behind(gain, token)
    G = _grad_to_slab(G, "ffn_w_gate", i, dg, fn, name=f"{tag}_dwg")
    G = _grad_to_slab(G, "ffn_w_up", i, du, fn, name=f"{tag}_dwu")
    dx, dgn = _ffn_in_bwd(dg, du, _shards(W, "ffn_w_gate", i), _shards(W, "ffn_w_up", i), x, gain, dy,
                          name=f"{tag}_ffn_in_bwd")
    grads["ffn_norm"][i] = dgn[0]
    return dx, G


def _gmlp_mask():
    chunk = jnp.arange(GMLP_BLOCK) // CHUNK
    return chunk[None, :] <= chunk[:, None]


def _even_block(x, W, w, tag):
    hn, proj = _norm_mm(x, w["e_norm"][0], _shards(W, "e_w_in"), split="n", out_dtype=F32, name=f"{tag}_w_in")
    wm = jnp.where(_gmlp_mask()[None], w["e_gmlp_w"][0], 0.0).astype(BF16)
    bcol = w["e_gmlp_b"][0][:, :, None]
    cw = jnp.pad(w["e_conv_w"][0], ((0, CONV_HALO - CONV_WIDTH), (0, 0)))
    cb, lg, lb = w["e_conv_b"], w["e_conv_ln_g"], w["e_conv_ln_b"]
    mix, hc = _even_fwd(proj, wm, bcol, cw, cb, lg, lb, name=f"{tag}_mixers")
    y = _mm_k(mix, _shards(W, "e_w_out"), add=x, name=f"{tag}_w_out")
    return y, (x, hn, proj, mix, hc, wm, bcol, cw)


def _even_block_bwd(dy, saved, W, w, tag, G, grads):
    x, hn, proj, mix, hc, wm, bcol, cw = saved
    dmix = _mm_k_t([(dy, _shards(W, "e_w_out"))], name=f"{tag}_dmix")
    G = _grad_to_slab(G, "e_w_out", 0, mix, dy, a_cols=256, name=f"{tag}_dw_out")
    wmt = jnp.swapaxes(wm, 1, 2)
    dpa, dhc, dwm, db, dlg, dlb, dcb = _even_bwd1(proj, dmix, hc, wm, wmt, bcol, w["e_conv_ln_g"], w["e_conv_ln_b"],
                                                  name=f"{tag}_mixers_bwd1")
    dpb, dcw = _even_bwd2(proj, dhc, cw, name=f"{tag}_mixers_bwd2")
    grads["e_gmlp_w"] = jnp.where(_gmlp_mask()[None], dwm, 0.0)[None]
    grads["e_gmlp_b"] = db[:, :, 0][None]
    grads["e_conv_ln_g"], grads["e_conv_ln_b"], grads["e_conv_b"] = dlg, dlb, dcb
    grads["e_conv_w"] = dcw[:CONV_WIDTH][None]
    G = _grad_to_slab(G, "e_w_in", 0, hn, dpa, b_cols=512, chips=(0, 2), name=f"{tag}_dw_in_a")
    G = _grad_to_slab(G, "e_w_in", 0, hn, dpb, b_cols=512, chips=(2, 2), name=f"{tag}_dw_in_b")
    dx, dg = _norm_bwd_n((dpa, dpb), _shards(W, "e_w_in"), x, w["e_norm"][0], dy, name=f"{tag}_in_bwd")
    grads["e_norm"] = dg
    return dx, G


def _odd_block(x, W, w, tag):
    S = x.shape[0]
    hn, u = _norm_mm(x, w["o_norm"][0], _shards(W, "o_w_in"), split="k", out_dtype=F32, name=f"{tag}_w_in")
    disc_in = (w["o_lam_re"][0], w["o_lam_im"][0], w["o_log_dt"][0], w["o_b_re"][0], w["o_b_im"][0])
    (ar, ai, bbr, bbi), disc_vjp = jax.vjp(_s5_discretize, *disc_in)
    sets = (N_SETS, C_GROUPS // N_SETS)
    per_set = N_STATE // N_SETS
    bset = jnp.concatenate([_set_diag(b.reshape(sets + b.shape[1:]), "jgpc,gh->jgchp").reshape(N_SETS, SET_CH, per_set)
                            for b in (bbr, bbi)], axis=2).astype(BF16)
    cset = jnp.concatenate([_set_diag(c.reshape(sets + c.shape[1:]), "jgcp,gh->jgphc").reshape(N_SETS, per_set, SET_CH)
                            for c in (w["o_c_re"][0], -w["o_c_im"][0])], axis=1).astype(BF16)
    powers, pr, pi = [], ar, ai
    for _ in range(SCAN_BLOCK):
        powers.append(jnp.concatenate([pr.reshape(STATE_ROWS, STATE_LANES), pi.reshape(STATE_ROWS, STATE_LANES)], 0))
        pr, pi = pr * ar - pi * ai, pr * ai + pi * ar
    pw = jnp.stack(powers, axis=0)
    state_rows = (S * _STATE_TILE, STATE_LANES)
    bu = _mm_to_state(u, bset, name=f"{tag}_bu")
    xs = _scan_fwd(bu.reshape(state_rows), pw, name=f"{tag}_scan").reshape(bu.shape)
    yv, yg = _s5_readout(xs, cset, u, w["o_d"], name=f"{tag}_readout")
    o, y = _glu_out(yg, _shards(W, "o_w_out"), x, name=f"{tag}_glu_out")
    return y, (x, hn, u, bset, cset, pw, xs, yv, yg, o, disc_vjp)


def _odd_block_bwd(dy, saved, W, w, tag, G, grads):
    x, hn, u, bset, cset, pw, xs, yv, yg, o, disc_vjp = saved
    S = x.shape[0]
    state_rows = (S * _STATE_TILE, STATE_LANES)
    do, dys, dus, dd = _glu_out_bwd(o, dy, _shards(W, "o_w_out"), yv, u, w["o_d"], name=f"{tag}_glu_out_bwd")
    G = _grad_to_slab(G, "o_w_out", 0, yg, do, b_cols=512, name=f"{tag}_dw_out")
    grads["o_d"] = dd
    dxs = _mm_to_state(dys, cset, nt=True, name=f"{tag}_dxs")
    dcset_t = _state_grad_sets(dys, xs, name=f"{tag}_dcd")
    gs, da = _scan_bwd(dxs.reshape(state_rows), xs.reshape(state_rows), pw, name=f"{tag}_scan_bwd")
    gs = gs.reshape(xs.shape)
    dbset = _state_grad_sets(u, gs, name=f"{tag}_dbd")
    du, dx, dg = _s5_in_bwd(gs, bset, dus, _shards(W, "o_w_in"), x, w["o_norm"][0], dy, name=f"{tag}_in_bwd")
    G = _grad_to_slab(G, "o_w_in", 0, hn, du, a_cols=256, name=f"{tag}_dw_in")
    grads["o_norm"] = dg
    per = C_GROUPS // N_SETS
    blocks = (N_SETS, per, C_GROUP_CH, 2, per, C_STATE)
    dc = _set_diag(dcset_t.reshape(blocks), "jhcrgp,gh->rjgcp").reshape(2, C_GROUPS, C_GROUP_CH, C_STATE)
    db = _set_diag(dbset.reshape(blocks), "jgcrhp,gh->rjgpc").reshape(2, C_GROUPS, C_STATE, C_GROUP_CH)
    dcr, dci, dbbr, dbbi = dc[0], -dc[1], db[0], db[1]
    dar = da[:STATE_ROWS].reshape(C_GROUPS, C_STATE)
    dai = da[STATE_ROWS:].reshape(C_GROUPS, C_STATE)
    dlr, dli, dldt, dbr, dbi = disc_vjp((dar, dai, dbbr, dbbi))
    grads["o_lam_re"], grads["o_lam_im"], grads["o_log_dt"] = dlr[None], dli[None], dldt[None]
    grads["o_b_re"], grads["o_b_im"], grads["o_c_re"], grads["o_c_im"] = dbr[None], dbi[None], dcr[None], dci[None]
    return dx, G


def _behind(value, token):
    return value + token[0, 0].astype(value.dtype)


class _NoExchange:
    def __init__(self, W):
        self.W = W

    def first_weights(self, w):
        return self.W, w

    def weights(self, stage, after):
        return {}

    def grads_ready(self, piece, G):
        return None

    def grads_crossed(self, piece, after):
        return None


def _forward_backward(xs_, mems_, tgt, w, G, exchange):
    W, w = exchange.first_weights(w)
    x1, s_mix0 = _even_block(xs_, W, w, "l0")
    W = {**W, **exchange.weights(1, x1)}
    x2, s_att0 = _attention_block(x1, mems_, W, w, 0, "l0")
    W = {**W, **exchange.weights(2, x2)}
    x3, s_ffn0 = _ffn_block(x2, W, w, 0, "l0")
    W = {**W, **exchange.weights(3, x3)}
    x4, s_mix1 = _odd_block(x3, W, w, "l1")
    x5, s_att1 = _attention_block(x4, mems_, W, w, 1, "l1")
    x6, s_ffn1 = _ffn_block(x5, W, w, 1, "l1")
    dx, dfinal, loss_lanes = _loss_head(x6, w["final_norm"], tgt, name="loss_head")

    grads = {n: [None, None] for n in ("ca_norm", "ca_mem_norm", "ffn_norm")}
    grads["final_norm"] = dfinal[0]
    dx, G = _ffn_block_bwd(dx, s_ffn1, W, w, 1, "l1", G, grads)
    dx, G = _attention_block_bwd(dx, s_att1, mems_, W, w, 1, "l1", G, grads)
    dx, G = _odd_block_bwd(dx, s_mix1, W, w, "l1", G, grads)
    token = exchange.grads_ready("l1", G)
    dx, G = _ffn_block_bwd(dx, s_ffn0, W, w, 0, "l0", G, grads, token,
                           lambda after: exchange.grads_crossed("l1", after))
    token = exchange.grads_ready("ffn0", G)
    dx, G = _attention_block_bwd(dx, s_att0, mems_, W, w, 0, "l0", G, grads, token,
                                 lambda after: exchange.grads_crossed("ffn0", after))
    dx, G = _even_block_bwd(dx, s_mix0, W, w, "l0", G, grads)
    for n in list(grads):
        if isinstance(grads[n], list):
            grads[n] = jnp.stack(grads[n], axis=0)
        grads[n] = grads[n].reshape(w[n].shape)
    return loss_lanes, dx, G, grads


class _Exchange:
    def __init__(self, local, chip, core):
        self.bufs = {s: lax.dynamic_update_slice(lax.empty((N_CHIPS, rows, width), BF16),
                                                 _local_slab(local, s, BF16)[None], (chip, 0, 0))
                     for s, (width, rows) in _SLABS.items()}
        self.half = core.reshape(1).astype(jnp.int32)
        self.where = jnp.stack([chip, core]).astype(jnp.int32)
        self.flights = []
        self.reduces = {}

    def weights(self, stage, after):
        send_sems, recv_sems, bufs, _ = self.flights[stage]
        bufs = _gather_ici_wait(send_sems, recv_sems, bufs, after, name=f"gather_stage{stage}_wait")
        return dict(zip(_STAGES[stage], _gather_forward(bufs, name=f"gather_stage{stage}_forward")))

    def first_weights(self, w):
        after = w["e_conv_w"].reshape(-1)[:STATE_LANES]
        for k, stage in enumerate(_STAGES):
            self.flights.append(_gather_ici_start([self.bufs[s] for s in stage], after, name=f"gather_stage{k}_start"))
            after = self.flights[-1][3]
        return self.weights(0, after), {**w, "e_norm": _behind(w["e_norm"], after)}

    def pair_start(self, G, slabs, tag):
        send_sems, recv_sems, gl, lands, token = _pair_exchange_start([G[s] for s in slabs],
                                                                      name=f"grad_{tag}_pair_start")
        return (slabs, send_sems, recv_sems, gl, lands), token

    def pair_land(self, state, after, tag):
        slabs, send_sems, recv_sems, gl, lands = state
        gl, other = _pair_exchange_wait(send_sems, recv_sems, gl, lands, after, name=f"grad_{tag}_pair_wait")
        pairs = [_pair_sum(g, r, self.half, name=f"grad_pair_sum_{s}") for s, g, r in zip(slabs, gl, other)]
        send_sems, recv_sems, pairs, lands, token = _chip_exchange_start(pairs, name=f"grad_{tag}_chip_start")
        return (slabs, gl, other, send_sems, recv_sems, pairs, lands), token

    def reduce_finish(self, state, after, tag):
        slabs, gl, other, send_sems, recv_sems, pairs, lands = state
        slots = _chip_exchange_wait(send_sems, recv_sems, pairs, lands, after, name=f"grad_{tag}_chip_wait")
        halves = [_chip_sum(g, r, sl, self.where, name=f"grad_chip_sum_{s}")
                  for s, g, r, sl in zip(slabs, gl, other, slots)]
        return dict(zip(slabs, _pair_share(halves, name=f"grad_{tag}_pair_share")))

    def grads_ready(self, piece, G):
        self.reduces[piece], token = self.pair_start(G, _GRAD_PIECES[piece], piece)
        return token

    def grads_crossed(self, piece, after):
        self.reduces[piece], token = self.pair_land(self.reduces[piece], after, piece)
        return token


def kernel(x, mem, e_norm, e_w_in, e_gmlp_w, e_gmlp_b, e_conv_w, e_conv_b, e_conv_ln_g, e_conv_ln_b, e_w_out, o_norm, o_w_in, o_lam_re, o_lam_im, o_log_dt, o_b_re, o_b_im, o_c_re, o_c_im, o_d, o_w_out, ca_norm, ca_mem_norm, ca_wq, ca_wk, ca_wv, ca_wo, ffn_norm, ffn_w_gate, ffn_w_up, ffn_w_down, final_norm, loss_target, m_e_norm, m_e_w_in, m_e_gmlp_w, m_e_gmlp_b, m_e_conv_w, m_e_conv_b, m_e_conv_ln_g, m_e_conv_ln_b, m_e_w_out, m_o_norm, m_o_w_in, m_o_lam_re, m_o_lam_im, m_o_log_dt, m_o_b_re, m_o_b_im, m_o_c_re, m_o_c_im, m_o_d, m_o_w_out, m_ca_norm, m_ca_mem_norm, m_ca_wq, m_ca_wk, m_ca_wv, m_ca_wo, m_ffn_norm, m_ffn_w_gate, m_ffn_w_up, m_ffn_w_down, m_final_norm, v_e_norm, v_e_w_in, v_e_gmlp_w, v_e_gmlp_b, v_e_conv_w, v_e_conv_b, v_e_conv_ln_g, v_e_conv_ln_b, v_e_w_out, v_o_norm, v_o_w_in, v_o_lam_re, v_o_lam_im, v_o_log_dt, v_o_b_re, v_o_b_im, v_o_c_re, v_o_c_im, v_o_d, v_o_w_out, v_ca_norm, v_ca_mem_norm, v_ca_wq, v_ca_wk, v_ca_wv, v_ca_wo, v_ffn_norm, v_ffn_w_gate, v_ffn_w_up, v_ffn_w_down, v_final_norm):
    args = dict(locals())
    local = {n: args[n] for n in _WEIGHTS}
    mom = {n: args["m_" + n] for n in _WEIGHTS}
    vel = {n: args["v_" + n] for n in _WEIGHTS}
    chip = 2 * lax.axis_index("x") + lax.axis_index("y")
    core = lax.axis_index("c")
    xs_, mems_, tgt = x[0], mem[0], loss_target[0]

    w = {n: local[n] for n in _REPLICATED}
    sm_slab, sm_spans = _pack_rows([local[n] for n, _ in _SMALL_SHARDED], SMALL_W, F32)
    sm_all = _allgather_small(sm_slab, name="gather_small_weights").reshape(N_DEV, -1, SMALL_W)
    for (n, ax), span in zip(_SMALL_SHARDED, sm_spans):
        shp = local[n].shape
        w[n] = jnp.concatenate([_unpack_rows(sm_all[2 * p], [span], [shp])[0] for p in range(N_CHIPS)], axis=ax)

    exchange = _Exchange(local, chip, core)
    G = {s: lax.empty((N_CHIPS, rows, width), F32) for s, (width, rows) in _SLABS.items()}
    loss_lanes, dx, G, grads = _forward_backward(xs_, mems_, tgt, w, G, exchange)

    gs_slab, gs_spans = _pack_rows([grads[n] for n in _SMALL] + [loss_lanes], SMALL_W, F32)
    small_flight = _all_to_all_start(gs_slab, dx, name="small_grads_start")
    exchange.grads_ready("rest0", G)
    gsum = exchange.reduce_finish(exchange.reduces["l1"], small_flight[4], "l1")
    gsum = {**gsum, **exchange.reduce_finish(exchange.reduces["ffn0"], small_flight[4], "ffn0")}
    token = exchange.grads_crossed("rest0", gsum["B0"])

    gs_slab, gs_all = _all_to_all_wait(*small_flight[:4], token, name="small_grads_wait")
    gs_all = lax.dynamic_update_slice(gs_all, gs_slab[None], (2 * chip + core, 0, 0))
    gs_sum = _sum_slots(gs_all, name="small_grad_sum")
    *small_sums, loss_sum = _unpack_rows(gs_sum, gs_spans, [grads[n].shape for n in _SMALL] + [loss_lanes.shape])
    out_grads = dict(zip(_SMALL, small_sums))
    for n, ax in _SMALL_SHARDED:
        width = local[n].shape[ax]
        out_grads[n] = lax.dynamic_slice_in_dim(out_grads[n], chip * width, width, axis=ax)

    delta, new_m, new_v = {}, {}, {}
    d_, m_, v_ = _adamw_small([_two_d(local[n]) for n in _SMALL], [_two_d(out_grads[n]) for n in _SMALL],
                              [_two_d(mom[n]) for n in _SMALL], [_two_d(vel[n]) for n in _SMALL], name="adamw_small")
    for n, dd, mm_, vv in zip(_SMALL, d_, m_, v_):
        shp = local[n].shape
        delta[n], new_m[n], new_v[n] = dd.reshape(shp), mm_.reshape(shp), vv.reshape(shp)
    def adamw_large(names):
        for n in names:
            shp = local[n].shape
            g_, d_, m_, v_ = _adamw_shard(_shard_rows(n, local[n]), [(gsum[s], r0) for s, r0 in _PLACE[n][1]],
                                          _shard_rows(n, mom[n]), _shard_rows(n, vel[n]), name=f"adamw_{n}")
            out_grads[n], delta[n], new_m[n], new_v[n] = (_from_shard_rows(n, t, shp) for t in (g_, d_, m_, v_))

    ready = [n for n, (_, where) in _PLACE.items() if all(s in gsum for s, _ in where)]
    adamw_large(ready)
    done = jnp.concatenate([delta[n].reshape(-1)[:1] for n in ready + list(_SMALL[:1])])
    gsum = {**gsum, **exchange.reduce_finish(exchange.reduces["rest0"], done, "rest0")}
    adamw_large([n for n in _PLACE if n not in ready])

    return (loss_sum[0, 0], dx[None], *[out_grads[n] for n in _WEIGHTS], *[delta[n] for n in _WEIGHTS],
            *[new_m[n] for n in _WEIGHTS], *[new_v[n] for n in _WEIGHTS])
```

```python
import functools
import math

import jax
import jax.numpy as jnp
from jax import lax
from jax.experimental import pallas as pl
from jax.experimental.pallas import tpu as pltpu

F32 = jnp.float32
BF16 = jnp.bfloat16
MESH = pl.DeviceIdType.MESH

EPS = 1e-6
D_MODEL = 1024
A_WIDTH = 512
A_GROUPS = 4
GMLP_BLOCK = 128
CHUNK = 64
B_WIDTH = 512
CONV_WIDTH = 31
CONV_HALO = 32
C_WIDTH = 512
C_GROUP_CH = 16
C_GROUPS = 32
C_STATE = 64
N_STATE = C_GROUPS * C_STATE
STATE_LANES = 128
STATE_ROWS = N_STATE // STATE_LANES
SCAN_BLOCK = 8
CA_HEADS = 4
CA_HEAD_DIM = 256
FFN_HIDDEN = 2816

ADAM_LR = 0.001
ADAM_B1 = 0.9
ADAM_B2 = 0.999
ADAM_EPS = 1e-08
ADAM_WD = 0.01
ADAM_STEP = 10

VMEM_LIMIT = 56 * 1024 * 1024
ACC_BYTES = 6 * 1024 * 1024
TN_VMEM_BYTES = 44 * 1024 * 1024
SMALL_W = 128
N_CHIPS = 4
N_DEV = 8

_SLABS = {"D0": (512, 1024), "E0": (1024, 256), "A0": (1024, 1024), "B0": (1024, 704), "C0": (1024, 1408),
          "D1": (512, 768), "A1": (1024, 1024), "B1": (1024, 704), "C1": (1024, 1408)}
_STAGES = (("D0", "E0"), ("A0",), ("B0", "C0"), ("D1", "A1", "B1", "C1"))
_GRAD_PIECES = {"l1": _STAGES[3], "ffn0": _STAGES[2], "rest0": _STAGES[0] + _STAGES[1]}
_PLACE = {
    "e_w_in": (1024, (("D0", 0),)), "e_w_out": (256, (("E0", 0),)),
    "o_w_out": (512, (("D1", 0),)), "o_w_in": (256, (("D1", 512),)),
    "ca_wq": (256, (("A0", 0), ("A1", 0))), "ca_wk": (256, (("A0", 256), ("A1", 256))),
    "ca_wv": (256, (("A0", 512), ("A1", 512))), "ca_wo": (256, (("A0", 768), ("A1", 768))),
    "ffn_w_down": (704, (("B0", 0), ("B1", 0))),
    "ffn_w_gate": (704, (("C0", 0), ("C1", 0))), "ffn_w_up": (704, (("C0", 704), ("C1", 704))),
}
_TRANSPOSED = ("ffn_w_gate", "ffn_w_up")


def _params(sem=None):
    return pltpu.CompilerParams(dimension_semantics=sem, vmem_limit_bytes=VMEM_LIMIT)


def _tile(n, pref, mult=128):
    if n <= pref:
        return n
    t = (pref // mult) * mult
    while t >= mult:
        if n % t == 0:
            return t
        t -= mult
    return n


def _blk(name, layer=0):
    rows, where = _PLACE[name]
    slab, r0 = where[layer]
    assert r0 % rows == 0
    return slab, rows, r0 // rows


def _shards(slabs, name, layer=0):
    slab, rows, b = _blk(name, layer)
    return [(slabs[slab], (None, rows, _SLABS[slab][0]), (p, b, 0)) for p in range(N_CHIPS)]


_GELU_C = 0.7978845608028654
_GELU_A = 0.044715


def _gelu(x):
    t = jnp.tanh(_GELU_C * (x + _GELU_A * (x * x * x)))
    return 0.5 * x * (1.0 + t), t


def _gelu_grad(x, t):
    return 0.5 * (1.0 + t) + 0.5 * x * (1.0 - t * t) * (_GELU_C * (1.0 + 3.0 * _GELU_A * x * x))


def _sigmoid(x):
    return 1.0 / (1.0 + jnp.exp(-x))


def _mean(x):
    return jnp.mean(x, axis=-1, keepdims=True)


def _dot(a, b):
    return jnp.dot(a, b, preferred_element_type=F32)


def _dot_nt(a, b):
    return lax.dot_general(a, b, (((1,), (1,)), ((), ())), preferred_element_type=F32)


def _dot_tn(a, b):
    return lax.dot_general(a, b, (((0,), (0,)), ((), ())), preferred_element_type=F32)


def _rms_tile(xv, gv):
    return (xv * lax.rsqrt(_mean(xv * xv) + EPS)) * gv


def _rms_bwd_tile(xv, gv, dyv):
    r = lax.rsqrt(_mean(xv * xv) + EPS)
    xh = xv * r
    dyg = dyv * gv
    return r * (dyg - xh * _mean(dyg * xh)), jnp.sum(dyv * xh, axis=0, keepdims=True)


def _cols(p, width):
    return slice(p * width, (p + 1) * width)


def _sum_k(a, ws, k):
    tot = None
    for p in range(N_CHIPS):
        y = _dot(a[:, _cols(p, k)], ws[p][...])
        tot = y if tot is None else tot + y
    return tot


def _cat_nt(a, ws):
    return jnp.concatenate([_dot_nt(a, ws[p][...]) for p in range(N_CHIPS)], axis=1)


def _rows_call(name, tm, rows, fulls, outs, accs, body, scratch=()):
    S = min(x.shape[-2] for x in rows if x.ndim != 4)
    nr, nf, no, na = len(rows), len(fulls), len(outs), len(accs)

    def kern(*refs):
        r, f = refs[:nr], refs[nr:nr + nf]
        o, a = refs[nr + nf:nr + nf + no], refs[nr + nf + no:nr + nf + no + na]
        if na:
            @pl.when(pl.program_id(0) == 0)
            def _():
                for ref in a:
                    ref[...] = jnp.zeros_like(ref)
        body(r, f, o, a, refs[nr + nf + no + na:])

    def whole(shape):
        nd = len(shape)
        return pl.BlockSpec(tuple(shape), lambda i: (0,) * nd)

    def row_spec(shape):
        if len(shape) == 4:
            return pl.BlockSpec((tm // 8,) + tuple(shape[1:]), lambda i: (i, 0, 0, 0))
        if len(shape) == 3:
            return pl.BlockSpec((shape[0], tm, shape[2]), lambda i: (0, i, 0))
        return pl.BlockSpec((tm, shape[1]), lambda i: (i, 0))

    def full_spec(x):
        if isinstance(x, tuple):
            _, bshape, bidx = x
            return pl.BlockSpec(bshape, lambda i: bidx, pipeline_mode=pl.Buffered(1))
        return whole(x.shape)

    def out_shape_of(o):
        if o[0] == "state":
            return (S // 8, 2 * STATE_ROWS, 8, STATE_LANES)
        return (S, o[0]) if len(o) == 2 else (o[0], S, o[1])

    out_shapes = [out_shape_of(o) for o in outs]
    res = pl.pallas_call(
        kern, name=name, grid=(S // tm,),
        in_specs=[row_spec(x.shape) for x in rows] + [full_spec(x) for x in fulls],
        out_specs=[row_spec(s) for s in out_shapes] + [whole(shp) for shp, _ in accs],
        out_shape=[jax.ShapeDtypeStruct(s, o[-1]) for s, o in zip(out_shapes, outs)]
        + [jax.ShapeDtypeStruct(tuple(shp), dt) for shp, dt in accs],
        scratch_shapes=list(scratch),
        compiler_params=_params(("arbitrary",) if na else ("parallel",)),
    )(*rows, *[x[0] if isinstance(x, tuple) else x for x in fulls])
    return res[:no], res[no:]


def _grad_to_slab(gslabs, wname, layer, a, b, *, a_cols=None, b_cols=None, chips=(0, N_CHIPS), name):
    slab, rows, bidx = _blk(wname, layer)
    width = _SLABS[slab][0]
    p0, n_p = chips
    assert p0 % n_p == 0
    S = a.shape[-2]

    def tile_bytes(x, ts):
        return ts * x.dtype.itemsize * (x.shape[2] * n_p if x.ndim == 3 else x.shape[1])

    acc_bytes = n_p * rows * (-(-width // 128) * 128) * 4
    ts = next(t for t in (2048, 1024, 512, 256, S) if S % t == 0
              and 2 * (tile_bytes(a, t) + tile_bytes(b, t) + acc_bytes) <= TN_VMEM_BYTES or t == S)

    def operand(x):
        if x.ndim == 3:
            return pl.BlockSpec((n_p, ts, x.shape[2]), lambda s: (p0 // n_p, s, 0))
        return pl.BlockSpec((ts, x.shape[1]), lambda s: (s, 0))

    def part(ref, cols, p):
        if len(ref.shape) == 3:
            return ref[p]
        return ref[...] if cols is None else ref[:, _cols(p, cols)]

    def body(a_ref, b_ref, slab_ref, o_ref):
        @pl.when(pl.program_id(0) == 0)
        def _():
            o_ref[...] = jnp.zeros_like(o_ref)

        for p in range(n_p):
            o_ref[p] += _dot_tn(part(a_ref, a_cols, p).astype(BF16), part(b_ref, b_cols, p).astype(BF16))

    g = gslabs[slab]
    out = pl.pallas_call(
        body, name=name, grid=(S // ts,),
        in_specs=[operand(a), operand(b), pl.BlockSpec(memory_space=pl.ANY)],
        out_specs=pl.BlockSpec((n_p, rows, width), lambda s: (p0 // n_p, bidx, 0)),
        out_shape=jax.ShapeDtypeStruct(g.shape, F32), input_output_aliases={2: 0},
        compiler_params=_params(("arbitrary",)),
    )(a, b, g)
    return {**gslabs, slab: out}


def _vec(g):
    return g.reshape(1, -1)


def _norm_mm(x, g, ws, *, split, out_dtype, name, tm=512):
    S, D = x.shape
    k, n = ws[0][1][1], ws[0][1][2]
    N = n if split == "k" else N_CHIPS * n

    def body(r, f, o, acc, s):
        xn = _rms_tile(r[0][...], f[0][...]).astype(BF16)
        o[0][...] = xn
        if split == "k":
            o[1][...] = _sum_k(xn, f[1:], k).astype(out_dtype)
        else:
            for p in range(N_CHIPS):
                o[1][:, _cols(p, n)] = _dot(xn, f[1 + p][...]).astype(out_dtype)

    (xn, y), _ = _rows_call(name, _tile(S, tm), [x], [_vec(g)] + ws, [(D, BF16), (N, out_dtype)], [], body)
    return xn, y


def _mm_k(a, ws, *, add=None, out_dtype=F32, name, tm=512):
    S = a.shape[-2]
    k, n = ws[0][1][1], ws[0][1][2]
    has_add = add is not None

    def body(r, f, o, acc, s):
        if a.ndim == 3:
            y = None
            for p in range(N_CHIPS):
                t = _dot(r[0][p].astype(BF16), f[p][...])
                y = t if y is None else y + t
        else:
            y = _sum_k(r[0][...].astype(BF16), f, k)
        if has_add:
            y = y + r[1][...]
        o[0][...] = y.astype(out_dtype)

    (y,), _ = _rows_call(name, _tile(S, tm), [a] + ([add] if has_add else []), ws, [(n, out_dtype)], [], body)
    return y


def _mm_k_t(terms, *, out_dtype=F32, name, tm=512):
    S = terms[0][0].shape[0]
    k = terms[0][1][0][1][1]

    def body(r, f, o, acc, s):
        y = None
        for t in range(len(terms)):
            yt = _cat_nt(r[t][...].astype(BF16), f[N_CHIPS * t:N_CHIPS * (t + 1)])
            y = yt if y is None else y + yt
        o[0][...] = y.astype(out_dtype)

    (y,), _ = _rows_call(name, _tile(S, tm), [a for a, _ in terms], [w for _, ws in terms for w in ws],
                         [(N_CHIPS * k, out_dtype)], [], body)
    return y


def _rms_fwd(x, g, *, name):
    def body(r, f, o, acc, s):
        o[0][...] = _rms_tile(r[0][...], f[0][...]).astype(BF16)

    (y,), _ = _rows_call(name, _tile(x.shape[0], 256, 8), [x], [_vec(g)], [(x.shape[1], BF16)], [], body)
    return y


def _rms_dg(x, g, dy, *, name):
    def body(r, f, o, acc, s):
        acc[0][...] += _rms_bwd_tile(r[0][...], f[0][...], r[1][...])[1]

    _, (dg,) = _rows_call(name, _tile(x.shape[0], 256, 8), [x, dy], [_vec(g)], [], [((1, x.shape[1]), F32)], body)
    return dg


def _ffn_up(x, g, wg, wu, *, name, tm=256):
    S, D = x.shape
    h = wg[0][1][1]

    def body(r, f, o, acc, s):
        xn = _rms_tile(r[0][...], f[0][...]).astype(BF16)
        o[0][...] = xn
        for p in range(N_CHIPS):
            gate = _dot_nt(xn, f[1 + p][...])
            up = _dot_nt(xn, f[1 + N_CHIPS + p][...])
            o[1][p] = gate.astype(BF16)
            o[2][p] = up.astype(BF16)
            o[3][p] = (gate * _sigmoid(gate) * up).astype(BF16)

    (xn, gate, up, hid), _ = _rows_call(name, _tile(S, tm), [x], [_vec(g)] + wg + wu,
                                        [(D, BF16), (N_CHIPS, h, BF16), (N_CHIPS, h, BF16), (N_CHIPS, h, BF16)], [],
                                        body)
    return xn, gate, up, hid


def _ffn_bwd_hidden(dy, wd, gate, up, token=None, *, name, tm=256):
    S = dy.shape[0]
    h = wd[0][1][1]

    def body(r, f, o, acc, s):
        dyv = r[0][...]
        if token is not None:
            dyv = dyv + jnp.sum(f[N_CHIPS][...])
        dyb = dyv.astype(BF16)
        for p in range(N_CHIPS):
            dh = _dot_nt(dyb, f[p][...])
            gv = r[1][p].astype(F32)
            sg = _sigmoid(gv)
            o[0][p] = (dh * r[2][p].astype(F32) * (sg * (1.0 + gv * (1.0 - sg)))).astype(BF16)
            o[1][p] = (dh * gv * sg).astype(BF16)

    (dg, du), _ = _rows_call(name, _tile(S, tm), [dy, gate, up], wd + ([] if token is None else [token]),
                             [(N_CHIPS, h, BF16), (N_CHIPS, h, BF16)], [], body)
    return dg, du


def _ffn_in_bwd(dg, du, wg, wu, x, g, dres, *, name, tm=256):
    S, D = x.shape

    def body(r, f, o, acc, s):
        tot = None
        for p in range(N_CHIPS):
            y = _dot(r[0][p], f[1 + p][...]) + _dot(r[1][p], f[1 + N_CHIPS + p][...])
            tot = y if tot is None else tot + y
        dx, dgn = _rms_bwd_tile(r[2][...], f[0][...], tot)
        o[0][...] = dx + r[3][...]
        acc[0][...] += dgn

    (dx,), (dgn,) = _rows_call(name, _tile(S, tm), [dg, du, x, dres], [_vec(g)] + wg + wu, [(D, F32)],
                               [((1, D), F32)], body)
    return dx, dgn


def _norm_bwd_k(da, ws, x, g, dres, *, name, tm=512):
    S, D = x.shape

    def body(r, f, o, acc, s):
        dx, dg = _rms_bwd_tile(r[1][...], f[0][...], _cat_nt(r[0][...].astype(BF16), f[1:]))
        o[0][...] = dx + r[2][...]
        acc[0][...] += dg

    (dx,), (dg,) = _rows_call(name, _tile(S, tm), [da, x, dres], [_vec(g)] + ws, [(D, F32)], [((1, D), F32)], body)
    return dx, dg


def _norm_bwd_n(das, ws, x, g, dres, *, name, tm=256):
    S, D = x.shape
    n = ws[0][1][2]

    def body(r, f, o, acc, s):
        tot = None
        for p in range(N_CHIPS):
            y = _dot_nt(r[p // 2][:, _cols(p % 2, n)], f[1 + p][...])
            tot = y if tot is None else tot + y
        dx, dg = _rms_bwd_tile(r[2][...], f[0][...], tot)
        o[0][...] = dx + r[3][...]
        acc[0][...] += dg

    (dx,), (dg,) = _rows_call(name, _tile(S, tm), list(das) + [x, dres], [_vec(g)] + ws, [(D, F32)], [((1, D), F32)],
                              body)
    return dx, dg


def _ln_stats(v):
    mu = _mean(v)
    xc = v - mu
    rstd = lax.rsqrt(_mean(xc * xc) + EPS)
    return xc * rstd, rstd


_SHIFTS = 8
_CONV_ROWS = 64


def _fill_shifts(sh_ref, ext_ref, tm):
    sh_ref[0] = ext_ref[...]
    for s in range(1, _SHIFTS):
        sh_ref[s, 0:tm + CONV_HALO - _SHIFTS, :] = ext_ref[pl.ds(s, tm + CONV_HALO - _SHIFTS), :]


def _window(sh_ref, off, tm):
    return sh_ref[off % _SHIFTS, pl.ds(off - off % _SHIFTS, tm), :]


def _even_fwd(proj, wm, bcol, cw, cb, lg, lb, *, name):
    S = proj.shape[0]
    tm = _tile(S, 256)
    hb = tm // CONV_HALO
    nblk = tm // GMLP_BLOCK

    def body(p_ref, halo_ref, wm_ref, b_ref, cw_ref, cb_ref, lg_ref, lb_ref, mix_ref, hc_ref, hext_ref, hsh_ref):
        i = pl.program_id(0)
        gu, _ = _gelu(p_ref[:, 0:A_WIDTH])
        gv, _ = _gelu(p_ref[:, A_WIDTH:2 * A_WIDTH])
        vn, _ = _ln_stats(gv)
        vnb = vn.astype(BF16)
        for n in range(nblk):
            rows = slice(n * GMLP_BLOCK, (n + 1) * GMLP_BLOCK)
            for g in range(A_GROUPS):
                cols = slice(g * GMLP_BLOCK, (g + 1) * GMLP_BLOCK)
                sg = jnp.dot(wm_ref[g], vnb[rows, cols], preferred_element_type=F32) + b_ref[g]
                mix_ref[rows, cols] = (gu[rows, cols] * sg).astype(BF16)
        h = p_ref[:, 1024:1536] * _sigmoid(p_ref[:, 1536:2048])
        hh = halo_ref[:, 0:B_WIDTH] * _sigmoid(halo_ref[:, B_WIDTH:2 * B_WIDTH])
        hext_ref[0:CONV_HALO, :] = jnp.where(i > 0, hh, 0.0)
        hext_ref[CONV_HALO:CONV_HALO + tm, :] = h
        _fill_shifts(hsh_ref, hext_ref, tm)
        for r0 in range(0, tm, _CONV_ROWS):
            acc = jnp.zeros((_CONV_ROWS, B_WIDTH), F32)
            for k in range(CONV_WIDTH):
                acc = acc + cw_ref[k:k + 1, :] * _window(hsh_ref, r0 + k + CONV_HALO - CONV_WIDTH + 1, _CONV_ROWS)
            hc_ref[r0:r0 + _CONV_ROWS, :] = acc + cb_ref[...]
        hc = hc_ref[...]
        hhat, _ = _ln_stats(hc)
        hl = hhat * lg_ref[...] + lb_ref[...]
        mix_ref[:, A_WIDTH:A_WIDTH + B_WIDTH] = (hl * _sigmoid(hl)).astype(BF16)

    vec = pl.BlockSpec((1, B_WIDTH), lambda i: (0, 0))
    return pl.pallas_call(
        body, name=name, grid=(S // tm,),
        in_specs=[
            pl.BlockSpec((tm, 2048), lambda i: (i, 0)),
            pl.BlockSpec((CONV_HALO, 1024), lambda i: (jnp.maximum(i * hb - 1, 0), 1)),
            pl.BlockSpec((A_GROUPS, GMLP_BLOCK, GMLP_BLOCK), lambda i: (0, 0, 0)),
            pl.BlockSpec((A_GROUPS, GMLP_BLOCK, 1), lambda i: (0, 0, 0)),
            pl.BlockSpec((CONV_HALO, B_WIDTH), lambda i: (0, 0)),
            vec, vec, vec,
        ],
        out_specs=[pl.BlockSpec((tm, 1024), lambda i: (i, 0)), pl.BlockSpec((tm, B_WIDTH), lambda i: (i, 0))],
        out_shape=[jax.ShapeDtypeStruct((S, 1024), BF16), jax.ShapeDtypeStruct((S, B_WIDTH), F32)],
        scratch_shapes=[pltpu.VMEM((tm + CONV_HALO, B_WIDTH), F32),
                        pltpu.VMEM((_SHIFTS, tm + CONV_HALO, B_WIDTH), F32)],
        compiler_params=_params(("parallel",)),
    )(proj, proj, wm, bcol, cw, cb, lg, lb)


def _even_bwd1(proj, dmix, hc, wm, wmt, bcol, lg, lb, *, name):
    S = proj.shape[0]
    tm = _tile(S, 256)
    nblk = tm // GMLP_BLOCK

    def body(p_ref, dm_ref, hc_ref, wm_ref, wmt_ref, b_ref, lg_ref, lb_ref,
             dpa_ref, dhc_ref, dwm_ref, db_ref, dlg_ref, dlb_ref, dcb_ref, dgu_ref, dvn_ref):
        @pl.when(pl.program_id(0) == 0)
        def _():
            dwm_ref[...] = jnp.zeros_like(dwm_ref)
            db_ref[...] = jnp.zeros_like(db_ref)
            dlg_ref[...] = jnp.zeros_like(dlg_ref)
            dlb_ref[...] = jnp.zeros_like(dlb_ref)
            dcb_ref[...] = jnp.zeros_like(dcb_ref)

        au = p_ref[:, 0:A_WIDTH]
        av = p_ref[:, A_WIDTH:2 * A_WIDTH]
        gu, tu = _gelu(au)
        gv, tv = _gelu(av)
        vn, rstd = _ln_stats(gv)
        vnb = vn.astype(BF16)
        for n in range(nblk):
            rows = slice(n * GMLP_BLOCK, (n + 1) * GMLP_BLOCK)
            for g in range(A_GROUPS):
                cols = slice(g * GMLP_BLOCK, (g + 1) * GMLP_BLOCK)
                vb = vnb[rows, cols]
                sg = jnp.dot(wm_ref[g], vb, preferred_element_type=F32) + b_ref[g]
                da = dm_ref[rows, cols]
                dsg = da * gu[rows, cols]
                dgu_ref[rows, cols] = da * sg
                dsgb = dsg.astype(BF16)
                dwm_ref[g] += _dot_nt(dsgb, vb)
                db_ref[g] += jnp.sum(dsg, axis=1, keepdims=True)
                dvn_ref[rows, cols] = jnp.dot(wmt_ref[g], dsgb, preferred_element_type=F32)
        dvn = dvn_ref[...]
        dgv = rstd * (dvn - _mean(dvn) - vn * _mean(dvn * vn))
        dpa_ref[:, 0:A_WIDTH] = (dgu_ref[...] * _gelu_grad(au, tu)).astype(BF16)
        dpa_ref[:, A_WIDTH:2 * A_WIDTH] = (dgv * _gelu_grad(av, tv)).astype(BF16)
        hhat, rstd2 = _ln_stats(hc_ref[...])
        lgv = lg_ref[...]
        hl = hhat * lgv + lb_ref[...]
        s = _sigmoid(hl)
        dhl = dm_ref[:, A_WIDTH:A_WIDTH + B_WIDTH] * (s * (1.0 + hl * (1.0 - s)))
        dlg_ref[...] += jnp.sum(dhl * hhat, axis=0, keepdims=True)
        dlb_ref[...] += jnp.sum(dhl, axis=0, keepdims=True)
        dhh = dhl * lgv
        dhc = rstd2 * (dhh - _mean(dhh) - hhat * _mean(dhh * hhat))
        dcb_ref[...] += jnp.sum(dhc, axis=0, keepdims=True)
        dhc_ref[...] = dhc

    vec = pl.BlockSpec((1, B_WIDTH), lambda i: (0, 0))
    w3 = pl.BlockSpec((A_GROUPS, GMLP_BLOCK, GMLP_BLOCK), lambda i: (0, 0, 0))
    b3 = pl.BlockSpec((A_GROUPS, GMLP_BLOCK, 1), lambda i: (0, 0, 0))
    return pl.pallas_call(
        body, name=name, grid=(S // tm,),
        in_specs=[
            pl.BlockSpec((tm, 1024), lambda i: (i, 0)),
            pl.BlockSpec((tm, 1024), lambda i: (i, 0)),
            pl.BlockSpec((tm, B_WIDTH), lambda i: (i, 0)),
            w3, w3, b3, vec, vec,
        ],
        out_specs=[pl.BlockSpec((tm, 1024), lambda i: (i, 0)), pl.BlockSpec((tm, B_WIDTH), lambda i: (i, 0)),
                   w3, b3, vec, vec, vec],
        out_shape=[
            jax.ShapeDtypeStruct((S, 1024), BF16), jax.ShapeDtypeStruct((S, B_WIDTH), F32),
            jax.ShapeDtypeStruct((A_GROUPS, GMLP_BLOCK, GMLP_BLOCK), F32),
            jax.ShapeDtypeStruct((A_GROUPS, GMLP_BLOCK, 1), F32),
            jax.ShapeDtypeStruct((1, B_WIDTH), F32), jax.ShapeDtypeStruct((1, B_WIDTH), F32),
            jax.ShapeDtypeStruct((1, B_WIDTH), F32),
        ],
        scratch_shapes=[pltpu.VMEM((tm, A_WIDTH), F32), pltpu.VMEM((tm, A_WIDTH), F32)],
        compiler_params=_params(("arbitrary",)),
    )(proj, dmix, hc, wm, wmt, bcol, lg, lb)


def _even_bwd2(proj, dhc, cw, *, name):
    S = proj.shape[0]
    tm = _tile(S, 256)
    hb = tm // CONV_HALO
    nt = S // tm
    last_halo = S // CONV_HALO - 1
    lo = CONV_HALO - CONV_WIDTH + 1

    def body(p_ref, halo_ref, d_ref, dnext_ref, cw_ref, dpb_ref, dcw_ref, hext_ref, dext_ref, hsh_ref, dsh_ref):
        i = pl.program_id(0)

        @pl.when(i == 0)
        def _():
            dcw_ref[...] = jnp.zeros_like(dcw_ref)

        hh = halo_ref[:, 0:B_WIDTH] * _sigmoid(halo_ref[:, B_WIDTH:2 * B_WIDTH])
        hext_ref[0:CONV_HALO, :] = jnp.where(i > 0, hh, 0.0)
        hext_ref[CONV_HALO:CONV_HALO + tm, :] = p_ref[:, 0:B_WIDTH] * _sigmoid(p_ref[:, B_WIDTH:2 * B_WIDTH])
        dext_ref[0:tm, :] = d_ref[...]
        dext_ref[tm:tm + CONV_HALO, :] = jnp.where(i < nt - 1, dnext_ref[...], 0.0)
        _fill_shifts(hsh_ref, hext_ref, tm)
        _fill_shifts(dsh_ref, dext_ref, tm)
        for r0 in range(0, tm, _CONV_ROWS):
            rows = slice(r0, r0 + _CONV_ROWS)
            dhc_b = d_ref[rows, :]
            dh = jnp.zeros((_CONV_ROWS, B_WIDTH), F32)
            for k in range(CONV_WIDTH):
                dh = dh + cw_ref[k:k + 1, :] * _window(dsh_ref, r0 + CONV_WIDTH - 1 - k, _CONV_ROWS)
                dcw_ref[k:k + 1, :] += jnp.sum(dhc_b * _window(hsh_ref, r0 + k + lo, _CONV_ROWS), axis=0,
                                               keepdims=True)
            ba_b = p_ref[rows, 0:B_WIDTH]
            sg_b = _sigmoid(p_ref[rows, B_WIDTH:2 * B_WIDTH])
            dpb_ref[rows, 0:B_WIDTH] = (dh * sg_b).astype(BF16)
            dpb_ref[rows, B_WIDTH:2 * B_WIDTH] = (dh * ba_b * sg_b * (1.0 - sg_b)).astype(BF16)

    return pl.pallas_call(
        body, name=name, grid=(nt,),
        in_specs=[
            pl.BlockSpec((tm, 1024), lambda i: (i, 1)),
            pl.BlockSpec((CONV_HALO, 1024), lambda i: (jnp.maximum(i * hb - 1, 0), 1)),
            pl.BlockSpec((tm, B_WIDTH), lambda i: (i, 0)),
            pl.BlockSpec((CONV_HALO, B_WIDTH), lambda i: (jnp.minimum((i + 1) * hb, last_halo), 0)),
            pl.BlockSpec((CONV_HALO, B_WIDTH), lambda i: (0, 0)),
        ],
        out_specs=[pl.BlockSpec((tm, 1024), lambda i: (i, 0)), pl.BlockSpec((CONV_HALO, B_WIDTH), lambda i: (0, 0))],
        out_shape=[jax.ShapeDtypeStruct((S, 1024), BF16), jax.ShapeDtypeStruct((CONV_HALO, B_WIDTH), F32)],
        scratch_shapes=[pltpu.VMEM((tm + CONV_HALO, B_WIDTH), F32), pltpu.VMEM((tm + CONV_HALO, B_WIDTH), F32),
                        pltpu.VMEM((_SHIFTS, tm + CONV_HALO, B_WIDTH), F32),
                        pltpu.VMEM((_SHIFTS, tm + CONV_HALO, B_WIDTH), F32)],
        compiler_params=_params(("arbitrary",)),
    )(proj, proj, dhc, dhc, cw)


_CA_SCALE = CA_HEAD_DIM ** -0.5


def _softmax_rows(s):
    e = jnp.exp(s - jnp.max(s, axis=-1, keepdims=True))
    return e / jnp.sum(e, axis=-1, keepdims=True)


def _attn_fwd(q, k, v, *, name):
    S = q.shape[0]

    def body(r, f, o, acc, s):
        for h in range(CA_HEADS):
            cols = _cols(h, CA_HEAD_DIM)
            p = _softmax_rows(_dot_nt(r[0][:, cols], f[0][:, cols]) * _CA_SCALE)
            o[0][:, cols] = _dot(p.astype(BF16), f[1][:, cols]).astype(BF16)

    (o_,), _ = _rows_call(name, _tile(S, 512), [q], [k, v], [(D_MODEL, BF16)], [], body)
    return o_


def _attn_bwd(dy, wo, q, k, v, *, name):
    S = q.shape[0]
    M = k.shape[0]

    def body(r, f, o, acc, s):
        dyb = r[0][...].astype(BF16)
        for h in range(CA_HEADS):
            cols = _cols(h, CA_HEAD_DIM)
            qh = r[1][:, cols]
            kh = f[0][:, cols]
            vh = f[1][:, cols]
            doh = _dot_nt(dyb, f[2 + h][...]).astype(BF16)
            p = _softmax_rows(_dot_nt(qh, kh) * _CA_SCALE)
            acc[1][:, cols] += _dot_tn(p.astype(BF16), doh)
            dp = _dot_nt(doh, vh)
            ds = (p * (dp - jnp.sum(dp * p, axis=-1, keepdims=True)) * _CA_SCALE).astype(BF16)
            o[0][:, cols] = _dot(ds, kh).astype(BF16)
            acc[0][:, cols] += _dot_tn(ds, qh)

    (dq,), (dk, dv) = _rows_call(name, _tile(S, 512), [dy, q], [k, v] + wo, [(D_MODEL, BF16)],
                                 [((M, D_MODEL), F32), ((M, D_MODEL), F32)], body)
    return dq, dk, dv


_STATE_TILE = 2 * STATE_ROWS
N_SETS = 4
SET_CH = C_WIDTH // N_SETS
SET_COLS = N_STATE // N_SETS // STATE_LANES


def _set_groups(j):
    return [SET_COLS * j + c for c in range(SET_COLS)] + [STATE_ROWS + SET_COLS * j + c for c in range(SET_COLS)]


def _pack_state(re, im):
    hi = lax.bitcast_convert_type(re.astype(BF16).astype(F32), jnp.uint32)
    lo = lax.bitcast_convert_type(im.astype(BF16).astype(F32), jnp.uint32) >> 16
    return hi | lo


def _unpack_state(word):
    re = lax.bitcast_convert_type(word & jnp.uint32(0xFFFF0000), F32)
    im = lax.bitcast_convert_type(word << 16, F32)
    return re, im


def _state_set(ref, tm, j):
    if ref.dtype == jnp.uint32:
        parts = [_unpack_state(ref[:, SET_COLS * j + c, :, :].reshape(tm, STATE_LANES)) for c in range(SET_COLS)]
        return jnp.concatenate([p[0].astype(BF16) for p in parts] + [p[1].astype(BF16) for p in parts], axis=1)
    return jnp.concatenate([ref[:, c, :, :].reshape(tm, STATE_LANES).astype(BF16) for c in _set_groups(j)], axis=1)


def _put_state_set(ref, y, tm, j):
    for k, c in enumerate(_set_groups(j)):
        ref[:, c, :, :] = y[:, _cols(k, STATE_LANES)].reshape(tm // 8, 8, STATE_LANES)


def _mm_to_state(a, wset, *, nt=False, name, tm=256):
    S = a.shape[0]
    tm = _tile(S, tm)

    def body(r, f, o, acc, s):
        for j in range(N_SETS):
            av = r[0][:, _cols(j, SET_CH)].astype(BF16)
            _put_state_set(o[0], _dot_nt(av, f[0][j]) if nt else _dot(av, f[0][j]), tm, j)

    (y,), _ = _rows_call(name, tm, [a], [wset], [("state", F32)], [], body)
    return y


def _s5_readout(xs, cset, u, d, *, name, tm=256):
    tm = _tile(u.shape[0], tm)

    def body(r, f, o, acc, s):
        y0 = jnp.concatenate([_dot(_state_set(r[0], tm, j), f[0][j]) for j in range(N_SETS)], axis=1)
        y = y0 + f[1][...] * r[1][...]
        o[0][...] = y
        o[1][...] = _gelu(y)[0].astype(BF16)

    (y, yg), _ = _rows_call(name, tm, [xs, u], [cset, d], [(C_WIDTH, F32), (C_WIDTH, BF16)], [], body)
    return y, yg


def _state_grad_sets(a, st, *, name, ts=256):
    ts = _tile(a.shape[0], ts)

    def body(r, f, o, acc, s):
        for j in range(N_SETS):
            acc[0][j] += _dot_tn(r[0][:, _cols(j, SET_CH)].astype(BF16), _state_set(r[1], ts, j))

    _, (out,) = _rows_call(name, ts, [a, st], [], [], [((N_SETS, SET_CH, 2 * N_STATE // N_SETS), F32)], body)
    return out


def _glu_out(yg, ws, x, *, name, tm=512):
    n = ws[0][1][2]

    def body(r, f, o, acc, s):
        ygv = r[0][...]
        ov = [_dot(ygv, f[p][...]) for p in range(N_CHIPS)]
        for p in range(N_CHIPS):
            o[0][:, _cols(p, n)] = ov[p].astype(BF16)
        for p in range(2):
            o[1][:, _cols(p, n)] = r[1][:, _cols(p, n)] + ov[p] * _sigmoid(ov[2 + p])

    (o_, y), _ = _rows_call(name, _tile(x.shape[0], tm), [yg, x], ws, [(2 * D_MODEL, BF16), (D_MODEL, F32)], [], body)
    return o_, y


def _glu_out_bwd(o_, dy, ws, y, u, d, *, name, tm=256):
    n = ws[0][1][2]

    def body(r, f, o, acc, s):
        o1 = r[0][:, 0:D_MODEL].astype(F32)
        sg = _sigmoid(r[0][:, D_MODEL:2 * D_MODEL].astype(F32))
        dyv = r[1][...]
        do1 = (dyv * sg).astype(BF16)
        do2 = (dyv * o1 * sg * (1.0 - sg)).astype(BF16)
        o[0][:, 0:D_MODEL] = do1
        o[0][:, D_MODEL:2 * D_MODEL] = do2
        dyg = None
        for p in range(N_CHIPS):
            t = _dot_nt((do1 if p < 2 else do2)[:, _cols(p % 2, n)], f[1 + p][...])
            dyg = t if dyg is None else dyg + t
        yv = r[2][...]
        dys = dyg * _gelu_grad(yv, _gelu(yv)[1])
        o[1][...] = dys.astype(BF16)
        o[2][...] = f[0][...] * dys
        acc[0][...] += jnp.sum(dys * r[3][...], axis=0, keepdims=True)

    (do, dys, dus), (dd,) = _rows_call(name, _tile(dy.shape[0], tm), [o_, dy, y, u], [d] + ws,
                                       [(2 * D_MODEL, BF16), (C_WIDTH, BF16), (C_WIDTH, F32)], [((1, C_WIDTH), F32)],
                                       body)
    return do, dys, dus, dd


def _s5_in_bwd(gs, bset, dus, ws, x, g, dres, *, name, tm=256):
    D = x.shape[1]
    tm = _tile(x.shape[0], tm)

    def body(r, f, o, acc, s):
        du0 = jnp.concatenate([_dot_nt(_state_set(r[0], tm, j), f[1][j]) for j in range(N_SETS)], axis=1)
        du = (du0 + r[1][...]).astype(BF16)
        o[0][...] = du
        dx, dg = _rms_bwd_tile(r[2][...], f[0][...], _cat_nt(du, f[2:]))
        o[1][...] = dx + r[3][...]
        acc[0][...] += dg

    (du, dx), (dg,) = _rows_call(name, tm, [gs, dus, x, dres], [_vec(g), bset] + ws,
                                 [(C_WIDTH, BF16), (D, F32)], [((1, D), F32)], body)
    return du, dx, dg


_SCAN_CHUNK = 128
_RE = slice(0, STATE_ROWS)
_IM = slice(STATE_ROWS, 2 * STATE_ROWS)
assert SCAN_BLOCK == 8


def _token(g, i, rows=_STATE_TILE):
    return pl.ds(pl.multiple_of(g * (rows * SCAN_BLOCK), rows * SCAN_BLOCK) + i, rows, stride=SCAN_BLOCK)


def _scan_fwd(bu, pw, *, name):
    S = bu.shape[0] // _STATE_TILE
    tc = _tile(S, _SCAN_CHUNK, 8)

    def body(bu_ref, pw_ref, xs_ref, st_ref):
        @pl.when(pl.program_id(0) == 0)
        def _():
            st_ref[...] = jnp.zeros_like(st_ref)

        ar = pw_ref[0, _RE, :]
        ai = pw_ref[0, _IM, :]

        def block(g, carry):
            xr, xi = carry
            cr = ci = nr = ni = None
            for j in range(SCAN_BLOCK):
                b = bu_ref[_token(g, j), :]
                br, bi = b[_RE], b[_IM]
                cr, ci = (br, bi) if j == 0 else (ar * cr - ai * ci + br, ar * ci + ai * cr + bi)
                pr, pi = pw_ref[j, _RE, :], pw_ref[j, _IM, :]
                nr = pr * xr - pi * xi + cr
                ni = pr * xi + pi * xr + ci
                xs_ref[_token(g, j, STATE_ROWS), :] = _pack_state(nr, ni)
            return nr, ni

        xr, xi = lax.fori_loop(0, tc // SCAN_BLOCK, block, (st_ref[_RE, :], st_ref[_IM, :]), unroll=2)
        st_ref[_RE, :] = xr
        st_ref[_IM, :] = xi

    blk = pl.BlockSpec((tc * _STATE_TILE, STATE_LANES), lambda i: (i, 0))
    return pl.pallas_call(
        body, name=name, grid=(S // tc,),
        in_specs=[blk, pl.BlockSpec(pw.shape, lambda i: (0, 0, 0))],
        out_specs=pl.BlockSpec((tc * STATE_ROWS, STATE_LANES), lambda i: (i, 0)),
        out_shape=jax.ShapeDtypeStruct((S * STATE_ROWS, STATE_LANES), jnp.uint32),
        scratch_shapes=[pltpu.VMEM((2 * STATE_ROWS, STATE_LANES), F32)],
        compiler_params=_params(("arbitrary",)),
    )(bu, pw)


def _scan_bwd(dxs, xs, pw, *, name):
    S = dxs.shape[0] // _STATE_TILE
    tc = _tile(S, _SCAN_CHUNK, 8)
    nc = S // tc

    def body(dx_ref, xs_ref, pw_ref, g_ref, da_ref, st_ref):
        @pl.when(pl.program_id(0) == 0)
        def _():
            st_ref[...] = jnp.zeros_like(st_ref)
            da_ref[...] = jnp.zeros_like(da_ref)

        ar = pw_ref[0, _RE, :]
        ai = pw_ref[0, _IM, :]

        def block(i, carry):
            gr, gi, dar, dai = carry
            g = tc // SCAN_BLOCK - 1 - i
            cr = ci = None
            pgr, pgi = gr, gi
            for j in range(SCAN_BLOCK):
                i = SCAN_BLOCK - 1 - j
                xr, xi = _unpack_state(xs_ref[_token(g, i, STATE_ROWS), :])
                dar = dar + pgr * xr + pgi * xi
                dai = dai + pgi * xr - pgr * xi
                d = dx_ref[_token(g, i), :]
                dr, di = d[_RE], d[_IM]
                cr, ci = (dr, di) if j == 0 else (ar * cr + ai * ci + dr, ar * ci - ai * cr + di)
                pr, pi = pw_ref[j, _RE, :], pw_ref[j, _IM, :]
                pgr = pr * gr + pi * gi + cr
                pgi = pr * gi - pi * gr + ci
                g_ref[_token(g, i, STATE_ROWS), :] = _pack_state(pgr, pgi)
            return pgr, pgi, dar, dai

        init = (st_ref[_RE, :], st_ref[_IM, :], da_ref[_RE, :], da_ref[_IM, :])
        gr, gi, dar, dai = lax.fori_loop(0, tc // SCAN_BLOCK, block, init, unroll=2)
        st_ref[_RE, :] = gr
        st_ref[_IM, :] = gi
        da_ref[_RE, :] = dar
        da_ref[_IM, :] = dai

    blk = pl.BlockSpec((tc * _STATE_TILE, STATE_LANES), lambda i: (nc - 1 - i, 0))
    packed = pl.BlockSpec((tc * STATE_ROWS, STATE_LANES), lambda i: (nc - 1 - i, 0))
    vec = pl.BlockSpec((2 * STATE_ROWS, STATE_LANES), lambda i: (0, 0))
    return pl.pallas_call(
        body, name=name, grid=(nc,), in_specs=[blk, packed, pl.BlockSpec(pw.shape, lambda i: (0, 0, 0))],
        out_specs=[packed, vec],
        out_shape=[jax.ShapeDtypeStruct(xs.shape, jnp.uint32), jax.ShapeDtypeStruct((2 * STATE_ROWS, STATE_LANES), F32)],
        scratch_shapes=[pltpu.VMEM((2 * STATE_ROWS, STATE_LANES), F32)],
        compiler_params=_params(("arbitrary",)),
    )(dxs, xs, pw)


def _loss_head(x, g, target, *, name):
    S, D = x.shape

    def body(r, f, o, acc, s):
        xv = r[0][...]
        gv = f[0][...]
        rs = lax.rsqrt(_mean(xv * xv) + EPS)
        xh = xv * rs
        err = xh * gv - r[1][...]
        acc[1][...] += 0.5 * jnp.sum(_mean(err * err), axis=0, keepdims=True)
        dy = err * (1.0 / D)
        dyg = dy * gv
        o[0][...] = rs * (dyg - xh * _mean(dyg * xh))
        acc[0][...] += jnp.sum(dy * xh, axis=0, keepdims=True)

    (dx,), (dg, loss) = _rows_call(name, _tile(S, 256, 8), [x, target], [_vec(g)], [(D, F32)],
                                   [((1, D), F32), ((1, 128), F32)], body)
    return dx, dg, loss


_ADAM_C1 = 1.0 - ADAM_B1 ** ADAM_STEP
_ADAM_C2 = 1.0 - ADAM_B2 ** ADAM_STEP
_ONE_BLOCK_BYTES = 8 * 1024 * 1024


def _adamw_math(w, g, m, v):
    nm = ADAM_B1 * m + (1.0 - ADAM_B1) * g
    nv = ADAM_B2 * v + (1.0 - ADAM_B2) * (g * g)
    m_hat = nm / _ADAM_C1
    v_hat = nv / _ADAM_C2
    return -ADAM_LR * (m_hat / (jnp.sqrt(v_hat) + ADAM_EPS) + ADAM_WD * w), nm, nv


def _adamw_shard(w, gsrc, m, v, *, name):
    R, C = w.shape
    n_l = len(gsrc)
    rows = R // n_l
    tr = rows
    for _, r0 in gsrc:
        tr = math.gcd(tr, r0) if r0 else tr
    tr = _tile(tr, 256, 8) if tr > 256 else tr
    nb = rows // tr
    assert rows % tr == 0 and all(r0 % tr == 0 for _, r0 in gsrc)

    def body(*refs):
        w_ref, g_refs, (m_ref, v_ref, go_ref, d_ref, nm_ref, nv_ref) = refs[0], refs[1:1 + n_l], refs[1 + n_l:]
        layer = pl.program_id(0) // nb
        gv = g_refs[0][...]
        for l in range(1, n_l):
            gv = jnp.where(layer == l, g_refs[l][...], gv)
        go_ref[...] = gv
        d_ref[...], nm_ref[...], nv_ref[...] = _adamw_math(w_ref[...], gv, m_ref[...], v_ref[...])

    def g_spec(l, r0):
        return pl.BlockSpec((tr, C), lambda i: (r0 // tr + jnp.clip(i - l * nb, 0, nb - 1), 0))

    blk = pl.BlockSpec((tr, C), lambda i: (i, 0))
    out = jax.ShapeDtypeStruct((R, C), F32)
    return pl.pallas_call(
        body, name=name, grid=(R // tr,),
        in_specs=[blk] + [g_spec(l, r0) for l, (_, r0) in enumerate(gsrc)] + [blk, blk], out_specs=[blk] * 4,
        out_shape=[out] * 4, compiler_params=_params(("parallel",)),
    )(w, *[g for g, _ in gsrc], m, v)


def _adamw_small(ws, gs, ms, vs, *, name):
    n = len(ws)

    def body(*refs):
        w_r, g_r, m_r, v_r = refs[:n], refs[n:2 * n], refs[2 * n:3 * n], refs[3 * n:4 * n]
        d_r, nm_r, nv_r = refs[4 * n:5 * n], refs[5 * n:6 * n], refs[6 * n:7 * n]
        for k in range(n):
            d_r[k][...], nm_r[k][...], nv_r[k][...] = _adamw_math(w_r[k][...], g_r[k][...], m_r[k][...], v_r[k][...])

    vm = pl.BlockSpec(memory_space=pltpu.VMEM)
    out = [jax.ShapeDtypeStruct(w.shape, F32) for w in ws]
    res = pl.pallas_call(body, name=name, in_specs=[vm] * (4 * n), out_specs=[vm] * (3 * n), out_shape=out * 3,
                         compiler_params=pltpu.CompilerParams(vmem_limit_bytes=VMEM_LIMIT))(*ws, *gs, *ms, *vs)
    return res[:n], res[n:2 * n], res[2 * n:]


def _sum_slots(x, *, name):
    n, R, C = x.shape
    tr = R if (n + 1) * R * C * 4 <= _ONE_BLOCK_BYTES else _tile(R, 256, 8)

    def body(x_ref, o_ref):
        acc = x_ref[0]
        for k in range(1, n):
            acc = acc + x_ref[k]
        o_ref[...] = acc

    return pl.pallas_call(
        body, name=name, grid=(R // tr,),
        in_specs=[pl.BlockSpec((n, tr, C), lambda i: (0, i, 0))], out_specs=pl.BlockSpec((tr, C), lambda i: (i, 0)),
        out_shape=jax.ShapeDtypeStruct((R, C), F32), compiler_params=_params(("parallel",)),
    )(x)


def _pair_sum(g, r, half, *, name):
    n, R, C = g.shape
    Rh = R // 2
    tr = _tile(Rh, 256, 8)
    nb = Rh // tr

    def body(half_ref, g_ref, r_ref, o_ref):
        o_ref[...] = (g_ref[...] + r_ref[...]).astype(BF16)

    return pl.pallas_call(
        body, name=name,
        grid_spec=pltpu.PrefetchScalarGridSpec(
            num_scalar_prefetch=1, grid=(n, nb),
            in_specs=[pl.BlockSpec((1, tr, C), lambda p, i, h: (p, h[0] * nb + i, 0)),
                      pl.BlockSpec((1, tr, C), lambda p, i, h: (p, i, 0))],
            out_specs=pl.BlockSpec((1, tr, C), lambda p, i, h: (p, i, 0)),
        ),
        out_shape=jax.ShapeDtypeStruct((n, Rh, C), BF16), compiler_params=_params(("parallel", "parallel")),
    )(half, g, r)


def _chip_sum(g, r, slots, where, *, name):
    n, R, C = g.shape
    Rh = R // 2
    tr = _tile(Rh, 256, 8)
    nb = Rh // tr

    def body(w_ref, g_ref, r_ref, s_ref, o_ref):
        acc = g_ref[0] + r_ref[0]
        for k in range(slots.shape[0]):
            acc = acc + s_ref[k].astype(F32)
        o_ref[...] = acc

    return pl.pallas_call(
        body, name=name,
        grid_spec=pltpu.PrefetchScalarGridSpec(
            num_scalar_prefetch=1, grid=(nb,),
            in_specs=[pl.BlockSpec((1, tr, C), lambda i, w: (w[0], w[1] * nb + i, 0)),
                      pl.BlockSpec((1, tr, C), lambda i, w: (w[0], i, 0)),
                      pl.BlockSpec((slots.shape[0], tr, C), lambda i, w: (0, i, 0))],
            out_specs=pl.BlockSpec((tr, C), lambda i, w: (w[1] * nb + i, 0)),
        ),
        out_shape=jax.ShapeDtypeStruct((R, C), F32), compiler_params=_params(("parallel",)),
    )(where, g, r, slots)


ANY = pl.BlockSpec(memory_space=pl.ANY)


def _place():
    return lax.axis_index("x"), lax.axis_index("y"), lax.axis_index("c")


def _other_chips(x, y):
    return [(1 - x, y), (x, 1 - y), (1 - x, 1 - y)]


def _allgather_small(v, *, name):
    R, C = v.shape

    def body(x_ref, out_ref, send_sems, recv_sems, local_sem):
        x, y, c = _place()
        me, sibling = (x, y, c), (x, y, 1 - c)
        chips = _other_chips(x, y)

        def rows(px, py, pc):
            return out_ref.at[pl.ds((4 * px + 2 * py + pc) * R, R), :]

        def copy(k, block, to, src=None):
            return pltpu.make_async_remote_copy(
                src_ref=rows(*block) if src is None else src, dst_ref=rows(*block),
                send_sem=send_sems.at[k], recv_sem=recv_sems.at[k], device_id=to, device_id_type=MESH)

        mine = pltpu.make_async_copy(x_ref, rows(*me), local_sem)
        mine.start()
        first = [copy(0, me, sibling, src=x_ref)]
        first += [copy(1 + j, me, (*chip, c), src=x_ref) for j, chip in enumerate(chips)]
        for cp in first:
            cp.start()
        passed = [copy(4 + j, (*chip, c), sibling) for j, chip in enumerate(chips)]
        for j, chip in enumerate(chips):
            copy(1 + j, (*chip, c), me).wait_recv()
            passed[j].start()
        copy(0, sibling, me).wait_recv()
        for j, chip in enumerate(chips):
            copy(4 + j, (*chip, 1 - c), me).wait_recv()
        for cp in first + passed:
            cp.wait_send()
        mine.wait()

    return pl.pallas_call(
        body, name=name, out_shape=jax.ShapeDtypeStruct((N_DEV * R, C), v.dtype),
        in_specs=[pl.BlockSpec(memory_space=pltpu.VMEM)], out_specs=pl.BlockSpec(memory_space=pltpu.VMEM),
        scratch_shapes=[pltpu.SemaphoreType.DMA((7,)), pltpu.SemaphoreType.DMA((7,)), pltpu.SemaphoreType.DMA],
        compiler_params=pltpu.CompilerParams(vmem_limit_bytes=VMEM_LIMIT),
    )(v)


def _aliased_comm_call(body, bufs, n_sems, *, name):
    n = len(bufs)
    return pl.pallas_call(
        body, name=name, out_shape=[jax.ShapeDtypeStruct(b.shape, b.dtype) for b in bufs],
        in_specs=[ANY] * n, out_specs=[ANY] * n, input_output_aliases={k: k for k in range(n)},
        scratch_shapes=[pltpu.SemaphoreType.DMA((n_sems,)), pltpu.SemaphoreType.DMA((n_sems,))],
    )(*bufs)


HBM = pl.BlockSpec(memory_space=pltpu.HBM)
SEM = pl.BlockSpec(memory_space=pltpu.SEMAPHORE)
_SPLIT = pltpu.CompilerParams(has_side_effects=pltpu.SideEffectType.DATAFLOW_SIDE_EFFECTING)


def _in_hbm(arrs):
    return [pltpu.with_memory_space_constraint(a, pltpu.HBM) for a in arrs]


def _gather_ici_start(bufs, after, *, name):
    n = len(bufs)

    def body(*refs):
        send_sems, recv_sems, outs, token = refs[n + 1], refs[n + 2], refs[n + 3:2 * n + 3], refs[2 * n + 3]
        x, y, c = _place()
        for b in range(n):
            rh = bufs[b].shape[1] // 2
            part = outs[b].at[2 * x + y, pl.ds(c * rh, rh), :]
            for j, chip in enumerate(_other_chips(x, y)):
                pltpu.make_async_remote_copy(src_ref=part, dst_ref=part, send_sem=send_sems.at[3 * b + j],
                                             recv_sem=recv_sems.at[3 * b + j], device_id=(*chip, c),
                                             device_id_type=MESH).start()
        token[...] = jnp.zeros_like(token)

    res = pl.pallas_call(
        body, name=name,
        out_shape=(pltpu.SemaphoreType.DMA((3 * n,)), pltpu.SemaphoreType.DMA((3 * n,)),
                   *[pltpu.HBM(b.shape, b.dtype) for b in bufs], jax.ShapeDtypeStruct((8, 128), F32)),
        in_specs=[HBM] * n + [ANY], out_specs=(SEM, SEM, *[HBM] * n, pl.BlockSpec(memory_space=pltpu.VMEM)),
        input_output_aliases={k: k + 2 for k in range(n)}, compiler_params=_SPLIT,
    )(*_in_hbm(bufs), after)
    return res[0], res[1], list(res[2:2 + n]), res[2 + n]


def _gather_ici_wait(send_sems, recv_sems, bufs, after, *, name):
    n = len(bufs)

    def body(*refs):
        ins, ss, rs = refs[:n], refs[n], refs[n + 1]
        x, y, c = _place()
        for b in range(n):
            rh = bufs[b].shape[1] // 2
            mine = ins[b].at[2 * x + y, pl.ds(c * rh, rh), :]
            for j, (cx, cy) in enumerate(_other_chips(x, y)):
                theirs = ins[b].at[2 * cx + cy, pl.ds(c * rh, rh), :]
                cp = pltpu.make_async_remote_copy(src_ref=mine, dst_ref=theirs, send_sem=ss.at[3 * b + j],
                                                  recv_sem=rs.at[3 * b + j], device_id=(cx, cy, c),
                                                  device_id_type=MESH)
                cp.wait_send()
                cp.wait_recv()

    return list(pl.pallas_call(
        body, name=name, out_shape=[pltpu.HBM(b.shape, b.dtype) for b in bufs],
        in_specs=[HBM] * n + [SEM, SEM, ANY], out_specs=[HBM] * n,
        input_output_aliases={k: k for k in range(n)}, compiler_params=_SPLIT,
    )(*bufs, send_sems, recv_sems, after))


def _gather_forward(bufs, *, name):
    n = len(bufs)

    def body(*refs):
        outs, send_sems, recv_sems = refs[n:2 * n], refs[2 * n], refs[2 * n + 1]
        x, y, c = _place()

        def copy(b, j, chip, hc):
            rh = bufs[b].shape[1] // 2
            part = outs[b].at[2 * chip[0] + chip[1], pl.ds(hc * rh, rh), :]
            return pltpu.make_async_remote_copy(src_ref=part, dst_ref=part, send_sem=send_sems.at[3 * b + j],
                                                recv_sem=recv_sems.at[3 * b + j], device_id=(x, y, 1 - c),
                                                device_id_type=MESH)

        sends = [copy(b, j, chip, c) for b in range(n) for j, chip in enumerate(_other_chips(x, y))]
        for cp in sends:
            cp.start()
        for b in range(n):
            for j, chip in enumerate(_other_chips(x, y)):
                copy(b, j, chip, 1 - c).wait_recv()
        for cp in sends:
            cp.wait_send()

    return _aliased_comm_call(body, bufs, 3 * n, name=name)


def _chip_exchange_start(hs, *, name):
    n = len(hs)
    lands = [lax.empty((3,) + h.shape[1:], h.dtype) for h in hs]

    def body(*refs):
        send_sems, recv_sems = refs[2 * n], refs[2 * n + 1]
        h_out, l_out, token = refs[2 * n + 2:3 * n + 2], refs[3 * n + 2:4 * n + 2], refs[4 * n + 2]
        x, y, c = _place()
        for b in range(n):
            for j, (cx, cy) in enumerate(_other_chips(x, y)):
                pltpu.make_async_remote_copy(src_ref=h_out[b].at[2 * cx + cy], dst_ref=l_out[b].at[j],
                                             send_sem=send_sems.at[3 * b + j], recv_sem=recv_sems.at[3 * b + j],
                                             device_id=(cx, cy, c), device_id_type=MESH).start()
        token[...] = jnp.zeros_like(token)

    res = pl.pallas_call(
        body, name=name,
        out_shape=(pltpu.SemaphoreType.DMA((3 * n,)), pltpu.SemaphoreType.DMA((3 * n,)),
                   *[pltpu.HBM(a.shape, a.dtype) for a in hs + lands], jax.ShapeDtypeStruct((8, 128), F32)),
        in_specs=[HBM] * (2 * n), out_specs=(SEM, SEM, *[HBM] * (2 * n), pl.BlockSpec(memory_space=pltpu.VMEM)),
        input_output_aliases={k: k + 2 for k in range(2 * n)}, compiler_params=_SPLIT,
    )(*_in_hbm(hs + lands))
    return res[0], res[1], list(res[2:2 + n]), list(res[2 + n:2 + 2 * n]), res[2 + 2 * n]


def _chip_exchange_wait(send_sems, recv_sems, hs, lands, after, *, name):
    n = len(hs)

    def body(*refs):
        h_in, l_in, ss, rs = refs[:n], refs[n:2 * n], refs[2 * n], refs[2 * n + 1]
        x, y, c = _place()
        for b in range(n):
            for j, (cx, cy) in enumerate(_other_chips(x, y)):
                cp = pltpu.make_async_remote_copy(src_ref=h_in[b].at[2 * cx + cy], dst_ref=l_in[b].at[j],
                                                  send_sem=ss.at[3 * b + j], recv_sem=rs.at[3 * b + j],
                                                  device_id=(cx, cy, c), device_id_type=MESH)
                cp.wait_send()
                cp.wait_recv()

    res = pl.pallas_call(
        body, name=name, out_shape=[pltpu.HBM(a.shape, a.dtype) for a in hs + lands],
        in_specs=[HBM] * (2 * n) + [SEM, SEM, ANY], out_specs=[HBM] * (2 * n),
        input_output_aliases={k: k for k in range(2 * n)}, compiler_params=_SPLIT,
    )(*hs, *lands, send_sems, recv_sems, after)
    return list(res[n:])


def _peers(x, y, c):
    return [((1 - x) if fx else x, (1 - y) if fy else y, (1 - c) if fc else c)
            for fx in (0, 1) for fy in (0, 1) for fc in (0, 1) if fx or fy or fc]


def _all_to_all_start(slab, after, *, name):
    land = lax.empty((N_DEV,) + slab.shape, slab.dtype)

    def body(slab_in, land_in, after_ref, send_sems, recv_sems, slab_out, land_out, token):
        x, y, c = _place()
        for k, peer in enumerate(_peers(x, y, c)):
            pltpu.make_async_remote_copy(src_ref=slab_out, dst_ref=land_out.at[4 * x + 2 * y + c],
                                         send_sem=send_sems.at[k], recv_sem=recv_sems.at[k], device_id=peer,
                                         device_id_type=MESH).start()
        token[...] = jnp.zeros_like(token)

    return pl.pallas_call(
        body, name=name,
        out_shape=(pltpu.SemaphoreType.DMA((N_DEV - 1,)), pltpu.SemaphoreType.DMA((N_DEV - 1,)),
                   pltpu.HBM(slab.shape, slab.dtype), pltpu.HBM(land.shape, land.dtype),
                   jax.ShapeDtypeStruct((8, 128), F32)),
        in_specs=[HBM, HBM, ANY], out_specs=(SEM, SEM, HBM, HBM, pl.BlockSpec(memory_space=pltpu.VMEM)),
        input_output_aliases={0: 2, 1: 3}, compiler_params=_SPLIT,
    )(*_in_hbm([slab, land]), after)


def _all_to_all_wait(send_sems, recv_sems, slab, land, after, *, name):
    def body(slab_in, land_in, ss, rs, after_ref, slab_out, land_out):
        x, y, c = _place()
        for k, (px, py, pc) in enumerate(_peers(x, y, c)):
            cp = pltpu.make_async_remote_copy(src_ref=slab_in, dst_ref=land_in.at[4 * px + 2 * py + pc],
                                              send_sem=ss.at[k], recv_sem=rs.at[k], device_id=(px, py, pc),
                                              device_id_type=MESH)
            cp.wait_send()
            cp.wait_recv()

    return pl.pallas_call(
        body, name=name, out_shape=[pltpu.HBM(slab.shape, slab.dtype), pltpu.HBM(land.shape, land.dtype)],
        in_specs=[HBM, HBM, SEM, SEM, ANY], out_specs=[HBM, HBM], input_output_aliases={0: 0, 1: 1},
        compiler_params=_SPLIT,
    )(slab, land, send_sems, recv_sems, after)


def _pair_exchange_start(gs, *, name):
    n = len(gs)
    lands = [lax.empty((g.shape[0], g.shape[1] // 2, g.shape[2]), g.dtype) for g in gs]

    def body(*refs):
        send_sems, recv_sems = refs[2 * n], refs[2 * n + 1]
        g_out, l_out, token = refs[2 * n + 2:3 * n + 2], refs[3 * n + 2:4 * n + 2], refs[4 * n + 2]
        x, y, c = _place()
        for b in range(n):
            rh = gs[b].shape[1] // 2
            pltpu.make_async_remote_copy(src_ref=g_out[b].at[:, pl.ds((1 - c) * rh, rh), :], dst_ref=l_out[b],
                                         send_sem=send_sems.at[b], recv_sem=recv_sems.at[b],
                                         device_id=(x, y, 1 - c), device_id_type=MESH).start()
        token[...] = jnp.zeros_like(token)

    res = pl.pallas_call(
        body, name=name,
        out_shape=(pltpu.SemaphoreType.DMA((n,)), pltpu.SemaphoreType.DMA((n,)),
                   *[pltpu.HBM(a.shape, a.dtype) for a in gs + lands], jax.ShapeDtypeStruct((8, 128), F32)),
        in_specs=[HBM] * (2 * n), out_specs=(SEM, SEM, *[HBM] * (2 * n), pl.BlockSpec(memory_space=pltpu.VMEM)),
        input_output_aliases={k: k + 2 for k in range(2 * n)}, compiler_params=_SPLIT,
    )(*_in_hbm(gs + lands))
    return res[0], res[1], list(res[2:2 + n]), list(res[2 + n:2 + 2 * n]), res[2 + 2 * n]


def _pair_exchange_wait(send_sems, recv_sems, gs, lands, after, *, name):
    n = len(gs)

    def body(*refs):
        g_in, l_in, ss, rs = refs[:n], refs[n:2 * n], refs[2 * n], refs[2 * n + 1]
        x, y, c = _place()
        for b in range(n):
            rh = gs[b].shape[1] // 2
            cp = pltpu.make_async_remote_copy(src_ref=g_in[b].at[:, pl.ds((1 - c) * rh, rh), :], dst_ref=l_in[b],
                                              send_sem=ss.at[b], recv_sem=rs.at[b], device_id=(x, y, 1 - c),
                                              device_id_type=MESH)
            cp.wait_send()
            cp.wait_recv()

    res = pl.pallas_call(
        body, name=name, out_shape=[pltpu.HBM(a.shape, a.dtype) for a in gs + lands],
        in_specs=[HBM] * (2 * n) + [SEM, SEM, ANY], out_specs=[HBM] * (2 * n),
        input_output_aliases={k: k for k in range(2 * n)}, compiler_params=_SPLIT,
    )(*gs, *lands, send_sems, recv_sems, after)
    return list(res[:n]), list(res[n:])


def _pair_share(ss, *, name):
    n = len(ss)

    def body(*refs):
        outs, send_sems, recv_sems = refs[n:2 * n], refs[2 * n], refs[2 * n + 1]
        x, y, c = _place()
        cps = []
        for b in range(n):
            rh = ss[b].shape[0] // 2
            mine = outs[b].at[pl.ds(c * rh, rh), :]
            cps.append(pltpu.make_async_remote_copy(src_ref=mine, dst_ref=mine, send_sem=send_sems.at[b],
                                                    recv_sem=recv_sems.at[b], device_id=(x, y, 1 - c),
                                                    device_id_type=MESH))
        for cp in cps:
            cp.start()
        for b, cp in enumerate(cps):
            rh = ss[b].shape[0] // 2
            theirs = outs[b].at[pl.ds((1 - c) * rh, rh), :]
            pltpu.make_async_remote_copy(src_ref=theirs, dst_ref=theirs, send_sem=send_sems.at[b],
                                         recv_sem=recv_sems.at[b], device_id=(x, y, 1 - c),
                                         device_id_type=MESH).wait_recv()
            cp.wait_send()

    return _aliased_comm_call(body, ss, n, name=name)


_SMALL_SHARDED = (("e_conv_w", 2), ("o_norm", 1), ("o_d", 1))
_REPLICATED = ("e_norm", "e_gmlp_w", "e_gmlp_b", "e_conv_b", "e_conv_ln_g", "e_conv_ln_b", "o_lam_re", "o_lam_im",
               "o_log_dt", "o_b_re", "o_b_im", "o_c_re", "o_c_im", "ca_norm", "ca_mem_norm", "ffn_norm", "final_norm")
_SMALL = tuple(n for n, _ in _SMALL_SHARDED) + _REPLICATED
_WEIGHTS = ("e_norm", "e_w_in", "e_gmlp_w", "e_gmlp_b", "e_conv_w", "e_conv_b", "e_conv_ln_g", "e_conv_ln_b",
            "e_w_out", "o_norm", "o_w_in", "o_lam_re", "o_lam_im", "o_log_dt", "o_b_re", "o_b_im", "o_c_re", "o_c_im",
            "o_d", "o_w_out", "ca_norm", "ca_mem_norm", "ca_wq", "ca_wk", "ca_wv", "ca_wo", "ffn_norm", "ffn_w_gate",
            "ffn_w_up", "ffn_w_down", "final_norm")


def _pack_rows(arrs, width, dtype, row_mult=8):
    parts, spans, r0 = [], [], 0
    for a in arrs:
        flat = a.reshape(-1).astype(dtype)
        rows = -(-flat.shape[0] // (width * row_mult)) * row_mult
        if rows * width != flat.shape[0]:
            flat = jnp.pad(flat, (0, rows * width - flat.shape[0]))
        parts.append(flat.reshape(rows, width))
        spans.append((r0, rows))
        r0 += rows
    return jnp.concatenate(parts, axis=0), spans


def _unpack_rows(slab, spans, shapes):
    out = []
    for (r0, rows), shp in zip(spans, shapes):
        n = math.prod(shp)
        out.append(slab[r0:r0 + rows].reshape(-1)[:n].reshape(shp))
    return out


def _two_d(a):
    return a.reshape(-1, a.shape[-1])


def _shard_rows(n, a):
    return _two_d(jnp.swapaxes(a, -1, -2) if n in _TRANSPOSED else a)


def _from_shard_rows(n, rows, shape):
    if n in _TRANSPOSED:
        return jnp.swapaxes(rows.reshape(shape[:-2] + (shape[-1], shape[-2])), -1, -2)
    return rows.reshape(shape)


def _local_slab(local, slab, dtype):
    parts = sorted((r0, n, l) for n, (_, where) in _PLACE.items() for l, (s, r0) in enumerate(where) if s == slab)
    shards = [_shard_rows(n, local[n] if len(_PLACE[n][1]) == 1 else local[n][l]) for _, n, l in parts]
    return jnp.concatenate([a.astype(dtype) for a in shards], axis=0)


def _set_diag(b, pattern):
    return jnp.einsum(pattern, b, jnp.eye(C_GROUPS // N_SETS, dtype=b.dtype))


def _s5_discretize(lam_re, lam_im, log_dt, b_re, b_im):
    dt = jnp.exp(log_dt)[:, None]
    mag = jnp.exp(lam_re * dt)
    ar = mag * jnp.cos(lam_im * dt)
    ai = mag * jnp.sin(lam_im * dt)
    den = lam_re * lam_re + lam_im * lam_im
    qr = ((ar - 1.0) * lam_re + ai * lam_im) / den
    qi = (ai * lam_re - (ar - 1.0) * lam_im) / den
    bbr = qr[..., None] * b_re - qi[..., None] * b_im
    bbi = qr[..., None] * b_im + qi[..., None] * b_re
    return ar, ai, bbr, bbi


def _attention_block(x, mem, W, w, i, tag):
    xn, q = _norm_mm(x, w["ca_norm"][i], _shards(W, "ca_wq", i), split="k", out_dtype=BF16, name=f"{tag}_q")
    memn = _rms_fwd(mem, w["ca_mem_norm"][i], name=f"{tag}_ca_memnorm")
    k = _mm_k(memn, _shards(W, "ca_wk", i), out_dtype=BF16, name=f"{tag}_k")
    v = _mm_k(memn, _shards(W, "ca_wv", i), out_dtype=BF16, name=f"{tag}_v")
    o = _attn_fwd(q, k, v, name=f"{tag}_attn")
    y = _mm_k(o, _shards(W, "ca_wo", i), add=x, name=f"{tag}_wo")
    return y, (x, xn, memn, q, k, v, o)


def _attention_block_bwd(dy, saved, mem, W, w, i, tag, G, grads, token=None, mid=None):
    x, xn, memn, q, k, v, o = saved
    gain = w["ca_norm"][i]
    if token is not None:
        k = _behind(k, token)
    G = _grad_to_slab(G, "ca_wo", i, o, dy, a_cols=256, name=f"{tag}_dwo")
    dq, dk, dv = _attn_bwd(dy, _shards(W, "ca_wo", i), q, k, v, name=f"{tag}_attn_bwd")
    token = mid(dq) if mid is not None else None
    if token is not None:
        gain = _behind(gain, token)
    G = _grad_to_slab(G, "ca_wq", i, xn, dq, a_cols=256, name=f"{tag}_dwq")
    G = _grad_to_slab(G, "ca_wk", i, memn, dk, a_cols=256, name=f"{tag}_dwk")
    G = _grad_to_slab(G, "ca_wv", i, memn, dv, a_cols=256, name=f"{tag}_dwv")
    dmemn = _mm_k_t([(dk, _shards(W, "ca_wk", i)), (dv, _shards(W, "ca_wv", i))], name=f"{tag}_dmemn")
    dx, dg = _norm_bwd_k(dq, _shards(W, "ca_wq", i), x, gain, dy, name=f"{tag}_dq_norm_bwd")
    grads["ca_norm"][i] = dg[0]
    grads["ca_mem_norm"][i] = _rms_dg(mem, w["ca_mem_norm"][i], dmemn, name=f"{tag}_ca_memnorm_bwd")[0]
    return dx, G


def _ffn_block(x, W, w, i, tag):
    fn, gate, up, h = _ffn_up(x, w["ffn_norm"][i], _shards(W, "ffn_w_gate", i), _shards(W, "ffn_w_up", i),
                              name=f"{tag}_ffn_up")
    y = _mm_k(h, _shards(W, "ffn_w_down", i), add=x, name=f"{tag}_down")
    return y, (x, fn, gate, up, h)


def _ffn_block_bwd(dy, saved, W, w, i, tag, G, grads, token=None, mid=None):
    x, fn, gate, up, h = saved
    gain = w["ffn_norm"][i]
    G = _grad_to_slab(G, "ffn_w_down", i, h, dy, name=f"{tag}_dwd")
    dg, du = _ffn_bwd_hidden(dy, _shards(W, "ffn_w_down", i), gate, up, token, name=f"{tag}_ffn_bwd_hidden")
    token = mid(dg) if mid is not None else None
    if token is not None:
        gain = _behind(gain, token)
    G = _grad_to_slab(G, "ffn_w_gate", i, dg, fn, name=f"{tag}_dwg")
    G = _grad_to_slab(G, "ffn_w_up", i, du, fn, name=f"{tag}_dwu")
    dx, dgn = _ffn_in_bwd(dg, du, _shards(W, "ffn_w_gate", i), _shards(W, "ffn_w_up", i), x, gain, dy,
                          name=f"{tag}_ffn_in_bwd")
    grads["ffn_norm"][i] = dgn[0]
    return dx, G


def _gmlp_mask():
    chunk = jnp.arange(GMLP_BLOCK) // CHUNK
    return chunk[None, :] <= chunk[:, None]


def _even_block(x, W, w, tag):
    hn, proj = _norm_mm(x, w["e_norm"][0], _shards(W, "e_w_in"), split="n", out_dtype=F32, name=f"{tag}_w_in")
    wm = jnp.where(_gmlp_mask()[None], w["e_gmlp_w"][0], 0.0).astype(BF16)
    bcol = w["e_gmlp_b"][0][:, :, None]
    cw = jnp.pad(w["e_conv_w"][0], ((0, CONV_HALO - CONV_WIDTH), (0, 0)))
    cb, lg, lb = w["e_conv_b"], w["e_conv_ln_g"], w["e_conv_ln_b"]
    mix, hc = _even_fwd(proj, wm, bcol, cw, cb, lg, lb, name=f"{tag}_mixers")
    y = _mm_k(mix, _shards(W, "e_w_out"), add=x, name=f"{tag}_w_out")
    return y, (x, hn, proj, mix, hc, wm, bcol, cw)


def _even_block_bwd(dy, saved, W, w, tag, G, grads):
    x, hn, proj, mix, hc, wm, bcol, cw = saved
    dmix = _mm_k_t([(dy, _shards(W, "e_w_out"))], name=f"{tag}_dmix")
    G = _grad_to_slab(G, "e_w_out", 0, mix, dy, a_cols=256, name=f"{tag}_dw_out")
    wmt = jnp.swapaxes(wm, 1, 2)
    dpa, dhc, dwm, db, dlg, dlb, dcb = _even_bwd1(proj, dmix, hc, wm, wmt, bcol, w["e_conv_ln_g"], w["e_conv_ln_b"],
                                                  name=f"{tag}_mixers_bwd1")
    dpb, dcw = _even_bwd2(proj, dhc, cw, name=f"{tag}_mixers_bwd2")
    grads["e_gmlp_w"] = jnp.where(_gmlp_mask()[None], dwm, 0.0)[None]
    grads["e_gmlp_b"] = db[:, :, 0][None]
    grads["e_conv_ln_g"], grads["e_conv_ln_b"], grads["e_conv_b"] = dlg, dlb, dcb
    grads["e_conv_w"] = dcw[:CONV_WIDTH][None]
    G = _grad_to_slab(G, "e_w_in", 0, hn, dpa, b_cols=512, chips=(0, 2), name=f"{tag}_dw_in_a")
    G = _grad_to_slab(G, "e_w_in", 0, hn, dpb, b_cols=512, chips=(2, 2), name=f"{tag}_dw_in_b")
    dx, dg = _norm_bwd_n((dpa, dpb), _shards(W, "e_w_in"), x, w["e_norm"][0], dy, name=f"{tag}_in_bwd")
    grads["e_norm"] = dg
    return dx, G


def _odd_block(x, W, w, tag):
    S = x.shape[0]
    hn, u = _norm_mm(x, w["o_norm"][0], _shards(W, "o_w_in"), split="k", out_dtype=F32, name=f"{tag}_w_in")
    disc_in = (w["o_lam_re"][0], w["o_lam_im"][0], w["o_log_dt"][0], w["o_b_re"][0], w["o_b_im"][0])
    (ar, ai, bbr, bbi), disc_vjp = jax.vjp(_s5_discretize, *disc_in)
    sets = (N_SETS, C_GROUPS // N_SETS)
    per_set = N_STATE // N_SETS
    bset = jnp.concatenate([_set_diag(b.reshape(sets + b.shape[1:]), "jgpc,gh->jgchp").reshape(N_SETS, SET_CH, per_set)
                            for b in (bbr, bbi)], axis=2).astype(BF16)
    cset = jnp.concatenate([_set_diag(c.reshape(sets + c.shape[1:]), "jgcp,gh->jgphc").reshape(N_SETS, per_set, SET_CH)
                            for c in (w["o_c_re"][0], -w["o_c_im"][0])], axis=1).astype(BF16)
    powers, pr, pi = [], ar, ai
    for _ in range(SCAN_BLOCK):
        powers.append(jnp.concatenate([pr.reshape(STATE_ROWS, STATE_LANES), pi.reshape(STATE_ROWS, STATE_LANES)], 0))
        pr, pi = pr * ar - pi * ai, pr * ai + pi * ar
    pw = jnp.stack(powers, axis=0)
    state_rows = (S * _STATE_TILE, STATE_LANES)
    bu = _mm_to_state(u, bset, name=f"{tag}_bu")
    xs = _scan_fwd(bu.reshape(state_rows), pw, name=f"{tag}_scan").reshape(S // 8, STATE_ROWS, 8, STATE_LANES)
    yv, yg = _s5_readout(xs, cset, u, w["o_d"], name=f"{tag}_readout")
    o, y = _glu_out(yg, _shards(W, "o_w_out"), x, name=f"{tag}_glu_out")
    return y, (x, hn, u, bset, cset, pw, xs, yv, yg, o, disc_vjp)


def _odd_block_bwd(dy, saved, W, w, tag, G, grads):
    x, hn, u, bset, cset, pw, xs, yv, yg, o, disc_vjp = saved
    S = x.shape[0]
    state_rows = (S * _STATE_TILE, STATE_LANES)
    do, dys, dus, dd = _glu_out_bwd(o, dy, _shards(W, "o_w_out"), yv, u, w["o_d"], name=f"{tag}_glu_out_bwd")
    G = _grad_to_slab(G, "o_w_out", 0, yg, do, b_cols=512, name=f"{tag}_dw_out")
    grads["o_d"] = dd
    dxs = _mm_to_state(dys, cset, nt=True, name=f"{tag}_dxs")
    dcset_t = _state_grad_sets(dys, xs, name=f"{tag}_dcd")
    gs, da = _scan_bwd(dxs.reshape(state_rows), xs.reshape(S * STATE_ROWS, STATE_LANES), pw, name=f"{tag}_scan_bwd")
    gs = gs.reshape(xs.shape)
    dbset = _state_grad_sets(u, gs, name=f"{tag}_dbd")
    du, dx, dg = _s5_in_bwd(gs, bset, dus, _shards(W, "o_w_in"), x, w["o_norm"][0], dy, name=f"{tag}_in_bwd")
    G = _grad_to_slab(G, "o_w_in", 0, hn, du, a_cols=256, name=f"{tag}_dw_in")
    grads["o_norm"] = dg
    per = C_GROUPS // N_SETS
    blocks = (N_SETS, per, C_GROUP_CH, 2, per, C_STATE)
    dc = _set_diag(dcset_t.reshape(blocks), "jhcrgp,gh->rjgcp").reshape(2, C_GROUPS, C_GROUP_CH, C_STATE)
    db = _set_diag(dbset.reshape(blocks), "jgcrhp,gh->rjgpc").reshape(2, C_GROUPS, C_STATE, C_GROUP_CH)
    dcr, dci, dbbr, dbbi = dc[0], -dc[1], db[0], db[1]
    dar = da[:STATE_ROWS].reshape(C_GROUPS, C_STATE)
    dai = da[STATE_ROWS:].reshape(C_GROUPS, C_STATE)
    dlr, dli, dldt, dbr, dbi = disc_vjp((dar, dai, dbbr, dbbi))
    grads["o_lam_re"], grads["o_lam_im"], grads["o_log_dt"] = dlr[None], dli[None], dldt[None]
    grads["o_b_re"], grads["o_b_im"], grads["o_c_re"], grads["o_c_im"] = dbr[None], dbi[None], dcr[None], dci[None]
    return dx, G


def _behind(value, token):
    return value + token[0, 0].astype(value.dtype)


class _NoExchange:
    def __init__(self, W):
        self.W = W

    def first_weights(self, w):
        return self.W, w

    def weights(self, stage, after):
        return {}

    def grads_ready(self, piece, G):
        return None

    def grads_crossed(self, piece, after):
        return None


def _forward_backward(xs_, mems_, tgt, w, G, exchange):
    W, w = exchange.first_weights(w)
    x1, s_mix0 = _even_block(xs_, W, w, "l0")
    W = {**W, **exchange.weights(1, x1)}
    x2, s_att0 = _attention_block(x1, mems_, W, w, 0, "l0")
    W = {**W, **exchange.weights(2, x2)}
    x3, s_ffn0 = _ffn_block(x2, W, w, 0, "l0")
    W = {**W, **exchange.weights(3, x3)}
    x4, s_mix1 = _odd_block(x3, W, w, "l1")
    x5, s_att1 = _attention_block(x4, mems_, W, w, 1, "l1")
    x6, s_ffn1 = _ffn_block(x5, W, w, 1, "l1")
    dx, dfinal, loss_lanes = _loss_head(x6, w["final_norm"], tgt, name="loss_head")

    grads = {n: [None, None] for n in ("ca_norm", "ca_mem_norm", "ffn_norm")}
    grads["final_norm"] = dfinal[0]
    dx, G = _ffn_block_bwd(dx, s_ffn1, W, w, 1, "l1", G, grads)
    dx, G = _attention_block_bwd(dx, s_att1, mems_, W, w, 1, "l1", G, grads)
    dx, G = _odd_block_bwd(dx, s_mix1, W, w, "l1", G, grads)
    token = exchange.grads_ready("l1", G)
    dx, G = _ffn_block_bwd(dx, s_ffn0, W, w, 0, "l0", G, grads, token,
                           lambda after: exchange.grads_crossed("l1", after))
    token = exchange.grads_ready("ffn0", G)
    dx, G = _attention_block_bwd(dx, s_att0, mems_, W, w, 0, "l0", G, grads, token,
                                 lambda after: exchange.grads_crossed("ffn0", after))
    dx, G = _even_block_bwd(dx, s_mix0, W, w, "l0", G, grads)
    for n in list(grads):
        if isinstance(grads[n], list):
            grads[n] = jnp.stack(grads[n], axis=0)
        grads[n] = grads[n].reshape(w[n].shape)
    return loss_lanes, dx, G, grads


class _Exchange:
    def __init__(self, local, chip, core):
        self.bufs = {s: lax.dynamic_update_slice(lax.empty((N_CHIPS, rows, width), BF16),
                                                 _local_slab(local, s, BF16)[None], (chip, 0, 0))
                     for s, (width, rows) in _SLABS.items()}
        self.half = core.reshape(1).astype(jnp.int32)
        self.where = jnp.stack([chip, core]).astype(jnp.int32)
        self.flights = []
        self.reduces = {}

    def weights(self, stage, after):
        send_sems, recv_sems, bufs, _ = self.flights[stage]
        bufs = _gather_ici_wait(send_sems, recv_sems, bufs, after, name=f"gather_stage{stage}_wait")
        return dict(zip(_STAGES[stage], _gather_forward(bufs, name=f"gather_stage{stage}_forward")))

    def first_weights(self, w):
        after = w["e_conv_w"].reshape(-1)[:STATE_LANES]
        for k, stage in enumerate(_STAGES):
            self.flights.append(_gather_ici_start([self.bufs[s] for s in stage], after, name=f"gather_stage{k}_start"))
            after = self.flights[-1][3]
        return self.weights(0, after), {**w, "e_norm": _behind(w["e_norm"], after)}

    def pair_start(self, G, slabs, tag):
        send_sems, recv_sems, gl, lands, token = _pair_exchange_start([G[s] for s in slabs],
                                                                      name=f"grad_{tag}_pair_start")
        return (slabs, send_sems, recv_sems, gl, lands), token

    def pair_land(self, state, after, tag):
        slabs, send_sems, recv_sems, gl, lands = state
        gl, other = _pair_exchange_wait(send_sems, recv_sems, gl, lands, after, name=f"grad_{tag}_pair_wait")
        pairs = [_pair_sum(g, r, self.half, name=f"grad_pair_sum_{s}") for s, g, r in zip(slabs, gl, other)]
        send_sems, recv_sems, pairs, lands, token = _chip_exchange_start(pairs, name=f"grad_{tag}_chip_start")
        return (slabs, gl, other, send_sems, recv_sems, pairs, lands), token

    def reduce_finish(self, state, after, tag):
        slabs, gl, other, send_sems, recv_sems, pairs, lands = state
        slots = _chip_exchange_wait(send_sems, recv_sems, pairs, lands, after, name=f"grad_{tag}_chip_wait")
        halves = [_chip_sum(g, r, sl, self.where, name=f"grad_chip_sum_{s}")
                  for s, g, r, sl in zip(slabs, gl, other, slots)]
        return dict(zip(slabs, _pair_share(halves, name=f"grad_{tag}_pair_share")))

    def grads_ready(self, piece, G):
        self.reduces[piece], token = self.pair_start(G, _GRAD_PIECES[piece], piece)
        return token

    def grads_crossed(self, piece, after):
        self.reduces[piece], token = self.pair_land(self.reduces[piece], after, piece)
        return token


def kernel(x, mem, e_norm, e_w_in, e_gmlp_w, e_gmlp_b, e_conv_w, e_conv_b, e_conv_ln_g, e_conv_ln_b, e_w_out, o_norm, o_w_in, o_lam_re, o_lam_im, o_log_dt, o_b_re, o_b_im, o_c_re, o_c_im, o_d, o_w_out, ca_norm, ca_mem_norm, ca_wq, ca_wk, ca_wv, ca_wo, ffn_norm, ffn_w_gate, ffn_w_up, ffn_w_down, final_norm, loss_target, m_e_norm, m_e_w_in, m_e_gmlp_w, m_e_gmlp_b, m_e_conv_w, m_e_conv_b, m_e_conv_ln_g, m_e_conv_ln_b, m_e_w_out, m_o_norm, m_o_w_in, m_o_lam_re, m_o_lam_im, m_o_log_dt, m_o_b_re, m_o_b_im, m_o_c_re, m_o_c_im, m_o_d, m_o_w_out, m_ca_norm, m_ca_mem_norm, m_ca_wq, m_ca_wk, m_ca_wv, m_ca_wo, m_ffn_norm, m_ffn_w_gate, m_ffn_w_up, m_ffn_w_down, m_final_norm, v_e_norm, v_e_w_in, v_e_gmlp_w, v_e_gmlp_b, v_e_conv_w, v_e_conv_b, v_e_conv_ln_g, v_e_conv_ln_b, v_e_w_out, v_o_norm, v_o_w_in, v_o_lam_re, v_o_lam_im, v_o_log_dt, v_o_b_re, v_o_b_im, v_o_c_re, v_o_c_im, v_o_d, v_o_w_out, v_ca_norm, v_ca_mem_norm, v_ca_wq, v_ca_wk, v_ca_wv, v_ca_wo, v_ffn_norm, v_ffn_w_gate, v_ffn_w_up, v_ffn_w_down, v_final_norm):
    args = dict(locals())
    local = {n: args[n] for n in _WEIGHTS}
    mom = {n: args["m_" + n] for n in _WEIGHTS}
    vel = {n: args["v_" + n] for n in _WEIGHTS}
    chip = 2 * lax.axis_index("x") + lax.axis_index("y")
    core = lax.axis_index("c")
    xs_, mems_, tgt = x[0], mem[0], loss_target[0]

    w = {n: local[n] for n in _REPLICATED}
    sm_slab, sm_spans = _pack_rows([local[n] for n, _ in _SMALL_SHARDED], SMALL_W, F32)
    sm_all = _allgather_small(sm_slab, name="gather_small_weights").reshape(N_DEV, -1, SMALL_W)
    for (n, ax), span in zip(_SMALL_SHARDED, sm_spans):
        shp = local[n].shape
        w[n] = jnp.concatenate([_unpack_rows(sm_all[2 * p], [span], [shp])[0] for p in range(N_CHIPS)], axis=ax)

    exchange = _Exchange(local, chip, core)
    G = {s: lax.empty((N_CHIPS, rows, width), F32) for s, (width, rows) in _SLABS.items()}
    loss_lanes, dx, G, grads = _forward_backward(xs_, mems_, tgt, w, G, exchange)

    gs_slab, gs_spans = _pack_rows([grads[n] for n in _SMALL] + [loss_lanes], SMALL_W, F32)
    small_flight = _all_to_all_start(gs_slab, dx, name="small_grads_start")
    exchange.grads_ready("rest0", G)
    gsum = exchange.reduce_finish(exchange.reduces["l1"], small_flight[4], "l1")
    gsum = {**gsum, **exchange.reduce_finish(exchange.reduces["ffn0"], small_flight[4], "ffn0")}
    token = exchange.grads_crossed("rest0", gsum["B0"])

    gs_slab, gs_all = _all_to_all_wait(*small_flight[:4], token, name="small_grads_wait")
    gs_all = lax.dynamic_update_slice(gs_all, gs_slab[None], (2 * chip + core, 0, 0))
    gs_sum = _sum_slots(gs_all, name="small_grad_sum")
    *small_sums, loss_sum = _unpack_rows(gs_sum, gs_spans, [grads[n].shape for n in _SMALL] + [loss_lanes.shape])
    out_grads = dict(zip(_SMALL, small_sums))
    for n, ax in _SMALL_SHARDED:
        width = local[n].shape[ax]
        out_grads[n] = lax.dynamic_slice_in_dim(out_grads[n], chip * width, width, axis=ax)

    delta, new_m, new_v = {}, {}, {}
    d_, m_, v_ = _adamw_small([_two_d(local[n]) for n in _SMALL], [_two_d(out_grads[n]) for n in _SMALL],
                              [_two_d(mom[n]) for n in _SMALL], [_two_d(vel[n]) for n in _SMALL], name="adamw_small")
    for n, dd, mm_, vv in zip(_SMALL, d_, m_, v_):
        shp = local[n].shape
        delta[n], new_m[n], new_v[n] = dd.reshape(shp), mm_.reshape(shp), vv.reshape(shp)
    def adamw_large(names):
        for n in names:
            shp = local[n].shape
            g_, d_, m_, v_ = _adamw_shard(_shard_rows(n, local[n]), [(gsum[s], r0) for s, r0 in _PLACE[n][1]],
                                          _shard_rows(n, mom[n]), _shard_rows(n, vel[n]), name=f"adamw_{n}")
            out_grads[n], delta[n], new_m[n], new_v[n] = (_from_shard_rows(n, t, shp) for t in (g_, d_, m_, v_))

    ready = [n for n, (_, where) in _PLACE.items() if all(s in gsum for s, _ in where)]
    adamw_large(ready)
    done = jnp.concatenate([delta[n].reshape(-1)[:1] for n in ready + list(_SMALL[:1])])
    gsum = {**gsum, **exchange.reduce_finish(exchange.reduces["rest0"], done, "rest0")}
    adamw_large([n for n in _PLACE if n not in ready])

    return (loss_sum[0, 0], dx[None], *[out_grads[n] for n in _WEIGHTS], *[delta[n] for n in _WEIGHTS],
            *[new_m[n] for n in _WEIGHTS], *[new_v[n] for n in _WEIGHTS])
```

```python
import functools
import math

import jax
import jax.numpy as jnp
from jax import lax
from jax.experimental import pallas as pl
from jax.experimental.pallas import tpu as pltpu

F32 = jnp.float32
BF16 = jnp.bfloat16
MESH = pl.DeviceIdType.MESH

EPS = 1e-6
D_MODEL = 1024
A_WIDTH = 512
A_GROUPS = 4
GMLP_BLOCK = 128
CHUNK = 64
B_WIDTH = 512
CONV_WIDTH = 31
CONV_HALO = 32
C_WIDTH = 512
C_GROUP_CH = 16
C_GROUPS = 32
C_STATE = 64
N_STATE = C_GROUPS * C_STATE
STATE_LANES = 128
STATE_ROWS = N_STATE // STATE_LANES
SCAN_BLOCK = 8
CA_HEADS = 4
CA_HEAD_DIM = 256
FFN_HIDDEN = 2816

ADAM_LR = 0.001
ADAM_B1 = 0.9
ADAM_B2 = 0.999
ADAM_EPS = 1e-08
ADAM_WD = 0.01
ADAM_STEP = 10

VMEM_LIMIT = 56 * 1024 * 1024
ACC_BYTES = 6 * 1024 * 1024
TN_VMEM_BYTES = 44 * 1024 * 1024
SMALL_W = 128
N_CHIPS = 4
N_DEV = 8

_SLABS = {"D0": (512, 1024), "E0": (1024, 256), "A0": (1024, 1024), "B0": (1024, 704), "C0": (1024, 1408),
          "D1": (512, 768), "A1": (1024, 1024), "B1": (1024, 704), "C1": (1024, 1408)}
_STAGES = (("D0", "E0"), ("A0",), ("B0", "C0"), ("D1", "A1", "B1", "C1"))
_GRAD_PIECES = {"l1": _STAGES[3], "ffn0": _STAGES[2], "rest0": _STAGES[0] + _STAGES[1]}
_PLACE = {
    "e_w_in": (1024, (("D0", 0),)), "e_w_out": (256, (("E0", 0),)),
    "o_w_out": (512, (("D1", 0),)), "o_w_in": (256, (("D1", 512),)),
    "ca_wq": (256, (("A0", 0), ("A1", 0))), "ca_wk": (256, (("A0", 256), ("A1", 256))),
    "ca_wv": (256, (("A0", 512), ("A1", 512))), "ca_wo": (256, (("A0", 768), ("A1", 768))),
    "ffn_w_down": (704, (("B0", 0), ("B1", 0))),
    "ffn_w_gate": (704, (("C0", 0), ("C1", 0))), "ffn_w_up": (704, (("C0", 704), ("C1", 704))),
}
_TRANSPOSED = ("ffn_w_gate", "ffn_w_up")


def _params(sem=None):
    return pltpu.CompilerParams(dimension_semantics=sem, vmem_limit_bytes=VMEM_LIMIT)


def _tile(n, pref, mult=128):
    if n <= pref:
        return n
    t = (pref // mult) * mult
    while t >= mult:
        if n % t == 0:
            return t
        t -= mult
    return n


def _blk(name, layer=0):
    rows, where = _PLACE[name]
    slab, r0 = where[layer]
    assert r0 % rows == 0
    return slab, rows, r0 // rows


def _shards(slabs, name, layer=0):
    slab, rows, b = _blk(name, layer)
    return [(slabs[slab], (None, rows, _SLABS[slab][0]), (p, b, 0)) for p in range(N_CHIPS)]


_GELU_C = 0.7978845608028654
_GELU_A = 0.044715


def _gelu(x):
    t = jnp.tanh(_GELU_C * (x + _GELU_A * (x * x * x)))
    return 0.5 * x * (1.0 + t), t


def _gelu_grad(x, t):
    return 0.5 * (1.0 + t) + 0.5 * x * (1.0 - t * t) * (_GELU_C * (1.0 + 3.0 * _GELU_A * x * x))


def _sigmoid(x):
    return 1.0 / (1.0 + jnp.exp(-x))


def _mean(x):
    return jnp.mean(x, axis=-1, keepdims=True)


def _dot(a, b):
    return jnp.dot(a, b, preferred_element_type=F32)


def _dot_nt(a, b):
    return lax.dot_general(a, b, (((1,), (1,)), ((), ())), preferred_element_type=F32)


def _dot_tn(a, b):
    return lax.dot_general(a, b, (((0,), (0,)), ((), ())), preferred_element_type=F32)


def _rms_tile(xv, gv):
    return (xv * lax.rsqrt(_mean(xv * xv) + EPS)) * gv


def _rms_bwd_tile(xv, gv, dyv):
    r = lax.rsqrt(_mean(xv * xv) + EPS)
    xh = xv * r
    dyg = dyv * gv
    return r * (dyg - xh * _mean(dyg * xh)), jnp.sum(dyv * xh, axis=0, keepdims=True)


def _cols(p, width):
    return slice(p * width, (p + 1) * width)


def _sum_k(a, ws, k):
    tot = None
    for p in range(N_CHIPS):
        y = _dot(a[:, _cols(p, k)], ws[p][...])
        tot = y if tot is None else tot + y
    return tot


def _cat_nt(a, ws):
    return jnp.concatenate([_dot_nt(a, ws[p][...]) for p in range(N_CHIPS)], axis=1)


def _rows_call(name, tm, rows, fulls, outs, accs, body, scratch=()):
    S = min(x.shape[-2] for x in rows if x.ndim != 4)
    nr, nf, no, na = len(rows), len(fulls), len(outs), len(accs)

    def kern(*refs):
        r, f = refs[:nr], refs[nr:nr + nf]
        o, a = refs[nr + nf:nr + nf + no], refs[nr + nf + no:nr + nf + no + na]
        if na:
            @pl.when(pl.program_id(0) == 0)
            def _():
                for ref in a:
                    ref[...] = jnp.zeros_like(ref)
        body(r, f, o, a, refs[nr + nf + no + na:])

    def whole(shape):
        nd = len(shape)
        return pl.BlockSpec(tuple(shape), lambda i: (0,) * nd)

    def row_spec(shape):
        if len(shape) == 4:
            return pl.BlockSpec((tm // 8,) + tuple(shape[1:]), lambda i: (i, 0, 0, 0))
        if len(shape) == 3:
            return pl.BlockSpec((shape[0], tm, shape[2]), lambda i: (0, i, 0))
        return pl.BlockSpec((tm, shape[1]), lambda i: (i, 0))

    def full_spec(x):
        if isinstance(x, tuple):
            _, bshape, bidx = x
            return pl.BlockSpec(bshape, lambda i: bidx, pipeline_mode=pl.Buffered(1))
        return whole(x.shape)

    out_shapes = [(S, o[0]) if len(o) == 2 else (o[0], S, o[1]) for o in outs]
    res = pl.pallas_call(
        kern, name=name, grid=(S // tm,),
        in_specs=[row_spec(x.shape) for x in rows] + [full_spec(x) for x in fulls],
        out_specs=[row_spec(s) for s in out_shapes] + [whole(shp) for shp, _ in accs],
        out_shape=[jax.ShapeDtypeStruct(s, o[-1]) for s, o in zip(out_shapes, outs)]
        + [jax.ShapeDtypeStruct(tuple(shp), dt) for shp, dt in accs],
        scratch_shapes=list(scratch),
        compiler_params=_params(("arbitrary",) if na else ("parallel",)),
    )(*rows, *[x[0] if isinstance(x, tuple) else x for x in fulls])
    return res[:no], res[no:]


def _grad_to_slab(gslabs, wname, layer, a, b, *, a_cols=None, b_cols=None, chips=(0, N_CHIPS), name):
    slab, rows, bidx = _blk(wname, layer)
    width = _SLABS[slab][0]
    p0, n_p = chips
    assert p0 % n_p == 0
    S = a.shape[-2]

    def tile_bytes(x, ts):
        return ts * x.dtype.itemsize * (x.shape[2] * n_p if x.ndim == 3 else x.shape[1])

    acc_bytes = n_p * rows * (-(-width // 128) * 128) * 4
    ts = next(t for t in (2048, 1024, 512, 256, S) if S % t == 0
              and 2 * (tile_bytes(a, t) + tile_bytes(b, t) + acc_bytes) <= TN_VMEM_BYTES or t == S)

    def operand(x):
        if x.ndim == 3:
            return pl.BlockSpec((n_p, ts, x.shape[2]), lambda s: (p0 // n_p, s, 0))
        return pl.BlockSpec((ts, x.shape[1]), lambda s: (s, 0))

    def part(ref, cols, p):
        if len(ref.shape) == 3:
            return ref[p]
        return ref[...] if cols is None else ref[:, _cols(p, cols)]

    def body(a_ref, b_ref, slab_ref, o_ref):
        @pl.when(pl.program_id(0) == 0)
        def _():
            o_ref[...] = jnp.zeros_like(o_ref)

        for p in range(n_p):
            o_ref[p] += _dot_tn(part(a_ref, a_cols, p).astype(BF16), part(b_ref, b_cols, p).astype(BF16))

    g = gslabs[slab]
    out = pl.pallas_call(
        body, name=name, grid=(S // ts,),
        in_specs=[operand(a), operand(b), pl.BlockSpec(memory_space=pl.ANY)],
        out_specs=pl.BlockSpec((n_p, rows, width), lambda s: (p0 // n_p, bidx, 0)),
        out_shape=jax.ShapeDtypeStruct(g.shape, F32), input_output_aliases={2: 0},
        compiler_params=_params(("arbitrary",)),
    )(a, b, g)
    return {**gslabs, slab: out}


def _vec(g):
    return g.reshape(1, -1)


def _norm_mm(x, g, ws, *, split, out_dtype, name, tm=512):
    S, D = x.shape
    k, n = ws[0][1][1], ws[0][1][2]
    N = n if split == "k" else N_CHIPS * n

    def body(r, f, o, acc, s):
        xn = _rms_tile(r[0][...], f[0][...]).astype(BF16)
        o[0][...] = xn
        if split == "k":
            o[1][...] = _sum_k(xn, f[1:], k).astype(out_dtype)
        else:
            for p in range(N_CHIPS):
                o[1][:, _cols(p, n)] = _dot(xn, f[1 + p][...]).astype(out_dtype)

    (xn, y), _ = _rows_call(name, _tile(S, tm), [x], [_vec(g)] + ws, [(D, BF16), (N, out_dtype)], [], body)
    return xn, y


def _mm_k(a, ws, *, add=None, out_dtype=F32, name, tm=512):
    S = a.shape[-2]
    k, n = ws[0][1][1], ws[0][1][2]
    has_add = add is not None

    def body(r, f, o, acc, s):
        if a.ndim == 3:
            y = None
            for p in range(N_CHIPS):
                t = _dot(r[0][p].astype(BF16), f[p][...])
                y = t if y is None else y + t
        else:
            y = _sum_k(r[0][...].astype(BF16), f, k)
        if has_add:
            y = y + r[1][...]
        o[0][...] = y.astype(out_dtype)

    (y,), _ = _rows_call(name, _tile(S, tm), [a] + ([add] if has_add else []), ws, [(n, out_dtype)], [], body)
    return y


def _mm_k_t(terms, *, out_dtype=F32, name, tm=512):
    S = terms[0][0].shape[0]
    k = terms[0][1][0][1][1]

    def body(r, f, o, acc, s):
        y = None
        for t in range(len(terms)):
            yt = _cat_nt(r[t][...].astype(BF16), f[N_CHIPS * t:N_CHIPS * (t + 1)])
            y = yt if y is None else y + yt
        o[0][...] = y.astype(out_dtype)

    (y,), _ = _rows_call(name, _tile(S, tm), [a for a, _ in terms], [w for _, ws in terms for w in ws],
                         [(N_CHIPS * k, out_dtype)], [], body)
    return y


def _rms_fwd(x, g, *, name):
    def body(r, f, o, acc, s):
        o[0][...] = _rms_tile(r[0][...], f[0][...]).astype(BF16)

    (y,), _ = _rows_call(name, _tile(x.shape[0], 256, 8), [x], [_vec(g)], [(x.shape[1], BF16)], [], body)
    return y


def _rms_dg(x, g, dy, *, name):
    def body(r, f, o, acc, s):
        acc[0][...] += _rms_bwd_tile(r[0][...], f[0][...], r[1][...])[1]

    _, (dg,) = _rows_call(name, _tile(x.shape[0], 256, 8), [x, dy], [_vec(g)], [], [((1, x.shape[1]), F32)], body)
    return dg


def _ffn_up(x, g, wg, wu, *, name, tm=256):
    S, D = x.shape
    h = wg[0][1][1]

    def body(r, f, o, acc, s):
        xn = _rms_tile(r[0][...], f[0][...]).astype(BF16)
        o[0][...] = xn
        for p in range(N_CHIPS):
            gate = _dot_nt(xn, f[1 + p][...])
            up = _dot_nt(xn, f[1 + N_CHIPS + p][...])
            o[1][p] = gate.astype(BF16)
            o[2][p] = up.astype(BF16)
            o[3][p] = (gate * _sigmoid(gate) * up).astype(BF16)

    (xn, gate, up, hid), _ = _rows_call(name, _tile(S, tm), [x], [_vec(g)] + wg + wu,
                                        [(D, BF16), (N_CHIPS, h, BF16), (N_CHIPS, h, BF16), (N_CHIPS, h, BF16)], [],
                                        body)
    return xn, gate, up, hid


def _ffn_bwd_hidden(dy, wd, gate, up, token=None, *, name, tm=256):
    S = dy.shape[0]
    h = wd[0][1][1]

    def body(r, f, o, acc, s):
        dyv = r[0][...]
        if token is not None:
            dyv = dyv + jnp.sum(f[N_CHIPS][...])
        dyb = dyv.astype(BF16)
        for p in range(N_CHIPS):
            dh = _dot_nt(dyb, f[p][...])
            gv = r[1][p].astype(F32)
            sg = _sigmoid(gv)
            o[0][p] = (dh * r[2][p].astype(F32) * (sg * (1.0 + gv * (1.0 - sg)))).astype(BF16)
            o[1][p] = (dh * gv * sg).astype(BF16)

    (dg, du), _ = _rows_call(name, _tile(S, tm), [dy, gate, up], wd + ([] if token is None else [token]),
                             [(N_CHIPS, h, BF16), (N_CHIPS, h, BF16)], [], body)
    return dg, du


def _ffn_in_bwd(dg, du, wg, wu, x, g, dres, *, name, tm=256):
    S, D = x.shape

    def body(r, f, o, acc, s):
        tot = None
        for p in range(N_CHIPS):
            y = _dot(r[0][p], f[1 + p][...]) + _dot(r[1][p], f[1 + N_CHIPS + p][...])
            tot = y if tot is None else tot + y
        dx, dgn = _rms_bwd_tile(r[2][...], f[0][...], tot)
        o[0][...] = dx + r[3][...]
        acc[0][...] += dgn

    (dx,), (dgn,) = _rows_call(name, _tile(S, tm), [dg, du, x, dres], [_vec(g)] + wg + wu, [(D, F32)],
                               [((1, D), F32)], body)
    return dx, dgn


def _norm_bwd_k(da, ws, x, g, dres, *, name, tm=512):
    S, D = x.shape

    def body(r, f, o, acc, s):
        dx, dg = _rms_bwd_tile(r[1][...], f[0][...], _cat_nt(r[0][...].astype(BF16), f[1:]))
        o[0][...] = dx + r[2][...]
        acc[0][...] += dg

    (dx,), (dg,) = _rows_call(name, _tile(S, tm), [da, x, dres], [_vec(g)] + ws, [(D, F32)], [((1, D), F32)], body)
    return dx, dg


def _norm_bwd_n(das, ws, x, g, dres, *, name, tm=256):
    S, D = x.shape
    n = ws[0][1][2]

    def body(r, f, o, acc, s):
        tot = None
        for p in range(N_CHIPS):
            y = _dot_nt(r[p // 2][:, _cols(p % 2, n)], f[1 + p][...])
            tot = y if tot is None else tot + y
        dx, dg = _rms_bwd_tile(r[2][...], f[0][...], tot)
        o[0][...] = dx + r[3][...]
        acc[0][...] += dg

    (dx,), (dg,) = _rows_call(name, _tile(S, tm), list(das) + [x, dres], [_vec(g)] + ws, [(D, F32)], [((1, D), F32)],
                              body)
    return dx, dg


def _ln_stats(v):
    mu = _mean(v)
    xc = v - mu
    rstd = lax.rsqrt(_mean(xc * xc) + EPS)
    return xc * rstd, rstd


_SHIFTS = 8
_CONV_ROWS = 64


def _fill_shifts(sh_ref, ext_ref, tm):
    sh_ref[0] = ext_ref[...]
    for s in range(1, _SHIFTS):
        sh_ref[s, 0:tm + CONV_HALO - _SHIFTS, :] = ext_ref[pl.ds(s, tm + CONV_HALO - _SHIFTS), :]


def _window(sh_ref, off, tm):
    return sh_ref[off % _SHIFTS, pl.ds(off - off % _SHIFTS, tm), :]


def _even_fwd(proj, wm, bcol, cw, cb, lg, lb, *, name):
    S = proj.shape[0]
    tm = _tile(S, 256)
    hb = tm // CONV_HALO
    nblk = tm // GMLP_BLOCK

    def body(p_ref, halo_ref, wm_ref, b_ref, cw_ref, cb_ref, lg_ref, lb_ref, mix_ref, hc_ref, hext_ref, hsh_ref):
        i = pl.program_id(0)
        gu, _ = _gelu(p_ref[:, 0:A_WIDTH])
        gv, _ = _gelu(p_ref[:, A_WIDTH:2 * A_WIDTH])
        vn, _ = _ln_stats(gv)
        vnb = vn.astype(BF16)
        for n in range(nblk):
            rows = slice(n * GMLP_BLOCK, (n + 1) * GMLP_BLOCK)
            for g in range(A_GROUPS):
                cols = slice(g * GMLP_BLOCK, (g + 1) * GMLP_BLOCK)
                sg = jnp.dot(wm_ref[g], vnb[rows, cols], preferred_element_type=F32) + b_ref[g]
                mix_ref[rows, cols] = (gu[rows, cols] * sg).astype(BF16)
        h = p_ref[:, 1024:1536] * _sigmoid(p_ref[:, 1536:2048])
        hh = halo_ref[:, 0:B_WIDTH] * _sigmoid(halo_ref[:, B_WIDTH:2 * B_WIDTH])
        hext_ref[0:CONV_HALO, :] = jnp.where(i > 0, hh, 0.0)
        hext_ref[CONV_HALO:CONV_HALO + tm, :] = h
        _fill_shifts(hsh_ref, hext_ref, tm)
        for r0 in range(0, tm, _CONV_ROWS):
            acc = jnp.zeros((_CONV_ROWS, B_WIDTH), F32)
            for k in range(CONV_WIDTH):
                acc = acc + cw_ref[k:k + 1, :] * _window(hsh_ref, r0 + k + CONV_HALO - CONV_WIDTH + 1, _CONV_ROWS)
            hc_ref[r0:r0 + _CONV_ROWS, :] = acc + cb_ref[...]
        hc = hc_ref[...]
        hhat, _ = _ln_stats(hc)
        hl = hhat * lg_ref[...] + lb_ref[...]
        mix_ref[:, A_WIDTH:A_WIDTH + B_WIDTH] = (hl * _sigmoid(hl)).astype(BF16)

    vec = pl.BlockSpec((1, B_WIDTH), lambda i: (0, 0))
    return pl.pallas_call(
        body, name=name, grid=(S // tm,),
        in_specs=[
            pl.BlockSpec((tm, 2048), lambda i: (i, 0)),
            pl.BlockSpec((CONV_HALO, 1024), lambda i: (jnp.maximum(i * hb - 1, 0), 1)),
            pl.BlockSpec((A_GROUPS, GMLP_BLOCK, GMLP_BLOCK), lambda i: (0, 0, 0)),
            pl.BlockSpec((A_GROUPS, GMLP_BLOCK, 1), lambda i: (0, 0, 0)),
            pl.BlockSpec((CONV_HALO, B_WIDTH), lambda i: (0, 0)),
            vec, vec, vec,
        ],
        out_specs=[pl.BlockSpec((tm, 1024), lambda i: (i, 0)), pl.BlockSpec((tm, B_WIDTH), lambda i: (i, 0))],
        out_shape=[jax.ShapeDtypeStruct((S, 1024), BF16), jax.ShapeDtypeStruct((S, B_WIDTH), F32)],
        scratch_shapes=[pltpu.VMEM((tm + CONV_HALO, B_WIDTH), F32),
                        pltpu.VMEM((_SHIFTS, tm + CONV_HALO, B_WIDTH), F32)],
        compiler_params=_params(("parallel",)),
    )(proj, proj, wm, bcol, cw, cb, lg, lb)


def _even_bwd1(proj, dmix, hc, wm, wmt, bcol, lg, lb, *, name):
    S = proj.shape[0]
    tm = _tile(S, 256)
    nblk = tm // GMLP_BLOCK

    def body(p_ref, dm_ref, hc_ref, wm_ref, wmt_ref, b_ref, lg_ref, lb_ref,
             dpa_ref, dhc_ref, dwm_ref, db_ref, dlg_ref, dlb_ref, dcb_ref, dgu_ref, dvn_ref):
        @pl.when(pl.program_id(0) == 0)
        def _():
            dwm_ref[...] = jnp.zeros_like(dwm_ref)
            db_ref[...] = jnp.zeros_like(db_ref)
            dlg_ref[...] = jnp.zeros_like(dlg_ref)
            dlb_ref[...] = jnp.zeros_like(dlb_ref)
            dcb_ref[...] = jnp.zeros_like(dcb_ref)

        au = p_ref[:, 0:A_WIDTH]
        av = p_ref[:, A_WIDTH:2 * A_WIDTH]
        gu, tu = _gelu(au)
        gv, tv = _gelu(av)
        vn, rstd = _ln_stats(gv)
        vnb = vn.astype(BF16)
        for n in range(nblk):
            rows = slice(n * GMLP_BLOCK, (n + 1) * GMLP_BLOCK)
            for g in range(A_GROUPS):
                cols = slice(g * GMLP_BLOCK, (g + 1) * GMLP_BLOCK)
                vb = vnb[rows, cols]
                sg = jnp.dot(wm_ref[g], vb, preferred_element_type=F32) + b_ref[g]
                da = dm_ref[rows, cols]
                dsg = da * gu[rows, cols]
                dgu_ref[rows, cols] = da * sg
                dsgb = dsg.astype(BF16)
                dwm_ref[g] += _dot_nt(dsgb, vb)
                db_ref[g] += jnp.sum(dsg, axis=1, keepdims=True)
                dvn_ref[rows, cols] = jnp.dot(wmt_ref[g], dsgb, preferred_element_type=F32)
        dvn = dvn_ref[...]
        dgv = rstd * (dvn - _mean(dvn) - vn * _mean(dvn * vn))
        dpa_ref[:, 0:A_WIDTH] = (dgu_ref[...] * _gelu_grad(au, tu)).astype(BF16)
        dpa_ref[:, A_WIDTH:2 * A_WIDTH] = (dgv * _gelu_grad(av, tv)).astype(BF16)
        hhat, rstd2 = _ln_stats(hc_ref[...])
        lgv = lg_ref[...]
        hl = hhat * lgv + lb_ref[...]
        s = _sigmoid(hl)
        dhl = dm_ref[:, A_WIDTH:A_WIDTH + B_WIDTH] * (s * (1.0 + hl * (1.0 - s)))
        dlg_ref[...] += jnp.sum(dhl * hhat, axis=0, keepdims=True)
        dlb_ref[...] += jnp.sum(dhl, axis=0, keepdims=True)
        dhh = dhl * lgv
        dhc = rstd2 * (dhh - _mean(dhh) - hhat * _mean(dhh * hhat))
        dcb_ref[...] += jnp.sum(dhc, axis=0, keepdims=True)
        dhc_ref[...] = dhc

    vec = pl.BlockSpec((1, B_WIDTH), lambda i: (0, 0))
    w3 = pl.BlockSpec((A_GROUPS, GMLP_BLOCK, GMLP_BLOCK), lambda i: (0, 0, 0))
    b3 = pl.BlockSpec((A_GROUPS, GMLP_BLOCK, 1), lambda i: (0, 0, 0))
    return pl.pallas_call(
        body, name=name, grid=(S // tm,),
        in_specs=[
            pl.BlockSpec((tm, 1024), lambda i: (i, 0)),
            pl.BlockSpec((tm, 1024), lambda i: (i, 0)),
            pl.BlockSpec((tm, B_WIDTH), lambda i: (i, 0)),
            w3, w3, b3, vec, vec,
        ],
        out_specs=[pl.BlockSpec((tm, 1024), lambda i: (i, 0)), pl.BlockSpec((tm, B_WIDTH), lambda i: (i, 0)),
                   w3, b3, vec, vec, vec],
        out_shape=[
            jax.ShapeDtypeStruct((S, 1024), BF16), jax.ShapeDtypeStruct((S, B_WIDTH), F32),
            jax.ShapeDtypeStruct((A_GROUPS, GMLP_BLOCK, GMLP_BLOCK), F32),
            jax.ShapeDtypeStruct((A_GROUPS, GMLP_BLOCK, 1), F32),
            jax.ShapeDtypeStruct((1, B_WIDTH), F32), jax.ShapeDtypeStruct((1, B_WIDTH), F32),
            jax.ShapeDtypeStruct((1, B_WIDTH), F32),
        ],
        scratch_shapes=[pltpu.VMEM((tm, A_WIDTH), F32), pltpu.VMEM((tm, A_WIDTH), F32)],
        compiler_params=_params(("arbitrary",)),
    )(proj, dmix, hc, wm, wmt, bcol, lg, lb)


def _even_bwd2(proj, dhc, cw, *, name):
    S = proj.shape[0]
    tm = _tile(S, 256)
    hb = tm // CONV_HALO
    nt = S // tm
    last_halo = S // CONV_HALO - 1
    lo = CONV_HALO - CONV_WIDTH + 1

    def body(p_ref, halo_ref, d_ref, dnext_ref, cw_ref, dpb_ref, dcw_ref, hext_ref, dext_ref, hsh_ref, dsh_ref):
        i = pl.program_id(0)

        @pl.when(i == 0)
        def _():
            dcw_ref[...] = jnp.zeros_like(dcw_ref)

        hh = halo_ref[:, 0:B_WIDTH] * _sigmoid(halo_ref[:, B_WIDTH:2 * B_WIDTH])
        hext_ref[0:CONV_HALO, :] = jnp.where(i > 0, hh, 0.0)
        hext_ref[CONV_HALO:CONV_HALO + tm, :] = p_ref[:, 0:B_WIDTH] * _sigmoid(p_ref[:, B_WIDTH:2 * B_WIDTH])
        dext_ref[0:tm, :] = d_ref[...]
        dext_ref[tm:tm + CONV_HALO, :] = jnp.where(i < nt - 1, dnext_ref[...], 0.0)
        _fill_shifts(hsh_ref, hext_ref, tm)
        _fill_shifts(dsh_ref, dext_ref, tm)
        for r0 in range(0, tm, _CONV_ROWS):
            rows = slice(r0, r0 + _CONV_ROWS)
            dhc_b = d_ref[rows, :]
            dh = jnp.zeros((_CONV_ROWS, B_WIDTH), F32)
            for k in range(CONV_WIDTH):
                dh = dh + cw_ref[k:k + 1, :] * _window(dsh_ref, r0 + CONV_WIDTH - 1 - k, _CONV_ROWS)
                dcw_ref[k:k + 1, :] += jnp.sum(dhc_b * _window(hsh_ref, r0 + k + lo, _CONV_ROWS), axis=0,
                                               keepdims=True)
            ba_b = p_ref[rows, 0:B_WIDTH]
            sg_b = _sigmoid(p_ref[rows, B_WIDTH:2 * B_WIDTH])
            dpb_ref[rows, 0:B_WIDTH] = (dh * sg_b).astype(BF16)
            dpb_ref[rows, B_WIDTH:2 * B_WIDTH] = (dh * ba_b * sg_b * (1.0 - sg_b)).astype(BF16)

    return pl.pallas_call(
        body, name=name, grid=(nt,),
        in_specs=[
            pl.BlockSpec((tm, 1024), lambda i: (i, 1)),
            pl.BlockSpec((CONV_HALO, 1024), lambda i: (jnp.maximum(i * hb - 1, 0), 1)),
            pl.BlockSpec((tm, B_WIDTH), lambda i: (i, 0)),
            pl.BlockSpec((CONV_HALO, B_WIDTH), lambda i: (jnp.minimum((i + 1) * hb, last_halo), 0)),
            pl.BlockSpec((CONV_HALO, B_WIDTH), lambda i: (0, 0)),
        ],
        out_specs=[pl.BlockSpec((tm, 1024), lambda i: (i, 0)), pl.BlockSpec((CONV_HALO, B_WIDTH), lambda i: (0, 0))],
        out_shape=[jax.ShapeDtypeStruct((S, 1024), BF16), jax.ShapeDtypeStruct((CONV_HALO, B_WIDTH), F32)],
        scratch_shapes=[pltpu.VMEM((tm + CONV_HALO, B_WIDTH), F32), pltpu.VMEM((tm + CONV_HALO, B_WIDTH), F32),
                        pltpu.VMEM((_SHIFTS, tm + CONV_HALO, B_WIDTH), F32),
                        pltpu.VMEM((_SHIFTS, tm + CONV_HALO, B_WIDTH), F32)],
        compiler_params=_params(("arbitrary",)),
    )(proj, proj, dhc, dhc, cw)


_CA_SCALE = CA_HEAD_DIM ** -0.5


def _softmax_rows(s):
    e = jnp.exp(s - jnp.max(s, axis=-1, keepdims=True))
    return e / jnp.sum(e, axis=-1, keepdims=True)


def _attn_fwd(q, k, v, *, name):
    S = q.shape[0]

    def body(r, f, o, acc, s):
        for h in range(CA_HEADS):
            cols = _cols(h, CA_HEAD_DIM)
            p = _softmax_rows(_dot_nt(r[0][:, cols], f[0][:, cols]) * _CA_SCALE)
            o[0][:, cols] = _dot(p.astype(BF16), f[1][:, cols]).astype(BF16)

    (o_,), _ = _rows_call(name, _tile(S, 512), [q], [k, v], [(D_MODEL, BF16)], [], body)
    return o_


def _attn_bwd(dy, wo, q, k, v, *, name):
    S = q.shape[0]
    M = k.shape[0]

    def body(r, f, o, acc, s):
        dyb = r[0][...].astype(BF16)
        for h in range(CA_HEADS):
            cols = _cols(h, CA_HEAD_DIM)
            qh = r[1][:, cols]
            kh = f[0][:, cols]
            vh = f[1][:, cols]
            doh = _dot_nt(dyb, f[2 + h][...]).astype(BF16)
            p = _softmax_rows(_dot_nt(qh, kh) * _CA_SCALE)
            acc[1][:, cols] += _dot_tn(p.astype(BF16), doh)
            dp = _dot_nt(doh, vh)
            ds = (p * (dp - jnp.sum(dp * p, axis=-1, keepdims=True)) * _CA_SCALE).astype(BF16)
            o[0][:, cols] = _dot(ds, kh).astype(BF16)
            acc[0][:, cols] += _dot_tn(ds, qh)

    (dq,), (dk, dv) = _rows_call(name, _tile(S, 512), [dy, q], [k, v] + wo, [(D_MODEL, BF16)],
                                 [((M, D_MODEL), F32), ((M, D_MODEL), F32)], body)
    return dq, dk, dv


_STATE_TILE = 2 * STATE_ROWS
N_SETS = 4
SET_CH = C_WIDTH // N_SETS
SET_COLS = N_STATE // N_SETS // STATE_LANES


def _set_groups(j):
    return [SET_COLS * j + c for c in range(SET_COLS)] + [STATE_ROWS + SET_COLS * j + c for c in range(SET_COLS)]


def _pack_state(re, im):
    hi = lax.bitcast_convert_type(re.astype(BF16).astype(F32), jnp.uint32)
    lo = lax.bitcast_convert_type(im.astype(BF16).astype(F32), jnp.uint32) >> 16
    return hi | lo


def _unpack_state(word):
    re = lax.bitcast_convert_type(word & jnp.uint32(0xFFFF0000), F32)
    im = lax.bitcast_convert_type(word << 16, F32)
    return re, im


def _state_set(ref, tm, j):
    parts = [_unpack_state(ref[:, SET_COLS * j + c, :, :].reshape(tm, STATE_LANES)) for c in range(SET_COLS)]
    return jnp.concatenate([p[0].astype(BF16) for p in parts] + [p[1].astype(BF16) for p in parts], axis=1)


def _s5_readout(xs, cset, u, d, *, name, tm=256):
    tm = _tile(u.shape[0], tm)

    def body(r, f, o, acc, s):
        y0 = jnp.concatenate([_dot(_state_set(r[0], tm, j), f[0][j]) for j in range(N_SETS)], axis=1)
        y = y0 + f[1][...] * r[1][...]
        o[0][...] = y
        o[1][...] = _gelu(y)[0].astype(BF16)

    (y, yg), _ = _rows_call(name, tm, [xs, u], [cset, d], [(C_WIDTH, F32), (C_WIDTH, BF16)], [], body)
    return y, yg


def _state_grad_sets(a, st, *, name, ts=256):
    ts = _tile(a.shape[0], ts)

    def body(r, f, o, acc, s):
        for j in range(N_SETS):
            acc[0][j] += _dot_tn(r[0][:, _cols(j, SET_CH)].astype(BF16), _state_set(r[1], ts, j))

    _, (out,) = _rows_call(name, ts, [a, st], [], [], [((N_SETS, SET_CH, 2 * N_STATE // N_SETS), F32)], body)
    return out


def _glu_out(yg, ws, x, *, name, tm=512):
    n = ws[0][1][2]

    def body(r, f, o, acc, s):
        ygv = r[0][...]
        ov = [_dot(ygv, f[p][...]) for p in range(N_CHIPS)]
        for p in range(N_CHIPS):
            o[0][:, _cols(p, n)] = ov[p].astype(BF16)
        for p in range(2):
            o[1][:, _cols(p, n)] = r[1][:, _cols(p, n)] + ov[p] * _sigmoid(ov[2 + p])

    (o_, y), _ = _rows_call(name, _tile(x.shape[0], tm), [yg, x], ws, [(2 * D_MODEL, BF16), (D_MODEL, F32)], [], body)
    return o_, y


def _glu_out_bwd(o_, dy, ws, y, u, d, *, name, tm=256):
    n = ws[0][1][2]

    def body(r, f, o, acc, s):
        o1 = r[0][:, 0:D_MODEL].astype(F32)
        sg = _sigmoid(r[0][:, D_MODEL:2 * D_MODEL].astype(F32))
        dyv = r[1][...]
        do1 = (dyv * sg).astype(BF16)
        do2 = (dyv * o1 * sg * (1.0 - sg)).astype(BF16)
        o[0][:, 0:D_MODEL] = do1
        o[0][:, D_MODEL:2 * D_MODEL] = do2
        dyg = None
        for p in range(N_CHIPS):
            t = _dot_nt((do1 if p < 2 else do2)[:, _cols(p % 2, n)], f[1 + p][...])
            dyg = t if dyg is None else dyg + t
        yv = r[2][...]
        dys = dyg * _gelu_grad(yv, _gelu(yv)[1])
        o[1][...] = dys.astype(BF16)
        o[2][...] = f[0][...] * dys
        acc[0][...] += jnp.sum(dys * r[3][...], axis=0, keepdims=True)

    (do, dys, dus), (dd,) = _rows_call(name, _tile(dy.shape[0], tm), [o_, dy, y, u], [d] + ws,
                                       [(2 * D_MODEL, BF16), (C_WIDTH, BF16), (C_WIDTH, F32)], [((1, C_WIDTH), F32)],
                                       body)
    return do, dys, dus, dd


def _s5_in_bwd(gs, bset, dus, ws, x, g, dres, *, name, tm=256):
    D = x.shape[1]
    tm = _tile(x.shape[0], tm)

    def body(r, f, o, acc, s):
        du0 = jnp.concatenate([_dot_nt(_state_set(r[0], tm, j), f[1][j]) for j in range(N_SETS)], axis=1)
        du = (du0 + r[1][...]).astype(BF16)
        o[0][...] = du
        dx, dg = _rms_bwd_tile(r[2][...], f[0][...], _cat_nt(du, f[2:]))
        o[1][...] = dx + r[3][...]
        acc[0][...] += dg

    (du, dx), (dg,) = _rows_call(name, tm, [gs, dus, x, dres], [_vec(g), bset] + ws,
                                 [(C_WIDTH, BF16), (D, F32)], [((1, D), F32)], body)
    return du, dx, dg


_SCAN_CHUNK = 128
_RE = slice(0, STATE_ROWS)
_IM = slice(STATE_ROWS, 2 * STATE_ROWS)
assert SCAN_BLOCK == 8


def _token(g, i, rows):
    return pl.ds(pl.multiple_of(g * (rows * SCAN_BLOCK), rows * SCAN_BLOCK) + i, rows, stride=SCAN_BLOCK)


def _fill_chunk(s3, a_ref, wset, tc, nt):
    for j in range(N_SETS):
        av = a_ref[:, _cols(j, SET_CH)].astype(BF16)
        y = _dot_nt(av, wset[j]) if nt else _dot(av, wset[j])
        for k, c in enumerate(_set_groups(j)):
            s3[:, 8 * c:8 * (c + 1), :] = y[:, _cols(k, STATE_LANES)].reshape(tc // 8, 8, STATE_LANES)


def _chunk_token(s3, g, i):
    return s3[g, pl.ds(i, _STATE_TILE, stride=SCAN_BLOCK), :]


def _scan_fwd(u, bset, pw, *, name):
    S = u.shape[0]
    tc = _tile(S, _SCAN_CHUNK, 8)

    def body(u_ref, bset_ref, pw_ref, xs_ref, st_ref, s3):
        @pl.when(pl.program_id(0) == 0)
        def _():
            st_ref[...] = jnp.zeros_like(st_ref)

        _fill_chunk(s3, u_ref, bset_ref, tc, nt=False)
        ar = pw_ref[0, _RE, :]
        ai = pw_ref[0, _IM, :]

        def block(g, carry):
            xr, xi = carry
            cr = ci = nr = ni = None
            for j in range(SCAN_BLOCK):
                b = _chunk_token(s3, g, j)
                br, bi = b[_RE], b[_IM]
                cr, ci = (br, bi) if j == 0 else (ar * cr - ai * ci + br, ar * ci + ai * cr + bi)
                pr, pi = pw_ref[j, _RE, :], pw_ref[j, _IM, :]
                nr = pr * xr - pi * xi + cr
                ni = pr * xi + pi * xr + ci
                xs_ref[_token(g, j, STATE_ROWS), :] = _pack_state(nr, ni)
            return nr, ni

        xr, xi = lax.fori_loop(0, tc // SCAN_BLOCK, block, (st_ref[_RE, :], st_ref[_IM, :]), unroll=2)
        st_ref[_RE, :] = xr
        st_ref[_IM, :] = xi

    return pl.pallas_call(
        body, name=name, grid=(S // tc,),
        in_specs=[pl.BlockSpec((tc, u.shape[1]), lambda i: (i, 0)), pl.BlockSpec(bset.shape, lambda i: (0, 0, 0)),
                  pl.BlockSpec(pw.shape, lambda i: (0, 0, 0))],
        out_specs=pl.BlockSpec((tc * STATE_ROWS, STATE_LANES), lambda i: (i, 0)),
        out_shape=jax.ShapeDtypeStruct((S * STATE_ROWS, STATE_LANES), jnp.uint32),
        scratch_shapes=[pltpu.VMEM((2 * STATE_ROWS, STATE_LANES), F32),
                        pltpu.VMEM((tc // 8, _STATE_TILE * 8, STATE_LANES), F32)],
        compiler_params=_params(("arbitrary",)),
    )(u, bset, pw)


def _scan_bwd(dys, cset, xs, pw, *, name):
    S = dys.shape[0]
    tc = _tile(S, _SCAN_CHUNK, 8)
    nc = S // tc

    def body(dys_ref, cset_ref, xs_ref, pw_ref, g_ref, da_ref, st_ref, s3):
        @pl.when(pl.program_id(0) == 0)
        def _():
            st_ref[...] = jnp.zeros_like(st_ref)
            da_ref[...] = jnp.zeros_like(da_ref)

        _fill_chunk(s3, dys_ref, cset_ref, tc, nt=True)
        ar = pw_ref[0, _RE, :]
        ai = pw_ref[0, _IM, :]

        def block(k, carry):
            gr, gi, dar, dai = carry
            g = tc // SCAN_BLOCK - 1 - k
            cr = ci = None
            pgr, pgi = gr, gi
            for j in range(SCAN_BLOCK):
                i = SCAN_BLOCK - 1 - j
                xr, xi = _unpack_state(xs_ref[_token(g, i, STATE_ROWS), :])
                dar = dar + pgr * xr + pgi * xi
                dai = dai + pgi * xr - pgr * xi
                d = _chunk_token(s3, g, i)
                dr, di = d[_RE], d[_IM]
                cr, ci = (dr, di) if j == 0 else (ar * cr + ai * ci + dr, ar * ci - ai * cr + di)
                pr, pi = pw_ref[j, _RE, :], pw_ref[j, _IM, :]
                pgr = pr * gr + pi * gi + cr
                pgi = pr * gi - pi * gr + ci
                g_ref[_token(g, i, STATE_ROWS), :] = _pack_state(pgr, pgi)
            return pgr, pgi, dar, dai

        init = (st_ref[_RE, :], st_ref[_IM, :], da_ref[_RE, :], da_ref[_IM, :])
        gr, gi, dar, dai = lax.fori_loop(0, tc // SCAN_BLOCK, block, init, unroll=2)
        st_ref[_RE, :] = gr
        st_ref[_IM, :] = gi
        da_ref[_RE, :] = dar
        da_ref[_IM, :] = dai

    packed = pl.BlockSpec((tc * STATE_ROWS, STATE_LANES), lambda i: (nc - 1 - i, 0))
    vec = pl.BlockSpec((2 * STATE_ROWS, STATE_LANES), lambda i: (0, 0))
    return pl.pallas_call(
        body, name=name, grid=(nc,),
        in_specs=[pl.BlockSpec((tc, dys.shape[1]), lambda i: (nc - 1 - i, 0)),
                  pl.BlockSpec(cset.shape, lambda i: (0, 0, 0)), packed, pl.BlockSpec(pw.shape, lambda i: (0, 0, 0))],
        out_specs=[packed, vec],
        out_shape=[jax.ShapeDtypeStruct(xs.shape, jnp.uint32), jax.ShapeDtypeStruct((2 * STATE_ROWS, STATE_LANES), F32)],
        scratch_shapes=[pltpu.VMEM((2 * STATE_ROWS, STATE_LANES), F32),
                        pltpu.VMEM((tc // 8, _STATE_TILE * 8, STATE_LANES), F32)],
        compiler_params=_params(("arbitrary",)),
    )(dys, cset, xs, pw)


def _loss_head(x, g, target, *, name):
    S, D = x.shape

    def body(r, f, o, acc, s):
        xv = r[0][...]
        gv = f[0][...]
        rs = lax.rsqrt(_mean(xv * xv) + EPS)
        xh = xv * rs
        err = xh * gv - r[1][...]
        acc[1][...] += 0.5 * jnp.sum(_mean(err * err), axis=0, keepdims=True)
        dy = err * (1.0 / D)
        dyg = dy * gv
        o[0][...] = rs * (dyg - xh * _mean(dyg * xh))
        acc[0][...] += jnp.sum(dy * xh, axis=0, keepdims=True)

    (dx,), (dg, loss) = _rows_call(name, _tile(S, 256, 8), [x, target], [_vec(g)], [(D, F32)],
                                   [((1, D), F32), ((1, 128), F32)], body)
    return dx, dg, loss


_ADAM_C1 = 1.0 - ADAM_B1 ** ADAM_STEP
_ADAM_C2 = 1.0 - ADAM_B2 ** ADAM_STEP
_ONE_BLOCK_BYTES = 8 * 1024 * 1024


def _adamw_math(w, g, m, v):
    nm = ADAM_B1 * m + (1.0 - ADAM_B1) * g
    nv = ADAM_B2 * v + (1.0 - ADAM_B2) * (g * g)
    m_hat = nm / _ADAM_C1
    v_hat = nv / _ADAM_C2
    return -ADAM_LR * (m_hat / (jnp.sqrt(v_hat) + ADAM_EPS) + ADAM_WD * w), nm, nv


def _adamw_shard(w, gsrc, m, v, *, name):
    R, C = w.shape
    n_l = len(gsrc)
    rows = R // n_l
    tr = rows
    for _, r0 in gsrc:
        tr = math.gcd(tr, r0) if r0 else tr
    tr = _tile(tr, 256, 8) if tr > 256 else tr
    nb = rows // tr
    assert rows % tr == 0 and all(r0 % tr == 0 for _, r0 in gsrc)

    def body(*refs):
        w_ref, g_refs, (m_ref, v_ref, go_ref, d_ref, nm_ref, nv_ref) = refs[0], refs[1:1 + n_l], refs[1 + n_l:]
        layer = pl.program_id(0) // nb
        gv = g_refs[0][...]
        for l in range(1, n_l):
            gv = jnp.where(layer == l, g_refs[l][...], gv)
        go_ref[...] = gv
        d_ref[...], nm_ref[...], nv_ref[...] = _adamw_math(w_ref[...], gv, m_ref[...], v_ref[...])

    def g_spec(l, r0):
        return pl.BlockSpec((tr, C), lambda i: (r0 // tr + jnp.clip(i - l * nb, 0, nb - 1), 0))

    blk = pl.BlockSpec((tr, C), lambda i: (i, 0))
    out = jax.ShapeDtypeStruct((R, C), F32)
    return pl.pallas_call(
        body, name=name, grid=(R // tr,),
        in_specs=[blk] + [g_spec(l, r0) for l, (_, r0) in enumerate(gsrc)] + [blk, blk], out_specs=[blk] * 4,
        out_shape=[out] * 4, compiler_params=_params(("parallel",)),
    )(w, *[g for g, _ in gsrc], m, v)


def _adamw_small(ws, gs, ms, vs, *, name):
    n = len(ws)

    def body(*refs):
        w_r, g_r, m_r, v_r = refs[:n], refs[n:2 * n], refs[2 * n:3 * n], refs[3 * n:4 * n]
        d_r, nm_r, nv_r = refs[4 * n:5 * n], refs[5 * n:6 * n], refs[6 * n:7 * n]
        for k in range(n):
            d_r[k][...], nm_r[k][...], nv_r[k][...] = _adamw_math(w_r[k][...], g_r[k][...], m_r[k][...], v_r[k][...])

    vm = pl.BlockSpec(memory_space=pltpu.VMEM)
    out = [jax.ShapeDtypeStruct(w.shape, F32) for w in ws]
    res = pl.pallas_call(body, name=name, in_specs=[vm] * (4 * n), out_specs=[vm] * (3 * n), out_shape=out * 3,
                         compiler_params=pltpu.CompilerParams(vmem_limit_bytes=VMEM_LIMIT))(*ws, *gs, *ms, *vs)
    return res[:n], res[n:2 * n], res[2 * n:]


def _sum_slots(x, *, name):
    n, R, C = x.shape
    tr = R if (n + 1) * R * C * 4 <= _ONE_BLOCK_BYTES else _tile(R, 256, 8)

    def body(x_ref, o_ref):
        acc = x_ref[0]
        for k in range(1, n):
            acc = acc + x_ref[k]
        o_ref[...] = acc

    return pl.pallas_call(
        body, name=name, grid=(R // tr,),
        in_specs=[pl.BlockSpec((n, tr, C), lambda i: (0, i, 0))], out_specs=pl.BlockSpec((tr, C), lambda i: (i, 0)),
        out_shape=jax.ShapeDtypeStruct((R, C), F32), compiler_params=_params(("parallel",)),
    )(x)


def _pair_sum(g, r, half, *, name):
    n, R, C = g.shape
    Rh = R // 2
    tr = _tile(Rh, 256, 8)
    nb = Rh // tr

    def body(half_ref, g_ref, r_ref, o_ref):
        o_ref[...] = (g_ref[...] + r_ref[...]).astype(BF16)

    return pl.pallas_call(
        body, name=name,
        grid_spec=pltpu.PrefetchScalarGridSpec(
            num_scalar_prefetch=1, grid=(n, nb),
            in_specs=[pl.BlockSpec((1, tr, C), lambda p, i, h: (p, h[0] * nb + i, 0)),
                      pl.BlockSpec((1, tr, C), lambda p, i, h: (p, i, 0))],
            out_specs=pl.BlockSpec((1, tr, C), lambda p, i, h: (p, i, 0)),
        ),
        out_shape=jax.ShapeDtypeStruct((n, Rh, C), BF16), compiler_params=_params(("parallel", "parallel")),
    )(half, g, r)


def _chip_sum(g, r, slots, where, *, name):
    n, R, C = g.shape
    Rh = R // 2
    tr = _tile(Rh, 256, 8)
    nb = Rh // tr

    def body(w_ref, g_ref, r_ref, s_ref, o_ref):
        acc = g_ref[0] + r_ref[0]
        for k in range(slots.shape[0]):
            acc = acc + s_ref[k].astype(F32)
        o_ref[...] = acc

    return pl.pallas_call(
        body, name=name,
        grid_spec=pltpu.PrefetchScalarGridSpec(
            num_scalar_prefetch=1, grid=(nb,),
            in_specs=[pl.BlockSpec((1, tr, C), lambda i, w: (w[0], w[1] * nb + i, 0)),
                      pl.BlockSpec((1, tr, C), lambda i, w: (w[0], i, 0)),
                      pl.BlockSpec((slots.shape[0], tr, C), lambda i, w: (0, i, 0))],
            out_specs=pl.BlockSpec((tr, C), lambda i, w: (w[1] * nb + i, 0)),
        ),
        out_shape=jax.ShapeDtypeStruct((R, C), F32), compiler_params=_params(("parallel",)),
    )(where, g, r, slots)


ANY = pl.BlockSpec(memory_space=pl.ANY)


def _place():
    return lax.axis_index("x"), lax.axis_index("y"), lax.axis_index("c")


def _other_chips(x, y):
    return [(1 - x, y), (x, 1 - y), (1 - x, 1 - y)]


def _allgather_small(v, *, name):
    R, C = v.shape

    def body(x_ref, out_ref, send_sems, recv_sems, local_sem):
        x, y, c = _place()
        me, sibling = (x, y, c), (x, y, 1 - c)
        chips = _other_chips(x, y)

        def rows(px, py, pc):
            return out_ref.at[pl.ds((4 * px + 2 * py + pc) * R, R), :]

        def copy(k, block, to, src=None):
            return pltpu.make_async_remote_copy(
                src_ref=rows(*block) if src is None else src, dst_ref=rows(*block),
                send_sem=send_sems.at[k], recv_sem=recv_sems.at[k], device_id=to, device_id_type=MESH)

        mine = pltpu.make_async_copy(x_ref, rows(*me), local_sem)
        mine.start()
        first = [copy(0, me, sibling, src=x_ref)]
        first += [copy(1 + j, me, (*chip, c), src=x_ref) for j, chip in enumerate(chips)]
        for cp in first:
            cp.start()
        passed = [copy(4 + j, (*chip, c), sibling) for j, chip in enumerate(chips)]
        for j, chip in enumerate(chips):
            copy(1 + j, (*chip, c), me).wait_recv()
            passed[j].start()
        copy(0, sibling, me).wait_recv()
        for j, chip in enumerate(chips):
            copy(4 + j, (*chip, 1 - c), me).wait_recv()
        for cp in first + passed:
            cp.wait_send()
        mine.wait()

    return pl.pallas_call(
        body, name=name, out_shape=jax.ShapeDtypeStruct((N_DEV * R, C), v.dtype),
        in_specs=[pl.BlockSpec(memory_space=pltpu.VMEM)], out_specs=pl.BlockSpec(memory_space=pltpu.VMEM),
        scratch_shapes=[pltpu.SemaphoreType.DMA((7,)), pltpu.SemaphoreType.DMA((7,)), pltpu.SemaphoreType.DMA],
        compiler_params=pltpu.CompilerParams(vmem_limit_bytes=VMEM_LIMIT),
    )(v)


def _aliased_comm_call(body, bufs, n_sems, *, name):
    n = len(bufs)
    return pl.pallas_call(
        body, name=name, out_shape=[jax.ShapeDtypeStruct(b.shape, b.dtype) for b in bufs],
        in_specs=[ANY] * n, out_specs=[ANY] * n, input_output_aliases={k: k for k in range(n)},
        scratch_shapes=[pltpu.SemaphoreType.DMA((n_sems,)), pltpu.SemaphoreType.DMA((n_sems,))],
    )(*bufs)


HBM = pl.BlockSpec(memory_space=pltpu.HBM)
SEM = pl.BlockSpec(memory_space=pltpu.SEMAPHORE)
_SPLIT = pltpu.CompilerParams(has_side_effects=pltpu.SideEffectType.DATAFLOW_SIDE_EFFECTING)


def _in_hbm(arrs):
    return [pltpu.with_memory_space_constraint(a, pltpu.HBM) for a in arrs]


def _gather_ici_start(bufs, after, *, name):
    n = len(bufs)

    def body(*refs):
        send_sems, recv_sems, outs, token = refs[n + 1], refs[n + 2], refs[n + 3:2 * n + 3], refs[2 * n + 3]
        x, y, c = _place()
        for b in range(n):
            rh = bufs[b].shape[1] // 2
            part = outs[b].at[2 * x + y, pl.ds(c * rh, rh), :]
            for j, chip in enumerate(_other_chips(x, y)):
                pltpu.make_async_remote_copy(src_ref=part, dst_ref=part, send_sem=send_sems.at[3 * b + j],
                                             recv_sem=recv_sems.at[3 * b + j], device_id=(*chip, c),
                                             device_id_type=MESH).start()
        token[...] = jnp.zeros_like(token)

    res = pl.pallas_call(
        body, name=name,
        out_shape=(pltpu.SemaphoreType.DMA((3 * n,)), pltpu.SemaphoreType.DMA((3 * n,)),
                   *[pltpu.HBM(b.shape, b.dtype) for b in bufs], jax.ShapeDtypeStruct((8, 128), F32)),
        in_specs=[HBM] * n + [ANY], out_specs=(SEM, SEM, *[HBM] * n, pl.BlockSpec(memory_space=pltpu.VMEM)),
        input_output_aliases={k: k + 2 for k in range(n)}, compiler_params=_SPLIT,
    )(*_in_hbm(bufs), after)
    return res[0], res[1], list(res[2:2 + n]), res[2 + n]


def _gather_ici_wait(send_sems, recv_sems, bufs, after, *, name):
    n = len(bufs)

    def body(*refs):
        ins, ss, rs = refs[:n], refs[n], refs[n + 1]
        x, y, c = _place()
        for b in range(n):
            rh = bufs[b].shape[1] // 2
            mine = ins[b].at[2 * x + y, pl.ds(c * rh, rh), :]
            for j, (cx, cy) in enumerate(_other_chips(x, y)):
                theirs = ins[b].at[2 * cx + cy, pl.ds(c * rh, rh), :]
                cp = pltpu.make_async_remote_copy(src_ref=mine, dst_ref=theirs, send_sem=ss.at[3 * b + j],
                                                  recv_sem=rs.at[3 * b + j], device_id=(cx, cy, c),
                                                  device_id_type=MESH)
                cp.wait_send()
                cp.wait_recv()

    return list(pl.pallas_call(
        body, name=name, out_shape=[pltpu.HBM(b.shape, b.dtype) for b in bufs],
        in_specs=[HBM] * n + [SEM, SEM, ANY], out_specs=[HBM] * n,
        input_output_aliases={k: k for k in range(n)}, compiler_params=_SPLIT,
    )(*bufs, send_sems, recv_sems, after))


def _gather_forward(bufs, *, name):
    n = len(bufs)

    def body(*refs):
        outs, send_sems, recv_sems = refs[n:2 * n], refs[2 * n], refs[2 * n + 1]
        x, y, c = _place()

        def copy(b, j, chip, hc):
            rh = bufs[b].shape[1] // 2
            part = outs[b].at[2 * chip[0] + chip[1], pl.ds(hc * rh, rh), :]
            return pltpu.make_async_remote_copy(src_ref=part, dst_ref=part, send_sem=send_sems.at[3 * b + j],
                                                recv_sem=recv_sems.at[3 * b + j], device_id=(x, y, 1 - c),
                                                device_id_type=MESH)

        sends = [copy(b, j, chip, c) for b in range(n) for j, chip in enumerate(_other_chips(x, y))]
        for cp in sends:
            cp.start()
        for b in range(n):
            for j, chip in enumerate(_other_chips(x, y)):
                copy(b, j, chip, 1 - c).wait_recv()
        for cp in sends:
            cp.wait_send()

    return _aliased_comm_call(body, bufs, 3 * n, name=name)


def _chip_exchange_start(hs, *, name):
    n = len(hs)
    lands = [lax.empty((3,) + h.shape[1:], h.dtype) for h in hs]

    def body(*refs):
        send_sems, recv_sems = refs[2 * n], refs[2 * n + 1]
        h_out, l_out, token = refs[2 * n + 2:3 * n + 2], refs[3 * n + 2:4 * n + 2], refs[4 * n + 2]
        x, y, c = _place()
        for b in range(n):
            for j, (cx, cy) in enumerate(_other_chips(x, y)):
                pltpu.make_async_remote_copy(src_ref=h_out[b].at[2 * cx + cy], dst_ref=l_out[b].at[j],
                                             send_sem=send_sems.at[3 * b + j], recv_sem=recv_sems.at[3 * b + j],
                                             device_id=(cx, cy, c), device_id_type=MESH).start()
        token[...] = jnp.zeros_like(token)

    res = pl.pallas_call(
        body, name=name,
        out_shape=(pltpu.SemaphoreType.DMA((3 * n,)), pltpu.SemaphoreType.DMA((3 * n,)),
                   *[pltpu.HBM(a.shape, a.dtype) for a in hs + lands], jax.ShapeDtypeStruct((8, 128), F32)),
        in_specs=[HBM] * (2 * n), out_specs=(SEM, SEM, *[HBM] * (2 * n), pl.BlockSpec(memory_space=pltpu.VMEM)),
        input_output_aliases={k: k + 2 for k in range(2 * n)}, compiler_params=_SPLIT,
    )(*_in_hbm(hs + lands))
    return res[0], res[1], list(res[2:2 + n]), list(res[2 + n:2 + 2 * n]), res[2 + 2 * n]


def _chip_exchange_wait(send_sems, recv_sems, hs, lands, after, *, name):
    n = len(hs)

    def body(*refs):
        h_in, l_in, ss, rs = refs[:n], refs[n:2 * n], refs[2 * n], refs[2 * n + 1]
        x, y, c = _place()
        for b in range(n):
            for j, (cx, cy) in enumerate(_other_chips(x, y)):
                cp = pltpu.make_async_remote_copy(src_ref=h_in[b].at[2 * cx + cy], dst_ref=l_in[b].at[j],
                                                  send_sem=ss.at[3 * b + j], recv_sem=rs.at[3 * b + j],
                                                  device_id=(cx, cy, c), device_id_type=MESH)
                cp.wait_send()
                cp.wait_recv()

    res = pl.pallas_call(
        body, name=name, out_shape=[pltpu.HBM(a.shape, a.dtype) for a in hs + lands],
        in_specs=[HBM] * (2 * n) + [SEM, SEM, ANY], out_specs=[HBM] * (2 * n),
        input_output_aliases={k: k for k in range(2 * n)}, compiler_params=_SPLIT,
    )(*hs, *lands, send_sems, recv_sems, after)
    return list(res[n:])


def _peers(x, y, c):
    return [((1 - x) if fx else x, (1 - y) if fy else y, (1 - c) if fc else c)
            for fx in (0, 1) for fy in (0, 1) for fc in (0, 1) if fx or fy or fc]


def _all_to_all_start(slab, after, *, name):
    land = lax.empty((N_DEV,) + slab.shape, slab.dtype)

    def body(slab_in, land_in, after_ref, send_sems, recv_sems, slab_out, land_out, token):
        x, y, c = _place()
        for k, peer in enumerate(_peers(x, y, c)):
            pltpu.make_async_remote_copy(src_ref=slab_out, dst_ref=land_out.at[4 * x + 2 * y + c],
                                         send_sem=send_sems.at[k], recv_sem=recv_sems.at[k], device_id=peer,
                                         device_id_type=MESH).start()
        token[...] = jnp.zeros_like(token)

    return pl.pallas_call(
        body, name=name,
        out_shape=(pltpu.SemaphoreType.DMA((N_DEV - 1,)), pltpu.SemaphoreType.DMA((N_DEV - 1,)),
                   pltpu.HBM(slab.shape, slab.dtype), pltpu.HBM(land.shape, land.dtype),
                   jax.ShapeDtypeStruct((8, 128), F32)),
        in_specs=[HBM, HBM, ANY], out_specs=(SEM, SEM, HBM, HBM, pl.BlockSpec(memory_space=pltpu.VMEM)),
        input_output_aliases={0: 2, 1: 3}, compiler_params=_SPLIT,
    )(*_in_hbm([slab, land]), after)


def _all_to_all_wait(send_sems, recv_sems, slab, land, after, *, name):
    def body(slab_in, land_in, ss, rs, after_ref, slab_out, land_out):
        x, y, c = _place()
        for k, (px, py, pc) in enumerate(_peers(x, y, c)):
            cp = pltpu.make_async_remote_copy(src_ref=slab_in, dst_ref=land_in.at[4 * px + 2 * py + pc],
                                              send_sem=ss.at[k], recv_sem=rs.at[k], device_id=(px, py, pc),
                                              device_id_type=MESH)
            cp.wait_send()
            cp.wait_recv()

    return pl.pallas_call(
        body, name=name, out_shape=[pltpu.HBM(slab.shape, slab.dtype), pltpu.HBM(land.shape, land.dtype)],
        in_specs=[HBM, HBM, SEM, SEM, ANY], out_specs=[HBM, HBM], input_output_aliases={0: 0, 1: 1},
        compiler_params=_SPLIT,
    )(slab, land, send_sems, recv_sems, after)


def _pair_exchange_start(gs, *, name):
    n = len(gs)
    lands = [lax.empty((g.shape[0], g.shape[1] // 2, g.shape[2]), g.dtype) for g in gs]

    def body(*refs):
        send_sems, recv_sems = refs[2 * n], refs[2 * n + 1]
        g_out, l_out, token = refs[2 * n + 2:3 * n + 2], refs[3 * n + 2:4 * n + 2], refs[4 * n + 2]
        x, y, c = _place()
        for b in range(n):
            rh = gs[b].shape[1] // 2
            pltpu.make_async_remote_copy(src_ref=g_out[b].at[:, pl.ds((1 - c) * rh, rh), :], dst_ref=l_out[b],
                                         send_sem=send_sems.at[b], recv_sem=recv_sems.at[b],
                                         device_id=(x, y, 1 - c), device_id_type=MESH).start()
        token[...] = jnp.zeros_like(token)

    res = pl.pallas_call(
        body, name=name,
        out_shape=(pltpu.SemaphoreType.DMA((n,)), pltpu.SemaphoreType.DMA((n,)),
                   *[pltpu.HBM(a.shape, a.dtype) for a in gs + lands], jax.ShapeDtypeStruct((8, 128), F32)),
        in_specs=[HBM] * (2 * n), out_specs=(SEM, SEM, *[HBM] * (2 * n), pl.BlockSpec(memory_space=pltpu.VMEM)),
        input_output_aliases={k: k + 2 for k in range(2 * n)}, compiler_params=_SPLIT,
    )(*_in_hbm(gs + lands))
    return res[0], res[1], list(res[2:2 + n]), list(res[2 + n:2 + 2 * n]), res[2 + 2 * n]


def _pair_exchange_wait(send_sems, recv_sems, gs, lands, after, *, name):
    n = len(gs)

    def body(*refs):
        g_in, l_in, ss, rs = refs[:n], refs[n:2 * n], refs[2 * n], refs[2 * n + 1]
        x, y, c = _place()
        for b in range(n):
            rh = gs[b].shape[1] // 2
            cp = pltpu.make_async_remote_copy(src_ref=g_in[b].at[:, pl.ds((1 - c) * rh, rh), :], dst_ref=l_in[b],
                                              send_sem=ss.at[b], recv_sem=rs.at[b], device_id=(x, y, 1 - c),
                                              device_id_type=MESH)
            cp.wait_send()
            cp.wait_recv()

    res = pl.pallas_call(
        body, name=name, out_shape=[pltpu.HBM(a.shape, a.dtype) for a in gs + lands],
        in_specs=[HBM] * (2 * n) + [SEM, SEM, ANY], out_specs=[HBM] * (2 * n),
        input_output_aliases={k: k for k in range(2 * n)}, compiler_params=_SPLIT,
    )(*gs, *lands, send_sems, recv_sems, after)
    return list(res[:n]), list(res[n:])


def _pair_share(ss, *, name):
    n = len(ss)

    def body(*refs):
        outs, send_sems, recv_sems = refs[n:2 * n], refs[2 * n], refs[2 * n + 1]
        x, y, c = _place()
        cps = []
        for b in range(n):
            rh = ss[b].shape[0] // 2
            mine = outs[b].at[pl.ds(c * rh, rh), :]
            cps.append(pltpu.make_async_remote_copy(src_ref=mine, dst_ref=mine, send_sem=send_sems.at[b],
                                                    recv_sem=recv_sems.at[b], device_id=(x, y, 1 - c),
                                                    device_id_type=MESH))
        for cp in cps:
            cp.start()
        for b, cp in enumerate(cps):
            rh = ss[b].shape[0] // 2
            theirs = outs[b].at[pl.ds((1 - c) * rh, rh), :]
            pltpu.make_async_remote_copy(src_ref=theirs, dst_ref=theirs, send_sem=send_sems.at[b],
                                         recv_sem=recv_sems.at[b], device_id=(x, y, 1 - c),
                                         device_id_type=MESH).wait_recv()
            cp.wait_send()

    return _aliased_comm_call(body, ss, n, name=name)


_SMALL_SHARDED = (("e_conv_w", 2), ("o_norm", 1), ("o_d", 1))
_REPLICATED = ("e_norm", "e_gmlp_w", "e_gmlp_b", "e_conv_b", "e_conv_ln_g", "e_conv_ln_b", "o_lam_re", "o_lam_im",
               "o_log_dt", "o_b_re", "o_b_im", "o_c_re", "o_c_im", "ca_norm", "ca_mem_norm", "ffn_norm", "final_norm")
_SMALL = tuple(n for n, _ in _SMALL_SHARDED) + _REPLICATED
_WEIGHTS = ("e_norm", "e_w_in", "e_gmlp_w", "e_gmlp_b", "e_conv_w", "e_conv_b", "e_conv_ln_g", "e_conv_ln_b",
            "e_w_out", "o_norm", "o_w_in", "o_lam_re", "o_lam_im", "o_log_dt", "o_b_re", "o_b_im", "o_c_re", "o_c_im",
            "o_d", "o_w_out", "ca_norm", "ca_mem_norm", "ca_wq", "ca_wk", "ca_wv", "ca_wo", "ffn_norm", "ffn_w_gate",
            "ffn_w_up", "ffn_w_down", "final_norm")


def _pack_rows(arrs, width, dtype, row_mult=8):
    parts, spans, r0 = [], [], 0
    for a in arrs:
        flat = a.reshape(-1).astype(dtype)
        rows = -(-flat.shape[0] // (width * row_mult)) * row_mult
        if rows * width != flat.shape[0]:
            flat = jnp.pad(flat, (0, rows * width - flat.shape[0]))
        parts.append(flat.reshape(rows, width))
        spans.append((r0, rows))
        r0 += rows
    return jnp.concatenate(parts, axis=0), spans


def _unpack_rows(slab, spans, shapes):
    out = []
    for (r0, rows), shp in zip(spans, shapes):
        n = math.prod(shp)
        out.append(slab[r0:r0 + rows].reshape(-1)[:n].reshape(shp))
    return out


def _two_d(a):
    return a.reshape(-1, a.shape[-1])


def _shard_rows(n, a):
    return _two_d(jnp.swapaxes(a, -1, -2) if n in _TRANSPOSED else a)


def _from_shard_rows(n, rows, shape):
    if n in _TRANSPOSED:
        return jnp.swapaxes(rows.reshape(shape[:-2] + (shape[-1], shape[-2])), -1, -2)
    return rows.reshape(shape)


def _local_slab(local, slab, dtype):
    parts = sorted((r0, n, l) for n, (_, where) in _PLACE.items() for l, (s, r0) in enumerate(where) if s == slab)
    shards = [_shard_rows(n, local[n] if len(_PLACE[n][1]) == 1 else local[n][l]) for _, n, l in parts]
    return jnp.concatenate([a.astype(dtype) for a in shards], axis=0)


def _set_diag(b, pattern):
    return jnp.einsum(pattern, b, jnp.eye(C_GROUPS // N_SETS, dtype=b.dtype))


def _s5_discretize(lam_re, lam_im, log_dt, b_re, b_im):
    dt = jnp.exp(log_dt)[:, None]
    mag = jnp.exp(lam_re * dt)
    ar = mag * jnp.cos(lam_im * dt)
    ai = mag * jnp.sin(lam_im * dt)
    den = lam_re * lam_re + lam_im * lam_im
    qr = ((ar - 1.0) * lam_re + ai * lam_im) / den
    qi = (ai * lam_re - (ar - 1.0) * lam_im) / den
    bbr = qr[..., None] * b_re - qi[..., None] * b_im
    bbi = qr[..., None] * b_im + qi[..., None] * b_re
    return ar, ai, bbr, bbi


def _attention_block(x, mem, W, w, i, tag):
    xn, q = _norm_mm(x, w["ca_norm"][i], _shards(W, "ca_wq", i), split="k", out_dtype=BF16, name=f"{tag}_q")
    memn = _rms_fwd(mem, w["ca_mem_norm"][i], name=f"{tag}_ca_memnorm")
    k = _mm_k(memn, _shards(W, "ca_wk", i), out_dtype=BF16, name=f"{tag}_k")
    v = _mm_k(memn, _shards(W, "ca_wv", i), out_dtype=BF16, name=f"{tag}_v")
    o = _attn_fwd(q, k, v, name=f"{tag}_attn")
    y = _mm_k(o, _shards(W, "ca_wo", i), add=x, name=f"{tag}_wo")
    return y, (x, xn, memn, q, k, v, o)


def _attention_block_bwd(dy, saved, mem, W, w, i, tag, G, grads, token=None, mid=None):
    x, xn, memn, q, k, v, o = saved
    gain = w["ca_norm"][i]
    if token is not None:
        k = _behind(k, token)
    G = _grad_to_slab(G, "ca_wo", i, o, dy, a_cols=256, name=f"{tag}_dwo")
    dq, dk, dv = _attn_bwd(dy, _shards(W, "ca_wo", i), q, k, v, name=f"{tag}_attn_bwd")
    token = mid(dq) if mid is not None else None
    if token is not None:
        gain = _behind(gain, token)
    G = _grad_to_slab(G, "ca_wq", i, xn, dq, a_cols=256, name=f"{tag}_dwq")
    G = _grad_to_slab(G, "ca_wk", i, memn, dk, a_cols=256, name=f"{tag}_dwk")
    G = _grad_to_slab(G, "ca_wv", i, memn, dv, a_cols=256, name=f"{tag}_dwv")
    dmemn = _mm_k_t([(dk, _shards(W, "ca_wk", i)), (dv, _shards(W, "ca_wv", i))], name=f"{tag}_dmemn")
    dx, dg = _norm_bwd_k(dq, _shards(W, "ca_wq", i), x, gain, dy, name=f"{tag}_dq_norm_bwd")
    grads["ca_norm"][i] = dg[0]
    grads["ca_mem_norm"][i] = _rms_dg(mem, w["ca_mem_norm"][i], dmemn, name=f"{tag}_ca_memnorm_bwd")[0]
    return dx, G


def _ffn_block(x, W, w, i, tag):
    fn, gate, up, h = _ffn_up(x, w["ffn_norm"][i], _shards(W, "ffn_w_gate", i), _shards(W, "ffn_w_up", i),
                              name=f"{tag}_ffn_up")
    y = _mm_k(h, _shards(W, "ffn_w_down", i), add=x, name=f"{tag}_down")
    return y, (x, fn, gate, up, h)


def _ffn_block_bwd(dy, saved, W, w, i, tag, G, grads, token=None, mid=None):
    x, fn, gate, up, h = saved
    gain = w["ffn_norm"][i]
    G = _grad_to_slab(G, "ffn_w_down", i, h, dy, name=f"{tag}_dwd")
    dg, du = _ffn_bwd_hidden(dy, _shards(W, "ffn_w_down", i), gate, up, token, name=f"{tag}_ffn_bwd_hidden")
    token = mid(dg) if mid is not None else None
    if token is not None:
        gain = _behind(gain, token)
    G = _grad_to_slab(G, "ffn_w_gate", i, dg, fn, name=f"{tag}_dwg")
    G = _grad_to_slab(G, "ffn_w_up", i, du, fn, name=f"{tag}_dwu")
    dx, dgn = _ffn_in_bwd(dg, du, _shards(W, "ffn_w_gate", i), _shards(W, "ffn_w_up", i), x, gain, dy,
                          name=f"{tag}_ffn_in_bwd")
    grads["ffn_norm"][i] = dgn[0]
    return dx, G


def _gmlp_mask():
    chunk = jnp.arange(GMLP_BLOCK) // CHUNK
    return chunk[None, :] <= chunk[:, None]


def _even_block(x, W, w, tag):
    hn, proj = _norm_mm(x, w["e_norm"][0], _shards(W, "e_w_in"), split="n", out_dtype=F32, name=f"{tag}_w_in")
    wm = jnp.where(_gmlp_mask()[None], w["e_gmlp_w"][0], 0.0).astype(BF16)
    bcol = w["e_gmlp_b"][0][:, :, None]
    cw = jnp.pad(w["e_conv_w"][0], ((0, CONV_HALO - CONV_WIDTH), (0, 0)))
    cb, lg, lb = w["e_conv_b"], w["e_conv_ln_g"], w["e_conv_ln_b"]
    mix, hc = _even_fwd(proj, wm, bcol, cw, cb, lg, lb, name=f"{tag}_mixers")
    y = _mm_k(mix, _shards(W, "e_w_out"), add=x, name=f"{tag}_w_out")
    return y, (x, hn, proj, mix, hc, wm, bcol, cw)


def _even_block_bwd(dy, saved, W, w, tag, G, grads):
    x, hn, proj, mix, hc, wm, bcol, cw = saved
    dmix = _mm_k_t([(dy, _shards(W, "e_w_out"))], name=f"{tag}_dmix")
    G = _grad_to_slab(G, "e_w_out", 0, mix, dy, a_cols=256, name=f"{tag}_dw_out")
    wmt = jnp.swapaxes(wm, 1, 2)
    dpa, dhc, dwm, db, dlg, dlb, dcb = _even_bwd1(proj, dmix, hc, wm, wmt, bcol, w["e_conv_ln_g"], w["e_conv_ln_b"],
                                                  name=f"{tag}_mixers_bwd1")
    dpb, dcw = _even_bwd2(proj, dhc, cw, name=f"{tag}_mixers_bwd2")
    grads["e_gmlp_w"] = jnp.where(_gmlp_mask()[None], dwm, 0.0)[None]
    grads["e_gmlp_b"] = db[:, :, 0][None]
    grads["e_conv_ln_g"], grads["e_conv_ln_b"], grads["e_conv_b"] = dlg, dlb, dcb
    grads["e_conv_w"] = dcw[:CONV_WIDTH][None]
    G = _grad_to_slab(G, "e_w_in", 0, hn, dpa, b_cols=512, chips=(0, 2), name=f"{tag}_dw_in_a")
    G = _grad_to_slab(G, "e_w_in", 0, hn, dpb, b_cols=512, chips=(2, 2), name=f"{tag}_dw_in_b")
    dx, dg = _norm_bwd_n((dpa, dpb), _shards(W, "e_w_in"), x, w["e_norm"][0], dy, name=f"{tag}_in_bwd")
    grads["e_norm"] = dg
    return dx, G


def _odd_block(x, W, w, tag):
    S = x.shape[0]
    hn, u = _norm_mm(x, w["o_norm"][0], _shards(W, "o_w_in"), split="k", out_dtype=F32, name=f"{tag}_w_in")
    disc_in = (w["o_lam_re"][0], w["o_lam_im"][0], w["o_log_dt"][0], w["o_b_re"][0], w["o_b_im"][0])
    (ar, ai, bbr, bbi), disc_vjp = jax.vjp(_s5_discretize, *disc_in)
    sets = (N_SETS, C_GROUPS // N_SETS)
    per_set = N_STATE // N_SETS
    bset = jnp.concatenate([_set_diag(b.reshape(sets + b.shape[1:]), "jgpc,gh->jgchp").reshape(N_SETS, SET_CH, per_set)
                            for b in (bbr, bbi)], axis=2).astype(BF16)
    cset = jnp.concatenate([_set_diag(c.reshape(sets + c.shape[1:]), "jgcp,gh->jgphc").reshape(N_SETS, per_set, SET_CH)
                            for c in (w["o_c_re"][0], -w["o_c_im"][0])], axis=1).astype(BF16)
    powers, pr, pi = [], ar, ai
    for _ in range(SCAN_BLOCK):
        powers.append(jnp.concatenate([pr.reshape(STATE_ROWS, STATE_LANES), pi.reshape(STATE_ROWS, STATE_LANES)], 0))
        pr, pi = pr * ar - pi * ai, pr * ai + pi * ar
    pw = jnp.stack(powers, axis=0)
    xs = _scan_fwd(u, bset, pw, name=f"{tag}_scan").reshape(S // 8, STATE_ROWS, 8, STATE_LANES)
    yv, yg = _s5_readout(xs, cset, u, w["o_d"], name=f"{tag}_readout")
    o, y = _glu_out(yg, _shards(W, "o_w_out"), x, name=f"{tag}_glu_out")
    return y, (x, hn, u, bset, cset, pw, xs, yv, yg, o, disc_vjp)


def _odd_block_bwd(dy, saved, W, w, tag, G, grads):
    x, hn, u, bset, cset, pw, xs, yv, yg, o, disc_vjp = saved
    S = x.shape[0]
    do, dys, dus, dd = _glu_out_bwd(o, dy, _shards(W, "o_w_out"), yv, u, w["o_d"], name=f"{tag}_glu_out_bwd")
    G = _grad_to_slab(G, "o_w_out", 0, yg, do, b_cols=512, name=f"{tag}_dw_out")
    grads["o_d"] = dd
    dcset_t = _state_grad_sets(dys, xs, name=f"{tag}_dcd")
    gs, da = _scan_bwd(dys, cset, xs.reshape(S * STATE_ROWS, STATE_LANES), pw, name=f"{tag}_scan_bwd")
    gs = gs.reshape(xs.shape)
    dbset = _state_grad_sets(u, gs, name=f"{tag}_dbd")
    du, dx, dg = _s5_in_bwd(gs, bset, dus, _shards(W, "o_w_in"), x, w["o_norm"][0], dy, name=f"{tag}_in_bwd")
    G = _grad_to_slab(G, "o_w_in", 0, hn, du, a_cols=256, name=f"{tag}_dw_in")
    grads["o_norm"] = dg
    per = C_GROUPS // N_SETS
    blocks = (N_SETS, per, C_GROUP_CH, 2, per, C_STATE)
    dc = _set_diag(dcset_t.reshape(blocks), "jhcrgp,gh->rjgcp").reshape(2, C_GROUPS, C_GROUP_CH, C_STATE)
    db = _set_diag(dbset.reshape(blocks), "jgcrhp,gh->rjgpc").reshape(2, C_GROUPS, C_STATE, C_GROUP_CH)
    dcr, dci, dbbr, dbbi = dc[0], -dc[1], db[0], db[1]
    dar = da[:STATE_ROWS].reshape(C_GROUPS, C_STATE)
    dai = da[STATE_ROWS:].reshape(C_GROUPS, C_STATE)
    dlr, dli, dldt, dbr, dbi = disc_vjp((dar, dai, dbbr, dbbi))
    grads["o_lam_re"], grads["o_lam_im"], grads["o_log_dt"] = dlr[None], dli[None], dldt[None]
    grads["o_b_re"], grads["o_b_im"], grads["o_c_re"], grads["o_c_im"] = dbr[None], dbi[None], dcr[None], dci[None]
    return dx, G


def _behind(value, token):
    return value + token[0, 0].astype(value.dtype)


class _NoExchange:
    def __init__(self, W):
        self.W = W

    def first_weights(self, w):
        return self.W, w

    def weights(self, stage, after):
        return {}

    def grads_ready(self, piece, G):
        return None

    def grads_crossed(self, piece, after):
        return None


def _forward_backward(xs_, mems_, tgt, w, G, exchange):
    W, w = exchange.first_weights(w)
    x1, s_mix0 = _even_block(xs_, W, w, "l0")
    W = {**W, **exchange.weights(1, x1)}
    x2, s_att0 = _attention_block(x1, mems_, W, w, 0, "l0")
    W = {**W, **exchange.weights(2, x2)}
    x3, s_ffn0 = _ffn_block(x2, W, w, 0, "l0")
    W = {**W, **exchange.weights(3, x3)}
    x4, s_mix1 = _odd_block(x3, W, w, "l1")
    x5, s_att1 = _attention_block(x4, mems_, W, w, 1, "l1")
    x6, s_ffn1 = _ffn_block(x5, W, w, 1, "l1")
    dx, dfinal, loss_lanes = _loss_head(x6, w["final_norm"], tgt, name="loss_head")

    grads = {n: [None, None] for n in ("ca_norm", "ca_mem_norm", "ffn_norm")}
    grads["final_norm"] = dfinal[0]
    dx, G = _ffn_block_bwd(dx, s_ffn1, W, w, 1, "l1", G, grads)
    dx, G = _attention_block_bwd(dx, s_att1, mems_, W, w, 1, "l1", G, grads)
    dx, G = _odd_block_bwd(dx, s_mix1, W, w, "l1", G, grads)
    token = exchange.grads_ready("l1", G)
    dx, G = _ffn_block_bwd(dx, s_ffn0, W, w, 0, "l0", G, grads, token,
                           lambda after: exchange.grads_crossed("l1", after))
    token = exchange.grads_ready("ffn0", G)
    dx, G = _attention_block_bwd(dx, s_att0, mems_, W, w, 0, "l0", G, grads, token,
                                 lambda after: exchange.grads_crossed("ffn0", after))
    dx, G = _even_block_bwd(dx, s_mix0, W, w, "l0", G, grads)
    for n in list(grads):
        if isinstance(grads[n], list):
            grads[n] = jnp.stack(grads[n], axis=0)
        grads[n] = grads[n].reshape(w[n].shape)
    return loss_lanes, dx, G, grads


class _Exchange:
    def __init__(self, local, chip, core):
        self.bufs = {s: lax.dynamic_update_slice(lax.empty((N_CHIPS, rows, width), BF16),
                                                 _local_slab(local, s, BF16)[None], (chip, 0, 0))
                     for s, (width, rows) in _SLABS.items()}
        self.half = core.reshape(1).astype(jnp.int32)
        self.where = jnp.stack([chip, core]).astype(jnp.int32)
        self.flights = []
        self.reduces = {}

    def weights(self, stage, after):
        send_sems, recv_sems, bufs, _ = self.flights[stage]
        bufs = _gather_ici_wait(send_sems, recv_sems, bufs, after, name=f"gather_stage{stage}_wait")
        return dict(zip(_STAGES[stage], _gather_forward(bufs, name=f"gather_stage{stage}_forward")))

    def first_weights(self, w):
        after = w["e_conv_w"].reshape(-1)[:STATE_LANES]
        for k, stage in enumerate(_STAGES):
            self.flights.append(_gather_ici_start([self.bufs[s] for s in stage], after, name=f"gather_stage{k}_start"))
            after = self.flights[-1][3]
        return self.weights(0, after), {**w, "e_norm": _behind(w["e_norm"], after)}

    def pair_start(self, G, slabs, tag):
        send_sems, recv_sems, gl, lands, token = _pair_exchange_start([G[s] for s in slabs],
                                                                      name=f"grad_{tag}_pair_start")
        return (slabs, send_sems, recv_sems, gl, lands), token

    def pair_land(self, state, after, tag):
        slabs, send_sems, recv_sems, gl, lands = state
        gl, other = _pair_exchange_wait(send_sems, recv_sems, gl, lands, after, name=f"grad_{tag}_pair_wait")
        pairs = [_pair_sum(g, r, self.half, name=f"grad_pair_sum_{s}") for s, g, r in zip(slabs, gl, other)]
        send_sems, recv_sems, pairs, lands, token = _chip_exchange_start(pairs, name=f"grad_{tag}_chip_start")
        return (slabs, gl, other, send_sems, recv_sems, pairs, lands), token

    def reduce_finish(self, state, after, tag):
        slabs, gl, other, send_sems, recv_sems, pairs, lands = state
        slots = _chip_exchange_wait(send_sems, recv_sems, pairs, lands, after, name=f"grad_{tag}_chip_wait")
        halves = [_chip_sum(g, r, sl, self.where, name=f"grad_chip_sum_{s}")
                  for s, g, r, sl in zip(slabs, gl, other, slots)]
        return dict(zip(slabs, _pair_share(halves, name=f"grad_{tag}_pair_share")))

    def grads_ready(self, piece, G):
        self.reduces[piece], token = self.pair_start(G, _GRAD_PIECES[piece], piece)
        return token

    def grads_crossed(self, piece, after):
        self.reduces[piece], token = self.pair_land(self.reduces[piece], after, piece)
        return token


def kernel(x, mem, e_norm, e_w_in, e_gmlp_w, e_gmlp_b, e_conv_w, e_conv_b, e_conv_ln_g, e_conv_ln_b, e_w_out, o_norm, o_w_in, o_lam_re, o_lam_im, o_log_dt, o_b_re, o_b_im, o_c_re, o_c_im, o_d, o_w_out, ca_norm, ca_mem_norm, ca_wq, ca_wk, ca_wv, ca_wo, ffn_norm, ffn_w_gate, ffn_w_up, ffn_w_down, final_norm, loss_target, m_e_norm, m_e_w_in, m_e_gmlp_w, m_e_gmlp_b, m_e_conv_w, m_e_conv_b, m_e_conv_ln_g, m_e_conv_ln_b, m_e_w_out, m_o_norm, m_o_w_in, m_o_lam_re, m_o_lam_im, m_o_log_dt, m_o_b_re, m_o_b_im, m_o_c_re, m_o_c_im, m_o_d, m_o_w_out, m_ca_norm, m_ca_mem_norm, m_ca_wq, m_ca_wk, m_ca_wv, m_ca_wo, m_ffn_norm, m_ffn_w_gate, m_ffn_w_up, m_ffn_w_down, m_final_norm, v_e_norm, v_e_w_in, v_e_gmlp_w, v_e_gmlp_b, v_e_conv_w, v_e_conv_b, v_e_conv_ln_g, v_e_conv_ln_b, v_e_w_out, v_o_norm, v_o_w_in, v_o_lam_re, v_o_lam_im, v_o_log_dt, v_o_b_re, v_o_b_im, v_o_c_re, v_o_c_im, v_o_d, v_o_w_out, v_ca_norm, v_ca_mem_norm, v_ca_wq, v_ca_wk, v_ca_wv, v_ca_wo, v_ffn_norm, v_ffn_w_gate, v_ffn_w_up, v_ffn_w_down, v_final_norm):
    args = dict(locals())
    local = {n: args[n] for n in _WEIGHTS}
    mom = {n: args["m_" + n] for n in _WEIGHTS}
    vel = {n: args["v_" + n] for n in _WEIGHTS}
    chip = 2 * lax.axis_index("x") + lax.axis_index("y")
    core = lax.axis_index("c")
    xs_, mems_, tgt = x[0], mem[0], loss_target[0]

    w = {n: local[n] for n in _REPLICATED}
    sm_slab, sm_spans = _pack_rows([local[n] for n, _ in _SMALL_SHARDED], SMALL_W, F32)
    sm_all = _allgather_small(sm_slab, name="gather_small_weights").reshape(N_DEV, -1, SMALL_W)
    for (n, ax), span in zip(_SMALL_SHARDED, sm_spans):
        shp = local[n].shape
        w[n] = jnp.concatenate([_unpack_rows(sm_all[2 * p], [span], [shp])[0] for p in range(N_CHIPS)], axis=ax)

    exchange = _Exchange(local, chip, core)
    G = {s: lax.empty((N_CHIPS, rows, width), F32) for s, (width, rows) in _SLABS.items()}
    loss_lanes, dx, G, grads = _forward_backward(xs_, mems_, tgt, w, G, exchange)

    gs_slab, gs_spans = _pack_rows([grads[n] for n in _SMALL] + [loss_lanes], SMALL_W, F32)
    small_flight = _all_to_all_start(gs_slab, dx, name="small_grads_start")
    exchange.grads_ready("rest0", G)
    gsum = exchange.reduce_finish(exchange.reduces["l1"], small_flight[4], "l1")
    gsum = {**gsum, **exchange.reduce_finish(exchange.reduces["ffn0"], small_flight[4], "ffn0")}
    token = exchange.grads_crossed("rest0", gsum["B0"])

    gs_slab, gs_all = _all_to_all_wait(*small_flight[:4], token, name="small_grads_wait")
    gs_all = lax.dynamic_update_slice(gs_all, gs_slab[None], (2 * chip + core, 0, 0))
    gs_sum = _sum_slots(gs_all, name="small_grad_sum")
    *small_sums, loss_sum = _unpack_rows(gs_sum, gs_spans, [grads[n].shape for n in _SMALL] + [loss_lanes.shape])
    out_grads = dict(zip(_SMALL, small_sums))
    for n, ax in _SMALL_SHARDED:
        width = local[n].shape[ax]
        out_grads[n] = lax.dynamic_slice_in_dim(out_grads[n], chip * width, width, axis=ax)

    delta, new_m, new_v = {}, {}, {}
    d_, m_, v_ = _adamw_small([_two_d(local[n]) for n in _SMALL], [_two_d(out_grads[n]) for n in _SMALL],
                              [_two_d(mom[n]) for n in _SMALL], [_two_d(vel[n]) for n in _SMALL], name="adamw_small")
    for n, dd, mm_, vv in zip(_SMALL, d_, m_, v_):
        shp = local[n].shape
        delta[n], new_m[n], new_v[n] = dd.reshape(shp), mm_.reshape(shp), vv.reshape(shp)
    def adamw_large(names):
        for n in names:
            shp = local[n].shape
            g_, d_, m_, v_ = _adamw_shard(_shard_rows(n, local[n]), [(gsum[s], r0) for s, r0 in _PLACE[n][1]],
                                          _shard_rows(n, mom[n]), _shard_rows(n, vel[n]), name=f"adamw_{n}")
            out_grads[n], delta[n], new_m[n], new_v[n] = (_from_shard_rows(n, t, shp) for t in (g_, d_, m_, v_))

    ready = [n for n, (_, where) in _PLACE.items() if all(s in gsum for s, _ in where)]
    adamw_large(ready)
    done = jnp.concatenate([delta[n].reshape(-1)[:1] for n in ready + list(_SMALL[:1])])
    gsum = {**gsum, **exchange.reduce_finish(exchange.reduces["rest0"], done, "rest0")}
    adamw_large([n for n in _PLACE if n not in ready])

    return (loss_sum[0, 0], dx[None], *[out_grads[n] for n in _WEIGHTS], *[delta[n] for n in _WEIGHTS],
            *[new_m[n] for n in _WEIGHTS], *[new_v[n] for n in _WEIGHTS])
```

```python
import functools
import math

import jax
import jax.numpy as jnp
from jax import lax
from jax.experimental import pallas as pl
from jax.experimental.pallas import tpu as pltpu

F32 = jnp.float32
BF16 = jnp.bfloat16
MESH = pl.DeviceIdType.MESH

EPS = 1e-6
D_MODEL = 1024
A_WIDTH = 512
A_GROUPS = 4
GMLP_BLOCK = 128
CHUNK = 64
B_WIDTH = 512
CONV_WIDTH = 31
CONV_HALO = 32
C_WIDTH = 512
C_GROUP_CH = 16
C_GROUPS = 32
C_STATE = 64
N_STATE = C_GROUPS * C_STATE
STATE_LANES = 128
STATE_ROWS = N_STATE // STATE_LANES
SCAN_BLOCK = 8
CA_HEADS = 4
CA_HEAD_DIM = 256
FFN_HIDDEN = 2816

ADAM_LR = 0.001
ADAM_B1 = 0.9
ADAM_B2 = 0.999
ADAM_EPS = 1e-08
ADAM_WD = 0.01
ADAM_STEP = 10

VMEM_LIMIT = 56 * 1024 * 1024
ACC_BYTES = 6 * 1024 * 1024
TN_VMEM_BYTES = 44 * 1024 * 1024
SMALL_W = 128
N_CHIPS = 4
N_DEV = 8

_SLABS = {"D0": (512, 1024), "E0": (1024, 256), "A0": (1024, 1024), "B0": (1024, 704), "C0": (1024, 1408),
          "D1": (512, 768), "A1": (1024, 1024), "B1": (1024, 704), "C1": (1024, 1408)}
_STAGES = (("D0", "E0"), ("A0",), ("B0", "C0"), ("D1", "A1", "B1", "C1"))
_GRAD_PIECES = {"l1": _STAGES[3], "ffn0": _STAGES[2], "rest0": _STAGES[0] + _STAGES[1]}
_PLACE = {
    "e_w_in": (1024, (("D0", 0),)), "e_w_out": (256, (("E0", 0),)),
    "o_w_out": (512, (("D1", 0),)), "o_w_in": (256, (("D1", 512),)),
    "ca_wq": (256, (("A0", 0), ("A1", 0))), "ca_wk": (256, (("A0", 256), ("A1", 256))),
    "ca_wv": (256, (("A0", 512), ("A1", 512))), "ca_wo": (256, (("A0", 768), ("A1", 768))),
    "ffn_w_down": (704, (("B0", 0), ("B1", 0))),
    "ffn_w_gate": (704, (("C0", 0), ("C1", 0))), "ffn_w_up": (704, (("C0", 704), ("C1", 704))),
}
_TRANSPOSED = ("ffn_w_gate", "ffn_w_up")


def _params(sem=None):
    return pltpu.CompilerParams(dimension_semantics=sem, vmem_limit_bytes=VMEM_LIMIT)


def _tile(n, pref, mult=128):
    if n <= pref:
        return n
    t = (pref // mult) * mult
    while t >= mult:
        if n % t == 0:
            return t
        t -= mult
    return n


def _blk(name, layer=0):
    rows, where = _PLACE[name]
    slab, r0 = where[layer]
    assert r0 % rows == 0
    return slab, rows, r0 // rows


def _shards(slabs, name, layer=0):
    slab, rows, b = _blk(name, layer)
    return [(slabs[slab], (None, rows, _SLABS[slab][0]), (p, b, 0)) for p in range(N_CHIPS)]


_GELU_C = 0.7978845608028654
_GELU_A = 0.044715


def _gelu(x):
    t = jnp.tanh(_GELU_C * (x + _GELU_A * (x * x * x)))
    return 0.5 * x * (1.0 + t), t


def _gelu_grad(x, t):
    return 0.5 * (1.0 + t) + 0.5 * x * (1.0 - t * t) * (_GELU_C * (1.0 + 3.0 * _GELU_A * x * x))


def _sigmoid(x):
    return 1.0 / (1.0 + jnp.exp(-x))


def _mean(x):
    return jnp.mean(x, axis=-1, keepdims=True)


def _dot(a, b):
    return jnp.dot(a, b, preferred_element_type=F32)


def _dot_nt(a, b):
    return lax.dot_general(a, b, (((1,), (1,)), ((), ())), preferred_element_type=F32)


def _dot_tn(a, b):
    return lax.dot_general(a, b, (((0,), (0,)), ((), ())), preferred_element_type=F32)


def _rms_tile(xv, gv):
    return (xv * lax.rsqrt(_mean(xv * xv) + EPS)) * gv


def _rms_bwd_tile(xv, gv, dyv):
    r = lax.rsqrt(_mean(xv * xv) + EPS)
    xh = xv * r
    dyg = dyv * gv
    return r * (dyg - xh * _mean(dyg * xh)), jnp.sum(dyv * xh, axis=0, keepdims=True)


def _cols(p, width):
    return slice(p * width, (p + 1) * width)


def _sum_k(a, ws, k):
    tot = None
    for p in range(N_CHIPS):
        y = _dot(a[:, _cols(p, k)], ws[p][...])
        tot = y if tot is None else tot + y
    return tot


def _cat_nt(a, ws):
    return jnp.concatenate([_dot_nt(a, ws[p][...]) for p in range(N_CHIPS)], axis=1)


def _rows_call(name, tm, rows, fulls, outs, accs, body, scratch=()):
    S = min(x.shape[-2] for x in rows if x.ndim != 4)
    nr, nf, no, na = len(rows), len(fulls), len(outs), len(accs)

    def kern(*refs):
        r, f = refs[:nr], refs[nr:nr + nf]
        o, a = refs[nr + nf:nr + nf + no], refs[nr + nf + no:nr + nf + no + na]
        if na:
            @pl.when(pl.program_id(0) == 0)
            def _():
                for ref in a:
                    ref[...] = jnp.zeros_like(ref)
        body(r, f, o, a, refs[nr + nf + no + na:])

    def whole(shape):
        nd = len(shape)
        return pl.BlockSpec(tuple(shape), lambda i: (0,) * nd)

    def row_spec(shape):
        if len(shape) == 4:
            return pl.BlockSpec((tm // 8,) + tuple(shape[1:]), lambda i: (i, 0, 0, 0))
        if len(shape) == 3:
            return pl.BlockSpec((shape[0], tm, shape[2]), lambda i: (0, i, 0))
        return pl.BlockSpec((tm, shape[1]), lambda i: (i, 0))

    def full_spec(x):
        if isinstance(x, tuple):
            _, bshape, bidx = x
            return pl.BlockSpec(bshape, lambda i: bidx, pipeline_mode=pl.Buffered(1))
        return whole(x.shape)

    out_shapes = [(S, o[0]) if len(o) == 2 else (o[0], S, o[1]) for o in outs]
    res = pl.pallas_call(
        kern, name=name, grid=(S // tm,),
        in_specs=[row_spec(x.shape) for x in rows] + [full_spec(x) for x in fulls],
        out_specs=[row_spec(s) for s in out_shapes] + [whole(shp) for shp, _ in accs],
        out_shape=[jax.ShapeDtypeStruct(s, o[-1]) for s, o in zip(out_shapes, outs)]
        + [jax.ShapeDtypeStruct(tuple(shp), dt) for shp, dt in accs],
        scratch_shapes=list(scratch),
        compiler_params=_params(("arbitrary",) if na else ("parallel",)),
    )(*rows, *[x[0] if isinstance(x, tuple) else x for x in fulls])
    return res[:no], res[no:]


def _grad_to_slab(gslabs, wname, layer, a, b, *, a_cols=None, b_cols=None, chips=(0, N_CHIPS), name):
    slab, rows, bidx = _blk(wname, layer)
    width = _SLABS[slab][0]
    p0, n_p = chips
    assert p0 % n_p == 0
    S = a.shape[-2]

    def tile_bytes(x, ts):
        return ts * x.dtype.itemsize * (x.shape[2] * n_p if x.ndim == 3 else x.shape[1])

    acc_bytes = n_p * rows * (-(-width // 128) * 128) * 4
    ts = next(t for t in (2048, 1024, 512, 256, S) if S % t == 0
              and 2 * (tile_bytes(a, t) + tile_bytes(b, t) + acc_bytes) <= TN_VMEM_BYTES or t == S)

    def operand(x):
        if x.ndim == 3:
            return pl.BlockSpec((n_p, ts, x.shape[2]), lambda s: (p0 // n_p, s, 0))
        return pl.BlockSpec((ts, x.shape[1]), lambda s: (s, 0))

    def part(ref, cols, p):
        if len(ref.shape) == 3:
            return ref[p]
        return ref[...] if cols is None else ref[:, _cols(p, cols)]

    def body(a_ref, b_ref, slab_ref, o_ref):
        @pl.when(pl.program_id(0) == 0)
        def _():
            o_ref[...] = jnp.zeros_like(o_ref)

        for p in range(n_p):
            o_ref[p] += _dot_tn(part(a_ref, a_cols, p).astype(BF16), part(b_ref, b_cols, p).astype(BF16))

    g = gslabs[slab]
    out = pl.pallas_call(
        body, name=name, grid=(S // ts,),
        in_specs=[operand(a), operand(b), pl.BlockSpec(memory_space=pl.ANY)],
        out_specs=pl.BlockSpec((n_p, rows, width), lambda s: (p0 // n_p, bidx, 0)),
        out_shape=jax.ShapeDtypeStruct(g.shape, F32), input_output_aliases={2: 0},
        compiler_params=_params(("arbitrary",)),
    )(a, b, g)
    return {**gslabs, slab: out}


def _vec(g):
    return g.reshape(1, -1)


def _norm_mm(x, g, ws, *, split, out_dtype, name, tm=512):
    S, D = x.shape
    k, n = ws[0][1][1], ws[0][1][2]
    N = n if split == "k" else N_CHIPS * n

    def body(r, f, o, acc, s):
        xn = _rms_tile(r[0][...], f[0][...]).astype(BF16)
        o[0][...] = xn
        if split == "k":
            o[1][...] = _sum_k(xn, f[1:], k).astype(out_dtype)
        else:
            for p in range(N_CHIPS):
                o[1][:, _cols(p, n)] = _dot(xn, f[1 + p][...]).astype(out_dtype)

    (xn, y), _ = _rows_call(name, _tile(S, tm), [x], [_vec(g)] + ws, [(D, BF16), (N, out_dtype)], [], body)
    return xn, y


def _mm_k(a, ws, *, add=None, out_dtype=F32, name, tm=512):
    S = a.shape[-2]
    k, n = ws[0][1][1], ws[0][1][2]
    has_add = add is not None

    def body(r, f, o, acc, s):
        if a.ndim == 3:
            y = None
            for p in range(N_CHIPS):
                t = _dot(r[0][p].astype(BF16), f[p][...])
                y = t if y is None else y + t
        else:
            y = _sum_k(r[0][...].astype(BF16), f, k)
        if has_add:
            y = y + r[1][...]
        o[0][...] = y.astype(out_dtype)

    (y,), _ = _rows_call(name, _tile(S, tm), [a] + ([add] if has_add else []), ws, [(n, out_dtype)], [], body)
    return y


def _mm_k_t(terms, *, out_dtype=F32, name, tm=512):
    S = terms[0][0].shape[0]
    k = terms[0][1][0][1][1]

    def body(r, f, o, acc, s):
        y = None
        for t in range(len(terms)):
            yt = _cat_nt(r[t][...].astype(BF16), f[N_CHIPS * t:N_CHIPS * (t + 1)])
            y = yt if y is None else y + yt
        o[0][...] = y.astype(out_dtype)

    (y,), _ = _rows_call(name, _tile(S, tm), [a for a, _ in terms], [w for _, ws in terms for w in ws],
                         [(N_CHIPS * k, out_dtype)], [], body)
    return y


def _rms_fwd(x, g, *, name):
    def body(r, f, o, acc, s):
        o[0][...] = _rms_tile(r[0][...], f[0][...]).astype(BF16)

    (y,), _ = _rows_call(name, _tile(x.shape[0], 256, 8), [x], [_vec(g)], [(x.shape[1], BF16)], [], body)
    return y


def _rms_dg(x, g, dy, *, name):
    def body(r, f, o, acc, s):
        acc[0][...] += _rms_bwd_tile(r[0][...], f[0][...], r[1][...])[1]

    _, (dg,) = _rows_call(name, _tile(x.shape[0], 256, 8), [x, dy], [_vec(g)], [], [((1, x.shape[1]), F32)], body)
    return dg


def _ffn_up(x, g, wg, wu, *, name, tm=512):
    S, D = x.shape
    h = wg[0][1][1]

    def body(r, f, o, acc, s):
        xn = _rms_tile(r[0][...], f[0][...]).astype(BF16)
        o[0][...] = xn
        for p in range(N_CHIPS):
            gate = _dot_nt(xn, f[1 + p][...])
            up = _dot_nt(xn, f[1 + N_CHIPS + p][...])
            o[1][p] = gate.astype(BF16)
            o[2][p] = up.astype(BF16)
            o[3][p] = (gate * _sigmoid(gate) * up).astype(BF16)

    (xn, gate, up, hid), _ = _rows_call(name, _tile(S, tm), [x], [_vec(g)] + wg + wu,
                                        [(D, BF16), (N_CHIPS, h, BF16), (N_CHIPS, h, BF16), (N_CHIPS, h, BF16)], [],
                                        body)
    return xn, gate, up, hid


def _ffn_bwd_hidden(dy, wd, gate, up, token=None, *, name, tm=512):
    S = dy.shape[0]
    h = wd[0][1][1]

    def body(r, f, o, acc, s):
        dyv = r[0][...]
        if token is not None:
            dyv = dyv + jnp.sum(f[N_CHIPS][...])
        dyb = dyv.astype(BF16)
        for p in range(N_CHIPS):
            dh = _dot_nt(dyb, f[p][...])
            gv = r[1][p].astype(F32)
            sg = _sigmoid(gv)
            o[0][p] = (dh * r[2][p].astype(F32) * (sg * (1.0 + gv * (1.0 - sg)))).astype(BF16)
            o[1][p] = (dh * gv * sg).astype(BF16)

    (dg, du), _ = _rows_call(name, _tile(S, tm), [dy, gate, up], wd + ([] if token is None else [token]),
                             [(N_CHIPS, h, BF16), (N_CHIPS, h, BF16)], [], body)
    return dg, du


def _ffn_in_bwd(dg, du, wg, wu, x, g, dres, *, name, tm=512):
    S, D = x.shape

    def body(r, f, o, acc, s):
        tot = None
        for p in range(N_CHIPS):
            y = _dot(r[0][p], f[1 + p][...]) + _dot(r[1][p], f[1 + N_CHIPS + p][...])
            tot = y if tot is None else tot + y
        dx, dgn = _rms_bwd_tile(r[2][...], f[0][...], tot)
        o[0][...] = dx + r[3][...]
        acc[0][...] += dgn

    (dx,), (dgn,) = _rows_call(name, _tile(S, tm), [dg, du, x, dres], [_vec(g)] + wg + wu, [(D, F32)],
                               [((1, D), F32)], body)
    return dx, dgn


def _norm_bwd_k(da, ws, x, g, dres, *, name, tm=512):
    S, D = x.shape

    def body(r, f, o, acc, s):
        dx, dg = _rms_bwd_tile(r[1][...], f[0][...], _cat_nt(r[0][...].astype(BF16), f[1:]))
        o[0][...] = dx + r[2][...]
        acc[0][...] += dg

    (dx,), (dg,) = _rows_call(name, _tile(S, tm), [da, x, dres], [_vec(g)] + ws, [(D, F32)], [((1, D), F32)], body)
    return dx, dg


def _norm_bwd_n(das, ws, x, g, dres, *, name, tm=256):
    S, D = x.shape
    n = ws[0][1][2]

    def body(r, f, o, acc, s):
        tot = None
        for p in range(N_CHIPS):
            y = _dot_nt(r[p // 2][:, _cols(p % 2, n)], f[1 + p][...])
            tot = y if tot is None else tot + y
        dx, dg = _rms_bwd_tile(r[2][...], f[0][...], tot)
        o[0][...] = dx + r[3][...]
        acc[0][...] += dg

    (dx,), (dg,) = _rows_call(name, _tile(S, tm), list(das) + [x, dres], [_vec(g)] + ws, [(D, F32)], [((1, D), F32)],
                              body)
    return dx, dg


def _ln_stats(v):
    mu = _mean(v)
    xc = v - mu
    rstd = lax.rsqrt(_mean(xc * xc) + EPS)
    return xc * rstd, rstd


_SHIFTS = 8
_CONV_ROWS = 64


def _fill_shifts(sh_ref, ext_ref, tm):
    sh_ref[0] = ext_ref[...]
    for s in range(1, _SHIFTS):
        sh_ref[s, 0:tm + CONV_HALO - _SHIFTS, :] = ext_ref[pl.ds(s, tm + CONV_HALO - _SHIFTS), :]


def _window(sh_ref, off, tm):
    return sh_ref[off % _SHIFTS, pl.ds(off - off % _SHIFTS, tm), :]


def _even_fwd(proj, wm, bcol, cw, cb, lg, lb, *, name):
    S = proj.shape[0]
    tm = _tile(S, 256)
    hb = tm // CONV_HALO
    nblk = tm // GMLP_BLOCK

    def body(p_ref, halo_ref, wm_ref, b_ref, cw_ref, cb_ref, lg_ref, lb_ref, mix_ref, hc_ref, hext_ref, hsh_ref):
        i = pl.program_id(0)
        gu, _ = _gelu(p_ref[:, 0:A_WIDTH])
        gv, _ = _gelu(p_ref[:, A_WIDTH:2 * A_WIDTH])
        vn, _ = _ln_stats(gv)
        vnb = vn.astype(BF16)
        for n in range(nblk):
            rows = slice(n * GMLP_BLOCK, (n + 1) * GMLP_BLOCK)
            for g in range(A_GROUPS):
                cols = slice(g * GMLP_BLOCK, (g + 1) * GMLP_BLOCK)
                sg = jnp.dot(wm_ref[g], vnb[rows, cols], preferred_element_type=F32) + b_ref[g]
                mix_ref[rows, cols] = (gu[rows, cols] * sg).astype(BF16)
        h = p_ref[:, 1024:1536] * _sigmoid(p_ref[:, 1536:2048])
        hh = halo_ref[:, 0:B_WIDTH] * _sigmoid(halo_ref[:, B_WIDTH:2 * B_WIDTH])
        hext_ref[0:CONV_HALO, :] = jnp.where(i > 0, hh, 0.0)
        hext_ref[CONV_HALO:CONV_HALO + tm, :] = h
        _fill_shifts(hsh_ref, hext_ref, tm)
        for r0 in range(0, tm, _CONV_ROWS):
            acc = jnp.zeros((_CONV_ROWS, B_WIDTH), F32)
            for k in range(CONV_WIDTH):
                acc = acc + cw_ref[k:k + 1, :] * _window(hsh_ref, r0 + k + CONV_HALO - CONV_WIDTH + 1, _CONV_ROWS)
            hc_ref[r0:r0 + _CONV_ROWS, :] = acc + cb_ref[...]
        hc = hc_ref[...]
        hhat, _ = _ln_stats(hc)
        hl = hhat * lg_ref[...] + lb_ref[...]
        mix_ref[:, A_WIDTH:A_WIDTH + B_WIDTH] = (hl * _sigmoid(hl)).astype(BF16)

    vec = pl.BlockSpec((1, B_WIDTH), lambda i: (0, 0))
    return pl.pallas_call(
        body, name=name, grid=(S // tm,),
        in_specs=[
            pl.BlockSpec((tm, 2048), lambda i: (i, 0)),
            pl.BlockSpec((CONV_HALO, 1024), lambda i: (jnp.maximum(i * hb - 1, 0), 1)),
            pl.BlockSpec((A_GROUPS, GMLP_BLOCK, GMLP_BLOCK), lambda i: (0, 0, 0)),
            pl.BlockSpec((A_GROUPS, GMLP_BLOCK, 1), lambda i: (0, 0, 0)),
            pl.BlockSpec((CONV_HALO, B_WIDTH), lambda i: (0, 0)),
            vec, vec, vec,
        ],
        out_specs=[pl.BlockSpec((tm, 1024), lambda i: (i, 0)), pl.BlockSpec((tm, B_WIDTH), lambda i: (i, 0))],
        out_shape=[jax.ShapeDtypeStruct((S, 1024), BF16), jax.ShapeDtypeStruct((S, B_WIDTH), F32)],
        scratch_shapes=[pltpu.VMEM((tm + CONV_HALO, B_WIDTH), F32),
                        pltpu.VMEM((_SHIFTS, tm + CONV_HALO, B_WIDTH), F32)],
        compiler_params=_params(("parallel",)),
    )(proj, proj, wm, bcol, cw, cb, lg, lb)


def _even_bwd1(proj, dmix, hc, wm, wmt, bcol, lg, lb, *, name):
    S = proj.shape[0]
    tm = _tile(S, 256)
    nblk = tm // GMLP_BLOCK

    def body(p_ref, dm_ref, hc_ref, wm_ref, wmt_ref, b_ref, lg_ref, lb_ref,
             dpa_ref, dhc_ref, dwm_ref, db_ref, dlg_ref, dlb_ref, dcb_ref, dgu_ref, dvn_ref):
        @pl.when(pl.program_id(0) == 0)
        def _():
            dwm_ref[...] = jnp.zeros_like(dwm_ref)
            db_ref[...] = jnp.zeros_like(db_ref)
            dlg_ref[...] = jnp.zeros_like(dlg_ref)
            dlb_ref[...] = jnp.zeros_like(dlb_ref)
            dcb_ref[...] = jnp.zeros_like(dcb_ref)

        au = p_ref[:, 0:A_WIDTH]
        av = p_ref[:, A_WIDTH:2 * A_WIDTH]
        gu, tu = _gelu(au)
        gv, tv = _gelu(av)
        vn, rstd = _ln_stats(gv)
        vnb = vn.astype(BF16)
        for n in range(nblk):
            rows = slice(n * GMLP_BLOCK, (n + 1) * GMLP_BLOCK)
            for g in range(A_GROUPS):
                cols = slice(g * GMLP_BLOCK, (g + 1) * GMLP_BLOCK)
                vb = vnb[rows, cols]
                sg = jnp.dot(wm_ref[g], vb, preferred_element_type=F32) + b_ref[g]
                da = dm_ref[rows, cols]
                dsg = da * gu[rows, cols]
                dgu_ref[rows, cols] = da * sg
                dsgb = dsg.astype(BF16)
                dwm_ref[g] += _dot_nt(dsgb, vb)
                db_ref[g] += jnp.sum(dsg, axis=1, keepdims=True)
                dvn_ref[rows, cols] = jnp.dot(wmt_ref[g], dsgb, preferred_element_type=F32)
        dvn = dvn_ref[...]
        dgv = rstd * (dvn - _mean(dvn) - vn * _mean(dvn * vn))
        dpa_ref[:, 0:A_WIDTH] = (dgu_ref[...] * _gelu_grad(au, tu)).astype(BF16)
        dpa_ref[:, A_WIDTH:2 * A_WIDTH] = (dgv * _gelu_grad(av, tv)).astype(BF16)
        hhat, rstd2 = _ln_stats(hc_ref[...])
        lgv = lg_ref[...]
        hl = hhat * lgv + lb_ref[...]
        s = _sigmoid(hl)
        dhl = dm_ref[:, A_WIDTH:A_WIDTH + B_WIDTH] * (s * (1.0 + hl * (1.0 - s)))
        dlg_ref[...] += jnp.sum(dhl * hhat, axis=0, keepdims=True)
        dlb_ref[...] += jnp.sum(dhl, axis=0, keepdims=True)
        dhh = dhl * lgv
        dhc = rstd2 * (dhh - _mean(dhh) - hhat * _mean(dhh * hhat))
        dcb_ref[...] += jnp.sum(dhc, axis=0, keepdims=True)
        dhc_ref[...] = dhc

    vec = pl.BlockSpec((1, B_WIDTH), lambda i: (0, 0))
    w3 = pl.BlockSpec((A_GROUPS, GMLP_BLOCK, GMLP_BLOCK), lambda i: (0, 0, 0))
    b3 = pl.BlockSpec((A_GROUPS, GMLP_BLOCK, 1), lambda i: (0, 0, 0))
    return pl.pallas_call(
        body, name=name, grid=(S // tm,),
        in_specs=[
            pl.BlockSpec((tm, 1024), lambda i: (i, 0)),
            pl.BlockSpec((tm, 1024), lambda i: (i, 0)),
            pl.BlockSpec((tm, B_WIDTH), lambda i: (i, 0)),
            w3, w3, b3, vec, vec,
        ],
        out_specs=[pl.BlockSpec((tm, 1024), lambda i: (i, 0)), pl.BlockSpec((tm, B_WIDTH), lambda i: (i, 0)),
                   w3, b3, vec, vec, vec],
        out_shape=[
            jax.ShapeDtypeStruct((S, 1024), BF16), jax.ShapeDtypeStruct((S, B_WIDTH), F32),
            jax.ShapeDtypeStruct((A_GROUPS, GMLP_BLOCK, GMLP_BLOCK), F32),
            jax.ShapeDtypeStruct((A_GROUPS, GMLP_BLOCK, 1), F32),
            jax.ShapeDtypeStruct((1, B_WIDTH), F32), jax.ShapeDtypeStruct((1, B_WIDTH), F32),
            jax.ShapeDtypeStruct((1, B_WIDTH), F32),
        ],
        scratch_shapes=[pltpu.VMEM((tm, A_WIDTH), F32), pltpu.VMEM((tm, A_WIDTH), F32)],
        compiler_params=_params(("arbitrary",)),
    )(proj, dmix, hc, wm, wmt, bcol, lg, lb)


def _even_bwd2(proj, dhc, cw, *, name):
    S = proj.shape[0]
    tm = _tile(S, 256)
    hb = tm // CONV_HALO
    nt = S // tm
    last_halo = S // CONV_HALO - 1
    lo = CONV_HALO - CONV_WIDTH + 1

    def body(p_ref, halo_ref, d_ref, dnext_ref, cw_ref, dpb_ref, dcw_ref, hext_ref, dext_ref, hsh_ref, dsh_ref):
        i = pl.program_id(0)

        @pl.when(i == 0)
        def _():
            dcw_ref[...] = jnp.zeros_like(dcw_ref)

        hh = halo_ref[:, 0:B_WIDTH] * _sigmoid(halo_ref[:, B_WIDTH:2 * B_WIDTH])
        hext_ref[0:CONV_HALO, :] = jnp.where(i > 0, hh, 0.0)
        hext_ref[CONV_HALO:CONV_HALO + tm, :] = p_ref[:, 0:B_WIDTH] * _sigmoid(p_ref[:, B_WIDTH:2 * B_WIDTH])
        dext_ref[0:tm, :] = d_ref[...]
        dext_ref[tm:tm + CONV_HALO, :] = jnp.where(i < nt - 1, dnext_ref[...], 0.0)
        _fill_shifts(hsh_ref, hext_ref, tm)
        _fill_shifts(dsh_ref, dext_ref, tm)
        for r0 in range(0, tm, _CONV_ROWS):
            rows = slice(r0, r0 + _CONV_ROWS)
            dhc_b = d_ref[rows, :]
            dh = jnp.zeros((_CONV_ROWS, B_WIDTH), F32)
            for k in range(CONV_WIDTH):
                dh = dh + cw_ref[k:k + 1, :] * _window(dsh_ref, r0 + CONV_WIDTH - 1 - k, _CONV_ROWS)
                dcw_ref[k:k + 1, :] += jnp.sum(dhc_b * _window(hsh_ref, r0 + k + lo, _CONV_ROWS), axis=0,
                                               keepdims=True)
            ba_b = p_ref[rows, 0:B_WIDTH]
            sg_b = _sigmoid(p_ref[rows, B_WIDTH:2 * B_WIDTH])
            dpb_ref[rows, 0:B_WIDTH] = (dh * sg_b).astype(BF16)
            dpb_ref[rows, B_WIDTH:2 * B_WIDTH] = (dh * ba_b * sg_b * (1.0 - sg_b)).astype(BF16)

    return pl.pallas_call(
        body, name=name, grid=(nt,),
        in_specs=[
            pl.BlockSpec((tm, 1024), lambda i: (i, 1)),
            pl.BlockSpec((CONV_HALO, 1024), lambda i: (jnp.maximum(i * hb - 1, 0), 1)),
            pl.BlockSpec((tm, B_WIDTH), lambda i: (i, 0)),
            pl.BlockSpec((CONV_HALO, B_WIDTH), lambda i: (jnp.minimum((i + 1) * hb, last_halo), 0)),
            pl.BlockSpec((CONV_HALO, B_WIDTH), lambda i: (0, 0)),
        ],
        out_specs=[pl.BlockSpec((tm, 1024), lambda i: (i, 0)), pl.BlockSpec((CONV_HALO, B_WIDTH), lambda i: (0, 0))],
        out_shape=[jax.ShapeDtypeStruct((S, 1024), BF16), jax.ShapeDtypeStruct((CONV_HALO, B_WIDTH), F32)],
        scratch_shapes=[pltpu.VMEM((tm + CONV_HALO, B_WIDTH), F32), pltpu.VMEM((tm + CONV_HALO, B_WIDTH), F32),
                        pltpu.VMEM((_SHIFTS, tm + CONV_HALO, B_WIDTH), F32),
                        pltpu.VMEM((_SHIFTS, tm + CONV_HALO, B_WIDTH), F32)],
        compiler_params=_params(("arbitrary",)),
    )(proj, proj, dhc, dhc, cw)


_CA_SCALE = CA_HEAD_DIM ** -0.5


def _softmax_rows(s):
    e = jnp.exp(s - jnp.max(s, axis=-1, keepdims=True))
    return e / jnp.sum(e, axis=-1, keepdims=True)


def _attn_fwd(q, k, v, *, name):
    S = q.shape[0]

    def body(r, f, o, acc, s):
        for h in range(CA_HEADS):
            cols = _cols(h, CA_HEAD_DIM)
            p = _softmax_rows(_dot_nt(r[0][:, cols], f[0][:, cols]) * _CA_SCALE)
            o[0][:, cols] = _dot(p.astype(BF16), f[1][:, cols]).astype(BF16)

    (o_,), _ = _rows_call(name, _tile(S, 512), [q], [k, v], [(D_MODEL, BF16)], [], body)
    return o_


def _attn_bwd(dy, wo, q, k, v, *, name):
    S = q.shape[0]
    M = k.shape[0]

    def body(r, f, o, acc, s):
        dyb = r[0][...].astype(BF16)
        for h in range(CA_HEADS):
            cols = _cols(h, CA_HEAD_DIM)
            qh = r[1][:, cols]
            kh = f[0][:, cols]
            vh = f[1][:, cols]
            doh = _dot_nt(dyb, f[2 + h][...]).astype(BF16)
            p = _softmax_rows(_dot_nt(qh, kh) * _CA_SCALE)
            acc[1][:, cols] += _dot_tn(p.astype(BF16), doh)
            dp = _dot_nt(doh, vh)
            ds = (p * (dp - jnp.sum(dp * p, axis=-1, keepdims=True)) * _CA_SCALE).astype(BF16)
            o[0][:, cols] = _dot(ds, kh).astype(BF16)
            acc[0][:, cols] += _dot_tn(ds, qh)

    (dq,), (dk, dv) = _rows_call(name, _tile(S, 512), [dy, q], [k, v] + wo, [(D_MODEL, BF16)],
                                 [((M, D_MODEL), F32), ((M, D_MODEL), F32)], body)
    return dq, dk, dv


_STATE_TILE = 2 * STATE_ROWS
N_SETS = 4
SET_CH = C_WIDTH // N_SETS
SET_COLS = N_STATE // N_SETS // STATE_LANES


def _set_groups(j):
    return [SET_COLS * j + c for c in range(SET_COLS)] + [STATE_ROWS + SET_COLS * j + c for c in range(SET_COLS)]


def _pack_state(re, im):
    hi = lax.bitcast_convert_type(re.astype(BF16).astype(F32), jnp.uint32)
    lo = lax.bitcast_convert_type(im.astype(BF16).astype(F32), jnp.uint32) >> 16
    return hi | lo


def _unpack_state(word):
    re = lax.bitcast_convert_type(word & jnp.uint32(0xFFFF0000), F32)
    im = lax.bitcast_convert_type(word << 16, F32)
    return re, im


def _state_set(ref, tm, j):
    parts = [_unpack_state(ref[:, SET_COLS * j + c, :, :].reshape(tm, STATE_LANES)) for c in range(SET_COLS)]
    return jnp.concatenate([p[0].astype(BF16) for p in parts] + [p[1].astype(BF16) for p in parts], axis=1)


def _s5_readout(xs, cset, u, d, *, name, tm=256):
    tm = _tile(u.shape[0], tm)

    def body(r, f, o, acc, s):
        y0 = jnp.concatenate([_dot(_state_set(r[0], tm, j), f[0][j]) for j in range(N_SETS)], axis=1)
        y = y0 + f[1][...] * r[1][...]
        o[0][...] = y
        o[1][...] = _gelu(y)[0].astype(BF16)

    (y, yg), _ = _rows_call(name, tm, [xs, u], [cset, d], [(C_WIDTH, F32), (C_WIDTH, BF16)], [], body)
    return y, yg


def _state_grad_sets(a, st, *, name, ts=256):
    ts = _tile(a.shape[0], ts)

    def body(r, f, o, acc, s):
        for j in range(N_SETS):
            acc[0][j] += _dot_tn(r[0][:, _cols(j, SET_CH)].astype(BF16), _state_set(r[1], ts, j))

    _, (out,) = _rows_call(name, ts, [a, st], [], [], [((N_SETS, SET_CH, 2 * N_STATE // N_SETS), F32)], body)
    return out


def _glu_out(yg, ws, x, *, name, tm=512):
    n = ws[0][1][2]

    def body(r, f, o, acc, s):
        ygv = r[0][...]
        ov = [_dot(ygv, f[p][...]) for p in range(N_CHIPS)]
        for p in range(N_CHIPS):
            o[0][:, _cols(p, n)] = ov[p].astype(BF16)
        for p in range(2):
            o[1][:, _cols(p, n)] = r[1][:, _cols(p, n)] + ov[p] * _sigmoid(ov[2 + p])

    (o_, y), _ = _rows_call(name, _tile(x.shape[0], tm), [yg, x], ws, [(2 * D_MODEL, BF16), (D_MODEL, F32)], [], body)
    return o_, y


def _glu_out_bwd(o_, dy, ws, y, u, d, *, name, tm=256):
    n = ws[0][1][2]

    def body(r, f, o, acc, s):
        o1 = r[0][:, 0:D_MODEL].astype(F32)
        sg = _sigmoid(r[0][:, D_MODEL:2 * D_MODEL].astype(F32))
        dyv = r[1][...]
        do1 = (dyv * sg).astype(BF16)
        do2 = (dyv * o1 * sg * (1.0 - sg)).astype(BF16)
        o[0][:, 0:D_MODEL] = do1
        o[0][:, D_MODEL:2 * D_MODEL] = do2
        dyg = None
        for p in range(N_CHIPS):
            t = _dot_nt((do1 if p < 2 else do2)[:, _cols(p % 2, n)], f[1 + p][...])
            dyg = t if dyg is None else dyg + t
        yv = r[2][...]
        dys = dyg * _gelu_grad(yv, _gelu(yv)[1])
        o[1][...] = dys.astype(BF16)
        o[2][...] = f[0][...] * dys
        acc[0][...] += jnp.sum(dys * r[3][...], axis=0, keepdims=True)

    (do, dys, dus), (dd,) = _rows_call(name, _tile(dy.shape[0], tm), [o_, dy, y, u], [d] + ws,
                                       [(2 * D_MODEL, BF16), (C_WIDTH, BF16), (C_WIDTH, F32)], [((1, C_WIDTH), F32)],
                                       body)
    return do, dys, dus, dd


def _s5_in_bwd(gs, bset, dus, ws, x, g, dres, *, name, tm=256):
    D = x.shape[1]
    tm = _tile(x.shape[0], tm)

    def body(r, f, o, acc, s):
        du0 = jnp.concatenate([_dot_nt(_state_set(r[0], tm, j), f[1][j]) for j in range(N_SETS)], axis=1)
        du = (du0 + r[1][...]).astype(BF16)
        o[0][...] = du
        dx, dg = _rms_bwd_tile(r[2][...], f[0][...], _cat_nt(du, f[2:]))
        o[1][...] = dx + r[3][...]
        acc[0][...] += dg

    (du, dx), (dg,) = _rows_call(name, tm, [gs, dus, x, dres], [_vec(g), bset] + ws,
                                 [(C_WIDTH, BF16), (D, F32)], [((1, D), F32)], body)
    return du, dx, dg


_SCAN_CHUNK = 256
_RE = slice(0, STATE_ROWS)
_IM = slice(STATE_ROWS, 2 * STATE_ROWS)
assert SCAN_BLOCK == 8


def _token(g, i, rows):
    return pl.ds(pl.multiple_of(g * (rows * SCAN_BLOCK), rows * SCAN_BLOCK) + i, rows, stride=SCAN_BLOCK)


def _fill_chunk(s3, a_ref, wset, tc, nt):
    for j in range(N_SETS):
        av = a_ref[:, _cols(j, SET_CH)].astype(BF16)
        y = _dot_nt(av, wset[j]) if nt else _dot(av, wset[j])
        for k, c in enumerate(_set_groups(j)):
            s3[:, 8 * c:8 * (c + 1), :] = y[:, _cols(k, STATE_LANES)].reshape(tc // 8, 8, STATE_LANES)


def _chunk_token(s3, g, i):
    return s3[g, pl.ds(i, _STATE_TILE, stride=SCAN_BLOCK), :]


def _scan_fwd(u, bset, pw, *, name):
    S = u.shape[0]
    tc = _tile(S, _SCAN_CHUNK, 8)

    def body(u_ref, bset_ref, pw_ref, xs_ref, st_ref, s3):
        @pl.when(pl.program_id(0) == 0)
        def _():
            st_ref[...] = jnp.zeros_like(st_ref)

        _fill_chunk(s3, u_ref, bset_ref, tc, nt=False)
        ar = pw_ref[0, _RE, :]
        ai = pw_ref[0, _IM, :]

        def block(g, carry):
            xr, xi = carry
            cr = ci = nr = ni = None
            for j in range(SCAN_BLOCK):
                b = _chunk_token(s3, g, j)
                br, bi = b[_RE], b[_IM]
                cr, ci = (br, bi) if j == 0 else (ar * cr - ai * ci + br, ar * ci + ai * cr + bi)
                pr, pi = pw_ref[j, _RE, :], pw_ref[j, _IM, :]
                nr = pr * xr - pi * xi + cr
                ni = pr * xi + pi * xr + ci
                xs_ref[_token(g, j, STATE_ROWS), :] = _pack_state(nr, ni)
            return nr, ni

        xr, xi = lax.fori_loop(0, tc // SCAN_BLOCK, block, (st_ref[_RE, :], st_ref[_IM, :]), unroll=4)
        st_ref[_RE, :] = xr
        st_ref[_IM, :] = xi

    return pl.pallas_call(
        body, name=name, grid=(S // tc,),
        in_specs=[pl.BlockSpec((tc, u.shape[1]), lambda i: (i, 0)), pl.BlockSpec(bset.shape, lambda i: (0, 0, 0)),
                  pl.BlockSpec(pw.shape, lambda i: (0, 0, 0))],
        out_specs=pl.BlockSpec((tc * STATE_ROWS, STATE_LANES), lambda i: (i, 0)),
        out_shape=jax.ShapeDtypeStruct((S * STATE_ROWS, STATE_LANES), jnp.uint32),
        scratch_shapes=[pltpu.VMEM((2 * STATE_ROWS, STATE_LANES), F32),
                        pltpu.VMEM((tc // 8, _STATE_TILE * 8, STATE_LANES), F32)],
        compiler_params=_params(("arbitrary",)),
    )(u, bset, pw)


def _scan_bwd(dys, cset, xs, pw, *, name):
    S = dys.shape[0]
    tc = _tile(S, _SCAN_CHUNK, 8)
    nc = S // tc

    def body(dys_ref, cset_ref, xs_ref, pw_ref, g_ref, da_ref, st_ref, s3):
        @pl.when(pl.program_id(0) == 0)
        def _():
            st_ref[...] = jnp.zeros_like(st_ref)
            da_ref[...] = jnp.zeros_like(da_ref)

        _fill_chunk(s3, dys_ref, cset_ref, tc, nt=True)
        ar = pw_ref[0, _RE, :]
        ai = pw_ref[0, _IM, :]

        def block(k, carry):
            gr, gi, dar, dai = carry
            g = tc // SCAN_BLOCK - 1 - k
            cr = ci = None
            pgr, pgi = gr, gi
            for j in range(SCAN_BLOCK):
                i = SCAN_BLOCK - 1 - j
                xr, xi = _unpack_state(xs_ref[_token(g, i, STATE_ROWS), :])
                dar = dar + pgr * xr + pgi * xi
                dai = dai + pgi * xr - pgr * xi
                d = _chunk_token(s3, g, i)
                dr, di = d[_RE], d[_IM]
                cr, ci = (dr, di) if j == 0 else (ar * cr + ai * ci + dr, ar * ci - ai * cr + di)
                pr, pi = pw_ref[j, _RE, :], pw_ref[j, _IM, :]
                pgr = pr * gr + pi * gi + cr
                pgi = pr * gi - pi * gr + ci
                g_ref[_token(g, i, STATE_ROWS), :] = _pack_state(pgr, pgi)
            return pgr, pgi, dar, dai

        init = (st_ref[_RE, :], st_ref[_IM, :], da_ref[_RE, :], da_ref[_IM, :])
        gr, gi, dar, dai = lax.fori_loop(0, tc // SCAN_BLOCK, block, init, unroll=4)
        st_ref[_RE, :] = gr
        st_ref[_IM, :] = gi
        da_ref[_RE, :] = dar
        da_ref[_IM, :] = dai

    packed = pl.BlockSpec((tc * STATE_ROWS, STATE_LANES), lambda i: (nc - 1 - i, 0))
    vec = pl.BlockSpec((2 * STATE_ROWS, STATE_LANES), lambda i: (0, 0))
    return pl.pallas_call(
        body, name=name, grid=(nc,),
        in_specs=[pl.BlockSpec((tc, dys.shape[1]), lambda i: (nc - 1 - i, 0)),
                  pl.BlockSpec(cset.shape, lambda i: (0, 0, 0)), packed, pl.BlockSpec(pw.shape, lambda i: (0, 0, 0))],
        out_specs=[packed, vec],
        out_shape=[jax.ShapeDtypeStruct(xs.shape, jnp.uint32), jax.ShapeDtypeStruct((2 * STATE_ROWS, STATE_LANES), F32)],
        scratch_shapes=[pltpu.VMEM((2 * STATE_ROWS, STATE_LANES), F32),
                        pltpu.VMEM((tc // 8, _STATE_TILE * 8, STATE_LANES), F32)],
        compiler_params=_params(("arbitrary",)),
    )(dys, cset, xs, pw)


def _loss_head(x, g, target, *, name):
    S, D = x.shape

    def body(r, f, o, acc, s):
        xv = r[0][...]
        gv = f[0][...]
        rs = lax.rsqrt(_mean(xv * xv) + EPS)
        xh = xv * rs
        err = xh * gv - r[1][...]
        acc[1][...] += 0.5 * jnp.sum(_mean(err * err), axis=0, keepdims=True)
        dy = err * (1.0 / D)
        dyg = dy * gv
        o[0][...] = rs * (dyg - xh * _mean(dyg * xh))
        acc[0][...] += jnp.sum(dy * xh, axis=0, keepdims=True)

    (dx,), (dg, loss) = _rows_call(name, _tile(S, 256, 8), [x, target], [_vec(g)], [(D, F32)],
                                   [((1, D), F32), ((1, 128), F32)], body)
    return dx, dg, loss


_ADAM_C1 = 1.0 - ADAM_B1 ** ADAM_STEP
_ADAM_C2 = 1.0 - ADAM_B2 ** ADAM_STEP
_ONE_BLOCK_BYTES = 8 * 1024 * 1024


def _adamw_math(w, g, m, v):
    nm = ADAM_B1 * m + (1.0 - ADAM_B1) * g
    nv = ADAM_B2 * v + (1.0 - ADAM_B2) * (g * g)
    m_hat = nm / _ADAM_C1
    v_hat = nv / _ADAM_C2
    return -ADAM_LR * (m_hat / (jnp.sqrt(v_hat) + ADAM_EPS) + ADAM_WD * w), nm, nv


def _adamw_shard(w, gsrc, m, v, *, name):
    R, C = w.shape
    n_l = len(gsrc)
    rows = R // n_l
    tr = rows
    for _, r0 in gsrc:
        tr = math.gcd(tr, r0) if r0 else tr
    tr = _tile(tr, 256, 8) if tr > 256 else tr
    nb = rows // tr
    assert rows % tr == 0 and all(r0 % tr == 0 for _, r0 in gsrc)

    def body(*refs):
        w_ref, g_refs, (m_ref, v_ref, go_ref, d_ref, nm_ref, nv_ref) = refs[0], refs[1:1 + n_l], refs[1 + n_l:]
        layer = pl.program_id(0) // nb
        gv = g_refs[0][...]
        for l in range(1, n_l):
            gv = jnp.where(layer == l, g_refs[l][...], gv)
        go_ref[...] = gv
        d_ref[...], nm_ref[...], nv_ref[...] = _adamw_math(w_ref[...], gv, m_ref[...], v_ref[...])

    def g_spec(l, r0):
        return pl.BlockSpec((tr, C), lambda i: (r0 // tr + jnp.clip(i - l * nb, 0, nb - 1), 0))

    blk = pl.BlockSpec((tr, C), lambda i: (i, 0))
    out = jax.ShapeDtypeStruct((R, C), F32)
    return pl.pallas_call(
        body, name=name, grid=(R // tr,),
        in_specs=[blk] + [g_spec(l, r0) for l, (_, r0) in enumerate(gsrc)] + [blk, blk], out_specs=[blk] * 4,
        out_shape=[out] * 4, compiler_params=_params(("parallel",)),
    )(w, *[g for g, _ in gsrc], m, v)


def _adamw_small(ws, gs, ms, vs, *, name):
    n = len(ws)

    def body(*refs):
        w_r, g_r, m_r, v_r = refs[:n], refs[n:2 * n], refs[2 * n:3 * n], refs[3 * n:4 * n]
        d_r, nm_r, nv_r = refs[4 * n:5 * n], refs[5 * n:6 * n], refs[6 * n:7 * n]
        for k in range(n):
            d_r[k][...], nm_r[k][...], nv_r[k][...] = _adamw_math(w_r[k][...], g_r[k][...], m_r[k][...], v_r[k][...])

    vm = pl.BlockSpec(memory_space=pltpu.VMEM)
    out = [jax.ShapeDtypeStruct(w.shape, F32) for w in ws]
    res = pl.pallas_call(body, name=name, in_specs=[vm] * (4 * n), out_specs=[vm] * (3 * n), out_shape=out * 3,
                         compiler_params=pltpu.CompilerParams(vmem_limit_bytes=VMEM_LIMIT))(*ws, *gs, *ms, *vs)
    return res[:n], res[n:2 * n], res[2 * n:]


def _sum_slots(x, *, name):
    n, R, C = x.shape
    tr = R if (n + 1) * R * C * 4 <= _ONE_BLOCK_BYTES else _tile(R, 256, 8)

    def body(x_ref, o_ref):
        acc = x_ref[0]
        for k in range(1, n):
            acc = acc + x_ref[k]
        o_ref[...] = acc

    return pl.pallas_call(
        body, name=name, grid=(R // tr,),
        in_specs=[pl.BlockSpec((n, tr, C), lambda i: (0, i, 0))], out_specs=pl.BlockSpec((tr, C), lambda i: (i, 0)),
        out_shape=jax.ShapeDtypeStruct((R, C), F32), compiler_params=_params(("parallel",)),
    )(x)


def _pair_sum(g, r, where, *, name):
    n, R, C = g.shape
    Rh = R // 2
    tr = _tile(Rh, 256, 8)
    nb = Rh // tr

    def body(where_ref, g_ref, r_ref, o_ref):
        o_ref[...] = (g_ref[...] + r_ref[...]).astype(BF16)

    def slot(p, w):
        return p + jnp.where(p >= w[0], 1, 0)

    return pl.pallas_call(
        body, name=name,
        grid_spec=pltpu.PrefetchScalarGridSpec(
            num_scalar_prefetch=1, grid=(n - 1, nb),
            in_specs=[pl.BlockSpec((1, tr, C), lambda p, i, w: (slot(p, w), w[1] * nb + i, 0)),
                      pl.BlockSpec((1, tr, C), lambda p, i, w: (slot(p, w), i, 0))],
            out_specs=pl.BlockSpec((1, tr, C), lambda p, i, w: (slot(p, w), i, 0)),
        ),
        out_shape=jax.ShapeDtypeStruct((n, Rh, C), BF16), compiler_params=_params(("parallel", "parallel")),
    )(where, g, r)


def _chip_sum(g, r, slots, where, *, name):
    n, R, C = g.shape
    Rh = R // 2
    tr = _tile(Rh, 256, 8)
    nb = Rh // tr

    def body(w_ref, g_ref, r_ref, s_ref, o_ref):
        acc = g_ref[0] + r_ref[0]
        for k in range(slots.shape[0]):
            acc = acc + s_ref[k].astype(F32)
        o_ref[...] = acc

    return pl.pallas_call(
        body, name=name,
        grid_spec=pltpu.PrefetchScalarGridSpec(
            num_scalar_prefetch=1, grid=(nb,),
            in_specs=[pl.BlockSpec((1, tr, C), lambda i, w: (w[0], w[1] * nb + i, 0)),
                      pl.BlockSpec((1, tr, C), lambda i, w: (w[0], i, 0)),
                      pl.BlockSpec((slots.shape[0], tr, C), lambda i, w: (0, i, 0))],
            out_specs=pl.BlockSpec((tr, C), lambda i, w: (w[1] * nb + i, 0)),
        ),
        out_shape=jax.ShapeDtypeStruct((R, C), F32), compiler_params=_params(("parallel",)),
    )(where, g, r, slots)


ANY = pl.BlockSpec(memory_space=pl.ANY)


def _place():
    return lax.axis_index("x"), lax.axis_index("y"), lax.axis_index("c")


def _other_chips(x, y):
    return [(1 - x, y), (x, 1 - y), (1 - x, 1 - y)]


def _allgather_small(v, *, name):
    R, C = v.shape

    def body(x_ref, out_ref, send_sems, recv_sems, local_sem):
        x, y, c = _place()
        me, sibling = (x, y, c), (x, y, 1 - c)
        chips = _other_chips(x, y)

        def rows(px, py, pc):
            return out_ref.at[pl.ds((4 * px + 2 * py + pc) * R, R), :]

        def copy(k, block, to, src=None):
            return pltpu.make_async_remote_copy(
                src_ref=rows(*block) if src is None else src, dst_ref=rows(*block),
                send_sem=send_sems.at[k], recv_sem=recv_sems.at[k], device_id=to, device_id_type=MESH)

        mine = pltpu.make_async_copy(x_ref, rows(*me), local_sem)
        mine.start()
        first = [copy(0, me, sibling, src=x_ref)]
        first += [copy(1 + j, me, (*chip, c), src=x_ref) for j, chip in enumerate(chips)]
        for cp in first:
            cp.start()
        passed = [copy(4 + j, (*chip, c), sibling) for j, chip in enumerate(chips)]
        for j, chip in enumerate(chips):
            copy(1 + j, (*chip, c), me).wait_recv()
            passed[j].start()
        copy(0, sibling, me).wait_recv()
        for j, chip in enumerate(chips):
            copy(4 + j, (*chip, 1 - c), me).wait_recv()
        for cp in first + passed:
            cp.wait_send()
        mine.wait()

    return pl.pallas_call(
        body, name=name, out_shape=jax.ShapeDtypeStruct((N_DEV * R, C), v.dtype),
        in_specs=[pl.BlockSpec(memory_space=pltpu.VMEM)], out_specs=pl.BlockSpec(memory_space=pltpu.VMEM),
        scratch_shapes=[pltpu.SemaphoreType.DMA((7,)), pltpu.SemaphoreType.DMA((7,)), pltpu.SemaphoreType.DMA],
        compiler_params=pltpu.CompilerParams(vmem_limit_bytes=VMEM_LIMIT),
    )(v)


def _aliased_comm_call(body, bufs, n_sems, *, name):
    n = len(bufs)
    return pl.pallas_call(
        body, name=name, out_shape=[jax.ShapeDtypeStruct(b.shape, b.dtype) for b in bufs],
        in_specs=[ANY] * n, out_specs=[ANY] * n, input_output_aliases={k: k for k in range(n)},
        scratch_shapes=[pltpu.SemaphoreType.DMA((n_sems,)), pltpu.SemaphoreType.DMA((n_sems,))],
    )(*bufs)


HBM = pl.BlockSpec(memory_space=pltpu.HBM)
SEM = pl.BlockSpec(memory_space=pltpu.SEMAPHORE)
_SPLIT = pltpu.CompilerParams(has_side_effects=pltpu.SideEffectType.DATAFLOW_SIDE_EFFECTING)


def _in_hbm(arrs):
    return [pltpu.with_memory_space_constraint(a, pltpu.HBM) for a in arrs]


def _gather_ici_start(bufs, after, *, name):
    n = len(bufs)

    def body(*refs):
        send_sems, recv_sems, outs, token = refs[n + 1], refs[n + 2], refs[n + 3:2 * n + 3], refs[2 * n + 3]
        x, y, c = _place()
        for b in range(n):
            rh = bufs[b].shape[1] // 2
            part = outs[b].at[2 * x + y, pl.ds(c * rh, rh), :]
            for j, chip in enumerate(_other_chips(x, y)):
                pltpu.make_async_remote_copy(src_ref=part, dst_ref=part, send_sem=send_sems.at[3 * b + j],
                                             recv_sem=recv_sems.at[3 * b + j], device_id=(*chip, c),
                                             device_id_type=MESH).start()
        token[...] = jnp.zeros_like(token)

    res = pl.pallas_call(
        body, name=name,
        out_shape=(pltpu.SemaphoreType.DMA((3 * n,)), pltpu.SemaphoreType.DMA((3 * n,)),
                   *[pltpu.HBM(b.shape, b.dtype) for b in bufs], jax.ShapeDtypeStruct((8, 128), F32)),
        in_specs=[HBM] * n + [ANY], out_specs=(SEM, SEM, *[HBM] * n, pl.BlockSpec(memory_space=pltpu.VMEM)),
        input_output_aliases={k: k + 2 for k in range(n)}, compiler_params=_SPLIT,
    )(*_in_hbm(bufs), after)
    return res[0], res[1], list(res[2:2 + n]), res[2 + n]


def _gather_ici_wait(send_sems, recv_sems, bufs, after, *, name):
    n = len(bufs)

    def body(*refs):
        ins, ss, rs = refs[:n], refs[n], refs[n + 1]
        x, y, c = _place()
        for b in range(n):
            rh = bufs[b].shape[1] // 2
            mine = ins[b].at[2 * x + y, pl.ds(c * rh, rh), :]
            for j, (cx, cy) in enumerate(_other_chips(x, y)):
                theirs = ins[b].at[2 * cx + cy, pl.ds(c * rh, rh), :]
                cp = pltpu.make_async_remote_copy(src_ref=mine, dst_ref=theirs, send_sem=ss.at[3 * b + j],
                                                  recv_sem=rs.at[3 * b + j], device_id=(cx, cy, c),
                                                  device_id_type=MESH)
                cp.wait_send()
                cp.wait_recv()

    return list(pl.pallas_call(
        body, name=name, out_shape=[pltpu.HBM(b.shape, b.dtype) for b in bufs],
        in_specs=[HBM] * n + [SEM, SEM, ANY], out_specs=[HBM] * n,
        input_output_aliases={k: k for k in range(n)}, compiler_params=_SPLIT,
    )(*bufs, send_sems, recv_sems, after))


def _gather_forward(bufs, *, name):
    n = len(bufs)

    def body(*refs):
        outs, send_sems, recv_sems = refs[n:2 * n], refs[2 * n], refs[2 * n + 1]
        x, y, c = _place()

        def copy(b, j, chip, hc):
            rh = bufs[b].shape[1] // 2
            part = outs[b].at[2 * chip[0] + chip[1], pl.ds(hc * rh, rh), :]
            return pltpu.make_async_remote_copy(src_ref=part, dst_ref=part, send_sem=send_sems.at[3 * b + j],
                                                recv_sem=recv_sems.at[3 * b + j], device_id=(x, y, 1 - c),
                                                device_id_type=MESH)

        sends = [copy(b, j, chip, c) for b in range(n) for j, chip in enumerate(_other_chips(x, y))]
        for cp in sends:
            cp.start()
        for b in range(n):
            for j, chip in enumerate(_other_chips(x, y)):
                copy(b, j, chip, 1 - c).wait_recv()
        for cp in sends:
            cp.wait_send()

    return _aliased_comm_call(body, bufs, 3 * n, name=name)


def _chip_exchange_start(hs, *, name):
    n = len(hs)
    lands = [lax.empty((3,) + h.shape[1:], h.dtype) for h in hs]

    def body(*refs):
        send_sems, recv_sems = refs[2 * n], refs[2 * n + 1]
        h_out, l_out, token = refs[2 * n + 2:3 * n + 2], refs[3 * n + 2:4 * n + 2], refs[4 * n + 2]
        x, y, c = _place()
        for b in range(n):
            for j, (cx, cy) in enumerate(_other_chips(x, y)):
                pltpu.make_async_remote_copy(src_ref=h_out[b].at[2 * cx + cy], dst_ref=l_out[b].at[j],
                                             send_sem=send_sems.at[3 * b + j], recv_sem=recv_sems.at[3 * b + j],
                                             device_id=(cx, cy, c), device_id_type=MESH).start()
        token[...] = jnp.zeros_like(token)

    res = pl.pallas_call(
        body, name=name,
        out_shape=(pltpu.SemaphoreType.DMA((3 * n,)), pltpu.SemaphoreType.DMA((3 * n,)),
                   *[pltpu.HBM(a.shape, a.dtype) for a in hs + lands], jax.ShapeDtypeStruct((8, 128), F32)),
        in_specs=[HBM] * (2 * n), out_specs=(SEM, SEM, *[HBM] * (2 * n), pl.BlockSpec(memory_space=pltpu.VMEM)),
        input_output_aliases={k: k + 2 for k in range(2 * n)}, compiler_params=_SPLIT,
    )(*_in_hbm(hs + lands))
    return res[0], res[1], list(res[2:2 + n]), list(res[2 + n:2 + 2 * n]), res[2 + 2 * n]


def _chip_exchange_wait(send_sems, recv_sems, hs, lands, after, *, name):
    n = len(hs)

    def body(*refs):
        h_in, l_in, ss, rs = refs[:n], refs[n:2 * n], refs[2 * n], refs[2 * n + 1]
        x, y, c = _place()
        for b in range(n):
            for j, (cx, cy) in enumerate(_other_chips(x, y)):
                cp = pltpu.make_async_remote_copy(src_ref=h_in[b].at[2 * cx + cy], dst_ref=l_in[b].at[j],
                                                  send_sem=ss.at[3 * b + j], recv_sem=rs.at[3 * b + j],
                                                  device_id=(cx, cy, c), device_id_type=MESH)
                cp.wait_send()
                cp.wait_recv()

    res = pl.pallas_call(
        body, name=name, out_shape=[pltpu.HBM(a.shape, a.dtype) for a in hs + lands],
        in_specs=[HBM] * (2 * n) + [SEM, SEM, ANY], out_specs=[HBM] * (2 * n),
        input_output_aliases={k: k for k in range(2 * n)}, compiler_params=_SPLIT,
    )(*hs, *lands, send_sems, recv_sems, after)
    return list(res[n:])


def _peers(x, y, c):
    return [((1 - x) if fx else x, (1 - y) if fy else y, (1 - c) if fc else c)
            for fx in (0, 1) for fy in (0, 1) for fc in (0, 1) if fx or fy or fc]


def _all_to_all_start(slab, after, *, name):
    land = lax.empty((N_DEV,) + slab.shape, slab.dtype)

    def body(slab_in, land_in, after_ref, send_sems, recv_sems, slab_out, land_out, token):
        x, y, c = _place()
        for k, peer in enumerate(_peers(x, y, c)):
            pltpu.make_async_remote_copy(src_ref=slab_out, dst_ref=land_out.at[4 * x + 2 * y + c],
                                         send_sem=send_sems.at[k], recv_sem=recv_sems.at[k], device_id=peer,
                                         device_id_type=MESH).start()
        token[...] = jnp.zeros_like(token)

    return pl.pallas_call(
        body, name=name,
        out_shape=(pltpu.SemaphoreType.DMA((N_DEV - 1,)), pltpu.SemaphoreType.DMA((N_DEV - 1,)),
                   pltpu.HBM(slab.shape, slab.dtype), pltpu.HBM(land.shape, land.dtype),
                   jax.ShapeDtypeStruct((8, 128), F32)),
        in_specs=[HBM, HBM, ANY], out_specs=(SEM, SEM, HBM, HBM, pl.BlockSpec(memory_space=pltpu.VMEM)),
        input_output_aliases={0: 2, 1: 3}, compiler_params=_SPLIT,
    )(*_in_hbm([slab, land]), after)


def _all_to_all_wait(send_sems, recv_sems, slab, land, after, *, name):
    def body(slab_in, land_in, ss, rs, after_ref, slab_out, land_out):
        x, y, c = _place()
        for k, (px, py, pc) in enumerate(_peers(x, y, c)):
            cp = pltpu.make_async_remote_copy(src_ref=slab_in, dst_ref=land_in.at[4 * px + 2 * py + pc],
                                              send_sem=ss.at[k], recv_sem=rs.at[k], device_id=(px, py, pc),
                                              device_id_type=MESH)
            cp.wait_send()
            cp.wait_recv()

    return pl.pallas_call(
        body, name=name, out_shape=[pltpu.HBM(slab.shape, slab.dtype), pltpu.HBM(land.shape, land.dtype)],
        in_specs=[HBM, HBM, SEM, SEM, ANY], out_specs=[HBM, HBM], input_output_aliases={0: 0, 1: 1},
        compiler_params=_SPLIT,
    )(slab, land, send_sems, recv_sems, after)


def _pair_exchange_start(gs, *, name):
    n = len(gs)
    lands = [lax.empty((g.shape[0], g.shape[1] // 2, g.shape[2]), g.dtype) for g in gs]

    def body(*refs):
        send_sems, recv_sems = refs[2 * n], refs[2 * n + 1]
        g_out, l_out, token = refs[2 * n + 2:3 * n + 2], refs[3 * n + 2:4 * n + 2], refs[4 * n + 2]
        x, y, c = _place()
        for b in range(n):
            rh = gs[b].shape[1] // 2
            pltpu.make_async_remote_copy(src_ref=g_out[b].at[:, pl.ds((1 - c) * rh, rh), :], dst_ref=l_out[b],
                                         send_sem=send_sems.at[b], recv_sem=recv_sems.at[b],
                                         device_id=(x, y, 1 - c), device_id_type=MESH).start()
        token[...] = jnp.zeros_like(token)

    res = pl.pallas_call(
        body, name=name,
        out_shape=(pltpu.SemaphoreType.DMA((n,)), pltpu.SemaphoreType.DMA((n,)),
                   *[pltpu.HBM(a.shape, a.dtype) for a in gs + lands], jax.ShapeDtypeStruct((8, 128), F32)),
        in_specs=[HBM] * (2 * n), out_specs=(SEM, SEM, *[HBM] * (2 * n), pl.BlockSpec(memory_space=pltpu.VMEM)),
        input_output_aliases={k: k + 2 for k in range(2 * n)}, compiler_params=_SPLIT,
    )(*_in_hbm(gs + lands))
    return res[0], res[1], list(res[2:2 + n]), list(res[2 + n:2 + 2 * n]), res[2 + 2 * n]


def _pair_exchange_wait(send_sems, recv_sems, gs, lands, after, *, name):
    n = len(gs)

    def body(*refs):
        g_in, l_in, ss, rs = refs[:n], refs[n:2 * n], refs[2 * n], refs[2 * n + 1]
        x, y, c = _place()
        for b in range(n):
            rh = gs[b].shape[1] // 2
            cp = pltpu.make_async_remote_copy(src_ref=g_in[b].at[:, pl.ds((1 - c) * rh, rh), :], dst_ref=l_in[b],
                                              send_sem=ss.at[b], recv_sem=rs.at[b], device_id=(x, y, 1 - c),
                                              device_id_type=MESH)
            cp.wait_send()
            cp.wait_recv()

    res = pl.pallas_call(
        body, name=name, out_shape=[pltpu.HBM(a.shape, a.dtype) for a in gs + lands],
        in_specs=[HBM] * (2 * n) + [SEM, SEM, ANY], out_specs=[HBM] * (2 * n),
        input_output_aliases={k: k for k in range(2 * n)}, compiler_params=_SPLIT,
    )(*gs, *lands, send_sems, recv_sems, after)
    return list(res[:n]), list(res[n:])


def _pair_share(ss, *, name):
    n = len(ss)

    def body(*refs):
        outs, send_sems, recv_sems = refs[n:2 * n], refs[2 * n], refs[2 * n + 1]
        x, y, c = _place()
        cps = []
        for b in range(n):
            rh = ss[b].shape[0] // 2
            mine = outs[b].at[pl.ds(c * rh, rh), :]
            cps.append(pltpu.make_async_remote_copy(src_ref=mine, dst_ref=mine, send_sem=send_sems.at[b],
                                                    recv_sem=recv_sems.at[b], device_id=(x, y, 1 - c),
                                                    device_id_type=MESH))
        for cp in cps:
            cp.start()
        for b, cp in enumerate(cps):
            rh = ss[b].shape[0] // 2
            theirs = outs[b].at[pl.ds((1 - c) * rh, rh), :]
            pltpu.make_async_remote_copy(src_ref=theirs, dst_ref=theirs, send_sem=send_sems.at[b],
                                         recv_sem=recv_sems.at[b], device_id=(x, y, 1 - c),
                                         device_id_type=MESH).wait_recv()
            cp.wait_send()

    return _aliased_comm_call(body, ss, n, name=name)


_SMALL_SHARDED = (("e_conv_w", 2), ("o_norm", 1), ("o_d", 1))
_REPLICATED = ("e_norm", "e_gmlp_w", "e_gmlp_b", "e_conv_b", "e_conv_ln_g", "e_conv_ln_b", "o_lam_re", "o_lam_im",
               "o_log_dt", "o_b_re", "o_b_im", "o_c_re", "o_c_im", "ca_norm", "ca_mem_norm", "ffn_norm", "final_norm")
_SMALL = tuple(n for n, _ in _SMALL_SHARDED) + _REPLICATED
_WEIGHTS = ("e_norm", "e_w_in", "e_gmlp_w", "e_gmlp_b", "e_conv_w", "e_conv_b", "e_conv_ln_g", "e_conv_ln_b",
            "e_w_out", "o_norm", "o_w_in", "o_lam_re", "o_lam_im", "o_log_dt", "o_b_re", "o_b_im", "o_c_re", "o_c_im",
            "o_d", "o_w_out", "ca_norm", "ca_mem_norm", "ca_wq", "ca_wk", "ca_wv", "ca_wo", "ffn_norm", "ffn_w_gate",
            "ffn_w_up", "ffn_w_down", "final_norm")


def _pack_rows(arrs, width, dtype, row_mult=8):
    parts, spans, r0 = [], [], 0
    for a in arrs:
        flat = a.reshape(-1).astype(dtype)
        rows = -(-flat.shape[0] // (width * row_mult)) * row_mult
        if rows * width != flat.shape[0]:
            flat = jnp.pad(flat, (0, rows * width - flat.shape[0]))
        parts.append(flat.reshape(rows, width))
        spans.append((r0, rows))
        r0 += rows
    return jnp.concatenate(parts, axis=0), spans


def _unpack_rows(slab, spans, shapes):
    out = []
    for (r0, rows), shp in zip(spans, shapes):
        n = math.prod(shp)
        out.append(slab[r0:r0 + rows].reshape(-1)[:n].reshape(shp))
    return out


def _two_d(a):
    return a.reshape(-1, a.shape[-1])


def _shard_rows(n, a):
    return _two_d(jnp.swapaxes(a, -1, -2) if n in _TRANSPOSED else a)


def _from_shard_rows(n, rows, shape):
    if n in _TRANSPOSED:
        return jnp.swapaxes(rows.reshape(shape[:-2] + (shape[-1], shape[-2])), -1, -2)
    return rows.reshape(shape)


def _local_slab(local, slab, dtype):
    parts = sorted((r0, n, l) for n, (_, where) in _PLACE.items() for l, (s, r0) in enumerate(where) if s == slab)
    shards = [_shard_rows(n, local[n] if len(_PLACE[n][1]) == 1 else local[n][l]) for _, n, l in parts]
    return jnp.concatenate([a.astype(dtype) for a in shards], axis=0)


def _set_diag(b, pattern):
    return jnp.einsum(pattern, b, jnp.eye(C_GROUPS // N_SETS, dtype=b.dtype))


def _s5_discretize(lam_re, lam_im, log_dt, b_re, b_im):
    dt = jnp.exp(log_dt)[:, None]
    mag = jnp.exp(lam_re * dt)
    ar = mag * jnp.cos(lam_im * dt)
    ai = mag * jnp.sin(lam_im * dt)
    den = lam_re * lam_re + lam_im * lam_im
    qr = ((ar - 1.0) * lam_re + ai * lam_im) / den
    qi = (ai * lam_re - (ar - 1.0) * lam_im) / den
    bbr = qr[..., None] * b_re - qi[..., None] * b_im
    bbi = qr[..., None] * b_im + qi[..., None] * b_re
    return ar, ai, bbr, bbi


def _attention_block(x, mem, W, w, i, tag):
    xn, q = _norm_mm(x, w["ca_norm"][i], _shards(W, "ca_wq", i), split="k", out_dtype=BF16, name=f"{tag}_q")
    memn = _rms_fwd(mem, w["ca_mem_norm"][i], name=f"{tag}_ca_memnorm")
    k = _mm_k(memn, _shards(W, "ca_wk", i), out_dtype=BF16, name=f"{tag}_k")
    v = _mm_k(memn, _shards(W, "ca_wv", i), out_dtype=BF16, name=f"{tag}_v")
    o = _attn_fwd(q, k, v, name=f"{tag}_attn")
    y = _mm_k(o, _shards(W, "ca_wo", i), add=x, name=f"{tag}_wo")
    return y, (x, xn, memn, q, k, v, o)


def _attention_block_bwd(dy, saved, mem, W, w, i, tag, G, grads, token=None, mid=None):
    x, xn, memn, q, k, v, o = saved
    gain = w["ca_norm"][i]
    if token is not None:
        k = _behind(k, token)
    G = _grad_to_slab(G, "ca_wo", i, o, dy, a_cols=256, name=f"{tag}_dwo")
    dq, dk, dv = _attn_bwd(dy, _shards(W, "ca_wo", i), q, k, v, name=f"{tag}_attn_bwd")
    token = mid(dq) if mid is not None else None
    if token is not None:
        gain = _behind(gain, token)
    G = _grad_to_slab(G, "ca_wq", i, xn, dq, a_cols=256, name=f"{tag}_dwq")
    G = _grad_to_slab(G, "ca_wk", i, memn, dk, a_cols=256, name=f"{tag}_dwk")
    G = _grad_to_slab(G, "ca_wv", i, memn, dv, a_cols=256, name=f"{tag}_dwv")
    dmemn = _mm_k_t([(dk, _shards(W, "ca_wk", i)), (dv, _shards(W, "ca_wv", i))], name=f"{tag}_dmemn")
    dx, dg = _norm_bwd_k(dq, _shards(W, "ca_wq", i), x, gain, dy, name=f"{tag}_dq_norm_bwd")
    grads["ca_norm"][i] = dg[0]
    grads["ca_mem_norm"][i] = _rms_dg(mem, w["ca_mem_norm"][i], dmemn, name=f"{tag}_ca_memnorm_bwd")[0]
    return dx, G


def _ffn_block(x, W, w, i, tag):
    fn, gate, up, h = _ffn_up(x, w["ffn_norm"][i], _shards(W, "ffn_w_gate", i), _shards(W, "ffn_w_up", i),
                              name=f"{tag}_ffn_up")
    y = _mm_k(h, _shards(W, "ffn_w_down", i), add=x, name=f"{tag}_down")
    return y, (x, fn, gate, up, h)


def _ffn_block_bwd(dy, saved, W, w, i, tag, G, grads, token=None, mid=None):
    x, fn, gate, up, h = saved
    gain = w["ffn_norm"][i]
    G = _grad_to_slab(G, "ffn_w_down", i, h, dy, name=f"{tag}_dwd")
    dg, du = _ffn_bwd_hidden(dy, _shards(W, "ffn_w_down", i), gate, up, token, name=f"{tag}_ffn_bwd_hidden")
    token = mid(dg) if mid is not None else None
    if token is not None:
        gain = _behind(gain, token)
    G = _grad_to_slab(G, "ffn_w_gate", i, dg, fn, name=f"{tag}_dwg")
    G = _grad_to_slab(G, "ffn_w_up", i, du, fn, name=f"{tag}_dwu")
    dx, dgn = _ffn_in_bwd(dg, du, _shards(W, "ffn_w_gate", i), _shards(W, "ffn_w_up", i), x, gain, dy,
                          name=f"{tag}_ffn_in_bwd")
    grads["ffn_norm"][i] = dgn[0]
    return dx, G


def _gmlp_mask():
    chunk = jnp.arange(GMLP_BLOCK) // CHUNK
    return chunk[None, :] <= chunk[:, None]


def _even_block(x, W, w, tag):
    hn, proj = _norm_mm(x, w["e_norm"][0], _shards(W, "e_w_in"), split="n", out_dtype=F32, name=f"{tag}_w_in")
    wm = jnp.where(_gmlp_mask()[None], w["e_gmlp_w"][0], 0.0).astype(BF16)
    bcol = w["e_gmlp_b"][0][:, :, None]
    cw = jnp.pad(w["e_conv_w"][0], ((0, CONV_HALO - CONV_WIDTH), (0, 0)))
    cb, lg, lb = w["e_conv_b"], w["e_conv_ln_g"], w["e_conv_ln_b"]
    mix, hc = _even_fwd(proj, wm, bcol, cw, cb, lg, lb, name=f"{tag}_mixers")
    y = _mm_k(mix, _shards(W, "e_w_out"), add=x, name=f"{tag}_w_out")
    return y, (x, hn, proj, mix, hc, wm, bcol, cw)


def _even_block_bwd(dy, saved, W, w, tag, G, grads):
    x, hn, proj, mix, hc, wm, bcol, cw = saved
    dmix = _mm_k_t([(dy, _shards(W, "e_w_out"))], name=f"{tag}_dmix")
    G = _grad_to_slab(G, "e_w_out", 0, mix, dy, a_cols=256, name=f"{tag}_dw_out")
    wmt = jnp.swapaxes(wm, 1, 2)
    dpa, dhc, dwm, db, dlg, dlb, dcb = _even_bwd1(proj, dmix, hc, wm, wmt, bcol, w["e_conv_ln_g"], w["e_conv_ln_b"],
                                                  name=f"{tag}_mixers_bwd1")
    dpb, dcw = _even_bwd2(proj, dhc, cw, name=f"{tag}_mixers_bwd2")
    grads["e_gmlp_w"] = jnp.where(_gmlp_mask()[None], dwm, 0.0)[None]
    grads["e_gmlp_b"] = db[:, :, 0][None]
    grads["e_conv_ln_g"], grads["e_conv_ln_b"], grads["e_conv_b"] = dlg, dlb, dcb
    grads["e_conv_w"] = dcw[:CONV_WIDTH][None]
    G = _grad_to_slab(G, "e_w_in", 0, hn, dpa, b_cols=512, chips=(0, 2), name=f"{tag}_dw_in_a")
    G = _grad_to_slab(G, "e_w_in", 0, hn, dpb, b_cols=512, chips=(2, 2), name=f"{tag}_dw_in_b")
    dx, dg = _norm_bwd_n((dpa, dpb), _shards(W, "e_w_in"), x, w["e_norm"][0], dy, name=f"{tag}_in_bwd")
    grads["e_norm"] = dg
    return dx, G


def _odd_block(x, W, w, tag):
    S = x.shape[0]
    hn, u = _norm_mm(x, w["o_norm"][0], _shards(W, "o_w_in"), split="k", out_dtype=F32, name=f"{tag}_w_in")
    disc_in = (w["o_lam_re"][0], w["o_lam_im"][0], w["o_log_dt"][0], w["o_b_re"][0], w["o_b_im"][0])
    (ar, ai, bbr, bbi), disc_vjp = jax.vjp(_s5_discretize, *disc_in)
    sets = (N_SETS, C_GROUPS // N_SETS)
    per_set = N_STATE // N_SETS
    bset = jnp.concatenate([_set_diag(b.reshape(sets + b.shape[1:]), "jgpc,gh->jgchp").reshape(N_SETS, SET_CH, per_set)
                            for b in (bbr, bbi)], axis=2).astype(BF16)
    cset = jnp.concatenate([_set_diag(c.reshape(sets + c.shape[1:]), "jgcp,gh->jgphc").reshape(N_SETS, per_set, SET_CH)
                            for c in (w["o_c_re"][0], -w["o_c_im"][0])], axis=1).astype(BF16)
    powers, pr, pi = [], ar, ai
    for _ in range(SCAN_BLOCK):
        powers.append(jnp.concatenate([pr.reshape(STATE_ROWS, STATE_LANES), pi.reshape(STATE_ROWS, STATE_LANES)], 0))
        pr, pi = pr * ar - pi * ai, pr * ai + pi * ar
    pw = jnp.stack(powers, axis=0)
    xs = _scan_fwd(u, bset, pw, name=f"{tag}_scan").reshape(S // 8, STATE_ROWS, 8, STATE_LANES)
    yv, yg = _s5_readout(xs, cset, u, w["o_d"], name=f"{tag}_readout")
    o, y = _glu_out(yg, _shards(W, "o_w_out"), x, name=f"{tag}_glu_out")
    return y, (x, hn, u, bset, cset, pw, xs, yv, yg, o, disc_vjp)


def _odd_block_bwd(dy, saved, W, w, tag, G, grads):
    x, hn, u, bset, cset, pw, xs, yv, yg, o, disc_vjp = saved
    S = x.shape[0]
    do, dys, dus, dd = _glu_out_bwd(o, dy, _shards(W, "o_w_out"), yv, u, w["o_d"], name=f"{tag}_glu_out_bwd")
    G = _grad_to_slab(G, "o_w_out", 0, yg, do, b_cols=512, name=f"{tag}_dw_out")
    grads["o_d"] = dd
    dcset_t = _state_grad_sets(dys, xs, name=f"{tag}_dcd")
    gs, da = _scan_bwd(dys, cset, xs.reshape(S * STATE_ROWS, STATE_LANES), pw, name=f"{tag}_scan_bwd")
    gs = gs.reshape(xs.shape)
    dbset = _state_grad_sets(u, gs, name=f"{tag}_dbd")
    du, dx, dg = _s5_in_bwd(gs, bset, dus, _shards(W, "o_w_in"), x, w["o_norm"][0], dy, name=f"{tag}_in_bwd")
    G = _grad_to_slab(G, "o_w_in", 0, hn, du, a_cols=256, name=f"{tag}_dw_in")
    grads["o_norm"] = dg
    per = C_GROUPS // N_SETS
    blocks = (N_SETS, per, C_GROUP_CH, 2, per, C_STATE)
    dc = _set_diag(dcset_t.reshape(blocks), "jhcrgp,gh->rjgcp").reshape(2, C_GROUPS, C_GROUP_CH, C_STATE)
    db = _set_diag(dbset.reshape(blocks), "jgcrhp,gh->rjgpc").reshape(2, C_GROUPS, C_STATE, C_GROUP_CH)
    dcr, dci, dbbr, dbbi = dc[0], -dc[1], db[0], db[1]
    dar = da[:STATE_ROWS].reshape(C_GROUPS, C_STATE)
    dai = da[STATE_ROWS:].reshape(C_GROUPS, C_STATE)
    dlr, dli, dldt, dbr, dbi = disc_vjp((dar, dai, dbbr, dbbi))
    grads["o_lam_re"], grads["o_lam_im"], grads["o_log_dt"] = dlr[None], dli[None], dldt[None]
    grads["o_b_re"], grads["o_b_im"], grads["o_c_re"], grads["o_c_im"] = dbr[None], dbi[None], dcr[None], dci[None]
    return dx, G


def _behind(value, token):
    return value + token[0, 0].astype(value.dtype)


class _NoExchange:
    def __init__(self, W):
        self.W = W

    def first_weights(self, w):
        return self.W, w

    def weights(self, stage, after):
        return {}

    def grads_ready(self, piece, G):
        return None

    def grads_crossed(self, piece, after):
        return None


def _forward_backward(xs_, mems_, tgt, w, G, exchange):
    W, w = exchange.first_weights(w)
    x1, s_mix0 = _even_block(xs_, W, w, "l0")
    W = {**W, **exchange.weights(1, x1)}
    x2, s_att0 = _attention_block(x1, mems_, W, w, 0, "l0")
    W = {**W, **exchange.weights(2, x2)}
    x3, s_ffn0 = _ffn_block(x2, W, w, 0, "l0")
    W = {**W, **exchange.weights(3, x3)}
    x4, s_mix1 = _odd_block(x3, W, w, "l1")
    x5, s_att1 = _attention_block(x4, mems_, W, w, 1, "l1")
    x6, s_ffn1 = _ffn_block(x5, W, w, 1, "l1")
    dx, dfinal, loss_lanes = _loss_head(x6, w["final_norm"], tgt, name="loss_head")

    grads = {n: [None, None] for n in ("ca_norm", "ca_mem_norm", "ffn_norm")}
    grads["final_norm"] = dfinal[0]
    dx, G = _ffn_block_bwd(dx, s_ffn1, W, w, 1, "l1", G, grads)
    dx, G = _attention_block_bwd(dx, s_att1, mems_, W, w, 1, "l1", G, grads)
    dx, G = _odd_block_bwd(dx, s_mix1, W, w, "l1", G, grads)
    token = exchange.grads_ready("l1", G)
    dx, G = _ffn_block_bwd(dx, s_ffn0, W, w, 0, "l0", G, grads, token,
                           lambda after: exchange.grads_crossed("l1", after))
    token = exchange.grads_ready("ffn0", G)
    dx, G = _attention_block_bwd(dx, s_att0, mems_, W, w, 0, "l0", G, grads, token,
                                 lambda after: exchange.grads_crossed("ffn0", after))
    dx, G = _even_block_bwd(dx, s_mix0, W, w, "l0", G, grads)
    for n in list(grads):
        if isinstance(grads[n], list):
            grads[n] = jnp.stack(grads[n], axis=0)
        grads[n] = grads[n].reshape(w[n].shape)
    return loss_lanes, dx, G, grads


class _Exchange:
    def __init__(self, local, chip, core):
        self.bufs = {s: lax.dynamic_update_slice(lax.empty((N_CHIPS, rows, width), BF16),
                                                 _local_slab(local, s, BF16)[None], (chip, 0, 0))
                     for s, (width, rows) in _SLABS.items()}
        self.where = jnp.stack([chip, core]).astype(jnp.int32)
        self.flights = []
        self.reduces = {}

    def weights(self, stage, after):
        send_sems, recv_sems, bufs, _ = self.flights[stage]
        bufs = _gather_ici_wait(send_sems, recv_sems, bufs, after, name=f"gather_stage{stage}_wait")
        return dict(zip(_STAGES[stage], _gather_forward(bufs, name=f"gather_stage{stage}_forward")))

    def first_weights(self, w):
        after = w["e_conv_w"].reshape(-1)[:STATE_LANES]
        for k, stage in enumerate(_STAGES):
            self.flights.append(_gather_ici_start([self.bufs[s] for s in stage], after, name=f"gather_stage{k}_start"))
            after = self.flights[-1][3]
        return self.weights(0, after), {**w, "e_norm": _behind(w["e_norm"], after)}

    def pair_start(self, G, slabs, tag):
        send_sems, recv_sems, gl, lands, token = _pair_exchange_start([G[s] for s in slabs],
                                                                      name=f"grad_{tag}_pair_start")
        return (slabs, send_sems, recv_sems, gl, lands), token

    def pair_land(self, state, after, tag):
        slabs, send_sems, recv_sems, gl, lands = state
        gl, other = _pair_exchange_wait(send_sems, recv_sems, gl, lands, after, name=f"grad_{tag}_pair_wait")
        pairs = [_pair_sum(g, r, self.where, name=f"grad_pair_sum_{s}") for s, g, r in zip(slabs, gl, other)]
        send_sems, recv_sems, pairs, lands, token = _chip_exchange_start(pairs, name=f"grad_{tag}_chip_start")
        return (slabs, gl, other, send_sems, recv_sems, pairs, lands), token

    def reduce_finish(self, state, after, tag):
        slabs, gl, other, send_sems, recv_sems, pairs, lands = state
        slots = _chip_exchange_wait(send_sems, recv_sems, pairs, lands, after, name=f"grad_{tag}_chip_wait")
        halves = [_chip_sum(g, r, sl, self.where, name=f"grad_chip_sum_{s}")
                  for s, g, r, sl in zip(slabs, gl, other, slots)]
        return dict(zip(slabs, _pair_share(halves, name=f"grad_{tag}_pair_share")))

    def grads_ready(self, piece, G):
        self.reduces[piece], token = self.pair_start(G, _GRAD_PIECES[piece], piece)
        return token

    def grads_crossed(self, piece, after):
        self.reduces[piece], token = self.pair_land(self.reduces[piece], after, piece)
        return token


def kernel(x, mem, e_norm, e_w_in, e_gmlp_w, e_gmlp_b, e_conv_w, e_conv_b, e_conv_ln_g, e_conv_ln_b, e_w_out, o_norm, o_w_in, o_lam_re, o_lam_im, o_log_dt, o_b_re, o_b_im, o_c_re, o_c_im, o_d, o_w_out, ca_norm, ca_mem_norm, ca_wq, ca_wk, ca_wv, ca_wo, ffn_norm, ffn_w_gate, ffn_w_up, ffn_w_down, final_norm, loss_target, m_e_norm, m_e_w_in, m_e_gmlp_w, m_e_gmlp_b, m_e_conv_w, m_e_conv_b, m_e_conv_ln_g, m_e_conv_ln_b, m_e_w_out, m_o_norm, m_o_w_in, m_o_lam_re, m_o_lam_im, m_o_log_dt, m_o_b_re, m_o_b_im, m_o_c_re, m_o_c_im, m_o_d, m_o_w_out, m_ca_norm, m_ca_mem_norm, m_ca_wq, m_ca_wk, m_ca_wv, m_ca_wo, m_ffn_norm, m_ffn_w_gate, m_ffn_w_up, m_ffn_w_down, m_final_norm, v_e_norm, v_e_w_in, v_e_gmlp_w, v_e_gmlp_b, v_e_conv_w, v_e_conv_b, v_e_conv_ln_g, v_e_conv_ln_b, v_e_w_out, v_o_norm, v_o_w_in, v_o_lam_re, v_o_lam_im, v_o_log_dt, v_o_b_re, v_o_b_im, v_o_c_re, v_o_c_im, v_o_d, v_o_w_out, v_ca_norm, v_ca_mem_norm, v_ca_wq, v_ca_wk, v_ca_wv, v_ca_wo, v_ffn_norm, v_ffn_w_gate, v_ffn_w_up, v_ffn_w_down, v_final_norm):
    args = dict(locals())
    local = {n: args[n] for n in _WEIGHTS}
    mom = {n: args["m_" + n] for n in _WEIGHTS}
    vel = {n: args["v_" + n] for n in _WEIGHTS}
    chip = 2 * lax.axis_index("x") + lax.axis_index("y")
    core = lax.axis_index("c")
    xs_, mems_, tgt = x[0], mem[0], loss_target[0]

    w = {n: local[n] for n in _REPLICATED}
    sm_slab, sm_spans = _pack_rows([local[n] for n, _ in _SMALL_SHARDED], SMALL_W, F32)
    sm_all = _allgather_small(sm_slab, name="gather_small_weights").reshape(N_DEV, -1, SMALL_W)
    for (n, ax), span in zip(_SMALL_SHARDED, sm_spans):
        shp = local[n].shape
        w[n] = jnp.concatenate([_unpack_rows(sm_all[2 * p], [span], [shp])[0] for p in range(N_CHIPS)], axis=ax)

    exchange = _Exchange(local, chip, core)
    G = {s: lax.empty((N_CHIPS, rows, width), F32) for s, (width, rows) in _SLABS.items()}
    loss_lanes, dx, G, grads = _forward_backward(xs_, mems_, tgt, w, G, exchange)

    gs_slab, gs_spans = _pack_rows([grads[n] for n in _SMALL] + [loss_lanes], SMALL_W, F32)
    small_flight = _all_to_all_start(gs_slab, dx, name="small_grads_start")
    exchange.grads_ready("rest0", G)
    gsum = exchange.reduce_finish(exchange.reduces["l1"], small_flight[4], "l1")
    gsum = {**gsum, **exchange.reduce_finish(exchange.reduces["ffn0"], small_flight[4], "ffn0")}
    token = exchange.grads_crossed("rest0", gsum["B0"])

    gs_slab, gs_all = _all_to_all_wait(*small_flight[:4], token, name="small_grads_wait")
    gs_all = lax.dynamic_update_slice(gs_all, gs_slab[None], (2 * chip + core, 0, 0))
    gs_sum = _sum_slots(gs_all, name="small_grad_sum")
    *small_sums, loss_sum = _unpack_rows(gs_sum, gs_spans, [grads[n].shape for n in _SMALL] + [loss_lanes.shape])
    out_grads = dict(zip(_SMALL, small_sums))
    for n, ax in _SMALL_SHARDED:
        width = local[n].shape[ax]
        out_grads[n] = lax.dynamic_slice_in_dim(out_grads[n], chip * width, width, axis=ax)

    delta, new_m, new_v = {}, {}, {}
    d_, m_, v_ = _adamw_small([_two_d(local[n]) for n in _SMALL], [_two_d(out_grads[n]) for n in _SMALL],
                              [_two_d(mom[n]) for n in _SMALL], [_two_d(vel[n]) for n in _SMALL], name="adamw_small")
    for n, dd, mm_, vv in zip(_SMALL, d_, m_, v_):
        shp = local[n].shape
        delta[n], new_m[n], new_v[n] = dd.reshape(shp), mm_.reshape(shp), vv.reshape(shp)
    def adamw_large(names):
        for n in names:
            shp = local[n].shape
            g_, d_, m_, v_ = _adamw_shard(_shard_rows(n, local[n]), [(gsum[s], r0) for s, r0 in _PLACE[n][1]],
                                          _shard_rows(n, mom[n]), _shard_rows(n, vel[n]), name=f"adamw_{n}")
            out_grads[n], delta[n], new_m[n], new_v[n] = (_from_shard_rows(n, t, shp) for t in (g_, d_, m_, v_))

    ready = [n for n, (_, where) in _PLACE.items() if all(s in gsum for s, _ in where)]
    adamw_large(ready)
    done = jnp.concatenate([delta[n].reshape(-1)[:1] for n in ready + list(_SMALL[:1])])
    gsum = {**gsum, **exchange.reduce_finish(exchange.reduces["rest0"], done, "rest0")}
    adamw_large([n for n in _PLACE if n not in ready])

    return (loss_sum[0, 0], dx[None], *[out_grads[n] for n in _WEIGHTS], *[delta[n] for n in _WEIGHTS],
            *[new_m[n] for n in _WEIGHTS], *[new_v[n] for n in _WEIGHTS])
```

```python
import functools
import math

import jax
import jax.numpy as jnp
from jax import lax
from jax.experimental import pallas as pl
from jax.experimental.pallas import tpu as pltpu

F32 = jnp.float32
BF16 = jnp.bfloat16
MESH = pl.DeviceIdType.MESH

EPS = 1e-6
D_MODEL = 1024
A_WIDTH = 512
A_GROUPS = 4
GMLP_BLOCK = 128
CHUNK = 64
B_WIDTH = 512
CONV_WIDTH = 31
CONV_HALO = 32
C_WIDTH = 512
C_GROUP_CH = 16
C_GROUPS = 32
C_STATE = 64
N_STATE = C_GROUPS * C_STATE
STATE_LANES = 128
STATE_ROWS = N_STATE // STATE_LANES
SCAN_BLOCK = 8
CA_HEADS = 4
CA_HEAD_DIM = 256
FFN_HIDDEN = 2816

ADAM_LR = 0.001
ADAM_B1 = 0.9
ADAM_B2 = 0.999
ADAM_EPS = 1e-08
ADAM_WD = 0.01
ADAM_STEP = 10

VMEM_LIMIT = 56 * 1024 * 1024
ACC_BYTES = 6 * 1024 * 1024
TN_VMEM_BYTES = 44 * 1024 * 1024
SMALL_W = 128
N_CHIPS = 4
N_DEV = 8

_SLABS = {"D0": (512, 1024), "E0": (1024, 256), "A0": (1024, 1024), "B0": (1024, 704), "C0": (1024, 1408),
          "D1": (512, 768), "A1": (1024, 1024), "B1": (1024, 704), "C1": (1024, 1408)}
_STAGES = (("D0", "E0"), ("A0",), ("B0", "C0"), ("D1", "A1", "B1", "C1"))
_GRAD_PIECES = {"l1": _STAGES[3], "ffn0": _STAGES[2], "att0": _STAGES[1], "even0": _STAGES[0]}
_PLACE = {
    "e_w_in": (1024, (("D0", 0),)), "e_w_out": (256, (("E0", 0),)),
    "o_w_out": (512, (("D1", 0),)), "o_w_in": (256, (("D1", 512),)),
    "ca_wq": (256, (("A0", 0), ("A1", 0))), "ca_wk": (256, (("A0", 256), ("A1", 256))),
    "ca_wv": (256, (("A0", 512), ("A1", 512))), "ca_wo": (256, (("A0", 768), ("A1", 768))),
    "ffn_w_down": (704, (("B0", 0), ("B1", 0))),
    "ffn_w_gate": (704, (("C0", 0), ("C1", 0))), "ffn_w_up": (704, (("C0", 704), ("C1", 704))),
}
_TRANSPOSED = ("ffn_w_gate", "ffn_w_up")


def _params(sem=None):
    return pltpu.CompilerParams(dimension_semantics=sem, vmem_limit_bytes=VMEM_LIMIT)


def _tile(n, pref, mult=128):
    if n <= pref:
        return n
    t = (pref // mult) * mult
    while t >= mult:
        if n % t == 0:
            return t
        t -= mult
    return n


def _blk(name, layer=0):
    rows, where = _PLACE[name]
    slab, r0 = where[layer]
    assert r0 % rows == 0
    return slab, rows, r0 // rows


def _shards(slabs, name, layer=0):
    slab, rows, b = _blk(name, layer)
    return [(slabs[slab], (None, rows, _SLABS[slab][0]), (p, b, 0)) for p in range(N_CHIPS)]


_GELU_C = 0.7978845608028654
_GELU_A = 0.044715


def _gelu(x):
    t = jnp.tanh(_GELU_C * (x + _GELU_A * (x * x * x)))
    return 0.5 * x * (1.0 + t), t


def _gelu_grad(x, t):
    return 0.5 * (1.0 + t) + 0.5 * x * (1.0 - t * t) * (_GELU_C * (1.0 + 3.0 * _GELU_A * x * x))


def _sigmoid(x):
    return 1.0 / (1.0 + jnp.exp(-x))


def _mean(x):
    return jnp.mean(x, axis=-1, keepdims=True)


def _dot(a, b):
    return jnp.dot(a, b, preferred_element_type=F32)


def _dot_nt(a, b):
    return lax.dot_general(a, b, (((1,), (1,)), ((), ())), preferred_element_type=F32)


def _dot_tn(a, b):
    return lax.dot_general(a, b, (((0,), (0,)), ((), ())), preferred_element_type=F32)


def _rms_tile(xv, gv):
    return (xv * lax.rsqrt(_mean(xv * xv) + EPS)) * gv


def _rms_bwd_tile(xv, gv, dyv):
    r = lax.rsqrt(_mean(xv * xv) + EPS)
    xh = xv * r
    dyg = dyv * gv
    return r * (dyg - xh * _mean(dyg * xh)), jnp.sum(dyv * xh, axis=0, keepdims=True)


def _cols(p, width):
    return slice(p * width, (p + 1) * width)


def _sum_k(a, ws, k):
    tot = None
    for p in range(N_CHIPS):
        y = _dot(a[:, _cols(p, k)], ws[p][...])
        tot = y if tot is None else tot + y
    return tot


def _cat_nt(a, ws):
    return jnp.concatenate([_dot_nt(a, ws[p][...]) for p in range(N_CHIPS)], axis=1)


def _rows_call(name, tm, rows, fulls, outs, accs, body, scratch=()):
    S = min(x.shape[-2] for x in rows if x.ndim != 4)
    nr, nf, no, na = len(rows), len(fulls), len(outs), len(accs)

    def kern(*refs):
        r, f = refs[:nr], refs[nr:nr + nf]
        o, a = refs[nr + nf:nr + nf + no], refs[nr + nf + no:nr + nf + no + na]
        if na:
            @pl.when(pl.program_id(0) == 0)
            def _():
                for ref in a:
                    ref[...] = jnp.zeros_like(ref)
        body(r, f, o, a, refs[nr + nf + no + na:])

    def whole(shape):
        nd = len(shape)
        return pl.BlockSpec(tuple(shape), lambda i: (0,) * nd)

    def row_spec(shape):
        if len(shape) == 4:
            return pl.BlockSpec((tm // 8,) + tuple(shape[1:]), lambda i: (i, 0, 0, 0))
        if len(shape) == 3:
            return pl.BlockSpec((shape[0], tm, shape[2]), lambda i: (0, i, 0))
        return pl.BlockSpec((tm, shape[1]), lambda i: (i, 0))

    def full_spec(x):
        if isinstance(x, tuple):
            _, bshape, bidx = x
            return pl.BlockSpec(bshape, lambda i: bidx, pipeline_mode=pl.Buffered(1))
        return whole(x.shape)

    out_shapes = [(S, o[0]) if len(o) == 2 else (o[0], S, o[1]) for o in outs]
    res = pl.pallas_call(
        kern, name=name, grid=(S // tm,),
        in_specs=[row_spec(x.shape) for x in rows] + [full_spec(x) for x in fulls],
        out_specs=[row_spec(s) for s in out_shapes] + [whole(shp) for shp, _ in accs],
        out_shape=[jax.ShapeDtypeStruct(s, o[-1]) for s, o in zip(out_shapes, outs)]
        + [jax.ShapeDtypeStruct(tuple(shp), dt) for shp, dt in accs],
        scratch_shapes=list(scratch),
        compiler_params=_params(("arbitrary",) if na else ("parallel",)),
    )(*rows, *[x[0] if isinstance(x, tuple) else x for x in fulls])
    return res[:no], res[no:]


def _grad_to_slab(gslabs, wname, layer, a, b, *, a_cols=None, b_cols=None, chips=(0, N_CHIPS), name):
    slab, rows, bidx = _blk(wname, layer)
    width = _SLABS[slab][0]
    p0, n_p = chips
    assert p0 % n_p == 0
    S = a.shape[-2]

    def tile_bytes(x, ts):
        return ts * x.dtype.itemsize * (x.shape[2] * n_p if x.ndim == 3 else x.shape[1])

    acc_bytes = n_p * rows * (-(-width // 128) * 128) * 4
    ts = next(t for t in (2048, 1024, 512, 256, S) if S % t == 0
              and 2 * (tile_bytes(a, t) + tile_bytes(b, t) + acc_bytes) <= TN_VMEM_BYTES or t == S)

    def operand(x):
        if x.ndim == 3:
            return pl.BlockSpec((n_p, ts, x.shape[2]), lambda s: (p0 // n_p, s, 0))
        return pl.BlockSpec((ts, x.shape[1]), lambda s: (s, 0))

    def part(ref, cols, p):
        if len(ref.shape) == 3:
            return ref[p]
        return ref[...] if cols is None else ref[:, _cols(p, cols)]

    def body(a_ref, b_ref, slab_ref, o_ref):
        @pl.when(pl.program_id(0) == 0)
        def _():
            o_ref[...] = jnp.zeros_like(o_ref)

        for p in range(n_p):
            o_ref[p] += _dot_tn(part(a_ref, a_cols, p).astype(BF16), part(b_ref, b_cols, p).astype(BF16))

    g = gslabs[slab]
    out = pl.pallas_call(
        body, name=name, grid=(S // ts,),
        in_specs=[operand(a), operand(b), pl.BlockSpec(memory_space=pl.ANY)],
        out_specs=pl.BlockSpec((n_p, rows, width), lambda s: (p0 // n_p, bidx, 0)),
        out_shape=jax.ShapeDtypeStruct(g.shape, F32), input_output_aliases={2: 0},
        compiler_params=_params(("arbitrary",)),
    )(a, b, g)
    return {**gslabs, slab: out}


def _vec(g):
    return g.reshape(1, -1)


def _norm_mm(x, g, ws, *, split, out_dtype, name, tm=512):
    S, D = x.shape
    k, n = ws[0][1][1], ws[0][1][2]
    N = n if split == "k" else N_CHIPS * n

    def body(r, f, o, acc, s):
        xn = _rms_tile(r[0][...], f[0][...]).astype(BF16)
        o[0][...] = xn
        if split == "k":
            o[1][...] = _sum_k(xn, f[1:], k).astype(out_dtype)
        else:
            for p in range(N_CHIPS):
                o[1][:, _cols(p, n)] = _dot(xn, f[1 + p][...]).astype(out_dtype)

    (xn, y), _ = _rows_call(name, _tile(S, tm), [x], [_vec(g)] + ws, [(D, BF16), (N, out_dtype)], [], body)
    return xn, y


def _mm_k(a, ws, *, add=None, out_dtype=F32, name, tm=512):
    S = a.shape[-2]
    k, n = ws[0][1][1], ws[0][1][2]
    has_add = add is not None

    def body(r, f, o, acc, s):
        if a.ndim == 3:
            y = None
            for p in range(N_CHIPS):
                t = _dot(r[0][p].astype(BF16), f[p][...])
                y = t if y is None else y + t
        else:
            y = _sum_k(r[0][...].astype(BF16), f, k)
        if has_add:
            y = y + r[1][...]
        o[0][...] = y.astype(out_dtype)

    (y,), _ = _rows_call(name, _tile(S, tm), [a] + ([add] if has_add else []), ws, [(n, out_dtype)], [], body)
    return y


def _mm_k_t(terms, *, out_dtype=F32, name, tm=512):
    S = terms[0][0].shape[0]
    k = terms[0][1][0][1][1]

    def body(r, f, o, acc, s):
        y = None
        for t in range(len(terms)):
            yt = _cat_nt(r[t][...].astype(BF16), f[N_CHIPS * t:N_CHIPS * (t + 1)])
            y = yt if y is None else y + yt
        o[0][...] = y.astype(out_dtype)

    (y,), _ = _rows_call(name, _tile(S, tm), [a for a, _ in terms], [w for _, ws in terms for w in ws],
                         [(N_CHIPS * k, out_dtype)], [], body)
    return y


def _rms_fwd(x, g, *, name):
    def body(r, f, o, acc, s):
        o[0][...] = _rms_tile(r[0][...], f[0][...]).astype(BF16)

    (y,), _ = _rows_call(name, _tile(x.shape[0], 256, 8), [x], [_vec(g)], [(x.shape[1], BF16)], [], body)
    return y


def _rms_dg(x, g, dy, *, name):
    def body(r, f, o, acc, s):
        acc[0][...] += _rms_bwd_tile(r[0][...], f[0][...], r[1][...])[1]

    _, (dg,) = _rows_call(name, _tile(x.shape[0], 256, 8), [x, dy], [_vec(g)], [], [((1, x.shape[1]), F32)], body)
    return dg


def _ffn_up(x, g, wg, wu, *, name, tm=512):
    S, D = x.shape
    h = wg[0][1][1]

    def body(r, f, o, acc, s):
        xn = _rms_tile(r[0][...], f[0][...]).astype(BF16)
        o[0][...] = xn
        for p in range(N_CHIPS):
            gate = _dot_nt(xn, f[1 + p][...])
            up = _dot_nt(xn, f[1 + N_CHIPS + p][...])
            o[1][p] = gate.astype(BF16)
            o[2][p] = up.astype(BF16)
            o[3][p] = (gate * _sigmoid(gate) * up).astype(BF16)

    (xn, gate, up, hid), _ = _rows_call(name, _tile(S, tm), [x], [_vec(g)] + wg + wu,
                                        [(D, BF16), (N_CHIPS, h, BF16), (N_CHIPS, h, BF16), (N_CHIPS, h, BF16)], [],
                                        body)
    return xn, gate, up, hid


def _ffn_bwd_hidden(dy, wd, gate, up, token=None, *, name, tm=512):
    S = dy.shape[0]
    h = wd[0][1][1]

    def body(r, f, o, acc, s):
        dyv = r[0][...]
        if token is not None:
            dyv = dyv + jnp.sum(f[N_CHIPS][...])
        dyb = dyv.astype(BF16)
        for p in range(N_CHIPS):
            dh = _dot_nt(dyb, f[p][...])
            gv = r[1][p].astype(F32)
            sg = _sigmoid(gv)
            o[0][p] = (dh * r[2][p].astype(F32) * (sg * (1.0 + gv * (1.0 - sg)))).astype(BF16)
            o[1][p] = (dh * gv * sg).astype(BF16)

    (dg, du), _ = _rows_call(name, _tile(S, tm), [dy, gate, up], wd + ([] if token is None else [token]),
                             [(N_CHIPS, h, BF16), (N_CHIPS, h, BF16)], [], body)
    return dg, du


def _ffn_in_bwd(dg, du, wg, wu, x, g, dres, *, name, tm=512):
    S, D = x.shape

    def body(r, f, o, acc, s):
        tot = None
        for p in range(N_CHIPS):
            y = _dot(r[0][p], f[1 + p][...]) + _dot(r[1][p], f[1 + N_CHIPS + p][...])
            tot = y if tot is None else tot + y
        dx, dgn = _rms_bwd_tile(r[2][...], f[0][...], tot)
        o[0][...] = dx + r[3][...]
        acc[0][...] += dgn

    (dx,), (dgn,) = _rows_call(name, _tile(S, tm), [dg, du, x, dres], [_vec(g)] + wg + wu, [(D, F32)],
                               [((1, D), F32)], body)
    return dx, dgn


def _norm_bwd_k(da, ws, x, g, dres, *, name, tm=512):
    S, D = x.shape

    def body(r, f, o, acc, s):
        dx, dg = _rms_bwd_tile(r[1][...], f[0][...], _cat_nt(r[0][...].astype(BF16), f[1:]))
        o[0][...] = dx + r[2][...]
        acc[0][...] += dg

    (dx,), (dg,) = _rows_call(name, _tile(S, tm), [da, x, dres], [_vec(g)] + ws, [(D, F32)], [((1, D), F32)], body)
    return dx, dg


def _norm_bwd_n(das, ws, x, g, dres, *, name, tm=256):
    S, D = x.shape
    n = ws[0][1][2]

    def body(r, f, o, acc, s):
        tot = None
        for p in range(N_CHIPS):
            y = _dot_nt(r[p // 2][:, _cols(p % 2, n)], f[1 + p][...])
            tot = y if tot is None else tot + y
        dx, dg = _rms_bwd_tile(r[2][...], f[0][...], tot)
        o[0][...] = dx + r[3][...]
        acc[0][...] += dg

    (dx,), (dg,) = _rows_call(name, _tile(S, tm), list(das) + [x, dres], [_vec(g)] + ws, [(D, F32)], [((1, D), F32)],
                              body)
    return dx, dg


def _ln_stats(v):
    mu = _mean(v)
    xc = v - mu
    rstd = lax.rsqrt(_mean(xc * xc) + EPS)
    return xc * rstd, rstd


_SHIFTS = 8
_CONV_ROWS = 64


def _fill_shifts(sh_ref, ext_ref, tm):
    sh_ref[0] = ext_ref[...]
    for s in range(1, _SHIFTS):
        sh_ref[s, 0:tm + CONV_HALO - _SHIFTS, :] = ext_ref[pl.ds(s, tm + CONV_HALO - _SHIFTS), :]


def _window(sh_ref, off, tm):
    return sh_ref[off % _SHIFTS, pl.ds(off - off % _SHIFTS, tm), :]


def _even_fwd(proj, wm, bcol, cw, cb, lg, lb, *, name):
    S = proj.shape[0]
    tm = _tile(S, 256)
    hb = tm // CONV_HALO
    nblk = tm // GMLP_BLOCK

    def body(p_ref, halo_ref, wm_ref, b_ref, cw_ref, cb_ref, lg_ref, lb_ref, mix_ref, hc_ref, hext_ref, hsh_ref):
        i = pl.program_id(0)
        gu, _ = _gelu(p_ref[:, 0:A_WIDTH])
        gv, _ = _gelu(p_ref[:, A_WIDTH:2 * A_WIDTH])
        vn, _ = _ln_stats(gv)
        vnb = vn.astype(BF16)
        for n in range(nblk):
            rows = slice(n * GMLP_BLOCK, (n + 1) * GMLP_BLOCK)
            for g in range(A_GROUPS):
                cols = slice(g * GMLP_BLOCK, (g + 1) * GMLP_BLOCK)
                sg = jnp.dot(wm_ref[g], vnb[rows, cols], preferred_element_type=F32) + b_ref[g]
                mix_ref[rows, cols] = (gu[rows, cols] * sg).astype(BF16)
        h = p_ref[:, 1024:1536] * _sigmoid(p_ref[:, 1536:2048])
        hh = halo_ref[:, 0:B_WIDTH] * _sigmoid(halo_ref[:, B_WIDTH:2 * B_WIDTH])
        hext_ref[0:CONV_HALO, :] = jnp.where(i > 0, hh, 0.0)
        hext_ref[CONV_HALO:CONV_HALO + tm, :] = h
        _fill_shifts(hsh_ref, hext_ref, tm)
        for r0 in range(0, tm, _CONV_ROWS):
            acc = jnp.zeros((_CONV_ROWS, B_WIDTH), F32)
            for k in range(CONV_WIDTH):
                acc = acc + cw_ref[k:k + 1, :] * _window(hsh_ref, r0 + k + CONV_HALO - CONV_WIDTH + 1, _CONV_ROWS)
            hc_ref[r0:r0 + _CONV_ROWS, :] = acc + cb_ref[...]
        hc = hc_ref[...]
        hhat, _ = _ln_stats(hc)
        hl = hhat * lg_ref[...] + lb_ref[...]
        mix_ref[:, A_WIDTH:A_WIDTH + B_WIDTH] = (hl * _sigmoid(hl)).astype(BF16)

    vec = pl.BlockSpec((1, B_WIDTH), lambda i: (0, 0))
    return pl.pallas_call(
        body, name=name, grid=(S // tm,),
        in_specs=[
            pl.BlockSpec((tm, 2048), lambda i: (i, 0)),
            pl.BlockSpec((CONV_HALO, 1024), lambda i: (jnp.maximum(i * hb - 1, 0), 1)),
            pl.BlockSpec((A_GROUPS, GMLP_BLOCK, GMLP_BLOCK), lambda i: (0, 0, 0)),
            pl.BlockSpec((A_GROUPS, GMLP_BLOCK, 1), lambda i: (0, 0, 0)),
            pl.BlockSpec((CONV_HALO, B_WIDTH), lambda i: (0, 0)),
            vec, vec, vec,
        ],
        out_specs=[pl.BlockSpec((tm, 1024), lambda i: (i, 0)), pl.BlockSpec((tm, B_WIDTH), lambda i: (i, 0))],
        out_shape=[jax.ShapeDtypeStruct((S, 1024), BF16), jax.ShapeDtypeStruct((S, B_WIDTH), F32)],
        scratch_shapes=[pltpu.VMEM((tm + CONV_HALO, B_WIDTH), F32),
                        pltpu.VMEM((_SHIFTS, tm + CONV_HALO, B_WIDTH), F32)],
        compiler_params=_params(("parallel",)),
    )(proj, proj, wm, bcol, cw, cb, lg, lb)


def _even_bwd1(proj, dmix, hc, wm, wmt, bcol, lg, lb, *, name):
    S = proj.shape[0]
    tm = _tile(S, 256)
    nblk = tm // GMLP_BLOCK

    def body(p_ref, dm_ref, hc_ref, wm_ref, wmt_ref, b_ref, lg_ref, lb_ref,
             dpa_ref, dhc_ref, dwm_ref, db_ref, dlg_ref, dlb_ref, dcb_ref, dgu_ref, dvn_ref):
        @pl.when(pl.program_id(0) == 0)
        def _():
            dwm_ref[...] = jnp.zeros_like(dwm_ref)
            db_ref[...] = jnp.zeros_like(db_ref)
            dlg_ref[...] = jnp.zeros_like(dlg_ref)
            dlb_ref[...] = jnp.zeros_like(dlb_ref)
            dcb_ref[...] = jnp.zeros_like(dcb_ref)

        au = p_ref[:, 0:A_WIDTH]
        av = p_ref[:, A_WIDTH:2 * A_WIDTH]
        gu, tu = _gelu(au)
        gv, tv = _gelu(av)
        vn, rstd = _ln_stats(gv)
        vnb = vn.astype(BF16)
        for n in range(nblk):
            rows = slice(n * GMLP_BLOCK, (n + 1) * GMLP_BLOCK)
            for g in range(A_GROUPS):
                cols = slice(g * GMLP_BLOCK, (g + 1) * GMLP_BLOCK)
                vb = vnb[rows, cols]
                sg = jnp.dot(wm_ref[g], vb, preferred_element_type=F32) + b_ref[g]
                da = dm_ref[rows, cols]
                dsg = da * gu[rows, cols]
                dgu_ref[rows, cols] = da * sg
                dsgb = dsg.astype(BF16)
                dwm_ref[g] += _dot_nt(dsgb, vb)
                db_ref[g] += jnp.sum(dsg, axis=1, keepdims=True)
                dvn_ref[rows, cols] = jnp.dot(wmt_ref[g], dsgb, preferred_element_type=F32)
        dvn = dvn_ref[...]
        dgv = rstd * (dvn - _mean(dvn) - vn * _mean(dvn * vn))
        dpa_ref[:, 0:A_WIDTH] = (dgu_ref[...] * _gelu_grad(au, tu)).astype(BF16)
        dpa_ref[:, A_WIDTH:2 * A_WIDTH] = (dgv * _gelu_grad(av, tv)).astype(BF16)
        hhat, rstd2 = _ln_stats(hc_ref[...])
        lgv = lg_ref[...]
        hl = hhat * lgv + lb_ref[...]
        s = _sigmoid(hl)
        dhl = dm_ref[:, A_WIDTH:A_WIDTH + B_WIDTH] * (s * (1.0 + hl * (1.0 - s)))
        dlg_ref[...] += jnp.sum(dhl * hhat, axis=0, keepdims=True)
        dlb_ref[...] += jnp.sum(dhl, axis=0, keepdims=True)
        dhh = dhl * lgv
        dhc = rstd2 * (dhh - _mean(dhh) - hhat * _mean(dhh * hhat))
        dcb_ref[...] += jnp.sum(dhc, axis=0, keepdims=True)
        dhc_ref[...] = dhc

    vec = pl.BlockSpec((1, B_WIDTH), lambda i: (0, 0))
    w3 = pl.BlockSpec((A_GROUPS, GMLP_BLOCK, GMLP_BLOCK), lambda i: (0, 0, 0))
    b3 = pl.BlockSpec((A_GROUPS, GMLP_BLOCK, 1), lambda i: (0, 0, 0))
    return pl.pallas_call(
        body, name=name, grid=(S // tm,),
        in_specs=[
            pl.BlockSpec((tm, 1024), lambda i: (i, 0)),
            pl.BlockSpec((tm, 1024), lambda i: (i, 0)),
            pl.BlockSpec((tm, B_WIDTH), lambda i: (i, 0)),
            w3, w3, b3, vec, vec,
        ],
        out_specs=[pl.BlockSpec((tm, 1024), lambda i: (i, 0)), pl.BlockSpec((tm, B_WIDTH), lambda i: (i, 0)),
                   w3, b3, vec, vec, vec],
        out_shape=[
            jax.ShapeDtypeStruct((S, 1024), BF16), jax.ShapeDtypeStruct((S, B_WIDTH), F32),
            jax.ShapeDtypeStruct((A_GROUPS, GMLP_BLOCK, GMLP_BLOCK), F32),
            jax.ShapeDtypeStruct((A_GROUPS, GMLP_BLOCK, 1), F32),
            jax.ShapeDtypeStruct((1, B_WIDTH), F32), jax.ShapeDtypeStruct((1, B_WIDTH), F32),
            jax.ShapeDtypeStruct((1, B_WIDTH), F32),
        ],
        scratch_shapes=[pltpu.VMEM((tm, A_WIDTH), F32), pltpu.VMEM((tm, A_WIDTH), F32)],
        compiler_params=_params(("arbitrary",)),
    )(proj, dmix, hc, wm, wmt, bcol, lg, lb)


def _even_bwd2(proj, dhc, cw, *, name):
    S = proj.shape[0]
    tm = _tile(S, 256)
    hb = tm // CONV_HALO
    nt = S // tm
    last_halo = S // CONV_HALO - 1
    lo = CONV_HALO - CONV_WIDTH + 1

    def body(p_ref, halo_ref, d_ref, dnext_ref, cw_ref, dpb_ref, dcw_ref, hext_ref, dext_ref, hsh_ref, dsh_ref):
        i = pl.program_id(0)

        @pl.when(i == 0)
        def _():
            dcw_ref[...] = jnp.zeros_like(dcw_ref)

        hh = halo_ref[:, 0:B_WIDTH] * _sigmoid(halo_ref[:, B_WIDTH:2 * B_WIDTH])
        hext_ref[0:CONV_HALO, :] = jnp.where(i > 0, hh, 0.0)
        hext_ref[CONV_HALO:CONV_HALO + tm, :] = p_ref[:, 0:B_WIDTH] * _sigmoid(p_ref[:, B_WIDTH:2 * B_WIDTH])
        dext_ref[0:tm, :] = d_ref[...]
        dext_ref[tm:tm + CONV_HALO, :] = jnp.where(i < nt - 1, dnext_ref[...], 0.0)
        _fill_shifts(hsh_ref, hext_ref, tm)
        _fill_shifts(dsh_ref, dext_ref, tm)
        for r0 in range(0, tm, _CONV_ROWS):
            rows = slice(r0, r0 + _CONV_ROWS)
            dhc_b = d_ref[rows, :]
            dh = jnp.zeros((_CONV_ROWS, B_WIDTH), F32)
            for k in range(CONV_WIDTH):
                dh = dh + cw_ref[k:k + 1, :] * _window(dsh_ref, r0 + CONV_WIDTH - 1 - k, _CONV_ROWS)
                dcw_ref[k:k + 1, :] += jnp.sum(dhc_b * _window(hsh_ref, r0 + k + lo, _CONV_ROWS), axis=0,
                                               keepdims=True)
            ba_b = p_ref[rows, 0:B_WIDTH]
            sg_b = _sigmoid(p_ref[rows, B_WIDTH:2 * B_WIDTH])
            dpb_ref[rows, 0:B_WIDTH] = (dh * sg_b).astype(BF16)
            dpb_ref[rows, B_WIDTH:2 * B_WIDTH] = (dh * ba_b * sg_b * (1.0 - sg_b)).astype(BF16)

    return pl.pallas_call(
        body, name=name, grid=(nt,),
        in_specs=[
            pl.BlockSpec((tm, 1024), lambda i: (i, 1)),
            pl.BlockSpec((CONV_HALO, 1024), lambda i: (jnp.maximum(i * hb - 1, 0), 1)),
            pl.BlockSpec((tm, B_WIDTH), lambda i: (i, 0)),
            pl.BlockSpec((CONV_HALO, B_WIDTH), lambda i: (jnp.minimum((i + 1) * hb, last_halo), 0)),
            pl.BlockSpec((CONV_HALO, B_WIDTH), lambda i: (0, 0)),
        ],
        out_specs=[pl.BlockSpec((tm, 1024), lambda i: (i, 0)), pl.BlockSpec((CONV_HALO, B_WIDTH), lambda i: (0, 0))],
        out_shape=[jax.ShapeDtypeStruct((S, 1024), BF16), jax.ShapeDtypeStruct((CONV_HALO, B_WIDTH), F32)],
        scratch_shapes=[pltpu.VMEM((tm + CONV_HALO, B_WIDTH), F32), pltpu.VMEM((tm + CONV_HALO, B_WIDTH), F32),
                        pltpu.VMEM((_SHIFTS, tm + CONV_HALO, B_WIDTH), F32),
                        pltpu.VMEM((_SHIFTS, tm + CONV_HALO, B_WIDTH), F32)],
        compiler_params=_params(("arbitrary",)),
    )(proj, proj, dhc, dhc, cw)


_CA_SCALE = CA_HEAD_DIM ** -0.5


def _softmax_rows(s):
    e = jnp.exp(s - jnp.max(s, axis=-1, keepdims=True))
    return e / jnp.sum(e, axis=-1, keepdims=True)


def _attn_fwd(q, k, v, *, name):
    S = q.shape[0]

    def body(r, f, o, acc, s):
        for h in range(CA_HEADS):
            cols = _cols(h, CA_HEAD_DIM)
            p = _softmax_rows(_dot_nt(r[0][:, cols], f[0][:, cols]) * _CA_SCALE)
            o[0][:, cols] = _dot(p.astype(BF16), f[1][:, cols]).astype(BF16)

    (o_,), _ = _rows_call(name, _tile(S, 512), [q], [k, v], [(D_MODEL, BF16)], [], body)
    return o_


def _attn_bwd(dy, wo, q, k, v, *, name):
    S = q.shape[0]
    M = k.shape[0]

    def body(r, f, o, acc, s):
        dyb = r[0][...].astype(BF16)
        for h in range(CA_HEADS):
            cols = _cols(h, CA_HEAD_DIM)
            qh = r[1][:, cols]
            kh = f[0][:, cols]
            vh = f[1][:, cols]
            doh = _dot_nt(dyb, f[2 + h][...]).astype(BF16)
            p = _softmax_rows(_dot_nt(qh, kh) * _CA_SCALE)
            acc[1][:, cols] += _dot_tn(p.astype(BF16), doh)
            dp = _dot_nt(doh, vh)
            ds = (p * (dp - jnp.sum(dp * p, axis=-1, keepdims=True)) * _CA_SCALE).astype(BF16)
            o[0][:, cols] = _dot(ds, kh).astype(BF16)
            acc[0][:, cols] += _dot_tn(ds, qh)

    (dq,), (dk, dv) = _rows_call(name, _tile(S, 512), [dy, q], [k, v] + wo, [(D_MODEL, BF16)],
                                 [((M, D_MODEL), F32), ((M, D_MODEL), F32)], body)
    return dq, dk, dv


_STATE_TILE = 2 * STATE_ROWS
N_SETS = 4
SET_CH = C_WIDTH // N_SETS
SET_COLS = N_STATE // N_SETS // STATE_LANES


def _set_groups(j):
    return [SET_COLS * j + c for c in range(SET_COLS)] + [STATE_ROWS + SET_COLS * j + c for c in range(SET_COLS)]


def _pack_state(re, im):
    hi = lax.bitcast_convert_type(re.astype(BF16).astype(F32), jnp.uint32)
    lo = lax.bitcast_convert_type(im.astype(BF16).astype(F32), jnp.uint32) >> 16
    return hi | lo


def _unpack_state(word):
    re = lax.bitcast_convert_type(word & jnp.uint32(0xFFFF0000), F32)
    im = lax.bitcast_convert_type(word << 16, F32)
    return re, im


def _state_set(ref, tm, j):
    parts = [_unpack_state(ref[:, SET_COLS * j + c, :, :].reshape(tm, STATE_LANES)) for c in range(SET_COLS)]
    return jnp.concatenate([p[0].astype(BF16) for p in parts] + [p[1].astype(BF16) for p in parts], axis=1)


def _s5_readout(xs, cset, u, d, *, name, tm=256):
    tm = _tile(u.shape[0], tm)

    def body(r, f, o, acc, s):
        y0 = jnp.concatenate([_dot(_state_set(r[0], tm, j), f[0][j]) for j in range(N_SETS)], axis=1)
        y = y0 + f[1][...] * r[1][...]
        o[0][...] = y
        o[1][...] = _gelu(y)[0].astype(BF16)

    (y, yg), _ = _rows_call(name, tm, [xs, u], [cset, d], [(C_WIDTH, F32), (C_WIDTH, BF16)], [], body)
    return y, yg


def _state_grad_sets(a, st, *, name, ts=256):
    ts = _tile(a.shape[0], ts)

    def body(r, f, o, acc, s):
        for j in range(N_SETS):
            acc[0][j] += _dot_tn(r[0][:, _cols(j, SET_CH)].astype(BF16), _state_set(r[1], ts, j))

    _, (out,) = _rows_call(name, ts, [a, st], [], [], [((N_SETS, SET_CH, 2 * N_STATE // N_SETS), F32)], body)
    return out


def _glu_out(yg, ws, x, *, name, tm=512):
    n = ws[0][1][2]

    def body(r, f, o, acc, s):
        ygv = r[0][...]
        ov = [_dot(ygv, f[p][...]) for p in range(N_CHIPS)]
        for p in range(N_CHIPS):
            o[0][:, _cols(p, n)] = ov[p].astype(BF16)
        for p in range(2):
            o[1][:, _cols(p, n)] = r[1][:, _cols(p, n)] + ov[p] * _sigmoid(ov[2 + p])

    (o_, y), _ = _rows_call(name, _tile(x.shape[0], tm), [yg, x], ws, [(2 * D_MODEL, BF16), (D_MODEL, F32)], [], body)
    return o_, y


def _glu_out_bwd(o_, dy, ws, y, u, d, *, name, tm=256):
    n = ws[0][1][2]

    def body(r, f, o, acc, s):
        o1 = r[0][:, 0:D_MODEL].astype(F32)
        sg = _sigmoid(r[0][:, D_MODEL:2 * D_MODEL].astype(F32))
        dyv = r[1][...]
        do1 = (dyv * sg).astype(BF16)
        do2 = (dyv * o1 * sg * (1.0 - sg)).astype(BF16)
        o[0][:, 0:D_MODEL] = do1
        o[0][:, D_MODEL:2 * D_MODEL] = do2
        dyg = None
        for p in range(N_CHIPS):
            t = _dot_nt((do1 if p < 2 else do2)[:, _cols(p % 2, n)], f[1 + p][...])
            dyg = t if dyg is None else dyg + t
        yv = r[2][...]
        dys = dyg * _gelu_grad(yv, _gelu(yv)[1])
        o[1][...] = dys.astype(BF16)
        o[2][...] = f[0][...] * dys
        acc[0][...] += jnp.sum(dys * r[3][...], axis=0, keepdims=True)

    (do, dys, dus), (dd,) = _rows_call(name, _tile(dy.shape[0], tm), [o_, dy, y, u], [d] + ws,
                                       [(2 * D_MODEL, BF16), (C_WIDTH, BF16), (C_WIDTH, F32)], [((1, C_WIDTH), F32)],
                                       body)
    return do, dys, dus, dd


def _s5_in_bwd(gs, bset, dus, ws, x, g, dres, *, name, tm=256):
    D = x.shape[1]
    tm = _tile(x.shape[0], tm)

    def body(r, f, o, acc, s):
        du0 = jnp.concatenate([_dot_nt(_state_set(r[0], tm, j), f[1][j]) for j in range(N_SETS)], axis=1)
        du = (du0 + r[1][...]).astype(BF16)
        o[0][...] = du
        dx, dg = _rms_bwd_tile(r[2][...], f[0][...], _cat_nt(du, f[2:]))
        o[1][...] = dx + r[3][...]
        acc[0][...] += dg

    (du, dx), (dg,) = _rows_call(name, tm, [gs, dus, x, dres], [_vec(g), bset] + ws,
                                 [(C_WIDTH, BF16), (D, F32)], [((1, D), F32)], body)
    return du, dx, dg


_SCAN_CHUNK = 256
_RE = slice(0, STATE_ROWS)
_IM = slice(STATE_ROWS, 2 * STATE_ROWS)
assert SCAN_BLOCK == 8


def _token(g, i, rows):
    return pl.ds(pl.multiple_of(g * (rows * SCAN_BLOCK), rows * SCAN_BLOCK) + i, rows, stride=SCAN_BLOCK)


def _fill_chunk(s3, a_ref, wset, tc, nt):
    for j in range(N_SETS):
        av = a_ref[:, _cols(j, SET_CH)].astype(BF16)
        y = _dot_nt(av, wset[j]) if nt else _dot(av, wset[j])
        for k, c in enumerate(_set_groups(j)):
            s3[:, 8 * c:8 * (c + 1), :] = y[:, _cols(k, STATE_LANES)].reshape(tc // 8, 8, STATE_LANES)


def _chunk_token(s3, g, i):
    return s3[g, pl.ds(i, _STATE_TILE, stride=SCAN_BLOCK), :]


def _scan_fwd(u, bset, pw, *, name):
    S = u.shape[0]
    tc = _tile(S, _SCAN_CHUNK, 8)

    def body(u_ref, bset_ref, pw_ref, xs_ref, st_ref, s3):
        @pl.when(pl.program_id(0) == 0)
        def _():
            st_ref[...] = jnp.zeros_like(st_ref)

        _fill_chunk(s3, u_ref, bset_ref, tc, nt=False)
        ar = pw_ref[0, _RE, :]
        ai = pw_ref[0, _IM, :]

        def block(g, carry):
            xr, xi = carry
            cr = ci = nr = ni = None
            for j in range(SCAN_BLOCK):
                b = _chunk_token(s3, g, j)
                br, bi = b[_RE], b[_IM]
                cr, ci = (br, bi) if j == 0 else (ar * cr - ai * ci + br, ar * ci + ai * cr + bi)
                pr, pi = pw_ref[j, _RE, :], pw_ref[j, _IM, :]
                nr = pr * xr - pi * xi + cr
                ni = pr * xi + pi * xr + ci
                xs_ref[_token(g, j, STATE_ROWS), :] = _pack_state(nr, ni)
            return nr, ni

        xr, xi = lax.fori_loop(0, tc // SCAN_BLOCK, block, (st_ref[_RE, :], st_ref[_IM, :]), unroll=4)
        st_ref[_RE, :] = xr
        st_ref[_IM, :] = xi

    return pl.pallas_call(
        body, name=name, grid=(S // tc,),
        in_specs=[pl.BlockSpec((tc, u.shape[1]), lambda i: (i, 0)), pl.BlockSpec(bset.shape, lambda i: (0, 0, 0)),
                  pl.BlockSpec(pw.shape, lambda i: (0, 0, 0))],
        out_specs=pl.BlockSpec((tc * STATE_ROWS, STATE_LANES), lambda i: (i, 0)),
        out_shape=jax.ShapeDtypeStruct((S * STATE_ROWS, STATE_LANES), jnp.uint32),
        scratch_shapes=[pltpu.VMEM((2 * STATE_ROWS, STATE_LANES), F32),
                        pltpu.VMEM((tc // 8, _STATE_TILE * 8, STATE_LANES), F32)],
        compiler_params=_params(("arbitrary",)),
    )(u, bset, pw)


def _scan_bwd(dys, cset, xs, pw, *, name):
    S = dys.shape[0]
    tc = _tile(S, _SCAN_CHUNK, 8)
    nc = S // tc

    def body(dys_ref, cset_ref, xs_ref, pw_ref, g_ref, da_ref, st_ref, s3):
        @pl.when(pl.program_id(0) == 0)
        def _():
            st_ref[...] = jnp.zeros_like(st_ref)
            da_ref[...] = jnp.zeros_like(da_ref)

        _fill_chunk(s3, dys_ref, cset_ref, tc, nt=True)
        ar = pw_ref[0, _RE, :]
        ai = pw_ref[0, _IM, :]

        def block(k, carry):
            gr, gi, dar, dai = carry
            g = tc // SCAN_BLOCK - 1 - k
            cr = ci = None
            pgr, pgi = gr, gi
            for j in range(SCAN_BLOCK):
                i = SCAN_BLOCK - 1 - j
                xr, xi = _unpack_state(xs_ref[_token(g, i, STATE_ROWS), :])
                dar = dar + pgr * xr + pgi * xi
                dai = dai + pgi * xr - pgr * xi
                d = _chunk_token(s3, g, i)
                dr, di = d[_RE], d[_IM]
                cr, ci = (dr, di) if j == 0 else (ar * cr + ai * ci + dr, ar * ci - ai * cr + di)
                pr, pi = pw_ref[j, _RE, :], pw_ref[j, _IM, :]
                pgr = pr * gr + pi * gi + cr
                pgi = pr * gi - pi * gr + ci
                g_ref[_token(g, i, STATE_ROWS), :] = _pack_state(pgr, pgi)
            return pgr, pgi, dar, dai

        init = (st_ref[_RE, :], st_ref[_IM, :], da_ref[_RE, :], da_ref[_IM, :])
        gr, gi, dar, dai = lax.fori_loop(0, tc // SCAN_BLOCK, block, init, unroll=4)
        st_ref[_RE, :] = gr
        st_ref[_IM, :] = gi
        da_ref[_RE, :] = dar
        da_ref[_IM, :] = dai

    packed = pl.BlockSpec((tc * STATE_ROWS, STATE_LANES), lambda i: (nc - 1 - i, 0))
    vec = pl.BlockSpec((2 * STATE_ROWS, STATE_LANES), lambda i: (0, 0))
    return pl.pallas_call(
        body, name=name, grid=(nc,),
        in_specs=[pl.BlockSpec((tc, dys.shape[1]), lambda i: (nc - 1 - i, 0)),
                  pl.BlockSpec(cset.shape, lambda i: (0, 0, 0)), packed, pl.BlockSpec(pw.shape, lambda i: (0, 0, 0))],
        out_specs=[packed, vec],
        out_shape=[jax.ShapeDtypeStruct(xs.shape, jnp.uint32), jax.ShapeDtypeStruct((2 * STATE_ROWS, STATE_LANES), F32)],
        scratch_shapes=[pltpu.VMEM((2 * STATE_ROWS, STATE_LANES), F32),
                        pltpu.VMEM((tc // 8, _STATE_TILE * 8, STATE_LANES), F32)],
        compiler_params=_params(("arbitrary",)),
    )(dys, cset, xs, pw)


def _loss_head(x, g, target, *, name):
    S, D = x.shape

    def body(r, f, o, acc, s):
        xv = r[0][...]
        gv = f[0][...]
        rs = lax.rsqrt(_mean(xv * xv) + EPS)
        xh = xv * rs
        err = xh * gv - r[1][...]
        acc[1][...] += 0.5 * jnp.sum(_mean(err * err), axis=0, keepdims=True)
        dy = err * (1.0 / D)
        dyg = dy * gv
        o[0][...] = rs * (dyg - xh * _mean(dyg * xh))
        acc[0][...] += jnp.sum(dy * xh, axis=0, keepdims=True)

    (dx,), (dg, loss) = _rows_call(name, _tile(S, 256, 8), [x, target], [_vec(g)], [(D, F32)],
                                   [((1, D), F32), ((1, 128), F32)], body)
    return dx, dg, loss


_ADAM_C1 = 1.0 - ADAM_B1 ** ADAM_STEP
_ADAM_C2 = 1.0 - ADAM_B2 ** ADAM_STEP
_ONE_BLOCK_BYTES = 8 * 1024 * 1024


def _adamw_math(w, g, m, v):
    nm = ADAM_B1 * m + (1.0 - ADAM_B1) * g
    nv = ADAM_B2 * v + (1.0 - ADAM_B2) * (g * g)
    m_hat = nm / _ADAM_C1
    v_hat = nv / _ADAM_C2
    return -ADAM_LR * (m_hat / (jnp.sqrt(v_hat) + ADAM_EPS) + ADAM_WD * w), nm, nv


def _adamw_shard(w, gsrc, m, v, *, name):
    R, C = w.shape
    n_l = len(gsrc)
    rows = R // n_l
    tr = rows
    for _, r0 in gsrc:
        tr = math.gcd(tr, r0) if r0 else tr
    tr = _tile(tr, 256, 8) if tr > 256 else tr
    nb = rows // tr
    assert rows % tr == 0 and all(r0 % tr == 0 for _, r0 in gsrc)

    def body(*refs):
        w_ref, g_refs, (m_ref, v_ref, go_ref, d_ref, nm_ref, nv_ref) = refs[0], refs[1:1 + n_l], refs[1 + n_l:]
        layer = pl.program_id(0) // nb
        gv = g_refs[0][...]
        for l in range(1, n_l):
            gv = jnp.where(layer == l, g_refs[l][...], gv)
        go_ref[...] = gv
        d_ref[...], nm_ref[...], nv_ref[...] = _adamw_math(w_ref[...], gv, m_ref[...], v_ref[...])

    def g_spec(l, r0):
        return pl.BlockSpec((tr, C), lambda i: (r0 // tr + jnp.clip(i - l * nb, 0, nb - 1), 0))

    blk = pl.BlockSpec((tr, C), lambda i: (i, 0))
    out = jax.ShapeDtypeStruct((R, C), F32)
    return pl.pallas_call(
        body, name=name, grid=(R // tr,),
        in_specs=[blk] + [g_spec(l, r0) for l, (_, r0) in enumerate(gsrc)] + [blk, blk], out_specs=[blk] * 4,
        out_shape=[out] * 4, compiler_params=_params(("parallel",)),
    )(w, *[g for g, _ in gsrc], m, v)


def _adamw_small(ws, gs, ms, vs, *, name):
    n = len(ws)

    def body(*refs):
        w_r, g_r, m_r, v_r = refs[:n], refs[n:2 * n], refs[2 * n:3 * n], refs[3 * n:4 * n]
        d_r, nm_r, nv_r = refs[4 * n:5 * n], refs[5 * n:6 * n], refs[6 * n:7 * n]
        for k in range(n):
            d_r[k][...], nm_r[k][...], nv_r[k][...] = _adamw_math(w_r[k][...], g_r[k][...], m_r[k][...], v_r[k][...])

    vm = pl.BlockSpec(memory_space=pltpu.VMEM)
    out = [jax.ShapeDtypeStruct(w.shape, F32) for w in ws]
    res = pl.pallas_call(body, name=name, in_specs=[vm] * (4 * n), out_specs=[vm] * (3 * n), out_shape=out * 3,
                         compiler_params=pltpu.CompilerParams(vmem_limit_bytes=VMEM_LIMIT))(*ws, *gs, *ms, *vs)
    return res[:n], res[n:2 * n], res[2 * n:]


def _sum_slots(x, *, name):
    n, R, C = x.shape
    tr = R if (n + 1) * R * C * 4 <= _ONE_BLOCK_BYTES else _tile(R, 256, 8)

    def body(x_ref, o_ref):
        acc = x_ref[0]
        for k in range(1, n):
            acc = acc + x_ref[k]
        o_ref[...] = acc

    return pl.pallas_call(
        body, name=name, grid=(R // tr,),
        in_specs=[pl.BlockSpec((n, tr, C), lambda i: (0, i, 0))], out_specs=pl.BlockSpec((tr, C), lambda i: (i, 0)),
        out_shape=jax.ShapeDtypeStruct((R, C), F32), compiler_params=_params(("parallel",)),
    )(x)


def _pair_sum(g, r, where, *, name):
    n, R, C = g.shape
    Rh = R // 2
    tr = _tile(Rh, 256, 8)
    nb = Rh // tr

    def body(where_ref, g_ref, r_ref, o_ref):
        o_ref[...] = (g_ref[...] + r_ref[...]).astype(BF16)

    def slot(p, w):
        return p + jnp.where(p >= w[0], 1, 0)

    return pl.pallas_call(
        body, name=name,
        grid_spec=pltpu.PrefetchScalarGridSpec(
            num_scalar_prefetch=1, grid=(n - 1, nb),
            in_specs=[pl.BlockSpec((1, tr, C), lambda p, i, w: (slot(p, w), w[1] * nb + i, 0)),
                      pl.BlockSpec((1, tr, C), lambda p, i, w: (slot(p, w), i, 0))],
            out_specs=pl.BlockSpec((1, tr, C), lambda p, i, w: (slot(p, w), i, 0)),
        ),
        out_shape=jax.ShapeDtypeStruct((n, Rh, C), BF16), compiler_params=_params(("parallel", "parallel")),
    )(where, g, r)


def _chip_sum(g, r, slots, where, *, name):
    n, R, C = g.shape
    Rh = R // 2
    tr = _tile(Rh, 256, 8)
    nb = Rh // tr

    def body(w_ref, g_ref, r_ref, s_ref, o_ref):
        acc = g_ref[0] + r_ref[0]
        for k in range(slots.shape[0]):
            acc = acc + s_ref[k].astype(F32)
        o_ref[...] = acc

    return pl.pallas_call(
        body, name=name,
        grid_spec=pltpu.PrefetchScalarGridSpec(
            num_scalar_prefetch=1, grid=(nb,),
            in_specs=[pl.BlockSpec((1, tr, C), lambda i, w: (w[0], w[1] * nb + i, 0)),
                      pl.BlockSpec((1, tr, C), lambda i, w: (w[0], i, 0)),
                      pl.BlockSpec((slots.shape[0], tr, C), lambda i, w: (0, i, 0))],
            out_specs=pl.BlockSpec((tr, C), lambda i, w: (w[1] * nb + i, 0)),
        ),
        out_shape=jax.ShapeDtypeStruct((R, C), F32), compiler_params=_params(("parallel",)),
    )(where, g, r, slots)


ANY = pl.BlockSpec(memory_space=pl.ANY)


def _place():
    return lax.axis_index("x"), lax.axis_index("y"), lax.axis_index("c")


def _other_chips(x, y):
    return [(1 - x, y), (x, 1 - y), (1 - x, 1 - y)]


def _allgather_small(v, *, name):
    R, C = v.shape

    def body(x_ref, out_ref, send_sems, recv_sems, local_sem):
        x, y, c = _place()
        me, sibling = (x, y, c), (x, y, 1 - c)
        chips = _other_chips(x, y)

        def rows(px, py, pc):
            return out_ref.at[pl.ds((4 * px + 2 * py + pc) * R, R), :]

        def copy(k, block, to, src=None):
            return pltpu.make_async_remote_copy(
                src_ref=rows(*block) if src is None else src, dst_ref=rows(*block),
                send_sem=send_sems.at[k], recv_sem=recv_sems.at[k], device_id=to, device_id_type=MESH)

        mine = pltpu.make_async_copy(x_ref, rows(*me), local_sem)
        mine.start()
        first = [copy(0, me, sibling, src=x_ref)]
        first += [copy(1 + j, me, (*chip, c), src=x_ref) for j, chip in enumerate(chips)]
        for cp in first:
            cp.start()
        passed = [copy(4 + j, (*chip, c), sibling) for j, chip in enumerate(chips)]
        for j, chip in enumerate(chips):
            copy(1 + j, (*chip, c), me).wait_recv()
            passed[j].start()
        copy(0, sibling, me).wait_recv()
        for j, chip in enumerate(chips):
            copy(4 + j, (*chip, 1 - c), me).wait_recv()
        for cp in first + passed:
            cp.wait_send()
        mine.wait()

    return pl.pallas_call(
        body, name=name, out_shape=jax.ShapeDtypeStruct((N_DEV * R, C), v.dtype),
        in_specs=[pl.BlockSpec(memory_space=pltpu.VMEM)], out_specs=pl.BlockSpec(memory_space=pltpu.VMEM),
        scratch_shapes=[pltpu.SemaphoreType.DMA((7,)), pltpu.SemaphoreType.DMA((7,)), pltpu.SemaphoreType.DMA],
        compiler_params=pltpu.CompilerParams(vmem_limit_bytes=VMEM_LIMIT),
    )(v)


def _aliased_comm_call(body, bufs, n_sems, *, name):
    n = len(bufs)
    return pl.pallas_call(
        body, name=name, out_shape=[jax.ShapeDtypeStruct(b.shape, b.dtype) for b in bufs],
        in_specs=[ANY] * n, out_specs=[ANY] * n, input_output_aliases={k: k for k in range(n)},
        scratch_shapes=[pltpu.SemaphoreType.DMA((n_sems,)), pltpu.SemaphoreType.DMA((n_sems,))],
    )(*bufs)


HBM = pl.BlockSpec(memory_space=pltpu.HBM)
SEM = pl.BlockSpec(memory_space=pltpu.SEMAPHORE)
_SPLIT = pltpu.CompilerParams(has_side_effects=pltpu.SideEffectType.DATAFLOW_SIDE_EFFECTING)


def _in_hbm(arrs):
    return [pltpu.with_memory_space_constraint(a, pltpu.HBM) for a in arrs]


def _gather_ici_start(bufs, after, *, name):
    n = len(bufs)

    def body(*refs):
        send_sems, recv_sems, outs, token = refs[n + 1], refs[n + 2], refs[n + 3:2 * n + 3], refs[2 * n + 3]
        x, y, c = _place()
        for b in range(n):
            rh = bufs[b].shape[1] // 2
            part = outs[b].at[2 * x + y, pl.ds(c * rh, rh), :]
            for j, chip in enumerate(_other_chips(x, y)):
                pltpu.make_async_remote_copy(src_ref=part, dst_ref=part, send_sem=send_sems.at[3 * b + j],
                                             recv_sem=recv_sems.at[3 * b + j], device_id=(*chip, c),
                                             device_id_type=MESH).start()
        token[...] = jnp.zeros_like(token)

    res = pl.pallas_call(
        body, name=name,
        out_shape=(pltpu.SemaphoreType.DMA((3 * n,)), pltpu.SemaphoreType.DMA((3 * n,)),
                   *[pltpu.HBM(b.shape, b.dtype) for b in bufs], jax.ShapeDtypeStruct((8, 128), F32)),
        in_specs=[HBM] * n + [ANY], out_specs=(SEM, SEM, *[HBM] * n, pl.BlockSpec(memory_space=pltpu.VMEM)),
        input_output_aliases={k: k + 2 for k in range(n)}, compiler_params=_SPLIT,
    )(*_in_hbm(bufs), after)
    return res[0], res[1], list(res[2:2 + n]), res[2 + n]


def _gather_ici_wait(send_sems, recv_sems, bufs, after, *, name):
    n = len(bufs)

    def body(*refs):
        ins, ss, rs = refs[:n], refs[n], refs[n + 1]
        x, y, c = _place()
        for b in range(n):
            rh = bufs[b].shape[1] // 2
            mine = ins[b].at[2 * x + y, pl.ds(c * rh, rh), :]
            for j, (cx, cy) in enumerate(_other_chips(x, y)):
                theirs = ins[b].at[2 * cx + cy, pl.ds(c * rh, rh), :]
                cp = pltpu.make_async_remote_copy(src_ref=mine, dst_ref=theirs, send_sem=ss.at[3 * b + j],
                                                  recv_sem=rs.at[3 * b + j], device_id=(cx, cy, c),
                                                  device_id_type=MESH)
                cp.wait_send()
                cp.wait_recv()

    return list(pl.pallas_call(
        body, name=name, out_shape=[pltpu.HBM(b.shape, b.dtype) for b in bufs],
        in_specs=[HBM] * n + [SEM, SEM, ANY], out_specs=[HBM] * n,
        input_output_aliases={k: k for k in range(n)}, compiler_params=_SPLIT,
    )(*bufs, send_sems, recv_sems, after))


def _gather_forward(bufs, *, name):
    n = len(bufs)

    def body(*refs):
        outs, send_sems, recv_sems = refs[n:2 * n], refs[2 * n], refs[2 * n + 1]
        x, y, c = _place()

        def copy(b, j, chip, hc):
            rh = bufs[b].shape[1] // 2
            part = outs[b].at[2 * chip[0] + chip[1], pl.ds(hc * rh, rh), :]
            return pltpu.make_async_remote_copy(src_ref=part, dst_ref=part, send_sem=send_sems.at[3 * b + j],
                                                recv_sem=recv_sems.at[3 * b + j], device_id=(x, y, 1 - c),
                                                device_id_type=MESH)

        sends = [copy(b, j, chip, c) for b in range(n) for j, chip in enumerate(_other_chips(x, y))]
        for cp in sends:
            cp.start()
        for b in range(n):
            for j, chip in enumerate(_other_chips(x, y)):
                copy(b, j, chip, 1 - c).wait_recv()
        for cp in sends:
            cp.wait_send()

    return _aliased_comm_call(body, bufs, 3 * n, name=name)


def _chip_exchange_start(hs, *, name):
    n = len(hs)
    lands = [lax.empty((3,) + h.shape[1:], h.dtype) for h in hs]

    def body(*refs):
        send_sems, recv_sems = refs[2 * n], refs[2 * n + 1]
        h_out, l_out, token = refs[2 * n + 2:3 * n + 2], refs[3 * n + 2:4 * n + 2], refs[4 * n + 2]
        x, y, c = _place()
        for b in range(n):
            for j, (cx, cy) in enumerate(_other_chips(x, y)):
                pltpu.make_async_remote_copy(src_ref=h_out[b].at[2 * cx + cy], dst_ref=l_out[b].at[j],
                                             send_sem=send_sems.at[3 * b + j], recv_sem=recv_sems.at[3 * b + j],
                                             device_id=(cx, cy, c), device_id_type=MESH).start()
        token[...] = jnp.zeros_like(token)

    res = pl.pallas_call(
        body, name=name,
        out_shape=(pltpu.SemaphoreType.DMA((3 * n,)), pltpu.SemaphoreType.DMA((3 * n,)),
                   *[pltpu.HBM(a.shape, a.dtype) for a in hs + lands], jax.ShapeDtypeStruct((8, 128), F32)),
        in_specs=[HBM] * (2 * n), out_specs=(SEM, SEM, *[HBM] * (2 * n), pl.BlockSpec(memory_space=pltpu.VMEM)),
        input_output_aliases={k: k + 2 for k in range(2 * n)}, compiler_params=_SPLIT,
    )(*_in_hbm(hs + lands))
    return res[0], res[1], list(res[2:2 + n]), list(res[2 + n:2 + 2 * n]), res[2 + 2 * n]


def _chip_exchange_wait(send_sems, recv_sems, hs, lands, after, *, name):
    n = len(hs)

    def body(*refs):
        h_in, l_in, ss, rs = refs[:n], refs[n:2 * n], refs[2 * n], refs[2 * n + 1]
        x, y, c = _place()
        for b in range(n):
            for j, (cx, cy) in enumerate(_other_chips(x, y)):
                cp = pltpu.make_async_remote_copy(src_ref=h_in[b].at[2 * cx + cy], dst_ref=l_in[b].at[j],
                                                  send_sem=ss.at[3 * b + j], recv_sem=rs.at[3 * b + j],
                                                  device_id=(cx, cy, c), device_id_type=MESH)
                cp.wait_send()
                cp.wait_recv()

    res = pl.pallas_call(
        body, name=name, out_shape=[pltpu.HBM(a.shape, a.dtype) for a in hs + lands],
        in_specs=[HBM] * (2 * n) + [SEM, SEM, ANY], out_specs=[HBM] * (2 * n),
        input_output_aliases={k: k for k in range(2 * n)}, compiler_params=_SPLIT,
    )(*hs, *lands, send_sems, recv_sems, after)
    return list(res[n:])


def _peers(x, y, c):
    return [((1 - x) if fx else x, (1 - y) if fy else y, (1 - c) if fc else c)
            for fx in (0, 1) for fy in (0, 1) for fc in (0, 1) if fx or fy or fc]


def _all_to_all_start(slab, after, *, name):
    land = lax.empty((N_DEV,) + slab.shape, slab.dtype)

    def body(slab_in, land_in, after_ref, send_sems, recv_sems, slab_out, land_out, token):
        x, y, c = _place()
        for k, peer in enumerate(_peers(x, y, c)):
            pltpu.make_async_remote_copy(src_ref=slab_out, dst_ref=land_out.at[4 * x + 2 * y + c],
                                         send_sem=send_sems.at[k], recv_sem=recv_sems.at[k], device_id=peer,
                                         device_id_type=MESH).start()
        token[...] = jnp.zeros_like(token)

    return pl.pallas_call(
        body, name=name,
        out_shape=(pltpu.SemaphoreType.DMA((N_DEV - 1,)), pltpu.SemaphoreType.DMA((N_DEV - 1,)),
                   pltpu.HBM(slab.shape, slab.dtype), pltpu.HBM(land.shape, land.dtype),
                   jax.ShapeDtypeStruct((8, 128), F32)),
        in_specs=[HBM, HBM, ANY], out_specs=(SEM, SEM, HBM, HBM, pl.BlockSpec(memory_space=pltpu.VMEM)),
        input_output_aliases={0: 2, 1: 3}, compiler_params=_SPLIT,
    )(*_in_hbm([slab, land]), after)


def _all_to_all_wait(send_sems, recv_sems, slab, land, after, *, name):
    def body(slab_in, land_in, ss, rs, after_ref, slab_out, land_out):
        x, y, c = _place()
        for k, (px, py, pc) in enumerate(_peers(x, y, c)):
            cp = pltpu.make_async_remote_copy(src_ref=slab_in, dst_ref=land_in.at[4 * px + 2 * py + pc],
                                              send_sem=ss.at[k], recv_sem=rs.at[k], device_id=(px, py, pc),
                                              device_id_type=MESH)
            cp.wait_send()
            cp.wait_recv()

    return pl.pallas_call(
        body, name=name, out_shape=[pltpu.HBM(slab.shape, slab.dtype), pltpu.HBM(land.shape, land.dtype)],
        in_specs=[HBM, HBM, SEM, SEM, ANY], out_specs=[HBM, HBM], input_output_aliases={0: 0, 1: 1},
        compiler_params=_SPLIT,
    )(slab, land, send_sems, recv_sems, after)


def _pair_exchange_start(gs, *, name):
    n = len(gs)
    lands = [lax.empty((g.shape[0], g.shape[1] // 2, g.shape[2]), g.dtype) for g in gs]

    def body(*refs):
        send_sems, recv_sems = refs[2 * n], refs[2 * n + 1]
        g_out, l_out, token = refs[2 * n + 2:3 * n + 2], refs[3 * n + 2:4 * n + 2], refs[4 * n + 2]
        x, y, c = _place()
        for b in range(n):
            rh = gs[b].shape[1] // 2
            pltpu.make_async_remote_copy(src_ref=g_out[b].at[:, pl.ds((1 - c) * rh, rh), :], dst_ref=l_out[b],
                                         send_sem=send_sems.at[b], recv_sem=recv_sems.at[b],
                                         device_id=(x, y, 1 - c), device_id_type=MESH).start()
        token[...] = jnp.zeros_like(token)

    res = pl.pallas_call(
        body, name=name,
        out_shape=(pltpu.SemaphoreType.DMA((n,)), pltpu.SemaphoreType.DMA((n,)),
                   *[pltpu.HBM(a.shape, a.dtype) for a in gs + lands], jax.ShapeDtypeStruct((8, 128), F32)),
        in_specs=[HBM] * (2 * n), out_specs=(SEM, SEM, *[HBM] * (2 * n), pl.BlockSpec(memory_space=pltpu.VMEM)),
        input_output_aliases={k: k + 2 for k in range(2 * n)}, compiler_params=_SPLIT,
    )(*_in_hbm(gs + lands))
    return res[0], res[1], list(res[2:2 + n]), list(res[2 + n:2 + 2 * n]), res[2 + 2 * n]


def _pair_exchange_wait(send_sems, recv_sems, gs, lands, after, *, name):
    n = len(gs)

    def body(*refs):
        g_in, l_in, ss, rs = refs[:n], refs[n:2 * n], refs[2 * n], refs[2 * n + 1]
        x, y, c = _place()
        for b in range(n):
            rh = gs[b].shape[1] // 2
            cp = pltpu.make_async_remote_copy(src_ref=g_in[b].at[:, pl.ds((1 - c) * rh, rh), :], dst_ref=l_in[b],
                                              send_sem=ss.at[b], recv_sem=rs.at[b], device_id=(x, y, 1 - c),
                                              device_id_type=MESH)
            cp.wait_send()
            cp.wait_recv()

    res = pl.pallas_call(
        body, name=name, out_shape=[pltpu.HBM(a.shape, a.dtype) for a in gs + lands],
        in_specs=[HBM] * (2 * n) + [SEM, SEM, ANY], out_specs=[HBM] * (2 * n),
        input_output_aliases={k: k for k in range(2 * n)}, compiler_params=_SPLIT,
    )(*gs, *lands, send_sems, recv_sems, after)
    return list(res[:n]), list(res[n:])


def _pair_share(ss, *, name):
    n = len(ss)

    def body(*refs):
        outs, send_sems, recv_sems = refs[n:2 * n], refs[2 * n], refs[2 * n + 1]
        x, y, c = _place()
        cps = []
        for b in range(n):
            rh = ss[b].shape[0] // 2
            mine = outs[b].at[pl.ds(c * rh, rh), :]
            cps.append(pltpu.make_async_remote_copy(src_ref=mine, dst_ref=mine, send_sem=send_sems.at[b],
                                                    recv_sem=recv_sems.at[b], device_id=(x, y, 1 - c),
                                                    device_id_type=MESH))
        for cp in cps:
            cp.start()
        for b, cp in enumerate(cps):
            rh = ss[b].shape[0] // 2
            theirs = outs[b].at[pl.ds((1 - c) * rh, rh), :]
            pltpu.make_async_remote_copy(src_ref=theirs, dst_ref=theirs, send_sem=send_sems.at[b],
                                         recv_sem=recv_sems.at[b], device_id=(x, y, 1 - c),
                                         device_id_type=MESH).wait_recv()
            cp.wait_send()

    return _aliased_comm_call(body, ss, n, name=name)


_SMALL_SHARDED = (("e_conv_w", 2), ("o_norm", 1), ("o_d", 1))
_REPLICATED = ("e_norm", "e_gmlp_w", "e_gmlp_b", "e_conv_b", "e_conv_ln_g", "e_conv_ln_b", "o_lam_re", "o_lam_im",
               "o_log_dt", "o_b_re", "o_b_im", "o_c_re", "o_c_im", "ca_norm", "ca_mem_norm", "ffn_norm", "final_norm")
_SMALL = tuple(n for n, _ in _SMALL_SHARDED) + _REPLICATED
_WEIGHTS = ("e_norm", "e_w_in", "e_gmlp_w", "e_gmlp_b", "e_conv_w", "e_conv_b", "e_conv_ln_g", "e_conv_ln_b",
            "e_w_out", "o_norm", "o_w_in", "o_lam_re", "o_lam_im", "o_log_dt", "o_b_re", "o_b_im", "o_c_re", "o_c_im",
            "o_d", "o_w_out", "ca_norm", "ca_mem_norm", "ca_wq", "ca_wk", "ca_wv", "ca_wo", "ffn_norm", "ffn_w_gate",
            "ffn_w_up", "ffn_w_down", "final_norm")


def _pack_rows(arrs, width, dtype, row_mult=8):
    parts, spans, r0 = [], [], 0
    for a in arrs:
        flat = a.reshape(-1).astype(dtype)
        rows = -(-flat.shape[0] // (width * row_mult)) * row_mult
        if rows * width != flat.shape[0]:
            flat = jnp.pad(flat, (0, rows * width - flat.shape[0]))
        parts.append(flat.reshape(rows, width))
        spans.append((r0, rows))
        r0 += rows
    return jnp.concatenate(parts, axis=0), spans


def _unpack_rows(slab, spans, shapes):
    out = []
    for (r0, rows), shp in zip(spans, shapes):
        n = math.prod(shp)
        out.append(slab[r0:r0 + rows].reshape(-1)[:n].reshape(shp))
    return out


def _two_d(a):
    return a.reshape(-1, a.shape[-1])


def _shard_rows(n, a):
    return _two_d(jnp.swapaxes(a, -1, -2) if n in _TRANSPOSED else a)


def _from_shard_rows(n, rows, shape):
    if n in _TRANSPOSED:
        return jnp.swapaxes(rows.reshape(shape[:-2] + (shape[-1], shape[-2])), -1, -2)
    return rows.reshape(shape)


def _local_slab(local, slab, dtype):
    parts = sorted((r0, n, l) for n, (_, where) in _PLACE.items() for l, (s, r0) in enumerate(where) if s == slab)
    shards = [_shard_rows(n, local[n] if len(_PLACE[n][1]) == 1 else local[n][l]) for _, n, l in parts]
    return jnp.concatenate([a.astype(dtype) for a in shards], axis=0)


def _set_diag(b, pattern):
    return jnp.einsum(pattern, b, jnp.eye(C_GROUPS // N_SETS, dtype=b.dtype))


def _s5_discretize(lam_re, lam_im, log_dt, b_re, b_im):
    dt = jnp.exp(log_dt)[:, None]
    mag = jnp.exp(lam_re * dt)
    ar = mag * jnp.cos(lam_im * dt)
    ai = mag * jnp.sin(lam_im * dt)
    den = lam_re * lam_re + lam_im * lam_im
    qr = ((ar - 1.0) * lam_re + ai * lam_im) / den
    qi = (ai * lam_re - (ar - 1.0) * lam_im) / den
    bbr = qr[..., None] * b_re - qi[..., None] * b_im
    bbi = qr[..., None] * b_im + qi[..., None] * b_re
    return ar, ai, bbr, bbi


def _attention_block(x, mem, W, w, i, tag):
    xn, q = _norm_mm(x, w["ca_norm"][i], _shards(W, "ca_wq", i), split="k", out_dtype=BF16, name=f"{tag}_q")
    memn = _rms_fwd(mem, w["ca_mem_norm"][i], name=f"{tag}_ca_memnorm")
    k = _mm_k(memn, _shards(W, "ca_wk", i), out_dtype=BF16, name=f"{tag}_k")
    v = _mm_k(memn, _shards(W, "ca_wv", i), out_dtype=BF16, name=f"{tag}_v")
    o = _attn_fwd(q, k, v, name=f"{tag}_attn")
    y = _mm_k(o, _shards(W, "ca_wo", i), add=x, name=f"{tag}_wo")
    return y, (x, xn, memn, q, k, v, o)


def _attention_block_bwd(dy, saved, mem, W, w, i, tag, G, grads, token=None, mid=None):
    x, xn, memn, q, k, v, o = saved
    gain = w["ca_norm"][i]
    if token is not None:
        k = _behind(k, token)
    G = _grad_to_slab(G, "ca_wo", i, o, dy, a_cols=256, name=f"{tag}_dwo")
    dq, dk, dv = _attn_bwd(dy, _shards(W, "ca_wo", i), q, k, v, name=f"{tag}_attn_bwd")
    token = mid(dq) if mid is not None else None
    if token is not None:
        gain = _behind(gain, token)
    G = _grad_to_slab(G, "ca_wq", i, xn, dq, a_cols=256, name=f"{tag}_dwq")
    G = _grad_to_slab(G, "ca_wk", i, memn, dk, a_cols=256, name=f"{tag}_dwk")
    G = _grad_to_slab(G, "ca_wv", i, memn, dv, a_cols=256, name=f"{tag}_dwv")
    dmemn = _mm_k_t([(dk, _shards(W, "ca_wk", i)), (dv, _shards(W, "ca_wv", i))], name=f"{tag}_dmemn")
    dx, dg = _norm_bwd_k(dq, _shards(W, "ca_wq", i), x, gain, dy, name=f"{tag}_dq_norm_bwd")
    grads["ca_norm"][i] = dg[0]
    grads["ca_mem_norm"][i] = _rms_dg(mem, w["ca_mem_norm"][i], dmemn, name=f"{tag}_ca_memnorm_bwd")[0]
    return dx, G


def _ffn_block(x, W, w, i, tag):
    fn, gate, up, h = _ffn_up(x, w["ffn_norm"][i], _shards(W, "ffn_w_gate", i), _shards(W, "ffn_w_up", i),
                              name=f"{tag}_ffn_up")
    y = _mm_k(h, _shards(W, "ffn_w_down", i), add=x, name=f"{tag}_down")
    return y, (x, fn, gate, up, h)


def _ffn_block_bwd(dy, saved, W, w, i, tag, G, grads, token=None, mid=None):
    x, fn, gate, up, h = saved
    gain = w["ffn_norm"][i]
    G = _grad_to_slab(G, "ffn_w_down", i, h, dy, name=f"{tag}_dwd")
    dg, du = _ffn_bwd_hidden(dy, _shards(W, "ffn_w_down", i), gate, up, token, name=f"{tag}_ffn_bwd_hidden")
    token = mid(dg) if mid is not None else None
    if token is not None:
        gain = _behind(gain, token)
    G = _grad_to_slab(G, "ffn_w_gate", i, dg, fn, name=f"{tag}_dwg")
    G = _grad_to_slab(G, "ffn_w_up", i, du, fn, name=f"{tag}_dwu")
    dx, dgn = _ffn_in_bwd(dg, du, _shards(W, "ffn_w_gate", i), _shards(W, "ffn_w_up", i), x, gain, dy,
                          name=f"{tag}_ffn_in_bwd")
    grads["ffn_norm"][i] = dgn[0]
    return dx, G


def _gmlp_mask():
    chunk = jnp.arange(GMLP_BLOCK) // CHUNK
    return chunk[None, :] <= chunk[:, None]


def _even_block(x, W, w, tag):
    hn, proj = _norm_mm(x, w["e_norm"][0], _shards(W, "e_w_in"), split="n", out_dtype=F32, name=f"{tag}_w_in")
    wm = jnp.where(_gmlp_mask()[None], w["e_gmlp_w"][0], 0.0).astype(BF16)
    bcol = w["e_gmlp_b"][0][:, :, None]
    cw = jnp.pad(w["e_conv_w"][0], ((0, CONV_HALO - CONV_WIDTH), (0, 0)))
    cb, lg, lb = w["e_conv_b"], w["e_conv_ln_g"], w["e_conv_ln_b"]
    mix, hc = _even_fwd(proj, wm, bcol, cw, cb, lg, lb, name=f"{tag}_mixers")
    y = _mm_k(mix, _shards(W, "e_w_out"), add=x, name=f"{tag}_w_out")
    return y, (x, hn, proj, mix, hc, wm, bcol, cw)


def _even_block_bwd(dy, saved, W, w, tag, G, grads, token=None, mid=None):
    x, hn, proj, mix, hc, wm, bcol, cw = saved
    ln_g = w["e_conv_ln_g"] if token is None else _behind(w["e_conv_ln_g"], token)
    dmix = _mm_k_t([(dy, _shards(W, "e_w_out"))], name=f"{tag}_dmix")
    G = _grad_to_slab(G, "e_w_out", 0, mix, dy, a_cols=256, name=f"{tag}_dw_out")
    wmt = jnp.swapaxes(wm, 1, 2)
    dpa, dhc, dwm, db, dlg, dlb, dcb = _even_bwd1(proj, dmix, hc, wm, wmt, bcol, ln_g, w["e_conv_ln_b"],
                                                  name=f"{tag}_mixers_bwd1")
    token = mid(dpa) if mid is not None else None
    if token is not None:
        cw = _behind(cw, token)
    dpb, dcw = _even_bwd2(proj, dhc, cw, name=f"{tag}_mixers_bwd2")
    grads["e_gmlp_w"] = jnp.where(_gmlp_mask()[None], dwm, 0.0)[None]
    grads["e_gmlp_b"] = db[:, :, 0][None]
    grads["e_conv_ln_g"], grads["e_conv_ln_b"], grads["e_conv_b"] = dlg, dlb, dcb
    grads["e_conv_w"] = dcw[:CONV_WIDTH][None]
    G = _grad_to_slab(G, "e_w_in", 0, hn, dpa, b_cols=512, chips=(0, 2), name=f"{tag}_dw_in_a")
    G = _grad_to_slab(G, "e_w_in", 0, hn, dpb, b_cols=512, chips=(2, 2), name=f"{tag}_dw_in_b")
    dx, dg = _norm_bwd_n((dpa, dpb), _shards(W, "e_w_in"), x, w["e_norm"][0], dy, name=f"{tag}_in_bwd")
    grads["e_norm"] = dg
    return dx, G


def _odd_block(x, W, w, tag):
    S = x.shape[0]
    hn, u = _norm_mm(x, w["o_norm"][0], _shards(W, "o_w_in"), split="k", out_dtype=F32, name=f"{tag}_w_in")
    disc_in = (w["o_lam_re"][0], w["o_lam_im"][0], w["o_log_dt"][0], w["o_b_re"][0], w["o_b_im"][0])
    (ar, ai, bbr, bbi), disc_vjp = jax.vjp(_s5_discretize, *disc_in)
    sets = (N_SETS, C_GROUPS // N_SETS)
    per_set = N_STATE // N_SETS
    bset = jnp.concatenate([_set_diag(b.reshape(sets + b.shape[1:]), "jgpc,gh->jgchp").reshape(N_SETS, SET_CH, per_set)
                            for b in (bbr, bbi)], axis=2).astype(BF16)
    cset = jnp.concatenate([_set_diag(c.reshape(sets + c.shape[1:]), "jgcp,gh->jgphc").reshape(N_SETS, per_set, SET_CH)
                            for c in (w["o_c_re"][0], -w["o_c_im"][0])], axis=1).astype(BF16)
    powers, pr, pi = [], ar, ai
    for _ in range(SCAN_BLOCK):
        powers.append(jnp.concatenate([pr.reshape(STATE_ROWS, STATE_LANES), pi.reshape(STATE_ROWS, STATE_LANES)], 0))
        pr, pi = pr * ar - pi * ai, pr * ai + pi * ar
    pw = jnp.stack(powers, axis=0)
    xs = _scan_fwd(u, bset, pw, name=f"{tag}_scan").reshape(S // 8, STATE_ROWS, 8, STATE_LANES)
    yv, yg = _s5_readout(xs, cset, u, w["o_d"], name=f"{tag}_readout")
    o, y = _glu_out(yg, _shards(W, "o_w_out"), x, name=f"{tag}_glu_out")
    return y, (x, hn, u, bset, cset, pw, xs, yv, yg, o, disc_vjp)


def _odd_block_bwd(dy, saved, W, w, tag, G, grads):
    x, hn, u, bset, cset, pw, xs, yv, yg, o, disc_vjp = saved
    S = x.shape[0]
    do, dys, dus, dd = _glu_out_bwd(o, dy, _shards(W, "o_w_out"), yv, u, w["o_d"], name=f"{tag}_glu_out_bwd")
    G = _grad_to_slab(G, "o_w_out", 0, yg, do, b_cols=512, name=f"{tag}_dw_out")
    grads["o_d"] = dd
    dcset_t = _state_grad_sets(dys, xs, name=f"{tag}_dcd")
    gs, da = _scan_bwd(dys, cset, xs.reshape(S * STATE_ROWS, STATE_LANES), pw, name=f"{tag}_scan_bwd")
    gs = gs.reshape(xs.shape)
    dbset = _state_grad_sets(u, gs, name=f"{tag}_dbd")
    du, dx, dg = _s5_in_bwd(gs, bset, dus, _shards(W, "o_w_in"), x, w["o_norm"][0], dy, name=f"{tag}_in_bwd")
    G = _grad_to_slab(G, "o_w_in", 0, hn, du, a_cols=256, name=f"{tag}_dw_in")
    grads["o_norm"] = dg
    per = C_GROUPS // N_SETS
    blocks = (N_SETS, per, C_GROUP_CH, 2, per, C_STATE)
    dc = _set_diag(dcset_t.reshape(blocks), "jhcrgp,gh->rjgcp").reshape(2, C_GROUPS, C_GROUP_CH, C_STATE)
    db = _set_diag(dbset.reshape(blocks), "jgcrhp,gh->rjgpc").reshape(2, C_GROUPS, C_STATE, C_GROUP_CH)
    dcr, dci, dbbr, dbbi = dc[0], -dc[1], db[0], db[1]
    dar = da[:STATE_ROWS].reshape(C_GROUPS, C_STATE)
    dai = da[STATE_ROWS:].reshape(C_GROUPS, C_STATE)
    dlr, dli, dldt, dbr, dbi = disc_vjp((dar, dai, dbbr, dbbi))
    grads["o_lam_re"], grads["o_lam_im"], grads["o_log_dt"] = dlr[None], dli[None], dldt[None]
    grads["o_b_re"], grads["o_b_im"], grads["o_c_re"], grads["o_c_im"] = dbr[None], dbi[None], dcr[None], dci[None]
    return dx, G


def _behind(value, token):
    return value + token[0, 0].astype(value.dtype)


class _NoExchange:
    def __init__(self, W):
        self.W = W

    def first_weights(self, w):
        return self.W, w

    def weights(self, stage, after):
        return {}

    def grads_ready(self, piece, G):
        return None

    def grads_crossed(self, piece, after):
        return None


def _forward_backward(xs_, mems_, tgt, w, G, exchange):
    W, w = exchange.first_weights(w)
    x1, s_mix0 = _even_block(xs_, W, w, "l0")
    W = {**W, **exchange.weights(1, x1)}
    x2, s_att0 = _attention_block(x1, mems_, W, w, 0, "l0")
    W = {**W, **exchange.weights(2, x2)}
    x3, s_ffn0 = _ffn_block(x2, W, w, 0, "l0")
    W = {**W, **exchange.weights(3, x3)}
    x4, s_mix1 = _odd_block(x3, W, w, "l1")
    x5, s_att1 = _attention_block(x4, mems_, W, w, 1, "l1")
    x6, s_ffn1 = _ffn_block(x5, W, w, 1, "l1")
    dx, dfinal, loss_lanes = _loss_head(x6, w["final_norm"], tgt, name="loss_head")

    grads = {n: [None, None] for n in ("ca_norm", "ca_mem_norm", "ffn_norm")}
    grads["final_norm"] = dfinal[0]
    dx, G = _ffn_block_bwd(dx, s_ffn1, W, w, 1, "l1", G, grads)
    dx, G = _attention_block_bwd(dx, s_att1, mems_, W, w, 1, "l1", G, grads)
    dx, G = _odd_block_bwd(dx, s_mix1, W, w, "l1", G, grads)
    token = exchange.grads_ready("l1", G)
    dx, G = _ffn_block_bwd(dx, s_ffn0, W, w, 0, "l0", G, grads, token,
                           lambda after: exchange.grads_crossed("l1", after))
    token = exchange.grads_ready("ffn0", G)
    dx, G = _attention_block_bwd(dx, s_att0, mems_, W, w, 0, "l0", G, grads, token,
                                 lambda after: exchange.grads_crossed("ffn0", after))
    token = exchange.grads_ready("att0", G)
    dx, G = _even_block_bwd(dx, s_mix0, W, w, "l0", G, grads, token,
                            lambda after: exchange.grads_crossed("att0", after))
    for n in list(grads):
        if isinstance(grads[n], list):
            grads[n] = jnp.stack(grads[n], axis=0)
        grads[n] = grads[n].reshape(w[n].shape)
    return loss_lanes, dx, G, grads


class _Exchange:
    def __init__(self, local, chip, core):
        self.bufs = {s: lax.dynamic_update_slice(lax.empty((N_CHIPS, rows, width), BF16),
                                                 _local_slab(local, s, BF16)[None], (chip, 0, 0))
                     for s, (width, rows) in _SLABS.items()}
        self.where = jnp.stack([chip, core]).astype(jnp.int32)
        self.flights = []
        self.reduces = {}

    def weights(self, stage, after):
        send_sems, recv_sems, bufs, _ = self.flights[stage]
        bufs = _gather_ici_wait(send_sems, recv_sems, bufs, after, name=f"gather_stage{stage}_wait")
        return dict(zip(_STAGES[stage], _gather_forward(bufs, name=f"gather_stage{stage}_forward")))

    def first_weights(self, w):
        after = w["e_conv_w"].reshape(-1)[:STATE_LANES]
        for k, stage in enumerate(_STAGES):
            self.flights.append(_gather_ici_start([self.bufs[s] for s in stage], after, name=f"gather_stage{k}_start"))
            after = self.flights[-1][3]
        return self.weights(0, after), {**w, "e_norm": _behind(w["e_norm"], after)}

    def pair_start(self, G, slabs, tag):
        send_sems, recv_sems, gl, lands, token = _pair_exchange_start([G[s] for s in slabs],
                                                                      name=f"grad_{tag}_pair_start")
        return (slabs, send_sems, recv_sems, gl, lands), token

    def pair_land(self, state, after, tag):
        slabs, send_sems, recv_sems, gl, lands = state
        gl, other = _pair_exchange_wait(send_sems, recv_sems, gl, lands, after, name=f"grad_{tag}_pair_wait")
        pairs = [_pair_sum(g, r, self.where, name=f"grad_pair_sum_{s}") for s, g, r in zip(slabs, gl, other)]
        send_sems, recv_sems, pairs, lands, token = _chip_exchange_start(pairs, name=f"grad_{tag}_chip_start")
        return (slabs, gl, other, send_sems, recv_sems, pairs, lands), token

    def reduce_finish(self, state, after, tag):
        slabs, gl, other, send_sems, recv_sems, pairs, lands = state
        slots = _chip_exchange_wait(send_sems, recv_sems, pairs, lands, after, name=f"grad_{tag}_chip_wait")
        halves = [_chip_sum(g, r, sl, self.where, name=f"grad_chip_sum_{s}")
                  for s, g, r, sl in zip(slabs, gl, other, slots)]
        return dict(zip(slabs, _pair_share(halves, name=f"grad_{tag}_pair_share")))

    def grads_ready(self, piece, G):
        self.reduces[piece], token = self.pair_start(G, _GRAD_PIECES[piece], piece)
        return token

    def grads_crossed(self, piece, after):
        self.reduces[piece], token = self.pair_land(self.reduces[piece], after, piece)
        return token


def kernel(x, mem, e_norm, e_w_in, e_gmlp_w, e_gmlp_b, e_conv_w, e_conv_b, e_conv_ln_g, e_conv_ln_b, e_w_out, o_norm, o_w_in, o_lam_re, o_lam_im, o_log_dt, o_b_re, o_b_im, o_c_re, o_c_im, o_d, o_w_out, ca_norm, ca_mem_norm, ca_wq, ca_wk, ca_wv, ca_wo, ffn_norm, ffn_w_gate, ffn_w_up, ffn_w_down, final_norm, loss_target, m_e_norm, m_e_w_in, m_e_gmlp_w, m_e_gmlp_b, m_e_conv_w, m_e_conv_b, m_e_conv_ln_g, m_e_conv_ln_b, m_e_w_out, m_o_norm, m_o_w_in, m_o_lam_re, m_o_lam_im, m_o_log_dt, m_o_b_re, m_o_b_im, m_o_c_re, m_o_c_im, m_o_d, m_o_w_out, m_ca_norm, m_ca_mem_norm, m_ca_wq, m_ca_wk, m_ca_wv, m_ca_wo, m_ffn_norm, m_ffn_w_gate, m_ffn_w_up, m_ffn_w_down, m_final_norm, v_e_norm, v_e_w_in, v_e_gmlp_w, v_e_gmlp_b, v_e_conv_w, v_e_conv_b, v_e_conv_ln_g, v_e_conv_ln_b, v_e_w_out, v_o_norm, v_o_w_in, v_o_lam_re, v_o_lam_im, v_o_log_dt, v_o_b_re, v_o_b_im, v_o_c_re, v_o_c_im, v_o_d, v_o_w_out, v_ca_norm, v_ca_mem_norm, v_ca_wq, v_ca_wk, v_ca_wv, v_ca_wo, v_ffn_norm, v_ffn_w_gate, v_ffn_w_up, v_ffn_w_down, v_final_norm):
    args = dict(locals())
    local = {n: args[n] for n in _WEIGHTS}
    mom = {n: args["m_" + n] for n in _WEIGHTS}
    vel = {n: args["v_" + n] for n in _WEIGHTS}
    chip = 2 * lax.axis_index("x") + lax.axis_index("y")
    core = lax.axis_index("c")
    xs_, mems_, tgt = x[0], mem[0], loss_target[0]

    w = {n: local[n] for n in _REPLICATED}
    sm_slab, sm_spans = _pack_rows([local[n] for n, _ in _SMALL_SHARDED], SMALL_W, F32)
    sm_all = _allgather_small(sm_slab, name="gather_small_weights").reshape(N_DEV, -1, SMALL_W)
    for (n, ax), span in zip(_SMALL_SHARDED, sm_spans):
        shp = local[n].shape
        w[n] = jnp.concatenate([_unpack_rows(sm_all[2 * p], [span], [shp])[0] for p in range(N_CHIPS)], axis=ax)

    exchange = _Exchange(local, chip, core)
    G = {s: lax.empty((N_CHIPS, rows, width), F32) for s, (width, rows) in _SLABS.items()}
    loss_lanes, dx, G, grads = _forward_backward(xs_, mems_, tgt, w, G, exchange)

    gs_slab, gs_spans = _pack_rows([grads[n] for n in _SMALL] + [loss_lanes], SMALL_W, F32)
    small_flight = _all_to_all_start(gs_slab, dx, name="small_grads_start")
    exchange.grads_ready("even0", G)
    gsum = exchange.reduce_finish(exchange.reduces["l1"], small_flight[4], "l1")
    gsum = {**gsum, **exchange.reduce_finish(exchange.reduces["ffn0"], small_flight[4], "ffn0")}
    gsum = {**gsum, **exchange.reduce_finish(exchange.reduces["att0"], small_flight[4], "att0")}
    token = exchange.grads_crossed("even0", gsum["A0"])

    gs_slab, gs_all = _all_to_all_wait(*small_flight[:4], token, name="small_grads_wait")
    gs_all = lax.dynamic_update_slice(gs_all, gs_slab[None], (2 * chip + core, 0, 0))
    gs_sum = _sum_slots(gs_all, name="small_grad_sum")
    *small_sums, loss_sum = _unpack_rows(gs_sum, gs_spans, [grads[n].shape for n in _SMALL] + [loss_lanes.shape])
    out_grads = dict(zip(_SMALL, small_sums))
    for n, ax in _SMALL_SHARDED:
        width = local[n].shape[ax]
        out_grads[n] = lax.dynamic_slice_in_dim(out_grads[n], chip * width, width, axis=ax)

    delta, new_m, new_v = {}, {}, {}
    d_, m_, v_ = _adamw_small([_two_d(local[n]) for n in _SMALL], [_two_d(out_grads[n]) for n in _SMALL],
                              [_two_d(mom[n]) for n in _SMALL], [_two_d(vel[n]) for n in _SMALL], name="adamw_small")
    for n, dd, mm_, vv in zip(_SMALL, d_, m_, v_):
        shp = local[n].shape
        delta[n], new_m[n], new_v[n] = dd.reshape(shp), mm_.reshape(shp), vv.reshape(shp)
    def adamw_large(names):
        for n in names:
            shp = local[n].shape
            g_, d_, m_, v_ = _adamw_shard(_shard_rows(n, local[n]), [(gsum[s], r0) for s, r0 in _PLACE[n][1]],
                                          _shard_rows(n, mom[n]), _shard_rows(n, vel[n]), name=f"adamw_{n}")
            out_grads[n], delta[n], new_m[n], new_v[n] = (_from_shard_rows(n, t, shp) for t in (g_, d_, m_, v_))

    ready = [n for n, (_, where) in _PLACE.items() if all(s in gsum for s, _ in where)]
    adamw_large(ready)
    done = jnp.concatenate([delta[n].reshape(-1)[:1] for n in ready + list(_SMALL[:1])])
    gsum = {**gsum, **exchange.reduce_finish(exchange.reduces["even0"], done, "even0")}
    adamw_large([n for n in _PLACE if n not in ready])

    return (loss_sum[0, 0], dx[None], *[out_grads[n] for n in _WEIGHTS], *[delta[n] for n in _WEIGHTS],
            *[new_m[n] for n in _WEIGHTS], *[new_v[n] for n in _WEIGHTS])
```

```python
import functools
import math

import jax
import jax.numpy as jnp
from jax import lax
from jax.experimental import pallas as pl
from jax.experimental.pallas import tpu as pltpu

F32 = jnp.float32
BF16 = jnp.bfloat16
MESH = pl.DeviceIdType.MESH

EPS = 1e-6
D_MODEL = 1024
A_WIDTH = 512
A_GROUPS = 4
GMLP_BLOCK = 128
CHUNK = 64
B_WIDTH = 512
CONV_WIDTH = 31
CONV_HALO = 32
C_WIDTH = 512
C_GROUP_CH = 16
C_GROUPS = 32
C_STATE = 64
N_STATE = C_GROUPS * C_STATE
STATE_LANES = 128
STATE_ROWS = N_STATE // STATE_LANES
SCAN_BLOCK = 8
CA_HEADS = 4
CA_HEAD_DIM = 256
FFN_HIDDEN = 2816

ADAM_LR = 0.001
ADAM_B1 = 0.9
ADAM_B2 = 0.999
ADAM_EPS = 1e-08
ADAM_WD = 0.01
ADAM_STEP = 10

VMEM_LIMIT = 56 * 1024 * 1024
ACC_BYTES = 6 * 1024 * 1024
TN_VMEM_BYTES = 44 * 1024 * 1024
SMALL_W = 128
N_CHIPS = 4
N_DEV = 8

_SLABS = {"D0": (512, 1024), "E0": (1024, 256), "A0": (1024, 1024), "B0": (1024, 704), "C0": (1024, 1408),
          "D1": (512, 768), "A1": (1024, 1024), "B1": (1024, 704), "C1": (1024, 1408)}
_STAGES = (("D0", "E0"), ("A0",), ("B0", "C0"), ("D1", "A1", "B1", "C1"))
_GRAD_PIECES = {"l1": _STAGES[3], "ffn0": _STAGES[2], "rest0": _STAGES[0] + _STAGES[1]}
_PLACE = {
    "e_w_in": (1024, (("D0", 0),)), "e_w_out": (256, (("E0", 0),)),
    "o_w_out": (512, (("D1", 0),)), "o_w_in": (256, (("D1", 512),)),
    "ca_wq": (256, (("A0", 0), ("A1", 0))), "ca_wk": (256, (("A0", 256), ("A1", 256))),
    "ca_wv": (256, (("A0", 512), ("A1", 512))), "ca_wo": (256, (("A0", 768), ("A1", 768))),
    "ffn_w_down": (704, (("B0", 0), ("B1", 0))),
    "ffn_w_gate": (704, (("C0", 0), ("C1", 0))), "ffn_w_up": (704, (("C0", 704), ("C1", 704))),
}
_SMALL_SLAB = "F0"
_SMALL_SLAB_ROWS = 48
_SMALL_PLACE = {"e_conv_w": (0, 31), "o_norm": (32, 2), "o_d": (34, 1)}
_TRANSPOSED = ("ffn_w_gate", "ffn_w_up")


def _params(sem=None):
    return pltpu.CompilerParams(dimension_semantics=sem, vmem_limit_bytes=VMEM_LIMIT)


def _tile(n, pref, mult=128):
    if n <= pref:
        return n
    t = (pref // mult) * mult
    while t >= mult:
        if n % t == 0:
            return t
        t -= mult
    return n


def _blk(name, layer=0):
    rows, where = _PLACE[name]
    slab, r0 = where[layer]
    assert r0 % rows == 0
    return slab, rows, r0 // rows


def _shards(slabs, name, layer=0):
    slab, rows, b = _blk(name, layer)
    return [(slabs[slab], (None, rows, _SLABS[slab][0]), (p, b, 0)) for p in range(N_CHIPS)]


_GELU_C = 0.7978845608028654
_GELU_A = 0.044715


def _gelu(x):
    t = jnp.tanh(_GELU_C * (x + _GELU_A * (x * x * x)))
    return 0.5 * x * (1.0 + t), t


def _gelu_grad(x, t):
    return 0.5 * (1.0 + t) + 0.5 * x * (1.0 - t * t) * (_GELU_C * (1.0 + 3.0 * _GELU_A * x * x))


def _sigmoid(x):
    return 1.0 / (1.0 + jnp.exp(-x))


def _mean(x):
    return jnp.mean(x, axis=-1, keepdims=True)


def _dot(a, b):
    return jnp.dot(a, b, preferred_element_type=F32)


def _dot_nt(a, b):
    return lax.dot_general(a, b, (((1,), (1,)), ((), ())), preferred_element_type=F32)


def _dot_tn(a, b):
    return lax.dot_general(a, b, (((0,), (0,)), ((), ())), preferred_element_type=F32)


def _rms_tile(xv, gv):
    return (xv * lax.rsqrt(_mean(xv * xv) + EPS)) * gv


def _rms_bwd_tile(xv, gv, dyv):
    r = lax.rsqrt(_mean(xv * xv) + EPS)
    xh = xv * r
    dyg = dyv * gv
    return r * (dyg - xh * _mean(dyg * xh)), jnp.sum(dyv * xh, axis=0, keepdims=True)


def _cols(p, width):
    return slice(p * width, (p + 1) * width)


def _sum_k(a, ws, k):
    tot = None
    for p in range(N_CHIPS):
        y = _dot(a[:, _cols(p, k)], ws[p][...])
        tot = y if tot is None else tot + y
    return tot


def _cat_nt(a, ws):
    return jnp.concatenate([_dot_nt(a, ws[p][...]) for p in range(N_CHIPS)], axis=1)


def _rows_call(name, tm, rows, fulls, outs, accs, body, scratch=()):
    S = min(x.shape[-2] for x in rows if x.ndim != 4)
    nr, nf, no, na = len(rows), len(fulls), len(outs), len(accs)

    def kern(*refs):
        r, f = refs[:nr], refs[nr:nr + nf]
        o, a = refs[nr + nf:nr + nf + no], refs[nr + nf + no:nr + nf + no + na]
        if na:
            @pl.when(pl.program_id(0) == 0)
            def _():
                for ref in a:
                    ref[...] = jnp.zeros_like(ref)
        body(r, f, o, a, refs[nr + nf + no + na:])

    def whole(shape):
        nd = len(shape)
        return pl.BlockSpec(tuple(shape), lambda i: (0,) * nd)

    def row_spec(shape):
        if len(shape) == 4:
            return pl.BlockSpec((tm // 8,) + tuple(shape[1:]), lambda i: (i, 0, 0, 0))
        if len(shape) == 3:
            return pl.BlockSpec((shape[0], tm, shape[2]), lambda i: (0, i, 0))
        return pl.BlockSpec((tm, shape[1]), lambda i: (i, 0))

    def full_spec(x):
        if isinstance(x, tuple):
            _, bshape, bidx = x
            return pl.BlockSpec(bshape, lambda i: bidx, pipeline_mode=pl.Buffered(1))
        return whole(x.shape)

    out_shapes = [(S, o[0]) if len(o) == 2 else (o[0], S, o[1]) for o in outs]
    res = pl.pallas_call(
        kern, name=name, grid=(S // tm,),
        in_specs=[row_spec(x.shape) for x in rows] + [full_spec(x) for x in fulls],
        out_specs=[row_spec(s) for s in out_shapes] + [whole(shp) for shp, _ in accs],
        out_shape=[jax.ShapeDtypeStruct(s, o[-1]) for s, o in zip(out_shapes, outs)]
        + [jax.ShapeDtypeStruct(tuple(shp), dt) for shp, dt in accs],
        scratch_shapes=list(scratch),
        compiler_params=_params(("arbitrary",) if na else ("parallel",)),
    )(*rows, *[x[0] if isinstance(x, tuple) else x for x in fulls])
    return res[:no], res[no:]


def _grad_to_slab(gslabs, wname, layer, a, b, *, a_cols=None, b_cols=None, chips=(0, N_CHIPS), name):
    slab, rows, bidx = _blk(wname, layer)
    width = _SLABS[slab][0]
    p0, n_p = chips
    assert p0 % n_p == 0
    S = a.shape[-2]

    def tile_bytes(x, ts):
        return ts * x.dtype.itemsize * (x.shape[2] * n_p if x.ndim == 3 else x.shape[1])

    acc_bytes = n_p * rows * (-(-width // 128) * 128) * 4
    ts = next(t for t in (2048, 1024, 512, 256, S) if S % t == 0
              and 2 * (tile_bytes(a, t) + tile_bytes(b, t) + acc_bytes) <= TN_VMEM_BYTES or t == S)

    def operand(x):
        if x.ndim == 3:
            return pl.BlockSpec((n_p, ts, x.shape[2]), lambda s: (p0 // n_p, s, 0))
        return pl.BlockSpec((ts, x.shape[1]), lambda s: (s, 0))

    def part(ref, cols, p):
        if len(ref.shape) == 3:
            return ref[p]
        return ref[...] if cols is None else ref[:, _cols(p, cols)]

    def body(a_ref, b_ref, slab_ref, o_ref):
        @pl.when(pl.program_id(0) == 0)
        def _():
            o_ref[...] = jnp.zeros_like(o_ref)

        for p in range(n_p):
            o_ref[p] += _dot_tn(part(a_ref, a_cols, p).astype(BF16), part(b_ref, b_cols, p).astype(BF16))

    g = gslabs[slab]
    out = pl.pallas_call(
        body, name=name, grid=(S // ts,),
        in_specs=[operand(a), operand(b), pl.BlockSpec(memory_space=pl.ANY)],
        out_specs=pl.BlockSpec((n_p, rows, width), lambda s: (p0 // n_p, bidx, 0)),
        out_shape=jax.ShapeDtypeStruct(g.shape, F32), input_output_aliases={2: 0},
        compiler_params=_params(("arbitrary",)),
    )(a, b, g)
    return {**gslabs, slab: out}


def _vec(g):
    return g.reshape(1, -1)


def _norm_mm(x, g, ws, *, split, out_dtype, name, tm=512):
    S, D = x.shape
    k, n = ws[0][1][1], ws[0][1][2]
    N = n if split == "k" else N_CHIPS * n

    def body(r, f, o, acc, s):
        xn = _rms_tile(r[0][...], f[0][...]).astype(BF16)
        o[0][...] = xn
        if split == "k":
            o[1][...] = _sum_k(xn, f[1:], k).astype(out_dtype)
        else:
            for p in range(N_CHIPS):
                o[1][:, _cols(p, n)] = _dot(xn, f[1 + p][...]).astype(out_dtype)

    (xn, y), _ = _rows_call(name, _tile(S, tm), [x], [_vec(g)] + ws, [(D, BF16), (N, out_dtype)], [], body)
    return xn, y


def _mm_k(a, ws, *, add=None, out_dtype=F32, name, tm=512):
    S = a.shape[-2]
    k, n = ws[0][1][1], ws[0][1][2]
    has_add = add is not None

    def body(r, f, o, acc, s):
        if a.ndim == 3:
            y = None
            for p in range(N_CHIPS):
                t = _dot(r[0][p].astype(BF16), f[p][...])
                y = t if y is None else y + t
        else:
            y = _sum_k(r[0][...].astype(BF16), f, k)
        if has_add:
            y = y + r[1][...]
        o[0][...] = y.astype(out_dtype)

    (y,), _ = _rows_call(name, _tile(S, tm), [a] + ([add] if has_add else []), ws, [(n, out_dtype)], [], body)
    return y


def _mm_k_t(terms, *, out_dtype=F32, name, tm=512):
    S = terms[0][0].shape[0]
    k = terms[0][1][0][1][1]

    def body(r, f, o, acc, s):
        y = None
        for t in range(len(terms)):
            yt = _cat_nt(r[t][...].astype(BF16), f[N_CHIPS * t:N_CHIPS * (t + 1)])
            y = yt if y is None else y + yt
        o[0][...] = y.astype(out_dtype)

    (y,), _ = _rows_call(name, _tile(S, tm), [a for a, _ in terms], [w for _, ws in terms for w in ws],
                         [(N_CHIPS * k, out_dtype)], [], body)
    return y


def _rms_fwd(x, g, *, name):
    def body(r, f, o, acc, s):
        o[0][...] = _rms_tile(r[0][...], f[0][...]).astype(BF16)

    (y,), _ = _rows_call(name, _tile(x.shape[0], 256, 8), [x], [_vec(g)], [(x.shape[1], BF16)], [], body)
    return y


def _rms_dg(x, g, dy, *, name):
    def body(r, f, o, acc, s):
        acc[0][...] += _rms_bwd_tile(r[0][...], f[0][...], r[1][...])[1]

    _, (dg,) = _rows_call(name, _tile(x.shape[0], 256, 8), [x, dy], [_vec(g)], [], [((1, x.shape[1]), F32)], body)
    return dg


def _ffn_up(x, g, wg, wu, *, name, tm=512):
    S, D = x.shape
    h = wg[0][1][1]

    def body(r, f, o, acc, s):
        xn = _rms_tile(r[0][...], f[0][...]).astype(BF16)
        o[0][...] = xn
        for p in range(N_CHIPS):
            gate = _dot_nt(xn, f[1 + p][...])
            up = _dot_nt(xn, f[1 + N_CHIPS + p][...])
            o[1][p] = gate.astype(BF16)
            o[2][p] = up.astype(BF16)
            o[3][p] = (gate * _sigmoid(gate) * up).astype(BF16)

    (xn, gate, up, hid), _ = _rows_call(name, _tile(S, tm), [x], [_vec(g)] + wg + wu,
                                        [(D, BF16), (N_CHIPS, h, BF16), (N_CHIPS, h, BF16), (N_CHIPS, h, BF16)], [],
                                        body)
    return xn, gate, up, hid


def _ffn_bwd_hidden(dy, wd, gate, up, token=None, *, name, tm=512):
    S = dy.shape[0]
    h = wd[0][1][1]

    def body(r, f, o, acc, s):
        dyv = r[0][...]
        if token is not None:
            dyv = dyv + jnp.sum(f[N_CHIPS][...])
        dyb = dyv.astype(BF16)
        for p in range(N_CHIPS):
            dh = _dot_nt(dyb, f[p][...])
            gv = r[1][p].astype(F32)
            sg = _sigmoid(gv)
            o[0][p] = (dh * r[2][p].astype(F32) * (sg * (1.0 + gv * (1.0 - sg)))).astype(BF16)
            o[1][p] = (dh * gv * sg).astype(BF16)

    (dg, du), _ = _rows_call(name, _tile(S, tm), [dy, gate, up], wd + ([] if token is None else [token]),
                             [(N_CHIPS, h, BF16), (N_CHIPS, h, BF16)], [], body)
    return dg, du


def _ffn_in_bwd(dg, du, wg, wu, x, g, dres, *, name, tm=512):
    S, D = x.shape

    def body(r, f, o, acc, s):
        tot = None
        for p in range(N_CHIPS):
            y = _dot(r[0][p], f[1 + p][...]) + _dot(r[1][p], f[1 + N_CHIPS + p][...])
            tot = y if tot is None else tot + y
        dx, dgn = _rms_bwd_tile(r[2][...], f[0][...], tot)
        o[0][...] = dx + r[3][...]
        acc[0][...] += dgn

    (dx,), (dgn,) = _rows_call(name, _tile(S, tm), [dg, du, x, dres], [_vec(g)] + wg + wu, [(D, F32)],
                               [((1, D), F32)], body)
    return dx, dgn


def _norm_bwd_k(da, ws, x, g, dres, *, name, tm=512):
    S, D = x.shape

    def body(r, f, o, acc, s):
        dx, dg = _rms_bwd_tile(r[1][...], f[0][...], _cat_nt(r[0][...].astype(BF16), f[1:]))
        o[0][...] = dx + r[2][...]
        acc[0][...] += dg

    (dx,), (dg,) = _rows_call(name, _tile(S, tm), [da, x, dres], [_vec(g)] + ws, [(D, F32)], [((1, D), F32)], body)
    return dx, dg


def _norm_bwd_n(das, ws, x, g, dres, *, name, tm=256):
    S, D = x.shape
    n = ws[0][1][2]

    def body(r, f, o, acc, s):
        tot = None
        for p in range(N_CHIPS):
            y = _dot_nt(r[p // 2][:, _cols(p % 2, n)], f[1 + p][...])
            tot = y if tot is None else tot + y
        dx, dg = _rms_bwd_tile(r[2][...], f[0][...], tot)
        o[0][...] = dx + r[3][...]
        acc[0][...] += dg

    (dx,), (dg,) = _rows_call(name, _tile(S, tm), list(das) + [x, dres], [_vec(g)] + ws, [(D, F32)], [((1, D), F32)],
                              body)
    return dx, dg


def _ln_stats(v):
    mu = _mean(v)
    xc = v - mu
    rstd = lax.rsqrt(_mean(xc * xc) + EPS)
    return xc * rstd, rstd


_SHIFTS = 8
_CONV_ROWS = 64


def _fill_shifts(sh_ref, ext_ref, tm):
    sh_ref[0] = ext_ref[...]
    for s in range(1, _SHIFTS):
        sh_ref[s, 0:tm + CONV_HALO - _SHIFTS, :] = ext_ref[pl.ds(s, tm + CONV_HALO - _SHIFTS), :]


def _window(sh_ref, off, tm):
    return sh_ref[off % _SHIFTS, pl.ds(off - off % _SHIFTS, tm), :]


def _even_fwd(proj, wm, bcol, cw, cb, lg, lb, *, name):
    S = proj.shape[0]
    tm = _tile(S, 256)
    hb = tm // CONV_HALO
    nblk = tm // GMLP_BLOCK

    def body(p_ref, halo_ref, wm_ref, b_ref, cw_ref, cb_ref, lg_ref, lb_ref, mix_ref, hc_ref, hext_ref, hsh_ref):
        i = pl.program_id(0)
        gu, _ = _gelu(p_ref[:, 0:A_WIDTH])
        gv, _ = _gelu(p_ref[:, A_WIDTH:2 * A_WIDTH])
        vn, _ = _ln_stats(gv)
        vnb = vn.astype(BF16)
        for n in range(nblk):
            rows = slice(n * GMLP_BLOCK, (n + 1) * GMLP_BLOCK)
            for g in range(A_GROUPS):
                cols = slice(g * GMLP_BLOCK, (g + 1) * GMLP_BLOCK)
                sg = jnp.dot(wm_ref[g], vnb[rows, cols], preferred_element_type=F32) + b_ref[g]
                mix_ref[rows, cols] = (gu[rows, cols] * sg).astype(BF16)
        h = p_ref[:, 1024:1536] * _sigmoid(p_ref[:, 1536:2048])
        hh = halo_ref[:, 0:B_WIDTH] * _sigmoid(halo_ref[:, B_WIDTH:2 * B_WIDTH])
        hext_ref[0:CONV_HALO, :] = jnp.where(i > 0, hh, 0.0)
        hext_ref[CONV_HALO:CONV_HALO + tm, :] = h
        _fill_shifts(hsh_ref, hext_ref, tm)
        for r0 in range(0, tm, _CONV_ROWS):
            acc = jnp.zeros((_CONV_ROWS, B_WIDTH), F32)
            for k in range(CONV_WIDTH):
                acc = acc + cw_ref[k:k + 1, :] * _window(hsh_ref, r0 + k + CONV_HALO - CONV_WIDTH + 1, _CONV_ROWS)
            hc_ref[r0:r0 + _CONV_ROWS, :] = acc + cb_ref[...]
        hc = hc_ref[...]
        hhat, _ = _ln_stats(hc)
        hl = hhat * lg_ref[...] + lb_ref[...]
        mix_ref[:, A_WIDTH:A_WIDTH + B_WIDTH] = (hl * _sigmoid(hl)).astype(BF16)

    vec = pl.BlockSpec((1, B_WIDTH), lambda i: (0, 0))
    return pl.pallas_call(
        body, name=name, grid=(S // tm,),
        in_specs=[
            pl.BlockSpec((tm, 2048), lambda i: (i, 0)),
            pl.BlockSpec((CONV_HALO, 1024), lambda i: (jnp.maximum(i * hb - 1, 0), 1)),
            pl.BlockSpec((A_GROUPS, GMLP_BLOCK, GMLP_BLOCK), lambda i: (0, 0, 0)),
            pl.BlockSpec((A_GROUPS, GMLP_BLOCK, 1), lambda i: (0, 0, 0)),
            pl.BlockSpec((CONV_HALO, B_WIDTH), lambda i: (0, 0)),
            vec, vec, vec,
        ],
        out_specs=[pl.BlockSpec((tm, 1024), lambda i: (i, 0)), pl.BlockSpec((tm, B_WIDTH), lambda i: (i, 0))],
        out_shape=[jax.ShapeDtypeStruct((S, 1024), BF16), jax.ShapeDtypeStruct((S, B_WIDTH), F32)],
        scratch_shapes=[pltpu.VMEM((tm + CONV_HALO, B_WIDTH), F32),
                        pltpu.VMEM((_SHIFTS, tm + CONV_HALO, B_WIDTH), F32)],
        compiler_params=_params(("parallel",)),
    )(proj, proj, wm, bcol, cw, cb, lg, lb)


def _even_bwd1(proj, dmix, hc, wm, wmt, bcol, lg, lb, *, name):
    S = proj.shape[0]
    tm = _tile(S, 256)
    nblk = tm // GMLP_BLOCK

    def body(p_ref, dm_ref, hc_ref, wm_ref, wmt_ref, b_ref, lg_ref, lb_ref,
             dpa_ref, dhc_ref, dwm_ref, db_ref, dlg_ref, dlb_ref, dcb_ref, dgu_ref, dvn_ref):
        @pl.when(pl.program_id(0) == 0)
        def _():
            dwm_ref[...] = jnp.zeros_like(dwm_ref)
            db_ref[...] = jnp.zeros_like(db_ref)
            dlg_ref[...] = jnp.zeros_like(dlg_ref)
            dlb_ref[...] = jnp.zeros_like(dlb_ref)
            dcb_ref[...] = jnp.zeros_like(dcb_ref)

        au = p_ref[:, 0:A_WIDTH]
        av = p_ref[:, A_WIDTH:2 * A_WIDTH]
        gu, tu = _gelu(au)
        gv, tv = _gelu(av)
        vn, rstd = _ln_stats(gv)
        vnb = vn.astype(BF16)
        for n in range(nblk):
            rows = slice(n * GMLP_BLOCK, (n + 1) * GMLP_BLOCK)
            for g in range(A_GROUPS):
                cols = slice(g * GMLP_BLOCK, (g + 1) * GMLP_BLOCK)
                vb = vnb[rows, cols]
                sg = jnp.dot(wm_ref[g], vb, preferred_element_type=F32) + b_ref[g]
                da = dm_ref[rows, cols]
                dsg = da * gu[rows, cols]
                dgu_ref[rows, cols] = da * sg
                dsgb = dsg.astype(BF16)
                dwm_ref[g] += _dot_nt(dsgb, vb)
                db_ref[g] += jnp.sum(dsg, axis=1, keepdims=True)
                dvn_ref[rows, cols] = jnp.dot(wmt_ref[g], dsgb, preferred_element_type=F32)
        dvn = dvn_ref[...]
        dgv = rstd * (dvn - _mean(dvn) - vn * _mean(dvn * vn))
        dpa_ref[:, 0:A_WIDTH] = (dgu_ref[...] * _gelu_grad(au, tu)).astype(BF16)
        dpa_ref[:, A_WIDTH:2 * A_WIDTH] = (dgv * _gelu_grad(av, tv)).astype(BF16)
        hhat, rstd2 = _ln_stats(hc_ref[...])
        lgv = lg_ref[...]
        hl = hhat * lgv + lb_ref[...]
        s = _sigmoid(hl)
        dhl = dm_ref[:, A_WIDTH:A_WIDTH + B_WIDTH] * (s * (1.0 + hl * (1.0 - s)))
        dlg_ref[...] += jnp.sum(dhl * hhat, axis=0, keepdims=True)
        dlb_ref[...] += jnp.sum(dhl, axis=0, keepdims=True)
        dhh = dhl * lgv
        dhc = rstd2 * (dhh - _mean(dhh) - hhat * _mean(dhh * hhat))
        dcb_ref[...] += jnp.sum(dhc, axis=0, keepdims=True)
        dhc_ref[...] = dhc

    vec = pl.BlockSpec((1, B_WIDTH), lambda i: (0, 0))
    w3 = pl.BlockSpec((A_GROUPS, GMLP_BLOCK, GMLP_BLOCK), lambda i: (0, 0, 0))
    b3 = pl.BlockSpec((A_GROUPS, GMLP_BLOCK, 1), lambda i: (0, 0, 0))
    return pl.pallas_call(
        body, name=name, grid=(S // tm,),
        in_specs=[
            pl.BlockSpec((tm, 1024), lambda i: (i, 0)),
            pl.BlockSpec((tm, 1024), lambda i: (i, 0)),
            pl.BlockSpec((tm, B_WIDTH), lambda i: (i, 0)),
            w3, w3, b3, vec, vec,
        ],
        out_specs=[pl.BlockSpec((tm, 1024), lambda i: (i, 0)), pl.BlockSpec((tm, B_WIDTH), lambda i: (i, 0)),
                   w3, b3, vec, vec, vec],
        out_shape=[
            jax.ShapeDtypeStruct((S, 1024), BF16), jax.ShapeDtypeStruct((S, B_WIDTH), F32),
            jax.ShapeDtypeStruct((A_GROUPS, GMLP_BLOCK, GMLP_BLOCK), F32),
            jax.ShapeDtypeStruct((A_GROUPS, GMLP_BLOCK, 1), F32),
            jax.ShapeDtypeStruct((1, B_WIDTH), F32), jax.ShapeDtypeStruct((1, B_WIDTH), F32),
            jax.ShapeDtypeStruct((1, B_WIDTH), F32),
        ],
        scratch_shapes=[pltpu.VMEM((tm, A_WIDTH), F32), pltpu.VMEM((tm, A_WIDTH), F32)],
        compiler_params=_params(("arbitrary",)),
    )(proj, dmix, hc, wm, wmt, bcol, lg, lb)


def _even_bwd2(proj, dhc, cw, *, name):
    S = proj.shape[0]
    tm = _tile(S, 256)
    hb = tm // CONV_HALO
    nt = S // tm
    last_halo = S // CONV_HALO - 1
    lo = CONV_HALO - CONV_WIDTH + 1

    def body(p_ref, halo_ref, d_ref, dnext_ref, cw_ref, dpb_ref, dcw_ref, hext_ref, dext_ref, hsh_ref, dsh_ref):
        i = pl.program_id(0)

        @pl.when(i == 0)
        def _():
            dcw_ref[...] = jnp.zeros_like(dcw_ref)

        hh = halo_ref[:, 0:B_WIDTH] * _sigmoid(halo_ref[:, B_WIDTH:2 * B_WIDTH])
        hext_ref[0:CONV_HALO, :] = jnp.where(i > 0, hh, 0.0)
        hext_ref[CONV_HALO:CONV_HALO + tm, :] = p_ref[:, 0:B_WIDTH] * _sigmoid(p_ref[:, B_WIDTH:2 * B_WIDTH])
        dext_ref[0:tm, :] = d_ref[...]
        dext_ref[tm:tm + CONV_HALO, :] = jnp.where(i < nt - 1, dnext_ref[...], 0.0)
        _fill_shifts(hsh_ref, hext_ref, tm)
        _fill_shifts(dsh_ref, dext_ref, tm)
        for r0 in range(0, tm, _CONV_ROWS):
            rows = slice(r0, r0 + _CONV_ROWS)
            dhc_b = d_ref[rows, :]
            dh = jnp.zeros((_CONV_ROWS, B_WIDTH), F32)
            for k in range(CONV_WIDTH):
                dh = dh + cw_ref[k:k + 1, :] * _window(dsh_ref, r0 + CONV_WIDTH - 1 - k, _CONV_ROWS)
                dcw_ref[k:k + 1, :] += jnp.sum(dhc_b * _window(hsh_ref, r0 + k + lo, _CONV_ROWS), axis=0,
                                               keepdims=True)
            ba_b = p_ref[rows, 0:B_WIDTH]
            sg_b = _sigmoid(p_ref[rows, B_WIDTH:2 * B_WIDTH])
            dpb_ref[rows, 0:B_WIDTH] = (dh * sg_b).astype(BF16)
            dpb_ref[rows, B_WIDTH:2 * B_WIDTH] = (dh * ba_b * sg_b * (1.0 - sg_b)).astype(BF16)

    return pl.pallas_call(
        body, name=name, grid=(nt,),
        in_specs=[
            pl.BlockSpec((tm, 1024), lambda i: (i, 1)),
            pl.BlockSpec((CONV_HALO, 1024), lambda i: (jnp.maximum(i * hb - 1, 0), 1)),
            pl.BlockSpec((tm, B_WIDTH), lambda i: (i, 0)),
            pl.BlockSpec((CONV_HALO, B_WIDTH), lambda i: (jnp.minimum((i + 1) * hb, last_halo), 0)),
            pl.BlockSpec((CONV_HALO, B_WIDTH), lambda i: (0, 0)),
        ],
        out_specs=[pl.BlockSpec((tm, 1024), lambda i: (i, 0)), pl.BlockSpec((CONV_HALO, B_WIDTH), lambda i: (0, 0))],
        out_shape=[jax.ShapeDtypeStruct((S, 1024), BF16), jax.ShapeDtypeStruct((CONV_HALO, B_WIDTH), F32)],
        scratch_shapes=[pltpu.VMEM((tm + CONV_HALO, B_WIDTH), F32), pltpu.VMEM((tm + CONV_HALO, B_WIDTH), F32),
                        pltpu.VMEM((_SHIFTS, tm + CONV_HALO, B_WIDTH), F32),
                        pltpu.VMEM((_SHIFTS, tm + CONV_HALO, B_WIDTH), F32)],
        compiler_params=_params(("arbitrary",)),
    )(proj, proj, dhc, dhc, cw)


_CA_SCALE = CA_HEAD_DIM ** -0.5


def _softmax_rows(s):
    e = jnp.exp(s - jnp.max(s, axis=-1, keepdims=True))
    return e / jnp.sum(e, axis=-1, keepdims=True)


def _attn_fwd(q, k, v, *, name):
    S = q.shape[0]

    def body(r, f, o, acc, s):
        for h in range(CA_HEADS):
            cols = _cols(h, CA_HEAD_DIM)
            p = _softmax_rows(_dot_nt(r[0][:, cols], f[0][:, cols]) * _CA_SCALE)
            o[0][:, cols] = _dot(p.astype(BF16), f[1][:, cols]).astype(BF16)

    (o_,), _ = _rows_call(name, _tile(S, 512), [q], [k, v], [(D_MODEL, BF16)], [], body)
    return o_


def _attn_bwd(dy, wo, q, k, v, *, name):
    S = q.shape[0]
    M = k.shape[0]

    def body(r, f, o, acc, s):
        dyb = r[0][...].astype(BF16)
        for h in range(CA_HEADS):
            cols = _cols(h, CA_HEAD_DIM)
            qh = r[1][:, cols]
            kh = f[0][:, cols]
            vh = f[1][:, cols]
            doh = _dot_nt(dyb, f[2 + h][...]).astype(BF16)
            p = _softmax_rows(_dot_nt(qh, kh) * _CA_SCALE)
            acc[1][:, cols] += _dot_tn(p.astype(BF16), doh)
            dp = _dot_nt(doh, vh)
            ds = (p * (dp - jnp.sum(dp * p, axis=-1, keepdims=True)) * _CA_SCALE).astype(BF16)
            o[0][:, cols] = _dot(ds, kh).astype(BF16)
            acc[0][:, cols] += _dot_tn(ds, qh)

    (dq,), (dk, dv) = _rows_call(name, _tile(S, 512), [dy, q], [k, v] + wo, [(D_MODEL, BF16)],
                                 [((M, D_MODEL), F32), ((M, D_MODEL), F32)], body)
    return dq, dk, dv


_STATE_TILE = 2 * STATE_ROWS
N_SETS = 4
SET_CH = C_WIDTH // N_SETS
SET_COLS = N_STATE // N_SETS // STATE_LANES


def _set_groups(j):
    return [SET_COLS * j + c for c in range(SET_COLS)] + [STATE_ROWS + SET_COLS * j + c for c in range(SET_COLS)]


def _pack_state(re, im):
    hi = lax.bitcast_convert_type(re.astype(BF16).astype(F32), jnp.uint32)
    lo = lax.bitcast_convert_type(im.astype(BF16).astype(F32), jnp.uint32) >> 16
    return hi | lo


def _unpack_state(word):
    re = lax.bitcast_convert_type(word & jnp.uint32(0xFFFF0000), F32)
    im = lax.bitcast_convert_type(word << 16, F32)
    return re, im


def _state_set(ref, tm, j):
    parts = [_unpack_state(ref[:, SET_COLS * j + c, :, :].reshape(tm, STATE_LANES)) for c in range(SET_COLS)]
    return jnp.concatenate([p[0].astype(BF16) for p in parts] + [p[1].astype(BF16) for p in parts], axis=1)


def _s5_readout(xs, cset, u, d, *, name, tm=256):
    tm = _tile(u.shape[0], tm)

    def body(r, f, o, acc, s):
        y0 = jnp.concatenate([_dot(_state_set(r[0], tm, j), f[0][j]) for j in range(N_SETS)], axis=1)
        y = y0 + f[1][...] * r[1][...]
        o[0][...] = y
        o[1][...] = _gelu(y)[0].astype(BF16)

    (y, yg), _ = _rows_call(name, tm, [xs, u], [cset, d], [(C_WIDTH, F32), (C_WIDTH, BF16)], [], body)
    return y, yg


def _state_grad_sets(a, st, *, name, ts=256):
    ts = _tile(a.shape[0], ts)

    def body(r, f, o, acc, s):
        for j in range(N_SETS):
            acc[0][j] += _dot_tn(r[0][:, _cols(j, SET_CH)].astype(BF16), _state_set(r[1], ts, j))

    _, (out,) = _rows_call(name, ts, [a, st], [], [], [((N_SETS, SET_CH, 2 * N_STATE // N_SETS), F32)], body)
    return out


def _glu_out(yg, ws, x, *, name, tm=512):
    n = ws[0][1][2]

    def body(r, f, o, acc, s):
        ygv = r[0][...]
        ov = [_dot(ygv, f[p][...]) for p in range(N_CHIPS)]
        for p in range(N_CHIPS):
            o[0][:, _cols(p, n)] = ov[p].astype(BF16)
        for p in range(2):
            o[1][:, _cols(p, n)] = r[1][:, _cols(p, n)] + ov[p] * _sigmoid(ov[2 + p])

    (o_, y), _ = _rows_call(name, _tile(x.shape[0], tm), [yg, x], ws, [(2 * D_MODEL, BF16), (D_MODEL, F32)], [], body)
    return o_, y


def _glu_out_bwd(o_, dy, ws, y, u, d, *, name, tm=256):
    n = ws[0][1][2]

    def body(r, f, o, acc, s):
        o1 = r[0][:, 0:D_MODEL].astype(F32)
        sg = _sigmoid(r[0][:, D_MODEL:2 * D_MODEL].astype(F32))
        dyv = r[1][...]
        do1 = (dyv * sg).astype(BF16)
        do2 = (dyv * o1 * sg * (1.0 - sg)).astype(BF16)
        o[0][:, 0:D_MODEL] = do1
        o[0][:, D_MODEL:2 * D_MODEL] = do2
        dyg = None
        for p in range(N_CHIPS):
            t = _dot_nt((do1 if p < 2 else do2)[:, _cols(p % 2, n)], f[1 + p][...])
            dyg = t if dyg is None else dyg + t
        yv = r[2][...]
        dys = dyg * _gelu_grad(yv, _gelu(yv)[1])
        o[1][...] = dys.astype(BF16)
        o[2][...] = f[0][...] * dys
        acc[0][...] += jnp.sum(dys * r[3][...], axis=0, keepdims=True)

    (do, dys, dus), (dd,) = _rows_call(name, _tile(dy.shape[0], tm), [o_, dy, y, u], [d] + ws,
                                       [(2 * D_MODEL, BF16), (C_WIDTH, BF16), (C_WIDTH, F32)], [((1, C_WIDTH), F32)],
                                       body)
    return do, dys, dus, dd


def _s5_in_bwd(gs, bset, dus, ws, x, g, dres, *, name, tm=256):
    D = x.shape[1]
    tm = _tile(x.shape[0], tm)

    def body(r, f, o, acc, s):
        du0 = jnp.concatenate([_dot_nt(_state_set(r[0], tm, j), f[1][j]) for j in range(N_SETS)], axis=1)
        du = (du0 + r[1][...]).astype(BF16)
        o[0][...] = du
        dx, dg = _rms_bwd_tile(r[2][...], f[0][...], _cat_nt(du, f[2:]))
        o[1][...] = dx + r[3][...]
        acc[0][...] += dg

    (du, dx), (dg,) = _rows_call(name, tm, [gs, dus, x, dres], [_vec(g), bset] + ws,
                                 [(C_WIDTH, BF16), (D, F32)], [((1, D), F32)], body)
    return du, dx, dg


_SCAN_CHUNK = 256
_RE = slice(0, STATE_ROWS)
_IM = slice(STATE_ROWS, 2 * STATE_ROWS)
assert SCAN_BLOCK == 8


def _token(g, i, rows):
    return pl.ds(pl.multiple_of(g * (rows * SCAN_BLOCK), rows * SCAN_BLOCK) + i, rows, stride=SCAN_BLOCK)


def _fill_chunk(s3, a_ref, wset, tc, nt):
    for j in range(N_SETS):
        av = a_ref[:, _cols(j, SET_CH)].astype(BF16)
        y = _dot_nt(av, wset[j]) if nt else _dot(av, wset[j])
        for k, c in enumerate(_set_groups(j)):
            s3[:, 8 * c:8 * (c + 1), :] = y[:, _cols(k, STATE_LANES)].reshape(tc // 8, 8, STATE_LANES)


def _chunk_token(s3, g, i):
    return s3[g, pl.ds(i, _STATE_TILE, stride=SCAN_BLOCK), :]


def _scan_fwd(u, bset, pw, *, name):
    S = u.shape[0]
    tc = _tile(S, _SCAN_CHUNK, 8)

    def body(u_ref, bset_ref, pw_ref, xs_ref, st_ref, s3):
        @pl.when(pl.program_id(0) == 0)
        def _():
            st_ref[...] = jnp.zeros_like(st_ref)

        _fill_chunk(s3, u_ref, bset_ref, tc, nt=False)
        ar = pw_ref[0, _RE, :]
        ai = pw_ref[0, _IM, :]

        def block(g, carry):
            xr, xi = carry
            cr = ci = nr = ni = None
            for j in range(SCAN_BLOCK):
                b = _chunk_token(s3, g, j)
                br, bi = b[_RE], b[_IM]
                cr, ci = (br, bi) if j == 0 else (ar * cr - ai * ci + br, ar * ci + ai * cr + bi)
                pr, pi = pw_ref[j, _RE, :], pw_ref[j, _IM, :]
                nr = pr * xr - pi * xi + cr
                ni = pr * xi + pi * xr + ci
                xs_ref[_token(g, j, STATE_ROWS), :] = _pack_state(nr, ni)
            return nr, ni

        xr, xi = lax.fori_loop(0, tc // SCAN_BLOCK, block, (st_ref[_RE, :], st_ref[_IM, :]), unroll=4)
        st_ref[_RE, :] = xr
        st_ref[_IM, :] = xi

    return pl.pallas_call(
        body, name=name, grid=(S // tc,),
        in_specs=[pl.BlockSpec((tc, u.shape[1]), lambda i: (i, 0)), pl.BlockSpec(bset.shape, lambda i: (0, 0, 0)),
                  pl.BlockSpec(pw.shape, lambda i: (0, 0, 0))],
        out_specs=pl.BlockSpec((tc * STATE_ROWS, STATE_LANES), lambda i: (i, 0)),
        out_shape=jax.ShapeDtypeStruct((S * STATE_ROWS, STATE_LANES), jnp.uint32),
        scratch_shapes=[pltpu.VMEM((2 * STATE_ROWS, STATE_LANES), F32),
                        pltpu.VMEM((tc // 8, _STATE_TILE * 8, STATE_LANES), F32)],
        compiler_params=_params(("arbitrary",)),
    )(u, bset, pw)


def _scan_bwd(dys, cset, xs, pw, *, name):
    S = dys.shape[0]
    tc = _tile(S, _SCAN_CHUNK, 8)
    nc = S // tc

    def body(dys_ref, cset_ref, xs_ref, pw_ref, g_ref, da_ref, st_ref, s3):
        @pl.when(pl.program_id(0) == 0)
        def _():
            st_ref[...] = jnp.zeros_like(st_ref)
            da_ref[...] = jnp.zeros_like(da_ref)

        _fill_chunk(s3, dys_ref, cset_ref, tc, nt=True)
        ar = pw_ref[0, _RE, :]
        ai = pw_ref[0, _IM, :]

        def block(k, carry):
            gr, gi, dar, dai = carry
            g = tc // SCAN_BLOCK - 1 - k
            cr = ci = None
            pgr, pgi = gr, gi
            for j in range(SCAN_BLOCK):
                i = SCAN_BLOCK - 1 - j
                xr, xi = _unpack_state(xs_ref[_token(g, i, STATE_ROWS), :])
                dar = dar + pgr * xr + pgi * xi
                dai = dai + pgi * xr - pgr * xi
                d = _chunk_token(s3, g, i)
                dr, di = d[_RE], d[_IM]
                cr, ci = (dr, di) if j == 0 else (ar * cr + ai * ci + dr, ar * ci - ai * cr + di)
                pr, pi = pw_ref[j, _RE, :], pw_ref[j, _IM, :]
                pgr = pr * gr + pi * gi + cr
                pgi = pr * gi - pi * gr + ci
                g_ref[_token(g, i, STATE_ROWS), :] = _pack_state(pgr, pgi)
            return pgr, pgi, dar, dai

        init = (st_ref[_RE, :], st_ref[_IM, :], da_ref[_RE, :], da_ref[_IM, :])
        gr, gi, dar, dai = lax.fori_loop(0, tc // SCAN_BLOCK, block, init, unroll=4)
        st_ref[_RE, :] = gr
        st_ref[_IM, :] = gi
        da_ref[_RE, :] = dar
        da_ref[_IM, :] = dai

    packed = pl.BlockSpec((tc * STATE_ROWS, STATE_LANES), lambda i: (nc - 1 - i, 0))
    vec = pl.BlockSpec((2 * STATE_ROWS, STATE_LANES), lambda i: (0, 0))
    return pl.pallas_call(
        body, name=name, grid=(nc,),
        in_specs=[pl.BlockSpec((tc, dys.shape[1]), lambda i: (nc - 1 - i, 0)),
                  pl.BlockSpec(cset.shape, lambda i: (0, 0, 0)), packed, pl.BlockSpec(pw.shape, lambda i: (0, 0, 0))],
        out_specs=[packed, vec],
        out_shape=[jax.ShapeDtypeStruct(xs.shape, jnp.uint32), jax.ShapeDtypeStruct((2 * STATE_ROWS, STATE_LANES), F32)],
        scratch_shapes=[pltpu.VMEM((2 * STATE_ROWS, STATE_LANES), F32),
                        pltpu.VMEM((tc // 8, _STATE_TILE * 8, STATE_LANES), F32)],
        compiler_params=_params(("arbitrary",)),
    )(dys, cset, xs, pw)


def _down_loss_head(h, ws, x, g, target, *, name, tm=512):
    S, D = x.shape

    def body(r, f, o, acc, s):
        xv = r[1][...]
        for p in range(N_CHIPS):
            xv = xv + _dot(r[0][p], f[1 + p][...])
        gv = f[0][...]
        rs = lax.rsqrt(_mean(xv * xv) + EPS)
        xh = xv * rs
        err = xh * gv - r[2][...]
        acc[1][...] += 0.5 * jnp.sum(_mean(err * err), axis=0, keepdims=True)
        dy = err * (1.0 / D)
        dyg = dy * gv
        o[0][...] = rs * (dyg - xh * _mean(dyg * xh))
        acc[0][...] += jnp.sum(dy * xh, axis=0, keepdims=True)

    (dx,), (dg, loss) = _rows_call(name, _tile(S, tm), [h, x, target], [_vec(g)] + ws, [(D, F32)],
                                   [((1, D), F32), ((1, 128), F32)], body)
    return dx, dg, loss


_ADAM_C1 = 1.0 - ADAM_B1 ** ADAM_STEP
_ADAM_C2 = 1.0 - ADAM_B2 ** ADAM_STEP
_ONE_BLOCK_BYTES = 8 * 1024 * 1024


def _adamw_math(w, g, m, v):
    nm = ADAM_B1 * m + (1.0 - ADAM_B1) * g
    nv = ADAM_B2 * v + (1.0 - ADAM_B2) * (g * g)
    m_hat = nm / _ADAM_C1
    v_hat = nv / _ADAM_C2
    return -ADAM_LR * (m_hat / (jnp.sqrt(v_hat) + ADAM_EPS) + ADAM_WD * w), nm, nv


def _adamw_shard(w, gsrc, m, v, *, name):
    R, C = w.shape
    n_l = len(gsrc)
    rows = R // n_l
    tr = rows
    for _, r0 in gsrc:
        tr = math.gcd(tr, r0) if r0 else tr
    tr = _tile(tr, 256, 8) if tr > 256 else tr
    nb = rows // tr
    assert rows % tr == 0 and all(r0 % tr == 0 for _, r0 in gsrc)

    def body(*refs):
        w_ref, g_refs, (m_ref, v_ref, go_ref, d_ref, nm_ref, nv_ref) = refs[0], refs[1:1 + n_l], refs[1 + n_l:]
        layer = pl.program_id(0) // nb
        gv = g_refs[0][...]
        for l in range(1, n_l):
            gv = jnp.where(layer == l, g_refs[l][...], gv)
        go_ref[...] = gv
        d_ref[...], nm_ref[...], nv_ref[...] = _adamw_math(w_ref[...], gv, m_ref[...], v_ref[...])

    def g_spec(l, r0):
        return pl.BlockSpec((tr, C), lambda i: (r0 // tr + jnp.clip(i - l * nb, 0, nb - 1), 0))

    blk = pl.BlockSpec((tr, C), lambda i: (i, 0))
    out = jax.ShapeDtypeStruct((R, C), F32)
    return pl.pallas_call(
        body, name=name, grid=(R // tr,),
        in_specs=[blk] + [g_spec(l, r0) for l, (_, r0) in enumerate(gsrc)] + [blk, blk], out_specs=[blk] * 4,
        out_shape=[out] * 4, compiler_params=_params(("parallel",)),
    )(w, *[g for g, _ in gsrc], m, v)


def _adamw_small(ws, gs, ms, vs, *, name):
    n = len(ws)

    def body(*refs):
        w_r, g_r, m_r, v_r = refs[:n], refs[n:2 * n], refs[2 * n:3 * n], refs[3 * n:4 * n]
        d_r, nm_r, nv_r = refs[4 * n:5 * n], refs[5 * n:6 * n], refs[6 * n:7 * n]
        for k in range(n):
            d_r[k][...], nm_r[k][...], nv_r[k][...] = _adamw_math(w_r[k][...], g_r[k][...], m_r[k][...], v_r[k][...])

    vm = pl.BlockSpec(memory_space=pltpu.VMEM)
    out = [jax.ShapeDtypeStruct(w.shape, F32) for w in ws]
    res = pl.pallas_call(body, name=name, in_specs=[vm] * (4 * n), out_specs=[vm] * (3 * n), out_shape=out * 3,
                         compiler_params=pltpu.CompilerParams(vmem_limit_bytes=VMEM_LIMIT))(*ws, *gs, *ms, *vs)
    return res[:n], res[n:2 * n], res[2 * n:]


def _sum_slots(x, *, name):
    n, R, C = x.shape
    tr = R if (n + 1) * R * C * 4 <= _ONE_BLOCK_BYTES else _tile(R, 256, 8)

    def body(x_ref, o_ref):
        acc = x_ref[0]
        for k in range(1, n):
            acc = acc + x_ref[k]
        o_ref[...] = acc

    return pl.pallas_call(
        body, name=name, grid=(R // tr,),
        in_specs=[pl.BlockSpec((n, tr, C), lambda i: (0, i, 0))], out_specs=pl.BlockSpec((tr, C), lambda i: (i, 0)),
        out_shape=jax.ShapeDtypeStruct((R, C), F32), compiler_params=_params(("parallel",)),
    )(x)


def _pair_sum(g, r, where, *, name):
    n, R, C = g.shape
    Rh = R // 2
    tr = _tile(Rh, 256, 8)
    nb = Rh // tr

    def body(where_ref, g_ref, r_ref, o_ref):
        o_ref[...] = (g_ref[...] + r_ref[...]).astype(BF16)

    def slot(p, w):
        return p + jnp.where(p >= w[0], 1, 0)

    return pl.pallas_call(
        body, name=name,
        grid_spec=pltpu.PrefetchScalarGridSpec(
            num_scalar_prefetch=1, grid=(n - 1, nb),
            in_specs=[pl.BlockSpec((1, tr, C), lambda p, i, w: (slot(p, w), w[1] * nb + i, 0)),
                      pl.BlockSpec((1, tr, C), lambda p, i, w: (slot(p, w), i, 0))],
            out_specs=pl.BlockSpec((1, tr, C), lambda p, i, w: (slot(p, w), i, 0)),
        ),
        out_shape=jax.ShapeDtypeStruct((n, Rh, C), BF16), compiler_params=_params(("parallel", "parallel")),
    )(where, g, r)


def _chip_sum(g, r, slots, where, *, name):
    n, R, C = g.shape
    Rh = R // 2
    tr = _tile(Rh, 256, 8)
    nb = Rh // tr

    def body(w_ref, g_ref, r_ref, s_ref, o_ref):
        acc = g_ref[0] + r_ref[0]
        for k in range(slots.shape[0]):
            acc = acc + s_ref[k].astype(F32)
        o_ref[...] = acc

    return pl.pallas_call(
        body, name=name,
        grid_spec=pltpu.PrefetchScalarGridSpec(
            num_scalar_prefetch=1, grid=(nb,),
            in_specs=[pl.BlockSpec((1, tr, C), lambda i, w: (w[0], w[1] * nb + i, 0)),
                      pl.BlockSpec((1, tr, C), lambda i, w: (w[0], i, 0)),
                      pl.BlockSpec((slots.shape[0], tr, C), lambda i, w: (0, i, 0))],
            out_specs=pl.BlockSpec((tr, C), lambda i, w: (w[1] * nb + i, 0)),
        ),
        out_shape=jax.ShapeDtypeStruct((R, C), F32), compiler_params=_params(("parallel",)),
    )(where, g, r, slots)


ANY = pl.BlockSpec(memory_space=pl.ANY)


def _place():
    return lax.axis_index("x"), lax.axis_index("y"), lax.axis_index("c")


def _other_chips(x, y):
    return [(1 - x, y), (x, 1 - y), (1 - x, 1 - y)]


def _aliased_comm_call(body, bufs, n_sems, *, name):
    n = len(bufs)
    return pl.pallas_call(
        body, name=name, out_shape=[jax.ShapeDtypeStruct(b.shape, b.dtype) for b in bufs],
        in_specs=[ANY] * n, out_specs=[ANY] * n, input_output_aliases={k: k for k in range(n)},
        scratch_shapes=[pltpu.SemaphoreType.DMA((n_sems,)), pltpu.SemaphoreType.DMA((n_sems,))],
    )(*bufs)


HBM = pl.BlockSpec(memory_space=pltpu.HBM)
SEM = pl.BlockSpec(memory_space=pltpu.SEMAPHORE)
_SPLIT = pltpu.CompilerParams(has_side_effects=pltpu.SideEffectType.DATAFLOW_SIDE_EFFECTING)


def _in_hbm(arrs):
    return [pltpu.with_memory_space_constraint(a, pltpu.HBM) for a in arrs]


def _gather_ici_start(bufs, after, *, name):
    n = len(bufs)

    def body(*refs):
        send_sems, recv_sems, outs, token = refs[n + 1], refs[n + 2], refs[n + 3:2 * n + 3], refs[2 * n + 3]
        x, y, c = _place()
        for b in range(n):
            rh = bufs[b].shape[1] // 2
            part = outs[b].at[2 * x + y, pl.ds(c * rh, rh), :]
            for j, chip in enumerate(_other_chips(x, y)):
                pltpu.make_async_remote_copy(src_ref=part, dst_ref=part, send_sem=send_sems.at[3 * b + j],
                                             recv_sem=recv_sems.at[3 * b + j], device_id=(*chip, c),
                                             device_id_type=MESH).start()
        token[...] = jnp.zeros_like(token)

    res = pl.pallas_call(
        body, name=name,
        out_shape=(pltpu.SemaphoreType.DMA((3 * n,)), pltpu.SemaphoreType.DMA((3 * n,)),
                   *[pltpu.HBM(b.shape, b.dtype) for b in bufs], jax.ShapeDtypeStruct((8, 128), F32)),
        in_specs=[HBM] * n + [ANY], out_specs=(SEM, SEM, *[HBM] * n, pl.BlockSpec(memory_space=pltpu.VMEM)),
        input_output_aliases={k: k + 2 for k in range(n)}, compiler_params=_SPLIT,
    )(*_in_hbm(bufs), after)
    return res[0], res[1], list(res[2:2 + n]), res[2 + n]


def _gather_ici_wait(send_sems, recv_sems, bufs, after, *, name):
    n = len(bufs)

    def body(*refs):
        ins, ss, rs = refs[:n], refs[n], refs[n + 1]
        x, y, c = _place()
        for b in range(n):
            rh = bufs[b].shape[1] // 2
            mine = ins[b].at[2 * x + y, pl.ds(c * rh, rh), :]
            for j, (cx, cy) in enumerate(_other_chips(x, y)):
                theirs = ins[b].at[2 * cx + cy, pl.ds(c * rh, rh), :]
                cp = pltpu.make_async_remote_copy(src_ref=mine, dst_ref=theirs, send_sem=ss.at[3 * b + j],
                                                  recv_sem=rs.at[3 * b + j], device_id=(cx, cy, c),
                                                  device_id_type=MESH)
                cp.wait_send()
                cp.wait_recv()

    return list(pl.pallas_call(
        body, name=name, out_shape=[pltpu.HBM(b.shape, b.dtype) for b in bufs],
        in_specs=[HBM] * n + [SEM, SEM, ANY], out_specs=[HBM] * n,
        input_output_aliases={k: k for k in range(n)}, compiler_params=_SPLIT,
    )(*bufs, send_sems, recv_sems, after))


def _gather_forward(bufs, *, name):
    n = len(bufs)

    def body(*refs):
        outs, send_sems, recv_sems = refs[n:2 * n], refs[2 * n], refs[2 * n + 1]
        x, y, c = _place()

        def copy(b, j, chip, hc):
            rh = bufs[b].shape[1] // 2
            part = outs[b].at[2 * chip[0] + chip[1], pl.ds(hc * rh, rh), :]
            return pltpu.make_async_remote_copy(src_ref=part, dst_ref=part, send_sem=send_sems.at[3 * b + j],
                                                recv_sem=recv_sems.at[3 * b + j], device_id=(x, y, 1 - c),
                                                device_id_type=MESH)

        sends = [copy(b, j, chip, c) for b in range(n) for j, chip in enumerate(_other_chips(x, y))]
        for cp in sends:
            cp.start()
        for b in range(n):
            for j, chip in enumerate(_other_chips(x, y)):
                copy(b, j, chip, 1 - c).wait_recv()
        for cp in sends:
            cp.wait_send()

    return _aliased_comm_call(body, bufs, 3 * n, name=name)


def _chip_exchange_start(hs, *, name):
    n = len(hs)
    lands = [lax.empty((3,) + h.shape[1:], h.dtype) for h in hs]

    def body(*refs):
        send_sems, recv_sems = refs[2 * n], refs[2 * n + 1]
        h_out, l_out, token = refs[2 * n + 2:3 * n + 2], refs[3 * n + 2:4 * n + 2], refs[4 * n + 2]
        x, y, c = _place()
        for b in range(n):
            for j, (cx, cy) in enumerate(_other_chips(x, y)):
                pltpu.make_async_remote_copy(src_ref=h_out[b].at[2 * cx + cy], dst_ref=l_out[b].at[j],
                                             send_sem=send_sems.at[3 * b + j], recv_sem=recv_sems.at[3 * b + j],
                                             device_id=(cx, cy, c), device_id_type=MESH).start()
        token[...] = jnp.zeros_like(token)

    res = pl.pallas_call(
        body, name=name,
        out_shape=(pltpu.SemaphoreType.DMA((3 * n,)), pltpu.SemaphoreType.DMA((3 * n,)),
                   *[pltpu.HBM(a.shape, a.dtype) for a in hs + lands], jax.ShapeDtypeStruct((8, 128), F32)),
        in_specs=[HBM] * (2 * n), out_specs=(SEM, SEM, *[HBM] * (2 * n), pl.BlockSpec(memory_space=pltpu.VMEM)),
        input_output_aliases={k: k + 2 for k in range(2 * n)}, compiler_params=_SPLIT,
    )(*_in_hbm(hs + lands))
    return res[0], res[1], list(res[2:2 + n]), list(res[2 + n:2 + 2 * n]), res[2 + 2 * n]


def _chip_exchange_wait(send_sems, recv_sems, hs, lands, after, *, name):
    n = len(hs)

    def body(*refs):
        h_in, l_in, ss, rs = refs[:n], refs[n:2 * n], refs[2 * n], refs[2 * n + 1]
        x, y, c = _place()
        for b in range(n):
            for j, (cx, cy) in enumerate(_other_chips(x, y)):
                cp = pltpu.make_async_remote_copy(src_ref=h_in[b].at[2 * cx + cy], dst_ref=l_in[b].at[j],
                                                  send_sem=ss.at[3 * b + j], recv_sem=rs.at[3 * b + j],
                                                  device_id=(cx, cy, c), device_id_type=MESH)
                cp.wait_send()
                cp.wait_recv()

    res = pl.pallas_call(
        body, name=name, out_shape=[pltpu.HBM(a.shape, a.dtype) for a in hs + lands],
        in_specs=[HBM] * (2 * n) + [SEM, SEM, ANY], out_specs=[HBM] * (2 * n),
        input_output_aliases={k: k for k in range(2 * n)}, compiler_params=_SPLIT,
    )(*hs, *lands, send_sems, recv_sems, after)
    return list(res[n:])


def _peers(x, y, c):
    return [((1 - x) if fx else x, (1 - y) if fy else y, (1 - c) if fc else c)
            for fx in (0, 1) for fy in (0, 1) for fc in (0, 1) if fx or fy or fc]


def _all_to_all_start(slab, after, *, name):
    land = lax.empty((N_DEV,) + slab.shape, slab.dtype)

    def body(slab_in, land_in, after_ref, send_sems, recv_sems, slab_out, land_out, token):
        x, y, c = _place()
        for k, peer in enumerate(_peers(x, y, c)):
            pltpu.make_async_remote_copy(src_ref=slab_out, dst_ref=land_out.at[4 * x + 2 * y + c],
                                         send_sem=send_sems.at[k], recv_sem=recv_sems.at[k], device_id=peer,
                                         device_id_type=MESH).start()
        token[...] = jnp.zeros_like(token)

    return pl.pallas_call(
        body, name=name,
        out_shape=(pltpu.SemaphoreType.DMA((N_DEV - 1,)), pltpu.SemaphoreType.DMA((N_DEV - 1,)),
                   pltpu.HBM(slab.shape, slab.dtype), pltpu.HBM(land.shape, land.dtype),
                   jax.ShapeDtypeStruct((8, 128), F32)),
        in_specs=[HBM, HBM, ANY], out_specs=(SEM, SEM, HBM, HBM, pl.BlockSpec(memory_space=pltpu.VMEM)),
        input_output_aliases={0: 2, 1: 3}, compiler_params=_SPLIT,
    )(*_in_hbm([slab, land]), after)


def _all_to_all_wait(send_sems, recv_sems, slab, land, after, *, name):
    def body(slab_in, land_in, ss, rs, after_ref, slab_out, land_out):
        x, y, c = _place()
        for k, (px, py, pc) in enumerate(_peers(x, y, c)):
            cp = pltpu.make_async_remote_copy(src_ref=slab_in, dst_ref=land_in.at[4 * px + 2 * py + pc],
                                              send_sem=ss.at[k], recv_sem=rs.at[k], device_id=(px, py, pc),
                                              device_id_type=MESH)
            cp.wait_send()
            cp.wait_recv()

    return pl.pallas_call(
        body, name=name, out_shape=[pltpu.HBM(slab.shape, slab.dtype), pltpu.HBM(land.shape, land.dtype)],
        in_specs=[HBM, HBM, SEM, SEM, ANY], out_specs=[HBM, HBM], input_output_aliases={0: 0, 1: 1},
        compiler_params=_SPLIT,
    )(slab, land, send_sems, recv_sems, after)


def _pair_exchange_start(gs, *, name):
    n = len(gs)
    lands = [lax.empty((g.shape[0], g.shape[1] // 2, g.shape[2]), g.dtype) for g in gs]

    def body(*refs):
        send_sems, recv_sems = refs[2 * n], refs[2 * n + 1]
        g_out, l_out, token = refs[2 * n + 2:3 * n + 2], refs[3 * n + 2:4 * n + 2], refs[4 * n + 2]
        x, y, c = _place()
        for b in range(n):
            rh = gs[b].shape[1] // 2
            pltpu.make_async_remote_copy(src_ref=g_out[b].at[:, pl.ds((1 - c) * rh, rh), :], dst_ref=l_out[b],
                                         send_sem=send_sems.at[b], recv_sem=recv_sems.at[b],
                                         device_id=(x, y, 1 - c), device_id_type=MESH).start()
        token[...] = jnp.zeros_like(token)

    res = pl.pallas_call(
        body, name=name,
        out_shape=(pltpu.SemaphoreType.DMA((n,)), pltpu.SemaphoreType.DMA((n,)),
                   *[pltpu.HBM(a.shape, a.dtype) for a in gs + lands], jax.ShapeDtypeStruct((8, 128), F32)),
        in_specs=[HBM] * (2 * n), out_specs=(SEM, SEM, *[HBM] * (2 * n), pl.BlockSpec(memory_space=pltpu.VMEM)),
        input_output_aliases={k: k + 2 for k in range(2 * n)}, compiler_params=_SPLIT,
    )(*_in_hbm(gs + lands))
    return res[0], res[1], list(res[2:2 + n]), list(res[2 + n:2 + 2 * n]), res[2 + 2 * n]


def _pair_exchange_wait(send_sems, recv_sems, gs, lands, after, *, name):
    n = len(gs)

    def body(*refs):
        g_in, l_in, ss, rs = refs[:n], refs[n:2 * n], refs[2 * n], refs[2 * n + 1]
        x, y, c = _place()
        for b in range(n):
            rh = gs[b].shape[1] // 2
            cp = pltpu.make_async_remote_copy(src_ref=g_in[b].at[:, pl.ds((1 - c) * rh, rh), :], dst_ref=l_in[b],
                                              send_sem=ss.at[b], recv_sem=rs.at[b], device_id=(x, y, 1 - c),
                                              device_id_type=MESH)
            cp.wait_send()
            cp.wait_recv()

    res = pl.pallas_call(
        body, name=name, out_shape=[pltpu.HBM(a.shape, a.dtype) for a in gs + lands],
        in_specs=[HBM] * (2 * n) + [SEM, SEM, ANY], out_specs=[HBM] * (2 * n),
        input_output_aliases={k: k for k in range(2 * n)}, compiler_params=_SPLIT,
    )(*gs, *lands, send_sems, recv_sems, after)
    return list(res[:n]), list(res[n:])


def _pair_share(ss, *, name):
    n = len(ss)

    def body(*refs):
        outs, send_sems, recv_sems = refs[n:2 * n], refs[2 * n], refs[2 * n + 1]
        x, y, c = _place()
        cps = []
        for b in range(n):
            rh = ss[b].shape[0] // 2
            mine = outs[b].at[pl.ds(c * rh, rh), :]
            cps.append(pltpu.make_async_remote_copy(src_ref=mine, dst_ref=mine, send_sem=send_sems.at[b],
                                                    recv_sem=recv_sems.at[b], device_id=(x, y, 1 - c),
                                                    device_id_type=MESH))
        for cp in cps:
            cp.start()
        for b, cp in enumerate(cps):
            rh = ss[b].shape[0] // 2
            theirs = outs[b].at[pl.ds((1 - c) * rh, rh), :]
            pltpu.make_async_remote_copy(src_ref=theirs, dst_ref=theirs, send_sem=send_sems.at[b],
                                         recv_sem=recv_sems.at[b], device_id=(x, y, 1 - c),
                                         device_id_type=MESH).wait_recv()
            cp.wait_send()

    return _aliased_comm_call(body, ss, n, name=name)


_SMALL_SHARDED = (("e_conv_w", 2), ("o_norm", 1), ("o_d", 1))
_REPLICATED = ("e_norm", "e_gmlp_w", "e_gmlp_b", "e_conv_b", "e_conv_ln_g", "e_conv_ln_b", "o_lam_re", "o_lam_im",
               "o_log_dt", "o_b_re", "o_b_im", "o_c_re", "o_c_im", "ca_norm", "ca_mem_norm", "ffn_norm", "final_norm")
_SMALL = tuple(n for n, _ in _SMALL_SHARDED) + _REPLICATED
_WEIGHTS = ("e_norm", "e_w_in", "e_gmlp_w", "e_gmlp_b", "e_conv_w", "e_conv_b", "e_conv_ln_g", "e_conv_ln_b",
            "e_w_out", "o_norm", "o_w_in", "o_lam_re", "o_lam_im", "o_log_dt", "o_b_re", "o_b_im", "o_c_re", "o_c_im",
            "o_d", "o_w_out", "ca_norm", "ca_mem_norm", "ca_wq", "ca_wk", "ca_wv", "ca_wo", "ffn_norm", "ffn_w_gate",
            "ffn_w_up", "ffn_w_down", "final_norm")


def _pack_rows(arrs, width, dtype, row_mult=8):
    parts, spans, r0 = [], [], 0
    for a in arrs:
        flat = a.reshape(-1).astype(dtype)
        rows = -(-flat.shape[0] // (width * row_mult)) * row_mult
        if rows * width != flat.shape[0]:
            flat = jnp.pad(flat, (0, rows * width - flat.shape[0]))
        parts.append(flat.reshape(rows, width))
        spans.append((r0, rows))
        r0 += rows
    return jnp.concatenate(parts, axis=0), spans


def _unpack_rows(slab, spans, shapes):
    out = []
    for (r0, rows), shp in zip(spans, shapes):
        n = math.prod(shp)
        out.append(slab[r0:r0 + rows].reshape(-1)[:n].reshape(shp))
    return out


def _two_d(a):
    return a.reshape(-1, a.shape[-1])


def _shard_rows(n, a):
    return _two_d(jnp.swapaxes(a, -1, -2) if n in _TRANSPOSED else a)


def _from_shard_rows(n, rows, shape):
    if n in _TRANSPOSED:
        return jnp.swapaxes(rows.reshape(shape[:-2] + (shape[-1], shape[-2])), -1, -2)
    return rows.reshape(shape)


def _local_slab(local, slab, dtype):
    parts = sorted((r0, n, l) for n, (_, where) in _PLACE.items() for l, (s, r0) in enumerate(where) if s == slab)
    shards = [_shard_rows(n, local[n] if len(_PLACE[n][1]) == 1 else local[n][l]) for _, n, l in parts]
    return jnp.concatenate([a.astype(dtype) for a in shards], axis=0)


def _set_diag(b, pattern):
    return jnp.einsum(pattern, b, jnp.eye(C_GROUPS // N_SETS, dtype=b.dtype))


def _s5_discretize(lam_re, lam_im, log_dt, b_re, b_im):
    dt = jnp.exp(log_dt)[:, None]
    mag = jnp.exp(lam_re * dt)
    ar = mag * jnp.cos(lam_im * dt)
    ai = mag * jnp.sin(lam_im * dt)
    den = lam_re * lam_re + lam_im * lam_im
    qr = ((ar - 1.0) * lam_re + ai * lam_im) / den
    qi = (ai * lam_re - (ar - 1.0) * lam_im) / den
    bbr = qr[..., None] * b_re - qi[..., None] * b_im
    bbi = qr[..., None] * b_im + qi[..., None] * b_re
    return ar, ai, bbr, bbi


def _attention_block(x, mem, W, w, i, tag):
    xn, q = _norm_mm(x, w["ca_norm"][i], _shards(W, "ca_wq", i), split="k", out_dtype=BF16, name=f"{tag}_q")
    memn = _rms_fwd(mem, w["ca_mem_norm"][i], name=f"{tag}_ca_memnorm")
    k = _mm_k(memn, _shards(W, "ca_wk", i), out_dtype=BF16, name=f"{tag}_k")
    v = _mm_k(memn, _shards(W, "ca_wv", i), out_dtype=BF16, name=f"{tag}_v")
    o = _attn_fwd(q, k, v, name=f"{tag}_attn")
    y = _mm_k(o, _shards(W, "ca_wo", i), add=x, name=f"{tag}_wo")
    return y, (x, xn, memn, q, k, v, o)


def _attention_block_bwd(dy, saved, mem, W, w, i, tag, G, grads, token=None, mid=None):
    x, xn, memn, q, k, v, o = saved
    gain = w["ca_norm"][i]
    if token is not None:
        k = _behind(k, token)
    G = _grad_to_slab(G, "ca_wo", i, o, dy, a_cols=256, name=f"{tag}_dwo")
    dq, dk, dv = _attn_bwd(dy, _shards(W, "ca_wo", i), q, k, v, name=f"{tag}_attn_bwd")
    token = mid(dq) if mid is not None else None
    if token is not None:
        gain = _behind(gain, token)
    G = _grad_to_slab(G, "ca_wq", i, xn, dq, a_cols=256, name=f"{tag}_dwq")
    G = _grad_to_slab(G, "ca_wk", i, memn, dk, a_cols=256, name=f"{tag}_dwk")
    G = _grad_to_slab(G, "ca_wv", i, memn, dv, a_cols=256, name=f"{tag}_dwv")
    dmemn = _mm_k_t([(dk, _shards(W, "ca_wk", i)), (dv, _shards(W, "ca_wv", i))], name=f"{tag}_dmemn")
    dx, dg = _norm_bwd_k(dq, _shards(W, "ca_wq", i), x, gain, dy, name=f"{tag}_dq_norm_bwd")
    grads["ca_norm"][i] = dg[0]
    grads["ca_mem_norm"][i] = _rms_dg(mem, w["ca_mem_norm"][i], dmemn, name=f"{tag}_ca_memnorm_bwd")[0]
    return dx, G


def _ffn_block(x, W, w, i, tag, head=None):
    fn, gate, up, h = _ffn_up(x, w["ffn_norm"][i], _shards(W, "ffn_w_gate", i), _shards(W, "ffn_w_up", i),
                              name=f"{tag}_ffn_up")
    if head is None:
        y = _mm_k(h, _shards(W, "ffn_w_down", i), add=x, name=f"{tag}_down")
    else:
        y = _down_loss_head(h, _shards(W, "ffn_w_down", i), x, *head, name=f"{tag}_down_loss_head")
    return y, (x, fn, gate, up, h)


def _ffn_block_bwd(dy, saved, W, w, i, tag, G, grads, token=None, mid=None):
    x, fn, gate, up, h = saved
    gain = w["ffn_norm"][i]
    G = _grad_to_slab(G, "ffn_w_down", i, h, dy, name=f"{tag}_dwd")
    dg, du = _ffn_bwd_hidden(dy, _shards(W, "ffn_w_down", i), gate, up, token, name=f"{tag}_ffn_bwd_hidden")
    token = mid(dg) if mid is not None else None
    if token is not None:
        gain = _behind(gain, token)
    G = _grad_to_slab(G, "ffn_w_gate", i, dg, fn, name=f"{tag}_dwg")
    G = _grad_to_slab(G, "ffn_w_up", i, du, fn, name=f"{tag}_dwu")
    dx, dgn = _ffn_in_bwd(dg, du, _shards(W, "ffn_w_gate", i), _shards(W, "ffn_w_up", i), x, gain, dy,
                          name=f"{tag}_ffn_in_bwd")
    grads["ffn_norm"][i] = dgn[0]
    return dx, G


def _gmlp_mask():
    chunk = jnp.arange(GMLP_BLOCK) // CHUNK
    return chunk[None, :] <= chunk[:, None]


def _even_block(x, W, w, tag):
    hn, proj = _norm_mm(x, w["e_norm"][0], _shards(W, "e_w_in"), split="n", out_dtype=F32, name=f"{tag}_w_in")
    wm = jnp.where(_gmlp_mask()[None], w["e_gmlp_w"][0], 0.0).astype(BF16)
    bcol = w["e_gmlp_b"][0][:, :, None]
    cw = jnp.pad(w["e_conv_w"][0], ((0, CONV_HALO - CONV_WIDTH), (0, 0)))
    cb, lg, lb = w["e_conv_b"], w["e_conv_ln_g"], w["e_conv_ln_b"]
    mix, hc = _even_fwd(proj, wm, bcol, cw, cb, lg, lb, name=f"{tag}_mixers")
    y = _mm_k(mix, _shards(W, "e_w_out"), add=x, name=f"{tag}_w_out")
    return y, (x, hn, proj, mix, hc, wm, bcol, cw)


def _even_block_bwd(dy, saved, W, w, tag, G, grads):
    x, hn, proj, mix, hc, wm, bcol, cw = saved
    dmix = _mm_k_t([(dy, _shards(W, "e_w_out"))], name=f"{tag}_dmix")
    G = _grad_to_slab(G, "e_w_out", 0, mix, dy, a_cols=256, name=f"{tag}_dw_out")
    wmt = jnp.swapaxes(wm, 1, 2)
    dpa, dhc, dwm, db, dlg, dlb, dcb = _even_bwd1(proj, dmix, hc, wm, wmt, bcol, w["e_conv_ln_g"], w["e_conv_ln_b"],
                                                  name=f"{tag}_mixers_bwd1")
    dpb, dcw = _even_bwd2(proj, dhc, cw, name=f"{tag}_mixers_bwd2")
    grads["e_gmlp_w"] = jnp.where(_gmlp_mask()[None], dwm, 0.0)[None]
    grads["e_gmlp_b"] = db[:, :, 0][None]
    grads["e_conv_ln_g"], grads["e_conv_ln_b"], grads["e_conv_b"] = dlg, dlb, dcb
    grads["e_conv_w"] = dcw[:CONV_WIDTH][None]
    G = _grad_to_slab(G, "e_w_in", 0, hn, dpa, b_cols=512, chips=(0, 2), name=f"{tag}_dw_in_a")
    G = _grad_to_slab(G, "e_w_in", 0, hn, dpb, b_cols=512, chips=(2, 2), name=f"{tag}_dw_in_b")
    dx, dg = _norm_bwd_n((dpa, dpb), _shards(W, "e_w_in"), x, w["e_norm"][0], dy, name=f"{tag}_in_bwd")
    grads["e_norm"] = dg
    return dx, G


def _odd_block(x, W, w, tag):
    S = x.shape[0]
    hn, u = _norm_mm(x, w["o_norm"][0], _shards(W, "o_w_in"), split="k", out_dtype=F32, name=f"{tag}_w_in")
    disc_in = (w["o_lam_re"][0], w["o_lam_im"][0], w["o_log_dt"][0], w["o_b_re"][0], w["o_b_im"][0])
    (ar, ai, bbr, bbi), disc_vjp = jax.vjp(_s5_discretize, *disc_in)
    sets = (N_SETS, C_GROUPS // N_SETS)
    per_set = N_STATE // N_SETS
    bset = jnp.concatenate([_set_diag(b.reshape(sets + b.shape[1:]), "jgpc,gh->jgchp").reshape(N_SETS, SET_CH, per_set)
                            for b in (bbr, bbi)], axis=2).astype(BF16)
    cset = jnp.concatenate([_set_diag(c.reshape(sets + c.shape[1:]), "jgcp,gh->jgphc").reshape(N_SETS, per_set, SET_CH)
                            for c in (w["o_c_re"][0], -w["o_c_im"][0])], axis=1).astype(BF16)
    powers, pr, pi = [], ar, ai
    for _ in range(SCAN_BLOCK):
        powers.append(jnp.concatenate([pr.reshape(STATE_ROWS, STATE_LANES), pi.reshape(STATE_ROWS, STATE_LANES)], 0))
        pr, pi = pr * ar - pi * ai, pr * ai + pi * ar
    pw = jnp.stack(powers, axis=0)
    xs = _scan_fwd(u, bset, pw, name=f"{tag}_scan").reshape(S // 8, STATE_ROWS, 8, STATE_LANES)
    yv, yg = _s5_readout(xs, cset, u, w["o_d"], name=f"{tag}_readout")
    o, y = _glu_out(yg, _shards(W, "o_w_out"), x, name=f"{tag}_glu_out")
    return y, (x, hn, u, bset, cset, pw, xs, yv, yg, o, disc_vjp)


def _odd_block_bwd(dy, saved, W, w, tag, G, grads):
    x, hn, u, bset, cset, pw, xs, yv, yg, o, disc_vjp = saved
    S = x.shape[0]
    do, dys, dus, dd = _glu_out_bwd(o, dy, _shards(W, "o_w_out"), yv, u, w["o_d"], name=f"{tag}_glu_out_bwd")
    G = _grad_to_slab(G, "o_w_out", 0, yg, do, b_cols=512, name=f"{tag}_dw_out")
    grads["o_d"] = dd
    dcset_t = _state_grad_sets(dys, xs, name=f"{tag}_dcd")
    gs, da = _scan_bwd(dys, cset, xs.reshape(S * STATE_ROWS, STATE_LANES), pw, name=f"{tag}_scan_bwd")
    gs = gs.reshape(xs.shape)
    dbset = _state_grad_sets(u, gs, name=f"{tag}_dbd")
    du, dx, dg = _s5_in_bwd(gs, bset, dus, _shards(W, "o_w_in"), x, w["o_norm"][0], dy, name=f"{tag}_in_bwd")
    G = _grad_to_slab(G, "o_w_in", 0, hn, du, a_cols=256, name=f"{tag}_dw_in")
    grads["o_norm"] = dg
    per = C_GROUPS // N_SETS
    blocks = (N_SETS, per, C_GROUP_CH, 2, per, C_STATE)
    dc = _set_diag(dcset_t.reshape(blocks), "jhcrgp,gh->rjgcp").reshape(2, C_GROUPS, C_GROUP_CH, C_STATE)
    db = _set_diag(dbset.reshape(blocks), "jgcrhp,gh->rjgpc").reshape(2, C_GROUPS, C_STATE, C_GROUP_CH)
    dcr, dci, dbbr, dbbi = dc[0], -dc[1], db[0], db[1]
    dar = da[:STATE_ROWS].reshape(C_GROUPS, C_STATE)
    dai = da[STATE_ROWS:].reshape(C_GROUPS, C_STATE)
    dlr, dli, dldt, dbr, dbi = disc_vjp((dar, dai, dbbr, dbbi))
    grads["o_lam_re"], grads["o_lam_im"], grads["o_log_dt"] = dlr[None], dli[None], dldt[None]
    grads["o_b_re"], grads["o_b_im"], grads["o_c_re"], grads["o_c_im"] = dbr[None], dbi[None], dcr[None], dci[None]
    return dx, G


def _behind(value, token):
    return value + token[0, 0].astype(value.dtype)


class _NoExchange:
    def __init__(self, W):
        self.W = W

    def first_weights(self, w):
        return self.W, w

    def weights(self, stage, after):
        return {}

    def grads_ready(self, piece, G):
        return None

    def grads_crossed(self, piece, after):
        return None


def _forward_backward(xs_, mems_, tgt, w, G, exchange):
    W, w = exchange.first_weights(w)
    x1, s_mix0 = _even_block(xs_, W, w, "l0")
    W = {**W, **exchange.weights(1, x1)}
    x2, s_att0 = _attention_block(x1, mems_, W, w, 0, "l0")
    W = {**W, **exchange.weights(2, x2)}
    x3, s_ffn0 = _ffn_block(x2, W, w, 0, "l0")
    W = {**W, **exchange.weights(3, x3)}
    x4, s_mix1 = _odd_block(x3, W, w, "l1")
    x5, s_att1 = _attention_block(x4, mems_, W, w, 1, "l1")
    (dx, dfinal, loss_lanes), s_ffn1 = _ffn_block(x5, W, w, 1, "l1", head=(w["final_norm"], tgt))

    grads = {n: [None, None] for n in ("ca_norm", "ca_mem_norm", "ffn_norm")}
    grads["final_norm"] = dfinal[0]
    dx, G = _ffn_block_bwd(dx, s_ffn1, W, w, 1, "l1", G, grads)
    dx, G = _attention_block_bwd(dx, s_att1, mems_, W, w, 1, "l1", G, grads)
    dx, G = _odd_block_bwd(dx, s_mix1, W, w, "l1", G, grads)
    token = exchange.grads_ready("l1", G)
    dx, G = _ffn_block_bwd(dx, s_ffn0, W, w, 0, "l0", G, grads, token,
                           lambda after: exchange.grads_crossed("l1", after))
    token = exchange.grads_ready("ffn0", G)
    dx, G = _attention_block_bwd(dx, s_att0, mems_, W, w, 0, "l0", G, grads, token,
                                 lambda after: exchange.grads_crossed("ffn0", after))
    dx, G = _even_block_bwd(dx, s_mix0, W, w, "l0", G, grads)
    for n in list(grads):
        if isinstance(grads[n], list):
            grads[n] = jnp.stack(grads[n], axis=0)
        grads[n] = grads[n].reshape(w[n].shape)
    return loss_lanes, dx, G, grads


class _Exchange:
    def __init__(self, local, chip, core):
        self.bufs = {s: lax.dynamic_update_slice(lax.empty((N_CHIPS, rows, width), BF16),
                                                 _local_slab(local, s, BF16)[None], (chip, 0, 0))
                     for s, (width, rows) in _SLABS.items()}
        small = jnp.zeros((_SMALL_SLAB_ROWS, SMALL_W), F32)
        for n, (r0, rows) in _SMALL_PLACE.items():
            small = small.at[r0:r0 + rows].set(local[n].reshape(rows, SMALL_W))
        self.bufs[_SMALL_SLAB] = lax.dynamic_update_slice(lax.empty((N_CHIPS, _SMALL_SLAB_ROWS, SMALL_W), F32),
                                                          small[None], (chip, 0, 0))
        self.shard_shapes = {n: local[n].shape for n in _SMALL_PLACE}
        self.where = jnp.stack([chip, core]).astype(jnp.int32)
        self.flights = []
        self.reduces = {}

    def weights(self, stage, after):
        send_sems, recv_sems, bufs, _ = self.flights[stage]
        bufs = _gather_ici_wait(send_sems, recv_sems, bufs, after, name=f"gather_stage{stage}_wait")
        return dict(zip(self.stage_slabs(stage), _gather_forward(bufs, name=f"gather_stage{stage}_forward")))

    @staticmethod
    def stage_slabs(stage):
        return _STAGES[stage] + ((_SMALL_SLAB,) if stage == 0 else ())

    def first_weights(self, w):
        after = w["e_norm"]
        for k in range(len(_STAGES)):
            self.flights.append(_gather_ici_start([self.bufs[s] for s in self.stage_slabs(k)], after,
                                                  name=f"gather_stage{k}_start"))
            after = self.flights[-1][3]
        W = self.weights(0, after)
        w = {**w, "e_norm": _behind(w["e_norm"], after)}
        for (n, ax), (r0, rows) in zip(_SMALL_SHARDED, _SMALL_PLACE.values()):
            shards = [W[_SMALL_SLAB][p, r0:r0 + rows].reshape(self.shard_shapes[n]) for p in range(N_CHIPS)]
            w[n] = jnp.concatenate(shards, axis=ax)
        return W, w

    def pair_start(self, G, slabs, tag):
        send_sems, recv_sems, gl, lands, token = _pair_exchange_start([G[s] for s in slabs],
                                                                      name=f"grad_{tag}_pair_start")
        return (slabs, send_sems, recv_sems, gl, lands), token

    def pair_land(self, state, after, tag):
        slabs, send_sems, recv_sems, gl, lands = state
        gl, other = _pair_exchange_wait(send_sems, recv_sems, gl, lands, after, name=f"grad_{tag}_pair_wait")
        pairs = [_pair_sum(g, r, self.where, name=f"grad_pair_sum_{s}") for s, g, r in zip(slabs, gl, other)]
        send_sems, recv_sems, pairs, lands, token = _chip_exchange_start(pairs, name=f"grad_{tag}_chip_start")
        return (slabs, gl, other, send_sems, recv_sems, pairs, lands), token

    def reduce_finish(self, state, after, tag):
        slabs, gl, other, send_sems, recv_sems, pairs, lands = state
        slots = _chip_exchange_wait(send_sems, recv_sems, pairs, lands, after, name=f"grad_{tag}_chip_wait")
        halves = [_chip_sum(g, r, sl, self.where, name=f"grad_chip_sum_{s}")
                  for s, g, r, sl in zip(slabs, gl, other, slots)]
        return dict(zip(slabs, _pair_share(halves, name=f"grad_{tag}_pair_share")))

    def grads_ready(self, piece, G):
        self.reduces[piece], token = self.pair_start(G, _GRAD_PIECES[piece], piece)
        return token

    def grads_crossed(self, piece, after):
        self.reduces[piece], token = self.pair_land(self.reduces[piece], after, piece)
        return token


def kernel(x, mem, e_norm, e_w_in, e_gmlp_w, e_gmlp_b, e_conv_w, e_conv_b, e_conv_ln_g, e_conv_ln_b, e_w_out, o_norm, o_w_in, o_lam_re, o_lam_im, o_log_dt, o_b_re, o_b_im, o_c_re, o_c_im, o_d, o_w_out, ca_norm, ca_mem_norm, ca_wq, ca_wk, ca_wv, ca_wo, ffn_norm, ffn_w_gate, ffn_w_up, ffn_w_down, final_norm, loss_target, m_e_norm, m_e_w_in, m_e_gmlp_w, m_e_gmlp_b, m_e_conv_w, m_e_conv_b, m_e_conv_ln_g, m_e_conv_ln_b, m_e_w_out, m_o_norm, m_o_w_in, m_o_lam_re, m_o_lam_im, m_o_log_dt, m_o_b_re, m_o_b_im, m_o_c_re, m_o_c_im, m_o_d, m_o_w_out, m_ca_norm, m_ca_mem_norm, m_ca_wq, m_ca_wk, m_ca_wv, m_ca_wo, m_ffn_norm, m_ffn_w_gate, m_ffn_w_up, m_ffn_w_down, m_final_norm, v_e_norm, v_e_w_in, v_e_gmlp_w, v_e_gmlp_b, v_e_conv_w, v_e_conv_b, v_e_conv_ln_g, v_e_conv_ln_b, v_e_w_out, v_o_norm, v_o_w_in, v_o_lam_re, v_o_lam_im, v_o_log_dt, v_o_b_re, v_o_b_im, v_o_c_re, v_o_c_im, v_o_d, v_o_w_out, v_ca_norm, v_ca_mem_norm, v_ca_wq, v_ca_wk, v_ca_wv, v_ca_wo, v_ffn_norm, v_ffn_w_gate, v_ffn_w_up, v_ffn_w_down, v_final_norm):
    args = dict(locals())
    local = {n: args[n] for n in _WEIGHTS}
    mom = {n: args["m_" + n] for n in _WEIGHTS}
    vel = {n: args["v_" + n] for n in _WEIGHTS}
    chip = 2 * lax.axis_index("x") + lax.axis_index("y")
    core = lax.axis_index("c")
    xs_, mems_, tgt = x[0], mem[0], loss_target[0]

    w = {n: local[n] for n in _REPLICATED}
    exchange = _Exchange(local, chip, core)
    G = {s: lax.empty((N_CHIPS, rows, width), F32) for s, (width, rows) in _SLABS.items()}
    loss_lanes, dx, G, grads = _forward_backward(xs_, mems_, tgt, w, G, exchange)

    gs_slab, gs_spans = _pack_rows([grads[n] for n in _SMALL] + [loss_lanes], SMALL_W, F32)
    small_flight = _all_to_all_start(gs_slab, dx, name="small_grads_start")
    exchange.grads_ready("rest0", G)
    gsum = exchange.reduce_finish(exchange.reduces["l1"], small_flight[4], "l1")
    gsum = {**gsum, **exchange.reduce_finish(exchange.reduces["ffn0"], small_flight[4], "ffn0")}
    token = exchange.grads_crossed("rest0", gsum["B0"])

    gs_slab, gs_all = _all_to_all_wait(*small_flight[:4], token, name="small_grads_wait")
    gs_all = lax.dynamic_update_slice(gs_all, gs_slab[None], (2 * chip + core, 0, 0))
    gs_sum = _sum_slots(gs_all, name="small_grad_sum")
    *small_sums, loss_sum = _unpack_rows(gs_sum, gs_spans, [grads[n].shape for n in _SMALL] + [loss_lanes.shape])
    out_grads = dict(zip(_SMALL, small_sums))
    for n, ax in _SMALL_SHARDED:
        width = local[n].shape[ax]
        out_grads[n] = lax.dynamic_slice_in_dim(out_grads[n], chip * width, width, axis=ax)

    delta, new_m, new_v = {}, {}, {}
    d_, m_, v_ = _adamw_small([_two_d(local[n]) for n in _SMALL], [_two_d(out_grads[n]) for n in _SMALL],
                              [_two_d(mom[n]) for n in _SMALL], [_two_d(vel[n]) for n in _SMALL], name="adamw_small")
    for n, dd, mm_, vv in zip(_SMALL, d_, m_, v_):
        shp = local[n].shape
        delta[n], new_m[n], new_v[n] = dd.reshape(shp), mm_.reshape(shp), vv.reshape(shp)
    def adamw_large(names):
        for n in names:
            shp = local[n].shape
            g_, d_, m_, v_ = _adamw_shard(_shard_rows(n, local[n]), [(gsum[s], r0) for s, r0 in _PLACE[n][1]],
                                          _shard_rows(n, mom[n]), _shard_rows(n, vel[n]), name=f"adamw_{n}")
            out_grads[n], delta[n], new_m[n], new_v[n] = (_from_shard_rows(n, t, shp) for t in (g_, d_, m_, v_))

    ready = [n for n, (_, where) in _PLACE.items() if all(s in gsum for s, _ in where)]
    adamw_large(ready)
    done = jnp.concatenate([delta[n].reshape(-1)[:1] for n in ready + list(_SMALL[:1])])
    gsum = {**gsum, **exchange.reduce_finish(exchange.reduces["rest0"], done, "rest0")}
    adamw_large([n for n in _PLACE if n not in ready])

    return (loss_sum[0, 0], dx[None], *[out_grads[n] for n in _WEIGHTS], *[delta[n] for n in _WEIGHTS],
            *[new_m[n] for n in _WEIGHTS], *[new_v[n] for n in _WEIGHTS])
```

```python
import functools
import math

import jax
import jax.numpy as jnp
from jax import lax
from jax.experimental import pallas as pl
from jax.experimental.pallas import tpu as pltpu

F32 = jnp.float32
BF16 = jnp.bfloat16
MESH = pl.DeviceIdType.MESH

EPS = 1e-6
D_MODEL = 1024
A_WIDTH = 512
A_GROUPS = 4
GMLP_BLOCK = 128
CHUNK = 64
B_WIDTH = 512
CONV_WIDTH = 31
CONV_HALO = 32
C_WIDTH = 512
C_GROUP_CH = 16
C_GROUPS = 32
C_STATE = 64
N_STATE = C_GROUPS * C_STATE
STATE_LANES = 128
STATE_ROWS = N_STATE // STATE_LANES
SCAN_BLOCK = 8
CA_HEADS = 4
CA_HEAD_DIM = 256
FFN_HIDDEN = 2816

ADAM_LR = 0.001
ADAM_B1 = 0.9
ADAM_B2 = 0.999
ADAM_EPS = 1e-08
ADAM_WD = 0.01
ADAM_STEP = 10

VMEM_LIMIT = 56 * 1024 * 1024
ACC_BYTES = 6 * 1024 * 1024
TN_VMEM_BYTES = 44 * 1024 * 1024
SMALL_W = 128
N_CHIPS = 4
N_DEV = 8

_SLABS = {"D0": (512, 1024), "E0": (1024, 256), "A0": (1024, 1024), "B0": (1024, 704), "C0": (1024, 1408),
          "D1": (512, 768), "A1": (1024, 1024), "B1": (1024, 704), "C1": (1024, 1408)}
_STAGES = (("D0", "E0"), ("A0",), ("B0", "C0"), ("D1", "A1", "B1", "C1"))
_GRAD_PIECES = {"l1": _STAGES[3], "ffn0": _STAGES[2], "rest0": _STAGES[0] + _STAGES[1]}
_PLACE = {
    "e_w_in": (1024, (("D0", 0),)), "e_w_out": (256, (("E0", 0),)),
    "o_w_out": (512, (("D1", 0),)), "o_w_in": (256, (("D1", 512),)),
    "ca_wq": (256, (("A0", 0), ("A1", 0))), "ca_wk": (256, (("A0", 256), ("A1", 256))),
    "ca_wv": (256, (("A0", 512), ("A1", 512))), "ca_wo": (256, (("A0", 768), ("A1", 768))),
    "ffn_w_down": (704, (("B0", 0), ("B1", 0))),
    "ffn_w_gate": (704, (("C0", 0), ("C1", 0))), "ffn_w_up": (704, (("C0", 704), ("C1", 704))),
}
_SMALL_SLAB = "F0"
_SMALL_SLAB_ROWS = 48
_SMALL_PLACE = {"e_conv_w": (0, 31), "o_norm": (32, 2), "o_d": (34, 1)}
_TRANSPOSED = ("ffn_w_gate", "ffn_w_up")


def _params(sem=None):
    return pltpu.CompilerParams(dimension_semantics=sem, vmem_limit_bytes=VMEM_LIMIT)


def _tile(n, pref, mult=128):
    if n <= pref:
        return n
    t = (pref // mult) * mult
    while t >= mult:
        if n % t == 0:
            return t
        t -= mult
    return n


def _blk(name, layer=0):
    rows, where = _PLACE[name]
    slab, r0 = where[layer]
    assert r0 % rows == 0
    return slab, rows, r0 // rows


def _shards(slabs, name, layer=0):
    slab, rows, b = _blk(name, layer)
    return [(slabs[slab], (None, rows, _SLABS[slab][0]), (p, b, 0)) for p in range(N_CHIPS)]


_GELU_C = 0.7978845608028654
_GELU_A = 0.044715


def _gelu(x):
    t = jnp.tanh(_GELU_C * (x + _GELU_A * (x * x * x)))
    return 0.5 * x * (1.0 + t), t


def _gelu_grad(x, t):
    return 0.5 * (1.0 + t) + 0.5 * x * (1.0 - t * t) * (_GELU_C * (1.0 + 3.0 * _GELU_A * x * x))


def _sigmoid(x):
    return 1.0 / (1.0 + jnp.exp(-x))


def _mean(x):
    return jnp.mean(x, axis=-1, keepdims=True)


def _dot(a, b):
    return jnp.dot(a, b, preferred_element_type=F32)


def _dot_nt(a, b):
    return lax.dot_general(a, b, (((1,), (1,)), ((), ())), preferred_element_type=F32)


def _dot_tn(a, b):
    return lax.dot_general(a, b, (((0,), (0,)), ((), ())), preferred_element_type=F32)


def _rms_tile(xv, gv):
    return (xv * lax.rsqrt(_mean(xv * xv) + EPS)) * gv


def _rms_bwd_tile(xv, gv, dyv):
    r = lax.rsqrt(_mean(xv * xv) + EPS)
    xh = xv * r
    dyg = dyv * gv
    return r * (dyg - xh * _mean(dyg * xh)), jnp.sum(dyv * xh, axis=0, keepdims=True)


def _cols(p, width):
    return slice(p * width, (p + 1) * width)


def _sum_k(a, ws, k):
    tot = None
    for p in range(N_CHIPS):
        y = _dot(a[:, _cols(p, k)], ws[p][...])
        tot = y if tot is None else tot + y
    return tot


def _cat_nt(a, ws):
    return jnp.concatenate([_dot_nt(a, ws[p][...]) for p in range(N_CHIPS)], axis=1)


def _rows_call(name, tm, rows, fulls, outs, accs, body, scratch=()):
    S = min(x.shape[-2] for x in rows if x.ndim != 4)
    nr, nf, no, na = len(rows), len(fulls), len(outs), len(accs)

    def kern(*refs):
        r, f = refs[:nr], refs[nr:nr + nf]
        o, a = refs[nr + nf:nr + nf + no], refs[nr + nf + no:nr + nf + no + na]
        if na:
            @pl.when(pl.program_id(0) == 0)
            def _():
                for ref in a:
                    ref[...] = jnp.zeros_like(ref)
        body(r, f, o, a, refs[nr + nf + no + na:])

    def whole(shape):
        nd = len(shape)
        return pl.BlockSpec(tuple(shape), lambda i: (0,) * nd)

    def row_spec(shape):
        if len(shape) == 4:
            return pl.BlockSpec((tm // 8,) + tuple(shape[1:]), lambda i: (i, 0, 0, 0))
        if len(shape) == 3:
            return pl.BlockSpec((shape[0], tm, shape[2]), lambda i: (0, i, 0))
        return pl.BlockSpec((tm, shape[1]), lambda i: (i, 0))

    def full_spec(x):
        if isinstance(x, tuple):
            _, bshape, bidx = x
            return pl.BlockSpec(bshape, lambda i: bidx, pipeline_mode=pl.Buffered(1))
        return whole(x.shape)

    out_shapes = [(S, o[0]) if len(o) == 2 else (o[0], S, o[1]) for o in outs]
    res = pl.pallas_call(
        kern, name=name, grid=(S // tm,),
        in_specs=[row_spec(x.shape) for x in rows] + [full_spec(x) for x in fulls],
        out_specs=[row_spec(s) for s in out_shapes] + [whole(shp) for shp, _ in accs],
        out_shape=[jax.ShapeDtypeStruct(s, o[-1]) for s, o in zip(out_shapes, outs)]
        + [jax.ShapeDtypeStruct(tuple(shp), dt) for shp, dt in accs],
        scratch_shapes=list(scratch),
        compiler_params=_params(("arbitrary",) if na else ("parallel",)),
    )(*rows, *[x[0] if isinstance(x, tuple) else x for x in fulls])
    return res[:no], res[no:]


def _grad_to_slab(gslabs, wname, layer, a, b, *, a_cols=None, b_cols=None, chips=(0, N_CHIPS), name):
    slab, rows, bidx = _blk(wname, layer)
    width = _SLABS[slab][0]
    p0, n_p = chips
    assert p0 % n_p == 0
    S = a.shape[-2]

    def tile_bytes(x, ts):
        return ts * x.dtype.itemsize * (x.shape[2] * n_p if x.ndim == 3 else x.shape[1])

    acc_bytes = n_p * rows * (-(-width // 128) * 128) * 4
    ts = next(t for t in (2048, 1024, 512, 256, S) if S % t == 0
              and 2 * (tile_bytes(a, t) + tile_bytes(b, t) + acc_bytes) <= TN_VMEM_BYTES or t == S)

    def operand(x):
        if x.ndim == 3:
            return pl.BlockSpec((n_p, ts, x.shape[2]), lambda s: (p0 // n_p, s, 0))
        return pl.BlockSpec((ts, x.shape[1]), lambda s: (s, 0))

    def part(ref, cols, p):
        if len(ref.shape) == 3:
            return ref[p]
        return ref[...] if cols is None else ref[:, _cols(p, cols)]

    def body(a_ref, b_ref, slab_ref, o_ref):
        @pl.when(pl.program_id(0) == 0)
        def _():
            o_ref[...] = jnp.zeros_like(o_ref)

        for p in range(n_p):
            o_ref[p] += _dot_tn(part(a_ref, a_cols, p).astype(BF16), part(b_ref, b_cols, p).astype(BF16))

    g = gslabs[slab]
    out = pl.pallas_call(
        body, name=name, grid=(S // ts,),
        in_specs=[operand(a), operand(b), pl.BlockSpec(memory_space=pl.ANY)],
        out_specs=pl.BlockSpec((n_p, rows, width), lambda s: (p0 // n_p, bidx, 0)),
        out_shape=jax.ShapeDtypeStruct(g.shape, F32), input_output_aliases={2: 0},
        compiler_params=_params(("arbitrary",)),
    )(a, b, g)
    return {**gslabs, slab: out}


def _vec(g):
    return g.reshape(1, -1)


def _norm_mm(x, g, ws, *, split, out_dtype, name, tm=512):
    S, D = x.shape
    k, n = ws[0][1][1], ws[0][1][2]
    N = n if split == "k" else N_CHIPS * n

    def body(r, f, o, acc, s):
        xn = _rms_tile(r[0][...], f[0][...]).astype(BF16)
        o[0][...] = xn
        if split == "k":
            o[1][...] = _sum_k(xn, f[1:], k).astype(out_dtype)
        else:
            for p in range(N_CHIPS):
                o[1][:, _cols(p, n)] = _dot(xn, f[1 + p][...]).astype(out_dtype)

    (xn, y), _ = _rows_call(name, _tile(S, tm), [x], [_vec(g)] + ws, [(D, BF16), (N, out_dtype)], [], body)
    return xn, y


def _mm_k(a, ws, *, add=None, out_dtype=F32, name, tm=512):
    S = a.shape[-2]
    k, n = ws[0][1][1], ws[0][1][2]
    has_add = add is not None

    def body(r, f, o, acc, s):
        if a.ndim == 3:
            y = None
            for p in range(N_CHIPS):
                t = _dot(r[0][p].astype(BF16), f[p][...])
                y = t if y is None else y + t
        else:
            y = _sum_k(r[0][...].astype(BF16), f, k)
        if has_add:
            y = y + r[1][...]
        o[0][...] = y.astype(out_dtype)

    (y,), _ = _rows_call(name, _tile(S, tm), [a] + ([add] if has_add else []), ws, [(n, out_dtype)], [], body)
    return y


def _mm_k_t(terms, *, out_dtype=F32, name, tm=512):
    S = terms[0][0].shape[0]
    k = terms[0][1][0][1][1]

    def body(r, f, o, acc, s):
        y = None
        for t in range(len(terms)):
            yt = _cat_nt(r[t][...].astype(BF16), f[N_CHIPS * t:N_CHIPS * (t + 1)])
            y = yt if y is None else y + yt
        o[0][...] = y.astype(out_dtype)

    (y,), _ = _rows_call(name, _tile(S, tm), [a for a, _ in terms], [w for _, ws in terms for w in ws],
                         [(N_CHIPS * k, out_dtype)], [], body)
    return y


def _rms_fwd(x, g, *, name):
    def body(r, f, o, acc, s):
        o[0][...] = _rms_tile(r[0][...], f[0][...]).astype(BF16)

    (y,), _ = _rows_call(name, _tile(x.shape[0], 256, 8), [x], [_vec(g)], [(x.shape[1], BF16)], [], body)
    return y


def _rms_dg(x, g, dy, *, name):
    def body(r, f, o, acc, s):
        acc[0][...] += _rms_bwd_tile(r[0][...], f[0][...], r[1][...])[1]

    _, (dg,) = _rows_call(name, _tile(x.shape[0], 256, 8), [x, dy], [_vec(g)], [], [((1, x.shape[1]), F32)], body)
    return dg


def _ffn_up(x, g, wg, wu, *, name, tm=512):
    S, D = x.shape
    h = wg[0][1][1]

    def body(r, f, o, acc, s):
        xn = _rms_tile(r[0][...], f[0][...]).astype(BF16)
        o[0][...] = xn
        for p in range(N_CHIPS):
            gate = _dot_nt(xn, f[1 + p][...])
            up = _dot_nt(xn, f[1 + N_CHIPS + p][...])
            o[1][p] = gate.astype(BF16)
            o[2][p] = up.astype(BF16)
            o[3][p] = (gate * _sigmoid(gate) * up).astype(BF16)

    (xn, gate, up, hid), _ = _rows_call(name, _tile(S, tm), [x], [_vec(g)] + wg + wu,
                                        [(D, BF16), (N_CHIPS, h, BF16), (N_CHIPS, h, BF16), (N_CHIPS, h, BF16)], [],
                                        body)
    return xn, gate, up, hid


def _ffn_bwd_hidden(dy, wd, gate, up, token=None, *, name, tm=512):
    S = dy.shape[0]
    h = wd[0][1][1]

    def body(r, f, o, acc, s):
        dyv = r[0][...]
        if token is not None:
            dyv = dyv + jnp.sum(f[N_CHIPS][...])
        dyb = dyv.astype(BF16)
        for p in range(N_CHIPS):
            dh = _dot_nt(dyb, f[p][...])
            gv = r[1][p].astype(F32)
            sg = _sigmoid(gv)
            o[0][p] = (dh * r[2][p].astype(F32) * (sg * (1.0 + gv * (1.0 - sg)))).astype(BF16)
            o[1][p] = (dh * gv * sg).astype(BF16)

    (dg, du), _ = _rows_call(name, _tile(S, tm), [dy, gate, up], wd + ([] if token is None else [token]),
                             [(N_CHIPS, h, BF16), (N_CHIPS, h, BF16)], [], body)
    return dg, du


def _ffn_in_bwd(dg, du, wg, wu, x, g, dres, *, name, tm=512):
    S, D = x.shape

    def body(r, f, o, acc, s):
        tot = None
        for p in range(N_CHIPS):
            y = _dot(r[0][p], f[1 + p][...]) + _dot(r[1][p], f[1 + N_CHIPS + p][...])
            tot = y if tot is None else tot + y
        dx, dgn = _rms_bwd_tile(r[2][...], f[0][...], tot)
        o[0][...] = dx + r[3][...]
        acc[0][...] += dgn

    (dx,), (dgn,) = _rows_call(name, _tile(S, tm), [dg, du, x, dres], [_vec(g)] + wg + wu, [(D, F32)],
                               [((1, D), F32)], body)
    return dx, dgn


def _norm_bwd_k(da, ws, x, g, dres, *, name, tm=512):
    S, D = x.shape

    def body(r, f, o, acc, s):
        dx, dg = _rms_bwd_tile(r[1][...], f[0][...], _cat_nt(r[0][...].astype(BF16), f[1:]))
        o[0][...] = dx + r[2][...]
        acc[0][...] += dg

    (dx,), (dg,) = _rows_call(name, _tile(S, tm), [da, x, dres], [_vec(g)] + ws, [(D, F32)], [((1, D), F32)], body)
    return dx, dg


def _norm_bwd_n(das, ws, x, g, dres, *, name, tm=256):
    S, D = x.shape
    n = ws[0][1][2]

    def body(r, f, o, acc, s):
        tot = None
        for p in range(N_CHIPS):
            y = _dot_nt(r[p // 2][:, _cols(p % 2, n)], f[1 + p][...])
            tot = y if tot is None else tot + y
        dx, dg = _rms_bwd_tile(r[2][...], f[0][...], tot)
        o[0][...] = dx + r[3][...]
        acc[0][...] += dg

    (dx,), (dg,) = _rows_call(name, _tile(S, tm), list(das) + [x, dres], [_vec(g)] + ws, [(D, F32)], [((1, D), F32)],
                              body)
    return dx, dg


def _ln_stats(v):
    mu = _mean(v)
    xc = v - mu
    rstd = lax.rsqrt(_mean(xc * xc) + EPS)
    return xc * rstd, rstd


_SHIFTS = 8
_CONV_ROWS = 64


def _fill_shifts(sh_ref, ext_ref, tm):
    sh_ref[0] = ext_ref[...]
    for s in range(1, _SHIFTS):
        sh_ref[s, 0:tm + CONV_HALO - _SHIFTS, :] = ext_ref[pl.ds(s, tm + CONV_HALO - _SHIFTS), :]


def _window(sh_ref, off, tm):
    return sh_ref[off % _SHIFTS, pl.ds(off - off % _SHIFTS, tm), :]


def _even_fwd(proj, wm, bcol, cw, cb, lg, lb, *, name):
    S = proj.shape[0]
    tm = _tile(S, 256)
    hb = tm // CONV_HALO
    nblk = tm // GMLP_BLOCK

    def body(p_ref, halo_ref, wm_ref, b_ref, cw_ref, cb_ref, lg_ref, lb_ref, mix_ref, hc_ref, hext_ref, hsh_ref):
        i = pl.program_id(0)
        gu, _ = _gelu(p_ref[:, 0:A_WIDTH])
        gv, _ = _gelu(p_ref[:, A_WIDTH:2 * A_WIDTH])
        vn, _ = _ln_stats(gv)
        vnb = vn.astype(BF16)
        for n in range(nblk):
            rows = slice(n * GMLP_BLOCK, (n + 1) * GMLP_BLOCK)
            for g in range(A_GROUPS):
                cols = slice(g * GMLP_BLOCK, (g + 1) * GMLP_BLOCK)
                sg = jnp.dot(wm_ref[g], vnb[rows, cols], preferred_element_type=F32) + b_ref[g]
                mix_ref[rows, cols] = (gu[rows, cols] * sg).astype(BF16)
        h = p_ref[:, 1024:1536] * _sigmoid(p_ref[:, 1536:2048])
        hh = halo_ref[:, 0:B_WIDTH] * _sigmoid(halo_ref[:, B_WIDTH:2 * B_WIDTH])
        hext_ref[0:CONV_HALO, :] = jnp.where(i > 0, hh, 0.0)
        hext_ref[CONV_HALO:CONV_HALO + tm, :] = h
        _fill_shifts(hsh_ref, hext_ref, tm)
        for r0 in range(0, tm, _CONV_ROWS):
            acc = jnp.zeros((_CONV_ROWS, B_WIDTH), F32)
            for k in range(CONV_WIDTH):
                acc = acc + cw_ref[k:k + 1, :] * _window(hsh_ref, r0 + k + CONV_HALO - CONV_WIDTH + 1, _CONV_ROWS)
            hc_ref[r0:r0 + _CONV_ROWS, :] = acc + cb_ref[...]
        hc = hc_ref[...]
        hhat, _ = _ln_stats(hc)
        hl = hhat * lg_ref[...] + lb_ref[...]
        mix_ref[:, A_WIDTH:A_WIDTH + B_WIDTH] = (hl * _sigmoid(hl)).astype(BF16)

    vec = pl.BlockSpec((1, B_WIDTH), lambda i: (0, 0))
    return pl.pallas_call(
        body, name=name, grid=(S // tm,),
        in_specs=[
            pl.BlockSpec((tm, 2048), lambda i: (i, 0)),
            pl.BlockSpec((CONV_HALO, 1024), lambda i: (jnp.maximum(i * hb - 1, 0), 1)),
            pl.BlockSpec((A_GROUPS, GMLP_BLOCK, GMLP_BLOCK), lambda i: (0, 0, 0)),
            pl.BlockSpec((A_GROUPS, GMLP_BLOCK, 1), lambda i: (0, 0, 0)),
            pl.BlockSpec((CONV_HALO, B_WIDTH), lambda i: (0, 0)),
            vec, vec, vec,
        ],
        out_specs=[pl.BlockSpec((tm, 1024), lambda i: (i, 0)), pl.BlockSpec((tm, B_WIDTH), lambda i: (i, 0))],
        out_shape=[jax.ShapeDtypeStruct((S, 1024), BF16), jax.ShapeDtypeStruct((S, B_WIDTH), F32)],
        scratch_shapes=[pltpu.VMEM((tm + CONV_HALO, B_WIDTH), F32),
                        pltpu.VMEM((_SHIFTS, tm + CONV_HALO, B_WIDTH), F32)],
        compiler_params=_params(("parallel",)),
    )(proj, proj, wm, bcol, cw, cb, lg, lb)


def _even_bwd1(proj, dmix, hc, wm, wmt, bcol, lg, lb, *, name):
    S = proj.shape[0]
    tm = _tile(S, 256)
    nblk = tm // GMLP_BLOCK

    def body(p_ref, dm_ref, hc_ref, wm_ref, wmt_ref, b_ref, lg_ref, lb_ref,
             dpa_ref, dhc_ref, dwm_ref, db_ref, dlg_ref, dlb_ref, dcb_ref, dgu_ref, dvn_ref):
        @pl.when(pl.program_id(0) == 0)
        def _():
            dwm_ref[...] = jnp.zeros_like(dwm_ref)
            db_ref[...] = jnp.zeros_like(db_ref)
            dlg_ref[...] = jnp.zeros_like(dlg_ref)
            dlb_ref[...] = jnp.zeros_like(dlb_ref)
            dcb_ref[...] = jnp.zeros_like(dcb_ref)

        au = p_ref[:, 0:A_WIDTH]
        av = p_ref[:, A_WIDTH:2 * A_WIDTH]
        gu, tu = _gelu(au)
        gv, tv = _gelu(av)
        vn, rstd = _ln_stats(gv)
        vnb = vn.astype(BF16)
        for n in range(nblk):
            rows = slice(n * GMLP_BLOCK, (n + 1) * GMLP_BLOCK)
            for g in range(A_GROUPS):
                cols = slice(g * GMLP_BLOCK, (g + 1) * GMLP_BLOCK)
                vb = vnb[rows, cols]
                sg = jnp.dot(wm_ref[g], vb, preferred_element_type=F32) + b_ref[g]
                da = dm_ref[rows, cols]
                dsg = da * gu[rows, cols]
                dgu_ref[rows, cols] = da * sg
                dsgb = dsg.astype(BF16)
                dwm_ref[g] += _dot_nt(dsgb, vb)
                db_ref[g] += jnp.sum(dsg, axis=1, keepdims=True)
                dvn_ref[rows, cols] = jnp.dot(wmt_ref[g], dsgb, preferred_element_type=F32)
        dvn = dvn_ref[...]
        dgv = rstd * (dvn - _mean(dvn) - vn * _mean(dvn * vn))
        dpa_ref[:, 0:A_WIDTH] = (dgu_ref[...] * _gelu_grad(au, tu)).astype(BF16)
        dpa_ref[:, A_WIDTH:2 * A_WIDTH] = (dgv * _gelu_grad(av, tv)).astype(BF16)
        hhat, rstd2 = _ln_stats(hc_ref[...])
        lgv = lg_ref[...]
        hl = hhat * lgv + lb_ref[...]
        s = _sigmoid(hl)
        dhl = dm_ref[:, A_WIDTH:A_WIDTH + B_WIDTH] * (s * (1.0 + hl * (1.0 - s)))
        dlg_ref[...] += jnp.sum(dhl * hhat, axis=0, keepdims=True)
        dlb_ref[...] += jnp.sum(dhl, axis=0, keepdims=True)
        dhh = dhl * lgv
        dhc = rstd2 * (dhh - _mean(dhh) - hhat * _mean(dhh * hhat))
        dcb_ref[...] += jnp.sum(dhc, axis=0, keepdims=True)
        dhc_ref[...] = dhc

    vec = pl.BlockSpec((1, B_WIDTH), lambda i: (0, 0))
    w3 = pl.BlockSpec((A_GROUPS, GMLP_BLOCK, GMLP_BLOCK), lambda i: (0, 0, 0))
    b3 = pl.BlockSpec((A_GROUPS, GMLP_BLOCK, 1), lambda i: (0, 0, 0))
    return pl.pallas_call(
        body, name=name, grid=(S // tm,),
        in_specs=[
            pl.BlockSpec((tm, 1024), lambda i: (i, 0)),
            pl.BlockSpec((tm, 1024), lambda i: (i, 0)),
            pl.BlockSpec((tm, B_WIDTH), lambda i: (i, 0)),
            w3, w3, b3, vec, vec,
        ],
        out_specs=[pl.BlockSpec((tm, 1024), lambda i: (i, 0)), pl.BlockSpec((tm, B_WIDTH), lambda i: (i, 0)),
                   w3, b3, vec, vec, vec],
        out_shape=[
            jax.ShapeDtypeStruct((S, 1024), BF16), jax.ShapeDtypeStruct((S, B_WIDTH), F32),
            jax.ShapeDtypeStruct((A_GROUPS, GMLP_BLOCK, GMLP_BLOCK), F32),
            jax.ShapeDtypeStruct((A_GROUPS, GMLP_BLOCK, 1), F32),
            jax.ShapeDtypeStruct((1, B_WIDTH), F32), jax.ShapeDtypeStruct((1, B_WIDTH), F32),
            jax.ShapeDtypeStruct((1, B_WIDTH), F32),
        ],
        scratch_shapes=[pltpu.VMEM((tm, A_WIDTH), F32), pltpu.VMEM((tm, A_WIDTH), F32)],
        compiler_params=_params(("arbitrary",)),
    )(proj, dmix, hc, wm, wmt, bcol, lg, lb)


def _even_bwd2(proj, dhc, cw, *, name):
    S = proj.shape[0]
    tm = _tile(S, 256)
    hb = tm // CONV_HALO
    nt = S // tm
    last_halo = S // CONV_HALO - 1
    lo = CONV_HALO - CONV_WIDTH + 1

    def body(p_ref, halo_ref, d_ref, dnext_ref, cw_ref, dpb_ref, dcw_ref, hext_ref, dext_ref, hsh_ref, dsh_ref):
        i = pl.program_id(0)

        @pl.when(i == 0)
        def _():
            dcw_ref[...] = jnp.zeros_like(dcw_ref)

        hh = halo_ref[:, 0:B_WIDTH] * _sigmoid(halo_ref[:, B_WIDTH:2 * B_WIDTH])
        hext_ref[0:CONV_HALO, :] = jnp.where(i > 0, hh, 0.0)
        hext_ref[CONV_HALO:CONV_HALO + tm, :] = p_ref[:, 0:B_WIDTH] * _sigmoid(p_ref[:, B_WIDTH:2 * B_WIDTH])
        dext_ref[0:tm, :] = d_ref[...]
        dext_ref[tm:tm + CONV_HALO, :] = jnp.where(i < nt - 1, dnext_ref[...], 0.0)
        _fill_shifts(hsh_ref, hext_ref, tm)
        _fill_shifts(dsh_ref, dext_ref, tm)
        for r0 in range(0, tm, _CONV_ROWS):
            rows = slice(r0, r0 + _CONV_ROWS)
            dhc_b = d_ref[rows, :]
            dh = jnp.zeros((_CONV_ROWS, B_WIDTH), F32)
            for k in range(CONV_WIDTH):
                dh = dh + cw_ref[k:k + 1, :] * _window(dsh_ref, r0 + CONV_WIDTH - 1 - k, _CONV_ROWS)
                dcw_ref[k:k + 1, :] += jnp.sum(dhc_b * _window(hsh_ref, r0 + k + lo, _CONV_ROWS), axis=0,
                                               keepdims=True)
            ba_b = p_ref[rows, 0:B_WIDTH]
            sg_b = _sigmoid(p_ref[rows, B_WIDTH:2 * B_WIDTH])
            dpb_ref[rows, 0:B_WIDTH] = (dh * sg_b).astype(BF16)
            dpb_ref[rows, B_WIDTH:2 * B_WIDTH] = (dh * ba_b * sg_b * (1.0 - sg_b)).astype(BF16)

    return pl.pallas_call(
        body, name=name, grid=(nt,),
        in_specs=[
            pl.BlockSpec((tm, 1024), lambda i: (i, 1)),
            pl.BlockSpec((CONV_HALO, 1024), lambda i: (jnp.maximum(i * hb - 1, 0), 1)),
            pl.BlockSpec((tm, B_WIDTH), lambda i: (i, 0)),
            pl.BlockSpec((CONV_HALO, B_WIDTH), lambda i: (jnp.minimum((i + 1) * hb, last_halo), 0)),
            pl.BlockSpec((CONV_HALO, B_WIDTH), lambda i: (0, 0)),
        ],
        out_specs=[pl.BlockSpec((tm, 1024), lambda i: (i, 0)), pl.BlockSpec((CONV_HALO, B_WIDTH), lambda i: (0, 0))],
        out_shape=[jax.ShapeDtypeStruct((S, 1024), BF16), jax.ShapeDtypeStruct((CONV_HALO, B_WIDTH), F32)],
        scratch_shapes=[pltpu.VMEM((tm + CONV_HALO, B_WIDTH), F32), pltpu.VMEM((tm + CONV_HALO, B_WIDTH), F32),
                        pltpu.VMEM((_SHIFTS, tm + CONV_HALO, B_WIDTH), F32),
                        pltpu.VMEM((_SHIFTS, tm + CONV_HALO, B_WIDTH), F32)],
        compiler_params=_params(("arbitrary",)),
    )(proj, proj, dhc, dhc, cw)


_CA_SCALE = CA_HEAD_DIM ** -0.5


def _softmax_rows(s):
    e = jnp.exp(s - jnp.max(s, axis=-1, keepdims=True))
    return e / jnp.sum(e, axis=-1, keepdims=True)


def _attn_fwd(q, k, v, *, name):
    S = q.shape[0]

    def body(r, f, o, acc, s):
        for h in range(CA_HEADS):
            cols = _cols(h, CA_HEAD_DIM)
            p = _softmax_rows(_dot_nt(r[0][:, cols], f[0][:, cols]) * _CA_SCALE)
            o[0][:, cols] = _dot(p.astype(BF16), f[1][:, cols]).astype(BF16)

    (o_,), _ = _rows_call(name, _tile(S, 512), [q], [k, v], [(D_MODEL, BF16)], [], body)
    return o_


def _attn_bwd(dy, wo, q, k, v, *, name):
    S = q.shape[0]
    M = k.shape[0]

    def body(r, f, o, acc, s):
        dyb = r[0][...].astype(BF16)
        for h in range(CA_HEADS):
            cols = _cols(h, CA_HEAD_DIM)
            qh = r[1][:, cols]
            kh = f[0][:, cols]
            vh = f[1][:, cols]
            doh = _dot_nt(dyb, f[2 + h][...]).astype(BF16)
            p = _softmax_rows(_dot_nt(qh, kh) * _CA_SCALE)
            acc[1][:, cols] += _dot_tn(p.astype(BF16), doh)
            dp = _dot_nt(doh, vh)
            ds = (p * (dp - jnp.sum(dp * p, axis=-1, keepdims=True)) * _CA_SCALE).astype(BF16)
            o[0][:, cols] = _dot(ds, kh).astype(BF16)
            acc[0][:, cols] += _dot_tn(ds, qh)

    (dq,), (dk, dv) = _rows_call(name, _tile(S, 512), [dy, q], [k, v] + wo, [(D_MODEL, BF16)],
                                 [((M, D_MODEL), F32), ((M, D_MODEL), F32)], body)
    return dq, dk, dv


_STATE_TILE = 2 * STATE_ROWS
N_SETS = 4
SET_CH = C_WIDTH // N_SETS
SET_COLS = N_STATE // N_SETS // STATE_LANES


def _set_groups(j):
    return [SET_COLS * j + c for c in range(SET_COLS)] + [STATE_ROWS + SET_COLS * j + c for c in range(SET_COLS)]


def _pack_state(re, im):
    hi = lax.bitcast_convert_type(re.astype(BF16).astype(F32), jnp.uint32)
    lo = lax.bitcast_convert_type(im.astype(BF16).astype(F32), jnp.uint32) >> 16
    return hi | lo


def _unpack_state(word):
    re = lax.bitcast_convert_type(word & jnp.uint32(0xFFFF0000), F32)
    im = lax.bitcast_convert_type(word << 16, F32)
    return re, im


def _state_set(ref, tm, j):
    parts = [_unpack_state(ref[:, SET_COLS * j + c, :, :].reshape(tm, STATE_LANES)) for c in range(SET_COLS)]
    return jnp.concatenate([p[0].astype(BF16) for p in parts] + [p[1].astype(BF16) for p in parts], axis=1)


def _s5_readout(xs, cset, u, d, *, name, tm=256):
    tm = _tile(u.shape[0], tm)

    def body(r, f, o, acc, s):
        y0 = jnp.concatenate([_dot(_state_set(r[0], tm, j), f[0][j]) for j in range(N_SETS)], axis=1)
        y = y0 + f[1][...] * r[1][...]
        o[0][...] = y
        o[1][...] = _gelu(y)[0].astype(BF16)

    (y, yg), _ = _rows_call(name, tm, [xs, u], [cset, d], [(C_WIDTH, F32), (C_WIDTH, BF16)], [], body)
    return y, yg


def _state_grad_sets(a, st, *, name, ts=256):
    ts = _tile(a.shape[0], ts)

    def body(r, f, o, acc, s):
        for j in range(N_SETS):
            acc[0][j] += _dot_tn(r[0][:, _cols(j, SET_CH)].astype(BF16), _state_set(r[1], ts, j))

    _, (out,) = _rows_call(name, ts, [a, st], [], [], [((N_SETS, SET_CH, 2 * N_STATE // N_SETS), F32)], body)
    return out


def _glu_out(yg, ws, x, *, name, tm=512):
    n = ws[0][1][2]

    def body(r, f, o, acc, s):
        ygv = r[0][...]
        ov = [_dot(ygv, f[p][...]) for p in range(N_CHIPS)]
        for p in range(N_CHIPS):
            o[0][:, _cols(p, n)] = ov[p].astype(BF16)
        for p in range(2):
            o[1][:, _cols(p, n)] = r[1][:, _cols(p, n)] + ov[p] * _sigmoid(ov[2 + p])

    (o_, y), _ = _rows_call(name, _tile(x.shape[0], tm), [yg, x], ws, [(2 * D_MODEL, BF16), (D_MODEL, F32)], [], body)
    return o_, y


def _glu_out_bwd(o_, dy, ws, y, u, d, *, name, tm=256):
    n = ws[0][1][2]

    def body(r, f, o, acc, s):
        o1 = r[0][:, 0:D_MODEL].astype(F32)
        sg = _sigmoid(r[0][:, D_MODEL:2 * D_MODEL].astype(F32))
        dyv = r[1][...]
        do1 = (dyv * sg).astype(BF16)
        do2 = (dyv * o1 * sg * (1.0 - sg)).astype(BF16)
        o[0][:, 0:D_MODEL] = do1
        o[0][:, D_MODEL:2 * D_MODEL] = do2
        dyg = None
        for p in range(N_CHIPS):
            t = _dot_nt((do1 if p < 2 else do2)[:, _cols(p % 2, n)], f[1 + p][...])
            dyg = t if dyg is None else dyg + t
        yv = r[2][...]
        dys = dyg * _gelu_grad(yv, _gelu(yv)[1])
        o[1][...] = dys.astype(BF16)
        o[2][...] = f[0][...] * dys
        acc[0][...] += jnp.sum(dys * r[3][...], axis=0, keepdims=True)

    (do, dys, dus), (dd,) = _rows_call(name, _tile(dy.shape[0], tm), [o_, dy, y, u], [d] + ws,
                                       [(2 * D_MODEL, BF16), (C_WIDTH, BF16), (C_WIDTH, F32)], [((1, C_WIDTH), F32)],
                                       body)
    return do, dys, dus, dd


def _s5_in_bwd(gs, bset, dus, ws, x, g, dres, *, name, tm=256):
    D = x.shape[1]
    tm = _tile(x.shape[0], tm)

    def body(r, f, o, acc, s):
        du0 = jnp.concatenate([_dot_nt(_state_set(r[0], tm, j), f[1][j]) for j in range(N_SETS)], axis=1)
        du = (du0 + r[1][...]).astype(BF16)
        o[0][...] = du
        dx, dg = _rms_bwd_tile(r[2][...], f[0][...], _cat_nt(du, f[2:]))
        o[1][...] = dx + r[3][...]
        acc[0][...] += dg

    (du, dx), (dg,) = _rows_call(name, tm, [gs, dus, x, dres], [_vec(g), bset] + ws,
                                 [(C_WIDTH, BF16), (D, F32)], [((1, D), F32)], body)
    return du, dx, dg


_SCAN_CHUNK = 256
_RE = slice(0, STATE_ROWS)
_IM = slice(STATE_ROWS, 2 * STATE_ROWS)
assert SCAN_BLOCK == 8


def _token(g, i, rows):
    return pl.ds(pl.multiple_of(g * (rows * SCAN_BLOCK), rows * SCAN_BLOCK) + i, rows, stride=SCAN_BLOCK)


def _fill_chunk(s3, a_ref, wset, tc, nt):
    for j in range(N_SETS):
        av = a_ref[:, _cols(j, SET_CH)].astype(BF16)
        y = _dot_nt(av, wset[j]) if nt else _dot(av, wset[j])
        for k, c in enumerate(_set_groups(j)):
            s3[:, 8 * c:8 * (c + 1), :] = y[:, _cols(k, STATE_LANES)].reshape(tc // 8, 8, STATE_LANES)


def _chunk_token(s3, g, i):
    return s3[g, pl.ds(i, _STATE_TILE, stride=SCAN_BLOCK), :]


def _scan_fwd(u, bset, pw, *, name):
    S = u.shape[0]
    tc = _tile(S, _SCAN_CHUNK, 8)

    def body(u_ref, bset_ref, pw_ref, xs_ref, st_ref, s3):
        @pl.when(pl.program_id(0) == 0)
        def _():
            st_ref[...] = jnp.zeros_like(st_ref)

        _fill_chunk(s3, u_ref, bset_ref, tc, nt=False)
        ar = pw_ref[0, _RE, :]
        ai = pw_ref[0, _IM, :]

        def block(g, carry):
            xr, xi = carry
            cr = ci = nr = ni = None
            for j in range(SCAN_BLOCK):
                b = _chunk_token(s3, g, j)
                br, bi = b[_RE], b[_IM]
                cr, ci = (br, bi) if j == 0 else (ar * cr - ai * ci + br, ar * ci + ai * cr + bi)
                pr, pi = pw_ref[j, _RE, :], pw_ref[j, _IM, :]
                nr = pr * xr - pi * xi + cr
                ni = pr * xi + pi * xr + ci
                xs_ref[_token(g, j, STATE_ROWS), :] = _pack_state(nr, ni)
            return nr, ni

        xr, xi = lax.fori_loop(0, tc // SCAN_BLOCK, block, (st_ref[_RE, :], st_ref[_IM, :]), unroll=4)
        st_ref[_RE, :] = xr
        st_ref[_IM, :] = xi

    return pl.pallas_call(
        body, name=name, grid=(S // tc,),
        in_specs=[pl.BlockSpec((tc, u.shape[1]), lambda i: (i, 0)), pl.BlockSpec(bset.shape, lambda i: (0, 0, 0)),
                  pl.BlockSpec(pw.shape, lambda i: (0, 0, 0))],
        out_specs=pl.BlockSpec((tc * STATE_ROWS, STATE_LANES), lambda i: (i, 0)),
        out_shape=jax.ShapeDtypeStruct((S * STATE_ROWS, STATE_LANES), jnp.uint32),
        scratch_shapes=[pltpu.VMEM((2 * STATE_ROWS, STATE_LANES), F32),
                        pltpu.VMEM((tc // 8, _STATE_TILE * 8, STATE_LANES), F32)],
        compiler_params=_params(("arbitrary",)),
    )(u, bset, pw)


def _scan_bwd(dys, cset, xs, pw, *, name):
    S = dys.shape[0]
    tc = _tile(S, _SCAN_CHUNK, 8)
    nc = S // tc

    def body(dys_ref, cset_ref, xs_ref, pw_ref, g_ref, da_ref, st_ref, s3):
        @pl.when(pl.program_id(0) == 0)
        def _():
            st_ref[...] = jnp.zeros_like(st_ref)
            da_ref[...] = jnp.zeros_like(da_ref)

        _fill_chunk(s3, dys_ref, cset_ref, tc, nt=True)
        ar = pw_ref[0, _RE, :]
        ai = pw_ref[0, _IM, :]

        def block(k, carry):
            gr, gi, dar, dai = carry
            g = tc // SCAN_BLOCK - 1 - k
            cr = ci = None
            pgr, pgi = gr, gi
            for j in range(SCAN_BLOCK):
                i = SCAN_BLOCK - 1 - j
                xr, xi = _unpack_state(xs_ref[_token(g, i, STATE_ROWS), :])
                dar = dar + pgr * xr + pgi * xi
                dai = dai + pgi * xr - pgr * xi
                d = _chunk_token(s3, g, i)
                dr, di = d[_RE], d[_IM]
                cr, ci = (dr, di) if j == 0 else (ar * cr + ai * ci + dr, ar * ci - ai * cr + di)
                pr, pi = pw_ref[j, _RE, :], pw_ref[j, _IM, :]
                pgr = pr * gr + pi * gi + cr
                pgi = pr * gi - pi * gr + ci
                g_ref[_token(g, i, STATE_ROWS), :] = _pack_state(pgr, pgi)
            return pgr, pgi, dar, dai

        init = (st_ref[_RE, :], st_ref[_IM, :], da_ref[_RE, :], da_ref[_IM, :])
        gr, gi, dar, dai = lax.fori_loop(0, tc // SCAN_BLOCK, block, init, unroll=4)
        st_ref[_RE, :] = gr
        st_ref[_IM, :] = gi
        da_ref[_RE, :] = dar
        da_ref[_IM, :] = dai

    packed = pl.BlockSpec((tc * STATE_ROWS, STATE_LANES), lambda i: (nc - 1 - i, 0))
    vec = pl.BlockSpec((2 * STATE_ROWS, STATE_LANES), lambda i: (0, 0))
    return pl.pallas_call(
        body, name=name, grid=(nc,),
        in_specs=[pl.BlockSpec((tc, dys.shape[1]), lambda i: (nc - 1 - i, 0)),
                  pl.BlockSpec(cset.shape, lambda i: (0, 0, 0)), packed, pl.BlockSpec(pw.shape, lambda i: (0, 0, 0))],
        out_specs=[packed, vec],
        out_shape=[jax.ShapeDtypeStruct(xs.shape, jnp.uint32), jax.ShapeDtypeStruct((2 * STATE_ROWS, STATE_LANES), F32)],
        scratch_shapes=[pltpu.VMEM((2 * STATE_ROWS, STATE_LANES), F32),
                        pltpu.VMEM((tc // 8, _STATE_TILE * 8, STATE_LANES), F32)],
        compiler_params=_params(("arbitrary",)),
    )(dys, cset, xs, pw)


def _down_loss_head(h, ws, x, g, target, *, name, tm=512):
    S, D = x.shape

    def body(r, f, o, acc, s):
        xv = r[1][...]
        for p in range(N_CHIPS):
            xv = xv + _dot(r[0][p], f[1 + p][...])
        gv = f[0][...]
        rs = lax.rsqrt(_mean(xv * xv) + EPS)
        xh = xv * rs
        err = xh * gv - r[2][...]
        acc[1][...] += 0.5 * jnp.sum(_mean(err * err), axis=0, keepdims=True)
        dy = err * (1.0 / D)
        dyg = dy * gv
        o[0][...] = rs * (dyg - xh * _mean(dyg * xh))
        acc[0][...] += jnp.sum(dy * xh, axis=0, keepdims=True)

    (dx,), (dg, loss) = _rows_call(name, _tile(S, tm), [h, x, target], [_vec(g)] + ws, [(D, F32)],
                                   [((1, D), F32), ((1, 128), F32)], body)
    return dx, dg, loss


_ADAM_C1 = 1.0 - ADAM_B1 ** ADAM_STEP
_ADAM_C2 = 1.0 - ADAM_B2 ** ADAM_STEP
_ONE_BLOCK_BYTES = 8 * 1024 * 1024


def _adamw_math(w, g, m, v):
    nm = ADAM_B1 * m + (1.0 - ADAM_B1) * g
    nv = ADAM_B2 * v + (1.0 - ADAM_B2) * (g * g)
    m_hat = nm / _ADAM_C1
    v_hat = nv / _ADAM_C2
    return -ADAM_LR * (m_hat / (jnp.sqrt(v_hat) + ADAM_EPS) + ADAM_WD * w), nm, nv


def _adamw_shard(w, gsrc, m, v, *, name):
    R, C = w.shape
    n_l = len(gsrc)
    rows = R // n_l
    tr = rows
    for _, r0 in gsrc:
        tr = math.gcd(tr, r0) if r0 else tr
    tr = _tile(tr, 256, 8) if tr > 256 else tr
    nb = rows // tr
    assert rows % tr == 0 and all(r0 % tr == 0 for _, r0 in gsrc)

    def body(*refs):
        w_ref, g_refs, (m_ref, v_ref, go_ref, d_ref, nm_ref, nv_ref) = refs[0], refs[1:1 + n_l], refs[1 + n_l:]
        layer = pl.program_id(0) // nb
        gv = g_refs[0][...]
        for l in range(1, n_l):
            gv = jnp.where(layer == l, g_refs[l][...], gv)
        go_ref[...] = gv
        d_ref[...], nm_ref[...], nv_ref[...] = _adamw_math(w_ref[...], gv, m_ref[...], v_ref[...])

    def g_spec(l, r0):
        return pl.BlockSpec((tr, C), lambda i: (r0 // tr + jnp.clip(i - l * nb, 0, nb - 1), 0))

    blk = pl.BlockSpec((tr, C), lambda i: (i, 0))
    out = jax.ShapeDtypeStruct((R, C), F32)
    return pl.pallas_call(
        body, name=name, grid=(R // tr,),
        in_specs=[blk] + [g_spec(l, r0) for l, (_, r0) in enumerate(gsrc)] + [blk, blk], out_specs=[blk] * 4,
        out_shape=[out] * 4, compiler_params=_params(("parallel",)),
    )(w, *[g for g, _ in gsrc], m, v)


def _adamw_small(ws, gs, ms, vs, *, name):
    n = len(ws)

    def body(*refs):
        w_r, g_r, m_r, v_r = refs[:n], refs[n:2 * n], refs[2 * n:3 * n], refs[3 * n:4 * n]
        d_r, nm_r, nv_r = refs[4 * n:5 * n], refs[5 * n:6 * n], refs[6 * n:7 * n]
        for k in range(n):
            d_r[k][...], nm_r[k][...], nv_r[k][...] = _adamw_math(w_r[k][...], g_r[k][...], m_r[k][...], v_r[k][...])

    vm = pl.BlockSpec(memory_space=pltpu.VMEM)
    out = [jax.ShapeDtypeStruct(w.shape, F32) for w in ws]
    res = pl.pallas_call(body, name=name, in_specs=[vm] * (4 * n), out_specs=[vm] * (3 * n), out_shape=out * 3,
                         compiler_params=pltpu.CompilerParams(vmem_limit_bytes=VMEM_LIMIT))(*ws, *gs, *ms, *vs)
    return res[:n], res[n:2 * n], res[2 * n:]


def _sum_slots(x, *, name):
    n, R, C = x.shape
    tr = R if (n + 1) * R * C * 4 <= _ONE_BLOCK_BYTES else _tile(R, 256, 8)

    def body(x_ref, o_ref):
        acc = x_ref[0]
        for k in range(1, n):
            acc = acc + x_ref[k]
        o_ref[...] = acc

    return pl.pallas_call(
        body, name=name, grid=(R // tr,),
        in_specs=[pl.BlockSpec((n, tr, C), lambda i: (0, i, 0))], out_specs=pl.BlockSpec((tr, C), lambda i: (i, 0)),
        out_shape=jax.ShapeDtypeStruct((R, C), F32), compiler_params=_params(("parallel",)),
    )(x)


def _pair_sum(g, r, where, *, name):
    n, R, C = g.shape
    Rh = R // 2
    tr = _tile(Rh, 256, 8)
    nb = Rh // tr

    def body(where_ref, g_ref, r_ref, o_ref):
        o_ref[...] = (g_ref[...] + r_ref[...]).astype(BF16)

    def slot(p, w):
        return p + jnp.where(p >= w[0], 1, 0)

    return pl.pallas_call(
        body, name=name,
        grid_spec=pltpu.PrefetchScalarGridSpec(
            num_scalar_prefetch=1, grid=(n - 1, nb),
            in_specs=[pl.BlockSpec((1, tr, C), lambda p, i, w: (slot(p, w), w[1] * nb + i, 0)),
                      pl.BlockSpec((1, tr, C), lambda p, i, w: (slot(p, w), i, 0))],
            out_specs=pl.BlockSpec((1, tr, C), lambda p, i, w: (slot(p, w), i, 0)),
        ),
        out_shape=jax.ShapeDtypeStruct((n, Rh, C), BF16), compiler_params=_params(("parallel", "parallel")),
    )(where, g, r)


def _chip_sum(g, r, slots, where, *, name):
    n, R, C = g.shape
    Rh = R // 2
    tr = _tile(Rh, 256, 8)
    nb = Rh // tr

    def body(w_ref, g_ref, r_ref, s_ref, o_ref):
        acc = g_ref[0] + r_ref[0]
        for k in range(slots.shape[0]):
            acc = acc + s_ref[k].astype(F32)
        o_ref[...] = acc

    return pl.pallas_call(
        body, name=name,
        grid_spec=pltpu.PrefetchScalarGridSpec(
            num_scalar_prefetch=1, grid=(nb,),
            in_specs=[pl.BlockSpec((1, tr, C), lambda i, w: (w[0], w[1] * nb + i, 0)),
                      pl.BlockSpec((1, tr, C), lambda i, w: (w[0], i, 0)),
                      pl.BlockSpec((slots.shape[0], tr, C), lambda i, w: (0, i, 0))],
            out_specs=pl.BlockSpec((tr, C), lambda i, w: (w[1] * nb + i, 0)),
        ),
        out_shape=jax.ShapeDtypeStruct((R, C), F32), compiler_params=_params(("parallel",)),
    )(where, g, r, slots)


ANY = pl.BlockSpec(memory_space=pl.ANY)


def _place():
    return lax.axis_index("x"), lax.axis_index("y"), lax.axis_index("c")


def _other_chips(x, y):
    return [(1 - x, y), (x, 1 - y), (1 - x, 1 - y)]


def _aliased_comm_call(body, bufs, n_sems, *, name):
    n = len(bufs)
    return pl.pallas_call(
        body, name=name, out_shape=[jax.ShapeDtypeStruct(b.shape, b.dtype) for b in bufs],
        in_specs=[ANY] * n, out_specs=[ANY] * n, input_output_aliases={k: k for k in range(n)},
        scratch_shapes=[pltpu.SemaphoreType.DMA((n_sems,)), pltpu.SemaphoreType.DMA((n_sems,))],
    )(*bufs)


HBM = pl.BlockSpec(memory_space=pltpu.HBM)
SEM = pl.BlockSpec(memory_space=pltpu.SEMAPHORE)
_SPLIT = pltpu.CompilerParams(has_side_effects=pltpu.SideEffectType.DATAFLOW_SIDE_EFFECTING)


def _in_hbm(arrs):
    return [pltpu.with_memory_space_constraint(a, pltpu.HBM) for a in arrs]


def _gather_ici_start(bufs, after, *, name):
    n = len(bufs)

    def body(*refs):
        send_sems, recv_sems, outs, token = refs[n + 1], refs[n + 2], refs[n + 3:2 * n + 3], refs[2 * n + 3]
        x, y, c = _place()
        for b in range(n):
            rh = bufs[b].shape[1] // 2
            part = outs[b].at[2 * x + y, pl.ds(c * rh, rh), :]
            for j, chip in enumerate(_other_chips(x, y)):
                pltpu.make_async_remote_copy(src_ref=part, dst_ref=part, send_sem=send_sems.at[3 * b + j],
                                             recv_sem=recv_sems.at[3 * b + j], device_id=(*chip, c),
                                             device_id_type=MESH).start()
        token[...] = jnp.zeros_like(token)

    res = pl.pallas_call(
        body, name=name,
        out_shape=(pltpu.SemaphoreType.DMA((3 * n,)), pltpu.SemaphoreType.DMA((3 * n,)),
                   *[pltpu.HBM(b.shape, b.dtype) for b in bufs], jax.ShapeDtypeStruct((8, 128), F32)),
        in_specs=[HBM] * n + [ANY], out_specs=(SEM, SEM, *[HBM] * n, pl.BlockSpec(memory_space=pltpu.VMEM)),
        input_output_aliases={k: k + 2 for k in range(n)}, compiler_params=_SPLIT,
    )(*_in_hbm(bufs), after)
    return res[0], res[1], list(res[2:2 + n]), res[2 + n]


def _gather_ici_wait(send_sems, recv_sems, bufs, after, *, name):
    n = len(bufs)

    def body(*refs):
        ins, ss, rs = refs[:n], refs[n], refs[n + 1]
        x, y, c = _place()
        for b in range(n):
            rh = bufs[b].shape[1] // 2
            mine = ins[b].at[2 * x + y, pl.ds(c * rh, rh), :]
            for j, (cx, cy) in enumerate(_other_chips(x, y)):
                theirs = ins[b].at[2 * cx + cy, pl.ds(c * rh, rh), :]
                cp = pltpu.make_async_remote_copy(src_ref=mine, dst_ref=theirs, send_sem=ss.at[3 * b + j],
                                                  recv_sem=rs.at[3 * b + j], device_id=(cx, cy, c),
                                                  device_id_type=MESH)
                cp.wait_send()
                cp.wait_recv()

    return list(pl.pallas_call(
        body, name=name, out_shape=[pltpu.HBM(b.shape, b.dtype) for b in bufs],
        in_specs=[HBM] * n + [SEM, SEM, ANY], out_specs=[HBM] * n,
        input_output_aliases={k: k for k in range(n)}, compiler_params=_SPLIT,
    )(*bufs, send_sems, recv_sems, after))


def _gather_forward(bufs, *, name):
    n = len(bufs)

    def body(*refs):
        outs, send_sems, recv_sems = refs[n:2 * n], refs[2 * n], refs[2 * n + 1]
        x, y, c = _place()

        def copy(b, j, chip, hc):
            rh = bufs[b].shape[1] // 2
            part = outs[b].at[2 * chip[0] + chip[1], pl.ds(hc * rh, rh), :]
            return pltpu.make_async_remote_copy(src_ref=part, dst_ref=part, send_sem=send_sems.at[3 * b + j],
                                                recv_sem=recv_sems.at[3 * b + j], device_id=(x, y, 1 - c),
                                                device_id_type=MESH)

        sends = [copy(b, j, chip, c) for b in range(n) for j, chip in enumerate(_other_chips(x, y))]
        for cp in sends:
            cp.start()
        for b in range(n):
            for j, chip in enumerate(_other_chips(x, y)):
                copy(b, j, chip, 1 - c).wait_recv()
        for cp in sends:
            cp.wait_send()

    return _aliased_comm_call(body, bufs, 3 * n, name=name)


def _chip_exchange_start(hs, *, name):
    n = len(hs)
    lands = [lax.empty((3,) + h.shape[1:], h.dtype) for h in hs]

    def body(*refs):
        send_sems, recv_sems = refs[2 * n], refs[2 * n + 1]
        h_out, l_out, token = refs[2 * n + 2:3 * n + 2], refs[3 * n + 2:4 * n + 2], refs[4 * n + 2]
        x, y, c = _place()
        for b in range(n):
            for j, (cx, cy) in enumerate(_other_chips(x, y)):
                pltpu.make_async_remote_copy(src_ref=h_out[b].at[2 * cx + cy], dst_ref=l_out[b].at[j],
                                             send_sem=send_sems.at[3 * b + j], recv_sem=recv_sems.at[3 * b + j],
                                             device_id=(cx, cy, c), device_id_type=MESH).start()
        token[...] = jnp.zeros_like(token)

    res = pl.pallas_call(
        body, name=name,
        out_shape=(pltpu.SemaphoreType.DMA((3 * n,)), pltpu.SemaphoreType.DMA((3 * n,)),
                   *[pltpu.HBM(a.shape, a.dtype) for a in hs + lands], jax.ShapeDtypeStruct((8, 128), F32)),
        in_specs=[HBM] * (2 * n), out_specs=(SEM, SEM, *[HBM] * (2 * n), pl.BlockSpec(memory_space=pltpu.VMEM)),
        input_output_aliases={k: k + 2 for k in range(2 * n)}, compiler_params=_SPLIT,
    )(*_in_hbm(hs + lands))
    return res[0], res[1], list(res[2:2 + n]), list(res[2 + n:2 + 2 * n]), res[2 + 2 * n]


def _chip_exchange_wait(send_sems, recv_sems, hs, lands, after, *, name):
    n = len(hs)

    def body(*refs):
        h_in, l_in, ss, rs = refs[:n], refs[n:2 * n], refs[2 * n], refs[2 * n + 1]
        x, y, c = _place()
        for b in range(n):
            for j, (cx, cy) in enumerate(_other_chips(x, y)):
                cp = pltpu.make_async_remote_copy(src_ref=h_in[b].at[2 * cx + cy], dst_ref=l_in[b].at[j],
                                                  send_sem=ss.at[3 * b + j], recv_sem=rs.at[3 * b + j],
                                                  device_id=(cx, cy, c), device_id_type=MESH)
                cp.wait_send()
                cp.wait_recv()

    res = pl.pallas_call(
        body, name=name, out_shape=[pltpu.HBM(a.shape, a.dtype) for a in hs + lands],
        in_specs=[HBM] * (2 * n) + [SEM, SEM, ANY], out_specs=[HBM] * (2 * n),
        input_output_aliases={k: k for k in range(2 * n)}, compiler_params=_SPLIT,
    )(*hs, *lands, send_sems, recv_sems, after)
    return list(res[n:])


def _peers(x, y, c):
    return [((1 - x) if fx else x, (1 - y) if fy else y, (1 - c) if fc else c)
            for fx in (0, 1) for fy in (0, 1) for fc in (0, 1) if fx or fy or fc]


def _all_to_all_start(slab, after, *, name):
    land = lax.empty((N_DEV,) + slab.shape, slab.dtype)

    def body(slab_in, land_in, after_ref, send_sems, recv_sems, slab_out, land_out, token):
        x, y, c = _place()
        for k, peer in enumerate(_peers(x, y, c)):
            pltpu.make_async_remote_copy(src_ref=slab_out, dst_ref=land_out.at[4 * x + 2 * y + c],
                                         send_sem=send_sems.at[k], recv_sem=recv_sems.at[k], device_id=peer,
                                         device_id_type=MESH).start()
        token[...] = jnp.zeros_like(token)

    return pl.pallas_call(
        body, name=name,
        out_shape=(pltpu.SemaphoreType.DMA((N_DEV - 1,)), pltpu.SemaphoreType.DMA((N_DEV - 1,)),
                   pltpu.HBM(slab.shape, slab.dtype), pltpu.HBM(land.shape, land.dtype),
                   jax.ShapeDtypeStruct((8, 128), F32)),
        in_specs=[HBM, HBM, ANY], out_specs=(SEM, SEM, HBM, HBM, pl.BlockSpec(memory_space=pltpu.VMEM)),
        input_output_aliases={0: 2, 1: 3}, compiler_params=_SPLIT,
    )(*_in_hbm([slab, land]), after)


def _all_to_all_wait(send_sems, recv_sems, slab, land, after, *, name):
    def body(slab_in, land_in, ss, rs, after_ref, slab_out, land_out):
        x, y, c = _place()
        for k, (px, py, pc) in enumerate(_peers(x, y, c)):
            cp = pltpu.make_async_remote_copy(src_ref=slab_in, dst_ref=land_in.at[4 * px + 2 * py + pc],
                                              send_sem=ss.at[k], recv_sem=rs.at[k], device_id=(px, py, pc),
                                              device_id_type=MESH)
            cp.wait_send()
            cp.wait_recv()

    return pl.pallas_call(
        body, name=name, out_shape=[pltpu.HBM(slab.shape, slab.dtype), pltpu.HBM(land.shape, land.dtype)],
        in_specs=[HBM, HBM, SEM, SEM, ANY], out_specs=[HBM, HBM], input_output_aliases={0: 0, 1: 1},
        compiler_params=_SPLIT,
    )(slab, land, send_sems, recv_sems, after)


def _pair_exchange_start(gs, *, name):
    n = len(gs)
    lands = [lax.empty((g.shape[0], g.shape[1] // 2, g.shape[2]), g.dtype) for g in gs]

    def body(*refs):
        send_sems, recv_sems = refs[2 * n], refs[2 * n + 1]
        g_out, l_out, token = refs[2 * n + 2:3 * n + 2], refs[3 * n + 2:4 * n + 2], refs[4 * n + 2]
        x, y, c = _place()
        for b in range(n):
            rh = gs[b].shape[1] // 2
            pltpu.make_async_remote_copy(src_ref=g_out[b].at[:, pl.ds((1 - c) * rh, rh), :], dst_ref=l_out[b],
                                         send_sem=send_sems.at[b], recv_sem=recv_sems.at[b],
                                         device_id=(x, y, 1 - c), device_id_type=MESH).start()
        token[...] = jnp.zeros_like(token)

    res = pl.pallas_call(
        body, name=name,
        out_shape=(pltpu.SemaphoreType.DMA((n,)), pltpu.SemaphoreType.DMA((n,)),
                   *[pltpu.HBM(a.shape, a.dtype) for a in gs + lands], jax.ShapeDtypeStruct((8, 128), F32)),
        in_specs=[HBM] * (2 * n), out_specs=(SEM, SEM, *[HBM] * (2 * n), pl.BlockSpec(memory_space=pltpu.VMEM)),
        input_output_aliases={k: k + 2 for k in range(2 * n)}, compiler_params=_SPLIT,
    )(*_in_hbm(gs + lands))
    return res[0], res[1], list(res[2:2 + n]), list(res[2 + n:2 + 2 * n]), res[2 + 2 * n]


def _pair_exchange_wait(send_sems, recv_sems, gs, lands, after, *, name):
    n = len(gs)

    def body(*refs):
        g_in, l_in, ss, rs = refs[:n], refs[n:2 * n], refs[2 * n], refs[2 * n + 1]
        x, y, c = _place()
        for b in range(n):
            rh = gs[b].shape[1] // 2
            cp = pltpu.make_async_remote_copy(src_ref=g_in[b].at[:, pl.ds((1 - c) * rh, rh), :], dst_ref=l_in[b],
                                              send_sem=ss.at[b], recv_sem=rs.at[b], device_id=(x, y, 1 - c),
                                              device_id_type=MESH)
            cp.wait_send()
            cp.wait_recv()

    res = pl.pallas_call(
        body, name=name, out_shape=[pltpu.HBM(a.shape, a.dtype) for a in gs + lands],
        in_specs=[HBM] * (2 * n) + [SEM, SEM, ANY], out_specs=[HBM] * (2 * n),
        input_output_aliases={k: k for k in range(2 * n)}, compiler_params=_SPLIT,
    )(*gs, *lands, send_sems, recv_sems, after)
    return list(res[:n]), list(res[n:])


def _pair_share_start(ss, after, *, name):
    n = len(ss)

    def body(*refs):
        send_sems, recv_sems, outs, token = refs[n + 1], refs[n + 2], refs[n + 3:2 * n + 3], refs[2 * n + 3]
        x, y, c = _place()
        for b in range(n):
            rh = ss[b].shape[0] // 2
            mine = outs[b].at[pl.ds(c * rh, rh), :]
            pltpu.make_async_remote_copy(src_ref=mine, dst_ref=mine, send_sem=send_sems.at[b],
                                         recv_sem=recv_sems.at[b], device_id=(x, y, 1 - c),
                                         device_id_type=MESH).start()
        token[...] = jnp.zeros_like(token)

    res = pl.pallas_call(
        body, name=name,
        out_shape=(pltpu.SemaphoreType.DMA((n,)), pltpu.SemaphoreType.DMA((n,)),
                   *[pltpu.HBM(a.shape, a.dtype) for a in ss], jax.ShapeDtypeStruct((8, 128), F32)),
        in_specs=[HBM] * n + [ANY], out_specs=(SEM, SEM, *[HBM] * n, pl.BlockSpec(memory_space=pltpu.VMEM)),
        input_output_aliases={k: k + 2 for k in range(n)}, compiler_params=_SPLIT,
    )(*_in_hbm(ss), after)
    return res[0], res[1], list(res[2:2 + n]), res[2 + n]


def _pair_share_wait(send_sems, recv_sems, ss, after, *, name):
    n = len(ss)

    def body(*refs):
        ins, sems_s, sems_r = refs[:n], refs[n], refs[n + 1]
        x, y, c = _place()
        for b in range(n):
            rh = ss[b].shape[0] // 2
            mine = ins[b].at[pl.ds(c * rh, rh), :]
            theirs = ins[b].at[pl.ds((1 - c) * rh, rh), :]
            cp = pltpu.make_async_remote_copy(src_ref=mine, dst_ref=theirs, send_sem=sems_s.at[b],
                                              recv_sem=sems_r.at[b], device_id=(x, y, 1 - c),
                                              device_id_type=MESH)
            cp.wait_send()
            cp.wait_recv()

    return list(pl.pallas_call(
        body, name=name, out_shape=[pltpu.HBM(a.shape, a.dtype) for a in ss],
        in_specs=[HBM] * n + [SEM, SEM, ANY], out_specs=[HBM] * n,
        input_output_aliases={k: k for k in range(n)}, compiler_params=_SPLIT,
    )(*ss, send_sems, recv_sems, after))


_SMALL_SHARDED = (("e_conv_w", 2), ("o_norm", 1), ("o_d", 1))
_REPLICATED = ("e_norm", "e_gmlp_w", "e_gmlp_b", "e_conv_b", "e_conv_ln_g", "e_conv_ln_b", "o_lam_re", "o_lam_im",
               "o_log_dt", "o_b_re", "o_b_im", "o_c_re", "o_c_im", "ca_norm", "ca_mem_norm", "ffn_norm", "final_norm")
_SMALL = tuple(n for n, _ in _SMALL_SHARDED) + _REPLICATED
_WEIGHTS = ("e_norm", "e_w_in", "e_gmlp_w", "e_gmlp_b", "e_conv_w", "e_conv_b", "e_conv_ln_g", "e_conv_ln_b",
            "e_w_out", "o_norm", "o_w_in", "o_lam_re", "o_lam_im", "o_log_dt", "o_b_re", "o_b_im", "o_c_re", "o_c_im",
            "o_d", "o_w_out", "ca_norm", "ca_mem_norm", "ca_wq", "ca_wk", "ca_wv", "ca_wo", "ffn_norm", "ffn_w_gate",
            "ffn_w_up", "ffn_w_down", "final_norm")


def _pack_rows(arrs, width, dtype, row_mult=8):
    parts, spans, r0 = [], [], 0
    for a in arrs:
        flat = a.reshape(-1).astype(dtype)
        rows = -(-flat.shape[0] // (width * row_mult)) * row_mult
        if rows * width != flat.shape[0]:
            flat = jnp.pad(flat, (0, rows * width - flat.shape[0]))
        parts.append(flat.reshape(rows, width))
        spans.append((r0, rows))
        r0 += rows
    return jnp.concatenate(parts, axis=0), spans


def _unpack_rows(slab, spans, shapes):
    out = []
    for (r0, rows), shp in zip(spans, shapes):
        n = math.prod(shp)
        out.append(slab[r0:r0 + rows].reshape(-1)[:n].reshape(shp))
    return out


def _two_d(a):
    return a.reshape(-1, a.shape[-1])


def _shard_rows(n, a):
    return _two_d(jnp.swapaxes(a, -1, -2) if n in _TRANSPOSED else a)


def _from_shard_rows(n, rows, shape):
    if n in _TRANSPOSED:
        return jnp.swapaxes(rows.reshape(shape[:-2] + (shape[-1], shape[-2])), -1, -2)
    return rows.reshape(shape)


def _local_slab(local, slab, dtype):
    parts = sorted((r0, n, l) for n, (_, where) in _PLACE.items() for l, (s, r0) in enumerate(where) if s == slab)
    shards = [_shard_rows(n, local[n] if len(_PLACE[n][1]) == 1 else local[n][l]) for _, n, l in parts]
    return jnp.concatenate([a.astype(dtype) for a in shards], axis=0)


def _set_diag(b, pattern):
    return jnp.einsum(pattern, b, jnp.eye(C_GROUPS // N_SETS, dtype=b.dtype))


def _s5_discretize(lam_re, lam_im, log_dt, b_re, b_im):
    dt = jnp.exp(log_dt)[:, None]
    mag = jnp.exp(lam_re * dt)
    ar = mag * jnp.cos(lam_im * dt)
    ai = mag * jnp.sin(lam_im * dt)
    den = lam_re * lam_re + lam_im * lam_im
    qr = ((ar - 1.0) * lam_re + ai * lam_im) / den
    qi = (ai * lam_re - (ar - 1.0) * lam_im) / den
    bbr = qr[..., None] * b_re - qi[..., None] * b_im
    bbi = qr[..., None] * b_im + qi[..., None] * b_re
    return ar, ai, bbr, bbi


def _attention_block(x, mem, W, w, i, tag):
    xn, q = _norm_mm(x, w["ca_norm"][i], _shards(W, "ca_wq", i), split="k", out_dtype=BF16, name=f"{tag}_q")
    memn = _rms_fwd(mem, w["ca_mem_norm"][i], name=f"{tag}_ca_memnorm")
    k = _mm_k(memn, _shards(W, "ca_wk", i), out_dtype=BF16, name=f"{tag}_k")
    v = _mm_k(memn, _shards(W, "ca_wv", i), out_dtype=BF16, name=f"{tag}_v")
    o = _attn_fwd(q, k, v, name=f"{tag}_attn")
    y = _mm_k(o, _shards(W, "ca_wo", i), add=x, name=f"{tag}_wo")
    return y, (x, xn, memn, q, k, v, o)


def _attention_block_bwd(dy, saved, mem, W, w, i, tag, G, grads, token=None, mid=None):
    x, xn, memn, q, k, v, o = saved
    gain = w["ca_norm"][i]
    if token is not None:
        k = _behind(k, token)
    G = _grad_to_slab(G, "ca_wo", i, o, dy, a_cols=256, name=f"{tag}_dwo")
    dq, dk, dv = _attn_bwd(dy, _shards(W, "ca_wo", i), q, k, v, name=f"{tag}_attn_bwd")
    token = mid(dq) if mid is not None else None
    if token is not None:
        gain = _behind(gain, token)
    G = _grad_to_slab(G, "ca_wq", i, xn, dq, a_cols=256, name=f"{tag}_dwq")
    G = _grad_to_slab(G, "ca_wk", i, memn, dk, a_cols=256, name=f"{tag}_dwk")
    G = _grad_to_slab(G, "ca_wv", i, memn, dv, a_cols=256, name=f"{tag}_dwv")
    dmemn = _mm_k_t([(dk, _shards(W, "ca_wk", i)), (dv, _shards(W, "ca_wv", i))], name=f"{tag}_dmemn")
    dx, dg = _norm_bwd_k(dq, _shards(W, "ca_wq", i), x, gain, dy, name=f"{tag}_dq_norm_bwd")
    grads["ca_norm"][i] = dg[0]
    grads["ca_mem_norm"][i] = _rms_dg(mem, w["ca_mem_norm"][i], dmemn, name=f"{tag}_ca_memnorm_bwd")[0]
    return dx, G


def _ffn_block(x, W, w, i, tag, head=None):
    fn, gate, up, h = _ffn_up(x, w["ffn_norm"][i], _shards(W, "ffn_w_gate", i), _shards(W, "ffn_w_up", i),
                              name=f"{tag}_ffn_up")
    if head is None:
        y = _mm_k(h, _shards(W, "ffn_w_down", i), add=x, name=f"{tag}_down")
    else:
        y = _down_loss_head(h, _shards(W, "ffn_w_down", i), x, *head, name=f"{tag}_down_loss_head")
    return y, (x, fn, gate, up, h)


def _ffn_block_bwd(dy, saved, W, w, i, tag, G, grads, token=None, mid=None):
    x, fn, gate, up, h = saved
    gain = w["ffn_norm"][i]
    G = _grad_to_slab(G, "ffn_w_down", i, h, dy, name=f"{tag}_dwd")
    dg, du = _ffn_bwd_hidden(dy, _shards(W, "ffn_w_down", i), gate, up, token, name=f"{tag}_ffn_bwd_hidden")
    token = mid(dg) if mid is not None else None
    if token is not None:
        gain = _behind(gain, token)
    G = _grad_to_slab(G, "ffn_w_gate", i, dg, fn, name=f"{tag}_dwg")
    G = _grad_to_slab(G, "ffn_w_up", i, du, fn, name=f"{tag}_dwu")
    dx, dgn = _ffn_in_bwd(dg, du, _shards(W, "ffn_w_gate", i), _shards(W, "ffn_w_up", i), x, gain, dy,
                          name=f"{tag}_ffn_in_bwd")
    grads["ffn_norm"][i] = dgn[0]
    return dx, G


def _gmlp_mask():
    chunk = jnp.arange(GMLP_BLOCK) // CHUNK
    return chunk[None, :] <= chunk[:, None]


def _even_block(x, W, w, tag):
    hn, proj = _norm_mm(x, w["e_norm"][0], _shards(W, "e_w_in"), split="n", out_dtype=F32, name=f"{tag}_w_in")
    wm = jnp.where(_gmlp_mask()[None], w["e_gmlp_w"][0], 0.0).astype(BF16)
    bcol = w["e_gmlp_b"][0][:, :, None]
    cw = jnp.pad(w["e_conv_w"][0], ((0, CONV_HALO - CONV_WIDTH), (0, 0)))
    cb, lg, lb = w["e_conv_b"], w["e_conv_ln_g"], w["e_conv_ln_b"]
    mix, hc = _even_fwd(proj, wm, bcol, cw, cb, lg, lb, name=f"{tag}_mixers")
    y = _mm_k(mix, _shards(W, "e_w_out"), add=x, name=f"{tag}_w_out")
    return y, (x, hn, proj, mix, hc, wm, bcol, cw)


def _even_block_bwd(dy, saved, W, w, tag, G, grads):
    x, hn, proj, mix, hc, wm, bcol, cw = saved
    dmix = _mm_k_t([(dy, _shards(W, "e_w_out"))], name=f"{tag}_dmix")
    G = _grad_to_slab(G, "e_w_out", 0, mix, dy, a_cols=256, name=f"{tag}_dw_out")
    wmt = jnp.swapaxes(wm, 1, 2)
    dpa, dhc, dwm, db, dlg, dlb, dcb = _even_bwd1(proj, dmix, hc, wm, wmt, bcol, w["e_conv_ln_g"], w["e_conv_ln_b"],
                                                  name=f"{tag}_mixers_bwd1")
    dpb, dcw = _even_bwd2(proj, dhc, cw, name=f"{tag}_mixers_bwd2")
    grads["e_gmlp_w"] = jnp.where(_gmlp_mask()[None], dwm, 0.0)[None]
    grads["e_gmlp_b"] = db[:, :, 0][None]
    grads["e_conv_ln_g"], grads["e_conv_ln_b"], grads["e_conv_b"] = dlg, dlb, dcb
    grads["e_conv_w"] = dcw[:CONV_WIDTH][None]
    G = _grad_to_slab(G, "e_w_in", 0, hn, dpa, b_cols=512, chips=(0, 2), name=f"{tag}_dw_in_a")
    G = _grad_to_slab(G, "e_w_in", 0, hn, dpb, b_cols=512, chips=(2, 2), name=f"{tag}_dw_in_b")
    dx, dg = _norm_bwd_n((dpa, dpb), _shards(W, "e_w_in"), x, w["e_norm"][0], dy, name=f"{tag}_in_bwd")
    grads["e_norm"] = dg
    return dx, G


def _odd_block(x, W, w, tag):
    S = x.shape[0]
    hn, u = _norm_mm(x, w["o_norm"][0], _shards(W, "o_w_in"), split="k", out_dtype=F32, name=f"{tag}_w_in")
    disc_in = (w["o_lam_re"][0], w["o_lam_im"][0], w["o_log_dt"][0], w["o_b_re"][0], w["o_b_im"][0])
    (ar, ai, bbr, bbi), disc_vjp = jax.vjp(_s5_discretize, *disc_in)
    sets = (N_SETS, C_GROUPS // N_SETS)
    per_set = N_STATE // N_SETS
    bset = jnp.concatenate([_set_diag(b.reshape(sets + b.shape[1:]), "jgpc,gh->jgchp").reshape(N_SETS, SET_CH, per_set)
                            for b in (bbr, bbi)], axis=2).astype(BF16)
    cset = jnp.concatenate([_set_diag(c.reshape(sets + c.shape[1:]), "jgcp,gh->jgphc").reshape(N_SETS, per_set, SET_CH)
                            for c in (w["o_c_re"][0], -w["o_c_im"][0])], axis=1).astype(BF16)
    powers, pr, pi = [], ar, ai
    for _ in range(SCAN_BLOCK):
        powers.append(jnp.concatenate([pr.reshape(STATE_ROWS, STATE_LANES), pi.reshape(STATE_ROWS, STATE_LANES)], 0))
        pr, pi = pr * ar - pi * ai, pr * ai + pi * ar
    pw = jnp.stack(powers, axis=0)
    xs = _scan_fwd(u, bset, pw, name=f"{tag}_scan").reshape(S // 8, STATE_ROWS, 8, STATE_LANES)
    yv, yg = _s5_readout(xs, cset, u, w["o_d"], name=f"{tag}_readout")
    o, y = _glu_out(yg, _shards(W, "o_w_out"), x, name=f"{tag}_glu_out")
    return y, (x, hn, u, bset, cset, pw, xs, yv, yg, o, disc_vjp)


def _odd_block_bwd(dy, saved, W, w, tag, G, grads):
    x, hn, u, bset, cset, pw, xs, yv, yg, o, disc_vjp = saved
    S = x.shape[0]
    do, dys, dus, dd = _glu_out_bwd(o, dy, _shards(W, "o_w_out"), yv, u, w["o_d"], name=f"{tag}_glu_out_bwd")
    G = _grad_to_slab(G, "o_w_out", 0, yg, do, b_cols=512, name=f"{tag}_dw_out")
    grads["o_d"] = dd
    dcset_t = _state_grad_sets(dys, xs, name=f"{tag}_dcd")
    gs, da = _scan_bwd(dys, cset, xs.reshape(S * STATE_ROWS, STATE_LANES), pw, name=f"{tag}_scan_bwd")
    gs = gs.reshape(xs.shape)
    dbset = _state_grad_sets(u, gs, name=f"{tag}_dbd")
    du, dx, dg = _s5_in_bwd(gs, bset, dus, _shards(W, "o_w_in"), x, w["o_norm"][0], dy, name=f"{tag}_in_bwd")
    G = _grad_to_slab(G, "o_w_in", 0, hn, du, a_cols=256, name=f"{tag}_dw_in")
    grads["o_norm"] = dg
    per = C_GROUPS // N_SETS
    blocks = (N_SETS, per, C_GROUP_CH, 2, per, C_STATE)
    dc = _set_diag(dcset_t.reshape(blocks), "jhcrgp,gh->rjgcp").reshape(2, C_GROUPS, C_GROUP_CH, C_STATE)
    db = _set_diag(dbset.reshape(blocks), "jgcrhp,gh->rjgpc").reshape(2, C_GROUPS, C_STATE, C_GROUP_CH)
    dcr, dci, dbbr, dbbi = dc[0], -dc[1], db[0], db[1]
    dar = da[:STATE_ROWS].reshape(C_GROUPS, C_STATE)
    dai = da[STATE_ROWS:].reshape(C_GROUPS, C_STATE)
    dlr, dli, dldt, dbr, dbi = disc_vjp((dar, dai, dbbr, dbbi))
    grads["o_lam_re"], grads["o_lam_im"], grads["o_log_dt"] = dlr[None], dli[None], dldt[None]
    grads["o_b_re"], grads["o_b_im"], grads["o_c_re"], grads["o_c_im"] = dbr[None], dbi[None], dcr[None], dci[None]
    return dx, G


def _behind(value, token):
    return value + token[0, 0].astype(value.dtype)


class _NoExchange:
    def __init__(self, W):
        self.W = W

    def first_weights(self, w):
        return self.W, w

    def weights(self, stage, after):
        return {}

    def grads_ready(self, piece, G):
        return None

    def grads_crossed(self, piece, after):
        return None


def _forward_backward(xs_, mems_, tgt, w, G, exchange):
    W, w = exchange.first_weights(w)
    x1, s_mix0 = _even_block(xs_, W, w, "l0")
    W = {**W, **exchange.weights(1, x1)}
    x2, s_att0 = _attention_block(x1, mems_, W, w, 0, "l0")
    W = {**W, **exchange.weights(2, x2)}
    x3, s_ffn0 = _ffn_block(x2, W, w, 0, "l0")
    W = {**W, **exchange.weights(3, x3)}
    x4, s_mix1 = _odd_block(x3, W, w, "l1")
    x5, s_att1 = _attention_block(x4, mems_, W, w, 1, "l1")
    (dx, dfinal, loss_lanes), s_ffn1 = _ffn_block(x5, W, w, 1, "l1", head=(w["final_norm"], tgt))

    grads = {n: [None, None] for n in ("ca_norm", "ca_mem_norm", "ffn_norm")}
    grads["final_norm"] = dfinal[0]
    dx, G = _ffn_block_bwd(dx, s_ffn1, W, w, 1, "l1", G, grads)
    dx, G = _attention_block_bwd(dx, s_att1, mems_, W, w, 1, "l1", G, grads)
    dx, G = _odd_block_bwd(dx, s_mix1, W, w, "l1", G, grads)
    token = exchange.grads_ready("l1", G)
    dx, G = _ffn_block_bwd(dx, s_ffn0, W, w, 0, "l0", G, grads, token,
                           lambda after: exchange.grads_crossed("l1", after))
    token = exchange.grads_ready("ffn0", G)
    dx, G = _attention_block_bwd(dx, s_att0, mems_, W, w, 0, "l0", G, grads, token,
                                 lambda after: exchange.grads_crossed("ffn0", after))
    dx, G = _even_block_bwd(dx, s_mix0, W, w, "l0", G, grads)
    for n in list(grads):
        if isinstance(grads[n], list):
            grads[n] = jnp.stack(grads[n], axis=0)
        grads[n] = grads[n].reshape(w[n].shape)
    return loss_lanes, dx, G, grads


class _Exchange:
    def __init__(self, local, chip, core):
        self.bufs = {s: lax.dynamic_update_slice(lax.empty((N_CHIPS, rows, width), BF16),
                                                 _local_slab(local, s, BF16)[None], (chip, 0, 0))
                     for s, (width, rows) in _SLABS.items()}
        small = jnp.zeros((_SMALL_SLAB_ROWS, SMALL_W), F32)
        for n, (r0, rows) in _SMALL_PLACE.items():
            small = small.at[r0:r0 + rows].set(local[n].reshape(rows, SMALL_W))
        self.bufs[_SMALL_SLAB] = lax.dynamic_update_slice(lax.empty((N_CHIPS, _SMALL_SLAB_ROWS, SMALL_W), F32),
                                                          small[None], (chip, 0, 0))
        self.shard_shapes = {n: local[n].shape for n in _SMALL_PLACE}
        self.where = jnp.stack([chip, core]).astype(jnp.int32)
        self.flights = []
        self.reduces = {}

    def weights(self, stage, after):
        send_sems, recv_sems, bufs, _ = self.flights[stage]
        bufs = _gather_ici_wait(send_sems, recv_sems, bufs, after, name=f"gather_stage{stage}_wait")
        return dict(zip(self.stage_slabs(stage), _gather_forward(bufs, name=f"gather_stage{stage}_forward")))

    @staticmethod
    def stage_slabs(stage):
        return _STAGES[stage] + ((_SMALL_SLAB,) if stage == 0 else ())

    def first_weights(self, w):
        after = w["e_norm"]
        for k in range(len(_STAGES)):
            self.flights.append(_gather_ici_start([self.bufs[s] for s in self.stage_slabs(k)], after,
                                                  name=f"gather_stage{k}_start"))
            after = self.flights[-1][3]
        W = self.weights(0, after)
        w = {**w, "e_norm": _behind(w["e_norm"], after)}
        for (n, ax), (r0, rows) in zip(_SMALL_SHARDED, _SMALL_PLACE.values()):
            shards = [W[_SMALL_SLAB][p, r0:r0 + rows].reshape(self.shard_shapes[n]) for p in range(N_CHIPS)]
            w[n] = jnp.concatenate(shards, axis=ax)
        return W, w

    def pair_start(self, G, slabs, tag):
        send_sems, recv_sems, gl, lands, token = _pair_exchange_start([G[s] for s in slabs],
                                                                      name=f"grad_{tag}_pair_start")
        return (slabs, send_sems, recv_sems, gl, lands), token

    def pair_land(self, state, after, tag):
        slabs, send_sems, recv_sems, gl, lands = state
        gl, other = _pair_exchange_wait(send_sems, recv_sems, gl, lands, after, name=f"grad_{tag}_pair_wait")
        pairs = [_pair_sum(g, r, self.where, name=f"grad_pair_sum_{s}") for s, g, r in zip(slabs, gl, other)]
        send_sems, recv_sems, pairs, lands, token = _chip_exchange_start(pairs, name=f"grad_{tag}_chip_start")
        return (slabs, gl, other, send_sems, recv_sems, pairs, lands), token

    def reduce_sum(self, state, after, tag):
        slabs, gl, other, send_sems, recv_sems, pairs, lands = state
        slots = _chip_exchange_wait(send_sems, recv_sems, pairs, lands, after, name=f"grad_{tag}_chip_wait")
        return slabs, [_chip_sum(g, r, sl, self.where, name=f"grad_chip_sum_{s}")
                       for s, g, r, sl in zip(slabs, gl, other, slots)]

    @staticmethod
    def share_start(slabs, halves, tag):
        send_sems, recv_sems, halves, token = _pair_share_start(halves, halves[0], name=f"grad_{tag}_share_start")
        return (slabs, send_sems, recv_sems, halves), token

    @staticmethod
    def share_finish(state, after, tag):
        slabs, send_sems, recv_sems, halves = state
        return dict(zip(slabs, _pair_share_wait(send_sems, recv_sems, halves, after, name=f"grad_{tag}_share_wait")))

    def grads_ready(self, piece, G):
        self.reduces[piece], token = self.pair_start(G, _GRAD_PIECES[piece], piece)
        return token

    def grads_crossed(self, piece, after):
        self.reduces[piece], token = self.pair_land(self.reduces[piece], after, piece)
        return token


def kernel(x, mem, e_norm, e_w_in, e_gmlp_w, e_gmlp_b, e_conv_w, e_conv_b, e_conv_ln_g, e_conv_ln_b, e_w_out, o_norm, o_w_in, o_lam_re, o_lam_im, o_log_dt, o_b_re, o_b_im, o_c_re, o_c_im, o_d, o_w_out, ca_norm, ca_mem_norm, ca_wq, ca_wk, ca_wv, ca_wo, ffn_norm, ffn_w_gate, ffn_w_up, ffn_w_down, final_norm, loss_target, m_e_norm, m_e_w_in, m_e_gmlp_w, m_e_gmlp_b, m_e_conv_w, m_e_conv_b, m_e_conv_ln_g, m_e_conv_ln_b, m_e_w_out, m_o_norm, m_o_w_in, m_o_lam_re, m_o_lam_im, m_o_log_dt, m_o_b_re, m_o_b_im, m_o_c_re, m_o_c_im, m_o_d, m_o_w_out, m_ca_norm, m_ca_mem_norm, m_ca_wq, m_ca_wk, m_ca_wv, m_ca_wo, m_ffn_norm, m_ffn_w_gate, m_ffn_w_up, m_ffn_w_down, m_final_norm, v_e_norm, v_e_w_in, v_e_gmlp_w, v_e_gmlp_b, v_e_conv_w, v_e_conv_b, v_e_conv_ln_g, v_e_conv_ln_b, v_e_w_out, v_o_norm, v_o_w_in, v_o_lam_re, v_o_lam_im, v_o_log_dt, v_o_b_re, v_o_b_im, v_o_c_re, v_o_c_im, v_o_d, v_o_w_out, v_ca_norm, v_ca_mem_norm, v_ca_wq, v_ca_wk, v_ca_wv, v_ca_wo, v_ffn_norm, v_ffn_w_gate, v_ffn_w_up, v_ffn_w_down, v_final_norm):
    args = dict(locals())
    local = {n: args[n] for n in _WEIGHTS}
    mom = {n: args["m_" + n] for n in _WEIGHTS}
    vel = {n: args["v_" + n] for n in _WEIGHTS}
    chip = 2 * lax.axis_index("x") + lax.axis_index("y")
    core = lax.axis_index("c")
    xs_, mems_, tgt = x[0], mem[0], loss_target[0]

    w = {n: local[n] for n in _REPLICATED}
    exchange = _Exchange(local, chip, core)
    G = {s: lax.empty((N_CHIPS, rows, width), F32) for s, (width, rows) in _SLABS.items()}
    loss_lanes, dx, G, grads = _forward_backward(xs_, mems_, tgt, w, G, exchange)

    gs_slab, gs_spans = _pack_rows([grads[n] for n in _SMALL] + [loss_lanes], SMALL_W, F32)
    rest0_token = exchange.grads_ready("rest0", G)
    small_flight = _all_to_all_start(gs_slab, rest0_token, name="small_grads_start")
    slabs_l1, halves_l1 = exchange.reduce_sum(exchange.reduces["l1"], small_flight[4], "l1")
    slabs_f0, halves_f0 = exchange.reduce_sum(exchange.reduces["ffn0"], small_flight[4], "ffn0")
    share, share_token = exchange.share_start(slabs_l1 + slabs_f0, halves_l1 + halves_f0, "l1_ffn0")
    token = exchange.grads_crossed("rest0", share_token)

    gs_slab, gs_all = _all_to_all_wait(*small_flight[:4], token, name="small_grads_wait")
    gs_all = lax.dynamic_update_slice(gs_all, gs_slab[None], (2 * chip + core, 0, 0))
    gs_sum = _sum_slots(gs_all, name="small_grad_sum")
    *small_sums, loss_sum = _unpack_rows(gs_sum, gs_spans, [grads[n].shape for n in _SMALL] + [loss_lanes.shape])
    out_grads = dict(zip(_SMALL, small_sums))
    for n, ax in _SMALL_SHARDED:
        width = local[n].shape[ax]
        out_grads[n] = lax.dynamic_slice_in_dim(out_grads[n], chip * width, width, axis=ax)

    delta, new_m, new_v = {}, {}, {}
    d_, m_, v_ = _adamw_small([_two_d(local[n]) for n in _SMALL], [_two_d(out_grads[n]) for n in _SMALL],
                              [_two_d(mom[n]) for n in _SMALL], [_two_d(vel[n]) for n in _SMALL], name="adamw_small")
    for n, dd, mm_, vv in zip(_SMALL, d_, m_, v_):
        shp = local[n].shape
        delta[n], new_m[n], new_v[n] = dd.reshape(shp), mm_.reshape(shp), vv.reshape(shp)
    def adamw_large(names):
        for n in names:
            shp = local[n].shape
            g_, d_, m_, v_ = _adamw_shard(_shard_rows(n, local[n]), [(gsum[s], r0) for s, r0 in _PLACE[n][1]],
                                          _shard_rows(n, mom[n]), _shard_rows(n, vel[n]), name=f"adamw_{n}")
            out_grads[n], delta[n], new_m[n], new_v[n] = (_from_shard_rows(n, t, shp) for t in (g_, d_, m_, v_))

    gsum = exchange.share_finish(share, delta[_SMALL[0]], "l1_ffn0")
    ready = [n for n, (_, where) in _PLACE.items() if all(s in gsum for s, _ in where)]
    adamw_large(ready)
    done = jnp.concatenate([delta[n].reshape(-1)[:1] for n in ready])
    slabs_r0, halves_r0 = exchange.reduce_sum(exchange.reduces["rest0"], done, "rest0")
    share, share_token = exchange.share_start(slabs_r0, halves_r0, "rest0")
    gsum = {**gsum, **exchange.share_finish(share, share_token, "rest0")}
    adamw_large([n for n in _PLACE if n not in ready])

    return (loss_sum[0, 0], dx[None], *[out_grads[n] for n in _WEIGHTS], *[delta[n] for n in _WEIGHTS],
            *[new_m[n] for n in _WEIGHTS], *[new_v[n] for n in _WEIGHTS])
```

```python
import functools
import math

import jax
import jax.numpy as jnp
from jax import lax
from jax.experimental import pallas as pl
from jax.experimental.pallas import tpu as pltpu

F32 = jnp.float32
BF16 = jnp.bfloat16
MESH = pl.DeviceIdType.MESH

EPS = 1e-6
D_MODEL = 1024
A_WIDTH = 512
A_GROUPS = 4
GMLP_BLOCK = 128
CHUNK = 64
B_WIDTH = 512
CONV_WIDTH = 31
CONV_HALO = 32
C_WIDTH = 512
C_GROUP_CH = 16
C_GROUPS = 32
C_STATE = 64
N_STATE = C_GROUPS * C_STATE
STATE_LANES = 128
STATE_ROWS = N_STATE // STATE_LANES
SCAN_BLOCK = 8
CA_HEADS = 4
CA_HEAD_DIM = 256
FFN_HIDDEN = 2816

ADAM_LR = 0.001
ADAM_B1 = 0.9
ADAM_B2 = 0.999
ADAM_EPS = 1e-08
ADAM_WD = 0.01
ADAM_STEP = 10

VMEM_LIMIT = 56 * 1024 * 1024
ACC_BYTES = 6 * 1024 * 1024
TN_VMEM_BYTES = 44 * 1024 * 1024
SMALL_W = 128
N_CHIPS = 4
N_DEV = 8

_SLABS = {"D0": (512, 1024), "E0": (1024, 256), "A0": (1024, 1024), "B0": (1024, 704), "C0": (1024, 1408),
          "D1": (512, 768), "A1": (1024, 1024), "B1": (1024, 704), "C1": (1024, 1408)}
_STAGES = (("D0", "E0"), ("A0",), ("B0", "C0"), ("D1", "A1", "B1", "C1"))
_GRAD_PIECES = {"l1": _STAGES[3], "ffn0": _STAGES[2], "rest0": _STAGES[0] + _STAGES[1]}
_PLACE = {
    "e_w_in": (1024, (("D0", 0),)), "e_w_out": (256, (("E0", 0),)),
    "o_w_out": (512, (("D1", 0),)), "o_w_in": (256, (("D1", 512),)),
    "ca_wq": (256, (("A0", 0), ("A1", 0))), "ca_wk": (256, (("A0", 256), ("A1", 256))),
    "ca_wv": (256, (("A0", 512), ("A1", 512))), "ca_wo": (256, (("A0", 768), ("A1", 768))),
    "ffn_w_down": (704, (("B0", 0), ("B1", 0))),
    "ffn_w_gate": (704, (("C0", 0), ("C1", 0))), "ffn_w_up": (704, (("C0", 704), ("C1", 704))),
}
_SMALL_SLAB = "F0"
_SMALL_SLAB_ROWS = 48
_SMALL_PLACE = {"e_conv_w": (0, 31), "o_norm": (32, 2), "o_d": (34, 1)}
_TRANSPOSED = ("ffn_w_gate", "ffn_w_up")


def _params(sem=None):
    return pltpu.CompilerParams(dimension_semantics=sem, vmem_limit_bytes=VMEM_LIMIT)


def _tile(n, pref, mult=128):
    if n <= pref:
        return n
    t = (pref // mult) * mult
    while t >= mult:
        if n % t == 0:
            return t
        t -= mult
    return n


def _blk(name, layer=0):
    rows, where = _PLACE[name]
    slab, r0 = where[layer]
    assert r0 % rows == 0
    return slab, rows, r0 // rows


def _shards(slabs, name, layer=0):
    slab, rows, b = _blk(name, layer)
    return [(slabs[slab], (None, rows, _SLABS[slab][0]), (p, b, 0)) for p in range(N_CHIPS)]


_GELU_C = 0.7978845608028654
_GELU_A = 0.044715


def _gelu(x):
    t = jnp.tanh(_GELU_C * (x + _GELU_A * (x * x * x)))
    return 0.5 * x * (1.0 + t), t


def _gelu_grad(x, t):
    return 0.5 * (1.0 + t) + 0.5 * x * (1.0 - t * t) * (_GELU_C * (1.0 + 3.0 * _GELU_A * x * x))


def _sigmoid(x):
    return 1.0 / (1.0 + jnp.exp(-x))


def _mean(x):
    return jnp.mean(x, axis=-1, keepdims=True)


def _dot(a, b):
    return jnp.dot(a, b, preferred_element_type=F32)


def _dot_nt(a, b):
    return lax.dot_general(a, b, (((1,), (1,)), ((), ())), preferred_element_type=F32)


def _dot_tn(a, b):
    return lax.dot_general(a, b, (((0,), (0,)), ((), ())), preferred_element_type=F32)


def _rms_tile(xv, gv):
    return (xv * lax.rsqrt(_mean(xv * xv) + EPS)) * gv


def _rms_bwd_tile(xv, gv, dyv):
    r = lax.rsqrt(_mean(xv * xv) + EPS)
    xh = xv * r
    dyg = dyv * gv
    return r * (dyg - xh * _mean(dyg * xh)), jnp.sum(dyv * xh, axis=0, keepdims=True)


def _cols(p, width):
    return slice(p * width, (p + 1) * width)


def _sum_k(a, ws, k):
    tot = None
    for p in range(N_CHIPS):
        y = _dot(a[:, _cols(p, k)], ws[p][...])
        tot = y if tot is None else tot + y
    return tot


def _cat_nt(a, ws):
    return jnp.concatenate([_dot_nt(a, ws[p][...]) for p in range(N_CHIPS)], axis=1)


def _rows_call(name, tm, rows, fulls, outs, accs, body, scratch=()):
    S = min(x.shape[-2] for x in rows if x.ndim != 4)
    nr, nf, no, na = len(rows), len(fulls), len(outs), len(accs)

    def kern(*refs):
        r, f = refs[:nr], refs[nr:nr + nf]
        o, a = refs[nr + nf:nr + nf + no], refs[nr + nf + no:nr + nf + no + na]
        if na:
            @pl.when(pl.program_id(0) == 0)
            def _():
                for ref in a:
                    ref[...] = jnp.zeros_like(ref)
        body(r, f, o, a, refs[nr + nf + no + na:])

    def whole(shape):
        nd = len(shape)
        return pl.BlockSpec(tuple(shape), lambda i: (0,) * nd)

    def row_spec(shape):
        if len(shape) == 4:
            return pl.BlockSpec((tm // 8,) + tuple(shape[1:]), lambda i: (i, 0, 0, 0))
        if len(shape) == 3:
            return pl.BlockSpec((shape[0], tm, shape[2]), lambda i: (0, i, 0))
        return pl.BlockSpec((tm, shape[1]), lambda i: (i, 0))

    def full_spec(x):
        if isinstance(x, tuple):
            _, bshape, bidx = x
            return pl.BlockSpec(bshape, lambda i: bidx, pipeline_mode=pl.Buffered(1))
        return whole(x.shape)

    out_shapes = [(S, o[0]) if len(o) == 2 else (o[0], S, o[1]) for o in outs]
    res = pl.pallas_call(
        kern, name=name, grid=(S // tm,),
        in_specs=[row_spec(x.shape) for x in rows] + [full_spec(x) for x in fulls],
        out_specs=[row_spec(s) for s in out_shapes] + [whole(shp) for shp, _ in accs],
        out_shape=[jax.ShapeDtypeStruct(s, o[-1]) for s, o in zip(out_shapes, outs)]
        + [jax.ShapeDtypeStruct(tuple(shp), dt) for shp, dt in accs],
        scratch_shapes=list(scratch),
        compiler_params=_params(("arbitrary",) if na else ("parallel",)),
    )(*rows, *[x[0] if isinstance(x, tuple) else x for x in fulls])
    return res[:no], res[no:]


def _grad_to_slab(gslabs, wname, layer, a, b, *, a_cols=None, b_cols=None, chips=(0, N_CHIPS), name):
    slab, rows, bidx = _blk(wname, layer)
    width = _SLABS[slab][0]
    p0, n_p = chips
    assert p0 % n_p == 0
    S = a.shape[-2]

    def tile_bytes(x, ts):
        return ts * x.dtype.itemsize * (x.shape[2] * n_p if x.ndim == 3 else x.shape[1])

    acc_bytes = n_p * rows * (-(-width // 128) * 128) * 4
    ts = next(t for t in (2048, 1024, 512, 256, S) if S % t == 0
              and 2 * (tile_bytes(a, t) + tile_bytes(b, t) + acc_bytes) <= TN_VMEM_BYTES or t == S)

    def operand(x):
        if x.ndim == 3:
            return pl.BlockSpec((n_p, ts, x.shape[2]), lambda s: (p0 // n_p, s, 0))
        return pl.BlockSpec((ts, x.shape[1]), lambda s: (s, 0))

    def part(ref, cols, p):
        if len(ref.shape) == 3:
            return ref[p]
        return ref[...] if cols is None else ref[:, _cols(p, cols)]

    def body(a_ref, b_ref, slab_ref, o_ref):
        @pl.when(pl.program_id(0) == 0)
        def _():
            o_ref[...] = jnp.zeros_like(o_ref)

        for p in range(n_p):
            o_ref[p] += _dot_tn(part(a_ref, a_cols, p).astype(BF16), part(b_ref, b_cols, p).astype(BF16))

    g = gslabs[slab]
    out = pl.pallas_call(
        body, name=name, grid=(S // ts,),
        in_specs=[operand(a), operand(b), pl.BlockSpec(memory_space=pl.ANY)],
        out_specs=pl.BlockSpec((n_p, rows, width), lambda s: (p0 // n_p, bidx, 0)),
        out_shape=jax.ShapeDtypeStruct(g.shape, F32), input_output_aliases={2: 0},
        compiler_params=_params(("arbitrary",)),
    )(a, b, g)
    return {**gslabs, slab: out}


def _vec(g):
    return g.reshape(1, -1)


def _norm_mm(x, g, ws, *, split, out_dtype, name, tm=512):
    S, D = x.shape
    k, n = ws[0][1][1], ws[0][1][2]
    N = n if split == "k" else N_CHIPS * n

    def body(r, f, o, acc, s):
        xn = _rms_tile(r[0][...], f[0][...]).astype(BF16)
        o[0][...] = xn
        if split == "k":
            o[1][...] = _sum_k(xn, f[1:], k).astype(out_dtype)
        else:
            for p in range(N_CHIPS):
                o[1][:, _cols(p, n)] = _dot(xn, f[1 + p][...]).astype(out_dtype)

    (xn, y), _ = _rows_call(name, _tile(S, tm), [x], [_vec(g)] + ws, [(D, BF16), (N, out_dtype)], [], body)
    return xn, y


def _mm_k(a, ws, *, add=None, out_dtype=F32, name, tm=512):
    S = a.shape[-2]
    k, n = ws[0][1][1], ws[0][1][2]
    has_add = add is not None

    def body(r, f, o, acc, s):
        if a.ndim == 3:
            y = None
            for p in range(N_CHIPS):
                t = _dot(r[0][p].astype(BF16), f[p][...])
                y = t if y is None else y + t
        else:
            y = _sum_k(r[0][...].astype(BF16), f, k)
        if has_add:
            y = y + r[1][...]
        o[0][...] = y.astype(out_dtype)

    (y,), _ = _rows_call(name, _tile(S, tm), [a] + ([add] if has_add else []), ws, [(n, out_dtype)], [], body)
    return y


def _mm_k_t(terms, *, out_dtype=F32, name, tm=512):
    S = terms[0][0].shape[0]
    k = terms[0][1][0][1][1]

    def body(r, f, o, acc, s):
        y = None
        for t in range(len(terms)):
            yt = _cat_nt(r[t][...].astype(BF16), f[N_CHIPS * t:N_CHIPS * (t + 1)])
            y = yt if y is None else y + yt
        o[0][...] = y.astype(out_dtype)

    (y,), _ = _rows_call(name, _tile(S, tm), [a for a, _ in terms], [w for _, ws in terms for w in ws],
                         [(N_CHIPS * k, out_dtype)], [], body)
    return y


def _rms_fwd(x, g, *, name):
    def body(r, f, o, acc, s):
        o[0][...] = _rms_tile(r[0][...], f[0][...]).astype(BF16)

    (y,), _ = _rows_call(name, _tile(x.shape[0], 256, 8), [x], [_vec(g)], [(x.shape[1], BF16)], [], body)
    return y


def _rms_dg(x, g, dy, *, name):
    def body(r, f, o, acc, s):
        acc[0][...] += _rms_bwd_tile(r[0][...], f[0][...], r[1][...])[1]

    _, (dg,) = _rows_call(name, _tile(x.shape[0], 256, 8), [x, dy], [_vec(g)], [], [((1, x.shape[1]), F32)], body)
    return dg


def _ffn_up(x, g, wg, wu, *, name, tm=512):
    S, D = x.shape
    h = wg[0][1][1]

    def body(r, f, o, acc, s):
        xn = _rms_tile(r[0][...], f[0][...]).astype(BF16)
        o[0][...] = xn
        for p in range(N_CHIPS):
            gate = _dot_nt(xn, f[1 + p][...])
            up = _dot_nt(xn, f[1 + N_CHIPS + p][...])
            o[1][p] = gate.astype(BF16)
            o[2][p] = up.astype(BF16)
            o[3][p] = (gate * _sigmoid(gate) * up).astype(BF16)

    (xn, gate, up, hid), _ = _rows_call(name, _tile(S, tm), [x], [_vec(g)] + wg + wu,
                                        [(D, BF16), (N_CHIPS, h, BF16), (N_CHIPS, h, BF16), (N_CHIPS, h, BF16)], [],
                                        body)
    return xn, gate, up, hid


def _ffn_bwd_hidden(dy, wd, gate, up, token=None, *, name, tm=512):
    S = dy.shape[0]
    h = wd[0][1][1]

    def body(r, f, o, acc, s):
        dyv = r[0][...]
        if token is not None:
            dyv = dyv + jnp.sum(f[N_CHIPS][...])
        dyb = dyv.astype(BF16)
        for p in range(N_CHIPS):
            dh = _dot_nt(dyb, f[p][...])
            gv = r[1][p].astype(F32)
            sg = _sigmoid(gv)
            o[0][p] = (dh * r[2][p].astype(F32) * (sg * (1.0 + gv * (1.0 - sg)))).astype(BF16)
            o[1][p] = (dh * gv * sg).astype(BF16)

    (dg, du), _ = _rows_call(name, _tile(S, tm), [dy, gate, up], wd + ([] if token is None else [token]),
                             [(N_CHIPS, h, BF16), (N_CHIPS, h, BF16)], [], body)
    return dg, du


def _ffn_in_bwd(dg, du, wg, wu, x, g, dres, *, name, tm=512):
    S, D = x.shape

    def body(r, f, o, acc, s):
        tot = None
        for p in range(N_CHIPS):
            y = _dot(r[0][p], f[1 + p][...]) + _dot(r[1][p], f[1 + N_CHIPS + p][...])
            tot = y if tot is None else tot + y
        dx, dgn = _rms_bwd_tile(r[2][...], f[0][...], tot)
        o[0][...] = dx + r[3][...]
        acc[0][...] += dgn

    (dx,), (dgn,) = _rows_call(name, _tile(S, tm), [dg, du, x, dres], [_vec(g)] + wg + wu, [(D, F32)],
                               [((1, D), F32)], body)
    return dx, dgn


def _norm_bwd_k(da, ws, x, g, dres, *, name, tm=512):
    S, D = x.shape

    def body(r, f, o, acc, s):
        dx, dg = _rms_bwd_tile(r[1][...], f[0][...], _cat_nt(r[0][...].astype(BF16), f[1:]))
        o[0][...] = dx + r[2][...]
        acc[0][...] += dg

    (dx,), (dg,) = _rows_call(name, _tile(S, tm), [da, x, dres], [_vec(g)] + ws, [(D, F32)], [((1, D), F32)], body)
    return dx, dg


def _norm_bwd_n(das, ws, x, g, dres, *, name, tm=256):
    S, D = x.shape
    n = ws[0][1][2]

    def body(r, f, o, acc, s):
        tot = None
        for p in range(N_CHIPS):
            y = _dot_nt(r[p // 2][:, _cols(p % 2, n)], f[1 + p][...])
            tot = y if tot is None else tot + y
        dx, dg = _rms_bwd_tile(r[2][...], f[0][...], tot)
        o[0][...] = dx + r[3][...]
        acc[0][...] += dg

    (dx,), (dg,) = _rows_call(name, _tile(S, tm), list(das) + [x, dres], [_vec(g)] + ws, [(D, F32)], [((1, D), F32)],
                              body)
    return dx, dg


def _ln_stats(v):
    mu = _mean(v)
    xc = v - mu
    rstd = lax.rsqrt(_mean(xc * xc) + EPS)
    return xc * rstd, rstd


_SHIFTS = 8
_CONV_ROWS = 64


def _fill_shifts(sh_ref, ext_ref, tm):
    sh_ref[0] = ext_ref[...]
    for s in range(1, _SHIFTS):
        sh_ref[s, 0:tm + CONV_HALO - _SHIFTS, :] = ext_ref[pl.ds(s, tm + CONV_HALO - _SHIFTS), :]


def _window(sh_ref, off, tm):
    return sh_ref[off % _SHIFTS, pl.ds(off - off % _SHIFTS, tm), :]


def _even_fwd(proj, wm, bcol, cw, cb, lg, lb, *, name):
    S = proj.shape[0]
    tm = _tile(S, 256)
    hb = tm // CONV_HALO
    nblk = tm // GMLP_BLOCK

    def body(p_ref, halo_ref, wm_ref, b_ref, cw_ref, cb_ref, lg_ref, lb_ref, mix_ref, hc_ref, hext_ref, hsh_ref):
        i = pl.program_id(0)
        gu, _ = _gelu(p_ref[:, 0:A_WIDTH])
        gv, _ = _gelu(p_ref[:, A_WIDTH:2 * A_WIDTH])
        vn, _ = _ln_stats(gv)
        vnb = vn.astype(BF16)
        for n in range(nblk):
            rows = slice(n * GMLP_BLOCK, (n + 1) * GMLP_BLOCK)
            for g in range(A_GROUPS):
                cols = slice(g * GMLP_BLOCK, (g + 1) * GMLP_BLOCK)
                sg = jnp.dot(wm_ref[g], vnb[rows, cols], preferred_element_type=F32) + b_ref[g]
                mix_ref[rows, cols] = (gu[rows, cols] * sg).astype(BF16)
        h = p_ref[:, 1024:1536] * _sigmoid(p_ref[:, 1536:2048])
        hh = halo_ref[:, 0:B_WIDTH] * _sigmoid(halo_ref[:, B_WIDTH:2 * B_WIDTH])
        hext_ref[0:CONV_HALO, :] = jnp.where(i > 0, hh, 0.0)
        hext_ref[CONV_HALO:CONV_HALO + tm, :] = h
        _fill_shifts(hsh_ref, hext_ref, tm)
        for r0 in range(0, tm, _CONV_ROWS):
            acc = jnp.zeros((_CONV_ROWS, B_WIDTH), F32)
            for k in range(CONV_WIDTH):
                acc = acc + cw_ref[k:k + 1, :] * _window(hsh_ref, r0 + k + CONV_HALO - CONV_WIDTH + 1, _CONV_ROWS)
            hc_ref[r0:r0 + _CONV_ROWS, :] = acc + cb_ref[...]
        hc = hc_ref[...]
        hhat, _ = _ln_stats(hc)
        hl = hhat * lg_ref[...] + lb_ref[...]
        mix_ref[:, A_WIDTH:A_WIDTH + B_WIDTH] = (hl * _sigmoid(hl)).astype(BF16)

    vec = pl.BlockSpec((1, B_WIDTH), lambda i: (0, 0))
    return pl.pallas_call(
        body, name=name, grid=(S // tm,),
        in_specs=[
            pl.BlockSpec((tm, 2048), lambda i: (i, 0)),
            pl.BlockSpec((CONV_HALO, 1024), lambda i: (jnp.maximum(i * hb - 1, 0), 1)),
            pl.BlockSpec((A_GROUPS, GMLP_BLOCK, GMLP_BLOCK), lambda i: (0, 0, 0)),
            pl.BlockSpec((A_GROUPS, GMLP_BLOCK, 1), lambda i: (0, 0, 0)),
            pl.BlockSpec((CONV_HALO, B_WIDTH), lambda i: (0, 0)),
            vec, vec, vec,
        ],
        out_specs=[pl.BlockSpec((tm, 1024), lambda i: (i, 0)), pl.BlockSpec((tm, B_WIDTH), lambda i: (i, 0))],
        out_shape=[jax.ShapeDtypeStruct((S, 1024), BF16), jax.ShapeDtypeStruct((S, B_WIDTH), F32)],
        scratch_shapes=[pltpu.VMEM((tm + CONV_HALO, B_WIDTH), F32),
                        pltpu.VMEM((_SHIFTS, tm + CONV_HALO, B_WIDTH), F32)],
        compiler_params=_params(("parallel",)),
    )(proj, proj, wm, bcol, cw, cb, lg, lb)


def _even_bwd1(proj, dmix, hc, wm, wmt, bcol, lg, lb, *, name):
    S = proj.shape[0]
    tm = _tile(S, 256)
    nblk = tm // GMLP_BLOCK

    def body(p_ref, dm_ref, hc_ref, wm_ref, wmt_ref, b_ref, lg_ref, lb_ref,
             dpa_ref, dhc_ref, dwm_ref, db_ref, dlg_ref, dlb_ref, dcb_ref, dgu_ref, dvn_ref):
        @pl.when(pl.program_id(0) == 0)
        def _():
            dwm_ref[...] = jnp.zeros_like(dwm_ref)
            db_ref[...] = jnp.zeros_like(db_ref)
            dlg_ref[...] = jnp.zeros_like(dlg_ref)
            dlb_ref[...] = jnp.zeros_like(dlb_ref)
            dcb_ref[...] = jnp.zeros_like(dcb_ref)

        au = p_ref[:, 0:A_WIDTH]
        av = p_ref[:, A_WIDTH:2 * A_WIDTH]
        gu, tu = _gelu(au)
        gv, tv = _gelu(av)
        vn, rstd = _ln_stats(gv)
        vnb = vn.astype(BF16)
        for n in range(nblk):
            rows = slice(n * GMLP_BLOCK, (n + 1) * GMLP_BLOCK)
            for g in range(A_GROUPS):
                cols = slice(g * GMLP_BLOCK, (g + 1) * GMLP_BLOCK)
                vb = vnb[rows, cols]
                sg = jnp.dot(wm_ref[g], vb, preferred_element_type=F32) + b_ref[g]
                da = dm_ref[rows, cols]
                dsg = da * gu[rows, cols]
                dgu_ref[rows, cols] = da * sg
                dsgb = dsg.astype(BF16)
                dwm_ref[g] += _dot_nt(dsgb, vb)
                db_ref[g] += jnp.sum(dsg, axis=1, keepdims=True)
                dvn_ref[rows, cols] = jnp.dot(wmt_ref[g], dsgb, preferred_element_type=F32)
        dvn = dvn_ref[...]
        dgv = rstd * (dvn - _mean(dvn) - vn * _mean(dvn * vn))
        dpa_ref[:, 0:A_WIDTH] = (dgu_ref[...] * _gelu_grad(au, tu)).astype(BF16)
        dpa_ref[:, A_WIDTH:2 * A_WIDTH] = (dgv * _gelu_grad(av, tv)).astype(BF16)
        hhat, rstd2 = _ln_stats(hc_ref[...])
        lgv = lg_ref[...]
        hl = hhat * lgv + lb_ref[...]
        s = _sigmoid(hl)
        dhl = dm_ref[:, A_WIDTH:A_WIDTH + B_WIDTH] * (s * (1.0 + hl * (1.0 - s)))
        dlg_ref[...] += jnp.sum(dhl * hhat, axis=0, keepdims=True)
        dlb_ref[...] += jnp.sum(dhl, axis=0, keepdims=True)
        dhh = dhl * lgv
        dhc = rstd2 * (dhh - _mean(dhh) - hhat * _mean(dhh * hhat))
        dcb_ref[...] += jnp.sum(dhc, axis=0, keepdims=True)
        dhc_ref[...] = dhc

    vec = pl.BlockSpec((1, B_WIDTH), lambda i: (0, 0))
    w3 = pl.BlockSpec((A_GROUPS, GMLP_BLOCK, GMLP_BLOCK), lambda i: (0, 0, 0))
    b3 = pl.BlockSpec((A_GROUPS, GMLP_BLOCK, 1), lambda i: (0, 0, 0))
    return pl.pallas_call(
        body, name=name, grid=(S // tm,),
        in_specs=[
            pl.BlockSpec((tm, 1024), lambda i: (i, 0)),
            pl.BlockSpec((tm, 1024), lambda i: (i, 0)),
            pl.BlockSpec((tm, B_WIDTH), lambda i: (i, 0)),
            w3, w3, b3, vec, vec,
        ],
        out_specs=[pl.BlockSpec((tm, 1024), lambda i: (i, 0)), pl.BlockSpec((tm, B_WIDTH), lambda i: (i, 0)),
                   w3, b3, vec, vec, vec],
        out_shape=[
            jax.ShapeDtypeStruct((S, 1024), BF16), jax.ShapeDtypeStruct((S, B_WIDTH), F32),
            jax.ShapeDtypeStruct((A_GROUPS, GMLP_BLOCK, GMLP_BLOCK), F32),
            jax.ShapeDtypeStruct((A_GROUPS, GMLP_BLOCK, 1), F32),
            jax.ShapeDtypeStruct((1, B_WIDTH), F32), jax.ShapeDtypeStruct((1, B_WIDTH), F32),
            jax.ShapeDtypeStruct((1, B_WIDTH), F32),
        ],
        scratch_shapes=[pltpu.VMEM((tm, A_WIDTH), F32), pltpu.VMEM((tm, A_WIDTH), F32)],
        compiler_params=_params(("arbitrary",)),
    )(proj, dmix, hc, wm, wmt, bcol, lg, lb)


def _even_bwd2(proj, dhc, cw, *, name):
    S = proj.shape[0]
    tm = _tile(S, 256)
    hb = tm // CONV_HALO
    nt = S // tm
    last_halo = S // CONV_HALO - 1
    lo = CONV_HALO - CONV_WIDTH + 1

    def body(p_ref, halo_ref, d_ref, dnext_ref, cw_ref, dpb_ref, dcw_ref, hext_ref, dext_ref, hsh_ref, dsh_ref):
        i = pl.program_id(0)

        @pl.when(i == 0)
        def _():
            dcw_ref[...] = jnp.zeros_like(dcw_ref)

        hh = halo_ref[:, 0:B_WIDTH] * _sigmoid(halo_ref[:, B_WIDTH:2 * B_WIDTH])
        hext_ref[0:CONV_HALO, :] = jnp.where(i > 0, hh, 0.0)
        hext_ref[CONV_HALO:CONV_HALO + tm, :] = p_ref[:, 0:B_WIDTH] * _sigmoid(p_ref[:, B_WIDTH:2 * B_WIDTH])
        dext_ref[0:tm, :] = d_ref[...]
        dext_ref[tm:tm + CONV_HALO, :] = jnp.where(i < nt - 1, dnext_ref[...], 0.0)
        _fill_shifts(hsh_ref, hext_ref, tm)
        _fill_shifts(dsh_ref, dext_ref, tm)
        for r0 in range(0, tm, _CONV_ROWS):
            rows = slice(r0, r0 + _CONV_ROWS)
            dhc_b = d_ref[rows, :]
            dh = jnp.zeros((_CONV_ROWS, B_WIDTH), F32)
            for k in range(CONV_WIDTH):
                dh = dh + cw_ref[k:k + 1, :] * _window(dsh_ref, r0 + CONV_WIDTH - 1 - k, _CONV_ROWS)
                dcw_ref[k:k + 1, :] += jnp.sum(dhc_b * _window(hsh_ref, r0 + k + lo, _CONV_ROWS), axis=0,
                                               keepdims=True)
            ba_b = p_ref[rows, 0:B_WIDTH]
            sg_b = _sigmoid(p_ref[rows, B_WIDTH:2 * B_WIDTH])
            dpb_ref[rows, 0:B_WIDTH] = (dh * sg_b).astype(BF16)
            dpb_ref[rows, B_WIDTH:2 * B_WIDTH] = (dh * ba_b * sg_b * (1.0 - sg_b)).astype(BF16)

    return pl.pallas_call(
        body, name=name, grid=(nt,),
        in_specs=[
            pl.BlockSpec((tm, 1024), lambda i: (i, 1)),
            pl.BlockSpec((CONV_HALO, 1024), lambda i: (jnp.maximum(i * hb - 1, 0), 1)),
            pl.BlockSpec((tm, B_WIDTH), lambda i: (i, 0)),
            pl.BlockSpec((CONV_HALO, B_WIDTH), lambda i: (jnp.minimum((i + 1) * hb, last_halo), 0)),
            pl.BlockSpec((CONV_HALO, B_WIDTH), lambda i: (0, 0)),
        ],
        out_specs=[pl.BlockSpec((tm, 1024), lambda i: (i, 0)), pl.BlockSpec((CONV_HALO, B_WIDTH), lambda i: (0, 0))],
        out_shape=[jax.ShapeDtypeStruct((S, 1024), BF16), jax.ShapeDtypeStruct((CONV_HALO, B_WIDTH), F32)],
        scratch_shapes=[pltpu.VMEM((tm + CONV_HALO, B_WIDTH), F32), pltpu.VMEM((tm + CONV_HALO, B_WIDTH), F32),
                        pltpu.VMEM((_SHIFTS, tm + CONV_HALO, B_WIDTH), F32),
                        pltpu.VMEM((_SHIFTS, tm + CONV_HALO, B_WIDTH), F32)],
        compiler_params=_params(("arbitrary",)),
    )(proj, proj, dhc, dhc, cw)


_CA_SCALE = CA_HEAD_DIM ** -0.5


def _softmax_rows(s):
    e = jnp.exp(s - jnp.max(s, axis=-1, keepdims=True))
    return e / jnp.sum(e, axis=-1, keepdims=True)


def _attn_fwd(q, k, v, *, name):
    S = q.shape[0]

    def body(r, f, o, acc, s):
        for h in range(CA_HEADS):
            cols = _cols(h, CA_HEAD_DIM)
            p = _softmax_rows(_dot_nt(r[0][:, cols], f[0][:, cols]) * _CA_SCALE)
            o[0][:, cols] = _dot(p.astype(BF16), f[1][:, cols]).astype(BF16)

    (o_,), _ = _rows_call(name, _tile(S, 512), [q], [k, v], [(D_MODEL, BF16)], [], body)
    return o_


def _attn_bwd(dy, wo, q, k, v, *, name):
    S = q.shape[0]
    M = k.shape[0]

    def body(r, f, o, acc, s):
        dyb = r[0][...].astype(BF16)
        for h in range(CA_HEADS):
            cols = _cols(h, CA_HEAD_DIM)
            qh = r[1][:, cols]
            kh = f[0][:, cols]
            vh = f[1][:, cols]
            doh = _dot_nt(dyb, f[2 + h][...]).astype(BF16)
            p = _softmax_rows(_dot_nt(qh, kh) * _CA_SCALE)
            acc[1][:, cols] += _dot_tn(p.astype(BF16), doh)
            dp = _dot_nt(doh, vh)
            ds = (p * (dp - jnp.sum(dp * p, axis=-1, keepdims=True)) * _CA_SCALE).astype(BF16)
            o[0][:, cols] = _dot(ds, kh).astype(BF16)
            acc[0][:, cols] += _dot_tn(ds, qh)

    (dq,), (dk, dv) = _rows_call(name, _tile(S, 512), [dy, q], [k, v] + wo, [(D_MODEL, BF16)],
                                 [((M, D_MODEL), F32), ((M, D_MODEL), F32)], body)
    return dq, dk, dv


_STATE_TILE = 2 * STATE_ROWS
N_SETS = 4
SET_CH = C_WIDTH // N_SETS
SET_COLS = N_STATE // N_SETS // STATE_LANES


def _set_groups(j):
    return [SET_COLS * j + c for c in range(SET_COLS)] + [STATE_ROWS + SET_COLS * j + c for c in range(SET_COLS)]


def _pack_state(re, im):
    hi = lax.bitcast_convert_type(re.astype(BF16).astype(F32), jnp.uint32)
    lo = lax.bitcast_convert_type(im.astype(BF16).astype(F32), jnp.uint32) >> 16
    return hi | lo


def _unpack_state(word):
    re = lax.bitcast_convert_type(word & jnp.uint32(0xFFFF0000), F32)
    im = lax.bitcast_convert_type(word << 16, F32)
    return re, im


def _state_set(ref, tm, j):
    parts = [_unpack_state(ref[:, SET_COLS * j + c, :, :].reshape(tm, STATE_LANES)) for c in range(SET_COLS)]
    return jnp.concatenate([p[0].astype(BF16) for p in parts] + [p[1].astype(BF16) for p in parts], axis=1)


def _s5_readout(xs, cset, u, d, *, name, tm=256):
    tm = _tile(u.shape[0], tm)

    def body(r, f, o, acc, s):
        y0 = jnp.concatenate([_dot(_state_set(r[0], tm, j), f[0][j]) for j in range(N_SETS)], axis=1)
        y = y0 + f[1][...] * r[1][...]
        o[0][...] = y
        o[1][...] = _gelu(y)[0].astype(BF16)

    (y, yg), _ = _rows_call(name, tm, [xs, u], [cset, d], [(C_WIDTH, F32), (C_WIDTH, BF16)], [], body)
    return y, yg


def _state_grad_sets(a, st, *, name, ts=256):
    ts = _tile(a.shape[0], ts)

    def body(r, f, o, acc, s):
        for j in range(N_SETS):
            acc[0][j] += _dot_tn(r[0][:, _cols(j, SET_CH)].astype(BF16), _state_set(r[1], ts, j))

    _, (out,) = _rows_call(name, ts, [a, st], [], [], [((N_SETS, SET_CH, 2 * N_STATE // N_SETS), F32)], body)
    return out


def _glu_out(yg, ws, x, *, name, tm=512):
    n = ws[0][1][2]

    def body(r, f, o, acc, s):
        ygv = r[0][...]
        ov = [_dot(ygv, f[p][...]) for p in range(N_CHIPS)]
        for p in range(N_CHIPS):
            o[0][:, _cols(p, n)] = ov[p].astype(BF16)
        for p in range(2):
            o[1][:, _cols(p, n)] = r[1][:, _cols(p, n)] + ov[p] * _sigmoid(ov[2 + p])

    (o_, y), _ = _rows_call(name, _tile(x.shape[0], tm), [yg, x], ws, [(2 * D_MODEL, BF16), (D_MODEL, F32)], [], body)
    return o_, y


def _glu_out_bwd(o_, dy, ws, y, u, d, *, name, tm=256):
    n = ws[0][1][2]

    def body(r, f, o, acc, s):
        o1 = r[0][:, 0:D_MODEL].astype(F32)
        sg = _sigmoid(r[0][:, D_MODEL:2 * D_MODEL].astype(F32))
        dyv = r[1][...]
        do1 = (dyv * sg).astype(BF16)
        do2 = (dyv * o1 * sg * (1.0 - sg)).astype(BF16)
        o[0][:, 0:D_MODEL] = do1
        o[0][:, D_MODEL:2 * D_MODEL] = do2
        dyg = None
        for p in range(N_CHIPS):
            t = _dot_nt((do1 if p < 2 else do2)[:, _cols(p % 2, n)], f[1 + p][...])
            dyg = t if dyg is None else dyg + t
        yv = r[2][...]
        dys = dyg * _gelu_grad(yv, _gelu(yv)[1])
        o[1][...] = dys.astype(BF16)
        o[2][...] = f[0][...] * dys
        acc[0][...] += jnp.sum(dys * r[3][...], axis=0, keepdims=True)

    (do, dys, dus), (dd,) = _rows_call(name, _tile(dy.shape[0], tm), [o_, dy, y, u], [d] + ws,
                                       [(2 * D_MODEL, BF16), (C_WIDTH, BF16), (C_WIDTH, F32)], [((1, C_WIDTH), F32)],
                                       body)
    return do, dys, dus, dd


def _s5_in_bwd(gs, bset, dus, ws, x, g, dres, *, name, tm=256):
    D = x.shape[1]
    tm = _tile(x.shape[0], tm)

    def body(r, f, o, acc, s):
        du0 = jnp.concatenate([_dot_nt(_state_set(r[0], tm, j), f[1][j]) for j in range(N_SETS)], axis=1)
        du = (du0 + r[1][...]).astype(BF16)
        o[0][...] = du
        dx, dg = _rms_bwd_tile(r[2][...], f[0][...], _cat_nt(du, f[2:]))
        o[1][...] = dx + r[3][...]
        acc[0][...] += dg

    (du, dx), (dg,) = _rows_call(name, tm, [gs, dus, x, dres], [_vec(g), bset] + ws,
                                 [(C_WIDTH, BF16), (D, F32)], [((1, D), F32)], body)
    return du, dx, dg


_SCAN_CHUNK = 256
_RE = slice(0, STATE_ROWS)
_IM = slice(STATE_ROWS, 2 * STATE_ROWS)
assert SCAN_BLOCK == 8


def _token(g, i, rows):
    return pl.ds(pl.multiple_of(g * (rows * SCAN_BLOCK), rows * SCAN_BLOCK) + i, rows, stride=SCAN_BLOCK)


def _fill_chunk(s3, a_ref, wset, tc, nt):
    for j in range(N_SETS):
        av = a_ref[:, _cols(j, SET_CH)].astype(BF16)
        y = _dot_nt(av, wset[j]) if nt else _dot(av, wset[j])
        for k, c in enumerate(_set_groups(j)):
            s3[:, 8 * c:8 * (c + 1), :] = y[:, _cols(k, STATE_LANES)].reshape(tc // 8, 8, STATE_LANES)


def _chunk_token(s3, g, i):
    return s3[g, pl.ds(i, _STATE_TILE, stride=SCAN_BLOCK), :]


def _scan_fwd(u, bset, pw, *, name):
    S = u.shape[0]
    tc = _tile(S, _SCAN_CHUNK, 8)

    def body(u_ref, bset_ref, pw_ref, xs_ref, st_ref, s3):
        @pl.when(pl.program_id(0) == 0)
        def _():
            st_ref[...] = jnp.zeros_like(st_ref)

        _fill_chunk(s3, u_ref, bset_ref, tc, nt=False)
        ar = pw_ref[0, _RE, :]
        ai = pw_ref[0, _IM, :]

        def block(g, carry):
            xr, xi = carry
            cr = ci = nr = ni = None
            for j in range(SCAN_BLOCK):
                b = _chunk_token(s3, g, j)
                br, bi = b[_RE], b[_IM]
                cr, ci = (br, bi) if j == 0 else (ar * cr - ai * ci + br, ar * ci + ai * cr + bi)
                pr, pi = pw_ref[j, _RE, :], pw_ref[j, _IM, :]
                nr = pr * xr - pi * xi + cr
                ni = pr * xi + pi * xr + ci
                xs_ref[_token(g, j, STATE_ROWS), :] = _pack_state(nr, ni)
            return nr, ni

        xr, xi = lax.fori_loop(0, tc // SCAN_BLOCK, block, (st_ref[_RE, :], st_ref[_IM, :]), unroll=4)
        st_ref[_RE, :] = xr
        st_ref[_IM, :] = xi

    return pl.pallas_call(
        body, name=name, grid=(S // tc,),
        in_specs=[pl.BlockSpec((tc, u.shape[1]), lambda i: (i, 0)), pl.BlockSpec(bset.shape, lambda i: (0, 0, 0)),
                  pl.BlockSpec(pw.shape, lambda i: (0, 0, 0))],
        out_specs=pl.BlockSpec((tc * STATE_ROWS, STATE_LANES), lambda i: (i, 0)),
        out_shape=jax.ShapeDtypeStruct((S * STATE_ROWS, STATE_LANES), jnp.uint32),
        scratch_shapes=[pltpu.VMEM((2 * STATE_ROWS, STATE_LANES), F32),
                        pltpu.VMEM((tc // 8, _STATE_TILE * 8, STATE_LANES), F32)],
        compiler_params=_params(("arbitrary",)),
    )(u, bset, pw)


def _scan_bwd(dys, cset, xs, pw, *, name):
    S = dys.shape[0]
    tc = _tile(S, _SCAN_CHUNK, 8)
    nc = S // tc

    def body(dys_ref, cset_ref, xs_ref, pw_ref, g_ref, da_ref, st_ref, s3):
        @pl.when(pl.program_id(0) == 0)
        def _():
            st_ref[...] = jnp.zeros_like(st_ref)
            da_ref[...] = jnp.zeros_like(da_ref)

        _fill_chunk(s3, dys_ref, cset_ref, tc, nt=True)
        ar = pw_ref[0, _RE, :]
        ai = pw_ref[0, _IM, :]

        def block(k, carry):
            gr, gi, dar, dai = carry
            g = tc // SCAN_BLOCK - 1 - k
            cr = ci = None
            pgr, pgi = gr, gi
            for j in range(SCAN_BLOCK):
                i = SCAN_BLOCK - 1 - j
                xr, xi = _unpack_state(xs_ref[_token(g, i, STATE_ROWS), :])
                dar = dar + pgr * xr + pgi * xi
                dai = dai + pgi * xr - pgr * xi
                d = _chunk_token(s3, g, i)
                dr, di = d[_RE], d[_IM]
                cr, ci = (dr, di) if j == 0 else (ar * cr + ai * ci + dr, ar * ci - ai * cr + di)
                pr, pi = pw_ref[j, _RE, :], pw_ref[j, _IM, :]
                pgr = pr * gr + pi * gi + cr
                pgi = pr * gi - pi * gr + ci
                g_ref[_token(g, i, STATE_ROWS), :] = _pack_state(pgr, pgi)
            return pgr, pgi, dar, dai

        init = (st_ref[_RE, :], st_ref[_IM, :], da_ref[_RE, :], da_ref[_IM, :])
        gr, gi, dar, dai = lax.fori_loop(0, tc // SCAN_BLOCK, block, init, unroll=4)
        st_ref[_RE, :] = gr
        st_ref[_IM, :] = gi
        da_ref[_RE, :] = dar
        da_ref[_IM, :] = dai

    packed = pl.BlockSpec((tc * STATE_ROWS, STATE_LANES), lambda i: (nc - 1 - i, 0))
    vec = pl.BlockSpec((2 * STATE_ROWS, STATE_LANES), lambda i: (0, 0))
    return pl.pallas_call(
        body, name=name, grid=(nc,),
        in_specs=[pl.BlockSpec((tc, dys.shape[1]), lambda i: (nc - 1 - i, 0)),
                  pl.BlockSpec(cset.shape, lambda i: (0, 0, 0)), packed, pl.BlockSpec(pw.shape, lambda i: (0, 0, 0))],
        out_specs=[packed, vec],
        out_shape=[jax.ShapeDtypeStruct(xs.shape, jnp.uint32), jax.ShapeDtypeStruct((2 * STATE_ROWS, STATE_LANES), F32)],
        scratch_shapes=[pltpu.VMEM((2 * STATE_ROWS, STATE_LANES), F32),
                        pltpu.VMEM((tc // 8, _STATE_TILE * 8, STATE_LANES), F32)],
        compiler_params=_params(("arbitrary",)),
    )(dys, cset, xs, pw)


def _down_loss_head(h, ws, x, g, target, *, name, tm=512):
    S, D = x.shape

    def body(r, f, o, acc, s):
        xv = r[1][...]
        for p in range(N_CHIPS):
            xv = xv + _dot(r[0][p], f[1 + p][...])
        gv = f[0][...]
        rs = lax.rsqrt(_mean(xv * xv) + EPS)
        xh = xv * rs
        err = xh * gv - r[2][...]
        acc[1][...] += 0.5 * jnp.sum(_mean(err * err), axis=0, keepdims=True)
        dy = err * (1.0 / D)
        dyg = dy * gv
        o[0][...] = rs * (dyg - xh * _mean(dyg * xh))
        acc[0][...] += jnp.sum(dy * xh, axis=0, keepdims=True)

    (dx,), (dg, loss) = _rows_call(name, _tile(S, tm), [h, x, target], [_vec(g)] + ws, [(D, F32)],
                                   [((1, D), F32), ((1, 128), F32)], body)
    return dx, dg, loss


_ADAM_C1 = 1.0 - ADAM_B1 ** ADAM_STEP
_ADAM_C2 = 1.0 - ADAM_B2 ** ADAM_STEP
_ONE_BLOCK_BYTES = 8 * 1024 * 1024


def _adamw_math(w, g, m, v):
    nm = ADAM_B1 * m + (1.0 - ADAM_B1) * g
    nv = ADAM_B2 * v + (1.0 - ADAM_B2) * (g * g)
    m_hat = nm / _ADAM_C1
    v_hat = nv / _ADAM_C2
    return -ADAM_LR * (m_hat / (jnp.sqrt(v_hat) + ADAM_EPS) + ADAM_WD * w), nm, nv


def _adamw_shard(w, gsrc, m, v, *, name):
    R, C = w.shape
    n_l = len(gsrc)
    rows = R // n_l
    tr = rows
    for _, r0 in gsrc:
        tr = math.gcd(tr, r0) if r0 else tr
    tr = _tile(tr, 256, 8) if tr > 256 else tr
    nb = rows // tr
    assert rows % tr == 0 and all(r0 % tr == 0 for _, r0 in gsrc)

    def body(*refs):
        w_ref, g_refs, (m_ref, v_ref, go_ref, d_ref, nm_ref, nv_ref) = refs[0], refs[1:1 + n_l], refs[1 + n_l:]
        layer = pl.program_id(0) // nb
        gv = g_refs[0][...]
        for l in range(1, n_l):
            gv = jnp.where(layer == l, g_refs[l][...], gv)
        go_ref[...] = gv
        d_ref[...], nm_ref[...], nv_ref[...] = _adamw_math(w_ref[...], gv, m_ref[...], v_ref[...])

    def g_spec(l, r0):
        return pl.BlockSpec((tr, C), lambda i: (r0 // tr + jnp.clip(i - l * nb, 0, nb - 1), 0))

    blk = pl.BlockSpec((tr, C), lambda i: (i, 0))
    out = jax.ShapeDtypeStruct((R, C), F32)
    return pl.pallas_call(
        body, name=name, grid=(R // tr,),
        in_specs=[blk] + [g_spec(l, r0) for l, (_, r0) in enumerate(gsrc)] + [blk, blk], out_specs=[blk] * 4,
        out_shape=[out] * 4, compiler_params=_params(("parallel",)),
    )(w, *[g for g, _ in gsrc], m, v)


def _adamw_small(ws, gs, ms, vs, *, name):
    n = len(ws)

    def body(*refs):
        w_r, g_r, m_r, v_r = refs[:n], refs[n:2 * n], refs[2 * n:3 * n], refs[3 * n:4 * n]
        d_r, nm_r, nv_r = refs[4 * n:5 * n], refs[5 * n:6 * n], refs[6 * n:7 * n]
        for k in range(n):
            d_r[k][...], nm_r[k][...], nv_r[k][...] = _adamw_math(w_r[k][...], g_r[k][...], m_r[k][...], v_r[k][...])

    vm = pl.BlockSpec(memory_space=pltpu.VMEM)
    out = [jax.ShapeDtypeStruct(w.shape, F32) for w in ws]
    res = pl.pallas_call(body, name=name, in_specs=[vm] * (4 * n), out_specs=[vm] * (3 * n), out_shape=out * 3,
                         compiler_params=pltpu.CompilerParams(vmem_limit_bytes=VMEM_LIMIT))(*ws, *gs, *ms, *vs)
    return res[:n], res[n:2 * n], res[2 * n:]


def _sum_slots(x, *, name):
    n, R, C = x.shape
    tr = R if (n + 1) * R * C * 4 <= _ONE_BLOCK_BYTES else _tile(R, 256, 8)

    def body(x_ref, o_ref):
        acc = x_ref[0]
        for k in range(1, n):
            acc = acc + x_ref[k]
        o_ref[...] = acc

    return pl.pallas_call(
        body, name=name, grid=(R // tr,),
        in_specs=[pl.BlockSpec((n, tr, C), lambda i: (0, i, 0))], out_specs=pl.BlockSpec((tr, C), lambda i: (i, 0)),
        out_shape=jax.ShapeDtypeStruct((R, C), F32), compiler_params=_params(("parallel",)),
    )(x)


def _pair_sum(g, r, where, *, name):
    n, R, C = g.shape
    Rh = R // 2
    tr = _tile(Rh, 256, 8)
    nb = Rh // tr

    def body(where_ref, g_ref, r_ref, o_ref):
        o_ref[...] = (g_ref[...] + r_ref[...]).astype(BF16)

    def slot(p, w):
        return p + jnp.where(p >= w[0], 1, 0)

    return pl.pallas_call(
        body, name=name,
        grid_spec=pltpu.PrefetchScalarGridSpec(
            num_scalar_prefetch=1, grid=(n - 1, nb),
            in_specs=[pl.BlockSpec((1, tr, C), lambda p, i, w: (slot(p, w), w[1] * nb + i, 0)),
                      pl.BlockSpec((1, tr, C), lambda p, i, w: (slot(p, w), i, 0))],
            out_specs=pl.BlockSpec((1, tr, C), lambda p, i, w: (slot(p, w), i, 0)),
        ),
        out_shape=jax.ShapeDtypeStruct((n, Rh, C), BF16), compiler_params=_params(("parallel", "parallel")),
    )(where, g, r)


def _chip_sum(g, r, slots, where, *, name):
    n, R, C = g.shape
    Rh = R // 2
    tr = _tile(Rh, 256, 8)
    nb = Rh // tr

    def body(w_ref, g_ref, r_ref, s_ref, o_ref):
        acc = g_ref[0] + r_ref[0]
        for k in range(slots.shape[0]):
            acc = acc + s_ref[k].astype(F32)
        o_ref[...] = acc

    return pl.pallas_call(
        body, name=name,
        grid_spec=pltpu.PrefetchScalarGridSpec(
            num_scalar_prefetch=1, grid=(nb,),
            in_specs=[pl.BlockSpec((1, tr, C), lambda i, w: (w[0], w[1] * nb + i, 0)),
                      pl.BlockSpec((1, tr, C), lambda i, w: (w[0], i, 0)),
                      pl.BlockSpec((slots.shape[0], tr, C), lambda i, w: (0, i, 0))],
            out_specs=pl.BlockSpec((tr, C), lambda i, w: (w[1] * nb + i, 0)),
        ),
        out_shape=jax.ShapeDtypeStruct((R, C), F32), compiler_params=_params(("parallel",)),
    )(where, g, r, slots)


ANY = pl.BlockSpec(memory_space=pl.ANY)


def _place():
    return lax.axis_index("x"), lax.axis_index("y"), lax.axis_index("c")


def _other_chips(x, y):
    return [(1 - x, y), (x, 1 - y), (1 - x, 1 - y)]


def _aliased_comm_call(body, bufs, n_sems, *, name):
    n = len(bufs)
    return pl.pallas_call(
        body, name=name, out_shape=[jax.ShapeDtypeStruct(b.shape, b.dtype) for b in bufs],
        in_specs=[ANY] * n, out_specs=[ANY] * n, input_output_aliases={k: k for k in range(n)},
        scratch_shapes=[pltpu.SemaphoreType.DMA((n_sems,)), pltpu.SemaphoreType.DMA((n_sems,))],
    )(*bufs)


HBM = pl.BlockSpec(memory_space=pltpu.HBM)
SEM = pl.BlockSpec(memory_space=pltpu.SEMAPHORE)
_SPLIT = pltpu.CompilerParams(has_side_effects=pltpu.SideEffectType.DATAFLOW_SIDE_EFFECTING)


def _in_hbm(arrs):
    return [pltpu.with_memory_space_constraint(a, pltpu.HBM) for a in arrs]


def _gather_ici_start(bufs, after, *, name):
    n = len(bufs)

    def body(*refs):
        send_sems, recv_sems, outs, token = refs[n + 1], refs[n + 2], refs[n + 3:2 * n + 3], refs[2 * n + 3]
        x, y, c = _place()
        for b in range(n):
            rh = bufs[b].shape[1] // 2
            part = outs[b].at[2 * x + y, pl.ds(c * rh, rh), :]
            for j, chip in enumerate(_other_chips(x, y)):
                pltpu.make_async_remote_copy(src_ref=part, dst_ref=part, send_sem=send_sems.at[3 * b + j],
                                             recv_sem=recv_sems.at[3 * b + j], device_id=(*chip, c),
                                             device_id_type=MESH).start()
        token[...] = jnp.zeros_like(token)

    res = pl.pallas_call(
        body, name=name,
        out_shape=(pltpu.SemaphoreType.DMA((3 * n,)), pltpu.SemaphoreType.DMA((3 * n,)),
                   *[pltpu.HBM(b.shape, b.dtype) for b in bufs], jax.ShapeDtypeStruct((8, 128), F32)),
        in_specs=[HBM] * n + [ANY], out_specs=(SEM, SEM, *[HBM] * n, pl.BlockSpec(memory_space=pltpu.VMEM)),
        input_output_aliases={k: k + 2 for k in range(n)}, compiler_params=_SPLIT,
    )(*_in_hbm(bufs), after)
    return res[0], res[1], list(res[2:2 + n]), res[2 + n]


def _gather_ici_wait(send_sems, recv_sems, bufs, after, *, name):
    n = len(bufs)

    def body(*refs):
        ins, ss, rs = refs[:n], refs[n], refs[n + 1]
        x, y, c = _place()
        for b in range(n):
            rh = bufs[b].shape[1] // 2
            mine = ins[b].at[2 * x + y, pl.ds(c * rh, rh), :]
            for j, (cx, cy) in enumerate(_other_chips(x, y)):
                theirs = ins[b].at[2 * cx + cy, pl.ds(c * rh, rh), :]
                cp = pltpu.make_async_remote_copy(src_ref=mine, dst_ref=theirs, send_sem=ss.at[3 * b + j],
                                                  recv_sem=rs.at[3 * b + j], device_id=(cx, cy, c),
                                                  device_id_type=MESH)
                cp.wait_send()
                cp.wait_recv()

    return list(pl.pallas_call(
        body, name=name, out_shape=[pltpu.HBM(b.shape, b.dtype) for b in bufs],
        in_specs=[HBM] * n + [SEM, SEM, ANY], out_specs=[HBM] * n,
        input_output_aliases={k: k for k in range(n)}, compiler_params=_SPLIT,
    )(*bufs, send_sems, recv_sems, after))


def _gather_forward(bufs, *, name):
    n = len(bufs)

    def body(*refs):
        outs, send_sems, recv_sems = refs[n:2 * n], refs[2 * n], refs[2 * n + 1]
        x, y, c = _place()

        def copy(b, j, chip, hc):
            rh = bufs[b].shape[1] // 2
            part = outs[b].at[2 * chip[0] + chip[1], pl.ds(hc * rh, rh), :]
            return pltpu.make_async_remote_copy(src_ref=part, dst_ref=part, send_sem=send_sems.at[3 * b + j],
                                                recv_sem=recv_sems.at[3 * b + j], device_id=(x, y, 1 - c),
                                                device_id_type=MESH)

        sends = [copy(b, j, chip, c) for b in range(n) for j, chip in enumerate(_other_chips(x, y))]
        for cp in sends:
            cp.start()
        for b in range(n):
            for j, chip in enumerate(_other_chips(x, y)):
                copy(b, j, chip, 1 - c).wait_recv()
        for cp in sends:
            cp.wait_send()

    return _aliased_comm_call(body, bufs, 3 * n, name=name)


def _chip_exchange_start(hs, *, name):
    n = len(hs)
    lands = [lax.empty((3,) + h.shape[1:], h.dtype) for h in hs]

    def body(*refs):
        send_sems, recv_sems = refs[2 * n], refs[2 * n + 1]
        h_out, l_out, token = refs[2 * n + 2:3 * n + 2], refs[3 * n + 2:4 * n + 2], refs[4 * n + 2]
        x, y, c = _place()
        for b in range(n):
            for j, (cx, cy) in enumerate(_other_chips(x, y)):
                pltpu.make_async_remote_copy(src_ref=h_out[b].at[2 * cx + cy], dst_ref=l_out[b].at[j],
                                             send_sem=send_sems.at[3 * b + j], recv_sem=recv_sems.at[3 * b + j],
                                             device_id=(cx, cy, c), device_id_type=MESH).start()
        token[...] = jnp.zeros_like(token)

    res = pl.pallas_call(
        body, name=name,
        out_shape=(pltpu.SemaphoreType.DMA((3 * n,)), pltpu.SemaphoreType.DMA((3 * n,)),
                   *[pltpu.HBM(a.shape, a.dtype) for a in hs + lands], jax.ShapeDtypeStruct((8, 128), F32)),
        in_specs=[HBM] * (2 * n), out_specs=(SEM, SEM, *[HBM] * (2 * n), pl.BlockSpec(memory_space=pltpu.VMEM)),
        input_output_aliases={k: k + 2 for k in range(2 * n)}, compiler_params=_SPLIT,
    )(*_in_hbm(hs + lands))
    return res[0], res[1], list(res[2:2 + n]), list(res[2 + n:2 + 2 * n]), res[2 + 2 * n]


def _chip_exchange_wait(send_sems, recv_sems, hs, lands, after, *, name):
    n = len(hs)

    def body(*refs):
        h_in, l_in, ss, rs = refs[:n], refs[n:2 * n], refs[2 * n], refs[2 * n + 1]
        x, y, c = _place()
        for b in range(n):
            for j, (cx, cy) in enumerate(_other_chips(x, y)):
                cp = pltpu.make_async_remote_copy(src_ref=h_in[b].at[2 * cx + cy], dst_ref=l_in[b].at[j],
                                                  send_sem=ss.at[3 * b + j], recv_sem=rs.at[3 * b + j],
                                                  device_id=(cx, cy, c), device_id_type=MESH)
                cp.wait_send()
                cp.wait_recv()

    res = pl.pallas_call(
        body, name=name, out_shape=[pltpu.HBM(a.shape, a.dtype) for a in hs + lands],
        in_specs=[HBM] * (2 * n) + [SEM, SEM, ANY], out_specs=[HBM] * (2 * n),
        input_output_aliases={k: k for k in range(2 * n)}, compiler_params=_SPLIT,
    )(*hs, *lands, send_sems, recv_sems, after)
    return list(res[n:])


def _peers(x, y, c):
    return [((1 - x) if fx else x, (1 - y) if fy else y, (1 - c) if fc else c)
            for fx in (0, 1) for fy in (0, 1) for fc in (0, 1) if fx or fy or fc]


def _all_to_all_start(slab, after, *, name):
    land = lax.empty((N_DEV,) + slab.shape, slab.dtype)

    def body(slab_in, land_in, after_ref, send_sems, recv_sems, slab_out, land_out, token):
        x, y, c = _place()
        for k, peer in enumerate(_peers(x, y, c)):
            pltpu.make_async_remote_copy(src_ref=slab_out, dst_ref=land_out.at[4 * x + 2 * y + c],
                                         send_sem=send_sems.at[k], recv_sem=recv_sems.at[k], device_id=peer,
                                         device_id_type=MESH).start()
        token[...] = jnp.zeros_like(token)

    return pl.pallas_call(
        body, name=name,
        out_shape=(pltpu.SemaphoreType.DMA((N_DEV - 1,)), pltpu.SemaphoreType.DMA((N_DEV - 1,)),
                   pltpu.HBM(slab.shape, slab.dtype), pltpu.HBM(land.shape, land.dtype),
                   jax.ShapeDtypeStruct((8, 128), F32)),
        in_specs=[HBM, HBM, ANY], out_specs=(SEM, SEM, HBM, HBM, pl.BlockSpec(memory_space=pltpu.VMEM)),
        input_output_aliases={0: 2, 1: 3}, compiler_params=_SPLIT,
    )(*_in_hbm([slab, land]), after)


def _all_to_all_wait(send_sems, recv_sems, slab, land, after, *, name):
    def body(slab_in, land_in, ss, rs, after_ref, slab_out, land_out):
        x, y, c = _place()
        for k, (px, py, pc) in enumerate(_peers(x, y, c)):
            cp = pltpu.make_async_remote_copy(src_ref=slab_in, dst_ref=land_in.at[4 * px + 2 * py + pc],
                                              send_sem=ss.at[k], recv_sem=rs.at[k], device_id=(px, py, pc),
                                              device_id_type=MESH)
            cp.wait_send()
            cp.wait_recv()

    return pl.pallas_call(
        body, name=name, out_shape=[pltpu.HBM(slab.shape, slab.dtype), pltpu.HBM(land.shape, land.dtype)],
        in_specs=[HBM, HBM, SEM, SEM, ANY], out_specs=[HBM, HBM], input_output_aliases={0: 0, 1: 1},
        compiler_params=_SPLIT,
    )(slab, land, send_sems, recv_sems, after)


def _pair_exchange_start(gs, *, name):
    n = len(gs)
    lands = [lax.empty((g.shape[0], g.shape[1] // 2, g.shape[2]), g.dtype) for g in gs]

    def body(*refs):
        send_sems, recv_sems = refs[2 * n], refs[2 * n + 1]
        g_out, l_out, token = refs[2 * n + 2:3 * n + 2], refs[3 * n + 2:4 * n + 2], refs[4 * n + 2]
        x, y, c = _place()
        for b in range(n):
            rh = gs[b].shape[1] // 2
            pltpu.make_async_remote_copy(src_ref=g_out[b].at[:, pl.ds((1 - c) * rh, rh), :], dst_ref=l_out[b],
                                         send_sem=send_sems.at[b], recv_sem=recv_sems.at[b],
                                         device_id=(x, y, 1 - c), device_id_type=MESH).start()
        token[...] = jnp.zeros_like(token)

    res = pl.pallas_call(
        body, name=name,
        out_shape=(pltpu.SemaphoreType.DMA((n,)), pltpu.SemaphoreType.DMA((n,)),
                   *[pltpu.HBM(a.shape, a.dtype) for a in gs + lands], jax.ShapeDtypeStruct((8, 128), F32)),
        in_specs=[HBM] * (2 * n), out_specs=(SEM, SEM, *[HBM] * (2 * n), pl.BlockSpec(memory_space=pltpu.VMEM)),
        input_output_aliases={k: k + 2 for k in range(2 * n)}, compiler_params=_SPLIT,
    )(*_in_hbm(gs + lands))
    return res[0], res[1], list(res[2:2 + n]), list(res[2 + n:2 + 2 * n]), res[2 + 2 * n]


def _pair_exchange_wait(send_sems, recv_sems, gs, lands, after, *, name):
    n = len(gs)

    def body(*refs):
        g_in, l_in, ss, rs = refs[:n], refs[n:2 * n], refs[2 * n], refs[2 * n + 1]
        x, y, c = _place()
        for b in range(n):
            rh = gs[b].shape[1] // 2
            cp = pltpu.make_async_remote_copy(src_ref=g_in[b].at[:, pl.ds((1 - c) * rh, rh), :], dst_ref=l_in[b],
                                              send_sem=ss.at[b], recv_sem=rs.at[b], device_id=(x, y, 1 - c),
                                              device_id_type=MESH)
            cp.wait_send()
            cp.wait_recv()

    res = pl.pallas_call(
        body, name=name, out_shape=[pltpu.HBM(a.shape, a.dtype) for a in gs + lands],
        in_specs=[HBM] * (2 * n) + [SEM, SEM, ANY], out_specs=[HBM] * (2 * n),
        input_output_aliases={k: k for k in range(2 * n)}, compiler_params=_SPLIT,
    )(*gs, *lands, send_sems, recv_sems, after)
    return list(res[:n]), list(res[n:])


def _pair_share_start(ss, *, name):
    n = len(ss)

    def body(*refs):
        send_sems, recv_sems, outs, token = refs[n], refs[n + 1], refs[n + 2:2 * n + 2], refs[2 * n + 2]
        x, y, c = _place()
        for b in range(n):
            rh = ss[b].shape[0] // 2
            mine = outs[b].at[pl.ds(c * rh, rh), :]
            pltpu.make_async_remote_copy(src_ref=mine, dst_ref=mine, send_sem=send_sems.at[b],
                                         recv_sem=recv_sems.at[b], device_id=(x, y, 1 - c),
                                         device_id_type=MESH).start()
        token[...] = jnp.zeros_like(token)

    res = pl.pallas_call(
        body, name=name,
        out_shape=(pltpu.SemaphoreType.DMA((n,)), pltpu.SemaphoreType.DMA((n,)),
                   *[pltpu.HBM(a.shape, a.dtype) for a in ss], jax.ShapeDtypeStruct((8, 128), F32)),
        in_specs=[HBM] * n, out_specs=(SEM, SEM, *[HBM] * n, pl.BlockSpec(memory_space=pltpu.VMEM)),
        input_output_aliases={k: k + 2 for k in range(n)}, compiler_params=_SPLIT,
    )(*_in_hbm(ss))
    return res[0], res[1], list(res[2:2 + n]), res[2 + n]


def _pair_share_wait(send_sems, recv_sems, ss, after, *, name):
    n = len(ss)

    def body(*refs):
        ins, sems_s, sems_r = refs[:n], refs[n], refs[n + 1]
        x, y, c = _place()
        for b in range(n):
            rh = ss[b].shape[0] // 2
            mine = ins[b].at[pl.ds(c * rh, rh), :]
            theirs = ins[b].at[pl.ds((1 - c) * rh, rh), :]
            cp = pltpu.make_async_remote_copy(src_ref=mine, dst_ref=theirs, send_sem=sems_s.at[b],
                                              recv_sem=sems_r.at[b], device_id=(x, y, 1 - c),
                                              device_id_type=MESH)
            cp.wait_send()
            cp.wait_recv()

    return list(pl.pallas_call(
        body, name=name, out_shape=[pltpu.HBM(a.shape, a.dtype) for a in ss],
        in_specs=[HBM] * n + [SEM, SEM, ANY], out_specs=[HBM] * n,
        input_output_aliases={k: k for k in range(n)}, compiler_params=_SPLIT,
    )(*ss, send_sems, recv_sems, after))


_SMALL_SHARDED = (("e_conv_w", 2), ("o_norm", 1), ("o_d", 1))
_REPLICATED = ("e_norm", "e_gmlp_w", "e_gmlp_b", "e_conv_b", "e_conv_ln_g", "e_conv_ln_b", "o_lam_re", "o_lam_im",
               "o_log_dt", "o_b_re", "o_b_im", "o_c_re", "o_c_im", "ca_norm", "ca_mem_norm", "ffn_norm", "final_norm")
_SMALL = tuple(n for n, _ in _SMALL_SHARDED) + _REPLICATED
_WEIGHTS = ("e_norm", "e_w_in", "e_gmlp_w", "e_gmlp_b", "e_conv_w", "e_conv_b", "e_conv_ln_g", "e_conv_ln_b",
            "e_w_out", "o_norm", "o_w_in", "o_lam_re", "o_lam_im", "o_log_dt", "o_b_re", "o_b_im", "o_c_re", "o_c_im",
            "o_d", "o_w_out", "ca_norm", "ca_mem_norm", "ca_wq", "ca_wk", "ca_wv", "ca_wo", "ffn_norm", "ffn_w_gate",
            "ffn_w_up", "ffn_w_down", "final_norm")


def _pack_rows(arrs, width, dtype, row_mult=8):
    parts, spans, r0 = [], [], 0
    for a in arrs:
        flat = a.reshape(-1).astype(dtype)
        rows = -(-flat.shape[0] // (width * row_mult)) * row_mult
        if rows * width != flat.shape[0]:
            flat = jnp.pad(flat, (0, rows * width - flat.shape[0]))
        parts.append(flat.reshape(rows, width))
        spans.append((r0, rows))
        r0 += rows
    return jnp.concatenate(parts, axis=0), spans


def _unpack_rows(slab, spans, shapes):
    out = []
    for (r0, rows), shp in zip(spans, shapes):
        n = math.prod(shp)
        out.append(slab[r0:r0 + rows].reshape(-1)[:n].reshape(shp))
    return out


def _two_d(a):
    return a.reshape(-1, a.shape[-1])


def _shard_rows(n, a):
    return _two_d(jnp.swapaxes(a, -1, -2) if n in _TRANSPOSED else a)


def _from_shard_rows(n, rows, shape):
    if n in _TRANSPOSED:
        return jnp.swapaxes(rows.reshape(shape[:-2] + (shape[-1], shape[-2])), -1, -2)
    return rows.reshape(shape)


def _local_slab(local, slab, dtype):
    parts = sorted((r0, n, l) for n, (_, where) in _PLACE.items() for l, (s, r0) in enumerate(where) if s == slab)
    shards = [_shard_rows(n, local[n] if len(_PLACE[n][1]) == 1 else local[n][l]) for _, n, l in parts]
    return jnp.concatenate([a.astype(dtype) for a in shards], axis=0)


def _set_diag(b, pattern):
    return jnp.einsum(pattern, b, jnp.eye(C_GROUPS // N_SETS, dtype=b.dtype))


def _s5_discretize(lam_re, lam_im, log_dt, b_re, b_im):
    dt = jnp.exp(log_dt)[:, None]
    mag = jnp.exp(lam_re * dt)
    ar = mag * jnp.cos(lam_im * dt)
    ai = mag * jnp.sin(lam_im * dt)
    den = lam_re * lam_re + lam_im * lam_im
    qr = ((ar - 1.0) * lam_re + ai * lam_im) / den
    qi = (ai * lam_re - (ar - 1.0) * lam_im) / den
    bbr = qr[..., None] * b_re - qi[..., None] * b_im
    bbi = qr[..., None] * b_im + qi[..., None] * b_re
    return ar, ai, bbr, bbi


def _attention_block(x, mem, W, w, i, tag):
    xn, q = _norm_mm(x, w["ca_norm"][i], _shards(W, "ca_wq", i), split="k", out_dtype=BF16, name=f"{tag}_q")
    memn = _rms_fwd(mem, w["ca_mem_norm"][i], name=f"{tag}_ca_memnorm")
    k = _mm_k(memn, _shards(W, "ca_wk", i), out_dtype=BF16, name=f"{tag}_k")
    v = _mm_k(memn, _shards(W, "ca_wv", i), out_dtype=BF16, name=f"{tag}_v")
    o = _attn_fwd(q, k, v, name=f"{tag}_attn")
    y = _mm_k(o, _shards(W, "ca_wo", i), add=x, name=f"{tag}_wo")
    return y, (x, xn, memn, q, k, v, o)


def _attention_block_bwd(dy, saved, mem, W, w, i, tag, G, grads, token=None, mid=None):
    x, xn, memn, q, k, v, o = saved
    gain = w["ca_norm"][i]
    if token is not None:
        k = _behind(k, token)
    G = _grad_to_slab(G, "ca_wo", i, o, dy, a_cols=256, name=f"{tag}_dwo")
    dq, dk, dv = _attn_bwd(dy, _shards(W, "ca_wo", i), q, k, v, name=f"{tag}_attn_bwd")
    token = mid(dq) if mid is not None else None
    if token is not None:
        gain = _behind(gain, token)
    G = _grad_to_slab(G, "ca_wq", i, xn, dq, a_cols=256, name=f"{tag}_dwq")
    G = _grad_to_slab(G, "ca_wk", i, memn, dk, a_cols=256, name=f"{tag}_dwk")
    G = _grad_to_slab(G, "ca_wv", i, memn, dv, a_cols=256, name=f"{tag}_dwv")
    dmemn = _mm_k_t([(dk, _shards(W, "ca_wk", i)), (dv, _shards(W, "ca_wv", i))], name=f"{tag}_dmemn")
    dx, dg = _norm_bwd_k(dq, _shards(W, "ca_wq", i), x, gain, dy, name=f"{tag}_dq_norm_bwd")
    grads["ca_norm"][i] = dg[0]
    grads["ca_mem_norm"][i] = _rms_dg(mem, w["ca_mem_norm"][i], dmemn, name=f"{tag}_ca_memnorm_bwd")[0]
    return dx, G


def _ffn_block(x, W, w, i, tag, head=None):
    fn, gate, up, h = _ffn_up(x, w["ffn_norm"][i], _shards(W, "ffn_w_gate", i), _shards(W, "ffn_w_up", i),
                              name=f"{tag}_ffn_up")
    if head is None:
        y = _mm_k(h, _shards(W, "ffn_w_down", i), add=x, name=f"{tag}_down")
    else:
        y = _down_loss_head(h, _shards(W, "ffn_w_down", i), x, *head, name=f"{tag}_down_loss_head")
    return y, (x, fn, gate, up, h)


def _ffn_block_bwd(dy, saved, W, w, i, tag, G, grads, token=None, mid=None):
    x, fn, gate, up, h = saved
    gain = w["ffn_norm"][i]
    G = _grad_to_slab(G, "ffn_w_down", i, h, dy, name=f"{tag}_dwd")
    dg, du = _ffn_bwd_hidden(dy, _shards(W, "ffn_w_down", i), gate, up, token, name=f"{tag}_ffn_bwd_hidden")
    token = mid(dg) if mid is not None else None
    if token is not None:
        gain = _behind(gain, token)
    G = _grad_to_slab(G, "ffn_w_gate", i, dg, fn, name=f"{tag}_dwg")
    G = _grad_to_slab(G, "ffn_w_up", i, du, fn, name=f"{tag}_dwu")
    dx, dgn = _ffn_in_bwd(dg, du, _shards(W, "ffn_w_gate", i), _shards(W, "ffn_w_up", i), x, gain, dy,
                          name=f"{tag}_ffn_in_bwd")
    grads["ffn_norm"][i] = dgn[0]
    return dx, G


def _gmlp_mask():
    chunk = jnp.arange(GMLP_BLOCK) // CHUNK
    return chunk[None, :] <= chunk[:, None]


def _even_block(x, W, w, tag):
    hn, proj = _norm_mm(x, w["e_norm"][0], _shards(W, "e_w_in"), split="n", out_dtype=F32, name=f"{tag}_w_in")
    wm = jnp.where(_gmlp_mask()[None], w["e_gmlp_w"][0], 0.0).astype(BF16)
    bcol = w["e_gmlp_b"][0][:, :, None]
    cw = jnp.pad(w["e_conv_w"][0], ((0, CONV_HALO - CONV_WIDTH), (0, 0)))
    cb, lg, lb = w["e_conv_b"], w["e_conv_ln_g"], w["e_conv_ln_b"]
    mix, hc = _even_fwd(proj, wm, bcol, cw, cb, lg, lb, name=f"{tag}_mixers")
    y = _mm_k(mix, _shards(W, "e_w_out"), add=x, name=f"{tag}_w_out")
    return y, (x, hn, proj, mix, hc, wm, bcol, cw)


def _even_block_bwd(dy, saved, W, w, tag, G, grads):
    x, hn, proj, mix, hc, wm, bcol, cw = saved
    dmix = _mm_k_t([(dy, _shards(W, "e_w_out"))], name=f"{tag}_dmix")
    G = _grad_to_slab(G, "e_w_out", 0, mix, dy, a_cols=256, name=f"{tag}_dw_out")
    wmt = jnp.swapaxes(wm, 1, 2)
    dpa, dhc, dwm, db, dlg, dlb, dcb = _even_bwd1(proj, dmix, hc, wm, wmt, bcol, w["e_conv_ln_g"], w["e_conv_ln_b"],
                                                  name=f"{tag}_mixers_bwd1")
    dpb, dcw = _even_bwd2(proj, dhc, cw, name=f"{tag}_mixers_bwd2")
    grads["e_gmlp_w"] = jnp.where(_gmlp_mask()[None], dwm, 0.0)[None]
    grads["e_gmlp_b"] = db[:, :, 0][None]
    grads["e_conv_ln_g"], grads["e_conv_ln_b"], grads["e_conv_b"] = dlg, dlb, dcb
    grads["e_conv_w"] = dcw[:CONV_WIDTH][None]
    G = _grad_to_slab(G, "e_w_in", 0, hn, dpa, b_cols=512, chips=(0, 2), name=f"{tag}_dw_in_a")
    G = _grad_to_slab(G, "e_w_in", 0, hn, dpb, b_cols=512, chips=(2, 2), name=f"{tag}_dw_in_b")
    dx, dg = _norm_bwd_n((dpa, dpb), _shards(W, "e_w_in"), x, w["e_norm"][0], dy, name=f"{tag}_in_bwd")
    grads["e_norm"] = dg
    return dx, G


def _odd_block(x, W, w, tag):
    S = x.shape[0]
    hn, u = _norm_mm(x, w["o_norm"][0], _shards(W, "o_w_in"), split="k", out_dtype=F32, name=f"{tag}_w_in")
    disc_in = (w["o_lam_re"][0], w["o_lam_im"][0], w["o_log_dt"][0], w["o_b_re"][0], w["o_b_im"][0])
    (ar, ai, bbr, bbi), disc_vjp = jax.vjp(_s5_discretize, *disc_in)
    sets = (N_SETS, C_GROUPS // N_SETS)
    per_set = N_STATE // N_SETS
    bset = jnp.concatenate([_set_diag(b.reshape(sets + b.shape[1:]), "jgpc,gh->jgchp").reshape(N_SETS, SET_CH, per_set)
                            for b in (bbr, bbi)], axis=2).astype(BF16)
    cset = jnp.concatenate([_set_diag(c.reshape(sets + c.shape[1:]), "jgcp,gh->jgphc").reshape(N_SETS, per_set, SET_CH)
                            for c in (w["o_c_re"][0], -w["o_c_im"][0])], axis=1).astype(BF16)
    powers, pr, pi = [], ar, ai
    for _ in range(SCAN_BLOCK):
        powers.append(jnp.concatenate([pr.reshape(STATE_ROWS, STATE_LANES), pi.reshape(STATE_ROWS, STATE_LANES)], 0))
        pr, pi = pr * ar - pi * ai, pr * ai + pi * ar
    pw = jnp.stack(powers, axis=0)
    xs = _scan_fwd(u, bset, pw, name=f"{tag}_scan").reshape(S // 8, STATE_ROWS, 8, STATE_LANES)
    yv, yg = _s5_readout(xs, cset, u, w["o_d"], name=f"{tag}_readout")
    o, y = _glu_out(yg, _shards(W, "o_w_out"), x, name=f"{tag}_glu_out")
    return y, (x, hn, u, bset, cset, pw, xs, yv, yg, o, disc_vjp)


def _odd_block_bwd(dy, saved, W, w, tag, G, grads):
    x, hn, u, bset, cset, pw, xs, yv, yg, o, disc_vjp = saved
    S = x.shape[0]
    do, dys, dus, dd = _glu_out_bwd(o, dy, _shards(W, "o_w_out"), yv, u, w["o_d"], name=f"{tag}_glu_out_bwd")
    G = _grad_to_slab(G, "o_w_out", 0, yg, do, b_cols=512, name=f"{tag}_dw_out")
    grads["o_d"] = dd
    dcset_t = _state_grad_sets(dys, xs, name=f"{tag}_dcd")
    gs, da = _scan_bwd(dys, cset, xs.reshape(S * STATE_ROWS, STATE_LANES), pw, name=f"{tag}_scan_bwd")
    gs = gs.reshape(xs.shape)
    dbset = _state_grad_sets(u, gs, name=f"{tag}_dbd")
    du, dx, dg = _s5_in_bwd(gs, bset, dus, _shards(W, "o_w_in"), x, w["o_norm"][0], dy, name=f"{tag}_in_bwd")
    G = _grad_to_slab(G, "o_w_in", 0, hn, du, a_cols=256, name=f"{tag}_dw_in")
    grads["o_norm"] = dg
    per = C_GROUPS // N_SETS
    blocks = (N_SETS, per, C_GROUP_CH, 2, per, C_STATE)
    dc = _set_diag(dcset_t.reshape(blocks), "jhcrgp,gh->rjgcp").reshape(2, C_GROUPS, C_GROUP_CH, C_STATE)
    db = _set_diag(dbset.reshape(blocks), "jgcrhp,gh->rjgpc").reshape(2, C_GROUPS, C_STATE, C_GROUP_CH)
    dcr, dci, dbbr, dbbi = dc[0], -dc[1], db[0], db[1]
    dar = da[:STATE_ROWS].reshape(C_GROUPS, C_STATE)
    dai = da[STATE_ROWS:].reshape(C_GROUPS, C_STATE)
    dlr, dli, dldt, dbr, dbi = disc_vjp((dar, dai, dbbr, dbbi))
    grads["o_lam_re"], grads["o_lam_im"], grads["o_log_dt"] = dlr[None], dli[None], dldt[None]
    grads["o_b_re"], grads["o_b_im"], grads["o_c_re"], grads["o_c_im"] = dbr[None], dbi[None], dcr[None], dci[None]
    return dx, G


def _behind(value, token):
    return value + token[0, 0].astype(value.dtype)


class _NoExchange:
    def __init__(self, W):
        self.W = W

    def first_weights(self, w):
        return self.W, w

    def weights(self, stage, after):
        return {}

    def grads_ready(self, piece, G):
        return None

    def grads_crossed(self, piece, after):
        return None


def _forward_backward(xs_, mems_, tgt, w, G, exchange):
    W, w = exchange.first_weights(w)
    x1, s_mix0 = _even_block(xs_, W, w, "l0")
    W = {**W, **exchange.weights(1, x1)}
    x2, s_att0 = _attention_block(x1, mems_, W, w, 0, "l0")
    W = {**W, **exchange.weights(2, x2)}
    x3, s_ffn0 = _ffn_block(x2, W, w, 0, "l0")
    W = {**W, **exchange.weights(3, x3)}
    x4, s_mix1 = _odd_block(x3, W, w, "l1")
    x5, s_att1 = _attention_block(x4, mems_, W, w, 1, "l1")
    (dx, dfinal, loss_lanes), s_ffn1 = _ffn_block(x5, W, w, 1, "l1", head=(w["final_norm"], tgt))

    grads = {n: [None, None] for n in ("ca_norm", "ca_mem_norm", "ffn_norm")}
    grads["final_norm"] = dfinal[0]
    dx, G = _ffn_block_bwd(dx, s_ffn1, W, w, 1, "l1", G, grads)
    dx, G = _attention_block_bwd(dx, s_att1, mems_, W, w, 1, "l1", G, grads)
    dx, G = _odd_block_bwd(dx, s_mix1, W, w, "l1", G, grads)
    token = exchange.grads_ready("l1", G)
    dx, G = _ffn_block_bwd(dx, s_ffn0, W, w, 0, "l0", G, grads, token,
                           lambda after: exchange.grads_crossed("l1", after))
    token = exchange.grads_ready("ffn0", G)
    dx, G = _attention_block_bwd(dx, s_att0, mems_, W, w, 0, "l0", G, grads, token,
                                 lambda after: exchange.grads_crossed("ffn0", after))
    dx, G = _even_block_bwd(dx, s_mix0, W, w, "l0", G, grads)
    for n in list(grads):
        if isinstance(grads[n], list):
            grads[n] = jnp.stack(grads[n], axis=0)
        grads[n] = grads[n].reshape(w[n].shape)
    return loss_lanes, dx, G, grads


class _Exchange:
    def __init__(self, local, chip, core):
        self.bufs = {s: lax.dynamic_update_slice(lax.empty((N_CHIPS, rows, width), BF16),
                                                 _local_slab(local, s, BF16)[None], (chip, 0, 0))
                     for s, (width, rows) in _SLABS.items()}
        small = jnp.zeros((_SMALL_SLAB_ROWS, SMALL_W), F32)
        for n, (r0, rows) in _SMALL_PLACE.items():
            small = small.at[r0:r0 + rows].set(local[n].reshape(rows, SMALL_W))
        self.bufs[_SMALL_SLAB] = lax.dynamic_update_slice(lax.empty((N_CHIPS, _SMALL_SLAB_ROWS, SMALL_W), F32),
                                                          small[None], (chip, 0, 0))
        self.shard_shapes = {n: local[n].shape for n in _SMALL_PLACE}
        self.where = jnp.stack([chip, core]).astype(jnp.int32)
        self.flights = []
        self.reduces = {}

    def weights(self, stage, after):
        send_sems, recv_sems, bufs, _ = self.flights[stage]
        bufs = _gather_ici_wait(send_sems, recv_sems, bufs, after, name=f"gather_stage{stage}_wait")
        return dict(zip(self.stage_slabs(stage), _gather_forward(bufs, name=f"gather_stage{stage}_forward")))

    @staticmethod
    def stage_slabs(stage):
        return _STAGES[stage] + ((_SMALL_SLAB,) if stage == 0 else ())

    def first_weights(self, w):
        after = w["e_norm"]
        for k in range(len(_STAGES)):
            self.flights.append(_gather_ici_start([self.bufs[s] for s in self.stage_slabs(k)], after,
                                                  name=f"gather_stage{k}_start"))
            after = self.flights[-1][3]
        W = self.weights(0, after)
        w = {**w, "e_norm": _behind(w["e_norm"], after)}
        for (n, ax), (r0, rows) in zip(_SMALL_SHARDED, _SMALL_PLACE.values()):
            shards = [W[_SMALL_SLAB][p, r0:r0 + rows].reshape(self.shard_shapes[n]) for p in range(N_CHIPS)]
            w[n] = jnp.concatenate(shards, axis=ax)
        return W, w

    def pair_start(self, G, slabs, tag):
        send_sems, recv_sems, gl, lands, token = _pair_exchange_start([G[s] for s in slabs],
                                                                      name=f"grad_{tag}_pair_start")
        return (slabs, send_sems, recv_sems, gl, lands), token

    def pair_land(self, state, after, tag):
        slabs, send_sems, recv_sems, gl, lands = state
        gl, other = _pair_exchange_wait(send_sems, recv_sems, gl, lands, after, name=f"grad_{tag}_pair_wait")
        pairs = [_pair_sum(g, r, self.where, name=f"grad_pair_sum_{s}") for s, g, r in zip(slabs, gl, other)]
        send_sems, recv_sems, pairs, lands, token = _chip_exchange_start(pairs, name=f"grad_{tag}_chip_start")
        return (slabs, gl, other, send_sems, recv_sems, pairs, lands), token

    def reduce_sum(self, state, after, tag):
        slabs, gl, other, send_sems, recv_sems, pairs, lands = state
        slots = _chip_exchange_wait(send_sems, recv_sems, pairs, lands, after, name=f"grad_{tag}_chip_wait")
        return slabs, [_chip_sum(g, r, sl, self.where, name=f"grad_chip_sum_{s}")
                       for s, g, r, sl in zip(slabs, gl, other, slots)]

    @staticmethod
    def share_start(slabs, halves, tag):
        send_sems, recv_sems, halves, token = _pair_share_start(halves, name=f"grad_{tag}_share_start")
        return (slabs, send_sems, recv_sems, halves), token

    @staticmethod
    def share_finish(state, after, tag):
        slabs, send_sems, recv_sems, halves = state
        return dict(zip(slabs, _pair_share_wait(send_sems, recv_sems, halves, after, name=f"grad_{tag}_share_wait")))

    def grads_ready(self, piece, G):
        self.reduces[piece], token = self.pair_start(G, _GRAD_PIECES[piece], piece)
        return token

    def grads_crossed(self, piece, after):
        self.reduces[piece], token = self.pair_land(self.reduces[piece], after, piece)
        return token


def kernel(x, mem, e_norm, e_w_in, e_gmlp_w, e_gmlp_b, e_conv_w, e_conv_b, e_conv_ln_g, e_conv_ln_b, e_w_out, o_norm, o_w_in, o_lam_re, o_lam_im, o_log_dt, o_b_re, o_b_im, o_c_re, o_c_im, o_d, o_w_out, ca_norm, ca_mem_norm, ca_wq, ca_wk, ca_wv, ca_wo, ffn_norm, ffn_w_gate, ffn_w_up, ffn_w_down, final_norm, loss_target, m_e_norm, m_e_w_in, m_e_gmlp_w, m_e_gmlp_b, m_e_conv_w, m_e_conv_b, m_e_conv_ln_g, m_e_conv_ln_b, m_e_w_out, m_o_norm, m_o_w_in, m_o_lam_re, m_o_lam_im, m_o_log_dt, m_o_b_re, m_o_b_im, m_o_c_re, m_o_c_im, m_o_d, m_o_w_out, m_ca_norm, m_ca_mem_norm, m_ca_wq, m_ca_wk, m_ca_wv, m_ca_wo, m_ffn_norm, m_ffn_w_gate, m_ffn_w_up, m_ffn_w_down, m_final_norm, v_e_norm, v_e_w_in, v_e_gmlp_w, v_e_gmlp_b, v_e_conv_w, v_e_conv_b, v_e_conv_ln_g, v_e_conv_ln_b, v_e_w_out, v_o_norm, v_o_w_in, v_o_lam_re, v_o_lam_im, v_o_log_dt, v_o_b_re, v_o_b_im, v_o_c_re, v_o_c_im, v_o_d, v_o_w_out, v_ca_norm, v_ca_mem_norm, v_ca_wq, v_ca_wk, v_ca_wv, v_ca_wo, v_ffn_norm, v_ffn_w_gate, v_ffn_w_up, v_ffn_w_down, v_final_norm):
    args = dict(locals())
    local = {n: args[n] for n in _WEIGHTS}
    mom = {n: args["m_" + n] for n in _WEIGHTS}
    vel = {n: args["v_" + n] for n in _WEIGHTS}
    chip = 2 * lax.axis_index("x") + lax.axis_index("y")
    core = lax.axis_index("c")
    xs_, mems_, tgt = x[0], mem[0], loss_target[0]

    w = {n: local[n] for n in _REPLICATED}
    exchange = _Exchange(local, chip, core)
    G = {s: lax.empty((N_CHIPS, rows, width), F32) for s, (width, rows) in _SLABS.items()}
    loss_lanes, dx, G, grads = _forward_backward(xs_, mems_, tgt, w, G, exchange)

    gs_slab, gs_spans = _pack_rows([grads[n] for n in _SMALL] + [loss_lanes], SMALL_W, F32)
    rest0_token = exchange.grads_ready("rest0", G)
    small_flight = _all_to_all_start(gs_slab, rest0_token, name="small_grads_start")
    slabs_l1, halves_l1 = exchange.reduce_sum(exchange.reduces["l1"], small_flight[4], "l1")
    slabs_f0, halves_f0 = exchange.reduce_sum(exchange.reduces["ffn0"], small_flight[4], "ffn0")
    share, share_token = exchange.share_start(slabs_l1 + slabs_f0, halves_l1 + halves_f0, "l1_ffn0")
    token = exchange.grads_crossed("rest0", share_token)

    out_grads, delta, new_m, new_v = {}, {}, {}, {}

    def adamw_large(names):
        firsts = []
        for n in names:
            shp = local[n].shape
            g_, d_, m_, v_ = _adamw_shard(_shard_rows(n, local[n]), [(gsum[s], r0) for s, r0 in _PLACE[n][1]],
                                          _shard_rows(n, mom[n]), _shard_rows(n, vel[n]), name=f"adamw_{n}")
            out_grads[n], delta[n], new_m[n], new_v[n] = (_from_shard_rows(n, t, shp) for t in (g_, d_, m_, v_))
            firsts.append(d_[:1, :1].reshape(1))
        return jnp.concatenate(firsts)

    gsum = exchange.share_finish(share, token, "l1_ffn0")
    ready = [n for n, (_, where) in _PLACE.items() if all(s in gsum for s, _ in where)]
    done = adamw_large(ready)

    gs_slab, gs_all = _all_to_all_wait(*small_flight[:4], done, name="small_grads_wait")
    gs_all = lax.dynamic_update_slice(gs_all, gs_slab[None], (2 * chip + core, 0, 0))
    gs_sum = _sum_slots(gs_all, name="small_grad_sum")
    *small_sums, loss_sum = _unpack_rows(gs_sum, gs_spans, [grads[n].shape for n in _SMALL] + [loss_lanes.shape])
    out_grads.update(zip(_SMALL, small_sums))
    for n, ax in _SMALL_SHARDED:
        width = local[n].shape[ax]
        out_grads[n] = lax.dynamic_slice_in_dim(out_grads[n], chip * width, width, axis=ax)
    d_, m_, v_ = _adamw_small([_two_d(local[n]) for n in _SMALL], [_two_d(out_grads[n]) for n in _SMALL],
                              [_two_d(mom[n]) for n in _SMALL], [_two_d(vel[n]) for n in _SMALL], name="adamw_small")
    for n, dd, mm_, vv in zip(_SMALL, d_, m_, v_):
        shp = local[n].shape
        delta[n], new_m[n], new_v[n] = dd.reshape(shp), mm_.reshape(shp), vv.reshape(shp)

    small_done = jnp.concatenate([dd[:1, :1].reshape(1) for dd in d_[:2]])
    slabs_r0, halves_r0 = exchange.reduce_sum(exchange.reduces["rest0"], small_done, "rest0")
    share, share_token = exchange.share_start(slabs_r0, halves_r0, "rest0")
    gsum = {**gsum, **exchange.share_finish(share, share_token, "rest0")}
    adamw_large([n for n in _PLACE if n not in ready])

    return (loss_sum[0, 0], dx[None], *[out_grads[n] for n in _WEIGHTS], *[delta[n] for n in _WEIGHTS],
            *[new_m[n] for n in _WEIGHTS], *[new_v[n] for n in _WEIGHTS])
```

```python
import functools
import math

import jax
import jax.numpy as jnp
from jax import lax
from jax.experimental import pallas as pl
from jax.experimental.pallas import tpu as pltpu

F32 = jnp.float32
BF16 = jnp.bfloat16
MESH = pl.DeviceIdType.MESH

EPS = 1e-6
D_MODEL = 1024
A_WIDTH = 512
A_GROUPS = 4
GMLP_BLOCK = 128
CHUNK = 64
B_WIDTH = 512
CONV_WIDTH = 31
CONV_HALO = 32
C_WIDTH = 512
C_GROUP_CH = 16
C_GROUPS = 32
C_STATE = 64
N_STATE = C_GROUPS * C_STATE
STATE_LANES = 128
STATE_ROWS = N_STATE // STATE_LANES
SCAN_BLOCK = 8
CA_HEADS = 4
CA_HEAD_DIM = 256
FFN_HIDDEN = 2816

ADAM_LR = 0.001
ADAM_B1 = 0.9
ADAM_B2 = 0.999
ADAM_EPS = 1e-08
ADAM_WD = 0.01
ADAM_STEP = 10

VMEM_LIMIT = 56 * 1024 * 1024
ACC_BYTES = 6 * 1024 * 1024
TN_VMEM_BYTES = 44 * 1024 * 1024
SMALL_W = 128
N_CHIPS = 4
N_DEV = 8

_SLABS = {"D0": (512, 1024), "E0": (1024, 256), "A0": (1024, 1024), "B0": (1024, 704), "C0": (1024, 1408),
          "D1": (512, 768), "A1": (1024, 1024), "B1": (1024, 704), "C1": (1024, 1408)}
_STAGES = (("D0", "E0"), ("A0",), ("B0", "C0"), ("D1", "A1", "B1", "C1"))
_GRAD_PIECES = {"l1": _STAGES[3], "ffn0": _STAGES[2], "rest0": _STAGES[0] + _STAGES[1]}
_PLACE = {
    "e_w_in": (1024, (("D0", 0),)), "e_w_out": (256, (("E0", 0),)),
    "o_w_out": (512, (("D1", 0),)), "o_w_in": (256, (("D1", 512),)),
    "ca_wq": (256, (("A0", 0), ("A1", 0))), "ca_wk": (256, (("A0", 256), ("A1", 256))),
    "ca_wv": (256, (("A0", 512), ("A1", 512))), "ca_wo": (256, (("A0", 768), ("A1", 768))),
    "ffn_w_down": (704, (("B0", 0), ("B1", 0))),
    "ffn_w_gate": (704, (("C0", 0), ("C1", 0))), "ffn_w_up": (704, (("C0", 704), ("C1", 704))),
}
_SMALL_SLAB = "F0"
_SMALL_SLAB_ROWS = 48
_SMALL_PLACE = {"e_conv_w": (0, 31), "o_norm": (32, 2), "o_d": (34, 1)}
_TRANSPOSED = ("ffn_w_gate", "ffn_w_up")


def _params(sem=None):
    return pltpu.CompilerParams(dimension_semantics=sem, vmem_limit_bytes=VMEM_LIMIT)


def _tile(n, pref, mult=128):
    if n <= pref:
        return n
    t = (pref // mult) * mult
    while t >= mult:
        if n % t == 0:
            return t
        t -= mult
    return n


def _blk(name, layer=0):
    rows, where = _PLACE[name]
    slab, r0 = where[layer]
    assert r0 % rows == 0
    return slab, rows, r0 // rows


def _shards(slabs, name, layer=0):
    slab, rows, b = _blk(name, layer)
    return [(slabs[slab], (None, rows, _SLABS[slab][0]), (p, b, 0)) for p in range(N_CHIPS)]


_GELU_C = 0.7978845608028654
_GELU_A = 0.044715


def _gelu(x):
    t = jnp.tanh(_GELU_C * (x + _GELU_A * (x * x * x)))
    return 0.5 * x * (1.0 + t), t


def _gelu_grad(x, t):
    return 0.5 * (1.0 + t) + 0.5 * x * (1.0 - t * t) * (_GELU_C * (1.0 + 3.0 * _GELU_A * x * x))


def _sigmoid(x):
    return 1.0 / (1.0 + jnp.exp(-x))


def _mean(x):
    return jnp.mean(x, axis=-1, keepdims=True)


def _dot(a, b):
    return jnp.dot(a, b, preferred_element_type=F32)


def _dot_nt(a, b):
    return lax.dot_general(a, b, (((1,), (1,)), ((), ())), preferred_element_type=F32)


def _dot_tn(a, b):
    return lax.dot_general(a, b, (((0,), (0,)), ((), ())), preferred_element_type=F32)


def _rms_tile(xv, gv):
    return (xv * lax.rsqrt(_mean(xv * xv) + EPS)) * gv


def _rms_bwd_tile(xv, gv, dyv):
    r = lax.rsqrt(_mean(xv * xv) + EPS)
    xh = xv * r
    dyg = dyv * gv
    return r * (dyg - xh * _mean(dyg * xh)), jnp.sum(dyv * xh, axis=0, keepdims=True)


def _cols(p, width):
    return slice(p * width, (p + 1) * width)


def _sum_k(a, ws, k):
    tot = None
    for p in range(N_CHIPS):
        y = _dot(a[:, _cols(p, k)], ws[p][...])
        tot = y if tot is None else tot + y
    return tot


def _cat_nt(a, ws):
    return jnp.concatenate([_dot_nt(a, ws[p][...]) for p in range(N_CHIPS)], axis=1)


def _rows_call(name, tm, rows, fulls, outs, accs, body, scratch=()):
    S = min(x.shape[-2] for x in rows if x.ndim != 4)
    nr, nf, no, na = len(rows), len(fulls), len(outs), len(accs)

    def kern(*refs):
        r, f = refs[:nr], refs[nr:nr + nf]
        o, a = refs[nr + nf:nr + nf + no], refs[nr + nf + no:nr + nf + no + na]
        if na:
            @pl.when(pl.program_id(0) == 0)
            def _():
                for ref in a:
                    ref[...] = jnp.zeros_like(ref)
        body(r, f, o, a, refs[nr + nf + no + na:])

    def whole(shape):
        nd = len(shape)
        return pl.BlockSpec(tuple(shape), lambda i: (0,) * nd)

    def row_spec(shape):
        if len(shape) == 4:
            return pl.BlockSpec((tm // 8,) + tuple(shape[1:]), lambda i: (i, 0, 0, 0))
        if len(shape) == 3:
            return pl.BlockSpec((shape[0], tm, shape[2]), lambda i: (0, i, 0))
        return pl.BlockSpec((tm, shape[1]), lambda i: (i, 0))

    def full_spec(x):
        if isinstance(x, tuple):
            _, bshape, bidx = x
            return pl.BlockSpec(bshape, lambda i: bidx, pipeline_mode=pl.Buffered(1))
        return whole(x.shape)

    out_shapes = [(S, o[0]) if len(o) == 2 else (o[0], S, o[1]) for o in outs]
    res = pl.pallas_call(
        kern, name=name, grid=(S // tm,),
        in_specs=[row_spec(x.shape) for x in rows] + [full_spec(x) for x in fulls],
        out_specs=[row_spec(s) for s in out_shapes] + [whole(shp) for shp, _ in accs],
        out_shape=[jax.ShapeDtypeStruct(s, o[-1]) for s, o in zip(out_shapes, outs)]
        + [jax.ShapeDtypeStruct(tuple(shp), dt) for shp, dt in accs],
        scratch_shapes=list(scratch),
        compiler_params=_params(("arbitrary",) if na else ("parallel",)),
    )(*rows, *[x[0] if isinstance(x, tuple) else x for x in fulls])
    return res[:no], res[no:]


def _grad_to_slab(gslabs, wname, layer, a, b, *, a_cols=None, b_cols=None, chips=(0, N_CHIPS), name):
    slab, rows, bidx = _blk(wname, layer)
    width = _SLABS[slab][0]
    p0, n_p = chips
    assert p0 % n_p == 0
    S = a.shape[-2]

    def tile_bytes(x, ts):
        return ts * x.dtype.itemsize * (x.shape[2] * n_p if x.ndim == 3 else x.shape[1])

    acc_bytes = n_p * rows * (-(-width // 128) * 128) * 4
    ts = next(t for t in (2048, 1024, 512, 256, S) if S % t == 0
              and 2 * (tile_bytes(a, t) + tile_bytes(b, t) + acc_bytes) <= TN_VMEM_BYTES or t == S)

    def operand(x):
        if x.ndim == 3:
            return pl.BlockSpec((n_p, ts, x.shape[2]), lambda s: (p0 // n_p, s, 0))
        return pl.BlockSpec((ts, x.shape[1]), lambda s: (s, 0))

    def part(ref, cols, p):
        if len(ref.shape) == 3:
            return ref[p]
        return ref[...] if cols is None else ref[:, _cols(p, cols)]

    def body(a_ref, b_ref, slab_ref, o_ref):
        @pl.when(pl.program_id(0) == 0)
        def _():
            o_ref[...] = jnp.zeros_like(o_ref)

        for p in range(n_p):
            o_ref[p] += _dot_tn(part(a_ref, a_cols, p).astype(BF16), part(b_ref, b_cols, p).astype(BF16))

    g = gslabs[slab]
    out = pl.pallas_call(
        body, name=name, grid=(S // ts,),
        in_specs=[operand(a), operand(b), pl.BlockSpec(memory_space=pl.ANY)],
        out_specs=pl.BlockSpec((n_p, rows, width), lambda s: (p0 // n_p, bidx, 0)),
        out_shape=jax.ShapeDtypeStruct(g.shape, F32), input_output_aliases={2: 0},
        compiler_params=_params(("arbitrary",)),
    )(a, b, g)
    return {**gslabs, slab: out}


def _vec(g):
    return g.reshape(1, -1)


def _norm_mm(x, g, ws, *, split, out_dtype, name, tm=512):
    S, D = x.shape
    k, n = ws[0][1][1], ws[0][1][2]
    N = n if split == "k" else N_CHIPS * n

    def body(r, f, o, acc, s):
        xn = _rms_tile(r[0][...], f[0][...]).astype(BF16)
        o[0][...] = xn
        if split == "k":
            o[1][...] = _sum_k(xn, f[1:], k).astype(out_dtype)
        else:
            for p in range(N_CHIPS):
                o[1][:, _cols(p, n)] = _dot(xn, f[1 + p][...]).astype(out_dtype)

    (xn, y), _ = _rows_call(name, _tile(S, tm), [x], [_vec(g)] + ws, [(D, BF16), (N, out_dtype)], [], body)
    return xn, y


def _mm_k(a, ws, *, add=None, out_dtype=F32, name, tm=512):
    S = a.shape[-2]
    k, n = ws[0][1][1], ws[0][1][2]
    has_add = add is not None

    def body(r, f, o, acc, s):
        if a.ndim == 3:
            y = None
            for p in range(N_CHIPS):
                t = _dot(r[0][p].astype(BF16), f[p][...])
                y = t if y is None else y + t
        else:
            y = _sum_k(r[0][...].astype(BF16), f, k)
        if has_add:
            y = y + r[1][...]
        o[0][...] = y.astype(out_dtype)

    (y,), _ = _rows_call(name, _tile(S, tm), [a] + ([add] if has_add else []), ws, [(n, out_dtype)], [], body)
    return y


def _mm_k_t(terms, *, out_dtype=F32, name, tm=512):
    S = terms[0][0].shape[0]
    k = terms[0][1][0][1][1]

    def body(r, f, o, acc, s):
        y = None
        for t in range(len(terms)):
            yt = _cat_nt(r[t][...].astype(BF16), f[N_CHIPS * t:N_CHIPS * (t + 1)])
            y = yt if y is None else y + yt
        o[0][...] = y.astype(out_dtype)

    (y,), _ = _rows_call(name, _tile(S, tm), [a for a, _ in terms], [w for _, ws in terms for w in ws],
                         [(N_CHIPS * k, out_dtype)], [], body)
    return y


def _rms_fwd(x, g, *, name):
    def body(r, f, o, acc, s):
        o[0][...] = _rms_tile(r[0][...], f[0][...]).astype(BF16)

    (y,), _ = _rows_call(name, _tile(x.shape[0], 256, 8), [x], [_vec(g)], [(x.shape[1], BF16)], [], body)
    return y


def _rms_dg(x, g, dy, *, name):
    def body(r, f, o, acc, s):
        acc[0][...] += _rms_bwd_tile(r[0][...], f[0][...], r[1][...])[1]

    _, (dg,) = _rows_call(name, _tile(x.shape[0], 256, 8), [x, dy], [_vec(g)], [], [((1, x.shape[1]), F32)], body)
    return dg


def _ffn_up(x, g, wg, wu, *, name, tm=512):
    S, D = x.shape
    h = wg[0][1][1]

    def body(r, f, o, acc, s):
        xn = _rms_tile(r[0][...], f[0][...]).astype(BF16)
        o[0][...] = xn
        for p in range(N_CHIPS):
            gate = _dot_nt(xn, f[1 + p][...])
            up = _dot_nt(xn, f[1 + N_CHIPS + p][...])
            o[1][p] = gate.astype(BF16)
            o[2][p] = up.astype(BF16)
            o[3][p] = (gate * _sigmoid(gate) * up).astype(BF16)

    (xn, gate, up, hid), _ = _rows_call(name, _tile(S, tm), [x], [_vec(g)] + wg + wu,
                                        [(D, BF16), (N_CHIPS, h, BF16), (N_CHIPS, h, BF16), (N_CHIPS, h, BF16)], [],
                                        body)
    return xn, gate, up, hid


def _ffn_bwd_hidden(dy, wd, gate, up, token=None, *, name, tm=512):
    S = dy.shape[0]
    h = wd[0][1][1]

    def body(r, f, o, acc, s):
        dyv = r[0][...]
        if token is not None:
            dyv = dyv + jnp.sum(f[N_CHIPS][...])
        dyb = dyv.astype(BF16)
        for p in range(N_CHIPS):
            dh = _dot_nt(dyb, f[p][...])
            gv = r[1][p].astype(F32)
            sg = _sigmoid(gv)
            o[0][p] = (dh * r[2][p].astype(F32) * (sg * (1.0 + gv * (1.0 - sg)))).astype(BF16)
            o[1][p] = (dh * gv * sg).astype(BF16)

    (dg, du), _ = _rows_call(name, _tile(S, tm), [dy, gate, up], wd + ([] if token is None else [token]),
                             [(N_CHIPS, h, BF16), (N_CHIPS, h, BF16)], [], body)
    return dg, du


def _ffn_in_bwd(dg, du, wg, wu, x, g, dres, *, name, tm=512):
    S, D = x.shape

    def body(r, f, o, acc, s):
        tot = None
        for p in range(N_CHIPS):
            y = _dot(r[0][p], f[1 + p][...]) + _dot(r[1][p], f[1 + N_CHIPS + p][...])
            tot = y if tot is None else tot + y
        dx, dgn = _rms_bwd_tile(r[2][...], f[0][...], tot)
        o[0][...] = dx + r[3][...]
        acc[0][...] += dgn

    (dx,), (dgn,) = _rows_call(name, _tile(S, tm), [dg, du, x, dres], [_vec(g)] + wg + wu, [(D, F32)],
                               [((1, D), F32)], body)
    return dx, dgn


def _norm_bwd_k(da, ws, x, g, dres, *, name, tm=512):
    S, D = x.shape

    def body(r, f, o, acc, s):
        dx, dg = _rms_bwd_tile(r[1][...], f[0][...], _cat_nt(r[0][...].astype(BF16), f[1:]))
        o[0][...] = dx + r[2][...]
        acc[0][...] += dg

    (dx,), (dg,) = _rows_call(name, _tile(S, tm), [da, x, dres], [_vec(g)] + ws, [(D, F32)], [((1, D), F32)], body)
    return dx, dg


def _norm_bwd_n(das, ws, x, g, dres, *, name, tm=256):
    S, D = x.shape
    n = ws[0][1][2]

    def body(r, f, o, acc, s):
        tot = None
        for p in range(N_CHIPS):
            y = _dot_nt(r[p // 2][:, _cols(p % 2, n)], f[1 + p][...])
            tot = y if tot is None else tot + y
        dx, dg = _rms_bwd_tile(r[2][...], f[0][...], tot)
        o[0][...] = dx + r[3][...]
        acc[0][...] += dg

    (dx,), (dg,) = _rows_call(name, _tile(S, tm), list(das) + [x, dres], [_vec(g)] + ws, [(D, F32)], [((1, D), F32)],
                              body)
    return dx, dg


def _ln_stats(v):
    mu = _mean(v)
    xc = v - mu
    rstd = lax.rsqrt(_mean(xc * xc) + EPS)
    return xc * rstd, rstd


_SHIFTS = 8
_CONV_ROWS = 64


def _fill_shifts(sh_ref, ext_ref, tm):
    sh_ref[0] = ext_ref[...]
    for s in range(1, _SHIFTS):
        sh_ref[s, 0:tm + CONV_HALO - _SHIFTS, :] = ext_ref[pl.ds(s, tm + CONV_HALO - _SHIFTS), :]


def _window(sh_ref, off, tm):
    return sh_ref[off % _SHIFTS, pl.ds(off - off % _SHIFTS, tm), :]


def _even_fwd(proj, wm, bcol, cw, cb, lg, lb, *, name):
    S = proj.shape[0]
    tm = _tile(S, 256)
    hb = tm // CONV_HALO
    nblk = tm // GMLP_BLOCK

    def body(p_ref, halo_ref, wm_ref, b_ref, cw_ref, cb_ref, lg_ref, lb_ref, mix_ref, hc_ref, hext_ref, hsh_ref):
        i = pl.program_id(0)
        gu, _ = _gelu(p_ref[:, 0:A_WIDTH])
        gv, _ = _gelu(p_ref[:, A_WIDTH:2 * A_WIDTH])
        vn, _ = _ln_stats(gv)
        vnb = vn.astype(BF16)
        for n in range(nblk):
            rows = slice(n * GMLP_BLOCK, (n + 1) * GMLP_BLOCK)
            for g in range(A_GROUPS):
                cols = slice(g * GMLP_BLOCK, (g + 1) * GMLP_BLOCK)
                sg = jnp.dot(wm_ref[g], vnb[rows, cols], preferred_element_type=F32) + b_ref[g]
                mix_ref[rows, cols] = (gu[rows, cols] * sg).astype(BF16)
        h = p_ref[:, 1024:1536] * _sigmoid(p_ref[:, 1536:2048])
        hh = halo_ref[:, 0:B_WIDTH] * _sigmoid(halo_ref[:, B_WIDTH:2 * B_WIDTH])
        hext_ref[0:CONV_HALO, :] = jnp.where(i > 0, hh, 0.0)
        hext_ref[CONV_HALO:CONV_HALO + tm, :] = h
        _fill_shifts(hsh_ref, hext_ref, tm)
        for r0 in range(0, tm, _CONV_ROWS):
            acc = jnp.zeros((_CONV_ROWS, B_WIDTH), F32)
            for k in range(CONV_WIDTH):
                acc = acc + cw_ref[k:k + 1, :] * _window(hsh_ref, r0 + k + CONV_HALO - CONV_WIDTH + 1, _CONV_ROWS)
            hc_ref[r0:r0 + _CONV_ROWS, :] = acc + cb_ref[...]
        hc = hc_ref[...]
        hhat, _ = _ln_stats(hc)
        hl = hhat * lg_ref[...] + lb_ref[...]
        mix_ref[:, A_WIDTH:A_WIDTH + B_WIDTH] = (hl * _sigmoid(hl)).astype(BF16)

    vec = pl.BlockSpec((1, B_WIDTH), lambda i: (0, 0))
    return pl.pallas_call(
        body, name=name, grid=(S // tm,),
        in_specs=[
            pl.BlockSpec((tm, 2048), lambda i: (i, 0)),
            pl.BlockSpec((CONV_HALO, 1024), lambda i: (jnp.maximum(i * hb - 1, 0), 1)),
            pl.BlockSpec((A_GROUPS, GMLP_BLOCK, GMLP_BLOCK), lambda i: (0, 0, 0)),
            pl.BlockSpec((A_GROUPS, GMLP_BLOCK, 1), lambda i: (0, 0, 0)),
            pl.BlockSpec((CONV_HALO, B_WIDTH), lambda i: (0, 0)),
            vec, vec, vec,
        ],
        out_specs=[pl.BlockSpec((tm, 1024), lambda i: (i, 0)), pl.BlockSpec((tm, B_WIDTH), lambda i: (i, 0))],
        out_shape=[jax.ShapeDtypeStruct((S, 1024), BF16), jax.ShapeDtypeStruct((S, B_WIDTH), F32)],
        scratch_shapes=[pltpu.VMEM((tm + CONV_HALO, B_WIDTH), F32),
                        pltpu.VMEM((_SHIFTS, tm + CONV_HALO, B_WIDTH), F32)],
        compiler_params=_params(("parallel",)),
    )(proj, proj, wm, bcol, cw, cb, lg, lb)


def _even_bwd1(proj, dmix, hc, wm, wmt, bcol, lg, lb, *, name):
    S = proj.shape[0]
    tm = _tile(S, 256)
    nblk = tm // GMLP_BLOCK

    def body(p_ref, dm_ref, hc_ref, wm_ref, wmt_ref, b_ref, lg_ref, lb_ref,
             dpa_ref, dhc_ref, dwm_ref, db_ref, dlg_ref, dlb_ref, dcb_ref, dgu_ref, dvn_ref):
        @pl.when(pl.program_id(0) == 0)
        def _():
            dwm_ref[...] = jnp.zeros_like(dwm_ref)
            db_ref[...] = jnp.zeros_like(db_ref)
            dlg_ref[...] = jnp.zeros_like(dlg_ref)
            dlb_ref[...] = jnp.zeros_like(dlb_ref)
            dcb_ref[...] = jnp.zeros_like(dcb_ref)

        au = p_ref[:, 0:A_WIDTH]
        av = p_ref[:, A_WIDTH:2 * A_WIDTH]
        gu, tu = _gelu(au)
        gv, tv = _gelu(av)
        vn, rstd = _ln_stats(gv)
        vnb = vn.astype(BF16)
        for n in range(nblk):
            rows = slice(n * GMLP_BLOCK, (n + 1) * GMLP_BLOCK)
            for g in range(A_GROUPS):
                cols = slice(g * GMLP_BLOCK, (g + 1) * GMLP_BLOCK)
                vb = vnb[rows, cols]
                sg = jnp.dot(wm_ref[g], vb, preferred_element_type=F32) + b_ref[g]
                da = dm_ref[rows, cols]
                dsg = da * gu[rows, cols]
                dgu_ref[rows, cols] = da * sg
                dsgb = dsg.astype(BF16)
                dwm_ref[g] += _dot_nt(dsgb, vb)
                db_ref[g] += jnp.sum(dsg, axis=1, keepdims=True)
                dvn_ref[rows, cols] = jnp.dot(wmt_ref[g], dsgb, preferred_element_type=F32)
        dvn = dvn_ref[...]
        dgv = rstd * (dvn - _mean(dvn) - vn * _mean(dvn * vn))
        dpa_ref[:, 0:A_WIDTH] = (dgu_ref[...] * _gelu_grad(au, tu)).astype(BF16)
        dpa_ref[:, A_WIDTH:2 * A_WIDTH] = (dgv * _gelu_grad(av, tv)).astype(BF16)
        hhat, rstd2 = _ln_stats(hc_ref[...])
        lgv = lg_ref[...]
        hl = hhat * lgv + lb_ref[...]
        s = _sigmoid(hl)
        dhl = dm_ref[:, A_WIDTH:A_WIDTH + B_WIDTH] * (s * (1.0 + hl * (1.0 - s)))
        dlg_ref[...] += jnp.sum(dhl * hhat, axis=0, keepdims=True)
        dlb_ref[...] += jnp.sum(dhl, axis=0, keepdims=True)
        dhh = dhl * lgv
        dhc = rstd2 * (dhh - _mean(dhh) - hhat * _mean(dhh * hhat))
        dcb_ref[...] += jnp.sum(dhc, axis=0, keepdims=True)
        dhc_ref[...] = dhc

    vec = pl.BlockSpec((1, B_WIDTH), lambda i: (0, 0))
    w3 = pl.BlockSpec((A_GROUPS, GMLP_BLOCK, GMLP_BLOCK), lambda i: (0, 0, 0))
    b3 = pl.BlockSpec((A_GROUPS, GMLP_BLOCK, 1), lambda i: (0, 0, 0))
    return pl.pallas_call(
        body, name=name, grid=(S // tm,),
        in_specs=[
            pl.BlockSpec((tm, 1024), lambda i: (i, 0)),
            pl.BlockSpec((tm, 1024), lambda i: (i, 0)),
            pl.BlockSpec((tm, B_WIDTH), lambda i: (i, 0)),
            w3, w3, b3, vec, vec,
        ],
        out_specs=[pl.BlockSpec((tm, 1024), lambda i: (i, 0)), pl.BlockSpec((tm, B_WIDTH), lambda i: (i, 0)),
                   w3, b3, vec, vec, vec],
        out_shape=[
            jax.ShapeDtypeStruct((S, 1024), BF16), jax.ShapeDtypeStruct((S, B_WIDTH), F32),
            jax.ShapeDtypeStruct((A_GROUPS, GMLP_BLOCK, GMLP_BLOCK), F32),
            jax.ShapeDtypeStruct((A_GROUPS, GMLP_BLOCK, 1), F32),
            jax.ShapeDtypeStruct((1, B_WIDTH), F32), jax.ShapeDtypeStruct((1, B_WIDTH), F32),
            jax.ShapeDtypeStruct((1, B_WIDTH), F32),
        ],
        scratch_shapes=[pltpu.VMEM((tm, A_WIDTH), F32), pltpu.VMEM((tm, A_WIDTH), F32)],
        compiler_params=_params(("arbitrary",)),
    )(proj, dmix, hc, wm, wmt, bcol, lg, lb)


def _even_bwd2(proj, dhc, cw, *, name):
    S = proj.shape[0]
    tm = _tile(S, 256)
    hb = tm // CONV_HALO
    nt = S // tm
    last_halo = S // CONV_HALO - 1
    lo = CONV_HALO - CONV_WIDTH + 1

    def body(p_ref, halo_ref, d_ref, dnext_ref, cw_ref, dpb_ref, dcw_ref, hext_ref, dext_ref, hsh_ref, dsh_ref):
        i = pl.program_id(0)

        @pl.when(i == 0)
        def _():
            dcw_ref[...] = jnp.zeros_like(dcw_ref)

        hh = halo_ref[:, 0:B_WIDTH] * _sigmoid(halo_ref[:, B_WIDTH:2 * B_WIDTH])
        hext_ref[0:CONV_HALO, :] = jnp.where(i > 0, hh, 0.0)
        hext_ref[CONV_HALO:CONV_HALO + tm, :] = p_ref[:, 0:B_WIDTH] * _sigmoid(p_ref[:, B_WIDTH:2 * B_WIDTH])
        dext_ref[0:tm, :] = d_ref[...]
        dext_ref[tm:tm + CONV_HALO, :] = jnp.where(i < nt - 1, dnext_ref[...], 0.0)
        _fill_shifts(hsh_ref, hext_ref, tm)
        _fill_shifts(dsh_ref, dext_ref, tm)
        for r0 in range(0, tm, _CONV_ROWS):
            rows = slice(r0, r0 + _CONV_ROWS)
            dhc_b = d_ref[rows, :]
            dh = jnp.zeros((_CONV_ROWS, B_WIDTH), F32)
            for k in range(CONV_WIDTH):
                dh = dh + cw_ref[k:k + 1, :] * _window(dsh_ref, r0 + CONV_WIDTH - 1 - k, _CONV_ROWS)
                dcw_ref[k:k + 1, :] += jnp.sum(dhc_b * _window(hsh_ref, r0 + k + lo, _CONV_ROWS), axis=0,
                                               keepdims=True)
            ba_b = p_ref[rows, 0:B_WIDTH]
            sg_b = _sigmoid(p_ref[rows, B_WIDTH:2 * B_WIDTH])
            dpb_ref[rows, 0:B_WIDTH] = (dh * sg_b).astype(BF16)
            dpb_ref[rows, B_WIDTH:2 * B_WIDTH] = (dh * ba_b * sg_b * (1.0 - sg_b)).astype(BF16)

    return pl.pallas_call(
        body, name=name, grid=(nt,),
        in_specs=[
            pl.BlockSpec((tm, 1024), lambda i: (i, 1)),
            pl.BlockSpec((CONV_HALO, 1024), lambda i: (jnp.maximum(i * hb - 1, 0), 1)),
            pl.BlockSpec((tm, B_WIDTH), lambda i: (i, 0)),
            pl.BlockSpec((CONV_HALO, B_WIDTH), lambda i: (jnp.minimum((i + 1) * hb, last_halo), 0)),
            pl.BlockSpec((CONV_HALO, B_WIDTH), lambda i: (0, 0)),
        ],
        out_specs=[pl.BlockSpec((tm, 1024), lambda i: (i, 0)), pl.BlockSpec((CONV_HALO, B_WIDTH), lambda i: (0, 0))],
        out_shape=[jax.ShapeDtypeStruct((S, 1024), BF16), jax.ShapeDtypeStruct((CONV_HALO, B_WIDTH), F32)],
        scratch_shapes=[pltpu.VMEM((tm + CONV_HALO, B_WIDTH), F32), pltpu.VMEM((tm + CONV_HALO, B_WIDTH), F32),
                        pltpu.VMEM((_SHIFTS, tm + CONV_HALO, B_WIDTH), F32),
                        pltpu.VMEM((_SHIFTS, tm + CONV_HALO, B_WIDTH), F32)],
        compiler_params=_params(("arbitrary",)),
    )(proj, proj, dhc, dhc, cw)


_CA_SCALE = CA_HEAD_DIM ** -0.5


def _softmax_rows(s):
    e = jnp.exp(s - jnp.max(s, axis=-1, keepdims=True))
    return e / jnp.sum(e, axis=-1, keepdims=True)


def _attn_fwd(q, k, v, *, name):
    S = q.shape[0]

    def body(r, f, o, acc, s):
        for h in range(CA_HEADS):
            cols = _cols(h, CA_HEAD_DIM)
            p = _softmax_rows(_dot_nt(r[0][:, cols], f[0][:, cols]) * _CA_SCALE)
            o[0][:, cols] = _dot(p.astype(BF16), f[1][:, cols]).astype(BF16)

    (o_,), _ = _rows_call(name, _tile(S, 512), [q], [k, v], [(D_MODEL, BF16)], [], body)
    return o_


def _attn_bwd(dy, wo, q, k, v, *, name):
    S = q.shape[0]
    M = k.shape[0]

    def body(r, f, o, acc, s):
        dyb = r[0][...].astype(BF16)
        for h in range(CA_HEADS):
            cols = _cols(h, CA_HEAD_DIM)
            qh = r[1][:, cols]
            kh = f[0][:, cols]
            vh = f[1][:, cols]
            doh = _dot_nt(dyb, f[2 + h][...]).astype(BF16)
            p = _softmax_rows(_dot_nt(qh, kh) * _CA_SCALE)
            acc[1][:, cols] += _dot_tn(p.astype(BF16), doh)
            dp = _dot_nt(doh, vh)
            ds = (p * (dp - jnp.sum(dp * p, axis=-1, keepdims=True)) * _CA_SCALE).astype(BF16)
            o[0][:, cols] = _dot(ds, kh).astype(BF16)
            acc[0][:, cols] += _dot_tn(ds, qh)

    (dq,), (dk, dv) = _rows_call(name, _tile(S, 512), [dy, q], [k, v] + wo, [(D_MODEL, BF16)],
                                 [((M, D_MODEL), F32), ((M, D_MODEL), F32)], body)
    return dq, dk, dv


_STATE_TILE = 2 * STATE_ROWS
N_SETS = 4
SET_CH = C_WIDTH // N_SETS
SET_COLS = N_STATE // N_SETS // STATE_LANES


def _set_groups(j):
    return [SET_COLS * j + c for c in range(SET_COLS)] + [STATE_ROWS + SET_COLS * j + c for c in range(SET_COLS)]


def _pack_state(re, im):
    hi = lax.bitcast_convert_type(re.astype(BF16).astype(F32), jnp.uint32)
    lo = lax.bitcast_convert_type(im.astype(BF16).astype(F32), jnp.uint32) >> 16
    return hi | lo


def _unpack_state(word):
    re = lax.bitcast_convert_type(word & jnp.uint32(0xFFFF0000), F32)
    im = lax.bitcast_convert_type(word << 16, F32)
    return re, im


def _state_set(ref, tm, j):
    parts = [_unpack_state(ref[:, SET_COLS * j + c, :, :].reshape(tm, STATE_LANES)) for c in range(SET_COLS)]
    return jnp.concatenate([p[0].astype(BF16) for p in parts] + [p[1].astype(BF16) for p in parts], axis=1)


def _s5_readout(xs, cset, u, d, *, name, tm=256):
    tm = _tile(u.shape[0], tm)

    def body(r, f, o, acc, s):
        y0 = jnp.concatenate([_dot(_state_set(r[0], tm, j), f[0][j]) for j in range(N_SETS)], axis=1)
        y = y0 + f[1][...] * r[1][...]
        o[0][...] = y
        o[1][...] = _gelu(y)[0].astype(BF16)

    (y, yg), _ = _rows_call(name, tm, [xs, u], [cset, d], [(C_WIDTH, F32), (C_WIDTH, BF16)], [], body)
    return y, yg


def _state_grad_sets(a, st, *, name, ts=256):
    ts = _tile(a.shape[0], ts)

    def body(r, f, o, acc, s):
        for j in range(N_SETS):
            acc[0][j] += _dot_tn(r[0][:, _cols(j, SET_CH)].astype(BF16), _state_set(r[1], ts, j))

    _, (out,) = _rows_call(name, ts, [a, st], [], [], [((N_SETS, SET_CH, 2 * N_STATE // N_SETS), F32)], body)
    return out


def _glu_out(yg, ws, x, *, name, tm=512):
    n = ws[0][1][2]

    def body(r, f, o, acc, s):
        ygv = r[0][...]
        ov = [_dot(ygv, f[p][...]) for p in range(N_CHIPS)]
        for p in range(N_CHIPS):
            o[0][:, _cols(p, n)] = ov[p].astype(BF16)
        for p in range(2):
            o[1][:, _cols(p, n)] = r[1][:, _cols(p, n)] + ov[p] * _sigmoid(ov[2 + p])

    (o_, y), _ = _rows_call(name, _tile(x.shape[0], tm), [yg, x], ws, [(2 * D_MODEL, BF16), (D_MODEL, F32)], [], body)
    return o_, y


def _glu_out_bwd(o_, dy, ws, y, u, d, *, name, tm=256):
    n = ws[0][1][2]

    def body(r, f, o, acc, s):
        o1 = r[0][:, 0:D_MODEL].astype(F32)
        sg = _sigmoid(r[0][:, D_MODEL:2 * D_MODEL].astype(F32))
        dyv = r[1][...]
        do1 = (dyv * sg).astype(BF16)
        do2 = (dyv * o1 * sg * (1.0 - sg)).astype(BF16)
        o[0][:, 0:D_MODEL] = do1
        o[0][:, D_MODEL:2 * D_MODEL] = do2
        dyg = None
        for p in range(N_CHIPS):
            t = _dot_nt((do1 if p < 2 else do2)[:, _cols(p % 2, n)], f[1 + p][...])
            dyg = t if dyg is None else dyg + t
        yv = r[2][...]
        dys = dyg * _gelu_grad(yv, _gelu(yv)[1])
        o[1][...] = dys.astype(BF16)
        o[2][...] = f[0][...] * dys
        acc[0][...] += jnp.sum(dys * r[3][...], axis=0, keepdims=True)

    (do, dys, dus), (dd,) = _rows_call(name, _tile(dy.shape[0], tm), [o_, dy, y, u], [d] + ws,
                                       [(2 * D_MODEL, BF16), (C_WIDTH, BF16), (C_WIDTH, F32)], [((1, C_WIDTH), F32)],
                                       body)
    return do, dys, dus, dd


def _s5_in_bwd(gs, bset, dus, ws, x, g, dres, *, name, tm=256):
    D = x.shape[1]
    tm = _tile(x.shape[0], tm)

    def body(r, f, o, acc, s):
        du0 = jnp.concatenate([_dot_nt(_state_set(r[0], tm, j), f[1][j]) for j in range(N_SETS)], axis=1)
        du = (du0 + r[1][...]).astype(BF16)
        o[0][...] = du
        dx, dg = _rms_bwd_tile(r[2][...], f[0][...], _cat_nt(du, f[2:]))
        o[1][...] = dx + r[3][...]
        acc[0][...] += dg

    (du, dx), (dg,) = _rows_call(name, tm, [gs, dus, x, dres], [_vec(g), bset] + ws,
                                 [(C_WIDTH, BF16), (D, F32)], [((1, D), F32)], body)
    return du, dx, dg


_SCAN_CHUNK = 256
_RE = slice(0, STATE_ROWS)
_IM = slice(STATE_ROWS, 2 * STATE_ROWS)
assert SCAN_BLOCK == 8


def _token(g, i, rows):
    return pl.ds(pl.multiple_of(g * (rows * SCAN_BLOCK), rows * SCAN_BLOCK) + i, rows, stride=SCAN_BLOCK)


def _fill_chunk(s3, a_ref, wset, tc, nt):
    for j in range(N_SETS):
        av = a_ref[:, _cols(j, SET_CH)].astype(BF16)
        y = _dot_nt(av, wset[j]) if nt else _dot(av, wset[j])
        for k, c in enumerate(_set_groups(j)):
            s3[:, 8 * c:8 * (c + 1), :] = y[:, _cols(k, STATE_LANES)].reshape(tc // 8, 8, STATE_LANES)


def _chunk_token(s3, g, i):
    return s3[g, pl.ds(i, _STATE_TILE, stride=SCAN_BLOCK), :]


def _scan_fwd(u, bset, pw, *, name):
    S = u.shape[0]
    tc = _tile(S, _SCAN_CHUNK, 8)

    def body(u_ref, bset_ref, pw_ref, xs_ref, st_ref, s3):
        @pl.when(pl.program_id(0) == 0)
        def _():
            st_ref[...] = jnp.zeros_like(st_ref)

        _fill_chunk(s3, u_ref, bset_ref, tc, nt=False)
        ar = pw_ref[0, _RE, :]
        ai = pw_ref[0, _IM, :]

        def block(g, carry):
            xr, xi = carry
            cr = ci = nr = ni = None
            for j in range(SCAN_BLOCK):
                b = _chunk_token(s3, g, j)
                br, bi = b[_RE], b[_IM]
                cr, ci = (br, bi) if j == 0 else (ar * cr - ai * ci + br, ar * ci + ai * cr + bi)
                pr, pi = pw_ref[j, _RE, :], pw_ref[j, _IM, :]
                nr = pr * xr - pi * xi + cr
                ni = pr * xi + pi * xr + ci
                xs_ref[_token(g, j, STATE_ROWS), :] = _pack_state(nr, ni)
            return nr, ni

        xr, xi = lax.fori_loop(0, tc // SCAN_BLOCK, block, (st_ref[_RE, :], st_ref[_IM, :]), unroll=4)
        st_ref[_RE, :] = xr
        st_ref[_IM, :] = xi

    return pl.pallas_call(
        body, name=name, grid=(S // tc,),
        in_specs=[pl.BlockSpec((tc, u.shape[1]), lambda i: (i, 0)), pl.BlockSpec(bset.shape, lambda i: (0, 0, 0)),
                  pl.BlockSpec(pw.shape, lambda i: (0, 0, 0))],
        out_specs=pl.BlockSpec((tc * STATE_ROWS, STATE_LANES), lambda i: (i, 0)),
        out_shape=jax.ShapeDtypeStruct((S * STATE_ROWS, STATE_LANES), jnp.uint32),
        scratch_shapes=[pltpu.VMEM((2 * STATE_ROWS, STATE_LANES), F32),
                        pltpu.VMEM((tc // 8, _STATE_TILE * 8, STATE_LANES), F32)],
        compiler_params=_params(("arbitrary",)),
    )(u, bset, pw)


def _scan_bwd(dys, cset, xs, pw, *, name):
    S = dys.shape[0]
    tc = _tile(S, _SCAN_CHUNK, 8)
    nc = S // tc

    def body(dys_ref, cset_ref, xs_ref, pw_ref, g_ref, da_ref, st_ref, s3):
        @pl.when(pl.program_id(0) == 0)
        def _():
            st_ref[...] = jnp.zeros_like(st_ref)
            da_ref[...] = jnp.zeros_like(da_ref)

        _fill_chunk(s3, dys_ref, cset_ref, tc, nt=True)
        ar = pw_ref[0, _RE, :]
        ai = pw_ref[0, _IM, :]

        def block(k, carry):
            gr, gi, dar, dai = carry
            g = tc // SCAN_BLOCK - 1 - k
            cr = ci = None
            pgr, pgi = gr, gi
            for j in range(SCAN_BLOCK):
                i = SCAN_BLOCK - 1 - j
                xr, xi = _unpack_state(xs_ref[_token(g, i, STATE_ROWS), :])
                dar = dar + pgr * xr + pgi * xi
                dai = dai + pgi * xr - pgr * xi
                d = _chunk_token(s3, g, i)
                dr, di = d[_RE], d[_IM]
                cr, ci = (dr, di) if j == 0 else (ar * cr + ai * ci + dr, ar * ci - ai * cr + di)
                pr, pi = pw_ref[j, _RE, :], pw_ref[j, _IM, :]
                pgr = pr * gr + pi * gi + cr
                pgi = pr * gi - pi * gr + ci
                g_ref[_token(g, i, STATE_ROWS), :] = _pack_state(pgr, pgi)
            return pgr, pgi, dar, dai

        init = (st_ref[_RE, :], st_ref[_IM, :], da_ref[_RE, :], da_ref[_IM, :])
        gr, gi, dar, dai = lax.fori_loop(0, tc // SCAN_BLOCK, block, init, unroll=4)
        st_ref[_RE, :] = gr
        st_ref[_IM, :] = gi
        da_ref[_RE, :] = dar
        da_ref[_IM, :] = dai

    packed = pl.BlockSpec((tc * STATE_ROWS, STATE_LANES), lambda i: (nc - 1 - i, 0))
    vec = pl.BlockSpec((2 * STATE_ROWS, STATE_LANES), lambda i: (0, 0))
    return pl.pallas_call(
        body, name=name, grid=(nc,),
        in_specs=[pl.BlockSpec((tc, dys.shape[1]), lambda i: (nc - 1 - i, 0)),
                  pl.BlockSpec(cset.shape, lambda i: (0, 0, 0)), packed, pl.BlockSpec(pw.shape, lambda i: (0, 0, 0))],
        out_specs=[packed, vec],
        out_shape=[jax.ShapeDtypeStruct(xs.shape, jnp.uint32), jax.ShapeDtypeStruct((2 * STATE_ROWS, STATE_LANES), F32)],
        scratch_shapes=[pltpu.VMEM((2 * STATE_ROWS, STATE_LANES), F32),
                        pltpu.VMEM((tc // 8, _STATE_TILE * 8, STATE_LANES), F32)],
        compiler_params=_params(("arbitrary",)),
    )(dys, cset, xs, pw)


def _down_loss_head(h, ws, x, g, target, *, name, tm=512):
    S, D = x.shape

    def body(r, f, o, acc, s):
        xv = r[1][...]
        for p in range(N_CHIPS):
            xv = xv + _dot(r[0][p], f[1 + p][...])
        gv = f[0][...]
        rs = lax.rsqrt(_mean(xv * xv) + EPS)
        xh = xv * rs
        err = xh * gv - r[2][...]
        acc[1][...] += 0.5 * jnp.sum(_mean(err * err), axis=0, keepdims=True)
        dy = err * (1.0 / D)
        dyg = dy * gv
        o[0][...] = rs * (dyg - xh * _mean(dyg * xh))
        acc[0][...] += jnp.sum(dy * xh, axis=0, keepdims=True)

    (dx,), (dg, loss) = _rows_call(name, _tile(S, tm), [h, x, target], [_vec(g)] + ws, [(D, F32)],
                                   [((1, D), F32), ((1, 128), F32)], body)
    return dx, dg, loss


_ADAM_C1 = 1.0 - ADAM_B1 ** ADAM_STEP
_ADAM_C2 = 1.0 - ADAM_B2 ** ADAM_STEP
_ONE_BLOCK_BYTES = 8 * 1024 * 1024


def _adamw_math(w, g, m, v):
    nm = ADAM_B1 * m + (1.0 - ADAM_B1) * g
    nv = ADAM_B2 * v + (1.0 - ADAM_B2) * (g * g)
    m_hat = nm / _ADAM_C1
    v_hat = nv / _ADAM_C2
    return -ADAM_LR * (m_hat / (jnp.sqrt(v_hat) + ADAM_EPS) + ADAM_WD * w), nm, nv


def _adamw_shard(w, gsrc, m, v, *, name):
    R, C = w.shape
    n_l = len(gsrc)
    rows = R // n_l
    tr = rows
    for _, r0 in gsrc:
        tr = math.gcd(tr, r0) if r0 else tr
    tr = _tile(tr, 256, 8) if tr > 256 else tr
    nb = rows // tr
    assert rows % tr == 0 and all(r0 % tr == 0 for _, r0 in gsrc)

    def body(*refs):
        w_ref, g_refs, (m_ref, v_ref, go_ref, d_ref, nm_ref, nv_ref) = refs[0], refs[1:1 + n_l], refs[1 + n_l:]
        layer = pl.program_id(0) // nb
        gv = g_refs[0][...]
        for l in range(1, n_l):
            gv = jnp.where(layer == l, g_refs[l][...], gv)
        go_ref[...] = gv
        d_ref[...], nm_ref[...], nv_ref[...] = _adamw_math(w_ref[...], gv, m_ref[...], v_ref[...])

    def g_spec(l, r0):
        return pl.BlockSpec((tr, C), lambda i: (r0 // tr + jnp.clip(i - l * nb, 0, nb - 1), 0))

    blk = pl.BlockSpec((tr, C), lambda i: (i, 0))
    out = jax.ShapeDtypeStruct((R, C), F32)
    return pl.pallas_call(
        body, name=name, grid=(R // tr,),
        in_specs=[blk] + [g_spec(l, r0) for l, (_, r0) in enumerate(gsrc)] + [blk, blk], out_specs=[blk] * 4,
        out_shape=[out] * 4, compiler_params=_params(("parallel",)),
    )(w, *[g for g, _ in gsrc], m, v)


def _adamw_small(ws, gs, ms, vs, *, name):
    n = len(ws)

    def body(*refs):
        w_r, g_r, m_r, v_r = refs[:n], refs[n:2 * n], refs[2 * n:3 * n], refs[3 * n:4 * n]
        d_r, nm_r, nv_r = refs[4 * n:5 * n], refs[5 * n:6 * n], refs[6 * n:7 * n]
        for k in range(n):
            d_r[k][...], nm_r[k][...], nv_r[k][...] = _adamw_math(w_r[k][...], g_r[k][...], m_r[k][...], v_r[k][...])

    vm = pl.BlockSpec(memory_space=pltpu.VMEM)
    out = [jax.ShapeDtypeStruct(w.shape, F32) for w in ws]
    res = pl.pallas_call(body, name=name, in_specs=[vm] * (4 * n), out_specs=[vm] * (3 * n), out_shape=out * 3,
                         compiler_params=pltpu.CompilerParams(vmem_limit_bytes=VMEM_LIMIT))(*ws, *gs, *ms, *vs)
    return res[:n], res[n:2 * n], res[2 * n:]


def _sum_slots(x, *, name):
    n, R, C = x.shape
    tr = R if (n + 1) * R * C * 4 <= _ONE_BLOCK_BYTES else _tile(R, 256, 8)

    def body(x_ref, o_ref):
        acc = x_ref[0]
        for k in range(1, n):
            acc = acc + x_ref[k]
        o_ref[...] = acc

    return pl.pallas_call(
        body, name=name, grid=(R // tr,),
        in_specs=[pl.BlockSpec((n, tr, C), lambda i: (0, i, 0))], out_specs=pl.BlockSpec((tr, C), lambda i: (i, 0)),
        out_shape=jax.ShapeDtypeStruct((R, C), F32), compiler_params=_params(("parallel",)),
    )(x)


def _pair_sum(g, r, where, *, name):
    n, R, C = g.shape
    Rh = R // 2
    tr = _tile(Rh, 256, 8)
    nb = Rh // tr

    def body(where_ref, g_ref, r_ref, o_ref):
        o_ref[...] = (g_ref[...] + r_ref[...]).astype(BF16)

    def slot(p, w):
        return p + jnp.where(p >= w[0], 1, 0)

    return pl.pallas_call(
        body, name=name,
        grid_spec=pltpu.PrefetchScalarGridSpec(
            num_scalar_prefetch=1, grid=(n - 1, nb),
            in_specs=[pl.BlockSpec((1, tr, C), lambda p, i, w: (slot(p, w), w[1] * nb + i, 0)),
                      pl.BlockSpec((1, tr, C), lambda p, i, w: (slot(p, w), i, 0))],
            out_specs=pl.BlockSpec((1, tr, C), lambda p, i, w: (slot(p, w), i, 0)),
        ),
        out_shape=jax.ShapeDtypeStruct((n, Rh, C), BF16), compiler_params=_params(("parallel", "parallel")),
    )(where, g, r)


def _chip_sum(g, r, slots, where, *, name):
    n, R, C = g.shape
    Rh = R // 2
    tr = _tile(Rh, 256, 8)
    nb = Rh // tr

    def body(w_ref, g_ref, r_ref, s_ref, o_ref):
        acc = g_ref[0] + r_ref[0]
        for k in range(slots.shape[0]):
            acc = acc + s_ref[k].astype(F32)
        o_ref[...] = acc

    return pl.pallas_call(
        body, name=name,
        grid_spec=pltpu.PrefetchScalarGridSpec(
            num_scalar_prefetch=1, grid=(nb,),
            in_specs=[pl.BlockSpec((1, tr, C), lambda i, w: (w[0], w[1] * nb + i, 0)),
                      pl.BlockSpec((1, tr, C), lambda i, w: (w[0], i, 0)),
                      pl.BlockSpec((slots.shape[0], tr, C), lambda i, w: (0, i, 0))],
            out_specs=pl.BlockSpec((tr, C), lambda i, w: (w[1] * nb + i, 0)),
        ),
        out_shape=jax.ShapeDtypeStruct((R, C), F32), compiler_params=_params(("parallel",)),
    )(where, g, r, slots)


ANY = pl.BlockSpec(memory_space=pl.ANY)


def _place():
    return lax.axis_index("x"), lax.axis_index("y"), lax.axis_index("c")


def _other_chips(x, y):
    return [(1 - x, y), (x, 1 - y), (1 - x, 1 - y)]


def _aliased_comm_call(body, bufs, n_sems, *, name):
    n = len(bufs)
    return pl.pallas_call(
        body, name=name, out_shape=[jax.ShapeDtypeStruct(b.shape, b.dtype) for b in bufs],
        in_specs=[ANY] * n, out_specs=[ANY] * n, input_output_aliases={k: k for k in range(n)},
        scratch_shapes=[pltpu.SemaphoreType.DMA((n_sems,)), pltpu.SemaphoreType.DMA((n_sems,))],
    )(*bufs)


HBM = pl.BlockSpec(memory_space=pltpu.HBM)
SEM = pl.BlockSpec(memory_space=pltpu.SEMAPHORE)
_SPLIT = pltpu.CompilerParams(has_side_effects=pltpu.SideEffectType.DATAFLOW_SIDE_EFFECTING)


def _in_hbm(arrs):
    return [pltpu.with_memory_space_constraint(a, pltpu.HBM) for a in arrs]


def _gather_ici_start(bufs, after, *, name):
    n = len(bufs)

    def body(*refs):
        send_sems, recv_sems, outs, token = refs[n + 1], refs[n + 2], refs[n + 3:2 * n + 3], refs[2 * n + 3]
        x, y, c = _place()
        for b in range(n):
            rh = bufs[b].shape[1] // 2
            part = outs[b].at[2 * x + y, pl.ds(c * rh, rh), :]
            for j, chip in enumerate(_other_chips(x, y)):
                pltpu.make_async_remote_copy(src_ref=part, dst_ref=part, send_sem=send_sems.at[3 * b + j],
                                             recv_sem=recv_sems.at[3 * b + j], device_id=(*chip, c),
                                             device_id_type=MESH).start()
        token[...] = jnp.zeros_like(token)

    res = pl.pallas_call(
        body, name=name,
        out_shape=(pltpu.SemaphoreType.DMA((3 * n,)), pltpu.SemaphoreType.DMA((3 * n,)),
                   *[pltpu.HBM(b.shape, b.dtype) for b in bufs], jax.ShapeDtypeStruct((8, 128), F32)),
        in_specs=[HBM] * n + [ANY], out_specs=(SEM, SEM, *[HBM] * n, pl.BlockSpec(memory_space=pltpu.VMEM)),
        input_output_aliases={k: k + 2 for k in range(n)}, compiler_params=_SPLIT,
    )(*_in_hbm(bufs), after)
    return res[0], res[1], list(res[2:2 + n]), res[2 + n]


def _gather_ici_wait(send_sems, recv_sems, bufs, after, *, name):
    n = len(bufs)

    def body(*refs):
        ins, ss, rs = refs[:n], refs[n], refs[n + 1]
        x, y, c = _place()
        for b in range(n):
            rh = bufs[b].shape[1] // 2
            mine = ins[b].at[2 * x + y, pl.ds(c * rh, rh), :]
            for j, (cx, cy) in enumerate(_other_chips(x, y)):
                theirs = ins[b].at[2 * cx + cy, pl.ds(c * rh, rh), :]
                cp = pltpu.make_async_remote_copy(src_ref=mine, dst_ref=theirs, send_sem=ss.at[3 * b + j],
                                                  recv_sem=rs.at[3 * b + j], device_id=(cx, cy, c),
                                                  device_id_type=MESH)
                cp.wait_send()
                cp.wait_recv()

    return list(pl.pallas_call(
        body, name=name, out_shape=[pltpu.HBM(b.shape, b.dtype) for b in bufs],
        in_specs=[HBM] * n + [SEM, SEM, ANY], out_specs=[HBM] * n,
        input_output_aliases={k: k for k in range(n)}, compiler_params=_SPLIT,
    )(*bufs, send_sems, recv_sems, after))


def _gather_forward(bufs, *, name):
    n = len(bufs)

    def body(*refs):
        outs, send_sems, recv_sems = refs[n:2 * n], refs[2 * n], refs[2 * n + 1]
        x, y, c = _place()

        def copy(b, j, chip, hc):
            rh = bufs[b].shape[1] // 2
            part = outs[b].at[2 * chip[0] + chip[1], pl.ds(hc * rh, rh), :]
            return pltpu.make_async_remote_copy(src_ref=part, dst_ref=part, send_sem=send_sems.at[3 * b + j],
                                                recv_sem=recv_sems.at[3 * b + j], device_id=(x, y, 1 - c),
                                                device_id_type=MESH)

        sends = [copy(b, j, chip, c) for b in range(n) for j, chip in enumerate(_other_chips(x, y))]
        for cp in sends:
            cp.start()
        for b in range(n):
            for j, chip in enumerate(_other_chips(x, y)):
                copy(b, j, chip, 1 - c).wait_recv()
        for cp in sends:
            cp.wait_send()

    return _aliased_comm_call(body, bufs, 3 * n, name=name)


def _chip_exchange_start(hs, *, name):
    n = len(hs)
    lands = [lax.empty((3,) + h.shape[1:], h.dtype) for h in hs]

    def body(*refs):
        send_sems, recv_sems = refs[2 * n], refs[2 * n + 1]
        h_out, l_out, token = refs[2 * n + 2:3 * n + 2], refs[3 * n + 2:4 * n + 2], refs[4 * n + 2]
        x, y, c = _place()
        for b in range(n):
            for j, (cx, cy) in enumerate(_other_chips(x, y)):
                pltpu.make_async_remote_copy(src_ref=h_out[b].at[2 * cx + cy], dst_ref=l_out[b].at[j],
                                             send_sem=send_sems.at[3 * b + j], recv_sem=recv_sems.at[3 * b + j],
                                             device_id=(cx, cy, c), device_id_type=MESH).start()
        token[...] = jnp.zeros_like(token)

    res = pl.pallas_call(
        body, name=name,
        out_shape=(pltpu.SemaphoreType.DMA((3 * n,)), pltpu.SemaphoreType.DMA((3 * n,)),
                   *[pltpu.HBM(a.shape, a.dtype) for a in hs + lands], jax.ShapeDtypeStruct((8, 128), F32)),
        in_specs=[HBM] * (2 * n), out_specs=(SEM, SEM, *[HBM] * (2 * n), pl.BlockSpec(memory_space=pltpu.VMEM)),
        input_output_aliases={k: k + 2 for k in range(2 * n)}, compiler_params=_SPLIT,
    )(*_in_hbm(hs + lands))
    return res[0], res[1], list(res[2:2 + n]), list(res[2 + n:2 + 2 * n]), res[2 + 2 * n]


def _chip_exchange_wait(send_sems, recv_sems, hs, lands, after, *, name):
    n = len(hs)

    def body(*refs):
        h_in, l_in, ss, rs = refs[:n], refs[n:2 * n], refs[2 * n], refs[2 * n + 1]
        x, y, c = _place()
        for b in range(n):
            for j, (cx, cy) in enumerate(_other_chips(x, y)):
                cp = pltpu.make_async_remote_copy(src_ref=h_in[b].at[2 * cx + cy], dst_ref=l_in[b].at[j],
                                                  send_sem=ss.at[3 * b + j], recv_sem=rs.at[3 * b + j],
                                                  device_id=(cx, cy, c), device_id_type=MESH)
                cp.wait_send()
                cp.wait_recv()

    res = pl.pallas_call(
        body, name=name, out_shape=[pltpu.HBM(a.shape, a.dtype) for a in hs + lands],
        in_specs=[HBM] * (2 * n) + [SEM, SEM, ANY], out_specs=[HBM] * (2 * n),
        input_output_aliases={k: k for k in range(2 * n)}, compiler_params=_SPLIT,
    )(*hs, *lands, send_sems, recv_sems, after)
    return list(res[n:])


def _peers(x, y, c):
    return [((1 - x) if fx else x, (1 - y) if fy else y, (1 - c) if fc else c)
            for fx in (0, 1) for fy in (0, 1) for fc in (0, 1) if fx or fy or fc]


def _all_to_all_start(slab, after, *, name):
    land = lax.empty((N_DEV,) + slab.shape, slab.dtype)

    def body(slab_in, land_in, after_ref, send_sems, recv_sems, slab_out, land_out, token):
        x, y, c = _place()
        for k, peer in enumerate(_peers(x, y, c)):
            pltpu.make_async_remote_copy(src_ref=slab_out, dst_ref=land_out.at[4 * x + 2 * y + c],
                                         send_sem=send_sems.at[k], recv_sem=recv_sems.at[k], device_id=peer,
                                         device_id_type=MESH).start()
        token[...] = jnp.zeros_like(token)

    return pl.pallas_call(
        body, name=name,
        out_shape=(pltpu.SemaphoreType.DMA((N_DEV - 1,)), pltpu.SemaphoreType.DMA((N_DEV - 1,)),
                   pltpu.HBM(slab.shape, slab.dtype), pltpu.HBM(land.shape, land.dtype),
                   jax.ShapeDtypeStruct((8, 128), F32)),
        in_specs=[HBM, HBM, ANY], out_specs=(SEM, SEM, HBM, HBM, pl.BlockSpec(memory_space=pltpu.VMEM)),
        input_output_aliases={0: 2, 1: 3}, compiler_params=_SPLIT,
    )(*_in_hbm([slab, land]), after)


def _all_to_all_wait(send_sems, recv_sems, slab, land, after, *, name):
    def body(slab_in, land_in, ss, rs, after_ref, slab_out, land_out):
        x, y, c = _place()
        for k, (px, py, pc) in enumerate(_peers(x, y, c)):
            cp = pltpu.make_async_remote_copy(src_ref=slab_in, dst_ref=land_in.at[4 * px + 2 * py + pc],
                                              send_sem=ss.at[k], recv_sem=rs.at[k], device_id=(px, py, pc),
                                              device_id_type=MESH)
            cp.wait_send()
            cp.wait_recv()

    return pl.pallas_call(
        body, name=name, out_shape=[pltpu.HBM(slab.shape, slab.dtype), pltpu.HBM(land.shape, land.dtype)],
        in_specs=[HBM, HBM, SEM, SEM, ANY], out_specs=[HBM, HBM], input_output_aliases={0: 0, 1: 1},
        compiler_params=_SPLIT,
    )(slab, land, send_sems, recv_sems, after)


def _pair_exchange_start(gs, *, name):
    n = len(gs)
    lands = [lax.empty((g.shape[0], g.shape[1] // 2, g.shape[2]), g.dtype) for g in gs]

    def body(*refs):
        send_sems, recv_sems = refs[2 * n], refs[2 * n + 1]
        g_out, l_out, token = refs[2 * n + 2:3 * n + 2], refs[3 * n + 2:4 * n + 2], refs[4 * n + 2]
        x, y, c = _place()
        for b in range(n):
            rh = gs[b].shape[1] // 2
            pltpu.make_async_remote_copy(src_ref=g_out[b].at[:, pl.ds((1 - c) * rh, rh), :], dst_ref=l_out[b],
                                         send_sem=send_sems.at[b], recv_sem=recv_sems.at[b],
                                         device_id=(x, y, 1 - c), device_id_type=MESH).start()
        token[...] = jnp.zeros_like(token)

    res = pl.pallas_call(
        body, name=name,
        out_shape=(pltpu.SemaphoreType.DMA((n,)), pltpu.SemaphoreType.DMA((n,)),
                   *[pltpu.HBM(a.shape, a.dtype) for a in gs + lands], jax.ShapeDtypeStruct((8, 128), F32)),
        in_specs=[HBM] * (2 * n), out_specs=(SEM, SEM, *[HBM] * (2 * n), pl.BlockSpec(memory_space=pltpu.VMEM)),
        input_output_aliases={k: k + 2 for k in range(2 * n)}, compiler_params=_SPLIT,
    )(*_in_hbm(gs + lands))
    return res[0], res[1], list(res[2:2 + n]), list(res[2 + n:2 + 2 * n]), res[2 + 2 * n]


def _pair_exchange_wait(send_sems, recv_sems, gs, lands, after, *, name):
    n = len(gs)

    def body(*refs):
        g_in, l_in, ss, rs = refs[:n], refs[n:2 * n], refs[2 * n], refs[2 * n + 1]
        x, y, c = _place()
        for b in range(n):
            rh = gs[b].shape[1] // 2
            cp = pltpu.make_async_remote_copy(src_ref=g_in[b].at[:, pl.ds((1 - c) * rh, rh), :], dst_ref=l_in[b],
                                              send_sem=ss.at[b], recv_sem=rs.at[b], device_id=(x, y, 1 - c),
                                              device_id_type=MESH)
            cp.wait_send()
            cp.wait_recv()

    res = pl.pallas_call(
        body, name=name, out_shape=[pltpu.HBM(a.shape, a.dtype) for a in gs + lands],
        in_specs=[HBM] * (2 * n) + [SEM, SEM, ANY], out_specs=[HBM] * (2 * n),
        input_output_aliases={k: k for k in range(2 * n)}, compiler_params=_SPLIT,
    )(*gs, *lands, send_sems, recv_sems, after)
    return list(res[:n]), list(res[n:])


def _pair_share_start(ss, *, name):
    n = len(ss)

    def body(*refs):
        send_sems, recv_sems, outs, token = refs[n], refs[n + 1], refs[n + 2:2 * n + 2], refs[2 * n + 2]
        x, y, c = _place()
        for b in range(n):
            rh = ss[b].shape[0] // 2
            mine = outs[b].at[pl.ds(c * rh, rh), :]
            pltpu.make_async_remote_copy(src_ref=mine, dst_ref=mine, send_sem=send_sems.at[b],
                                         recv_sem=recv_sems.at[b], device_id=(x, y, 1 - c),
                                         device_id_type=MESH).start()
        token[...] = jnp.zeros_like(token)

    res = pl.pallas_call(
        body, name=name,
        out_shape=(pltpu.SemaphoreType.DMA((n,)), pltpu.SemaphoreType.DMA((n,)),
                   *[pltpu.HBM(a.shape, a.dtype) for a in ss], jax.ShapeDtypeStruct((8, 128), F32)),
        in_specs=[HBM] * n, out_specs=(SEM, SEM, *[HBM] * n, pl.BlockSpec(memory_space=pltpu.VMEM)),
        input_output_aliases={k: k + 2 for k in range(n)}, compiler_params=_SPLIT,
    )(*_in_hbm(ss))
    return res[0], res[1], list(res[2:2 + n]), res[2 + n]


def _pair_share_wait(send_sems, recv_sems, ss, after, *, name):
    n = len(ss)

    def body(*refs):
        ins, sems_s, sems_r = refs[:n], refs[n], refs[n + 1]
        x, y, c = _place()
        for b in range(n):
            rh = ss[b].shape[0] // 2
            mine = ins[b].at[pl.ds(c * rh, rh), :]
            theirs = ins[b].at[pl.ds((1 - c) * rh, rh), :]
            cp = pltpu.make_async_remote_copy(src_ref=mine, dst_ref=theirs, send_sem=sems_s.at[b],
                                              recv_sem=sems_r.at[b], device_id=(x, y, 1 - c),
                                              device_id_type=MESH)
            cp.wait_send()
            cp.wait_recv()

    return list(pl.pallas_call(
        body, name=name, out_shape=[pltpu.HBM(a.shape, a.dtype) for a in ss],
        in_specs=[HBM] * n + [SEM, SEM, ANY], out_specs=[HBM] * n,
        input_output_aliases={k: k for k in range(n)}, compiler_params=_SPLIT,
    )(*ss, send_sems, recv_sems, after))


_SMALL_SHARDED = (("e_conv_w", 2), ("o_norm", 1), ("o_d", 1))
_REPLICATED = ("e_norm", "e_gmlp_w", "e_gmlp_b", "e_conv_b", "e_conv_ln_g", "e_conv_ln_b", "o_lam_re", "o_lam_im",
               "o_log_dt", "o_b_re", "o_b_im", "o_c_re", "o_c_im", "ca_norm", "ca_mem_norm", "ffn_norm", "final_norm")
_SMALL = tuple(n for n, _ in _SMALL_SHARDED) + _REPLICATED
_WEIGHTS = ("e_norm", "e_w_in", "e_gmlp_w", "e_gmlp_b", "e_conv_w", "e_conv_b", "e_conv_ln_g", "e_conv_ln_b",
            "e_w_out", "o_norm", "o_w_in", "o_lam_re", "o_lam_im", "o_log_dt", "o_b_re", "o_b_im", "o_c_re", "o_c_im",
            "o_d", "o_w_out", "ca_norm", "ca_mem_norm", "ca_wq", "ca_wk", "ca_wv", "ca_wo", "ffn_norm", "ffn_w_gate",
            "ffn_w_up", "ffn_w_down", "final_norm")


def _pack_rows(arrs, width, dtype, row_mult=8):
    parts, spans, r0 = [], [], 0
    for a in arrs:
        flat = a.reshape(-1).astype(dtype)
        rows = -(-flat.shape[0] // (width * row_mult)) * row_mult
        if rows * width != flat.shape[0]:
            flat = jnp.pad(flat, (0, rows * width - flat.shape[0]))
        parts.append(flat.reshape(rows, width))
        spans.append((r0, rows))
        r0 += rows
    return jnp.concatenate(parts, axis=0), spans


def _unpack_rows(slab, spans, shapes):
    out = []
    for (r0, rows), shp in zip(spans, shapes):
        n = math.prod(shp)
        out.append(slab[r0:r0 + rows].reshape(-1)[:n].reshape(shp))
    return out


def _two_d(a):
    return a.reshape(-1, a.shape[-1])


def _shard_rows(n, a):
    return _two_d(jnp.swapaxes(a, -1, -2) if n in _TRANSPOSED else a)


def _from_shard_rows(n, rows, shape):
    if n in _TRANSPOSED:
        return jnp.swapaxes(rows.reshape(shape[:-2] + (shape[-1], shape[-2])), -1, -2)
    return rows.reshape(shape)


def _local_slab(local, slab, dtype):
    parts = sorted((r0, n, l) for n, (_, where) in _PLACE.items() for l, (s, r0) in enumerate(where) if s == slab)
    shards = [_shard_rows(n, local[n] if len(_PLACE[n][1]) == 1 else local[n][l]) for _, n, l in parts]
    return jnp.concatenate([a.astype(dtype) for a in shards], axis=0)


def _set_diag(b, pattern):
    return jnp.einsum(pattern, b, jnp.eye(C_GROUPS // N_SETS, dtype=b.dtype))


def _s5_discretize(lam_re, lam_im, log_dt, b_re, b_im):
    dt = jnp.exp(log_dt)[:, None]
    mag = jnp.exp(lam_re * dt)
    ar = mag * jnp.cos(lam_im * dt)
    ai = mag * jnp.sin(lam_im * dt)
    den = lam_re * lam_re + lam_im * lam_im
    qr = ((ar - 1.0) * lam_re + ai * lam_im) / den
    qi = (ai * lam_re - (ar - 1.0) * lam_im) / den
    bbr = qr[..., None] * b_re - qi[..., None] * b_im
    bbi = qr[..., None] * b_im + qi[..., None] * b_re
    return ar, ai, bbr, bbi


def _attention_block(x, mem, W, w, i, tag):
    xn, q = _norm_mm(x, w["ca_norm"][i], _shards(W, "ca_wq", i), split="k", out_dtype=BF16, name=f"{tag}_q")
    memn = _rms_fwd(mem, w["ca_mem_norm"][i], name=f"{tag}_ca_memnorm")
    k = _mm_k(memn, _shards(W, "ca_wk", i), out_dtype=BF16, name=f"{tag}_k")
    v = _mm_k(memn, _shards(W, "ca_wv", i), out_dtype=BF16, name=f"{tag}_v")
    o = _attn_fwd(q, k, v, name=f"{tag}_attn")
    y = _mm_k(o, _shards(W, "ca_wo", i), add=x, name=f"{tag}_wo")
    return y, (x, xn, memn, q, k, v, o)


def _attention_block_bwd(dy, saved, mem, W, w, i, tag, G, grads, token=None, mid=None):
    x, xn, memn, q, k, v, o = saved
    gain = w["ca_norm"][i]
    if token is not None:
        k = _behind(k, token)
    G = _grad_to_slab(G, "ca_wo", i, o, dy, a_cols=256, name=f"{tag}_dwo")
    dq, dk, dv = _attn_bwd(dy, _shards(W, "ca_wo", i), q, k, v, name=f"{tag}_attn_bwd")
    token = mid(dq) if mid is not None else None
    if token is not None:
        gain = _behind(gain, token)
    G = _grad_to_slab(G, "ca_wq", i, xn, dq, a_cols=256, name=f"{tag}_dwq")
    G = _grad_to_slab(G, "ca_wk", i, memn, dk, a_cols=256, name=f"{tag}_dwk")
    G = _grad_to_slab(G, "ca_wv", i, memn, dv, a_cols=256, name=f"{tag}_dwv")
    dmemn = _mm_k_t([(dk, _shards(W, "ca_wk", i)), (dv, _shards(W, "ca_wv", i))], name=f"{tag}_dmemn")
    dx, dg = _norm_bwd_k(dq, _shards(W, "ca_wq", i), x, gain, dy, name=f"{tag}_dq_norm_bwd")
    grads["ca_norm"][i] = dg[0]
    grads["ca_mem_norm"][i] = _rms_dg(mem, w["ca_mem_norm"][i], dmemn, name=f"{tag}_ca_memnorm_bwd")[0]
    return dx, G


def _ffn_block(x, W, w, i, tag, head=None):
    fn, gate, up, h = _ffn_up(x, w["ffn_norm"][i], _shards(W, "ffn_w_gate", i), _shards(W, "ffn_w_up", i),
                              name=f"{tag}_ffn_up")
    if head is None:
        y = _mm_k(h, _shards(W, "ffn_w_down", i), add=x, name=f"{tag}_down")
    else:
        y = _down_loss_head(h, _shards(W, "ffn_w_down", i), x, *head, name=f"{tag}_down_loss_head")
    return y, (x, fn, gate, up, h)


def _ffn_block_bwd(dy, saved, W, w, i, tag, G, grads, token=None, mid=None):
    x, fn, gate, up, h = saved
    gain = w["ffn_norm"][i]
    G = _grad_to_slab(G, "ffn_w_down", i, h, dy, name=f"{tag}_dwd")
    dg, du = _ffn_bwd_hidden(dy, _shards(W, "ffn_w_down", i), gate, up, token, name=f"{tag}_ffn_bwd_hidden")
    token = mid(dg) if mid is not None else None
    if token is not None:
        gain = _behind(gain, token)
    G = _grad_to_slab(G, "ffn_w_gate", i, dg, fn, name=f"{tag}_dwg")
    G = _grad_to_slab(G, "ffn_w_up", i, du, fn, name=f"{tag}_dwu")
    dx, dgn = _ffn_in_bwd(dg, du, _shards(W, "ffn_w_gate", i), _shards(W, "ffn_w_up", i), x, gain, dy,
                          name=f"{tag}_ffn_in_bwd")
    grads["ffn_norm"][i] = dgn[0]
    return dx, G


def _gmlp_mask():
    chunk = jnp.arange(GMLP_BLOCK) // CHUNK
    return chunk[None, :] <= chunk[:, None]


def _even_block(x, W, w, tag):
    hn, proj = _norm_mm(x, w["e_norm"][0], _shards(W, "e_w_in"), split="n", out_dtype=F32, name=f"{tag}_w_in")
    wm = jnp.where(_gmlp_mask()[None], w["e_gmlp_w"][0], 0.0).astype(BF16)
    bcol = w["e_gmlp_b"][0][:, :, None]
    cw = jnp.pad(w["e_conv_w"][0], ((0, CONV_HALO - CONV_WIDTH), (0, 0)))
    cb, lg, lb = w["e_conv_b"], w["e_conv_ln_g"], w["e_conv_ln_b"]
    mix, hc = _even_fwd(proj, wm, bcol, cw, cb, lg, lb, name=f"{tag}_mixers")
    y = _mm_k(mix, _shards(W, "e_w_out"), add=x, name=f"{tag}_w_out")
    return y, (x, hn, proj, mix, hc, wm, bcol, cw)


def _even_block_bwd(dy, saved, W, w, tag, G, grads):
    x, hn, proj, mix, hc, wm, bcol, cw = saved
    dmix = _mm_k_t([(dy, _shards(W, "e_w_out"))], name=f"{tag}_dmix")
    G = _grad_to_slab(G, "e_w_out", 0, mix, dy, a_cols=256, name=f"{tag}_dw_out")
    wmt = jnp.swapaxes(wm, 1, 2)
    dpa, dhc, dwm, db, dlg, dlb, dcb = _even_bwd1(proj, dmix, hc, wm, wmt, bcol, w["e_conv_ln_g"], w["e_conv_ln_b"],
                                                  name=f"{tag}_mixers_bwd1")
    dpb, dcw = _even_bwd2(proj, dhc, cw, name=f"{tag}_mixers_bwd2")
    grads["e_gmlp_w"] = jnp.where(_gmlp_mask()[None], dwm, 0.0)[None]
    grads["e_gmlp_b"] = db[:, :, 0][None]
    grads["e_conv_ln_g"], grads["e_conv_ln_b"], grads["e_conv_b"] = dlg, dlb, dcb
    grads["e_conv_w"] = dcw[:CONV_WIDTH][None]
    G = _grad_to_slab(G, "e_w_in", 0, hn, dpa, b_cols=512, chips=(0, 2), name=f"{tag}_dw_in_a")
    G = _grad_to_slab(G, "e_w_in", 0, hn, dpb, b_cols=512, chips=(2, 2), name=f"{tag}_dw_in_b")
    dx, dg = _norm_bwd_n((dpa, dpb), _shards(W, "e_w_in"), x, w["e_norm"][0], dy, name=f"{tag}_in_bwd")
    grads["e_norm"] = dg
    return dx, G


def _odd_block(x, W, w, tag):
    S = x.shape[0]
    hn, u = _norm_mm(x, w["o_norm"][0], _shards(W, "o_w_in"), split="k", out_dtype=F32, name=f"{tag}_w_in")
    disc_in = (w["o_lam_re"][0], w["o_lam_im"][0], w["o_log_dt"][0], w["o_b_re"][0], w["o_b_im"][0])
    (ar, ai, bbr, bbi), disc_vjp = jax.vjp(_s5_discretize, *disc_in)
    sets = (N_SETS, C_GROUPS // N_SETS)
    per_set = N_STATE // N_SETS
    bset = jnp.concatenate([_set_diag(b.reshape(sets + b.shape[1:]), "jgpc,gh->jgchp").reshape(N_SETS, SET_CH, per_set)
                            for b in (bbr, bbi)], axis=2).astype(BF16)
    cset = jnp.concatenate([_set_diag(c.reshape(sets + c.shape[1:]), "jgcp,gh->jgphc").reshape(N_SETS, per_set, SET_CH)
                            for c in (w["o_c_re"][0], -w["o_c_im"][0])], axis=1).astype(BF16)
    powers, pr, pi = [], ar, ai
    for _ in range(SCAN_BLOCK):
        powers.append(jnp.concatenate([pr.reshape(STATE_ROWS, STATE_LANES), pi.reshape(STATE_ROWS, STATE_LANES)], 0))
        pr, pi = pr * ar - pi * ai, pr * ai + pi * ar
    pw = jnp.stack(powers, axis=0)
    xs = _scan_fwd(u, bset, pw, name=f"{tag}_scan").reshape(S // 8, STATE_ROWS, 8, STATE_LANES)
    yv, yg = _s5_readout(xs, cset, u, w["o_d"], name=f"{tag}_readout")
    o, y = _glu_out(yg, _shards(W, "o_w_out"), x, name=f"{tag}_glu_out")
    return y, (x, hn, u, bset, cset, pw, xs, yv, yg, o, disc_vjp)


def _odd_block_bwd(dy, saved, W, w, tag, G, grads):
    x, hn, u, bset, cset, pw, xs, yv, yg, o, disc_vjp = saved
    S = x.shape[0]
    do, dys, dus, dd = _glu_out_bwd(o, dy, _shards(W, "o_w_out"), yv, u, w["o_d"], name=f"{tag}_glu_out_bwd")
    G = _grad_to_slab(G, "o_w_out", 0, yg, do, b_cols=512, name=f"{tag}_dw_out")
    grads["o_d"] = dd
    dcset_t = _state_grad_sets(dys, xs, name=f"{tag}_dcd")
    gs, da = _scan_bwd(dys, cset, xs.reshape(S * STATE_ROWS, STATE_LANES), pw, name=f"{tag}_scan_bwd")
    gs = gs.reshape(xs.shape)
    dbset = _state_grad_sets(u, gs, name=f"{tag}_dbd")
    du, dx, dg = _s5_in_bwd(gs, bset, dus, _shards(W, "o_w_in"), x, w["o_norm"][0], dy, name=f"{tag}_in_bwd")
    G = _grad_to_slab(G, "o_w_in", 0, hn, du, a_cols=256, name=f"{tag}_dw_in")
    grads["o_norm"] = dg
    per = C_GROUPS // N_SETS
    blocks = (N_SETS, per, C_GROUP_CH, 2, per, C_STATE)
    dc = _set_diag(dcset_t.reshape(blocks), "jhcrgp,gh->rjgcp").reshape(2, C_GROUPS, C_GROUP_CH, C_STATE)
    db = _set_diag(dbset.reshape(blocks), "jgcrhp,gh->rjgpc").reshape(2, C_GROUPS, C_STATE, C_GROUP_CH)
    dcr, dci, dbbr, dbbi = dc[0], -dc[1], db[0], db[1]
    dar = da[:STATE_ROWS].reshape(C_GROUPS, C_STATE)
    dai = da[STATE_ROWS:].reshape(C_GROUPS, C_STATE)
    dlr, dli, dldt, dbr, dbi = disc_vjp((dar, dai, dbbr, dbbi))
    grads["o_lam_re"], grads["o_lam_im"], grads["o_log_dt"] = dlr[None], dli[None], dldt[None]
    grads["o_b_re"], grads["o_b_im"], grads["o_c_re"], grads["o_c_im"] = dbr[None], dbi[None], dcr[None], dci[None]
    return dx, G


def _behind(value, token):
    return value + token[0, 0].astype(value.dtype)


class _NoExchange:
    def __init__(self, W):
        self.W = W

    def first_weights(self, w):
        return self.W, w

    def weights(self, stage, after):
        return {}

    def grads_ready(self, piece, G):
        return None

    def grads_crossed(self, piece, after):
        return None


def _forward_backward(xs_, mems_, tgt, w, G, exchange):
    W, w = exchange.first_weights(w)
    x1, s_mix0 = _even_block(xs_, W, w, "l0")
    W = {**W, **exchange.weights(1, x1)}
    x2, s_att0 = _attention_block(x1, mems_, W, w, 0, "l0")
    W = {**W, **exchange.weights(2, x2)}
    x3, s_ffn0 = _ffn_block(x2, W, w, 0, "l0")
    W = {**W, **exchange.weights(3, x3)}
    x4, s_mix1 = _odd_block(x3, W, w, "l1")
    x5, s_att1 = _attention_block(x4, mems_, W, w, 1, "l1")
    (dx, dfinal, loss_lanes), s_ffn1 = _ffn_block(x5, W, w, 1, "l1", head=(w["final_norm"], tgt))

    grads = {n: [None, None] for n in ("ca_norm", "ca_mem_norm", "ffn_norm")}
    grads["final_norm"] = dfinal[0]
    dx, G = _ffn_block_bwd(dx, s_ffn1, W, w, 1, "l1", G, grads)
    dx, G = _attention_block_bwd(dx, s_att1, mems_, W, w, 1, "l1", G, grads)
    dx, G = _odd_block_bwd(dx, s_mix1, W, w, "l1", G, grads)
    token = exchange.grads_ready("l1", G)
    dx, G = _ffn_block_bwd(dx, s_ffn0, W, w, 0, "l0", G, grads, token,
                           lambda after: exchange.grads_crossed("l1", after))
    token = exchange.grads_ready("ffn0", G)
    dx, G = _attention_block_bwd(dx, s_att0, mems_, W, w, 0, "l0", G, grads, token,
                                 lambda after: exchange.grads_crossed("ffn0", after))
    dx, G = _even_block_bwd(dx, s_mix0, W, w, "l0", G, grads)
    for n in list(grads):
        if isinstance(grads[n], list):
            grads[n] = jnp.stack(grads[n], axis=0)
        grads[n] = grads[n].reshape(w[n].shape)
    return loss_lanes, dx, G, grads


class _Exchange:
    def __init__(self, local, chip, core):
        self.bufs = {s: lax.dynamic_update_slice(lax.empty((N_CHIPS, rows, width), BF16),
                                                 _local_slab(local, s, BF16)[None], (chip, 0, 0))
                     for s, (width, rows) in _SLABS.items()}
        small = jnp.zeros((_SMALL_SLAB_ROWS, SMALL_W), F32)
        for n, (r0, rows) in _SMALL_PLACE.items():
            small = small.at[r0:r0 + rows].set(local[n].reshape(rows, SMALL_W))
        self.bufs[_SMALL_SLAB] = lax.dynamic_update_slice(lax.empty((N_CHIPS, _SMALL_SLAB_ROWS, SMALL_W), F32),
                                                          small[None], (chip, 0, 0))
        self.shard_shapes = {n: local[n].shape for n in _SMALL_PLACE}
        self.where = jnp.stack([chip, core]).astype(jnp.int32)
        self.flights = []
        self.reduces = {}

    def weights(self, stage, after):
        send_sems, recv_sems, bufs, _ = self.flights[stage]
        bufs = _gather_ici_wait(send_sems, recv_sems, bufs, after, name=f"gather_stage{stage}_wait")
        return dict(zip(self.stage_slabs(stage), _gather_forward(bufs, name=f"gather_stage{stage}_forward")))

    @staticmethod
    def stage_slabs(stage):
        return _STAGES[stage] + ((_SMALL_SLAB,) if stage == 0 else ())

    def first_weights(self, w):
        after = w["e_norm"]
        for k in range(len(_STAGES)):
            self.flights.append(_gather_ici_start([self.bufs[s] for s in self.stage_slabs(k)], after,
                                                  name=f"gather_stage{k}_start"))
            after = self.flights[-1][3]
        W = self.weights(0, after)
        w = {**w, "e_norm": _behind(w["e_norm"], after)}
        for (n, ax), (r0, rows) in zip(_SMALL_SHARDED, _SMALL_PLACE.values()):
            shards = [W[_SMALL_SLAB][p, r0:r0 + rows].reshape(self.shard_shapes[n]) for p in range(N_CHIPS)]
            w[n] = jnp.concatenate(shards, axis=ax)
        return W, w

    def pair_start(self, G, slabs, tag):
        send_sems, recv_sems, gl, lands, token = _pair_exchange_start([G[s] for s in slabs],
                                                                      name=f"grad_{tag}_pair_start")
        return (slabs, send_sems, recv_sems, gl, lands), token

    def pair_land(self, state, after, tag):
        slabs, send_sems, recv_sems, gl, lands = state
        gl, other = _pair_exchange_wait(send_sems, recv_sems, gl, lands, after, name=f"grad_{tag}_pair_wait")
        pairs = [_pair_sum(g, r, self.where, name=f"grad_pair_sum_{s}") for s, g, r in zip(slabs, gl, other)]
        send_sems, recv_sems, pairs, lands, token = _chip_exchange_start(pairs, name=f"grad_{tag}_chip_start")
        return (slabs, gl, other, send_sems, recv_sems, pairs, lands), token

    def reduce_sum(self, state, after, tag):
        slabs, gl, other, send_sems, recv_sems, pairs, lands = state
        slots = _chip_exchange_wait(send_sems, recv_sems, pairs, lands, after, name=f"grad_{tag}_chip_wait")
        return slabs, [_chip_sum(g, r, sl, self.where, name=f"grad_chip_sum_{s}")
                       for s, g, r, sl in zip(slabs, gl, other, slots)]

    @staticmethod
    def share_start(slabs, halves, tag):
        send_sems, recv_sems, halves, token = _pair_share_start(halves, name=f"grad_{tag}_share_start")
        return (slabs, send_sems, recv_sems, halves), token

    @staticmethod
    def share_finish(state, after, tag):
        slabs, send_sems, recv_sems, halves = state
        return dict(zip(slabs, _pair_share_wait(send_sems, recv_sems, halves, after, name=f"grad_{tag}_share_wait")))

    def grads_ready(self, piece, G):
        self.reduces[piece], token = self.pair_start(G, _GRAD_PIECES[piece], piece)
        return token

    def grads_crossed(self, piece, after):
        self.reduces[piece], token = self.pair_land(self.reduces[piece], after, piece)
        return token


def kernel(x, mem, e_norm, e_w_in, e_gmlp_w, e_gmlp_b, e_conv_w, e_conv_b, e_conv_ln_g, e_conv_ln_b, e_w_out, o_norm, o_w_in, o_lam_re, o_lam_im, o_log_dt, o_b_re, o_b_im, o_c_re, o_c_im, o_d, o_w_out, ca_norm, ca_mem_norm, ca_wq, ca_wk, ca_wv, ca_wo, ffn_norm, ffn_w_gate, ffn_w_up, ffn_w_down, final_norm, loss_target, m_e_norm, m_e_w_in, m_e_gmlp_w, m_e_gmlp_b, m_e_conv_w, m_e_conv_b, m_e_conv_ln_g, m_e_conv_ln_b, m_e_w_out, m_o_norm, m_o_w_in, m_o_lam_re, m_o_lam_im, m_o_log_dt, m_o_b_re, m_o_b_im, m_o_c_re, m_o_c_im, m_o_d, m_o_w_out, m_ca_norm, m_ca_mem_norm, m_ca_wq, m_ca_wk, m_ca_wv, m_ca_wo, m_ffn_norm, m_ffn_w_gate, m_ffn_w_up, m_ffn_w_down, m_final_norm, v_e_norm, v_e_w_in, v_e_gmlp_w, v_e_gmlp_b, v_e_conv_w, v_e_conv_b, v_e_conv_ln_g, v_e_conv_ln_b, v_e_w_out, v_o_norm, v_o_w_in, v_o_lam_re, v_o_lam_im, v_o_log_dt, v_o_b_re, v_o_b_im, v_o_c_re, v_o_c_im, v_o_d, v_o_w_out, v_ca_norm, v_ca_mem_norm, v_ca_wq, v_ca_wk, v_ca_wv, v_ca_wo, v_ffn_norm, v_ffn_w_gate, v_ffn_w_up, v_ffn_w_down, v_final_norm):
    args = dict(locals())
    local = {n: args[n] for n in _WEIGHTS}
    mom = {n: args["m_" + n] for n in _WEIGHTS}
    vel = {n: args["v_" + n] for n in _WEIGHTS}
    chip = 2 * lax.axis_index("x") + lax.axis_index("y")
    core = lax.axis_index("c")
    xs_, mems_, tgt = x[0], mem[0], loss_target[0]

    w = {n: local[n] for n in _REPLICATED}
    exchange = _Exchange(local, chip, core)
    G = {s: lax.empty((N_CHIPS, rows, width), F32) for s, (width, rows) in _SLABS.items()}
    loss_lanes, dx, G, grads = _forward_backward(xs_, mems_, tgt, w, G, exchange)

    gs_slab, gs_spans = _pack_rows([grads[n] for n in _SMALL] + [loss_lanes], SMALL_W, F32)
    rest0_token = exchange.grads_ready("rest0", G)
    small_flight = _all_to_all_start(gs_slab, rest0_token, name="small_grads_start")
    slabs_l1, halves_l1 = exchange.reduce_sum(exchange.reduces["l1"], small_flight[4], "l1")
    slabs_f0, halves_f0 = exchange.reduce_sum(exchange.reduces["ffn0"], small_flight[4], "ffn0")
    share, share_token = exchange.share_start(slabs_l1 + slabs_f0, halves_l1 + halves_f0, "l1_ffn0")
    token = exchange.grads_crossed("rest0", share_token)

    out_grads, delta, new_m, new_v = {}, {}, {}, {}

    def adamw_large(names):
        for n in names:
            shp = local[n].shape
            g_, d_, m_, v_ = _adamw_shard(_shard_rows(n, local[n]), [(gsum[s], r0) for s, r0 in _PLACE[n][1]],
                                          _shard_rows(n, mom[n]), _shard_rows(n, vel[n]), name=f"adamw_{n}")
            out_grads[n], delta[n], new_m[n], new_v[n] = (_from_shard_rows(n, t, shp) for t in (g_, d_, m_, v_))
        return d_

    gsum = exchange.share_finish(share, token, "l1_ffn0")
    ready = [n for n, (_, where) in _PLACE.items() if all(s in gsum for s, _ in where)]
    done = adamw_large(ready)

    slabs_r0, halves_r0 = exchange.reduce_sum(exchange.reduces["rest0"], done, "rest0")
    share, share_token = exchange.share_start(slabs_r0, halves_r0, "rest0")
    gs_slab, gs_all = _all_to_all_wait(*small_flight[:4], share_token, name="small_grads_wait")
    gs_all = lax.dynamic_update_slice(gs_all, gs_slab[None], (2 * chip + core, 0, 0))
    gs_sum = _sum_slots(gs_all, name="small_grad_sum")
    *small_sums, loss_sum = _unpack_rows(gs_sum, gs_spans, [grads[n].shape for n in _SMALL] + [loss_lanes.shape])
    out_grads.update(zip(_SMALL, small_sums))
    for n, ax in _SMALL_SHARDED:
        width = local[n].shape[ax]
        out_grads[n] = lax.dynamic_slice_in_dim(out_grads[n], chip * width, width, axis=ax)
    d_, m_, v_ = _adamw_small([_two_d(local[n]) for n in _SMALL], [_two_d(out_grads[n]) for n in _SMALL],
                              [_two_d(mom[n]) for n in _SMALL], [_two_d(vel[n]) for n in _SMALL], name="adamw_small")
    for n, dd, mm_, vv in zip(_SMALL, d_, m_, v_):
        shp = local[n].shape
        delta[n], new_m[n], new_v[n] = dd.reshape(shp), mm_.reshape(shp), vv.reshape(shp)

    gsum = {**gsum, **exchange.share_finish(share, d_[0], "rest0")}
    adamw_large([n for n in _PLACE if n not in ready])

    return (loss_sum[0, 0], dx[None], *[out_grads[n] for n in _WEIGHTS], *[delta[n] for n in _WEIGHTS],
            *[new_m[n] for n in _WEIGHTS], *[new_v[n] for n in _WEIGHTS])
```

```python
import functools
import math

import jax
import jax.numpy as jnp
from jax import lax
from jax.experimental import pallas as pl
from jax.experimental.pallas import tpu as pltpu

F32 = jnp.float32
BF16 = jnp.bfloat16
MESH = pl.DeviceIdType.MESH

EPS = 1e-6
D_MODEL = 1024
A_WIDTH = 512
A_GROUPS = 4
GMLP_BLOCK = 128
CHUNK = 64
B_WIDTH = 512
CONV_WIDTH = 31
CONV_HALO = 32
C_WIDTH = 512
C_GROUP_CH = 16
C_GROUPS = 32
C_STATE = 64
N_STATE = C_GROUPS * C_STATE
STATE_LANES = 128
STATE_ROWS = N_STATE // STATE_LANES
SCAN_BLOCK = 8
CA_HEADS = 4
CA_HEAD_DIM = 256
FFN_HIDDEN = 2816

ADAM_LR = 0.001
ADAM_B1 = 0.9
ADAM_B2 = 0.999
ADAM_EPS = 1e-08
ADAM_WD = 0.01
ADAM_STEP = 10

VMEM_LIMIT = 56 * 1024 * 1024
ACC_BYTES = 6 * 1024 * 1024
TN_VMEM_BYTES = 44 * 1024 * 1024
SMALL_W = 128
N_CHIPS = 4
N_DEV = 8

_SLABS = {"D0": (512, 1024), "E0": (1024, 256), "A0": (1024, 1024), "B0": (1024, 704), "C0": (1024, 1408),
          "D1": (512, 768), "A1": (1024, 1024), "B1": (1024, 704), "C1": (1024, 1408)}
_STAGES = (("D0", "E0"), ("A0",), ("B0", "C0"), ("D1", "A1", "B1", "C1"))
_GRAD_PIECES = {"l1": _STAGES[3], "ffn0": _STAGES[2], "rest0": _STAGES[0] + _STAGES[1]}
_PLACE = {
    "e_w_in": (1024, (("D0", 0),)), "e_w_out": (256, (("E0", 0),)),
    "o_w_out": (512, (("D1", 0),)), "o_w_in": (256, (("D1", 512),)),
    "ca_wq": (256, (("A0", 0), ("A1", 0))), "ca_wk": (256, (("A0", 256), ("A1", 256))),
    "ca_wv": (256, (("A0", 512), ("A1", 512))), "ca_wo": (256, (("A0", 768), ("A1", 768))),
    "ffn_w_down": (704, (("B0", 0), ("B1", 0))),
    "ffn_w_gate": (704, (("C0", 0), ("C1", 0))), "ffn_w_up": (704, (("C0", 704), ("C1", 704))),
}
_SMALL_SLAB = "F0"
_SMALL_SLAB_ROWS = 48
_SMALL_PLACE = {"e_conv_w": (0, 31), "o_norm": (32, 2), "o_d": (34, 1)}
_TRANSPOSED = ("ffn_w_gate", "ffn_w_up")


def _params(sem=None):
    return pltpu.CompilerParams(dimension_semantics=sem, vmem_limit_bytes=VMEM_LIMIT)


def _tile(n, pref, mult=128):
    if n <= pref:
        return n
    t = (pref // mult) * mult
    while t >= mult:
        if n % t == 0:
            return t
        t -= mult
    return n


def _blk(name, layer=0):
    rows, where = _PLACE[name]
    slab, r0 = where[layer]
    assert r0 % rows == 0
    return slab, rows, r0 // rows


def _shards(slabs, name, layer=0):
    slab, rows, b = _blk(name, layer)
    return [(slabs[slab], (None, rows, _SLABS[slab][0]), (p, b, 0)) for p in range(N_CHIPS)]


_GELU_C = 0.7978845608028654
_GELU_A = 0.044715


def _gelu(x):
    t = jnp.tanh(_GELU_C * (x + _GELU_A * (x * x * x)))
    return 0.5 * x * (1.0 + t), t


def _gelu_grad(x, t):
    return 0.5 * (1.0 + t) + 0.5 * x * (1.0 - t * t) * (_GELU_C * (1.0 + 3.0 * _GELU_A * x * x))


def _sigmoid(x):
    return 1.0 / (1.0 + jnp.exp(-x))


def _mean(x):
    return jnp.mean(x, axis=-1, keepdims=True)


def _dot(a, b):
    return jnp.dot(a, b, preferred_element_type=F32)


def _dot_nt(a, b):
    return lax.dot_general(a, b, (((1,), (1,)), ((), ())), preferred_element_type=F32)


def _dot_tn(a, b):
    return lax.dot_general(a, b, (((0,), (0,)), ((), ())), preferred_element_type=F32)


def _rms_tile(xv, gv):
    return (xv * lax.rsqrt(_mean(xv * xv) + EPS)) * gv


def _rms_bwd_tile(xv, gv, dyv):
    r = lax.rsqrt(_mean(xv * xv) + EPS)
    xh = xv * r
    dyg = dyv * gv
    return r * (dyg - xh * _mean(dyg * xh)), jnp.sum(dyv * xh, axis=0, keepdims=True)


def _cols(p, width):
    return slice(p * width, (p + 1) * width)


def _sum_k(a, ws, k):
    tot = None
    for p in range(N_CHIPS):
        y = _dot(a[:, _cols(p, k)], ws[p][...])
        tot = y if tot is None else tot + y
    return tot


def _cat_nt(a, ws):
    return jnp.concatenate([_dot_nt(a, ws[p][...]) for p in range(N_CHIPS)], axis=1)


def _rows_call(name, tm, rows, fulls, outs, accs, body, scratch=()):
    S = min(x.shape[-2] for x in rows if x.ndim != 4)
    nr, nf, no, na = len(rows), len(fulls), len(outs), len(accs)

    def kern(*refs):
        r, f = refs[:nr], refs[nr:nr + nf]
        o, a = refs[nr + nf:nr + nf + no], refs[nr + nf + no:nr + nf + no + na]
        if na:
            @pl.when(pl.program_id(0) == 0)
            def _():
                for ref in a:
                    ref[...] = jnp.zeros_like(ref)
        body(r, f, o, a, refs[nr + nf + no + na:])

    def whole(shape):
        nd = len(shape)
        return pl.BlockSpec(tuple(shape), lambda i: (0,) * nd)

    def row_spec(shape):
        if len(shape) == 4:
            return pl.BlockSpec((tm // 8,) + tuple(shape[1:]), lambda i: (i, 0, 0, 0))
        if len(shape) == 3:
            return pl.BlockSpec((shape[0], tm, shape[2]), lambda i: (0, i, 0))
        return pl.BlockSpec((tm, shape[1]), lambda i: (i, 0))

    def full_spec(x):
        if isinstance(x, tuple):
            _, bshape, bidx = x
            return pl.BlockSpec(bshape, lambda i: bidx, pipeline_mode=pl.Buffered(1))
        return whole(x.shape)

    out_shapes = [(S, o[0]) if len(o) == 2 else (o[0], S, o[1]) for o in outs]
    res = pl.pallas_call(
        kern, name=name, grid=(S // tm,),
        in_specs=[row_spec(x.shape) for x in rows] + [full_spec(x) for x in fulls],
        out_specs=[row_spec(s) for s in out_shapes] + [whole(shp) for shp, _ in accs],
        out_shape=[jax.ShapeDtypeStruct(s, o[-1]) for s, o in zip(out_shapes, outs)]
        + [jax.ShapeDtypeStruct(tuple(shp), dt) for shp, dt in accs],
        scratch_shapes=list(scratch),
        compiler_params=_params(("arbitrary",) if na else ("parallel",)),
    )(*rows, *[x[0] if isinstance(x, tuple) else x for x in fulls])
    return res[:no], res[no:]


def _grad_to_slab(gslabs, wname, layer, a, b, *, a_cols=None, b_cols=None, chips=(0, N_CHIPS), name):
    slab, rows, bidx = _blk(wname, layer)
    width = _SLABS[slab][0]
    p0, n_p = chips
    assert p0 % n_p == 0
    S = a.shape[-2]

    def tile_bytes(x, ts):
        return ts * x.dtype.itemsize * (x.shape[2] * n_p if x.ndim == 3 else x.shape[1])

    acc_bytes = n_p * rows * (-(-width // 128) * 128) * 4
    ts = next(t for t in (2048, 1024, 512, 256, S) if S % t == 0
              and 2 * (tile_bytes(a, t) + tile_bytes(b, t) + acc_bytes) <= TN_VMEM_BYTES or t == S)

    def operand(x):
        if x.ndim == 3:
            return pl.BlockSpec((n_p, ts, x.shape[2]), lambda s: (p0 // n_p, s, 0))
        return pl.BlockSpec((ts, x.shape[1]), lambda s: (s, 0))

    def part(ref, cols, p):
        if len(ref.shape) == 3:
            return ref[p]
        return ref[...] if cols is None else ref[:, _cols(p, cols)]

    def body(a_ref, b_ref, slab_ref, o_ref):
        @pl.when(pl.program_id(0) == 0)
        def _():
            o_ref[...] = jnp.zeros_like(o_ref)

        for p in range(n_p):
            o_ref[p] += _dot_tn(part(a_ref, a_cols, p).astype(BF16), part(b_ref, b_cols, p).astype(BF16))

    g = gslabs[slab]
    out = pl.pallas_call(
        body, name=name, grid=(S // ts,),
        in_specs=[operand(a), operand(b), pl.BlockSpec(memory_space=pl.ANY)],
        out_specs=pl.BlockSpec((n_p, rows, width), lambda s: (p0 // n_p, bidx, 0)),
        out_shape=jax.ShapeDtypeStruct(g.shape, F32), input_output_aliases={2: 0},
        compiler_params=_params(("arbitrary",)),
    )(a, b, g)
    return {**gslabs, slab: out}


def _vec(g):
    return g.reshape(1, -1)


def _norm_mm(x, g, ws, *, split, out_dtype, name, tm=512):
    S, D = x.shape
    k, n = ws[0][1][1], ws[0][1][2]
    N = n if split == "k" else N_CHIPS * n

    def body(r, f, o, acc, s):
        xn = _rms_tile(r[0][...], f[0][...]).astype(BF16)
        o[0][...] = xn
        if split == "k":
            o[1][...] = _sum_k(xn, f[1:], k).astype(out_dtype)
        else:
            for p in range(N_CHIPS):
                o[1][:, _cols(p, n)] = _dot(xn, f[1 + p][...]).astype(out_dtype)

    (xn, y), _ = _rows_call(name, _tile(S, tm), [x], [_vec(g)] + ws, [(D, BF16), (N, out_dtype)], [], body)
    return xn, y


def _mm_k(a, ws, *, add=None, out_dtype=F32, name, tm=512):
    S = a.shape[-2]
    k, n = ws[0][1][1], ws[0][1][2]
    has_add = add is not None

    def body(r, f, o, acc, s):
        if a.ndim == 3:
            y = None
            for p in range(N_CHIPS):
                t = _dot(r[0][p].astype(BF16), f[p][...])
                y = t if y is None else y + t
        else:
            y = _sum_k(r[0][...].astype(BF16), f, k)
        if has_add:
            y = y + r[1][...]
        o[0][...] = y.astype(out_dtype)

    (y,), _ = _rows_call(name, _tile(S, tm), [a] + ([add] if has_add else []), ws, [(n, out_dtype)], [], body)
    return y


def _mm_k_t(terms, *, out_dtype=F32, name, tm=512):
    S = terms[0][0].shape[0]
    k = terms[0][1][0][1][1]

    def body(r, f, o, acc, s):
        y = None
        for t in range(len(terms)):
            yt = _cat_nt(r[t][...].astype(BF16), f[N_CHIPS * t:N_CHIPS * (t + 1)])
            y = yt if y is None else y + yt
        o[0][...] = y.astype(out_dtype)

    (y,), _ = _rows_call(name, _tile(S, tm), [a for a, _ in terms], [w for _, ws in terms for w in ws],
                         [(N_CHIPS * k, out_dtype)], [], body)
    return y


def _rms_fwd(x, g, *, name):
    def body(r, f, o, acc, s):
        o[0][...] = _rms_tile(r[0][...], f[0][...]).astype(BF16)

    (y,), _ = _rows_call(name, _tile(x.shape[0], 256, 8), [x], [_vec(g)], [(x.shape[1], BF16)], [], body)
    return y


def _rms_dg(x, g, dy, *, name):
    def body(r, f, o, acc, s):
        acc[0][...] += _rms_bwd_tile(r[0][...], f[0][...], r[1][...])[1]

    _, (dg,) = _rows_call(name, _tile(x.shape[0], 256, 8), [x, dy], [_vec(g)], [], [((1, x.shape[1]), F32)], body)
    return dg


def _ffn_up(x, g, wg, wu, *, name, tm=512):
    S, D = x.shape
    h = wg[0][1][1]

    def body(r, f, o, acc, s):
        xn = _rms_tile(r[0][...], f[0][...]).astype(BF16)
        o[0][...] = xn
        for p in range(N_CHIPS):
            gate = _dot_nt(xn, f[1 + p][...])
            up = _dot_nt(xn, f[1 + N_CHIPS + p][...])
            o[1][p] = gate.astype(BF16)
            o[2][p] = up.astype(BF16)
            o[3][p] = (gate * _sigmoid(gate) * up).astype(BF16)

    (xn, gate, up, hid), _ = _rows_call(name, _tile(S, tm), [x], [_vec(g)] + wg + wu,
                                        [(D, BF16), (N_CHIPS, h, BF16), (N_CHIPS, h, BF16), (N_CHIPS, h, BF16)], [],
                                        body)
    return xn, gate, up, hid


def _ffn_bwd_hidden(dy, wd, gate, up, token=None, *, name, tm=512):
    S = dy.shape[0]
    h = wd[0][1][1]

    def body(r, f, o, acc, s):
        dyv = r[0][...]
        if token is not None:
            dyv = dyv + jnp.sum(f[N_CHIPS][...])
        dyb = dyv.astype(BF16)
        for p in range(N_CHIPS):
            dh = _dot_nt(dyb, f[p][...])
            gv = r[1][p].astype(F32)
            sg = _sigmoid(gv)
            o[0][p] = (dh * r[2][p].astype(F32) * (sg * (1.0 + gv * (1.0 - sg)))).astype(BF16)
            o[1][p] = (dh * gv * sg).astype(BF16)

    (dg, du), _ = _rows_call(name, _tile(S, tm), [dy, gate, up], wd + ([] if token is None else [token]),
                             [(N_CHIPS, h, BF16), (N_CHIPS, h, BF16)], [], body)
    return dg, du


def _ffn_in_bwd(dg, du, wg, wu, x, g, dres, *, name, tm=512):
    S, D = x.shape

    def body(r, f, o, acc, s):
        tot = None
        for p in range(N_CHIPS):
            y = _dot(r[0][p], f[1 + p][...]) + _dot(r[1][p], f[1 + N_CHIPS + p][...])
            tot = y if tot is None else tot + y
        dx, dgn = _rms_bwd_tile(r[2][...], f[0][...], tot)
        o[0][...] = dx + r[3][...]
        acc[0][...] += dgn

    (dx,), (dgn,) = _rows_call(name, _tile(S, tm), [dg, du, x, dres], [_vec(g)] + wg + wu, [(D, F32)],
                               [((1, D), F32)], body)
    return dx, dgn


def _norm_bwd_k(da, ws, x, g, dres, *, name, tm=512):
    S, D = x.shape

    def body(r, f, o, acc, s):
        dx, dg = _rms_bwd_tile(r[1][...], f[0][...], _cat_nt(r[0][...].astype(BF16), f[1:]))
        o[0][...] = dx + r[2][...]
        acc[0][...] += dg

    (dx,), (dg,) = _rows_call(name, _tile(S, tm), [da, x, dres], [_vec(g)] + ws, [(D, F32)], [((1, D), F32)], body)
    return dx, dg


def _norm_bwd_n(das, ws, x, g, dres, *, name, tm=256):
    S, D = x.shape
    n = ws[0][1][2]

    def body(r, f, o, acc, s):
        tot = None
        for p in range(N_CHIPS):
            y = _dot_nt(r[p // 2][:, _cols(p % 2, n)], f[1 + p][...])
            tot = y if tot is None else tot + y
        dx, dg = _rms_bwd_tile(r[2][...], f[0][...], tot)
        o[0][...] = dx + r[3][...]
        acc[0][...] += dg

    (dx,), (dg,) = _rows_call(name, _tile(S, tm), list(das) + [x, dres], [_vec(g)] + ws, [(D, F32)], [((1, D), F32)],
                              body)
    return dx, dg


def _ln_stats(v):
    mu = _mean(v)
    xc = v - mu
    rstd = lax.rsqrt(_mean(xc * xc) + EPS)
    return xc * rstd, rstd


_SHIFTS = 8
_CONV_ROWS = 64


def _fill_shifts(sh_ref, ext_ref, tm):
    sh_ref[0] = ext_ref[...]
    for s in range(1, _SHIFTS):
        sh_ref[s, 0:tm + CONV_HALO - _SHIFTS, :] = ext_ref[pl.ds(s, tm + CONV_HALO - _SHIFTS), :]


def _window(sh_ref, off, tm):
    return sh_ref[off % _SHIFTS, pl.ds(off - off % _SHIFTS, tm), :]


def _even_fwd(proj, wm, bcol, cw, cb, lg, lb, *, name):
    S = proj.shape[0]
    tm = _tile(S, 256)
    hb = tm // CONV_HALO
    nblk = tm // GMLP_BLOCK

    def body(p_ref, halo_ref, wm_ref, b_ref, cw_ref, cb_ref, lg_ref, lb_ref, mix_ref, hc_ref, hext_ref, hsh_ref):
        i = pl.program_id(0)
        gu, _ = _gelu(p_ref[:, 0:A_WIDTH])
        gv, _ = _gelu(p_ref[:, A_WIDTH:2 * A_WIDTH])
        vn, _ = _ln_stats(gv)
        vnb = vn.astype(BF16)
        for n in range(nblk):
            rows = slice(n * GMLP_BLOCK, (n + 1) * GMLP_BLOCK)
            for g in range(A_GROUPS):
                cols = slice(g * GMLP_BLOCK, (g + 1) * GMLP_BLOCK)
                sg = jnp.dot(wm_ref[g], vnb[rows, cols], preferred_element_type=F32) + b_ref[g]
                mix_ref[rows, cols] = (gu[rows, cols] * sg).astype(BF16)
        h = p_ref[:, 1024:1536] * _sigmoid(p_ref[:, 1536:2048])
        hh = halo_ref[:, 0:B_WIDTH] * _sigmoid(halo_ref[:, B_WIDTH:2 * B_WIDTH])
        hext_ref[0:CONV_HALO, :] = jnp.where(i > 0, hh, 0.0)
        hext_ref[CONV_HALO:CONV_HALO + tm, :] = h
        _fill_shifts(hsh_ref, hext_ref, tm)
        for r0 in range(0, tm, _CONV_ROWS):
            acc = jnp.zeros((_CONV_ROWS, B_WIDTH), F32)
            for k in range(CONV_WIDTH):
                acc = acc + cw_ref[k:k + 1, :] * _window(hsh_ref, r0 + k + CONV_HALO - CONV_WIDTH + 1, _CONV_ROWS)
            hc_ref[r0:r0 + _CONV_ROWS, :] = acc + cb_ref[...]
        hc = hc_ref[...]
        hhat, _ = _ln_stats(hc)
        hl = hhat * lg_ref[...] + lb_ref[...]
        mix_ref[:, A_WIDTH:A_WIDTH + B_WIDTH] = (hl * _sigmoid(hl)).astype(BF16)

    vec = pl.BlockSpec((1, B_WIDTH), lambda i: (0, 0))
    return pl.pallas_call(
        body, name=name, grid=(S // tm,),
        in_specs=[
            pl.BlockSpec((tm, 2048), lambda i: (i, 0)),
            pl.BlockSpec((CONV_HALO, 1024), lambda i: (jnp.maximum(i * hb - 1, 0), 1)),
            pl.BlockSpec((A_GROUPS, GMLP_BLOCK, GMLP_BLOCK), lambda i: (0, 0, 0)),
            pl.BlockSpec((A_GROUPS, GMLP_BLOCK, 1), lambda i: (0, 0, 0)),
            pl.BlockSpec((CONV_HALO, B_WIDTH), lambda i: (0, 0)),
            vec, vec, vec,
        ],
        out_specs=[pl.BlockSpec((tm, 1024), lambda i: (i, 0)), pl.BlockSpec((tm, B_WIDTH), lambda i: (i, 0))],
        out_shape=[jax.ShapeDtypeStruct((S, 1024), BF16), jax.ShapeDtypeStruct((S, B_WIDTH), F32)],
        scratch_shapes=[pltpu.VMEM((tm + CONV_HALO, B_WIDTH), F32),
                        pltpu.VMEM((_SHIFTS, tm + CONV_HALO, B_WIDTH), F32)],
        compiler_params=_params(("parallel",)),
    )(proj, proj, wm, bcol, cw, cb, lg, lb)


def _even_bwd1(proj, dmix, hc, wm, wmt, bcol, lg, lb, *, name):
    S = proj.shape[0]
    tm = _tile(S, 256)
    nblk = tm // GMLP_BLOCK

    def body(p_ref, dm_ref, hc_ref, wm_ref, wmt_ref, b_ref, lg_ref, lb_ref,
             dpa_ref, dhc_ref, dwm_ref, db_ref, dlg_ref, dlb_ref, dcb_ref, dgu_ref, dvn_ref):
        @pl.when(pl.program_id(0) == 0)
        def _():
            dwm_ref[...] = jnp.zeros_like(dwm_ref)
            db_ref[...] = jnp.zeros_like(db_ref)
            dlg_ref[...] = jnp.zeros_like(dlg_ref)
            dlb_ref[...] = jnp.zeros_like(dlb_ref)
            dcb_ref[...] = jnp.zeros_like(dcb_ref)

        au = p_ref[:, 0:A_WIDTH]
        av = p_ref[:, A_WIDTH:2 * A_WIDTH]
        gu, tu = _gelu(au)
        gv, tv = _gelu(av)
        vn, rstd = _ln_stats(gv)
        vnb = vn.astype(BF16)
        for n in range(nblk):
            rows = slice(n * GMLP_BLOCK, (n + 1) * GMLP_BLOCK)
            for g in range(A_GROUPS):
                cols = slice(g * GMLP_BLOCK, (g + 1) * GMLP_BLOCK)
                vb = vnb[rows, cols]
                sg = jnp.dot(wm_ref[g], vb, preferred_element_type=F32) + b_ref[g]
                da = dm_ref[rows, cols]
                dsg = da * gu[rows, cols]
                dgu_ref[rows, cols] = da * sg
                dsgb = dsg.astype(BF16)
                dwm_ref[g] += _dot_nt(dsgb, vb)
                db_ref[g] += jnp.sum(dsg, axis=1, keepdims=True)
                dvn_ref[rows, cols] = jnp.dot(wmt_ref[g], dsgb, preferred_element_type=F32)
        dvn = dvn_ref[...]
        dgv = rstd * (dvn - _mean(dvn) - vn * _mean(dvn * vn))
        dpa_ref[:, 0:A_WIDTH] = (dgu_ref[...] * _gelu_grad(au, tu)).astype(BF16)
        dpa_ref[:, A_WIDTH:2 * A_WIDTH] = (dgv * _gelu_grad(av, tv)).astype(BF16)
        hhat, rstd2 = _ln_stats(hc_ref[...])
        lgv = lg_ref[...]
        hl = hhat * lgv + lb_ref[...]
        s = _sigmoid(hl)
        dhl = dm_ref[:, A_WIDTH:A_WIDTH + B_WIDTH] * (s * (1.0 + hl * (1.0 - s)))
        dlg_ref[...] += jnp.sum(dhl * hhat, axis=0, keepdims=True)
        dlb_ref[...] += jnp.sum(dhl, axis=0, keepdims=True)
        dhh = dhl * lgv
        dhc = rstd2 * (dhh - _mean(dhh) - hhat * _mean(dhh * hhat))
        dcb_ref[...] += jnp.sum(dhc, axis=0, keepdims=True)
        dhc_ref[...] = dhc

    vec = pl.BlockSpec((1, B_WIDTH), lambda i: (0, 0))
    w3 = pl.BlockSpec((A_GROUPS, GMLP_BLOCK, GMLP_BLOCK), lambda i: (0, 0, 0))
    b3 = pl.BlockSpec((A_GROUPS, GMLP_BLOCK, 1), lambda i: (0, 0, 0))
    return pl.pallas_call(
        body, name=name, grid=(S // tm,),
        in_specs=[
            pl.BlockSpec((tm, 1024), lambda i: (i, 0)),
            pl.BlockSpec((tm, 1024), lambda i: (i, 0)),
            pl.BlockSpec((tm, B_WIDTH), lambda i: (i, 0)),
            w3, w3, b3, vec, vec,
        ],
        out_specs=[pl.BlockSpec((tm, 1024), lambda i: (i, 0)), pl.BlockSpec((tm, B_WIDTH), lambda i: (i, 0)),
                   w3, b3, vec, vec, vec],
        out_shape=[
            jax.ShapeDtypeStruct((S, 1024), BF16), jax.ShapeDtypeStruct((S, B_WIDTH), F32),
            jax.ShapeDtypeStruct((A_GROUPS, GMLP_BLOCK, GMLP_BLOCK), F32),
            jax.ShapeDtypeStruct((A_GROUPS, GMLP_BLOCK, 1), F32),
            jax.ShapeDtypeStruct((1, B_WIDTH), F32), jax.ShapeDtypeStruct((1, B_WIDTH), F32),
            jax.ShapeDtypeStruct((1, B_WIDTH), F32),
        ],
        scratch_shapes=[pltpu.VMEM((tm, A_WIDTH), F32), pltpu.VMEM((tm, A_WIDTH), F32)],
        compiler_params=_params(("arbitrary",)),
    )(proj, dmix, hc, wm, wmt, bcol, lg, lb)


def _even_bwd2(proj, dhc, cw, *, name):
    S = proj.shape[0]
    tm = _tile(S, 256)
    hb = tm // CONV_HALO
    nt = S // tm
    last_halo = S // CONV_HALO - 1
    lo = CONV_HALO - CONV_WIDTH + 1

    def body(p_ref, halo_ref, d_ref, dnext_ref, cw_ref, dpb_ref, dcw_ref, hext_ref, dext_ref, hsh_ref, dsh_ref):
        i = pl.program_id(0)

        @pl.when(i == 0)
        def _():
            dcw_ref[...] = jnp.zeros_like(dcw_ref)

        hh = halo_ref[:, 0:B_WIDTH] * _sigmoid(halo_ref[:, B_WIDTH:2 * B_WIDTH])
        hext_ref[0:CONV_HALO, :] = jnp.where(i > 0, hh, 0.0)
        hext_ref[CONV_HALO:CONV_HALO + tm, :] = p_ref[:, 0:B_WIDTH] * _sigmoid(p_ref[:, B_WIDTH:2 * B_WIDTH])
        dext_ref[0:tm, :] = d_ref[...]
        dext_ref[tm:tm + CONV_HALO, :] = jnp.where(i < nt - 1, dnext_ref[...], 0.0)
        _fill_shifts(hsh_ref, hext_ref, tm)
        _fill_shifts(dsh_ref, dext_ref, tm)
        for r0 in range(0, tm, _CONV_ROWS):
            rows = slice(r0, r0 + _CONV_ROWS)
            dhc_b = d_ref[rows, :]
            dh = jnp.zeros((_CONV_ROWS, B_WIDTH), F32)
            for k in range(CONV_WIDTH):
                dh = dh + cw_ref[k:k + 1, :] * _window(dsh_ref, r0 + CONV_WIDTH - 1 - k, _CONV_ROWS)
                dcw_ref[k:k + 1, :] += jnp.sum(dhc_b * _window(hsh_ref, r0 + k + lo, _CONV_ROWS), axis=0,
                                               keepdims=True)
            ba_b = p_ref[rows, 0:B_WIDTH]
            sg_b = _sigmoid(p_ref[rows, B_WIDTH:2 * B_WIDTH])
            dpb_ref[rows, 0:B_WIDTH] = (dh * sg_b).astype(BF16)
            dpb_ref[rows, B_WIDTH:2 * B_WIDTH] = (dh * ba_b * sg_b * (1.0 - sg_b)).astype(BF16)

    return pl.pallas_call(
        body, name=name, grid=(nt,),
        in_specs=[
            pl.BlockSpec((tm, 1024), lambda i: (i, 1)),
            pl.BlockSpec((CONV_HALO, 1024), lambda i: (jnp.maximum(i * hb - 1, 0), 1)),
            pl.BlockSpec((tm, B_WIDTH), lambda i: (i, 0)),
            pl.BlockSpec((CONV_HALO, B_WIDTH), lambda i: (jnp.minimum((i + 1) * hb, last_halo), 0)),
            pl.BlockSpec((CONV_HALO, B_WIDTH), lambda i: (0, 0)),
        ],
        out_specs=[pl.BlockSpec((tm, 1024), lambda i: (i, 0)), pl.BlockSpec((CONV_HALO, B_WIDTH), lambda i: (0, 0))],
        out_shape=[jax.ShapeDtypeStruct((S, 1024), BF16), jax.ShapeDtypeStruct((CONV_HALO, B_WIDTH), F32)],
        scratch_shapes=[pltpu.VMEM((tm + CONV_HALO, B_WIDTH), F32), pltpu.VMEM((tm + CONV_HALO, B_WIDTH), F32),
                        pltpu.VMEM((_SHIFTS, tm + CONV_HALO, B_WIDTH), F32),
                        pltpu.VMEM((_SHIFTS, tm + CONV_HALO, B_WIDTH), F32)],
        compiler_params=_params(("arbitrary",)),
    )(proj, proj, dhc, dhc, cw)


_CA_SCALE = CA_HEAD_DIM ** -0.5


def _softmax_rows(s):
    e = jnp.exp(s - jnp.max(s, axis=-1, keepdims=True))
    return e / jnp.sum(e, axis=-1, keepdims=True)


def _attn_fwd(q, k, v, *, name):
    S = q.shape[0]

    def body(r, f, o, acc, s):
        for h in range(CA_HEADS):
            cols = _cols(h, CA_HEAD_DIM)
            p = _softmax_rows(_dot_nt(r[0][:, cols], f[0][:, cols]) * _CA_SCALE)
            o[0][:, cols] = _dot(p.astype(BF16), f[1][:, cols]).astype(BF16)

    (o_,), _ = _rows_call(name, _tile(S, 512), [q], [k, v], [(D_MODEL, BF16)], [], body)
    return o_


def _attn_bwd(dy, wo, q, k, v, *, name):
    S = q.shape[0]
    M = k.shape[0]

    def body(r, f, o, acc, s):
        dyb = r[0][...].astype(BF16)
        for h in range(CA_HEADS):
            cols = _cols(h, CA_HEAD_DIM)
            qh = r[1][:, cols]
            kh = f[0][:, cols]
            vh = f[1][:, cols]
            doh = _dot_nt(dyb, f[2 + h][...]).astype(BF16)
            p = _softmax_rows(_dot_nt(qh, kh) * _CA_SCALE)
            acc[1][:, cols] += _dot_tn(p.astype(BF16), doh)
            dp = _dot_nt(doh, vh)
            ds = (p * (dp - jnp.sum(dp * p, axis=-1, keepdims=True)) * _CA_SCALE).astype(BF16)
            o[0][:, cols] = _dot(ds, kh).astype(BF16)
            acc[0][:, cols] += _dot_tn(ds, qh)

    (dq,), (dk, dv) = _rows_call(name, _tile(S, 512), [dy, q], [k, v] + wo, [(D_MODEL, BF16)],
                                 [((M, D_MODEL), F32), ((M, D_MODEL), F32)], body)
    return dq, dk, dv


_STATE_TILE = 2 * STATE_ROWS
N_SETS = 4
SET_CH = C_WIDTH // N_SETS
SET_COLS = N_STATE // N_SETS // STATE_LANES


def _set_groups(j):
    return [SET_COLS * j + c for c in range(SET_COLS)] + [STATE_ROWS + SET_COLS * j + c for c in range(SET_COLS)]


def _pack_state(re, im):
    hi = lax.bitcast_convert_type(re.astype(BF16).astype(F32), jnp.uint32)
    lo = lax.bitcast_convert_type(im.astype(BF16).astype(F32), jnp.uint32) >> 16
    return hi | lo


def _unpack_state(word):
    re = lax.bitcast_convert_type(word & jnp.uint32(0xFFFF0000), F32)
    im = lax.bitcast_convert_type(word << 16, F32)
    return re, im


def _state_set(ref, tm, j):
    parts = [_unpack_state(ref[:, SET_COLS * j + c, :, :].reshape(tm, STATE_LANES)) for c in range(SET_COLS)]
    return jnp.concatenate([p[0].astype(BF16) for p in parts] + [p[1].astype(BF16) for p in parts], axis=1)


def _s5_readout(xs, cset, u, d, *, name, tm=256):
    tm = _tile(u.shape[0], tm)

    def body(r, f, o, acc, s):
        y0 = jnp.concatenate([_dot(_state_set(r[0], tm, j), f[0][j]) for j in range(N_SETS)], axis=1)
        y = y0 + f[1][...] * r[1][...]
        o[0][...] = y
        o[1][...] = _gelu(y)[0].astype(BF16)

    (y, yg), _ = _rows_call(name, tm, [xs, u], [cset, d], [(C_WIDTH, F32), (C_WIDTH, BF16)], [], body)
    return y, yg


def _state_grad_sets(a, st, *, name, ts=256):
    ts = _tile(a.shape[0], ts)

    def body(r, f, o, acc, s):
        for j in range(N_SETS):
            acc[0][j] += _dot_tn(r[0][:, _cols(j, SET_CH)].astype(BF16), _state_set(r[1], ts, j))

    _, (out,) = _rows_call(name, ts, [a, st], [], [], [((N_SETS, SET_CH, 2 * N_STATE // N_SETS), F32)], body)
    return out


def _glu_out(yg, ws, x, *, name, tm=512):
    n = ws[0][1][2]

    def body(r, f, o, acc, s):
        ygv = r[0][...]
        ov = [_dot(ygv, f[p][...]) for p in range(N_CHIPS)]
        for p in range(N_CHIPS):
            o[0][:, _cols(p, n)] = ov[p].astype(BF16)
        for p in range(2):
            o[1][:, _cols(p, n)] = r[1][:, _cols(p, n)] + ov[p] * _sigmoid(ov[2 + p])

    (o_, y), _ = _rows_call(name, _tile(x.shape[0], tm), [yg, x], ws, [(2 * D_MODEL, BF16), (D_MODEL, F32)], [], body)
    return o_, y


def _glu_out_bwd(o_, dy, ws, y, u, d, *, name, tm=256):
    n = ws[0][1][2]

    def body(r, f, o, acc, s):
        o1 = r[0][:, 0:D_MODEL].astype(F32)
        sg = _sigmoid(r[0][:, D_MODEL:2 * D_MODEL].astype(F32))
        dyv = r[1][...]
        do1 = (dyv * sg).astype(BF16)
        do2 = (dyv * o1 * sg * (1.0 - sg)).astype(BF16)
        o[0][:, 0:D_MODEL] = do1
        o[0][:, D_MODEL:2 * D_MODEL] = do2
        dyg = None
        for p in range(N_CHIPS):
            t = _dot_nt((do1 if p < 2 else do2)[:, _cols(p % 2, n)], f[1 + p][...])
            dyg = t if dyg is None else dyg + t
        yv = r[2][...]
        dys = dyg * _gelu_grad(yv, _gelu(yv)[1])
        o[1][...] = dys.astype(BF16)
        o[2][...] = f[0][...] * dys
        acc[0][...] += jnp.sum(dys * r[3][...], axis=0, keepdims=True)

    (do, dys, dus), (dd,) = _rows_call(name, _tile(dy.shape[0], tm), [o_, dy, y, u], [d] + ws,
                                       [(2 * D_MODEL, BF16), (C_WIDTH, BF16), (C_WIDTH, F32)], [((1, C_WIDTH), F32)],
                                       body)
    return do, dys, dus, dd


def _s5_in_bwd(gs, bset, dus, ws, x, g, dres, *, name, tm=256):
    D = x.shape[1]
    tm = _tile(x.shape[0], tm)

    def body(r, f, o, acc, s):
        du0 = jnp.concatenate([_dot_nt(_state_set(r[0], tm, j), f[1][j]) for j in range(N_SETS)], axis=1)
        du = (du0 + r[1][...]).astype(BF16)
        o[0][...] = du
        dx, dg = _rms_bwd_tile(r[2][...], f[0][...], _cat_nt(du, f[2:]))
        o[1][...] = dx + r[3][...]
        acc[0][...] += dg

    (du, dx), (dg,) = _rows_call(name, tm, [gs, dus, x, dres], [_vec(g), bset] + ws,
                                 [(C_WIDTH, BF16), (D, F32)], [((1, D), F32)], body)
    return du, dx, dg


_SCAN_CHUNK = 256
_RE = slice(0, STATE_ROWS)
_IM = slice(STATE_ROWS, 2 * STATE_ROWS)
assert SCAN_BLOCK == 8


def _token(g, i, rows):
    return pl.ds(pl.multiple_of(g * (rows * SCAN_BLOCK), rows * SCAN_BLOCK) + i, rows, stride=SCAN_BLOCK)


def _fill_chunk(s3, a_ref, wset, tc, nt):
    for j in range(N_SETS):
        av = a_ref[:, _cols(j, SET_CH)].astype(BF16)
        y = _dot_nt(av, wset[j]) if nt else _dot(av, wset[j])
        for k, c in enumerate(_set_groups(j)):
            s3[:, 8 * c:8 * (c + 1), :] = y[:, _cols(k, STATE_LANES)].reshape(tc // 8, 8, STATE_LANES)


def _chunk_token(s3, g, i):
    return s3[g, pl.ds(i, _STATE_TILE, stride=SCAN_BLOCK), :]


def _scan_fwd(u, bset, pw, *, name):
    S = u.shape[0]
    tc = _tile(S, _SCAN_CHUNK, 8)

    def body(u_ref, bset_ref, pw_ref, xs_ref, st_ref, s3):
        @pl.when(pl.program_id(0) == 0)
        def _():
            st_ref[...] = jnp.zeros_like(st_ref)

        _fill_chunk(s3, u_ref, bset_ref, tc, nt=False)
        ar = pw_ref[0, _RE, :]
        ai = pw_ref[0, _IM, :]

        def block(g, carry):
            xr, xi = carry
            cr = ci = nr = ni = None
            for j in range(SCAN_BLOCK):
                b = _chunk_token(s3, g, j)
                br, bi = b[_RE], b[_IM]
                cr, ci = (br, bi) if j == 0 else (ar * cr - ai * ci + br, ar * ci + ai * cr + bi)
                pr, pi = pw_ref[j, _RE, :], pw_ref[j, _IM, :]
                nr = pr * xr - pi * xi + cr
                ni = pr * xi + pi * xr + ci
                xs_ref[_token(g, j, STATE_ROWS), :] = _pack_state(nr, ni)
            return nr, ni

        xr, xi = lax.fori_loop(0, tc // SCAN_BLOCK, block, (st_ref[_RE, :], st_ref[_IM, :]), unroll=4)
        st_ref[_RE, :] = xr
        st_ref[_IM, :] = xi

    return pl.pallas_call(
        body, name=name, grid=(S // tc,),
        in_specs=[pl.BlockSpec((tc, u.shape[1]), lambda i: (i, 0)), pl.BlockSpec(bset.shape, lambda i: (0, 0, 0)),
                  pl.BlockSpec(pw.shape, lambda i: (0, 0, 0))],
        out_specs=pl.BlockSpec((tc * STATE_ROWS, STATE_LANES), lambda i: (i, 0)),
        out_shape=jax.ShapeDtypeStruct((S * STATE_ROWS, STATE_LANES), jnp.uint32),
        scratch_shapes=[pltpu.VMEM((2 * STATE_ROWS, STATE_LANES), F32),
                        pltpu.VMEM((tc // 8, _STATE_TILE * 8, STATE_LANES), F32)],
        compiler_params=_params(("arbitrary",)),
    )(u, bset, pw)


def _scan_bwd(dys, cset, xs, pw, *, name):
    S = dys.shape[0]
    tc = _tile(S, _SCAN_CHUNK, 8)
    nc = S // tc

    def body(dys_ref, cset_ref, xs_ref, pw_ref, g_ref, da_ref, st_ref, s3):
        @pl.when(pl.program_id(0) == 0)
        def _():
            st_ref[...] = jnp.zeros_like(st_ref)
            da_ref[...] = jnp.zeros_like(da_ref)

        _fill_chunk(s3, dys_ref, cset_ref, tc, nt=True)
        ar = pw_ref[0, _RE, :]
        ai = pw_ref[0, _IM, :]

        def block(k, carry):
            gr, gi, dar, dai = carry
            g = tc // SCAN_BLOCK - 1 - k
            cr = ci = None
            pgr, pgi = gr, gi
            for j in range(SCAN_BLOCK):
                i = SCAN_BLOCK - 1 - j
                xr, xi = _unpack_state(xs_ref[_token(g, i, STATE_ROWS), :])
                dar = dar + pgr * xr + pgi * xi
                dai = dai + pgi * xr - pgr * xi
                d = _chunk_token(s3, g, i)
                dr, di = d[_RE], d[_IM]
                cr, ci = (dr, di) if j == 0 else (ar * cr + ai * ci + dr, ar * ci - ai * cr + di)
                pr, pi = pw_ref[j, _RE, :], pw_ref[j, _IM, :]
                pgr = pr * gr + pi * gi + cr
                pgi = pr * gi - pi * gr + ci
                g_ref[_token(g, i, STATE_ROWS), :] = _pack_state(pgr, pgi)
            return pgr, pgi, dar, dai

        init = (st_ref[_RE, :], st_ref[_IM, :], da_ref[_RE, :], da_ref[_IM, :])
        gr, gi, dar, dai = lax.fori_loop(0, tc // SCAN_BLOCK, block, init, unroll=4)
        st_ref[_RE, :] = gr
        st_ref[_IM, :] = gi
        da_ref[_RE, :] = dar
        da_ref[_IM, :] = dai

    packed = pl.BlockSpec((tc * STATE_ROWS, STATE_LANES), lambda i: (nc - 1 - i, 0))
    vec = pl.BlockSpec((2 * STATE_ROWS, STATE_LANES), lambda i: (0, 0))
    return pl.pallas_call(
        body, name=name, grid=(nc,),
        in_specs=[pl.BlockSpec((tc, dys.shape[1]), lambda i: (nc - 1 - i, 0)),
                  pl.BlockSpec(cset.shape, lambda i: (0, 0, 0)), packed, pl.BlockSpec(pw.shape, lambda i: (0, 0, 0))],
        out_specs=[packed, vec],
        out_shape=[jax.ShapeDtypeStruct(xs.shape, jnp.uint32), jax.ShapeDtypeStruct((2 * STATE_ROWS, STATE_LANES), F32)],
        scratch_shapes=[pltpu.VMEM((2 * STATE_ROWS, STATE_LANES), F32),
                        pltpu.VMEM((tc // 8, _STATE_TILE * 8, STATE_LANES), F32)],
        compiler_params=_params(("arbitrary",)),
    )(dys, cset, xs, pw)


def _down_loss_head(h, ws, x, g, target, *, name, tm=512):
    S, D = x.shape

    def body(r, f, o, acc, s):
        xv = r[1][...]
        for p in range(N_CHIPS):
            xv = xv + _dot(r[0][p], f[1 + p][...])
        gv = f[0][...]
        rs = lax.rsqrt(_mean(xv * xv) + EPS)
        xh = xv * rs
        err = xh * gv - r[2][...]
        acc[1][...] += 0.5 * jnp.sum(_mean(err * err), axis=0, keepdims=True)
        dy = err * (1.0 / D)
        dyg = dy * gv
        o[0][...] = rs * (dyg - xh * _mean(dyg * xh))
        acc[0][...] += jnp.sum(dy * xh, axis=0, keepdims=True)

    (dx,), (dg, loss) = _rows_call(name, _tile(S, tm), [h, x, target], [_vec(g)] + ws, [(D, F32)],
                                   [((1, D), F32), ((1, 128), F32)], body)
    return dx, dg, loss


_ADAM_C1 = 1.0 - ADAM_B1 ** ADAM_STEP
_ADAM_C2 = 1.0 - ADAM_B2 ** ADAM_STEP
_ONE_BLOCK_BYTES = 8 * 1024 * 1024


def _adamw_math(w, g, m, v):
    nm = ADAM_B1 * m + (1.0 - ADAM_B1) * g
    nv = ADAM_B2 * v + (1.0 - ADAM_B2) * (g * g)
    m_hat = nm / _ADAM_C1
    v_hat = nv / _ADAM_C2
    return -ADAM_LR * (m_hat / (jnp.sqrt(v_hat) + ADAM_EPS) + ADAM_WD * w), nm, nv


def _adamw_shard(w, gsrc, m, v, *, name):
    R, C = w.shape
    n_l = len(gsrc)
    rows = R // n_l
    tr = rows
    for _, r0 in gsrc:
        tr = math.gcd(tr, r0) if r0 else tr
    tr = _tile(tr, 256, 8) if tr > 256 else tr
    nb = rows // tr
    assert rows % tr == 0 and all(r0 % tr == 0 for _, r0 in gsrc)

    def body(*refs):
        w_ref, g_refs, (m_ref, v_ref, go_ref, d_ref, nm_ref, nv_ref) = refs[0], refs[1:1 + n_l], refs[1 + n_l:]
        layer = pl.program_id(0) // nb
        gv = g_refs[0][...]
        for l in range(1, n_l):
            gv = jnp.where(layer == l, g_refs[l][...], gv)
        go_ref[...] = gv
        d_ref[...], nm_ref[...], nv_ref[...] = _adamw_math(w_ref[...], gv, m_ref[...], v_ref[...])

    def g_spec(l, r0):
        return pl.BlockSpec((tr, C), lambda i: (r0 // tr + jnp.clip(i - l * nb, 0, nb - 1), 0))

    blk = pl.BlockSpec((tr, C), lambda i: (i, 0))
    out = jax.ShapeDtypeStruct((R, C), F32)
    return pl.pallas_call(
        body, name=name, grid=(R // tr,),
        in_specs=[blk] + [g_spec(l, r0) for l, (_, r0) in enumerate(gsrc)] + [blk, blk], out_specs=[blk] * 4,
        out_shape=[out] * 4, compiler_params=_params(("parallel",)),
    )(w, *[g for g, _ in gsrc], m, v)


def _adamw_small(ws, gs, ms, vs, *, name):
    n = len(ws)

    def body(*refs):
        w_r, g_r, m_r, v_r = refs[:n], refs[n:2 * n], refs[2 * n:3 * n], refs[3 * n:4 * n]
        d_r, nm_r, nv_r = refs[4 * n:5 * n], refs[5 * n:6 * n], refs[6 * n:7 * n]
        for k in range(n):
            d_r[k][...], nm_r[k][...], nv_r[k][...] = _adamw_math(w_r[k][...], g_r[k][...], m_r[k][...], v_r[k][...])

    vm = pl.BlockSpec(memory_space=pltpu.VMEM)
    out = [jax.ShapeDtypeStruct(w.shape, F32) for w in ws]
    res = pl.pallas_call(body, name=name, in_specs=[vm] * (4 * n), out_specs=[vm] * (3 * n), out_shape=out * 3,
                         compiler_params=pltpu.CompilerParams(vmem_limit_bytes=VMEM_LIMIT))(*ws, *gs, *ms, *vs)
    return res[:n], res[n:2 * n], res[2 * n:]


def _sum_slots(x, *, name):
    n, R, C = x.shape
    tr = R if (n + 1) * R * C * 4 <= _ONE_BLOCK_BYTES else _tile(R, 256, 8)

    def body(x_ref, o_ref):
        acc = x_ref[0]
        for k in range(1, n):
            acc = acc + x_ref[k]
        o_ref[...] = acc

    return pl.pallas_call(
        body, name=name, grid=(R // tr,),
        in_specs=[pl.BlockSpec((n, tr, C), lambda i: (0, i, 0))], out_specs=pl.BlockSpec((tr, C), lambda i: (i, 0)),
        out_shape=jax.ShapeDtypeStruct((R, C), F32), compiler_params=_params(("parallel",)),
    )(x)


def _pair_sum(g, r, where, *, name):
    n, R, C = g.shape
    Rh = R // 2
    tr = _tile(Rh, 256, 8)
    nb = Rh // tr

    def body(where_ref, g_ref, r_ref, o_ref):
        o_ref[...] = (g_ref[...] + r_ref[...]).astype(BF16)

    def slot(p, w):
        return p + jnp.where(p >= w[0], 1, 0)

    return pl.pallas_call(
        body, name=name,
        grid_spec=pltpu.PrefetchScalarGridSpec(
            num_scalar_prefetch=1, grid=(n - 1, nb),
            in_specs=[pl.BlockSpec((1, tr, C), lambda p, i, w: (slot(p, w), w[1] * nb + i, 0)),
                      pl.BlockSpec((1, tr, C), lambda p, i, w: (slot(p, w), i, 0))],
            out_specs=pl.BlockSpec((1, tr, C), lambda p, i, w: (slot(p, w), i, 0)),
        ),
        out_shape=jax.ShapeDtypeStruct((n, Rh, C), BF16), compiler_params=_params(("parallel", "parallel")),
    )(where, g, r)


def _chip_sum(g, r, slots, where, *, name):
    n, R, C = g.shape
    Rh = R // 2
    tr = _tile(Rh, 256, 8)
    nb = Rh // tr

    def body(w_ref, g_ref, r_ref, s_ref, o_ref):
        acc = g_ref[0] + r_ref[0]
        for k in range(slots.shape[0]):
            acc = acc + s_ref[k].astype(F32)
        o_ref[...] = acc

    return pl.pallas_call(
        body, name=name,
        grid_spec=pltpu.PrefetchScalarGridSpec(
            num_scalar_prefetch=1, grid=(nb,),
            in_specs=[pl.BlockSpec((1, tr, C), lambda i, w: (w[0], w[1] * nb + i, 0)),
                      pl.BlockSpec((1, tr, C), lambda i, w: (w[0], i, 0)),
                      pl.BlockSpec((slots.shape[0], tr, C), lambda i, w: (0, i, 0))],
            out_specs=pl.BlockSpec((tr, C), lambda i, w: (w[1] * nb + i, 0)),
        ),
        out_shape=jax.ShapeDtypeStruct((R, C), F32), compiler_params=_params(("parallel",)),
    )(where, g, r, slots)


ANY = pl.BlockSpec(memory_space=pl.ANY)


def _place():
    return lax.axis_index("x"), lax.axis_index("y"), lax.axis_index("c")


def _other_chips(x, y):
    return [(1 - x, y), (x, 1 - y), (1 - x, 1 - y)]


def _aliased_comm_call(body, bufs, n_sems, *, name):
    n = len(bufs)
    return pl.pallas_call(
        body, name=name, out_shape=[jax.ShapeDtypeStruct(b.shape, b.dtype) for b in bufs],
        in_specs=[ANY] * n, out_specs=[ANY] * n, input_output_aliases={k: k for k in range(n)},
        scratch_shapes=[pltpu.SemaphoreType.DMA((n_sems,)), pltpu.SemaphoreType.DMA((n_sems,))],
    )(*bufs)


HBM = pl.BlockSpec(memory_space=pltpu.HBM)
SEM = pl.BlockSpec(memory_space=pltpu.SEMAPHORE)
_SPLIT = pltpu.CompilerParams(has_side_effects=pltpu.SideEffectType.DATAFLOW_SIDE_EFFECTING)


def _in_hbm(arrs):
    return [pltpu.with_memory_space_constraint(a, pltpu.HBM) for a in arrs]


def _gather_ici_start(bufs, after, *, name):
    n = len(bufs)

    def body(*refs):
        send_sems, recv_sems, outs, token = refs[n + 1], refs[n + 2], refs[n + 3:2 * n + 3], refs[2 * n + 3]
        x, y, c = _place()
        for b in range(n):
            rh = bufs[b].shape[1] // 2
            part = outs[b].at[2 * x + y, pl.ds(c * rh, rh), :]
            for j, chip in enumerate(_other_chips(x, y)):
                pltpu.make_async_remote_copy(src_ref=part, dst_ref=part, send_sem=send_sems.at[3 * b + j],
                                             recv_sem=recv_sems.at[3 * b + j], device_id=(*chip, c),
                                             device_id_type=MESH).start()
        token[...] = jnp.zeros_like(token)

    res = pl.pallas_call(
        body, name=name,
        out_shape=(pltpu.SemaphoreType.DMA((3 * n,)), pltpu.SemaphoreType.DMA((3 * n,)),
                   *[pltpu.HBM(b.shape, b.dtype) for b in bufs], jax.ShapeDtypeStruct((8, 128), F32)),
        in_specs=[HBM] * n + [ANY], out_specs=(SEM, SEM, *[HBM] * n, pl.BlockSpec(memory_space=pltpu.VMEM)),
        input_output_aliases={k: k + 2 for k in range(n)}, compiler_params=_SPLIT,
    )(*_in_hbm(bufs), after)
    return res[0], res[1], list(res[2:2 + n]), res[2 + n]


def _gather_ici_wait(send_sems, recv_sems, bufs, after, *, name):
    n = len(bufs)

    def body(*refs):
        ins, ss, rs = refs[:n], refs[n], refs[n + 1]
        x, y, c = _place()
        for b in range(n):
            rh = bufs[b].shape[1] // 2
            mine = ins[b].at[2 * x + y, pl.ds(c * rh, rh), :]
            for j, (cx, cy) in enumerate(_other_chips(x, y)):
                theirs = ins[b].at[2 * cx + cy, pl.ds(c * rh, rh), :]
                cp = pltpu.make_async_remote_copy(src_ref=mine, dst_ref=theirs, send_sem=ss.at[3 * b + j],
                                                  recv_sem=rs.at[3 * b + j], device_id=(cx, cy, c),
                                                  device_id_type=MESH)
                cp.wait_send()
                cp.wait_recv()

    return list(pl.pallas_call(
        body, name=name, out_shape=[pltpu.HBM(b.shape, b.dtype) for b in bufs],
        in_specs=[HBM] * n + [SEM, SEM, ANY], out_specs=[HBM] * n,
        input_output_aliases={k: k for k in range(n)}, compiler_params=_SPLIT,
    )(*bufs, send_sems, recv_sems, after))


def _gather_forward(bufs, *, name):
    n = len(bufs)

    def body(*refs):
        outs, send_sems, recv_sems = refs[n:2 * n], refs[2 * n], refs[2 * n + 1]
        x, y, c = _place()

        def copy(b, j, chip, hc):
            rh = bufs[b].shape[1] // 2
            part = outs[b].at[2 * chip[0] + chip[1], pl.ds(hc * rh, rh), :]
            return pltpu.make_async_remote_copy(src_ref=part, dst_ref=part, send_sem=send_sems.at[3 * b + j],
                                                recv_sem=recv_sems.at[3 * b + j], device_id=(x, y, 1 - c),
                                                device_id_type=MESH)

        sends = [copy(b, j, chip, c) for b in range(n) for j, chip in enumerate(_other_chips(x, y))]
        for cp in sends:
            cp.start()
        for b in range(n):
            for j, chip in enumerate(_other_chips(x, y)):
                copy(b, j, chip, 1 - c).wait_recv()
        for cp in sends:
            cp.wait_send()

    return _aliased_comm_call(body, bufs, 3 * n, name=name)


def _chip_exchange_start(hs, *, name):
    n = len(hs)
    lands = [lax.empty((3,) + h.shape[1:], h.dtype) for h in hs]

    def body(*refs):
        send_sems, recv_sems = refs[2 * n], refs[2 * n + 1]
        h_out, l_out, token = refs[2 * n + 2:3 * n + 2], refs[3 * n + 2:4 * n + 2], refs[4 * n + 2]
        x, y, c = _place()
        for b in range(n):
            for j, (cx, cy) in enumerate(_other_chips(x, y)):
                pltpu.make_async_remote_copy(src_ref=h_out[b].at[2 * cx + cy], dst_ref=l_out[b].at[j],
                                             send_sem=send_sems.at[3 * b + j], recv_sem=recv_sems.at[3 * b + j],
                                             device_id=(cx, cy, c), device_id_type=MESH).start()
        token[...] = jnp.zeros_like(token)

    res = pl.pallas_call(
        body, name=name,
        out_shape=(pltpu.SemaphoreType.DMA((3 * n,)), pltpu.SemaphoreType.DMA((3 * n,)),
                   *[pltpu.HBM(a.shape, a.dtype) for a in hs + lands], jax.ShapeDtypeStruct((8, 128), F32)),
        in_specs=[HBM] * (2 * n), out_specs=(SEM, SEM, *[HBM] * (2 * n), pl.BlockSpec(memory_space=pltpu.VMEM)),
        input_output_aliases={k: k + 2 for k in range(2 * n)}, compiler_params=_SPLIT,
    )(*_in_hbm(hs + lands))
    return res[0], res[1], list(res[2:2 + n]), list(res[2 + n:2 + 2 * n]), res[2 + 2 * n]


def _chip_exchange_wait(send_sems, recv_sems, hs, lands, after, *, name):
    n = len(hs)

    def body(*refs):
        h_in, l_in, ss, rs = refs[:n], refs[n:2 * n], refs[2 * n], refs[2 * n + 1]
        x, y, c = _place()
        for b in range(n):
            for j, (cx, cy) in enumerate(_other_chips(x, y)):
                cp = pltpu.make_async_remote_copy(src_ref=h_in[b].at[2 * cx + cy], dst_ref=l_in[b].at[j],
                                                  send_sem=ss.at[3 * b + j], recv_sem=rs.at[3 * b + j],
                                                  device_id=(cx, cy, c), device_id_type=MESH)
                cp.wait_send()
                cp.wait_recv()

    res = pl.pallas_call(
        body, name=name, out_shape=[pltpu.HBM(a.shape, a.dtype) for a in hs + lands],
        in_specs=[HBM] * (2 * n) + [SEM, SEM, ANY], out_specs=[HBM] * (2 * n),
        input_output_aliases={k: k for k in range(2 * n)}, compiler_params=_SPLIT,
    )(*hs, *lands, send_sems, recv_sems, after)
    return list(res[n:])


def _peers(x, y, c):
    return [((1 - x) if fx else x, (1 - y) if fy else y, (1 - c) if fc else c)
            for fx in (0, 1) for fy in (0, 1) for fc in (0, 1) if fx or fy or fc]


def _all_to_all_start(slab, after, *, name):
    land = lax.empty((N_DEV,) + slab.shape, slab.dtype)

    def body(slab_in, land_in, after_ref, send_sems, recv_sems, slab_out, land_out, token):
        x, y, c = _place()
        for k, peer in enumerate(_peers(x, y, c)):
            pltpu.make_async_remote_copy(src_ref=slab_out, dst_ref=land_out.at[4 * x + 2 * y + c],
                                         send_sem=send_sems.at[k], recv_sem=recv_sems.at[k], device_id=peer,
                                         device_id_type=MESH).start()
        token[...] = jnp.zeros_like(token)

    return pl.pallas_call(
        body, name=name,
        out_shape=(pltpu.SemaphoreType.DMA((N_DEV - 1,)), pltpu.SemaphoreType.DMA((N_DEV - 1,)),
                   pltpu.HBM(slab.shape, slab.dtype), pltpu.HBM(land.shape, land.dtype),
                   jax.ShapeDtypeStruct((8, 128), F32)),
        in_specs=[HBM, HBM, ANY], out_specs=(SEM, SEM, HBM, HBM, pl.BlockSpec(memory_space=pltpu.VMEM)),
        input_output_aliases={0: 2, 1: 3}, compiler_params=_SPLIT,
    )(*_in_hbm([slab, land]), after)


def _all_to_all_wait(send_sems, recv_sems, slab, land, afters, *, name):
    def body(slab_in, land_in, ss, rs, *_):
        x, y, c = _place()
        for k, (px, py, pc) in enumerate(_peers(x, y, c)):
            cp = pltpu.make_async_remote_copy(src_ref=slab_in, dst_ref=land_in.at[4 * px + 2 * py + pc],
                                              send_sem=ss.at[k], recv_sem=rs.at[k], device_id=(px, py, pc),
                                              device_id_type=MESH)
            cp.wait_send()
            cp.wait_recv()

    return pl.pallas_call(
        body, name=name, out_shape=[pltpu.HBM(slab.shape, slab.dtype), pltpu.HBM(land.shape, land.dtype)],
        in_specs=[HBM, HBM, SEM, SEM] + [ANY] * len(afters), out_specs=[HBM, HBM], input_output_aliases={0: 0, 1: 1},
        compiler_params=_SPLIT,
    )(slab, land, send_sems, recv_sems, *afters)


def _pair_exchange_start(gs, *, name):
    n = len(gs)
    lands = [lax.empty((g.shape[0], g.shape[1] // 2, g.shape[2]), g.dtype) for g in gs]

    def body(*refs):
        send_sems, recv_sems = refs[2 * n], refs[2 * n + 1]
        g_out, l_out, token = refs[2 * n + 2:3 * n + 2], refs[3 * n + 2:4 * n + 2], refs[4 * n + 2]
        x, y, c = _place()
        for b in range(n):
            rh = gs[b].shape[1] // 2
            pltpu.make_async_remote_copy(src_ref=g_out[b].at[:, pl.ds((1 - c) * rh, rh), :], dst_ref=l_out[b],
                                         send_sem=send_sems.at[b], recv_sem=recv_sems.at[b],
                                         device_id=(x, y, 1 - c), device_id_type=MESH).start()
        token[...] = jnp.zeros_like(token)

    res = pl.pallas_call(
        body, name=name,
        out_shape=(pltpu.SemaphoreType.DMA((n,)), pltpu.SemaphoreType.DMA((n,)),
                   *[pltpu.HBM(a.shape, a.dtype) for a in gs + lands], jax.ShapeDtypeStruct((8, 128), F32)),
        in_specs=[HBM] * (2 * n), out_specs=(SEM, SEM, *[HBM] * (2 * n), pl.BlockSpec(memory_space=pltpu.VMEM)),
        input_output_aliases={k: k + 2 for k in range(2 * n)}, compiler_params=_SPLIT,
    )(*_in_hbm(gs + lands))
    return res[0], res[1], list(res[2:2 + n]), list(res[2 + n:2 + 2 * n]), res[2 + 2 * n]


def _pair_exchange_wait(send_sems, recv_sems, gs, lands, after, *, name):
    n = len(gs)

    def body(*refs):
        g_in, l_in, ss, rs = refs[:n], refs[n:2 * n], refs[2 * n], refs[2 * n + 1]
        x, y, c = _place()
        for b in range(n):
            rh = gs[b].shape[1] // 2
            cp = pltpu.make_async_remote_copy(src_ref=g_in[b].at[:, pl.ds((1 - c) * rh, rh), :], dst_ref=l_in[b],
                                              send_sem=ss.at[b], recv_sem=rs.at[b], device_id=(x, y, 1 - c),
                                              device_id_type=MESH)
            cp.wait_send()
            cp.wait_recv()

    res = pl.pallas_call(
        body, name=name, out_shape=[pltpu.HBM(a.shape, a.dtype) for a in gs + lands],
        in_specs=[HBM] * (2 * n) + [SEM, SEM, ANY], out_specs=[HBM] * (2 * n),
        input_output_aliases={k: k for k in range(2 * n)}, compiler_params=_SPLIT,
    )(*gs, *lands, send_sems, recv_sems, after)
    return list(res[:n]), list(res[n:])


def _pair_share_start(ss, *, name):
    n = len(ss)

    def body(*refs):
        send_sems, recv_sems, outs, token = refs[n], refs[n + 1], refs[n + 2:2 * n + 2], refs[2 * n + 2]
        x, y, c = _place()
        for b in range(n):
            rh = ss[b].shape[0] // 2
            mine = outs[b].at[pl.ds(c * rh, rh), :]
            pltpu.make_async_remote_copy(src_ref=mine, dst_ref=mine, send_sem=send_sems.at[b],
                                         recv_sem=recv_sems.at[b], device_id=(x, y, 1 - c),
                                         device_id_type=MESH).start()
        token[...] = jnp.zeros_like(token)

    res = pl.pallas_call(
        body, name=name,
        out_shape=(pltpu.SemaphoreType.DMA((n,)), pltpu.SemaphoreType.DMA((n,)),
                   *[pltpu.HBM(a.shape, a.dtype) for a in ss], jax.ShapeDtypeStruct((8, 128), F32)),
        in_specs=[HBM] * n, out_specs=(SEM, SEM, *[HBM] * n, pl.BlockSpec(memory_space=pltpu.VMEM)),
        input_output_aliases={k: k + 2 for k in range(n)}, compiler_params=_SPLIT,
    )(*_in_hbm(ss))
    return res[0], res[1], list(res[2:2 + n]), res[2 + n]


def _pair_share_wait(send_sems, recv_sems, ss, after, *, name):
    n = len(ss)

    def body(*refs):
        ins, sems_s, sems_r = refs[:n], refs[n], refs[n + 1]
        x, y, c = _place()
        for b in range(n):
            rh = ss[b].shape[0] // 2
            mine = ins[b].at[pl.ds(c * rh, rh), :]
            theirs = ins[b].at[pl.ds((1 - c) * rh, rh), :]
            cp = pltpu.make_async_remote_copy(src_ref=mine, dst_ref=theirs, send_sem=sems_s.at[b],
                                              recv_sem=sems_r.at[b], device_id=(x, y, 1 - c),
                                              device_id_type=MESH)
            cp.wait_send()
            cp.wait_recv()

    return list(pl.pallas_call(
        body, name=name, out_shape=[pltpu.HBM(a.shape, a.dtype) for a in ss],
        in_specs=[HBM] * n + [SEM, SEM, ANY], out_specs=[HBM] * n,
        input_output_aliases={k: k for k in range(n)}, compiler_params=_SPLIT,
    )(*ss, send_sems, recv_sems, after))


_SMALL_SHARDED = (("e_conv_w", 2), ("o_norm", 1), ("o_d", 1))
_REPLICATED = ("e_norm", "e_gmlp_w", "e_gmlp_b", "e_conv_b", "e_conv_ln_g", "e_conv_ln_b", "o_lam_re", "o_lam_im",
               "o_log_dt", "o_b_re", "o_b_im", "o_c_re", "o_c_im", "ca_norm", "ca_mem_norm", "ffn_norm", "final_norm")
_SMALL = tuple(n for n, _ in _SMALL_SHARDED) + _REPLICATED
_WEIGHTS = ("e_norm", "e_w_in", "e_gmlp_w", "e_gmlp_b", "e_conv_w", "e_conv_b", "e_conv_ln_g", "e_conv_ln_b",
            "e_w_out", "o_norm", "o_w_in", "o_lam_re", "o_lam_im", "o_log_dt", "o_b_re", "o_b_im", "o_c_re", "o_c_im",
            "o_d", "o_w_out", "ca_norm", "ca_mem_norm", "ca_wq", "ca_wk", "ca_wv", "ca_wo", "ffn_norm", "ffn_w_gate",
            "ffn_w_up", "ffn_w_down", "final_norm")


def _pack_rows(arrs, width, dtype, row_mult=8):
    parts, spans, r0 = [], [], 0
    for a in arrs:
        flat = a.reshape(-1).astype(dtype)
        rows = -(-flat.shape[0] // (width * row_mult)) * row_mult
        if rows * width != flat.shape[0]:
            flat = jnp.pad(flat, (0, rows * width - flat.shape[0]))
        parts.append(flat.reshape(rows, width))
        spans.append((r0, rows))
        r0 += rows
    return jnp.concatenate(parts, axis=0), spans


def _unpack_rows(slab, spans, shapes):
    out = []
    for (r0, rows), shp in zip(spans, shapes):
        n = math.prod(shp)
        out.append(slab[r0:r0 + rows].reshape(-1)[:n].reshape(shp))
    return out


def _two_d(a):
    return a.reshape(-1, a.shape[-1])


def _shard_rows(n, a):
    return _two_d(jnp.swapaxes(a, -1, -2) if n in _TRANSPOSED else a)


def _from_shard_rows(n, rows, shape):
    if n in _TRANSPOSED:
        return jnp.swapaxes(rows.reshape(shape[:-2] + (shape[-1], shape[-2])), -1, -2)
    return rows.reshape(shape)


def _local_slab(local, slab, dtype):
    parts = sorted((r0, n, l) for n, (_, where) in _PLACE.items() for l, (s, r0) in enumerate(where) if s == slab)
    shards = [_shard_rows(n, local[n] if len(_PLACE[n][1]) == 1 else local[n][l]) for _, n, l in parts]
    return jnp.concatenate([a.astype(dtype) for a in shards], axis=0)


def _set_diag(b, pattern):
    return jnp.einsum(pattern, b, jnp.eye(C_GROUPS // N_SETS, dtype=b.dtype))


def _s5_discretize(lam_re, lam_im, log_dt, b_re, b_im):
    dt = jnp.exp(log_dt)[:, None]
    mag = jnp.exp(lam_re * dt)
    ar = mag * jnp.cos(lam_im * dt)
    ai = mag * jnp.sin(lam_im * dt)
    den = lam_re * lam_re + lam_im * lam_im
    qr = ((ar - 1.0) * lam_re + ai * lam_im) / den
    qi = (ai * lam_re - (ar - 1.0) * lam_im) / den
    bbr = qr[..., None] * b_re - qi[..., None] * b_im
    bbi = qr[..., None] * b_im + qi[..., None] * b_re
    return ar, ai, bbr, bbi


def _attention_block(x, mem, W, w, i, tag):
    xn, q = _norm_mm(x, w["ca_norm"][i], _shards(W, "ca_wq", i), split="k", out_dtype=BF16, name=f"{tag}_q")
    memn = _rms_fwd(mem, w["ca_mem_norm"][i], name=f"{tag}_ca_memnorm")
    k = _mm_k(memn, _shards(W, "ca_wk", i), out_dtype=BF16, name=f"{tag}_k")
    v = _mm_k(memn, _shards(W, "ca_wv", i), out_dtype=BF16, name=f"{tag}_v")
    o = _attn_fwd(q, k, v, name=f"{tag}_attn")
    y = _mm_k(o, _shards(W, "ca_wo", i), add=x, name=f"{tag}_wo")
    return y, (x, xn, memn, q, k, v, o)


def _attention_block_bwd(dy, saved, mem, W, w, i, tag, G, grads, token=None, mid=None):
    x, xn, memn, q, k, v, o = saved
    gain = w["ca_norm"][i]
    if token is not None:
        k = _behind(k, token)
    G = _grad_to_slab(G, "ca_wo", i, o, dy, a_cols=256, name=f"{tag}_dwo")
    dq, dk, dv = _attn_bwd(dy, _shards(W, "ca_wo", i), q, k, v, name=f"{tag}_attn_bwd")
    token = mid(dq) if mid is not None else None
    if token is not None:
        gain = _behind(gain, token)
    G = _grad_to_slab(G, "ca_wq", i, xn, dq, a_cols=256, name=f"{tag}_dwq")
    G = _grad_to_slab(G, "ca_wk", i, memn, dk, a_cols=256, name=f"{tag}_dwk")
    G = _grad_to_slab(G, "ca_wv", i, memn, dv, a_cols=256, name=f"{tag}_dwv")
    dmemn = _mm_k_t([(dk, _shards(W, "ca_wk", i)), (dv, _shards(W, "ca_wv", i))], name=f"{tag}_dmemn")
    dx, dg = _norm_bwd_k(dq, _shards(W, "ca_wq", i), x, gain, dy, name=f"{tag}_dq_norm_bwd")
    grads["ca_norm"][i] = dg[0]
    grads["ca_mem_norm"][i] = _rms_dg(mem, w["ca_mem_norm"][i], dmemn, name=f"{tag}_ca_memnorm_bwd")[0]
    return dx, G


def _ffn_block(x, W, w, i, tag, head=None):
    fn, gate, up, h = _ffn_up(x, w["ffn_norm"][i], _shards(W, "ffn_w_gate", i), _shards(W, "ffn_w_up", i),
                              name=f"{tag}_ffn_up")
    if head is None:
        y = _mm_k(h, _shards(W, "ffn_w_down", i), add=x, name=f"{tag}_down")
    else:
        y = _down_loss_head(h, _shards(W, "ffn_w_down", i), x, *head, name=f"{tag}_down_loss_head")
    return y, (x, fn, gate, up, h)


def _ffn_block_bwd(dy, saved, W, w, i, tag, G, grads, token=None, mid=None):
    x, fn, gate, up, h = saved
    gain = w["ffn_norm"][i]
    G = _grad_to_slab(G, "ffn_w_down", i, h, dy, name=f"{tag}_dwd")
    dg, du = _ffn_bwd_hidden(dy, _shards(W, "ffn_w_down", i), gate, up, token, name=f"{tag}_ffn_bwd_hidden")
    token = mid(dg) if mid is not None else None
    if token is not None:
        gain = _behind(gain, token)
    G = _grad_to_slab(G, "ffn_w_gate", i, dg, fn, name=f"{tag}_dwg")
    G = _grad_to_slab(G, "ffn_w_up", i, du, fn, name=f"{tag}_dwu")
    dx, dgn = _ffn_in_bwd(dg, du, _shards(W, "ffn_w_gate", i), _shards(W, "ffn_w_up", i), x, gain, dy,
                          name=f"{tag}_ffn_in_bwd")
    grads["ffn_norm"][i] = dgn[0]
    return dx, G


def _gmlp_mask():
    chunk = jnp.arange(GMLP_BLOCK) // CHUNK
    return chunk[None, :] <= chunk[:, None]


def _even_block(x, W, w, tag):
    hn, proj = _norm_mm(x, w["e_norm"][0], _shards(W, "e_w_in"), split="n", out_dtype=F32, name=f"{tag}_w_in")
    wm = jnp.where(_gmlp_mask()[None], w["e_gmlp_w"][0], 0.0).astype(BF16)
    bcol = w["e_gmlp_b"][0][:, :, None]
    cw = jnp.pad(w["e_conv_w"][0], ((0, CONV_HALO - CONV_WIDTH), (0, 0)))
    cb, lg, lb = w["e_conv_b"], w["e_conv_ln_g"], w["e_conv_ln_b"]
    mix, hc = _even_fwd(proj, wm, bcol, cw, cb, lg, lb, name=f"{tag}_mixers")
    y = _mm_k(mix, _shards(W, "e_w_out"), add=x, name=f"{tag}_w_out")
    return y, (x, hn, proj, mix, hc, wm, bcol, cw)


def _even_block_bwd(dy, saved, W, w, tag, G, grads):
    x, hn, proj, mix, hc, wm, bcol, cw = saved
    dmix = _mm_k_t([(dy, _shards(W, "e_w_out"))], name=f"{tag}_dmix")
    G = _grad_to_slab(G, "e_w_out", 0, mix, dy, a_cols=256, name=f"{tag}_dw_out")
    wmt = jnp.swapaxes(wm, 1, 2)
    dpa, dhc, dwm, db, dlg, dlb, dcb = _even_bwd1(proj, dmix, hc, wm, wmt, bcol, w["e_conv_ln_g"], w["e_conv_ln_b"],
                                                  name=f"{tag}_mixers_bwd1")
    dpb, dcw = _even_bwd2(proj, dhc, cw, name=f"{tag}_mixers_bwd2")
    grads["e_gmlp_w"] = jnp.where(_gmlp_mask()[None], dwm, 0.0)[None]
    grads["e_gmlp_b"] = db[:, :, 0][None]
    grads["e_conv_ln_g"], grads["e_conv_ln_b"], grads["e_conv_b"] = dlg, dlb, dcb
    grads["e_conv_w"] = dcw[:CONV_WIDTH][None]
    G = _grad_to_slab(G, "e_w_in", 0, hn, dpa, b_cols=512, chips=(0, 2), name=f"{tag}_dw_in_a")
    G = _grad_to_slab(G, "e_w_in", 0, hn, dpb, b_cols=512, chips=(2, 2), name=f"{tag}_dw_in_b")
    dx, dg = _norm_bwd_n((dpa, dpb), _shards(W, "e_w_in"), x, w["e_norm"][0], dy, name=f"{tag}_in_bwd")
    grads["e_norm"] = dg
    return dx, G


def _odd_block(x, W, w, tag):
    S = x.shape[0]
    hn, u = _norm_mm(x, w["o_norm"][0], _shards(W, "o_w_in"), split="k", out_dtype=F32, name=f"{tag}_w_in")
    disc_in = (w["o_lam_re"][0], w["o_lam_im"][0], w["o_log_dt"][0], w["o_b_re"][0], w["o_b_im"][0])
    (ar, ai, bbr, bbi), disc_vjp = jax.vjp(_s5_discretize, *disc_in)
    sets = (N_SETS, C_GROUPS // N_SETS)
    per_set = N_STATE // N_SETS
    bset = jnp.concatenate([_set_diag(b.reshape(sets + b.shape[1:]), "jgpc,gh->jgchp").reshape(N_SETS, SET_CH, per_set)
                            for b in (bbr, bbi)], axis=2).astype(BF16)
    cset = jnp.concatenate([_set_diag(c.reshape(sets + c.shape[1:]), "jgcp,gh->jgphc").reshape(N_SETS, per_set, SET_CH)
                            for c in (w["o_c_re"][0], -w["o_c_im"][0])], axis=1).astype(BF16)
    powers, pr, pi = [], ar, ai
    for _ in range(SCAN_BLOCK):
        powers.append(jnp.concatenate([pr.reshape(STATE_ROWS, STATE_LANES), pi.reshape(STATE_ROWS, STATE_LANES)], 0))
        pr, pi = pr * ar - pi * ai, pr * ai + pi * ar
    pw = jnp.stack(powers, axis=0)
    xs = _scan_fwd(u, bset, pw, name=f"{tag}_scan").reshape(S // 8, STATE_ROWS, 8, STATE_LANES)
    yv, yg = _s5_readout(xs, cset, u, w["o_d"], name=f"{tag}_readout")
    o, y = _glu_out(yg, _shards(W, "o_w_out"), x, name=f"{tag}_glu_out")
    return y, (x, hn, u, bset, cset, pw, xs, yv, yg, o, disc_vjp)


def _odd_block_bwd(dy, saved, W, w, tag, G, grads):
    x, hn, u, bset, cset, pw, xs, yv, yg, o, disc_vjp = saved
    S = x.shape[0]
    do, dys, dus, dd = _glu_out_bwd(o, dy, _shards(W, "o_w_out"), yv, u, w["o_d"], name=f"{tag}_glu_out_bwd")
    G = _grad_to_slab(G, "o_w_out", 0, yg, do, b_cols=512, name=f"{tag}_dw_out")
    grads["o_d"] = dd
    dcset_t = _state_grad_sets(dys, xs, name=f"{tag}_dcd")
    gs, da = _scan_bwd(dys, cset, xs.reshape(S * STATE_ROWS, STATE_LANES), pw, name=f"{tag}_scan_bwd")
    gs = gs.reshape(xs.shape)
    dbset = _state_grad_sets(u, gs, name=f"{tag}_dbd")
    du, dx, dg = _s5_in_bwd(gs, bset, dus, _shards(W, "o_w_in"), x, w["o_norm"][0], dy, name=f"{tag}_in_bwd")
    G = _grad_to_slab(G, "o_w_in", 0, hn, du, a_cols=256, name=f"{tag}_dw_in")
    grads["o_norm"] = dg
    per = C_GROUPS // N_SETS
    blocks = (N_SETS, per, C_GROUP_CH, 2, per, C_STATE)
    dc = _set_diag(dcset_t.reshape(blocks), "jhcrgp,gh->rjgcp").reshape(2, C_GROUPS, C_GROUP_CH, C_STATE)
    db = _set_diag(dbset.reshape(blocks), "jgcrhp,gh->rjgpc").reshape(2, C_GROUPS, C_STATE, C_GROUP_CH)
    dcr, dci, dbbr, dbbi = dc[0], -dc[1], db[0], db[1]
    dar = da[:STATE_ROWS].reshape(C_GROUPS, C_STATE)
    dai = da[STATE_ROWS:].reshape(C_GROUPS, C_STATE)
    dlr, dli, dldt, dbr, dbi = disc_vjp((dar, dai, dbbr, dbbi))
    grads["o_lam_re"], grads["o_lam_im"], grads["o_log_dt"] = dlr[None], dli[None], dldt[None]
    grads["o_b_re"], grads["o_b_im"], grads["o_c_re"], grads["o_c_im"] = dbr[None], dbi[None], dcr[None], dci[None]
    return dx, G


def _behind(value, token):
    return value + token[0, 0].astype(value.dtype)


class _NoExchange:
    def __init__(self, W):
        self.W = W

    def first_weights(self, w):
        return self.W, w

    def weights(self, stage, after):
        return {}

    def grads_ready(self, piece, G):
        return None

    def grads_crossed(self, piece, after):
        return None


def _forward_backward(xs_, mems_, tgt, w, G, exchange):
    W, w = exchange.first_weights(w)
    x1, s_mix0 = _even_block(xs_, W, w, "l0")
    W = {**W, **exchange.weights(1, x1)}
    x2, s_att0 = _attention_block(x1, mems_, W, w, 0, "l0")
    W = {**W, **exchange.weights(2, x2)}
    x3, s_ffn0 = _ffn_block(x2, W, w, 0, "l0")
    W = {**W, **exchange.weights(3, x3)}
    x4, s_mix1 = _odd_block(x3, W, w, "l1")
    x5, s_att1 = _attention_block(x4, mems_, W, w, 1, "l1")
    (dx, dfinal, loss_lanes), s_ffn1 = _ffn_block(x5, W, w, 1, "l1", head=(w["final_norm"], tgt))

    grads = {n: [None, None] for n in ("ca_norm", "ca_mem_norm", "ffn_norm")}
    grads["final_norm"] = dfinal[0]
    dx, G = _ffn_block_bwd(dx, s_ffn1, W, w, 1, "l1", G, grads)
    dx, G = _attention_block_bwd(dx, s_att1, mems_, W, w, 1, "l1", G, grads)
    dx, G = _odd_block_bwd(dx, s_mix1, W, w, "l1", G, grads)
    token = exchange.grads_ready("l1", G)
    dx, G = _ffn_block_bwd(dx, s_ffn0, W, w, 0, "l0", G, grads, token,
                           lambda after: exchange.grads_crossed("l1", after))
    token = exchange.grads_ready("ffn0", G)
    dx, G = _attention_block_bwd(dx, s_att0, mems_, W, w, 0, "l0", G, grads, token,
                                 lambda after: exchange.grads_crossed("ffn0", after))
    dx, G = _even_block_bwd(dx, s_mix0, W, w, "l0", G, grads)
    for n in list(grads):
        if isinstance(grads[n], list):
            grads[n] = jnp.stack(grads[n], axis=0)
        grads[n] = grads[n].reshape(w[n].shape)
    return loss_lanes, dx, G, grads


class _Exchange:
    def __init__(self, local, chip, core):
        self.bufs = {s: lax.dynamic_update_slice(lax.empty((N_CHIPS, rows, width), BF16),
                                                 _local_slab(local, s, BF16)[None], (chip, 0, 0))
                     for s, (width, rows) in _SLABS.items()}
        small = jnp.zeros((_SMALL_SLAB_ROWS, SMALL_W), F32)
        for n, (r0, rows) in _SMALL_PLACE.items():
            small = small.at[r0:r0 + rows].set(local[n].reshape(rows, SMALL_W))
        self.bufs[_SMALL_SLAB] = lax.dynamic_update_slice(lax.empty((N_CHIPS, _SMALL_SLAB_ROWS, SMALL_W), F32),
                                                          small[None], (chip, 0, 0))
        self.shard_shapes = {n: local[n].shape for n in _SMALL_PLACE}
        self.where = jnp.stack([chip, core]).astype(jnp.int32)
        self.flights = []
        self.reduces = {}

    def weights(self, stage, after):
        send_sems, recv_sems, bufs, _ = self.flights[stage]
        bufs = _gather_ici_wait(send_sems, recv_sems, bufs, after, name=f"gather_stage{stage}_wait")
        return dict(zip(self.stage_slabs(stage), _gather_forward(bufs, name=f"gather_stage{stage}_forward")))

    @staticmethod
    def stage_slabs(stage):
        return _STAGES[stage] + ((_SMALL_SLAB,) if stage == 0 else ())

    def first_weights(self, w):
        after = w["e_norm"]
        for k in range(len(_STAGES)):
            self.flights.append(_gather_ici_start([self.bufs[s] for s in self.stage_slabs(k)], after,
                                                  name=f"gather_stage{k}_start"))
            after = self.flights[-1][3]
        W = self.weights(0, after)
        w = {**w, "e_norm": _behind(w["e_norm"], after)}
        for (n, ax), (r0, rows) in zip(_SMALL_SHARDED, _SMALL_PLACE.values()):
            shards = [W[_SMALL_SLAB][p, r0:r0 + rows].reshape(self.shard_shapes[n]) for p in range(N_CHIPS)]
            w[n] = jnp.concatenate(shards, axis=ax)
        return W, w

    def pair_start(self, G, slabs, tag):
        send_sems, recv_sems, gl, lands, token = _pair_exchange_start([G[s] for s in slabs],
                                                                      name=f"grad_{tag}_pair_start")
        return (slabs, send_sems, recv_sems, gl, lands), token

    def pair_land(self, state, after, tag):
        slabs, send_sems, recv_sems, gl, lands = state
        gl, other = _pair_exchange_wait(send_sems, recv_sems, gl, lands, after, name=f"grad_{tag}_pair_wait")
        pairs = [_pair_sum(g, r, self.where, name=f"grad_pair_sum_{s}") for s, g, r in zip(slabs, gl, other)]
        send_sems, recv_sems, pairs, lands, token = _chip_exchange_start(pairs, name=f"grad_{tag}_chip_start")
        return (slabs, gl, other, send_sems, recv_sems, pairs, lands), token

    def reduce_sum(self, state, after, tag):
        slabs, gl, other, send_sems, recv_sems, pairs, lands = state
        slots = _chip_exchange_wait(send_sems, recv_sems, pairs, lands, after, name=f"grad_{tag}_chip_wait")
        return slabs, [_chip_sum(g, r, sl, self.where, name=f"grad_chip_sum_{s}")
                       for s, g, r, sl in zip(slabs, gl, other, slots)]

    @staticmethod
    def share_start(slabs, halves, tag):
        send_sems, recv_sems, halves, token = _pair_share_start(halves, name=f"grad_{tag}_share_start")
        return (slabs, send_sems, recv_sems, halves), token

    @staticmethod
    def share_finish(state, after, tag):
        slabs, send_sems, recv_sems, halves = state
        return dict(zip(slabs, _pair_share_wait(send_sems, recv_sems, halves, after, name=f"grad_{tag}_share_wait")))

    def grads_ready(self, piece, G):
        self.reduces[piece], token = self.pair_start(G, _GRAD_PIECES[piece], piece)
        return token

    def grads_crossed(self, piece, after):
        self.reduces[piece], token = self.pair_land(self.reduces[piece], after, piece)
        return token


def kernel(x, mem, e_norm, e_w_in, e_gmlp_w, e_gmlp_b, e_conv_w, e_conv_b, e_conv_ln_g, e_conv_ln_b, e_w_out, o_norm, o_w_in, o_lam_re, o_lam_im, o_log_dt, o_b_re, o_b_im, o_c_re, o_c_im, o_d, o_w_out, ca_norm, ca_mem_norm, ca_wq, ca_wk, ca_wv, ca_wo, ffn_norm, ffn_w_gate, ffn_w_up, ffn_w_down, final_norm, loss_target, m_e_norm, m_e_w_in, m_e_gmlp_w, m_e_gmlp_b, m_e_conv_w, m_e_conv_b, m_e_conv_ln_g, m_e_conv_ln_b, m_e_w_out, m_o_norm, m_o_w_in, m_o_lam_re, m_o_lam_im, m_o_log_dt, m_o_b_re, m_o_b_im, m_o_c_re, m_o_c_im, m_o_d, m_o_w_out, m_ca_norm, m_ca_mem_norm, m_ca_wq, m_ca_wk, m_ca_wv, m_ca_wo, m_ffn_norm, m_ffn_w_gate, m_ffn_w_up, m_ffn_w_down, m_final_norm, v_e_norm, v_e_w_in, v_e_gmlp_w, v_e_gmlp_b, v_e_conv_w, v_e_conv_b, v_e_conv_ln_g, v_e_conv_ln_b, v_e_w_out, v_o_norm, v_o_w_in, v_o_lam_re, v_o_lam_im, v_o_log_dt, v_o_b_re, v_o_b_im, v_o_c_re, v_o_c_im, v_o_d, v_o_w_out, v_ca_norm, v_ca_mem_norm, v_ca_wq, v_ca_wk, v_ca_wv, v_ca_wo, v_ffn_norm, v_ffn_w_gate, v_ffn_w_up, v_ffn_w_down, v_final_norm):
    args = dict(locals())
    local = {n: args[n] for n in _WEIGHTS}
    mom = {n: args["m_" + n] for n in _WEIGHTS}
    vel = {n: args["v_" + n] for n in _WEIGHTS}
    chip = 2 * lax.axis_index("x") + lax.axis_index("y")
    core = lax.axis_index("c")
    xs_, mems_, tgt = x[0], mem[0], loss_target[0]

    w = {n: local[n] for n in _REPLICATED}
    exchange = _Exchange(local, chip, core)
    G = {s: lax.empty((N_CHIPS, rows, width), F32) for s, (width, rows) in _SLABS.items()}
    loss_lanes, dx, G, grads = _forward_backward(xs_, mems_, tgt, w, G, exchange)

    gs_slab, gs_spans = _pack_rows([grads[n] for n in _SMALL] + [loss_lanes], SMALL_W, F32)
    rest0_token = exchange.grads_ready("rest0", G)
    small_flight = _all_to_all_start(gs_slab, rest0_token, name="small_grads_start")
    slabs_l1, halves_l1 = exchange.reduce_sum(exchange.reduces["l1"], small_flight[4], "l1")
    slabs_f0, halves_f0 = exchange.reduce_sum(exchange.reduces["ffn0"], small_flight[4], "ffn0")
    share, share_token = exchange.share_start(slabs_l1 + slabs_f0, halves_l1 + halves_f0, "l1_ffn0")
    token = exchange.grads_crossed("rest0", share_token)

    out_grads, delta, new_m, new_v = {}, {}, {}, {}

    def adamw_large(names):
        raw = []
        for n in names:
            shp = local[n].shape
            g_, d_, m_, v_ = _adamw_shard(_shard_rows(n, local[n]), [(gsum[s], r0) for s, r0 in _PLACE[n][1]],
                                          _shard_rows(n, mom[n]), _shard_rows(n, vel[n]), name=f"adamw_{n}")
            out_grads[n], delta[n], new_m[n], new_v[n] = (_from_shard_rows(n, t, shp) for t in (g_, d_, m_, v_))
            raw.append(d_)
        return raw

    gsum = exchange.share_finish(share, token, "l1_ffn0")
    ready = [n for n, (_, where) in _PLACE.items() if all(s in gsum for s, _ in where)]
    done = adamw_large(ready)

    gs_slab, gs_all = _all_to_all_wait(*small_flight[:4], done, name="small_grads_wait")
    gs_all = lax.dynamic_update_slice(gs_all, gs_slab[None], (2 * chip + core, 0, 0))
    gs_sum = _sum_slots(gs_all, name="small_grad_sum")
    *small_sums, loss_sum = _unpack_rows(gs_sum, gs_spans, [grads[n].shape for n in _SMALL] + [loss_lanes.shape])
    out_grads.update(zip(_SMALL, small_sums))
    for n, ax in _SMALL_SHARDED:
        width = local[n].shape[ax]
        out_grads[n] = lax.dynamic_slice_in_dim(out_grads[n], chip * width, width, axis=ax)
    d_, m_, v_ = _adamw_small([_two_d(local[n]) for n in _SMALL], [_two_d(out_grads[n]) for n in _SMALL],
                              [_two_d(mom[n]) for n in _SMALL], [_two_d(vel[n]) for n in _SMALL], name="adamw_small")
    for n, dd, mm_, vv in zip(_SMALL, d_, m_, v_):
        shp = local[n].shape
        delta[n], new_m[n], new_v[n] = dd.reshape(shp), mm_.reshape(shp), vv.reshape(shp)

    slabs_r0, halves_r0 = exchange.reduce_sum(exchange.reduces["rest0"], d_[0], "rest0")
    share, share_token = exchange.share_start(slabs_r0, halves_r0, "rest0")
    gsum = {**gsum, **exchange.share_finish(share, share_token, "rest0")}
    adamw_large([n for n in _PLACE if n not in ready])

    return (loss_sum[0, 0], dx[None], *[out_grads[n] for n in _WEIGHTS], *[delta[n] for n in _WEIGHTS],
            *[new_m[n] for n in _WEIGHTS], *[new_v[n] for n in _WEIGHTS])
```

```python
import functools
import math

import jax
import jax.numpy as jnp
from jax import lax
from jax.experimental import pallas as pl
from jax.experimental.pallas import tpu as pltpu

F32 = jnp.float32
BF16 = jnp.bfloat16
MESH = pl.DeviceIdType.MESH

EPS = 1e-6
D_MODEL = 1024
A_WIDTH = 512
A_GROUPS = 4
GMLP_BLOCK = 128
CHUNK = 64
B_WIDTH = 512
CONV_WIDTH = 31
CONV_HALO = 32
C_WIDTH = 512
C_GROUP_CH = 16
C_GROUPS = 32
C_STATE = 64
N_STATE = C_GROUPS * C_STATE
STATE_LANES = 128
STATE_ROWS = N_STATE // STATE_LANES
SCAN_BLOCK = 8
CA_HEADS = 4
CA_HEAD_DIM = 256
FFN_HIDDEN = 2816

ADAM_LR = 0.001
ADAM_B1 = 0.9
ADAM_B2 = 0.999
ADAM_EPS = 1e-08
ADAM_WD = 0.01
ADAM_STEP = 10

VMEM_LIMIT = 56 * 1024 * 1024
ACC_BYTES = 6 * 1024 * 1024
TN_VMEM_BYTES = 44 * 1024 * 1024
SMALL_W = 128
N_CHIPS = 4
N_DEV = 8

_SLABS = {"D0": (512, 1024), "E0": (1024, 256), "A0": (1024, 1024), "B0": (1024, 704), "C0": (1024, 1408),
          "D1": (512, 768), "A1": (1024, 1024), "B1": (1024, 704), "C1": (1024, 1408)}
_STAGES = (("D0", "E0"), ("A0",), ("B0", "C0"), ("D1", "A1", "B1", "C1"))
_LATE_SLABS = ("A1", "B1", "C1")
_GRAD_PIECES = {"l1": _STAGES[3], "ffn0": _STAGES[2], "rest0": _STAGES[0] + _STAGES[1]}
_PLACE = {
    "e_w_in": (1024, (("D0", 0),)), "e_w_out": (256, (("E0", 0),)),
    "o_w_out": (512, (("D1", 0),)), "o_w_in": (256, (("D1", 512),)),
    "ca_wq": (256, (("A0", 0), ("A1", 0))), "ca_wk": (256, (("A0", 256), ("A1", 256))),
    "ca_wv": (256, (("A0", 512), ("A1", 512))), "ca_wo": (256, (("A0", 768), ("A1", 768))),
    "ffn_w_down": (704, (("B0", 0), ("B1", 0))),
    "ffn_w_gate": (704, (("C0", 0), ("C1", 0))), "ffn_w_up": (704, (("C0", 704), ("C1", 704))),
}
_SMALL_SLAB = "F0"
_SMALL_SLAB_ROWS = 48
_SMALL_PLACE = {"e_conv_w": (0, 31), "o_norm": (32, 2), "o_d": (34, 1)}
_TRANSPOSED = ("ffn_w_gate", "ffn_w_up")


def _params(sem=None):
    return pltpu.CompilerParams(dimension_semantics=sem, vmem_limit_bytes=VMEM_LIMIT)


def _tile(n, pref, mult=128):
    if n <= pref:
        return n
    t = (pref // mult) * mult
    while t >= mult:
        if n % t == 0:
            return t
        t -= mult
    return n


def _blk(name, layer=0):
    rows, where = _PLACE[name]
    slab, r0 = where[layer]
    assert r0 % rows == 0
    return slab, rows, r0 // rows


def _shards(slabs, name, layer=0):
    slab, rows, b = _blk(name, layer)
    return [(slabs[slab], (None, rows, _SLABS[slab][0]), (p, b, 0)) for p in range(N_CHIPS)]


_GELU_C = 0.7978845608028654
_GELU_A = 0.044715


def _gelu(x):
    t = jnp.tanh(_GELU_C * (x + _GELU_A * (x * x * x)))
    return 0.5 * x * (1.0 + t), t


def _gelu_grad(x, t):
    return 0.5 * (1.0 + t) + 0.5 * x * (1.0 - t * t) * (_GELU_C * (1.0 + 3.0 * _GELU_A * x * x))


def _sigmoid(x):
    return 1.0 / (1.0 + jnp.exp(-x))


def _mean(x):
    return jnp.mean(x, axis=-1, keepdims=True)


def _dot(a, b):
    return jnp.dot(a, b, preferred_element_type=F32)


def _dot_nt(a, b):
    return lax.dot_general(a, b, (((1,), (1,)), ((), ())), preferred_element_type=F32)


def _dot_tn(a, b):
    return lax.dot_general(a, b, (((0,), (0,)), ((), ())), preferred_element_type=F32)


def _rms_tile(xv, gv):
    return (xv * lax.rsqrt(_mean(xv * xv) + EPS)) * gv


def _rms_bwd_tile(xv, gv, dyv):
    r = lax.rsqrt(_mean(xv * xv) + EPS)
    xh = xv * r
    dyg = dyv * gv
    return r * (dyg - xh * _mean(dyg * xh)), jnp.sum(dyv * xh, axis=0, keepdims=True)


def _cols(p, width):
    return slice(p * width, (p + 1) * width)


def _sum_k(a, ws, k):
    tot = None
    for p in range(N_CHIPS):
        y = _dot(a[:, _cols(p, k)], ws[p][...])
        tot = y if tot is None else tot + y
    return tot


def _cat_nt(a, ws):
    return jnp.concatenate([_dot_nt(a, ws[p][...]) for p in range(N_CHIPS)], axis=1)


def _rows_call(name, tm, rows, fulls, outs, accs, body, scratch=()):
    S = min(x.shape[-2] for x in rows if x.ndim != 4)
    nr, nf, no, na = len(rows), len(fulls), len(outs), len(accs)

    def kern(*refs):
        r, f = refs[:nr], refs[nr:nr + nf]
        o, a = refs[nr + nf:nr + nf + no], refs[nr + nf + no:nr + nf + no + na]
        if na:
            @pl.when(pl.program_id(0) == 0)
            def _():
                for ref in a:
                    ref[...] = jnp.zeros_like(ref)
        body(r, f, o, a, refs[nr + nf + no + na:])

    def whole(shape):
        nd = len(shape)
        return pl.BlockSpec(tuple(shape), lambda i: (0,) * nd)

    def row_spec(shape):
        if len(shape) == 4:
            return pl.BlockSpec((tm // 8,) + tuple(shape[1:]), lambda i: (i, 0, 0, 0))
        if len(shape) == 3:
            return pl.BlockSpec((shape[0], tm, shape[2]), lambda i: (0, i, 0))
        return pl.BlockSpec((tm, shape[1]), lambda i: (i, 0))

    def full_spec(x):
        if isinstance(x, tuple):
            _, bshape, bidx = x
            return pl.BlockSpec(bshape, lambda i: bidx, pipeline_mode=pl.Buffered(1))
        return whole(x.shape)

    out_shapes = [(S, o[0]) if len(o) == 2 else (o[0], S, o[1]) for o in outs]
    res = pl.pallas_call(
        kern, name=name, grid=(S // tm,),
        in_specs=[row_spec(x.shape) for x in rows] + [full_spec(x) for x in fulls],
        out_specs=[row_spec(s) for s in out_shapes] + [whole(shp) for shp, _ in accs],
        out_shape=[jax.ShapeDtypeStruct(s, o[-1]) for s, o in zip(out_shapes, outs)]
        + [jax.ShapeDtypeStruct(tuple(shp), dt) for shp, dt in accs],
        scratch_shapes=list(scratch),
        compiler_params=_params(("arbitrary",) if na else ("parallel",)),
    )(*rows, *[x[0] if isinstance(x, tuple) else x for x in fulls])
    return res[:no], res[no:]


def _grad_to_slab(gslabs, wname, layer, a, b, *, a_cols=None, b_cols=None, chips=(0, N_CHIPS), name):
    slab, rows, bidx = _blk(wname, layer)
    width = _SLABS[slab][0]
    p0, n_p = chips
    assert p0 % n_p == 0
    S = a.shape[-2]

    def tile_bytes(x, ts):
        return ts * x.dtype.itemsize * (x.shape[2] * n_p if x.ndim == 3 else x.shape[1])

    acc_bytes = n_p * rows * (-(-width // 128) * 128) * 4
    ts = next(t for t in (2048, 1024, 512, 256, S) if S % t == 0
              and 2 * (tile_bytes(a, t) + tile_bytes(b, t) + acc_bytes) <= TN_VMEM_BYTES or t == S)

    def operand(x):
        if x.ndim == 3:
            return pl.BlockSpec((n_p, ts, x.shape[2]), lambda s: (p0 // n_p, s, 0))
        return pl.BlockSpec((ts, x.shape[1]), lambda s: (s, 0))

    def part(ref, cols, p):
        if len(ref.shape) == 3:
            return ref[p]
        return ref[...] if cols is None else ref[:, _cols(p, cols)]

    def body(a_ref, b_ref, slab_ref, o_ref):
        @pl.when(pl.program_id(0) == 0)
        def _():
            o_ref[...] = jnp.zeros_like(o_ref)

        for p in range(n_p):
            o_ref[p] += _dot_tn(part(a_ref, a_cols, p).astype(BF16), part(b_ref, b_cols, p).astype(BF16))

    g = gslabs[slab]
    out = pl.pallas_call(
        body, name=name, grid=(S // ts,),
        in_specs=[operand(a), operand(b), pl.BlockSpec(memory_space=pl.ANY)],
        out_specs=pl.BlockSpec((n_p, rows, width), lambda s: (p0 // n_p, bidx, 0)),
        out_shape=jax.ShapeDtypeStruct(g.shape, F32), input_output_aliases={2: 0},
        compiler_params=_params(("arbitrary",)),
    )(a, b, g)
    return {**gslabs, slab: out}


def _vec(g):
    return g.reshape(1, -1)


def _norm_mm(x, g, ws, *, split, out_dtype, name, tm=512):
    S, D = x.shape
    k, n = ws[0][1][1], ws[0][1][2]
    N = n if split == "k" else N_CHIPS * n

    def body(r, f, o, acc, s):
        xn = _rms_tile(r[0][...], f[0][...]).astype(BF16)
        o[0][...] = xn
        if split == "k":
            o[1][...] = _sum_k(xn, f[1:], k).astype(out_dtype)
        else:
            for p in range(N_CHIPS):
                o[1][:, _cols(p, n)] = _dot(xn, f[1 + p][...]).astype(out_dtype)

    (xn, y), _ = _rows_call(name, _tile(S, tm), [x], [_vec(g)] + ws, [(D, BF16), (N, out_dtype)], [], body)
    return xn, y


def _mm_k(a, ws, *, add=None, out_dtype=F32, name, tm=512):
    S = a.shape[-2]
    k, n = ws[0][1][1], ws[0][1][2]
    has_add = add is not None

    def body(r, f, o, acc, s):
        if a.ndim == 3:
            y = None
            for p in range(N_CHIPS):
                t = _dot(r[0][p].astype(BF16), f[p][...])
                y = t if y is None else y + t
        else:
            y = _sum_k(r[0][...].astype(BF16), f, k)
        if has_add:
            y = y + r[1][...]
        o[0][...] = y.astype(out_dtype)

    (y,), _ = _rows_call(name, _tile(S, tm), [a] + ([add] if has_add else []), ws, [(n, out_dtype)], [], body)
    return y


def _mm_k_t(terms, *, out_dtype=F32, name, tm=512):
    S = terms[0][0].shape[0]
    k = terms[0][1][0][1][1]

    def body(r, f, o, acc, s):
        y = None
        for t in range(len(terms)):
            yt = _cat_nt(r[t][...].astype(BF16), f[N_CHIPS * t:N_CHIPS * (t + 1)])
            y = yt if y is None else y + yt
        o[0][...] = y.astype(out_dtype)

    (y,), _ = _rows_call(name, _tile(S, tm), [a for a, _ in terms], [w for _, ws in terms for w in ws],
                         [(N_CHIPS * k, out_dtype)], [], body)
    return y


def _rms_fwd(x, g, *, name):
    def body(r, f, o, acc, s):
        o[0][...] = _rms_tile(r[0][...], f[0][...]).astype(BF16)

    (y,), _ = _rows_call(name, _tile(x.shape[0], 256, 8), [x], [_vec(g)], [(x.shape[1], BF16)], [], body)
    return y


def _rms_dg(x, g, dy, *, name):
    def body(r, f, o, acc, s):
        acc[0][...] += _rms_bwd_tile(r[0][...], f[0][...], r[1][...])[1]

    _, (dg,) = _rows_call(name, _tile(x.shape[0], 256, 8), [x, dy], [_vec(g)], [], [((1, x.shape[1]), F32)], body)
    return dg


def _ffn_up(x, g, wg, wu, *, name, tm=512):
    S, D = x.shape
    h = wg[0][1][1]

    def body(r, f, o, acc, s):
        xn = _rms_tile(r[0][...], f[0][...]).astype(BF16)
        o[0][...] = xn
        for p in range(N_CHIPS):
            gate = _dot_nt(xn, f[1 + p][...])
            up = _dot_nt(xn, f[1 + N_CHIPS + p][...])
            o[1][p] = gate.astype(BF16)
            o[2][p] = up.astype(BF16)
            o[3][p] = (gate * _sigmoid(gate) * up).astype(BF16)

    (xn, gate, up, hid), _ = _rows_call(name, _tile(S, tm), [x], [_vec(g)] + wg + wu,
                                        [(D, BF16), (N_CHIPS, h, BF16), (N_CHIPS, h, BF16), (N_CHIPS, h, BF16)], [],
                                        body)
    return xn, gate, up, hid


def _ffn_bwd_hidden(dy, wd, gate, up, token=None, *, name, tm=512):
    S = dy.shape[0]
    h = wd[0][1][1]

    def body(r, f, o, acc, s):
        dyv = r[0][...]
        if token is not None:
            dyv = dyv + jnp.sum(f[N_CHIPS][...])
        dyb = dyv.astype(BF16)
        for p in range(N_CHIPS):
            dh = _dot_nt(dyb, f[p][...])
            gv = r[1][p].astype(F32)
            sg = _sigmoid(gv)
            o[0][p] = (dh * r[2][p].astype(F32) * (sg * (1.0 + gv * (1.0 - sg)))).astype(BF16)
            o[1][p] = (dh * gv * sg).astype(BF16)

    (dg, du), _ = _rows_call(name, _tile(S, tm), [dy, gate, up], wd + ([] if token is None else [token]),
                             [(N_CHIPS, h, BF16), (N_CHIPS, h, BF16)], [], body)
    return dg, du


def _ffn_in_bwd(dg, du, wg, wu, x, g, dres, *, name, tm=512):
    S, D = x.shape

    def body(r, f, o, acc, s):
        tot = None
        for p in range(N_CHIPS):
            y = _dot(r[0][p], f[1 + p][...]) + _dot(r[1][p], f[1 + N_CHIPS + p][...])
            tot = y if tot is None else tot + y
        dx, dgn = _rms_bwd_tile(r[2][...], f[0][...], tot)
        o[0][...] = dx + r[3][...]
        acc[0][...] += dgn

    (dx,), (dgn,) = _rows_call(name, _tile(S, tm), [dg, du, x, dres], [_vec(g)] + wg + wu, [(D, F32)],
                               [((1, D), F32)], body)
    return dx, dgn


def _norm_bwd_k(da, ws, x, g, dres, *, name, tm=512):
    S, D = x.shape

    def body(r, f, o, acc, s):
        dx, dg = _rms_bwd_tile(r[1][...], f[0][...], _cat_nt(r[0][...].astype(BF16), f[1:]))
        o[0][...] = dx + r[2][...]
        acc[0][...] += dg

    (dx,), (dg,) = _rows_call(name, _tile(S, tm), [da, x, dres], [_vec(g)] + ws, [(D, F32)], [((1, D), F32)], body)
    return dx, dg


def _norm_bwd_n(das, ws, x, g, dres, *, name, tm=256):
    S, D = x.shape
    n = ws[0][1][2]

    def body(r, f, o, acc, s):
        tot = None
        for p in range(N_CHIPS):
            y = _dot_nt(r[p // 2][:, _cols(p % 2, n)], f[1 + p][...])
            tot = y if tot is None else tot + y
        dx, dg = _rms_bwd_tile(r[2][...], f[0][...], tot)
        o[0][...] = dx + r[3][...]
        acc[0][...] += dg

    (dx,), (dg,) = _rows_call(name, _tile(S, tm), list(das) + [x, dres], [_vec(g)] + ws, [(D, F32)], [((1, D), F32)],
                              body)
    return dx, dg


def _ln_stats(v):
    mu = _mean(v)
    xc = v - mu
    rstd = lax.rsqrt(_mean(xc * xc) + EPS)
    return xc * rstd, rstd


_SHIFTS = 8
_CONV_ROWS = 64


def _fill_shifts(sh_ref, ext_ref, tm):
    sh_ref[0] = ext_ref[...]
    for s in range(1, _SHIFTS):
        sh_ref[s, 0:tm + CONV_HALO - _SHIFTS, :] = ext_ref[pl.ds(s, tm + CONV_HALO - _SHIFTS), :]


def _window(sh_ref, off, tm):
    return sh_ref[off % _SHIFTS, pl.ds(off - off % _SHIFTS, tm), :]


def _even_fwd(proj, wm, bcol, cw, cb, lg, lb, *, name):
    S = proj.shape[0]
    tm = _tile(S, 256)
    hb = tm // CONV_HALO
    nblk = tm // GMLP_BLOCK

    def body(p_ref, halo_ref, wm_ref, b_ref, cw_ref, cb_ref, lg_ref, lb_ref, mix_ref, hc_ref, hext_ref, hsh_ref):
        i = pl.program_id(0)
        gu, _ = _gelu(p_ref[:, 0:A_WIDTH])
        gv, _ = _gelu(p_ref[:, A_WIDTH:2 * A_WIDTH])
        vn, _ = _ln_stats(gv)
        vnb = vn.astype(BF16)
        for n in range(nblk):
            rows = slice(n * GMLP_BLOCK, (n + 1) * GMLP_BLOCK)
            for g in range(A_GROUPS):
                cols = slice(g * GMLP_BLOCK, (g + 1) * GMLP_BLOCK)
                sg = jnp.dot(wm_ref[g], vnb[rows, cols], preferred_element_type=F32) + b_ref[g]
                mix_ref[rows, cols] = (gu[rows, cols] * sg).astype(BF16)
        h = p_ref[:, 1024:1536] * _sigmoid(p_ref[:, 1536:2048])
        hh = halo_ref[:, 0:B_WIDTH] * _sigmoid(halo_ref[:, B_WIDTH:2 * B_WIDTH])
        hext_ref[0:CONV_HALO, :] = jnp.where(i > 0, hh, 0.0)
        hext_ref[CONV_HALO:CONV_HALO + tm, :] = h
        _fill_shifts(hsh_ref, hext_ref, tm)
        for r0 in range(0, tm, _CONV_ROWS):
            acc = jnp.zeros((_CONV_ROWS, B_WIDTH), F32)
            for k in range(CONV_WIDTH):
                acc = acc + cw_ref[k:k + 1, :] * _window(hsh_ref, r0 + k + CONV_HALO - CONV_WIDTH + 1, _CONV_ROWS)
            hc_ref[r0:r0 + _CONV_ROWS, :] = acc + cb_ref[...]
        hc = hc_ref[...]
        hhat, _ = _ln_stats(hc)
        hl = hhat * lg_ref[...] + lb_ref[...]
        mix_ref[:, A_WIDTH:A_WIDTH + B_WIDTH] = (hl * _sigmoid(hl)).astype(BF16)

    vec = pl.BlockSpec((1, B_WIDTH), lambda i: (0, 0))
    return pl.pallas_call(
        body, name=name, grid=(S // tm,),
        in_specs=[
            pl.BlockSpec((tm, 2048), lambda i: (i, 0)),
            pl.BlockSpec((CONV_HALO, 1024), lambda i: (jnp.maximum(i * hb - 1, 0), 1)),
            pl.BlockSpec((A_GROUPS, GMLP_BLOCK, GMLP_BLOCK), lambda i: (0, 0, 0)),
            pl.BlockSpec((A_GROUPS, GMLP_BLOCK, 1), lambda i: (0, 0, 0)),
            pl.BlockSpec((CONV_HALO, B_WIDTH), lambda i: (0, 0)),
            vec, vec, vec,
        ],
        out_specs=[pl.BlockSpec((tm, 1024), lambda i: (i, 0)), pl.BlockSpec((tm, B_WIDTH), lambda i: (i, 0))],
        out_shape=[jax.ShapeDtypeStruct((S, 1024), BF16), jax.ShapeDtypeStruct((S, B_WIDTH), F32)],
        scratch_shapes=[pltpu.VMEM((tm + CONV_HALO, B_WIDTH), F32),
                        pltpu.VMEM((_SHIFTS, tm + CONV_HALO, B_WIDTH), F32)],
        compiler_params=_params(("parallel",)),
    )(proj, proj, wm, bcol, cw, cb, lg, lb)


def _even_bwd1(proj, dmix, hc, wm, wmt, bcol, lg, lb, *, name):
    S = proj.shape[0]
    tm = _tile(S, 256)
    nblk = tm // GMLP_BLOCK

    def body(p_ref, dm_ref, hc_ref, wm_ref, wmt_ref, b_ref, lg_ref, lb_ref,
             dpa_ref, dhc_ref, dwm_ref, db_ref, dlg_ref, dlb_ref, dcb_ref, dgu_ref, dvn_ref):
        @pl.when(pl.program_id(0) == 0)
        def _():
            dwm_ref[...] = jnp.zeros_like(dwm_ref)
            db_ref[...] = jnp.zeros_like(db_ref)
            dlg_ref[...] = jnp.zeros_like(dlg_ref)
            dlb_ref[...] = jnp.zeros_like(dlb_ref)
            dcb_ref[...] = jnp.zeros_like(dcb_ref)

        au = p_ref[:, 0:A_WIDTH]
        av = p_ref[:, A_WIDTH:2 * A_WIDTH]
        gu, tu = _gelu(au)
        gv, tv = _gelu(av)
        vn, rstd = _ln_stats(gv)
        vnb = vn.astype(BF16)
        for n in range(nblk):
            rows = slice(n * GMLP_BLOCK, (n + 1) * GMLP_BLOCK)
            for g in range(A_GROUPS):
                cols = slice(g * GMLP_BLOCK, (g + 1) * GMLP_BLOCK)
                vb = vnb[rows, cols]
                sg = jnp.dot(wm_ref[g], vb, preferred_element_type=F32) + b_ref[g]
                da = dm_ref[rows, cols]
                dsg = da * gu[rows, cols]
                dgu_ref[rows, cols] = da * sg
                dsgb = dsg.astype(BF16)
                dwm_ref[g] += _dot_nt(dsgb, vb)
                db_ref[g] += jnp.sum(dsg, axis=1, keepdims=True)
                dvn_ref[rows, cols] = jnp.dot(wmt_ref[g], dsgb, preferred_element_type=F32)
        dvn = dvn_ref[...]
        dgv = rstd * (dvn - _mean(dvn) - vn * _mean(dvn * vn))
        dpa_ref[:, 0:A_WIDTH] = (dgu_ref[...] * _gelu_grad(au, tu)).astype(BF16)
        dpa_ref[:, A_WIDTH:2 * A_WIDTH] = (dgv * _gelu_grad(av, tv)).astype(BF16)
        hhat, rstd2 = _ln_stats(hc_ref[...])
        lgv = lg_ref[...]
        hl = hhat * lgv + lb_ref[...]
        s = _sigmoid(hl)
        dhl = dm_ref[:, A_WIDTH:A_WIDTH + B_WIDTH] * (s * (1.0 + hl * (1.0 - s)))
        dlg_ref[...] += jnp.sum(dhl * hhat, axis=0, keepdims=True)
        dlb_ref[...] += jnp.sum(dhl, axis=0, keepdims=True)
        dhh = dhl * lgv
        dhc = rstd2 * (dhh - _mean(dhh) - hhat * _mean(dhh * hhat))
        dcb_ref[...] += jnp.sum(dhc, axis=0, keepdims=True)
        dhc_ref[...] = dhc

    vec = pl.BlockSpec((1, B_WIDTH), lambda i: (0, 0))
    w3 = pl.BlockSpec((A_GROUPS, GMLP_BLOCK, GMLP_BLOCK), lambda i: (0, 0, 0))
    b3 = pl.BlockSpec((A_GROUPS, GMLP_BLOCK, 1), lambda i: (0, 0, 0))
    return pl.pallas_call(
        body, name=name, grid=(S // tm,),
        in_specs=[
            pl.BlockSpec((tm, 1024), lambda i: (i, 0)),
            pl.BlockSpec((tm, 1024), lambda i: (i, 0)),
            pl.BlockSpec((tm, B_WIDTH), lambda i: (i, 0)),
            w3, w3, b3, vec, vec,
        ],
        out_specs=[pl.BlockSpec((tm, 1024), lambda i: (i, 0)), pl.BlockSpec((tm, B_WIDTH), lambda i: (i, 0)),
                   w3, b3, vec, vec, vec],
        out_shape=[
            jax.ShapeDtypeStruct((S, 1024), BF16), jax.ShapeDtypeStruct((S, B_WIDTH), F32),
            jax.ShapeDtypeStruct((A_GROUPS, GMLP_BLOCK, GMLP_BLOCK), F32),
            jax.ShapeDtypeStruct((A_GROUPS, GMLP_BLOCK, 1), F32),
            jax.ShapeDtypeStruct((1, B_WIDTH), F32), jax.ShapeDtypeStruct((1, B_WIDTH), F32),
            jax.ShapeDtypeStruct((1, B_WIDTH), F32),
        ],
        scratch_shapes=[pltpu.VMEM((tm, A_WIDTH), F32), pltpu.VMEM((tm, A_WIDTH), F32)],
        compiler_params=_params(("arbitrary",)),
    )(proj, dmix, hc, wm, wmt, bcol, lg, lb)


def _even_bwd2(proj, dhc, cw, *, name):
    S = proj.shape[0]
    tm = _tile(S, 256)
    hb = tm // CONV_HALO
    nt = S // tm
    last_halo = S // CONV_HALO - 1
    lo = CONV_HALO - CONV_WIDTH + 1

    def body(p_ref, halo_ref, d_ref, dnext_ref, cw_ref, dpb_ref, dcw_ref, hext_ref, dext_ref, hsh_ref, dsh_ref):
        i = pl.program_id(0)

        @pl.when(i == 0)
        def _():
            dcw_ref[...] = jnp.zeros_like(dcw_ref)

        hh = halo_ref[:, 0:B_WIDTH] * _sigmoid(halo_ref[:, B_WIDTH:2 * B_WIDTH])
        hext_ref[0:CONV_HALO, :] = jnp.where(i > 0, hh, 0.0)
        hext_ref[CONV_HALO:CONV_HALO + tm, :] = p_ref[:, 0:B_WIDTH] * _sigmoid(p_ref[:, B_WIDTH:2 * B_WIDTH])
        dext_ref[0:tm, :] = d_ref[...]
        dext_ref[tm:tm + CONV_HALO, :] = jnp.where(i < nt - 1, dnext_ref[...], 0.0)
        _fill_shifts(hsh_ref, hext_ref, tm)
        _fill_shifts(dsh_ref, dext_ref, tm)
        for r0 in range(0, tm, _CONV_ROWS):
            rows = slice(r0, r0 + _CONV_ROWS)
            dhc_b = d_ref[rows, :]
            dh = jnp.zeros((_CONV_ROWS, B_WIDTH), F32)
            for k in range(CONV_WIDTH):
                dh = dh + cw_ref[k:k + 1, :] * _window(dsh_ref, r0 + CONV_WIDTH - 1 - k, _CONV_ROWS)
                dcw_ref[k:k + 1, :] += jnp.sum(dhc_b * _window(hsh_ref, r0 + k + lo, _CONV_ROWS), axis=0,
                                               keepdims=True)
            ba_b = p_ref[rows, 0:B_WIDTH]
            sg_b = _sigmoid(p_ref[rows, B_WIDTH:2 * B_WIDTH])
            dpb_ref[rows, 0:B_WIDTH] = (dh * sg_b).astype(BF16)
            dpb_ref[rows, B_WIDTH:2 * B_WIDTH] = (dh * ba_b * sg_b * (1.0 - sg_b)).astype(BF16)

    return pl.pallas_call(
        body, name=name, grid=(nt,),
        in_specs=[
            pl.BlockSpec((tm, 1024), lambda i: (i, 1)),
            pl.BlockSpec((CONV_HALO, 1024), lambda i: (jnp.maximum(i * hb - 1, 0), 1)),
            pl.BlockSpec((tm, B_WIDTH), lambda i: (i, 0)),
            pl.BlockSpec((CONV_HALO, B_WIDTH), lambda i: (jnp.minimum((i + 1) * hb, last_halo), 0)),
            pl.BlockSpec((CONV_HALO, B_WIDTH), lambda i: (0, 0)),
        ],
        out_specs=[pl.BlockSpec((tm, 1024), lambda i: (i, 0)), pl.BlockSpec((CONV_HALO, B_WIDTH), lambda i: (0, 0))],
        out_shape=[jax.ShapeDtypeStruct((S, 1024), BF16), jax.ShapeDtypeStruct((CONV_HALO, B_WIDTH), F32)],
        scratch_shapes=[pltpu.VMEM((tm + CONV_HALO, B_WIDTH), F32), pltpu.VMEM((tm + CONV_HALO, B_WIDTH), F32),
                        pltpu.VMEM((_SHIFTS, tm + CONV_HALO, B_WIDTH), F32),
                        pltpu.VMEM((_SHIFTS, tm + CONV_HALO, B_WIDTH), F32)],
        compiler_params=_params(("arbitrary",)),
    )(proj, proj, dhc, dhc, cw)


_CA_SCALE = CA_HEAD_DIM ** -0.5


def _softmax_rows(s):
    e = jnp.exp(s - jnp.max(s, axis=-1, keepdims=True))
    return e / jnp.sum(e, axis=-1, keepdims=True)


def _attn_fwd(q, k, v, *, name):
    S = q.shape[0]

    def body(r, f, o, acc, s):
        for h in range(CA_HEADS):
            cols = _cols(h, CA_HEAD_DIM)
            p = _softmax_rows(_dot_nt(r[0][:, cols], f[0][:, cols]) * _CA_SCALE)
            o[0][:, cols] = _dot(p.astype(BF16), f[1][:, cols]).astype(BF16)

    (o_,), _ = _rows_call(name, _tile(S, 512), [q], [k, v], [(D_MODEL, BF16)], [], body)
    return o_


def _attn_bwd(dy, wo, q, k, v, *, name):
    S = q.shape[0]
    M = k.shape[0]

    def body(r, f, o, acc, s):
        dyb = r[0][...].astype(BF16)
        for h in range(CA_HEADS):
            cols = _cols(h, CA_HEAD_DIM)
            qh = r[1][:, cols]
            kh = f[0][:, cols]
            vh = f[1][:, cols]
            doh = _dot_nt(dyb, f[2 + h][...]).astype(BF16)
            p = _softmax_rows(_dot_nt(qh, kh) * _CA_SCALE)
            acc[1][:, cols] += _dot_tn(p.astype(BF16), doh)
            dp = _dot_nt(doh, vh)
            ds = (p * (dp - jnp.sum(dp * p, axis=-1, keepdims=True)) * _CA_SCALE).astype(BF16)
            o[0][:, cols] = _dot(ds, kh).astype(BF16)
            acc[0][:, cols] += _dot_tn(ds, qh)

    (dq,), (dk, dv) = _rows_call(name, _tile(S, 512), [dy, q], [k, v] + wo, [(D_MODEL, BF16)],
                                 [((M, D_MODEL), F32), ((M, D_MODEL), F32)], body)
    return dq, dk, dv


_STATE_TILE = 2 * STATE_ROWS
N_SETS = 4
SET_CH = C_WIDTH // N_SETS
SET_COLS = N_STATE // N_SETS // STATE_LANES


def _set_groups(j):
    return [SET_COLS * j + c for c in range(SET_COLS)] + [STATE_ROWS + SET_COLS * j + c for c in range(SET_COLS)]


def _pack_state(re, im):
    hi = lax.bitcast_convert_type(re.astype(BF16).astype(F32), jnp.uint32)
    lo = lax.bitcast_convert_type(im.astype(BF16).astype(F32), jnp.uint32) >> 16
    return hi | lo


def _unpack_state(word):
    re = lax.bitcast_convert_type(word & jnp.uint32(0xFFFF0000), F32)
    im = lax.bitcast_convert_type(word << 16, F32)
    return re, im


def _state_set(ref, tm, j):
    parts = [_unpack_state(ref[:, SET_COLS * j + c, :, :].reshape(tm, STATE_LANES)) for c in range(SET_COLS)]
    return jnp.concatenate([p[0].astype(BF16) for p in parts] + [p[1].astype(BF16) for p in parts], axis=1)


def _s5_readout(xs, cset, u, d, *, name, tm=256):
    tm = _tile(u.shape[0], tm)

    def body(r, f, o, acc, s):
        y0 = jnp.concatenate([_dot(_state_set(r[0], tm, j), f[0][j]) for j in range(N_SETS)], axis=1)
        y = y0 + f[1][...] * r[1][...]
        o[0][...] = y
        o[1][...] = _gelu(y)[0].astype(BF16)

    (y, yg), _ = _rows_call(name, tm, [xs, u], [cset, d], [(C_WIDTH, F32), (C_WIDTH, BF16)], [], body)
    return y, yg


def _state_grad_sets(a, st, *, name, ts=256):
    ts = _tile(a.shape[0], ts)

    def body(r, f, o, acc, s):
        for j in range(N_SETS):
            acc[0][j] += _dot_tn(r[0][:, _cols(j, SET_CH)].astype(BF16), _state_set(r[1], ts, j))

    _, (out,) = _rows_call(name, ts, [a, st], [], [], [((N_SETS, SET_CH, 2 * N_STATE // N_SETS), F32)], body)
    return out


def _glu_out(yg, ws, x, *, name, tm=512):
    n = ws[0][1][2]

    def body(r, f, o, acc, s):
        ygv = r[0][...]
        ov = [_dot(ygv, f[p][...]) for p in range(N_CHIPS)]
        for p in range(N_CHIPS):
            o[0][:, _cols(p, n)] = ov[p].astype(BF16)
        for p in range(2):
            o[1][:, _cols(p, n)] = r[1][:, _cols(p, n)] + ov[p] * _sigmoid(ov[2 + p])

    (o_, y), _ = _rows_call(name, _tile(x.shape[0], tm), [yg, x], ws, [(2 * D_MODEL, BF16), (D_MODEL, F32)], [], body)
    return o_, y


def _glu_out_bwd(o_, dy, ws, y, u, d, *, name, tm=256):
    n = ws[0][1][2]

    def body(r, f, o, acc, s):
        o1 = r[0][:, 0:D_MODEL].astype(F32)
        sg = _sigmoid(r[0][:, D_MODEL:2 * D_MODEL].astype(F32))
        dyv = r[1][...]
        do1 = (dyv * sg).astype(BF16)
        do2 = (dyv * o1 * sg * (1.0 - sg)).astype(BF16)
        o[0][:, 0:D_MODEL] = do1
        o[0][:, D_MODEL:2 * D_MODEL] = do2
        dyg = None
        for p in range(N_CHIPS):
            t = _dot_nt((do1 if p < 2 else do2)[:, _cols(p % 2, n)], f[1 + p][...])
            dyg = t if dyg is None else dyg + t
        yv = r[2][...]
        dys = dyg * _gelu_grad(yv, _gelu(yv)[1])
        o[1][...] = dys.astype(BF16)
        o[2][...] = f[0][...] * dys
        acc[0][...] += jnp.sum(dys * r[3][...], axis=0, keepdims=True)

    (do, dys, dus), (dd,) = _rows_call(name, _tile(dy.shape[0], tm), [o_, dy, y, u], [d] + ws,
                                       [(2 * D_MODEL, BF16), (C_WIDTH, BF16), (C_WIDTH, F32)], [((1, C_WIDTH), F32)],
                                       body)
    return do, dys, dus, dd


def _s5_in_bwd(gs, bset, dus, ws, x, g, dres, *, name, tm=256):
    D = x.shape[1]
    tm = _tile(x.shape[0], tm)

    def body(r, f, o, acc, s):
        du0 = jnp.concatenate([_dot_nt(_state_set(r[0], tm, j), f[1][j]) for j in range(N_SETS)], axis=1)
        du = (du0 + r[1][...]).astype(BF16)
        o[0][...] = du
        dx, dg = _rms_bwd_tile(r[2][...], f[0][...], _cat_nt(du, f[2:]))
        o[1][...] = dx + r[3][...]
        acc[0][...] += dg

    (du, dx), (dg,) = _rows_call(name, tm, [gs, dus, x, dres], [_vec(g), bset] + ws,
                                 [(C_WIDTH, BF16), (D, F32)], [((1, D), F32)], body)
    return du, dx, dg


_SCAN_CHUNK = 256
_RE = slice(0, STATE_ROWS)
_IM = slice(STATE_ROWS, 2 * STATE_ROWS)
assert SCAN_BLOCK == 8


def _token(g, i, rows):
    return pl.ds(pl.multiple_of(g * (rows * SCAN_BLOCK), rows * SCAN_BLOCK) + i, rows, stride=SCAN_BLOCK)


def _fill_chunk(s3, a_ref, wset, tc, nt):
    for j in range(N_SETS):
        av = a_ref[:, _cols(j, SET_CH)].astype(BF16)
        y = _dot_nt(av, wset[j]) if nt else _dot(av, wset[j])
        for k, c in enumerate(_set_groups(j)):
            s3[:, 8 * c:8 * (c + 1), :] = y[:, _cols(k, STATE_LANES)].reshape(tc // 8, 8, STATE_LANES)


def _chunk_token(s3, g, i):
    return s3[g, pl.ds(i, _STATE_TILE, stride=SCAN_BLOCK), :]


def _scan_fwd(u, bset, pw, *, name):
    S = u.shape[0]
    tc = _tile(S, _SCAN_CHUNK, 8)

    def body(u_ref, bset_ref, pw_ref, xs_ref, st_ref, s3):
        @pl.when(pl.program_id(0) == 0)
        def _():
            st_ref[...] = jnp.zeros_like(st_ref)

        _fill_chunk(s3, u_ref, bset_ref, tc, nt=False)
        ar = pw_ref[0, _RE, :]
        ai = pw_ref[0, _IM, :]

        def block(g, carry):
            xr, xi = carry
            cr = ci = nr = ni = None
            for j in range(SCAN_BLOCK):
                b = _chunk_token(s3, g, j)
                br, bi = b[_RE], b[_IM]
                cr, ci = (br, bi) if j == 0 else (ar * cr - ai * ci + br, ar * ci + ai * cr + bi)
                pr, pi = pw_ref[j, _RE, :], pw_ref[j, _IM, :]
                nr = pr * xr - pi * xi + cr
                ni = pr * xi + pi * xr + ci
                xs_ref[_token(g, j, STATE_ROWS), :] = _pack_state(nr, ni)
            return nr, ni

        xr, xi = lax.fori_loop(0, tc // SCAN_BLOCK, block, (st_ref[_RE, :], st_ref[_IM, :]), unroll=4)
        st_ref[_RE, :] = xr
        st_ref[_IM, :] = xi

    return pl.pallas_call(
        body, name=name, grid=(S // tc,),
        in_specs=[pl.BlockSpec((tc, u.shape[1]), lambda i: (i, 0)), pl.BlockSpec(bset.shape, lambda i: (0, 0, 0)),
                  pl.BlockSpec(pw.shape, lambda i: (0, 0, 0))],
        out_specs=pl.BlockSpec((tc * STATE_ROWS, STATE_LANES), lambda i: (i, 0)),
        out_shape=jax.ShapeDtypeStruct((S * STATE_ROWS, STATE_LANES), jnp.uint32),
        scratch_shapes=[pltpu.VMEM((2 * STATE_ROWS, STATE_LANES), F32),
                        pltpu.VMEM((tc // 8, _STATE_TILE * 8, STATE_LANES), F32)],
        compiler_params=_params(("arbitrary",)),
    )(u, bset, pw)


def _scan_bwd(dys, cset, xs, pw, *, name):
    S = dys.shape[0]
    tc = _tile(S, _SCAN_CHUNK, 8)
    nc = S // tc

    def body(dys_ref, cset_ref, xs_ref, pw_ref, g_ref, da_ref, st_ref, s3):
        @pl.when(pl.program_id(0) == 0)
        def _():
            st_ref[...] = jnp.zeros_like(st_ref)
            da_ref[...] = jnp.zeros_like(da_ref)

        _fill_chunk(s3, dys_ref, cset_ref, tc, nt=True)
        ar = pw_ref[0, _RE, :]
        ai = pw_ref[0, _IM, :]

        def block(k, carry):
            gr, gi, dar, dai = carry
            g = tc // SCAN_BLOCK - 1 - k
            cr = ci = None
            pgr, pgi = gr, gi
            for j in range(SCAN_BLOCK):
                i = SCAN_BLOCK - 1 - j
                xr, xi = _unpack_state(xs_ref[_token(g, i, STATE_ROWS), :])
                dar = dar + pgr * xr + pgi * xi
                dai = dai + pgi * xr - pgr * xi
                d = _chunk_token(s3, g, i)
                dr, di = d[_RE], d[_IM]
                cr, ci = (dr, di) if j == 0 else (ar * cr + ai * ci + dr, ar * ci - ai * cr + di)
                pr, pi = pw_ref[j, _RE, :], pw_ref[j, _IM, :]
                pgr = pr * gr + pi * gi + cr
                pgi = pr * gi - pi * gr + ci
                g_ref[_token(g, i, STATE_ROWS), :] = _pack_state(pgr, pgi)
            return pgr, pgi, dar, dai

        init = (st_ref[_RE, :], st_ref[_IM, :], da_ref[_RE, :], da_ref[_IM, :])
        gr, gi, dar, dai = lax.fori_loop(0, tc // SCAN_BLOCK, block, init, unroll=4)
        st_ref[_RE, :] = gr
        st_ref[_IM, :] = gi
        da_ref[_RE, :] = dar
        da_ref[_IM, :] = dai

    packed = pl.BlockSpec((tc * STATE_ROWS, STATE_LANES), lambda i: (nc - 1 - i, 0))
    vec = pl.BlockSpec((2 * STATE_ROWS, STATE_LANES), lambda i: (0, 0))
    return pl.pallas_call(
        body, name=name, grid=(nc,),
        in_specs=[pl.BlockSpec((tc, dys.shape[1]), lambda i: (nc - 1 - i, 0)),
                  pl.BlockSpec(cset.shape, lambda i: (0, 0, 0)), packed, pl.BlockSpec(pw.shape, lambda i: (0, 0, 0))],
        out_specs=[packed, vec],
        out_shape=[jax.ShapeDtypeStruct(xs.shape, jnp.uint32), jax.ShapeDtypeStruct((2 * STATE_ROWS, STATE_LANES), F32)],
        scratch_shapes=[pltpu.VMEM((2 * STATE_ROWS, STATE_LANES), F32),
                        pltpu.VMEM((tc // 8, _STATE_TILE * 8, STATE_LANES), F32)],
        compiler_params=_params(("arbitrary",)),
    )(dys, cset, xs, pw)


def _down_loss_head(h, ws, x, g, target, *, name, tm=512):
    S, D = x.shape

    def body(r, f, o, acc, s):
        xv = r[1][...]
        for p in range(N_CHIPS):
            xv = xv + _dot(r[0][p], f[1 + p][...])
        gv = f[0][...]
        rs = lax.rsqrt(_mean(xv * xv) + EPS)
        xh = xv * rs
        err = xh * gv - r[2][...]
        acc[1][...] += 0.5 * jnp.sum(_mean(err * err), axis=0, keepdims=True)
        dy = err * (1.0 / D)
        dyg = dy * gv
        o[0][...] = rs * (dyg - xh * _mean(dyg * xh))
        acc[0][...] += jnp.sum(dy * xh, axis=0, keepdims=True)

    (dx,), (dg, loss) = _rows_call(name, _tile(S, tm), [h, x, target], [_vec(g)] + ws, [(D, F32)],
                                   [((1, D), F32), ((1, 128), F32)], body)
    return dx, dg, loss


_ADAM_C1 = 1.0 - ADAM_B1 ** ADAM_STEP
_ADAM_C2 = 1.0 - ADAM_B2 ** ADAM_STEP
_ONE_BLOCK_BYTES = 8 * 1024 * 1024


def _adamw_math(w, g, m, v):
    nm = ADAM_B1 * m + (1.0 - ADAM_B1) * g
    nv = ADAM_B2 * v + (1.0 - ADAM_B2) * (g * g)
    m_hat = nm / _ADAM_C1
    v_hat = nv / _ADAM_C2
    return -ADAM_LR * (m_hat / (jnp.sqrt(v_hat) + ADAM_EPS) + ADAM_WD * w), nm, nv


def _adamw_shard(w, gsrc, m, v, *, name):
    R, C = w.shape
    n_l = len(gsrc)
    rows = R // n_l
    tr = rows
    for _, r0 in gsrc:
        tr = math.gcd(tr, r0) if r0 else tr
    tr = _tile(tr, 256, 8) if tr > 256 else tr
    nb = rows // tr
    assert rows % tr == 0 and all(r0 % tr == 0 for _, r0 in gsrc)

    def body(*refs):
        w_ref, g_refs, (m_ref, v_ref, go_ref, d_ref, nm_ref, nv_ref) = refs[0], refs[1:1 + n_l], refs[1 + n_l:]
        layer = pl.program_id(0) // nb
        gv = g_refs[0][...]
        for l in range(1, n_l):
            gv = jnp.where(layer == l, g_refs[l][...], gv)
        go_ref[...] = gv
        d_ref[...], nm_ref[...], nv_ref[...] = _adamw_math(w_ref[...], gv, m_ref[...], v_ref[...])

    def g_spec(l, r0):
        return pl.BlockSpec((tr, C), lambda i: (r0 // tr + jnp.clip(i - l * nb, 0, nb - 1), 0))

    blk = pl.BlockSpec((tr, C), lambda i: (i, 0))
    out = jax.ShapeDtypeStruct((R, C), F32)
    return pl.pallas_call(
        body, name=name, grid=(R // tr,),
        in_specs=[blk] + [g_spec(l, r0) for l, (_, r0) in enumerate(gsrc)] + [blk, blk], out_specs=[blk] * 4,
        out_shape=[out] * 4, compiler_params=_params(("parallel",)),
    )(w, *[g for g, _ in gsrc], m, v)


def _adamw_small(ws, gs, ms, vs, *, name):
    n = len(ws)

    def body(*refs):
        w_r, g_r, m_r, v_r = refs[:n], refs[n:2 * n], refs[2 * n:3 * n], refs[3 * n:4 * n]
        d_r, nm_r, nv_r = refs[4 * n:5 * n], refs[5 * n:6 * n], refs[6 * n:7 * n]
        for k in range(n):
            d_r[k][...], nm_r[k][...], nv_r[k][...] = _adamw_math(w_r[k][...], g_r[k][...], m_r[k][...], v_r[k][...])

    vm = pl.BlockSpec(memory_space=pltpu.VMEM)
    out = [jax.ShapeDtypeStruct(w.shape, F32) for w in ws]
    res = pl.pallas_call(body, name=name, in_specs=[vm] * (4 * n), out_specs=[vm] * (3 * n), out_shape=out * 3,
                         compiler_params=pltpu.CompilerParams(vmem_limit_bytes=VMEM_LIMIT))(*ws, *gs, *ms, *vs)
    return res[:n], res[n:2 * n], res[2 * n:]


def _sum_slots(x, *, name):
    n, R, C = x.shape
    tr = R if (n + 1) * R * C * 4 <= _ONE_BLOCK_BYTES else _tile(R, 256, 8)

    def body(x_ref, o_ref):
        acc = x_ref[0]
        for k in range(1, n):
            acc = acc + x_ref[k]
        o_ref[...] = acc

    return pl.pallas_call(
        body, name=name, grid=(R // tr,),
        in_specs=[pl.BlockSpec((n, tr, C), lambda i: (0, i, 0))], out_specs=pl.BlockSpec((tr, C), lambda i: (i, 0)),
        out_shape=jax.ShapeDtypeStruct((R, C), F32), compiler_params=_params(("parallel",)),
    )(x)


def _pair_sum(g, r, where, *, name):
    n, R, C = g.shape
    Rh = R // 2
    tr = _tile(Rh, 256, 8)
    nb = Rh // tr

    def body(where_ref, g_ref, r_ref, o_ref):
        o_ref[...] = (g_ref[...] + r_ref[...]).astype(BF16)

    def slot(p, w):
        return p + jnp.where(p >= w[0], 1, 0)

    return pl.pallas_call(
        body, name=name,
        grid_spec=pltpu.PrefetchScalarGridSpec(
            num_scalar_prefetch=1, grid=(n - 1, nb),
            in_specs=[pl.BlockSpec((1, tr, C), lambda p, i, w: (slot(p, w), w[1] * nb + i, 0)),
                      pl.BlockSpec((1, tr, C), lambda p, i, w: (slot(p, w), i, 0))],
            out_specs=pl.BlockSpec((1, tr, C), lambda p, i, w: (slot(p, w), i, 0)),
        ),
        out_shape=jax.ShapeDtypeStruct((n, Rh, C), BF16), compiler_params=_params(("parallel", "parallel")),
    )(where, g, r)


def _chip_sum(g, r, slots, where, *, name):
    n, R, C = g.shape
    Rh = R // 2
    tr = _tile(Rh, 256, 8)
    nb = Rh // tr

    def body(w_ref, g_ref, r_ref, s_ref, o_ref):
        acc = g_ref[0] + r_ref[0]
        for k in range(slots.shape[0]):
            acc = acc + s_ref[k].astype(F32)
        o_ref[...] = acc

    return pl.pallas_call(
        body, name=name,
        grid_spec=pltpu.PrefetchScalarGridSpec(
            num_scalar_prefetch=1, grid=(nb,),
            in_specs=[pl.BlockSpec((1, tr, C), lambda i, w: (w[0], w[1] * nb + i, 0)),
                      pl.BlockSpec((1, tr, C), lambda i, w: (w[0], i, 0)),
                      pl.BlockSpec((slots.shape[0], tr, C), lambda i, w: (0, i, 0))],
            out_specs=pl.BlockSpec((tr, C), lambda i, w: (w[1] * nb + i, 0)),
        ),
        out_shape=jax.ShapeDtypeStruct((R, C), F32), compiler_params=_params(("parallel",)),
    )(where, g, r, slots)


ANY = pl.BlockSpec(memory_space=pl.ANY)


def _place():
    return lax.axis_index("x"), lax.axis_index("y"), lax.axis_index("c")


def _other_chips(x, y):
    return [(1 - x, y), (x, 1 - y), (1 - x, 1 - y)]


def _aliased_comm_call(body, bufs, n_sems, *, name):
    n = len(bufs)
    return pl.pallas_call(
        body, name=name, out_shape=[jax.ShapeDtypeStruct(b.shape, b.dtype) for b in bufs],
        in_specs=[ANY] * n, out_specs=[ANY] * n, input_output_aliases={k: k for k in range(n)},
        scratch_shapes=[pltpu.SemaphoreType.DMA((n_sems,)), pltpu.SemaphoreType.DMA((n_sems,))],
    )(*bufs)


HBM = pl.BlockSpec(memory_space=pltpu.HBM)
SEM = pl.BlockSpec(memory_space=pltpu.SEMAPHORE)
_SPLIT = pltpu.CompilerParams(has_side_effects=pltpu.SideEffectType.DATAFLOW_SIDE_EFFECTING)


def _in_hbm(arrs):
    return [pltpu.with_memory_space_constraint(a, pltpu.HBM) for a in arrs]


def _gather_ici_start(bufs, after, *, name):
    n = len(bufs)

    def body(*refs):
        send_sems, recv_sems, outs, token = refs[n + 1], refs[n + 2], refs[n + 3:2 * n + 3], refs[2 * n + 3]
        x, y, c = _place()
        for b in range(n):
            rh = bufs[b].shape[1] // 2
            part = outs[b].at[2 * x + y, pl.ds(c * rh, rh), :]
            for j, chip in enumerate(_other_chips(x, y)):
                pltpu.make_async_remote_copy(src_ref=part, dst_ref=part, send_sem=send_sems.at[3 * b + j],
                                             recv_sem=recv_sems.at[3 * b + j], device_id=(*chip, c),
                                             device_id_type=MESH).start()
        token[...] = jnp.zeros_like(token)

    res = pl.pallas_call(
        body, name=name,
        out_shape=(pltpu.SemaphoreType.DMA((3 * n,)), pltpu.SemaphoreType.DMA((3 * n,)),
                   *[pltpu.HBM(b.shape, b.dtype) for b in bufs], jax.ShapeDtypeStruct((8, 128), F32)),
        in_specs=[HBM] * n + [ANY], out_specs=(SEM, SEM, *[HBM] * n, pl.BlockSpec(memory_space=pltpu.VMEM)),
        input_output_aliases={k: k + 2 for k in range(n)}, compiler_params=_SPLIT,
    )(*_in_hbm(bufs), after)
    return res[0], res[1], list(res[2:2 + n]), res[2 + n]


def _gather_ici_wait(send_sems, recv_sems, bufs, first, after, *, name):
    n = len(bufs)

    def body(*refs):
        ins, ss, rs = refs[:n], refs[n], refs[n + 1]
        x, y, c = _place()
        for b in range(n):
            rh = bufs[b].shape[1] // 2
            mine = ins[b].at[2 * x + y, pl.ds(c * rh, rh), :]
            for j, (cx, cy) in enumerate(_other_chips(x, y)):
                theirs = ins[b].at[2 * cx + cy, pl.ds(c * rh, rh), :]
                cp = pltpu.make_async_remote_copy(src_ref=mine, dst_ref=theirs, send_sem=ss.at[3 * (first + b) + j],
                                                  recv_sem=rs.at[3 * (first + b) + j], device_id=(cx, cy, c),
                                                  device_id_type=MESH)
                cp.wait_send()
                cp.wait_recv()

    return list(pl.pallas_call(
        body, name=name, out_shape=[pltpu.HBM(b.shape, b.dtype) for b in bufs],
        in_specs=[HBM] * n + [SEM, SEM, ANY], out_specs=[HBM] * n,
        input_output_aliases={k: k for k in range(n)}, compiler_params=_SPLIT,
    )(*bufs, send_sems, recv_sems, after))


def _gather_forward(bufs, *, name):
    n = len(bufs)

    def body(*refs):
        outs, send_sems, recv_sems = refs[n:2 * n], refs[2 * n], refs[2 * n + 1]
        x, y, c = _place()

        def copy(b, j, chip, hc):
            rh = bufs[b].shape[1] // 2
            part = outs[b].at[2 * chip[0] + chip[1], pl.ds(hc * rh, rh), :]
            return pltpu.make_async_remote_copy(src_ref=part, dst_ref=part, send_sem=send_sems.at[3 * b + j],
                                                recv_sem=recv_sems.at[3 * b + j], device_id=(x, y, 1 - c),
                                                device_id_type=MESH)

        sends = [copy(b, j, chip, c) for b in range(n) for j, chip in enumerate(_other_chips(x, y))]
        for cp in sends:
            cp.start()
        for b in range(n):
            for j, chip in enumerate(_other_chips(x, y)):
                copy(b, j, chip, 1 - c).wait_recv()
        for cp in sends:
            cp.wait_send()

    return _aliased_comm_call(body, bufs, 3 * n, name=name)


def _forward_copy(buf, send_sems, recv_sems, k, chip, half, to):
    rh = buf.shape[1] // 2
    part = buf.at[2 * chip[0] + chip[1], pl.ds(half * rh, rh), :]
    return pltpu.make_async_remote_copy(src_ref=part, dst_ref=part, send_sem=send_sems.at[k], recv_sem=recv_sems.at[k],
                                        device_id=to, device_id_type=MESH)


def _gather_forward_start(bufs, *, name):
    n = len(bufs)

    def body(*refs):
        send_sems, recv_sems, outs, token = refs[n], refs[n + 1], refs[n + 2:2 * n + 2], refs[2 * n + 2]
        x, y, c = _place()
        for b in range(n):
            for j, chip in enumerate(_other_chips(x, y)):
                _forward_copy(outs[b], send_sems, recv_sems, 3 * b + j, chip, c, (x, y, 1 - c)).start()
        token[...] = jnp.zeros_like(token)

    res = pl.pallas_call(
        body, name=name,
        out_shape=(pltpu.SemaphoreType.DMA((3 * n,)), pltpu.SemaphoreType.DMA((3 * n,)),
                   *[pltpu.HBM(b.shape, b.dtype) for b in bufs], jax.ShapeDtypeStruct((8, 128), F32)),
        in_specs=[HBM] * n, out_specs=(SEM, SEM, *[HBM] * n, pl.BlockSpec(memory_space=pltpu.VMEM)),
        input_output_aliases={k: k + 2 for k in range(n)}, compiler_params=_SPLIT,
    )(*_in_hbm(bufs))
    return res[0], res[1], list(res[2:2 + n]), res[2 + n]


def _gather_forward_wait(send_sems, recv_sems, bufs, after, *, name):
    n = len(bufs)

    def body(*refs):
        ins, ss, rs = refs[:n], refs[n], refs[n + 1]
        x, y, c = _place()
        for b in range(n):
            for j, chip in enumerate(_other_chips(x, y)):
                _forward_copy(ins[b], ss, rs, 3 * b + j, chip, c, (x, y, 1 - c)).wait_send()
                _forward_copy(ins[b], ss, rs, 3 * b + j, chip, 1 - c, (x, y, 1 - c)).wait_recv()

    return list(pl.pallas_call(
        body, name=name, out_shape=[pltpu.HBM(b.shape, b.dtype) for b in bufs],
        in_specs=[HBM] * n + [SEM, SEM, ANY], out_specs=[HBM] * n,
        input_output_aliases={k: k for k in range(n)}, compiler_params=_SPLIT,
    )(*bufs, send_sems, recv_sems, after))


def _chip_exchange_start(hs, *, name):
    n = len(hs)
    lands = [lax.empty((3,) + h.shape[1:], h.dtype) for h in hs]

    def body(*refs):
        send_sems, recv_sems = refs[2 * n], refs[2 * n + 1]
        h_out, l_out, token = refs[2 * n + 2:3 * n + 2], refs[3 * n + 2:4 * n + 2], refs[4 * n + 2]
        x, y, c = _place()
        for b in range(n):
            for j, (cx, cy) in enumerate(_other_chips(x, y)):
                pltpu.make_async_remote_copy(src_ref=h_out[b].at[2 * cx + cy], dst_ref=l_out[b].at[j],
                                             send_sem=send_sems.at[3 * b + j], recv_sem=recv_sems.at[3 * b + j],
                                             device_id=(cx, cy, c), device_id_type=MESH).start()
        token[...] = jnp.zeros_like(token)

    res = pl.pallas_call(
        body, name=name,
        out_shape=(pltpu.SemaphoreType.DMA((3 * n,)), pltpu.SemaphoreType.DMA((3 * n,)),
                   *[pltpu.HBM(a.shape, a.dtype) for a in hs + lands], jax.ShapeDtypeStruct((8, 128), F32)),
        in_specs=[HBM] * (2 * n), out_specs=(SEM, SEM, *[HBM] * (2 * n), pl.BlockSpec(memory_space=pltpu.VMEM)),
        input_output_aliases={k: k + 2 for k in range(2 * n)}, compiler_params=_SPLIT,
    )(*_in_hbm(hs + lands))
    return res[0], res[1], list(res[2:2 + n]), list(res[2 + n:2 + 2 * n]), res[2 + 2 * n]


def _chip_exchange_wait(send_sems, recv_sems, hs, lands, after, *, name):
    n = len(hs)

    def body(*refs):
        h_in, l_in, ss, rs = refs[:n], refs[n:2 * n], refs[2 * n], refs[2 * n + 1]
        x, y, c = _place()
        for b in range(n):
            for j, (cx, cy) in enumerate(_other_chips(x, y)):
                cp = pltpu.make_async_remote_copy(src_ref=h_in[b].at[2 * cx + cy], dst_ref=l_in[b].at[j],
                                                  send_sem=ss.at[3 * b + j], recv_sem=rs.at[3 * b + j],
                                                  device_id=(cx, cy, c), device_id_type=MESH)
                cp.wait_send()
                cp.wait_recv()

    res = pl.pallas_call(
        body, name=name, out_shape=[pltpu.HBM(a.shape, a.dtype) for a in hs + lands],
        in_specs=[HBM] * (2 * n) + [SEM, SEM, ANY], out_specs=[HBM] * (2 * n),
        input_output_aliases={k: k for k in range(2 * n)}, compiler_params=_SPLIT,
    )(*hs, *lands, send_sems, recv_sems, after)
    return list(res[n:])


def _peers(x, y, c):
    return [((1 - x) if fx else x, (1 - y) if fy else y, (1 - c) if fc else c)
            for fx in (0, 1) for fy in (0, 1) for fc in (0, 1) if fx or fy or fc]


def _all_to_all_start(slab, after, *, name):
    land = lax.empty((N_DEV,) + slab.shape, slab.dtype)

    def body(slab_in, land_in, after_ref, send_sems, recv_sems, slab_out, land_out, token):
        x, y, c = _place()
        for k, peer in enumerate(_peers(x, y, c)):
            pltpu.make_async_remote_copy(src_ref=slab_out, dst_ref=land_out.at[4 * x + 2 * y + c],
                                         send_sem=send_sems.at[k], recv_sem=recv_sems.at[k], device_id=peer,
                                         device_id_type=MESH).start()
        token[...] = jnp.zeros_like(token)

    return pl.pallas_call(
        body, name=name,
        out_shape=(pltpu.SemaphoreType.DMA((N_DEV - 1,)), pltpu.SemaphoreType.DMA((N_DEV - 1,)),
                   pltpu.HBM(slab.shape, slab.dtype), pltpu.HBM(land.shape, land.dtype),
                   jax.ShapeDtypeStruct((8, 128), F32)),
        in_specs=[HBM, HBM, ANY], out_specs=(SEM, SEM, HBM, HBM, pl.BlockSpec(memory_space=pltpu.VMEM)),
        input_output_aliases={0: 2, 1: 3}, compiler_params=_SPLIT,
    )(*_in_hbm([slab, land]), after)


def _all_to_all_wait(send_sems, recv_sems, slab, land, afters, *, name):
    def body(slab_in, land_in, ss, rs, *_):
        x, y, c = _place()
        for k, (px, py, pc) in enumerate(_peers(x, y, c)):
            cp = pltpu.make_async_remote_copy(src_ref=slab_in, dst_ref=land_in.at[4 * px + 2 * py + pc],
                                              send_sem=ss.at[k], recv_sem=rs.at[k], device_id=(px, py, pc),
                                              device_id_type=MESH)
            cp.wait_send()
            cp.wait_recv()

    return pl.pallas_call(
        body, name=name, out_shape=[pltpu.HBM(slab.shape, slab.dtype), pltpu.HBM(land.shape, land.dtype)],
        in_specs=[HBM, HBM, SEM, SEM] + [ANY] * len(afters), out_specs=[HBM, HBM], input_output_aliases={0: 0, 1: 1},
        compiler_params=_SPLIT,
    )(slab, land, send_sems, recv_sems, *afters)


def _pair_exchange_start(gs, *, name):
    n = len(gs)
    lands = [lax.empty((g.shape[0], g.shape[1] // 2, g.shape[2]), g.dtype) for g in gs]

    def body(*refs):
        send_sems, recv_sems = refs[2 * n], refs[2 * n + 1]
        g_out, l_out, token = refs[2 * n + 2:3 * n + 2], refs[3 * n + 2:4 * n + 2], refs[4 * n + 2]
        x, y, c = _place()
        for b in range(n):
            rh = gs[b].shape[1] // 2
            pltpu.make_async_remote_copy(src_ref=g_out[b].at[:, pl.ds((1 - c) * rh, rh), :], dst_ref=l_out[b],
                                         send_sem=send_sems.at[b], recv_sem=recv_sems.at[b],
                                         device_id=(x, y, 1 - c), device_id_type=MESH).start()
        token[...] = jnp.zeros_like(token)

    res = pl.pallas_call(
        body, name=name,
        out_shape=(pltpu.SemaphoreType.DMA((n,)), pltpu.SemaphoreType.DMA((n,)),
                   *[pltpu.HBM(a.shape, a.dtype) for a in gs + lands], jax.ShapeDtypeStruct((8, 128), F32)),
        in_specs=[HBM] * (2 * n), out_specs=(SEM, SEM, *[HBM] * (2 * n), pl.BlockSpec(memory_space=pltpu.VMEM)),
        input_output_aliases={k: k + 2 for k in range(2 * n)}, compiler_params=_SPLIT,
    )(*_in_hbm(gs + lands))
    return res[0], res[1], list(res[2:2 + n]), list(res[2 + n:2 + 2 * n]), res[2 + 2 * n]


def _pair_exchange_wait(send_sems, recv_sems, gs, lands, after, *, name):
    n = len(gs)

    def body(*refs):
        g_in, l_in, ss, rs = refs[:n], refs[n:2 * n], refs[2 * n], refs[2 * n + 1]
        x, y, c = _place()
        for b in range(n):
            rh = gs[b].shape[1] // 2
            cp = pltpu.make_async_remote_copy(src_ref=g_in[b].at[:, pl.ds((1 - c) * rh, rh), :], dst_ref=l_in[b],
                                              send_sem=ss.at[b], recv_sem=rs.at[b], device_id=(x, y, 1 - c),
                                              device_id_type=MESH)
            cp.wait_send()
            cp.wait_recv()

    res = pl.pallas_call(
        body, name=name, out_shape=[pltpu.HBM(a.shape, a.dtype) for a in gs + lands],
        in_specs=[HBM] * (2 * n) + [SEM, SEM, ANY], out_specs=[HBM] * (2 * n),
        input_output_aliases={k: k for k in range(2 * n)}, compiler_params=_SPLIT,
    )(*gs, *lands, send_sems, recv_sems, after)
    return list(res[:n]), list(res[n:])


def _pair_share_start(ss, *, name):
    n = len(ss)

    def body(*refs):
        send_sems, recv_sems, outs, token = refs[n], refs[n + 1], refs[n + 2:2 * n + 2], refs[2 * n + 2]
        x, y, c = _place()
        for b in range(n):
            rh = ss[b].shape[0] // 2
            mine = outs[b].at[pl.ds(c * rh, rh), :]
            pltpu.make_async_remote_copy(src_ref=mine, dst_ref=mine, send_sem=send_sems.at[b],
                                         recv_sem=recv_sems.at[b], device_id=(x, y, 1 - c),
                                         device_id_type=MESH).start()
        token[...] = jnp.zeros_like(token)

    res = pl.pallas_call(
        body, name=name,
        out_shape=(pltpu.SemaphoreType.DMA((n,)), pltpu.SemaphoreType.DMA((n,)),
                   *[pltpu.HBM(a.shape, a.dtype) for a in ss], jax.ShapeDtypeStruct((8, 128), F32)),
        in_specs=[HBM] * n, out_specs=(SEM, SEM, *[HBM] * n, pl.BlockSpec(memory_space=pltpu.VMEM)),
        input_output_aliases={k: k + 2 for k in range(n)}, compiler_params=_SPLIT,
    )(*_in_hbm(ss))
    return res[0], res[1], list(res[2:2 + n]), res[2 + n]


def _pair_share_wait(send_sems, recv_sems, ss, after, *, name):
    n = len(ss)

    def body(*refs):
        ins, sems_s, sems_r = refs[:n], refs[n], refs[n + 1]
        x, y, c = _place()
        for b in range(n):
            rh = ss[b].shape[0] // 2
            mine = ins[b].at[pl.ds(c * rh, rh), :]
            theirs = ins[b].at[pl.ds((1 - c) * rh, rh), :]
            cp = pltpu.make_async_remote_copy(src_ref=mine, dst_ref=theirs, send_sem=sems_s.at[b],
                                              recv_sem=sems_r.at[b], device_id=(x, y, 1 - c),
                                              device_id_type=MESH)
            cp.wait_send()
            cp.wait_recv()

    return list(pl.pallas_call(
        body, name=name, out_shape=[pltpu.HBM(a.shape, a.dtype) for a in ss],
        in_specs=[HBM] * n + [SEM, SEM, ANY], out_specs=[HBM] * n,
        input_output_aliases={k: k for k in range(n)}, compiler_params=_SPLIT,
    )(*ss, send_sems, recv_sems, after))


_SMALL_SHARDED = (("e_conv_w", 2), ("o_norm", 1), ("o_d", 1))
_REPLICATED = ("e_norm", "e_gmlp_w", "e_gmlp_b", "e_conv_b", "e_conv_ln_g", "e_conv_ln_b", "o_lam_re", "o_lam_im",
               "o_log_dt", "o_b_re", "o_b_im", "o_c_re", "o_c_im", "ca_norm", "ca_mem_norm", "ffn_norm", "final_norm")
_SMALL = tuple(n for n, _ in _SMALL_SHARDED) + _REPLICATED
_WEIGHTS = ("e_norm", "e_w_in", "e_gmlp_w", "e_gmlp_b", "e_conv_w", "e_conv_b", "e_conv_ln_g", "e_conv_ln_b",
            "e_w_out", "o_norm", "o_w_in", "o_lam_re", "o_lam_im", "o_log_dt", "o_b_re", "o_b_im", "o_c_re", "o_c_im",
            "o_d", "o_w_out", "ca_norm", "ca_mem_norm", "ca_wq", "ca_wk", "ca_wv", "ca_wo", "ffn_norm", "ffn_w_gate",
            "ffn_w_up", "ffn_w_down", "final_norm")


def _pack_rows(arrs, width, dtype, row_mult=8):
    parts, spans, r0 = [], [], 0
    for a in arrs:
        flat = a.reshape(-1).astype(dtype)
        rows = -(-flat.shape[0] // (width * row_mult)) * row_mult
        if rows * width != flat.shape[0]:
            flat = jnp.pad(flat, (0, rows * width - flat.shape[0]))
        parts.append(flat.reshape(rows, width))
        spans.append((r0, rows))
        r0 += rows
    return jnp.concatenate(parts, axis=0), spans


def _unpack_rows(slab, spans, shapes):
    out = []
    for (r0, rows), shp in zip(spans, shapes):
        n = math.prod(shp)
        out.append(slab[r0:r0 + rows].reshape(-1)[:n].reshape(shp))
    return out


def _two_d(a):
    return a.reshape(-1, a.shape[-1])


def _shard_rows(n, a):
    return _two_d(jnp.swapaxes(a, -1, -2) if n in _TRANSPOSED else a)


def _from_shard_rows(n, rows, shape):
    if n in _TRANSPOSED:
        return jnp.swapaxes(rows.reshape(shape[:-2] + (shape[-1], shape[-2])), -1, -2)
    return rows.reshape(shape)


def _local_slab(local, slab, dtype):
    parts = sorted((r0, n, l) for n, (_, where) in _PLACE.items() for l, (s, r0) in enumerate(where) if s == slab)
    shards = [_shard_rows(n, local[n] if len(_PLACE[n][1]) == 1 else local[n][l]) for _, n, l in parts]
    return jnp.concatenate([a.astype(dtype) for a in shards], axis=0)


def _set_diag(b, pattern):
    return jnp.einsum(pattern, b, jnp.eye(C_GROUPS // N_SETS, dtype=b.dtype))


def _s5_discretize(lam_re, lam_im, log_dt, b_re, b_im):
    dt = jnp.exp(log_dt)[:, None]
    mag = jnp.exp(lam_re * dt)
    ar = mag * jnp.cos(lam_im * dt)
    ai = mag * jnp.sin(lam_im * dt)
    den = lam_re * lam_re + lam_im * lam_im
    qr = ((ar - 1.0) * lam_re + ai * lam_im) / den
    qi = (ai * lam_re - (ar - 1.0) * lam_im) / den
    bbr = qr[..., None] * b_re - qi[..., None] * b_im
    bbi = qr[..., None] * b_im + qi[..., None] * b_re
    return ar, ai, bbr, bbi


def _attention_block(x, mem, W, w, i, tag):
    xn, q = _norm_mm(x, w["ca_norm"][i], _shards(W, "ca_wq", i), split="k", out_dtype=BF16, name=f"{tag}_q")
    memn = _rms_fwd(mem, w["ca_mem_norm"][i], name=f"{tag}_ca_memnorm")
    k = _mm_k(memn, _shards(W, "ca_wk", i), out_dtype=BF16, name=f"{tag}_k")
    v = _mm_k(memn, _shards(W, "ca_wv", i), out_dtype=BF16, name=f"{tag}_v")
    o = _attn_fwd(q, k, v, name=f"{tag}_attn")
    y = _mm_k(o, _shards(W, "ca_wo", i), add=x, name=f"{tag}_wo")
    return y, (x, xn, memn, q, k, v, o)


def _attention_block_bwd(dy, saved, mem, W, w, i, tag, G, grads, token=None, mid=None):
    x, xn, memn, q, k, v, o = saved
    gain = w["ca_norm"][i]
    if token is not None:
        k = _behind(k, token)
    G = _grad_to_slab(G, "ca_wo", i, o, dy, a_cols=256, name=f"{tag}_dwo")
    dq, dk, dv = _attn_bwd(dy, _shards(W, "ca_wo", i), q, k, v, name=f"{tag}_attn_bwd")
    token = mid(dq) if mid is not None else None
    if token is not None:
        gain = _behind(gain, token)
    G = _grad_to_slab(G, "ca_wq", i, xn, dq, a_cols=256, name=f"{tag}_dwq")
    G = _grad_to_slab(G, "ca_wk", i, memn, dk, a_cols=256, name=f"{tag}_dwk")
    G = _grad_to_slab(G, "ca_wv", i, memn, dv, a_cols=256, name=f"{tag}_dwv")
    dmemn = _mm_k_t([(dk, _shards(W, "ca_wk", i)), (dv, _shards(W, "ca_wv", i))], name=f"{tag}_dmemn")
    dx, dg = _norm_bwd_k(dq, _shards(W, "ca_wq", i), x, gain, dy, name=f"{tag}_dq_norm_bwd")
    grads["ca_norm"][i] = dg[0]
    grads["ca_mem_norm"][i] = _rms_dg(mem, w["ca_mem_norm"][i], dmemn, name=f"{tag}_ca_memnorm_bwd")[0]
    return dx, G


def _ffn_block(x, W, w, i, tag, head=None):
    fn, gate, up, h = _ffn_up(x, w["ffn_norm"][i], _shards(W, "ffn_w_gate", i), _shards(W, "ffn_w_up", i),
                              name=f"{tag}_ffn_up")
    if head is None:
        y = _mm_k(h, _shards(W, "ffn_w_down", i), add=x, name=f"{tag}_down")
    else:
        y = _down_loss_head(h, _shards(W, "ffn_w_down", i), x, *head, name=f"{tag}_down_loss_head")
    return y, (x, fn, gate, up, h)


def _ffn_block_bwd(dy, saved, W, w, i, tag, G, grads, token=None, mid=None):
    x, fn, gate, up, h = saved
    gain = w["ffn_norm"][i]
    G = _grad_to_slab(G, "ffn_w_down", i, h, dy, name=f"{tag}_dwd")
    dg, du = _ffn_bwd_hidden(dy, _shards(W, "ffn_w_down", i), gate, up, token, name=f"{tag}_ffn_bwd_hidden")
    token = mid(dg) if mid is not None else None
    if token is not None:
        gain = _behind(gain, token)
    G = _grad_to_slab(G, "ffn_w_gate", i, dg, fn, name=f"{tag}_dwg")
    G = _grad_to_slab(G, "ffn_w_up", i, du, fn, name=f"{tag}_dwu")
    dx, dgn = _ffn_in_bwd(dg, du, _shards(W, "ffn_w_gate", i), _shards(W, "ffn_w_up", i), x, gain, dy,
                          name=f"{tag}_ffn_in_bwd")
    grads["ffn_norm"][i] = dgn[0]
    return dx, G


def _gmlp_mask():
    chunk = jnp.arange(GMLP_BLOCK) // CHUNK
    return chunk[None, :] <= chunk[:, None]


def _even_block(x, W, w, tag):
    hn, proj = _norm_mm(x, w["e_norm"][0], _shards(W, "e_w_in"), split="n", out_dtype=F32, name=f"{tag}_w_in")
    wm = jnp.where(_gmlp_mask()[None], w["e_gmlp_w"][0], 0.0).astype(BF16)
    bcol = w["e_gmlp_b"][0][:, :, None]
    cw = jnp.pad(w["e_conv_w"][0], ((0, CONV_HALO - CONV_WIDTH), (0, 0)))
    cb, lg, lb = w["e_conv_b"], w["e_conv_ln_g"], w["e_conv_ln_b"]
    mix, hc = _even_fwd(proj, wm, bcol, cw, cb, lg, lb, name=f"{tag}_mixers")
    y = _mm_k(mix, _shards(W, "e_w_out"), add=x, name=f"{tag}_w_out")
    return y, (x, hn, proj, mix, hc, wm, bcol, cw)


def _even_block_bwd(dy, saved, W, w, tag, G, grads):
    x, hn, proj, mix, hc, wm, bcol, cw = saved
    dmix = _mm_k_t([(dy, _shards(W, "e_w_out"))], name=f"{tag}_dmix")
    G = _grad_to_slab(G, "e_w_out", 0, mix, dy, a_cols=256, name=f"{tag}_dw_out")
    wmt = jnp.swapaxes(wm, 1, 2)
    dpa, dhc, dwm, db, dlg, dlb, dcb = _even_bwd1(proj, dmix, hc, wm, wmt, bcol, w["e_conv_ln_g"], w["e_conv_ln_b"],
                                                  name=f"{tag}_mixers_bwd1")
    dpb, dcw = _even_bwd2(proj, dhc, cw, name=f"{tag}_mixers_bwd2")
    grads["e_gmlp_w"] = jnp.where(_gmlp_mask()[None], dwm, 0.0)[None]
    grads["e_gmlp_b"] = db[:, :, 0][None]
    grads["e_conv_ln_g"], grads["e_conv_ln_b"], grads["e_conv_b"] = dlg, dlb, dcb
    grads["e_conv_w"] = dcw[:CONV_WIDTH][None]
    G = _grad_to_slab(G, "e_w_in", 0, hn, dpa, b_cols=512, chips=(0, 2), name=f"{tag}_dw_in_a")
    G = _grad_to_slab(G, "e_w_in", 0, hn, dpb, b_cols=512, chips=(2, 2), name=f"{tag}_dw_in_b")
    dx, dg = _norm_bwd_n((dpa, dpb), _shards(W, "e_w_in"), x, w["e_norm"][0], dy, name=f"{tag}_in_bwd")
    grads["e_norm"] = dg
    return dx, G


def _odd_block(x, W, w, tag):
    S = x.shape[0]
    hn, u = _norm_mm(x, w["o_norm"][0], _shards(W, "o_w_in"), split="k", out_dtype=F32, name=f"{tag}_w_in")
    disc_in = (w["o_lam_re"][0], w["o_lam_im"][0], w["o_log_dt"][0], w["o_b_re"][0], w["o_b_im"][0])
    (ar, ai, bbr, bbi), disc_vjp = jax.vjp(_s5_discretize, *disc_in)
    sets = (N_SETS, C_GROUPS // N_SETS)
    per_set = N_STATE // N_SETS
    bset = jnp.concatenate([_set_diag(b.reshape(sets + b.shape[1:]), "jgpc,gh->jgchp").reshape(N_SETS, SET_CH, per_set)
                            for b in (bbr, bbi)], axis=2).astype(BF16)
    cset = jnp.concatenate([_set_diag(c.reshape(sets + c.shape[1:]), "jgcp,gh->jgphc").reshape(N_SETS, per_set, SET_CH)
                            for c in (w["o_c_re"][0], -w["o_c_im"][0])], axis=1).astype(BF16)
    powers, pr, pi = [], ar, ai
    for _ in range(SCAN_BLOCK):
        powers.append(jnp.concatenate([pr.reshape(STATE_ROWS, STATE_LANES), pi.reshape(STATE_ROWS, STATE_LANES)], 0))
        pr, pi = pr * ar - pi * ai, pr * ai + pi * ar
    pw = jnp.stack(powers, axis=0)
    xs = _scan_fwd(u, bset, pw, name=f"{tag}_scan").reshape(S // 8, STATE_ROWS, 8, STATE_LANES)
    yv, yg = _s5_readout(xs, cset, u, w["o_d"], name=f"{tag}_readout")
    o, y = _glu_out(yg, _shards(W, "o_w_out"), x, name=f"{tag}_glu_out")
    return y, (x, hn, u, bset, cset, pw, xs, yv, yg, o, disc_vjp)


def _odd_block_bwd(dy, saved, W, w, tag, G, grads):
    x, hn, u, bset, cset, pw, xs, yv, yg, o, disc_vjp = saved
    S = x.shape[0]
    do, dys, dus, dd = _glu_out_bwd(o, dy, _shards(W, "o_w_out"), yv, u, w["o_d"], name=f"{tag}_glu_out_bwd")
    G = _grad_to_slab(G, "o_w_out", 0, yg, do, b_cols=512, name=f"{tag}_dw_out")
    grads["o_d"] = dd
    dcset_t = _state_grad_sets(dys, xs, name=f"{tag}_dcd")
    gs, da = _scan_bwd(dys, cset, xs.reshape(S * STATE_ROWS, STATE_LANES), pw, name=f"{tag}_scan_bwd")
    gs = gs.reshape(xs.shape)
    dbset = _state_grad_sets(u, gs, name=f"{tag}_dbd")
    du, dx, dg = _s5_in_bwd(gs, bset, dus, _shards(W, "o_w_in"), x, w["o_norm"][0], dy, name=f"{tag}_in_bwd")
    G = _grad_to_slab(G, "o_w_in", 0, hn, du, a_cols=256, name=f"{tag}_dw_in")
    grads["o_norm"] = dg
    per = C_GROUPS // N_SETS
    blocks = (N_SETS, per, C_GROUP_CH, 2, per, C_STATE)
    dc = _set_diag(dcset_t.reshape(blocks), "jhcrgp,gh->rjgcp").reshape(2, C_GROUPS, C_GROUP_CH, C_STATE)
    db = _set_diag(dbset.reshape(blocks), "jgcrhp,gh->rjgpc").reshape(2, C_GROUPS, C_STATE, C_GROUP_CH)
    dcr, dci, dbbr, dbbi = dc[0], -dc[1], db[0], db[1]
    dar = da[:STATE_ROWS].reshape(C_GROUPS, C_STATE)
    dai = da[STATE_ROWS:].reshape(C_GROUPS, C_STATE)
    dlr, dli, dldt, dbr, dbi = disc_vjp((dar, dai, dbbr, dbbi))
    grads["o_lam_re"], grads["o_lam_im"], grads["o_log_dt"] = dlr[None], dli[None], dldt[None]
    grads["o_b_re"], grads["o_b_im"], grads["o_c_re"], grads["o_c_im"] = dbr[None], dbi[None], dcr[None], dci[None]
    return dx, G


def _behind(value, token):
    return value + token[0, 0].astype(value.dtype)


class _NoExchange:
    def __init__(self, W):
        self.W = W

    def first_weights(self, w):
        return self.W, w

    def weights(self, stage, after):
        return {}

    def behind_late_start(self, w):
        return w

    def late_weights(self, after):
        return {}

    def grads_ready(self, piece, G):
        return None

    def grads_crossed(self, piece, after):
        return None


def _forward_backward(xs_, mems_, tgt, w, G, exchange):
    W, w = exchange.first_weights(w)
    x1, s_mix0 = _even_block(xs_, W, w, "l0")
    W = {**W, **exchange.weights(1, x1)}
    x2, s_att0 = _attention_block(x1, mems_, W, w, 0, "l0")
    W = {**W, **exchange.weights(2, x2)}
    x3, s_ffn0 = _ffn_block(x2, W, w, 0, "l0")
    W = {**W, **exchange.weights(3, x3)}
    w = exchange.behind_late_start(w)
    x4, s_mix1 = _odd_block(x3, W, w, "l1")
    W = {**W, **exchange.late_weights(x4)}
    x5, s_att1 = _attention_block(x4, mems_, W, w, 1, "l1")
    (dx, dfinal, loss_lanes), s_ffn1 = _ffn_block(x5, W, w, 1, "l1", head=(w["final_norm"], tgt))

    grads = {n: [None, None] for n in ("ca_norm", "ca_mem_norm", "ffn_norm")}
    grads["final_norm"] = dfinal[0]
    dx, G = _ffn_block_bwd(dx, s_ffn1, W, w, 1, "l1", G, grads)
    dx, G = _attention_block_bwd(dx, s_att1, mems_, W, w, 1, "l1", G, grads)
    dx, G = _odd_block_bwd(dx, s_mix1, W, w, "l1", G, grads)
    token = exchange.grads_ready("l1", G)
    dx, G = _ffn_block_bwd(dx, s_ffn0, W, w, 0, "l0", G, grads, token,
                           lambda after: exchange.grads_crossed("l1", after))
    token = exchange.grads_ready("ffn0", G)
    dx, G = _attention_block_bwd(dx, s_att0, mems_, W, w, 0, "l0", G, grads, token,
                                 lambda after: exchange.grads_crossed("ffn0", after))
    dx, G = _even_block_bwd(dx, s_mix0, W, w, "l0", G, grads)
    for n in list(grads):
        if isinstance(grads[n], list):
            grads[n] = jnp.stack(grads[n], axis=0)
        grads[n] = grads[n].reshape(w[n].shape)
    return loss_lanes, dx, G, grads


class _Exchange:
    def __init__(self, local, chip, core):
        self.bufs = {s: lax.dynamic_update_slice(lax.empty((N_CHIPS, rows, width), BF16),
                                                 _local_slab(local, s, BF16)[None], (chip, 0, 0))
                     for s, (width, rows) in _SLABS.items()}
        small = jnp.zeros((_SMALL_SLAB_ROWS, SMALL_W), F32)
        for n, (r0, rows) in _SMALL_PLACE.items():
            small = small.at[r0:r0 + rows].set(local[n].reshape(rows, SMALL_W))
        self.bufs[_SMALL_SLAB] = lax.dynamic_update_slice(lax.empty((N_CHIPS, _SMALL_SLAB_ROWS, SMALL_W), F32),
                                                          small[None], (chip, 0, 0))
        self.shard_shapes = {n: local[n].shape for n in _SMALL_PLACE}
        self.where = jnp.stack([chip, core]).astype(jnp.int32)
        self.reduces = {}

    def weights(self, stage, after):
        send_sems, recv_sems, flying = self.flight
        slabs = self.stage_slabs(stage)
        first = list(flying).index(slabs[0])
        bufs = _gather_ici_wait(send_sems, recv_sems, [flying[s] for s in slabs], first, after,
                                name=f"gather_stage{stage}_wait")
        now = [k for k, s in enumerate(slabs) if s not in _LATE_SLABS]
        late = [k for k, s in enumerate(slabs) if s in _LATE_SLABS]
        if late:
            *state, self.late_token = _gather_forward_start([bufs[k] for k in late], name="gather_late_start")
            self.late = ([slabs[k] for k in late], *state)
        return dict(zip([slabs[k] for k in now],
                        _gather_forward([bufs[k] for k in now], name=f"gather_stage{stage}_forward")))

    def behind_late_start(self, w):
        return {**w, "o_norm": _behind(w["o_norm"], self.late_token)}

    def late_weights(self, after):
        slabs, send_sems, recv_sems, bufs = self.late
        return dict(zip(slabs, _gather_forward_wait(send_sems, recv_sems, bufs, after, name="gather_late_wait")))

    @staticmethod
    def stage_slabs(stage):
        return _STAGES[stage] + ((_SMALL_SLAB,) if stage == 0 else ())

    def first_weights(self, w):
        order = [s for k in range(len(_STAGES)) for s in self.stage_slabs(k)]
        send_sems, recv_sems, bufs, after = _gather_ici_start([self.bufs[s] for s in order], w["e_norm"],
                                                              name="gather_start")
        self.flight = (send_sems, recv_sems, dict(zip(order, bufs)))
        W = self.weights(0, after)
        w = {**w, "e_norm": _behind(w["e_norm"], after)}
        for (n, ax), (r0, rows) in zip(_SMALL_SHARDED, _SMALL_PLACE.values()):
            shards = [W[_SMALL_SLAB][p, r0:r0 + rows].reshape(self.shard_shapes[n]) for p in range(N_CHIPS)]
            w[n] = jnp.concatenate(shards, axis=ax)
        return W, w

    def pair_start(self, G, slabs, tag):
        send_sems, recv_sems, gl, lands, token = _pair_exchange_start([G[s] for s in slabs],
                                                                      name=f"grad_{tag}_pair_start")
        return (slabs, send_sems, recv_sems, gl, lands), token

    def pair_land(self, state, after, tag):
        slabs, send_sems, recv_sems, gl, lands = state
        gl, other = _pair_exchange_wait(send_sems, recv_sems, gl, lands, after, name=f"grad_{tag}_pair_wait")
        pairs = [_pair_sum(g, r, self.where, name=f"grad_pair_sum_{s}") for s, g, r in zip(slabs, gl, other)]
        send_sems, recv_sems, pairs, lands, token = _chip_exchange_start(pairs, name=f"grad_{tag}_chip_start")
        return (slabs, gl, other, send_sems, recv_sems, pairs, lands), token

    def reduce_sum(self, state, after, tag):
        slabs, gl, other, send_sems, recv_sems, pairs, lands = state
        slots = _chip_exchange_wait(send_sems, recv_sems, pairs, lands, after, name=f"grad_{tag}_chip_wait")
        return slabs, [_chip_sum(g, r, sl, self.where, name=f"grad_chip_sum_{s}")
                       for s, g, r, sl in zip(slabs, gl, other, slots)]

    @staticmethod
    def share_start(slabs, halves, tag):
        send_sems, recv_sems, halves, token = _pair_share_start(halves, name=f"grad_{tag}_share_start")
        return (slabs, send_sems, recv_sems, halves), token

    @staticmethod
    def share_finish(state, after, tag):
        slabs, send_sems, recv_sems, halves = state
        return dict(zip(slabs, _pair_share_wait(send_sems, recv_sems, halves, after, name=f"grad_{tag}_share_wait")))

    def grads_ready(self, piece, G):
        self.reduces[piece], token = self.pair_start(G, _GRAD_PIECES[piece], piece)
        return token

    def grads_crossed(self, piece, after):
        self.reduces[piece], token = self.pair_land(self.reduces[piece], after, piece)
        return token


def kernel(x, mem, e_norm, e_w_in, e_gmlp_w, e_gmlp_b, e_conv_w, e_conv_b, e_conv_ln_g, e_conv_ln_b, e_w_out, o_norm, o_w_in, o_lam_re, o_lam_im, o_log_dt, o_b_re, o_b_im, o_c_re, o_c_im, o_d, o_w_out, ca_norm, ca_mem_norm, ca_wq, ca_wk, ca_wv, ca_wo, ffn_norm, ffn_w_gate, ffn_w_up, ffn_w_down, final_norm, loss_target, m_e_norm, m_e_w_in, m_e_gmlp_w, m_e_gmlp_b, m_e_conv_w, m_e_conv_b, m_e_conv_ln_g, m_e_conv_ln_b, m_e_w_out, m_o_norm, m_o_w_in, m_o_lam_re, m_o_lam_im, m_o_log_dt, m_o_b_re, m_o_b_im, m_o_c_re, m_o_c_im, m_o_d, m_o_w_out, m_ca_norm, m_ca_mem_norm, m_ca_wq, m_ca_wk, m_ca_wv, m_ca_wo, m_ffn_norm, m_ffn_w_gate, m_ffn_w_up, m_ffn_w_down, m_final_norm, v_e_norm, v_e_w_in, v_e_gmlp_w, v_e_gmlp_b, v_e_conv_w, v_e_conv_b, v_e_conv_ln_g, v_e_conv_ln_b, v_e_w_out, v_o_norm, v_o_w_in, v_o_lam_re, v_o_lam_im, v_o_log_dt, v_o_b_re, v_o_b_im, v_o_c_re, v_o_c_im, v_o_d, v_o_w_out, v_ca_norm, v_ca_mem_norm, v_ca_wq, v_ca_wk, v_ca_wv, v_ca_wo, v_ffn_norm, v_ffn_w_gate, v_ffn_w_up, v_ffn_w_down, v_final_norm):
    args = dict(locals())
    local = {n: args[n] for n in _WEIGHTS}
    mom = {n: args["m_" + n] for n in _WEIGHTS}
    vel = {n: args["v_" + n] for n in _WEIGHTS}
    chip = 2 * lax.axis_index("x") + lax.axis_index("y")
    core = lax.axis_index("c")
    xs_, mems_, tgt = x[0], mem[0], loss_target[0]

    w = {n: local[n] for n in _REPLICATED}
    exchange = _Exchange(local, chip, core)
    G = {s: lax.empty((N_CHIPS, rows, width), F32) for s, (width, rows) in _SLABS.items()}
    loss_lanes, dx, G, grads = _forward_backward(xs_, mems_, tgt, w, G, exchange)

    gs_slab, gs_spans = _pack_rows([grads[n] for n in _SMALL] + [loss_lanes], SMALL_W, F32)
    rest0_token = exchange.grads_ready("rest0", G)
    small_flight = _all_to_all_start(gs_slab, rest0_token, name="small_grads_start")
    slabs_l1, halves_l1 = exchange.reduce_sum(exchange.reduces["l1"], small_flight[4], "l1")
    slabs_f0, halves_f0 = exchange.reduce_sum(exchange.reduces["ffn0"], small_flight[4], "ffn0")
    share, share_token = exchange.share_start(slabs_l1 + slabs_f0, halves_l1 + halves_f0, "l1_ffn0")
    token = exchange.grads_crossed("rest0", share_token)

    out_grads, delta, new_m, new_v = {}, {}, {}, {}

    def adamw_large(names):
        raw = []
        for n in names:
            shp = local[n].shape
            g_, d_, m_, v_ = _adamw_shard(_shard_rows(n, local[n]), [(gsum[s], r0) for s, r0 in _PLACE[n][1]],
                                          _shard_rows(n, mom[n]), _shard_rows(n, vel[n]), name=f"adamw_{n}")
            out_grads[n], delta[n], new_m[n], new_v[n] = (_from_shard_rows(n, t, shp) for t in (g_, d_, m_, v_))
            raw.append(d_)
        return raw

    gsum = exchange.share_finish(share, token, "l1_ffn0")
    ready = [n for n, (_, where) in _PLACE.items() if all(s in gsum for s, _ in where)]
    done = adamw_large(ready)

    gs_slab, gs_all = _all_to_all_wait(*small_flight[:4], done, name="small_grads_wait")
    gs_all = lax.dynamic_update_slice(gs_all, gs_slab[None], (2 * chip + core, 0, 0))
    gs_sum = _sum_slots(gs_all, name="small_grad_sum")
    *small_sums, loss_sum = _unpack_rows(gs_sum, gs_spans, [grads[n].shape for n in _SMALL] + [loss_lanes.shape])
    out_grads.update(zip(_SMALL, small_sums))
    for n, ax in _SMALL_SHARDED:
        width = local[n].shape[ax]
        out_grads[n] = lax.dynamic_slice_in_dim(out_grads[n], chip * width, width, axis=ax)
    d_, m_, v_ = _adamw_small([_two_d(local[n]) for n in _SMALL], [_two_d(out_grads[n]) for n in _SMALL],
                              [_two_d(mom[n]) for n in _SMALL], [_two_d(vel[n]) for n in _SMALL], name="adamw_small")
    for n, dd, mm_, vv in zip(_SMALL, d_, m_, v_):
        shp = local[n].shape
        delta[n], new_m[n], new_v[n] = dd.reshape(shp), mm_.reshape(shp), vv.reshape(shp)

    slabs_r0, halves_r0 = exchange.reduce_sum(exchange.reduces["rest0"], d_[0], "rest0")
    share, share_token = exchange.share_start(slabs_r0, halves_r0, "rest0")
    gsum = {**gsum, **exchange.share_finish(share, share_token, "rest0")}
    adamw_large([n for n in _PLACE if n not in ready])

    return (loss_sum[0, 0], dx[None], *[out_grads[n] for n in _WEIGHTS], *[delta[n] for n in _WEIGHTS],
            *[new_m[n] for n in _WEIGHTS], *[new_v[n] for n in _WEIGHTS])
```

```python
import functools
import math

import jax
import jax.numpy as jnp
from jax import lax
from jax.experimental import pallas as pl
from jax.experimental.pallas import tpu as pltpu

F32 = jnp.float32
BF16 = jnp.bfloat16
MESH = pl.DeviceIdType.MESH

EPS = 1e-6
D_MODEL = 1024
A_WIDTH = 512
A_GROUPS = 4
GMLP_BLOCK = 128
CHUNK = 64
B_WIDTH = 512
CONV_WIDTH = 31
CONV_HALO = 32
C_WIDTH = 512
C_GROUP_CH = 16
C_GROUPS = 32
C_STATE = 64
N_STATE = C_GROUPS * C_STATE
STATE_LANES = 128
STATE_ROWS = N_STATE // STATE_LANES
SCAN_BLOCK = 8
CA_HEADS = 4
CA_HEAD_DIM = 256
FFN_HIDDEN = 2816

ADAM_LR = 0.001
ADAM_B1 = 0.9
ADAM_B2 = 0.999
ADAM_EPS = 1e-08
ADAM_WD = 0.01
ADAM_STEP = 10

VMEM_LIMIT = 56 * 1024 * 1024
ACC_BYTES = 6 * 1024 * 1024
TN_VMEM_BYTES = 44 * 1024 * 1024
SMALL_W = 128
N_CHIPS = 4
N_DEV = 8

_SLABS = {"D0": (512, 1024), "E0": (1024, 256), "A0": (1024, 1024), "B0": (1024, 704), "C0": (1024, 1408),
          "D1": (512, 768), "A1": (1024, 1024), "B1": (1024, 704), "C1": (1024, 1408)}
_STAGES = (("D0", "E0"), ("A0",), ("B0", "C0"), ("D1", "A1", "B1", "C1"))
_LATE_SLABS = ("A1", "B1", "C1")
_GRAD_PIECES = {"l1": _STAGES[3], "ffn0": _STAGES[2], "rest0": _STAGES[0] + _STAGES[1]}
_PLACE = {
    "e_w_in": (1024, (("D0", 0),)), "e_w_out": (256, (("E0", 0),)),
    "o_w_out": (512, (("D1", 0),)), "o_w_in": (256, (("D1", 512),)),
    "ca_wq": (256, (("A0", 0), ("A1", 0))), "ca_wk": (256, (("A0", 256), ("A1", 256))),
    "ca_wv": (256, (("A0", 512), ("A1", 512))), "ca_wo": (256, (("A0", 768), ("A1", 768))),
    "ffn_w_down": (704, (("B0", 0), ("B1", 0))),
    "ffn_w_gate": (704, (("C0", 0), ("C1", 0))), "ffn_w_up": (704, (("C0", 704), ("C1", 704))),
}
_SMALL_SLAB = "F0"
_SMALL_SLAB_ROWS = 48
_SMALL_PLACE = {"e_conv_w": (0, 31), "o_norm": (32, 2), "o_d": (34, 1)}
_TRANSPOSED = ("ffn_w_gate", "ffn_w_up")


def _params(sem=None):
    return pltpu.CompilerParams(dimension_semantics=sem, vmem_limit_bytes=VMEM_LIMIT)


def _tile(n, pref, mult=128):
    if n <= pref:
        return n
    t = (pref // mult) * mult
    while t >= mult:
        if n % t == 0:
            return t
        t -= mult
    return n


def _blk(name, layer=0):
    rows, where = _PLACE[name]
    slab, r0 = where[layer]
    assert r0 % rows == 0
    return slab, rows, r0 // rows


def _shards(slabs, name, layer=0):
    slab, rows, b = _blk(name, layer)
    return [(slabs[slab], (None, rows, _SLABS[slab][0]), (p, b, 0)) for p in range(N_CHIPS)]


_GELU_C = 0.7978845608028654
_GELU_A = 0.044715


def _gelu(x):
    t = jnp.tanh(_GELU_C * (x + _GELU_A * (x * x * x)))
    return 0.5 * x * (1.0 + t), t


def _gelu_grad(x, t):
    return 0.5 * (1.0 + t) + 0.5 * x * (1.0 - t * t) * (_GELU_C * (1.0 + 3.0 * _GELU_A * x * x))


def _sigmoid(x):
    return 1.0 / (1.0 + jnp.exp(-x))


def _mean(x):
    return jnp.mean(x, axis=-1, keepdims=True)


def _dot(a, b):
    return jnp.dot(a, b, preferred_element_type=F32)


def _dot_nt(a, b):
    return lax.dot_general(a, b, (((1,), (1,)), ((), ())), preferred_element_type=F32)


def _dot_tn(a, b):
    return lax.dot_general(a, b, (((0,), (0,)), ((), ())), preferred_element_type=F32)


def _rms_tile(xv, gv):
    return (xv * lax.rsqrt(_mean(xv * xv) + EPS)) * gv


def _rms_bwd_tile(xv, gv, dyv):
    r = lax.rsqrt(_mean(xv * xv) + EPS)
    xh = xv * r
    dyg = dyv * gv
    return r * (dyg - xh * _mean(dyg * xh)), jnp.sum(dyv * xh, axis=0, keepdims=True)


def _cols(p, width):
    return slice(p * width, (p + 1) * width)


def _sum_k(a, ws, k):
    tot = None
    for p in range(N_CHIPS):
        y = _dot(a[:, _cols(p, k)], ws[p][...])
        tot = y if tot is None else tot + y
    return tot


def _cat_nt(a, ws):
    return jnp.concatenate([_dot_nt(a, ws[p][...]) for p in range(N_CHIPS)], axis=1)


def _rows_call(name, tm, rows, fulls, outs, accs, body, scratch=()):
    S = min(x.shape[-2] for x in rows if x.ndim != 4)
    nr, nf, no, na = len(rows), len(fulls), len(outs), len(accs)

    def kern(*refs):
        r, f = refs[:nr], refs[nr:nr + nf]
        o, a = refs[nr + nf:nr + nf + no], refs[nr + nf + no:nr + nf + no + na]
        if na:
            @pl.when(pl.program_id(0) == 0)
            def _():
                for ref in a:
                    ref[...] = jnp.zeros_like(ref)
        body(r, f, o, a, refs[nr + nf + no + na:])

    def whole(shape):
        nd = len(shape)
        return pl.BlockSpec(tuple(shape), lambda i: (0,) * nd)

    def row_spec(shape):
        if len(shape) == 4:
            return pl.BlockSpec((tm // 8,) + tuple(shape[1:]), lambda i: (i, 0, 0, 0))
        if len(shape) == 3:
            return pl.BlockSpec((shape[0], tm, shape[2]), lambda i: (0, i, 0))
        return pl.BlockSpec((tm, shape[1]), lambda i: (i, 0))

    def full_spec(x):
        if isinstance(x, tuple):
            _, bshape, bidx = x
            return pl.BlockSpec(bshape, lambda i: bidx, pipeline_mode=pl.Buffered(1))
        return whole(x.shape)

    out_shapes = [(S, o[0]) if len(o) == 2 else (o[0], S, o[1]) for o in outs]
    res = pl.pallas_call(
        kern, name=name, grid=(S // tm,),
        in_specs=[row_spec(x.shape) for x in rows] + [full_spec(x) for x in fulls],
        out_specs=[row_spec(s) for s in out_shapes] + [whole(shp) for shp, _ in accs],
        out_shape=[jax.ShapeDtypeStruct(s, o[-1]) for s, o in zip(out_shapes, outs)]
        + [jax.ShapeDtypeStruct(tuple(shp), dt) for shp, dt in accs],
        scratch_shapes=list(scratch),
        compiler_params=_params(("arbitrary",) if na else ("parallel",)),
    )(*rows, *[x[0] if isinstance(x, tuple) else x for x in fulls])
    return res[:no], res[no:]


def _grad_to_slab(gslabs, wname, layer, a, b, *, a_cols=None, b_cols=None, chips=(0, N_CHIPS), name):
    slab, rows, bidx = _blk(wname, layer)
    width = _SLABS[slab][0]
    p0, n_p = chips
    assert p0 % n_p == 0
    S = a.shape[-2]

    def tile_bytes(x, ts):
        return ts * x.dtype.itemsize * (x.shape[2] * n_p if x.ndim == 3 else x.shape[1])

    acc_bytes = n_p * rows * (-(-width // 128) * 128) * 4
    ts = next(t for t in (2048, 1024, 512, 256, S) if S % t == 0
              and 2 * (tile_bytes(a, t) + tile_bytes(b, t) + acc_bytes) <= TN_VMEM_BYTES or t == S)

    def operand(x):
        if x.ndim == 3:
            return pl.BlockSpec((n_p, ts, x.shape[2]), lambda s: (p0 // n_p, s, 0))
        return pl.BlockSpec((ts, x.shape[1]), lambda s: (s, 0))

    def part(ref, cols, p):
        if len(ref.shape) == 3:
            return ref[p]
        return ref[...] if cols is None else ref[:, _cols(p, cols)]

    def body(a_ref, b_ref, slab_ref, o_ref):
        @pl.when(pl.program_id(0) == 0)
        def _():
            o_ref[...] = jnp.zeros_like(o_ref)

        for p in range(n_p):
            o_ref[p] += _dot_tn(part(a_ref, a_cols, p).astype(BF16), part(b_ref, b_cols, p).astype(BF16))

    g = gslabs[slab]
    out = pl.pallas_call(
        body, name=name, grid=(S // ts,),
        in_specs=[operand(a), operand(b), pl.BlockSpec(memory_space=pl.ANY)],
        out_specs=pl.BlockSpec((n_p, rows, width), lambda s: (p0 // n_p, bidx, 0)),
        out_shape=jax.ShapeDtypeStruct(g.shape, F32), input_output_aliases={2: 0},
        compiler_params=_params(("arbitrary",)),
    )(a, b, g)
    return {**gslabs, slab: out}


def _vec(g):
    return g.reshape(1, -1)


def _norm_mm(x, g, ws, *, split, out_dtype, name, tm=512):
    S, D = x.shape
    k, n = ws[0][1][1], ws[0][1][2]
    N = n if split == "k" else N_CHIPS * n

    def body(r, f, o, acc, s):
        xn = _rms_tile(r[0][...], f[0][...]).astype(BF16)
        o[0][...] = xn
        if split == "k":
            o[1][...] = _sum_k(xn, f[1:], k).astype(out_dtype)
        else:
            for p in range(N_CHIPS):
                o[1][:, _cols(p, n)] = _dot(xn, f[1 + p][...]).astype(out_dtype)

    (xn, y), _ = _rows_call(name, _tile(S, tm), [x], [_vec(g)] + ws, [(D, BF16), (N, out_dtype)], [], body)
    return xn, y


def _mm_k(a, ws, *, add=None, out_dtype=F32, name, tm=512):
    S = a.shape[-2]
    k, n = ws[0][1][1], ws[0][1][2]
    has_add = add is not None

    def body(r, f, o, acc, s):
        if a.ndim == 3:
            y = None
            for p in range(N_CHIPS):
                t = _dot(r[0][p].astype(BF16), f[p][...])
                y = t if y is None else y + t
        else:
            y = _sum_k(r[0][...].astype(BF16), f, k)
        if has_add:
            y = y + r[1][...]
        o[0][...] = y.astype(out_dtype)

    (y,), _ = _rows_call(name, _tile(S, tm), [a] + ([add] if has_add else []), ws, [(n, out_dtype)], [], body)
    return y


def _mm_k_t(terms, *, out_dtype=F32, name, tm=512):
    S = terms[0][0].shape[0]
    k = terms[0][1][0][1][1]

    def body(r, f, o, acc, s):
        y = None
        for t in range(len(terms)):
            yt = _cat_nt(r[t][...].astype(BF16), f[N_CHIPS * t:N_CHIPS * (t + 1)])
            y = yt if y is None else y + yt
        o[0][...] = y.astype(out_dtype)

    (y,), _ = _rows_call(name, _tile(S, tm), [a for a, _ in terms], [w for _, ws in terms for w in ws],
                         [(N_CHIPS * k, out_dtype)], [], body)
    return y


def _rms_fwd(x, g, *, name):
    def body(r, f, o, acc, s):
        o[0][...] = _rms_tile(r[0][...], f[0][...]).astype(BF16)

    (y,), _ = _rows_call(name, _tile(x.shape[0], 256, 8), [x], [_vec(g)], [(x.shape[1], BF16)], [], body)
    return y


def _rms_dg(x, g, dy, *, name):
    def body(r, f, o, acc, s):
        acc[0][...] += _rms_bwd_tile(r[0][...], f[0][...], r[1][...])[1]

    _, (dg,) = _rows_call(name, _tile(x.shape[0], 256, 8), [x, dy], [_vec(g)], [], [((1, x.shape[1]), F32)], body)
    return dg


def _ffn_up(x, g, wg, wu, *, name, tm=512):
    S, D = x.shape
    h = wg[0][1][1]

    def body(r, f, o, acc, s):
        xn = _rms_tile(r[0][...], f[0][...]).astype(BF16)
        o[0][...] = xn
        for p in range(N_CHIPS):
            gate = _dot_nt(xn, f[1 + p][...])
            up = _dot_nt(xn, f[1 + N_CHIPS + p][...])
            o[1][p] = gate.astype(BF16)
            o[2][p] = up.astype(BF16)
            o[3][p] = (gate * _sigmoid(gate) * up).astype(BF16)

    (xn, gate, up, hid), _ = _rows_call(name, _tile(S, tm), [x], [_vec(g)] + wg + wu,
                                        [(D, BF16), (N_CHIPS, h, BF16), (N_CHIPS, h, BF16), (N_CHIPS, h, BF16)], [],
                                        body)
    return xn, gate, up, hid


def _ffn_bwd_hidden(dy, wd, gate, up, token=None, *, name, tm=512):
    S = dy.shape[0]
    h = wd[0][1][1]

    def body(r, f, o, acc, s):
        dyv = r[0][...]
        if token is not None:
            dyv = dyv + jnp.sum(f[N_CHIPS][...])
        dyb = dyv.astype(BF16)
        for p in range(N_CHIPS):
            dh = _dot_nt(dyb, f[p][...])
            gv = r[1][p].astype(F32)
            sg = _sigmoid(gv)
            o[0][p] = (dh * r[2][p].astype(F32) * (sg * (1.0 + gv * (1.0 - sg)))).astype(BF16)
            o[1][p] = (dh * gv * sg).astype(BF16)

    (dg, du), _ = _rows_call(name, _tile(S, tm), [dy, gate, up], wd + ([] if token is None else [token]),
                             [(N_CHIPS, h, BF16), (N_CHIPS, h, BF16)], [], body)
    return dg, du


def _ffn_in_bwd(dg, du, wg, wu, x, g, dres, *, name, tm=512):
    S, D = x.shape

    def body(r, f, o, acc, s):
        tot = None
        for p in range(N_CHIPS):
            y = _dot(r[0][p], f[1 + p][...]) + _dot(r[1][p], f[1 + N_CHIPS + p][...])
            tot = y if tot is None else tot + y
        dx, dgn = _rms_bwd_tile(r[2][...], f[0][...], tot)
        o[0][...] = dx + r[3][...]
        acc[0][...] += dgn

    (dx,), (dgn,) = _rows_call(name, _tile(S, tm), [dg, du, x, dres], [_vec(g)] + wg + wu, [(D, F32)],
                               [((1, D), F32)], body)
    return dx, dgn


def _norm_bwd_k(da, ws, x, g, dres, *, name, tm=512):
    S, D = x.shape

    def body(r, f, o, acc, s):
        dx, dg = _rms_bwd_tile(r[1][...], f[0][...], _cat_nt(r[0][...].astype(BF16), f[1:]))
        o[0][...] = dx + r[2][...]
        acc[0][...] += dg

    (dx,), (dg,) = _rows_call(name, _tile(S, tm), [da, x, dres], [_vec(g)] + ws, [(D, F32)], [((1, D), F32)], body)
    return dx, dg


def _norm_bwd_n(das, ws, x, g, dres, *, name, tm=256):
    S, D = x.shape
    n = ws[0][1][2]

    def body(r, f, o, acc, s):
        tot = None
        for p in range(N_CHIPS):
            y = _dot_nt(r[p // 2][:, _cols(p % 2, n)], f[1 + p][...])
            tot = y if tot is None else tot + y
        dx, dg = _rms_bwd_tile(r[2][...], f[0][...], tot)
        o[0][...] = dx + r[3][...]
        acc[0][...] += dg

    (dx,), (dg,) = _rows_call(name, _tile(S, tm), list(das) + [x, dres], [_vec(g)] + ws, [(D, F32)], [((1, D), F32)],
                              body)
    return dx, dg


def _ln_stats(v):
    mu = _mean(v)
    xc = v - mu
    rstd = lax.rsqrt(_mean(xc * xc) + EPS)
    return xc * rstd, rstd


_SHIFTS = 8
_CONV_ROWS = 64


def _fill_shifts(sh_ref, ext_ref, tm):
    sh_ref[0] = ext_ref[...]
    for s in range(1, _SHIFTS):
        sh_ref[s, 0:tm + CONV_HALO - _SHIFTS, :] = ext_ref[pl.ds(s, tm + CONV_HALO - _SHIFTS), :]


def _window(sh_ref, off, tm):
    return sh_ref[off % _SHIFTS, pl.ds(off - off % _SHIFTS, tm), :]


def _even_fwd(proj, wm, bcol, cw, cb, lg, lb, *, name):
    S = proj.shape[0]
    tm = _tile(S, 256)
    hb = tm // CONV_HALO
    nblk = tm // GMLP_BLOCK

    def body(p_ref, halo_ref, wm_ref, b_ref, cw_ref, cb_ref, lg_ref, lb_ref, mix_ref, hc_ref, hext_ref, hsh_ref):
        i = pl.program_id(0)
        gu, _ = _gelu(p_ref[:, 0:A_WIDTH])
        gv, _ = _gelu(p_ref[:, A_WIDTH:2 * A_WIDTH])
        vn, _ = _ln_stats(gv)
        vnb = vn.astype(BF16)
        for n in range(nblk):
            rows = slice(n * GMLP_BLOCK, (n + 1) * GMLP_BLOCK)
            for g in range(A_GROUPS):
                cols = slice(g * GMLP_BLOCK, (g + 1) * GMLP_BLOCK)
                sg = jnp.dot(wm_ref[g], vnb[rows, cols], preferred_element_type=F32) + b_ref[g]
                mix_ref[rows, cols] = (gu[rows, cols] * sg).astype(BF16)
        h = p_ref[:, 1024:1536] * _sigmoid(p_ref[:, 1536:2048])
        hh = halo_ref[:, 0:B_WIDTH] * _sigmoid(halo_ref[:, B_WIDTH:2 * B_WIDTH])
        hext_ref[0:CONV_HALO, :] = jnp.where(i > 0, hh, 0.0)
        hext_ref[CONV_HALO:CONV_HALO + tm, :] = h
        _fill_shifts(hsh_ref, hext_ref, tm)
        for r0 in range(0, tm, _CONV_ROWS):
            acc = jnp.zeros((_CONV_ROWS, B_WIDTH), F32)
            for k in range(CONV_WIDTH):
                acc = acc + cw_ref[k:k + 1, :] * _window(hsh_ref, r0 + k + CONV_HALO - CONV_WIDTH + 1, _CONV_ROWS)
            hc_ref[r0:r0 + _CONV_ROWS, :] = acc + cb_ref[...]
        hc = hc_ref[...]
        hhat, _ = _ln_stats(hc)
        hl = hhat * lg_ref[...] + lb_ref[...]
        mix_ref[:, A_WIDTH:A_WIDTH + B_WIDTH] = (hl * _sigmoid(hl)).astype(BF16)

    vec = pl.BlockSpec((1, B_WIDTH), lambda i: (0, 0))
    return pl.pallas_call(
        body, name=name, grid=(S // tm,),
        in_specs=[
            pl.BlockSpec((tm, 2048), lambda i: (i, 0)),
            pl.BlockSpec((CONV_HALO, 1024), lambda i: (jnp.maximum(i * hb - 1, 0), 1)),
            pl.BlockSpec((A_GROUPS, GMLP_BLOCK, GMLP_BLOCK), lambda i: (0, 0, 0)),
            pl.BlockSpec((A_GROUPS, GMLP_BLOCK, 1), lambda i: (0, 0, 0)),
            pl.BlockSpec((CONV_HALO, B_WIDTH), lambda i: (0, 0)),
            vec, vec, vec,
        ],
        out_specs=[pl.BlockSpec((tm, 1024), lambda i: (i, 0)), pl.BlockSpec((tm, B_WIDTH), lambda i: (i, 0))],
        out_shape=[jax.ShapeDtypeStruct((S, 1024), BF16), jax.ShapeDtypeStruct((S, B_WIDTH), F32)],
        scratch_shapes=[pltpu.VMEM((tm + CONV_HALO, B_WIDTH), F32),
                        pltpu.VMEM((_SHIFTS, tm + CONV_HALO, B_WIDTH), F32)],
        compiler_params=_params(("parallel",)),
    )(proj, proj, wm, bcol, cw, cb, lg, lb)


def _even_bwd1(proj, dmix, hc, wm, wmt, bcol, lg, lb, *, name):
    S = proj.shape[0]
    tm = _tile(S, 256)
    nblk = tm // GMLP_BLOCK

    def body(p_ref, dm_ref, hc_ref, wm_ref, wmt_ref, b_ref, lg_ref, lb_ref,
             dpa_ref, dhc_ref, dwm_ref, db_ref, dlg_ref, dlb_ref, dcb_ref, dgu_ref, dvn_ref):
        @pl.when(pl.program_id(0) == 0)
        def _():
            dwm_ref[...] = jnp.zeros_like(dwm_ref)
            db_ref[...] = jnp.zeros_like(db_ref)
            dlg_ref[...] = jnp.zeros_like(dlg_ref)
            dlb_ref[...] = jnp.zeros_like(dlb_ref)
            dcb_ref[...] = jnp.zeros_like(dcb_ref)

        au = p_ref[:, 0:A_WIDTH]
        av = p_ref[:, A_WIDTH:2 * A_WIDTH]
        gu, tu = _gelu(au)
        gv, tv = _gelu(av)
        vn, rstd = _ln_stats(gv)
        vnb = vn.astype(BF16)
        for n in range(nblk):
            rows = slice(n * GMLP_BLOCK, (n + 1) * GMLP_BLOCK)
            for g in range(A_GROUPS):
                cols = slice(g * GMLP_BLOCK, (g + 1) * GMLP_BLOCK)
                vb = vnb[rows, cols]
                sg = jnp.dot(wm_ref[g], vb, preferred_element_type=F32) + b_ref[g]
                da = dm_ref[rows, cols]
                dsg = da * gu[rows, cols]
                dgu_ref[rows, cols] = da * sg
                dsgb = dsg.astype(BF16)
                dwm_ref[g] += _dot_nt(dsgb, vb)
                db_ref[g] += jnp.sum(dsg, axis=1, keepdims=True)
                dvn_ref[rows, cols] = jnp.dot(wmt_ref[g], dsgb, preferred_element_type=F32)
        dvn = dvn_ref[...]
        dgv = rstd * (dvn - _mean(dvn) - vn * _mean(dvn * vn))
        dpa_ref[:, 0:A_WIDTH] = (dgu_ref[...] * _gelu_grad(au, tu)).astype(BF16)
        dpa_ref[:, A_WIDTH:2 * A_WIDTH] = (dgv * _gelu_grad(av, tv)).astype(BF16)
        hhat, rstd2 = _ln_stats(hc_ref[...])
        lgv = lg_ref[...]
        hl = hhat * lgv + lb_ref[...]
        s = _sigmoid(hl)
        dhl = dm_ref[:, A_WIDTH:A_WIDTH + B_WIDTH] * (s * (1.0 + hl * (1.0 - s)))
        dlg_ref[...] += jnp.sum(dhl * hhat, axis=0, keepdims=True)
        dlb_ref[...] += jnp.sum(dhl, axis=0, keepdims=True)
        dhh = dhl * lgv
        dhc = rstd2 * (dhh - _mean(dhh) - hhat * _mean(dhh * hhat))
        dcb_ref[...] += jnp.sum(dhc, axis=0, keepdims=True)
        dhc_ref[...] = dhc

    vec = pl.BlockSpec((1, B_WIDTH), lambda i: (0, 0))
    w3 = pl.BlockSpec((A_GROUPS, GMLP_BLOCK, GMLP_BLOCK), lambda i: (0, 0, 0))
    b3 = pl.BlockSpec((A_GROUPS, GMLP_BLOCK, 1), lambda i: (0, 0, 0))
    return pl.pallas_call(
        body, name=name, grid=(S // tm,),
        in_specs=[
            pl.BlockSpec((tm, 1024), lambda i: (i, 0)),
            pl.BlockSpec((tm, 1024), lambda i: (i, 0)),
            pl.BlockSpec((tm, B_WIDTH), lambda i: (i, 0)),
            w3, w3, b3, vec, vec,
        ],
        out_specs=[pl.BlockSpec((tm, 1024), lambda i: (i, 0)), pl.BlockSpec((tm, B_WIDTH), lambda i: (i, 0)),
                   w3, b3, vec, vec, vec],
        out_shape=[
            jax.ShapeDtypeStruct((S, 1024), BF16), jax.ShapeDtypeStruct((S, B_WIDTH), F32),
            jax.ShapeDtypeStruct((A_GROUPS, GMLP_BLOCK, GMLP_BLOCK), F32),
            jax.ShapeDtypeStruct((A_GROUPS, GMLP_BLOCK, 1), F32),
            jax.ShapeDtypeStruct((1, B_WIDTH), F32), jax.ShapeDtypeStruct((1, B_WIDTH), F32),
            jax.ShapeDtypeStruct((1, B_WIDTH), F32),
        ],
        scratch_shapes=[pltpu.VMEM((tm, A_WIDTH), F32), pltpu.VMEM((tm, A_WIDTH), F32)],
        compiler_params=_params(("arbitrary",)),
    )(proj, dmix, hc, wm, wmt, bcol, lg, lb)


def _even_bwd2(proj, dhc, cw, *, name):
    S = proj.shape[0]
    tm = _tile(S, 256)
    hb = tm // CONV_HALO
    nt = S // tm
    last_halo = S // CONV_HALO - 1
    lo = CONV_HALO - CONV_WIDTH + 1

    def body(p_ref, halo_ref, d_ref, dnext_ref, cw_ref, dpb_ref, dcw_ref, hext_ref, dext_ref, hsh_ref, dsh_ref):
        i = pl.program_id(0)

        @pl.when(i == 0)
        def _():
            dcw_ref[...] = jnp.zeros_like(dcw_ref)

        hh = halo_ref[:, 0:B_WIDTH] * _sigmoid(halo_ref[:, B_WIDTH:2 * B_WIDTH])
        hext_ref[0:CONV_HALO, :] = jnp.where(i > 0, hh, 0.0)
        hext_ref[CONV_HALO:CONV_HALO + tm, :] = p_ref[:, 0:B_WIDTH] * _sigmoid(p_ref[:, B_WIDTH:2 * B_WIDTH])
        dext_ref[0:tm, :] = d_ref[...]
        dext_ref[tm:tm + CONV_HALO, :] = jnp.where(i < nt - 1, dnext_ref[...], 0.0)
        _fill_shifts(hsh_ref, hext_ref, tm)
        _fill_shifts(dsh_ref, dext_ref, tm)
        for r0 in range(0, tm, _CONV_ROWS):
            rows = slice(r0, r0 + _CONV_ROWS)
            dhc_b = d_ref[rows, :]
            dh = jnp.zeros((_CONV_ROWS, B_WIDTH), F32)
            for k in range(CONV_WIDTH):
                dh = dh + cw_ref[k:k + 1, :] * _window(dsh_ref, r0 + CONV_WIDTH - 1 - k, _CONV_ROWS)
                dcw_ref[k:k + 1, :] += jnp.sum(dhc_b * _window(hsh_ref, r0 + k + lo, _CONV_ROWS), axis=0,
                                               keepdims=True)
            ba_b = p_ref[rows, 0:B_WIDTH]
            sg_b = _sigmoid(p_ref[rows, B_WIDTH:2 * B_WIDTH])
            dpb_ref[rows, 0:B_WIDTH] = (dh * sg_b).astype(BF16)
            dpb_ref[rows, B_WIDTH:2 * B_WIDTH] = (dh * ba_b * sg_b * (1.0 - sg_b)).astype(BF16)

    return pl.pallas_call(
        body, name=name, grid=(nt,),
        in_specs=[
            pl.BlockSpec((tm, 1024), lambda i: (i, 1)),
            pl.BlockSpec((CONV_HALO, 1024), lambda i: (jnp.maximum(i * hb - 1, 0), 1)),
            pl.BlockSpec((tm, B_WIDTH), lambda i: (i, 0)),
            pl.BlockSpec((CONV_HALO, B_WIDTH), lambda i: (jnp.minimum((i + 1) * hb, last_halo), 0)),
            pl.BlockSpec((CONV_HALO, B_WIDTH), lambda i: (0, 0)),
        ],
        out_specs=[pl.BlockSpec((tm, 1024), lambda i: (i, 0)), pl.BlockSpec((CONV_HALO, B_WIDTH), lambda i: (0, 0))],
        out_shape=[jax.ShapeDtypeStruct((S, 1024), BF16), jax.ShapeDtypeStruct((CONV_HALO, B_WIDTH), F32)],
        scratch_shapes=[pltpu.VMEM((tm + CONV_HALO, B_WIDTH), F32), pltpu.VMEM((tm + CONV_HALO, B_WIDTH), F32),
                        pltpu.VMEM((_SHIFTS, tm + CONV_HALO, B_WIDTH), F32),
                        pltpu.VMEM((_SHIFTS, tm + CONV_HALO, B_WIDTH), F32)],
        compiler_params=_params(("arbitrary",)),
    )(proj, proj, dhc, dhc, cw)


_CA_SCALE = CA_HEAD_DIM ** -0.5


def _softmax_rows(s):
    e = jnp.exp(s - jnp.max(s, axis=-1, keepdims=True))
    return e / jnp.sum(e, axis=-1, keepdims=True)


def _attn_fwd(q, k, v, *, name):
    S = q.shape[0]

    def body(r, f, o, acc, s):
        for h in range(CA_HEADS):
            cols = _cols(h, CA_HEAD_DIM)
            p = _softmax_rows(_dot_nt(r[0][:, cols], f[0][:, cols]) * _CA_SCALE)
            o[0][:, cols] = _dot(p.astype(BF16), f[1][:, cols]).astype(BF16)

    (o_,), _ = _rows_call(name, _tile(S, 512), [q], [k, v], [(D_MODEL, BF16)], [], body)
    return o_


def _attn_bwd(dy, wo, q, k, v, *, name):
    S = q.shape[0]
    M = k.shape[0]

    def body(r, f, o, acc, s):
        dyb = r[0][...].astype(BF16)
        for h in range(CA_HEADS):
            cols = _cols(h, CA_HEAD_DIM)
            qh = r[1][:, cols]
            kh = f[0][:, cols]
            vh = f[1][:, cols]
            doh = _dot_nt(dyb, f[2 + h][...]).astype(BF16)
            p = _softmax_rows(_dot_nt(qh, kh) * _CA_SCALE)
            acc[1][:, cols] += _dot_tn(p.astype(BF16), doh)
            dp = _dot_nt(doh, vh)
            ds = (p * (dp - jnp.sum(dp * p, axis=-1, keepdims=True)) * _CA_SCALE).astype(BF16)
            o[0][:, cols] = _dot(ds, kh).astype(BF16)
            acc[0][:, cols] += _dot_tn(ds, qh)

    (dq,), (dk, dv) = _rows_call(name, _tile(S, 512), [dy, q], [k, v] + wo, [(D_MODEL, BF16)],
                                 [((M, D_MODEL), F32), ((M, D_MODEL), F32)], body)
    return dq, dk, dv


_STATE_TILE = 2 * STATE_ROWS
N_SETS = 4
SET_CH = C_WIDTH // N_SETS
SET_COLS = N_STATE // N_SETS // STATE_LANES


def _set_groups(j):
    return [SET_COLS * j + c for c in range(SET_COLS)] + [STATE_ROWS + SET_COLS * j + c for c in range(SET_COLS)]


def _pack_state(re, im):
    hi = lax.bitcast_convert_type(re.astype(BF16).astype(F32), jnp.uint32)
    lo = lax.bitcast_convert_type(im.astype(BF16).astype(F32), jnp.uint32) >> 16
    return hi | lo


def _unpack_state(word):
    re = lax.bitcast_convert_type(word & jnp.uint32(0xFFFF0000), F32)
    im = lax.bitcast_convert_type(word << 16, F32)
    return re, im


def _state_set(ref, tm, j):
    parts = [_unpack_state(ref[:, SET_COLS * j + c, :, :].reshape(tm, STATE_LANES)) for c in range(SET_COLS)]
    return jnp.concatenate([p[0].astype(BF16) for p in parts] + [p[1].astype(BF16) for p in parts], axis=1)


def _s5_readout(xs, cset, u, d, *, name, tm=256):
    tm = _tile(u.shape[0], tm)

    def body(r, f, o, acc, s):
        y0 = jnp.concatenate([_dot(_state_set(r[0], tm, j), f[0][j]) for j in range(N_SETS)], axis=1)
        y = y0 + f[1][...] * r[1][...]
        o[0][...] = y
        o[1][...] = _gelu(y)[0].astype(BF16)

    (y, yg), _ = _rows_call(name, tm, [xs, u], [cset, d], [(C_WIDTH, F32), (C_WIDTH, BF16)], [], body)
    return y, yg


def _state_grad_sets(a, st, *, name, ts=256):
    ts = _tile(a.shape[0], ts)

    def body(r, f, o, acc, s):
        for j in range(N_SETS):
            acc[0][j] += _dot_tn(r[0][:, _cols(j, SET_CH)].astype(BF16), _state_set(r[1], ts, j))

    _, (out,) = _rows_call(name, ts, [a, st], [], [], [((N_SETS, SET_CH, 2 * N_STATE // N_SETS), F32)], body)
    return out


def _glu_out(yg, ws, x, *, name, tm=512):
    n = ws[0][1][2]

    def body(r, f, o, acc, s):
        ygv = r[0][...]
        ov = [_dot(ygv, f[p][...]) for p in range(N_CHIPS)]
        for p in range(N_CHIPS):
            o[0][:, _cols(p, n)] = ov[p].astype(BF16)
        for p in range(2):
            o[1][:, _cols(p, n)] = r[1][:, _cols(p, n)] + ov[p] * _sigmoid(ov[2 + p])

    (o_, y), _ = _rows_call(name, _tile(x.shape[0], tm), [yg, x], ws, [(2 * D_MODEL, BF16), (D_MODEL, F32)], [], body)
    return o_, y


def _glu_out_bwd(o_, dy, ws, y, u, d, *, name, tm=256):
    n = ws[0][1][2]

    def body(r, f, o, acc, s):
        o1 = r[0][:, 0:D_MODEL].astype(F32)
        sg = _sigmoid(r[0][:, D_MODEL:2 * D_MODEL].astype(F32))
        dyv = r[1][...]
        do1 = (dyv * sg).astype(BF16)
        do2 = (dyv * o1 * sg * (1.0 - sg)).astype(BF16)
        o[0][:, 0:D_MODEL] = do1
        o[0][:, D_MODEL:2 * D_MODEL] = do2
        dyg = None
        for p in range(N_CHIPS):
            t = _dot_nt((do1 if p < 2 else do2)[:, _cols(p % 2, n)], f[1 + p][...])
            dyg = t if dyg is None else dyg + t
        yv = r[2][...]
        dys = dyg * _gelu_grad(yv, _gelu(yv)[1])
        o[1][...] = dys.astype(BF16)
        o[2][...] = f[0][...] * dys
        acc[0][...] += jnp.sum(dys * r[3][...], axis=0, keepdims=True)

    (do, dys, dus), (dd,) = _rows_call(name, _tile(dy.shape[0], tm), [o_, dy, y, u], [d] + ws,
                                       [(2 * D_MODEL, BF16), (C_WIDTH, BF16), (C_WIDTH, F32)], [((1, C_WIDTH), F32)],
                                       body)
    return do, dys, dus, dd


def _s5_in_bwd(gs, bset, dus, ws, x, g, dres, *, name, tm=256):
    D = x.shape[1]
    tm = _tile(x.shape[0], tm)

    def body(r, f, o, acc, s):
        du0 = jnp.concatenate([_dot_nt(_state_set(r[0], tm, j), f[1][j]) for j in range(N_SETS)], axis=1)
        du = (du0 + r[1][...]).astype(BF16)
        o[0][...] = du
        dx, dg = _rms_bwd_tile(r[2][...], f[0][...], _cat_nt(du, f[2:]))
        o[1][...] = dx + r[3][...]
        acc[0][...] += dg

    (du, dx), (dg,) = _rows_call(name, tm, [gs, dus, x, dres], [_vec(g), bset] + ws,
                                 [(C_WIDTH, BF16), (D, F32)], [((1, D), F32)], body)
    return du, dx, dg


_SCAN_CHUNK = 256
_RE = slice(0, STATE_ROWS)
_IM = slice(STATE_ROWS, 2 * STATE_ROWS)
assert SCAN_BLOCK == 8


def _token(g, i, rows):
    return pl.ds(pl.multiple_of(g * (rows * SCAN_BLOCK), rows * SCAN_BLOCK) + i, rows, stride=SCAN_BLOCK)


def _fill_chunk(s3, a_ref, wset, tc, nt):
    for j in range(N_SETS):
        av = a_ref[:, _cols(j, SET_CH)].astype(BF16)
        y = _dot_nt(av, wset[j]) if nt else _dot(av, wset[j])
        for k, c in enumerate(_set_groups(j)):
            s3[:, 8 * c:8 * (c + 1), :] = y[:, _cols(k, STATE_LANES)].reshape(tc // 8, 8, STATE_LANES)


def _chunk_token(s3, g, i):
    return s3[g, pl.ds(i, _STATE_TILE, stride=SCAN_BLOCK), :]


def _scan_fwd(u, bset, pw, *, name):
    S = u.shape[0]
    tc = _tile(S, _SCAN_CHUNK, 8)

    def body(u_ref, bset_ref, pw_ref, xs_ref, st_ref, s3):
        @pl.when(pl.program_id(0) == 0)
        def _():
            st_ref[...] = jnp.zeros_like(st_ref)

        _fill_chunk(s3, u_ref, bset_ref, tc, nt=False)
        ar = pw_ref[0, _RE, :]
        ai = pw_ref[0, _IM, :]

        def block(g, carry):
            xr, xi = carry
            cr = ci = nr = ni = None
            for j in range(SCAN_BLOCK):
                b = _chunk_token(s3, g, j)
                br, bi = b[_RE], b[_IM]
                cr, ci = (br, bi) if j == 0 else (ar * cr - ai * ci + br, ar * ci + ai * cr + bi)
                pr, pi = pw_ref[j, _RE, :], pw_ref[j, _IM, :]
                nr = pr * xr - pi * xi + cr
                ni = pr * xi + pi * xr + ci
                xs_ref[_token(g, j, STATE_ROWS), :] = _pack_state(nr, ni)
            return nr, ni

        xr, xi = lax.fori_loop(0, tc // SCAN_BLOCK, block, (st_ref[_RE, :], st_ref[_IM, :]), unroll=4)
        st_ref[_RE, :] = xr
        st_ref[_IM, :] = xi

    return pl.pallas_call(
        body, name=name, grid=(S // tc,),
        in_specs=[pl.BlockSpec((tc, u.shape[1]), lambda i: (i, 0)), pl.BlockSpec(bset.shape, lambda i: (0, 0, 0)),
                  pl.BlockSpec(pw.shape, lambda i: (0, 0, 0))],
        out_specs=pl.BlockSpec((tc * STATE_ROWS, STATE_LANES), lambda i: (i, 0)),
        out_shape=jax.ShapeDtypeStruct((S * STATE_ROWS, STATE_LANES), jnp.uint32),
        scratch_shapes=[pltpu.VMEM((2 * STATE_ROWS, STATE_LANES), F32),
                        pltpu.VMEM((tc // 8, _STATE_TILE * 8, STATE_LANES), F32)],
        compiler_params=_params(("arbitrary",)),
    )(u, bset, pw)


def _scan_bwd(dys, cset, xs, pw, *, name):
    S = dys.shape[0]
    tc = _tile(S, _SCAN_CHUNK, 8)
    nc = S // tc

    def body(dys_ref, cset_ref, xs_ref, pw_ref, g_ref, da_ref, st_ref, s3):
        @pl.when(pl.program_id(0) == 0)
        def _():
            st_ref[...] = jnp.zeros_like(st_ref)
            da_ref[...] = jnp.zeros_like(da_ref)

        _fill_chunk(s3, dys_ref, cset_ref, tc, nt=True)
        ar = pw_ref[0, _RE, :]
        ai = pw_ref[0, _IM, :]

        def block(k, carry):
            gr, gi, dar, dai = carry
            g = tc // SCAN_BLOCK - 1 - k
            cr = ci = None
            pgr, pgi = gr, gi
            for j in range(SCAN_BLOCK):
                i = SCAN_BLOCK - 1 - j
                xr, xi = _unpack_state(xs_ref[_token(g, i, STATE_ROWS), :])
                dar = dar + pgr * xr + pgi * xi
                dai = dai + pgi * xr - pgr * xi
                d = _chunk_token(s3, g, i)
                dr, di = d[_RE], d[_IM]
                cr, ci = (dr, di) if j == 0 else (ar * cr + ai * ci + dr, ar * ci - ai * cr + di)
                pr, pi = pw_ref[j, _RE, :], pw_ref[j, _IM, :]
                pgr = pr * gr + pi * gi + cr
                pgi = pr * gi - pi * gr + ci
                g_ref[_token(g, i, STATE_ROWS), :] = _pack_state(pgr, pgi)
            return pgr, pgi, dar, dai

        init = (st_ref[_RE, :], st_ref[_IM, :], da_ref[_RE, :], da_ref[_IM, :])
        gr, gi, dar, dai = lax.fori_loop(0, tc // SCAN_BLOCK, block, init, unroll=4)
        st_ref[_RE, :] = gr
        st_ref[_IM, :] = gi
        da_ref[_RE, :] = dar
        da_ref[_IM, :] = dai

    packed = pl.BlockSpec((tc * STATE_ROWS, STATE_LANES), lambda i: (nc - 1 - i, 0))
    vec = pl.BlockSpec((2 * STATE_ROWS, STATE_LANES), lambda i: (0, 0))
    return pl.pallas_call(
        body, name=name, grid=(nc,),
        in_specs=[pl.BlockSpec((tc, dys.shape[1]), lambda i: (nc - 1 - i, 0)),
                  pl.BlockSpec(cset.shape, lambda i: (0, 0, 0)), packed, pl.BlockSpec(pw.shape, lambda i: (0, 0, 0))],
        out_specs=[packed, vec],
        out_shape=[jax.ShapeDtypeStruct(xs.shape, jnp.uint32), jax.ShapeDtypeStruct((2 * STATE_ROWS, STATE_LANES), F32)],
        scratch_shapes=[pltpu.VMEM((2 * STATE_ROWS, STATE_LANES), F32),
                        pltpu.VMEM((tc // 8, _STATE_TILE * 8, STATE_LANES), F32)],
        compiler_params=_params(("arbitrary",)),
    )(dys, cset, xs, pw)


def _down_loss_head(h, ws, x, g, target, *, name, tm=512):
    S, D = x.shape

    def body(r, f, o, acc, s):
        xv = r[1][...]
        for p in range(N_CHIPS):
            xv = xv + _dot(r[0][p], f[1 + p][...])
        gv = f[0][...]
        rs = lax.rsqrt(_mean(xv * xv) + EPS)
        xh = xv * rs
        err = xh * gv - r[2][...]
        acc[1][...] += 0.5 * jnp.sum(_mean(err * err), axis=0, keepdims=True)
        dy = err * (1.0 / D)
        dyg = dy * gv
        o[0][...] = rs * (dyg - xh * _mean(dyg * xh))
        acc[0][...] += jnp.sum(dy * xh, axis=0, keepdims=True)

    (dx,), (dg, loss) = _rows_call(name, _tile(S, tm), [h, x, target], [_vec(g)] + ws, [(D, F32)],
                                   [((1, D), F32), ((1, 128), F32)], body)
    return dx, dg, loss


_ADAM_C1 = 1.0 - ADAM_B1 ** ADAM_STEP
_ADAM_C2 = 1.0 - ADAM_B2 ** ADAM_STEP
_ONE_BLOCK_BYTES = 8 * 1024 * 1024


def _adamw_math(w, g, m, v):
    nm = ADAM_B1 * m + (1.0 - ADAM_B1) * g
    nv = ADAM_B2 * v + (1.0 - ADAM_B2) * (g * g)
    m_hat = nm / _ADAM_C1
    v_hat = nv / _ADAM_C2
    return -ADAM_LR * (m_hat / (jnp.sqrt(v_hat) + ADAM_EPS) + ADAM_WD * w), nm, nv


def _adamw_shard(w, gsrc, m, v, *, name):
    R, C = w.shape
    n_l = len(gsrc)
    rows = R // n_l
    tr = rows
    for _, r0 in gsrc:
        tr = math.gcd(tr, r0) if r0 else tr
    tr = _tile(tr, 256, 8) if tr > 256 else tr
    nb = rows // tr
    assert rows % tr == 0 and all(r0 % tr == 0 for _, r0 in gsrc)

    def body(*refs):
        w_ref, g_refs, (m_ref, v_ref, go_ref, d_ref, nm_ref, nv_ref) = refs[0], refs[1:1 + n_l], refs[1 + n_l:]
        layer = pl.program_id(0) // nb
        gv = g_refs[0][...]
        for l in range(1, n_l):
            gv = jnp.where(layer == l, g_refs[l][...], gv)
        go_ref[...] = gv
        d_ref[...], nm_ref[...], nv_ref[...] = _adamw_math(w_ref[...], gv, m_ref[...], v_ref[...])

    def g_spec(l, r0):
        return pl.BlockSpec((tr, C), lambda i: (r0 // tr + jnp.clip(i - l * nb, 0, nb - 1), 0))

    blk = pl.BlockSpec((tr, C), lambda i: (i, 0))
    out = jax.ShapeDtypeStruct((R, C), F32)
    return pl.pallas_call(
        body, name=name, grid=(R // tr,),
        in_specs=[blk] + [g_spec(l, r0) for l, (_, r0) in enumerate(gsrc)] + [blk, blk], out_specs=[blk] * 4,
        out_shape=[out] * 4, compiler_params=_params(("parallel",)),
    )(w, *[g for g, _ in gsrc], m, v)


def _adamw_small(ws, gs, ms, vs, *, name):
    n = len(ws)

    def body(*refs):
        w_r, g_r, m_r, v_r = refs[:n], refs[n:2 * n], refs[2 * n:3 * n], refs[3 * n:4 * n]
        d_r, nm_r, nv_r = refs[4 * n:5 * n], refs[5 * n:6 * n], refs[6 * n:7 * n]
        for k in range(n):
            d_r[k][...], nm_r[k][...], nv_r[k][...] = _adamw_math(w_r[k][...], g_r[k][...], m_r[k][...], v_r[k][...])

    vm = pl.BlockSpec(memory_space=pltpu.VMEM)
    out = [jax.ShapeDtypeStruct(w.shape, F32) for w in ws]
    res = pl.pallas_call(body, name=name, in_specs=[vm] * (4 * n), out_specs=[vm] * (3 * n), out_shape=out * 3,
                         compiler_params=pltpu.CompilerParams(vmem_limit_bytes=VMEM_LIMIT))(*ws, *gs, *ms, *vs)
    return res[:n], res[n:2 * n], res[2 * n:]


def _sum_slots(x, *, name):
    n, R, C = x.shape
    tr = R if (n + 1) * R * C * 4 <= _ONE_BLOCK_BYTES else _tile(R, 256, 8)

    def body(x_ref, o_ref):
        acc = x_ref[0]
        for k in range(1, n):
            acc = acc + x_ref[k]
        o_ref[...] = acc

    return pl.pallas_call(
        body, name=name, grid=(R // tr,),
        in_specs=[pl.BlockSpec((n, tr, C), lambda i: (0, i, 0))], out_specs=pl.BlockSpec((tr, C), lambda i: (i, 0)),
        out_shape=jax.ShapeDtypeStruct((R, C), F32), compiler_params=_params(("parallel",)),
    )(x)


def _pair_sum(g, r, where, *, name):
    n, R, C = g.shape
    Rh = R // 2
    tr = _tile(Rh, 256, 8)
    nb = Rh // tr

    def body(where_ref, g_ref, r_ref, o_ref):
        o_ref[...] = (g_ref[...] + r_ref[...]).astype(BF16)

    def slot(p, w):
        return p + jnp.where(p >= w[0], 1, 0)

    return pl.pallas_call(
        body, name=name,
        grid_spec=pltpu.PrefetchScalarGridSpec(
            num_scalar_prefetch=1, grid=(n - 1, nb),
            in_specs=[pl.BlockSpec((1, tr, C), lambda p, i, w: (slot(p, w), w[1] * nb + i, 0)),
                      pl.BlockSpec((1, tr, C), lambda p, i, w: (slot(p, w), i, 0))],
            out_specs=pl.BlockSpec((1, tr, C), lambda p, i, w: (slot(p, w), i, 0)),
        ),
        out_shape=jax.ShapeDtypeStruct((n, Rh, C), BF16), compiler_params=_params(("parallel", "parallel")),
    )(where, g, r)


def _chip_sum(g, r, slots, where, *, name):
    n, R, C = g.shape
    Rh = R // 2
    tr = _tile(Rh, 256, 8)
    nb = Rh // tr

    def body(w_ref, g_ref, r_ref, s_ref, o_ref):
        acc = g_ref[0] + r_ref[0]
        for k in range(slots.shape[0]):
            acc = acc + s_ref[k].astype(F32)
        o_ref[...] = acc

    return pl.pallas_call(
        body, name=name,
        grid_spec=pltpu.PrefetchScalarGridSpec(
            num_scalar_prefetch=1, grid=(nb,),
            in_specs=[pl.BlockSpec((1, tr, C), lambda i, w: (w[0], w[1] * nb + i, 0)),
                      pl.BlockSpec((1, tr, C), lambda i, w: (w[0], i, 0)),
                      pl.BlockSpec((slots.shape[0], tr, C), lambda i, w: (0, i, 0))],
            out_specs=pl.BlockSpec((tr, C), lambda i, w: (w[1] * nb + i, 0)),
        ),
        out_shape=jax.ShapeDtypeStruct((R, C), F32), compiler_params=_params(("parallel",)),
    )(where, g, r, slots)


ANY = pl.BlockSpec(memory_space=pl.ANY)


def _place():
    return lax.axis_index("x"), lax.axis_index("y"), lax.axis_index("c")


def _other_chips(x, y):
    return [(1 - x, y), (x, 1 - y), (1 - x, 1 - y)]


def _aliased_comm_call(body, bufs, n_sems, *, name):
    n = len(bufs)
    return pl.pallas_call(
        body, name=name, out_shape=[jax.ShapeDtypeStruct(b.shape, b.dtype) for b in bufs],
        in_specs=[ANY] * n, out_specs=[ANY] * n, input_output_aliases={k: k for k in range(n)},
        scratch_shapes=[pltpu.SemaphoreType.DMA((n_sems,)), pltpu.SemaphoreType.DMA((n_sems,))],
    )(*bufs)


HBM = pl.BlockSpec(memory_space=pltpu.HBM)
SEM = pl.BlockSpec(memory_space=pltpu.SEMAPHORE)
_SPLIT = pltpu.CompilerParams(has_side_effects=pltpu.SideEffectType.DATAFLOW_SIDE_EFFECTING)


def _in_hbm(arrs):
    return [pltpu.with_memory_space_constraint(a, pltpu.HBM) for a in arrs]


def _gather_ici_start(bufs, after, *, name):
    n = len(bufs)

    def body(*refs):
        send_sems, recv_sems, outs, token = refs[n + 1], refs[n + 2], refs[n + 3:2 * n + 3], refs[2 * n + 3]
        x, y, c = _place()
        for b in range(n):
            rh = bufs[b].shape[1] // 2
            part = outs[b].at[2 * x + y, pl.ds(c * rh, rh), :]
            for j, chip in enumerate(_other_chips(x, y)):
                pltpu.make_async_remote_copy(src_ref=part, dst_ref=part, send_sem=send_sems.at[3 * b + j],
                                             recv_sem=recv_sems.at[3 * b + j], device_id=(*chip, c),
                                             device_id_type=MESH).start()
        token[...] = jnp.zeros_like(token)

    res = pl.pallas_call(
        body, name=name,
        out_shape=(pltpu.SemaphoreType.DMA((3 * n,)), pltpu.SemaphoreType.DMA((3 * n,)),
                   *[pltpu.HBM(b.shape, b.dtype) for b in bufs], jax.ShapeDtypeStruct((8, 128), F32)),
        in_specs=[HBM] * n + [ANY], out_specs=(SEM, SEM, *[HBM] * n, pl.BlockSpec(memory_space=pltpu.VMEM)),
        input_output_aliases={k: k + 2 for k in range(n)}, compiler_params=_SPLIT,
    )(*_in_hbm(bufs), after)
    return res[0], res[1], list(res[2:2 + n]), res[2 + n]


def _gather_ici_wait(send_sems, recv_sems, bufs, first, after, *, name):
    n = len(bufs)

    def body(*refs):
        ins, ss, rs = refs[:n], refs[n], refs[n + 1]
        x, y, c = _place()
        for b in range(n):
            rh = bufs[b].shape[1] // 2
            mine = ins[b].at[2 * x + y, pl.ds(c * rh, rh), :]
            for j, (cx, cy) in enumerate(_other_chips(x, y)):
                theirs = ins[b].at[2 * cx + cy, pl.ds(c * rh, rh), :]
                cp = pltpu.make_async_remote_copy(src_ref=mine, dst_ref=theirs, send_sem=ss.at[3 * (first + b) + j],
                                                  recv_sem=rs.at[3 * (first + b) + j], device_id=(cx, cy, c),
                                                  device_id_type=MESH)
                cp.wait_send()
                cp.wait_recv()

    return list(pl.pallas_call(
        body, name=name, out_shape=[pltpu.HBM(b.shape, b.dtype) for b in bufs],
        in_specs=[HBM] * n + [SEM, SEM, ANY], out_specs=[HBM] * n,
        input_output_aliases={k: k for k in range(n)}, compiler_params=_SPLIT,
    )(*bufs, send_sems, recv_sems, after))


def _gather_forward(bufs, *, name):
    n = len(bufs)

    def body(*refs):
        outs, send_sems, recv_sems = refs[n:2 * n], refs[2 * n], refs[2 * n + 1]
        x, y, c = _place()

        def copy(b, j, chip, hc):
            rh = bufs[b].shape[1] // 2
            part = outs[b].at[2 * chip[0] + chip[1], pl.ds(hc * rh, rh), :]
            return pltpu.make_async_remote_copy(src_ref=part, dst_ref=part, send_sem=send_sems.at[3 * b + j],
                                                recv_sem=recv_sems.at[3 * b + j], device_id=(x, y, 1 - c),
                                                device_id_type=MESH)

        sends = [copy(b, j, chip, c) for b in range(n) for j, chip in enumerate(_other_chips(x, y))]
        for cp in sends:
            cp.start()
        for b in range(n):
            for j, chip in enumerate(_other_chips(x, y)):
                copy(b, j, chip, 1 - c).wait_recv()
        for cp in sends:
            cp.wait_send()

    return _aliased_comm_call(body, bufs, 3 * n, name=name)


def _forward_copy(buf, send_sems, recv_sems, k, chip, half, to):
    rh = buf.shape[1] // 2
    part = buf.at[2 * chip[0] + chip[1], pl.ds(half * rh, rh), :]
    return pltpu.make_async_remote_copy(src_ref=part, dst_ref=part, send_sem=send_sems.at[k], recv_sem=recv_sems.at[k],
                                        device_id=to, device_id_type=MESH)


def _gather_forward_start(bufs, after, *, name):
    n = len(bufs)

    def body(*refs):
        send_sems, recv_sems, outs, token = refs[n + 1], refs[n + 2], refs[n + 3:2 * n + 3], refs[2 * n + 3]
        x, y, c = _place()
        for b in range(n):
            for j, chip in enumerate(_other_chips(x, y)):
                _forward_copy(outs[b], send_sems, recv_sems, 3 * b + j, chip, c, (x, y, 1 - c)).start()
        token[...] = jnp.zeros_like(token)

    res = pl.pallas_call(
        body, name=name,
        out_shape=(pltpu.SemaphoreType.DMA((3 * n,)), pltpu.SemaphoreType.DMA((3 * n,)),
                   *[pltpu.HBM(b.shape, b.dtype) for b in bufs], jax.ShapeDtypeStruct((8, 128), F32)),
        in_specs=[HBM] * n + [ANY], out_specs=(SEM, SEM, *[HBM] * n, pl.BlockSpec(memory_space=pltpu.VMEM)),
        input_output_aliases={k: k + 2 for k in range(n)}, compiler_params=_SPLIT,
    )(*_in_hbm(bufs), after)
    return res[0], res[1], list(res[2:2 + n]), res[2 + n]


def _gather_forward_wait(send_sems, recv_sems, bufs, after, *, name):
    n = len(bufs)

    def body(*refs):
        ins, ss, rs = refs[:n], refs[n], refs[n + 1]
        x, y, c = _place()
        for b in range(n):
            for j, chip in enumerate(_other_chips(x, y)):
                _forward_copy(ins[b], ss, rs, 3 * b + j, chip, c, (x, y, 1 - c)).wait_send()
                _forward_copy(ins[b], ss, rs, 3 * b + j, chip, 1 - c, (x, y, 1 - c)).wait_recv()

    return list(pl.pallas_call(
        body, name=name, out_shape=[pltpu.HBM(b.shape, b.dtype) for b in bufs],
        in_specs=[HBM] * n + [SEM, SEM, ANY], out_specs=[HBM] * n,
        input_output_aliases={k: k for k in range(n)}, compiler_params=_SPLIT,
    )(*bufs, send_sems, recv_sems, after))


def _chip_exchange_start(hs, *, name):
    n = len(hs)
    lands = [lax.empty((3,) + h.shape[1:], h.dtype) for h in hs]

    def body(*refs):
        send_sems, recv_sems = refs[2 * n], refs[2 * n + 1]
        h_out, l_out, token = refs[2 * n + 2:3 * n + 2], refs[3 * n + 2:4 * n + 2], refs[4 * n + 2]
        x, y, c = _place()
        for b in range(n):
            for j, (cx, cy) in enumerate(_other_chips(x, y)):
                pltpu.make_async_remote_copy(src_ref=h_out[b].at[2 * cx + cy], dst_ref=l_out[b].at[j],
                                             send_sem=send_sems.at[3 * b + j], recv_sem=recv_sems.at[3 * b + j],
                                             device_id=(cx, cy, c), device_id_type=MESH).start()
        token[...] = jnp.zeros_like(token)

    res = pl.pallas_call(
        body, name=name,
        out_shape=(pltpu.SemaphoreType.DMA((3 * n,)), pltpu.SemaphoreType.DMA((3 * n,)),
                   *[pltpu.HBM(a.shape, a.dtype) for a in hs + lands], jax.ShapeDtypeStruct((8, 128), F32)),
        in_specs=[HBM] * (2 * n), out_specs=(SEM, SEM, *[HBM] * (2 * n), pl.BlockSpec(memory_space=pltpu.VMEM)),
        input_output_aliases={k: k + 2 for k in range(2 * n)}, compiler_params=_SPLIT,
    )(*_in_hbm(hs + lands))
    return res[0], res[1], list(res[2:2 + n]), list(res[2 + n:2 + 2 * n]), res[2 + 2 * n]


def _chip_exchange_wait(send_sems, recv_sems, hs, lands, after, *, name):
    n = len(hs)

    def body(*refs):
        h_in, l_in, ss, rs = refs[:n], refs[n:2 * n], refs[2 * n], refs[2 * n + 1]
        x, y, c = _place()
        for b in range(n):
            for j, (cx, cy) in enumerate(_other_chips(x, y)):
                cp = pltpu.make_async_remote_copy(src_ref=h_in[b].at[2 * cx + cy], dst_ref=l_in[b].at[j],
                                                  send_sem=ss.at[3 * b + j], recv_sem=rs.at[3 * b + j],
                                                  device_id=(cx, cy, c), device_id_type=MESH)
                cp.wait_send()
                cp.wait_recv()

    res = pl.pallas_call(
        body, name=name, out_shape=[pltpu.HBM(a.shape, a.dtype) for a in hs + lands],
        in_specs=[HBM] * (2 * n) + [SEM, SEM, ANY], out_specs=[HBM] * (2 * n),
        input_output_aliases={k: k for k in range(2 * n)}, compiler_params=_SPLIT,
    )(*hs, *lands, send_sems, recv_sems, after)
    return list(res[n:])


def _peers(x, y, c):
    return [((1 - x) if fx else x, (1 - y) if fy else y, (1 - c) if fc else c)
            for fx in (0, 1) for fy in (0, 1) for fc in (0, 1) if fx or fy or fc]


def _all_to_all_start(slab, after, *, name):
    land = lax.empty((N_DEV,) + slab.shape, slab.dtype)

    def body(slab_in, land_in, after_ref, send_sems, recv_sems, slab_out, land_out, token):
        x, y, c = _place()
        for k, peer in enumerate(_peers(x, y, c)):
            pltpu.make_async_remote_copy(src_ref=slab_out, dst_ref=land_out.at[4 * x + 2 * y + c],
                                         send_sem=send_sems.at[k], recv_sem=recv_sems.at[k], device_id=peer,
                                         device_id_type=MESH).start()
        token[...] = jnp.zeros_like(token)

    return pl.pallas_call(
        body, name=name,
        out_shape=(pltpu.SemaphoreType.DMA((N_DEV - 1,)), pltpu.SemaphoreType.DMA((N_DEV - 1,)),
                   pltpu.HBM(slab.shape, slab.dtype), pltpu.HBM(land.shape, land.dtype),
                   jax.ShapeDtypeStruct((8, 128), F32)),
        in_specs=[HBM, HBM, ANY], out_specs=(SEM, SEM, HBM, HBM, pl.BlockSpec(memory_space=pltpu.VMEM)),
        input_output_aliases={0: 2, 1: 3}, compiler_params=_SPLIT,
    )(*_in_hbm([slab, land]), after)


def _all_to_all_wait(send_sems, recv_sems, slab, land, afters, *, name):
    def body(slab_in, land_in, ss, rs, *_):
        x, y, c = _place()
        for k, (px, py, pc) in enumerate(_peers(x, y, c)):
            cp = pltpu.make_async_remote_copy(src_ref=slab_in, dst_ref=land_in.at[4 * px + 2 * py + pc],
                                              send_sem=ss.at[k], recv_sem=rs.at[k], device_id=(px, py, pc),
                                              device_id_type=MESH)
            cp.wait_send()
            cp.wait_recv()

    return pl.pallas_call(
        body, name=name, out_shape=[pltpu.HBM(slab.shape, slab.dtype), pltpu.HBM(land.shape, land.dtype)],
        in_specs=[HBM, HBM, SEM, SEM] + [ANY] * len(afters), out_specs=[HBM, HBM], input_output_aliases={0: 0, 1: 1},
        compiler_params=_SPLIT,
    )(slab, land, send_sems, recv_sems, *afters)


def _pair_exchange_start(gs, *, name):
    n = len(gs)
    lands = [lax.empty((g.shape[0], g.shape[1] // 2, g.shape[2]), g.dtype) for g in gs]

    def body(*refs):
        send_sems, recv_sems = refs[2 * n], refs[2 * n + 1]
        g_out, l_out, token = refs[2 * n + 2:3 * n + 2], refs[3 * n + 2:4 * n + 2], refs[4 * n + 2]
        x, y, c = _place()
        for b in range(n):
            rh = gs[b].shape[1] // 2
            pltpu.make_async_remote_copy(src_ref=g_out[b].at[:, pl.ds((1 - c) * rh, rh), :], dst_ref=l_out[b],
                                         send_sem=send_sems.at[b], recv_sem=recv_sems.at[b],
                                         device_id=(x, y, 1 - c), device_id_type=MESH).start()
        token[...] = jnp.zeros_like(token)

    res = pl.pallas_call(
        body, name=name,
        out_shape=(pltpu.SemaphoreType.DMA((n,)), pltpu.SemaphoreType.DMA((n,)),
                   *[pltpu.HBM(a.shape, a.dtype) for a in gs + lands], jax.ShapeDtypeStruct((8, 128), F32)),
        in_specs=[HBM] * (2 * n), out_specs=(SEM, SEM, *[HBM] * (2 * n), pl.BlockSpec(memory_space=pltpu.VMEM)),
        input_output_aliases={k: k + 2 for k in range(2 * n)}, compiler_params=_SPLIT,
    )(*_in_hbm(gs + lands))
    return res[0], res[1], list(res[2:2 + n]), list(res[2 + n:2 + 2 * n]), res[2 + 2 * n]


def _pair_exchange_wait(send_sems, recv_sems, gs, lands, after, *, name):
    n = len(gs)

    def body(*refs):
        g_in, l_in, ss, rs = refs[:n], refs[n:2 * n], refs[2 * n], refs[2 * n + 1]
        x, y, c = _place()
        for b in range(n):
            rh = gs[b].shape[1] // 2
            cp = pltpu.make_async_remote_copy(src_ref=g_in[b].at[:, pl.ds((1 - c) * rh, rh), :], dst_ref=l_in[b],
                                              send_sem=ss.at[b], recv_sem=rs.at[b], device_id=(x, y, 1 - c),
                                              device_id_type=MESH)
            cp.wait_send()
            cp.wait_recv()

    res = pl.pallas_call(
        body, name=name, out_shape=[pltpu.HBM(a.shape, a.dtype) for a in gs + lands],
        in_specs=[HBM] * (2 * n) + [SEM, SEM, ANY], out_specs=[HBM] * (2 * n),
        input_output_aliases={k: k for k in range(2 * n)}, compiler_params=_SPLIT,
    )(*gs, *lands, send_sems, recv_sems, after)
    return list(res[:n]), list(res[n:])


def _pair_share_start(ss, *, name):
    n = len(ss)

    def body(*refs):
        send_sems, recv_sems, outs, token = refs[n], refs[n + 1], refs[n + 2:2 * n + 2], refs[2 * n + 2]
        x, y, c = _place()
        for b in range(n):
            rh = ss[b].shape[0] // 2
            mine = outs[b].at[pl.ds(c * rh, rh), :]
            pltpu.make_async_remote_copy(src_ref=mine, dst_ref=mine, send_sem=send_sems.at[b],
                                         recv_sem=recv_sems.at[b], device_id=(x, y, 1 - c),
                                         device_id_type=MESH).start()
        token[...] = jnp.zeros_like(token)

    res = pl.pallas_call(
        body, name=name,
        out_shape=(pltpu.SemaphoreType.DMA((n,)), pltpu.SemaphoreType.DMA((n,)),
                   *[pltpu.HBM(a.shape, a.dtype) for a in ss], jax.ShapeDtypeStruct((8, 128), F32)),
        in_specs=[HBM] * n, out_specs=(SEM, SEM, *[HBM] * n, pl.BlockSpec(memory_space=pltpu.VMEM)),
        input_output_aliases={k: k + 2 for k in range(n)}, compiler_params=_SPLIT,
    )(*_in_hbm(ss))
    return res[0], res[1], list(res[2:2 + n]), res[2 + n]


def _pair_share_wait(send_sems, recv_sems, ss, after, *, name):
    n = len(ss)

    def body(*refs):
        ins, sems_s, sems_r = refs[:n], refs[n], refs[n + 1]
        x, y, c = _place()
        for b in range(n):
            rh = ss[b].shape[0] // 2
            mine = ins[b].at[pl.ds(c * rh, rh), :]
            theirs = ins[b].at[pl.ds((1 - c) * rh, rh), :]
            cp = pltpu.make_async_remote_copy(src_ref=mine, dst_ref=theirs, send_sem=sems_s.at[b],
                                              recv_sem=sems_r.at[b], device_id=(x, y, 1 - c),
                                              device_id_type=MESH)
            cp.wait_send()
            cp.wait_recv()

    return list(pl.pallas_call(
        body, name=name, out_shape=[pltpu.HBM(a.shape, a.dtype) for a in ss],
        in_specs=[HBM] * n + [SEM, SEM, ANY], out_specs=[HBM] * n,
        input_output_aliases={k: k for k in range(n)}, compiler_params=_SPLIT,
    )(*ss, send_sems, recv_sems, after))


_SMALL_SHARDED = (("e_conv_w", 2), ("o_norm", 1), ("o_d", 1))
_REPLICATED = ("e_norm", "e_gmlp_w", "e_gmlp_b", "e_conv_b", "e_conv_ln_g", "e_conv_ln_b", "o_lam_re", "o_lam_im",
               "o_log_dt", "o_b_re", "o_b_im", "o_c_re", "o_c_im", "ca_norm", "ca_mem_norm", "ffn_norm", "final_norm")
_SMALL = tuple(n for n, _ in _SMALL_SHARDED) + _REPLICATED
_WEIGHTS = ("e_norm", "e_w_in", "e_gmlp_w", "e_gmlp_b", "e_conv_w", "e_conv_b", "e_conv_ln_g", "e_conv_ln_b",
            "e_w_out", "o_norm", "o_w_in", "o_lam_re", "o_lam_im", "o_log_dt", "o_b_re", "o_b_im", "o_c_re", "o_c_im",
            "o_d", "o_w_out", "ca_norm", "ca_mem_norm", "ca_wq", "ca_wk", "ca_wv", "ca_wo", "ffn_norm", "ffn_w_gate",
            "ffn_w_up", "ffn_w_down", "final_norm")


def _pack_rows(arrs, width, dtype, row_mult=8):
    parts, spans, r0 = [], [], 0
    for a in arrs:
        flat = a.reshape(-1).astype(dtype)
        rows = -(-flat.shape[0] // (width * row_mult)) * row_mult
        if rows * width != flat.shape[0]:
            flat = jnp.pad(flat, (0, rows * width - flat.shape[0]))
        parts.append(flat.reshape(rows, width))
        spans.append((r0, rows))
        r0 += rows
    return jnp.concatenate(parts, axis=0), spans


def _unpack_rows(slab, spans, shapes):
    out = []
    for (r0, rows), shp in zip(spans, shapes):
        n = math.prod(shp)
        out.append(slab[r0:r0 + rows].reshape(-1)[:n].reshape(shp))
    return out


def _two_d(a):
    return a.reshape(-1, a.shape[-1])


def _shard_rows(n, a):
    return _two_d(jnp.swapaxes(a, -1, -2) if n in _TRANSPOSED else a)


def _from_shard_rows(n, rows, shape):
    if n in _TRANSPOSED:
        return jnp.swapaxes(rows.reshape(shape[:-2] + (shape[-1], shape[-2])), -1, -2)
    return rows.reshape(shape)


def _local_slab(local, slab, dtype):
    parts = sorted((r0, n, l) for n, (_, where) in _PLACE.items() for l, (s, r0) in enumerate(where) if s == slab)
    shards = [_shard_rows(n, local[n] if len(_PLACE[n][1]) == 1 else local[n][l]) for _, n, l in parts]
    return jnp.concatenate([a.astype(dtype) for a in shards], axis=0)


def _set_diag(b, pattern):
    return jnp.einsum(pattern, b, jnp.eye(C_GROUPS // N_SETS, dtype=b.dtype))


def _s5_discretize(lam_re, lam_im, log_dt, b_re, b_im):
    dt = jnp.exp(log_dt)[:, None]
    mag = jnp.exp(lam_re * dt)
    ar = mag * jnp.cos(lam_im * dt)
    ai = mag * jnp.sin(lam_im * dt)
    den = lam_re * lam_re + lam_im * lam_im
    qr = ((ar - 1.0) * lam_re + ai * lam_im) / den
    qi = (ai * lam_re - (ar - 1.0) * lam_im) / den
    bbr = qr[..., None] * b_re - qi[..., None] * b_im
    bbi = qr[..., None] * b_im + qi[..., None] * b_re
    return ar, ai, bbr, bbi


def _attention_block(x, mem, W, w, i, tag):
    xn, q = _norm_mm(x, w["ca_norm"][i], _shards(W, "ca_wq", i), split="k", out_dtype=BF16, name=f"{tag}_q")
    memn = _rms_fwd(mem, w["ca_mem_norm"][i], name=f"{tag}_ca_memnorm")
    k = _mm_k(memn, _shards(W, "ca_wk", i), out_dtype=BF16, name=f"{tag}_k")
    v = _mm_k(memn, _shards(W, "ca_wv", i), out_dtype=BF16, name=f"{tag}_v")
    o = _attn_fwd(q, k, v, name=f"{tag}_attn")
    y = _mm_k(o, _shards(W, "ca_wo", i), add=x, name=f"{tag}_wo")
    return y, (x, xn, memn, q, k, v, o)


def _attention_block_bwd(dy, saved, mem, W, w, i, tag, G, grads, token=None, mid=None):
    x, xn, memn, q, k, v, o = saved
    gain = w["ca_norm"][i]
    if token is not None:
        k = _behind(k, token)
    G = _grad_to_slab(G, "ca_wo", i, o, dy, a_cols=256, name=f"{tag}_dwo")
    dq, dk, dv = _attn_bwd(dy, _shards(W, "ca_wo", i), q, k, v, name=f"{tag}_attn_bwd")
    token = mid(dq) if mid is not None else None
    if token is not None:
        gain = _behind(gain, token)
    G = _grad_to_slab(G, "ca_wq", i, xn, dq, a_cols=256, name=f"{tag}_dwq")
    G = _grad_to_slab(G, "ca_wk", i, memn, dk, a_cols=256, name=f"{tag}_dwk")
    G = _grad_to_slab(G, "ca_wv", i, memn, dv, a_cols=256, name=f"{tag}_dwv")
    dmemn = _mm_k_t([(dk, _shards(W, "ca_wk", i)), (dv, _shards(W, "ca_wv", i))], name=f"{tag}_dmemn")
    dx, dg = _norm_bwd_k(dq, _shards(W, "ca_wq", i), x, gain, dy, name=f"{tag}_dq_norm_bwd")
    grads["ca_norm"][i] = dg[0]
    grads["ca_mem_norm"][i] = _rms_dg(mem, w["ca_mem_norm"][i], dmemn, name=f"{tag}_ca_memnorm_bwd")[0]
    return dx, G


def _ffn_block(x, W, w, i, tag, head=None):
    fn, gate, up, h = _ffn_up(x, w["ffn_norm"][i], _shards(W, "ffn_w_gate", i), _shards(W, "ffn_w_up", i),
                              name=f"{tag}_ffn_up")
    if head is None:
        y = _mm_k(h, _shards(W, "ffn_w_down", i), add=x, name=f"{tag}_down")
    else:
        y = _down_loss_head(h, _shards(W, "ffn_w_down", i), x, *head, name=f"{tag}_down_loss_head")
    return y, (x, fn, gate, up, h)


def _ffn_block_bwd(dy, saved, W, w, i, tag, G, grads, token=None, mid=None):
    x, fn, gate, up, h = saved
    gain = w["ffn_norm"][i]
    G = _grad_to_slab(G, "ffn_w_down", i, h, dy, name=f"{tag}_dwd")
    dg, du = _ffn_bwd_hidden(dy, _shards(W, "ffn_w_down", i), gate, up, token, name=f"{tag}_ffn_bwd_hidden")
    token = mid(dg) if mid is not None else None
    if token is not None:
        gain = _behind(gain, token)
    G = _grad_to_slab(G, "ffn_w_gate", i, dg, fn, name=f"{tag}_dwg")
    G = _grad_to_slab(G, "ffn_w_up", i, du, fn, name=f"{tag}_dwu")
    dx, dgn = _ffn_in_bwd(dg, du, _shards(W, "ffn_w_gate", i), _shards(W, "ffn_w_up", i), x, gain, dy,
                          name=f"{tag}_ffn_in_bwd")
    grads["ffn_norm"][i] = dgn[0]
    return dx, G


def _gmlp_mask():
    chunk = jnp.arange(GMLP_BLOCK) // CHUNK
    return chunk[None, :] <= chunk[:, None]


def _even_block(x, W, w, tag):
    hn, proj = _norm_mm(x, w["e_norm"][0], _shards(W, "e_w_in"), split="n", out_dtype=F32, name=f"{tag}_w_in")
    wm = jnp.where(_gmlp_mask()[None], w["e_gmlp_w"][0], 0.0).astype(BF16)
    bcol = w["e_gmlp_b"][0][:, :, None]
    cw = jnp.pad(w["e_conv_w"][0], ((0, CONV_HALO - CONV_WIDTH), (0, 0)))
    cb, lg, lb = w["e_conv_b"], w["e_conv_ln_g"], w["e_conv_ln_b"]
    mix, hc = _even_fwd(proj, wm, bcol, cw, cb, lg, lb, name=f"{tag}_mixers")
    y = _mm_k(mix, _shards(W, "e_w_out"), add=x, name=f"{tag}_w_out")
    return y, (x, hn, proj, mix, hc, wm, bcol, cw)


def _even_block_bwd(dy, saved, W, w, tag, G, grads):
    x, hn, proj, mix, hc, wm, bcol, cw = saved
    dmix = _mm_k_t([(dy, _shards(W, "e_w_out"))], name=f"{tag}_dmix")
    G = _grad_to_slab(G, "e_w_out", 0, mix, dy, a_cols=256, name=f"{tag}_dw_out")
    wmt = jnp.swapaxes(wm, 1, 2)
    dpa, dhc, dwm, db, dlg, dlb, dcb = _even_bwd1(proj, dmix, hc, wm, wmt, bcol, w["e_conv_ln_g"], w["e_conv_ln_b"],
                                                  name=f"{tag}_mixers_bwd1")
    dpb, dcw = _even_bwd2(proj, dhc, cw, name=f"{tag}_mixers_bwd2")
    grads["e_gmlp_w"] = jnp.where(_gmlp_mask()[None], dwm, 0.0)[None]
    grads["e_gmlp_b"] = db[:, :, 0][None]
    grads["e_conv_ln_g"], grads["e_conv_ln_b"], grads["e_conv_b"] = dlg, dlb, dcb
    grads["e_conv_w"] = dcw[:CONV_WIDTH][None]
    G = _grad_to_slab(G, "e_w_in", 0, hn, dpa, b_cols=512, chips=(0, 2), name=f"{tag}_dw_in_a")
    G = _grad_to_slab(G, "e_w_in", 0, hn, dpb, b_cols=512, chips=(2, 2), name=f"{tag}_dw_in_b")
    dx, dg = _norm_bwd_n((dpa, dpb), _shards(W, "e_w_in"), x, w["e_norm"][0], dy, name=f"{tag}_in_bwd")
    grads["e_norm"] = dg
    return dx, G


def _odd_block(x, W, w, tag):
    S = x.shape[0]
    hn, u = _norm_mm(x, w["o_norm"][0], _shards(W, "o_w_in"), split="k", out_dtype=F32, name=f"{tag}_w_in")
    disc_in = (w["o_lam_re"][0], w["o_lam_im"][0], w["o_log_dt"][0], w["o_b_re"][0], w["o_b_im"][0])
    (ar, ai, bbr, bbi), disc_vjp = jax.vjp(_s5_discretize, *disc_in)
    sets = (N_SETS, C_GROUPS // N_SETS)
    per_set = N_STATE // N_SETS
    bset = jnp.concatenate([_set_diag(b.reshape(sets + b.shape[1:]), "jgpc,gh->jgchp").reshape(N_SETS, SET_CH, per_set)
                            for b in (bbr, bbi)], axis=2).astype(BF16)
    cset = jnp.concatenate([_set_diag(c.reshape(sets + c.shape[1:]), "jgcp,gh->jgphc").reshape(N_SETS, per_set, SET_CH)
                            for c in (w["o_c_re"][0], -w["o_c_im"][0])], axis=1).astype(BF16)
    powers, pr, pi = [], ar, ai
    for _ in range(SCAN_BLOCK):
        powers.append(jnp.concatenate([pr.reshape(STATE_ROWS, STATE_LANES), pi.reshape(STATE_ROWS, STATE_LANES)], 0))
        pr, pi = pr * ar - pi * ai, pr * ai + pi * ar
    pw = jnp.stack(powers, axis=0)
    xs = _scan_fwd(u, bset, pw, name=f"{tag}_scan").reshape(S // 8, STATE_ROWS, 8, STATE_LANES)
    yv, yg = _s5_readout(xs, cset, u, w["o_d"], name=f"{tag}_readout")
    o, y = _glu_out(yg, _shards(W, "o_w_out"), x, name=f"{tag}_glu_out")
    return y, (x, hn, u, bset, cset, pw, xs, yv, yg, o, disc_vjp)


def _odd_block_bwd(dy, saved, W, w, tag, G, grads):
    x, hn, u, bset, cset, pw, xs, yv, yg, o, disc_vjp = saved
    S = x.shape[0]
    do, dys, dus, dd = _glu_out_bwd(o, dy, _shards(W, "o_w_out"), yv, u, w["o_d"], name=f"{tag}_glu_out_bwd")
    G = _grad_to_slab(G, "o_w_out", 0, yg, do, b_cols=512, name=f"{tag}_dw_out")
    grads["o_d"] = dd
    dcset_t = _state_grad_sets(dys, xs, name=f"{tag}_dcd")
    gs, da = _scan_bwd(dys, cset, xs.reshape(S * STATE_ROWS, STATE_LANES), pw, name=f"{tag}_scan_bwd")
    gs = gs.reshape(xs.shape)
    dbset = _state_grad_sets(u, gs, name=f"{tag}_dbd")
    du, dx, dg = _s5_in_bwd(gs, bset, dus, _shards(W, "o_w_in"), x, w["o_norm"][0], dy, name=f"{tag}_in_bwd")
    G = _grad_to_slab(G, "o_w_in", 0, hn, du, a_cols=256, name=f"{tag}_dw_in")
    grads["o_norm"] = dg
    per = C_GROUPS // N_SETS
    blocks = (N_SETS, per, C_GROUP_CH, 2, per, C_STATE)
    dc = _set_diag(dcset_t.reshape(blocks), "jhcrgp,gh->rjgcp").reshape(2, C_GROUPS, C_GROUP_CH, C_STATE)
    db = _set_diag(dbset.reshape(blocks), "jgcrhp,gh->rjgpc").reshape(2, C_GROUPS, C_STATE, C_GROUP_CH)
    dcr, dci, dbbr, dbbi = dc[0], -dc[1], db[0], db[1]
    dar = da[:STATE_ROWS].reshape(C_GROUPS, C_STATE)
    dai = da[STATE_ROWS:].reshape(C_GROUPS, C_STATE)
    dlr, dli, dldt, dbr, dbi = disc_vjp((dar, dai, dbbr, dbbi))
    grads["o_lam_re"], grads["o_lam_im"], grads["o_log_dt"] = dlr[None], dli[None], dldt[None]
    grads["o_b_re"], grads["o_b_im"], grads["o_c_re"], grads["o_c_im"] = dbr[None], dbi[None], dcr[None], dci[None]
    return dx, G


def _behind(value, token):
    return value + token[0, 0].astype(value.dtype)


class _NoExchange:
    def __init__(self, W):
        self.W = W

    def first_weights(self, w):
        return self.W, w

    def weights(self, stage, after):
        return {}

    def behind_late_start(self, w):
        return w

    def late_weights(self, after):
        return {}

    def grads_ready(self, piece, G):
        return None

    def grads_crossed(self, piece, after):
        return None


def _forward_backward(xs_, mems_, tgt, w, G, exchange):
    W, w = exchange.first_weights(w)
    x1, s_mix0 = _even_block(xs_, W, w, "l0")
    W = {**W, **exchange.weights(1, x1)}
    x2, s_att0 = _attention_block(x1, mems_, W, w, 0, "l0")
    W = {**W, **exchange.weights(2, x2)}
    x3, s_ffn0 = _ffn_block(x2, W, w, 0, "l0")
    W = {**W, **exchange.weights(3, x3)}
    w = exchange.behind_late_start(w)
    x4, s_mix1 = _odd_block(x3, W, w, "l1")
    W = {**W, **exchange.late_weights(x4)}
    x5, s_att1 = _attention_block(x4, mems_, W, w, 1, "l1")
    (dx, dfinal, loss_lanes), s_ffn1 = _ffn_block(x5, W, w, 1, "l1", head=(w["final_norm"], tgt))

    grads = {n: [None, None] for n in ("ca_norm", "ca_mem_norm", "ffn_norm")}
    grads["final_norm"] = dfinal[0]
    dx, G = _ffn_block_bwd(dx, s_ffn1, W, w, 1, "l1", G, grads)
    dx, G = _attention_block_bwd(dx, s_att1, mems_, W, w, 1, "l1", G, grads)
    dx, G = _odd_block_bwd(dx, s_mix1, W, w, "l1", G, grads)
    token = exchange.grads_ready("l1", G)
    dx, G = _ffn_block_bwd(dx, s_ffn0, W, w, 0, "l0", G, grads, token,
                           lambda after: exchange.grads_crossed("l1", after))
    token = exchange.grads_ready("ffn0", G)
    dx, G = _attention_block_bwd(dx, s_att0, mems_, W, w, 0, "l0", G, grads, token,
                                 lambda after: exchange.grads_crossed("ffn0", after))
    dx, G = _even_block_bwd(dx, s_mix0, W, w, "l0", G, grads)
    for n in list(grads):
        if isinstance(grads[n], list):
            grads[n] = jnp.stack(grads[n], axis=0)
        grads[n] = grads[n].reshape(w[n].shape)
    return loss_lanes, dx, G, grads


class _Exchange:
    def __init__(self, local, chip, core):
        self.bufs = {s: lax.dynamic_update_slice(lax.empty((N_CHIPS, rows, width), BF16),
                                                 _local_slab(local, s, BF16)[None], (chip, 0, 0))
                     for s, (width, rows) in _SLABS.items()}
        small = jnp.zeros((_SMALL_SLAB_ROWS, SMALL_W), F32)
        for n, (r0, rows) in _SMALL_PLACE.items():
            small = small.at[r0:r0 + rows].set(local[n].reshape(rows, SMALL_W))
        self.bufs[_SMALL_SLAB] = lax.dynamic_update_slice(lax.empty((N_CHIPS, _SMALL_SLAB_ROWS, SMALL_W), F32),
                                                          small[None], (chip, 0, 0))
        self.shard_shapes = {n: local[n].shape for n in _SMALL_PLACE}
        self.where = jnp.stack([chip, core]).astype(jnp.int32)
        self.reduces = {}

    def weights(self, stage, after):
        send_sems, recv_sems, flying = self.flight
        slabs = self.stage_slabs(stage)
        first = list(flying).index(slabs[0])
        bufs = _gather_ici_wait(send_sems, recv_sems, [flying[s] for s in slabs], first, after,
                                name=f"gather_stage{stage}_wait")
        now = [k for k, s in enumerate(slabs) if s not in _LATE_SLABS]
        late = [k for k, s in enumerate(slabs) if s in _LATE_SLABS]
        whole = _gather_forward([bufs[k] for k in now], name=f"gather_stage{stage}_forward")
        if late:
            *state, self.late_token = _gather_forward_start([bufs[k] for k in late], whole[0], name="gather_late_start")
            self.late = ([slabs[k] for k in late], *state)
        return dict(zip([slabs[k] for k in now], whole))

    def behind_late_start(self, w):
        return {**w, "o_norm": _behind(w["o_norm"], self.late_token)}

    def late_weights(self, after):
        slabs, send_sems, recv_sems, bufs = self.late
        return dict(zip(slabs, _gather_forward_wait(send_sems, recv_sems, bufs, after, name="gather_late_wait")))

    @staticmethod
    def stage_slabs(stage):
        return _STAGES[stage] + ((_SMALL_SLAB,) if stage == 0 else ())

    def first_weights(self, w):
        order = [s for k in range(len(_STAGES)) for s in self.stage_slabs(k)]
        send_sems, recv_sems, bufs, after = _gather_ici_start([self.bufs[s] for s in order], w["e_norm"],
                                                              name="gather_start")
        self.flight = (send_sems, recv_sems, dict(zip(order, bufs)))
        W = self.weights(0, after)
        w = {**w, "e_norm": _behind(w["e_norm"], after)}
        for (n, ax), (r0, rows) in zip(_SMALL_SHARDED, _SMALL_PLACE.values()):
            shards = [W[_SMALL_SLAB][p, r0:r0 + rows].reshape(self.shard_shapes[n]) for p in range(N_CHIPS)]
            w[n] = jnp.concatenate(shards, axis=ax)
        return W, w

    def pair_start(self, G, slabs, tag):
        send_sems, recv_sems, gl, lands, token = _pair_exchange_start([G[s] for s in slabs],
                                                                      name=f"grad_{tag}_pair_start")
        return (slabs, send_sems, recv_sems, gl, lands), token

    def pair_land(self, state, after, tag):
        slabs, send_sems, recv_sems, gl, lands = state
        gl, other = _pair_exchange_wait(send_sems, recv_sems, gl, lands, after, name=f"grad_{tag}_pair_wait")
        pairs = [_pair_sum(g, r, self.where, name=f"grad_pair_sum_{s}") for s, g, r in zip(slabs, gl, other)]
        send_sems, recv_sems, pairs, lands, token = _chip_exchange_start(pairs, name=f"grad_{tag}_chip_start")
        return (slabs, gl, other, send_sems, recv_sems, pairs, lands), token

    def reduce_sum(self, state, after, tag):
        slabs, gl, other, send_sems, recv_sems, pairs, lands = state
        slots = _chip_exchange_wait(send_sems, recv_sems, pairs, lands, after, name=f"grad_{tag}_chip_wait")
        return slabs, [_chip_sum(g, r, sl, self.where, name=f"grad_chip_sum_{s}")
                       for s, g, r, sl in zip(slabs, gl, other, slots)]

    @staticmethod
    def share_start(slabs, halves, tag):
        send_sems, recv_sems, halves, token = _pair_share_start(halves, name=f"grad_{tag}_share_start")
        return (slabs, send_sems, recv_sems, halves), token

    @staticmethod
    def share_finish(state, after, tag):
        slabs, send_sems, recv_sems, halves = state
        return dict(zip(slabs, _pair_share_wait(send_sems, recv_sems, halves, after, name=f"grad_{tag}_share_wait")))

    def grads_ready(self, piece, G):
        self.reduces[piece], token = self.pair_start(G, _GRAD_PIECES[piece], piece)
        return token

    def grads_crossed(self, piece, after):
        self.reduces[piece], token = self.pair_land(self.reduces[piece], after, piece)
        return token


def kernel(x, mem, e_norm, e_w_in, e_gmlp_w, e_gmlp_b, e_conv_w, e_conv_b, e_conv_ln_g, e_conv_ln_b, e_w_out, o_norm, o_w_in, o_lam_re, o_lam_im, o_log_dt, o_b_re, o_b_im, o_c_re, o_c_im, o_d, o_w_out, ca_norm, ca_mem_norm, ca_wq, ca_wk, ca_wv, ca_wo, ffn_norm, ffn_w_gate, ffn_w_up, ffn_w_down, final_norm, loss_target, m_e_norm, m_e_w_in, m_e_gmlp_w, m_e_gmlp_b, m_e_conv_w, m_e_conv_b, m_e_conv_ln_g, m_e_conv_ln_b, m_e_w_out, m_o_norm, m_o_w_in, m_o_lam_re, m_o_lam_im, m_o_log_dt, m_o_b_re, m_o_b_im, m_o_c_re, m_o_c_im, m_o_d, m_o_w_out, m_ca_norm, m_ca_mem_norm, m_ca_wq, m_ca_wk, m_ca_wv, m_ca_wo, m_ffn_norm, m_ffn_w_gate, m_ffn_w_up, m_ffn_w_down, m_final_norm, v_e_norm, v_e_w_in, v_e_gmlp_w, v_e_gmlp_b, v_e_conv_w, v_e_conv_b, v_e_conv_ln_g, v_e_conv_ln_b, v_e_w_out, v_o_norm, v_o_w_in, v_o_lam_re, v_o_lam_im, v_o_log_dt, v_o_b_re, v_o_b_im, v_o_c_re, v_o_c_im, v_o_d, v_o_w_out, v_ca_norm, v_ca_mem_norm, v_ca_wq, v_ca_wk, v_ca_wv, v_ca_wo, v_ffn_norm, v_ffn_w_gate, v_ffn_w_up, v_ffn_w_down, v_final_norm):
    args = dict(locals())
    local = {n: args[n] for n in _WEIGHTS}
    mom = {n: args["m_" + n] for n in _WEIGHTS}
    vel = {n: args["v_" + n] for n in _WEIGHTS}
    chip = 2 * lax.axis_index("x") + lax.axis_index("y")
    core = lax.axis_index("c")
    xs_, mems_, tgt = x[0], mem[0], loss_target[0]

    w = {n: local[n] for n in _REPLICATED}
    exchange = _Exchange(local, chip, core)
    G = {s: lax.empty((N_CHIPS, rows, width), F32) for s, (width, rows) in _SLABS.items()}
    loss_lanes, dx, G, grads = _forward_backward(xs_, mems_, tgt, w, G, exchange)

    gs_slab, gs_spans = _pack_rows([grads[n] for n in _SMALL] + [loss_lanes], SMALL_W, F32)
    rest0_token = exchange.grads_ready("rest0", G)
    small_flight = _all_to_all_start(gs_slab, rest0_token, name="small_grads_start")
    slabs_l1, halves_l1 = exchange.reduce_sum(exchange.reduces["l1"], small_flight[4], "l1")
    slabs_f0, halves_f0 = exchange.reduce_sum(exchange.reduces["ffn0"], small_flight[4], "ffn0")
    share, share_token = exchange.share_start(slabs_l1 + slabs_f0, halves_l1 + halves_f0, "l1_ffn0")
    token = exchange.grads_crossed("rest0", share_token)

    out_grads, delta, new_m, new_v = {}, {}, {}, {}

    def adamw_large(names):
        raw = []
        for n in names:
            shp = local[n].shape
            g_, d_, m_, v_ = _adamw_shard(_shard_rows(n, local[n]), [(gsum[s], r0) for s, r0 in _PLACE[n][1]],
                                          _shard_rows(n, mom[n]), _shard_rows(n, vel[n]), name=f"adamw_{n}")
            out_grads[n], delta[n], new_m[n], new_v[n] = (_from_shard_rows(n, t, shp) for t in (g_, d_, m_, v_))
            raw.append(d_)
        return raw

    gsum = exchange.share_finish(share, token, "l1_ffn0")
    ready = [n for n, (_, where) in _PLACE.items() if all(s in gsum for s, _ in where)]
    done = adamw_large(ready)

    gs_slab, gs_all = _all_to_all_wait(*small_flight[:4], done, name="small_grads_wait")
    gs_all = lax.dynamic_update_slice(gs_all, gs_slab[None], (2 * chip + core, 0, 0))
    gs_sum = _sum_slots(gs_all, name="small_grad_sum")
    *small_sums, loss_sum = _unpack_rows(gs_sum, gs_spans, [grads[n].shape for n in _SMALL] + [loss_lanes.shape])
    out_grads.update(zip(_SMALL, small_sums))
    for n, ax in _SMALL_SHARDED:
        width = local[n].shape[ax]
        out_grads[n] = lax.dynamic_slice_in_dim(out_grads[n], chip * width, width, axis=ax)
    d_, m_, v_ = _adamw_small([_two_d(local[n]) for n in _SMALL], [_two_d(out_grads[n]) for n in _SMALL],
                              [_two_d(mom[n]) for n in _SMALL], [_two_d(vel[n]) for n in _SMALL], name="adamw_small")
    for n, dd, mm_, vv in zip(_SMALL, d_, m_, v_):
        shp = local[n].shape
        delta[n], new_m[n], new_v[n] = dd.reshape(shp), mm_.reshape(shp), vv.reshape(shp)

    slabs_r0, halves_r0 = exchange.reduce_sum(exchange.reduces["rest0"], d_[0], "rest0")
    share, share_token = exchange.share_start(slabs_r0, halves_r0, "rest0")
    gsum = {**gsum, **exchange.share_finish(share, share_token, "rest0")}
    adamw_large([n for n in _PLACE if n not in ready])

    return (loss_sum[0, 0], dx[None], *[out_grads[n] for n in _WEIGHTS], *[delta[n] for n in _WEIGHTS],
            *[new_m[n] for n in _WEIGHTS], *[new_v[n] for n in _WEIGHTS])
```

```python
import functools
import math

import jax
import jax.numpy as jnp
from jax import lax
from jax.experimental import pallas as pl
from jax.experimental.pallas import tpu as pltpu

F32 = jnp.float32
BF16 = jnp.bfloat16
MESH = pl.DeviceIdType.MESH

EPS = 1e-6
D_MODEL = 1024
A_WIDTH = 512
A_GROUPS = 4
GMLP_BLOCK = 128
CHUNK = 64
B_WIDTH = 512
CONV_WIDTH = 31
CONV_HALO = 32
C_WIDTH = 512
C_GROUP_CH = 16
C_GROUPS = 32
C_STATE = 64
N_STATE = C_GROUPS * C_STATE
STATE_LANES = 128
STATE_ROWS = N_STATE // STATE_LANES
SCAN_BLOCK = 8
CA_HEADS = 4
CA_HEAD_DIM = 256
FFN_HIDDEN = 2816

ADAM_LR = 0.001
ADAM_B1 = 0.9
ADAM_B2 = 0.999
ADAM_EPS = 1e-08
ADAM_WD = 0.01
ADAM_STEP = 10

VMEM_LIMIT = 56 * 1024 * 1024
ACC_BYTES = 6 * 1024 * 1024
TN_VMEM_BYTES = 44 * 1024 * 1024
SMALL_W = 128
N_CHIPS = 4
N_DEV = 8

_SLABS = {"D0": (512, 1024), "E0": (1024, 256), "A0": (1024, 1024), "B0": (1024, 704), "C0": (1024, 1408),
          "D1": (512, 768), "A1": (1024, 1024), "B1": (1024, 704), "C1": (1024, 1408)}
_STAGES = (("D0", "E0"), ("A0",), ("B0", "C0"), ("D1", "A1", "B1", "C1"))
_LATE_SLABS = ("A1", "B1", "C1")
_GRAD_PIECES = {"l1": _STAGES[3], "ffn0": _STAGES[2], "rest0": _STAGES[0] + _STAGES[1]}
_PLACE = {
    "e_w_in": (1024, (("D0", 0),)), "e_w_out": (256, (("E0", 0),)),
    "o_w_out": (512, (("D1", 0),)), "o_w_in": (256, (("D1", 512),)),
    "ca_wq": (256, (("A0", 0), ("A1", 0))), "ca_wk": (256, (("A0", 256), ("A1", 256))),
    "ca_wv": (256, (("A0", 512), ("A1", 512))), "ca_wo": (256, (("A0", 768), ("A1", 768))),
    "ffn_w_down": (704, (("B0", 0), ("B1", 0))),
    "ffn_w_gate": (704, (("C0", 0), ("C1", 0))), "ffn_w_up": (704, (("C0", 704), ("C1", 704))),
}
_SMALL_SLAB = "F0"
_SMALL_SLAB_ROWS = 48
_SMALL_PLACE = {"e_conv_w": (0, 31), "o_norm": (32, 2), "o_d": (34, 1)}
_TRANSPOSED = ("ffn_w_gate", "ffn_w_up")


def _params(sem=None):
    return pltpu.CompilerParams(dimension_semantics=sem, vmem_limit_bytes=VMEM_LIMIT)


def _tile(n, pref, mult=128):
    if n <= pref:
        return n
    t = (pref // mult) * mult
    while t >= mult:
        if n % t == 0:
            return t
        t -= mult
    return n


def _blk(name, layer=0):
    rows, where = _PLACE[name]
    slab, r0 = where[layer]
    assert r0 % rows == 0
    return slab, rows, r0 // rows


def _shards(slabs, name, layer=0):
    slab, rows, b = _blk(name, layer)
    return [(slabs[slab], (None, rows, _SLABS[slab][0]), (p, b, 0)) for p in range(N_CHIPS)]


_GELU_C = 0.7978845608028654
_GELU_A = 0.044715


def _gelu(x):
    t = jnp.tanh(_GELU_C * (x + _GELU_A * (x * x * x)))
    return 0.5 * x * (1.0 + t), t


def _gelu_grad(x, t):
    return 0.5 * (1.0 + t) + 0.5 * x * (1.0 - t * t) * (_GELU_C * (1.0 + 3.0 * _GELU_A * x * x))


def _sigmoid(x):
    return 1.0 / (1.0 + jnp.exp(-x))


def _mean(x):
    return jnp.mean(x, axis=-1, keepdims=True)


def _dot(a, b):
    return jnp.dot(a, b, preferred_element_type=F32)


def _dot_nt(a, b):
    return lax.dot_general(a, b, (((1,), (1,)), ((), ())), preferred_element_type=F32)


def _dot_tn(a, b):
    return lax.dot_general(a, b, (((0,), (0,)), ((), ())), preferred_element_type=F32)


def _rms_tile(xv, gv):
    return (xv * lax.rsqrt(_mean(xv * xv) + EPS)) * gv


def _rms_bwd_tile(xv, gv, dyv):
    r = lax.rsqrt(_mean(xv * xv) + EPS)
    xh = xv * r
    dyg = dyv * gv
    return r * (dyg - xh * _mean(dyg * xh)), jnp.sum(dyv * xh, axis=0, keepdims=True)


def _cols(p, width):
    return slice(p * width, (p + 1) * width)


def _sum_k(a, ws, k):
    tot = None
    for p in range(N_CHIPS):
        y = _dot(a[:, _cols(p, k)], ws[p][...])
        tot = y if tot is None else tot + y
    return tot


def _cat_nt(a, ws):
    return jnp.concatenate([_dot_nt(a, ws[p][...]) for p in range(N_CHIPS)], axis=1)


def _rows_call(name, tm, rows, fulls, outs, accs, body, scratch=()):
    S = min(x.shape[-2] for x in rows if x.ndim != 4)
    nr, nf, no, na = len(rows), len(fulls), len(outs), len(accs)

    def kern(*refs):
        r, f = refs[:nr], refs[nr:nr + nf]
        o, a = refs[nr + nf:nr + nf + no], refs[nr + nf + no:nr + nf + no + na]
        if na:
            @pl.when(pl.program_id(0) == 0)
            def _():
                for ref in a:
                    ref[...] = jnp.zeros_like(ref)
        body(r, f, o, a, refs[nr + nf + no + na:])

    def whole(shape):
        nd = len(shape)
        return pl.BlockSpec(tuple(shape), lambda i: (0,) * nd)

    def row_spec(shape):
        if len(shape) == 4:
            return pl.BlockSpec((tm // 8,) + tuple(shape[1:]), lambda i: (i, 0, 0, 0))
        if len(shape) == 3:
            return pl.BlockSpec((shape[0], tm, shape[2]), lambda i: (0, i, 0))
        return pl.BlockSpec((tm, shape[1]), lambda i: (i, 0))

    def full_spec(x):
        if isinstance(x, tuple):
            _, bshape, bidx = x
            return pl.BlockSpec(bshape, lambda i: bidx, pipeline_mode=pl.Buffered(1))
        return whole(x.shape)

    out_shapes = [(S, o[0]) if len(o) == 2 else (o[0], S, o[1]) for o in outs]
    res = pl.pallas_call(
        kern, name=name, grid=(S // tm,),
        in_specs=[row_spec(x.shape) for x in rows] + [full_spec(x) for x in fulls],
        out_specs=[row_spec(s) for s in out_shapes] + [whole(shp) for shp, _ in accs],
        out_shape=[jax.ShapeDtypeStruct(s, o[-1]) for s, o in zip(out_shapes, outs)]
        + [jax.ShapeDtypeStruct(tuple(shp), dt) for shp, dt in accs],
        scratch_shapes=list(scratch),
        compiler_params=_params(("arbitrary",) if na else ("parallel",)),
    )(*rows, *[x[0] if isinstance(x, tuple) else x for x in fulls])
    return res[:no], res[no:]


def _grad_to_slab(gslabs, wname, layer, a, b, *, a_cols=None, b_cols=None, chips=(0, N_CHIPS), name):
    slab, rows, bidx = _blk(wname, layer)
    width = _SLABS[slab][0]
    p0, n_p = chips
    assert p0 % n_p == 0
    S = a.shape[-2]

    def tile_bytes(x, ts):
        return ts * x.dtype.itemsize * (x.shape[2] * n_p if x.ndim == 3 else x.shape[1])

    acc_bytes = n_p * rows * (-(-width // 128) * 128) * 4
    ts = next(t for t in (2048, 1024, 512, 256, S) if S % t == 0
              and 2 * (tile_bytes(a, t) + tile_bytes(b, t) + acc_bytes) <= TN_VMEM_BYTES or t == S)

    def operand(x):
        if x.ndim == 3:
            return pl.BlockSpec((n_p, ts, x.shape[2]), lambda s: (p0 // n_p, s, 0))
        return pl.BlockSpec((ts, x.shape[1]), lambda s: (s, 0))

    def part(ref, cols, p):
        if len(ref.shape) == 3:
            return ref[p]
        return ref[...] if cols is None else ref[:, _cols(p, cols)]

    def body(a_ref, b_ref, slab_ref, o_ref):
        @pl.when(pl.program_id(0) == 0)
        def _():
            o_ref[...] = jnp.zeros_like(o_ref)

        for p in range(n_p):
            o_ref[p] += _dot_tn(part(a_ref, a_cols, p).astype(BF16), part(b_ref, b_cols, p).astype(BF16))

    g = gslabs[slab]
    out = pl.pallas_call(
        body, name=name, grid=(S // ts,),
        in_specs=[operand(a), operand(b), pl.BlockSpec(memory_space=pl.ANY)],
        out_specs=pl.BlockSpec((n_p, rows, width), lambda s: (p0 // n_p, bidx, 0)),
        out_shape=jax.ShapeDtypeStruct(g.shape, F32), input_output_aliases={2: 0},
        compiler_params=_params(("arbitrary",)),
    )(a, b, g)
    return {**gslabs, slab: out}


def _vec(g):
    return g.reshape(1, -1)


def _norm_mm(x, g, ws, *, split, out_dtype, name, tm=512):
    S, D = x.shape
    k, n = ws[0][1][1], ws[0][1][2]
    N = n if split == "k" else N_CHIPS * n

    def body(r, f, o, acc, s):
        xn = _rms_tile(r[0][...], f[0][...]).astype(BF16)
        o[0][...] = xn
        if split == "k":
            o[1][...] = _sum_k(xn, f[1:], k).astype(out_dtype)
        else:
            for p in range(N_CHIPS):
                o[1][:, _cols(p, n)] = _dot(xn, f[1 + p][...]).astype(out_dtype)

    (xn, y), _ = _rows_call(name, _tile(S, tm), [x], [_vec(g)] + ws, [(D, BF16), (N, out_dtype)], [], body)
    return xn, y


def _mm_k(a, ws, *, add=None, out_dtype=F32, name, tm=512):
    S = a.shape[-2]
    k, n = ws[0][1][1], ws[0][1][2]
    has_add = add is not None

    def body(r, f, o, acc, s):
        if a.ndim == 3:
            y = None
            for p in range(N_CHIPS):
                t = _dot(r[0][p].astype(BF16), f[p][...])
                y = t if y is None else y + t
        else:
            y = _sum_k(r[0][...].astype(BF16), f, k)
        if has_add:
            y = y + r[1][...]
        o[0][...] = y.astype(out_dtype)

    (y,), _ = _rows_call(name, _tile(S, tm), [a] + ([add] if has_add else []), ws, [(n, out_dtype)], [], body)
    return y


def _mm_k_t(terms, *, out_dtype=F32, name, tm=512):
    S = terms[0][0].shape[0]
    k = terms[0][1][0][1][1]

    def body(r, f, o, acc, s):
        y = None
        for t in range(len(terms)):
            yt = _cat_nt(r[t][...].astype(BF16), f[N_CHIPS * t:N_CHIPS * (t + 1)])
            y = yt if y is None else y + yt
        o[0][...] = y.astype(out_dtype)

    (y,), _ = _rows_call(name, _tile(S, tm), [a for a, _ in terms], [w for _, ws in terms for w in ws],
                         [(N_CHIPS * k, out_dtype)], [], body)
    return y


def _rms_fwd(x, g, *, name):
    def body(r, f, o, acc, s):
        o[0][...] = _rms_tile(r[0][...], f[0][...]).astype(BF16)

    (y,), _ = _rows_call(name, _tile(x.shape[0], 256, 8), [x], [_vec(g)], [(x.shape[1], BF16)], [], body)
    return y


def _rms_dg(x, g, dy, *, name):
    def body(r, f, o, acc, s):
        acc[0][...] += _rms_bwd_tile(r[0][...], f[0][...], r[1][...])[1]

    _, (dg,) = _rows_call(name, _tile(x.shape[0], 256, 8), [x, dy], [_vec(g)], [], [((1, x.shape[1]), F32)], body)
    return dg


def _ffn_up(x, g, wg, wu, *, name, tm=512):
    S, D = x.shape
    h = wg[0][1][1]

    def body(r, f, o, acc, s):
        xn = _rms_tile(r[0][...], f[0][...]).astype(BF16)
        o[0][...] = xn
        for p in range(N_CHIPS):
            gate = _dot_nt(xn, f[1 + p][...])
            up = _dot_nt(xn, f[1 + N_CHIPS + p][...])
            o[1][p] = gate.astype(BF16)
            o[2][p] = up.astype(BF16)
            o[3][p] = (gate * _sigmoid(gate) * up).astype(BF16)

    (xn, gate, up, hid), _ = _rows_call(name, _tile(S, tm), [x], [_vec(g)] + wg + wu,
                                        [(D, BF16), (N_CHIPS, h, BF16), (N_CHIPS, h, BF16), (N_CHIPS, h, BF16)], [],
                                        body)
    return xn, gate, up, hid


def _ffn_bwd_hidden(dy, wd, gate, up, token=None, *, name, tm=512):
    S = dy.shape[0]
    h = wd[0][1][1]

    def body(r, f, o, acc, s):
        dyv = r[0][...]
        if token is not None:
            dyv = dyv + jnp.sum(f[N_CHIPS][...])
        dyb = dyv.astype(BF16)
        for p in range(N_CHIPS):
            dh = _dot_nt(dyb, f[p][...])
            gv = r[1][p].astype(F32)
            sg = _sigmoid(gv)
            o[0][p] = (dh * r[2][p].astype(F32) * (sg * (1.0 + gv * (1.0 - sg)))).astype(BF16)
            o[1][p] = (dh * gv * sg).astype(BF16)

    (dg, du), _ = _rows_call(name, _tile(S, tm), [dy, gate, up], wd + ([] if token is None else [token]),
                             [(N_CHIPS, h, BF16), (N_CHIPS, h, BF16)], [], body)
    return dg, du


def _ffn_in_bwd(dg, du, wg, wu, x, g, dres, *, name, tm=512):
    S, D = x.shape

    def body(r, f, o, acc, s):
        tot = None
        for p in range(N_CHIPS):
            y = _dot(r[0][p], f[1 + p][...]) + _dot(r[1][p], f[1 + N_CHIPS + p][...])
            tot = y if tot is None else tot + y
        dx, dgn = _rms_bwd_tile(r[2][...], f[0][...], tot)
        o[0][...] = dx + r[3][...]
        acc[0][...] += dgn

    (dx,), (dgn,) = _rows_call(name, _tile(S, tm), [dg, du, x, dres], [_vec(g)] + wg + wu, [(D, F32)],
                               [((1, D), F32)], body)
    return dx, dgn


def _norm_bwd_k(da, ws, x, g, dres, *, name, tm=512):
    S, D = x.shape

    def body(r, f, o, acc, s):
        dx, dg = _rms_bwd_tile(r[1][...], f[0][...], _cat_nt(r[0][...].astype(BF16), f[1:]))
        o[0][...] = dx + r[2][...]
        acc[0][...] += dg

    (dx,), (dg,) = _rows_call(name, _tile(S, tm), [da, x, dres], [_vec(g)] + ws, [(D, F32)], [((1, D), F32)], body)
    return dx, dg


def _norm_bwd_n(das, ws, x, g, dres, *, name, tm=256):
    S, D = x.shape
    n = ws[0][1][2]

    def body(r, f, o, acc, s):
        tot = None
        for p in range(N_CHIPS):
            y = _dot_nt(r[p // 2][:, _cols(p % 2, n)], f[1 + p][...])
            tot = y if tot is None else tot + y
        dx, dg = _rms_bwd_tile(r[2][...], f[0][...], tot)
        o[0][...] = dx + r[3][...]
        acc[0][...] += dg

    (dx,), (dg,) = _rows_call(name, _tile(S, tm), list(das) + [x, dres], [_vec(g)] + ws, [(D, F32)], [((1, D), F32)],
                              body)
    return dx, dg


def _ln_stats(v):
    mu = _mean(v)
    xc = v - mu
    rstd = lax.rsqrt(_mean(xc * xc) + EPS)
    return xc * rstd, rstd


_SHIFTS = 8
_CONV_ROWS = 64


def _fill_shifts(sh_ref, ext_ref, tm):
    sh_ref[0] = ext_ref[...]
    for s in range(1, _SHIFTS):
        sh_ref[s, 0:tm + CONV_HALO - _SHIFTS, :] = ext_ref[pl.ds(s, tm + CONV_HALO - _SHIFTS), :]


def _window(sh_ref, off, tm):
    return sh_ref[off % _SHIFTS, pl.ds(off - off % _SHIFTS, tm), :]


def _even_fwd(proj, wm, bcol, cw, cb, lg, lb, *, name):
    S = proj.shape[0]
    tm = _tile(S, 256)
    hb = tm // CONV_HALO
    nblk = tm // GMLP_BLOCK

    def body(p_ref, halo_ref, wm_ref, b_ref, cw_ref, cb_ref, lg_ref, lb_ref, mix_ref, hc_ref, hext_ref, hsh_ref):
        i = pl.program_id(0)
        gu, _ = _gelu(p_ref[:, 0:A_WIDTH])
        gv, _ = _gelu(p_ref[:, A_WIDTH:2 * A_WIDTH])
        vn, _ = _ln_stats(gv)
        vnb = vn.astype(BF16)
        for n in range(nblk):
            rows = slice(n * GMLP_BLOCK, (n + 1) * GMLP_BLOCK)
            for g in range(A_GROUPS):
                cols = slice(g * GMLP_BLOCK, (g + 1) * GMLP_BLOCK)
                sg = jnp.dot(wm_ref[g], vnb[rows, cols], preferred_element_type=F32) + b_ref[g]
                mix_ref[rows, cols] = (gu[rows, cols] * sg).astype(BF16)
        h = p_ref[:, 1024:1536] * _sigmoid(p_ref[:, 1536:2048])
        hh = halo_ref[:, 0:B_WIDTH] * _sigmoid(halo_ref[:, B_WIDTH:2 * B_WIDTH])
        hext_ref[0:CONV_HALO, :] = jnp.where(i > 0, hh, 0.0)
        hext_ref[CONV_HALO:CONV_HALO + tm, :] = h
        _fill_shifts(hsh_ref, hext_ref, tm)
        for r0 in range(0, tm, _CONV_ROWS):
            acc = jnp.zeros((_CONV_ROWS, B_WIDTH), F32)
            for k in range(CONV_WIDTH):
                acc = acc + cw_ref[k:k + 1, :] * _window(hsh_ref, r0 + k + CONV_HALO - CONV_WIDTH + 1, _CONV_ROWS)
            hc_ref[r0:r0 + _CONV_ROWS, :] = acc + cb_ref[...]
        hc = hc_ref[...]
        hhat, _ = _ln_stats(hc)
        hl = hhat * lg_ref[...] + lb_ref[...]
        mix_ref[:, A_WIDTH:A_WIDTH + B_WIDTH] = (hl * _sigmoid(hl)).astype(BF16)

    vec = pl.BlockSpec((1, B_WIDTH), lambda i: (0, 0))
    return pl.pallas_call(
        body, name=name, grid=(S // tm,),
        in_specs=[
            pl.BlockSpec((tm, 2048), lambda i: (i, 0)),
            pl.BlockSpec((CONV_HALO, 1024), lambda i: (jnp.maximum(i * hb - 1, 0), 1)),
            pl.BlockSpec((A_GROUPS, GMLP_BLOCK, GMLP_BLOCK), lambda i: (0, 0, 0)),
            pl.BlockSpec((A_GROUPS, GMLP_BLOCK, 1), lambda i: (0, 0, 0)),
            pl.BlockSpec((CONV_HALO, B_WIDTH), lambda i: (0, 0)),
            vec, vec, vec,
        ],
        out_specs=[pl.BlockSpec((tm, 1024), lambda i: (i, 0)), pl.BlockSpec((tm, B_WIDTH), lambda i: (i, 0))],
        out_shape=[jax.ShapeDtypeStruct((S, 1024), BF16), jax.ShapeDtypeStruct((S, B_WIDTH), F32)],
        scratch_shapes=[pltpu.VMEM((tm + CONV_HALO, B_WIDTH), F32),
                        pltpu.VMEM((_SHIFTS, tm + CONV_HALO, B_WIDTH), F32)],
        compiler_params=_params(("parallel",)),
    )(proj, proj, wm, bcol, cw, cb, lg, lb)


def _even_bwd1(proj, dmix, hc, wm, wmt, bcol, lg, lb, *, name):
    S = proj.shape[0]
    tm = _tile(S, 256)
    nblk = tm // GMLP_BLOCK

    def body(p_ref, dm_ref, hc_ref, wm_ref, wmt_ref, b_ref, lg_ref, lb_ref,
             dpa_ref, dhc_ref, dwm_ref, db_ref, dlg_ref, dlb_ref, dcb_ref, dgu_ref, dvn_ref):
        @pl.when(pl.program_id(0) == 0)
        def _():
            dwm_ref[...] = jnp.zeros_like(dwm_ref)
            db_ref[...] = jnp.zeros_like(db_ref)
            dlg_ref[...] = jnp.zeros_like(dlg_ref)
            dlb_ref[...] = jnp.zeros_like(dlb_ref)
            dcb_ref[...] = jnp.zeros_like(dcb_ref)

        au = p_ref[:, 0:A_WIDTH]
        av = p_ref[:, A_WIDTH:2 * A_WIDTH]
        gu, tu = _gelu(au)
        gv, tv = _gelu(av)
        vn, rstd = _ln_stats(gv)
        vnb = vn.astype(BF16)
        for n in range(nblk):
            rows = slice(n * GMLP_BLOCK, (n + 1) * GMLP_BLOCK)
            for g in range(A_GROUPS):
                cols = slice(g * GMLP_BLOCK, (g + 1) * GMLP_BLOCK)
                vb = vnb[rows, cols]
                sg = jnp.dot(wm_ref[g], vb, preferred_element_type=F32) + b_ref[g]
                da = dm_ref[rows, cols]
                dsg = da * gu[rows, cols]
                dgu_ref[rows, cols] = da * sg
                dsgb = dsg.astype(BF16)
                dwm_ref[g] += _dot_nt(dsgb, vb)
                db_ref[g] += jnp.sum(dsg, axis=1, keepdims=True)
                dvn_ref[rows, cols] = jnp.dot(wmt_ref[g], dsgb, preferred_element_type=F32)
        dvn = dvn_ref[...]
        dgv = rstd * (dvn - _mean(dvn) - vn * _mean(dvn * vn))
        dpa_ref[:, 0:A_WIDTH] = (dgu_ref[...] * _gelu_grad(au, tu)).astype(BF16)
        dpa_ref[:, A_WIDTH:2 * A_WIDTH] = (dgv * _gelu_grad(av, tv)).astype(BF16)
        hhat, rstd2 = _ln_stats(hc_ref[...])
        lgv = lg_ref[...]
        hl = hhat * lgv + lb_ref[...]
        s = _sigmoid(hl)
        dhl = dm_ref[:, A_WIDTH:A_WIDTH + B_WIDTH] * (s * (1.0 + hl * (1.0 - s)))
        dlg_ref[...] += jnp.sum(dhl * hhat, axis=0, keepdims=True)
        dlb_ref[...] += jnp.sum(dhl, axis=0, keepdims=True)
        dhh = dhl * lgv
        dhc = rstd2 * (dhh - _mean(dhh) - hhat * _mean(dhh * hhat))
        dcb_ref[...] += jnp.sum(dhc, axis=0, keepdims=True)
        dhc_ref[...] = dhc

    vec = pl.BlockSpec((1, B_WIDTH), lambda i: (0, 0))
    w3 = pl.BlockSpec((A_GROUPS, GMLP_BLOCK, GMLP_BLOCK), lambda i: (0, 0, 0))
    b3 = pl.BlockSpec((A_GROUPS, GMLP_BLOCK, 1), lambda i: (0, 0, 0))
    return pl.pallas_call(
        body, name=name, grid=(S // tm,),
        in_specs=[
            pl.BlockSpec((tm, 1024), lambda i: (i, 0)),
            pl.BlockSpec((tm, 1024), lambda i: (i, 0)),
            pl.BlockSpec((tm, B_WIDTH), lambda i: (i, 0)),
            w3, w3, b3, vec, vec,
        ],
        out_specs=[pl.BlockSpec((tm, 1024), lambda i: (i, 0)), pl.BlockSpec((tm, B_WIDTH), lambda i: (i, 0)),
                   w3, b3, vec, vec, vec],
        out_shape=[
            jax.ShapeDtypeStruct((S, 1024), BF16), jax.ShapeDtypeStruct((S, B_WIDTH), F32),
            jax.ShapeDtypeStruct((A_GROUPS, GMLP_BLOCK, GMLP_BLOCK), F32),
            jax.ShapeDtypeStruct((A_GROUPS, GMLP_BLOCK, 1), F32),
            jax.ShapeDtypeStruct((1, B_WIDTH), F32), jax.ShapeDtypeStruct((1, B_WIDTH), F32),
            jax.ShapeDtypeStruct((1, B_WIDTH), F32),
        ],
        scratch_shapes=[pltpu.VMEM((tm, A_WIDTH), F32), pltpu.VMEM((tm, A_WIDTH), F32)],
        compiler_params=_params(("arbitrary",)),
    )(proj, dmix, hc, wm, wmt, bcol, lg, lb)


def _even_bwd2(proj, dhc, cw, *, name):
    S = proj.shape[0]
    tm = _tile(S, 256)
    hb = tm // CONV_HALO
    nt = S // tm
    last_halo = S // CONV_HALO - 1
    lo = CONV_HALO - CONV_WIDTH + 1

    def body(p_ref, halo_ref, d_ref, dnext_ref, cw_ref, dpb_ref, dcw_ref, hext_ref, dext_ref, hsh_ref, dsh_ref):
        i = pl.program_id(0)

        @pl.when(i == 0)
        def _():
            dcw_ref[...] = jnp.zeros_like(dcw_ref)

        hh = halo_ref[:, 0:B_WIDTH] * _sigmoid(halo_ref[:, B_WIDTH:2 * B_WIDTH])
        hext_ref[0:CONV_HALO, :] = jnp.where(i > 0, hh, 0.0)
        hext_ref[CONV_HALO:CONV_HALO + tm, :] = p_ref[:, 0:B_WIDTH] * _sigmoid(p_ref[:, B_WIDTH:2 * B_WIDTH])
        dext_ref[0:tm, :] = d_ref[...]
        dext_ref[tm:tm + CONV_HALO, :] = jnp.where(i < nt - 1, dnext_ref[...], 0.0)
        _fill_shifts(hsh_ref, hext_ref, tm)
        _fill_shifts(dsh_ref, dext_ref, tm)
        for r0 in range(0, tm, _CONV_ROWS):
            rows = slice(r0, r0 + _CONV_ROWS)
            dhc_b = d_ref[rows, :]
            dh = jnp.zeros((_CONV_ROWS, B_WIDTH), F32)
            for k in range(CONV_WIDTH):
                dh = dh + cw_ref[k:k + 1, :] * _window(dsh_ref, r0 + CONV_WIDTH - 1 - k, _CONV_ROWS)
                dcw_ref[k:k + 1, :] += jnp.sum(dhc_b * _window(hsh_ref, r0 + k + lo, _CONV_ROWS), axis=0,
                                               keepdims=True)
            ba_b = p_ref[rows, 0:B_WIDTH]
            sg_b = _sigmoid(p_ref[rows, B_WIDTH:2 * B_WIDTH])
            dpb_ref[rows, 0:B_WIDTH] = (dh * sg_b).astype(BF16)
            dpb_ref[rows, B_WIDTH:2 * B_WIDTH] = (dh * ba_b * sg_b * (1.0 - sg_b)).astype(BF16)

    return pl.pallas_call(
        body, name=name, grid=(nt,),
        in_specs=[
            pl.BlockSpec((tm, 1024), lambda i: (i, 1)),
            pl.BlockSpec((CONV_HALO, 1024), lambda i: (jnp.maximum(i * hb - 1, 0), 1)),
            pl.BlockSpec((tm, B_WIDTH), lambda i: (i, 0)),
            pl.BlockSpec((CONV_HALO, B_WIDTH), lambda i: (jnp.minimum((i + 1) * hb, last_halo), 0)),
            pl.BlockSpec((CONV_HALO, B_WIDTH), lambda i: (0, 0)),
        ],
        out_specs=[pl.BlockSpec((tm, 1024), lambda i: (i, 0)), pl.BlockSpec((CONV_HALO, B_WIDTH), lambda i: (0, 0))],
        out_shape=[jax.ShapeDtypeStruct((S, 1024), BF16), jax.ShapeDtypeStruct((CONV_HALO, B_WIDTH), F32)],
        scratch_shapes=[pltpu.VMEM((tm + CONV_HALO, B_WIDTH), F32), pltpu.VMEM((tm + CONV_HALO, B_WIDTH), F32),
                        pltpu.VMEM((_SHIFTS, tm + CONV_HALO, B_WIDTH), F32),
                        pltpu.VMEM((_SHIFTS, tm + CONV_HALO, B_WIDTH), F32)],
        compiler_params=_params(("arbitrary",)),
    )(proj, proj, dhc, dhc, cw)


_CA_SCALE = CA_HEAD_DIM ** -0.5


def _softmax_rows(s):
    e = jnp.exp(s - jnp.max(s, axis=-1, keepdims=True))
    return e / jnp.sum(e, axis=-1, keepdims=True)


def _attn_fwd(q, k, v, *, name):
    S = q.shape[0]

    def body(r, f, o, acc, s):
        for h in range(CA_HEADS):
            cols = _cols(h, CA_HEAD_DIM)
            p = _softmax_rows(_dot_nt(r[0][:, cols], f[0][:, cols]) * _CA_SCALE)
            o[0][:, cols] = _dot(p.astype(BF16), f[1][:, cols]).astype(BF16)

    (o_,), _ = _rows_call(name, _tile(S, 512), [q], [k, v], [(D_MODEL, BF16)], [], body)
    return o_


def _attn_bwd(dy, wo, q, k, v, *, name):
    S = q.shape[0]
    M = k.shape[0]

    def body(r, f, o, acc, s):
        dyb = r[0][...].astype(BF16)
        for h in range(CA_HEADS):
            cols = _cols(h, CA_HEAD_DIM)
            qh = r[1][:, cols]
            kh = f[0][:, cols]
            vh = f[1][:, cols]
            doh = _dot_nt(dyb, f[2 + h][...]).astype(BF16)
            p = _softmax_rows(_dot_nt(qh, kh) * _CA_SCALE)
            acc[1][:, cols] += _dot_tn(p.astype(BF16), doh)
            dp = _dot_nt(doh, vh)
            ds = (p * (dp - jnp.sum(dp * p, axis=-1, keepdims=True)) * _CA_SCALE).astype(BF16)
            o[0][:, cols] = _dot(ds, kh).astype(BF16)
            acc[0][:, cols] += _dot_tn(ds, qh)

    (dq,), (dk, dv) = _rows_call(name, _tile(S, 512), [dy, q], [k, v] + wo, [(D_MODEL, BF16)],
                                 [((M, D_MODEL), F32), ((M, D_MODEL), F32)], body)
    return dq, dk, dv


_STATE_TILE = 2 * STATE_ROWS
N_SETS = 4
SET_CH = C_WIDTH // N_SETS
SET_COLS = N_STATE // N_SETS // STATE_LANES


def _set_groups(j):
    return [SET_COLS * j + c for c in range(SET_COLS)] + [STATE_ROWS + SET_COLS * j + c for c in range(SET_COLS)]


def _pack_state(re, im):
    hi = lax.bitcast_convert_type(re.astype(BF16).astype(F32), jnp.uint32)
    lo = lax.bitcast_convert_type(im.astype(BF16).astype(F32), jnp.uint32) >> 16
    return hi | lo


def _unpack_state(word):
    re = lax.bitcast_convert_type(word & jnp.uint32(0xFFFF0000), F32)
    im = lax.bitcast_convert_type(word << 16, F32)
    return re, im


def _state_set(ref, tm, j):
    parts = [_unpack_state(ref[:, SET_COLS * j + c, :, :].reshape(tm, STATE_LANES)) for c in range(SET_COLS)]
    return jnp.concatenate([p[0].astype(BF16) for p in parts] + [p[1].astype(BF16) for p in parts], axis=1)


def _s5_readout(xs, cset, u, d, *, name, tm=256):
    tm = _tile(u.shape[0], tm)

    def body(r, f, o, acc, s):
        y0 = jnp.concatenate([_dot(_state_set(r[0], tm, j), f[0][j]) for j in range(N_SETS)], axis=1)
        y = y0 + f[1][...] * r[1][...]
        o[0][...] = y
        o[1][...] = _gelu(y)[0].astype(BF16)

    (y, yg), _ = _rows_call(name, tm, [xs, u], [cset, d], [(C_WIDTH, F32), (C_WIDTH, BF16)], [], body)
    return y, yg


def _state_grad_sets(a, st, *, name, ts=256):
    ts = _tile(a.shape[0], ts)

    def body(r, f, o, acc, s):
        for j in range(N_SETS):
            acc[0][j] += _dot_tn(r[0][:, _cols(j, SET_CH)].astype(BF16), _state_set(r[1], ts, j))

    _, (out,) = _rows_call(name, ts, [a, st], [], [], [((N_SETS, SET_CH, 2 * N_STATE // N_SETS), F32)], body)
    return out


def _glu_out(yg, ws, x, *, name, tm=512):
    n = ws[0][1][2]

    def body(r, f, o, acc, s):
        ygv = r[0][...]
        ov = [_dot(ygv, f[p][...]) for p in range(N_CHIPS)]
        for p in range(N_CHIPS):
            o[0][:, _cols(p, n)] = ov[p].astype(BF16)
        for p in range(2):
            o[1][:, _cols(p, n)] = r[1][:, _cols(p, n)] + ov[p] * _sigmoid(ov[2 + p])

    (o_, y), _ = _rows_call(name, _tile(x.shape[0], tm), [yg, x], ws, [(2 * D_MODEL, BF16), (D_MODEL, F32)], [], body)
    return o_, y


def _glu_out_bwd(o_, dy, ws, y, u, d, *, name, tm=256):
    n = ws[0][1][2]

    def body(r, f, o, acc, s):
        o1 = r[0][:, 0:D_MODEL].astype(F32)
        sg = _sigmoid(r[0][:, D_MODEL:2 * D_MODEL].astype(F32))
        dyv = r[1][...]
        do1 = (dyv * sg).astype(BF16)
        do2 = (dyv * o1 * sg * (1.0 - sg)).astype(BF16)
        o[0][:, 0:D_MODEL] = do1
        o[0][:, D_MODEL:2 * D_MODEL] = do2
        dyg = None
        for p in range(N_CHIPS):
            t = _dot_nt((do1 if p < 2 else do2)[:, _cols(p % 2, n)], f[1 + p][...])
            dyg = t if dyg is None else dyg + t
        yv = r[2][...]
        dys = dyg * _gelu_grad(yv, _gelu(yv)[1])
        o[1][...] = dys.astype(BF16)
        o[2][...] = f[0][...] * dys
        acc[0][...] += jnp.sum(dys * r[3][...], axis=0, keepdims=True)

    (do, dys, dus), (dd,) = _rows_call(name, _tile(dy.shape[0], tm), [o_, dy, y, u], [d] + ws,
                                       [(2 * D_MODEL, BF16), (C_WIDTH, BF16), (C_WIDTH, F32)], [((1, C_WIDTH), F32)],
                                       body)
    return do, dys, dus, dd


def _s5_in_bwd(gs, bset, dus, ws, x, g, dres, *, name, tm=256):
    D = x.shape[1]
    tm = _tile(x.shape[0], tm)

    def body(r, f, o, acc, s):
        du0 = jnp.concatenate([_dot_nt(_state_set(r[0], tm, j), f[1][j]) for j in range(N_SETS)], axis=1)
        du = (du0 + r[1][...]).astype(BF16)
        o[0][...] = du
        dx, dg = _rms_bwd_tile(r[2][...], f[0][...], _cat_nt(du, f[2:]))
        o[1][...] = dx + r[3][...]
        acc[0][...] += dg

    (du, dx), (dg,) = _rows_call(name, tm, [gs, dus, x, dres], [_vec(g), bset] + ws,
                                 [(C_WIDTH, BF16), (D, F32)], [((1, D), F32)], body)
    return du, dx, dg


_SCAN_CHUNK = 256
_RE = slice(0, STATE_ROWS)
_IM = slice(STATE_ROWS, 2 * STATE_ROWS)
assert SCAN_BLOCK == 8


def _token(g, i, rows):
    return pl.ds(pl.multiple_of(g * (rows * SCAN_BLOCK), rows * SCAN_BLOCK) + i, rows, stride=SCAN_BLOCK)


def _fill_chunk(s3, a_ref, wset, tc, nt):
    for j in range(N_SETS):
        av = a_ref[:, _cols(j, SET_CH)].astype(BF16)
        y = _dot_nt(av, wset[j]) if nt else _dot(av, wset[j])
        for k, c in enumerate(_set_groups(j)):
            s3[:, 8 * c:8 * (c + 1), :] = y[:, _cols(k, STATE_LANES)].reshape(tc // 8, 8, STATE_LANES)


def _chunk_token(s3, g, i):
    return s3[g, pl.ds(i, _STATE_TILE, stride=SCAN_BLOCK), :]


def _scan_fwd(u, bset, pw, *, name):
    S = u.shape[0]
    tc = _tile(S, _SCAN_CHUNK, 8)

    def body(u_ref, bset_ref, pw_ref, xs_ref, st_ref, s3):
        @pl.when(pl.program_id(0) == 0)
        def _():
            st_ref[...] = jnp.zeros_like(st_ref)

        _fill_chunk(s3, u_ref, bset_ref, tc, nt=False)
        ar = pw_ref[0, _RE, :]
        ai = pw_ref[0, _IM, :]

        def block(g, carry):
            xr, xi = carry
            cr = ci = nr = ni = None
            for j in range(SCAN_BLOCK):
                b = _chunk_token(s3, g, j)
                br, bi = b[_RE], b[_IM]
                cr, ci = (br, bi) if j == 0 else (ar * cr - ai * ci + br, ar * ci + ai * cr + bi)
                pr, pi = pw_ref[j, _RE, :], pw_ref[j, _IM, :]
                nr = pr * xr - pi * xi + cr
                ni = pr * xi + pi * xr + ci
                xs_ref[_token(g, j, STATE_ROWS), :] = _pack_state(nr, ni)
            return nr, ni

        xr, xi = lax.fori_loop(0, tc // SCAN_BLOCK, block, (st_ref[_RE, :], st_ref[_IM, :]), unroll=4)
        st_ref[_RE, :] = xr
        st_ref[_IM, :] = xi

    return pl.pallas_call(
        body, name=name, grid=(S // tc,),
        in_specs=[pl.BlockSpec((tc, u.shape[1]), lambda i: (i, 0)), pl.BlockSpec(bset.shape, lambda i: (0, 0, 0)),
                  pl.BlockSpec(pw.shape, lambda i: (0, 0, 0))],
        out_specs=pl.BlockSpec((tc * STATE_ROWS, STATE_LANES), lambda i: (i, 0)),
        out_shape=jax.ShapeDtypeStruct((S * STATE_ROWS, STATE_LANES), jnp.uint32),
        scratch_shapes=[pltpu.VMEM((2 * STATE_ROWS, STATE_LANES), F32),
                        pltpu.VMEM((tc // 8, _STATE_TILE * 8, STATE_LANES), F32)],
        compiler_params=_params(("arbitrary",)),
    )(u, bset, pw)


def _scan_bwd(dys, cset, xs, pw, *, name):
    S = dys.shape[0]
    tc = _tile(S, _SCAN_CHUNK, 8)
    nc = S // tc

    def body(dys_ref, cset_ref, xs_ref, pw_ref, g_ref, da_ref, st_ref, s3):
        @pl.when(pl.program_id(0) == 0)
        def _():
            st_ref[...] = jnp.zeros_like(st_ref)
            da_ref[...] = jnp.zeros_like(da_ref)

        _fill_chunk(s3, dys_ref, cset_ref, tc, nt=True)
        ar = pw_ref[0, _RE, :]
        ai = pw_ref[0, _IM, :]

        def block(k, carry):
            gr, gi, dar, dai = carry
            g = tc // SCAN_BLOCK - 1 - k
            cr = ci = None
            pgr, pgi = gr, gi
            for j in range(SCAN_BLOCK):
                i = SCAN_BLOCK - 1 - j
                xr, xi = _unpack_state(xs_ref[_token(g, i, STATE_ROWS), :])
                dar = dar + pgr * xr + pgi * xi
                dai = dai + pgi * xr - pgr * xi
                d = _chunk_token(s3, g, i)
                dr, di = d[_RE], d[_IM]
                cr, ci = (dr, di) if j == 0 else (ar * cr + ai * ci + dr, ar * ci - ai * cr + di)
                pr, pi = pw_ref[j, _RE, :], pw_ref[j, _IM, :]
                pgr = pr * gr + pi * gi + cr
                pgi = pr * gi - pi * gr + ci
                g_ref[_token(g, i, STATE_ROWS), :] = _pack_state(pgr, pgi)
            return pgr, pgi, dar, dai

        init = (st_ref[_RE, :], st_ref[_IM, :], da_ref[_RE, :], da_ref[_IM, :])
        gr, gi, dar, dai = lax.fori_loop(0, tc // SCAN_BLOCK, block, init, unroll=4)
        st_ref[_RE, :] = gr
        st_ref[_IM, :] = gi
        da_ref[_RE, :] = dar
        da_ref[_IM, :] = dai

    packed = pl.BlockSpec((tc * STATE_ROWS, STATE_LANES), lambda i: (nc - 1 - i, 0))
    vec = pl.BlockSpec((2 * STATE_ROWS, STATE_LANES), lambda i: (0, 0))
    return pl.pallas_call(
        body, name=name, grid=(nc,),
        in_specs=[pl.BlockSpec((tc, dys.shape[1]), lambda i: (nc - 1 - i, 0)),
                  pl.BlockSpec(cset.shape, lambda i: (0, 0, 0)), packed, pl.BlockSpec(pw.shape, lambda i: (0, 0, 0))],
        out_specs=[packed, vec],
        out_shape=[jax.ShapeDtypeStruct(xs.shape, jnp.uint32), jax.ShapeDtypeStruct((2 * STATE_ROWS, STATE_LANES), F32)],
        scratch_shapes=[pltpu.VMEM((2 * STATE_ROWS, STATE_LANES), F32),
                        pltpu.VMEM((tc // 8, _STATE_TILE * 8, STATE_LANES), F32)],
        compiler_params=_params(("arbitrary",)),
    )(dys, cset, xs, pw)


def _down_loss_head(h, ws, x, g, target, *, name, tm=512):
    S, D = x.shape

    def body(r, f, o, acc, s):
        xv = r[1][...]
        for p in range(N_CHIPS):
            xv = xv + _dot(r[0][p], f[1 + p][...])
        gv = f[0][...]
        rs = lax.rsqrt(_mean(xv * xv) + EPS)
        xh = xv * rs
        err = xh * gv - r[2][...]
        acc[1][...] += 0.5 * jnp.sum(_mean(err * err), axis=0, keepdims=True)
        dy = err * (1.0 / D)
        dyg = dy * gv
        o[0][...] = rs * (dyg - xh * _mean(dyg * xh))
        acc[0][...] += jnp.sum(dy * xh, axis=0, keepdims=True)

    (dx,), (dg, loss) = _rows_call(name, _tile(S, tm), [h, x, target], [_vec(g)] + ws, [(D, F32)],
                                   [((1, D), F32), ((1, 128), F32)], body)
    return dx, dg, loss


_ADAM_C1 = 1.0 - ADAM_B1 ** ADAM_STEP
_ADAM_C2 = 1.0 - ADAM_B2 ** ADAM_STEP
_ONE_BLOCK_BYTES = 8 * 1024 * 1024
_ADAM_ROWS = 512


def _adamw_math(w, g, m, v):
    nm = ADAM_B1 * m + (1.0 - ADAM_B1) * g
    nv = ADAM_B2 * v + (1.0 - ADAM_B2) * (g * g)
    m_hat = nm / _ADAM_C1
    v_hat = nv / _ADAM_C2
    return -ADAM_LR * (m_hat / (jnp.sqrt(v_hat) + ADAM_EPS) + ADAM_WD * w), nm, nv


def _adamw_shard(w, gsrc, m, v, *, name):
    R, C = w.shape
    n_l = len(gsrc)
    rows = R // n_l
    tr = rows
    for _, r0 in gsrc:
        tr = math.gcd(tr, r0) if r0 else tr
    tr = _tile(tr, _ADAM_ROWS, 8)
    nb = rows // tr
    assert rows % tr == 0 and all(r0 % tr == 0 for _, r0 in gsrc)

    def body(*refs):
        w_ref, g_refs, (m_ref, v_ref, go_ref, d_ref, nm_ref, nv_ref) = refs[0], refs[1:1 + n_l], refs[1 + n_l:]
        layer = pl.program_id(0) // nb
        gv = g_refs[0][...]
        for l in range(1, n_l):
            gv = jnp.where(layer == l, g_refs[l][...], gv)
        go_ref[...] = gv
        d_ref[...], nm_ref[...], nv_ref[...] = _adamw_math(w_ref[...], gv, m_ref[...], v_ref[...])

    def g_spec(l, r0):
        return pl.BlockSpec((tr, C), lambda i: (r0 // tr + jnp.clip(i - l * nb, 0, nb - 1), 0))

    blk = pl.BlockSpec((tr, C), lambda i: (i, 0))
    out = jax.ShapeDtypeStruct((R, C), F32)
    return pl.pallas_call(
        body, name=name, grid=(R // tr,),
        in_specs=[blk] + [g_spec(l, r0) for l, (_, r0) in enumerate(gsrc)] + [blk, blk], out_specs=[blk] * 4,
        out_shape=[out] * 4, compiler_params=_params(("parallel",)),
    )(w, *[g for g, _ in gsrc], m, v)


def _adamw_small(ws, gs, ms, vs, *, name):
    n = len(ws)

    def body(*refs):
        w_r, g_r, m_r, v_r = refs[:n], refs[n:2 * n], refs[2 * n:3 * n], refs[3 * n:4 * n]
        d_r, nm_r, nv_r = refs[4 * n:5 * n], refs[5 * n:6 * n], refs[6 * n:7 * n]
        for k in range(n):
            d_r[k][...], nm_r[k][...], nv_r[k][...] = _adamw_math(w_r[k][...], g_r[k][...], m_r[k][...], v_r[k][...])

    vm = pl.BlockSpec(memory_space=pltpu.VMEM)
    out = [jax.ShapeDtypeStruct(w.shape, F32) for w in ws]
    res = pl.pallas_call(body, name=name, in_specs=[vm] * (4 * n), out_specs=[vm] * (3 * n), out_shape=out * 3,
                         compiler_params=pltpu.CompilerParams(vmem_limit_bytes=VMEM_LIMIT))(*ws, *gs, *ms, *vs)
    return res[:n], res[n:2 * n], res[2 * n:]


def _sum_slots(x, *, name):
    n, R, C = x.shape
    tr = R if (n + 1) * R * C * 4 <= _ONE_BLOCK_BYTES else _tile(R, 256, 8)

    def body(x_ref, o_ref):
        acc = x_ref[0]
        for k in range(1, n):
            acc = acc + x_ref[k]
        o_ref[...] = acc

    return pl.pallas_call(
        body, name=name, grid=(R // tr,),
        in_specs=[pl.BlockSpec((n, tr, C), lambda i: (0, i, 0))], out_specs=pl.BlockSpec((tr, C), lambda i: (i, 0)),
        out_shape=jax.ShapeDtypeStruct((R, C), F32), compiler_params=_params(("parallel",)),
    )(x)


def _pair_sum(g, r, where, *, name):
    n, R, C = g.shape
    Rh = R // 2
    tr = _tile(Rh, 256, 8)
    nb = Rh // tr

    def body(where_ref, g_ref, r_ref, o_ref):
        o_ref[...] = (g_ref[...] + r_ref[...]).astype(BF16)

    def slot(p, w):
        return p + jnp.where(p >= w[0], 1, 0)

    return pl.pallas_call(
        body, name=name,
        grid_spec=pltpu.PrefetchScalarGridSpec(
            num_scalar_prefetch=1, grid=(n - 1, nb),
            in_specs=[pl.BlockSpec((1, tr, C), lambda p, i, w: (slot(p, w), w[1] * nb + i, 0)),
                      pl.BlockSpec((1, tr, C), lambda p, i, w: (slot(p, w), i, 0))],
            out_specs=pl.BlockSpec((1, tr, C), lambda p, i, w: (slot(p, w), i, 0)),
        ),
        out_shape=jax.ShapeDtypeStruct((n, Rh, C), BF16), compiler_params=_params(("parallel", "parallel")),
    )(where, g, r)


def _chip_sum(g, r, slots, where, *, name):
    n, R, C = g.shape
    Rh = R // 2
    tr = _tile(Rh, 256, 8)
    nb = Rh // tr

    def body(w_ref, g_ref, r_ref, s_ref, o_ref):
        acc = g_ref[0] + r_ref[0]
        for k in range(slots.shape[0]):
            acc = acc + s_ref[k].astype(F32)
        o_ref[...] = acc

    return pl.pallas_call(
        body, name=name,
        grid_spec=pltpu.PrefetchScalarGridSpec(
            num_scalar_prefetch=1, grid=(nb,),
            in_specs=[pl.BlockSpec((1, tr, C), lambda i, w: (w[0], w[1] * nb + i, 0)),
                      pl.BlockSpec((1, tr, C), lambda i, w: (w[0], i, 0)),
                      pl.BlockSpec((slots.shape[0], tr, C), lambda i, w: (0, i, 0))],
            out_specs=pl.BlockSpec((tr, C), lambda i, w: (w[1] * nb + i, 0)),
        ),
        out_shape=jax.ShapeDtypeStruct((R, C), F32), compiler_params=_params(("parallel",)),
    )(where, g, r, slots)


ANY = pl.BlockSpec(memory_space=pl.ANY)


def _place():
    return lax.axis_index("x"), lax.axis_index("y"), lax.axis_index("c")


def _other_chips(x, y):
    return [(1 - x, y), (x, 1 - y), (1 - x, 1 - y)]


def _aliased_comm_call(body, bufs, n_sems, *, name):
    n = len(bufs)
    return pl.pallas_call(
        body, name=name, out_shape=[jax.ShapeDtypeStruct(b.shape, b.dtype) for b in bufs],
        in_specs=[ANY] * n, out_specs=[ANY] * n, input_output_aliases={k: k for k in range(n)},
        scratch_shapes=[pltpu.SemaphoreType.DMA((n_sems,)), pltpu.SemaphoreType.DMA((n_sems,))],
    )(*bufs)


HBM = pl.BlockSpec(memory_space=pltpu.HBM)
SEM = pl.BlockSpec(memory_space=pltpu.SEMAPHORE)
_SPLIT = pltpu.CompilerParams(has_side_effects=pltpu.SideEffectType.DATAFLOW_SIDE_EFFECTING)


def _in_hbm(arrs):
    return [pltpu.with_memory_space_constraint(a, pltpu.HBM) for a in arrs]


def _gather_ici_start(bufs, after, *, name):
    n = len(bufs)

    def body(*refs):
        send_sems, recv_sems, outs, token = refs[n + 1], refs[n + 2], refs[n + 3:2 * n + 3], refs[2 * n + 3]
        x, y, c = _place()
        for b in range(n):
            rh = bufs[b].shape[1] // 2
            part = outs[b].at[2 * x + y, pl.ds(c * rh, rh), :]
            for j, chip in enumerate(_other_chips(x, y)):
                pltpu.make_async_remote_copy(src_ref=part, dst_ref=part, send_sem=send_sems.at[3 * b + j],
                                             recv_sem=recv_sems.at[3 * b + j], device_id=(*chip, c),
                                             device_id_type=MESH).start()
        token[...] = jnp.zeros_like(token)

    res = pl.pallas_call(
        body, name=name,
        out_shape=(pltpu.SemaphoreType.DMA((3 * n,)), pltpu.SemaphoreType.DMA((3 * n,)),
                   *[pltpu.HBM(b.shape, b.dtype) for b in bufs], jax.ShapeDtypeStruct((8, 128), F32)),
        in_specs=[HBM] * n + [ANY], out_specs=(SEM, SEM, *[HBM] * n, pl.BlockSpec(memory_space=pltpu.VMEM)),
        input_output_aliases={k: k + 2 for k in range(n)}, compiler_params=_SPLIT,
    )(*_in_hbm(bufs), after)
    return res[0], res[1], list(res[2:2 + n]), res[2 + n]


def _gather_ici_wait(send_sems, recv_sems, bufs, first, after, *, name):
    n = len(bufs)

    def body(*refs):
        ins, ss, rs = refs[:n], refs[n], refs[n + 1]
        x, y, c = _place()
        for b in range(n):
            rh = bufs[b].shape[1] // 2
            mine = ins[b].at[2 * x + y, pl.ds(c * rh, rh), :]
            for j, (cx, cy) in enumerate(_other_chips(x, y)):
                theirs = ins[b].at[2 * cx + cy, pl.ds(c * rh, rh), :]
                cp = pltpu.make_async_remote_copy(src_ref=mine, dst_ref=theirs, send_sem=ss.at[3 * (first + b) + j],
                                                  recv_sem=rs.at[3 * (first + b) + j], device_id=(cx, cy, c),
                                                  device_id_type=MESH)
                cp.wait_send()
                cp.wait_recv()

    return list(pl.pallas_call(
        body, name=name, out_shape=[pltpu.HBM(b.shape, b.dtype) for b in bufs],
        in_specs=[HBM] * n + [SEM, SEM, ANY], out_specs=[HBM] * n,
        input_output_aliases={k: k for k in range(n)}, compiler_params=_SPLIT,
    )(*bufs, send_sems, recv_sems, after))


def _gather_forward(bufs, *, name):
    n = len(bufs)

    def body(*refs):
        outs, send_sems, recv_sems = refs[n:2 * n], refs[2 * n], refs[2 * n + 1]
        x, y, c = _place()

        def copy(b, j, chip, hc):
            rh = bufs[b].shape[1] // 2
            part = outs[b].at[2 * chip[0] + chip[1], pl.ds(hc * rh, rh), :]
            return pltpu.make_async_remote_copy(src_ref=part, dst_ref=part, send_sem=send_sems.at[3 * b + j],
                                                recv_sem=recv_sems.at[3 * b + j], device_id=(x, y, 1 - c),
                                                device_id_type=MESH)

        sends = [copy(b, j, chip, c) for b in range(n) for j, chip in enumerate(_other_chips(x, y))]
        for cp in sends:
            cp.start()
        for b in range(n):
            for j, chip in enumerate(_other_chips(x, y)):
                copy(b, j, chip, 1 - c).wait_recv()
        for cp in sends:
            cp.wait_send()

    return _aliased_comm_call(body, bufs, 3 * n, name=name)


def _forward_copy(buf, send_sems, recv_sems, k, chip, half, to):
    rh = buf.shape[1] // 2
    part = buf.at[2 * chip[0] + chip[1], pl.ds(half * rh, rh), :]
    return pltpu.make_async_remote_copy(src_ref=part, dst_ref=part, send_sem=send_sems.at[k], recv_sem=recv_sems.at[k],
                                        device_id=to, device_id_type=MESH)


def _gather_forward_start(bufs, after, *, name):
    n = len(bufs)

    def body(*refs):
        send_sems, recv_sems, outs, token = refs[n + 1], refs[n + 2], refs[n + 3:2 * n + 3], refs[2 * n + 3]
        x, y, c = _place()
        for b in range(n):
            for j, chip in enumerate(_other_chips(x, y)):
                _forward_copy(outs[b], send_sems, recv_sems, 3 * b + j, chip, c, (x, y, 1 - c)).start()
        token[...] = jnp.zeros_like(token)

    res = pl.pallas_call(
        body, name=name,
        out_shape=(pltpu.SemaphoreType.DMA((3 * n,)), pltpu.SemaphoreType.DMA((3 * n,)),
                   *[pltpu.HBM(b.shape, b.dtype) for b in bufs], jax.ShapeDtypeStruct((8, 128), F32)),
        in_specs=[HBM] * n + [ANY], out_specs=(SEM, SEM, *[HBM] * n, pl.BlockSpec(memory_space=pltpu.VMEM)),
        input_output_aliases={k: k + 2 for k in range(n)}, compiler_params=_SPLIT,
    )(*_in_hbm(bufs), after)
    return res[0], res[1], list(res[2:2 + n]), res[2 + n]


def _gather_forward_wait(send_sems, recv_sems, bufs, after, *, name):
    n = len(bufs)

    def body(*refs):
        ins, ss, rs = refs[:n], refs[n], refs[n + 1]
        x, y, c = _place()
        for b in range(n):
            for j, chip in enumerate(_other_chips(x, y)):
                _forward_copy(ins[b], ss, rs, 3 * b + j, chip, c, (x, y, 1 - c)).wait_send()
                _forward_copy(ins[b], ss, rs, 3 * b + j, chip, 1 - c, (x, y, 1 - c)).wait_recv()

    return list(pl.pallas_call(
        body, name=name, out_shape=[pltpu.HBM(b.shape, b.dtype) for b in bufs],
        in_specs=[HBM] * n + [SEM, SEM, ANY], out_specs=[HBM] * n,
        input_output_aliases={k: k for k in range(n)}, compiler_params=_SPLIT,
    )(*bufs, send_sems, recv_sems, after))


def _chip_exchange_start(hs, *, name):
    n = len(hs)
    lands = [lax.empty((3,) + h.shape[1:], h.dtype) for h in hs]

    def body(*refs):
        send_sems, recv_sems = refs[2 * n], refs[2 * n + 1]
        h_out, l_out, token = refs[2 * n + 2:3 * n + 2], refs[3 * n + 2:4 * n + 2], refs[4 * n + 2]
        x, y, c = _place()
        for b in range(n):
            for j, (cx, cy) in enumerate(_other_chips(x, y)):
                pltpu.make_async_remote_copy(src_ref=h_out[b].at[2 * cx + cy], dst_ref=l_out[b].at[j],
                                             send_sem=send_sems.at[3 * b + j], recv_sem=recv_sems.at[3 * b + j],
                                             device_id=(cx, cy, c), device_id_type=MESH).start()
        token[...] = jnp.zeros_like(token)

    res = pl.pallas_call(
        body, name=name,
        out_shape=(pltpu.SemaphoreType.DMA((3 * n,)), pltpu.SemaphoreType.DMA((3 * n,)),
                   *[pltpu.HBM(a.shape, a.dtype) for a in hs + lands], jax.ShapeDtypeStruct((8, 128), F32)),
        in_specs=[HBM] * (2 * n), out_specs=(SEM, SEM, *[HBM] * (2 * n), pl.BlockSpec(memory_space=pltpu.VMEM)),
        input_output_aliases={k: k + 2 for k in range(2 * n)}, compiler_params=_SPLIT,
    )(*_in_hbm(hs + lands))
    return res[0], res[1], list(res[2:2 + n]), list(res[2 + n:2 + 2 * n]), res[2 + 2 * n]


def _chip_exchange_wait(send_sems, recv_sems, hs, lands, after, *, name):
    n = len(hs)

    def body(*refs):
        h_in, l_in, ss, rs = refs[:n], refs[n:2 * n], refs[2 * n], refs[2 * n + 1]
        x, y, c = _place()
        for b in range(n):
            for j, (cx, cy) in enumerate(_other_chips(x, y)):
                cp = pltpu.make_async_remote_copy(src_ref=h_in[b].at[2 * cx + cy], dst_ref=l_in[b].at[j],
                                                  send_sem=ss.at[3 * b + j], recv_sem=rs.at[3 * b + j],
                                                  device_id=(cx, cy, c), device_id_type=MESH)
                cp.wait_send()
                cp.wait_recv()

    res = pl.pallas_call(
        body, name=name, out_shape=[pltpu.HBM(a.shape, a.dtype) for a in hs + lands],
        in_specs=[HBM] * (2 * n) + [SEM, SEM, ANY], out_specs=[HBM] * (2 * n),
        input_output_aliases={k: k for k in range(2 * n)}, compiler_params=_SPLIT,
    )(*hs, *lands, send_sems, recv_sems, after)
    return list(res[n:])


def _peers(x, y, c):
    return [((1 - x) if fx else x, (1 - y) if fy else y, (1 - c) if fc else c)
            for fx in (0, 1) for fy in (0, 1) for fc in (0, 1) if fx or fy or fc]


def _all_to_all_start(slab, after, *, name):
    land = lax.empty((N_DEV,) + slab.shape, slab.dtype)

    def body(slab_in, land_in, after_ref, send_sems, recv_sems, slab_out, land_out, token):
        x, y, c = _place()
        for k, peer in enumerate(_peers(x, y, c)):
            pltpu.make_async_remote_copy(src_ref=slab_out, dst_ref=land_out.at[4 * x + 2 * y + c],
                                         send_sem=send_sems.at[k], recv_sem=recv_sems.at[k], device_id=peer,
                                         device_id_type=MESH).start()
        token[...] = jnp.zeros_like(token)

    return pl.pallas_call(
        body, name=name,
        out_shape=(pltpu.SemaphoreType.DMA((N_DEV - 1,)), pltpu.SemaphoreType.DMA((N_DEV - 1,)),
                   pltpu.HBM(slab.shape, slab.dtype), pltpu.HBM(land.shape, land.dtype),
                   jax.ShapeDtypeStruct((8, 128), F32)),
        in_specs=[HBM, HBM, ANY], out_specs=(SEM, SEM, HBM, HBM, pl.BlockSpec(memory_space=pltpu.VMEM)),
        input_output_aliases={0: 2, 1: 3}, compiler_params=_SPLIT,
    )(*_in_hbm([slab, land]), after)


def _all_to_all_wait(send_sems, recv_sems, slab, land, afters, *, name):
    def body(slab_in, land_in, ss, rs, *_):
        x, y, c = _place()
        for k, (px, py, pc) in enumerate(_peers(x, y, c)):
            cp = pltpu.make_async_remote_copy(src_ref=slab_in, dst_ref=land_in.at[4 * px + 2 * py + pc],
                                              send_sem=ss.at[k], recv_sem=rs.at[k], device_id=(px, py, pc),
                                              device_id_type=MESH)
            cp.wait_send()
            cp.wait_recv()

    return pl.pallas_call(
        body, name=name, out_shape=[pltpu.HBM(slab.shape, slab.dtype), pltpu.HBM(land.shape, land.dtype)],
        in_specs=[HBM, HBM, SEM, SEM] + [ANY] * len(afters), out_specs=[HBM, HBM], input_output_aliases={0: 0, 1: 1},
        compiler_params=_SPLIT,
    )(slab, land, send_sems, recv_sems, *afters)


def _pair_exchange_start(gs, *, name):
    n = len(gs)
    lands = [lax.empty((g.shape[0], g.shape[1] // 2, g.shape[2]), g.dtype) for g in gs]

    def body(*refs):
        send_sems, recv_sems = refs[2 * n], refs[2 * n + 1]
        g_out, l_out, token = refs[2 * n + 2:3 * n + 2], refs[3 * n + 2:4 * n + 2], refs[4 * n + 2]
        x, y, c = _place()
        for b in range(n):
            rh = gs[b].shape[1] // 2
            pltpu.make_async_remote_copy(src_ref=g_out[b].at[:, pl.ds((1 - c) * rh, rh), :], dst_ref=l_out[b],
                                         send_sem=send_sems.at[b], recv_sem=recv_sems.at[b],
                                         device_id=(x, y, 1 - c), device_id_type=MESH).start()
        token[...] = jnp.zeros_like(token)

    res = pl.pallas_call(
        body, name=name,
        out_shape=(pltpu.SemaphoreType.DMA((n,)), pltpu.SemaphoreType.DMA((n,)),
                   *[pltpu.HBM(a.shape, a.dtype) for a in gs + lands], jax.ShapeDtypeStruct((8, 128), F32)),
        in_specs=[HBM] * (2 * n), out_specs=(SEM, SEM, *[HBM] * (2 * n), pl.BlockSpec(memory_space=pltpu.VMEM)),
        input_output_aliases={k: k + 2 for k in range(2 * n)}, compiler_params=_SPLIT,
    )(*_in_hbm(gs + lands))
    return res[0], res[1], list(res[2:2 + n]), list(res[2 + n:2 + 2 * n]), res[2 + 2 * n]


def _pair_exchange_wait(send_sems, recv_sems, gs, lands, after, *, name):
    n = len(gs)

    def body(*refs):
        g_in, l_in, ss, rs = refs[:n], refs[n:2 * n], refs[2 * n], refs[2 * n + 1]
        x, y, c = _place()
        for b in range(n):
            rh = gs[b].shape[1] // 2
            cp = pltpu.make_async_remote_copy(src_ref=g_in[b].at[:, pl.ds((1 - c) * rh, rh), :], dst_ref=l_in[b],
                                              send_sem=ss.at[b], recv_sem=rs.at[b], device_id=(x, y, 1 - c),
                                              device_id_type=MESH)
            cp.wait_send()
            cp.wait_recv()

    res = pl.pallas_call(
        body, name=name, out_shape=[pltpu.HBM(a.shape, a.dtype) for a in gs + lands],
        in_specs=[HBM] * (2 * n) + [SEM, SEM, ANY], out_specs=[HBM] * (2 * n),
        input_output_aliases={k: k for k in range(2 * n)}, compiler_params=_SPLIT,
    )(*gs, *lands, send_sems, recv_sems, after)
    return list(res[:n]), list(res[n:])


def _pair_share_start(ss, *, name):
    n = len(ss)

    def body(*refs):
        send_sems, recv_sems, outs, token = refs[n], refs[n + 1], refs[n + 2:2 * n + 2], refs[2 * n + 2]
        x, y, c = _place()
        for b in range(n):
            rh = ss[b].shape[0] // 2
            mine = outs[b].at[pl.ds(c * rh, rh), :]
            pltpu.make_async_remote_copy(src_ref=mine, dst_ref=mine, send_sem=send_sems.at[b],
                                         recv_sem=recv_sems.at[b], device_id=(x, y, 1 - c),
                                         device_id_type=MESH).start()
        token[...] = jnp.zeros_like(token)

    res = pl.pallas_call(
        body, name=name,
        out_shape=(pltpu.SemaphoreType.DMA((n,)), pltpu.SemaphoreType.DMA((n,)),
                   *[pltpu.HBM(a.shape, a.dtype) for a in ss], jax.ShapeDtypeStruct((8, 128), F32)),
        in_specs=[HBM] * n, out_specs=(SEM, SEM, *[HBM] * n, pl.BlockSpec(memory_space=pltpu.VMEM)),
        input_output_aliases={k: k + 2 for k in range(n)}, compiler_params=_SPLIT,
    )(*_in_hbm(ss))
    return res[0], res[1], list(res[2:2 + n]), res[2 + n]


def _pair_share_wait(send_sems, recv_sems, ss, after, *, name):
    n = len(ss)

    def body(*refs):
        ins, sems_s, sems_r = refs[:n], refs[n], refs[n + 1]
        x, y, c = _place()
        for b in range(n):
            rh = ss[b].shape[0] // 2
            mine = ins[b].at[pl.ds(c * rh, rh), :]
            theirs = ins[b].at[pl.ds((1 - c) * rh, rh), :]
            cp = pltpu.make_async_remote_copy(src_ref=mine, dst_ref=theirs, send_sem=sems_s.at[b],
                                              recv_sem=sems_r.at[b], device_id=(x, y, 1 - c),
                                              device_id_type=MESH)
            cp.wait_send()
            cp.wait_recv()

    return list(pl.pallas_call(
        body, name=name, out_shape=[pltpu.HBM(a.shape, a.dtype) for a in ss],
        in_specs=[HBM] * n + [SEM, SEM, ANY], out_specs=[HBM] * n,
        input_output_aliases={k: k for k in range(n)}, compiler_params=_SPLIT,
    )(*ss, send_sems, recv_sems, after))


_SMALL_SHARDED = (("e_conv_w", 2), ("o_norm", 1), ("o_d", 1))
_REPLICATED = ("e_norm", "e_gmlp_w", "e_gmlp_b", "e_conv_b", "e_conv_ln_g", "e_conv_ln_b", "o_lam_re", "o_lam_im",
               "o_log_dt", "o_b_re", "o_b_im", "o_c_re", "o_c_im", "ca_norm", "ca_mem_norm", "ffn_norm", "final_norm")
_SMALL = tuple(n for n, _ in _SMALL_SHARDED) + _REPLICATED
_WEIGHTS = ("e_norm", "e_w_in", "e_gmlp_w", "e_gmlp_b", "e_conv_w", "e_conv_b", "e_conv_ln_g", "e_conv_ln_b",
            "e_w_out", "o_norm", "o_w_in", "o_lam_re", "o_lam_im", "o_log_dt", "o_b_re", "o_b_im", "o_c_re", "o_c_im",
            "o_d", "o_w_out", "ca_norm", "ca_mem_norm", "ca_wq", "ca_wk", "ca_wv", "ca_wo", "ffn_norm", "ffn_w_gate",
            "ffn_w_up", "ffn_w_down", "final_norm")


def _pack_rows(arrs, width, dtype, row_mult=8):
    parts, spans, r0 = [], [], 0
    for a in arrs:
        flat = a.reshape(-1).astype(dtype)
        rows = -(-flat.shape[0] // (width * row_mult)) * row_mult
        if rows * width != flat.shape[0]:
            flat = jnp.pad(flat, (0, rows * width - flat.shape[0]))
        parts.append(flat.reshape(rows, width))
        spans.append((r0, rows))
        r0 += rows
    return jnp.concatenate(parts, axis=0), spans


def _unpack_rows(slab, spans, shapes):
    out = []
    for (r0, rows), shp in zip(spans, shapes):
        n = math.prod(shp)
        out.append(slab[r0:r0 + rows].reshape(-1)[:n].reshape(shp))
    return out


def _two_d(a):
    return a.reshape(-1, a.shape[-1])


def _shard_rows(n, a):
    return _two_d(jnp.swapaxes(a, -1, -2) if n in _TRANSPOSED else a)


def _from_shard_rows(n, rows, shape):
    if n in _TRANSPOSED:
        return jnp.swapaxes(rows.reshape(shape[:-2] + (shape[-1], shape[-2])), -1, -2)
    return rows.reshape(shape)


def _local_slab(local, slab, dtype):
    parts = sorted((r0, n, l) for n, (_, where) in _PLACE.items() for l, (s, r0) in enumerate(where) if s == slab)
    shards = [_shard_rows(n, local[n] if len(_PLACE[n][1]) == 1 else local[n][l]) for _, n, l in parts]
    return jnp.concatenate([a.astype(dtype) for a in shards], axis=0)


def _set_diag(b, pattern):
    return jnp.einsum(pattern, b, jnp.eye(C_GROUPS // N_SETS, dtype=b.dtype))


def _s5_discretize(lam_re, lam_im, log_dt, b_re, b_im):
    dt = jnp.exp(log_dt)[:, None]
    mag = jnp.exp(lam_re * dt)
    ar = mag * jnp.cos(lam_im * dt)
    ai = mag * jnp.sin(lam_im * dt)
    den = lam_re * lam_re + lam_im * lam_im
    qr = ((ar - 1.0) * lam_re + ai * lam_im) / den
    qi = (ai * lam_re - (ar - 1.0) * lam_im) / den
    bbr = qr[..., None] * b_re - qi[..., None] * b_im
    bbi = qr[..., None] * b_im + qi[..., None] * b_re
    return ar, ai, bbr, bbi


def _attention_block(x, mem, W, w, i, tag):
    xn, q = _norm_mm(x, w["ca_norm"][i], _shards(W, "ca_wq", i), split="k", out_dtype=BF16, name=f"{tag}_q")
    memn = _rms_fwd(mem, w["ca_mem_norm"][i], name=f"{tag}_ca_memnorm")
    k = _mm_k(memn, _shards(W, "ca_wk", i), out_dtype=BF16, name=f"{tag}_k")
    v = _mm_k(memn, _shards(W, "ca_wv", i), out_dtype=BF16, name=f"{tag}_v")
    o = _attn_fwd(q, k, v, name=f"{tag}_attn")
    y = _mm_k(o, _shards(W, "ca_wo", i), add=x, name=f"{tag}_wo")
    return y, (x, xn, memn, q, k, v, o)


def _attention_block_bwd(dy, saved, mem, W, w, i, tag, G, grads, token=None, mid=None):
    x, xn, memn, q, k, v, o = saved
    gain = w["ca_norm"][i]
    if token is not None:
        k = _behind(k, token)
    G = _grad_to_slab(G, "ca_wo", i, o, dy, a_cols=256, name=f"{tag}_dwo")
    dq, dk, dv = _attn_bwd(dy, _shards(W, "ca_wo", i), q, k, v, name=f"{tag}_attn_bwd")
    token = mid(dq) if mid is not None else None
    if token is not None:
        gain = _behind(gain, token)
    G = _grad_to_slab(G, "ca_wq", i, xn, dq, a_cols=256, name=f"{tag}_dwq")
    G = _grad_to_slab(G, "ca_wk", i, memn, dk, a_cols=256, name=f"{tag}_dwk")
    G = _grad_to_slab(G, "ca_wv", i, memn, dv, a_cols=256, name=f"{tag}_dwv")
    dmemn = _mm_k_t([(dk, _shards(W, "ca_wk", i)), (dv, _shards(W, "ca_wv", i))], name=f"{tag}_dmemn")
    dx, dg = _norm_bwd_k(dq, _shards(W, "ca_wq", i), x, gain, dy, name=f"{tag}_dq_norm_bwd")
    grads["ca_norm"][i] = dg[0]
    grads["ca_mem_norm"][i] = _rms_dg(mem, w["ca_mem_norm"][i], dmemn, name=f"{tag}_ca_memnorm_bwd")[0]
    return dx, G


def _ffn_block(x, W, w, i, tag, head=None):
    fn, gate, up, h = _ffn_up(x, w["ffn_norm"][i], _shards(W, "ffn_w_gate", i), _shards(W, "ffn_w_up", i),
                              name=f"{tag}_ffn_up")
    if head is None:
        y = _mm_k(h, _shards(W, "ffn_w_down", i), add=x, name=f"{tag}_down")
    else:
        y = _down_loss_head(h, _shards(W, "ffn_w_down", i), x, *head, name=f"{tag}_down_loss_head")
    return y, (x, fn, gate, up, h)


def _ffn_block_bwd(dy, saved, W, w, i, tag, G, grads, token=None, mid=None):
    x, fn, gate, up, h = saved
    gain = w["ffn_norm"][i]
    G = _grad_to_slab(G, "ffn_w_down", i, h, dy, name=f"{tag}_dwd")
    dg, du = _ffn_bwd_hidden(dy, _shards(W, "ffn_w_down", i), gate, up, token, name=f"{tag}_ffn_bwd_hidden")
    token = mid(dg) if mid is not None else None
    if token is not None:
        gain = _behind(gain, token)
    G = _grad_to_slab(G, "ffn_w_gate", i, dg, fn, name=f"{tag}_dwg")
    G = _grad_to_slab(G, "ffn_w_up", i, du, fn, name=f"{tag}_dwu")
    dx, dgn = _ffn_in_bwd(dg, du, _shards(W, "ffn_w_gate", i), _shards(W, "ffn_w_up", i), x, gain, dy,
                          name=f"{tag}_ffn_in_bwd")
    grads["ffn_norm"][i] = dgn[0]
    return dx, G


def _gmlp_mask():
    chunk = jnp.arange(GMLP_BLOCK) // CHUNK
    return chunk[None, :] <= chunk[:, None]


def _even_block(x, W, w, tag):
    hn, proj = _norm_mm(x, w["e_norm"][0], _shards(W, "e_w_in"), split="n", out_dtype=F32, name=f"{tag}_w_in")
    wm = jnp.where(_gmlp_mask()[None], w["e_gmlp_w"][0], 0.0).astype(BF16)
    bcol = w["e_gmlp_b"][0][:, :, None]
    cw = jnp.pad(w["e_conv_w"][0], ((0, CONV_HALO - CONV_WIDTH), (0, 0)))
    cb, lg, lb = w["e_conv_b"], w["e_conv_ln_g"], w["e_conv_ln_b"]
    mix, hc = _even_fwd(proj, wm, bcol, cw, cb, lg, lb, name=f"{tag}_mixers")
    y = _mm_k(mix, _shards(W, "e_w_out"), add=x, name=f"{tag}_w_out")
    return y, (x, hn, proj, mix, hc, wm, bcol, cw)


def _even_block_bwd(dy, saved, W, w, tag, G, grads):
    x, hn, proj, mix, hc, wm, bcol, cw = saved
    dmix = _mm_k_t([(dy, _shards(W, "e_w_out"))], name=f"{tag}_dmix")
    G = _grad_to_slab(G, "e_w_out", 0, mix, dy, a_cols=256, name=f"{tag}_dw_out")
    wmt = jnp.swapaxes(wm, 1, 2)
    dpa, dhc, dwm, db, dlg, dlb, dcb = _even_bwd1(proj, dmix, hc, wm, wmt, bcol, w["e_conv_ln_g"], w["e_conv_ln_b"],
                                                  name=f"{tag}_mixers_bwd1")
    dpb, dcw = _even_bwd2(proj, dhc, cw, name=f"{tag}_mixers_bwd2")
    grads["e_gmlp_w"] = jnp.where(_gmlp_mask()[None], dwm, 0.0)[None]
    grads["e_gmlp_b"] = db[:, :, 0][None]
    grads["e_conv_ln_g"], grads["e_conv_ln_b"], grads["e_conv_b"] = dlg, dlb, dcb
    grads["e_conv_w"] = dcw[:CONV_WIDTH][None]
    G = _grad_to_slab(G, "e_w_in", 0, hn, dpa, b_cols=512, chips=(0, 2), name=f"{tag}_dw_in_a")
    G = _grad_to_slab(G, "e_w_in", 0, hn, dpb, b_cols=512, chips=(2, 2), name=f"{tag}_dw_in_b")
    dx, dg = _norm_bwd_n((dpa, dpb), _shards(W, "e_w_in"), x, w["e_norm"][0], dy, name=f"{tag}_in_bwd")
    grads["e_norm"] = dg
    return dx, G


def _odd_block(x, W, w, tag):
    S = x.shape[0]
    hn, u = _norm_mm(x, w["o_norm"][0], _shards(W, "o_w_in"), split="k", out_dtype=F32, name=f"{tag}_w_in")
    disc_in = (w["o_lam_re"][0], w["o_lam_im"][0], w["o_log_dt"][0], w["o_b_re"][0], w["o_b_im"][0])
    (ar, ai, bbr, bbi), disc_vjp = jax.vjp(_s5_discretize, *disc_in)
    sets = (N_SETS, C_GROUPS // N_SETS)
    per_set = N_STATE // N_SETS
    bset = jnp.concatenate([_set_diag(b.reshape(sets + b.shape[1:]), "jgpc,gh->jgchp").reshape(N_SETS, SET_CH, per_set)
                            for b in (bbr, bbi)], axis=2).astype(BF16)
    cset = jnp.concatenate([_set_diag(c.reshape(sets + c.shape[1:]), "jgcp,gh->jgphc").reshape(N_SETS, per_set, SET_CH)
                            for c in (w["o_c_re"][0], -w["o_c_im"][0])], axis=1).astype(BF16)
    powers, pr, pi = [], ar, ai
    for _ in range(SCAN_BLOCK):
        powers.append(jnp.concatenate([pr.reshape(STATE_ROWS, STATE_LANES), pi.reshape(STATE_ROWS, STATE_LANES)], 0))
        pr, pi = pr * ar - pi * ai, pr * ai + pi * ar
    pw = jnp.stack(powers, axis=0)
    xs = _scan_fwd(u, bset, pw, name=f"{tag}_scan").reshape(S // 8, STATE_ROWS, 8, STATE_LANES)
    yv, yg = _s5_readout(xs, cset, u, w["o_d"], name=f"{tag}_readout")
    o, y = _glu_out(yg, _shards(W, "o_w_out"), x, name=f"{tag}_glu_out")
    return y, (x, hn, u, bset, cset, pw, xs, yv, yg, o, disc_vjp)


def _odd_block_bwd(dy, saved, W, w, tag, G, grads):
    x, hn, u, bset, cset, pw, xs, yv, yg, o, disc_vjp = saved
    S = x.shape[0]
    do, dys, dus, dd = _glu_out_bwd(o, dy, _shards(W, "o_w_out"), yv, u, w["o_d"], name=f"{tag}_glu_out_bwd")
    G = _grad_to_slab(G, "o_w_out", 0, yg, do, b_cols=512, name=f"{tag}_dw_out")
    grads["o_d"] = dd
    dcset_t = _state_grad_sets(dys, xs, name=f"{tag}_dcd")
    gs, da = _scan_bwd(dys, cset, xs.reshape(S * STATE_ROWS, STATE_LANES), pw, name=f"{tag}_scan_bwd")
    gs = gs.reshape(xs.shape)
    dbset = _state_grad_sets(u, gs, name=f"{tag}_dbd")
    du, dx, dg = _s5_in_bwd(gs, bset, dus, _shards(W, "o_w_in"), x, w["o_norm"][0], dy, name=f"{tag}_in_bwd")
    G = _grad_to_slab(G, "o_w_in", 0, hn, du, a_cols=256, name=f"{tag}_dw_in")
    grads["o_norm"] = dg
    per = C_GROUPS // N_SETS
    blocks = (N_SETS, per, C_GROUP_CH, 2, per, C_STATE)
    dc = _set_diag(dcset_t.reshape(blocks), "jhcrgp,gh->rjgcp").reshape(2, C_GROUPS, C_GROUP_CH, C_STATE)
    db = _set_diag(dbset.reshape(blocks), "jgcrhp,gh->rjgpc").reshape(2, C_GROUPS, C_STATE, C_GROUP_CH)
    dcr, dci, dbbr, dbbi = dc[0], -dc[1], db[0], db[1]
    dar = da[:STATE_ROWS].reshape(C_GROUPS, C_STATE)
    dai = da[STATE_ROWS:].reshape(C_GROUPS, C_STATE)
    dlr, dli, dldt, dbr, dbi = disc_vjp((dar, dai, dbbr, dbbi))
    grads["o_lam_re"], grads["o_lam_im"], grads["o_log_dt"] = dlr[None], dli[None], dldt[None]
    grads["o_b_re"], grads["o_b_im"], grads["o_c_re"], grads["o_c_im"] = dbr[None], dbi[None], dcr[None], dci[None]
    return dx, G


def _behind(value, token):
    return value + token[0, 0].astype(value.dtype)


class _NoExchange:
    def __init__(self, W):
        self.W = W

    def first_weights(self, w):
        return self.W, w

    def weights(self, stage, after):
        return {}

    def behind_late_start(self, w):
        return w

    def late_weights(self, after):
        return {}

    def grads_ready(self, piece, G):
        return None

    def grads_crossed(self, piece, after):
        return None


def _forward_backward(xs_, mems_, tgt, w, G, exchange):
    W, w = exchange.first_weights(w)
    x1, s_mix0 = _even_block(xs_, W, w, "l0")
    W = {**W, **exchange.weights(1, x1)}
    x2, s_att0 = _attention_block(x1, mems_, W, w, 0, "l0")
    W = {**W, **exchange.weights(2, x2)}
    x3, s_ffn0 = _ffn_block(x2, W, w, 0, "l0")
    W = {**W, **exchange.weights(3, x3)}
    w = exchange.behind_late_start(w)
    x4, s_mix1 = _odd_block(x3, W, w, "l1")
    W = {**W, **exchange.late_weights(x4)}
    x5, s_att1 = _attention_block(x4, mems_, W, w, 1, "l1")
    (dx, dfinal, loss_lanes), s_ffn1 = _ffn_block(x5, W, w, 1, "l1", head=(w["final_norm"], tgt))

    grads = {n: [None, None] for n in ("ca_norm", "ca_mem_norm", "ffn_norm")}
    grads["final_norm"] = dfinal[0]
    dx, G = _ffn_block_bwd(dx, s_ffn1, W, w, 1, "l1", G, grads)
    dx, G = _attention_block_bwd(dx, s_att1, mems_, W, w, 1, "l1", G, grads)
    dx, G = _odd_block_bwd(dx, s_mix1, W, w, "l1", G, grads)
    token = exchange.grads_ready("l1", G)
    dx, G = _ffn_block_bwd(dx, s_ffn0, W, w, 0, "l0", G, grads, token,
                           lambda after: exchange.grads_crossed("l1", after))
    token = exchange.grads_ready("ffn0", G)
    dx, G = _attention_block_bwd(dx, s_att0, mems_, W, w, 0, "l0", G, grads, token,
                                 lambda after: exchange.grads_crossed("ffn0", after))
    dx, G = _even_block_bwd(dx, s_mix0, W, w, "l0", G, grads)
    for n in list(grads):
        if isinstance(grads[n], list):
            grads[n] = jnp.stack(grads[n], axis=0)
        grads[n] = grads[n].reshape(w[n].shape)
    return loss_lanes, dx, G, grads


class _Exchange:
    def __init__(self, local, chip, core):
        self.bufs = {s: lax.dynamic_update_slice(lax.empty((N_CHIPS, rows, width), BF16),
                                                 _local_slab(local, s, BF16)[None], (chip, 0, 0))
                     for s, (width, rows) in _SLABS.items()}
        small = jnp.zeros((_SMALL_SLAB_ROWS, SMALL_W), F32)
        for n, (r0, rows) in _SMALL_PLACE.items():
            small = small.at[r0:r0 + rows].set(local[n].reshape(rows, SMALL_W))
        self.bufs[_SMALL_SLAB] = lax.dynamic_update_slice(lax.empty((N_CHIPS, _SMALL_SLAB_ROWS, SMALL_W), F32),
                                                          small[None], (chip, 0, 0))
        self.shard_shapes = {n: local[n].shape for n in _SMALL_PLACE}
        self.where = jnp.stack([chip, core]).astype(jnp.int32)
        self.reduces = {}

    def weights(self, stage, after):
        send_sems, recv_sems, flying = self.flight
        slabs = self.stage_slabs(stage)
        first = list(flying).index(slabs[0])
        bufs = _gather_ici_wait(send_sems, recv_sems, [flying[s] for s in slabs], first, after,
                                name=f"gather_stage{stage}_wait")
        now = [k for k, s in enumerate(slabs) if s not in _LATE_SLABS]
        late = [k for k, s in enumerate(slabs) if s in _LATE_SLABS]
        whole = _gather_forward([bufs[k] for k in now], name=f"gather_stage{stage}_forward")
        if late:
            *state, self.late_token = _gather_forward_start([bufs[k] for k in late], whole[0], name="gather_late_start")
            self.late = ([slabs[k] for k in late], *state)
        return dict(zip([slabs[k] for k in now], whole))

    def behind_late_start(self, w):
        return {**w, "o_norm": _behind(w["o_norm"], self.late_token)}

    def late_weights(self, after):
        slabs, send_sems, recv_sems, bufs = self.late
        return dict(zip(slabs, _gather_forward_wait(send_sems, recv_sems, bufs, after, name="gather_late_wait")))

    @staticmethod
    def stage_slabs(stage):
        return _STAGES[stage] + ((_SMALL_SLAB,) if stage == 0 else ())

    def first_weights(self, w):
        order = [s for k in range(len(_STAGES)) for s in self.stage_slabs(k)]
        send_sems, recv_sems, bufs, after = _gather_ici_start([self.bufs[s] for s in order], w["e_norm"],
                                                              name="gather_start")
        self.flight = (send_sems, recv_sems, dict(zip(order, bufs)))
        W = self.weights(0, after)
        w = {**w, "e_norm": _behind(w["e_norm"], after)}
        for (n, ax), (r0, rows) in zip(_SMALL_SHARDED, _SMALL_PLACE.values()):
            shards = [W[_SMALL_SLAB][p, r0:r0 + rows].reshape(self.shard_shapes[n]) for p in range(N_CHIPS)]
            w[n] = jnp.concatenate(shards, axis=ax)
        return W, w

    def pair_start(self, G, slabs, tag):
        send_sems, recv_sems, gl, lands, token = _pair_exchange_start([G[s] for s in slabs],
                                                                      name=f"grad_{tag}_pair_start")
        return (slabs, send_sems, recv_sems, gl, lands), token

    def pair_land(self, state, after, tag):
        slabs, send_sems, recv_sems, gl, lands = state
        gl, other = _pair_exchange_wait(send_sems, recv_sems, gl, lands, after, name=f"grad_{tag}_pair_wait")
        pairs = [_pair_sum(g, r, self.where, name=f"grad_pair_sum_{s}") for s, g, r in zip(slabs, gl, other)]
        send_sems, recv_sems, pairs, lands, token = _chip_exchange_start(pairs, name=f"grad_{tag}_chip_start")
        return (slabs, gl, other, send_sems, recv_sems, pairs, lands), token

    def reduce_sum(self, state, after, tag):
        slabs, gl, other, send_sems, recv_sems, pairs, lands = state
        slots = _chip_exchange_wait(send_sems, recv_sems, pairs, lands, after, name=f"grad_{tag}_chip_wait")
        return slabs, [_chip_sum(g, r, sl, self.where, name=f"grad_chip_sum_{s}")
                       for s, g, r, sl in zip(slabs, gl, other, slots)]

    @staticmethod
    def share_start(slabs, halves, tag):
        send_sems, recv_sems, halves, token = _pair_share_start(halves, name=f"grad_{tag}_share_start")
        return (slabs, send_sems, recv_sems, halves), token

    @staticmethod
    def share_finish(state, after, tag):
        slabs, send_sems, recv_sems, halves = state
        return dict(zip(slabs, _pair_share_wait(send_sems, recv_sems, halves, after, name=f"grad_{tag}_share_wait")))

    def grads_ready(self, piece, G):
        self.reduces[piece], token = self.pair_start(G, _GRAD_PIECES[piece], piece)
        return token

    def grads_crossed(self, piece, after):
        self.reduces[piece], token = self.pair_land(self.reduces[piece], after, piece)
        return token


def kernel(x, mem, e_norm, e_w_in, e_gmlp_w, e_gmlp_b, e_conv_w, e_conv_b, e_conv_ln_g, e_conv_ln_b, e_w_out, o_norm, o_w_in, o_lam_re, o_lam_im, o_log_dt, o_b_re, o_b_im, o_c_re, o_c_im, o_d, o_w_out, ca_norm, ca_mem_norm, ca_wq, ca_wk, ca_wv, ca_wo, ffn_norm, ffn_w_gate, ffn_w_up, ffn_w_down, final_norm, loss_target, m_e_norm, m_e_w_in, m_e_gmlp_w, m_e_gmlp_b, m_e_conv_w, m_e_conv_b, m_e_conv_ln_g, m_e_conv_ln_b, m_e_w_out, m_o_norm, m_o_w_in, m_o_lam_re, m_o_lam_im, m_o_log_dt, m_o_b_re, m_o_b_im, m_o_c_re, m_o_c_im, m_o_d, m_o_w_out, m_ca_norm, m_ca_mem_norm, m_ca_wq, m_ca_wk, m_ca_wv, m_ca_wo, m_ffn_norm, m_ffn_w_gate, m_ffn_w_up, m_ffn_w_down, m_final_norm, v_e_norm, v_e_w_in, v_e_gmlp_w, v_e_gmlp_b, v_e_conv_w, v_e_conv_b, v_e_conv_ln_g, v_e_conv_ln_b, v_e_w_out, v_o_norm, v_o_w_in, v_o_lam_re, v_o_lam_im, v_o_log_dt, v_o_b_re, v_o_b_im, v_o_c_re, v_o_c_im, v_o_d, v_o_w_out, v_ca_norm, v_ca_mem_norm, v_ca_wq, v_ca_wk, v_ca_wv, v_ca_wo, v_ffn_norm, v_ffn_w_gate, v_ffn_w_up, v_ffn_w_down, v_final_norm):
    args = dict(locals())
    local = {n: args[n] for n in _WEIGHTS}
    mom = {n: args["m_" + n] for n in _WEIGHTS}
    vel = {n: args["v_" + n] for n in _WEIGHTS}
    chip = 2 * lax.axis_index("x") + lax.axis_index("y")
    core = lax.axis_index("c")
    xs_, mems_, tgt = x[0], mem[0], loss_target[0]

    w = {n: local[n] for n in _REPLICATED}
    exchange = _Exchange(local, chip, core)
    G = {s: lax.empty((N_CHIPS, rows, width), F32) for s, (width, rows) in _SLABS.items()}
    loss_lanes, dx, G, grads = _forward_backward(xs_, mems_, tgt, w, G, exchange)

    gs_slab, gs_spans = _pack_rows([grads[n] for n in _SMALL] + [loss_lanes], SMALL_W, F32)
    rest0_token = exchange.grads_ready("rest0", G)
    small_flight = _all_to_all_start(gs_slab, rest0_token, name="small_grads_start")
    slabs_l1, halves_l1 = exchange.reduce_sum(exchange.reduces["l1"], small_flight[4], "l1")
    slabs_f0, halves_f0 = exchange.reduce_sum(exchange.reduces["ffn0"], small_flight[4], "ffn0")
    share, share_token = exchange.share_start(slabs_l1 + slabs_f0, halves_l1 + halves_f0, "l1_ffn0")
    token = exchange.grads_crossed("rest0", share_token)

    out_grads, delta, new_m, new_v = {}, {}, {}, {}

    def adamw_large(names):
        raw = []
        for n in names:
            shp = local[n].shape
            g_, d_, m_, v_ = _adamw_shard(_shard_rows(n, local[n]), [(gsum[s], r0) for s, r0 in _PLACE[n][1]],
                                          _shard_rows(n, mom[n]), _shard_rows(n, vel[n]), name=f"adamw_{n}")
            out_grads[n], delta[n], new_m[n], new_v[n] = (_from_shard_rows(n, t, shp) for t in (g_, d_, m_, v_))
            raw.append(d_)
        return raw

    gsum = exchange.share_finish(share, token, "l1_ffn0")
    ready = [n for n, (_, where) in _PLACE.items() if all(s in gsum for s, _ in where)]
    done = adamw_large(ready)

    gs_slab, gs_all = _all_to_all_wait(*small_flight[:4], done, name="small_grads_wait")
    gs_all = lax.dynamic_update_slice(gs_all, gs_slab[None], (2 * chip + core, 0, 0))
    gs_sum = _sum_slots(gs_all, name="small_grad_sum")
    *small_sums, loss_sum = _unpack_rows(gs_sum, gs_spans, [grads[n].shape for n in _SMALL] + [loss_lanes.shape])
    out_grads.update(zip(_SMALL, small_sums))
    for n, ax in _SMALL_SHARDED:
        width = local[n].shape[ax]
        out_grads[n] = lax.dynamic_slice_in_dim(out_grads[n], chip * width, width, axis=ax)
    d_, m_, v_ = _adamw_small([_two_d(local[n]) for n in _SMALL], [_two_d(out_grads[n]) for n in _SMALL],
                              [_two_d(mom[n]) for n in _SMALL], [_two_d(vel[n]) for n in _SMALL], name="adamw_small")
    for n, dd, mm_, vv in zip(_SMALL, d_, m_, v_):
        shp = local[n].shape
        delta[n], new_m[n], new_v[n] = dd.reshape(shp), mm_.reshape(shp), vv.reshape(shp)

    slabs_r0, halves_r0 = exchange.reduce_sum(exchange.reduces["rest0"], d_[0], "rest0")
    share, share_token = exchange.share_start(slabs_r0, halves_r0, "rest0")
    gsum = {**gsum, **exchange.share_finish(share, share_token, "rest0")}
    adamw_large([n for n in _PLACE if n not in ready])

    return (loss_sum[0, 0], dx[None], *[out_grads[n] for n in _WEIGHTS], *[delta[n] for n in _WEIGHTS],
            *[new_m[n] for n in _WEIGHTS], *[new_v[n] for n in _WEIGHTS])
```

```python
import functools
import math

import jax
import jax.numpy as jnp
from jax import lax
from jax.experimental import pallas as pl
from jax.experimental.pallas import tpu as pltpu

F32 = jnp.float32
BF16 = jnp.bfloat16
MESH = pl.DeviceIdType.MESH

EPS = 1e-6
D_MODEL = 1024
A_WIDTH = 512
A_GROUPS = 4
GMLP_BLOCK = 128
CHUNK = 64
B_WIDTH = 512
CONV_WIDTH = 31
CONV_HALO = 32
C_WIDTH = 512
C_GROUP_CH = 16
C_GROUPS = 32
C_STATE = 64
N_STATE = C_GROUPS * C_STATE
STATE_LANES = 128
STATE_ROWS = N_STATE // STATE_LANES
SCAN_BLOCK = 8
CA_HEADS = 4
CA_HEAD_DIM = 256
FFN_HIDDEN = 2816

ADAM_LR = 0.001
ADAM_B1 = 0.9
ADAM_B2 = 0.999
ADAM_EPS = 1e-08
ADAM_WD = 0.01
ADAM_STEP = 10

VMEM_LIMIT = 56 * 1024 * 1024
ACC_BYTES = 6 * 1024 * 1024
TN_VMEM_BYTES = 44 * 1024 * 1024
SMALL_W = 128
N_CHIPS = 4
N_DEV = 8

_SLABS = {"D0": (512, 1024), "E0": (1024, 256), "A0": (1024, 1024), "B0": (1024, 704), "C0": (1024, 1408),
          "D1": (512, 768), "A1": (1024, 1024), "B1": (1024, 704), "C1": (1024, 1408)}
_STAGES = (("D0", "E0"), ("A0",), ("B0", "C0"), ("D1", "A1", "B1", "C1"))
_LATE_SLABS = ("A1", "B1", "C1")
_GRAD_PIECES = {"l1": _STAGES[3], "ffn0": _STAGES[2], "rest0": _STAGES[0] + _STAGES[1]}
_PLACE = {
    "e_w_in": (1024, (("D0", 0),)), "e_w_out": (256, (("E0", 0),)),
    "o_w_out": (512, (("D1", 0),)), "o_w_in": (256, (("D1", 512),)),
    "ca_wq": (256, (("A0", 0), ("A1", 0))), "ca_wk": (256, (("A0", 256), ("A1", 256))),
    "ca_wv": (256, (("A0", 512), ("A1", 512))), "ca_wo": (256, (("A0", 768), ("A1", 768))),
    "ffn_w_down": (704, (("B0", 0), ("B1", 0))),
    "ffn_w_gate": (704, (("C0", 0), ("C1", 0))), "ffn_w_up": (704, (("C0", 704), ("C1", 704))),
}
_SMALL_SLAB = "F0"
_SMALL_SLAB_ROWS = 48
_SMALL_PLACE = {"e_conv_w": (0, 31), "o_norm": (32, 2), "o_d": (34, 1)}
_TRANSPOSED = ("ffn_w_gate", "ffn_w_up")


def _params(sem=None):
    return pltpu.CompilerParams(dimension_semantics=sem, vmem_limit_bytes=VMEM_LIMIT)


def _tile(n, pref, mult=128):
    if n <= pref:
        return n
    t = (pref // mult) * mult
    while t >= mult:
        if n % t == 0:
            return t
        t -= mult
    return n


def _blk(name, layer=0):
    rows, where = _PLACE[name]
    slab, r0 = where[layer]
    assert r0 % rows == 0
    return slab, rows, r0 // rows


def _shards(slabs, name, layer=0):
    slab, rows, b = _blk(name, layer)
    return [(slabs[slab], (None, rows, _SLABS[slab][0]), (p, b, 0)) for p in range(N_CHIPS)]


_GELU_C = 0.7978845608028654
_GELU_A = 0.044715


def _gelu(x):
    t = jnp.tanh(_GELU_C * (x + _GELU_A * (x * x * x)))
    return 0.5 * x * (1.0 + t), t


def _gelu_grad(x, t):
    return 0.5 * (1.0 + t) + 0.5 * x * (1.0 - t * t) * (_GELU_C * (1.0 + 3.0 * _GELU_A * x * x))


def _sigmoid(x):
    return 1.0 / (1.0 + jnp.exp(-x))


def _mean(x):
    return jnp.mean(x, axis=-1, keepdims=True)


def _dot(a, b):
    return jnp.dot(a, b, preferred_element_type=F32)


def _dot_nt(a, b):
    return lax.dot_general(a, b, (((1,), (1,)), ((), ())), preferred_element_type=F32)


def _dot_tn(a, b):
    return lax.dot_general(a, b, (((0,), (0,)), ((), ())), preferred_element_type=F32)


def _rms_tile(xv, gv):
    return (xv * lax.rsqrt(_mean(xv * xv) + EPS)) * gv


def _rms_bwd_tile(xv, gv, dyv):
    r = lax.rsqrt(_mean(xv * xv) + EPS)
    xh = xv * r
    dyg = dyv * gv
    return r * (dyg - xh * _mean(dyg * xh)), jnp.sum(dyv * xh, axis=0, keepdims=True)


def _cols(p, width):
    return slice(p * width, (p + 1) * width)


def _sum_k(a, ws, k):
    tot = None
    for p in range(N_CHIPS):
        y = _dot(a[:, _cols(p, k)], ws[p][...])
        tot = y if tot is None else tot + y
    return tot


def _cat_nt(a, ws):
    return jnp.concatenate([_dot_nt(a, ws[p][...]) for p in range(N_CHIPS)], axis=1)


def _rows_call(name, tm, rows, fulls, outs, accs, body, scratch=()):
    S = min(x.shape[-2] for x in rows if x.ndim != 4)
    nr, nf, no, na = len(rows), len(fulls), len(outs), len(accs)

    def kern(*refs):
        r, f = refs[:nr], refs[nr:nr + nf]
        o, a = refs[nr + nf:nr + nf + no], refs[nr + nf + no:nr + nf + no + na]
        if na:
            @pl.when(pl.program_id(0) == 0)
            def _():
                for ref in a:
                    ref[...] = jnp.zeros_like(ref)
        body(r, f, o, a, refs[nr + nf + no + na:])

    def whole(shape):
        nd = len(shape)
        return pl.BlockSpec(tuple(shape), lambda i: (0,) * nd)

    def row_spec(shape):
        if len(shape) == 4:
            return pl.BlockSpec((tm // 8,) + tuple(shape[1:]), lambda i: (i, 0, 0, 0))
        if len(shape) == 3:
            return pl.BlockSpec((shape[0], tm, shape[2]), lambda i: (0, i, 0))
        return pl.BlockSpec((tm, shape[1]), lambda i: (i, 0))

    def full_spec(x):
        if isinstance(x, tuple):
            _, bshape, bidx = x
            return pl.BlockSpec(bshape, lambda i: bidx, pipeline_mode=pl.Buffered(1))
        return whole(x.shape)

    out_shapes = [(S, o[0]) if len(o) == 2 else (o[0], S, o[1]) for o in outs]
    res = pl.pallas_call(
        kern, name=name, grid=(S // tm,),
        in_specs=[row_spec(x.shape) for x in rows] + [full_spec(x) for x in fulls],
        out_specs=[row_spec(s) for s in out_shapes] + [whole(shp) for shp, _ in accs],
        out_shape=[jax.ShapeDtypeStruct(s, o[-1]) for s, o in zip(out_shapes, outs)]
        + [jax.ShapeDtypeStruct(tuple(shp), dt) for shp, dt in accs],
        scratch_shapes=list(scratch),
        compiler_params=_params(("arbitrary",) if na else ("parallel",)),
    )(*rows, *[x[0] if isinstance(x, tuple) else x for x in fulls])
    return res[:no], res[no:]


def _grad_to_slab(gslabs, wname, layer, a, b, *, a_cols=None, b_cols=None, chips=(0, N_CHIPS), name):
    slab, rows, bidx = _blk(wname, layer)
    width = _SLABS[slab][0]
    p0, n_p = chips
    assert p0 % n_p == 0
    S = a.shape[-2]

    def tile_bytes(x, ts):
        return ts * x.dtype.itemsize * (x.shape[2] * n_p if x.ndim == 3 else x.shape[1])

    acc_bytes = n_p * rows * (-(-width // 128) * 128) * 4
    ts = next(t for t in (2048, 1024, 512, 256, S) if S % t == 0
              and 2 * (tile_bytes(a, t) + tile_bytes(b, t) + acc_bytes) <= TN_VMEM_BYTES or t == S)

    def operand(x):
        if x.ndim == 3:
            return pl.BlockSpec((n_p, ts, x.shape[2]), lambda s: (p0 // n_p, s, 0))
        return pl.BlockSpec((ts, x.shape[1]), lambda s: (s, 0))

    def part(ref, cols, p):
        if len(ref.shape) == 3:
            return ref[p]
        return ref[...] if cols is None else ref[:, _cols(p, cols)]

    def body(a_ref, b_ref, slab_ref, o_ref):
        @pl.when(pl.program_id(0) == 0)
        def _():
            o_ref[...] = jnp.zeros_like(o_ref)

        for p in range(n_p):
            o_ref[p] += _dot_tn(part(a_ref, a_cols, p).astype(BF16), part(b_ref, b_cols, p).astype(BF16))

    g = gslabs[slab]
    out = pl.pallas_call(
        body, name=name, grid=(S // ts,),
        in_specs=[operand(a), operand(b), pl.BlockSpec(memory_space=pl.ANY)],
        out_specs=pl.BlockSpec((n_p, rows, width), lambda s: (p0 // n_p, bidx, 0)),
        out_shape=jax.ShapeDtypeStruct(g.shape, F32), input_output_aliases={2: 0},
        compiler_params=_params(("arbitrary",)),
    )(a, b, g)
    return {**gslabs, slab: out}


def _vec(g):
    return g.reshape(1, -1)


def _norm_mm(x, g, ws, *, split, out_dtype, name, tm=512):
    S, D = x.shape
    k, n = ws[0][1][1], ws[0][1][2]
    N = n if split == "k" else N_CHIPS * n

    def body(r, f, o, acc, s):
        xn = _rms_tile(r[0][...], f[0][...]).astype(BF16)
        o[0][...] = xn
        if split == "k":
            o[1][...] = _sum_k(xn, f[1:], k).astype(out_dtype)
        else:
            for p in range(N_CHIPS):
                o[1][:, _cols(p, n)] = _dot(xn, f[1 + p][...]).astype(out_dtype)

    (xn, y), _ = _rows_call(name, _tile(S, tm), [x], [_vec(g)] + ws, [(D, BF16), (N, out_dtype)], [], body)
    return xn, y


def _mm_k(a, ws, *, add=None, out_dtype=F32, name, tm=512):
    S = a.shape[-2]
    k, n = ws[0][1][1], ws[0][1][2]
    has_add = add is not None

    def body(r, f, o, acc, s):
        if a.ndim == 3:
            y = None
            for p in range(N_CHIPS):
                t = _dot(r[0][p].astype(BF16), f[p][...])
                y = t if y is None else y + t
        else:
            y = _sum_k(r[0][...].astype(BF16), f, k)
        if has_add:
            y = y + r[1][...]
        o[0][...] = y.astype(out_dtype)

    (y,), _ = _rows_call(name, _tile(S, tm), [a] + ([add] if has_add else []), ws, [(n, out_dtype)], [], body)
    return y


def _mm_k_t(terms, *, out_dtype=F32, name, tm=512):
    S = terms[0][0].shape[0]
    k = terms[0][1][0][1][1]

    def body(r, f, o, acc, s):
        y = None
        for t in range(len(terms)):
            yt = _cat_nt(r[t][...].astype(BF16), f[N_CHIPS * t:N_CHIPS * (t + 1)])
            y = yt if y is None else y + yt
        o[0][...] = y.astype(out_dtype)

    (y,), _ = _rows_call(name, _tile(S, tm), [a for a, _ in terms], [w for _, ws in terms for w in ws],
                         [(N_CHIPS * k, out_dtype)], [], body)
    return y


def _rms_fwd(x, g, *, name):
    def body(r, f, o, acc, s):
        o[0][...] = _rms_tile(r[0][...], f[0][...]).astype(BF16)

    (y,), _ = _rows_call(name, _tile(x.shape[0], 256, 8), [x], [_vec(g)], [(x.shape[1], BF16)], [], body)
    return y


def _rms_dg(x, g, dy, *, name):
    def body(r, f, o, acc, s):
        acc[0][...] += _rms_bwd_tile(r[0][...], f[0][...], r[1][...])[1]

    _, (dg,) = _rows_call(name, _tile(x.shape[0], 256, 8), [x, dy], [_vec(g)], [], [((1, x.shape[1]), F32)], body)
    return dg


def _ffn_up(x, g, wg, wu, *, name, tm=512):
    S, D = x.shape
    h = wg[0][1][1]

    def body(r, f, o, acc, s):
        xn = _rms_tile(r[0][...], f[0][...]).astype(BF16)
        o[0][...] = xn
        for p in range(N_CHIPS):
            gate = _dot_nt(xn, f[1 + p][...])
            up = _dot_nt(xn, f[1 + N_CHIPS + p][...])
            o[1][p] = gate.astype(BF16)
            o[2][p] = up.astype(BF16)
            o[3][p] = (gate * _sigmoid(gate) * up).astype(BF16)

    (xn, gate, up, hid), _ = _rows_call(name, _tile(S, tm), [x], [_vec(g)] + wg + wu,
                                        [(D, BF16), (N_CHIPS, h, BF16), (N_CHIPS, h, BF16), (N_CHIPS, h, BF16)], [],
                                        body)
    return xn, gate, up, hid


def _ffn_bwd_hidden(dy, wd, gate, up, token=None, *, name, tm=512):
    S = dy.shape[0]
    h = wd[0][1][1]

    def body(r, f, o, acc, s):
        dyv = r[0][...]
        if token is not None:
            dyv = dyv + jnp.sum(f[N_CHIPS][...])
        dyb = dyv.astype(BF16)
        for p in range(N_CHIPS):
            dh = _dot_nt(dyb, f[p][...])
            gv = r[1][p].astype(F32)
            sg = _sigmoid(gv)
            o[0][p] = (dh * r[2][p].astype(F32) * (sg * (1.0 + gv * (1.0 - sg)))).astype(BF16)
            o[1][p] = (dh * gv * sg).astype(BF16)

    (dg, du), _ = _rows_call(name, _tile(S, tm), [dy, gate, up], wd + ([] if token is None else [token]),
                             [(N_CHIPS, h, BF16), (N_CHIPS, h, BF16)], [], body)
    return dg, du


def _ffn_in_bwd(dg, du, wg, wu, x, g, dres, *, name, tm=512):
    S, D = x.shape

    def body(r, f, o, acc, s):
        tot = None
        for p in range(N_CHIPS):
            y = _dot(r[0][p], f[1 + p][...]) + _dot(r[1][p], f[1 + N_CHIPS + p][...])
            tot = y if tot is None else tot + y
        dx, dgn = _rms_bwd_tile(r[2][...], f[0][...], tot)
        o[0][...] = dx + r[3][...]
        acc[0][...] += dgn

    (dx,), (dgn,) = _rows_call(name, _tile(S, tm), [dg, du, x, dres], [_vec(g)] + wg + wu, [(D, F32)],
                               [((1, D), F32)], body)
    return dx, dgn


def _norm_bwd_k(da, ws, x, g, dres, *, name, tm=512):
    S, D = x.shape

    def body(r, f, o, acc, s):
        dx, dg = _rms_bwd_tile(r[1][...], f[0][...], _cat_nt(r[0][...].astype(BF16), f[1:]))
        o[0][...] = dx + r[2][...]
        acc[0][...] += dg

    (dx,), (dg,) = _rows_call(name, _tile(S, tm), [da, x, dres], [_vec(g)] + ws, [(D, F32)], [((1, D), F32)], body)
    return dx, dg


def _norm_bwd_n(das, ws, x, g, dres, *, name, tm=256):
    S, D = x.shape
    n = ws[0][1][2]

    def body(r, f, o, acc, s):
        tot = None
        for p in range(N_CHIPS):
            y = _dot_nt(r[p // 2][:, _cols(p % 2, n)], f[1 + p][...])
            tot = y if tot is None else tot + y
        dx, dg = _rms_bwd_tile(r[2][...], f[0][...], tot)
        o[0][...] = dx + r[3][...]
        acc[0][...] += dg

    (dx,), (dg,) = _rows_call(name, _tile(S, tm), list(das) + [x, dres], [_vec(g)] + ws, [(D, F32)], [((1, D), F32)],
                              body)
    return dx, dg


def _ln_stats(v):
    mu = _mean(v)
    xc = v - mu
    rstd = lax.rsqrt(_mean(xc * xc) + EPS)
    return xc * rstd, rstd


_SHIFTS = 8
_CONV_ROWS = 64


def _fill_shifts(sh_ref, ext_ref, tm):
    sh_ref[0] = ext_ref[...]
    for s in range(1, _SHIFTS):
        sh_ref[s, 0:tm + CONV_HALO - _SHIFTS, :] = ext_ref[pl.ds(s, tm + CONV_HALO - _SHIFTS), :]


def _window(sh_ref, off, tm):
    return sh_ref[off % _SHIFTS, pl.ds(off - off % _SHIFTS, tm), :]


def _even_fwd(proj, wm, bcol, cw, cb, lg, lb, *, name):
    S = proj.shape[0]
    tm = _tile(S, 256)
    hb = tm // CONV_HALO
    nblk = tm // GMLP_BLOCK

    def body(p_ref, halo_ref, wm_ref, b_ref, cw_ref, cb_ref, lg_ref, lb_ref, mix_ref, hc_ref, hext_ref, hsh_ref):
        i = pl.program_id(0)
        gu, _ = _gelu(p_ref[:, 0:A_WIDTH])
        gv, _ = _gelu(p_ref[:, A_WIDTH:2 * A_WIDTH])
        vn, _ = _ln_stats(gv)
        vnb = vn.astype(BF16)
        for n in range(nblk):
            rows = slice(n * GMLP_BLOCK, (n + 1) * GMLP_BLOCK)
            for g in range(A_GROUPS):
                cols = slice(g * GMLP_BLOCK, (g + 1) * GMLP_BLOCK)
                sg = jnp.dot(wm_ref[g], vnb[rows, cols], preferred_element_type=F32) + b_ref[g]
                mix_ref[rows, cols] = (gu[rows, cols] * sg).astype(BF16)
        h = p_ref[:, 1024:1536] * _sigmoid(p_ref[:, 1536:2048])
        hh = halo_ref[:, 0:B_WIDTH] * _sigmoid(halo_ref[:, B_WIDTH:2 * B_WIDTH])
        hext_ref[0:CONV_HALO, :] = jnp.where(i > 0, hh, 0.0)
        hext_ref[CONV_HALO:CONV_HALO + tm, :] = h
        _fill_shifts(hsh_ref, hext_ref, tm)
        for r0 in range(0, tm, _CONV_ROWS):
            acc = jnp.zeros((_CONV_ROWS, B_WIDTH), F32)
            for k in range(CONV_WIDTH):
                acc = acc + cw_ref[k:k + 1, :] * _window(hsh_ref, r0 + k + CONV_HALO - CONV_WIDTH + 1, _CONV_ROWS)
            hc_ref[r0:r0 + _CONV_ROWS, :] = acc + cb_ref[...]
        hc = hc_ref[...]
        hhat, _ = _ln_stats(hc)
        hl = hhat * lg_ref[...] + lb_ref[...]
        mix_ref[:, A_WIDTH:A_WIDTH + B_WIDTH] = (hl * _sigmoid(hl)).astype(BF16)

    vec = pl.BlockSpec((1, B_WIDTH), lambda i: (0, 0))
    return pl.pallas_call(
        body, name=name, grid=(S // tm,),
        in_specs=[
            pl.BlockSpec((tm, 2048), lambda i: (i, 0)),
            pl.BlockSpec((CONV_HALO, 1024), lambda i: (jnp.maximum(i * hb - 1, 0), 1)),
            pl.BlockSpec((A_GROUPS, GMLP_BLOCK, GMLP_BLOCK), lambda i: (0, 0, 0)),
            pl.BlockSpec((A_GROUPS, GMLP_BLOCK, 1), lambda i: (0, 0, 0)),
            pl.BlockSpec((CONV_HALO, B_WIDTH), lambda i: (0, 0)),
            vec, vec, vec,
        ],
        out_specs=[pl.BlockSpec((tm, 1024), lambda i: (i, 0)), pl.BlockSpec((tm, B_WIDTH), lambda i: (i, 0))],
        out_shape=[jax.ShapeDtypeStruct((S, 1024), BF16), jax.ShapeDtypeStruct((S, B_WIDTH), F32)],
        scratch_shapes=[pltpu.VMEM((tm + CONV_HALO, B_WIDTH), F32),
                        pltpu.VMEM((_SHIFTS, tm + CONV_HALO, B_WIDTH), F32)],
        compiler_params=_params(("parallel",)),
    )(proj, proj, wm, bcol, cw, cb, lg, lb)


def _even_bwd1(proj, dmix, hc, wm, wmt, bcol, lg, lb, *, name):
    S = proj.shape[0]
    tm = _tile(S, 256)
    nblk = tm // GMLP_BLOCK

    def body(p_ref, dm_ref, hc_ref, wm_ref, wmt_ref, b_ref, lg_ref, lb_ref,
             dpa_ref, dhc_ref, dwm_ref, db_ref, dlg_ref, dlb_ref, dcb_ref, dgu_ref, dvn_ref):
        @pl.when(pl.program_id(0) == 0)
        def _():
            dwm_ref[...] = jnp.zeros_like(dwm_ref)
            db_ref[...] = jnp.zeros_like(db_ref)
            dlg_ref[...] = jnp.zeros_like(dlg_ref)
            dlb_ref[...] = jnp.zeros_like(dlb_ref)
            dcb_ref[...] = jnp.zeros_like(dcb_ref)

        au = p_ref[:, 0:A_WIDTH]
        av = p_ref[:, A_WIDTH:2 * A_WIDTH]
        gu, tu = _gelu(au)
        gv, tv = _gelu(av)
        vn, rstd = _ln_stats(gv)
        vnb = vn.astype(BF16)
        for n in range(nblk):
            rows = slice(n * GMLP_BLOCK, (n + 1) * GMLP_BLOCK)
            for g in range(A_GROUPS):
                cols = slice(g * GMLP_BLOCK, (g + 1) * GMLP_BLOCK)
                vb = vnb[rows, cols]
                sg = jnp.dot(wm_ref[g], vb, preferred_element_type=F32) + b_ref[g]
                da = dm_ref[rows, cols]
                dsg = da * gu[rows, cols]
                dgu_ref[rows, cols] = da * sg
                dsgb = dsg.astype(BF16)
                dwm_ref[g] += _dot_nt(dsgb, vb)
                db_ref[g] += jnp.sum(dsg, axis=1, keepdims=True)
                dvn_ref[rows, cols] = jnp.dot(wmt_ref[g], dsgb, preferred_element_type=F32)
        dvn = dvn_ref[...]
        dgv = rstd * (dvn - _mean(dvn) - vn * _mean(dvn * vn))
        dpa_ref[:, 0:A_WIDTH] = (dgu_ref[...] * _gelu_grad(au, tu)).astype(BF16)
        dpa_ref[:, A_WIDTH:2 * A_WIDTH] = (dgv * _gelu_grad(av, tv)).astype(BF16)
        hhat, rstd2 = _ln_stats(hc_ref[...])
        lgv = lg_ref[...]
        hl = hhat * lgv + lb_ref[...]
        s = _sigmoid(hl)
        dhl = dm_ref[:, A_WIDTH:A_WIDTH + B_WIDTH] * (s * (1.0 + hl * (1.0 - s)))
        dlg_ref[...] += jnp.sum(dhl * hhat, axis=0, keepdims=True)
        dlb_ref[...] += jnp.sum(dhl, axis=0, keepdims=True)
        dhh = dhl * lgv
        dhc = rstd2 * (dhh - _mean(dhh) - hhat * _mean(dhh * hhat))
        dcb_ref[...] += jnp.sum(dhc, axis=0, keepdims=True)
        dhc_ref[...] = dhc

    vec = pl.BlockSpec((1, B_WIDTH), lambda i: (0, 0))
    w3 = pl.BlockSpec((A_GROUPS, GMLP_BLOCK, GMLP_BLOCK), lambda i: (0, 0, 0))
    b3 = pl.BlockSpec((A_GROUPS, GMLP_BLOCK, 1), lambda i: (0, 0, 0))
    return pl.pallas_call(
        body, name=name, grid=(S // tm,),
        in_specs=[
            pl.BlockSpec((tm, 1024), lambda i: (i, 0)),
            pl.BlockSpec((tm, 1024), lambda i: (i, 0)),
            pl.BlockSpec((tm, B_WIDTH), lambda i: (i, 0)),
            w3, w3, b3, vec, vec,
        ],
        out_specs=[pl.BlockSpec((tm, 1024), lambda i: (i, 0)), pl.BlockSpec((tm, B_WIDTH), lambda i: (i, 0)),
                   w3, b3, vec, vec, vec],
        out_shape=[
            jax.ShapeDtypeStruct((S, 1024), BF16), jax.ShapeDtypeStruct((S, B_WIDTH), F32),
            jax.ShapeDtypeStruct((A_GROUPS, GMLP_BLOCK, GMLP_BLOCK), F32),
            jax.ShapeDtypeStruct((A_GROUPS, GMLP_BLOCK, 1), F32),
            jax.ShapeDtypeStruct((1, B_WIDTH), F32), jax.ShapeDtypeStruct((1, B_WIDTH), F32),
            jax.ShapeDtypeStruct((1, B_WIDTH), F32),
        ],
        scratch_shapes=[pltpu.VMEM((tm, A_WIDTH), F32), pltpu.VMEM((tm, A_WIDTH), F32)],
        compiler_params=_params(("arbitrary",)),
    )(proj, dmix, hc, wm, wmt, bcol, lg, lb)


def _even_bwd2(proj, dhc, cw, *, name):
    S = proj.shape[0]
    tm = _tile(S, 256)
    hb = tm // CONV_HALO
    nt = S // tm
    last_halo = S // CONV_HALO - 1
    lo = CONV_HALO - CONV_WIDTH + 1

    def body(p_ref, halo_ref, d_ref, dnext_ref, cw_ref, dpb_ref, dcw_ref, hext_ref, dext_ref, hsh_ref, dsh_ref):
        i = pl.program_id(0)

        @pl.when(i == 0)
        def _():
            dcw_ref[...] = jnp.zeros_like(dcw_ref)

        hh = halo_ref[:, 0:B_WIDTH] * _sigmoid(halo_ref[:, B_WIDTH:2 * B_WIDTH])
        hext_ref[0:CONV_HALO, :] = jnp.where(i > 0, hh, 0.0)
        hext_ref[CONV_HALO:CONV_HALO + tm, :] = p_ref[:, 0:B_WIDTH] * _sigmoid(p_ref[:, B_WIDTH:2 * B_WIDTH])
        dext_ref[0:tm, :] = d_ref[...]
        dext_ref[tm:tm + CONV_HALO, :] = jnp.where(i < nt - 1, dnext_ref[...], 0.0)
        _fill_shifts(hsh_ref, hext_ref, tm)
        _fill_shifts(dsh_ref, dext_ref, tm)
        for r0 in range(0, tm, _CONV_ROWS):
            rows = slice(r0, r0 + _CONV_ROWS)
            dhc_b = d_ref[rows, :]
            dh = jnp.zeros((_CONV_ROWS, B_WIDTH), F32)
            for k in range(CONV_WIDTH):
                dh = dh + cw_ref[k:k + 1, :] * _window(dsh_ref, r0 + CONV_WIDTH - 1 - k, _CONV_ROWS)
                dcw_ref[k:k + 1, :] += jnp.sum(dhc_b * _window(hsh_ref, r0 + k + lo, _CONV_ROWS), axis=0,
                                               keepdims=True)
            ba_b = p_ref[rows, 0:B_WIDTH]
            sg_b = _sigmoid(p_ref[rows, B_WIDTH:2 * B_WIDTH])
            dpb_ref[rows, 0:B_WIDTH] = (dh * sg_b).astype(BF16)
            dpb_ref[rows, B_WIDTH:2 * B_WIDTH] = (dh * ba_b * sg_b * (1.0 - sg_b)).astype(BF16)

    return pl.pallas_call(
        body, name=name, grid=(nt,),
        in_specs=[
            pl.BlockSpec((tm, 1024), lambda i: (i, 1)),
            pl.BlockSpec((CONV_HALO, 1024), lambda i: (jnp.maximum(i * hb - 1, 0), 1)),
            pl.BlockSpec((tm, B_WIDTH), lambda i: (i, 0)),
            pl.BlockSpec((CONV_HALO, B_WIDTH), lambda i: (jnp.minimum((i + 1) * hb, last_halo), 0)),
            pl.BlockSpec((CONV_HALO, B_WIDTH), lambda i: (0, 0)),
        ],
        out_specs=[pl.BlockSpec((tm, 1024), lambda i: (i, 0)), pl.BlockSpec((CONV_HALO, B_WIDTH), lambda i: (0, 0))],
        out_shape=[jax.ShapeDtypeStruct((S, 1024), BF16), jax.ShapeDtypeStruct((CONV_HALO, B_WIDTH), F32)],
        scratch_shapes=[pltpu.VMEM((tm + CONV_HALO, B_WIDTH), F32), pltpu.VMEM((tm + CONV_HALO, B_WIDTH), F32),
                        pltpu.VMEM((_SHIFTS, tm + CONV_HALO, B_WIDTH), F32),
                        pltpu.VMEM((_SHIFTS, tm + CONV_HALO, B_WIDTH), F32)],
        compiler_params=_params(("arbitrary",)),
    )(proj, proj, dhc, dhc, cw)


_CA_SCALE = CA_HEAD_DIM ** -0.5


def _softmax_rows(s):
    e = jnp.exp(s - jnp.max(s, axis=-1, keepdims=True))
    return e / jnp.sum(e, axis=-1, keepdims=True)


def _attn_fwd(q, k, v, *, name):
    S = q.shape[0]

    def body(r, f, o, acc, s):
        for h in range(CA_HEADS):
            cols = _cols(h, CA_HEAD_DIM)
            p = _softmax_rows(_dot_nt(r[0][:, cols], f[0][:, cols]) * _CA_SCALE)
            o[0][:, cols] = _dot(p.astype(BF16), f[1][:, cols]).astype(BF16)

    (o_,), _ = _rows_call(name, _tile(S, 512), [q], [k, v], [(D_MODEL, BF16)], [], body)
    return o_


def _attn_bwd(dy, wo, q, k, v, *, name):
    S = q.shape[0]
    M = k.shape[0]

    def body(r, f, o, acc, s):
        dyb = r[0][...].astype(BF16)
        for h in range(CA_HEADS):
            cols = _cols(h, CA_HEAD_DIM)
            qh = r[1][:, cols]
            kh = f[0][:, cols]
            vh = f[1][:, cols]
            doh = _dot_nt(dyb, f[2 + h][...]).astype(BF16)
            p = _softmax_rows(_dot_nt(qh, kh) * _CA_SCALE)
            acc[1][:, cols] += _dot_tn(p.astype(BF16), doh)
            dp = _dot_nt(doh, vh)
            ds = (p * (dp - jnp.sum(dp * p, axis=-1, keepdims=True)) * _CA_SCALE).astype(BF16)
            o[0][:, cols] = _dot(ds, kh).astype(BF16)
            acc[0][:, cols] += _dot_tn(ds, qh)

    (dq,), (dk, dv) = _rows_call(name, _tile(S, 512), [dy, q], [k, v] + wo, [(D_MODEL, BF16)],
                                 [((M, D_MODEL), F32), ((M, D_MODEL), F32)], body)
    return dq, dk, dv


_STATE_TILE = 2 * STATE_ROWS
N_SETS = 4
SET_CH = C_WIDTH // N_SETS
SET_COLS = N_STATE // N_SETS // STATE_LANES


def _set_groups(j):
    return [SET_COLS * j + c for c in range(SET_COLS)] + [STATE_ROWS + SET_COLS * j + c for c in range(SET_COLS)]


def _pack_state(re, im):
    hi = lax.bitcast_convert_type(re.astype(BF16).astype(F32), jnp.uint32)
    lo = lax.bitcast_convert_type(im.astype(BF16).astype(F32), jnp.uint32) >> 16
    return hi | lo


def _unpack_state(word):
    re = lax.bitcast_convert_type(word & jnp.uint32(0xFFFF0000), F32)
    im = lax.bitcast_convert_type(word << 16, F32)
    return re, im


def _state_set(ref, tm, j):
    parts = [_unpack_state(ref[:, SET_COLS * j + c, :, :].reshape(tm, STATE_LANES)) for c in range(SET_COLS)]
    return jnp.concatenate([p[0].astype(BF16) for p in parts] + [p[1].astype(BF16) for p in parts], axis=1)


def _s5_readout(xs, cset, u, d, *, name, tm=256):
    tm = _tile(u.shape[0], tm)

    def body(r, f, o, acc, s):
        y0 = jnp.concatenate([_dot(_state_set(r[0], tm, j), f[0][j]) for j in range(N_SETS)], axis=1)
        y = y0 + f[1][...] * r[1][...]
        o[0][...] = y
        o[1][...] = _gelu(y)[0].astype(BF16)

    (y, yg), _ = _rows_call(name, tm, [xs, u], [cset, d], [(C_WIDTH, F32), (C_WIDTH, BF16)], [], body)
    return y, yg


def _state_grad_sets(a, st, *, name, ts=256):
    ts = _tile(a.shape[0], ts)

    def body(r, f, o, acc, s):
        for j in range(N_SETS):
            acc[0][j] += _dot_tn(r[0][:, _cols(j, SET_CH)].astype(BF16), _state_set(r[1], ts, j))

    _, (out,) = _rows_call(name, ts, [a, st], [], [], [((N_SETS, SET_CH, 2 * N_STATE // N_SETS), F32)], body)
    return out


def _glu_out(yg, ws, x, *, name, tm=512):
    n = ws[0][1][2]

    def body(r, f, o, acc, s):
        ygv = r[0][...]
        ov = [_dot(ygv, f[p][...]) for p in range(N_CHIPS)]
        for p in range(N_CHIPS):
            o[0][:, _cols(p, n)] = ov[p].astype(BF16)
        for p in range(2):
            o[1][:, _cols(p, n)] = r[1][:, _cols(p, n)] + ov[p] * _sigmoid(ov[2 + p])

    (o_, y), _ = _rows_call(name, _tile(x.shape[0], tm), [yg, x], ws, [(2 * D_MODEL, BF16), (D_MODEL, F32)], [], body)
    return o_, y


def _glu_out_bwd(o_, dy, ws, y, u, d, *, name, tm=256):
    n = ws[0][1][2]

    def body(r, f, o, acc, s):
        o1 = r[0][:, 0:D_MODEL].astype(F32)
        sg = _sigmoid(r[0][:, D_MODEL:2 * D_MODEL].astype(F32))
        dyv = r[1][...]
        do1 = (dyv * sg).astype(BF16)
        do2 = (dyv * o1 * sg * (1.0 - sg)).astype(BF16)
        o[0][:, 0:D_MODEL] = do1
        o[0][:, D_MODEL:2 * D_MODEL] = do2
        dyg = None
        for p in range(N_CHIPS):
            t = _dot_nt((do1 if p < 2 else do2)[:, _cols(p % 2, n)], f[1 + p][...])
            dyg = t if dyg is None else dyg + t
        yv = r[2][...]
        dys = dyg * _gelu_grad(yv, _gelu(yv)[1])
        o[1][...] = dys.astype(BF16)
        o[2][...] = f[0][...] * dys
        acc[0][...] += jnp.sum(dys * r[3][...], axis=0, keepdims=True)

    (do, dys, dus), (dd,) = _rows_call(name, _tile(dy.shape[0], tm), [o_, dy, y, u], [d] + ws,
                                       [(2 * D_MODEL, BF16), (C_WIDTH, BF16), (C_WIDTH, F32)], [((1, C_WIDTH), F32)],
                                       body)
    return do, dys, dus, dd


def _s5_in_bwd(gs, bset, dus, ws, x, g, dres, *, name, tm=256):
    D = x.shape[1]
    tm = _tile(x.shape[0], tm)

    def body(r, f, o, acc, s):
        du0 = jnp.concatenate([_dot_nt(_state_set(r[0], tm, j), f[1][j]) for j in range(N_SETS)], axis=1)
        du = (du0 + r[1][...]).astype(BF16)
        o[0][...] = du
        dx, dg = _rms_bwd_tile(r[2][...], f[0][...], _cat_nt(du, f[2:]))
        o[1][...] = dx + r[3][...]
        acc[0][...] += dg

    (du, dx), (dg,) = _rows_call(name, tm, [gs, dus, x, dres], [_vec(g), bset] + ws,
                                 [(C_WIDTH, BF16), (D, F32)], [((1, D), F32)], body)
    return du, dx, dg


_SCAN_CHUNK = 256
_RE = slice(0, STATE_ROWS)
_IM = slice(STATE_ROWS, 2 * STATE_ROWS)
assert SCAN_BLOCK == 8


def _token(g, i, rows):
    return pl.ds(pl.multiple_of(g * (rows * SCAN_BLOCK), rows * SCAN_BLOCK) + i, rows, stride=SCAN_BLOCK)


def _fill_chunk(s3, a_ref, wset, tc, nt):
    for j in range(N_SETS):
        av = a_ref[:, _cols(j, SET_CH)].astype(BF16)
        y = _dot_nt(av, wset[j]) if nt else _dot(av, wset[j])
        for k, c in enumerate(_set_groups(j)):
            s3[:, 8 * c:8 * (c + 1), :] = y[:, _cols(k, STATE_LANES)].reshape(tc // 8, 8, STATE_LANES)


def _chunk_token(s3, g, i):
    return s3[g, pl.ds(i, _STATE_TILE, stride=SCAN_BLOCK), :]


def _scan_fwd(u, bset, pw, *, name):
    S = u.shape[0]
    tc = _tile(S, _SCAN_CHUNK, 8)

    def body(u_ref, bset_ref, pw_ref, xs_ref, st_ref, s3):
        @pl.when(pl.program_id(0) == 0)
        def _():
            st_ref[...] = jnp.zeros_like(st_ref)

        _fill_chunk(s3, u_ref, bset_ref, tc, nt=False)
        ar = pw_ref[0, _RE, :]
        ai = pw_ref[0, _IM, :]

        def block(g, carry):
            xr, xi = carry
            cr = ci = nr = ni = None
            for j in range(SCAN_BLOCK):
                b = _chunk_token(s3, g, j)
                br, bi = b[_RE], b[_IM]
                cr, ci = (br, bi) if j == 0 else (ar * cr - ai * ci + br, ar * ci + ai * cr + bi)
                pr, pi = pw_ref[j, _RE, :], pw_ref[j, _IM, :]
                nr = pr * xr - pi * xi + cr
                ni = pr * xi + pi * xr + ci
                xs_ref[_token(g, j, STATE_ROWS), :] = _pack_state(nr, ni)
            return nr, ni

        xr, xi = lax.fori_loop(0, tc // SCAN_BLOCK, block, (st_ref[_RE, :], st_ref[_IM, :]), unroll=4)
        st_ref[_RE, :] = xr
        st_ref[_IM, :] = xi

    return pl.pallas_call(
        body, name=name, grid=(S // tc,),
        in_specs=[pl.BlockSpec((tc, u.shape[1]), lambda i: (i, 0)), pl.BlockSpec(bset.shape, lambda i: (0, 0, 0)),
                  pl.BlockSpec(pw.shape, lambda i: (0, 0, 0))],
        out_specs=pl.BlockSpec((tc * STATE_ROWS, STATE_LANES), lambda i: (i, 0)),
        out_shape=jax.ShapeDtypeStruct((S * STATE_ROWS, STATE_LANES), jnp.uint32),
        scratch_shapes=[pltpu.VMEM((2 * STATE_ROWS, STATE_LANES), F32),
                        pltpu.VMEM((tc // 8, _STATE_TILE * 8, STATE_LANES), F32)],
        compiler_params=_params(("arbitrary",)),
    )(u, bset, pw)


def _scan_bwd(dys, cset, xs, pw, *, name):
    S = dys.shape[0]
    tc = _tile(S, _SCAN_CHUNK, 8)
    nc = S // tc

    def body(dys_ref, cset_ref, xs_ref, pw_ref, g_ref, da_ref, st_ref, s3):
        @pl.when(pl.program_id(0) == 0)
        def _():
            st_ref[...] = jnp.zeros_like(st_ref)
            da_ref[...] = jnp.zeros_like(da_ref)

        _fill_chunk(s3, dys_ref, cset_ref, tc, nt=True)
        ar = pw_ref[0, _RE, :]
        ai = pw_ref[0, _IM, :]

        def block(k, carry):
            gr, gi, dar, dai = carry
            g = tc // SCAN_BLOCK - 1 - k
            cr = ci = None
            pgr, pgi = gr, gi
            for j in range(SCAN_BLOCK):
                i = SCAN_BLOCK - 1 - j
                xr, xi = _unpack_state(xs_ref[_token(g, i, STATE_ROWS), :])
                dar = dar + pgr * xr + pgi * xi
                dai = dai + pgi * xr - pgr * xi
                d = _chunk_token(s3, g, i)
                dr, di = d[_RE], d[_IM]
                cr, ci = (dr, di) if j == 0 else (ar * cr + ai * ci + dr, ar * ci - ai * cr + di)
                pr, pi = pw_ref[j, _RE, :], pw_ref[j, _IM, :]
                pgr = pr * gr + pi * gi + cr
                pgi = pr * gi - pi * gr + ci
                g_ref[_token(g, i, STATE_ROWS), :] = _pack_state(pgr, pgi)
            return pgr, pgi, dar, dai

        init = (st_ref[_RE, :], st_ref[_IM, :], da_ref[_RE, :], da_ref[_IM, :])
        gr, gi, dar, dai = lax.fori_loop(0, tc // SCAN_BLOCK, block, init, unroll=4)
        st_ref[_RE, :] = gr
        st_ref[_IM, :] = gi
        da_ref[_RE, :] = dar
        da_ref[_IM, :] = dai

    packed = pl.BlockSpec((tc * STATE_ROWS, STATE_LANES), lambda i: (nc - 1 - i, 0))
    vec = pl.BlockSpec((2 * STATE_ROWS, STATE_LANES), lambda i: (0, 0))
    return pl.pallas_call(
        body, name=name, grid=(nc,),
        in_specs=[pl.BlockSpec((tc, dys.shape[1]), lambda i: (nc - 1 - i, 0)),
                  pl.BlockSpec(cset.shape, lambda i: (0, 0, 0)), packed, pl.BlockSpec(pw.shape, lambda i: (0, 0, 0))],
        out_specs=[packed, vec],
        out_shape=[jax.ShapeDtypeStruct(xs.shape, jnp.uint32), jax.ShapeDtypeStruct((2 * STATE_ROWS, STATE_LANES), F32)],
        scratch_shapes=[pltpu.VMEM((2 * STATE_ROWS, STATE_LANES), F32),
                        pltpu.VMEM((tc // 8, _STATE_TILE * 8, STATE_LANES), F32)],
        compiler_params=_params(("arbitrary",)),
    )(dys, cset, xs, pw)


def _down_loss_head(h, ws, x, g, target, *, name, tm=512):
    S, D = x.shape

    def body(r, f, o, acc, s):
        xv = r[1][...]
        for p in range(N_CHIPS):
            xv = xv + _dot(r[0][p], f[1 + p][...])
        gv = f[0][...]
        rs = lax.rsqrt(_mean(xv * xv) + EPS)
        xh = xv * rs
        err = xh * gv - r[2][...]
        acc[1][...] += 0.5 * jnp.sum(_mean(err * err), axis=0, keepdims=True)
        dy = err * (1.0 / D)
        dyg = dy * gv
        o[0][...] = rs * (dyg - xh * _mean(dyg * xh))
        acc[0][...] += jnp.sum(dy * xh, axis=0, keepdims=True)

    (dx,), (dg, loss) = _rows_call(name, _tile(S, tm), [h, x, target], [_vec(g)] + ws, [(D, F32)],
                                   [((1, D), F32), ((1, 128), F32)], body)
    return dx, dg, loss


_ADAM_C1 = 1.0 - ADAM_B1 ** ADAM_STEP
_ADAM_C2 = 1.0 - ADAM_B2 ** ADAM_STEP
_ONE_BLOCK_BYTES = 8 * 1024 * 1024
_SUM_ROWS = 512
_ADAM_ROWS = 512


def _adamw_math(w, g, m, v):
    nm = ADAM_B1 * m + (1.0 - ADAM_B1) * g
    nv = ADAM_B2 * v + (1.0 - ADAM_B2) * (g * g)
    m_hat = nm / _ADAM_C1
    v_hat = nv / _ADAM_C2
    return -ADAM_LR * (m_hat / (jnp.sqrt(v_hat) + ADAM_EPS) + ADAM_WD * w), nm, nv


def _adamw_shard(w, gsrc, m, v, *, name):
    R, C = w.shape
    n_l = len(gsrc)
    rows = R // n_l
    tr = rows
    for _, r0 in gsrc:
        tr = math.gcd(tr, r0) if r0 else tr
    tr = _tile(tr, _ADAM_ROWS, 8)
    nb = rows // tr
    assert rows % tr == 0 and all(r0 % tr == 0 for _, r0 in gsrc)

    def body(*refs):
        w_ref, g_refs, (m_ref, v_ref, go_ref, d_ref, nm_ref, nv_ref) = refs[0], refs[1:1 + n_l], refs[1 + n_l:]
        layer = pl.program_id(0) // nb
        gv = g_refs[0][...]
        for l in range(1, n_l):
            gv = jnp.where(layer == l, g_refs[l][...], gv)
        go_ref[...] = gv
        d_ref[...], nm_ref[...], nv_ref[...] = _adamw_math(w_ref[...], gv, m_ref[...], v_ref[...])

    def g_spec(l, r0):
        return pl.BlockSpec((tr, C), lambda i: (r0 // tr + jnp.clip(i - l * nb, 0, nb - 1), 0))

    blk = pl.BlockSpec((tr, C), lambda i: (i, 0))
    out = jax.ShapeDtypeStruct((R, C), F32)
    return pl.pallas_call(
        body, name=name, grid=(R // tr,),
        in_specs=[blk] + [g_spec(l, r0) for l, (_, r0) in enumerate(gsrc)] + [blk, blk], out_specs=[blk] * 4,
        out_shape=[out] * 4, compiler_params=_params(("parallel",)),
    )(w, *[g for g, _ in gsrc], m, v)


def _adamw_small(ws, gs, ms, vs, *, name):
    n = len(ws)

    def body(*refs):
        w_r, g_r, m_r, v_r = refs[:n], refs[n:2 * n], refs[2 * n:3 * n], refs[3 * n:4 * n]
        d_r, nm_r, nv_r = refs[4 * n:5 * n], refs[5 * n:6 * n], refs[6 * n:7 * n]
        for k in range(n):
            d_r[k][...], nm_r[k][...], nv_r[k][...] = _adamw_math(w_r[k][...], g_r[k][...], m_r[k][...], v_r[k][...])

    vm = pl.BlockSpec(memory_space=pltpu.VMEM)
    out = [jax.ShapeDtypeStruct(w.shape, F32) for w in ws]
    res = pl.pallas_call(body, name=name, in_specs=[vm] * (4 * n), out_specs=[vm] * (3 * n), out_shape=out * 3,
                         compiler_params=pltpu.CompilerParams(vmem_limit_bytes=VMEM_LIMIT))(*ws, *gs, *ms, *vs)
    return res[:n], res[n:2 * n], res[2 * n:]


def _sum_slots(x, *, name):
    n, R, C = x.shape
    tr = R if (n + 1) * R * C * 4 <= _ONE_BLOCK_BYTES else _tile(R, 256, 8)

    def body(x_ref, o_ref):
        acc = x_ref[0]
        for k in range(1, n):
            acc = acc + x_ref[k]
        o_ref[...] = acc

    return pl.pallas_call(
        body, name=name, grid=(R // tr,),
        in_specs=[pl.BlockSpec((n, tr, C), lambda i: (0, i, 0))], out_specs=pl.BlockSpec((tr, C), lambda i: (i, 0)),
        out_shape=jax.ShapeDtypeStruct((R, C), F32), compiler_params=_params(("parallel",)),
    )(x)


def _pair_sum(g, r, where, *, name):
    n, R, C = g.shape
    Rh = R // 2
    tr = _tile(Rh, _SUM_ROWS, 16)
    nb = Rh // tr

    def body(where_ref, g_ref, r_ref, o_ref):
        o_ref[...] = (g_ref[...] + r_ref[...]).astype(BF16)

    def slot(p, w):
        return p + jnp.where(p >= w[0], 1, 0)

    return pl.pallas_call(
        body, name=name,
        grid_spec=pltpu.PrefetchScalarGridSpec(
            num_scalar_prefetch=1, grid=(n - 1, nb),
            in_specs=[pl.BlockSpec((1, tr, C), lambda p, i, w: (slot(p, w), w[1] * nb + i, 0)),
                      pl.BlockSpec((1, tr, C), lambda p, i, w: (slot(p, w), i, 0))],
            out_specs=pl.BlockSpec((1, tr, C), lambda p, i, w: (slot(p, w), i, 0)),
        ),
        out_shape=jax.ShapeDtypeStruct((n, Rh, C), BF16), compiler_params=_params(("parallel", "parallel")),
    )(where, g, r)


def _chip_sum(g, r, slots, where, *, name):
    n, R, C = g.shape
    Rh = R // 2
    tr = _tile(Rh, _SUM_ROWS, 16)
    nb = Rh // tr

    def body(w_ref, g_ref, r_ref, s_ref, o_ref):
        acc = g_ref[0] + r_ref[0]
        for k in range(slots.shape[0]):
            acc = acc + s_ref[k].astype(F32)
        o_ref[...] = acc

    return pl.pallas_call(
        body, name=name,
        grid_spec=pltpu.PrefetchScalarGridSpec(
            num_scalar_prefetch=1, grid=(nb,),
            in_specs=[pl.BlockSpec((1, tr, C), lambda i, w: (w[0], w[1] * nb + i, 0)),
                      pl.BlockSpec((1, tr, C), lambda i, w: (w[0], i, 0)),
                      pl.BlockSpec((slots.shape[0], tr, C), lambda i, w: (0, i, 0))],
            out_specs=pl.BlockSpec((tr, C), lambda i, w: (w[1] * nb + i, 0)),
        ),
        out_shape=jax.ShapeDtypeStruct((R, C), F32), compiler_params=_params(("parallel",)),
    )(where, g, r, slots)


ANY = pl.BlockSpec(memory_space=pl.ANY)


def _place():
    return lax.axis_index("x"), lax.axis_index("y"), lax.axis_index("c")


def _other_chips(x, y):
    return [(1 - x, y), (x, 1 - y), (1 - x, 1 - y)]


def _aliased_comm_call(body, bufs, n_sems, *, name):
    n = len(bufs)
    return pl.pallas_call(
        body, name=name, out_shape=[jax.ShapeDtypeStruct(b.shape, b.dtype) for b in bufs],
        in_specs=[ANY] * n, out_specs=[ANY] * n, input_output_aliases={k: k for k in range(n)},
        scratch_shapes=[pltpu.SemaphoreType.DMA((n_sems,)), pltpu.SemaphoreType.DMA((n_sems,))],
    )(*bufs)


HBM = pl.BlockSpec(memory_space=pltpu.HBM)
SEM = pl.BlockSpec(memory_space=pltpu.SEMAPHORE)
_SPLIT = pltpu.CompilerParams(has_side_effects=pltpu.SideEffectType.DATAFLOW_SIDE_EFFECTING)


def _in_hbm(arrs):
    return [pltpu.with_memory_space_constraint(a, pltpu.HBM) for a in arrs]


def _gather_ici_start(bufs, after, *, name):
    n = len(bufs)

    def body(*refs):
        send_sems, recv_sems, outs, token = refs[n + 1], refs[n + 2], refs[n + 3:2 * n + 3], refs[2 * n + 3]
        x, y, c = _place()
        for b in range(n):
            rh = bufs[b].shape[1] // 2
            part = outs[b].at[2 * x + y, pl.ds(c * rh, rh), :]
            for j, chip in enumerate(_other_chips(x, y)):
                pltpu.make_async_remote_copy(src_ref=part, dst_ref=part, send_sem=send_sems.at[3 * b + j],
                                             recv_sem=recv_sems.at[3 * b + j], device_id=(*chip, c),
                                             device_id_type=MESH).start()
        token[...] = jnp.zeros_like(token)

    res = pl.pallas_call(
        body, name=name,
        out_shape=(pltpu.SemaphoreType.DMA((3 * n,)), pltpu.SemaphoreType.DMA((3 * n,)),
                   *[pltpu.HBM(b.shape, b.dtype) for b in bufs], jax.ShapeDtypeStruct((8, 128), F32)),
        in_specs=[HBM] * n + [ANY], out_specs=(SEM, SEM, *[HBM] * n, pl.BlockSpec(memory_space=pltpu.VMEM)),
        input_output_aliases={k: k + 2 for k in range(n)}, compiler_params=_SPLIT,
    )(*_in_hbm(bufs), after)
    return res[0], res[1], list(res[2:2 + n]), res[2 + n]


def _gather_ici_wait(send_sems, recv_sems, bufs, first, after, *, name):
    n = len(bufs)

    def body(*refs):
        ins, ss, rs = refs[:n], refs[n], refs[n + 1]
        x, y, c = _place()
        for b in range(n):
            rh = bufs[b].shape[1] // 2
            mine = ins[b].at[2 * x + y, pl.ds(c * rh, rh), :]
            for j, (cx, cy) in enumerate(_other_chips(x, y)):
                theirs = ins[b].at[2 * cx + cy, pl.ds(c * rh, rh), :]
                cp = pltpu.make_async_remote_copy(src_ref=mine, dst_ref=theirs, send_sem=ss.at[3 * (first + b) + j],
                                                  recv_sem=rs.at[3 * (first + b) + j], device_id=(cx, cy, c),
                                                  device_id_type=MESH)
                cp.wait_send()
                cp.wait_recv()

    return list(pl.pallas_call(
        body, name=name, out_shape=[pltpu.HBM(b.shape, b.dtype) for b in bufs],
        in_specs=[HBM] * n + [SEM, SEM, ANY], out_specs=[HBM] * n,
        input_output_aliases={k: k for k in range(n)}, compiler_params=_SPLIT,
    )(*bufs, send_sems, recv_sems, after))


def _gather_forward(bufs, *, name):
    n = len(bufs)

    def body(*refs):
        outs, send_sems, recv_sems = refs[n:2 * n], refs[2 * n], refs[2 * n + 1]
        x, y, c = _place()

        def copy(b, j, chip, hc):
            rh = bufs[b].shape[1] // 2
            part = outs[b].at[2 * chip[0] + chip[1], pl.ds(hc * rh, rh), :]
            return pltpu.make_async_remote_copy(src_ref=part, dst_ref=part, send_sem=send_sems.at[3 * b + j],
                                                recv_sem=recv_sems.at[3 * b + j], device_id=(x, y, 1 - c),
                                                device_id_type=MESH)

        sends = [copy(b, j, chip, c) for b in range(n) for j, chip in enumerate(_other_chips(x, y))]
        for cp in sends:
            cp.start()
        for b in range(n):
            for j, chip in enumerate(_other_chips(x, y)):
                copy(b, j, chip, 1 - c).wait_recv()
        for cp in sends:
            cp.wait_send()

    return _aliased_comm_call(body, bufs, 3 * n, name=name)


def _forward_copy(buf, send_sems, recv_sems, k, chip, half, to):
    rh = buf.shape[1] // 2
    part = buf.at[2 * chip[0] + chip[1], pl.ds(half * rh, rh), :]
    return pltpu.make_async_remote_copy(src_ref=part, dst_ref=part, send_sem=send_sems.at[k], recv_sem=recv_sems.at[k],
                                        device_id=to, device_id_type=MESH)


def _gather_forward_start(bufs, after, *, name):
    n = len(bufs)

    def body(*refs):
        send_sems, recv_sems, outs, token = refs[n + 1], refs[n + 2], refs[n + 3:2 * n + 3], refs[2 * n + 3]
        x, y, c = _place()
        for b in range(n):
            for j, chip in enumerate(_other_chips(x, y)):
                _forward_copy(outs[b], send_sems, recv_sems, 3 * b + j, chip, c, (x, y, 1 - c)).start()
        token[...] = jnp.zeros_like(token)

    res = pl.pallas_call(
        body, name=name,
        out_shape=(pltpu.SemaphoreType.DMA((3 * n,)), pltpu.SemaphoreType.DMA((3 * n,)),
                   *[pltpu.HBM(b.shape, b.dtype) for b in bufs], jax.ShapeDtypeStruct((8, 128), F32)),
        in_specs=[HBM] * n + [ANY], out_specs=(SEM, SEM, *[HBM] * n, pl.BlockSpec(memory_space=pltpu.VMEM)),
        input_output_aliases={k: k + 2 for k in range(n)}, compiler_params=_SPLIT,
    )(*_in_hbm(bufs), after)
    return res[0], res[1], list(res[2:2 + n]), res[2 + n]


def _gather_forward_wait(send_sems, recv_sems, bufs, after, *, name):
    n = len(bufs)

    def body(*refs):
        ins, ss, rs = refs[:n], refs[n], refs[n + 1]
        x, y, c = _place()
        for b in range(n):
            for j, chip in enumerate(_other_chips(x, y)):
                _forward_copy(ins[b], ss, rs, 3 * b + j, chip, c, (x, y, 1 - c)).wait_send()
                _forward_copy(ins[b], ss, rs, 3 * b + j, chip, 1 - c, (x, y, 1 - c)).wait_recv()

    return list(pl.pallas_call(
        body, name=name, out_shape=[pltpu.HBM(b.shape, b.dtype) for b in bufs],
        in_specs=[HBM] * n + [SEM, SEM, ANY], out_specs=[HBM] * n,
        input_output_aliases={k: k for k in range(n)}, compiler_params=_SPLIT,
    )(*bufs, send_sems, recv_sems, after))


def _chip_exchange_start(hs, *, name):
    n = len(hs)
    lands = [lax.empty((3,) + h.shape[1:], h.dtype) for h in hs]

    def body(*refs):
        send_sems, recv_sems = refs[2 * n], refs[2 * n + 1]
        h_out, l_out, token = refs[2 * n + 2:3 * n + 2], refs[3 * n + 2:4 * n + 2], refs[4 * n + 2]
        x, y, c = _place()
        for b in range(n):
            for j, (cx, cy) in enumerate(_other_chips(x, y)):
                pltpu.make_async_remote_copy(src_ref=h_out[b].at[2 * cx + cy], dst_ref=l_out[b].at[j],
                                             send_sem=send_sems.at[3 * b + j], recv_sem=recv_sems.at[3 * b + j],
                                             device_id=(cx, cy, c), device_id_type=MESH).start()
        token[...] = jnp.zeros_like(token)

    res = pl.pallas_call(
        body, name=name,
        out_shape=(pltpu.SemaphoreType.DMA((3 * n,)), pltpu.SemaphoreType.DMA((3 * n,)),
                   *[pltpu.HBM(a.shape, a.dtype) for a in hs + lands], jax.ShapeDtypeStruct((8, 128), F32)),
        in_specs=[HBM] * (2 * n), out_specs=(SEM, SEM, *[HBM] * (2 * n), pl.BlockSpec(memory_space=pltpu.VMEM)),
        input_output_aliases={k: k + 2 for k in range(2 * n)}, compiler_params=_SPLIT,
    )(*_in_hbm(hs + lands))
    return res[0], res[1], list(res[2:2 + n]), list(res[2 + n:2 + 2 * n]), res[2 + 2 * n]


def _chip_exchange_wait(send_sems, recv_sems, hs, lands, after, *, name):
    n = len(hs)

    def body(*refs):
        h_in, l_in, ss, rs = refs[:n], refs[n:2 * n], refs[2 * n], refs[2 * n + 1]
        x, y, c = _place()
        for b in range(n):
            for j, (cx, cy) in enumerate(_other_chips(x, y)):
                cp = pltpu.make_async_remote_copy(src_ref=h_in[b].at[2 * cx + cy], dst_ref=l_in[b].at[j],
                                                  send_sem=ss.at[3 * b + j], recv_sem=rs.at[3 * b + j],
                                                  device_id=(cx, cy, c), device_id_type=MESH)
                cp.wait_send()
                cp.wait_recv()

    res = pl.pallas_call(
        body, name=name, out_shape=[pltpu.HBM(a.shape, a.dtype) for a in hs + lands],
        in_specs=[HBM] * (2 * n) + [SEM, SEM, ANY], out_specs=[HBM] * (2 * n),
        input_output_aliases={k: k for k in range(2 * n)}, compiler_params=_SPLIT,
    )(*hs, *lands, send_sems, recv_sems, after)
    return list(res[n:])


def _peers(x, y, c):
    return [((1 - x) if fx else x, (1 - y) if fy else y, (1 - c) if fc else c)
            for fx in (0, 1) for fy in (0, 1) for fc in (0, 1) if fx or fy or fc]


def _all_to_all_start(slab, after, *, name):
    land = lax.empty((N_DEV,) + slab.shape, slab.dtype)

    def body(slab_in, land_in, after_ref, send_sems, recv_sems, slab_out, land_out, token):
        x, y, c = _place()
        for k, peer in enumerate(_peers(x, y, c)):
            pltpu.make_async_remote_copy(src_ref=slab_out, dst_ref=land_out.at[4 * x + 2 * y + c],
                                         send_sem=send_sems.at[k], recv_sem=recv_sems.at[k], device_id=peer,
                                         device_id_type=MESH).start()
        token[...] = jnp.zeros_like(token)

    return pl.pallas_call(
        body, name=name,
        out_shape=(pltpu.SemaphoreType.DMA((N_DEV - 1,)), pltpu.SemaphoreType.DMA((N_DEV - 1,)),
                   pltpu.HBM(slab.shape, slab.dtype), pltpu.HBM(land.shape, land.dtype),
                   jax.ShapeDtypeStruct((8, 128), F32)),
        in_specs=[HBM, HBM, ANY], out_specs=(SEM, SEM, HBM, HBM, pl.BlockSpec(memory_space=pltpu.VMEM)),
        input_output_aliases={0: 2, 1: 3}, compiler_params=_SPLIT,
    )(*_in_hbm([slab, land]), after)


def _all_to_all_wait(send_sems, recv_sems, slab, land, afters, *, name):
    def body(slab_in, land_in, ss, rs, *_):
        x, y, c = _place()
        for k, (px, py, pc) in enumerate(_peers(x, y, c)):
            cp = pltpu.make_async_remote_copy(src_ref=slab_in, dst_ref=land_in.at[4 * px + 2 * py + pc],
                                              send_sem=ss.at[k], recv_sem=rs.at[k], device_id=(px, py, pc),
                                              device_id_type=MESH)
            cp.wait_send()
            cp.wait_recv()

    return pl.pallas_call(
        body, name=name, out_shape=[pltpu.HBM(slab.shape, slab.dtype), pltpu.HBM(land.shape, land.dtype)],
        in_specs=[HBM, HBM, SEM, SEM] + [ANY] * len(afters), out_specs=[HBM, HBM], input_output_aliases={0: 0, 1: 1},
        compiler_params=_SPLIT,
    )(slab, land, send_sems, recv_sems, *afters)


def _pair_exchange_start(gs, *, name):
    n = len(gs)
    lands = [lax.empty((g.shape[0], g.shape[1] // 2, g.shape[2]), g.dtype) for g in gs]

    def body(*refs):
        send_sems, recv_sems = refs[2 * n], refs[2 * n + 1]
        g_out, l_out, token = refs[2 * n + 2:3 * n + 2], refs[3 * n + 2:4 * n + 2], refs[4 * n + 2]
        x, y, c = _place()
        for b in range(n):
            rh = gs[b].shape[1] // 2
            pltpu.make_async_remote_copy(src_ref=g_out[b].at[:, pl.ds((1 - c) * rh, rh), :], dst_ref=l_out[b],
                                         send_sem=send_sems.at[b], recv_sem=recv_sems.at[b],
                                         device_id=(x, y, 1 - c), device_id_type=MESH).start()
        token[...] = jnp.zeros_like(token)

    res = pl.pallas_call(
        body, name=name,
        out_shape=(pltpu.SemaphoreType.DMA((n,)), pltpu.SemaphoreType.DMA((n,)),
                   *[pltpu.HBM(a.shape, a.dtype) for a in gs + lands], jax.ShapeDtypeStruct((8, 128), F32)),
        in_specs=[HBM] * (2 * n), out_specs=(SEM, SEM, *[HBM] * (2 * n), pl.BlockSpec(memory_space=pltpu.VMEM)),
        input_output_aliases={k: k + 2 for k in range(2 * n)}, compiler_params=_SPLIT,
    )(*_in_hbm(gs + lands))
    return res[0], res[1], list(res[2:2 + n]), list(res[2 + n:2 + 2 * n]), res[2 + 2 * n]


def _pair_exchange_wait(send_sems, recv_sems, gs, lands, after, *, name):
    n = len(gs)

    def body(*refs):
        g_in, l_in, ss, rs = refs[:n], refs[n:2 * n], refs[2 * n], refs[2 * n + 1]
        x, y, c = _place()
        for b in range(n):
            rh = gs[b].shape[1] // 2
            cp = pltpu.make_async_remote_copy(src_ref=g_in[b].at[:, pl.ds((1 - c) * rh, rh), :], dst_ref=l_in[b],
                                              send_sem=ss.at[b], recv_sem=rs.at[b], device_id=(x, y, 1 - c),
                                              device_id_type=MESH)
            cp.wait_send()
            cp.wait_recv()

    res = pl.pallas_call(
        body, name=name, out_shape=[pltpu.HBM(a.shape, a.dtype) for a in gs + lands],
        in_specs=[HBM] * (2 * n) + [SEM, SEM, ANY], out_specs=[HBM] * (2 * n),
        input_output_aliases={k: k for k in range(2 * n)}, compiler_params=_SPLIT,
    )(*gs, *lands, send_sems, recv_sems, after)
    return list(res[:n]), list(res[n:])


def _pair_share_start(ss, *, name):
    n = len(ss)

    def body(*refs):
        send_sems, recv_sems, outs, token = refs[n], refs[n + 1], refs[n + 2:2 * n + 2], refs[2 * n + 2]
        x, y, c = _place()
        for b in range(n):
            rh = ss[b].shape[0] // 2
            mine = outs[b].at[pl.ds(c * rh, rh), :]
            pltpu.make_async_remote_copy(src_ref=mine, dst_ref=mine, send_sem=send_sems.at[b],
                                         recv_sem=recv_sems.at[b], device_id=(x, y, 1 - c),
                                         device_id_type=MESH).start()
        token[...] = jnp.zeros_like(token)

    res = pl.pallas_call(
        body, name=name,
        out_shape=(pltpu.SemaphoreType.DMA((n,)), pltpu.SemaphoreType.DMA((n,)),
                   *[pltpu.HBM(a.shape, a.dtype) for a in ss], jax.ShapeDtypeStruct((8, 128), F32)),
        in_specs=[HBM] * n, out_specs=(SEM, SEM, *[HBM] * n, pl.BlockSpec(memory_space=pltpu.VMEM)),
        input_output_aliases={k: k + 2 for k in range(n)}, compiler_params=_SPLIT,
    )(*_in_hbm(ss))
    return res[0], res[1], list(res[2:2 + n]), res[2 + n]


def _pair_share_wait(send_sems, recv_sems, ss, after, *, name):
    n = len(ss)

    def body(*refs):
        ins, sems_s, sems_r = refs[:n], refs[n], refs[n + 1]
        x, y, c = _place()
        for b in range(n):
            rh = ss[b].shape[0] // 2
            mine = ins[b].at[pl.ds(c * rh, rh), :]
            theirs = ins[b].at[pl.ds((1 - c) * rh, rh), :]
            cp = pltpu.make_async_remote_copy(src_ref=mine, dst_ref=theirs, send_sem=sems_s.at[b],
                                              recv_sem=sems_r.at[b], device_id=(x, y, 1 - c),
                                              device_id_type=MESH)
            cp.wait_send()
            cp.wait_recv()

    return list(pl.pallas_call(
        body, name=name, out_shape=[pltpu.HBM(a.shape, a.dtype) for a in ss],
        in_specs=[HBM] * n + [SEM, SEM, ANY], out_specs=[HBM] * n,
        input_output_aliases={k: k for k in range(n)}, compiler_params=_SPLIT,
    )(*ss, send_sems, recv_sems, after))


_SMALL_SHARDED = (("e_conv_w", 2), ("o_norm", 1), ("o_d", 1))
_REPLICATED = ("e_norm", "e_gmlp_w", "e_gmlp_b", "e_conv_b", "e_conv_ln_g", "e_conv_ln_b", "o_lam_re", "o_lam_im",
               "o_log_dt", "o_b_re", "o_b_im", "o_c_re", "o_c_im", "ca_norm", "ca_mem_norm", "ffn_norm", "final_norm")
_SMALL = tuple(n for n, _ in _SMALL_SHARDED) + _REPLICATED
_WEIGHTS = ("e_norm", "e_w_in", "e_gmlp_w", "e_gmlp_b", "e_conv_w", "e_conv_b", "e_conv_ln_g", "e_conv_ln_b",
            "e_w_out", "o_norm", "o_w_in", "o_lam_re", "o_lam_im", "o_log_dt", "o_b_re", "o_b_im", "o_c_re", "o_c_im",
            "o_d", "o_w_out", "ca_norm", "ca_mem_norm", "ca_wq", "ca_wk", "ca_wv", "ca_wo", "ffn_norm", "ffn_w_gate",
            "ffn_w_up", "ffn_w_down", "final_norm")


def _pack_rows(arrs, width, dtype, row_mult=8):
    parts, spans, r0 = [], [], 0
    for a in arrs:
        flat = a.reshape(-1).astype(dtype)
        rows = -(-flat.shape[0] // (width * row_mult)) * row_mult
        if rows * width != flat.shape[0]:
            flat = jnp.pad(flat, (0, rows * width - flat.shape[0]))
        parts.append(flat.reshape(rows, width))
        spans.append((r0, rows))
        r0 += rows
    return jnp.concatenate(parts, axis=0), spans


def _unpack_rows(slab, spans, shapes):
    out = []
    for (r0, rows), shp in zip(spans, shapes):
        n = math.prod(shp)
        out.append(slab[r0:r0 + rows].reshape(-1)[:n].reshape(shp))
    return out


def _two_d(a):
    return a.reshape(-1, a.shape[-1])


def _shard_rows(n, a):
    return _two_d(jnp.swapaxes(a, -1, -2) if n in _TRANSPOSED else a)


def _from_shard_rows(n, rows, shape):
    if n in _TRANSPOSED:
        return jnp.swapaxes(rows.reshape(shape[:-2] + (shape[-1], shape[-2])), -1, -2)
    return rows.reshape(shape)


def _local_slab(local, slab, dtype):
    parts = sorted((r0, n, l) for n, (_, where) in _PLACE.items() for l, (s, r0) in enumerate(where) if s == slab)
    shards = [_shard_rows(n, local[n] if len(_PLACE[n][1]) == 1 else local[n][l]) for _, n, l in parts]
    return jnp.concatenate([a.astype(dtype) for a in shards], axis=0)


def _set_diag(b, pattern):
    return jnp.einsum(pattern, b, jnp.eye(C_GROUPS // N_SETS, dtype=b.dtype))


def _s5_discretize(lam_re, lam_im, log_dt, b_re, b_im):
    dt = jnp.exp(log_dt)[:, None]
    mag = jnp.exp(lam_re * dt)
    ar = mag * jnp.cos(lam_im * dt)
    ai = mag * jnp.sin(lam_im * dt)
    den = lam_re * lam_re + lam_im * lam_im
    qr = ((ar - 1.0) * lam_re + ai * lam_im) / den
    qi = (ai * lam_re - (ar - 1.0) * lam_im) / den
    bbr = qr[..., None] * b_re - qi[..., None] * b_im
    bbi = qr[..., None] * b_im + qi[..., None] * b_re
    return ar, ai, bbr, bbi


def _attention_block(x, mem, W, w, i, tag):
    xn, q = _norm_mm(x, w["ca_norm"][i], _shards(W, "ca_wq", i), split="k", out_dtype=BF16, name=f"{tag}_q")
    memn = _rms_fwd(mem, w["ca_mem_norm"][i], name=f"{tag}_ca_memnorm")
    k = _mm_k(memn, _shards(W, "ca_wk", i), out_dtype=BF16, name=f"{tag}_k")
    v = _mm_k(memn, _shards(W, "ca_wv", i), out_dtype=BF16, name=f"{tag}_v")
    o = _attn_fwd(q, k, v, name=f"{tag}_attn")
    y = _mm_k(o, _shards(W, "ca_wo", i), add=x, name=f"{tag}_wo")
    return y, (x, xn, memn, q, k, v, o)


def _attention_block_bwd(dy, saved, mem, W, w, i, tag, G, grads, token=None, mid=None):
    x, xn, memn, q, k, v, o = saved
    gain = w["ca_norm"][i]
    if token is not None:
        k = _behind(k, token)
    G = _grad_to_slab(G, "ca_wo", i, o, dy, a_cols=256, name=f"{tag}_dwo")
    dq, dk, dv = _attn_bwd(dy, _shards(W, "ca_wo", i), q, k, v, name=f"{tag}_attn_bwd")
    token = mid(dq) if mid is not None else None
    if token is not None:
        gain = _behind(gain, token)
    G = _grad_to_slab(G, "ca_wq", i, xn, dq, a_cols=256, name=f"{tag}_dwq")
    G = _grad_to_slab(G, "ca_wk", i, memn, dk, a_cols=256, name=f"{tag}_dwk")
    G = _grad_to_slab(G, "ca_wv", i, memn, dv, a_cols=256, name=f"{tag}_dwv")
    dmemn = _mm_k_t([(dk, _shards(W, "ca_wk", i)), (dv, _shards(W, "ca_wv", i))], name=f"{tag}_dmemn")
    dx, dg = _norm_bwd_k(dq, _shards(W, "ca_wq", i), x, gain, dy, name=f"{tag}_dq_norm_bwd")
    grads["ca_norm"][i] = dg[0]
    grads["ca_mem_norm"][i] = _rms_dg(mem, w["ca_mem_norm"][i], dmemn, name=f"{tag}_ca_memnorm_bwd")[0]
    return dx, G


def _ffn_block(x, W, w, i, tag, head=None):
    fn, gate, up, h = _ffn_up(x, w["ffn_norm"][i], _shards(W, "ffn_w_gate", i), _shards(W, "ffn_w_up", i),
                              name=f"{tag}_ffn_up")
    if head is None:
        y = _mm_k(h, _shards(W, "ffn_w_down", i), add=x, name=f"{tag}_down")
    else:
        y = _down_loss_head(h, _shards(W, "ffn_w_down", i), x, *head, name=f"{tag}_down_loss_head")
    return y, (x, fn, gate, up, h)


def _ffn_block_bwd(dy, saved, W, w, i, tag, G, grads, token=None, mid=None):
    x, fn, gate, up, h = saved
    gain = w["ffn_norm"][i]
    G = _grad_to_slab(G, "ffn_w_down", i, h, dy, name=f"{tag}_dwd")
    dg, du = _ffn_bwd_hidden(dy, _shards(W, "ffn_w_down", i), gate, up, token, name=f"{tag}_ffn_bwd_hidden")
    token = mid(dg) if mid is not None else None
    if token is not None:
        gain = _behind(gain, token)
    G = _grad_to_slab(G, "ffn_w_gate", i, dg, fn, name=f"{tag}_dwg")
    G = _grad_to_slab(G, "ffn_w_up", i, du, fn, name=f"{tag}_dwu")
    dx, dgn = _ffn_in_bwd(dg, du, _shards(W, "ffn_w_gate", i), _shards(W, "ffn_w_up", i), x, gain, dy,
                          name=f"{tag}_ffn_in_bwd")
    grads["ffn_norm"][i] = dgn[0]
    return dx, G


def _gmlp_mask():
    chunk = jnp.arange(GMLP_BLOCK) // CHUNK
    return chunk[None, :] <= chunk[:, None]


def _even_block(x, W, w, tag):
    hn, proj = _norm_mm(x, w["e_norm"][0], _shards(W, "e_w_in"), split="n", out_dtype=F32, name=f"{tag}_w_in")
    wm = jnp.where(_gmlp_mask()[None], w["e_gmlp_w"][0], 0.0).astype(BF16)
    bcol = w["e_gmlp_b"][0][:, :, None]
    cw = jnp.pad(w["e_conv_w"][0], ((0, CONV_HALO - CONV_WIDTH), (0, 0)))
    cb, lg, lb = w["e_conv_b"], w["e_conv_ln_g"], w["e_conv_ln_b"]
    mix, hc = _even_fwd(proj, wm, bcol, cw, cb, lg, lb, name=f"{tag}_mixers")
    y = _mm_k(mix, _shards(W, "e_w_out"), add=x, name=f"{tag}_w_out")
    return y, (x, hn, proj, mix, hc, wm, bcol, cw)


def _even_block_bwd(dy, saved, W, w, tag, G, grads):
    x, hn, proj, mix, hc, wm, bcol, cw = saved
    dmix = _mm_k_t([(dy, _shards(W, "e_w_out"))], name=f"{tag}_dmix")
    G = _grad_to_slab(G, "e_w_out", 0, mix, dy, a_cols=256, name=f"{tag}_dw_out")
    wmt = jnp.swapaxes(wm, 1, 2)
    dpa, dhc, dwm, db, dlg, dlb, dcb = _even_bwd1(proj, dmix, hc, wm, wmt, bcol, w["e_conv_ln_g"], w["e_conv_ln_b"],
                                                  name=f"{tag}_mixers_bwd1")
    dpb, dcw = _even_bwd2(proj, dhc, cw, name=f"{tag}_mixers_bwd2")
    grads["e_gmlp_w"] = jnp.where(_gmlp_mask()[None], dwm, 0.0)[None]
    grads["e_gmlp_b"] = db[:, :, 0][None]
    grads["e_conv_ln_g"], grads["e_conv_ln_b"], grads["e_conv_b"] = dlg, dlb, dcb
    grads["e_conv_w"] = dcw[:CONV_WIDTH][None]
    G = _grad_to_slab(G, "e_w_in", 0, hn, dpa, b_cols=512, chips=(0, 2), name=f"{tag}_dw_in_a")
    G = _grad_to_slab(G, "e_w_in", 0, hn, dpb, b_cols=512, chips=(2, 2), name=f"{tag}_dw_in_b")
    dx, dg = _norm_bwd_n((dpa, dpb), _shards(W, "e_w_in"), x, w["e_norm"][0], dy, name=f"{tag}_in_bwd")
    grads["e_norm"] = dg
    return dx, G


def _odd_block(x, W, w, tag):
    S = x.shape[0]
    hn, u = _norm_mm(x, w["o_norm"][0], _shards(W, "o_w_in"), split="k", out_dtype=F32, name=f"{tag}_w_in")
    disc_in = (w["o_lam_re"][0], w["o_lam_im"][0], w["o_log_dt"][0], w["o_b_re"][0], w["o_b_im"][0])
    (ar, ai, bbr, bbi), disc_vjp = jax.vjp(_s5_discretize, *disc_in)
    sets = (N_SETS, C_GROUPS // N_SETS)
    per_set = N_STATE // N_SETS
    bset = jnp.concatenate([_set_diag(b.reshape(sets + b.shape[1:]), "jgpc,gh->jgchp").reshape(N_SETS, SET_CH, per_set)
                            for b in (bbr, bbi)], axis=2).astype(BF16)
    cset = jnp.concatenate([_set_diag(c.reshape(sets + c.shape[1:]), "jgcp,gh->jgphc").reshape(N_SETS, per_set, SET_CH)
                            for c in (w["o_c_re"][0], -w["o_c_im"][0])], axis=1).astype(BF16)
    powers, pr, pi = [], ar, ai
    for _ in range(SCAN_BLOCK):
        powers.append(jnp.concatenate([pr.reshape(STATE_ROWS, STATE_LANES), pi.reshape(STATE_ROWS, STATE_LANES)], 0))
        pr, pi = pr * ar - pi * ai, pr * ai + pi * ar
    pw = jnp.stack(powers, axis=0)
    xs = _scan_fwd(u, bset, pw, name=f"{tag}_scan").reshape(S // 8, STATE_ROWS, 8, STATE_LANES)
    yv, yg = _s5_readout(xs, cset, u, w["o_d"], name=f"{tag}_readout")
    o, y = _glu_out(yg, _shards(W, "o_w_out"), x, name=f"{tag}_glu_out")
    return y, (x, hn, u, bset, cset, pw, xs, yv, yg, o, disc_vjp)


def _odd_block_bwd(dy, saved, W, w, tag, G, grads):
    x, hn, u, bset, cset, pw, xs, yv, yg, o, disc_vjp = saved
    S = x.shape[0]
    do, dys, dus, dd = _glu_out_bwd(o, dy, _shards(W, "o_w_out"), yv, u, w["o_d"], name=f"{tag}_glu_out_bwd")
    G = _grad_to_slab(G, "o_w_out", 0, yg, do, b_cols=512, name=f"{tag}_dw_out")
    grads["o_d"] = dd
    dcset_t = _state_grad_sets(dys, xs, name=f"{tag}_dcd")
    gs, da = _scan_bwd(dys, cset, xs.reshape(S * STATE_ROWS, STATE_LANES), pw, name=f"{tag}_scan_bwd")
    gs = gs.reshape(xs.shape)
    dbset = _state_grad_sets(u, gs, name=f"{tag}_dbd")
    du, dx, dg = _s5_in_bwd(gs, bset, dus, _shards(W, "o_w_in"), x, w["o_norm"][0], dy, name=f"{tag}_in_bwd")
    G = _grad_to_slab(G, "o_w_in", 0, hn, du, a_cols=256, name=f"{tag}_dw_in")
    grads["o_norm"] = dg
    per = C_GROUPS // N_SETS
    blocks = (N_SETS, per, C_GROUP_CH, 2, per, C_STATE)
    dc = _set_diag(dcset_t.reshape(blocks), "jhcrgp,gh->rjgcp").reshape(2, C_GROUPS, C_GROUP_CH, C_STATE)
    db = _set_diag(dbset.reshape(blocks), "jgcrhp,gh->rjgpc").reshape(2, C_GROUPS, C_STATE, C_GROUP_CH)
    dcr, dci, dbbr, dbbi = dc[0], -dc[1], db[0], db[1]
    dar = da[:STATE_ROWS].reshape(C_GROUPS, C_STATE)
    dai = da[STATE_ROWS:].reshape(C_GROUPS, C_STATE)
    dlr, dli, dldt, dbr, dbi = disc_vjp((dar, dai, dbbr, dbbi))
    grads["o_lam_re"], grads["o_lam_im"], grads["o_log_dt"] = dlr[None], dli[None], dldt[None]
    grads["o_b_re"], grads["o_b_im"], grads["o_c_re"], grads["o_c_im"] = dbr[None], dbi[None], dcr[None], dci[None]
    return dx, G


def _behind(value, token):
    return value + token[0, 0].astype(value.dtype)


class _NoExchange:
    def __init__(self, W):
        self.W = W

    def first_weights(self, w):
        return self.W, w

    def weights(self, stage, after):
        return {}

    def behind_late_start(self, w):
        return w

    def late_weights(self, after):
        return {}

    def grads_ready(self, piece, G):
        return None

    def grads_crossed(self, piece, after):
        return None


def _forward_backward(xs_, mems_, tgt, w, G, exchange):
    W, w = exchange.first_weights(w)
    x1, s_mix0 = _even_block(xs_, W, w, "l0")
    W = {**W, **exchange.weights(1, x1)}
    x2, s_att0 = _attention_block(x1, mems_, W, w, 0, "l0")
    W = {**W, **exchange.weights(2, x2)}
    x3, s_ffn0 = _ffn_block(x2, W, w, 0, "l0")
    W = {**W, **exchange.weights(3, x3)}
    w = exchange.behind_late_start(w)
    x4, s_mix1 = _odd_block(x3, W, w, "l1")
    W = {**W, **exchange.late_weights(x4)}
    x5, s_att1 = _attention_block(x4, mems_, W, w, 1, "l1")
    (dx, dfinal, loss_lanes), s_ffn1 = _ffn_block(x5, W, w, 1, "l1", head=(w["final_norm"], tgt))

    grads = {n: [None, None] for n in ("ca_norm", "ca_mem_norm", "ffn_norm")}
    grads["final_norm"] = dfinal[0]
    dx, G = _ffn_block_bwd(dx, s_ffn1, W, w, 1, "l1", G, grads)
    dx, G = _attention_block_bwd(dx, s_att1, mems_, W, w, 1, "l1", G, grads)
    dx, G = _odd_block_bwd(dx, s_mix1, W, w, "l1", G, grads)
    token = exchange.grads_ready("l1", G)
    dx, G = _ffn_block_bwd(dx, s_ffn0, W, w, 0, "l0", G, grads, token,
                           lambda after: exchange.grads_crossed("l1", after))
    token = exchange.grads_ready("ffn0", G)
    dx, G = _attention_block_bwd(dx, s_att0, mems_, W, w, 0, "l0", G, grads, token,
                                 lambda after: exchange.grads_crossed("ffn0", after))
    dx, G = _even_block_bwd(dx, s_mix0, W, w, "l0", G, grads)
    for n in list(grads):
        if isinstance(grads[n], list):
            grads[n] = jnp.stack(grads[n], axis=0)
        grads[n] = grads[n].reshape(w[n].shape)
    return loss_lanes, dx, G, grads


class _Exchange:
    def __init__(self, local, chip, core):
        self.bufs = {s: lax.dynamic_update_slice(lax.empty((N_CHIPS, rows, width), BF16),
                                                 _local_slab(local, s, BF16)[None], (chip, 0, 0))
                     for s, (width, rows) in _SLABS.items()}
        small = jnp.zeros((_SMALL_SLAB_ROWS, SMALL_W), F32)
        for n, (r0, rows) in _SMALL_PLACE.items():
            small = small.at[r0:r0 + rows].set(local[n].reshape(rows, SMALL_W))
        self.bufs[_SMALL_SLAB] = lax.dynamic_update_slice(lax.empty((N_CHIPS, _SMALL_SLAB_ROWS, SMALL_W), F32),
                                                          small[None], (chip, 0, 0))
        self.shard_shapes = {n: local[n].shape for n in _SMALL_PLACE}
        self.where = jnp.stack([chip, core]).astype(jnp.int32)
        self.reduces = {}

    def weights(self, stage, after):
        send_sems, recv_sems, flying = self.flight
        slabs = self.stage_slabs(stage)
        first = list(flying).index(slabs[0])
        bufs = _gather_ici_wait(send_sems, recv_sems, [flying[s] for s in slabs], first, after,
                                name=f"gather_stage{stage}_wait")
        now = [k for k, s in enumerate(slabs) if s not in _LATE_SLABS]
        late = [k for k, s in enumerate(slabs) if s in _LATE_SLABS]
        whole = _gather_forward([bufs[k] for k in now], name=f"gather_stage{stage}_forward")
        if late:
            *state, self.late_token = _gather_forward_start([bufs[k] for k in late], whole[0], name="gather_late_start")
            self.late = ([slabs[k] for k in late], *state)
        return dict(zip([slabs[k] for k in now], whole))

    def behind_late_start(self, w):
        return {**w, "o_norm": _behind(w["o_norm"], self.late_token)}

    def late_weights(self, after):
        slabs, send_sems, recv_sems, bufs = self.late
        return dict(zip(slabs, _gather_forward_wait(send_sems, recv_sems, bufs, after, name="gather_late_wait")))

    @staticmethod
    def stage_slabs(stage):
        return _STAGES[stage] + ((_SMALL_SLAB,) if stage == 0 else ())

    def first_weights(self, w):
        order = [s for k in range(len(_STAGES)) for s in self.stage_slabs(k)]
        send_sems, recv_sems, bufs, after = _gather_ici_start([self.bufs[s] for s in order], w["e_norm"],
                                                              name="gather_start")
        self.flight = (send_sems, recv_sems, dict(zip(order, bufs)))
        W = self.weights(0, after)
        w = {**w, "e_norm": _behind(w["e_norm"], after)}
        for (n, ax), (r0, rows) in zip(_SMALL_SHARDED, _SMALL_PLACE.values()):
            shards = [W[_SMALL_SLAB][p, r0:r0 + rows].reshape(self.shard_shapes[n]) for p in range(N_CHIPS)]
            w[n] = jnp.concatenate(shards, axis=ax)
        return W, w

    def pair_start(self, G, slabs, tag):
        send_sems, recv_sems, gl, lands, token = _pair_exchange_start([G[s] for s in slabs],
                                                                      name=f"grad_{tag}_pair_start")
        return (slabs, send_sems, recv_sems, gl, lands), token

    def pair_land(self, state, after, tag):
        slabs, send_sems, recv_sems, gl, lands = state
        gl, other = _pair_exchange_wait(send_sems, recv_sems, gl, lands, after, name=f"grad_{tag}_pair_wait")
        pairs = [_pair_sum(g, r, self.where, name=f"grad_pair_sum_{s}") for s, g, r in zip(slabs, gl, other)]
        send_sems, recv_sems, pairs, lands, token = _chip_exchange_start(pairs, name=f"grad_{tag}_chip_start")
        return (slabs, gl, other, send_sems, recv_sems, pairs, lands), token

    def reduce_sum(self, state, after, tag):
        slabs, gl, other, send_sems, recv_sems, pairs, lands = state
        slots = _chip_exchange_wait(send_sems, recv_sems, pairs, lands, after, name=f"grad_{tag}_chip_wait")
        return slabs, [_chip_sum(g, r, sl, self.where, name=f"grad_chip_sum_{s}")
                       for s, g, r, sl in zip(slabs, gl, other, slots)]

    @staticmethod
    def share_start(slabs, halves, tag):
        send_sems, recv_sems, halves, token = _pair_share_start(halves, name=f"grad_{tag}_share_start")
        return (slabs, send_sems, recv_sems, halves), token

    @staticmethod
    def share_finish(state, after, tag):
        slabs, send_sems, recv_sems, halves = state
        return dict(zip(slabs, _pair_share_wait(send_sems, recv_sems, halves, after, name=f"grad_{tag}_share_wait")))

    def grads_ready(self, piece, G):
        self.reduces[piece], token = self.pair_start(G, _GRAD_PIECES[piece], piece)
        return token

    def grads_crossed(self, piece, after):
        self.reduces[piece], token = self.pair_land(self.reduces[piece], after, piece)
        return token


def kernel(x, mem, e_norm, e_w_in, e_gmlp_w, e_gmlp_b, e_conv_w, e_conv_b, e_conv_ln_g, e_conv_ln_b, e_w_out, o_norm, o_w_in, o_lam_re, o_lam_im, o_log_dt, o_b_re, o_b_im, o_c_re, o_c_im, o_d, o_w_out, ca_norm, ca_mem_norm, ca_wq, ca_wk, ca_wv, ca_wo, ffn_norm, ffn_w_gate, ffn_w_up, ffn_w_down, final_norm, loss_target, m_e_norm, m_e_w_in, m_e_gmlp_w, m_e_gmlp_b, m_e_conv_w, m_e_conv_b, m_e_conv_ln_g, m_e_conv_ln_b, m_e_w_out, m_o_norm, m_o_w_in, m_o_lam_re, m_o_lam_im, m_o_log_dt, m_o_b_re, m_o_b_im, m_o_c_re, m_o_c_im, m_o_d, m_o_w_out, m_ca_norm, m_ca_mem_norm, m_ca_wq, m_ca_wk, m_ca_wv, m_ca_wo, m_ffn_norm, m_ffn_w_gate, m_ffn_w_up, m_ffn_w_down, m_final_norm, v_e_norm, v_e_w_in, v_e_gmlp_w, v_e_gmlp_b, v_e_conv_w, v_e_conv_b, v_e_conv_ln_g, v_e_conv_ln_b, v_e_w_out, v_o_norm, v_o_w_in, v_o_lam_re, v_o_lam_im, v_o_log_dt, v_o_b_re, v_o_b_im, v_o_c_re, v_o_c_im, v_o_d, v_o_w_out, v_ca_norm, v_ca_mem_norm, v_ca_wq, v_ca_wk, v_ca_wv, v_ca_wo, v_ffn_norm, v_ffn_w_gate, v_ffn_w_up, v_ffn_w_down, v_final_norm):
    args = dict(locals())
    local = {n: args[n] for n in _WEIGHTS}
    mom = {n: args["m_" + n] for n in _WEIGHTS}
    vel = {n: args["v_" + n] for n in _WEIGHTS}
    chip = 2 * lax.axis_index("x") + lax.axis_index("y")
    core = lax.axis_index("c")
    xs_, mems_, tgt = x[0], mem[0], loss_target[0]

    w = {n: local[n] for n in _REPLICATED}
    exchange = _Exchange(local, chip, core)
    G = {s: lax.empty((N_CHIPS, rows, width), F32) for s, (width, rows) in _SLABS.items()}
    loss_lanes, dx, G, grads = _forward_backward(xs_, mems_, tgt, w, G, exchange)

    gs_slab, gs_spans = _pack_rows([grads[n] for n in _SMALL] + [loss_lanes], SMALL_W, F32)
    rest0_token = exchange.grads_ready("rest0", G)
    small_flight = _all_to_all_start(gs_slab, rest0_token, name="small_grads_start")
    slabs_l1, halves_l1 = exchange.reduce_sum(exchange.reduces["l1"], small_flight[4], "l1")
    slabs_f0, halves_f0 = exchange.reduce_sum(exchange.reduces["ffn0"], small_flight[4], "ffn0")
    share, share_token = exchange.share_start(slabs_l1 + slabs_f0, halves_l1 + halves_f0, "l1_ffn0")
    token = exchange.grads_crossed("rest0", share_token)

    out_grads, delta, new_m, new_v = {}, {}, {}, {}

    def adamw_large(names):
        raw = []
        for n in names:
            shp = local[n].shape
            g_, d_, m_, v_ = _adamw_shard(_shard_rows(n, local[n]), [(gsum[s], r0) for s, r0 in _PLACE[n][1]],
                                          _shard_rows(n, mom[n]), _shard_rows(n, vel[n]), name=f"adamw_{n}")
            out_grads[n], delta[n], new_m[n], new_v[n] = (_from_shard_rows(n, t, shp) for t in (g_, d_, m_, v_))
            raw.append(d_)
        return raw

    gsum = exchange.share_finish(share, token, "l1_ffn0")
    ready = [n for n, (_, where) in _PLACE.items() if all(s in gsum for s, _ in where)]
    done = adamw_large(ready)

    gs_slab, gs_all = _all_to_all_wait(*small_flight[:4], done, name="small_grads_wait")
    gs_all = lax.dynamic_update_slice(gs_all, gs_slab[None], (2 * chip + core, 0, 0))
    gs_sum = _sum_slots(gs_all, name="small_grad_sum")
    *small_sums, loss_sum = _unpack_rows(gs_sum, gs_spans, [grads[n].shape for n in _SMALL] + [loss_lanes.shape])
    out_grads.update(zip(_SMALL, small_sums))
    for n, ax in _SMALL_SHARDED:
        width = local[n].shape[ax]
        out_grads[n] = lax.dynamic_slice_in_dim(out_grads[n], chip * width, width, axis=ax)
    d_, m_, v_ = _adamw_small([_two_d(local[n]) for n in _SMALL], [_two_d(out_grads[n]) for n in _SMALL],
                              [_two_d(mom[n]) for n in _SMALL], [_two_d(vel[n]) for n in _SMALL], name="adamw_small")
    for n, dd, mm_, vv in zip(_SMALL, d_, m_, v_):
        shp = local[n].shape
        delta[n], new_m[n], new_v[n] = dd.reshape(shp), mm_.reshape(shp), vv.reshape(shp)

    slabs_r0, halves_r0 = exchange.reduce_sum(exchange.reduces["rest0"], d_[0], "rest0")
    share, share_token = exchange.share_start(slabs_r0, halves_r0, "rest0")
    gsum = {**gsum, **exchange.share_finish(share, share_token, "rest0")}
    adamw_large([n for n in _PLACE if n not in ready])

    return (loss_sum[0, 0], dx[None], *[out_grads[n] for n in _WEIGHTS], *[delta[n] for n in _WEIGHTS],
            *[new_m[n] for n in _WEIGHTS], *[new_v[n] for n in _WEIGHTS])
```

```python
import functools
import math

import jax
import jax.numpy as jnp
from jax import lax
from jax.experimental import pallas as pl
from jax.experimental.pallas import tpu as pltpu

F32 = jnp.float32
BF16 = jnp.bfloat16
MESH = pl.DeviceIdType.MESH

EPS = 1e-6
D_MODEL = 1024
A_WIDTH = 512
A_GROUPS = 4
GMLP_BLOCK = 128
CHUNK = 64
B_WIDTH = 512
CONV_WIDTH = 31
CONV_HALO = 32
C_WIDTH = 512
C_GROUP_CH = 16
C_GROUPS = 32
C_STATE = 64
N_STATE = C_GROUPS * C_STATE
STATE_LANES = 128
STATE_ROWS = N_STATE // STATE_LANES
SCAN_BLOCK = 8
CA_HEADS = 4
CA_HEAD_DIM = 256
FFN_HIDDEN = 2816

ADAM_LR = 0.001
ADAM_B1 = 0.9
ADAM_B2 = 0.999
ADAM_EPS = 1e-08
ADAM_WD = 0.01
ADAM_STEP = 10

VMEM_LIMIT = 56 * 1024 * 1024
ACC_BYTES = 6 * 1024 * 1024
TN_VMEM_BYTES = 44 * 1024 * 1024
SMALL_W = 128
N_CHIPS = 4
N_DEV = 8

_SLABS = {"D0": (512, 1024), "E0": (1024, 256), "A0": (1024, 1024), "B0": (1024, 704), "C0": (1024, 1408),
          "D1": (512, 768), "A1": (1024, 1024), "B1": (1024, 704), "C1": (1024, 1408)}
_STAGES = (("D0", "E0"), ("A0",), ("B0", "C0"), ("D1", "A1", "B1", "C1"))
_LATE_SLABS = ("A1", "B1", "C1")
_GRAD_PIECES = {"l1": _STAGES[3], "ffn0": _STAGES[2], "rest0": _STAGES[0] + _STAGES[1]}
_PLACE = {
    "e_w_in": (1024, (("D0", 0),)), "e_w_out": (256, (("E0", 0),)),
    "o_w_out": (512, (("D1", 0),)), "o_w_in": (256, (("D1", 512),)),
    "ca_wq": (256, (("A0", 0), ("A1", 0))), "ca_wk": (256, (("A0", 256), ("A1", 256))),
    "ca_wv": (256, (("A0", 512), ("A1", 512))), "ca_wo": (256, (("A0", 768), ("A1", 768))),
    "ffn_w_down": (704, (("B0", 0), ("B1", 0))),
    "ffn_w_gate": (704, (("C0", 0), ("C1", 0))), "ffn_w_up": (704, (("C0", 704), ("C1", 704))),
}
_SMALL_SLAB = "F0"
_SMALL_SLAB_ROWS = 48
_SMALL_PLACE = {"e_conv_w": (0, 31), "o_norm": (32, 2), "o_d": (34, 1)}
_TRANSPOSED = ("ffn_w_gate", "ffn_w_up")


def _params(sem=None):
    return pltpu.CompilerParams(dimension_semantics=sem, vmem_limit_bytes=VMEM_LIMIT)


def _tile(n, pref, mult=128):
    if n <= pref:
        return n
    t = (pref // mult) * mult
    while t >= mult:
        if n % t == 0:
            return t
        t -= mult
    return n


def _blk(name, layer=0):
    rows, where = _PLACE[name]
    slab, r0 = where[layer]
    assert r0 % rows == 0
    return slab, rows, r0 // rows


def _shards(slabs, name, layer=0):
    slab, rows, b = _blk(name, layer)
    return [(slabs[slab], (None, rows, _SLABS[slab][0]), (p, b, 0)) for p in range(N_CHIPS)]


_GELU_C = 0.7978845608028654
_GELU_A = 0.044715


def _gelu(x):
    t = jnp.tanh(_GELU_C * (x + _GELU_A * (x * x * x)))
    return 0.5 * x * (1.0 + t), t


def _gelu_grad(x, t):
    return 0.5 * (1.0 + t) + 0.5 * x * (1.0 - t * t) * (_GELU_C * (1.0 + 3.0 * _GELU_A * x * x))


def _sigmoid(x):
    return 1.0 / (1.0 + jnp.exp(-x))


def _mean(x):
    return jnp.mean(x, axis=-1, keepdims=True)


def _dot(a, b):
    return jnp.dot(a, b, preferred_element_type=F32)


def _dot_nt(a, b):
    return lax.dot_general(a, b, (((1,), (1,)), ((), ())), preferred_element_type=F32)


def _dot_tn(a, b):
    return lax.dot_general(a, b, (((0,), (0,)), ((), ())), preferred_element_type=F32)


def _rms_tile(xv, gv):
    return (xv * lax.rsqrt(_mean(xv * xv) + EPS)) * gv


def _rms_bwd_tile(xv, gv, dyv):
    r = lax.rsqrt(_mean(xv * xv) + EPS)
    xh = xv * r
    dyg = dyv * gv
    return r * (dyg - xh * _mean(dyg * xh)), jnp.sum(dyv * xh, axis=0, keepdims=True)


def _cols(p, width):
    return slice(p * width, (p + 1) * width)


def _sum_k(a, ws, k):
    tot = None
    for p in range(N_CHIPS):
        y = _dot(a[:, _cols(p, k)], ws[p][...])
        tot = y if tot is None else tot + y
    return tot


def _cat_nt(a, ws):
    return jnp.concatenate([_dot_nt(a, ws[p][...]) for p in range(N_CHIPS)], axis=1)


def _rows_call(name, tm, rows, fulls, outs, accs, body, scratch=()):
    S = min(x.shape[-2] for x in rows if x.ndim != 4)
    nr, nf, no, na = len(rows), len(fulls), len(outs), len(accs)

    def kern(*refs):
        r, f = refs[:nr], refs[nr:nr + nf]
        o, a = refs[nr + nf:nr + nf + no], refs[nr + nf + no:nr + nf + no + na]
        if na:
            @pl.when(pl.program_id(0) == 0)
            def _():
                for ref in a:
                    ref[...] = jnp.zeros_like(ref)
        body(r, f, o, a, refs[nr + nf + no + na:])

    def whole(shape):
        nd = len(shape)
        return pl.BlockSpec(tuple(shape), lambda i: (0,) * nd)

    def row_spec(shape):
        if len(shape) == 4:
            return pl.BlockSpec((tm // 8,) + tuple(shape[1:]), lambda i: (i, 0, 0, 0))
        if len(shape) == 3:
            return pl.BlockSpec((shape[0], tm, shape[2]), lambda i: (0, i, 0))
        return pl.BlockSpec((tm, shape[1]), lambda i: (i, 0))

    def full_spec(x):
        if isinstance(x, tuple):
            _, bshape, bidx = x
            return pl.BlockSpec(bshape, lambda i: bidx, pipeline_mode=pl.Buffered(1))
        return whole(x.shape)

    out_shapes = [(S, o[0]) if len(o) == 2 else (o[0], S, o[1]) for o in outs]
    res = pl.pallas_call(
        kern, name=name, grid=(S // tm,),
        in_specs=[row_spec(x.shape) for x in rows] + [full_spec(x) for x in fulls],
        out_specs=[row_spec(s) for s in out_shapes] + [whole(shp) for shp, _ in accs],
        out_shape=[jax.ShapeDtypeStruct(s, o[-1]) for s, o in zip(out_shapes, outs)]
        + [jax.ShapeDtypeStruct(tuple(shp), dt) for shp, dt in accs],
        scratch_shapes=list(scratch),
        compiler_params=_params(("arbitrary",) if na else ("parallel",)),
    )(*rows, *[x[0] if isinstance(x, tuple) else x for x in fulls])
    return res[:no], res[no:]


def _grad_to_slab(gslabs, wname, layer, a, b, *, a_cols=None, b_cols=None, chips=(0, N_CHIPS), name):
    slab, rows, bidx = _blk(wname, layer)
    width = _SLABS[slab][0]
    p0, n_p = chips
    assert p0 % n_p == 0
    S = a.shape[-2]

    def tile_bytes(x, ts):
        return ts * x.dtype.itemsize * (x.shape[2] * n_p if x.ndim == 3 else x.shape[1])

    acc_bytes = n_p * rows * (-(-width // 128) * 128) * 4
    ts = next(t for t in (2048, 1024, 512, 256, S) if S % t == 0
              and 2 * (tile_bytes(a, t) + tile_bytes(b, t) + acc_bytes) <= TN_VMEM_BYTES or t == S)

    def operand(x):
        if x.ndim == 3:
            return pl.BlockSpec((n_p, ts, x.shape[2]), lambda s: (p0 // n_p, s, 0))
        return pl.BlockSpec((ts, x.shape[1]), lambda s: (s, 0))

    def part(ref, cols, p):
        if len(ref.shape) == 3:
            return ref[p]
        return ref[...] if cols is None else ref[:, _cols(p, cols)]

    def body(a_ref, b_ref, slab_ref, o_ref):
        @pl.when(pl.program_id(0) == 0)
        def _():
            o_ref[...] = jnp.zeros_like(o_ref)

        for p in range(n_p):
            o_ref[p] += _dot_tn(part(a_ref, a_cols, p).astype(BF16), part(b_ref, b_cols, p).astype(BF16))

    g = gslabs[slab]
    out = pl.pallas_call(
        body, name=name, grid=(S // ts,),
        in_specs=[operand(a), operand(b), pl.BlockSpec(memory_space=pl.ANY)],
        out_specs=pl.BlockSpec((n_p, rows, width), lambda s: (p0 // n_p, bidx, 0)),
        out_shape=jax.ShapeDtypeStruct(g.shape, F32), input_output_aliases={2: 0},
        compiler_params=_params(("arbitrary",)),
    )(a, b, g)
    return {**gslabs, slab: out}


def _vec(g):
    return g.reshape(1, -1)


def _norm_mm(x, g, ws, *, split, out_dtype, name, tm=512):
    S, D = x.shape
    k, n = ws[0][1][1], ws[0][1][2]
    N = n if split == "k" else N_CHIPS * n

    def body(r, f, o, acc, s):
        xn = _rms_tile(r[0][...], f[0][...]).astype(BF16)
        o[0][...] = xn
        if split == "k":
            o[1][...] = _sum_k(xn, f[1:], k).astype(out_dtype)
        else:
            for p in range(N_CHIPS):
                o[1][:, _cols(p, n)] = _dot(xn, f[1 + p][...]).astype(out_dtype)

    (xn, y), _ = _rows_call(name, _tile(S, tm), [x], [_vec(g)] + ws, [(D, BF16), (N, out_dtype)], [], body)
    return xn, y


def _mm_k(a, ws, *, add=None, out_dtype=F32, name, tm=512):
    S = a.shape[-2]
    k, n = ws[0][1][1], ws[0][1][2]
    has_add = add is not None

    def body(r, f, o, acc, s):
        if a.ndim == 3:
            y = None
            for p in range(N_CHIPS):
                t = _dot(r[0][p].astype(BF16), f[p][...])
                y = t if y is None else y + t
        else:
            y = _sum_k(r[0][...].astype(BF16), f, k)
        if has_add:
            y = y + r[1][...]
        o[0][...] = y.astype(out_dtype)

    (y,), _ = _rows_call(name, _tile(S, tm), [a] + ([add] if has_add else []), ws, [(n, out_dtype)], [], body)
    return y


def _mm_k_t(terms, *, out_dtype=F32, name, tm=512):
    S = terms[0][0].shape[0]
    k = terms[0][1][0][1][1]

    def body(r, f, o, acc, s):
        y = None
        for t in range(len(terms)):
            yt = _cat_nt(r[t][...].astype(BF16), f[N_CHIPS * t:N_CHIPS * (t + 1)])
            y = yt if y is None else y + yt
        o[0][...] = y.astype(out_dtype)

    (y,), _ = _rows_call(name, _tile(S, tm), [a for a, _ in terms], [w for _, ws in terms for w in ws],
                         [(N_CHIPS * k, out_dtype)], [], body)
    return y


def _rms_fwd(x, g, *, name):
    def body(r, f, o, acc, s):
        o[0][...] = _rms_tile(r[0][...], f[0][...]).astype(BF16)

    (y,), _ = _rows_call(name, _tile(x.shape[0], 256, 8), [x], [_vec(g)], [(x.shape[1], BF16)], [], body)
    return y


def _rms_dg(x, g, dy, *, name):
    def body(r, f, o, acc, s):
        acc[0][...] += _rms_bwd_tile(r[0][...], f[0][...], r[1][...])[1]

    _, (dg,) = _rows_call(name, _tile(x.shape[0], 256, 8), [x, dy], [_vec(g)], [], [((1, x.shape[1]), F32)], body)
    return dg


def _ffn_up(x, g, wg, wu, *, name, tm=512):
    S, D = x.shape
    h = wg[0][1][1]

    def body(r, f, o, acc, s):
        xn = _rms_tile(r[0][...], f[0][...]).astype(BF16)
        o[0][...] = xn
        for p in range(N_CHIPS):
            gate = _dot_nt(xn, f[1 + p][...])
            up = _dot_nt(xn, f[1 + N_CHIPS + p][...])
            o[1][p] = gate.astype(BF16)
            o[2][p] = up.astype(BF16)
            o[3][p] = (gate * _sigmoid(gate) * up).astype(BF16)

    (xn, gate, up, hid), _ = _rows_call(name, _tile(S, tm), [x], [_vec(g)] + wg + wu,
                                        [(D, BF16), (N_CHIPS, h, BF16), (N_CHIPS, h, BF16), (N_CHIPS, h, BF16)], [],
                                        body)
    return xn, gate, up, hid


def _ffn_bwd_hidden(dy, wd, gate, up, token=None, *, name, tm=512):
    S = dy.shape[0]
    h = wd[0][1][1]

    def body(r, f, o, acc, s):
        dyv = r[0][...]
        if token is not None:
            dyv = dyv + jnp.sum(f[N_CHIPS][...])
        dyb = dyv.astype(BF16)
        for p in range(N_CHIPS):
            dh = _dot_nt(dyb, f[p][...])
            gv = r[1][p].astype(F32)
            sg = _sigmoid(gv)
            o[0][p] = (dh * r[2][p].astype(F32) * (sg * (1.0 + gv * (1.0 - sg)))).astype(BF16)
            o[1][p] = (dh * gv * sg).astype(BF16)

    (dg, du), _ = _rows_call(name, _tile(S, tm), [dy, gate, up], wd + ([] if token is None else [token]),
                             [(N_CHIPS, h, BF16), (N_CHIPS, h, BF16)], [], body)
    return dg, du


def _ffn_in_bwd(dg, du, wg, wu, x, g, dres, *, name, tm=512):
    S, D = x.shape

    def body(r, f, o, acc, s):
        tot = None
        for p in range(N_CHIPS):
            y = _dot(r[0][p], f[1 + p][...]) + _dot(r[1][p], f[1 + N_CHIPS + p][...])
            tot = y if tot is None else tot + y
        dx, dgn = _rms_bwd_tile(r[2][...], f[0][...], tot)
        o[0][...] = dx + r[3][...]
        acc[0][...] += dgn

    (dx,), (dgn,) = _rows_call(name, _tile(S, tm), [dg, du, x, dres], [_vec(g)] + wg + wu, [(D, F32)],
                               [((1, D), F32)], body)
    return dx, dgn


def _norm_bwd_k(da, ws, x, g, dres, *, name, tm=512):
    S, D = x.shape

    def body(r, f, o, acc, s):
        dx, dg = _rms_bwd_tile(r[1][...], f[0][...], _cat_nt(r[0][...].astype(BF16), f[1:]))
        o[0][...] = dx + r[2][...]
        acc[0][...] += dg

    (dx,), (dg,) = _rows_call(name, _tile(S, tm), [da, x, dres], [_vec(g)] + ws, [(D, F32)], [((1, D), F32)], body)
    return dx, dg


def _norm_bwd_n(das, ws, x, g, dres, *, name, tm=512):
    S, D = x.shape
    n = ws[0][1][2]

    def body(r, f, o, acc, s):
        tot = None
        for p in range(N_CHIPS):
            y = _dot_nt(r[p // 2][:, _cols(p % 2, n)], f[1 + p][...])
            tot = y if tot is None else tot + y
        dx, dg = _rms_bwd_tile(r[2][...], f[0][...], tot)
        o[0][...] = dx + r[3][...]
        acc[0][...] += dg

    (dx,), (dg,) = _rows_call(name, _tile(S, tm), list(das) + [x, dres], [_vec(g)] + ws, [(D, F32)], [((1, D), F32)],
                              body)
    return dx, dg


def _ln_stats(v):
    mu = _mean(v)
    xc = v - mu
    rstd = lax.rsqrt(_mean(xc * xc) + EPS)
    return xc * rstd, rstd


_SHIFTS = 8
_CONV_ROWS = 64


def _fill_shifts(sh_ref, ext_ref, tm):
    sh_ref[0] = ext_ref[...]
    for s in range(1, _SHIFTS):
        sh_ref[s, 0:tm + CONV_HALO - _SHIFTS, :] = ext_ref[pl.ds(s, tm + CONV_HALO - _SHIFTS), :]


def _window(sh_ref, off, tm):
    return sh_ref[off % _SHIFTS, pl.ds(off - off % _SHIFTS, tm), :]


def _even_fwd(proj, wm, bcol, cw, cb, lg, lb, *, name):
    S = proj.shape[0]
    tm = _tile(S, 256)
    hb = tm // CONV_HALO
    nblk = tm // GMLP_BLOCK

    def body(p_ref, halo_ref, wm_ref, b_ref, cw_ref, cb_ref, lg_ref, lb_ref, mix_ref, hc_ref, hext_ref, hsh_ref):
        i = pl.program_id(0)
        gu, _ = _gelu(p_ref[:, 0:A_WIDTH])
        gv, _ = _gelu(p_ref[:, A_WIDTH:2 * A_WIDTH])
        vn, _ = _ln_stats(gv)
        vnb = vn.astype(BF16)
        for n in range(nblk):
            rows = slice(n * GMLP_BLOCK, (n + 1) * GMLP_BLOCK)
            for g in range(A_GROUPS):
                cols = slice(g * GMLP_BLOCK, (g + 1) * GMLP_BLOCK)
                sg = jnp.dot(wm_ref[g], vnb[rows, cols], preferred_element_type=F32) + b_ref[g]
                mix_ref[rows, cols] = (gu[rows, cols] * sg).astype(BF16)
        h = p_ref[:, 1024:1536] * _sigmoid(p_ref[:, 1536:2048])
        hh = halo_ref[:, 0:B_WIDTH] * _sigmoid(halo_ref[:, B_WIDTH:2 * B_WIDTH])
        hext_ref[0:CONV_HALO, :] = jnp.where(i > 0, hh, 0.0)
        hext_ref[CONV_HALO:CONV_HALO + tm, :] = h
        _fill_shifts(hsh_ref, hext_ref, tm)
        for r0 in range(0, tm, _CONV_ROWS):
            acc = jnp.zeros((_CONV_ROWS, B_WIDTH), F32)
            for k in range(CONV_WIDTH):
                acc = acc + cw_ref[k:k + 1, :] * _window(hsh_ref, r0 + k + CONV_HALO - CONV_WIDTH + 1, _CONV_ROWS)
            hc_ref[r0:r0 + _CONV_ROWS, :] = acc + cb_ref[...]
        hc = hc_ref[...]
        hhat, _ = _ln_stats(hc)
        hl = hhat * lg_ref[...] + lb_ref[...]
        mix_ref[:, A_WIDTH:A_WIDTH + B_WIDTH] = (hl * _sigmoid(hl)).astype(BF16)

    vec = pl.BlockSpec((1, B_WIDTH), lambda i: (0, 0))
    return pl.pallas_call(
        body, name=name, grid=(S // tm,),
        in_specs=[
            pl.BlockSpec((tm, 2048), lambda i: (i, 0)),
            pl.BlockSpec((CONV_HALO, 1024), lambda i: (jnp.maximum(i * hb - 1, 0), 1)),
            pl.BlockSpec((A_GROUPS, GMLP_BLOCK, GMLP_BLOCK), lambda i: (0, 0, 0)),
            pl.BlockSpec((A_GROUPS, GMLP_BLOCK, 1), lambda i: (0, 0, 0)),
            pl.BlockSpec((CONV_HALO, B_WIDTH), lambda i: (0, 0)),
            vec, vec, vec,
        ],
        out_specs=[pl.BlockSpec((tm, 1024), lambda i: (i, 0)), pl.BlockSpec((tm, B_WIDTH), lambda i: (i, 0))],
        out_shape=[jax.ShapeDtypeStruct((S, 1024), BF16), jax.ShapeDtypeStruct((S, B_WIDTH), F32)],
        scratch_shapes=[pltpu.VMEM((tm + CONV_HALO, B_WIDTH), F32),
                        pltpu.VMEM((_SHIFTS, tm + CONV_HALO, B_WIDTH), F32)],
        compiler_params=_params(("parallel",)),
    )(proj, proj, wm, bcol, cw, cb, lg, lb)


def _even_bwd1(proj, dmix, hc, wm, wmt, bcol, lg, lb, *, name):
    S = proj.shape[0]
    tm = _tile(S, 256)
    nblk = tm // GMLP_BLOCK

    def body(p_ref, dm_ref, hc_ref, wm_ref, wmt_ref, b_ref, lg_ref, lb_ref,
             dpa_ref, dhc_ref, dwm_ref, db_ref, dlg_ref, dlb_ref, dcb_ref, dgu_ref, dvn_ref):
        @pl.when(pl.program_id(0) == 0)
        def _():
            dwm_ref[...] = jnp.zeros_like(dwm_ref)
            db_ref[...] = jnp.zeros_like(db_ref)
            dlg_ref[...] = jnp.zeros_like(dlg_ref)
            dlb_ref[...] = jnp.zeros_like(dlb_ref)
            dcb_ref[...] = jnp.zeros_like(dcb_ref)

        au = p_ref[:, 0:A_WIDTH]
        av = p_ref[:, A_WIDTH:2 * A_WIDTH]
        gu, tu = _gelu(au)
        gv, tv = _gelu(av)
        vn, rstd = _ln_stats(gv)
        vnb = vn.astype(BF16)
        for n in range(nblk):
            rows = slice(n * GMLP_BLOCK, (n + 1) * GMLP_BLOCK)
            for g in range(A_GROUPS):
                cols = slice(g * GMLP_BLOCK, (g + 1) * GMLP_BLOCK)
                vb = vnb[rows, cols]
                sg = jnp.dot(wm_ref[g], vb, preferred_element_type=F32) + b_ref[g]
                da = dm_ref[rows, cols]
                dsg = da * gu[rows, cols]
                dgu_ref[rows, cols] = da * sg
                dsgb = dsg.astype(BF16)
                dwm_ref[g] += _dot_nt(dsgb, vb)
                db_ref[g] += jnp.sum(dsg, axis=1, keepdims=True)
                dvn_ref[rows, cols] = jnp.dot(wmt_ref[g], dsgb, preferred_element_type=F32)
        dvn = dvn_ref[...]
        dgv = rstd * (dvn - _mean(dvn) - vn * _mean(dvn * vn))
        dpa_ref[:, 0:A_WIDTH] = (dgu_ref[...] * _gelu_grad(au, tu)).astype(BF16)
        dpa_ref[:, A_WIDTH:2 * A_WIDTH] = (dgv * _gelu_grad(av, tv)).astype(BF16)
        hhat, rstd2 = _ln_stats(hc_ref[...])
        lgv = lg_ref[...]
        hl = hhat * lgv + lb_ref[...]
        s = _sigmoid(hl)
        dhl = dm_ref[:, A_WIDTH:A_WIDTH + B_WIDTH] * (s * (1.0 + hl * (1.0 - s)))
        dlg_ref[...] += jnp.sum(dhl * hhat, axis=0, keepdims=True)
        dlb_ref[...] += jnp.sum(dhl, axis=0, keepdims=True)
        dhh = dhl * lgv
        dhc = rstd2 * (dhh - _mean(dhh) - hhat * _mean(dhh * hhat))
        dcb_ref[...] += jnp.sum(dhc, axis=0, keepdims=True)
        dhc_ref[...] = dhc

    vec = pl.BlockSpec((1, B_WIDTH), lambda i: (0, 0))
    w3 = pl.BlockSpec((A_GROUPS, GMLP_BLOCK, GMLP_BLOCK), lambda i: (0, 0, 0))
    b3 = pl.BlockSpec((A_GROUPS, GMLP_BLOCK, 1), lambda i: (0, 0, 0))
    return pl.pallas_call(
        body, name=name, grid=(S // tm,),
        in_specs=[
            pl.BlockSpec((tm, 1024), lambda i: (i, 0)),
            pl.BlockSpec((tm, 1024), lambda i: (i, 0)),
            pl.BlockSpec((tm, B_WIDTH), lambda i: (i, 0)),
            w3, w3, b3, vec, vec,
        ],
        out_specs=[pl.BlockSpec((tm, 1024), lambda i: (i, 0)), pl.BlockSpec((tm, B_WIDTH), lambda i: (i, 0)),
                   w3, b3, vec, vec, vec],
        out_shape=[
            jax.ShapeDtypeStruct((S, 1024), BF16), jax.ShapeDtypeStruct((S, B_WIDTH), F32),
            jax.ShapeDtypeStruct((A_GROUPS, GMLP_BLOCK, GMLP_BLOCK), F32),
            jax.ShapeDtypeStruct((A_GROUPS, GMLP_BLOCK, 1), F32),
            jax.ShapeDtypeStruct((1, B_WIDTH), F32), jax.ShapeDtypeStruct((1, B_WIDTH), F32),
            jax.ShapeDtypeStruct((1, B_WIDTH), F32),
        ],
        scratch_shapes=[pltpu.VMEM((tm, A_WIDTH), F32), pltpu.VMEM((tm, A_WIDTH), F32)],
        compiler_params=_params(("arbitrary",)),
    )(proj, dmix, hc, wm, wmt, bcol, lg, lb)


def _even_bwd2(proj, dhc, cw, *, name):
    S = proj.shape[0]
    tm = _tile(S, 256)
    hb = tm // CONV_HALO
    nt = S // tm
    last_halo = S // CONV_HALO - 1
    lo = CONV_HALO - CONV_WIDTH + 1

    def body(p_ref, halo_ref, d_ref, dnext_ref, cw_ref, dpb_ref, dcw_ref, hext_ref, dext_ref, hsh_ref, dsh_ref):
        i = pl.program_id(0)

        @pl.when(i == 0)
        def _():
            dcw_ref[...] = jnp.zeros_like(dcw_ref)

        hh = halo_ref[:, 0:B_WIDTH] * _sigmoid(halo_ref[:, B_WIDTH:2 * B_WIDTH])
        hext_ref[0:CONV_HALO, :] = jnp.where(i > 0, hh, 0.0)
        hext_ref[CONV_HALO:CONV_HALO + tm, :] = p_ref[:, 0:B_WIDTH] * _sigmoid(p_ref[:, B_WIDTH:2 * B_WIDTH])
        dext_ref[0:tm, :] = d_ref[...]
        dext_ref[tm:tm + CONV_HALO, :] = jnp.where(i < nt - 1, dnext_ref[...], 0.0)
        _fill_shifts(hsh_ref, hext_ref, tm)
        _fill_shifts(dsh_ref, dext_ref, tm)
        for r0 in range(0, tm, _CONV_ROWS):
            rows = slice(r0, r0 + _CONV_ROWS)
            dhc_b = d_ref[rows, :]
            dh = jnp.zeros((_CONV_ROWS, B_WIDTH), F32)
            for k in range(CONV_WIDTH):
                dh = dh + cw_ref[k:k + 1, :] * _window(dsh_ref, r0 + CONV_WIDTH - 1 - k, _CONV_ROWS)
                dcw_ref[k:k + 1, :] += jnp.sum(dhc_b * _window(hsh_ref, r0 + k + lo, _CONV_ROWS), axis=0,
                                               keepdims=True)
            ba_b = p_ref[rows, 0:B_WIDTH]
            sg_b = _sigmoid(p_ref[rows, B_WIDTH:2 * B_WIDTH])
            dpb_ref[rows, 0:B_WIDTH] = (dh * sg_b).astype(BF16)
            dpb_ref[rows, B_WIDTH:2 * B_WIDTH] = (dh * ba_b * sg_b * (1.0 - sg_b)).astype(BF16)

    return pl.pallas_call(
        body, name=name, grid=(nt,),
        in_specs=[
            pl.BlockSpec((tm, 1024), lambda i: (i, 1)),
            pl.BlockSpec((CONV_HALO, 1024), lambda i: (jnp.maximum(i * hb - 1, 0), 1)),
            pl.BlockSpec((tm, B_WIDTH), lambda i: (i, 0)),
            pl.BlockSpec((CONV_HALO, B_WIDTH), lambda i: (jnp.minimum((i + 1) * hb, last_halo), 0)),
            pl.BlockSpec((CONV_HALO, B_WIDTH), lambda i: (0, 0)),
        ],
        out_specs=[pl.BlockSpec((tm, 1024), lambda i: (i, 0)), pl.BlockSpec((CONV_HALO, B_WIDTH), lambda i: (0, 0))],
        out_shape=[jax.ShapeDtypeStruct((S, 1024), BF16), jax.ShapeDtypeStruct((CONV_HALO, B_WIDTH), F32)],
        scratch_shapes=[pltpu.VMEM((tm + CONV_HALO, B_WIDTH), F32), pltpu.VMEM((tm + CONV_HALO, B_WIDTH), F32),
                        pltpu.VMEM((_SHIFTS, tm + CONV_HALO, B_WIDTH), F32),
                        pltpu.VMEM((_SHIFTS, tm + CONV_HALO, B_WIDTH), F32)],
        compiler_params=_params(("arbitrary",)),
    )(proj, proj, dhc, dhc, cw)


_CA_SCALE = CA_HEAD_DIM ** -0.5


def _softmax_rows(s):
    e = jnp.exp(s - jnp.max(s, axis=-1, keepdims=True))
    return e / jnp.sum(e, axis=-1, keepdims=True)


def _attn_fwd(q, k, v, *, name):
    S = q.shape[0]

    def body(r, f, o, acc, s):
        for h in range(CA_HEADS):
            cols = _cols(h, CA_HEAD_DIM)
            p = _softmax_rows(_dot_nt(r[0][:, cols], f[0][:, cols]) * _CA_SCALE)
            o[0][:, cols] = _dot(p.astype(BF16), f[1][:, cols]).astype(BF16)

    (o_,), _ = _rows_call(name, _tile(S, 512), [q], [k, v], [(D_MODEL, BF16)], [], body)
    return o_


def _attn_bwd(dy, wo, q, k, v, *, name):
    S = q.shape[0]
    M = k.shape[0]

    def body(r, f, o, acc, s):
        dyb = r[0][...].astype(BF16)
        for h in range(CA_HEADS):
            cols = _cols(h, CA_HEAD_DIM)
            qh = r[1][:, cols]
            kh = f[0][:, cols]
            vh = f[1][:, cols]
            doh = _dot_nt(dyb, f[2 + h][...]).astype(BF16)
            p = _softmax_rows(_dot_nt(qh, kh) * _CA_SCALE)
            acc[1][:, cols] += _dot_tn(p.astype(BF16), doh)
            dp = _dot_nt(doh, vh)
            ds = (p * (dp - jnp.sum(dp * p, axis=-1, keepdims=True)) * _CA_SCALE).astype(BF16)
            o[0][:, cols] = _dot(ds, kh).astype(BF16)
            acc[0][:, cols] += _dot_tn(ds, qh)

    (dq,), (dk, dv) = _rows_call(name, _tile(S, 512), [dy, q], [k, v] + wo, [(D_MODEL, BF16)],
                                 [((M, D_MODEL), F32), ((M, D_MODEL), F32)], body)
    return dq, dk, dv


_STATE_TILE = 2 * STATE_ROWS
N_SETS = 4
SET_CH = C_WIDTH // N_SETS
SET_COLS = N_STATE // N_SETS // STATE_LANES


def _set_groups(j):
    return [SET_COLS * j + c for c in range(SET_COLS)] + [STATE_ROWS + SET_COLS * j + c for c in range(SET_COLS)]


def _pack_state(re, im):
    hi = lax.bitcast_convert_type(re.astype(BF16).astype(F32), jnp.uint32)
    lo = lax.bitcast_convert_type(im.astype(BF16).astype(F32), jnp.uint32) >> 16
    return hi | lo


def _unpack_state(word):
    re = lax.bitcast_convert_type(word & jnp.uint32(0xFFFF0000), F32)
    im = lax.bitcast_convert_type(word << 16, F32)
    return re, im


def _state_set(ref, tm, j):
    parts = [_unpack_state(ref[:, SET_COLS * j + c, :, :].reshape(tm, STATE_LANES)) for c in range(SET_COLS)]
    return jnp.concatenate([p[0].astype(BF16) for p in parts] + [p[1].astype(BF16) for p in parts], axis=1)


def _s5_readout(xs, cset, u, d, *, name, tm=512):
    tm = _tile(u.shape[0], tm)

    def body(r, f, o, acc, s):
        y0 = jnp.concatenate([_dot(_state_set(r[0], tm, j), f[0][j]) for j in range(N_SETS)], axis=1)
        y = y0 + f[1][...] * r[1][...]
        o[0][...] = y
        o[1][...] = _gelu(y)[0].astype(BF16)

    (y, yg), _ = _rows_call(name, tm, [xs, u], [cset, d], [(C_WIDTH, F32), (C_WIDTH, BF16)], [], body)
    return y, yg


def _state_grad_sets(a, st, *, name, ts=256):
    ts = _tile(a.shape[0], ts)

    def body(r, f, o, acc, s):
        for j in range(N_SETS):
            acc[0][j] += _dot_tn(r[0][:, _cols(j, SET_CH)].astype(BF16), _state_set(r[1], ts, j))

    _, (out,) = _rows_call(name, ts, [a, st], [], [], [((N_SETS, SET_CH, 2 * N_STATE // N_SETS), F32)], body)
    return out


def _glu_out(yg, ws, x, *, name, tm=512):
    n = ws[0][1][2]

    def body(r, f, o, acc, s):
        ygv = r[0][...]
        ov = [_dot(ygv, f[p][...]) for p in range(N_CHIPS)]
        for p in range(N_CHIPS):
            o[0][:, _cols(p, n)] = ov[p].astype(BF16)
        for p in range(2):
            o[1][:, _cols(p, n)] = r[1][:, _cols(p, n)] + ov[p] * _sigmoid(ov[2 + p])

    (o_, y), _ = _rows_call(name, _tile(x.shape[0], tm), [yg, x], ws, [(2 * D_MODEL, BF16), (D_MODEL, F32)], [], body)
    return o_, y


def _glu_out_bwd(o_, dy, ws, y, u, d, *, name, tm=512):
    n = ws[0][1][2]

    def body(r, f, o, acc, s):
        o1 = r[0][:, 0:D_MODEL].astype(F32)
        sg = _sigmoid(r[0][:, D_MODEL:2 * D_MODEL].astype(F32))
        dyv = r[1][...]
        do1 = (dyv * sg).astype(BF16)
        do2 = (dyv * o1 * sg * (1.0 - sg)).astype(BF16)
        o[0][:, 0:D_MODEL] = do1
        o[0][:, D_MODEL:2 * D_MODEL] = do2
        dyg = None
        for p in range(N_CHIPS):
            t = _dot_nt((do1 if p < 2 else do2)[:, _cols(p % 2, n)], f[1 + p][...])
            dyg = t if dyg is None else dyg + t
        yv = r[2][...]
        dys = dyg * _gelu_grad(yv, _gelu(yv)[1])
        o[1][...] = dys.astype(BF16)
        o[2][...] = f[0][...] * dys
        acc[0][...] += jnp.sum(dys * r[3][...], axis=0, keepdims=True)

    (do, dys, dus), (dd,) = _rows_call(name, _tile(dy.shape[0], tm), [o_, dy, y, u], [d] + ws,
                                       [(2 * D_MODEL, BF16), (C_WIDTH, BF16), (C_WIDTH, F32)], [((1, C_WIDTH), F32)],
                                       body)
    return do, dys, dus, dd


def _s5_in_bwd(gs, bset, dus, ws, x, g, dres, *, name, tm=512):
    D = x.shape[1]
    tm = _tile(x.shape[0], tm)

    def body(r, f, o, acc, s):
        du0 = jnp.concatenate([_dot_nt(_state_set(r[0], tm, j), f[1][j]) for j in range(N_SETS)], axis=1)
        du = (du0 + r[1][...]).astype(BF16)
        o[0][...] = du
        dx, dg = _rms_bwd_tile(r[2][...], f[0][...], _cat_nt(du, f[2:]))
        o[1][...] = dx + r[3][...]
        acc[0][...] += dg

    (du, dx), (dg,) = _rows_call(name, tm, [gs, dus, x, dres], [_vec(g), bset] + ws,
                                 [(C_WIDTH, BF16), (D, F32)], [((1, D), F32)], body)
    return du, dx, dg


_SCAN_CHUNK = 256
_RE = slice(0, STATE_ROWS)
_IM = slice(STATE_ROWS, 2 * STATE_ROWS)
assert SCAN_BLOCK == 8


def _token(g, i, rows):
    return pl.ds(pl.multiple_of(g * (rows * SCAN_BLOCK), rows * SCAN_BLOCK) + i, rows, stride=SCAN_BLOCK)


def _fill_chunk(s3, a_ref, wset, tc, nt):
    for j in range(N_SETS):
        av = a_ref[:, _cols(j, SET_CH)].astype(BF16)
        y = _dot_nt(av, wset[j]) if nt else _dot(av, wset[j])
        for k, c in enumerate(_set_groups(j)):
            s3[:, 8 * c:8 * (c + 1), :] = y[:, _cols(k, STATE_LANES)].reshape(tc // 8, 8, STATE_LANES)


def _chunk_token(s3, g, i):
    return s3[g, pl.ds(i, _STATE_TILE, stride=SCAN_BLOCK), :]


def _scan_fwd(u, bset, pw, *, name):
    S = u.shape[0]
    tc = _tile(S, _SCAN_CHUNK, 8)

    def body(u_ref, bset_ref, pw_ref, xs_ref, st_ref, s3):
        @pl.when(pl.program_id(0) == 0)
        def _():
            st_ref[...] = jnp.zeros_like(st_ref)

        _fill_chunk(s3, u_ref, bset_ref, tc, nt=False)
        ar = pw_ref[0, _RE, :]
        ai = pw_ref[0, _IM, :]

        def block(g, carry):
            xr, xi = carry
            cr = ci = nr = ni = None
            for j in range(SCAN_BLOCK):
                b = _chunk_token(s3, g, j)
                br, bi = b[_RE], b[_IM]
                cr, ci = (br, bi) if j == 0 else (ar * cr - ai * ci + br, ar * ci + ai * cr + bi)
                pr, pi = pw_ref[j, _RE, :], pw_ref[j, _IM, :]
                nr = pr * xr - pi * xi + cr
                ni = pr * xi + pi * xr + ci
                xs_ref[_token(g, j, STATE_ROWS), :] = _pack_state(nr, ni)
            return nr, ni

        xr, xi = lax.fori_loop(0, tc // SCAN_BLOCK, block, (st_ref[_RE, :], st_ref[_IM, :]), unroll=4)
        st_ref[_RE, :] = xr
        st_ref[_IM, :] = xi

    return pl.pallas_call(
        body, name=name, grid=(S // tc,),
        in_specs=[pl.BlockSpec((tc, u.shape[1]), lambda i: (i, 0)), pl.BlockSpec(bset.shape, lambda i: (0, 0, 0)),
                  pl.BlockSpec(pw.shape, lambda i: (0, 0, 0))],
        out_specs=pl.BlockSpec((tc * STATE_ROWS, STATE_LANES), lambda i: (i, 0)),
        out_shape=jax.ShapeDtypeStruct((S * STATE_ROWS, STATE_LANES), jnp.uint32),
        scratch_shapes=[pltpu.VMEM((2 * STATE_ROWS, STATE_LANES), F32),
                        pltpu.VMEM((tc // 8, _STATE_TILE * 8, STATE_LANES), F32)],
        compiler_params=_params(("arbitrary",)),
    )(u, bset, pw)


def _scan_bwd(dys, cset, xs, pw, *, name):
    S = dys.shape[0]
    tc = _tile(S, _SCAN_CHUNK, 8)
    nc = S // tc

    def body(dys_ref, cset_ref, xs_ref, pw_ref, g_ref, da_ref, st_ref, s3):
        @pl.when(pl.program_id(0) == 0)
        def _():
            st_ref[...] = jnp.zeros_like(st_ref)
            da_ref[...] = jnp.zeros_like(da_ref)

        _fill_chunk(s3, dys_ref, cset_ref, tc, nt=True)
        ar = pw_ref[0, _RE, :]
        ai = pw_ref[0, _IM, :]

        def block(k, carry):
            gr, gi, dar, dai = carry
            g = tc // SCAN_BLOCK - 1 - k
            cr = ci = None
            pgr, pgi = gr, gi
            for j in range(SCAN_BLOCK):
                i = SCAN_BLOCK - 1 - j
                xr, xi = _unpack_state(xs_ref[_token(g, i, STATE_ROWS), :])
                dar = dar + pgr * xr + pgi * xi
                dai = dai + pgi * xr - pgr * xi
                d = _chunk_token(s3, g, i)
                dr, di = d[_RE], d[_IM]
                cr, ci = (dr, di) if j == 0 else (ar * cr + ai * ci + dr, ar * ci - ai * cr + di)
                pr, pi = pw_ref[j, _RE, :], pw_ref[j, _IM, :]
                pgr = pr * gr + pi * gi + cr
                pgi = pr * gi - pi * gr + ci
                g_ref[_token(g, i, STATE_ROWS), :] = _pack_state(pgr, pgi)
            return pgr, pgi, dar, dai

        init = (st_ref[_RE, :], st_ref[_IM, :], da_ref[_RE, :], da_ref[_IM, :])
        gr, gi, dar, dai = lax.fori_loop(0, tc // SCAN_BLOCK, block, init, unroll=4)
        st_ref[_RE, :] = gr
        st_ref[_IM, :] = gi
        da_ref[_RE, :] = dar
        da_ref[_IM, :] = dai

    packed = pl.BlockSpec((tc * STATE_ROWS, STATE_LANES), lambda i: (nc - 1 - i, 0))
    vec = pl.BlockSpec((2 * STATE_ROWS, STATE_LANES), lambda i: (0, 0))
    return pl.pallas_call(
        body, name=name, grid=(nc,),
        in_specs=[pl.BlockSpec((tc, dys.shape[1]), lambda i: (nc - 1 - i, 0)),
                  pl.BlockSpec(cset.shape, lambda i: (0, 0, 0)), packed, pl.BlockSpec(pw.shape, lambda i: (0, 0, 0))],
        out_specs=[packed, vec],
        out_shape=[jax.ShapeDtypeStruct(xs.shape, jnp.uint32), jax.ShapeDtypeStruct((2 * STATE_ROWS, STATE_LANES), F32)],
        scratch_shapes=[pltpu.VMEM((2 * STATE_ROWS, STATE_LANES), F32),
                        pltpu.VMEM((tc // 8, _STATE_TILE * 8, STATE_LANES), F32)],
        compiler_params=_params(("arbitrary",)),
    )(dys, cset, xs, pw)


def _down_loss_head(h, ws, x, g, target, *, name, tm=512):
    S, D = x.shape

    def body(r, f, o, acc, s):
        xv = r[1][...]
        for p in range(N_CHIPS):
            xv = xv + _dot(r[0][p], f[1 + p][...])
        gv = f[0][...]
        rs = lax.rsqrt(_mean(xv * xv) + EPS)
        xh = xv * rs
        err = xh * gv - r[2][...]
        acc[1][...] += 0.5 * jnp.sum(_mean(err * err), axis=0, keepdims=True)
        dy = err * (1.0 / D)
        dyg = dy * gv
        o[0][...] = rs * (dyg - xh * _mean(dyg * xh))
        acc[0][...] += jnp.sum(dy * xh, axis=0, keepdims=True)

    (dx,), (dg, loss) = _rows_call(name, _tile(S, tm), [h, x, target], [_vec(g)] + ws, [(D, F32)],
                                   [((1, D), F32), ((1, 128), F32)], body)
    return dx, dg, loss


_ADAM_C1 = 1.0 - ADAM_B1 ** ADAM_STEP
_ADAM_C2 = 1.0 - ADAM_B2 ** ADAM_STEP
_ONE_BLOCK_BYTES = 8 * 1024 * 1024
_SUM_ROWS = 512
_ADAM_ROWS = 512


def _adamw_math(w, g, m, v):
    nm = ADAM_B1 * m + (1.0 - ADAM_B1) * g
    nv = ADAM_B2 * v + (1.0 - ADAM_B2) * (g * g)
    m_hat = nm / _ADAM_C1
    v_hat = nv / _ADAM_C2
    return -ADAM_LR * (m_hat / (jnp.sqrt(v_hat) + ADAM_EPS) + ADAM_WD * w), nm, nv


def _adamw_shard(w, gsrc, m, v, *, name):
    R, C = w.shape
    n_l = len(gsrc)
    rows = R // n_l
    tr = rows
    for _, r0 in gsrc:
        tr = math.gcd(tr, r0) if r0 else tr
    tr = _tile(tr, _ADAM_ROWS, 8)
    nb = rows // tr
    assert rows % tr == 0 and all(r0 % tr == 0 for _, r0 in gsrc)

    def body(*refs):
        w_ref, g_refs, (m_ref, v_ref, go_ref, d_ref, nm_ref, nv_ref) = refs[0], refs[1:1 + n_l], refs[1 + n_l:]
        layer = pl.program_id(0) // nb
        gv = g_refs[0][...]
        for l in range(1, n_l):
            gv = jnp.where(layer == l, g_refs[l][...], gv)
        go_ref[...] = gv
        d_ref[...], nm_ref[...], nv_ref[...] = _adamw_math(w_ref[...], gv, m_ref[...], v_ref[...])

    def g_spec(l, r0):
        return pl.BlockSpec((tr, C), lambda i: (r0 // tr + jnp.clip(i - l * nb, 0, nb - 1), 0))

    blk = pl.BlockSpec((tr, C), lambda i: (i, 0))
    out = jax.ShapeDtypeStruct((R, C), F32)
    return pl.pallas_call(
        body, name=name, grid=(R // tr,),
        in_specs=[blk] + [g_spec(l, r0) for l, (_, r0) in enumerate(gsrc)] + [blk, blk], out_specs=[blk] * 4,
        out_shape=[out] * 4, compiler_params=_params(("parallel",)),
    )(w, *[g for g, _ in gsrc], m, v)


def _adamw_small(ws, gs, ms, vs, *, name):
    n = len(ws)

    def body(*refs):
        w_r, g_r, m_r, v_r = refs[:n], refs[n:2 * n], refs[2 * n:3 * n], refs[3 * n:4 * n]
        d_r, nm_r, nv_r = refs[4 * n:5 * n], refs[5 * n:6 * n], refs[6 * n:7 * n]
        for k in range(n):
            d_r[k][...], nm_r[k][...], nv_r[k][...] = _adamw_math(w_r[k][...], g_r[k][...], m_r[k][...], v_r[k][...])

    vm = pl.BlockSpec(memory_space=pltpu.VMEM)
    out = [jax.ShapeDtypeStruct(w.shape, F32) for w in ws]
    res = pl.pallas_call(body, name=name, in_specs=[vm] * (4 * n), out_specs=[vm] * (3 * n), out_shape=out * 3,
                         compiler_params=pltpu.CompilerParams(vmem_limit_bytes=VMEM_LIMIT))(*ws, *gs, *ms, *vs)
    return res[:n], res[n:2 * n], res[2 * n:]


def _sum_slots(x, *, name):
    n, R, C = x.shape
    tr = R if (n + 1) * R * C * 4 <= _ONE_BLOCK_BYTES else _tile(R, 256, 8)

    def body(x_ref, o_ref):
        acc = x_ref[0]
        for k in range(1, n):
            acc = acc + x_ref[k]
        o_ref[...] = acc

    return pl.pallas_call(
        body, name=name, grid=(R // tr,),
        in_specs=[pl.BlockSpec((n, tr, C), lambda i: (0, i, 0))], out_specs=pl.BlockSpec((tr, C), lambda i: (i, 0)),
        out_shape=jax.ShapeDtypeStruct((R, C), F32), compiler_params=_params(("parallel",)),
    )(x)


def _pair_sum(g, r, where, *, name):
    n, R, C = g.shape
    Rh = R // 2
    tr = _tile(Rh, _SUM_ROWS, 16)
    nb = Rh // tr

    def body(where_ref, g_ref, r_ref, o_ref):
        o_ref[...] = (g_ref[...] + r_ref[...]).astype(BF16)

    def slot(p, w):
        return p + jnp.where(p >= w[0], 1, 0)

    return pl.pallas_call(
        body, name=name,
        grid_spec=pltpu.PrefetchScalarGridSpec(
            num_scalar_prefetch=1, grid=(n - 1, nb),
            in_specs=[pl.BlockSpec((1, tr, C), lambda p, i, w: (slot(p, w), w[1] * nb + i, 0)),
                      pl.BlockSpec((1, tr, C), lambda p, i, w: (slot(p, w), i, 0))],
            out_specs=pl.BlockSpec((1, tr, C), lambda p, i, w: (slot(p, w), i, 0)),
        ),
        out_shape=jax.ShapeDtypeStruct((n, Rh, C), BF16), compiler_params=_params(("parallel", "parallel")),
    )(where, g, r)


def _chip_sum(g, r, slots, where, *, name):
    n, R, C = g.shape
    Rh = R // 2
    tr = _tile(Rh, _SUM_ROWS, 16)
    nb = Rh // tr

    def body(w_ref, g_ref, r_ref, s_ref, o_ref):
        acc = g_ref[0] + r_ref[0]
        for k in range(slots.shape[0]):
            acc = acc + s_ref[k].astype(F32)
        o_ref[...] = acc

    return pl.pallas_call(
        body, name=name,
        grid_spec=pltpu.PrefetchScalarGridSpec(
            num_scalar_prefetch=1, grid=(nb,),
            in_specs=[pl.BlockSpec((1, tr, C), lambda i, w: (w[0], w[1] * nb + i, 0)),
                      pl.BlockSpec((1, tr, C), lambda i, w: (w[0], i, 0)),
                      pl.BlockSpec((slots.shape[0], tr, C), lambda i, w: (0, i, 0))],
            out_specs=pl.BlockSpec((tr, C), lambda i, w: (w[1] * nb + i, 0)),
        ),
        out_shape=jax.ShapeDtypeStruct((R, C), F32), compiler_params=_params(("parallel",)),
    )(where, g, r, slots)


ANY = pl.BlockSpec(memory_space=pl.ANY)


def _place():
    return lax.axis_index("x"), lax.axis_index("y"), lax.axis_index("c")


def _other_chips(x, y):
    return [(1 - x, y), (x, 1 - y), (1 - x, 1 - y)]


def _aliased_comm_call(body, bufs, n_sems, *, name):
    n = len(bufs)
    return pl.pallas_call(
        body, name=name, out_shape=[jax.ShapeDtypeStruct(b.shape, b.dtype) for b in bufs],
        in_specs=[ANY] * n, out_specs=[ANY] * n, input_output_aliases={k: k for k in range(n)},
        scratch_shapes=[pltpu.SemaphoreType.DMA((n_sems,)), pltpu.SemaphoreType.DMA((n_sems,))],
    )(*bufs)


HBM = pl.BlockSpec(memory_space=pltpu.HBM)
SEM = pl.BlockSpec(memory_space=pltpu.SEMAPHORE)
_SPLIT = pltpu.CompilerParams(has_side_effects=pltpu.SideEffectType.DATAFLOW_SIDE_EFFECTING)


def _in_hbm(arrs):
    return [pltpu.with_memory_space_constraint(a, pltpu.HBM) for a in arrs]


def _gather_ici_start(bufs, after, *, name):
    n = len(bufs)

    def body(*refs):
        send_sems, recv_sems, outs, token = refs[n + 1], refs[n + 2], refs[n + 3:2 * n + 3], refs[2 * n + 3]
        x, y, c = _place()
        for b in range(n):
            rh = bufs[b].shape[1] // 2
            part = outs[b].at[2 * x + y, pl.ds(c * rh, rh), :]
            for j, chip in enumerate(_other_chips(x, y)):
                pltpu.make_async_remote_copy(src_ref=part, dst_ref=part, send_sem=send_sems.at[3 * b + j],
                                             recv_sem=recv_sems.at[3 * b + j], device_id=(*chip, c),
                                             device_id_type=MESH).start()
        token[...] = jnp.zeros_like(token)

    res = pl.pallas_call(
        body, name=name,
        out_shape=(pltpu.SemaphoreType.DMA((3 * n,)), pltpu.SemaphoreType.DMA((3 * n,)),
                   *[pltpu.HBM(b.shape, b.dtype) for b in bufs], jax.ShapeDtypeStruct((8, 128), F32)),
        in_specs=[HBM] * n + [ANY], out_specs=(SEM, SEM, *[HBM] * n, pl.BlockSpec(memory_space=pltpu.VMEM)),
        input_output_aliases={k: k + 2 for k in range(n)}, compiler_params=_SPLIT,
    )(*_in_hbm(bufs), after)
    return res[0], res[1], list(res[2:2 + n]), res[2 + n]


def _gather_ici_wait(send_sems, recv_sems, bufs, first, after, *, name):
    n = len(bufs)

    def body(*refs):
        ins, ss, rs = refs[:n], refs[n], refs[n + 1]
        x, y, c = _place()
        for b in range(n):
            rh = bufs[b].shape[1] // 2
            mine = ins[b].at[2 * x + y, pl.ds(c * rh, rh), :]
            for j, (cx, cy) in enumerate(_other_chips(x, y)):
                theirs = ins[b].at[2 * cx + cy, pl.ds(c * rh, rh), :]
                cp = pltpu.make_async_remote_copy(src_ref=mine, dst_ref=theirs, send_sem=ss.at[3 * (first + b) + j],
                                                  recv_sem=rs.at[3 * (first + b) + j], device_id=(cx, cy, c),
                                                  device_id_type=MESH)
                cp.wait_send()
                cp.wait_recv()

    return list(pl.pallas_call(
        body, name=name, out_shape=[pltpu.HBM(b.shape, b.dtype) for b in bufs],
        in_specs=[HBM] * n + [SEM, SEM, ANY], out_specs=[HBM] * n,
        input_output_aliases={k: k for k in range(n)}, compiler_params=_SPLIT,
    )(*bufs, send_sems, recv_sems, after))


def _gather_forward(bufs, *, name):
    n = len(bufs)

    def body(*refs):
        outs, send_sems, recv_sems = refs[n:2 * n], refs[2 * n], refs[2 * n + 1]
        x, y, c = _place()

        def copy(b, j, chip, hc):
            rh = bufs[b].shape[1] // 2
            part = outs[b].at[2 * chip[0] + chip[1], pl.ds(hc * rh, rh), :]
            return pltpu.make_async_remote_copy(src_ref=part, dst_ref=part, send_sem=send_sems.at[3 * b + j],
                                                recv_sem=recv_sems.at[3 * b + j], device_id=(x, y, 1 - c),
                                                device_id_type=MESH)

        sends = [copy(b, j, chip, c) for b in range(n) for j, chip in enumerate(_other_chips(x, y))]
        for cp in sends:
            cp.start()
        for b in range(n):
            for j, chip in enumerate(_other_chips(x, y)):
                copy(b, j, chip, 1 - c).wait_recv()
        for cp in sends:
            cp.wait_send()

    return _aliased_comm_call(body, bufs, 3 * n, name=name)


def _forward_copy(buf, send_sems, recv_sems, k, chip, half, to):
    rh = buf.shape[1] // 2
    part = buf.at[2 * chip[0] + chip[1], pl.ds(half * rh, rh), :]
    return pltpu.make_async_remote_copy(src_ref=part, dst_ref=part, send_sem=send_sems.at[k], recv_sem=recv_sems.at[k],
                                        device_id=to, device_id_type=MESH)


def _gather_forward_start(bufs, after, *, name):
    n = len(bufs)

    def body(*refs):
        send_sems, recv_sems, outs, token = refs[n + 1], refs[n + 2], refs[n + 3:2 * n + 3], refs[2 * n + 3]
        x, y, c = _place()
        for b in range(n):
            for j, chip in enumerate(_other_chips(x, y)):
                _forward_copy(outs[b], send_sems, recv_sems, 3 * b + j, chip, c, (x, y, 1 - c)).start()
        token[...] = jnp.zeros_like(token)

    res = pl.pallas_call(
        body, name=name,
        out_shape=(pltpu.SemaphoreType.DMA((3 * n,)), pltpu.SemaphoreType.DMA((3 * n,)),
                   *[pltpu.HBM(b.shape, b.dtype) for b in bufs], jax.ShapeDtypeStruct((8, 128), F32)),
        in_specs=[HBM] * n + [ANY], out_specs=(SEM, SEM, *[HBM] * n, pl.BlockSpec(memory_space=pltpu.VMEM)),
        input_output_aliases={k: k + 2 for k in range(n)}, compiler_params=_SPLIT,
    )(*_in_hbm(bufs), after)
    return res[0], res[1], list(res[2:2 + n]), res[2 + n]


def _gather_forward_wait(send_sems, recv_sems, bufs, after, *, name):
    n = len(bufs)

    def body(*refs):
        ins, ss, rs = refs[:n], refs[n], refs[n + 1]
        x, y, c = _place()
        for b in range(n):
            for j, chip in enumerate(_other_chips(x, y)):
                _forward_copy(ins[b], ss, rs, 3 * b + j, chip, c, (x, y, 1 - c)).wait_send()
                _forward_copy(ins[b], ss, rs, 3 * b + j, chip, 1 - c, (x, y, 1 - c)).wait_recv()

    return list(pl.pallas_call(
        body, name=name, out_shape=[pltpu.HBM(b.shape, b.dtype) for b in bufs],
        in_specs=[HBM] * n + [SEM, SEM, ANY], out_specs=[HBM] * n,
        input_output_aliases={k: k for k in range(n)}, compiler_params=_SPLIT,
    )(*bufs, send_sems, recv_sems, after))


def _chip_exchange_start(hs, *, name):
    n = len(hs)
    lands = [lax.empty((3,) + h.shape[1:], h.dtype) for h in hs]

    def body(*refs):
        send_sems, recv_sems = refs[2 * n], refs[2 * n + 1]
        h_out, l_out, token = refs[2 * n + 2:3 * n + 2], refs[3 * n + 2:4 * n + 2], refs[4 * n + 2]
        x, y, c = _place()
        for b in range(n):
            for j, (cx, cy) in enumerate(_other_chips(x, y)):
                pltpu.make_async_remote_copy(src_ref=h_out[b].at[2 * cx + cy], dst_ref=l_out[b].at[j],
                                             send_sem=send_sems.at[3 * b + j], recv_sem=recv_sems.at[3 * b + j],
                                             device_id=(cx, cy, c), device_id_type=MESH).start()
        token[...] = jnp.zeros_like(token)

    res = pl.pallas_call(
        body, name=name,
        out_shape=(pltpu.SemaphoreType.DMA((3 * n,)), pltpu.SemaphoreType.DMA((3 * n,)),
                   *[pltpu.HBM(a.shape, a.dtype) for a in hs + lands], jax.ShapeDtypeStruct((8, 128), F32)),
        in_specs=[HBM] * (2 * n), out_specs=(SEM, SEM, *[HBM] * (2 * n), pl.BlockSpec(memory_space=pltpu.VMEM)),
        input_output_aliases={k: k + 2 for k in range(2 * n)}, compiler_params=_SPLIT,
    )(*_in_hbm(hs + lands))
    return res[0], res[1], list(res[2:2 + n]), list(res[2 + n:2 + 2 * n]), res[2 + 2 * n]


def _chip_exchange_wait(send_sems, recv_sems, hs, lands, after, *, name):
    n = len(hs)

    def body(*refs):
        h_in, l_in, ss, rs = refs[:n], refs[n:2 * n], refs[2 * n], refs[2 * n + 1]
        x, y, c = _place()
        for b in range(n):
            for j, (cx, cy) in enumerate(_other_chips(x, y)):
                cp = pltpu.make_async_remote_copy(src_ref=h_in[b].at[2 * cx + cy], dst_ref=l_in[b].at[j],
                                                  send_sem=ss.at[3 * b + j], recv_sem=rs.at[3 * b + j],
                                                  device_id=(cx, cy, c), device_id_type=MESH)
                cp.wait_send()
                cp.wait_recv()

    res = pl.pallas_call(
        body, name=name, out_shape=[pltpu.HBM(a.shape, a.dtype) for a in hs + lands],
        in_specs=[HBM] * (2 * n) + [SEM, SEM, ANY], out_specs=[HBM] * (2 * n),
        input_output_aliases={k: k for k in range(2 * n)}, compiler_params=_SPLIT,
    )(*hs, *lands, send_sems, recv_sems, after)
    return list(res[n:])


def _peers(x, y, c):
    return [((1 - x) if fx else x, (1 - y) if fy else y, (1 - c) if fc else c)
            for fx in (0, 1) for fy in (0, 1) for fc in (0, 1) if fx or fy or fc]


def _all_to_all_start(slab, after, *, name):
    land = lax.empty((N_DEV,) + slab.shape, slab.dtype)

    def body(slab_in, land_in, after_ref, send_sems, recv_sems, slab_out, land_out, token):
        x, y, c = _place()
        for k, peer in enumerate(_peers(x, y, c)):
            pltpu.make_async_remote_copy(src_ref=slab_out, dst_ref=land_out.at[4 * x + 2 * y + c],
                                         send_sem=send_sems.at[k], recv_sem=recv_sems.at[k], device_id=peer,
                                         device_id_type=MESH).start()
        token[...] = jnp.zeros_like(token)

    return pl.pallas_call(
        body, name=name,
        out_shape=(pltpu.SemaphoreType.DMA((N_DEV - 1,)), pltpu.SemaphoreType.DMA((N_DEV - 1,)),
                   pltpu.HBM(slab.shape, slab.dtype), pltpu.HBM(land.shape, land.dtype),
                   jax.ShapeDtypeStruct((8, 128), F32)),
        in_specs=[HBM, HBM, ANY], out_specs=(SEM, SEM, HBM, HBM, pl.BlockSpec(memory_space=pltpu.VMEM)),
        input_output_aliases={0: 2, 1: 3}, compiler_params=_SPLIT,
    )(*_in_hbm([slab, land]), after)


def _all_to_all_wait(send_sems, recv_sems, slab, land, afters, *, name):
    def body(slab_in, land_in, ss, rs, *_):
        x, y, c = _place()
        for k, (px, py, pc) in enumerate(_peers(x, y, c)):
            cp = pltpu.make_async_remote_copy(src_ref=slab_in, dst_ref=land_in.at[4 * px + 2 * py + pc],
                                              send_sem=ss.at[k], recv_sem=rs.at[k], device_id=(px, py, pc),
                                              device_id_type=MESH)
            cp.wait_send()
            cp.wait_recv()

    return pl.pallas_call(
        body, name=name, out_shape=[pltpu.HBM(slab.shape, slab.dtype), pltpu.HBM(land.shape, land.dtype)],
        in_specs=[HBM, HBM, SEM, SEM] + [ANY] * len(afters), out_specs=[HBM, HBM], input_output_aliases={0: 0, 1: 1},
        compiler_params=_SPLIT,
    )(slab, land, send_sems, recv_sems, *afters)


def _pair_exchange_start(gs, *, name):
    n = len(gs)
    lands = [lax.empty((g.shape[0], g.shape[1] // 2, g.shape[2]), g.dtype) for g in gs]

    def body(*refs):
        send_sems, recv_sems = refs[2 * n], refs[2 * n + 1]
        g_out, l_out, token = refs[2 * n + 2:3 * n + 2], refs[3 * n + 2:4 * n + 2], refs[4 * n + 2]
        x, y, c = _place()
        for b in range(n):
            rh = gs[b].shape[1] // 2
            pltpu.make_async_remote_copy(src_ref=g_out[b].at[:, pl.ds((1 - c) * rh, rh), :], dst_ref=l_out[b],
                                         send_sem=send_sems.at[b], recv_sem=recv_sems.at[b],
                                         device_id=(x, y, 1 - c), device_id_type=MESH).start()
        token[...] = jnp.zeros_like(token)

    res = pl.pallas_call(
        body, name=name,
        out_shape=(pltpu.SemaphoreType.DMA((n,)), pltpu.SemaphoreType.DMA((n,)),
                   *[pltpu.HBM(a.shape, a.dtype) for a in gs + lands], jax.ShapeDtypeStruct((8, 128), F32)),
        in_specs=[HBM] * (2 * n), out_specs=(SEM, SEM, *[HBM] * (2 * n), pl.BlockSpec(memory_space=pltpu.VMEM)),
        input_output_aliases={k: k + 2 for k in range(2 * n)}, compiler_params=_SPLIT,
    )(*_in_hbm(gs + lands))
    return res[0], res[1], list(res[2:2 + n]), list(res[2 + n:2 + 2 * n]), res[2 + 2 * n]


def _pair_exchange_wait(send_sems, recv_sems, gs, lands, after, *, name):
    n = len(gs)

    def body(*refs):
        g_in, l_in, ss, rs = refs[:n], refs[n:2 * n], refs[2 * n], refs[2 * n + 1]
        x, y, c = _place()
        for b in range(n):
            rh = gs[b].shape[1] // 2
            cp = pltpu.make_async_remote_copy(src_ref=g_in[b].at[:, pl.ds((1 - c) * rh, rh), :], dst_ref=l_in[b],
                                              send_sem=ss.at[b], recv_sem=rs.at[b], device_id=(x, y, 1 - c),
                                              device_id_type=MESH)
            cp.wait_send()
            cp.wait_recv()

    res = pl.pallas_call(
        body, name=name, out_shape=[pltpu.HBM(a.shape, a.dtype) for a in gs + lands],
        in_specs=[HBM] * (2 * n) + [SEM, SEM, ANY], out_specs=[HBM] * (2 * n),
        input_output_aliases={k: k for k in range(2 * n)}, compiler_params=_SPLIT,
    )(*gs, *lands, send_sems, recv_sems, after)
    return list(res[:n]), list(res[n:])


def _pair_share_start(ss, *, name):
    n = len(ss)

    def body(*refs):
        send_sems, recv_sems, outs, token = refs[n], refs[n + 1], refs[n + 2:2 * n + 2], refs[2 * n + 2]
        x, y, c = _place()
        for b in range(n):
            rh = ss[b].shape[0] // 2
            mine = outs[b].at[pl.ds(c * rh, rh), :]
            pltpu.make_async_remote_copy(src_ref=mine, dst_ref=mine, send_sem=send_sems.at[b],
                                         recv_sem=recv_sems.at[b], device_id=(x, y, 1 - c),
                                         device_id_type=MESH).start()
        token[...] = jnp.zeros_like(token)

    res = pl.pallas_call(
        body, name=name,
        out_shape=(pltpu.SemaphoreType.DMA((n,)), pltpu.SemaphoreType.DMA((n,)),
                   *[pltpu.HBM(a.shape, a.dtype) for a in ss], jax.ShapeDtypeStruct((8, 128), F32)),
        in_specs=[HBM] * n, out_specs=(SEM, SEM, *[HBM] * n, pl.BlockSpec(memory_space=pltpu.VMEM)),
        input_output_aliases={k: k + 2 for k in range(n)}, compiler_params=_SPLIT,
    )(*_in_hbm(ss))
    return res[0], res[1], list(res[2:2 + n]), res[2 + n]


def _pair_share_wait(send_sems, recv_sems, ss, after, *, name):
    n = len(ss)

    def body(*refs):
        ins, sems_s, sems_r = refs[:n], refs[n], refs[n + 1]
        x, y, c = _place()
        for b in range(n):
            rh = ss[b].shape[0] // 2
            mine = ins[b].at[pl.ds(c * rh, rh), :]
            theirs = ins[b].at[pl.ds((1 - c) * rh, rh), :]
            cp = pltpu.make_async_remote_copy(src_ref=mine, dst_ref=theirs, send_sem=sems_s.at[b],
                                              recv_sem=sems_r.at[b], device_id=(x, y, 1 - c),
                                              device_id_type=MESH)
            cp.wait_send()
            cp.wait_recv()

    return list(pl.pallas_call(
        body, name=name, out_shape=[pltpu.HBM(a.shape, a.dtype) for a in ss],
        in_specs=[HBM] * n + [SEM, SEM, ANY], out_specs=[HBM] * n,
        input_output_aliases={k: k for k in range(n)}, compiler_params=_SPLIT,
    )(*ss, send_sems, recv_sems, after))


_SMALL_SHARDED = (("e_conv_w", 2), ("o_norm", 1), ("o_d", 1))
_REPLICATED = ("e_norm", "e_gmlp_w", "e_gmlp_b", "e_conv_b", "e_conv_ln_g", "e_conv_ln_b", "o_lam_re", "o_lam_im",
               "o_log_dt", "o_b_re", "o_b_im", "o_c_re", "o_c_im", "ca_norm", "ca_mem_norm", "ffn_norm", "final_norm")
_SMALL = tuple(n for n, _ in _SMALL_SHARDED) + _REPLICATED
_WEIGHTS = ("e_norm", "e_w_in", "e_gmlp_w", "e_gmlp_b", "e_conv_w", "e_conv_b", "e_conv_ln_g", "e_conv_ln_b",
            "e_w_out", "o_norm", "o_w_in", "o_lam_re", "o_lam_im", "o_log_dt", "o_b_re", "o_b_im", "o_c_re", "o_c_im",
            "o_d", "o_w_out", "ca_norm", "ca_mem_norm", "ca_wq", "ca_wk", "ca_wv", "ca_wo", "ffn_norm", "ffn_w_gate",
            "ffn_w_up", "ffn_w_down", "final_norm")


def _pack_rows(arrs, width, dtype, row_mult=8):
    parts, spans, r0 = [], [], 0
    for a in arrs:
        flat = a.reshape(-1).astype(dtype)
        rows = -(-flat.shape[0] // (width * row_mult)) * row_mult
        if rows * width != flat.shape[0]:
            flat = jnp.pad(flat, (0, rows * width - flat.shape[0]))
        parts.append(flat.reshape(rows, width))
        spans.append((r0, rows))
        r0 += rows
    return jnp.concatenate(parts, axis=0), spans


def _unpack_rows(slab, spans, shapes):
    out = []
    for (r0, rows), shp in zip(spans, shapes):
        n = math.prod(shp)
        out.append(slab[r0:r0 + rows].reshape(-1)[:n].reshape(shp))
    return out


def _two_d(a):
    return a.reshape(-1, a.shape[-1])


def _shard_rows(n, a):
    return _two_d(jnp.swapaxes(a, -1, -2) if n in _TRANSPOSED else a)


def _from_shard_rows(n, rows, shape):
    if n in _TRANSPOSED:
        return jnp.swapaxes(rows.reshape(shape[:-2] + (shape[-1], shape[-2])), -1, -2)
    return rows.reshape(shape)


def _local_slab(local, slab, dtype):
    parts = sorted((r0, n, l) for n, (_, where) in _PLACE.items() for l, (s, r0) in enumerate(where) if s == slab)
    shards = [_shard_rows(n, local[n] if len(_PLACE[n][1]) == 1 else local[n][l]) for _, n, l in parts]
    return jnp.concatenate([a.astype(dtype) for a in shards], axis=0)


def _set_diag(b, pattern):
    return jnp.einsum(pattern, b, jnp.eye(C_GROUPS // N_SETS, dtype=b.dtype))


def _s5_discretize(lam_re, lam_im, log_dt, b_re, b_im):
    dt = jnp.exp(log_dt)[:, None]
    mag = jnp.exp(lam_re * dt)
    ar = mag * jnp.cos(lam_im * dt)
    ai = mag * jnp.sin(lam_im * dt)
    den = lam_re * lam_re + lam_im * lam_im
    qr = ((ar - 1.0) * lam_re + ai * lam_im) / den
    qi = (ai * lam_re - (ar - 1.0) * lam_im) / den
    bbr = qr[..., None] * b_re - qi[..., None] * b_im
    bbi = qr[..., None] * b_im + qi[..., None] * b_re
    return ar, ai, bbr, bbi


def _attention_block(x, mem, W, w, i, tag):
    xn, q = _norm_mm(x, w["ca_norm"][i], _shards(W, "ca_wq", i), split="k", out_dtype=BF16, name=f"{tag}_q")
    memn = _rms_fwd(mem, w["ca_mem_norm"][i], name=f"{tag}_ca_memnorm")
    k = _mm_k(memn, _shards(W, "ca_wk", i), out_dtype=BF16, name=f"{tag}_k")
    v = _mm_k(memn, _shards(W, "ca_wv", i), out_dtype=BF16, name=f"{tag}_v")
    o = _attn_fwd(q, k, v, name=f"{tag}_attn")
    y = _mm_k(o, _shards(W, "ca_wo", i), add=x, name=f"{tag}_wo")
    return y, (x, xn, memn, q, k, v, o)


def _attention_block_bwd(dy, saved, mem, W, w, i, tag, G, grads, token=None, mid=None):
    x, xn, memn, q, k, v, o = saved
    gain = w["ca_norm"][i]
    if token is not None:
        k = _behind(k, token)
    G = _grad_to_slab(G, "ca_wo", i, o, dy, a_cols=256, name=f"{tag}_dwo")
    dq, dk, dv = _attn_bwd(dy, _shards(W, "ca_wo", i), q, k, v, name=f"{tag}_attn_bwd")
    token = mid(dq) if mid is not None else None
    if token is not None:
        gain = _behind(gain, token)
    G = _grad_to_slab(G, "ca_wq", i, xn, dq, a_cols=256, name=f"{tag}_dwq")
    G = _grad_to_slab(G, "ca_wk", i, memn, dk, a_cols=256, name=f"{tag}_dwk")
    G = _grad_to_slab(G, "ca_wv", i, memn, dv, a_cols=256, name=f"{tag}_dwv")
    dmemn = _mm_k_t([(dk, _shards(W, "ca_wk", i)), (dv, _shards(W, "ca_wv", i))], name=f"{tag}_dmemn")
    dx, dg = _norm_bwd_k(dq, _shards(W, "ca_wq", i), x, gain, dy, name=f"{tag}_dq_norm_bwd")
    grads["ca_norm"][i] = dg[0]
    grads["ca_mem_norm"][i] = _rms_dg(mem, w["ca_mem_norm"][i], dmemn, name=f"{tag}_ca_memnorm_bwd")[0]
    return dx, G


def _ffn_block(x, W, w, i, tag, head=None):
    fn, gate, up, h = _ffn_up(x, w["ffn_norm"][i], _shards(W, "ffn_w_gate", i), _shards(W, "ffn_w_up", i),
                              name=f"{tag}_ffn_up")
    if head is None:
        y = _mm_k(h, _shards(W, "ffn_w_down", i), add=x, name=f"{tag}_down")
    else:
        y = _down_loss_head(h, _shards(W, "ffn_w_down", i), x, *head, name=f"{tag}_down_loss_head")
    return y, (x, fn, gate, up, h)


def _ffn_block_bwd(dy, saved, W, w, i, tag, G, grads, token=None, mid=None):
    x, fn, gate, up, h = saved
    gain = w["ffn_norm"][i]
    G = _grad_to_slab(G, "ffn_w_down", i, h, dy, name=f"{tag}_dwd")
    dg, du = _ffn_bwd_hidden(dy, _shards(W, "ffn_w_down", i), gate, up, token, name=f"{tag}_ffn_bwd_hidden")
    token = mid(dg) if mid is not None else None
    if token is not None:
        gain = _behind(gain, token)
    G = _grad_to_slab(G, "ffn_w_gate", i, dg, fn, name=f"{tag}_dwg")
    G = _grad_to_slab(G, "ffn_w_up", i, du, fn, name=f"{tag}_dwu")
    dx, dgn = _ffn_in_bwd(dg, du, _shards(W, "ffn_w_gate", i), _shards(W, "ffn_w_up", i), x, gain, dy,
                          name=f"{tag}_ffn_in_bwd")
    grads["ffn_norm"][i] = dgn[0]
    return dx, G


def _gmlp_mask():
    chunk = jnp.arange(GMLP_BLOCK) // CHUNK
    return chunk[None, :] <= chunk[:, None]


def _even_block(x, W, w, tag):
    hn, proj = _norm_mm(x, w["e_norm"][0], _shards(W, "e_w_in"), split="n", out_dtype=F32, name=f"{tag}_w_in")
    wm = jnp.where(_gmlp_mask()[None], w["e_gmlp_w"][0], 0.0).astype(BF16)
    bcol = w["e_gmlp_b"][0][:, :, None]
    cw = jnp.pad(w["e_conv_w"][0], ((0, CONV_HALO - CONV_WIDTH), (0, 0)))
    cb, lg, lb = w["e_conv_b"], w["e_conv_ln_g"], w["e_conv_ln_b"]
    mix, hc = _even_fwd(proj, wm, bcol, cw, cb, lg, lb, name=f"{tag}_mixers")
    y = _mm_k(mix, _shards(W, "e_w_out"), add=x, name=f"{tag}_w_out")
    return y, (x, hn, proj, mix, hc, wm, bcol, cw)


def _even_block_bwd(dy, saved, W, w, tag, G, grads):
    x, hn, proj, mix, hc, wm, bcol, cw = saved
    dmix = _mm_k_t([(dy, _shards(W, "e_w_out"))], name=f"{tag}_dmix")
    G = _grad_to_slab(G, "e_w_out", 0, mix, dy, a_cols=256, name=f"{tag}_dw_out")
    wmt = jnp.swapaxes(wm, 1, 2)
    dpa, dhc, dwm, db, dlg, dlb, dcb = _even_bwd1(proj, dmix, hc, wm, wmt, bcol, w["e_conv_ln_g"], w["e_conv_ln_b"],
                                                  name=f"{tag}_mixers_bwd1")
    dpb, dcw = _even_bwd2(proj, dhc, cw, name=f"{tag}_mixers_bwd2")
    grads["e_gmlp_w"] = jnp.where(_gmlp_mask()[None], dwm, 0.0)[None]
    grads["e_gmlp_b"] = db[:, :, 0][None]
    grads["e_conv_ln_g"], grads["e_conv_ln_b"], grads["e_conv_b"] = dlg, dlb, dcb
    grads["e_conv_w"] = dcw[:CONV_WIDTH][None]
    G = _grad_to_slab(G, "e_w_in", 0, hn, dpa, b_cols=512, chips=(0, 2), name=f"{tag}_dw_in_a")
    G = _grad_to_slab(G, "e_w_in", 0, hn, dpb, b_cols=512, chips=(2, 2), name=f"{tag}_dw_in_b")
    dx, dg = _norm_bwd_n((dpa, dpb), _shards(W, "e_w_in"), x, w["e_norm"][0], dy, name=f"{tag}_in_bwd")
    grads["e_norm"] = dg
    return dx, G


def _odd_block(x, W, w, tag):
    S = x.shape[0]
    hn, u = _norm_mm(x, w["o_norm"][0], _shards(W, "o_w_in"), split="k", out_dtype=F32, name=f"{tag}_w_in")
    disc_in = (w["o_lam_re"][0], w["o_lam_im"][0], w["o_log_dt"][0], w["o_b_re"][0], w["o_b_im"][0])
    (ar, ai, bbr, bbi), disc_vjp = jax.vjp(_s5_discretize, *disc_in)
    sets = (N_SETS, C_GROUPS // N_SETS)
    per_set = N_STATE // N_SETS
    bset = jnp.concatenate([_set_diag(b.reshape(sets + b.shape[1:]), "jgpc,gh->jgchp").reshape(N_SETS, SET_CH, per_set)
                            for b in (bbr, bbi)], axis=2).astype(BF16)
    cset = jnp.concatenate([_set_diag(c.reshape(sets + c.shape[1:]), "jgcp,gh->jgphc").reshape(N_SETS, per_set, SET_CH)
                            for c in (w["o_c_re"][0], -w["o_c_im"][0])], axis=1).astype(BF16)
    powers, pr, pi = [], ar, ai
    for _ in range(SCAN_BLOCK):
        powers.append(jnp.concatenate([pr.reshape(STATE_ROWS, STATE_LANES), pi.reshape(STATE_ROWS, STATE_LANES)], 0))
        pr, pi = pr * ar - pi * ai, pr * ai + pi * ar
    pw = jnp.stack(powers, axis=0)
    xs = _scan_fwd(u, bset, pw, name=f"{tag}_scan").reshape(S // 8, STATE_ROWS, 8, STATE_LANES)
    yv, yg = _s5_readout(xs, cset, u, w["o_d"], name=f"{tag}_readout")
    o, y = _glu_out(yg, _shards(W, "o_w_out"), x, name=f"{tag}_glu_out")
    return y, (x, hn, u, bset, cset, pw, xs, yv, yg, o, disc_vjp)


def _odd_block_bwd(dy, saved, W, w, tag, G, grads):
    x, hn, u, bset, cset, pw, xs, yv, yg, o, disc_vjp = saved
    S = x.shape[0]
    do, dys, dus, dd = _glu_out_bwd(o, dy, _shards(W, "o_w_out"), yv, u, w["o_d"], name=f"{tag}_glu_out_bwd")
    G = _grad_to_slab(G, "o_w_out", 0, yg, do, b_cols=512, name=f"{tag}_dw_out")
    grads["o_d"] = dd
    dcset_t = _state_grad_sets(dys, xs, name=f"{tag}_dcd")
    gs, da = _scan_bwd(dys, cset, xs.reshape(S * STATE_ROWS, STATE_LANES), pw, name=f"{tag}_scan_bwd")
    gs = gs.reshape(xs.shape)
    dbset = _state_grad_sets(u, gs, name=f"{tag}_dbd")
    du, dx, dg = _s5_in_bwd(gs, bset, dus, _shards(W, "o_w_in"), x, w["o_norm"][0], dy, name=f"{tag}_in_bwd")
    G = _grad_to_slab(G, "o_w_in", 0, hn, du, a_cols=256, name=f"{tag}_dw_in")
    grads["o_norm"] = dg
    per = C_GROUPS // N_SETS
    blocks = (N_SETS, per, C_GROUP_CH, 2, per, C_STATE)
    dc = _set_diag(dcset_t.reshape(blocks), "jhcrgp,gh->rjgcp").reshape(2, C_GROUPS, C_GROUP_CH, C_STATE)
    db = _set_diag(dbset.reshape(blocks), "jgcrhp,gh->rjgpc").reshape(2, C_GROUPS, C_STATE, C_GROUP_CH)
    dcr, dci, dbbr, dbbi = dc[0], -dc[1], db[0], db[1]
    dar = da[:STATE_ROWS].reshape(C_GROUPS, C_STATE)
    dai = da[STATE_ROWS:].reshape(C_GROUPS, C_STATE)
    dlr, dli, dldt, dbr, dbi = disc_vjp((dar, dai, dbbr, dbbi))
    grads["o_lam_re"], grads["o_lam_im"], grads["o_log_dt"] = dlr[None], dli[None], dldt[None]
    grads["o_b_re"], grads["o_b_im"], grads["o_c_re"], grads["o_c_im"] = dbr[None], dbi[None], dcr[None], dci[None]
    return dx, G


def _behind(value, token):
    return value + token[0, 0].astype(value.dtype)


class _NoExchange:
    def __init__(self, W):
        self.W = W

    def first_weights(self, w):
        return self.W, w

    def weights(self, stage, after):
        return {}

    def behind_late_start(self, w):
        return w

    def late_weights(self, after):
        return {}

    def grads_ready(self, piece, G):
        return None

    def grads_crossed(self, piece, after):
        return None


def _forward_backward(xs_, mems_, tgt, w, G, exchange):
    W, w = exchange.first_weights(w)
    x1, s_mix0 = _even_block(xs_, W, w, "l0")
    W = {**W, **exchange.weights(1, x1)}
    x2, s_att0 = _attention_block(x1, mems_, W, w, 0, "l0")
    W = {**W, **exchange.weights(2, x2)}
    x3, s_ffn0 = _ffn_block(x2, W, w, 0, "l0")
    W = {**W, **exchange.weights(3, x3)}
    w = exchange.behind_late_start(w)
    x4, s_mix1 = _odd_block(x3, W, w, "l1")
    W = {**W, **exchange.late_weights(x4)}
    x5, s_att1 = _attention_block(x4, mems_, W, w, 1, "l1")
    (dx, dfinal, loss_lanes), s_ffn1 = _ffn_block(x5, W, w, 1, "l1", head=(w["final_norm"], tgt))

    grads = {n: [None, None] for n in ("ca_norm", "ca_mem_norm", "ffn_norm")}
    grads["final_norm"] = dfinal[0]
    dx, G = _ffn_block_bwd(dx, s_ffn1, W, w, 1, "l1", G, grads)
    dx, G = _attention_block_bwd(dx, s_att1, mems_, W, w, 1, "l1", G, grads)
    dx, G = _odd_block_bwd(dx, s_mix1, W, w, "l1", G, grads)
    token = exchange.grads_ready("l1", G)
    dx, G = _ffn_block_bwd(dx, s_ffn0, W, w, 0, "l0", G, grads, token,
                           lambda after: exchange.grads_crossed("l1", after))
    token = exchange.grads_ready("ffn0", G)
    dx, G = _attention_block_bwd(dx, s_att0, mems_, W, w, 0, "l0", G, grads, token,
                                 lambda after: exchange.grads_crossed("ffn0", after))
    dx, G = _even_block_bwd(dx, s_mix0, W, w, "l0", G, grads)
    for n in list(grads):
        if isinstance(grads[n], list):
            grads[n] = jnp.stack(grads[n], axis=0)
        grads[n] = grads[n].reshape(w[n].shape)
    return loss_lanes, dx, G, grads


class _Exchange:
    def __init__(self, local, chip, core):
        self.bufs = {s: lax.dynamic_update_slice(lax.empty((N_CHIPS, rows, width), BF16),
                                                 _local_slab(local, s, BF16)[None], (chip, 0, 0))
                     for s, (width, rows) in _SLABS.items()}
        small = jnp.zeros((_SMALL_SLAB_ROWS, SMALL_W), F32)
        for n, (r0, rows) in _SMALL_PLACE.items():
            small = small.at[r0:r0 + rows].set(local[n].reshape(rows, SMALL_W))
        self.bufs[_SMALL_SLAB] = lax.dynamic_update_slice(lax.empty((N_CHIPS, _SMALL_SLAB_ROWS, SMALL_W), F32),
                                                          small[None], (chip, 0, 0))
        self.shard_shapes = {n: local[n].shape for n in _SMALL_PLACE}
        self.where = jnp.stack([chip, core]).astype(jnp.int32)
        self.reduces = {}

    def weights(self, stage, after):
        send_sems, recv_sems, flying = self.flight
        slabs = self.stage_slabs(stage)
        first = list(flying).index(slabs[0])
        bufs = _gather_ici_wait(send_sems, recv_sems, [flying[s] for s in slabs], first, after,
                                name=f"gather_stage{stage}_wait")
        now = [k for k, s in enumerate(slabs) if s not in _LATE_SLABS]
        late = [k for k, s in enumerate(slabs) if s in _LATE_SLABS]
        whole = _gather_forward([bufs[k] for k in now], name=f"gather_stage{stage}_forward")
        if late:
            *state, self.late_token = _gather_forward_start([bufs[k] for k in late], whole[0], name="gather_late_start")
            self.late = ([slabs[k] for k in late], *state)
        return dict(zip([slabs[k] for k in now], whole))

    def behind_late_start(self, w):
        return {**w, "o_norm": _behind(w["o_norm"], self.late_token)}

    def late_weights(self, after):
        slabs, send_sems, recv_sems, bufs = self.late
        return dict(zip(slabs, _gather_forward_wait(send_sems, recv_sems, bufs, after, name="gather_late_wait")))

    @staticmethod
    def stage_slabs(stage):
        return _STAGES[stage] + ((_SMALL_SLAB,) if stage == 0 else ())

    def first_weights(self, w):
        order = [s for k in range(len(_STAGES)) for s in self.stage_slabs(k)]
        send_sems, recv_sems, bufs, after = _gather_ici_start([self.bufs[s] for s in order], w["e_norm"],
                                                              name="gather_start")
        self.flight = (send_sems, recv_sems, dict(zip(order, bufs)))
        W = self.weights(0, after)
        w = {**w, "e_norm": _behind(w["e_norm"], after)}
        for (n, ax), (r0, rows) in zip(_SMALL_SHARDED, _SMALL_PLACE.values()):
            shards = [W[_SMALL_SLAB][p, r0:r0 + rows].reshape(self.shard_shapes[n]) for p in range(N_CHIPS)]
            w[n] = jnp.concatenate(shards, axis=ax)
        return W, w

    def pair_start(self, G, slabs, tag):
        send_sems, recv_sems, gl, lands, token = _pair_exchange_start([G[s] for s in slabs],
                                                                      name=f"grad_{tag}_pair_start")
        return (slabs, send_sems, recv_sems, gl, lands), token

    def pair_land(self, state, after, tag):
        slabs, send_sems, recv_sems, gl, lands = state
        gl, other = _pair_exchange_wait(send_sems, recv_sems, gl, lands, after, name=f"grad_{tag}_pair_wait")
        pairs = [_pair_sum(g, r, self.where, name=f"grad_pair_sum_{s}") for s, g, r in zip(slabs, gl, other)]
        send_sems, recv_sems, pairs, lands, token = _chip_exchange_start(pairs, name=f"grad_{tag}_chip_start")
        return (slabs, gl, other, send_sems, recv_sems, pairs, lands), token

    def reduce_sum(self, state, after, tag):
        slabs, gl, other, send_sems, recv_sems, pairs, lands = state
        slots = _chip_exchange_wait(send_sems, recv_sems, pairs, lands, after, name=f"grad_{tag}_chip_wait")
        return slabs, [_chip_sum(g, r, sl, self.where, name=f"grad_chip_sum_{s}")
                       for s, g, r, sl in zip(slabs, gl, other, slots)]

    @staticmethod
    def share_start(slabs, halves, tag):
        send_sems, recv_sems, halves, token = _pair_share_start(halves, name=f"grad_{tag}_share_start")
        return (slabs, send_sems, recv_sems, halves), token

    @staticmethod
    def share_finish(state, after, tag):
        slabs, send_sems, recv_sems, halves = state
        return dict(zip(slabs, _pair_share_wait(send_sems, recv_sems, halves, after, name=f"grad_{tag}_share_wait")))

    def grads_ready(self, piece, G):
        self.reduces[piece], token = self.pair_start(G, _GRAD_PIECES[piece], piece)
        return token

    def grads_crossed(self, piece, after):
        self.reduces[piece], token = self.pair_land(self.reduces[piece], after, piece)
        return token


def kernel(x, mem, e_norm, e_w_in, e_gmlp_w, e_gmlp_b, e_conv_w, e_conv_b, e_conv_ln_g, e_conv_ln_b, e_w_out, o_norm, o_w_in, o_lam_re, o_lam_im, o_log_dt, o_b_re, o_b_im, o_c_re, o_c_im, o_d, o_w_out, ca_norm, ca_mem_norm, ca_wq, ca_wk, ca_wv, ca_wo, ffn_norm, ffn_w_gate, ffn_w_up, ffn_w_down, final_norm, loss_target, m_e_norm, m_e_w_in, m_e_gmlp_w, m_e_gmlp_b, m_e_conv_w, m_e_conv_b, m_e_conv_ln_g, m_e_conv_ln_b, m_e_w_out, m_o_norm, m_o_w_in, m_o_lam_re, m_o_lam_im, m_o_log_dt, m_o_b_re, m_o_b_im, m_o_c_re, m_o_c_im, m_o_d, m_o_w_out, m_ca_norm, m_ca_mem_norm, m_ca_wq, m_ca_wk, m_ca_wv, m_ca_wo, m_ffn_norm, m_ffn_w_gate, m_ffn_w_up, m_ffn_w_down, m_final_norm, v_e_norm, v_e_w_in, v_e_gmlp_w, v_e_gmlp_b, v_e_conv_w, v_e_conv_b, v_e_conv_ln_g, v_e_conv_ln_b, v_e_w_out, v_o_norm, v_o_w_in, v_o_lam_re, v_o_lam_im, v_o_log_dt, v_o_b_re, v_o_b_im, v_o_c_re, v_o_c_im, v_o_d, v_o_w_out, v_ca_norm, v_ca_mem_norm, v_ca_wq, v_ca_wk, v_ca_wv, v_ca_wo, v_ffn_norm, v_ffn_w_gate, v_ffn_w_up, v_ffn_w_down, v_final_norm):
    args = dict(locals())
    local = {n: args[n] for n in _WEIGHTS}
    mom = {n: args["m_" + n] for n in _WEIGHTS}
    vel = {n: args["v_" + n] for n in _WEIGHTS}
    chip = 2 * lax.axis_index("x") + lax.axis_index("y")
    core = lax.axis_index("c")
    xs_, mems_, tgt = x[0], mem[0], loss_target[0]

    w = {n: local[n] for n in _REPLICATED}
    exchange = _Exchange(local, chip, core)
    G = {s: lax.empty((N_CHIPS, rows, width), F32) for s, (width, rows) in _SLABS.items()}
    loss_lanes, dx, G, grads = _forward_backward(xs_, mems_, tgt, w, G, exchange)

    gs_slab, gs_spans = _pack_rows([grads[n] for n in _SMALL] + [loss_lanes], SMALL_W, F32)
    rest0_token = exchange.grads_ready("rest0", G)
    small_flight = _all_to_all_start(gs_slab, rest0_token, name="small_grads_start")
    slabs_l1, halves_l1 = exchange.reduce_sum(exchange.reduces["l1"], small_flight[4], "l1")
    slabs_f0, halves_f0 = exchange.reduce_sum(exchange.reduces["ffn0"], small_flight[4], "ffn0")
    share, share_token = exchange.share_start(slabs_l1 + slabs_f0, halves_l1 + halves_f0, "l1_ffn0")
    token = exchange.grads_crossed("rest0", share_token)

    out_grads, delta, new_m, new_v = {}, {}, {}, {}

    def adamw_large(names):
        raw = []
        for n in names:
            shp = local[n].shape
            g_, d_, m_, v_ = _adamw_shard(_shard_rows(n, local[n]), [(gsum[s], r0) for s, r0 in _PLACE[n][1]],
                                          _shard_rows(n, mom[n]), _shard_rows(n, vel[n]), name=f"adamw_{n}")
            out_grads[n], delta[n], new_m[n], new_v[n] = (_from_shard_rows(n, t, shp) for t in (g_, d_, m_, v_))
            raw.append(d_)
        return raw

    gsum = exchange.share_finish(share, token, "l1_ffn0")
    ready = [n for n, (_, where) in _PLACE.items() if all(s in gsum for s, _ in where)]
    done = adamw_large(ready)

    gs_slab, gs_all = _all_to_all_wait(*small_flight[:4], done, name="small_grads_wait")
    gs_all = lax.dynamic_update_slice(gs_all, gs_slab[None], (2 * chip + core, 0, 0))
    gs_sum = _sum_slots(gs_all, name="small_grad_sum")
    *small_sums, loss_sum = _unpack_rows(gs_sum, gs_spans, [grads[n].shape for n in _SMALL] + [loss_lanes.shape])
    out_grads.update(zip(_SMALL, small_sums))
    for n, ax in _SMALL_SHARDED:
        width = local[n].shape[ax]
        out_grads[n] = lax.dynamic_slice_in_dim(out_grads[n], chip * width, width, axis=ax)
    d_, m_, v_ = _adamw_small([_two_d(local[n]) for n in _SMALL], [_two_d(out_grads[n]) for n in _SMALL],
                              [_two_d(mom[n]) for n in _SMALL], [_two_d(vel[n]) for n in _SMALL], name="adamw_small")
    for n, dd, mm_, vv in zip(_SMALL, d_, m_, v_):
        shp = local[n].shape
        delta[n], new_m[n], new_v[n] = dd.reshape(shp), mm_.reshape(shp), vv.reshape(shp)

    slabs_r0, halves_r0 = exchange.reduce_sum(exchange.reduces["rest0"], d_[0], "rest0")
    share, share_token = exchange.share_start(slabs_r0, halves_r0, "rest0")
    gsum = {**gsum, **exchange.share_finish(share, share_token, "rest0")}
    adamw_large([n for n in _PLACE if n not in ready])

    return (loss_sum[0, 0], dx[None], *[out_grads[n] for n in _WEIGHTS], *[delta[n] for n in _WEIGHTS],
            *[new_m[n] for n in _WEIGHTS], *[new_v[n] for n in _WEIGHTS])
```

```python
import functools
import math

import jax
import jax.numpy as jnp
from jax import lax
from jax.experimental import pallas as pl
from jax.experimental.pallas import tpu as pltpu

F32 = jnp.float32
BF16 = jnp.bfloat16
MESH = pl.DeviceIdType.MESH

EPS = 1e-6
D_MODEL = 1024
A_WIDTH = 512
A_GROUPS = 4
GMLP_BLOCK = 128
CHUNK = 64
B_WIDTH = 512
CONV_WIDTH = 31
CONV_HALO = 32
C_WIDTH = 512
C_GROUP_CH = 16
C_GROUPS = 32
C_STATE = 64
N_STATE = C_GROUPS * C_STATE
STATE_LANES = 128
STATE_ROWS = N_STATE // STATE_LANES
SCAN_BLOCK = 8
CA_HEADS = 4
CA_HEAD_DIM = 256
FFN_HIDDEN = 2816

ADAM_LR = 0.001
ADAM_B1 = 0.9
ADAM_B2 = 0.999
ADAM_EPS = 1e-08
ADAM_WD = 0.01
ADAM_STEP = 10

VMEM_LIMIT = 56 * 1024 * 1024
ACC_BYTES = 6 * 1024 * 1024
TN_VMEM_BYTES = 44 * 1024 * 1024
SMALL_W = 128
N_CHIPS = 4
N_DEV = 8

_SLABS = {"D0": (512, 1024), "E0": (1024, 256), "A0": (1024, 1024), "B0": (1024, 704), "C0": (1024, 1408),
          "D1": (512, 768), "A1": (1024, 1024), "B1": (1024, 704), "C1": (1024, 1408)}
_STAGES = (("D0", "E0"), ("A0",), ("B0", "C0"), ("D1", "A1", "B1", "C1"))
_LATE_SLABS = ("A1", "B1", "C1")
_GRAD_PIECES = {"l1": _STAGES[3], "ffn0": _STAGES[2], "rest0": _STAGES[0] + _STAGES[1]}
_PLACE = {
    "e_w_in": (1024, (("D0", 0),)), "e_w_out": (256, (("E0", 0),)),
    "o_w_out": (512, (("D1", 0),)), "o_w_in": (256, (("D1", 512),)),
    "ca_wq": (256, (("A0", 0), ("A1", 0))), "ca_wk": (256, (("A0", 256), ("A1", 256))),
    "ca_wv": (256, (("A0", 512), ("A1", 512))), "ca_wo": (256, (("A0", 768), ("A1", 768))),
    "ffn_w_down": (704, (("B0", 0), ("B1", 0))),
    "ffn_w_gate": (704, (("C0", 0), ("C1", 0))), "ffn_w_up": (704, (("C0", 704), ("C1", 704))),
}
_SMALL_SLAB = "F0"
_SMALL_SLAB_ROWS = 48
_SMALL_PLACE = {"e_conv_w": (0, 31), "o_norm": (32, 2), "o_d": (34, 1)}
_TRANSPOSED = ("ffn_w_gate", "ffn_w_up")


def _params(sem=None):
    return pltpu.CompilerParams(dimension_semantics=sem, vmem_limit_bytes=VMEM_LIMIT)


def _tile(n, pref, mult=128):
    if n <= pref:
        return n
    t = (pref // mult) * mult
    while t >= mult:
        if n % t == 0:
            return t
        t -= mult
    return n


def _blk(name, layer=0):
    rows, where = _PLACE[name]
    slab, r0 = where[layer]
    assert r0 % rows == 0
    return slab, rows, r0 // rows


def _shards(slabs, name, layer=0):
    slab, rows, b = _blk(name, layer)
    return [(slabs[slab], (None, rows, _SLABS[slab][0]), (p, b, 0)) for p in range(N_CHIPS)]


_GELU_C = 0.7978845608028654
_GELU_A = 0.044715


def _gelu(x):
    t = jnp.tanh(_GELU_C * (x + _GELU_A * (x * x * x)))
    return 0.5 * x * (1.0 + t), t


def _gelu_grad(x, t):
    return 0.5 * (1.0 + t) + 0.5 * x * (1.0 - t * t) * (_GELU_C * (1.0 + 3.0 * _GELU_A * x * x))


def _sigmoid(x):
    return 1.0 / (1.0 + jnp.exp(-x))


def _mean(x):
    return jnp.mean(x, axis=-1, keepdims=True)


def _dot(a, b):
    return jnp.dot(a, b, preferred_element_type=F32)


def _dot_nt(a, b):
    return lax.dot_general(a, b, (((1,), (1,)), ((), ())), preferred_element_type=F32)


def _dot_tn(a, b):
    return lax.dot_general(a, b, (((0,), (0,)), ((), ())), preferred_element_type=F32)


def _rms_tile(xv, gv):
    return (xv * lax.rsqrt(_mean(xv * xv) + EPS)) * gv


def _rms_bwd_tile(xv, gv, dyv):
    r = lax.rsqrt(_mean(xv * xv) + EPS)
    xh = xv * r
    dyg = dyv * gv
    return r * (dyg - xh * _mean(dyg * xh)), jnp.sum(dyv * xh, axis=0, keepdims=True)


def _cols(p, width):
    return slice(p * width, (p + 1) * width)


def _sum_k(a, ws, k):
    tot = None
    for p in range(N_CHIPS):
        y = _dot(a[:, _cols(p, k)], ws[p][...])
        tot = y if tot is None else tot + y
    return tot


def _cat_nt(a, ws):
    return jnp.concatenate([_dot_nt(a, ws[p][...]) for p in range(N_CHIPS)], axis=1)


def _rows_call(name, tm, rows, fulls, outs, accs, body, scratch=()):
    S = min(x.shape[-2] for x in rows if x.ndim != 4)
    nr, nf, no, na = len(rows), len(fulls), len(outs), len(accs)

    def kern(*refs):
        r, f = refs[:nr], refs[nr:nr + nf]
        o, a = refs[nr + nf:nr + nf + no], refs[nr + nf + no:nr + nf + no + na]
        if na:
            @pl.when(pl.program_id(0) == 0)
            def _():
                for ref in a:
                    ref[...] = jnp.zeros_like(ref)
        body(r, f, o, a, refs[nr + nf + no + na:])

    def whole(shape):
        nd = len(shape)
        return pl.BlockSpec(tuple(shape), lambda i: (0,) * nd)

    def row_spec(shape):
        if len(shape) == 4:
            return pl.BlockSpec((tm // 8,) + tuple(shape[1:]), lambda i: (i, 0, 0, 0))
        if len(shape) == 3:
            return pl.BlockSpec((shape[0], tm, shape[2]), lambda i: (0, i, 0))
        return pl.BlockSpec((tm, shape[1]), lambda i: (i, 0))

    def full_spec(x):
        if isinstance(x, tuple):
            _, bshape, bidx = x
            return pl.BlockSpec(bshape, lambda i: bidx, pipeline_mode=pl.Buffered(1))
        return whole(x.shape)

    out_shapes = [(S, o[0]) if len(o) == 2 else (o[0], S, o[1]) for o in outs]
    res = pl.pallas_call(
        kern, name=name, grid=(S // tm,),
        in_specs=[row_spec(x.shape) for x in rows] + [full_spec(x) for x in fulls],
        out_specs=[row_spec(s) for s in out_shapes] + [whole(shp) for shp, _ in accs],
        out_shape=[jax.ShapeDtypeStruct(s, o[-1]) for s, o in zip(out_shapes, outs)]
        + [jax.ShapeDtypeStruct(tuple(shp), dt) for shp, dt in accs],
        scratch_shapes=list(scratch),
        compiler_params=_params(("arbitrary",) if na else ("parallel",)),
    )(*rows, *[x[0] if isinstance(x, tuple) else x for x in fulls])
    return res[:no], res[no:]


def _grad_to_slab(gslabs, wname, layer, a, b, *, a_cols=None, b_cols=None, chips=(0, N_CHIPS), name):
    slab, rows, bidx = _blk(wname, layer)
    width = _SLABS[slab][0]
    p0, n_p = chips
    assert p0 % n_p == 0
    S = a.shape[-2]

    def tile_bytes(x, ts):
        return ts * x.dtype.itemsize * (x.shape[2] * n_p if x.ndim == 3 else x.shape[1])

    acc_bytes = n_p * rows * (-(-width // 128) * 128) * 4
    ts = next(t for t in (2048, 1024, 512, 256, S) if S % t == 0
              and 2 * (tile_bytes(a, t) + tile_bytes(b, t) + acc_bytes) <= TN_VMEM_BYTES or t == S)

    def operand(x):
        if x.ndim == 3:
            return pl.BlockSpec((n_p, ts, x.shape[2]), lambda s: (p0 // n_p, s, 0))
        return pl.BlockSpec((ts, x.shape[1]), lambda s: (s, 0))

    def part(ref, cols, p):
        if len(ref.shape) == 3:
            return ref[p]
        return ref[...] if cols is None else ref[:, _cols(p, cols)]

    def body(a_ref, b_ref, slab_ref, o_ref):
        @pl.when(pl.program_id(0) == 0)
        def _():
            o_ref[...] = jnp.zeros_like(o_ref)

        for p in range(n_p):
            o_ref[p] += _dot_tn(part(a_ref, a_cols, p).astype(BF16), part(b_ref, b_cols, p).astype(BF16))

    g = gslabs[slab]
    out = pl.pallas_call(
        body, name=name, grid=(S // ts,),
        in_specs=[operand(a), operand(b), pl.BlockSpec(memory_space=pl.ANY)],
        out_specs=pl.BlockSpec((n_p, rows, width), lambda s: (p0 // n_p, bidx, 0)),
        out_shape=jax.ShapeDtypeStruct(g.shape, F32), input_output_aliases={2: 0},
        compiler_params=_params(("arbitrary",)),
    )(a, b, g)
    return {**gslabs, slab: out}


def _vec(g):
    return g.reshape(1, -1)


def _norm_mm(x, g, ws, *, split, out_dtype, name, tm=512):
    S, D = x.shape
    k, n = ws[0][1][1], ws[0][1][2]
    N = n if split == "k" else N_CHIPS * n

    def body(r, f, o, acc, s):
        xn = _rms_tile(r[0][...], f[0][...]).astype(BF16)
        o[0][...] = xn
        if split == "k":
            o[1][...] = _sum_k(xn, f[1:], k).astype(out_dtype)
        else:
            for p in range(N_CHIPS):
                o[1][:, _cols(p, n)] = _dot(xn, f[1 + p][...]).astype(out_dtype)

    (xn, y), _ = _rows_call(name, _tile(S, tm), [x], [_vec(g)] + ws, [(D, BF16), (N, out_dtype)], [], body)
    return xn, y


def _mm_k(a, ws, *, add=None, out_dtype=F32, name, tm=512):
    S = a.shape[-2]
    k, n = ws[0][1][1], ws[0][1][2]
    has_add = add is not None

    def body(r, f, o, acc, s):
        if a.ndim == 3:
            y = None
            for p in range(N_CHIPS):
                t = _dot(r[0][p].astype(BF16), f[p][...])
                y = t if y is None else y + t
        else:
            y = _sum_k(r[0][...].astype(BF16), f, k)
        if has_add:
            y = y + r[1][...]
        o[0][...] = y.astype(out_dtype)

    (y,), _ = _rows_call(name, _tile(S, tm), [a] + ([add] if has_add else []), ws, [(n, out_dtype)], [], body)
    return y


def _mm_k_t(terms, *, out_dtype=F32, name, tm=512):
    S = terms[0][0].shape[0]
    k = terms[0][1][0][1][1]

    def body(r, f, o, acc, s):
        y = None
        for t in range(len(terms)):
            yt = _cat_nt(r[t][...].astype(BF16), f[N_CHIPS * t:N_CHIPS * (t + 1)])
            y = yt if y is None else y + yt
        o[0][...] = y.astype(out_dtype)

    (y,), _ = _rows_call(name, _tile(S, tm), [a for a, _ in terms], [w for _, ws in terms for w in ws],
                         [(N_CHIPS * k, out_dtype)], [], body)
    return y


def _rms_fwd(x, g, *, name):
    def body(r, f, o, acc, s):
        o[0][...] = _rms_tile(r[0][...], f[0][...]).astype(BF16)

    (y,), _ = _rows_call(name, _tile(x.shape[0], 256, 8), [x], [_vec(g)], [(x.shape[1], BF16)], [], body)
    return y


def _rms_dg(x, g, dy, *, name):
    def body(r, f, o, acc, s):
        acc[0][...] += _rms_bwd_tile(r[0][...], f[0][...], r[1][...])[1]

    _, (dg,) = _rows_call(name, _tile(x.shape[0], 256, 8), [x, dy], [_vec(g)], [], [((1, x.shape[1]), F32)], body)
    return dg


def _ffn_up(x, g, wg, wu, *, name, tm=512):
    S, D = x.shape
    h = wg[0][1][1]

    def body(r, f, o, acc, s):
        xn = _rms_tile(r[0][...], f[0][...]).astype(BF16)
        o[0][...] = xn
        for p in range(N_CHIPS):
            gate = _dot_nt(xn, f[1 + p][...])
            up = _dot_nt(xn, f[1 + N_CHIPS + p][...])
            o[1][p] = gate.astype(BF16)
            o[2][p] = up.astype(BF16)
            o[3][p] = (gate * _sigmoid(gate) * up).astype(BF16)

    (xn, gate, up, hid), _ = _rows_call(name, _tile(S, tm), [x], [_vec(g)] + wg + wu,
                                        [(D, BF16), (N_CHIPS, h, BF16), (N_CHIPS, h, BF16), (N_CHIPS, h, BF16)], [],
                                        body)
    return xn, gate, up, hid


def _ffn_bwd_hidden(dy, wd, gate, up, token=None, *, name, tm=512):
    S = dy.shape[0]
    h = wd[0][1][1]

    def body(r, f, o, acc, s):
        dyv = r[0][...]
        if token is not None:
            dyv = dyv + jnp.sum(f[N_CHIPS][...])
        dyb = dyv.astype(BF16)
        for p in range(N_CHIPS):
            dh = _dot_nt(dyb, f[p][...])
            gv = r[1][p].astype(F32)
            sg = _sigmoid(gv)
            o[0][p] = (dh * r[2][p].astype(F32) * (sg * (1.0 + gv * (1.0 - sg)))).astype(BF16)
            o[1][p] = (dh * gv * sg).astype(BF16)

    (dg, du), _ = _rows_call(name, _tile(S, tm), [dy, gate, up], wd + ([] if token is None else [token]),
                             [(N_CHIPS, h, BF16), (N_CHIPS, h, BF16)], [], body)
    return dg, du


def _ffn_in_bwd(dg, du, wg, wu, x, g, dres, *, name, tm=512):
    S, D = x.shape

    def body(r, f, o, acc, s):
        tot = None
        for p in range(N_CHIPS):
            y = _dot(r[0][p], f[1 + p][...]) + _dot(r[1][p], f[1 + N_CHIPS + p][...])
            tot = y if tot is None else tot + y
        dx, dgn = _rms_bwd_tile(r[2][...], f[0][...], tot)
        o[0][...] = dx + r[3][...]
        acc[0][...] += dgn

    (dx,), (dgn,) = _rows_call(name, _tile(S, tm), [dg, du, x, dres], [_vec(g)] + wg + wu, [(D, F32)],
                               [((1, D), F32)], body)
    return dx, dgn


def _norm_bwd_k(da, ws, x, g, dres, *, name, tm=512):
    S, D = x.shape

    def body(r, f, o, acc, s):
        dx, dg = _rms_bwd_tile(r[1][...], f[0][...], _cat_nt(r[0][...].astype(BF16), f[1:]))
        o[0][...] = dx + r[2][...]
        acc[0][...] += dg

    (dx,), (dg,) = _rows_call(name, _tile(S, tm), [da, x, dres], [_vec(g)] + ws, [(D, F32)], [((1, D), F32)], body)
    return dx, dg


def _norm_bwd_n(das, ws, x, g, dres, *, name, tm=512):
    S, D = x.shape
    n = ws[0][1][2]

    def body(r, f, o, acc, s):
        tot = None
        for p in range(N_CHIPS):
            y = _dot_nt(r[p // 2][:, _cols(p % 2, n)], f[1 + p][...])
            tot = y if tot is None else tot + y
        dx, dg = _rms_bwd_tile(r[2][...], f[0][...], tot)
        o[0][...] = dx + r[3][...]
        acc[0][...] += dg

    (dx,), (dg,) = _rows_call(name, _tile(S, tm), list(das) + [x, dres], [_vec(g)] + ws, [(D, F32)], [((1, D), F32)],
                              body)
    return dx, dg


def _ln_stats(v):
    mu = _mean(v)
    xc = v - mu
    rstd = lax.rsqrt(_mean(xc * xc) + EPS)
    return xc * rstd, rstd


_SHIFTS = 8
_CONV_ROWS = 64


def _fill_shifts(sh_ref, ext_ref, tm):
    sh_ref[0] = ext_ref[...]
    for s in range(1, _SHIFTS):
        sh_ref[s, 0:tm + CONV_HALO - _SHIFTS, :] = ext_ref[pl.ds(s, tm + CONV_HALO - _SHIFTS), :]


def _window(sh_ref, off, tm):
    return sh_ref[off % _SHIFTS, pl.ds(off - off % _SHIFTS, tm), :]


def _even_fwd(proj, wm, bcol, cw, cb, lg, lb, *, name):
    S = proj.shape[0]
    tm = _tile(S, 512)
    hb = tm // CONV_HALO
    nblk = tm // GMLP_BLOCK

    def body(p_ref, halo_ref, wm_ref, b_ref, cw_ref, cb_ref, lg_ref, lb_ref, mix_ref, hc_ref, hext_ref, hsh_ref):
        i = pl.program_id(0)
        gu, _ = _gelu(p_ref[:, 0:A_WIDTH])
        gv, _ = _gelu(p_ref[:, A_WIDTH:2 * A_WIDTH])
        vn, _ = _ln_stats(gv)
        vnb = vn.astype(BF16)
        for n in range(nblk):
            rows = slice(n * GMLP_BLOCK, (n + 1) * GMLP_BLOCK)
            for g in range(A_GROUPS):
                cols = slice(g * GMLP_BLOCK, (g + 1) * GMLP_BLOCK)
                sg = jnp.dot(wm_ref[g], vnb[rows, cols], preferred_element_type=F32) + b_ref[g]
                mix_ref[rows, cols] = (gu[rows, cols] * sg).astype(BF16)
        h = p_ref[:, 1024:1536] * _sigmoid(p_ref[:, 1536:2048])
        hh = halo_ref[:, 0:B_WIDTH] * _sigmoid(halo_ref[:, B_WIDTH:2 * B_WIDTH])
        hext_ref[0:CONV_HALO, :] = jnp.where(i > 0, hh, 0.0)
        hext_ref[CONV_HALO:CONV_HALO + tm, :] = h
        _fill_shifts(hsh_ref, hext_ref, tm)
        for r0 in range(0, tm, _CONV_ROWS):
            acc = jnp.zeros((_CONV_ROWS, B_WIDTH), F32)
            for k in range(CONV_WIDTH):
                acc = acc + cw_ref[k:k + 1, :] * _window(hsh_ref, r0 + k + CONV_HALO - CONV_WIDTH + 1, _CONV_ROWS)
            hc_ref[r0:r0 + _CONV_ROWS, :] = acc + cb_ref[...]
        hc = hc_ref[...]
        hhat, _ = _ln_stats(hc)
        hl = hhat * lg_ref[...] + lb_ref[...]
        mix_ref[:, A_WIDTH:A_WIDTH + B_WIDTH] = (hl * _sigmoid(hl)).astype(BF16)

    vec = pl.BlockSpec((1, B_WIDTH), lambda i: (0, 0))
    return pl.pallas_call(
        body, name=name, grid=(S // tm,),
        in_specs=[
            pl.BlockSpec((tm, 2048), lambda i: (i, 0)),
            pl.BlockSpec((CONV_HALO, 1024), lambda i: (jnp.maximum(i * hb - 1, 0), 1)),
            pl.BlockSpec((A_GROUPS, GMLP_BLOCK, GMLP_BLOCK), lambda i: (0, 0, 0)),
            pl.BlockSpec((A_GROUPS, GMLP_BLOCK, 1), lambda i: (0, 0, 0)),
            pl.BlockSpec((CONV_HALO, B_WIDTH), lambda i: (0, 0)),
            vec, vec, vec,
        ],
        out_specs=[pl.BlockSpec((tm, 1024), lambda i: (i, 0)), pl.BlockSpec((tm, B_WIDTH), lambda i: (i, 0))],
        out_shape=[jax.ShapeDtypeStruct((S, 1024), BF16), jax.ShapeDtypeStruct((S, B_WIDTH), F32)],
        scratch_shapes=[pltpu.VMEM((tm + CONV_HALO, B_WIDTH), F32),
                        pltpu.VMEM((_SHIFTS, tm + CONV_HALO, B_WIDTH), F32)],
        compiler_params=_params(("parallel",)),
    )(proj, proj, wm, bcol, cw, cb, lg, lb)


def _even_bwd1(proj, dmix, hc, wm, wmt, bcol, lg, lb, *, name):
    S = proj.shape[0]
    tm = _tile(S, 512)
    nblk = tm // GMLP_BLOCK

    def body(p_ref, dm_ref, hc_ref, wm_ref, wmt_ref, b_ref, lg_ref, lb_ref,
             dpa_ref, dhc_ref, dwm_ref, db_ref, dlg_ref, dlb_ref, dcb_ref, dgu_ref, dvn_ref):
        @pl.when(pl.program_id(0) == 0)
        def _():
            dwm_ref[...] = jnp.zeros_like(dwm_ref)
            db_ref[...] = jnp.zeros_like(db_ref)
            dlg_ref[...] = jnp.zeros_like(dlg_ref)
            dlb_ref[...] = jnp.zeros_like(dlb_ref)
            dcb_ref[...] = jnp.zeros_like(dcb_ref)

        au = p_ref[:, 0:A_WIDTH]
        av = p_ref[:, A_WIDTH:2 * A_WIDTH]
        gu, tu = _gelu(au)
        gv, tv = _gelu(av)
        vn, rstd = _ln_stats(gv)
        vnb = vn.astype(BF16)
        for n in range(nblk):
            rows = slice(n * GMLP_BLOCK, (n + 1) * GMLP_BLOCK)
            for g in range(A_GROUPS):
                cols = slice(g * GMLP_BLOCK, (g + 1) * GMLP_BLOCK)
                vb = vnb[rows, cols]
                sg = jnp.dot(wm_ref[g], vb, preferred_element_type=F32) + b_ref[g]
                da = dm_ref[rows, cols]
                dsg = da * gu[rows, cols]
                dgu_ref[rows, cols] = da * sg
                dsgb = dsg.astype(BF16)
                dwm_ref[g] += _dot_nt(dsgb, vb)
                db_ref[g] += jnp.sum(dsg, axis=1, keepdims=True)
                dvn_ref[rows, cols] = jnp.dot(wmt_ref[g], dsgb, preferred_element_type=F32)
        dvn = dvn_ref[...]
        dgv = rstd * (dvn - _mean(dvn) - vn * _mean(dvn * vn))
        dpa_ref[:, 0:A_WIDTH] = (dgu_ref[...] * _gelu_grad(au, tu)).astype(BF16)
        dpa_ref[:, A_WIDTH:2 * A_WIDTH] = (dgv * _gelu_grad(av, tv)).astype(BF16)
        hhat, rstd2 = _ln_stats(hc_ref[...])
        lgv = lg_ref[...]
        hl = hhat * lgv + lb_ref[...]
        s = _sigmoid(hl)
        dhl = dm_ref[:, A_WIDTH:A_WIDTH + B_WIDTH] * (s * (1.0 + hl * (1.0 - s)))
        dlg_ref[...] += jnp.sum(dhl * hhat, axis=0, keepdims=True)
        dlb_ref[...] += jnp.sum(dhl, axis=0, keepdims=True)
        dhh = dhl * lgv
        dhc = rstd2 * (dhh - _mean(dhh) - hhat * _mean(dhh * hhat))
        dcb_ref[...] += jnp.sum(dhc, axis=0, keepdims=True)
        dhc_ref[...] = dhc

    vec = pl.BlockSpec((1, B_WIDTH), lambda i: (0, 0))
    w3 = pl.BlockSpec((A_GROUPS, GMLP_BLOCK, GMLP_BLOCK), lambda i: (0, 0, 0))
    b3 = pl.BlockSpec((A_GROUPS, GMLP_BLOCK, 1), lambda i: (0, 0, 0))
    return pl.pallas_call(
        body, name=name, grid=(S // tm,),
        in_specs=[
            pl.BlockSpec((tm, 1024), lambda i: (i, 0)),
            pl.BlockSpec((tm, 1024), lambda i: (i, 0)),
            pl.BlockSpec((tm, B_WIDTH), lambda i: (i, 0)),
            w3, w3, b3, vec, vec,
        ],
        out_specs=[pl.BlockSpec((tm, 1024), lambda i: (i, 0)), pl.BlockSpec((tm, B_WIDTH), lambda i: (i, 0)),
                   w3, b3, vec, vec, vec],
        out_shape=[
            jax.ShapeDtypeStruct((S, 1024), BF16), jax.ShapeDtypeStruct((S, B_WIDTH), F32),
            jax.ShapeDtypeStruct((A_GROUPS, GMLP_BLOCK, GMLP_BLOCK), F32),
            jax.ShapeDtypeStruct((A_GROUPS, GMLP_BLOCK, 1), F32),
            jax.ShapeDtypeStruct((1, B_WIDTH), F32), jax.ShapeDtypeStruct((1, B_WIDTH), F32),
            jax.ShapeDtypeStruct((1, B_WIDTH), F32),
        ],
        scratch_shapes=[pltpu.VMEM((tm, A_WIDTH), F32), pltpu.VMEM((tm, A_WIDTH), F32)],
        compiler_params=_params(("arbitrary",)),
    )(proj, dmix, hc, wm, wmt, bcol, lg, lb)


def _even_bwd2(proj, dhc, cw, *, name):
    S = proj.shape[0]
    tm = _tile(S, 512)
    hb = tm // CONV_HALO
    nt = S // tm
    last_halo = S // CONV_HALO - 1
    lo = CONV_HALO - CONV_WIDTH + 1

    def body(p_ref, halo_ref, d_ref, dnext_ref, cw_ref, dpb_ref, dcw_ref, hext_ref, dext_ref, hsh_ref, dsh_ref):
        i = pl.program_id(0)

        @pl.when(i == 0)
        def _():
            dcw_ref[...] = jnp.zeros_like(dcw_ref)

        hh = halo_ref[:, 0:B_WIDTH] * _sigmoid(halo_ref[:, B_WIDTH:2 * B_WIDTH])
        hext_ref[0:CONV_HALO, :] = jnp.where(i > 0, hh, 0.0)
        hext_ref[CONV_HALO:CONV_HALO + tm, :] = p_ref[:, 0:B_WIDTH] * _sigmoid(p_ref[:, B_WIDTH:2 * B_WIDTH])
        dext_ref[0:tm, :] = d_ref[...]
        dext_ref[tm:tm + CONV_HALO, :] = jnp.where(i < nt - 1, dnext_ref[...], 0.0)
        _fill_shifts(hsh_ref, hext_ref, tm)
        _fill_shifts(dsh_ref, dext_ref, tm)
        for r0 in range(0, tm, _CONV_ROWS):
            rows = slice(r0, r0 + _CONV_ROWS)
            dhc_b = d_ref[rows, :]
            dh = jnp.zeros((_CONV_ROWS, B_WIDTH), F32)
            for k in range(CONV_WIDTH):
                dh = dh + cw_ref[k:k + 1, :] * _window(dsh_ref, r0 + CONV_WIDTH - 1 - k, _CONV_ROWS)
                dcw_ref[k:k + 1, :] += jnp.sum(dhc_b * _window(hsh_ref, r0 + k + lo, _CONV_ROWS), axis=0,
                                               keepdims=True)
            ba_b = p_ref[rows, 0:B_WIDTH]
            sg_b = _sigmoid(p_ref[rows, B_WIDTH:2 * B_WIDTH])
            dpb_ref[rows, 0:B_WIDTH] = (dh * sg_b).astype(BF16)
            dpb_ref[rows, B_WIDTH:2 * B_WIDTH] = (dh * ba_b * sg_b * (1.0 - sg_b)).astype(BF16)

    return pl.pallas_call(
        body, name=name, grid=(nt,),
        in_specs=[
            pl.BlockSpec((tm, 1024), lambda i: (i, 1)),
            pl.BlockSpec((CONV_HALO, 1024), lambda i: (jnp.maximum(i * hb - 1, 0), 1)),
            pl.BlockSpec((tm, B_WIDTH), lambda i: (i, 0)),
            pl.BlockSpec((CONV_HALO, B_WIDTH), lambda i: (jnp.minimum((i + 1) * hb, last_halo), 0)),
            pl.BlockSpec((CONV_HALO, B_WIDTH), lambda i: (0, 0)),
        ],
        out_specs=[pl.BlockSpec((tm, 1024), lambda i: (i, 0)), pl.BlockSpec((CONV_HALO, B_WIDTH), lambda i: (0, 0))],
        out_shape=[jax.ShapeDtypeStruct((S, 1024), BF16), jax.ShapeDtypeStruct((CONV_HALO, B_WIDTH), F32)],
        scratch_shapes=[pltpu.VMEM((tm + CONV_HALO, B_WIDTH), F32), pltpu.VMEM((tm + CONV_HALO, B_WIDTH), F32),
                        pltpu.VMEM((_SHIFTS, tm + CONV_HALO, B_WIDTH), F32),
                        pltpu.VMEM((_SHIFTS, tm + CONV_HALO, B_WIDTH), F32)],
        compiler_params=_params(("arbitrary",)),
    )(proj, proj, dhc, dhc, cw)


_CA_SCALE = CA_HEAD_DIM ** -0.5


def _softmax_rows(s):
    e = jnp.exp(s - jnp.max(s, axis=-1, keepdims=True))
    return e / jnp.sum(e, axis=-1, keepdims=True)


def _attn_fwd(q, k, v, *, name):
    S = q.shape[0]

    def body(r, f, o, acc, s):
        for h in range(CA_HEADS):
            cols = _cols(h, CA_HEAD_DIM)
            p = _softmax_rows(_dot_nt(r[0][:, cols], f[0][:, cols]) * _CA_SCALE)
            o[0][:, cols] = _dot(p.astype(BF16), f[1][:, cols]).astype(BF16)

    (o_,), _ = _rows_call(name, _tile(S, 512), [q], [k, v], [(D_MODEL, BF16)], [], body)
    return o_


def _attn_bwd(dy, wo, q, k, v, *, name):
    S = q.shape[0]
    M = k.shape[0]

    def body(r, f, o, acc, s):
        dyb = r[0][...].astype(BF16)
        for h in range(CA_HEADS):
            cols = _cols(h, CA_HEAD_DIM)
            qh = r[1][:, cols]
            kh = f[0][:, cols]
            vh = f[1][:, cols]
            doh = _dot_nt(dyb, f[2 + h][...]).astype(BF16)
            p = _softmax_rows(_dot_nt(qh, kh) * _CA_SCALE)
            acc[1][:, cols] += _dot_tn(p.astype(BF16), doh)
            dp = _dot_nt(doh, vh)
            ds = (p * (dp - jnp.sum(dp * p, axis=-1, keepdims=True)) * _CA_SCALE).astype(BF16)
            o[0][:, cols] = _dot(ds, kh).astype(BF16)
            acc[0][:, cols] += _dot_tn(ds, qh)

    (dq,), (dk, dv) = _rows_call(name, _tile(S, 512), [dy, q], [k, v] + wo, [(D_MODEL, BF16)],
                                 [((M, D_MODEL), F32), ((M, D_MODEL), F32)], body)
    return dq, dk, dv


_STATE_TILE = 2 * STATE_ROWS
N_SETS = 4
SET_CH = C_WIDTH // N_SETS
SET_COLS = N_STATE // N_SETS // STATE_LANES


def _set_groups(j):
    return [SET_COLS * j + c for c in range(SET_COLS)] + [STATE_ROWS + SET_COLS * j + c for c in range(SET_COLS)]


def _pack_state(re, im):
    hi = lax.bitcast_convert_type(re.astype(BF16).astype(F32), jnp.uint32)
    lo = lax.bitcast_convert_type(im.astype(BF16).astype(F32), jnp.uint32) >> 16
    return hi | lo


def _unpack_state(word):
    re = lax.bitcast_convert_type(word & jnp.uint32(0xFFFF0000), F32)
    im = lax.bitcast_convert_type(word << 16, F32)
    return re, im


def _state_set(ref, tm, j):
    parts = [_unpack_state(ref[:, SET_COLS * j + c, :, :].reshape(tm, STATE_LANES)) for c in range(SET_COLS)]
    return jnp.concatenate([p[0].astype(BF16) for p in parts] + [p[1].astype(BF16) for p in parts], axis=1)


def _s5_readout(xs, cset, u, d, *, name, tm=512):
    tm = _tile(u.shape[0], tm)

    def body(r, f, o, acc, s):
        y0 = jnp.concatenate([_dot(_state_set(r[0], tm, j), f[0][j]) for j in range(N_SETS)], axis=1)
        y = y0 + f[1][...] * r[1][...]
        o[0][...] = y
        o[1][...] = _gelu(y)[0].astype(BF16)

    (y, yg), _ = _rows_call(name, tm, [xs, u], [cset, d], [(C_WIDTH, F32), (C_WIDTH, BF16)], [], body)
    return y, yg


def _state_grad_sets(a, st, *, name, ts=256):
    ts = _tile(a.shape[0], ts)

    def body(r, f, o, acc, s):
        for j in range(N_SETS):
            acc[0][j] += _dot_tn(r[0][:, _cols(j, SET_CH)].astype(BF16), _state_set(r[1], ts, j))

    _, (out,) = _rows_call(name, ts, [a, st], [], [], [((N_SETS, SET_CH, 2 * N_STATE // N_SETS), F32)], body)
    return out


def _glu_out(yg, ws, x, *, name, tm=512):
    n = ws[0][1][2]

    def body(r, f, o, acc, s):
        ygv = r[0][...]
        ov = [_dot(ygv, f[p][...]) for p in range(N_CHIPS)]
        for p in range(N_CHIPS):
            o[0][:, _cols(p, n)] = ov[p].astype(BF16)
        for p in range(2):
            o[1][:, _cols(p, n)] = r[1][:, _cols(p, n)] + ov[p] * _sigmoid(ov[2 + p])

    (o_, y), _ = _rows_call(name, _tile(x.shape[0], tm), [yg, x], ws, [(2 * D_MODEL, BF16), (D_MODEL, F32)], [], body)
    return o_, y


def _glu_out_bwd(o_, dy, ws, y, u, d, *, name, tm=512):
    n = ws[0][1][2]

    def body(r, f, o, acc, s):
        o1 = r[0][:, 0:D_MODEL].astype(F32)
        sg = _sigmoid(r[0][:, D_MODEL:2 * D_MODEL].astype(F32))
        dyv = r[1][...]
        do1 = (dyv * sg).astype(BF16)
        do2 = (dyv * o1 * sg * (1.0 - sg)).astype(BF16)
        o[0][:, 0:D_MODEL] = do1
        o[0][:, D_MODEL:2 * D_MODEL] = do2
        dyg = None
        for p in range(N_CHIPS):
            t = _dot_nt((do1 if p < 2 else do2)[:, _cols(p % 2, n)], f[1 + p][...])
            dyg = t if dyg is None else dyg + t
        yv = r[2][...]
        dys = dyg * _gelu_grad(yv, _gelu(yv)[1])
        o[1][...] = dys.astype(BF16)
        o[2][...] = f[0][...] * dys
        acc[0][...] += jnp.sum(dys * r[3][...], axis=0, keepdims=True)

    (do, dys, dus), (dd,) = _rows_call(name, _tile(dy.shape[0], tm), [o_, dy, y, u], [d] + ws,
                                       [(2 * D_MODEL, BF16), (C_WIDTH, BF16), (C_WIDTH, F32)], [((1, C_WIDTH), F32)],
                                       body)
    return do, dys, dus, dd


def _s5_in_bwd(gs, bset, dus, ws, x, g, dres, *, name, tm=512):
    D = x.shape[1]
    tm = _tile(x.shape[0], tm)

    def body(r, f, o, acc, s):
        du0 = jnp.concatenate([_dot_nt(_state_set(r[0], tm, j), f[1][j]) for j in range(N_SETS)], axis=1)
        du = (du0 + r[1][...]).astype(BF16)
        o[0][...] = du
        dx, dg = _rms_bwd_tile(r[2][...], f[0][...], _cat_nt(du, f[2:]))
        o[1][...] = dx + r[3][...]
        acc[0][...] += dg

    (du, dx), (dg,) = _rows_call(name, tm, [gs, dus, x, dres], [_vec(g), bset] + ws,
                                 [(C_WIDTH, BF16), (D, F32)], [((1, D), F32)], body)
    return du, dx, dg


_SCAN_CHUNK = 256
_RE = slice(0, STATE_ROWS)
_IM = slice(STATE_ROWS, 2 * STATE_ROWS)
assert SCAN_BLOCK == 8


def _token(g, i, rows):
    return pl.ds(pl.multiple_of(g * (rows * SCAN_BLOCK), rows * SCAN_BLOCK) + i, rows, stride=SCAN_BLOCK)


def _fill_chunk(s3, a_ref, wset, tc, nt):
    for j in range(N_SETS):
        av = a_ref[:, _cols(j, SET_CH)].astype(BF16)
        y = _dot_nt(av, wset[j]) if nt else _dot(av, wset[j])
        for k, c in enumerate(_set_groups(j)):
            s3[:, 8 * c:8 * (c + 1), :] = y[:, _cols(k, STATE_LANES)].reshape(tc // 8, 8, STATE_LANES)


def _chunk_token(s3, g, i):
    return s3[g, pl.ds(i, _STATE_TILE, stride=SCAN_BLOCK), :]


def _scan_fwd(u, bset, pw, *, name):
    S = u.shape[0]
    tc = _tile(S, _SCAN_CHUNK, 8)

    def body(u_ref, bset_ref, pw_ref, xs_ref, st_ref, s3):
        @pl.when(pl.program_id(0) == 0)
        def _():
            st_ref[...] = jnp.zeros_like(st_ref)

        _fill_chunk(s3, u_ref, bset_ref, tc, nt=False)
        ar = pw_ref[0, _RE, :]
        ai = pw_ref[0, _IM, :]

        def block(g, carry):
            xr, xi = carry
            cr = ci = nr = ni = None
            for j in range(SCAN_BLOCK):
                b = _chunk_token(s3, g, j)
                br, bi = b[_RE], b[_IM]
                cr, ci = (br, bi) if j == 0 else (ar * cr - ai * ci + br, ar * ci + ai * cr + bi)
                pr, pi = pw_ref[j, _RE, :], pw_ref[j, _IM, :]
                nr = pr * xr - pi * xi + cr
                ni = pr * xi + pi * xr + ci
                xs_ref[_token(g, j, STATE_ROWS), :] = _pack_state(nr, ni)
            return nr, ni

        xr, xi = lax.fori_loop(0, tc // SCAN_BLOCK, block, (st_ref[_RE, :], st_ref[_IM, :]), unroll=4)
        st_ref[_RE, :] = xr
        st_ref[_IM, :] = xi

    return pl.pallas_call(
        body, name=name, grid=(S // tc,),
        in_specs=[pl.BlockSpec((tc, u.shape[1]), lambda i: (i, 0)), pl.BlockSpec(bset.shape, lambda i: (0, 0, 0)),
                  pl.BlockSpec(pw.shape, lambda i: (0, 0, 0))],
        out_specs=pl.BlockSpec((tc * STATE_ROWS, STATE_LANES), lambda i: (i, 0)),
        out_shape=jax.ShapeDtypeStruct((S * STATE_ROWS, STATE_LANES), jnp.uint32),
        scratch_shapes=[pltpu.VMEM((2 * STATE_ROWS, STATE_LANES), F32),
                        pltpu.VMEM((tc // 8, _STATE_TILE * 8, STATE_LANES), F32)],
        compiler_params=_params(("arbitrary",)),
    )(u, bset, pw)


def _scan_bwd(dys, cset, xs, pw, *, name):
    S = dys.shape[0]
    tc = _tile(S, _SCAN_CHUNK, 8)
    nc = S // tc

    def body(dys_ref, cset_ref, xs_ref, pw_ref, g_ref, da_ref, st_ref, s3):
        @pl.when(pl.program_id(0) == 0)
        def _():
            st_ref[...] = jnp.zeros_like(st_ref)
            da_ref[...] = jnp.zeros_like(da_ref)

        _fill_chunk(s3, dys_ref, cset_ref, tc, nt=True)
        ar = pw_ref[0, _RE, :]
        ai = pw_ref[0, _IM, :]

        def block(k, carry):
            gr, gi, dar, dai = carry
            g = tc // SCAN_BLOCK - 1 - k
            cr = ci = None
            pgr, pgi = gr, gi
            for j in range(SCAN_BLOCK):
                i = SCAN_BLOCK - 1 - j
                xr, xi = _unpack_state(xs_ref[_token(g, i, STATE_ROWS), :])
                dar = dar + pgr * xr + pgi * xi
                dai = dai + pgi * xr - pgr * xi
                d = _chunk_token(s3, g, i)
                dr, di = d[_RE], d[_IM]
                cr, ci = (dr, di) if j == 0 else (ar * cr + ai * ci + dr, ar * ci - ai * cr + di)
                pr, pi = pw_ref[j, _RE, :], pw_ref[j, _IM, :]
                pgr = pr * gr + pi * gi + cr
                pgi = pr * gi - pi * gr + ci
                g_ref[_token(g, i, STATE_ROWS), :] = _pack_state(pgr, pgi)
            return pgr, pgi, dar, dai

        init = (st_ref[_RE, :], st_ref[_IM, :], da_ref[_RE, :], da_ref[_IM, :])
        gr, gi, dar, dai = lax.fori_loop(0, tc // SCAN_BLOCK, block, init, unroll=4)
        st_ref[_RE, :] = gr
        st_ref[_IM, :] = gi
        da_ref[_RE, :] = dar
        da_ref[_IM, :] = dai

    packed = pl.BlockSpec((tc * STATE_ROWS, STATE_LANES), lambda i: (nc - 1 - i, 0))
    vec = pl.BlockSpec((2 * STATE_ROWS, STATE_LANES), lambda i: (0, 0))
    return pl.pallas_call(
        body, name=name, grid=(nc,),
        in_specs=[pl.BlockSpec((tc, dys.shape[1]), lambda i: (nc - 1 - i, 0)),
                  pl.BlockSpec(cset.shape, lambda i: (0, 0, 0)), packed, pl.BlockSpec(pw.shape, lambda i: (0, 0, 0))],
        out_specs=[packed, vec],
        out_shape=[jax.ShapeDtypeStruct(xs.shape, jnp.uint32), jax.ShapeDtypeStruct((2 * STATE_ROWS, STATE_LANES), F32)],
        scratch_shapes=[pltpu.VMEM((2 * STATE_ROWS, STATE_LANES), F32),
                        pltpu.VMEM((tc // 8, _STATE_TILE * 8, STATE_LANES), F32)],
        compiler_params=_params(("arbitrary",)),
    )(dys, cset, xs, pw)


def _down_loss_head(h, ws, x, g, target, *, name, tm=512):
    S, D = x.shape

    def body(r, f, o, acc, s):
        xv = r[1][...]
        for p in range(N_CHIPS):
            xv = xv + _dot(r[0][p], f[1 + p][...])
        gv = f[0][...]
        rs = lax.rsqrt(_mean(xv * xv) + EPS)
        xh = xv * rs
        err = xh * gv - r[2][...]
        acc[1][...] += 0.5 * jnp.sum(_mean(err * err), axis=0, keepdims=True)
        dy = err * (1.0 / D)
        dyg = dy * gv
        o[0][...] = rs * (dyg - xh * _mean(dyg * xh))
        acc[0][...] += jnp.sum(dy * xh, axis=0, keepdims=True)

    (dx,), (dg, loss) = _rows_call(name, _tile(S, tm), [h, x, target], [_vec(g)] + ws, [(D, F32)],
                                   [((1, D), F32), ((1, 128), F32)], body)
    return dx, dg, loss


_ADAM_C1 = 1.0 - ADAM_B1 ** ADAM_STEP
_ADAM_C2 = 1.0 - ADAM_B2 ** ADAM_STEP
_ONE_BLOCK_BYTES = 8 * 1024 * 1024
_SUM_ROWS = 512
_ADAM_ROWS = 512


def _adamw_math(w, g, m, v):
    nm = ADAM_B1 * m + (1.0 - ADAM_B1) * g
    nv = ADAM_B2 * v + (1.0 - ADAM_B2) * (g * g)
    m_hat = nm / _ADAM_C1
    v_hat = nv / _ADAM_C2
    return -ADAM_LR * (m_hat / (jnp.sqrt(v_hat) + ADAM_EPS) + ADAM_WD * w), nm, nv


def _adamw_shard(w, gsrc, m, v, *, name):
    R, C = w.shape
    n_l = len(gsrc)
    rows = R // n_l
    tr = rows
    for _, r0 in gsrc:
        tr = math.gcd(tr, r0) if r0 else tr
    tr = _tile(tr, _ADAM_ROWS, 8)
    nb = rows // tr
    assert rows % tr == 0 and all(r0 % tr == 0 for _, r0 in gsrc)

    def body(*refs):
        w_ref, g_refs, (m_ref, v_ref, go_ref, d_ref, nm_ref, nv_ref) = refs[0], refs[1:1 + n_l], refs[1 + n_l:]
        layer = pl.program_id(0) // nb
        gv = g_refs[0][...]
        for l in range(1, n_l):
            gv = jnp.where(layer == l, g_refs[l][...], gv)
        go_ref[...] = gv
        d_ref[...], nm_ref[...], nv_ref[...] = _adamw_math(w_ref[...], gv, m_ref[...], v_ref[...])

    def g_spec(l, r0):
        return pl.BlockSpec((tr, C), lambda i: (r0 // tr + jnp.clip(i - l * nb, 0, nb - 1), 0))

    blk = pl.BlockSpec((tr, C), lambda i: (i, 0))
    out = jax.ShapeDtypeStruct((R, C), F32)
    return pl.pallas_call(
        body, name=name, grid=(R // tr,),
        in_specs=[blk] + [g_spec(l, r0) for l, (_, r0) in enumerate(gsrc)] + [blk, blk], out_specs=[blk] * 4,
        out_shape=[out] * 4, compiler_params=_params(("parallel",)),
    )(w, *[g for g, _ in gsrc], m, v)


def _adamw_small(ws, gs, ms, vs, *, name):
    n = len(ws)

    def body(*refs):
        w_r, g_r, m_r, v_r = refs[:n], refs[n:2 * n], refs[2 * n:3 * n], refs[3 * n:4 * n]
        d_r, nm_r, nv_r = refs[4 * n:5 * n], refs[5 * n:6 * n], refs[6 * n:7 * n]
        for k in range(n):
            d_r[k][...], nm_r[k][...], nv_r[k][...] = _adamw_math(w_r[k][...], g_r[k][...], m_r[k][...], v_r[k][...])

    vm = pl.BlockSpec(memory_space=pltpu.VMEM)
    out = [jax.ShapeDtypeStruct(w.shape, F32) for w in ws]
    res = pl.pallas_call(body, name=name, in_specs=[vm] * (4 * n), out_specs=[vm] * (3 * n), out_shape=out * 3,
                         compiler_params=pltpu.CompilerParams(vmem_limit_bytes=VMEM_LIMIT))(*ws, *gs, *ms, *vs)
    return res[:n], res[n:2 * n], res[2 * n:]


def _sum_slots(x, *, name):
    n, R, C = x.shape
    tr = R if (n + 1) * R * C * 4 <= _ONE_BLOCK_BYTES else _tile(R, 256, 8)

    def body(x_ref, o_ref):
        acc = x_ref[0]
        for k in range(1, n):
            acc = acc + x_ref[k]
        o_ref[...] = acc

    return pl.pallas_call(
        body, name=name, grid=(R // tr,),
        in_specs=[pl.BlockSpec((n, tr, C), lambda i: (0, i, 0))], out_specs=pl.BlockSpec((tr, C), lambda i: (i, 0)),
        out_shape=jax.ShapeDtypeStruct((R, C), F32), compiler_params=_params(("parallel",)),
    )(x)


def _pair_sum(g, r, where, *, name):
    n, R, C = g.shape
    Rh = R // 2
    tr = _tile(Rh, _SUM_ROWS, 16)
    nb = Rh // tr

    def body(where_ref, g_ref, r_ref, o_ref):
        o_ref[...] = (g_ref[...] + r_ref[...]).astype(BF16)

    def slot(p, w):
        return p + jnp.where(p >= w[0], 1, 0)

    return pl.pallas_call(
        body, name=name,
        grid_spec=pltpu.PrefetchScalarGridSpec(
            num_scalar_prefetch=1, grid=(n - 1, nb),
            in_specs=[pl.BlockSpec((1, tr, C), lambda p, i, w: (slot(p, w), w[1] * nb + i, 0)),
                      pl.BlockSpec((1, tr, C), lambda p, i, w: (slot(p, w), i, 0))],
            out_specs=pl.BlockSpec((1, tr, C), lambda p, i, w: (slot(p, w), i, 0)),
        ),
        out_shape=jax.ShapeDtypeStruct((n, Rh, C), BF16), compiler_params=_params(("parallel", "parallel")),
    )(where, g, r)


def _chip_sum(g, r, slots, where, *, name):
    n, R, C = g.shape
    Rh = R // 2
    tr = _tile(Rh, _SUM_ROWS, 16)
    nb = Rh // tr

    def body(w_ref, g_ref, r_ref, s_ref, o_ref):
        acc = g_ref[0] + r_ref[0]
        for k in range(slots.shape[0]):
            acc = acc + s_ref[k].astype(F32)
        o_ref[...] = acc

    return pl.pallas_call(
        body, name=name,
        grid_spec=pltpu.PrefetchScalarGridSpec(
            num_scalar_prefetch=1, grid=(nb,),
            in_specs=[pl.BlockSpec((1, tr, C), lambda i, w: (w[0], w[1] * nb + i, 0)),
                      pl.BlockSpec((1, tr, C), lambda i, w: (w[0], i, 0)),
                      pl.BlockSpec((slots.shape[0], tr, C), lambda i, w: (0, i, 0))],
            out_specs=pl.BlockSpec((tr, C), lambda i, w: (w[1] * nb + i, 0)),
        ),
        out_shape=jax.ShapeDtypeStruct((R, C), F32), compiler_params=_params(("parallel",)),
    )(where, g, r, slots)


ANY = pl.BlockSpec(memory_space=pl.ANY)


def _place():
    return lax.axis_index("x"), lax.axis_index("y"), lax.axis_index("c")


def _other_chips(x, y):
    return [(1 - x, y), (x, 1 - y), (1 - x, 1 - y)]


def _aliased_comm_call(body, bufs, n_sems, *, name):
    n = len(bufs)
    return pl.pallas_call(
        body, name=name, out_shape=[jax.ShapeDtypeStruct(b.shape, b.dtype) for b in bufs],
        in_specs=[ANY] * n, out_specs=[ANY] * n, input_output_aliases={k: k for k in range(n)},
        scratch_shapes=[pltpu.SemaphoreType.DMA((n_sems,)), pltpu.SemaphoreType.DMA((n_sems,))],
    )(*bufs)


HBM = pl.BlockSpec(memory_space=pltpu.HBM)
SEM = pl.BlockSpec(memory_space=pltpu.SEMAPHORE)
_SPLIT = pltpu.CompilerParams(has_side_effects=pltpu.SideEffectType.DATAFLOW_SIDE_EFFECTING)


def _in_hbm(arrs):
    return [pltpu.with_memory_space_constraint(a, pltpu.HBM) for a in arrs]


def _gather_ici_start(bufs, after, *, name):
    n = len(bufs)

    def body(*refs):
        send_sems, recv_sems, outs, token = refs[n + 1], refs[n + 2], refs[n + 3:2 * n + 3], refs[2 * n + 3]
        x, y, c = _place()
        for b in range(n):
            rh = bufs[b].shape[1] // 2
            part = outs[b].at[2 * x + y, pl.ds(c * rh, rh), :]
            for j, chip in enumerate(_other_chips(x, y)):
                pltpu.make_async_remote_copy(src_ref=part, dst_ref=part, send_sem=send_sems.at[3 * b + j],
                                             recv_sem=recv_sems.at[3 * b + j], device_id=(*chip, c),
                                             device_id_type=MESH).start()
        token[...] = jnp.zeros_like(token)

    res = pl.pallas_call(
        body, name=name,
        out_shape=(pltpu.SemaphoreType.DMA((3 * n,)), pltpu.SemaphoreType.DMA((3 * n,)),
                   *[pltpu.HBM(b.shape, b.dtype) for b in bufs], jax.ShapeDtypeStruct((8, 128), F32)),
        in_specs=[HBM] * n + [ANY], out_specs=(SEM, SEM, *[HBM] * n, pl.BlockSpec(memory_space=pltpu.VMEM)),
        input_output_aliases={k: k + 2 for k in range(n)}, compiler_params=_SPLIT,
    )(*_in_hbm(bufs), after)
    return res[0], res[1], list(res[2:2 + n]), res[2 + n]


def _gather_ici_wait(send_sems, recv_sems, bufs, first, after, *, name):
    n = len(bufs)

    def body(*refs):
        ins, ss, rs = refs[:n], refs[n], refs[n + 1]
        x, y, c = _place()
        for b in range(n):
            rh = bufs[b].shape[1] // 2
            mine = ins[b].at[2 * x + y, pl.ds(c * rh, rh), :]
            for j, (cx, cy) in enumerate(_other_chips(x, y)):
                theirs = ins[b].at[2 * cx + cy, pl.ds(c * rh, rh), :]
                cp = pltpu.make_async_remote_copy(src_ref=mine, dst_ref=theirs, send_sem=ss.at[3 * (first + b) + j],
                                                  recv_sem=rs.at[3 * (first + b) + j], device_id=(cx, cy, c),
                                                  device_id_type=MESH)
                cp.wait_send()
                cp.wait_recv()

    return list(pl.pallas_call(
        body, name=name, out_shape=[pltpu.HBM(b.shape, b.dtype) for b in bufs],
        in_specs=[HBM] * n + [SEM, SEM, ANY], out_specs=[HBM] * n,
        input_output_aliases={k: k for k in range(n)}, compiler_params=_SPLIT,
    )(*bufs, send_sems, recv_sems, after))


def _gather_forward(bufs, *, name):
    n = len(bufs)

    def body(*refs):
        outs, send_sems, recv_sems = refs[n:2 * n], refs[2 * n], refs[2 * n + 1]
        x, y, c = _place()

        def copy(b, j, chip, hc):
            rh = bufs[b].shape[1] // 2
            part = outs[b].at[2 * chip[0] + chip[1], pl.ds(hc * rh, rh), :]
            return pltpu.make_async_remote_copy(src_ref=part, dst_ref=part, send_sem=send_sems.at[3 * b + j],
                                                recv_sem=recv_sems.at[3 * b + j], device_id=(x, y, 1 - c),
                                                device_id_type=MESH)

        sends = [copy(b, j, chip, c) for b in range(n) for j, chip in enumerate(_other_chips(x, y))]
        for cp in sends:
            cp.start()
        for b in range(n):
            for j, chip in enumerate(_other_chips(x, y)):
                copy(b, j, chip, 1 - c).wait_recv()
        for cp in sends:
            cp.wait_send()

    return _aliased_comm_call(body, bufs, 3 * n, name=name)


def _forward_copy(buf, send_sems, recv_sems, k, chip, half, to):
    rh = buf.shape[1] // 2
    part = buf.at[2 * chip[0] + chip[1], pl.ds(half * rh, rh), :]
    return pltpu.make_async_remote_copy(src_ref=part, dst_ref=part, send_sem=send_sems.at[k], recv_sem=recv_sems.at[k],
                                        device_id=to, device_id_type=MESH)


def _gather_forward_start(bufs, after, *, name):
    n = len(bufs)

    def body(*refs):
        send_sems, recv_sems, outs, token = refs[n + 1], refs[n + 2], refs[n + 3:2 * n + 3], refs[2 * n + 3]
        x, y, c = _place()
        for b in range(n):
            for j, chip in enumerate(_other_chips(x, y)):
                _forward_copy(outs[b], send_sems, recv_sems, 3 * b + j, chip, c, (x, y, 1 - c)).start()
        token[...] = jnp.zeros_like(token)

    res = pl.pallas_call(
        body, name=name,
        out_shape=(pltpu.SemaphoreType.DMA((3 * n,)), pltpu.SemaphoreType.DMA((3 * n,)),
                   *[pltpu.HBM(b.shape, b.dtype) for b in bufs], jax.ShapeDtypeStruct((8, 128), F32)),
        in_specs=[HBM] * n + [ANY], out_specs=(SEM, SEM, *[HBM] * n, pl.BlockSpec(memory_space=pltpu.VMEM)),
        input_output_aliases={k: k + 2 for k in range(n)}, compiler_params=_SPLIT,
    )(*_in_hbm(bufs), after)
    return res[0], res[1], list(res[2:2 + n]), res[2 + n]


def _gather_forward_wait(send_sems, recv_sems, bufs, after, *, name):
    n = len(bufs)

    def body(*refs):
        ins, ss, rs = refs[:n], refs[n], refs[n + 1]
        x, y, c = _place()
        for b in range(n):
            for j, chip in enumerate(_other_chips(x, y)):
                _forward_copy(ins[b], ss, rs, 3 * b + j, chip, c, (x, y, 1 - c)).wait_send()
                _forward_copy(ins[b], ss, rs, 3 * b + j, chip, 1 - c, (x, y, 1 - c)).wait_recv()

    return list(pl.pallas_call(
        body, name=name, out_shape=[pltpu.HBM(b.shape, b.dtype) for b in bufs],
        in_specs=[HBM] * n + [SEM, SEM, ANY], out_specs=[HBM] * n,
        input_output_aliases={k: k for k in range(n)}, compiler_params=_SPLIT,
    )(*bufs, send_sems, recv_sems, after))


def _chip_exchange_start(hs, *, name):
    n = len(hs)
    lands = [lax.empty((3,) + h.shape[1:], h.dtype) for h in hs]

    def body(*refs):
        send_sems, recv_sems = refs[2 * n], refs[2 * n + 1]
        h_out, l_out, token = refs[2 * n + 2:3 * n + 2], refs[3 * n + 2:4 * n + 2], refs[4 * n + 2]
        x, y, c = _place()
        for b in range(n):
            for j, (cx, cy) in enumerate(_other_chips(x, y)):
                pltpu.make_async_remote_copy(src_ref=h_out[b].at[2 * cx + cy], dst_ref=l_out[b].at[j],
                                             send_sem=send_sems.at[3 * b + j], recv_sem=recv_sems.at[3 * b + j],
                                             device_id=(cx, cy, c), device_id_type=MESH).start()
        token[...] = jnp.zeros_like(token)

    res = pl.pallas_call(
        body, name=name,
        out_shape=(pltpu.SemaphoreType.DMA((3 * n,)), pltpu.SemaphoreType.DMA((3 * n,)),
                   *[pltpu.HBM(a.shape, a.dtype) for a in hs + lands], jax.ShapeDtypeStruct((8, 128), F32)),
        in_specs=[HBM] * (2 * n), out_specs=(SEM, SEM, *[HBM] * (2 * n), pl.BlockSpec(memory_space=pltpu.VMEM)),
        input_output_aliases={k: k + 2 for k in range(2 * n)}, compiler_params=_SPLIT,
    )(*_in_hbm(hs + lands))
    return res[0], res[1], list(res[2:2 + n]), list(res[2 + n:2 + 2 * n]), res[2 + 2 * n]


def _chip_exchange_wait(send_sems, recv_sems, hs, lands, after, *, name):
    n = len(hs)

    def body(*refs):
        h_in, l_in, ss, rs = refs[:n], refs[n:2 * n], refs[2 * n], refs[2 * n + 1]
        x, y, c = _place()
        for b in range(n):
            for j, (cx, cy) in enumerate(_other_chips(x, y)):
                cp = pltpu.make_async_remote_copy(src_ref=h_in[b].at[2 * cx + cy], dst_ref=l_in[b].at[j],
                                                  send_sem=ss.at[3 * b + j], recv_sem=rs.at[3 * b + j],
                                                  device_id=(cx, cy, c), device_id_type=MESH)
                cp.wait_send()
                cp.wait_recv()

    res = pl.pallas_call(
        body, name=name, out_shape=[pltpu.HBM(a.shape, a.dtype) for a in hs + lands],
        in_specs=[HBM] * (2 * n) + [SEM, SEM, ANY], out_specs=[HBM] * (2 * n),
        input_output_aliases={k: k for k in range(2 * n)}, compiler_params=_SPLIT,
    )(*hs, *lands, send_sems, recv_sems, after)
    return list(res[n:])


def _peers(x, y, c):
    return [((1 - x) if fx else x, (1 - y) if fy else y, (1 - c) if fc else c)
            for fx in (0, 1) for fy in (0, 1) for fc in (0, 1) if fx or fy or fc]


def _all_to_all_start(slab, after, *, name):
    land = lax.empty((N_DEV,) + slab.shape, slab.dtype)

    def body(slab_in, land_in, after_ref, send_sems, recv_sems, slab_out, land_out, token):
        x, y, c = _place()
        for k, peer in enumerate(_peers(x, y, c)):
            pltpu.make_async_remote_copy(src_ref=slab_out, dst_ref=land_out.at[4 * x + 2 * y + c],
                                         send_sem=send_sems.at[k], recv_sem=recv_sems.at[k], device_id=peer,
                                         device_id_type=MESH).start()
        token[...] = jnp.zeros_like(token)

    return pl.pallas_call(
        body, name=name,
        out_shape=(pltpu.SemaphoreType.DMA((N_DEV - 1,)), pltpu.SemaphoreType.DMA((N_DEV - 1,)),
                   pltpu.HBM(slab.shape, slab.dtype), pltpu.HBM(land.shape, land.dtype),
                   jax.ShapeDtypeStruct((8, 128), F32)),
        in_specs=[HBM, HBM, ANY], out_specs=(SEM, SEM, HBM, HBM, pl.BlockSpec(memory_space=pltpu.VMEM)),
        input_output_aliases={0: 2, 1: 3}, compiler_params=_SPLIT,
    )(*_in_hbm([slab, land]), after)


def _all_to_all_wait(send_sems, recv_sems, slab, land, afters, *, name):
    def body(slab_in, land_in, ss, rs, *_):
        x, y, c = _place()
        for k, (px, py, pc) in enumerate(_peers(x, y, c)):
            cp = pltpu.make_async_remote_copy(src_ref=slab_in, dst_ref=land_in.at[4 * px + 2 * py + pc],
                                              send_sem=ss.at[k], recv_sem=rs.at[k], device_id=(px, py, pc),
                                              device_id_type=MESH)
            cp.wait_send()
            cp.wait_recv()

    return pl.pallas_call(
        body, name=name, out_shape=[pltpu.HBM(slab.shape, slab.dtype), pltpu.HBM(land.shape, land.dtype)],
        in_specs=[HBM, HBM, SEM, SEM] + [ANY] * len(afters), out_specs=[HBM, HBM], input_output_aliases={0: 0, 1: 1},
        compiler_params=_SPLIT,
    )(slab, land, send_sems, recv_sems, *afters)


def _pair_exchange_start(gs, *, name):
    n = len(gs)
    lands = [lax.empty((g.shape[0], g.shape[1] // 2, g.shape[2]), g.dtype) for g in gs]

    def body(*refs):
        send_sems, recv_sems = refs[2 * n], refs[2 * n + 1]
        g_out, l_out, token = refs[2 * n + 2:3 * n + 2], refs[3 * n + 2:4 * n + 2], refs[4 * n + 2]
        x, y, c = _place()
        for b in range(n):
            rh = gs[b].shape[1] // 2
            pltpu.make_async_remote_copy(src_ref=g_out[b].at[:, pl.ds((1 - c) * rh, rh), :], dst_ref=l_out[b],
                                         send_sem=send_sems.at[b], recv_sem=recv_sems.at[b],
                                         device_id=(x, y, 1 - c), device_id_type=MESH).start()
        token[...] = jnp.zeros_like(token)

    res = pl.pallas_call(
        body, name=name,
        out_shape=(pltpu.SemaphoreType.DMA((n,)), pltpu.SemaphoreType.DMA((n,)),
                   *[pltpu.HBM(a.shape, a.dtype) for a in gs + lands], jax.ShapeDtypeStruct((8, 128), F32)),
        in_specs=[HBM] * (2 * n), out_specs=(SEM, SEM, *[HBM] * (2 * n), pl.BlockSpec(memory_space=pltpu.VMEM)),
        input_output_aliases={k: k + 2 for k in range(2 * n)}, compiler_params=_SPLIT,
    )(*_in_hbm(gs + lands))
    return res[0], res[1], list(res[2:2 + n]), list(res[2 + n:2 + 2 * n]), res[2 + 2 * n]


def _pair_exchange_wait(send_sems, recv_sems, gs, lands, after, *, name):
    n = len(gs)

    def body(*refs):
        g_in, l_in, ss, rs = refs[:n], refs[n:2 * n], refs[2 * n], refs[2 * n + 1]
        x, y, c = _place()
        for b in range(n):
            rh = gs[b].shape[1] // 2
            cp = pltpu.make_async_remote_copy(src_ref=g_in[b].at[:, pl.ds((1 - c) * rh, rh), :], dst_ref=l_in[b],
                                              send_sem=ss.at[b], recv_sem=rs.at[b], device_id=(x, y, 1 - c),
                                              device_id_type=MESH)
            cp.wait_send()
            cp.wait_recv()

    res = pl.pallas_call(
        body, name=name, out_shape=[pltpu.HBM(a.shape, a.dtype) for a in gs + lands],
        in_specs=[HBM] * (2 * n) + [SEM, SEM, ANY], out_specs=[HBM] * (2 * n),
        input_output_aliases={k: k for k in range(2 * n)}, compiler_params=_SPLIT,
    )(*gs, *lands, send_sems, recv_sems, after)
    return list(res[:n]), list(res[n:])


def _pair_share_start(ss, *, name):
    n = len(ss)

    def body(*refs):
        send_sems, recv_sems, outs, token = refs[n], refs[n + 1], refs[n + 2:2 * n + 2], refs[2 * n + 2]
        x, y, c = _place()
        for b in range(n):
            rh = ss[b].shape[0] // 2
            mine = outs[b].at[pl.ds(c * rh, rh), :]
            pltpu.make_async_remote_copy(src_ref=mine, dst_ref=mine, send_sem=send_sems.at[b],
                                         recv_sem=recv_sems.at[b], device_id=(x, y, 1 - c),
                                         device_id_type=MESH).start()
        token[...] = jnp.zeros_like(token)

    res = pl.pallas_call(
        body, name=name,
        out_shape=(pltpu.SemaphoreType.DMA((n,)), pltpu.SemaphoreType.DMA((n,)),
                   *[pltpu.HBM(a.shape, a.dtype) for a in ss], jax.ShapeDtypeStruct((8, 128), F32)),
        in_specs=[HBM] * n, out_specs=(SEM, SEM, *[HBM] * n, pl.BlockSpec(memory_space=pltpu.VMEM)),
        input_output_aliases={k: k + 2 for k in range(n)}, compiler_params=_SPLIT,
    )(*_in_hbm(ss))
    return res[0], res[1], list(res[2:2 + n]), res[2 + n]


def _pair_share_wait(send_sems, recv_sems, ss, after, *, name):
    n = len(ss)

    def body(*refs):
        ins, sems_s, sems_r = refs[:n], refs[n], refs[n + 1]
        x, y, c = _place()
        for b in range(n):
            rh = ss[b].shape[0] // 2
            mine = ins[b].at[pl.ds(c * rh, rh), :]
            theirs = ins[b].at[pl.ds((1 - c) * rh, rh), :]
            cp = pltpu.make_async_remote_copy(src_ref=mine, dst_ref=theirs, send_sem=sems_s.at[b],
                                              recv_sem=sems_r.at[b], device_id=(x, y, 1 - c),
                                              device_id_type=MESH)
            cp.wait_send()
            cp.wait_recv()

    return list(pl.pallas_call(
        body, name=name, out_shape=[pltpu.HBM(a.shape, a.dtype) for a in ss],
        in_specs=[HBM] * n + [SEM, SEM, ANY], out_specs=[HBM] * n,
        input_output_aliases={k: k for k in range(n)}, compiler_params=_SPLIT,
    )(*ss, send_sems, recv_sems, after))


_SMALL_SHARDED = (("e_conv_w", 2), ("o_norm", 1), ("o_d", 1))
_REPLICATED = ("e_norm", "e_gmlp_w", "e_gmlp_b", "e_conv_b", "e_conv_ln_g", "e_conv_ln_b", "o_lam_re", "o_lam_im",
               "o_log_dt", "o_b_re", "o_b_im", "o_c_re", "o_c_im", "ca_norm", "ca_mem_norm", "ffn_norm", "final_norm")
_SMALL = tuple(n for n, _ in _SMALL_SHARDED) + _REPLICATED
_WEIGHTS = ("e_norm", "e_w_in", "e_gmlp_w", "e_gmlp_b", "e_conv_w", "e_conv_b", "e_conv_ln_g", "e_conv_ln_b",
            "e_w_out", "o_norm", "o_w_in", "o_lam_re", "o_lam_im", "o_log_dt", "o_b_re", "o_b_im", "o_c_re", "o_c_im",
            "o_d", "o_w_out", "ca_norm", "ca_mem_norm", "ca_wq", "ca_wk", "ca_wv", "ca_wo", "ffn_norm", "ffn_w_gate",
            "ffn_w_up", "ffn_w_down", "final_norm")


def _pack_rows(arrs, width, dtype, row_mult=8):
    parts, spans, r0 = [], [], 0
    for a in arrs:
        flat = a.reshape(-1).astype(dtype)
        rows = -(-flat.shape[0] // (width * row_mult)) * row_mult
        if rows * width != flat.shape[0]:
            flat = jnp.pad(flat, (0, rows * width - flat.shape[0]))
        parts.append(flat.reshape(rows, width))
        spans.append((r0, rows))
        r0 += rows
    return jnp.concatenate(parts, axis=0), spans


def _unpack_rows(slab, spans, shapes):
    out = []
    for (r0, rows), shp in zip(spans, shapes):
        n = math.prod(shp)
        out.append(slab[r0:r0 + rows].reshape(-1)[:n].reshape(shp))
    return out


def _two_d(a):
    return a.reshape(-1, a.shape[-1])


def _shard_rows(n, a):
    return _two_d(jnp.swapaxes(a, -1, -2) if n in _TRANSPOSED else a)


def _from_shard_rows(n, rows, shape):
    if n in _TRANSPOSED:
        return jnp.swapaxes(rows.reshape(shape[:-2] + (shape[-1], shape[-2])), -1, -2)
    return rows.reshape(shape)


def _local_slab(local, slab, dtype):
    parts = sorted((r0, n, l) for n, (_, where) in _PLACE.items() for l, (s, r0) in enumerate(where) if s == slab)
    shards = [_shard_rows(n, local[n] if len(_PLACE[n][1]) == 1 else local[n][l]) for _, n, l in parts]
    return jnp.concatenate([a.astype(dtype) for a in shards], axis=0)


def _set_diag(b, pattern):
    return jnp.einsum(pattern, b, jnp.eye(C_GROUPS // N_SETS, dtype=b.dtype))


def _s5_discretize(lam_re, lam_im, log_dt, b_re, b_im):
    dt = jnp.exp(log_dt)[:, None]
    mag = jnp.exp(lam_re * dt)
    ar = mag * jnp.cos(lam_im * dt)
    ai = mag * jnp.sin(lam_im * dt)
    den = lam_re * lam_re + lam_im * lam_im
    qr = ((ar - 1.0) * lam_re + ai * lam_im) / den
    qi = (ai * lam_re - (ar - 1.0) * lam_im) / den
    bbr = qr[..., None] * b_re - qi[..., None] * b_im
    bbi = qr[..., None] * b_im + qi[..., None] * b_re
    return ar, ai, bbr, bbi


def _attention_block(x, mem, W, w, i, tag):
    xn, q = _norm_mm(x, w["ca_norm"][i], _shards(W, "ca_wq", i), split="k", out_dtype=BF16, name=f"{tag}_q")
    memn = _rms_fwd(mem, w["ca_mem_norm"][i], name=f"{tag}_ca_memnorm")
    k = _mm_k(memn, _shards(W, "ca_wk", i), out_dtype=BF16, name=f"{tag}_k")
    v = _mm_k(memn, _shards(W, "ca_wv", i), out_dtype=BF16, name=f"{tag}_v")
    o = _attn_fwd(q, k, v, name=f"{tag}_attn")
    y = _mm_k(o, _shards(W, "ca_wo", i), add=x, name=f"{tag}_wo")
    return y, (x, xn, memn, q, k, v, o)


def _attention_block_bwd(dy, saved, mem, W, w, i, tag, G, grads, token=None, mid=None):
    x, xn, memn, q, k, v, o = saved
    gain = w["ca_norm"][i]
    if token is not None:
        k = _behind(k, token)
    G = _grad_to_slab(G, "ca_wo", i, o, dy, a_cols=256, name=f"{tag}_dwo")
    dq, dk, dv = _attn_bwd(dy, _shards(W, "ca_wo", i), q, k, v, name=f"{tag}_attn_bwd")
    token = mid(dq) if mid is not None else None
    if token is not None:
        gain = _behind(gain, token)
    G = _grad_to_slab(G, "ca_wq", i, xn, dq, a_cols=256, name=f"{tag}_dwq")
    G = _grad_to_slab(G, "ca_wk", i, memn, dk, a_cols=256, name=f"{tag}_dwk")
    G = _grad_to_slab(G, "ca_wv", i, memn, dv, a_cols=256, name=f"{tag}_dwv")
    dmemn = _mm_k_t([(dk, _shards(W, "ca_wk", i)), (dv, _shards(W, "ca_wv", i))], name=f"{tag}_dmemn")
    dx, dg = _norm_bwd_k(dq, _shards(W, "ca_wq", i), x, gain, dy, name=f"{tag}_dq_norm_bwd")
    grads["ca_norm"][i] = dg[0]
    grads["ca_mem_norm"][i] = _rms_dg(mem, w["ca_mem_norm"][i], dmemn, name=f"{tag}_ca_memnorm_bwd")[0]
    return dx, G


def _ffn_block(x, W, w, i, tag, head=None):
    fn, gate, up, h = _ffn_up(x, w["ffn_norm"][i], _shards(W, "ffn_w_gate", i), _shards(W, "ffn_w_up", i),
                              name=f"{tag}_ffn_up")
    if head is None:
        y = _mm_k(h, _shards(W, "ffn_w_down", i), add=x, name=f"{tag}_down")
    else:
        y = _down_loss_head(h, _shards(W, "ffn_w_down", i), x, *head, name=f"{tag}_down_loss_head")
    return y, (x, fn, gate, up, h)


def _ffn_block_bwd(dy, saved, W, w, i, tag, G, grads, token=None, mid=None):
    x, fn, gate, up, h = saved
    gain = w["ffn_norm"][i]
    G = _grad_to_slab(G, "ffn_w_down", i, h, dy, name=f"{tag}_dwd")
    dg, du = _ffn_bwd_hidden(dy, _shards(W, "ffn_w_down", i), gate, up, token, name=f"{tag}_ffn_bwd_hidden")
    token = mid(dg) if mid is not None else None
    if token is not None:
        gain = _behind(gain, token)
    G = _grad_to_slab(G, "ffn_w_gate", i, dg, fn, name=f"{tag}_dwg")
    G = _grad_to_slab(G, "ffn_w_up", i, du, fn, name=f"{tag}_dwu")
    dx, dgn = _ffn_in_bwd(dg, du, _shards(W, "ffn_w_gate", i), _shards(W, "ffn_w_up", i), x, gain, dy,
                          name=f"{tag}_ffn_in_bwd")
    grads["ffn_norm"][i] = dgn[0]
    return dx, G


def _gmlp_mask():
    chunk = jnp.arange(GMLP_BLOCK) // CHUNK
    return chunk[None, :] <= chunk[:, None]


def _even_block(x, W, w, tag):
    hn, proj = _norm_mm(x, w["e_norm"][0], _shards(W, "e_w_in"), split="n", out_dtype=F32, name=f"{tag}_w_in")
    wm = jnp.where(_gmlp_mask()[None], w["e_gmlp_w"][0], 0.0).astype(BF16)
    bcol = w["e_gmlp_b"][0][:, :, None]
    cw = jnp.pad(w["e_conv_w"][0], ((0, CONV_HALO - CONV_WIDTH), (0, 0)))
    cb, lg, lb = w["e_conv_b"], w["e_conv_ln_g"], w["e_conv_ln_b"]
    mix, hc = _even_fwd(proj, wm, bcol, cw, cb, lg, lb, name=f"{tag}_mixers")
    y = _mm_k(mix, _shards(W, "e_w_out"), add=x, name=f"{tag}_w_out")
    return y, (x, hn, proj, mix, hc, wm, bcol, cw)


def _even_block_bwd(dy, saved, W, w, tag, G, grads):
    x, hn, proj, mix, hc, wm, bcol, cw = saved
    dmix = _mm_k_t([(dy, _shards(W, "e_w_out"))], name=f"{tag}_dmix")
    G = _grad_to_slab(G, "e_w_out", 0, mix, dy, a_cols=256, name=f"{tag}_dw_out")
    wmt = jnp.swapaxes(wm, 1, 2)
    dpa, dhc, dwm, db, dlg, dlb, dcb = _even_bwd1(proj, dmix, hc, wm, wmt, bcol, w["e_conv_ln_g"], w["e_conv_ln_b"],
                                                  name=f"{tag}_mixers_bwd1")
    dpb, dcw = _even_bwd2(proj, dhc, cw, name=f"{tag}_mixers_bwd2")
    grads["e_gmlp_w"] = jnp.where(_gmlp_mask()[None], dwm, 0.0)[None]
    grads["e_gmlp_b"] = db[:, :, 0][None]
    grads["e_conv_ln_g"], grads["e_conv_ln_b"], grads["e_conv_b"] = dlg, dlb, dcb
    grads["e_conv_w"] = dcw[:CONV_WIDTH][None]
    G = _grad_to_slab(G, "e_w_in", 0, hn, dpa, b_cols=512, chips=(0, 2), name=f"{tag}_dw_in_a")
    G = _grad_to_slab(G, "e_w_in", 0, hn, dpb, b_cols=512, chips=(2, 2), name=f"{tag}_dw_in_b")
    dx, dg = _norm_bwd_n((dpa, dpb), _shards(W, "e_w_in"), x, w["e_norm"][0], dy, name=f"{tag}_in_bwd")
    grads["e_norm"] = dg
    return dx, G


def _odd_block(x, W, w, tag):
    S = x.shape[0]
    hn, u = _norm_mm(x, w["o_norm"][0], _shards(W, "o_w_in"), split="k", out_dtype=F32, name=f"{tag}_w_in")
    disc_in = (w["o_lam_re"][0], w["o_lam_im"][0], w["o_log_dt"][0], w["o_b_re"][0], w["o_b_im"][0])
    (ar, ai, bbr, bbi), disc_vjp = jax.vjp(_s5_discretize, *disc_in)
    sets = (N_SETS, C_GROUPS // N_SETS)
    per_set = N_STATE // N_SETS
    bset = jnp.concatenate([_set_diag(b.reshape(sets + b.shape[1:]), "jgpc,gh->jgchp").reshape(N_SETS, SET_CH, per_set)
                            for b in (bbr, bbi)], axis=2).astype(BF16)
    cset = jnp.concatenate([_set_diag(c.reshape(sets + c.shape[1:]), "jgcp,gh->jgphc").reshape(N_SETS, per_set, SET_CH)
                            for c in (w["o_c_re"][0], -w["o_c_im"][0])], axis=1).astype(BF16)
    powers, pr, pi = [], ar, ai
    for _ in range(SCAN_BLOCK):
        powers.append(jnp.concatenate([pr.reshape(STATE_ROWS, STATE_LANES), pi.reshape(STATE_ROWS, STATE_LANES)], 0))
        pr, pi = pr * ar - pi * ai, pr * ai + pi * ar
    pw = jnp.stack(powers, axis=0)
    xs = _scan_fwd(u, bset, pw, name=f"{tag}_scan").reshape(S // 8, STATE_ROWS, 8, STATE_LANES)
    yv, yg = _s5_readout(xs, cset, u, w["o_d"], name=f"{tag}_readout")
    o, y = _glu_out(yg, _shards(W, "o_w_out"), x, name=f"{tag}_glu_out")
    return y, (x, hn, u, bset, cset, pw, xs, yv, yg, o, disc_vjp)


def _odd_block_bwd(dy, saved, W, w, tag, G, grads):
    x, hn, u, bset, cset, pw, xs, yv, yg, o, disc_vjp = saved
    S = x.shape[0]
    do, dys, dus, dd = _glu_out_bwd(o, dy, _shards(W, "o_w_out"), yv, u, w["o_d"], name=f"{tag}_glu_out_bwd")
    G = _grad_to_slab(G, "o_w_out", 0, yg, do, b_cols=512, name=f"{tag}_dw_out")
    grads["o_d"] = dd
    dcset_t = _state_grad_sets(dys, xs, name=f"{tag}_dcd")
    gs, da = _scan_bwd(dys, cset, xs.reshape(S * STATE_ROWS, STATE_LANES), pw, name=f"{tag}_scan_bwd")
    gs = gs.reshape(xs.shape)
    dbset = _state_grad_sets(u, gs, name=f"{tag}_dbd")
    du, dx, dg = _s5_in_bwd(gs, bset, dus, _shards(W, "o_w_in"), x, w["o_norm"][0], dy, name=f"{tag}_in_bwd")
    G = _grad_to_slab(G, "o_w_in", 0, hn, du, a_cols=256, name=f"{tag}_dw_in")
    grads["o_norm"] = dg
    per = C_GROUPS // N_SETS
    blocks = (N_SETS, per, C_GROUP_CH, 2, per, C_STATE)
    dc = _set_diag(dcset_t.reshape(blocks), "jhcrgp,gh->rjgcp").reshape(2, C_GROUPS, C_GROUP_CH, C_STATE)
    db = _set_diag(dbset.reshape(blocks), "jgcrhp,gh->rjgpc").reshape(2, C_GROUPS, C_STATE, C_GROUP_CH)
    dcr, dci, dbbr, dbbi = dc[0], -dc[1], db[0], db[1]
    dar = da[:STATE_ROWS].reshape(C_GROUPS, C_STATE)
    dai = da[STATE_ROWS:].reshape(C_GROUPS, C_STATE)
    dlr, dli, dldt, dbr, dbi = disc_vjp((dar, dai, dbbr, dbbi))
    grads["o_lam_re"], grads["o_lam_im"], grads["o_log_dt"] = dlr[None], dli[None], dldt[None]
    grads["o_b_re"], grads["o_b_im"], grads["o_c_re"], grads["o_c_im"] = dbr[None], dbi[None], dcr[None], dci[None]
    return dx, G


def _behind(value, token):
    return value + token[0, 0].astype(value.dtype)


class _NoExchange:
    def __init__(self, W):
        self.W = W

    def first_weights(self, w):
        return self.W, w

    def weights(self, stage, after):
        return {}

    def behind_late_start(self, w):
        return w

    def late_weights(self, after):
        return {}

    def grads_ready(self, piece, G):
        return None

    def grads_crossed(self, piece, after):
        return None


def _forward_backward(xs_, mems_, tgt, w, G, exchange):
    W, w = exchange.first_weights(w)
    x1, s_mix0 = _even_block(xs_, W, w, "l0")
    W = {**W, **exchange.weights(1, x1)}
    x2, s_att0 = _attention_block(x1, mems_, W, w, 0, "l0")
    W = {**W, **exchange.weights(2, x2)}
    x3, s_ffn0 = _ffn_block(x2, W, w, 0, "l0")
    W = {**W, **exchange.weights(3, x3)}
    w = exchange.behind_late_start(w)
    x4, s_mix1 = _odd_block(x3, W, w, "l1")
    W = {**W, **exchange.late_weights(x4)}
    x5, s_att1 = _attention_block(x4, mems_, W, w, 1, "l1")
    (dx, dfinal, loss_lanes), s_ffn1 = _ffn_block(x5, W, w, 1, "l1", head=(w["final_norm"], tgt))

    grads = {n: [None, None] for n in ("ca_norm", "ca_mem_norm", "ffn_norm")}
    grads["final_norm"] = dfinal[0]
    dx, G = _ffn_block_bwd(dx, s_ffn1, W, w, 1, "l1", G, grads)
    dx, G = _attention_block_bwd(dx, s_att1, mems_, W, w, 1, "l1", G, grads)
    dx, G = _odd_block_bwd(dx, s_mix1, W, w, "l1", G, grads)
    token = exchange.grads_ready("l1", G)
    dx, G = _ffn_block_bwd(dx, s_ffn0, W, w, 0, "l0", G, grads, token,
                           lambda after: exchange.grads_crossed("l1", after))
    token = exchange.grads_ready("ffn0", G)
    dx, G = _attention_block_bwd(dx, s_att0, mems_, W, w, 0, "l0", G, grads, token,
                                 lambda after: exchange.grads_crossed("ffn0", after))
    dx, G = _even_block_bwd(dx, s_mix0, W, w, "l0", G, grads)
    for n in list(grads):
        if isinstance(grads[n], list):
            grads[n] = jnp.stack(grads[n], axis=0)
        grads[n] = grads[n].reshape(w[n].shape)
    return loss_lanes, dx, G, grads


class _Exchange:
    def __init__(self, local, chip, core):
        self.bufs = {s: lax.dynamic_update_slice(lax.empty((N_CHIPS, rows, width), BF16),
                                                 _local_slab(local, s, BF16)[None], (chip, 0, 0))
                     for s, (width, rows) in _SLABS.items()}
        small = jnp.zeros((_SMALL_SLAB_ROWS, SMALL_W), F32)
        for n, (r0, rows) in _SMALL_PLACE.items():
            small = small.at[r0:r0 + rows].set(local[n].reshape(rows, SMALL_W))
        self.bufs[_SMALL_SLAB] = lax.dynamic_update_slice(lax.empty((N_CHIPS, _SMALL_SLAB_ROWS, SMALL_W), F32),
                                                          small[None], (chip, 0, 0))
        self.shard_shapes = {n: local[n].shape for n in _SMALL_PLACE}
        self.where = jnp.stack([chip, core]).astype(jnp.int32)
        self.reduces = {}

    def weights(self, stage, after):
        send_sems, recv_sems, flying = self.flight
        slabs = self.stage_slabs(stage)
        first = list(flying).index(slabs[0])
        bufs = _gather_ici_wait(send_sems, recv_sems, [flying[s] for s in slabs], first, after,
                                name=f"gather_stage{stage}_wait")
        now = [k for k, s in enumerate(slabs) if s not in _LATE_SLABS]
        late = [k for k, s in enumerate(slabs) if s in _LATE_SLABS]
        whole = _gather_forward([bufs[k] for k in now], name=f"gather_stage{stage}_forward")
        if late:
            *state, self.late_token = _gather_forward_start([bufs[k] for k in late], whole[0], name="gather_late_start")
            self.late = ([slabs[k] for k in late], *state)
        return dict(zip([slabs[k] for k in now], whole))

    def behind_late_start(self, w):
        return {**w, "o_norm": _behind(w["o_norm"], self.late_token)}

    def late_weights(self, after):
        slabs, send_sems, recv_sems, bufs = self.late
        return dict(zip(slabs, _gather_forward_wait(send_sems, recv_sems, bufs, after, name="gather_late_wait")))

    @staticmethod
    def stage_slabs(stage):
        return _STAGES[stage] + ((_SMALL_SLAB,) if stage == 0 else ())

    def first_weights(self, w):
        order = [s for k in range(len(_STAGES)) for s in self.stage_slabs(k)]
        send_sems, recv_sems, bufs, after = _gather_ici_start([self.bufs[s] for s in order], w["e_norm"],
                                                              name="gather_start")
        self.flight = (send_sems, recv_sems, dict(zip(order, bufs)))
        W = self.weights(0, after)
        w = {**w, "e_norm": _behind(w["e_norm"], after)}
        for (n, ax), (r0, rows) in zip(_SMALL_SHARDED, _SMALL_PLACE.values()):
            shards = [W[_SMALL_SLAB][p, r0:r0 + rows].reshape(self.shard_shapes[n]) for p in range(N_CHIPS)]
            w[n] = jnp.concatenate(shards, axis=ax)
        return W, w

    def pair_start(self, G, slabs, tag):
        send_sems, recv_sems, gl, lands, token = _pair_exchange_start([G[s] for s in slabs],
                                                                      name=f"grad_{tag}_pair_start")
        return (slabs, send_sems, recv_sems, gl, lands), token

    def pair_land(self, state, after, tag):
        slabs, send_sems, recv_sems, gl, lands = state
        gl, other = _pair_exchange_wait(send_sems, recv_sems, gl, lands, after, name=f"grad_{tag}_pair_wait")
        pairs = [_pair_sum(g, r, self.where, name=f"grad_pair_sum_{s}") for s, g, r in zip(slabs, gl, other)]
        send_sems, recv_sems, pairs, lands, token = _chip_exchange_start(pairs, name=f"grad_{tag}_chip_start")
        return (slabs, gl, other, send_sems, recv_sems, pairs, lands), token

    def reduce_sum(self, state, after, tag):
        slabs, gl, other, send_sems, recv_sems, pairs, lands = state
        slots = _chip_exchange_wait(send_sems, recv_sems, pairs, lands, after, name=f"grad_{tag}_chip_wait")
        return slabs, [_chip_sum(g, r, sl, self.where, name=f"grad_chip_sum_{s}")
                       for s, g, r, sl in zip(slabs, gl, other, slots)]

    @staticmethod
    def share_start(slabs, halves, tag):
        send_sems, recv_sems, halves, token = _pair_share_start(halves, name=f"grad_{tag}_share_start")
        return (slabs, send_sems, recv_sems, halves), token

    @staticmethod
    def share_finish(state, after, tag):
        slabs, send_sems, recv_sems, halves = state
        return dict(zip(slabs, _pair_share_wait(send_sems, recv_sems, halves, after, name=f"grad_{tag}_share_wait")))

    def grads_ready(self, piece, G):
        self.reduces[piece], token = self.pair_start(G, _GRAD_PIECES[piece], piece)
        return token

    def grads_crossed(self, piece, after):
        self.reduces[piece], token = self.pair_land(self.reduces[piece], after, piece)
        return token


def kernel(x, mem, e_norm, e_w_in, e_gmlp_w, e_gmlp_b, e_conv_w, e_conv_b, e_conv_ln_g, e_conv_ln_b, e_w_out, o_norm, o_w_in, o_lam_re, o_lam_im, o_log_dt, o_b_re, o_b_im, o_c_re, o_c_im, o_d, o_w_out, ca_norm, ca_mem_norm, ca_wq, ca_wk, ca_wv, ca_wo, ffn_norm, ffn_w_gate, ffn_w_up, ffn_w_down, final_norm, loss_target, m_e_norm, m_e_w_in, m_e_gmlp_w, m_e_gmlp_b, m_e_conv_w, m_e_conv_b, m_e_conv_ln_g, m_e_conv_ln_b, m_e_w_out, m_o_norm, m_o_w_in, m_o_lam_re, m_o_lam_im, m_o_log_dt, m_o_b_re, m_o_b_im, m_o_c_re, m_o_c_im, m_o_d, m_o_w_out, m_ca_norm, m_ca_mem_norm, m_ca_wq, m_ca_wk, m_ca_wv, m_ca_wo, m_ffn_norm, m_ffn_w_gate, m_ffn_w_up, m_ffn_w_down, m_final_norm, v_e_norm, v_e_w_in, v_e_gmlp_w, v_e_gmlp_b, v_e_conv_w, v_e_conv_b, v_e_conv_ln_g, v_e_conv_ln_b, v_e_w_out, v_o_norm, v_o_w_in, v_o_lam_re, v_o_lam_im, v_o_log_dt, v_o_b_re, v_o_b_im, v_o_c_re, v_o_c_im, v_o_d, v_o_w_out, v_ca_norm, v_ca_mem_norm, v_ca_wq, v_ca_wk, v_ca_wv, v_ca_wo, v_ffn_norm, v_ffn_w_gate, v_ffn_w_up, v_ffn_w_down, v_final_norm):
    args = dict(locals())
    local = {n: args[n] for n in _WEIGHTS}
    mom = {n: args["m_" + n] for n in _WEIGHTS}
    vel = {n: args["v_" + n] for n in _WEIGHTS}
    chip = 2 * lax.axis_index("x") + lax.axis_index("y")
    core = lax.axis_index("c")
    xs_, mems_, tgt = x[0], mem[0], loss_target[0]

    w = {n: local[n] for n in _REPLICATED}
    exchange = _Exchange(local, chip, core)
    G = {s: lax.empty((N_CHIPS, rows, width), F32) for s, (width, rows) in _SLABS.items()}
    loss_lanes, dx, G, grads = _forward_backward(xs_, mems_, tgt, w, G, exchange)

    gs_slab, gs_spans = _pack_rows([grads[n] for n in _SMALL] + [loss_lanes], SMALL_W, F32)
    rest0_token = exchange.grads_ready("rest0", G)
    small_flight = _all_to_all_start(gs_slab, rest0_token, name="small_grads_start")
    slabs_l1, halves_l1 = exchange.reduce_sum(exchange.reduces["l1"], small_flight[4], "l1")
    slabs_f0, halves_f0 = exchange.reduce_sum(exchange.reduces["ffn0"], small_flight[4], "ffn0")
    share, share_token = exchange.share_start(slabs_l1 + slabs_f0, halves_l1 + halves_f0, "l1_ffn0")
    token = exchange.grads_crossed("rest0", share_token)

    out_grads, delta, new_m, new_v = {}, {}, {}, {}

    def adamw_large(names):
        raw = []
        for n in names:
            shp = local[n].shape
            g_, d_, m_, v_ = _adamw_shard(_shard_rows(n, local[n]), [(gsum[s], r0) for s, r0 in _PLACE[n][1]],
                                          _shard_rows(n, mom[n]), _shard_rows(n, vel[n]), name=f"adamw_{n}")
            out_grads[n], delta[n], new_m[n], new_v[n] = (_from_shard_rows(n, t, shp) for t in (g_, d_, m_, v_))
            raw.append(d_)
        return raw

    gsum = exchange.share_finish(share, token, "l1_ffn0")
    ready = [n for n, (_, where) in _PLACE.items() if all(s in gsum for s, _ in where)]
    done = adamw_large(ready)

    gs_slab, gs_all = _all_to_all_wait(*small_flight[:4], done, name="small_grads_wait")
    gs_all = lax.dynamic_update_slice(gs_all, gs_slab[None], (2 * chip + core, 0, 0))
    gs_sum = _sum_slots(gs_all, name="small_grad_sum")
    *small_sums, loss_sum = _unpack_rows(gs_sum, gs_spans, [grads[n].shape for n in _SMALL] + [loss_lanes.shape])
    out_grads.update(zip(_SMALL, small_sums))
    for n, ax in _SMALL_SHARDED:
        width = local[n].shape[ax]
        out_grads[n] = lax.dynamic_slice_in_dim(out_grads[n], chip * width, width, axis=ax)
    d_, m_, v_ = _adamw_small([_two_d(local[n]) for n in _SMALL], [_two_d(out_grads[n]) for n in _SMALL],
                              [_two_d(mom[n]) for n in _SMALL], [_two_d(vel[n]) for n in _SMALL], name="adamw_small")
    for n, dd, mm_, vv in zip(_SMALL, d_, m_, v_):
        shp = local[n].shape
        delta[n], new_m[n], new_v[n] = dd.reshape(shp), mm_.reshape(shp), vv.reshape(shp)

    slabs_r0, halves_r0 = exchange.reduce_sum(exchange.reduces["rest0"], d_[0], "rest0")
    share, share_token = exchange.share_start(slabs_r0, halves_r0, "rest0")
    gsum = {**gsum, **exchange.share_finish(share, share_token, "rest0")}
    adamw_large([n for n in _PLACE if n not in ready])

    return (loss_sum[0, 0], dx[None], *[out_grads[n] for n in _WEIGHTS], *[delta[n] for n in _WEIGHTS],
            *[new_m[n] for n in _WEIGHTS], *[new_v[n] for n in _WEIGHTS])
```
